```python
import jax, jax.numpy as jnp
from jax import lax
import numpy as np

D_MODEL = 1024
BATCH = 8
SEQ = 4096
DEPTH = 1

D_MIX = D_MODEL
W_A = D_MIX // 2
HA_HEAD_DIM = 128
HA_HEADS = W_A // HA_HEAD_DIM
W_B = D_MIX - W_A
HB_HEAD_DIM = 64
HB_HEADS = W_B // HB_HEAD_DIM
HGRN_CHUNK = 64
DECAY_LORA = max(32, int(round(1.8 * W_B ** 0.5 / 32)) * 32)
AAA_LORA = max(32, int(round(1.8 * W_B ** 0.5 / 32)) * 32)
GATE_LORA = max(32, int(round(0.6 * W_B ** 0.8 / 32)) * 32)
N_HGRN_COLS = 4 * W_A
N_RWKV_COLS = 3 * W_B + DECAY_LORA + AAA_LORA + GATE_LORA
D_IN_PROJ = N_HGRN_COLS + N_RWKV_COLS
D_FF = ((8 * D_MODEL // 3 + 255) // 256) * 256
NORM_EPS = 1e-6
RWKV_GN_EPS = 64e-5
L2_EPS = 1e-12

kernel_name = 'hybrid_hgrn2_rwkv7_macaron'


def rms_norm(x, g):
    xf = x.astype(jnp.float32)
    y = xf * lax.rsqrt(jnp.mean(xf * xf, axis=-1, keepdims=True) + NORM_EPS)
    return (y * g).astype(x.dtype)


def swiglu(h, w_gate, w_up, w_down):
    return (jax.nn.silu(h @ w_gate) * (h @ w_up)) @ w_down


def token_shift(t):
    return jnp.pad(t, ((0, 0), (1, 0), (0, 0)))[:, :-1]


def hgrn2_chunkwise(q, k, v, log_f):
    B, T, H, K = q.shape
    V = v.shape[-1]
    C = HGRN_CHUNK
    n = T // C

    def blocks(t):
        return t.reshape(B, n, C, H, t.shape[-1]).transpose(0, 3, 1, 2, 4)

    q, k, v, log_f = blocks(q), blocks(k), blocks(v), blocks(log_f)
    b = jnp.cumsum(log_f, axis=3)
    b_ref = b[:, :, :, C // 2:C // 2 + 1]
    b_last = b[:, :, :, C - 1:]
    scores = jnp.einsum('bhntk,bhnsk->bhnts', q * jnp.exp(b - b_ref), k * jnp.exp(b_ref - b))
    causal = jnp.tril(jnp.ones((C, C), dtype=bool))
    scores = jnp.where(causal, scores, 0.0)
    o = jnp.einsum('bhnts,bhnsv->bhntv', scores, v)
    u = jnp.einsum('bhnsk,bhnsv->bhnkv', k * jnp.exp(b_last - b), v)
    d = jnp.exp(b_last[:, :, :, 0])

    def chunk_step(s, inp):
        u_n, d_n = inp
        return d_n[..., None] * s + u_n, s

    _, s_prev = lax.scan(chunk_step, jnp.zeros((B, H, K, V), jnp.float32),
                         (jnp.moveaxis(u, 2, 0), jnp.moveaxis(d, 2, 0)))
    s_prev = jnp.moveaxis(s_prev, 0, 2)
    o = o + jnp.einsum('bhntk,bhnkv->bhntv', q * jnp.exp(b), s_prev)
    return o.transpose(0, 2, 3, 1, 4).reshape(B, T, H, V)


def rwkv7_scan(r, w, k, v, a, b):
    B, T, H, N = r.shape

    def step(s, inp):
        r_t, w_t, k_t, v_t, a_t, b_t = inp
        sa = jnp.einsum('bhvk,bhk->bhv', s, a_t)
        s = s * w_t[:, :, None, :] + sa[..., None] * b_t[:, :, None, :] + v_t[..., None] * k_t[:, :, None, :]
        return s, jnp.einsum('bhvk,bhk->bhv', s, r_t)

    tm = lambda t: jnp.moveaxis(t, 1, 0)
    _, y = lax.scan(step, jnp.zeros((B, H, N, N), jnp.float32),
                    (tm(r), tm(w), tm(k), tm(v), tm(a), tm(b)))
    return jnp.moveaxis(y, 0, 1)


def hybrid_mixer(h, w_in, lb, hgrn_out_norm, mu, w0, w2, a0, a2, g2, k_k, k_a, r_k, gn_w, gn_b, w_out):
    B, T, _ = h.shape
    f32 = jnp.float32
    p = h @ w_in

    q_a, f_a, i_a, g_a = jnp.split(p[..., :N_HGRN_COLS].astype(f32), 4, axis=-1)
    forget = lb + (1.0 - lb) * jax.nn.sigmoid(f_a)
    heads_a = lambda t: t.reshape(B, T, HA_HEADS, HA_HEAD_DIM)
    o_a = hgrn2_chunkwise(heads_a(jax.nn.silu(q_a)), heads_a(1.0 - forget),
                          heads_a(i_a), heads_a(jnp.log(forget)))
    o_a = o_a * lax.rsqrt(jnp.mean(o_a * o_a, axis=-1, keepdims=True) + NORM_EPS)
    o_a = o_a.reshape(B, T, W_A) * hgrn_out_norm * jax.nn.silu(g_a)

    pr = p[..., N_HGRN_COLS:].astype(f32)
    pr = pr + mu * (token_shift(pr) - pr)
    splits = [W_B, 2 * W_B, 3 * W_B, 3 * W_B + DECAY_LORA, 3 * W_B + DECAY_LORA + AAA_LORA]
    r, k, v, w_low, a_low, g_low = jnp.split(pr, splits, axis=-1)
    w_log = -jax.nn.softplus(-(w0 + jnp.tanh(w_low) @ w2)) - 0.5
    decay = jnp.exp(-jnp.exp(w_log))
    a = jax.nn.sigmoid(a0 + a_low @ a2)
    g = jax.nn.sigmoid(g_low) @ g2
    heads_b = lambda t: t.reshape(B, T, HB_HEADS, HB_HEAD_DIM)
    kk = heads_b(k * k_k)
    kk = kk / jnp.maximum(jnp.sqrt(jnp.sum(kk * kk, axis=-1, keepdims=True)), L2_EPS)
    k = k * (1.0 + (a - 1.0) * k_a)
    r_h, k_h, v_h, a_h = heads_b(r), heads_b(k), heads_b(v), heads_b(a)
    y = rwkv7_scan(r_h, heads_b(decay), k_h, v_h, -kk, kk * a_h)
    mean = jnp.mean(y, axis=-1, keepdims=True)
    var = jnp.mean(jnp.square(y - mean), axis=-1, keepdims=True)
    y = ((y - mean) * lax.rsqrt(var + RWKV_GN_EPS)).reshape(B, T, W_B) * gn_w + gn_b
    bonus = (jnp.sum(r_h * k_h * r_k, axis=-1, keepdims=True) * v_h).reshape(B, T, W_B)
    o_b = (y + bonus) * g

    return jnp.concatenate([o_a, o_b], axis=-1).astype(h.dtype) @ w_out


def _fwd_setup_inputs(seed: int = 0) -> dict:
    key = jax.random.key(seed)
    ks = iter(jax.random.split(key, 40))
    f32 = jnp.float32
    nrm = lambda shape, scale: jax.random.normal(next(ks), shape, f32) * scale
    uni = lambda shape, lo, hi: jax.random.uniform(next(ks), shape, f32, lo, hi)
    L = DEPTH
    return {
        'x': nrm((BATCH, SEQ, D_MODEL), 1.0),
        'ffn1_norm': 1.0 + nrm((L, D_MODEL), 0.02),
        'ffn1_w_gate': nrm((L, D_MODEL, D_FF), D_MODEL ** -0.5),
        'ffn1_w_up': nrm((L, D_MODEL, D_FF), D_MODEL ** -0.5),
        'ffn1_w_down': nrm((L, D_FF, D_MODEL), D_FF ** -0.5),
        'mix_norm': 1.0 + nrm((L, D_MODEL), 0.02),
        'w_in': nrm((L, D_MODEL, D_IN_PROJ), D_MODEL ** -0.5),
        'hgrn_lb_logits': nrm((L + 1, W_A), 0.1),
        'hgrn_out_norm': 1.0 + nrm((L, W_A), 0.02),
        'rwkv_shift_mu': uni((L, N_RWKV_COLS), 0.0, 1.0),
        'rwkv_w0': uni((L, W_B), -5.0, 1.0),
        'rwkv_w2': nrm((L, DECAY_LORA, W_B), 0.5 * DECAY_LORA ** -0.5),
        'rwkv_a0': nrm((L, W_B), 0.1),
        'rwkv_a2': nrm((L, AAA_LORA, W_B), 0.5 * AAA_LORA ** -0.5),
        'rwkv_g2': nrm((L, GATE_LORA, W_B), GATE_LORA ** -0.5),
        'rwkv_k_k': 0.85 + nrm((L, W_B), 0.05),
        'rwkv_k_a': 1.0 + nrm((L, W_B), 0.05),
        'rwkv_r_k': nrm((L, HB_HEADS, HB_HEAD_DIM), 0.1),
        'rwkv_gn_w': 1.0 + nrm((L, W_B), 0.02),
        'rwkv_gn_b': nrm((L, W_B), 0.02),
        'w_out': nrm((L, D_MIX, D_MODEL), D_MIX ** -0.5),
        'ffn2_norm': 1.0 + nrm((L, D_MODEL), 0.02),
        'ffn2_w_gate': nrm((L, D_MODEL, D_FF), D_MODEL ** -0.5),
        'ffn2_w_up': nrm((L, D_MODEL, D_FF), D_MODEL ** -0.5),
        'ffn2_w_down': nrm((L, D_FF, D_MODEL), D_FF ** -0.5),
        'final_norm': 1.0 + nrm((D_MODEL,), 0.02),
    }


def _fwd_reference(x, ffn1_norm, ffn1_w_gate, ffn1_w_up, ffn1_w_down, mix_norm, w_in, hgrn_lb_logits,
              hgrn_out_norm, rwkv_shift_mu, rwkv_w0, rwkv_w2, rwkv_a0, rwkv_a2, rwkv_g2, rwkv_k_k,
              rwkv_k_a, rwkv_r_k, rwkv_gn_w, rwkv_gn_b, w_out, ffn2_norm, ffn2_w_gate, ffn2_w_up,
              ffn2_w_down, final_norm):
    lower_bounds = jnp.cumsum(jax.nn.softmax(hgrn_lb_logits.astype(jnp.float32), axis=0), axis=0)
    for l in range(DEPTH):
        x = x + 0.5 * swiglu(rms_norm(x, ffn1_norm[l]), ffn1_w_gate[l], ffn1_w_up[l], ffn1_w_down[l])
        x = x + hybrid_mixer(rms_norm(x, mix_norm[l]), w_in[l], lower_bounds[l], hgrn_out_norm[l],
                             rwkv_shift_mu[l], rwkv_w0[l], rwkv_w2[l], rwkv_a0[l], rwkv_a2[l],
                             rwkv_g2[l], rwkv_k_k[l], rwkv_k_a[l], rwkv_r_k[l], rwkv_gn_w[l],
                             rwkv_gn_b[l], w_out[l])
        x = x + 0.5 * swiglu(rms_norm(x, ffn2_norm[l]), ffn2_w_gate[l], ffn2_w_up[l], ffn2_w_down[l])
    return rms_norm(x, final_norm)


import jax as _jax
import jax.numpy as _jnp

TWIN_FORMAT = 'train_step'
FWD_PARAMS = ['x', 'ffn1_norm', 'ffn1_w_gate', 'ffn1_w_up', 'ffn1_w_down', 'mix_norm', 'w_in', 'hgrn_lb_logits', 'hgrn_out_norm', 'rwkv_shift_mu', 'rwkv_w0', 'rwkv_w2', 'rwkv_a0', 'rwkv_a2', 'rwkv_g2', 'rwkv_k_k', 'rwkv_k_a', 'rwkv_r_k', 'rwkv_gn_w', 'rwkv_gn_b', 'w_out', 'ffn2_norm', 'ffn2_w_gate', 'ffn2_w_up', 'ffn2_w_down', 'final_norm']
TWIN_WEIGHTS = ['ffn1_norm', 'ffn1_w_gate', 'ffn1_w_up', 'ffn1_w_down', 'mix_norm', 'w_in', 'hgrn_lb_logits', 'hgrn_out_norm', 'rwkv_shift_mu', 'rwkv_w0', 'rwkv_w2', 'rwkv_a0', 'rwkv_a2', 'rwkv_g2', 'rwkv_k_k', 'rwkv_k_a', 'rwkv_r_k', 'rwkv_gn_w', 'rwkv_gn_b', 'w_out', 'ffn2_norm', 'ffn2_w_gate', 'ffn2_w_up', 'ffn2_w_down', 'final_norm']
TWIN_DIFF_INPUT = 'x'
TWIN_INPUTS = ['x', 'ffn1_norm', 'ffn1_w_gate', 'ffn1_w_up', 'ffn1_w_down', 'mix_norm', 'w_in', 'hgrn_lb_logits', 'hgrn_out_norm', 'rwkv_shift_mu', 'rwkv_w0', 'rwkv_w2', 'rwkv_a0', 'rwkv_a2', 'rwkv_g2', 'rwkv_k_k', 'rwkv_k_a', 'rwkv_r_k', 'rwkv_gn_w', 'rwkv_gn_b', 'w_out', 'ffn2_norm', 'ffn2_w_gate', 'ffn2_w_up', 'ffn2_w_down', 'final_norm', 'loss_target', 'm_ffn1_norm', 'm_ffn1_w_gate', 'm_ffn1_w_up', 'm_ffn1_w_down', 'm_mix_norm', 'm_w_in', 'm_hgrn_lb_logits', 'm_hgrn_out_norm', 'm_rwkv_shift_mu', 'm_rwkv_w0', 'm_rwkv_w2', 'm_rwkv_a0', 'm_rwkv_a2', 'm_rwkv_g2', 'm_rwkv_k_k', 'm_rwkv_k_a', 'm_rwkv_r_k', 'm_rwkv_gn_w', 'm_rwkv_gn_b', 'm_w_out', 'm_ffn2_norm', 'm_ffn2_w_gate', 'm_ffn2_w_up', 'm_ffn2_w_down', 'm_final_norm', 'v_ffn1_norm', 'v_ffn1_w_gate', 'v_ffn1_w_up', 'v_ffn1_w_down', 'v_mix_norm', 'v_w_in', 'v_hgrn_lb_logits', 'v_hgrn_out_norm', 'v_rwkv_shift_mu', 'v_rwkv_w0', 'v_rwkv_w2', 'v_rwkv_a0', 'v_rwkv_a2', 'v_rwkv_g2', 'v_rwkv_k_k', 'v_rwkv_k_a', 'v_rwkv_r_k', 'v_rwkv_gn_w', 'v_rwkv_gn_b', 'v_w_out', 'v_ffn2_norm', 'v_ffn2_w_gate', 'v_ffn2_w_up', 'v_ffn2_w_down', 'v_final_norm']
TWIN_OUTPUTS = ['loss', 'grad_x', 'grad_ffn1_norm', 'grad_ffn1_w_gate', 'grad_ffn1_w_up', 'grad_ffn1_w_down', 'grad_mix_norm', 'grad_w_in', 'grad_hgrn_lb_logits', 'grad_hgrn_out_norm', 'grad_rwkv_shift_mu', 'grad_rwkv_w0', 'grad_rwkv_w2', 'grad_rwkv_a0', 'grad_rwkv_a2', 'grad_rwkv_g2', 'grad_rwkv_k_k', 'grad_rwkv_k_a', 'grad_rwkv_r_k', 'grad_rwkv_gn_w', 'grad_rwkv_gn_b', 'grad_w_out', 'grad_ffn2_norm', 'grad_ffn2_w_gate', 'grad_ffn2_w_up', 'grad_ffn2_w_down', 'grad_final_norm', 'delta_ffn1_norm', 'delta_ffn1_w_gate', 'delta_ffn1_w_up', 'delta_ffn1_w_down', 'delta_mix_norm', 'delta_w_in', 'delta_hgrn_lb_logits', 'delta_hgrn_out_norm', 'delta_rwkv_shift_mu', 'delta_rwkv_w0', 'delta_rwkv_w2', 'delta_rwkv_a0', 'delta_rwkv_a2', 'delta_rwkv_g2', 'delta_rwkv_k_k', 'delta_rwkv_k_a', 'delta_rwkv_r_k', 'delta_rwkv_gn_w', 'delta_rwkv_gn_b', 'delta_w_out', 'delta_ffn2_norm', 'delta_ffn2_w_gate', 'delta_ffn2_w_up', 'delta_ffn2_w_down', 'delta_final_norm', 'new_m_ffn1_norm', 'new_m_ffn1_w_gate', 'new_m_ffn1_w_up', 'new_m_ffn1_w_down', 'new_m_mix_norm', 'new_m_w_in', 'new_m_hgrn_lb_logits', 'new_m_hgrn_out_norm', 'new_m_rwkv_shift_mu', 'new_m_rwkv_w0', 'new_m_rwkv_w2', 'new_m_rwkv_a0', 'new_m_rwkv_a2', 'new_m_rwkv_g2', 'new_m_rwkv_k_k', 'new_m_rwkv_k_a', 'new_m_rwkv_r_k', 'new_m_rwkv_gn_w', 'new_m_rwkv_gn_b', 'new_m_w_out', 'new_m_ffn2_norm', 'new_m_ffn2_w_gate', 'new_m_ffn2_w_up', 'new_m_ffn2_w_down', 'new_m_final_norm', 'new_v_ffn1_norm', 'new_v_ffn1_w_gate', 'new_v_ffn1_w_up', 'new_v_ffn1_w_down', 'new_v_mix_norm', 'new_v_w_in', 'new_v_hgrn_lb_logits', 'new_v_hgrn_out_norm', 'new_v_rwkv_shift_mu', 'new_v_rwkv_w0', 'new_v_rwkv_w2', 'new_v_rwkv_a0', 'new_v_rwkv_a2', 'new_v_rwkv_g2', 'new_v_rwkv_k_k', 'new_v_rwkv_k_a', 'new_v_rwkv_r_k', 'new_v_rwkv_gn_w', 'new_v_rwkv_gn_b', 'new_v_w_out', 'new_v_ffn2_norm', 'new_v_ffn2_w_gate', 'new_v_ffn2_w_up', 'new_v_ffn2_w_down', 'new_v_final_norm']
TWIN_LEAF_KINDS = {'loss': 'loss', 'grad_x': 'grad_x', 'grad_ffn1_norm': 'grad_w', 'grad_ffn1_w_gate': 'grad_w', 'grad_ffn1_w_up': 'grad_w', 'grad_ffn1_w_down': 'grad_w', 'grad_mix_norm': 'grad_w', 'grad_w_in': 'grad_w', 'grad_hgrn_lb_logits': 'grad_w', 'grad_hgrn_out_norm': 'grad_w', 'grad_rwkv_shift_mu': 'grad_w', 'grad_rwkv_w0': 'grad_w', 'grad_rwkv_w2': 'grad_w', 'grad_rwkv_a0': 'grad_w', 'grad_rwkv_a2': 'grad_w', 'grad_rwkv_g2': 'grad_w', 'grad_rwkv_k_k': 'grad_w', 'grad_rwkv_k_a': 'grad_w', 'grad_rwkv_r_k': 'grad_w', 'grad_rwkv_gn_w': 'grad_w', 'grad_rwkv_gn_b': 'grad_w', 'grad_w_out': 'grad_w', 'grad_ffn2_norm': 'grad_w', 'grad_ffn2_w_gate': 'grad_w', 'grad_ffn2_w_up': 'grad_w', 'grad_ffn2_w_down': 'grad_w', 'grad_final_norm': 'grad_w', 'delta_ffn1_norm': 'delta_w', 'delta_ffn1_w_gate': 'delta_w', 'delta_ffn1_w_up': 'delta_w', 'delta_ffn1_w_down': 'delta_w', 'delta_mix_norm': 'delta_w', 'delta_w_in': 'delta_w', 'delta_hgrn_lb_logits': 'delta_w', 'delta_hgrn_out_norm': 'delta_w', 'delta_rwkv_shift_mu': 'delta_w', 'delta_rwkv_w0': 'delta_w', 'delta_rwkv_w2': 'delta_w', 'delta_rwkv_a0': 'delta_w', 'delta_rwkv_a2': 'delta_w', 'delta_rwkv_g2': 'delta_w', 'delta_rwkv_k_k': 'delta_w', 'delta_rwkv_k_a': 'delta_w', 'delta_rwkv_r_k': 'delta_w', 'delta_rwkv_gn_w': 'delta_w', 'delta_rwkv_gn_b': 'delta_w', 'delta_w_out': 'delta_w', 'delta_ffn2_norm': 'delta_w', 'delta_ffn2_w_gate': 'delta_w', 'delta_ffn2_w_up': 'delta_w', 'delta_ffn2_w_down': 'delta_w', 'delta_final_norm': 'delta_w', 'new_m_ffn1_norm': 'new_m', 'new_m_ffn1_w_gate': 'new_m', 'new_m_ffn1_w_up': 'new_m', 'new_m_ffn1_w_down': 'new_m', 'new_m_mix_norm': 'new_m', 'new_m_w_in': 'new_m', 'new_m_hgrn_lb_logits': 'new_m', 'new_m_hgrn_out_norm': 'new_m', 'new_m_rwkv_shift_mu': 'new_m', 'new_m_rwkv_w0': 'new_m', 'new_m_rwkv_w2': 'new_m', 'new_m_rwkv_a0': 'new_m', 'new_m_rwkv_a2': 'new_m', 'new_m_rwkv_g2': 'new_m', 'new_m_rwkv_k_k': 'new_m', 'new_m_rwkv_k_a': 'new_m', 'new_m_rwkv_r_k': 'new_m', 'new_m_rwkv_gn_w': 'new_m', 'new_m_rwkv_gn_b': 'new_m', 'new_m_w_out': 'new_m', 'new_m_ffn2_norm': 'new_m', 'new_m_ffn2_w_gate': 'new_m', 'new_m_ffn2_w_up': 'new_m', 'new_m_ffn2_w_down': 'new_m', 'new_m_final_norm': 'new_m', 'new_v_ffn1_norm': 'new_v', 'new_v_ffn1_w_gate': 'new_v', 'new_v_ffn1_w_up': 'new_v', 'new_v_ffn1_w_down': 'new_v', 'new_v_mix_norm': 'new_v', 'new_v_w_in': 'new_v', 'new_v_hgrn_lb_logits': 'new_v', 'new_v_hgrn_out_norm': 'new_v', 'new_v_rwkv_shift_mu': 'new_v', 'new_v_rwkv_w0': 'new_v', 'new_v_rwkv_w2': 'new_v', 'new_v_rwkv_a0': 'new_v', 'new_v_rwkv_a2': 'new_v', 'new_v_rwkv_g2': 'new_v', 'new_v_rwkv_k_k': 'new_v', 'new_v_rwkv_k_a': 'new_v', 'new_v_rwkv_r_k': 'new_v', 'new_v_rwkv_gn_w': 'new_v', 'new_v_rwkv_gn_b': 'new_v', 'new_v_w_out': 'new_v', 'new_v_ffn2_norm': 'new_v', 'new_v_ffn2_w_gate': 'new_v', 'new_v_ffn2_w_up': 'new_v', 'new_v_ffn2_w_down': 'new_v', 'new_v_final_norm': 'new_v'}


def _forward(args):
    return _fwd_reference(*[args[k] for k in FWD_PARAMS])


def _output_shape():
    out = _jax.eval_shape(lambda: _forward(_fwd_setup_inputs(0)))
    return out.shape, out.dtype

N_MICROBATCH = 1
ADAM_LR = 0.001
ADAM_B1 = 0.9
ADAM_B2 = 0.999
ADAM_EPS = 1e-08
ADAM_WD = 0.01
ADAM_STEP = 10
PER_EXAMPLE_BATCH_AXIS = {'x': 0, 'loss_target': 0}
SHARED_INPUTS = []
_WEIGHT_DTYPES = {'ffn1_norm': _jnp.float32, 'ffn1_w_gate': _jnp.float32, 'ffn1_w_up': _jnp.float32, 'ffn1_w_down': _jnp.float32, 'mix_norm': _jnp.float32, 'w_in': _jnp.float32, 'hgrn_lb_logits': _jnp.float32, 'hgrn_out_norm': _jnp.float32, 'rwkv_shift_mu': _jnp.float32, 'rwkv_w0': _jnp.float32, 'rwkv_w2': _jnp.float32, 'rwkv_a0': _jnp.float32, 'rwkv_a2': _jnp.float32, 'rwkv_g2': _jnp.float32, 'rwkv_k_k': _jnp.float32, 'rwkv_k_a': _jnp.float32, 'rwkv_r_k': _jnp.float32, 'rwkv_gn_w': _jnp.float32, 'rwkv_gn_b': _jnp.float32, 'w_out': _jnp.float32, 'ffn2_norm': _jnp.float32, 'ffn2_w_gate': _jnp.float32, 'ffn2_w_up': _jnp.float32, 'ffn2_w_down': _jnp.float32, 'final_norm': _jnp.float32}
MOMENT_SCALE = {'ffn1_norm': 9.283981e-02, 'ffn1_w_gate': 3.829771e-02, 'ffn1_w_up': 3.703476e-02, 'ffn1_w_down': 6.141737e-02, 'mix_norm': 1.509299e-01, 'w_in': 7.503702e-02, 'hgrn_lb_logits': 8.738469e-03, 'hgrn_out_norm': 9.692267e-02, 'rwkv_shift_mu': 1.377327e-01, 'rwkv_w0': 4.472346e-02, 'rwkv_w2': 5.190318e-03, 'rwkv_a0': 3.532720e-02, 'rwkv_a2': 3.122919e-02, 'rwkv_g2': 8.679019e-02, 'rwkv_k_k': 8.076054e-02, 'rwkv_k_a': 8.526667e-02, 'rwkv_r_k': 2.104064e-01, 'rwkv_gn_w': 8.339854e-02, 'rwkv_gn_b': 9.069404e-02, 'w_out': 8.745459e-02, 'ffn2_norm': 6.424383e-02, 'ffn2_w_gate': 2.689178e-02, 'ffn2_w_up': 2.606208e-02, 'ffn2_w_down': 4.325823e-02, 'final_norm': 3.195786e+01}


def _to_microbatches(a, axis):
    t = _jnp.moveaxis(a, axis, 0)
    t = t.reshape((N_MICROBATCH, t.shape[0] // N_MICROBATCH) + t.shape[1:])
    return _jnp.moveaxis(t, 1, axis + 1)


def setup_inputs(seed: int = 0) -> dict:
    inp = _fwd_setup_inputs(seed)
    key = _jax.random.fold_in(_jax.random.key(seed), 7919)
    shape, _ = _output_shape()
    out = dict(inp)
    out["loss_target"] = _jax.random.normal(_jax.random.fold_in(key, 0), shape, _jnp.float32)
    for i, name in enumerate(TWIN_WEIGHTS):
        w = inp[name].astype(_jnp.float32)
        if MOMENT_SCALE is None:
            s = _jnp.sqrt(_jnp.mean(_jnp.square(w)) + 1e-30)
        else:
            s = MOMENT_SCALE[name]
        km, kv = _jax.random.split(_jax.random.fold_in(key, i + 1))
        out[name] = w
        out["m_" + name] = s * _jax.random.normal(km, w.shape, _jnp.float32)
        out["v_" + name] = (s * s) * _jax.random.uniform(kv, w.shape, _jnp.float32, 0.5, 1.5)
    if N_MICROBATCH > 1:
        for name, axis in PER_EXAMPLE_BATCH_AXIS.items():
            out[name] = _to_microbatches(out[name], axis)
    return {'x': out['x'], 'ffn1_norm': out['ffn1_norm'], 'ffn1_w_gate': out['ffn1_w_gate'], 'ffn1_w_up': out['ffn1_w_up'], 'ffn1_w_down': out['ffn1_w_down'], 'mix_norm': out['mix_norm'], 'w_in': out['w_in'], 'hgrn_lb_logits': out['hgrn_lb_logits'], 'hgrn_out_norm': out['hgrn_out_norm'], 'rwkv_shift_mu': out['rwkv_shift_mu'], 'rwkv_w0': out['rwkv_w0'], 'rwkv_w2': out['rwkv_w2'], 'rwkv_a0': out['rwkv_a0'], 'rwkv_a2': out['rwkv_a2'], 'rwkv_g2': out['rwkv_g2'], 'rwkv_k_k': out['rwkv_k_k'], 'rwkv_k_a': out['rwkv_k_a'], 'rwkv_r_k': out['rwkv_r_k'], 'rwkv_gn_w': out['rwkv_gn_w'], 'rwkv_gn_b': out['rwkv_gn_b'], 'w_out': out['w_out'], 'ffn2_norm': out['ffn2_norm'], 'ffn2_w_gate': out['ffn2_w_gate'], 'ffn2_w_up': out['ffn2_w_up'], 'ffn2_w_down': out['ffn2_w_down'], 'final_norm': out['final_norm'], 'loss_target': out['loss_target'], 'm_ffn1_norm': out['m_ffn1_norm'], 'm_ffn1_w_gate': out['m_ffn1_w_gate'], 'm_ffn1_w_up': out['m_ffn1_w_up'], 'm_ffn1_w_down': out['m_ffn1_w_down'], 'm_mix_norm': out['m_mix_norm'], 'm_w_in': out['m_w_in'], 'm_hgrn_lb_logits': out['m_hgrn_lb_logits'], 'm_hgrn_out_norm': out['m_hgrn_out_norm'], 'm_rwkv_shift_mu': out['m_rwkv_shift_mu'], 'm_rwkv_w0': out['m_rwkv_w0'], 'm_rwkv_w2': out['m_rwkv_w2'], 'm_rwkv_a0': out['m_rwkv_a0'], 'm_rwkv_a2': out['m_rwkv_a2'], 'm_rwkv_g2': out['m_rwkv_g2'], 'm_rwkv_k_k': out['m_rwkv_k_k'], 'm_rwkv_k_a': out['m_rwkv_k_a'], 'm_rwkv_r_k': out['m_rwkv_r_k'], 'm_rwkv_gn_w': out['m_rwkv_gn_w'], 'm_rwkv_gn_b': out['m_rwkv_gn_b'], 'm_w_out': out['m_w_out'], 'm_ffn2_norm': out['m_ffn2_norm'], 'm_ffn2_w_gate': out['m_ffn2_w_gate'], 'm_ffn2_w_up': out['m_ffn2_w_up'], 'm_ffn2_w_down': out['m_ffn2_w_down'], 'm_final_norm': out['m_final_norm'], 'v_ffn1_norm': out['v_ffn1_norm'], 'v_ffn1_w_gate': out['v_ffn1_w_gate'], 'v_ffn1_w_up': out['v_ffn1_w_up'], 'v_ffn1_w_down': out['v_ffn1_w_down'], 'v_mix_norm': out['v_mix_norm'], 'v_w_in': out['v_w_in'], 'v_hgrn_lb_logits': out['v_hgrn_lb_logits'], 'v_hgrn_out_norm': out['v_hgrn_out_norm'], 'v_rwkv_shift_mu': out['v_rwkv_shift_mu'], 'v_rwkv_w0': out['v_rwkv_w0'], 'v_rwkv_w2': out['v_rwkv_w2'], 'v_rwkv_a0': out['v_rwkv_a0'], 'v_rwkv_a2': out['v_rwkv_a2'], 'v_rwkv_g2': out['v_rwkv_g2'], 'v_rwkv_k_k': out['v_rwkv_k_k'], 'v_rwkv_k_a': out['v_rwkv_k_a'], 'v_rwkv_r_k': out['v_rwkv_r_k'], 'v_rwkv_gn_w': out['v_rwkv_gn_w'], 'v_rwkv_gn_b': out['v_rwkv_gn_b'], 'v_w_out': out['v_w_out'], 'v_ffn2_norm': out['v_ffn2_norm'], 'v_ffn2_w_gate': out['v_ffn2_w_gate'], 'v_ffn2_w_up': out['v_ffn2_w_up'], 'v_ffn2_w_down': out['v_ffn2_w_down'], 'v_final_norm': out['v_final_norm']}


def _loss(weights, diff, rest, loss_target):
    with _jax.named_scope("forward"):
        args = {**rest, TWIN_DIFF_INPUT: diff, **{k: w.astype(_WEIGHT_DTYPES[k]) for k, w in weights.items()}}
        y = _forward(args)
    with _jax.named_scope("loss_head"):
        err = _jnp.square(y.astype(_jnp.float32) - loss_target)
        return 0.5 * _jnp.sum(_jnp.mean(err, axis=-1)) if err.ndim else 0.5 * err


def _adamw(w, g, m, v):
    m = ADAM_B1 * m + (1.0 - ADAM_B1) * g
    v = ADAM_B2 * v + (1.0 - ADAM_B2) * _jnp.square(g)
    m_hat = m / (1.0 - ADAM_B1 ** ADAM_STEP)
    v_hat = v / (1.0 - ADAM_B2 ** ADAM_STEP)
    delta = -ADAM_LR * (m_hat / (_jnp.sqrt(v_hat) + ADAM_EPS) + ADAM_WD * w)
    return delta, m, v


def reference(x, ffn1_norm, ffn1_w_gate, ffn1_w_up, ffn1_w_down, mix_norm, w_in, hgrn_lb_logits, hgrn_out_norm, rwkv_shift_mu, rwkv_w0, rwkv_w2, rwkv_a0, rwkv_a2, rwkv_g2, rwkv_k_k, rwkv_k_a, rwkv_r_k, rwkv_gn_w, rwkv_gn_b, w_out, ffn2_norm, ffn2_w_gate, ffn2_w_up, ffn2_w_down, final_norm, loss_target, m_ffn1_norm, m_ffn1_w_gate, m_ffn1_w_up, m_ffn1_w_down, m_mix_norm, m_w_in, m_hgrn_lb_logits, m_hgrn_out_norm, m_rwkv_shift_mu, m_rwkv_w0, m_rwkv_w2, m_rwkv_a0, m_rwkv_a2, m_rwkv_g2, m_rwkv_k_k, m_rwkv_k_a, m_rwkv_r_k, m_rwkv_gn_w, m_rwkv_gn_b, m_w_out, m_ffn2_norm, m_ffn2_w_gate, m_ffn2_w_up, m_ffn2_w_down, m_final_norm, v_ffn1_norm, v_ffn1_w_gate, v_ffn1_w_up, v_ffn1_w_down, v_mix_norm, v_w_in, v_hgrn_lb_logits, v_hgrn_out_norm, v_rwkv_shift_mu, v_rwkv_w0, v_rwkv_w2, v_rwkv_a0, v_rwkv_a2, v_rwkv_g2, v_rwkv_k_k, v_rwkv_k_a, v_rwkv_r_k, v_rwkv_gn_w, v_rwkv_gn_b, v_w_out, v_ffn2_norm, v_ffn2_w_gate, v_ffn2_w_up, v_ffn2_w_down, v_final_norm):
    given = dict(x=x, ffn1_norm=ffn1_norm, ffn1_w_gate=ffn1_w_gate, ffn1_w_up=ffn1_w_up, ffn1_w_down=ffn1_w_down, mix_norm=mix_norm, w_in=w_in, hgrn_lb_logits=hgrn_lb_logits, hgrn_out_norm=hgrn_out_norm, rwkv_shift_mu=rwkv_shift_mu, rwkv_w0=rwkv_w0, rwkv_w2=rwkv_w2, rwkv_a0=rwkv_a0, rwkv_a2=rwkv_a2, rwkv_g2=rwkv_g2, rwkv_k_k=rwkv_k_k, rwkv_k_a=rwkv_k_a, rwkv_r_k=rwkv_r_k, rwkv_gn_w=rwkv_gn_w, rwkv_gn_b=rwkv_gn_b, w_out=w_out, ffn2_norm=ffn2_norm, ffn2_w_gate=ffn2_w_gate, ffn2_w_up=ffn2_w_up, ffn2_w_down=ffn2_w_down, final_norm=final_norm, loss_target=loss_target, m_ffn1_norm=m_ffn1_norm, m_ffn1_w_gate=m_ffn1_w_gate, m_ffn1_w_up=m_ffn1_w_up, m_ffn1_w_down=m_ffn1_w_down, m_mix_norm=m_mix_norm, m_w_in=m_w_in, m_hgrn_lb_logits=m_hgrn_lb_logits, m_hgrn_out_norm=m_hgrn_out_norm, m_rwkv_shift_mu=m_rwkv_shift_mu, m_rwkv_w0=m_rwkv_w0, m_rwkv_w2=m_rwkv_w2, m_rwkv_a0=m_rwkv_a0, m_rwkv_a2=m_rwkv_a2, m_rwkv_g2=m_rwkv_g2, m_rwkv_k_k=m_rwkv_k_k, m_rwkv_k_a=m_rwkv_k_a, m_rwkv_r_k=m_rwkv_r_k, m_rwkv_gn_w=m_rwkv_gn_w, m_rwkv_gn_b=m_rwkv_gn_b, m_w_out=m_w_out, m_ffn2_norm=m_ffn2_norm, m_ffn2_w_gate=m_ffn2_w_gate, m_ffn2_w_up=m_ffn2_w_up, m_ffn2_w_down=m_ffn2_w_down, m_final_norm=m_final_norm, v_ffn1_norm=v_ffn1_norm, v_ffn1_w_gate=v_ffn1_w_gate, v_ffn1_w_up=v_ffn1_w_up, v_ffn1_w_down=v_ffn1_w_down, v_mix_norm=v_mix_norm, v_w_in=v_w_in, v_hgrn_lb_logits=v_hgrn_lb_logits, v_hgrn_out_norm=v_hgrn_out_norm, v_rwkv_shift_mu=v_rwkv_shift_mu, v_rwkv_w0=v_rwkv_w0, v_rwkv_w2=v_rwkv_w2, v_rwkv_a0=v_rwkv_a0, v_rwkv_a2=v_rwkv_a2, v_rwkv_g2=v_rwkv_g2, v_rwkv_k_k=v_rwkv_k_k, v_rwkv_k_a=v_rwkv_k_a, v_rwkv_r_k=v_rwkv_r_k, v_rwkv_gn_w=v_rwkv_gn_w, v_rwkv_gn_b=v_rwkv_gn_b, v_w_out=v_w_out, v_ffn2_norm=v_ffn2_norm, v_ffn2_w_gate=v_ffn2_w_gate, v_ffn2_w_up=v_ffn2_w_up, v_ffn2_w_down=v_ffn2_w_down, v_final_norm=v_final_norm)
    weights = {n: given[n] for n in TWIN_WEIGHTS}
    shared = {n: given[n] for n in SHARED_INPUTS}
    per_example = {n: given[n] for n in ['x']}
    grad_fn = _jax.value_and_grad(_loss, argnums=(0, 1))

    def one_microbatch(ex, loss_target):
        ex = dict(ex)
        diff = ex.pop(TWIN_DIFF_INPUT)
        return grad_fn(weights, diff, {**shared, **ex}, loss_target)

    if N_MICROBATCH == 1:
        loss, (grad_w, grad_x) = one_microbatch(per_example, given["loss_target"])
    else:
        def body(carry, xs):
            loss_sum, grad_sum = carry
            l_k, (gw_k, gx_k) = one_microbatch(xs[0], xs[1])
            with _jax.named_scope("update"):
                return (loss_sum + l_k, _jax.tree.map(_jnp.add, grad_sum, gw_k)), gx_k

        init = (_jnp.zeros((), _jnp.float32), _jax.tree.map(_jnp.zeros_like, weights))
        (loss, grad_w), grad_x = _jax.lax.scan(body, init, (per_example, given["loss_target"]))
    with _jax.named_scope("update"):
        delta_w, new_m, new_v = {}, {}, {}
        for n in TWIN_WEIGHTS:
            delta_w[n], new_m[n], new_v[n] = _adamw(weights[n], grad_w[n], given["m_" + n], given["v_" + n])
    return (loss, grad_x, *[grad_w[n] for n in TWIN_WEIGHTS], *[delta_w[n] for n in TWIN_WEIGHTS],
            *[new_m[n] for n in TWIN_WEIGHTS], *[new_v[n] for n in TWIN_WEIGHTS])
```

```python
import functools

import jax
import jax.numpy as jnp
from jax import lax
from jax.experimental import pallas as pl
from jax.experimental.pallas import tpu as pltpu

F32 = jnp.float32
BF16 = jnp.bfloat16
SDS = jax.ShapeDtypeStruct
MESH = pl.DeviceIdType.MESH

D_MODEL = 1024
D_FF = 2816
W_A = 512
W_B = 512
HA_HEADS, HA_DIM = 4, 128
HB_HEADS, HB_DIM = 8, 64
HGRN_CHUNK = 64
RWKV_CHUNK = 16
N_HGRN_COLS = 4 * W_A
N_RWKV_COLS = 3 * W_B + 32 + 32 + 96
N_RWKV_PAD = 1792
LORA_PAD = 256
NORM_EPS = 1e-6
RWKV_GN_EPS = 64e-5
L2_EPS = 1e-12
ADAM_LR, ADAM_B1, ADAM_B2, ADAM_EPS, ADAM_WD, ADAM_STEP = 0.001, 0.9, 0.999, 1e-8, 0.01, 10

N_CHIPS = 4
VMEM_LIMIT_V7X = 56 * 1024 * 1024
LANES = 1024

SHARDED = ("ffn1_w_gate", "ffn1_w_up", "ffn1_w_down", "w_in", "rwkv_w2", "rwkv_a2", "rwkv_g2", "w_out",
           "ffn2_w_gate", "ffn2_w_up", "ffn2_w_down")
SHARDED_SHAPES = {
    "ffn1_w_gate": ((D_MODEL, D_FF), 1), "ffn1_w_up": ((D_MODEL, D_FF), 1), "ffn1_w_down": ((D_FF, D_MODEL), 0),
    "w_in": ((D_MODEL, N_HGRN_COLS + N_RWKV_COLS), 1), "rwkv_w2": ((32, W_B), 1), "rwkv_a2": ((32, W_B), 1),
    "rwkv_g2": ((96, W_B), 1), "w_out": ((D_MODEL, D_MODEL), 0),
    "ffn2_w_gate": ((D_MODEL, D_FF), 1), "ffn2_w_up": ((D_MODEL, D_FF), 1), "ffn2_w_down": ((D_FF, D_MODEL), 0),
}
SMALL = ("ffn1_norm", "mix_norm", "hgrn_lb_logits", "hgrn_out_norm", "rwkv_shift_mu", "rwkv_w0", "rwkv_a0",
         "rwkv_k_k", "rwkv_k_a", "rwkv_r_k", "rwkv_gn_w", "rwkv_gn_b", "ffn2_norm", "final_norm")
ALL_WEIGHTS = ("ffn1_norm", "ffn1_w_gate", "ffn1_w_up", "ffn1_w_down", "mix_norm", "w_in", "hgrn_lb_logits",
               "hgrn_out_norm", "rwkv_shift_mu", "rwkv_w0", "rwkv_w2", "rwkv_a0", "rwkv_a2", "rwkv_g2", "rwkv_k_k",
               "rwkv_k_a", "rwkv_r_k", "rwkv_gn_w", "rwkv_gn_b", "w_out", "ffn2_norm", "ffn2_w_gate", "ffn2_w_up",
               "ffn2_w_down", "final_norm")


def _shard_shape(name):
    shape, ax = SHARDED_SHAPES[name]
    return tuple(s // N_CHIPS if i == ax else s for i, s in enumerate(shape))


def _numel(shape):
    n = 1
    for s in shape:
        n *= s
    return n


N_SHARDED_ELEMS = sum(_numel(_shard_shape(n)) for n in SHARDED)
W_ROWS = -(-N_SHARDED_ELEMS // (16 * LANES)) * 16


def _params(sem=None):
    return pltpu.CompilerParams(dimension_semantics=sem, vmem_limit_bytes=VMEM_LIMIT_V7X)


def _dg(x, y, cx, cy, hi):
    dn = (((cx,), (cy,)), ((), ()))
    if hi:
        return lax.dot_general(x.astype(F32), y.astype(F32), dn, precision=lax.Precision.HIGHEST,
                               preferred_element_type=F32)
    return lax.dot_general(x.astype(BF16), y.astype(BF16), dn, preferred_element_type=F32)


def _make_mm(hi):
    @jax.custom_vjp
    def nn(x, y):
        return _dg(x, y, 1, 0, hi)

    @jax.custom_vjp
    def nt(x, y):
        return _dg(x, y, 1, 1, hi)

    @jax.custom_vjp
    def tn(x, y):
        return _dg(x, y, 0, 0, hi)

    nn.defvjp(lambda x, y: (nn(x, y), (x, y)), lambda r, g: (nt(g, r[1]), tn(r[0], g)))
    nt.defvjp(lambda x, y: (nt(x, y), (x, y)), lambda r, g: (nn(g, r[1]), tn(g, r[0])))
    tn.defvjp(lambda x, y: (tn(x, y), (x, y)), lambda r, g: (nt(r[1], g), nn(r[0], g)))
    return nn, nt, tn


_nn, _nt, _tn = _make_mm(False)
_nn_hi, _nt_hi, _tn_hi = _make_mm(True)


def _sigmoid(x):
    return 1.0 / (1.0 + jnp.exp(-x))


def _silu(x):
    return x * _sigmoid(x)


def _softplus(z):
    return jnp.maximum(z, 0.0) + jnp.log(1.0 + jnp.exp(-jnp.abs(z)))


def _mm(a, b, *, ta=False, tb=False, tm, tn, tk, name, out_dtype=F32, res=None, scale=None):
    m = a.shape[1] if ta else a.shape[0]
    kdim = a.shape[0] if ta else a.shape[1]
    n = b.shape[0] if tb else b.shape[1]
    assert (b.shape[1] if tb else b.shape[0]) == kdim
    tm, tn, tk = min(tm, m), min(tn, n), min(tk, kdim)
    assert m % tm == 0 and n % tn == 0 and kdim % tk == 0, (name, m, n, kdim)
    nk = kdim // tk
    a_spec = pl.BlockSpec((tk, tm), lambda i, j, k: (k, i)) if ta else pl.BlockSpec((tm, tk), lambda i, j, k: (i, k))
    b_spec = pl.BlockSpec((tn, tk), lambda i, j, k: (j, k)) if tb else pl.BlockSpec((tk, tn), lambda i, j, k: (k, j))
    o_spec = pl.BlockSpec((tm, tn), lambda i, j, k: (i, j))
    ca, cb = (0 if ta else 1), (1 if tb else 0)

    def body(*refs):
        if res is not None:
            a_ref, b_ref, r_ref, o_ref, acc_ref = refs
        else:
            a_ref, b_ref, o_ref, acc_ref = refs
        k = pl.program_id(2)

        @pl.when(k == 0)
        def _():
            acc_ref[...] = jnp.zeros_like(acc_ref)

        acc_ref[...] += _dg(a_ref[...], b_ref[...], ca, cb, False)

        @pl.when(k == nk - 1)
        def _():
            acc = acc_ref[...]
            if scale is not None:
                acc = acc * scale
            if res is not None:
                acc = r_ref[...] + acc
            o_ref[...] = acc.astype(out_dtype)

    in_specs = [a_spec, b_spec] + ([o_spec] if res is not None else [])
    args = (a, b) + ((res,) if res is not None else ())
    return pl.pallas_call(
        body, name=name, grid=(m // tm, n // tn, nk), in_specs=in_specs, out_specs=o_spec,
        out_shape=SDS((m, n), out_dtype), scratch_shapes=[pltpu.VMEM((tm, tn), F32)],
        compiler_params=_params(("parallel", "parallel", "arbitrary")))(*args)


def _row_spec(x, tm):
    if isinstance(x, tuple):
        arr, w, j = x
        return arr, pl.BlockSpec((tm, w), lambda i, j=j: (i, j))
    return x, pl.BlockSpec((tm, x.shape[1]), lambda i: (i, 0))


def _par_spec(p):
    if isinstance(p, tuple):
        arr, w, j = p
        return arr, pl.BlockSpec((arr.shape[0], w), lambda i, j=j: (0, j))
    return p, pl.BlockSpec(p.shape, lambda i: (0, 0))


def _store_groups(refs, groups, vals):
    for ref, idxs in zip(refs, groups):
        off = 0
        for ix in idxs:
            v = vals[ix]
            ref[:, off:off + v.shape[1]] = v.astype(ref.dtype)
            off += v.shape[1]


def _rowwise(f, xs, params, out_groups, out_dtypes, *, tm, name):
    tm = min(tm, (xs[0][0] if isinstance(xs[0], tuple) else xs[0]).shape[0])
    xa, xspecs = zip(*[_row_spec(x, tm) for x in xs])
    pa, pspecs = (zip(*[_par_spec(p) for p in params]) if params else ((), ()))
    t = xa[0].shape[0]
    nx, npar = len(xa), len(pa)
    widths = [None] * len(out_groups)

    def probe(*vals):
        return f(*vals)

    x_sds = [SDS(s.block_shape, F32) for s in xspecs]
    p_sds = [SDS(s.block_shape, F32) for s in pspecs]
    outs_sds = jax.eval_shape(probe, *x_sds, *p_sds)
    for g, idxs in enumerate(out_groups):
        widths[g] = sum(outs_sds[ix].shape[1] for ix in idxs)

    def body(*refs):
        vals = [r[...].astype(F32) for r in refs[:nx + npar]]
        outs = f(*vals)
        _store_groups(refs[nx + npar:], out_groups, outs)

    return pl.pallas_call(
        body, name=name, grid=(t // tm,), in_specs=list(xspecs) + list(pspecs),
        out_specs=[pl.BlockSpec((tm, w), lambda i: (i, 0)) for w in widths],
        out_shape=[SDS((t, w), dt) for w, dt in zip(widths, out_dtypes)],
        compiler_params=_params(("parallel",)))(*xa, *pa)


def _rowwise_bwd(f, xs, params, cots, *, x_grad, p_grad, dx_groups, dx_dtypes, tm, name, extra=None):
    tm = min(tm, (xs[0][0] if isinstance(xs[0], tuple) else xs[0]).shape[0])
    xa, xspecs = zip(*[_row_spec(x, tm) for x in xs])
    pa, pspecs = (zip(*[_par_spec(p) for p in params]) if params else ((), ()))
    ca, cspecs = zip(*[_row_spec(c, tm) for c in cots])
    extra = extra or {}
    ekeys = sorted(extra)
    ea, especs = (zip(*[_row_spec(extra[k], tm) for k in ekeys]) if ekeys else ((), ()))
    t = xa[0].shape[0]
    nx, npar, nc, ne = len(xa), len(pa), len(ca), len(ea)
    gx = [i for i in range(nx) if x_grad[i]]
    gp = [i for i in range(npar) if p_grad[i]]
    widths = [sum(xspecs[gx[ix]].block_shape[1] for ix in idxs) for idxs in dx_groups]
    ng = len(dx_groups)

    def body(*refs):
        ins = refs[:nx + npar + nc + ne]
        outs = refs[nx + npar + nc + ne:]
        vals = [r[...].astype(F32) for r in ins[:nx + npar]]
        cvals = tuple(r[...].astype(F32) for r in ins[nx + npar:nx + npar + nc])
        evals = [r[...].astype(F32) for r in ins[nx + npar + nc:]]
        diff_idx = gx + [nx + i for i in gp]

        def g(*dargs):
            full = list(vals)
            for ix, v in zip(diff_idx, dargs):
                full[ix] = v
            return tuple(f(*full))

        _, vjp = jax.vjp(g, *[vals[ix] for ix in diff_idx])
        grads = vjp(cvals)
        dxs = list(grads[:len(gx)])
        for k, ev in zip(ekeys, evals):
            dxs[k] = dxs[k] + ev
        _store_groups(outs[:ng], dx_groups, dxs)
        i = pl.program_id(0)
        for ref, gval in zip(outs[ng:], grads[len(gx):]):
            @pl.when(i == 0)
            def _(ref=ref):
                ref[...] = jnp.zeros_like(ref)
            ref[...] += gval

    dp_specs = [pl.BlockSpec(pspecs[i].block_shape, lambda i: (0, 0)) for i in gp]
    dp_shapes = [SDS(pspecs[i].block_shape, F32) for i in gp]
    return pl.pallas_call(
        body, name=name, grid=(t // tm,), in_specs=list(xspecs) + list(pspecs) + list(cspecs) + list(especs),
        out_specs=[pl.BlockSpec((tm, w), lambda i: (i, 0)) for w in widths] + dp_specs,
        out_shape=[SDS((t, w), dt) for w, dt in zip(widths, dx_dtypes)] + dp_shapes,
        compiler_params=_params(("arbitrary",)))(*xa, *pa, *ca, *ea)


def _rms_f(x, g):
    return (x * lax.rsqrt(jnp.mean(x * x, axis=-1, keepdims=True) + NORM_EPS) * g,)


def _swiglu_f(a, u):
    return (_silu(a) * u,)


def _group_sum(x, ones_bd):
    return _nn_hi(x, ones_bd)


def _rwkv_prep_f(r, k, v, lo, rp, kp, vp, lop, mu_r, mu_k, mu_v, mu_lo, w0, w2p, a0, a2p, g2p, k_k, k_a, ones_bd):
    r = r + mu_r * (rp - r)
    k = k + mu_k * (kp - k)
    v = v + mu_v * (vp - v)
    lo = lo + mu_lo * (lop - lo)
    w_log = -_softplus(-(w0 + _nn(jnp.tanh(lo), w2p))) - 0.5
    lw = -jnp.exp(w_log)
    a_g = _sigmoid(a0 + _nn(lo, a2p))
    g = _nn(_sigmoid(lo), g2p)
    kk = k * k_k
    kk = kk / jnp.maximum(jnp.sqrt(_group_sum(kk * kk, ones_bd)), L2_EPS)
    k2 = k * (1.0 + (a_g - 1.0) * k_a)
    return r, lw, k2, v, -kk, kk * a_g, g


def _rwkv_post_f(y, r, k2, v, g, r_k, gn_w, gn_b, ones_bd):
    inv_n = 1.0 / HB_DIM
    mean = _group_sum(y, ones_bd) * inv_n
    yc = y - mean
    var = _group_sum(yc * yc, ones_bd) * inv_n
    yn = yc * lax.rsqrt(var + RWKV_GN_EPS) * gn_w + gn_b
    bonus = _group_sum(r * k2 * r_k, ones_bd) * v
    return ((yn + bonus) * g,)


def _tri(c, strict=False):
    ii = lax.broadcasted_iota(jnp.int32, (c, c), 0)
    jj = lax.broadcasted_iota(jnp.int32, (c, c), 1)
    return (jj < ii) if strict else (jj <= ii)


def _hgrn_head(st0, q_a, f_a, i_a, g_a, l0, l1, onorm):
    c = q_a.shape[0]
    mx = jnp.maximum(l0, l1)
    e0, e1 = jnp.exp(l0 - mx), jnp.exp(l1 - mx)
    lb = e0 / (e0 + e1)
    forget = lb + (1.0 - lb) * _sigmoid(f_a)
    q = _silu(q_a)
    kk = 1.0 - forget
    lf = jnp.log(forget)
    incl = _tri(c)
    bcum = _nn_hi(incl.astype(F32), lf)
    rows = lax.broadcasted_iota(jnp.int32, (c, 1), 0)
    bref = jnp.sum(jnp.where(rows <= c // 2, lf, 0.0), axis=0, keepdims=True)
    blast = jnp.sum(lf, axis=0, keepdims=True)
    scores = jnp.where(incl, _nt(q * jnp.exp(bcum - bref), kk * jnp.exp(bref - bcum)), 0.0)
    o = _nn(scores, i_a) + _nt(q * jnp.exp(bcum), st0)
    st1 = st0 * jnp.exp(blast) + _tn(i_a, kk * jnp.exp(blast - bcum))
    o = o * lax.rsqrt(jnp.mean(o * o, axis=-1, keepdims=True) + NORM_EPS)
    return o * onorm * _silu(g_a), st1


def _hgrn_fwd(p_h, l0, l1, onorm):
    t = p_h.shape[0]
    c, n = HGRN_CHUNK, p_h.shape[0] // HGRN_CHUNK

    def body(q_ref, f_ref, i_ref, g_ref, l0_ref, l1_ref, on_ref, o_ref, hs_ref, st_ref):
        @pl.when(pl.program_id(0) == 0)
        def _():
            st_ref[...] = jnp.zeros_like(st_ref)

        hs_ref[0] = st_ref[...]
        for h in range(HA_HEADS):
            sl = slice(h * HA_DIM, (h + 1) * HA_DIM)
            o, st1 = _hgrn_head(st_ref[h], q_ref[:, sl], f_ref[:, sl], i_ref[:, sl], g_ref[:, sl],
                                l0_ref[:, sl], l1_ref[:, sl], on_ref[:, sl])
            o_ref[:, sl] = o
            st_ref[h] = st1

    col = lambda j: pl.BlockSpec((c, W_A), lambda i, j=j: (i, j))
    par = pl.BlockSpec((1, W_A), lambda i: (0, 0))
    return pl.pallas_call(
        body, name="hgrn_fwd", grid=(n,), in_specs=[col(0), col(1), col(2), col(3), par, par, par],
        out_specs=[pl.BlockSpec((c, W_A), lambda i: (i, 0)),
                   pl.BlockSpec((1, HA_HEADS, HA_DIM, HA_DIM), lambda i: (i, 0, 0, 0))],
        out_shape=[SDS((t, W_A), F32), SDS((n, HA_HEADS, HA_DIM, HA_DIM), F32)],
        scratch_shapes=[pltpu.VMEM((HA_HEADS, HA_DIM, HA_DIM), F32)],
        compiler_params=_params(("arbitrary",)))(p_h, p_h, p_h, p_h, l0, l1, onorm)


def _hgrn_bwd(p_h, l0, l1, onorm, hs, do, do_col):
    t = p_h.shape[0]
    c, n = HGRN_CHUNK, p_h.shape[0] // HGRN_CHUNK

    def body(q_ref, f_ref, i_ref, g_ref, l0_ref, l1_ref, on_ref, hs_ref, do_ref,
             dp_ref, dl0_ref, dl1_ref, don_ref, dst_ref):
        @pl.when(pl.program_id(0) == 0)
        def _():
            dst_ref[...] = jnp.zeros_like(dst_ref)
            dl0_ref[...] = jnp.zeros_like(dl0_ref)
            dl1_ref[...] = jnp.zeros_like(dl1_ref)
            don_ref[...] = jnp.zeros_like(don_ref)

        for h in range(HA_HEADS):
            sl = slice(h * HA_DIM, (h + 1) * HA_DIM)
            args = (hs_ref[0, h], q_ref[:, sl], f_ref[:, sl], i_ref[:, sl], g_ref[:, sl],
                    l0_ref[:, sl], l1_ref[:, sl], on_ref[:, sl])
            _, vjp = jax.vjp(_hgrn_head, *args)
            dst0, dq, df, di, dg, dl0, dl1, don = vjp((do_ref[:, sl], dst_ref[h]))
            for j, dv in enumerate((dq, df, di, dg)):
                dp_ref[:, j * W_A + h * HA_DIM:j * W_A + (h + 1) * HA_DIM] = dv
            dl0_ref[:, sl] += dl0
            dl1_ref[:, sl] += dl1
            don_ref[:, sl] += don
            dst_ref[h] = dst0

    col = lambda j: pl.BlockSpec((c, W_A), lambda i, j=j: (n - 1 - i, j))
    par = pl.BlockSpec((1, W_A), lambda i: (0, 0))
    return pl.pallas_call(
        body, name="hgrn_bwd", grid=(n,),
        in_specs=[col(0), col(1), col(2), col(3), par, par, par,
                  pl.BlockSpec((1, HA_HEADS, HA_DIM, HA_DIM), lambda i: (n - 1 - i, 0, 0, 0)),
                  pl.BlockSpec((c, W_A), lambda i: (n - 1 - i, do_col))],
        out_specs=[pl.BlockSpec((c, N_HGRN_COLS), lambda i: (n - 1 - i, 0)), par, par, par],
        out_shape=[SDS((t, N_HGRN_COLS), F32), SDS((1, W_A), F32), SDS((1, W_A), F32), SDS((1, W_A), F32)],
        scratch_shapes=[pltpu.VMEM((HA_HEADS, HA_DIM, HA_DIM), F32)],
        compiler_params=_params(("arbitrary",)))(p_h, p_h, p_h, p_h, l0, l1, onorm, hs, do)


def _rwkv_head(s0, r, lw, k, v, a, b):
    c = r.shape[0]
    incl, strict = _tri(c), _tri(c, strict=True)
    gam = _nn_hi(incl.astype(F32), lw)
    gtot = jnp.sum(lw, axis=0, keepdims=True)
    at = a * jnp.exp(gam - lw)
    rt = r * jnp.exp(gam)
    eneg = jnp.exp(-gam)
    bt, kt = b * eneg, k * eneg
    edec = jnp.exp(gtot - gam)
    a_ab = jnp.where(strict, _nt(at, bt), 0.0)
    a_ak = jnp.where(strict, _nt(at, kt), 0.0)
    a_rb = jnp.where(incl, _nt(rt, bt), 0.0)
    a_rk = jnp.where(incl, _nt(rt, kt), 0.0)
    eye = (lax.broadcasted_iota(jnp.int32, (c, c), 0) == lax.broadcasted_iota(jnp.int32, (c, c), 1)).astype(F32)
    tinv = eye + a_ab
    pw = a_ab
    span = 2
    while span < c:
        pw = _nn_hi(pw, pw)
        tinv = tinv + _nn_hi(pw, tinv)
        span *= 2
    u = _nn_hi(tinv, _nt(at, s0) + _nn(a_ak, v))
    y = _nt(rt, s0) + _nn(a_rb, u) + _nn(a_rk, v)
    s1 = s0 * jnp.exp(gtot) + _tn(u, b * edec) + _tn(v, k * edec)
    return y, s1


def _rwkv_fwd(seqs):
    t = seqs[0].shape[1]
    c, n = RWKV_CHUNK, seqs[0].shape[1] // RWKV_CHUNK

    def body(r_ref, lw_ref, k_ref, v_ref, a_ref, b_ref, y_ref, hs_ref, st_ref):
        @pl.when(pl.program_id(0) == 0)
        def _():
            st_ref[...] = jnp.zeros_like(st_ref)

        hs_ref[0] = st_ref[...]
        for h in range(HB_HEADS):
            y, s1 = _rwkv_head(st_ref[h], r_ref[h], lw_ref[h], k_ref[h], v_ref[h], a_ref[h], b_ref[h])
            y_ref[h] = y
            st_ref[h] = s1

    seq = pl.BlockSpec((HB_HEADS, c, HB_DIM), lambda i: (0, i, 0))
    return pl.pallas_call(
        body, name="rwkv_fwd", grid=(n,), in_specs=[seq] * 6,
        out_specs=[seq, pl.BlockSpec((1, HB_HEADS, HB_DIM, HB_DIM), lambda i: (i, 0, 0, 0))],
        out_shape=[SDS((HB_HEADS, t, HB_DIM), F32), SDS((n, HB_HEADS, HB_DIM, HB_DIM), F32)],
        scratch_shapes=[pltpu.VMEM((HB_HEADS, HB_DIM, HB_DIM), F32)],
        compiler_params=_params(("arbitrary",)))(*seqs)


def _rwkv_bwd(seqs, hs, dy):
    t = seqs[0].shape[1]
    c, n = RWKV_CHUNK, seqs[0].shape[1] // RWKV_CHUNK

    def body(r_ref, lw_ref, k_ref, v_ref, a_ref, b_ref, hs_ref, dy_ref,
             dr_ref, dlw_ref, dk_ref, dv_ref, da_ref, db_ref, dst_ref):
        @pl.when(pl.program_id(0) == 0)
        def _():
            dst_ref[...] = jnp.zeros_like(dst_ref)

        for h in range(HB_HEADS):
            args = (hs_ref[0, h], r_ref[h], lw_ref[h], k_ref[h], v_ref[h], a_ref[h], b_ref[h])
            _, vjp = jax.vjp(_rwkv_head, *args)
            ds0, dr, dlw, dk, dv, da, db = vjp((dy_ref[h], dst_ref[h]))
            dr_ref[h] = dr
            dlw_ref[h] = dlw
            dk_ref[h] = dk
            dv_ref[h] = dv
            da_ref[h] = da
            db_ref[h] = db
            dst_ref[h] = ds0

    seq = pl.BlockSpec((HB_HEADS, c, HB_DIM), lambda i: (0, n - 1 - i, 0))
    return pl.pallas_call(
        body, name="rwkv_bwd", grid=(n,),
        in_specs=[seq] * 6 + [pl.BlockSpec((1, HB_HEADS, HB_DIM, HB_DIM), lambda i: (n - 1 - i, 0, 0, 0)), seq],
        out_specs=[seq] * 6, out_shape=[SDS((HB_HEADS, t, HB_DIM), F32)] * 6,
        scratch_shapes=[pltpu.VMEM((HB_HEADS, HB_DIM, HB_DIM), F32)],
        compiler_params=_params(("arbitrary",)))(*seqs, hs, dy)


def _final_loss(x3, fnorm, target, *, tm):
    t, d = x3.shape

    def body(x_ref, g_ref, t_ref, dx_ref, dg_ref, loss_ref):
        @pl.when(pl.program_id(0) == 0)
        def _():
            dg_ref[...] = jnp.zeros_like(dg_ref)
            loss_ref[...] = jnp.zeros_like(loss_ref)

        x, g = x_ref[...], g_ref[...]
        rinv = lax.rsqrt(jnp.mean(x * x, axis=-1, keepdims=True) + NORM_EPS)
        xh = x * rinv
        diff = xh * g - t_ref[...]
        loss_ref[...] += 0.5 * jnp.sum(jnp.mean(diff * diff, axis=-1, keepdims=True))
        dy = diff * (1.0 / d)
        dg_ref[...] += jnp.sum(dy * xh, axis=0, keepdims=True)
        dxh = dy * g
        dx_ref[...] = rinv * (dxh - xh * jnp.mean(dxh * xh, axis=-1, keepdims=True))

    row = pl.BlockSpec((tm, d), lambda i: (i, 0))
    return pl.pallas_call(
        body, name="final_loss", grid=(t // tm,), in_specs=[row, pl.BlockSpec((1, d), lambda i: (0, 0)), row],
        out_specs=[row, pl.BlockSpec((1, d), lambda i: (0, 0)), pl.BlockSpec((8, 128), lambda i: (0, 0))],
        out_shape=[SDS((t, d), F32), SDS((1, d), F32), SDS((8, 128), F32)],
        compiler_params=_params(("arbitrary",)))(x3, fnorm, target)


def _ffn_fwd(x, norm, wgu, wd, tag):
    h, = _rowwise(_rms_f, [x], [norm], [[0]], [BF16], tm=512, name=f"{tag}_rms")
    au = _mm(h, wgu, tm=512, tn=512, tk=D_MODEL, name=f"{tag}_gate_up")
    act, = _rowwise(_swiglu_f, [(au, D_FF, 0), (au, D_FF, 1)], [], [[0]], [BF16], tm=256, name=f"{tag}_act")
    out = _mm(act, wd, tm=512, tn=D_MODEL, tk=D_FF // 2, name=f"{tag}_down", res=x, scale=0.5)
    return out, (h, au, act)


def _ffn_bwd(dout, x, norm, wgu, wd, saved, tag):
    h, au, act = saved
    dact = _mm(dout, wd, tb=True, tm=512, tn=D_FF // 2, tk=D_MODEL, name=f"{tag}_dact", scale=0.5)
    dwd = _mm(act, dout, ta=True, tm=D_FF // 2, tn=D_MODEL, tk=512, name=f"{tag}_dwd", scale=0.5)
    dau, = _rowwise_bwd(_swiglu_f, [(au, D_FF, 0), (au, D_FF, 1)], [], [dact], x_grad=[True, True], p_grad=[],
                        dx_groups=[[0, 1]], dx_dtypes=[BF16], tm=256, name=f"{tag}_dau")
    dwgu = _mm(h, dau, ta=True, tm=D_MODEL, tn=D_FF // 2, tk=512, name=f"{tag}_dwgu")
    dh = _mm(dau, wgu, tb=True, tm=512, tn=D_MODEL, tk=D_FF // 2, name=f"{tag}_dh")
    dx, dnorm = _rowwise_bwd(_rms_f, [x], [norm], [dh], x_grad=[True], p_grad=[True], dx_groups=[[0]],
                             dx_dtypes=[F32], tm=256, name=f"{tag}_drms", extra={0: dout})
    return dx, dnorm, dwgu, dwd


def _to_heads(z):
    return z.reshape(z.shape[0], HB_HEADS, HB_DIM).transpose(1, 0, 2)


def _from_heads(z):
    return z.transpose(1, 0, 2).reshape(z.shape[1], W_B)


def _shift_down(z):
    return jnp.concatenate([jnp.zeros((1, z.shape[1]), z.dtype), z[:-1]], axis=0)


def _shift_up(z):
    return jnp.concatenate([z[1:], jnp.zeros((1, z.shape[1]), z.dtype)], axis=0)


def _local_step(x, target, w):
    ones_bd = jnp.kron(jnp.eye(HB_HEADS, dtype=F32), jnp.ones((HB_DIM, HB_DIM), F32))
    g = {}
    x1, ffn1_saved = _ffn_fwd(x, w["ffn1_norm"], w["ffn1_wgu"], w["ffn1_wd"], "ffn1")
    hm, = _rowwise(_rms_f, [x1], [w["mix_norm"]], [[0]], [BF16], tm=512, name="mix_rms")
    p_h = _mm(hm, w["w_in_h"], tm=512, tn=512, tk=D_MODEL, name="inproj_h")
    p_r = _mm(hm, w["w_in_r"], tm=512, tn=N_RWKV_PAD // 2, tk=D_MODEL, name="inproj_r")
    o_a, hgrn_states = _hgrn_fwd(p_h, w["lb0"], w["lb1"], w["hgrn_out_norm"])

    p_r_prev = _shift_down(p_r)
    mu = w["mu_pad"]
    prep_xs = [(p_r, W_B, 0), (p_r, W_B, 1), (p_r, W_B, 2), (p_r, LORA_PAD, 6),
               (p_r_prev, W_B, 0), (p_r_prev, W_B, 1), (p_r_prev, W_B, 2), (p_r_prev, LORA_PAD, 6)]
    prep_ps = [(mu, W_B, 0), (mu, W_B, 1), (mu, W_B, 2), (mu, LORA_PAD, 6), w["rwkv_w0"], w["w2_pad"], w["rwkv_a0"],
               w["a2_pad"], w["g2_pad"], w["rwkv_k_k"], w["rwkv_k_a"], ones_bd]
    prep_f = _rwkv_prep_f
    r, lw, k2, v, a_vec, b_vec, gate = _rowwise(prep_f, prep_xs, prep_ps, [[0], [1], [2], [3], [4], [5], [6]],
                                                [F32] * 7, tm=256, name="rwkv_prep")
    seqs = [_to_heads(z) for z in (r, lw, k2, v, a_vec, b_vec)]
    y_h, rwkv_states = _rwkv_fwd(seqs)
    y = _from_heads(y_h)
    post_f = _rwkv_post_f
    post_xs = [y, r, k2, v, gate]
    post_ps = [w["rwkv_r_k"], w["rwkv_gn_w"], w["rwkv_gn_b"], ones_bd]
    o_b, = _rowwise(post_f, post_xs, post_ps, [[0]], [F32], tm=256, name="rwkv_post")
    o = jnp.concatenate([o_a, o_b], axis=1)
    x2 = _mm(o, w["w_out"], tm=512, tn=D_MODEL, tk=D_MODEL, name="outproj", res=x1)
    x3, ffn2_saved = _ffn_fwd(x2, w["ffn2_norm"], w["ffn2_wgu"], w["ffn2_wd"], "ffn2")
    dx3, g["final_norm"], loss = _final_loss(x3, w["final_norm"], target, tm=256)

    dx2, g["ffn2_norm"], g["ffn2_wgu"], g["ffn2_wd"] = _ffn_bwd(dx3, x2, w["ffn2_norm"], w["ffn2_wgu"], w["ffn2_wd"],
                                                                ffn2_saved, "ffn2")
    do = _mm(dx2, w["w_out"], tb=True, tm=512, tn=D_MODEL, tk=D_MODEL, name="outproj_do")
    g["w_out"] = _mm(o, dx2, ta=True, tm=D_MODEL, tn=D_MODEL, tk=512, name="outproj_dw")

    dp_h, g["lb0"], g["lb1"], g["hgrn_out_norm"] = _hgrn_bwd(p_h, w["lb0"], w["lb1"], w["hgrn_out_norm"],
                                                             hgrn_states, do, 0)
    post_out = _rowwise_bwd(post_f, post_xs, post_ps, [(do, W_B, 1)], x_grad=[True] * 5, p_grad=[True] * 3 + [False],
                            dx_groups=[[0], [1], [2], [3], [4]], dx_dtypes=[F32] * 5, tm=256, name="rwkv_post_bwd")
    dy, dr1, dk1, dv1, dgate, g["rwkv_r_k"], g["rwkv_gn_w"], g["rwkv_gn_b"] = post_out
    dseq = _rwkv_bwd(seqs, rwkv_states, _to_heads(dy))
    dr2, dlw, dk2, dv2, da_vec, db_vec = [_from_heads(z) for z in dseq]

    def prep2_f(*vals):
        r_, lw_, k2_, v_, a_, b_, g_ = prep_f(*vals)
        return r_, lw_, k2_, v_, a_, b_, g_, r_, k2_, v_

    prep_out = _rowwise_bwd(prep2_f, prep_xs, prep_ps, [dr2, dlw, dk2, dv2, da_vec, db_vec, dgate, dr1, dk1, dv1],
                            x_grad=[True] * 8, p_grad=[True] * 11 + [False], dx_groups=[[0, 1, 2, 3], [4, 5, 6, 7]],
                            dx_dtypes=[F32, F32], tm=256, name="rwkv_prep_bwd")
    dpr_main, dpr_prev = prep_out[0], prep_out[1]
    (dmu_r, dmu_k, dmu_v, dmu_lo, g["rwkv_w0"], g["w2_pad"], g["rwkv_a0"], g["a2_pad"], g["g2_pad"],
     g["rwkv_k_k"], g["rwkv_k_a"]) = prep_out[2:]
    g["mu_pad"] = jnp.concatenate([dmu_r, dmu_k, dmu_v, dmu_lo], axis=1)
    dp_r, = _rowwise(lambda u_, s_: (u_ + s_,), [dpr_main, _shift_up(dpr_prev)], [], [[0]], [F32], tm=512,
                     name="rwkv_dp_sum")
    dhm = _mm(dp_h, w["w_in_h"], tb=True, tm=512, tn=D_MODEL, tk=D_MODEL, name="inproj_dh_h")
    dhm = _mm(dp_r, w["w_in_r"], tb=True, tm=512, tn=D_MODEL, tk=N_RWKV_PAD // 2, name="inproj_dh_r", res=dhm)
    g["w_in_h"] = _mm(hm, dp_h, ta=True, tm=D_MODEL, tn=D_MODEL, tk=512, name="inproj_dw_h")
    g["w_in_r"] = _mm(hm, dp_r, ta=True, tm=D_MODEL, tn=N_RWKV_PAD // 2, tk=512, name="inproj_dw_r")
    dx1, g["mix_norm"] = _rowwise_bwd(_rms_f, [x1], [w["mix_norm"]], [dhm], x_grad=[True], p_grad=[True],
                                      dx_groups=[[0]], dx_dtypes=[F32], tm=256, name="mix_drms", extra={0: dx2})
    dx0, g["ffn1_norm"], g["ffn1_wgu"], g["ffn1_wd"] = _ffn_bwd(dx1, x, w["ffn1_norm"], w["ffn1_wgu"], w["ffn1_wd"],
                                                                ffn1_saved, "ffn1")
    return loss, dx0, g


HBM_SPEC = pl.BlockSpec(memory_space=pl.ANY)


def _chips(x, y):
    return [(1 - x, y), (x, 1 - y), (1 - x, 1 - y)]


def _gather_weights(wq):
    def body(w_ref, out_ref, send_sems, recv_sems, local_sem):
        x, y, c = lax.axis_index("x"), lax.axis_index("y"), lax.axis_index("c")
        me = 2 * x + y
        mine = pltpu.make_async_copy(w_ref, out_ref.at[me], local_sem)
        mine.start()

        def copy(j, slot, px, py):
            return pltpu.make_async_remote_copy(src_ref=w_ref, dst_ref=out_ref.at[slot], send_sem=send_sems.at[j],
                                                recv_sem=recv_sems.at[j], device_id=(px, py, c), device_id_type=MESH)

        sends = [copy(j, me, px, py) for j, (px, py) in enumerate(_chips(x, y))]
        for cp in sends:
            cp.start()
        for j, (px, py) in enumerate(_chips(x, y)):
            copy(j, 2 * px + py, px, py).wait_recv()
        for cp in sends:
            cp.wait_send()
        mine.wait()

    return pl.pallas_call(
        body, name="gather_weights", in_specs=[HBM_SPEC], out_specs=HBM_SPEC,
        out_shape=SDS((N_CHIPS,) + wq.shape, wq.dtype),
        scratch_shapes=[pltpu.SemaphoreType.DMA((3,)), pltpu.SemaphoreType.DMA((3,)), pltpu.SemaphoreType.DMA(())],
    )(wq)


def _sibling_exchange(g4):
    def body(g_ref, out_ref, send_sems, recv_sems):
        x, y, c = lax.axis_index("x"), lax.axis_index("y"), lax.axis_index("c")
        cps = [pltpu.make_async_remote_copy(src_ref=g_ref.at[q, 1 - c], dst_ref=out_ref.at[q],
                                            send_sem=send_sems.at[q], recv_sem=recv_sems.at[q],
                                            device_id=(x, y, 1 - c), device_id_type=MESH) for q in range(N_CHIPS)]
        for cp in cps:
            cp.start()
        for cp in cps:
            cp.wait()

    return pl.pallas_call(
        body, name="grad_sibling_exchange", in_specs=[HBM_SPEC], out_specs=HBM_SPEC,
        out_shape=SDS((N_CHIPS,) + g4.shape[2:], g4.dtype),
        scratch_shapes=[pltpu.SemaphoreType.DMA((N_CHIPS,)), pltpu.SemaphoreType.DMA((N_CHIPS,))],
    )(g4)


def _chip_exchange(s4):
    def body(s_ref, out_ref, send_sems, recv_sems, local_sem):
        x, y, c = lax.axis_index("x"), lax.axis_index("y"), lax.axis_index("c")
        me = 2 * x + y
        mine = pltpu.make_async_copy(s_ref.at[me], out_ref.at[me], local_sem)
        mine.start()

        def copy(j, px, py, src_slot, dst_slot):
            return pltpu.make_async_remote_copy(src_ref=s_ref.at[src_slot], dst_ref=out_ref.at[dst_slot],
                                                send_sem=send_sems.at[j], recv_sem=recv_sems.at[j],
                                                device_id=(px, py, c), device_id_type=MESH)

        sends = [copy(j, px, py, 2 * px + py, me) for j, (px, py) in enumerate(_chips(x, y))]
        for cp in sends:
            cp.start()
        for j, (px, py) in enumerate(_chips(x, y)):
            copy(j, px, py, me, 2 * px + py).wait_recv()
        for cp in sends:
            cp.wait_send()
        mine.wait()

    return pl.pallas_call(
        body, name="grad_chip_exchange", in_specs=[HBM_SPEC], out_specs=HBM_SPEC, out_shape=SDS(s4.shape, s4.dtype),
        scratch_shapes=[pltpu.SemaphoreType.DMA((3,)), pltpu.SemaphoreType.DMA((3,)), pltpu.SemaphoreType.DMA(())],
    )(s4)


def _sibling_allgather(f):
    def body(f_ref, out_ref, send_sem, recv_sem, local_sem):
        x, y, c = lax.axis_index("x"), lax.axis_index("y"), lax.axis_index("c")
        mine = pltpu.make_async_copy(f_ref, out_ref.at[c], local_sem)
        mine.start()
        cp = pltpu.make_async_remote_copy(src_ref=f_ref, dst_ref=out_ref.at[c], send_sem=send_sem, recv_sem=recv_sem,
                                          device_id=(x, y, 1 - c), device_id_type=MESH)
        cp.start()
        pltpu.make_async_remote_copy(src_ref=f_ref, dst_ref=out_ref.at[1 - c], send_sem=send_sem, recv_sem=recv_sem,
                                     device_id=(x, y, 1 - c), device_id_type=MESH).wait_recv()
        cp.wait_send()
        mine.wait()

    return pl.pallas_call(
        body, name="grad_sibling_allgather", in_specs=[HBM_SPEC], out_specs=HBM_SPEC,
        out_shape=SDS((2,) + f.shape, f.dtype),
        scratch_shapes=[pltpu.SemaphoreType.DMA(()), pltpu.SemaphoreType.DMA(()), pltpu.SemaphoreType.DMA(())],
    )(f)


def _add_halves(g4, r4, c_idx, *, tr):
    _, _, rows, lanes = g4.shape

    def body(c_ref, a_ref, b_ref, o_ref):
        o_ref[...] = a_ref[...] + b_ref[...]

    grid_spec = pltpu.PrefetchScalarGridSpec(
        num_scalar_prefetch=1, grid=(N_CHIPS, rows // tr),
        in_specs=[pl.BlockSpec((None, None, tr, lanes), lambda q, i, c_ref: (q, c_ref[0], i, 0)),
                  pl.BlockSpec((None, tr, lanes), lambda q, i, c_ref: (q, i, 0))],
        out_specs=pl.BlockSpec((None, tr, lanes), lambda q, i, c_ref: (q, i, 0)))
    return pl.pallas_call(body, name="grad_add_halves", grid_spec=grid_spec, out_shape=SDS(r4.shape, F32),
                          compiler_params=_params(("parallel", "parallel")))(c_idx, g4, r4)


def _sum_chips(r4, *, tr):
    _, rows, lanes = r4.shape

    def body(a_ref, b_ref, c_ref, d_ref, o_ref):
        o_ref[...] = ((a_ref[...] + b_ref[...]) + c_ref[...]) + d_ref[...]

    specs = [pl.BlockSpec((None, tr, lanes), lambda i, q=q: (q, i, 0)) for q in range(N_CHIPS)]
    return pl.pallas_call(body, name="grad_sum_chips", grid=(rows // tr,), in_specs=specs,
                          out_specs=pl.BlockSpec((tr, lanes), lambda i: (i, 0)), out_shape=SDS((rows, lanes), F32),
                          compiler_params=_params(("parallel",)))(r4, r4, r4, r4)


def _adamw(wf, gf, mf, vf, *, tr):
    rows, lanes = wf.shape
    c1 = 1.0 / (1.0 - ADAM_B1 ** ADAM_STEP)
    c2 = 1.0 / (1.0 - ADAM_B2 ** ADAM_STEP)

    def body(w_ref, g_ref, m_ref, v_ref, d_ref, nm_ref, nv_ref):
        gv = g_ref[...]
        m = ADAM_B1 * m_ref[...] + (1.0 - ADAM_B1) * gv
        v = ADAM_B2 * v_ref[...] + (1.0 - ADAM_B2) * (gv * gv)
        d_ref[...] = -ADAM_LR * ((m * c1) / (jnp.sqrt(v * c2) + ADAM_EPS) + ADAM_WD * w_ref[...])
        nm_ref[...] = m
        nv_ref[...] = v

    spec = pl.BlockSpec((tr, lanes), lambda i: (i, 0))
    return pl.pallas_call(body, name="adamw", grid=(rows // tr,), in_specs=[spec] * 4, out_specs=[spec] * 3,
                          out_shape=[SDS((rows, lanes), F32)] * 3, compiler_params=_params(("parallel",)))(wf, gf, mf, vf)


N_SMALL_ELEMS = 4 * D_MODEL + 2 * W_A + W_A + N_RWKV_COLS + 7 * W_B
SEG_ELEMS = N_SHARDED_ELEMS + N_SMALL_ELEMS
HALF_ROWS = -(-SEG_ELEMS // (2 * 8 * LANES)) * 8
SEG_ROWS = 2 * HALF_ROWS
SEG_PAD = SEG_ROWS * LANES


def _flat_segment(sharded_parts, small_parts):
    parts = [p.reshape(-1) for p in sharded_parts] + [p.reshape(-1) for p in small_parts]
    flat = jnp.concatenate(parts)
    return jnp.pad(flat, (0, SEG_PAD - flat.shape[0]))


def _split_segment(flat):
    out, off = {}, 0
    for n in SHARDED:
        shp = _shard_shape(n)
        out[n] = flat[off:off + _numel(shp)].reshape((1,) + shp)
        off += _numel(shp)
    small_shapes = {"ffn1_norm": (1, D_MODEL), "mix_norm": (1, D_MODEL), "hgrn_lb_logits": (2, W_A),
                    "hgrn_out_norm": (1, W_A), "rwkv_shift_mu": (1, N_RWKV_COLS), "rwkv_w0": (1, W_B),
                    "rwkv_a0": (1, W_B), "rwkv_k_k": (1, W_B), "rwkv_k_a": (1, W_B),
                    "rwkv_r_k": (1, HB_HEADS, HB_DIM), "rwkv_gn_w": (1, W_B), "rwkv_gn_b": (1, W_B),
                    "ffn2_norm": (1, D_MODEL), "final_norm": (D_MODEL,)}
    for n in SMALL:
        shp = small_shapes[n]
        out[n] = flat[off:off + _numel(shp)].reshape(shp)
        off += _numel(shp)
    return out


def _quarter(full, name, q):
    shape, ax = SHARDED_SHAPES[name]
    w = shape[ax] // N_CHIPS
    return lax.slice_in_dim(full, q * w, (q + 1) * w, axis=ax)


def kernel(x, ffn1_norm, ffn1_w_gate, ffn1_w_up, ffn1_w_down, mix_norm, w_in, hgrn_lb_logits, hgrn_out_norm, rwkv_shift_mu, rwkv_w0, rwkv_w2, rwkv_a0, rwkv_a2, rwkv_g2, rwkv_k_k, rwkv_k_a, rwkv_r_k, rwkv_gn_w, rwkv_gn_b, w_out, ffn2_norm, ffn2_w_gate, ffn2_w_up, ffn2_w_down, final_norm, loss_target, m_ffn1_norm, m_ffn1_w_gate, m_ffn1_w_up, m_ffn1_w_down, m_mix_norm, m_w_in, m_hgrn_lb_logits, m_hgrn_out_norm, m_rwkv_shift_mu, m_rwkv_w0, m_rwkv_w2, m_rwkv_a0, m_rwkv_a2, m_rwkv_g2, m_rwkv_k_k, m_rwkv_k_a, m_rwkv_r_k, m_rwkv_gn_w, m_rwkv_gn_b, m_w_out, m_ffn2_norm, m_ffn2_w_gate, m_ffn2_w_up, m_ffn2_w_down, m_final_norm, v_ffn1_norm, v_ffn1_w_gate, v_ffn1_w_up, v_ffn1_w_down, v_mix_norm, v_w_in, v_hgrn_lb_logits, v_hgrn_out_norm, v_rwkv_shift_mu, v_rwkv_w0, v_rwkv_w2, v_rwkv_a0, v_rwkv_a2, v_rwkv_g2, v_rwkv_k_k, v_rwkv_k_a, v_rwkv_r_k, v_rwkv_gn_w, v_rwkv_gn_b, v_w_out, v_ffn2_norm, v_ffn2_w_gate, v_ffn2_w_up, v_ffn2_w_down, v_final_norm):
    args = dict(locals())
    wts = {n: args[n] for n in ALL_WEIGHTS}
    moms = {n: args["m_" + n] for n in ALL_WEIGHTS}
    vars_ = {n: args["v_" + n] for n in ALL_WEIGHTS}

    wq = jnp.concatenate([wts[n].reshape(-1) for n in SHARDED]).astype(BF16)
    wq = jnp.pad(wq, (0, W_ROWS * LANES - wq.shape[0])).reshape(W_ROWS, LANES)
    gathered = _gather_weights(wq).reshape(N_CHIPS, W_ROWS * LANES)
    full, off = {}, 0
    for n in SHARDED:
        shp = _shard_shape(n)
        ax = SHARDED_SHAPES[n][1]
        full[n] = jnp.concatenate([gathered[q, off:off + _numel(shp)].reshape(shp) for q in range(N_CHIPS)], axis=ax)
        off += _numel(shp)

    w = {}
    for tag in ("ffn1", "ffn2"):
        w[f"{tag}_wgu"] = jnp.concatenate([full[f"{tag}_w_gate"], full[f"{tag}_w_up"]], axis=1)
        w[f"{tag}_wd"] = full[f"{tag}_w_down"]
        w[f"{tag}_norm"] = wts[f"{tag}_norm"]
    w["w_in_h"] = full["w_in"][:, :N_HGRN_COLS]
    w["w_in_r"] = jnp.pad(full["w_in"][:, N_HGRN_COLS:], ((0, 0), (0, N_RWKV_PAD - N_RWKV_COLS)))
    w["w_out"] = full["w_out"]
    zrow = lambda nrow: jnp.zeros((nrow, W_B), BF16)
    w["w2_pad"] = jnp.concatenate([full["rwkv_w2"], zrow(LORA_PAD - 32)], axis=0)
    w["a2_pad"] = jnp.concatenate([zrow(32), full["rwkv_a2"], zrow(LORA_PAD - 64)], axis=0)
    w["g2_pad"] = jnp.concatenate([zrow(64), full["rwkv_g2"], zrow(LORA_PAD - 160)], axis=0)
    w["mix_norm"] = mix_norm
    w["lb0"], w["lb1"] = hgrn_lb_logits[0:1], hgrn_lb_logits[1:2]
    w["hgrn_out_norm"] = hgrn_out_norm
    w["mu_pad"] = jnp.pad(rwkv_shift_mu, ((0, 0), (0, N_RWKV_PAD - N_RWKV_COLS)))
    for n in ("rwkv_w0", "rwkv_a0", "rwkv_k_k", "rwkv_k_a", "rwkv_gn_w", "rwkv_gn_b"):
        w[n] = wts[n]
    w["rwkv_r_k"] = rwkv_r_k.reshape(1, W_B)
    w["final_norm"] = final_norm.reshape(1, D_MODEL)

    loss_slab, grad_x, g = _local_step(x[0], loss_target[0], w)
    loss = lax.psum(loss_slab[0, 0], ("x", "y", "c"))

    gfull = {
        "ffn1_w_gate": g["ffn1_wgu"][:, :D_FF], "ffn1_w_up": g["ffn1_wgu"][:, D_FF:], "ffn1_w_down": g["ffn1_wd"],
        "ffn2_w_gate": g["ffn2_wgu"][:, :D_FF], "ffn2_w_up": g["ffn2_wgu"][:, D_FF:], "ffn2_w_down": g["ffn2_wd"],
        "w_in": jnp.concatenate([g["w_in_h"], g["w_in_r"][:, :N_RWKV_COLS]], axis=1),
        "rwkv_w2": g["w2_pad"][0:32], "rwkv_a2": g["a2_pad"][32:64], "rwkv_g2": g["g2_pad"][64:160],
        "w_out": g["w_out"],
    }
    gsmall = {
        "ffn1_norm": g["ffn1_norm"], "mix_norm": g["mix_norm"],
        "hgrn_lb_logits": jnp.concatenate([g["lb0"], g["lb1"]], axis=0), "hgrn_out_norm": g["hgrn_out_norm"],
        "rwkv_shift_mu": g["mu_pad"][:, :N_RWKV_COLS], "rwkv_w0": g["rwkv_w0"], "rwkv_a0": g["rwkv_a0"],
        "rwkv_k_k": g["rwkv_k_k"], "rwkv_k_a": g["rwkv_k_a"], "rwkv_r_k": g["rwkv_r_k"],
        "rwkv_gn_w": g["rwkv_gn_w"], "rwkv_gn_b": g["rwkv_gn_b"], "ffn2_norm": g["ffn2_norm"],
        "final_norm": g["final_norm"],
    }
    small_list = [gsmall[n] for n in SMALL]
    g4 = jnp.stack([_flat_segment([_quarter(gfull[n], n, q) for n in SHARDED], small_list) for q in range(N_CHIPS)])
    g4 = g4.reshape(N_CHIPS, 2, HALF_ROWS, LANES)
    c_idx = lax.axis_index("c").astype(jnp.int32).reshape(1)
    tr = HALF_ROWS // 11
    r1 = _sibling_exchange(g4)
    s4 = _add_halves(g4, r1, c_idx, tr=tr)
    r2 = _chip_exchange(s4)
    fhalf = _sum_chips(r2, tr=tr)
    gq = _sibling_allgather(fhalf).reshape(SEG_ROWS, LANES)

    flat_of = lambda d: _flat_segment([d[n] for n in SHARDED], [d[n] for n in SMALL]).reshape(SEG_ROWS, LANES)
    delta, new_m, new_v = _adamw(flat_of(wts), gq, flat_of(moms), flat_of(vars_), tr=SEG_ROWS // 22)

    grads, deltas, nms, nvs = (_split_segment(z.reshape(-1)) for z in (gq, delta, new_m, new_v))
    return (loss, grad_x[None], *[grads[n] for n in ALL_WEIGHTS], *[deltas[n] for n in ALL_WEIGHTS],
            *[nms[n] for n in ALL_WEIGHTS], *[nvs[n] for n in ALL_WEIGHTS])
```

```python
import functools

import jax
import jax.numpy as jnp
from jax import lax
from jax.experimental import pallas as pl
from jax.experimental.pallas import tpu as pltpu

F32 = jnp.float32
BF16 = jnp.bfloat16
SDS = jax.ShapeDtypeStruct
MESH = pl.DeviceIdType.MESH

D_MODEL = 1024
D_FF = 2816
W_A = 512
W_B = 512
HA_HEADS, HA_DIM = 4, 128
HB_HEADS, HB_DIM = 8, 64
HGRN_CHUNK = 64
RWKV_CHUNK = 16
RWKV_GROUP = 4
N_HGRN_COLS = 4 * W_A
N_RWKV_COLS = 3 * W_B + 32 + 32 + 96
N_RWKV_PAD = 1792
LORA_PAD = 256
NORM_EPS = 1e-6
RWKV_GN_EPS = 64e-5
L2_EPS = 1e-12
ADAM_LR, ADAM_B1, ADAM_B2, ADAM_EPS, ADAM_WD, ADAM_STEP = 0.001, 0.9, 0.999, 1e-8, 0.01, 10

N_CHIPS = 4
VMEM_LIMIT_V7X = 56 * 1024 * 1024
LANES = 1024

SHARDED = ("ffn1_w_gate", "ffn1_w_up", "ffn1_w_down", "w_in", "rwkv_w2", "rwkv_a2", "rwkv_g2", "w_out",
           "ffn2_w_gate", "ffn2_w_up", "ffn2_w_down")
SHARDED_SHAPES = {
    "ffn1_w_gate": ((D_MODEL, D_FF), 1), "ffn1_w_up": ((D_MODEL, D_FF), 1), "ffn1_w_down": ((D_FF, D_MODEL), 0),
    "w_in": ((D_MODEL, N_HGRN_COLS + N_RWKV_COLS), 1), "rwkv_w2": ((32, W_B), 1), "rwkv_a2": ((32, W_B), 1),
    "rwkv_g2": ((96, W_B), 1), "w_out": ((D_MODEL, D_MODEL), 0),
    "ffn2_w_gate": ((D_MODEL, D_FF), 1), "ffn2_w_up": ((D_MODEL, D_FF), 1), "ffn2_w_down": ((D_FF, D_MODEL), 0),
}
SMALL = ("ffn1_norm", "mix_norm", "hgrn_lb_logits", "hgrn_out_norm", "rwkv_shift_mu", "rwkv_w0", "rwkv_a0",
         "rwkv_k_k", "rwkv_k_a", "rwkv_r_k", "rwkv_gn_w", "rwkv_gn_b", "ffn2_norm", "final_norm")
ALL_WEIGHTS = ("ffn1_norm", "ffn1_w_gate", "ffn1_w_up", "ffn1_w_down", "mix_norm", "w_in", "hgrn_lb_logits",
               "hgrn_out_norm", "rwkv_shift_mu", "rwkv_w0", "rwkv_w2", "rwkv_a0", "rwkv_a2", "rwkv_g2", "rwkv_k_k",
               "rwkv_k_a", "rwkv_r_k", "rwkv_gn_w", "rwkv_gn_b", "w_out", "ffn2_norm", "ffn2_w_gate", "ffn2_w_up",
               "ffn2_w_down", "final_norm")


def _shard_shape(name):
    shape, ax = SHARDED_SHAPES[name]
    return tuple(s // N_CHIPS if i == ax else s for i, s in enumerate(shape))


def _numel(shape):
    n = 1
    for s in shape:
        n *= s
    return n


N_SHARDED_ELEMS = sum(_numel(_shard_shape(n)) for n in SHARDED)
W_ROWS = -(-N_SHARDED_ELEMS // (16 * LANES)) * 16


def _params(sem=None):
    return pltpu.CompilerParams(dimension_semantics=sem, vmem_limit_bytes=VMEM_LIMIT_V7X)


def _dg(x, y, cx, cy, hi):
    dn = (((cx,), (cy,)), ((), ()))
    if hi:
        return lax.dot_general(x.astype(F32), y.astype(F32), dn, precision=lax.Precision.HIGHEST,
                               preferred_element_type=F32)
    return lax.dot_general(x.astype(BF16), y.astype(BF16), dn, preferred_element_type=F32)


def _make_mm(hi):
    @jax.custom_vjp
    def nn(x, y):
        return _dg(x, y, 1, 0, hi)

    @jax.custom_vjp
    def nt(x, y):
        return _dg(x, y, 1, 1, hi)

    @jax.custom_vjp
    def tn(x, y):
        return _dg(x, y, 0, 0, hi)

    nn.defvjp(lambda x, y: (nn(x, y), (x, y)), lambda r, g: (nt(g, r[1]), tn(r[0], g)))
    nt.defvjp(lambda x, y: (nt(x, y), (x, y)), lambda r, g: (nn(g, r[1]), tn(g, r[0])))
    tn.defvjp(lambda x, y: (tn(x, y), (x, y)), lambda r, g: (nt(r[1], g), nn(r[0], g)))
    return nn, nt, tn


_nn, _nt, _tn = _make_mm(False)
_nn_hi, _nt_hi, _tn_hi = _make_mm(True)


def _sigmoid(x):
    return 1.0 / (1.0 + jnp.exp(-x))


def _silu(x):
    return x * _sigmoid(x)


def _softplus(z):
    return jnp.maximum(z, 0.0) + jnp.log(1.0 + jnp.exp(-jnp.abs(z)))


def _mm(a, b, *, ta=False, tb=False, tm, tn, tk, name, out_dtype=F32, res=None, scale=None):
    m = a.shape[1] if ta else a.shape[0]
    kdim = a.shape[0] if ta else a.shape[1]
    n = b.shape[0] if tb else b.shape[1]
    assert (b.shape[1] if tb else b.shape[0]) == kdim
    tm, tn, tk = min(tm, m), min(tn, n), min(tk, kdim)
    assert m % tm == 0 and n % tn == 0 and kdim % tk == 0, (name, m, n, kdim)
    nk = kdim // tk
    a_spec = pl.BlockSpec((tk, tm), lambda i, j, k: (k, i)) if ta else pl.BlockSpec((tm, tk), lambda i, j, k: (i, k))
    b_spec = pl.BlockSpec((tn, tk), lambda i, j, k: (j, k)) if tb else pl.BlockSpec((tk, tn), lambda i, j, k: (k, j))
    o_spec = pl.BlockSpec((tm, tn), lambda i, j, k: (i, j))
    ca, cb = (0 if ta else 1), (1 if tb else 0)

    def body(*refs):
        if res is not None:
            a_ref, b_ref, r_ref, o_ref, acc_ref = refs
        else:
            a_ref, b_ref, o_ref, acc_ref = refs
        k = pl.program_id(2)

        @pl.when(k == 0)
        def _():
            acc_ref[...] = jnp.zeros_like(acc_ref)

        acc_ref[...] += _dg(a_ref[...], b_ref[...], ca, cb, False)

        @pl.when(k == nk - 1)
        def _():
            acc = acc_ref[...]
            if scale is not None:
                acc = acc * scale
            if res is not None:
                acc = r_ref[...] + acc
            o_ref[...] = acc.astype(out_dtype)

    in_specs = [a_spec, b_spec] + ([o_spec] if res is not None else [])
    args = (a, b) + ((res,) if res is not None else ())
    return pl.pallas_call(
        body, name=name, grid=(m // tm, n // tn, nk), in_specs=in_specs, out_specs=o_spec,
        out_shape=SDS((m, n), out_dtype), scratch_shapes=[pltpu.VMEM((tm, tn), F32)],
        compiler_params=_params(("parallel", "parallel", "arbitrary")))(*args)


def _row_spec(x, tm):
    if isinstance(x, tuple):
        arr, w, j = x
        return arr, pl.BlockSpec((tm, w), lambda i, j=j: (i, j))
    return x, pl.BlockSpec((tm, x.shape[1]), lambda i: (i, 0))


def _par_spec(p):
    if isinstance(p, tuple):
        arr, w, j = p
        return arr, pl.BlockSpec((arr.shape[0], w), lambda i, j=j: (0, j))
    return p, pl.BlockSpec(p.shape, lambda i: (0, 0))


def _store_groups(refs, groups, vals):
    for ref, idxs in zip(refs, groups):
        off = 0
        for ix in idxs:
            v = vals[ix]
            ref[:, off:off + v.shape[1]] = v.astype(ref.dtype)
            off += v.shape[1]


def _rowwise(f, xs, params, out_groups, out_dtypes, *, tm, name):
    tm = min(tm, (xs[0][0] if isinstance(xs[0], tuple) else xs[0]).shape[0])
    xa, xspecs = zip(*[_row_spec(x, tm) for x in xs])
    pa, pspecs = (zip(*[_par_spec(p) for p in params]) if params else ((), ()))
    t = xa[0].shape[0]
    nx, npar = len(xa), len(pa)
    widths = [None] * len(out_groups)

    def probe(*vals):
        return f(*vals)

    x_sds = [SDS(s.block_shape, F32) for s in xspecs]
    p_sds = [SDS(s.block_shape, F32) for s in pspecs]
    outs_sds = jax.eval_shape(probe, *x_sds, *p_sds)
    for g, idxs in enumerate(out_groups):
        widths[g] = sum(outs_sds[ix].shape[1] for ix in idxs)

    def body(*refs):
        vals = [r[...].astype(F32) for r in refs[:nx + npar]]
        outs = f(*vals)
        _store_groups(refs[nx + npar:], out_groups, outs)

    return pl.pallas_call(
        body, name=name, grid=(t // tm,), in_specs=list(xspecs) + list(pspecs),
        out_specs=[pl.BlockSpec((tm, w), lambda i: (i, 0)) for w in widths],
        out_shape=[SDS((t, w), dt) for w, dt in zip(widths, out_dtypes)],
        compiler_params=_params(("parallel",)))(*xa, *pa)


def _rowwise_bwd(f, xs, params, cots, *, x_grad, p_grad, dx_groups, dx_dtypes, tm, name, extra=None):
    tm = min(tm, (xs[0][0] if isinstance(xs[0], tuple) else xs[0]).shape[0])
    xa, xspecs = zip(*[_row_spec(x, tm) for x in xs])
    pa, pspecs = (zip(*[_par_spec(p) for p in params]) if params else ((), ()))
    ca, cspecs = zip(*[_row_spec(c, tm) for c in cots])
    extra = extra or {}
    ekeys = sorted(extra)
    ea, especs = (zip(*[_row_spec(extra[k], tm) for k in ekeys]) if ekeys else ((), ()))
    t = xa[0].shape[0]
    nx, npar, nc, ne = len(xa), len(pa), len(ca), len(ea)
    gx = [i for i in range(nx) if x_grad[i]]
    gp = [i for i in range(npar) if p_grad[i]]
    widths = [sum(xspecs[gx[ix]].block_shape[1] for ix in idxs) for idxs in dx_groups]
    ng = len(dx_groups)

    def body(*refs):
        ins = refs[:nx + npar + nc + ne]
        outs = refs[nx + npar + nc + ne:]
        vals = [r[...].astype(F32) for r in ins[:nx + npar]]
        cvals = tuple(r[...].astype(F32) for r in ins[nx + npar:nx + npar + nc])
        evals = [r[...].astype(F32) for r in ins[nx + npar + nc:]]
        diff_idx = gx + [nx + i for i in gp]

        def g(*dargs):
            full = list(vals)
            for ix, v in zip(diff_idx, dargs):
                full[ix] = v
            return tuple(f(*full))

        _, vjp = jax.vjp(g, *[vals[ix] for ix in diff_idx])
        grads = vjp(cvals)
        dxs = list(grads[:len(gx)])
        for k, ev in zip(ekeys, evals):
            dxs[k] = dxs[k] + ev
        _store_groups(outs[:ng], dx_groups, dxs)
        i = pl.program_id(0)
        for ref, gval in zip(outs[ng:], grads[len(gx):]):
            @pl.when(i == 0)
            def _(ref=ref):
                ref[...] = jnp.zeros_like(ref)
            ref[...] += gval

    dp_specs = [pl.BlockSpec(pspecs[i].block_shape, lambda i: (0, 0)) for i in gp]
    dp_shapes = [SDS(pspecs[i].block_shape, F32) for i in gp]
    return pl.pallas_call(
        body, name=name, grid=(t // tm,), in_specs=list(xspecs) + list(pspecs) + list(cspecs) + list(especs),
        out_specs=[pl.BlockSpec((tm, w), lambda i: (i, 0)) for w in widths] + dp_specs,
        out_shape=[SDS((t, w), dt) for w, dt in zip(widths, dx_dtypes)] + dp_shapes,
        compiler_params=_params(("arbitrary",)))(*xa, *pa, *ca, *ea)


def _rms_f(x, g):
    return (x * lax.rsqrt(jnp.mean(x * x, axis=-1, keepdims=True) + NORM_EPS) * g,)


def _swiglu_f(a, u):
    return (_silu(a) * u,)


def _group_sum(x, ones_bd):
    return _nn_hi(x, ones_bd)


def _rwkv_prep_f(r, k, v, lo, rp, kp, vp, lop, mu_r, mu_k, mu_v, mu_lo, w0, w2p, a0, a2p, g2p, k_k, k_a, ones_bd):
    r = r + mu_r * (rp - r)
    k = k + mu_k * (kp - k)
    v = v + mu_v * (vp - v)
    lo = lo + mu_lo * (lop - lo)
    w_log = -_softplus(-(w0 + _nn(jnp.tanh(lo), w2p))) - 0.5
    lw = -jnp.exp(w_log)
    a_g = _sigmoid(a0 + _nn(lo, a2p))
    g = _nn(_sigmoid(lo), g2p)
    kk = k * k_k
    kk = kk / jnp.maximum(jnp.sqrt(_group_sum(kk * kk, ones_bd)), L2_EPS)
    k2 = k * (1.0 + (a_g - 1.0) * k_a)
    return r, lw, k2, v, -kk, kk * a_g, g


def _rwkv_post_f(y, r, k2, v, g, r_k, gn_w, gn_b, ones_bd):
    inv_n = 1.0 / HB_DIM
    mean = _group_sum(y, ones_bd) * inv_n
    yc = y - mean
    var = _group_sum(yc * yc, ones_bd) * inv_n
    yn = yc * lax.rsqrt(var + RWKV_GN_EPS) * gn_w + gn_b
    bonus = _group_sum(r * k2 * r_k, ones_bd) * v
    return ((yn + bonus) * g,)


def _tri(c, strict=False):
    ii = lax.broadcasted_iota(jnp.int32, (c, c), 0)
    jj = lax.broadcasted_iota(jnp.int32, (c, c), 1)
    return (jj < ii) if strict else (jj <= ii)


def _hgrn_head(st0, q_a, f_a, i_a, g_a, l0, l1, onorm):
    c = q_a.shape[0]
    mx = jnp.maximum(l0, l1)
    e0, e1 = jnp.exp(l0 - mx), jnp.exp(l1 - mx)
    lb = e0 / (e0 + e1)
    forget = lb + (1.0 - lb) * _sigmoid(f_a)
    q = _silu(q_a)
    kk = 1.0 - forget
    lf = jnp.log(forget)
    incl = _tri(c)
    bcum = _nn_hi(incl.astype(F32), lf)
    rows = lax.broadcasted_iota(jnp.int32, (c, 1), 0)
    bref = jnp.sum(jnp.where(rows <= c // 2, lf, 0.0), axis=0, keepdims=True)
    blast = jnp.sum(lf, axis=0, keepdims=True)
    scores = jnp.where(incl, _nt(q * jnp.exp(bcum - bref), kk * jnp.exp(bref - bcum)), 0.0)
    o = _nn(scores, i_a) + _nt(q * jnp.exp(bcum), st0)
    st1 = st0 * jnp.exp(blast) + _tn(i_a, kk * jnp.exp(blast - bcum))
    o = o * lax.rsqrt(jnp.mean(o * o, axis=-1, keepdims=True) + NORM_EPS)
    return o * onorm * _silu(g_a), st1


def _hgrn_fwd(p_h, l0, l1, onorm):
    t = p_h.shape[0]
    c, n = HGRN_CHUNK, p_h.shape[0] // HGRN_CHUNK

    def body(q_ref, f_ref, i_ref, g_ref, l0_ref, l1_ref, on_ref, o_ref, hs_ref, st_ref):
        @pl.when(pl.program_id(0) == 0)
        def _():
            st_ref[...] = jnp.zeros_like(st_ref)

        hs_ref[0] = st_ref[...]
        for h in range(HA_HEADS):
            sl = slice(h * HA_DIM, (h + 1) * HA_DIM)
            o, st1 = _hgrn_head(st_ref[h], q_ref[:, sl], f_ref[:, sl], i_ref[:, sl], g_ref[:, sl],
                                l0_ref[:, sl], l1_ref[:, sl], on_ref[:, sl])
            o_ref[:, sl] = o
            st_ref[h] = st1

    col = lambda j: pl.BlockSpec((c, W_A), lambda i, j=j: (i, j))
    par = pl.BlockSpec((1, W_A), lambda i: (0, 0))
    return pl.pallas_call(
        body, name="hgrn_fwd", grid=(n,), in_specs=[col(0), col(1), col(2), col(3), par, par, par],
        out_specs=[pl.BlockSpec((c, W_A), lambda i: (i, 0)),
                   pl.BlockSpec((1, HA_HEADS, HA_DIM, HA_DIM), lambda i: (i, 0, 0, 0))],
        out_shape=[SDS((t, W_A), F32), SDS((n, HA_HEADS, HA_DIM, HA_DIM), F32)],
        scratch_shapes=[pltpu.VMEM((HA_HEADS, HA_DIM, HA_DIM), F32)],
        compiler_params=_params(("arbitrary",)))(p_h, p_h, p_h, p_h, l0, l1, onorm)


def _hgrn_bwd(p_h, l0, l1, onorm, hs, do, do_col):
    t = p_h.shape[0]
    c, n = HGRN_CHUNK, p_h.shape[0] // HGRN_CHUNK

    def body(q_ref, f_ref, i_ref, g_ref, l0_ref, l1_ref, on_ref, hs_ref, do_ref,
             dp_ref, dl0_ref, dl1_ref, don_ref, dst_ref):
        @pl.when(pl.program_id(0) == 0)
        def _():
            dst_ref[...] = jnp.zeros_like(dst_ref)
            dl0_ref[...] = jnp.zeros_like(dl0_ref)
            dl1_ref[...] = jnp.zeros_like(dl1_ref)
            don_ref[...] = jnp.zeros_like(don_ref)

        for h in range(HA_HEADS):
            sl = slice(h * HA_DIM, (h + 1) * HA_DIM)
            args = (hs_ref[0, h], q_ref[:, sl], f_ref[:, sl], i_ref[:, sl], g_ref[:, sl],
                    l0_ref[:, sl], l1_ref[:, sl], on_ref[:, sl])
            _, vjp = jax.vjp(_hgrn_head, *args)
            dst0, dq, df, di, dg, dl0, dl1, don = vjp((do_ref[:, sl], dst_ref[h]))
            for j, dv in enumerate((dq, df, di, dg)):
                dp_ref[:, j * W_A + h * HA_DIM:j * W_A + (h + 1) * HA_DIM] = dv
            dl0_ref[:, sl] += dl0
            dl1_ref[:, sl] += dl1
            don_ref[:, sl] += don
            dst_ref[h] = dst0

    col = lambda j: pl.BlockSpec((c, W_A), lambda i, j=j: (n - 1 - i, j))
    par = pl.BlockSpec((1, W_A), lambda i: (0, 0))
    return pl.pallas_call(
        body, name="hgrn_bwd", grid=(n,),
        in_specs=[col(0), col(1), col(2), col(3), par, par, par,
                  pl.BlockSpec((1, HA_HEADS, HA_DIM, HA_DIM), lambda i: (n - 1 - i, 0, 0, 0)),
                  pl.BlockSpec((c, W_A), lambda i: (n - 1 - i, do_col))],
        out_specs=[pl.BlockSpec((c, N_HGRN_COLS), lambda i: (n - 1 - i, 0)), par, par, par],
        out_shape=[SDS((t, N_HGRN_COLS), F32), SDS((1, W_A), F32), SDS((1, W_A), F32), SDS((1, W_A), F32)],
        scratch_shapes=[pltpu.VMEM((HA_HEADS, HA_DIM, HA_DIM), F32)],
        compiler_params=_params(("arbitrary",)))(p_h, p_h, p_h, p_h, l0, l1, onorm, hs, do)


def _rwkv_step(s0, r, lw, k, v, a, b):
    nh, nj = len(r), len(r[0])
    c = r[0][0].shape[0]
    pairs = [(j, h) for j in range(nj) for h in range(nh)]
    every = lambda fn: {p: fn(p) for p in pairs}
    at_ = lambda d: (lambda p: d[p[1]][p[0]])
    r_, lw_, k_, v_, a_, b_ = (at_(z) for z in (r, lw, k, v, a, b))
    incl, strict = _tri(c), _tri(c, strict=True)
    incl_f = incl.astype(F32)
    eye = (lax.broadcasted_iota(jnp.int32, (c, c), 0) == lax.broadcasted_iota(jnp.int32, (c, c), 1)).astype(F32)

    gam = every(lambda p: _nn_hi(incl_f, lw_(p)))
    gtot = every(lambda p: jnp.sum(lw_(p), axis=0, keepdims=True))
    at = every(lambda p: a_(p) * jnp.exp(gam[p] - lw_(p)))
    rt = every(lambda p: r_(p) * jnp.exp(gam[p]))
    eneg = every(lambda p: jnp.exp(-gam[p]))
    bt = every(lambda p: b_(p) * eneg[p])
    kt = every(lambda p: k_(p) * eneg[p])
    edec = every(lambda p: jnp.exp(gtot[p] - gam[p]))
    bdec = every(lambda p: b_(p) * edec[p])
    kdec = every(lambda p: k_(p) * edec[p])
    a_ab = every(lambda p: jnp.where(strict, _nt(at[p], bt[p]), 0.0))
    a_ak = every(lambda p: jnp.where(strict, _nt(at[p], kt[p]), 0.0))
    a_rb = every(lambda p: jnp.where(incl, _nt(rt[p], bt[p]), 0.0))
    a_rk = every(lambda p: jnp.where(incl, _nt(rt[p], kt[p]), 0.0))
    tinv = every(lambda p: eye + a_ab[p])
    pw = a_ab
    span = 2
    while span < c:
        pw = every(lambda p, pw=pw: _nn_hi(pw[p], pw[p]))
        tinv = every(lambda p, pw=pw, tinv=tinv: tinv[p] + _nn_hi(pw[p], tinv[p]))
        span *= 2
    akv = every(lambda p: _nn(a_ak[p], v_(p)))
    w1 = every(lambda p: _nn_hi(tinv[p], at[p]))
    u0 = every(lambda p: _nn_hi(tinv[p], akv[p]))
    r1 = every(lambda p: rt[p] + _nn(a_rb[p], w1[p]))
    y0 = every(lambda p: _nn(a_rb[p], u0[p]) + _nn(a_rk[p], v_(p)))
    mm = every(lambda p: _tn(w1[p], bdec[p]))
    zz = every(lambda p: _tn(u0[p], bdec[p]) + _tn(v_(p), kdec[p]))
    gdec = every(lambda p: jnp.exp(gtot[p]))

    s = list(s0)
    y = [[None] * nj for _ in range(nh)]
    for j in range(nj):
        for h in range(nh):
            y[h][j] = _nt(r1[(j, h)], s[h]) + y0[(j, h)]
        s = [s[h] * gdec[(j, h)] + _nn(s[h], mm[(j, h)]) + zz[(j, h)] for h in range(nh)]
    return y, s


def _rwkv_blocks(ref, nj, c):
    return [[ref[h, j * c:(j + 1) * c, :] for j in range(nj)] for h in range(HB_HEADS)]


def _rwkv_fwd(seqs):
    t = seqs[0].shape[1]
    c, nj = RWKV_CHUNK, RWKV_GROUP
    n = t // (c * nj)

    def body(r_ref, lw_ref, k_ref, v_ref, a_ref, b_ref, y_ref, hs_ref, st_ref):
        @pl.when(pl.program_id(0) == 0)
        def _():
            st_ref[...] = jnp.zeros_like(st_ref)

        hs_ref[0] = st_ref[...]
        s0 = [st_ref[h] for h in range(HB_HEADS)]
        y, s1 = _rwkv_step(s0, *[_rwkv_blocks(ref, nj, c) for ref in (r_ref, lw_ref, k_ref, v_ref, a_ref, b_ref)])
        for h in range(HB_HEADS):
            for j in range(nj):
                y_ref[h, j * c:(j + 1) * c, :] = y[h][j]
            st_ref[h] = s1[h]

    seq = pl.BlockSpec((HB_HEADS, c * nj, HB_DIM), lambda i: (0, i, 0))
    return pl.pallas_call(
        body, name="rwkv_fwd", grid=(n,), in_specs=[seq] * 6,
        out_specs=[seq, pl.BlockSpec((1, HB_HEADS, HB_DIM, HB_DIM), lambda i: (i, 0, 0, 0))],
        out_shape=[SDS((HB_HEADS, t, HB_DIM), F32), SDS((n, HB_HEADS, HB_DIM, HB_DIM), F32)],
        scratch_shapes=[pltpu.VMEM((HB_HEADS, HB_DIM, HB_DIM), F32)],
        compiler_params=_params(("arbitrary",)))(*seqs)


def _rwkv_bwd(seqs, hs, dy):
    t = seqs[0].shape[1]
    c, nj = RWKV_CHUNK, RWKV_GROUP
    n = t // (c * nj)

    def body(r_ref, lw_ref, k_ref, v_ref, a_ref, b_ref, hs_ref, dy_ref,
             dr_ref, dlw_ref, dk_ref, dv_ref, da_ref, db_ref, dst_ref):
        @pl.when(pl.program_id(0) == 0)
        def _():
            dst_ref[...] = jnp.zeros_like(dst_ref)

        s0 = [hs_ref[0, h] for h in range(HB_HEADS)]
        seq_vals = [_rwkv_blocks(ref, nj, c) for ref in (r_ref, lw_ref, k_ref, v_ref, a_ref, b_ref)]
        _, vjp = jax.vjp(_rwkv_step, s0, *seq_vals)
        grads = vjp((_rwkv_blocks(dy_ref, nj, c), [dst_ref[h] for h in range(HB_HEADS)]))
        for ref, gr in zip((dr_ref, dlw_ref, dk_ref, dv_ref, da_ref, db_ref), grads[1:]):
            for h in range(HB_HEADS):
                for j in range(nj):
                    ref[h, j * c:(j + 1) * c, :] = gr[h][j]
        for h in range(HB_HEADS):
            dst_ref[h] = grads[0][h]

    seq = pl.BlockSpec((HB_HEADS, c * nj, HB_DIM), lambda i: (0, n - 1 - i, 0))
    return pl.pallas_call(
        body, name="rwkv_bwd", grid=(n,),
        in_specs=[seq] * 6 + [pl.BlockSpec((1, HB_HEADS, HB_DIM, HB_DIM), lambda i: (n - 1 - i, 0, 0, 0)), seq],
        out_specs=[seq] * 6, out_shape=[SDS((HB_HEADS, t, HB_DIM), F32)] * 6,
        scratch_shapes=[pltpu.VMEM((HB_HEADS, HB_DIM, HB_DIM), F32)],
        compiler_params=_params(("arbitrary",)))(*seqs, hs, dy)


def _final_loss(x3, fnorm, target, *, tm):
    t, d = x3.shape

    def body(x_ref, g_ref, t_ref, dx_ref, dg_ref, loss_ref):
        @pl.when(pl.program_id(0) == 0)
        def _():
            dg_ref[...] = jnp.zeros_like(dg_ref)
            loss_ref[...] = jnp.zeros_like(loss_ref)

        x, g = x_ref[...], g_ref[...]
        rinv = lax.rsqrt(jnp.mean(x * x, axis=-1, keepdims=True) + NORM_EPS)
        xh = x * rinv
        diff = xh * g - t_ref[...]
        loss_ref[...] += 0.5 * jnp.sum(jnp.mean(diff * diff, axis=-1, keepdims=True))
        dy = diff * (1.0 / d)
        dg_ref[...] += jnp.sum(dy * xh, axis=0, keepdims=True)
        dxh = dy * g
        dx_ref[...] = rinv * (dxh - xh * jnp.mean(dxh * xh, axis=-1, keepdims=True))

    row = pl.BlockSpec((tm, d), lambda i: (i, 0))
    return pl.pallas_call(
        body, name="final_loss", grid=(t // tm,), in_specs=[row, pl.BlockSpec((1, d), lambda i: (0, 0)), row],
        out_specs=[row, pl.BlockSpec((1, d), lambda i: (0, 0)), pl.BlockSpec((8, 128), lambda i: (0, 0))],
        out_shape=[SDS((t, d), F32), SDS((1, d), F32), SDS((8, 128), F32)],
        compiler_params=_params(("arbitrary",)))(x3, fnorm, target)


def _ffn_fwd(x, norm, wgu, wd, tag):
    h, = _rowwise(_rms_f, [x], [norm], [[0]], [BF16], tm=512, name=f"{tag}_rms")
    au = _mm(h, wgu, tm=512, tn=512, tk=D_MODEL, name=f"{tag}_gate_up")
    act, = _rowwise(_swiglu_f, [(au, D_FF, 0), (au, D_FF, 1)], [], [[0]], [BF16], tm=256, name=f"{tag}_act")
    out = _mm(act, wd, tm=512, tn=D_MODEL, tk=D_FF // 2, name=f"{tag}_down", res=x, scale=0.5)
    return out, (h, au, act)


def _ffn_bwd(dout, x, norm, wgu, wd, saved, tag):
    h, au, act = saved
    dact = _mm(dout, wd, tb=True, tm=512, tn=D_FF // 2, tk=D_MODEL, name=f"{tag}_dact", scale=0.5)
    dwd = _mm(act, dout, ta=True, tm=D_FF // 2, tn=D_MODEL, tk=512, name=f"{tag}_dwd", scale=0.5)
    dau, = _rowwise_bwd(_swiglu_f, [(au, D_FF, 0), (au, D_FF, 1)], [], [dact], x_grad=[True, True], p_grad=[],
                        dx_groups=[[0, 1]], dx_dtypes=[BF16], tm=256, name=f"{tag}_dau")
    dwgu = _mm(h, dau, ta=True, tm=D_MODEL, tn=D_FF // 2, tk=512, name=f"{tag}_dwgu")
    dh = _mm(dau, wgu, tb=True, tm=512, tn=D_MODEL, tk=D_FF // 2, name=f"{tag}_dh")
    dx, dnorm = _rowwise_bwd(_rms_f, [x], [norm], [dh], x_grad=[True], p_grad=[True], dx_groups=[[0]],
                             dx_dtypes=[F32], tm=256, name=f"{tag}_drms", extra={0: dout})
    return dx, dnorm, dwgu, dwd


def _to_heads(z):
    return z.reshape(z.shape[0], HB_HEADS, HB_DIM).transpose(1, 0, 2)


def _from_heads(z):
    return z.transpose(1, 0, 2).reshape(z.shape[1], W_B)


def _shift_down(z):
    return jnp.concatenate([jnp.zeros((1, z.shape[1]), z.dtype), z[:-1]], axis=0)


def _shift_up(z):
    return jnp.concatenate([z[1:], jnp.zeros((1, z.shape[1]), z.dtype)], axis=0)


def _local_step(x, target, w):
    ones_bd = jnp.kron(jnp.eye(HB_HEADS, dtype=F32), jnp.ones((HB_DIM, HB_DIM), F32))
    g = {}
    x1, ffn1_saved = _ffn_fwd(x, w["ffn1_norm"], w["ffn1_wgu"], w["ffn1_wd"], "ffn1")
    hm, = _rowwise(_rms_f, [x1], [w["mix_norm"]], [[0]], [BF16], tm=512, name="mix_rms")
    p_h = _mm(hm, w["w_in_h"], tm=512, tn=512, tk=D_MODEL, name="inproj_h")
    p_r = _mm(hm, w["w_in_r"], tm=512, tn=N_RWKV_PAD // 2, tk=D_MODEL, name="inproj_r")
    o_a, hgrn_states = _hgrn_fwd(p_h, w["lb0"], w["lb1"], w["hgrn_out_norm"])

    p_r_prev = _shift_down(p_r)
    mu = w["mu_pad"]
    prep_xs = [(p_r, W_B, 0), (p_r, W_B, 1), (p_r, W_B, 2), (p_r, LORA_PAD, 6),
               (p_r_prev, W_B, 0), (p_r_prev, W_B, 1), (p_r_prev, W_B, 2), (p_r_prev, LORA_PAD, 6)]
    prep_ps = [(mu, W_B, 0), (mu, W_B, 1), (mu, W_B, 2), (mu, LORA_PAD, 6), w["rwkv_w0"], w["w2_pad"], w["rwkv_a0"],
               w["a2_pad"], w["g2_pad"], w["rwkv_k_k"], w["rwkv_k_a"], ones_bd]
    prep_f = _rwkv_prep_f
    r, lw, k2, v, a_vec, b_vec, gate = _rowwise(prep_f, prep_xs, prep_ps, [[0], [1], [2], [3], [4], [5], [6]],
                                                [F32] * 7, tm=256, name="rwkv_prep")
    seqs = [_to_heads(z) for z in (r, lw, k2, v, a_vec, b_vec)]
    y_h, rwkv_states = _rwkv_fwd(seqs)
    y = _from_heads(y_h)
    post_f = _rwkv_post_f
    post_xs = [y, r, k2, v, gate]
    post_ps = [w["rwkv_r_k"], w["rwkv_gn_w"], w["rwkv_gn_b"], ones_bd]
    o_b, = _rowwise(post_f, post_xs, post_ps, [[0]], [F32], tm=256, name="rwkv_post")
    o = jnp.concatenate([o_a, o_b], axis=1)
    x2 = _mm(o, w["w_out"], tm=512, tn=D_MODEL, tk=D_MODEL, name="outproj", res=x1)
    x3, ffn2_saved = _ffn_fwd(x2, w["ffn2_norm"], w["ffn2_wgu"], w["ffn2_wd"], "ffn2")
    dx3, g["final_norm"], loss = _final_loss(x3, w["final_norm"], target, tm=256)

    dx2, g["ffn2_norm"], g["ffn2_wgu"], g["ffn2_wd"] = _ffn_bwd(dx3, x2, w["ffn2_norm"], w["ffn2_wgu"], w["ffn2_wd"],
                                                                ffn2_saved, "ffn2")
    do = _mm(dx2, w["w_out"], tb=True, tm=512, tn=D_MODEL, tk=D_MODEL, name="outproj_do")
    g["w_out"] = _mm(o, dx2, ta=True, tm=D_MODEL, tn=D_MODEL, tk=512, name="outproj_dw")

    dp_h, g["lb0"], g["lb1"], g["hgrn_out_norm"] = _hgrn_bwd(p_h, w["lb0"], w["lb1"], w["hgrn_out_norm"],
                                                             hgrn_states, do, 0)
    post_out = _rowwise_bwd(post_f, post_xs, post_ps, [(do, W_B, 1)], x_grad=[True] * 5, p_grad=[True] * 3 + [False],
                            dx_groups=[[0], [1], [2], [3], [4]], dx_dtypes=[F32] * 5, tm=256, name="rwkv_post_bwd")
    dy, dr1, dk1, dv1, dgate, g["rwkv_r_k"], g["rwkv_gn_w"], g["rwkv_gn_b"] = post_out
    dseq = _rwkv_bwd(seqs, rwkv_states, _to_heads(dy))
    dr2, dlw, dk2, dv2, da_vec, db_vec = [_from_heads(z) for z in dseq]

    def prep2_f(*vals):
        r_, lw_, k2_, v_, a_, b_, g_ = prep_f(*vals)
        return r_, lw_, k2_, v_, a_, b_, g_, r_, k2_, v_

    prep_out = _rowwise_bwd(prep2_f, prep_xs, prep_ps, [dr2, dlw, dk2, dv2, da_vec, db_vec, dgate, dr1, dk1, dv1],
                            x_grad=[True] * 8, p_grad=[True] * 11 + [False], dx_groups=[[0, 1, 2, 3], [4, 5, 6, 7]],
                            dx_dtypes=[F32, F32], tm=256, name="rwkv_prep_bwd")
    dpr_main, dpr_prev = prep_out[0], prep_out[1]
    (dmu_r, dmu_k, dmu_v, dmu_lo, g["rwkv_w0"], g["w2_pad"], g["rwkv_a0"], g["a2_pad"], g["g2_pad"],
     g["rwkv_k_k"], g["rwkv_k_a"]) = prep_out[2:]
    g["mu_pad"] = jnp.concatenate([dmu_r, dmu_k, dmu_v, dmu_lo], axis=1)
    dp_r, = _rowwise(lambda u_, s_: (u_ + s_,), [dpr_main, _shift_up(dpr_prev)], [], [[0]], [F32], tm=512,
                     name="rwkv_dp_sum")
    dhm = _mm(dp_h, w["w_in_h"], tb=True, tm=512, tn=D_MODEL, tk=D_MODEL, name="inproj_dh_h")
    dhm = _mm(dp_r, w["w_in_r"], tb=True, tm=512, tn=D_MODEL, tk=N_RWKV_PAD // 2, name="inproj_dh_r", res=dhm)
    g["w_in_h"] = _mm(hm, dp_h, ta=True, tm=D_MODEL, tn=D_MODEL, tk=512, name="inproj_dw_h")
    g["w_in_r"] = _mm(hm, dp_r, ta=True, tm=D_MODEL, tn=N_RWKV_PAD // 2, tk=512, name="inproj_dw_r")
    dx1, g["mix_norm"] = _rowwise_bwd(_rms_f, [x1], [w["mix_norm"]], [dhm], x_grad=[True], p_grad=[True],
                                      dx_groups=[[0]], dx_dtypes=[F32], tm=256, name="mix_drms", extra={0: dx2})
    dx0, g["ffn1_norm"], g["ffn1_wgu"], g["ffn1_wd"] = _ffn_bwd(dx1, x, w["ffn1_norm"], w["ffn1_wgu"], w["ffn1_wd"],
                                                                ffn1_saved, "ffn1")
    return loss, dx0, g


HBM_SPEC = pl.BlockSpec(memory_space=pl.ANY)


def _chips(x, y):
    return [(1 - x, y), (x, 1 - y), (1 - x, 1 - y)]


def _gather_weights(wq):
    def body(w_ref, out_ref, send_sems, recv_sems, local_sem):
        x, y, c = lax.axis_index("x"), lax.axis_index("y"), lax.axis_index("c")
        me = 2 * x + y
        mine = pltpu.make_async_copy(w_ref, out_ref.at[me], local_sem)
        mine.start()

        def copy(j, slot, px, py):
            return pltpu.make_async_remote_copy(src_ref=w_ref, dst_ref=out_ref.at[slot], send_sem=send_sems.at[j],
                                                recv_sem=recv_sems.at[j], device_id=(px, py, c), device_id_type=MESH)

        sends = [copy(j, me, px, py) for j, (px, py) in enumerate(_chips(x, y))]
        for cp in sends:
            cp.start()
        for j, (px, py) in enumerate(_chips(x, y)):
            copy(j, 2 * px + py, px, py).wait_recv()
        for cp in sends:
            cp.wait_send()
        mine.wait()

    return pl.pallas_call(
        body, name="gather_weights", in_specs=[HBM_SPEC], out_specs=HBM_SPEC,
        out_shape=SDS((N_CHIPS,) + wq.shape, wq.dtype),
        scratch_shapes=[pltpu.SemaphoreType.DMA((3,)), pltpu.SemaphoreType.DMA((3,)), pltpu.SemaphoreType.DMA(())],
    )(wq)


def _sibling_exchange(g4):
    def body(g_ref, out_ref, send_sems, recv_sems):
        x, y, c = lax.axis_index("x"), lax.axis_index("y"), lax.axis_index("c")
        cps = [pltpu.make_async_remote_copy(src_ref=g_ref.at[q, 1 - c], dst_ref=out_ref.at[q],
                                            send_sem=send_sems.at[q], recv_sem=recv_sems.at[q],
                                            device_id=(x, y, 1 - c), device_id_type=MESH) for q in range(N_CHIPS)]
        for cp in cps:
            cp.start()
        for cp in cps:
            cp.wait()

    return pl.pallas_call(
        body, name="grad_sibling_exchange", in_specs=[HBM_SPEC], out_specs=HBM_SPEC,
        out_shape=SDS((N_CHIPS,) + g4.shape[2:], g4.dtype),
        scratch_shapes=[pltpu.SemaphoreType.DMA((N_CHIPS,)), pltpu.SemaphoreType.DMA((N_CHIPS,))],
    )(g4)


def _chip_exchange(s4):
    def body(s_ref, out_ref, send_sems, recv_sems, local_sem):
        x, y, c = lax.axis_index("x"), lax.axis_index("y"), lax.axis_index("c")
        me = 2 * x + y
        mine = pltpu.make_async_copy(s_ref.at[me], out_ref.at[me], local_sem)
        mine.start()

        def copy(j, px, py, src_slot, dst_slot):
            return pltpu.make_async_remote_copy(src_ref=s_ref.at[src_slot], dst_ref=out_ref.at[dst_slot],
                                                send_sem=send_sems.at[j], recv_sem=recv_sems.at[j],
                                                device_id=(px, py, c), device_id_type=MESH)

        sends = [copy(j, px, py, 2 * px + py, me) for j, (px, py) in enumerate(_chips(x, y))]
        for cp in sends:
            cp.start()
        for j, (px, py) in enumerate(_chips(x, y)):
            copy(j, px, py, me, 2 * px + py).wait_recv()
        for cp in sends:
            cp.wait_send()
        mine.wait()

    return pl.pallas_call(
        body, name="grad_chip_exchange", in_specs=[HBM_SPEC], out_specs=HBM_SPEC, out_shape=SDS(s4.shape, s4.dtype),
        scratch_shapes=[pltpu.SemaphoreType.DMA((3,)), pltpu.SemaphoreType.DMA((3,)), pltpu.SemaphoreType.DMA(())],
    )(s4)


def _sibling_allgather(f):
    def body(f_ref, out_ref, send_sem, recv_sem, local_sem):
        x, y, c = lax.axis_index("x"), lax.axis_index("y"), lax.axis_index("c")
        mine = pltpu.make_async_copy(f_ref, out_ref.at[c], local_sem)
        mine.start()
        cp = pltpu.make_async_remote_copy(src_ref=f_ref, dst_ref=out_ref.at[c], send_sem=send_sem, recv_sem=recv_sem,
                                          device_id=(x, y, 1 - c), device_id_type=MESH)
        cp.start()
        pltpu.make_async_remote_copy(src_ref=f_ref, dst_ref=out_ref.at[1 - c], send_sem=send_sem, recv_sem=recv_sem,
                                     device_id=(x, y, 1 - c), device_id_type=MESH).wait_recv()
        cp.wait_send()
        mine.wait()

    return pl.pallas_call(
        body, name="grad_sibling_allgather", in_specs=[HBM_SPEC], out_specs=HBM_SPEC,
        out_shape=SDS((2,) + f.shape, f.dtype),
        scratch_shapes=[pltpu.SemaphoreType.DMA(()), pltpu.SemaphoreType.DMA(()), pltpu.SemaphoreType.DMA(())],
    )(f)


def _add_halves(g4, r4, c_idx, *, tr):
    _, _, rows, lanes = g4.shape

    def body(c_ref, a_ref, b_ref, o_ref):
        o_ref[...] = a_ref[...] + b_ref[...]

    grid_spec = pltpu.PrefetchScalarGridSpec(
        num_scalar_prefetch=1, grid=(N_CHIPS, rows // tr),
        in_specs=[pl.BlockSpec((None, None, tr, lanes), lambda q, i, c_ref: (q, c_ref[0], i, 0)),
                  pl.BlockSpec((None, tr, lanes), lambda q, i, c_ref: (q, i, 0))],
        out_specs=pl.BlockSpec((None, tr, lanes), lambda q, i, c_ref: (q, i, 0)))
    return pl.pallas_call(body, name="grad_add_halves", grid_spec=grid_spec, out_shape=SDS(r4.shape, F32),
                          compiler_params=_params(("parallel", "parallel")))(c_idx, g4, r4)


def _sum_chips(r4, *, tr):
    _, rows, lanes = r4.shape

    def body(a_ref, b_ref, c_ref, d_ref, o_ref):
        o_ref[...] = ((a_ref[...] + b_ref[...]) + c_ref[...]) + d_ref[...]

    specs = [pl.BlockSpec((None, tr, lanes), lambda i, q=q: (q, i, 0)) for q in range(N_CHIPS)]
    return pl.pallas_call(body, name="grad_sum_chips", grid=(rows // tr,), in_specs=specs,
                          out_specs=pl.BlockSpec((tr, lanes), lambda i: (i, 0)), out_shape=SDS((rows, lanes), F32),
                          compiler_params=_params(("parallel",)))(r4, r4, r4, r4)


def _adamw(wf, gf, mf, vf, *, tr):
    rows, lanes = wf.shape
    c1 = 1.0 / (1.0 - ADAM_B1 ** ADAM_STEP)
    c2 = 1.0 / (1.0 - ADAM_B2 ** ADAM_STEP)

    def body(w_ref, g_ref, m_ref, v_ref, d_ref, nm_ref, nv_ref):
        gv = g_ref[...]
        m = ADAM_B1 * m_ref[...] + (1.0 - ADAM_B1) * gv
        v = ADAM_B2 * v_ref[...] + (1.0 - ADAM_B2) * (gv * gv)
        d_ref[...] = -ADAM_LR * ((m * c1) / (jnp.sqrt(v * c2) + ADAM_EPS) + ADAM_WD * w_ref[...])
        nm_ref[...] = m
        nv_ref[...] = v

    spec = pl.BlockSpec((tr, lanes), lambda i: (i, 0))
    return pl.pallas_call(body, name="adamw", grid=(rows // tr,), in_specs=[spec] * 4, out_specs=[spec] * 3,
                          out_shape=[SDS((rows, lanes), F32)] * 3, compiler_params=_params(("parallel",)))(wf, gf, mf, vf)


N_SMALL_ELEMS = 4 * D_MODEL + 2 * W_A + W_A + N_RWKV_COLS + 7 * W_B
SEG_ELEMS = N_SHARDED_ELEMS + N_SMALL_ELEMS
HALF_ROWS = -(-SEG_ELEMS // (2 * 8 * LANES)) * 8
SEG_ROWS = 2 * HALF_ROWS
SEG_PAD = SEG_ROWS * LANES


def _flat_segment(sharded_parts, small_parts):
    parts = [p.reshape(-1) for p in sharded_parts] + [p.reshape(-1) for p in small_parts]
    flat = jnp.concatenate(parts)
    return jnp.pad(flat, (0, SEG_PAD - flat.shape[0]))


def _split_segment(flat):
    out, off = {}, 0
    for n in SHARDED:
        shp = _shard_shape(n)
        out[n] = flat[off:off + _numel(shp)].reshape((1,) + shp)
        off += _numel(shp)
    small_shapes = {"ffn1_norm": (1, D_MODEL), "mix_norm": (1, D_MODEL), "hgrn_lb_logits": (2, W_A),
                    "hgrn_out_norm": (1, W_A), "rwkv_shift_mu": (1, N_RWKV_COLS), "rwkv_w0": (1, W_B),
                    "rwkv_a0": (1, W_B), "rwkv_k_k": (1, W_B), "rwkv_k_a": (1, W_B),
                    "rwkv_r_k": (1, HB_HEADS, HB_DIM), "rwkv_gn_w": (1, W_B), "rwkv_gn_b": (1, W_B),
                    "ffn2_norm": (1, D_MODEL), "final_norm": (D_MODEL,)}
    for n in SMALL:
        shp = small_shapes[n]
        out[n] = flat[off:off + _numel(shp)].reshape(shp)
        off += _numel(shp)
    return out


def _quarter(full, name, q):
    shape, ax = SHARDED_SHAPES[name]
    w = shape[ax] // N_CHIPS
    return lax.slice_in_dim(full, q * w, (q + 1) * w, axis=ax)


def kernel(x, ffn1_norm, ffn1_w_gate, ffn1_w_up, ffn1_w_down, mix_norm, w_in, hgrn_lb_logits, hgrn_out_norm, rwkv_shift_mu, rwkv_w0, rwkv_w2, rwkv_a0, rwkv_a2, rwkv_g2, rwkv_k_k, rwkv_k_a, rwkv_r_k, rwkv_gn_w, rwkv_gn_b, w_out, ffn2_norm, ffn2_w_gate, ffn2_w_up, ffn2_w_down, final_norm, loss_target, m_ffn1_norm, m_ffn1_w_gate, m_ffn1_w_up, m_ffn1_w_down, m_mix_norm, m_w_in, m_hgrn_lb_logits, m_hgrn_out_norm, m_rwkv_shift_mu, m_rwkv_w0, m_rwkv_w2, m_rwkv_a0, m_rwkv_a2, m_rwkv_g2, m_rwkv_k_k, m_rwkv_k_a, m_rwkv_r_k, m_rwkv_gn_w, m_rwkv_gn_b, m_w_out, m_ffn2_norm, m_ffn2_w_gate, m_ffn2_w_up, m_ffn2_w_down, m_final_norm, v_ffn1_norm, v_ffn1_w_gate, v_ffn1_w_up, v_ffn1_w_down, v_mix_norm, v_w_in, v_hgrn_lb_logits, v_hgrn_out_norm, v_rwkv_shift_mu, v_rwkv_w0, v_rwkv_w2, v_rwkv_a0, v_rwkv_a2, v_rwkv_g2, v_rwkv_k_k, v_rwkv_k_a, v_rwkv_r_k, v_rwkv_gn_w, v_rwkv_gn_b, v_w_out, v_ffn2_norm, v_ffn2_w_gate, v_ffn2_w_up, v_ffn2_w_down, v_final_norm):
    args = dict(locals())
    wts = {n: args[n] for n in ALL_WEIGHTS}
    moms = {n: args["m_" + n] for n in ALL_WEIGHTS}
    vars_ = {n: args["v_" + n] for n in ALL_WEIGHTS}

    wq = jnp.concatenate([wts[n].reshape(-1) for n in SHARDED]).astype(BF16)
    wq = jnp.pad(wq, (0, W_ROWS * LANES - wq.shape[0])).reshape(W_ROWS, LANES)
    gathered = _gather_weights(wq).reshape(N_CHIPS, W_ROWS * LANES)
    full, off = {}, 0
    for n in SHARDED:
        shp = _shard_shape(n)
        ax = SHARDED_SHAPES[n][1]
        full[n] = jnp.concatenate([gathered[q, off:off + _numel(shp)].reshape(shp) for q in range(N_CHIPS)], axis=ax)
        off += _numel(shp)

    w = {}
    for tag in ("ffn1", "ffn2"):
        w[f"{tag}_wgu"] = jnp.concatenate([full[f"{tag}_w_gate"], full[f"{tag}_w_up"]], axis=1)
        w[f"{tag}_wd"] = full[f"{tag}_w_down"]
        w[f"{tag}_norm"] = wts[f"{tag}_norm"]
    w["w_in_h"] = full["w_in"][:, :N_HGRN_COLS]
    w["w_in_r"] = jnp.pad(full["w_in"][:, N_HGRN_COLS:], ((0, 0), (0, N_RWKV_PAD - N_RWKV_COLS)))
    w["w_out"] = full["w_out"]
    zrow = lambda nrow: jnp.zeros((nrow, W_B), BF16)
    w["w2_pad"] = jnp.concatenate([full["rwkv_w2"], zrow(LORA_PAD - 32)], axis=0)
    w["a2_pad"] = jnp.concatenate([zrow(32), full["rwkv_a2"], zrow(LORA_PAD - 64)], axis=0)
    w["g2_pad"] = jnp.concatenate([zrow(64), full["rwkv_g2"], zrow(LORA_PAD - 160)], axis=0)
    w["mix_norm"] = mix_norm
    w["lb0"], w["lb1"] = hgrn_lb_logits[0:1], hgrn_lb_logits[1:2]
    w["hgrn_out_norm"] = hgrn_out_norm
    w["mu_pad"] = jnp.pad(rwkv_shift_mu, ((0, 0), (0, N_RWKV_PAD - N_RWKV_COLS)))
    for n in ("rwkv_w0", "rwkv_a0", "rwkv_k_k", "rwkv_k_a", "rwkv_gn_w", "rwkv_gn_b"):
        w[n] = wts[n]
    w["rwkv_r_k"] = rwkv_r_k.reshape(1, W_B)
    w["final_norm"] = final_norm.reshape(1, D_MODEL)

    loss_slab, grad_x, g = _local_step(x[0], loss_target[0], w)
    loss = lax.psum(loss_slab[0, 0], ("x", "y", "c"))

    gfull = {
        "ffn1_w_gate": g["ffn1_wgu"][:, :D_FF], "ffn1_w_up": g["ffn1_wgu"][:, D_FF:], "ffn1_w_down": g["ffn1_wd"],
        "ffn2_w_gate": g["ffn2_wgu"][:, :D_FF], "ffn2_w_up": g["ffn2_wgu"][:, D_FF:], "ffn2_w_down": g["ffn2_wd"],
        "w_in": jnp.concatenate([g["w_in_h"], g["w_in_r"][:, :N_RWKV_COLS]], axis=1),
        "rwkv_w2": g["w2_pad"][0:32], "rwkv_a2": g["a2_pad"][32:64], "rwkv_g2": g["g2_pad"][64:160],
        "w_out": g["w_out"],
    }
    gsmall = {
        "ffn1_norm": g["ffn1_norm"], "mix_norm": g["mix_norm"],
        "hgrn_lb_logits": jnp.concatenate([g["lb0"], g["lb1"]], axis=0), "hgrn_out_norm": g["hgrn_out_norm"],
        "rwkv_shift_mu": g["mu_pad"][:, :N_RWKV_COLS], "rwkv_w0": g["rwkv_w0"], "rwkv_a0": g["rwkv_a0"],
        "rwkv_k_k": g["rwkv_k_k"], "rwkv_k_a": g["rwkv_k_a"], "rwkv_r_k": g["rwkv_r_k"],
        "rwkv_gn_w": g["rwkv_gn_w"], "rwkv_gn_b": g["rwkv_gn_b"], "ffn2_norm": g["ffn2_norm"],
        "final_norm": g["final_norm"],
    }
    small_list = [gsmall[n] for n in SMALL]
    g4 = jnp.stack([_flat_segment([_quarter(gfull[n], n, q) for n in SHARDED], small_list) for q in range(N_CHIPS)])
    g4 = g4.reshape(N_CHIPS, 2, HALF_ROWS, LANES)
    c_idx = lax.axis_index("c").astype(jnp.int32).reshape(1)
    tr = HALF_ROWS // 11
    r1 = _sibling_exchange(g4)
    s4 = _add_halves(g4, r1, c_idx, tr=tr)
    r2 = _chip_exchange(s4)
    fhalf = _sum_chips(r2, tr=tr)
    gq = _sibling_allgather(fhalf).reshape(SEG_ROWS, LANES)

    flat_of = lambda d: _flat_segment([d[n] for n in SHARDED], [d[n] for n in SMALL]).reshape(SEG_ROWS, LANES)
    delta, new_m, new_v = _adamw(flat_of(wts), gq, flat_of(moms), flat_of(vars_), tr=SEG_ROWS // 22)

    grads, deltas, nms, nvs = (_split_segment(z.reshape(-1)) for z in (gq, delta, new_m, new_v))
    return (loss, grad_x[None], *[grads[n] for n in ALL_WEIGHTS], *[deltas[n] for n in ALL_WEIGHTS],
            *[nms[n] for n in ALL_WEIGHTS], *[nvs[n] for n in ALL_WEIGHTS])
```

```python
import functools

import jax
import jax.numpy as jnp
from jax import lax
from jax.experimental import pallas as pl
from jax.experimental.pallas import tpu as pltpu

F32 = jnp.float32
BF16 = jnp.bfloat16
SDS = jax.ShapeDtypeStruct
MESH = pl.DeviceIdType.MESH

D_MODEL = 1024
D_FF = 2816
W_A = 512
W_B = 512
HA_HEADS, HA_DIM = 4, 128
HB_HEADS, HB_DIM = 8, 64
HGRN_CHUNK = 64
RWKV_CHUNK = 16
RWKV_GROUP = 4
N_HGRN_COLS = 4 * W_A
N_RWKV_COLS = 3 * W_B + 32 + 32 + 96
N_RWKV_PAD = 1792
LORA_PAD = 256
NORM_EPS = 1e-6
RWKV_GN_EPS = 64e-5
L2_EPS = 1e-12
ADAM_LR, ADAM_B1, ADAM_B2, ADAM_EPS, ADAM_WD, ADAM_STEP = 0.001, 0.9, 0.999, 1e-8, 0.01, 10

N_CHIPS = 4
VMEM_LIMIT_V7X = 56 * 1024 * 1024
LANES = 1024

SHARDED_SHAPES = {
    "ffn1_w_gate": ((D_MODEL, D_FF), 1), "ffn1_w_up": ((D_MODEL, D_FF), 1), "ffn1_w_down": ((D_FF, D_MODEL), 0),
    "w_in": ((D_MODEL, N_HGRN_COLS + N_RWKV_COLS), 1), "rwkv_w2": ((32, W_B), 1), "rwkv_a2": ((32, W_B), 1),
    "rwkv_g2": ((96, W_B), 1), "w_out": ((D_MODEL, D_MODEL), 0),
    "ffn2_w_gate": ((D_MODEL, D_FF), 1), "ffn2_w_up": ((D_MODEL, D_FF), 1), "ffn2_w_down": ((D_FF, D_MODEL), 0),
}
SMALL = ("ffn1_norm", "mix_norm", "hgrn_lb_logits", "hgrn_out_norm", "rwkv_shift_mu", "rwkv_w0", "rwkv_a0",
         "rwkv_k_k", "rwkv_k_a", "rwkv_r_k", "rwkv_gn_w", "rwkv_gn_b", "ffn2_norm", "final_norm")
ALL_WEIGHTS = ("ffn1_norm", "ffn1_w_gate", "ffn1_w_up", "ffn1_w_down", "mix_norm", "w_in", "hgrn_lb_logits",
               "hgrn_out_norm", "rwkv_shift_mu", "rwkv_w0", "rwkv_w2", "rwkv_a0", "rwkv_a2", "rwkv_g2", "rwkv_k_k",
               "rwkv_k_a", "rwkv_r_k", "rwkv_gn_w", "rwkv_gn_b", "w_out", "ffn2_norm", "ffn2_w_gate", "ffn2_w_up",
               "ffn2_w_down", "final_norm")


def _shard_shape(name):
    shape, ax = SHARDED_SHAPES[name]
    return tuple(s // N_CHIPS if i == ax else s for i, s in enumerate(shape))


def _numel(shape):
    n = 1
    for s in shape:
        n *= s
    return n


def _params(sem=None):
    return pltpu.CompilerParams(dimension_semantics=sem, vmem_limit_bytes=VMEM_LIMIT_V7X)


def _dg(x, y, cx, cy, hi):
    dn = (((cx,), (cy,)), ((), ()))
    if hi:
        return lax.dot_general(x.astype(F32), y.astype(F32), dn, precision=lax.Precision.HIGHEST,
                               preferred_element_type=F32)
    return lax.dot_general(x.astype(BF16), y.astype(BF16), dn, preferred_element_type=F32)


def _make_mm(hi):
    @jax.custom_vjp
    def nn(x, y):
        return _dg(x, y, 1, 0, hi)

    @jax.custom_vjp
    def nt(x, y):
        return _dg(x, y, 1, 1, hi)

    @jax.custom_vjp
    def tn(x, y):
        return _dg(x, y, 0, 0, hi)

    nn.defvjp(lambda x, y: (nn(x, y), (x, y)), lambda r, g: (nt(g, r[1]), tn(r[0], g)))
    nt.defvjp(lambda x, y: (nt(x, y), (x, y)), lambda r, g: (nn(g, r[1]), tn(g, r[0])))
    tn.defvjp(lambda x, y: (tn(x, y), (x, y)), lambda r, g: (nt(r[1], g), nn(r[0], g)))
    return nn, nt, tn


_nn, _nt, _tn = _make_mm(False)
_nn_hi, _nt_hi, _tn_hi = _make_mm(True)


def _sigmoid(x):
    return 1.0 / (1.0 + jnp.exp(-x))


def _silu(x):
    return x * _sigmoid(x)


def _softplus(z):
    return jnp.maximum(z, 0.0) + jnp.log(1.0 + jnp.exp(-jnp.abs(z)))


def _mm(a, b, *, ta=False, tb=False, tm, tn, tk, name, out_dtype=F32, res=None, scale=None):
    m = a.shape[1] if ta else a.shape[0]
    kdim = a.shape[0] if ta else a.shape[1]
    n = b.shape[0] if tb else b.shape[1]
    assert (b.shape[1] if tb else b.shape[0]) == kdim
    tm, tn, tk = min(tm, m), min(tn, n), min(tk, kdim)
    assert m % tm == 0 and n % tn == 0 and kdim % tk == 0, (name, m, n, kdim)
    nk = kdim // tk
    a_spec = pl.BlockSpec((tk, tm), lambda i, j, k: (k, i)) if ta else pl.BlockSpec((tm, tk), lambda i, j, k: (i, k))
    b_spec = pl.BlockSpec((tn, tk), lambda i, j, k: (j, k)) if tb else pl.BlockSpec((tk, tn), lambda i, j, k: (k, j))
    o_spec = pl.BlockSpec((tm, tn), lambda i, j, k: (i, j))
    ca, cb = (0 if ta else 1), (1 if tb else 0)

    def body(*refs):
        if res is not None:
            a_ref, b_ref, r_ref, o_ref, acc_ref = refs
        else:
            a_ref, b_ref, o_ref, acc_ref = refs
        k = pl.program_id(2)

        @pl.when(k == 0)
        def _():
            acc_ref[...] = jnp.zeros_like(acc_ref)

        acc_ref[...] += _dg(a_ref[...], b_ref[...], ca, cb, False)

        @pl.when(k == nk - 1)
        def _():
            acc = acc_ref[...]
            if scale is not None:
                acc = acc * scale
            if res is not None:
                acc = r_ref[...] + acc
            o_ref[...] = acc.astype(out_dtype)

    in_specs = [a_spec, b_spec] + ([o_spec] if res is not None else [])
    args = (a, b) + ((res,) if res is not None else ())
    return pl.pallas_call(
        body, name=name, grid=(m // tm, n // tn, nk), in_specs=in_specs, out_specs=o_spec,
        out_shape=SDS((m, n), out_dtype), scratch_shapes=[pltpu.VMEM((tm, tn), F32)],
        compiler_params=_params(("parallel", "parallel", "arbitrary")))(*args)


def _row_spec(x, tm):
    if isinstance(x, tuple):
        arr, w, j = x
        return arr, pl.BlockSpec((tm, w), lambda i, j=j: (i, j))
    return x, pl.BlockSpec((tm, x.shape[1]), lambda i: (i, 0))


def _par_spec(p):
    if isinstance(p, tuple):
        arr, w, j = p
        return arr, pl.BlockSpec((arr.shape[0], w), lambda i, j=j: (0, j))
    return p, pl.BlockSpec(p.shape, lambda i: (0, 0))


def _store_groups(refs, groups, vals):
    for ref, idxs in zip(refs, groups):
        off = 0
        for ix in idxs:
            v = vals[ix]
            ref[:, off:off + v.shape[1]] = v.astype(ref.dtype)
            off += v.shape[1]


SUBLANES = 8


def _x_plan(xs, tm, t):
    arrays, specs, plan = [], [], []
    nb = tm // SUBLANES
    for x in xs:
        if isinstance(x, tuple) and isinstance(x[0], str):
            kind, arr, w, j = x
            if kind == "prev":
                halo = lambda i, j=j: (jnp.maximum(i * nb - 1, 0), j)
            else:
                halo = lambda i, j=j: (jnp.minimum((i + 1) * nb, t // SUBLANES - 1), j)
            arrays += [arr, arr]
            specs += [pl.BlockSpec((tm, w), lambda i, j=j: (i, j)), pl.BlockSpec((SUBLANES, w), halo)]
            plan.append((kind, 2, w))
        else:
            arr, spec = _row_spec(x, tm)
            arrays.append(arr)
            specs.append(spec)
            plan.append(("plain", 1, spec.block_shape[1]))
    return arrays, specs, plan


def _x_vals(refs, plan, tm, nt):
    vals, k = [], 0
    i = pl.program_id(0)
    rows = lax.broadcasted_iota(jnp.int32, (tm, 1), 0)
    for kind, n, _ in plan:
        main = refs[k][...].astype(F32)
        if kind == "prev":
            edge = jnp.where(i == 0, 0.0, refs[k + 1][SUBLANES - 1:SUBLANES, :].astype(F32))
            main = jnp.where(rows == 0, edge, pltpu.roll(main, 1, 0))
        elif kind == "next":
            edge = jnp.where(i == nt - 1, 0.0, refs[k + 1][0:1, :].astype(F32))
            main = jnp.where(rows == tm - 1, edge, pltpu.roll(main, tm - 1, 0))
        vals.append(main)
        k += n
    return vals


def _tile_rows(xs, tm):
    arr = xs[0]
    if isinstance(arr, tuple):
        arr = arr[1] if isinstance(arr[0], str) else arr[0]
    return min(tm, arr.shape[0]), arr.shape[0]


def _rowwise(f, xs, params, out_groups, out_dtypes, *, tm, name):
    tm, t = _tile_rows(xs, tm)
    nt = t // tm
    xa, xspecs, plan = _x_plan(xs, tm, t)
    pa, pspecs = (zip(*[_par_spec(p) for p in params]) if params else ((), ()))
    nxr, npar = len(xa), len(pa)
    x_sds = [SDS((tm, w), F32) for _, _, w in plan]
    p_sds = [SDS(s.block_shape, F32) for s in pspecs]
    outs_sds = jax.eval_shape(lambda *vals: f(*vals), *x_sds, *p_sds)
    widths = [sum(outs_sds[ix].shape[1] for ix in idxs) for idxs in out_groups]

    def body(*refs):
        vals = _x_vals(refs[:nxr], plan, tm, nt) + [r[...].astype(F32) for r in refs[nxr:nxr + npar]]
        outs = f(*vals)
        _store_groups(refs[nxr + npar:], out_groups, outs)

    return pl.pallas_call(
        body, name=name, grid=(nt,), in_specs=list(xspecs) + list(pspecs),
        out_specs=[pl.BlockSpec((tm, w), lambda i: (i, 0)) for w in widths],
        out_shape=[SDS((t, w), dt) for w, dt in zip(widths, out_dtypes)],
        compiler_params=_params(("parallel",)))(*xa, *pa)


def _rowwise_bwd(f, xs, params, cots, *, x_grad, p_grad, dx_groups, dx_dtypes, tm, name, extra=None):
    tm, t = _tile_rows(xs, tm)
    nt = t // tm
    xa, xspecs, plan = _x_plan(xs, tm, t)
    pa, pspecs = (zip(*[_par_spec(p) for p in params]) if params else ((), ()))
    ca, cspecs = zip(*[_row_spec(c, tm) for c in cots])
    extra = extra or {}
    ekeys = sorted(extra)
    ea, especs = (zip(*[_row_spec(extra[k], tm) for k in ekeys]) if ekeys else ((), ()))
    nx, nxr, npar, nc, ne = len(plan), len(xa), len(pa), len(ca), len(ea)
    gx = [i for i in range(nx) if x_grad[i]]
    gp = [i for i in range(npar) if p_grad[i]]
    widths = [sum(plan[gx[ix]][2] for ix in idxs) for idxs in dx_groups]
    ng = len(dx_groups)

    def body(*refs):
        ins = refs[:nxr + npar + nc + ne]
        outs = refs[nxr + npar + nc + ne:]
        vals = _x_vals(ins[:nxr], plan, tm, nt) + [r[...].astype(F32) for r in ins[nxr:nxr + npar]]
        cvals = tuple(r[...].astype(F32) for r in ins[nxr + npar:nxr + npar + nc])
        evals = [r[...].astype(F32) for r in ins[nxr + npar + nc:]]
        diff_idx = gx + [nx + i for i in gp]

        def g(*dargs):
            full = list(vals)
            for ix, v in zip(diff_idx, dargs):
                full[ix] = v
            return tuple(f(*full))

        _, vjp = jax.vjp(g, *[vals[ix] for ix in diff_idx])
        grads = vjp(cvals)
        dxs = list(grads[:len(gx)])
        for k, ev in zip(ekeys, evals):
            dxs[k] = dxs[k] + ev
        _store_groups(outs[:ng], dx_groups, dxs)
        i = pl.program_id(0)
        for ref, gval in zip(outs[ng:], grads[len(gx):]):
            @pl.when(i == 0)
            def _(ref=ref):
                ref[...] = jnp.zeros_like(ref)
            ref[...] += gval

    dp_specs = [pl.BlockSpec(pspecs[i].block_shape, lambda i: (0, 0)) for i in gp]
    dp_shapes = [SDS(pspecs[i].block_shape, F32) for i in gp]
    return pl.pallas_call(
        body, name=name, grid=(nt,), in_specs=list(xspecs) + list(pspecs) + list(cspecs) + list(especs),
        out_specs=[pl.BlockSpec((tm, w), lambda i: (i, 0)) for w in widths] + dp_specs,
        out_shape=[SDS((t, w), dt) for w, dt in zip(widths, dx_dtypes)] + dp_shapes,
        compiler_params=_params(("arbitrary",)))(*xa, *pa, *ca, *ea)


def _rms_f(x, g):
    return (x * lax.rsqrt(jnp.mean(x * x, axis=-1, keepdims=True) + NORM_EPS) * g,)


def _group_sum(x, ones_bd):
    return _nn_hi(x, ones_bd)


def _rwkv_prep_f(r, k, v, lo, rp, kp, vp, lop, mu_r, mu_k, mu_v, mu_lo, w0, w2p, a0, a2p, g2p, k_k, k_a, ones_bd):
    r = r + mu_r * (rp - r)
    k = k + mu_k * (kp - k)
    v = v + mu_v * (vp - v)
    lo = lo + mu_lo * (lop - lo)
    w_log = -_softplus(-(w0 + _nn(jnp.tanh(lo), w2p))) - 0.5
    lw = -jnp.exp(w_log)
    a_g = _sigmoid(a0 + _nn(lo, a2p))
    g = _nn(_sigmoid(lo), g2p)
    kk = k * k_k
    kk = kk / jnp.maximum(jnp.sqrt(_group_sum(kk * kk, ones_bd)), L2_EPS)
    k2 = k * (1.0 + (a_g - 1.0) * k_a)
    return r, lw, k2, v, -kk, kk * a_g, g


def _rwkv_post_f(y, r, k2, v, g, r_k, gn_w, gn_b, ones_bd):
    inv_n = 1.0 / HB_DIM
    mean = _group_sum(y, ones_bd) * inv_n
    yc = y - mean
    var = _group_sum(yc * yc, ones_bd) * inv_n
    yn = yc * lax.rsqrt(var + RWKV_GN_EPS) * gn_w + gn_b
    bonus = _group_sum(r * k2 * r_k, ones_bd) * v
    return ((yn + bonus) * g,)


def _tri(c, strict=False):
    ii = lax.broadcasted_iota(jnp.int32, (c, c), 0)
    jj = lax.broadcasted_iota(jnp.int32, (c, c), 1)
    return (jj < ii) if strict else (jj <= ii)


def _hgrn_head(st0, q_a, f_a, i_a, g_a, l0, l1, onorm):
    c = q_a.shape[0]
    mx = jnp.maximum(l0, l1)
    e0, e1 = jnp.exp(l0 - mx), jnp.exp(l1 - mx)
    lb = e0 / (e0 + e1)
    forget = lb + (1.0 - lb) * _sigmoid(f_a)
    q = _silu(q_a)
    kk = 1.0 - forget
    lf = jnp.log(forget)
    incl = _tri(c)
    bcum = _nn_hi(incl.astype(F32), lf)
    rows = lax.broadcasted_iota(jnp.int32, (c, 1), 0)
    bref = jnp.sum(jnp.where(rows <= c // 2, lf, 0.0), axis=0, keepdims=True)
    blast = jnp.sum(lf, axis=0, keepdims=True)
    scores = jnp.where(incl, _nt(q * jnp.exp(bcum - bref), kk * jnp.exp(bref - bcum)), 0.0)
    o = _nn(scores, i_a) + _nt(q * jnp.exp(bcum), st0)
    st1 = st0 * jnp.exp(blast) + _tn(i_a, kk * jnp.exp(blast - bcum))
    o = o * lax.rsqrt(jnp.mean(o * o, axis=-1, keepdims=True) + NORM_EPS)
    return o * onorm * _silu(g_a), st1


def _hgrn_fwd(p_h, l0, l1, onorm):
    t = p_h.shape[0]
    c, n = HGRN_CHUNK, p_h.shape[0] // HGRN_CHUNK

    def body(q_ref, f_ref, i_ref, g_ref, l0_ref, l1_ref, on_ref, o_ref, hs_ref, st_ref):
        @pl.when(pl.program_id(0) == 0)
        def _():
            st_ref[...] = jnp.zeros_like(st_ref)

        hs_ref[0] = st_ref[...]
        for h in range(HA_HEADS):
            sl = slice(h * HA_DIM, (h + 1) * HA_DIM)
            o, st1 = _hgrn_head(st_ref[h], q_ref[:, sl], f_ref[:, sl], i_ref[:, sl], g_ref[:, sl],
                                l0_ref[:, sl], l1_ref[:, sl], on_ref[:, sl])
            o_ref[:, sl] = o
            st_ref[h] = st1

    col = lambda j: pl.BlockSpec((c, W_A), lambda i, j=j: (i, j))
    par = pl.BlockSpec((1, W_A), lambda i: (0, 0))
    return pl.pallas_call(
        body, name="hgrn_fwd", grid=(n,), in_specs=[col(0), col(1), col(2), col(3), par, par, par],
        out_specs=[pl.BlockSpec((c, W_A), lambda i: (i, 0)),
                   pl.BlockSpec((1, HA_HEADS, HA_DIM, HA_DIM), lambda i: (i, 0, 0, 0))],
        out_shape=[SDS((t, W_A), F32), SDS((n, HA_HEADS, HA_DIM, HA_DIM), F32)],
        scratch_shapes=[pltpu.VMEM((HA_HEADS, HA_DIM, HA_DIM), F32)],
        compiler_params=_params(("arbitrary",)))(p_h, p_h, p_h, p_h, l0, l1, onorm)


def _hgrn_bwd(p_h, l0, l1, onorm, hs, do, do_col):
    t = p_h.shape[0]
    c, n = HGRN_CHUNK, p_h.shape[0] // HGRN_CHUNK

    def body(q_ref, f_ref, i_ref, g_ref, l0_ref, l1_ref, on_ref, hs_ref, do_ref,
             dp_ref, dl0_ref, dl1_ref, don_ref, dst_ref):
        @pl.when(pl.program_id(0) == 0)
        def _():
            dst_ref[...] = jnp.zeros_like(dst_ref)
            dl0_ref[...] = jnp.zeros_like(dl0_ref)
            dl1_ref[...] = jnp.zeros_like(dl1_ref)
            don_ref[...] = jnp.zeros_like(don_ref)

        for h in range(HA_HEADS):
            sl = slice(h * HA_DIM, (h + 1) * HA_DIM)
            args = (hs_ref[0, h], q_ref[:, sl], f_ref[:, sl], i_ref[:, sl], g_ref[:, sl],
                    l0_ref[:, sl], l1_ref[:, sl], on_ref[:, sl])
            _, vjp = jax.vjp(_hgrn_head, *args)
            dst0, dq, df, di, dg, dl0, dl1, don = vjp((do_ref[:, sl], dst_ref[h]))
            for j, dv in enumerate((dq, df, di, dg)):
                dp_ref[:, j * W_A + h * HA_DIM:j * W_A + (h + 1) * HA_DIM] = dv
            dl0_ref[:, sl] += dl0
            dl1_ref[:, sl] += dl1
            don_ref[:, sl] += don
            dst_ref[h] = dst0

    col = lambda j: pl.BlockSpec((c, W_A), lambda i, j=j: (n - 1 - i, j))
    par = pl.BlockSpec((1, W_A), lambda i: (0, 0))
    return pl.pallas_call(
        body, name="hgrn_bwd", grid=(n,),
        in_specs=[col(0), col(1), col(2), col(3), par, par, par,
                  pl.BlockSpec((1, HA_HEADS, HA_DIM, HA_DIM), lambda i: (n - 1 - i, 0, 0, 0)),
                  pl.BlockSpec((c, W_A), lambda i: (n - 1 - i, do_col))],
        out_specs=[pl.BlockSpec((c, N_HGRN_COLS), lambda i: (n - 1 - i, 0)), par, par, par],
        out_shape=[SDS((t, N_HGRN_COLS), F32), SDS((1, W_A), F32), SDS((1, W_A), F32), SDS((1, W_A), F32)],
        scratch_shapes=[pltpu.VMEM((HA_HEADS, HA_DIM, HA_DIM), F32)],
        compiler_params=_params(("arbitrary",)))(p_h, p_h, p_h, p_h, l0, l1, onorm, hs, do)


def _rwkv_step(s0, r, lw, k, v, a, b):
    nh, nj = len(r), len(r[0])
    c = r[0][0].shape[0]
    pairs = [(j, h) for j in range(nj) for h in range(nh)]
    every = lambda fn: {p: fn(p) for p in pairs}
    at_ = lambda d: (lambda p: d[p[1]][p[0]])
    r_, lw_, k_, v_, a_, b_ = (at_(z) for z in (r, lw, k, v, a, b))
    incl, strict = _tri(c), _tri(c, strict=True)
    incl_f = incl.astype(F32)
    eye = (lax.broadcasted_iota(jnp.int32, (c, c), 0) == lax.broadcasted_iota(jnp.int32, (c, c), 1)).astype(F32)

    gam = every(lambda p: _nn_hi(incl_f, lw_(p)))
    gtot = every(lambda p: jnp.sum(lw_(p), axis=0, keepdims=True))
    at = every(lambda p: a_(p) * jnp.exp(gam[p] - lw_(p)))
    rt = every(lambda p: r_(p) * jnp.exp(gam[p]))
    eneg = every(lambda p: jnp.exp(-gam[p]))
    bt = every(lambda p: b_(p) * eneg[p])
    kt = every(lambda p: k_(p) * eneg[p])
    edec = every(lambda p: jnp.exp(gtot[p] - gam[p]))
    bdec = every(lambda p: b_(p) * edec[p])
    kdec = every(lambda p: k_(p) * edec[p])
    a_ab = every(lambda p: jnp.where(strict, _nt(at[p], bt[p]), 0.0))
    a_ak = every(lambda p: jnp.where(strict, _nt(at[p], kt[p]), 0.0))
    a_rb = every(lambda p: jnp.where(incl, _nt(rt[p], bt[p]), 0.0))
    a_rk = every(lambda p: jnp.where(incl, _nt(rt[p], kt[p]), 0.0))
    tinv = every(lambda p: eye + a_ab[p])
    pw = a_ab
    span = 2
    while span < c:
        pw = every(lambda p, pw=pw: _nn_hi(pw[p], pw[p]))
        tinv = every(lambda p, pw=pw, tinv=tinv: tinv[p] + _nn_hi(pw[p], tinv[p]))
        span *= 2
    akv = every(lambda p: _nn(a_ak[p], v_(p)))
    w1 = every(lambda p: _nn_hi(tinv[p], at[p]))
    u0 = every(lambda p: _nn_hi(tinv[p], akv[p]))
    r1 = every(lambda p: rt[p] + _nn(a_rb[p], w1[p]))
    y0 = every(lambda p: _nn(a_rb[p], u0[p]) + _nn(a_rk[p], v_(p)))
    mm = every(lambda p: _tn(w1[p], bdec[p]))
    zz = every(lambda p: _tn(u0[p], bdec[p]) + _tn(v_(p), kdec[p]))
    gdec = every(lambda p: jnp.exp(gtot[p]))

    s = list(s0)
    y = [[None] * nj for _ in range(nh)]
    for j in range(nj):
        for h in range(nh):
            y[h][j] = _nt(r1[(j, h)], s[h]) + y0[(j, h)]
        s = [s[h] * gdec[(j, h)] + _nn(s[h], mm[(j, h)]) + zz[(j, h)] for h in range(nh)]
    return y, s


def _rwkv_blocks(ref, nj, c):
    return [[ref[h, j * c:(j + 1) * c, :] for j in range(nj)] for h in range(HB_HEADS)]


def _rwkv_fwd(seqs):
    t = seqs[0].shape[1]
    c, nj = RWKV_CHUNK, RWKV_GROUP
    n = t // (c * nj)

    def body(r_ref, lw_ref, k_ref, v_ref, a_ref, b_ref, y_ref, hs_ref, st_ref):
        @pl.when(pl.program_id(0) == 0)
        def _():
            st_ref[...] = jnp.zeros_like(st_ref)

        hs_ref[0] = st_ref[...]
        s0 = [st_ref[h] for h in range(HB_HEADS)]
        y, s1 = _rwkv_step(s0, *[_rwkv_blocks(ref, nj, c) for ref in (r_ref, lw_ref, k_ref, v_ref, a_ref, b_ref)])
        for h in range(HB_HEADS):
            for j in range(nj):
                y_ref[h, j * c:(j + 1) * c, :] = y[h][j]
            st_ref[h] = s1[h]

    seq = pl.BlockSpec((HB_HEADS, c * nj, HB_DIM), lambda i: (0, i, 0))
    return pl.pallas_call(
        body, name="rwkv_fwd", grid=(n,), in_specs=[seq] * 6,
        out_specs=[seq, pl.BlockSpec((1, HB_HEADS, HB_DIM, HB_DIM), lambda i: (i, 0, 0, 0))],
        out_shape=[SDS((HB_HEADS, t, HB_DIM), F32), SDS((n, HB_HEADS, HB_DIM, HB_DIM), F32)],
        scratch_shapes=[pltpu.VMEM((HB_HEADS, HB_DIM, HB_DIM), F32)],
        compiler_params=_params(("arbitrary",)))(*seqs)


def _rwkv_bwd(seqs, hs, dy):
    t = seqs[0].shape[1]
    c, nj = RWKV_CHUNK, RWKV_GROUP
    n = t // (c * nj)

    def body(r_ref, lw_ref, k_ref, v_ref, a_ref, b_ref, hs_ref, dy_ref,
             dr_ref, dlw_ref, dk_ref, dv_ref, da_ref, db_ref, dst_ref):
        @pl.when(pl.program_id(0) == 0)
        def _():
            dst_ref[...] = jnp.zeros_like(dst_ref)

        s0 = [hs_ref[0, h] for h in range(HB_HEADS)]
        seq_vals = [_rwkv_blocks(ref, nj, c) for ref in (r_ref, lw_ref, k_ref, v_ref, a_ref, b_ref)]
        _, vjp = jax.vjp(_rwkv_step, s0, *seq_vals)
        grads = vjp((_rwkv_blocks(dy_ref, nj, c), [dst_ref[h] for h in range(HB_HEADS)]))
        for ref, gr in zip((dr_ref, dlw_ref, dk_ref, dv_ref, da_ref, db_ref), grads[1:]):
            for h in range(HB_HEADS):
                for j in range(nj):
                    ref[h, j * c:(j + 1) * c, :] = gr[h][j]
        for h in range(HB_HEADS):
            dst_ref[h] = grads[0][h]

    seq = pl.BlockSpec((HB_HEADS, c * nj, HB_DIM), lambda i: (0, n - 1 - i, 0))
    return pl.pallas_call(
        body, name="rwkv_bwd", grid=(n,),
        in_specs=[seq] * 6 + [pl.BlockSpec((1, HB_HEADS, HB_DIM, HB_DIM), lambda i: (n - 1 - i, 0, 0, 0)), seq],
        out_specs=[seq] * 6, out_shape=[SDS((HB_HEADS, t, HB_DIM), F32)] * 6,
        scratch_shapes=[pltpu.VMEM((HB_HEADS, HB_DIM, HB_DIM), F32)],
        compiler_params=_params(("arbitrary",)))(*seqs, hs, dy)


def _final_loss(x3, fnorm, target, *, tm):
    t, d = x3.shape

    def body(x_ref, g_ref, t_ref, dx_ref, dg_ref, loss_ref):
        @pl.when(pl.program_id(0) == 0)
        def _():
            dg_ref[...] = jnp.zeros_like(dg_ref)
            loss_ref[...] = jnp.zeros_like(loss_ref)

        x, g = x_ref[...], g_ref[...]
        rinv = lax.rsqrt(jnp.mean(x * x, axis=-1, keepdims=True) + NORM_EPS)
        xh = x * rinv
        diff = xh * g - t_ref[...]
        loss_ref[...] += 0.5 * jnp.sum(jnp.mean(diff * diff, axis=-1, keepdims=True))
        dy = diff * (1.0 / d)
        dg_ref[...] += jnp.sum(dy * xh, axis=0, keepdims=True)
        dxh = dy * g
        dx_ref[...] = rinv * (dxh - xh * jnp.mean(dxh * xh, axis=-1, keepdims=True))

    row = pl.BlockSpec((tm, d), lambda i: (i, 0))
    return pl.pallas_call(
        body, name="final_loss", grid=(t // tm,), in_specs=[row, pl.BlockSpec((1, d), lambda i: (0, 0)), row],
        out_specs=[row, pl.BlockSpec((1, d), lambda i: (0, 0)), pl.BlockSpec((8, 128), lambda i: (0, 0))],
        out_shape=[SDS((t, d), F32), SDS((1, d), F32), SDS((8, 128), F32)],
        compiler_params=_params(("arbitrary",)))(x3, fnorm, target)


def _gate_up_act(h, wgt, wut, *, tm, tn, name):
    t, d = h.shape
    tm = min(tm, t)

    def body(h_ref, g_ref, u_ref, a_out, u_out, act_out):
        hv = h_ref[...]
        a = _dg(hv, g_ref[...], 1, 1, False)
        u = _dg(hv, u_ref[...], 1, 1, False)
        a_out[...] = a
        u_out[...] = u
        act_out[...] = (_silu(a) * u).astype(act_out.dtype)

    wspec = pl.BlockSpec((tn, d), lambda i, j: (j, 0))
    ospec = pl.BlockSpec((tm, tn), lambda i, j: (i, j))
    return pl.pallas_call(
        body, name=name, grid=(t // tm, D_FF // tn), in_specs=[pl.BlockSpec((tm, d), lambda i, j: (i, 0)), wspec, wspec],
        out_specs=[ospec, ospec, ospec], out_shape=[SDS((t, D_FF), F32), SDS((t, D_FF), F32), SDS((t, D_FF), BF16)],
        compiler_params=_params(("parallel", "parallel")))(h, wgt, wut)


def _dact_swiglu(dout, wd, a, u, *, tm, tn, name):
    t, d = dout.shape
    tm = min(tm, t)

    def body(d_ref, w_ref, a_ref, u_ref, da_out, du_out):
        dact = 0.5 * _dg(d_ref[...], w_ref[...], 1, 1, False)
        av, uv = a_ref[...], u_ref[...]
        s = _sigmoid(av)
        da_out[...] = (dact * uv * (s * (1.0 + av * (1.0 - s)))).astype(da_out.dtype)
        du_out[...] = (dact * (av * s)).astype(du_out.dtype)

    tile = pl.BlockSpec((tm, tn), lambda i, j: (i, j))
    return pl.pallas_call(
        body, name=name, grid=(t // tm, D_FF // tn),
        in_specs=[pl.BlockSpec((tm, d), lambda i, j: (i, 0)), pl.BlockSpec((tn, d), lambda i, j: (j, 0)), tile, tile],
        out_specs=[tile, tile], out_shape=[SDS((t, D_FF), BF16), SDS((t, D_FF), BF16)],
        compiler_params=_params(("parallel", "parallel")))(dout, wd, a, u)


def _ffn_fwd(x, norm, wgt, wut, wd, tag):
    h, = _rowwise(_rms_f, [x], [norm], [[0]], [BF16], tm=512, name=f"{tag}_rms")
    a, u, act = _gate_up_act(h, wgt, wut, tm=512, tn=D_FF // 2, name=f"{tag}_gate_up")
    out = _mm(act, wd, tm=512, tn=D_MODEL, tk=D_FF // 2, name=f"{tag}_down", res=x, scale=0.5)
    return out, (h, a, u, act)


def _ffn_bwd(dout, x, norm, wgt, wut, wd, saved, tag):
    h, a, u, act = saved
    da, du = _dact_swiglu(dout, wd, a, u, tm=512, tn=D_FF // 2, name=f"{tag}_dact")
    dwd = _mm(act, dout, ta=True, tm=D_FF // 2, tn=D_MODEL, tk=512, name=f"{tag}_dwd", scale=0.5)
    dwgt = _mm(da, h, ta=True, tm=D_FF // 2, tn=D_MODEL, tk=512, name=f"{tag}_dwg")
    dwut = _mm(du, h, ta=True, tm=D_FF // 2, tn=D_MODEL, tk=512, name=f"{tag}_dwu")
    dh = _mm(da, wgt, tm=512, tn=D_MODEL, tk=D_FF // 2, name=f"{tag}_dh_g")
    dh = _mm(du, wut, tm=512, tn=D_MODEL, tk=D_FF // 2, name=f"{tag}_dh_u", res=dh)
    dx, dnorm = _rowwise_bwd(_rms_f, [x], [norm], [dh], x_grad=[True], p_grad=[True], dx_groups=[[0]],
                             dx_dtypes=[F32], tm=256, name=f"{tag}_drms", extra={0: dout})
    return dx, dnorm, dwgt, dwut, dwd


def _to_heads(z):
    return z.reshape(z.shape[0], HB_HEADS, HB_DIM).transpose(1, 0, 2)


def _from_heads(z):
    return z.transpose(1, 0, 2).reshape(z.shape[1], W_B)


def _local_step(x, target, w):
    ones_bd = jnp.kron(jnp.eye(HB_HEADS, dtype=F32), jnp.ones((HB_DIM, HB_DIM), F32))
    g = {}
    x1, ffn1_saved = _ffn_fwd(x, w["ffn1_norm"], w["ffn1_wgt"], w["ffn1_wut"], w["ffn1_wd"], "ffn1")
    hm, = _rowwise(_rms_f, [x1], [w["mix_norm"]], [[0]], [BF16], tm=512, name="mix_rms")
    p_h = _mm(hm, w["w_in_h"], tm=512, tn=512, tk=D_MODEL, name="inproj_h")
    p_r = _mm(hm, w["w_in_r"], tm=512, tn=N_RWKV_PAD // 2, tk=D_MODEL, name="inproj_r")
    o_a, hgrn_states = _hgrn_fwd(p_h, w["lb0"], w["lb1"], w["hgrn_out_norm"])

    mu = w["mu_pad"]
    prep_xs = [(p_r, W_B, 0), (p_r, W_B, 1), (p_r, W_B, 2), (p_r, LORA_PAD, 6),
               ("prev", p_r, W_B, 0), ("prev", p_r, W_B, 1), ("prev", p_r, W_B, 2), ("prev", p_r, LORA_PAD, 6)]
    prep_ps = [(mu, W_B, 0), (mu, W_B, 1), (mu, W_B, 2), (mu, LORA_PAD, 6), w["rwkv_w0"], w["w2_pad"], w["rwkv_a0"],
               w["a2_pad"], w["g2_pad"], w["rwkv_k_k"], w["rwkv_k_a"], ones_bd]
    prep_f = _rwkv_prep_f
    r, lw, k2, v, a_vec, b_vec, gate = _rowwise(prep_f, prep_xs, prep_ps, [[0], [1], [2], [3], [4], [5], [6]],
                                                [F32] * 7, tm=256, name="rwkv_prep")
    seqs = [_to_heads(z) for z in (r, lw, k2, v, a_vec, b_vec)]
    y_h, rwkv_states = _rwkv_fwd(seqs)
    y = _from_heads(y_h)
    post_f = _rwkv_post_f
    post_xs = [y, r, k2, v, gate]
    post_ps = [w["rwkv_r_k"], w["rwkv_gn_w"], w["rwkv_gn_b"], ones_bd]
    o_b, = _rowwise(post_f, post_xs, post_ps, [[0]], [F32], tm=256, name="rwkv_post")
    x2 = _mm(o_a, w["w_out_a"], tm=512, tn=D_MODEL, tk=W_A, name="outproj_a", res=x1)
    x2 = _mm(o_b, w["w_out_b"], tm=512, tn=D_MODEL, tk=W_B, name="outproj_b", res=x2)
    x3, ffn2_saved = _ffn_fwd(x2, w["ffn2_norm"], w["ffn2_wgt"], w["ffn2_wut"], w["ffn2_wd"], "ffn2")
    dx3, g["final_norm"], loss = _final_loss(x3, w["final_norm"], target, tm=256)

    dx2, g["ffn2_norm"], g["ffn2_wgt"], g["ffn2_wut"], g["ffn2_wd"] = _ffn_bwd(
        dx3, x2, w["ffn2_norm"], w["ffn2_wgt"], w["ffn2_wut"], w["ffn2_wd"], ffn2_saved, "ffn2")
    do_a = _mm(dx2, w["w_out_a"], tb=True, tm=512, tn=W_A, tk=D_MODEL, name="outproj_do_a")
    do_b = _mm(dx2, w["w_out_b"], tb=True, tm=512, tn=W_B, tk=D_MODEL, name="outproj_do_b")
    g["w_out_a"] = _mm(o_a, dx2, ta=True, tm=W_A, tn=D_MODEL, tk=512, name="outproj_dw_a")
    g["w_out_b"] = _mm(o_b, dx2, ta=True, tm=W_B, tn=D_MODEL, tk=512, name="outproj_dw_b")

    dp_h, g["lb0"], g["lb1"], g["hgrn_out_norm"] = _hgrn_bwd(p_h, w["lb0"], w["lb1"], w["hgrn_out_norm"],
                                                             hgrn_states, do_a, 0)
    post_out = _rowwise_bwd(post_f, post_xs, post_ps, [do_b], x_grad=[True] * 5, p_grad=[True] * 3 + [False],
                            dx_groups=[[0], [1], [2], [3], [4]], dx_dtypes=[F32] * 5, tm=256, name="rwkv_post_bwd")
    dy, dr1, dk1, dv1, dgate, g["rwkv_r_k"], g["rwkv_gn_w"], g["rwkv_gn_b"] = post_out
    dseq = _rwkv_bwd(seqs, rwkv_states, _to_heads(dy))
    dr2, dlw, dk2, dv2, da_vec, db_vec = [_from_heads(z) for z in dseq]

    def prep2_f(*vals):
        r_, lw_, k2_, v_, a_, b_, g_ = prep_f(*vals)
        return r_, lw_, k2_, v_, a_, b_, g_, r_, k2_, v_

    prep_out = _rowwise_bwd(prep2_f, prep_xs, prep_ps, [dr2, dlw, dk2, dv2, da_vec, db_vec, dgate, dr1, dk1, dv1],
                            x_grad=[True] * 8, p_grad=[True] * 11 + [False], dx_groups=[[0, 1, 2, 3], [4, 5, 6, 7]],
                            dx_dtypes=[F32, F32], tm=256, name="rwkv_prep_bwd")
    dpr_main, dpr_prev = prep_out[0], prep_out[1]
    (dmu_r, dmu_k, dmu_v, dmu_lo, g["rwkv_w0"], g["w2_pad"], g["rwkv_a0"], g["a2_pad"], g["g2_pad"],
     g["rwkv_k_k"], g["rwkv_k_a"]) = prep_out[2:]
    g["mu_pad"] = jnp.concatenate([dmu_r, dmu_k, dmu_v, dmu_lo], axis=1)
    dp_r, = _rowwise(lambda u_, s_: (u_ + s_,), [dpr_main, ("next", dpr_prev, N_RWKV_PAD, 0)], [], [[0]], [F32],
                     tm=512, name="rwkv_dp_sum")
    dhm = _mm(dp_h, w["w_in_h"], tb=True, tm=512, tn=D_MODEL, tk=D_MODEL, name="inproj_dh_h")
    dhm = _mm(dp_r, w["w_in_r"], tb=True, tm=512, tn=D_MODEL, tk=N_RWKV_PAD // 2, name="inproj_dh_r", res=dhm)
    g["w_in_h"] = _mm(hm, dp_h, ta=True, tm=D_MODEL, tn=D_MODEL, tk=512, name="inproj_dw_h")
    g["w_in_r"] = _mm(hm, dp_r, ta=True, tm=D_MODEL, tn=N_RWKV_PAD // 2, tk=512, name="inproj_dw_r")
    dx1, g["mix_norm"] = _rowwise_bwd(_rms_f, [x1], [w["mix_norm"]], [dhm], x_grad=[True], p_grad=[True],
                                      dx_groups=[[0]], dx_dtypes=[F32], tm=256, name="mix_drms", extra={0: dx2})
    dx0, g["ffn1_norm"], g["ffn1_wgt"], g["ffn1_wut"], g["ffn1_wd"] = _ffn_bwd(
        dx1, x, w["ffn1_norm"], w["ffn1_wgt"], w["ffn1_wut"], w["ffn1_wd"], ffn1_saved, "ffn1")
    return loss, dx0, g


HBM_SPEC = pl.BlockSpec(memory_space=pl.ANY)


def _chips(x, y):
    return [(1 - x, y), (x, 1 - y), (1 - x, 1 - y)]


def _gather_weights(shards):
    n = len(shards)

    def body(*refs):
        ins, outs = refs[:n], refs[n:2 * n]
        send_sems, recv_sems, local_sems = refs[2 * n:]
        x, y, c = lax.axis_index("x"), lax.axis_index("y"), lax.axis_index("c")
        me = 2 * x + y
        mine = [pltpu.make_async_copy(ins[t], outs[t].at[me], local_sems.at[t]) for t in range(n)]
        for cp in mine:
            cp.start()

        def copy(t, j, slot, px, py):
            return pltpu.make_async_remote_copy(src_ref=ins[t], dst_ref=outs[t].at[slot], send_sem=send_sems.at[3 * t + j],
                                                recv_sem=recv_sems.at[3 * t + j], device_id=(px, py, c),
                                                device_id_type=MESH)

        sends = [copy(t, j, me, px, py) for t in range(n) for j, (px, py) in enumerate(_chips(x, y))]
        for cp in sends:
            cp.start()
        for t in range(n):
            for j, (px, py) in enumerate(_chips(x, y)):
                copy(t, j, 2 * px + py, px, py).wait_recv()
        for cp in sends:
            cp.wait_send()
        for cp in mine:
            cp.wait()

    return pl.pallas_call(
        body, name="gather_weights", in_specs=[HBM_SPEC] * n, out_specs=[HBM_SPEC] * n,
        out_shape=[SDS((N_CHIPS,) + s.shape, s.dtype) for s in shards],
        scratch_shapes=[pltpu.SemaphoreType.DMA((3 * n,)), pltpu.SemaphoreType.DMA((3 * n,)),
                        pltpu.SemaphoreType.DMA((n,))],
    )(*shards)


def _sibling_exchange(gs):
    n = len(gs)

    def body(*refs):
        ins, outs = refs[:n], refs[n:2 * n]
        send_sems, recv_sems = refs[2 * n:]
        x, y, c = lax.axis_index("x"), lax.axis_index("y"), lax.axis_index("c")
        cps = []
        for t in range(n):
            hr = gs[t].shape[1] // 2
            src = ins[t].at[:, pl.ds(pl.multiple_of((1 - c) * hr, SUBLANES), hr), :]
            cps.append(pltpu.make_async_remote_copy(src_ref=src, dst_ref=outs[t], send_sem=send_sems.at[t],
                                                    recv_sem=recv_sems.at[t], device_id=(x, y, 1 - c),
                                                    device_id_type=MESH))
        for cp in cps:
            cp.start()
        for cp in cps:
            cp.wait()

    return pl.pallas_call(
        body, name="grad_sibling_exchange", in_specs=[HBM_SPEC] * n, out_specs=[HBM_SPEC] * n,
        out_shape=[SDS((N_CHIPS, g.shape[1] // 2, g.shape[2]), g.dtype) for g in gs],
        scratch_shapes=[pltpu.SemaphoreType.DMA((n,)), pltpu.SemaphoreType.DMA((n,))],
    )(*gs)


def _chip_exchange(ss):
    n = len(ss)

    def body(*refs):
        ins, outs = refs[:n], refs[n:2 * n]
        send_sems, recv_sems, local_sems = refs[2 * n:]
        x, y, c = lax.axis_index("x"), lax.axis_index("y"), lax.axis_index("c")
        me = 2 * x + y
        mine = [pltpu.make_async_copy(ins[t].at[me], outs[t].at[me], local_sems.at[t]) for t in range(n)]
        for cp in mine:
            cp.start()

        def copy(t, j, px, py, src_slot, dst_slot):
            return pltpu.make_async_remote_copy(src_ref=ins[t].at[src_slot], dst_ref=outs[t].at[dst_slot],
                                                send_sem=send_sems.at[3 * t + j], recv_sem=recv_sems.at[3 * t + j],
                                                device_id=(px, py, c), device_id_type=MESH)

        sends = [copy(t, j, px, py, 2 * px + py, me) for t in range(n) for j, (px, py) in enumerate(_chips(x, y))]
        for cp in sends:
            cp.start()
        for t in range(n):
            for j, (px, py) in enumerate(_chips(x, y)):
                copy(t, j, px, py, me, 2 * px + py).wait_recv()
        for cp in sends:
            cp.wait_send()
        for cp in mine:
            cp.wait()

    return pl.pallas_call(
        body, name="grad_chip_exchange", in_specs=[HBM_SPEC] * n, out_specs=[HBM_SPEC] * n,
        out_shape=[SDS(s.shape, s.dtype) for s in ss],
        scratch_shapes=[pltpu.SemaphoreType.DMA((3 * n,)), pltpu.SemaphoreType.DMA((3 * n,)),
                        pltpu.SemaphoreType.DMA((n,))],
    )(*ss)


def _sibling_allgather(fs):
    n = len(fs)

    def body(*refs):
        ins, outs = refs[:n], refs[n:2 * n]
        send_sems, recv_sems, local_sems = refs[2 * n:]
        x, y, c = lax.axis_index("x"), lax.axis_index("y"), lax.axis_index("c")
        mine = [pltpu.make_async_copy(ins[t], outs[t].at[c], local_sems.at[t]) for t in range(n)]
        for cp in mine:
            cp.start()

        def copy(t, slot):
            return pltpu.make_async_remote_copy(src_ref=ins[t], dst_ref=outs[t].at[slot], send_sem=send_sems.at[t],
                                                recv_sem=recv_sems.at[t], device_id=(x, y, 1 - c),
                                                device_id_type=MESH)

        sends = [copy(t, c) for t in range(n)]
        for cp in sends:
            cp.start()
        for t in range(n):
            copy(t, 1 - c).wait_recv()
        for cp in sends:
            cp.wait_send()
        for cp in mine:
            cp.wait()

    return pl.pallas_call(
        body, name="grad_sibling_allgather", in_specs=[HBM_SPEC] * n, out_specs=[HBM_SPEC] * n,
        out_shape=[SDS((2,) + f.shape, f.dtype) for f in fs],
        scratch_shapes=[pltpu.SemaphoreType.DMA((n,)), pltpu.SemaphoreType.DMA((n,)), pltpu.SemaphoreType.DMA((n,))],
    )(*fs)


def _row_tile(rows, cap=512):
    best = SUBLANES
    for tr in range(SUBLANES, min(rows, cap) + 1, SUBLANES):
        if rows % tr == 0:
            best = tr
    return best


def _add_halves(g4, r4, c_idx, name):
    _, hr, lanes = r4.shape
    tr = _row_tile(hr)
    nb = hr // tr

    def body(c_ref, a_ref, b_ref, o_ref):
        o_ref[...] = a_ref[...] + b_ref[...]

    grid_spec = pltpu.PrefetchScalarGridSpec(
        num_scalar_prefetch=1, grid=(N_CHIPS, nb),
        in_specs=[pl.BlockSpec((None, tr, lanes), lambda q, i, c_ref: (q, c_ref[0] * nb + i, 0)),
                  pl.BlockSpec((None, tr, lanes), lambda q, i, c_ref: (q, i, 0))],
        out_specs=pl.BlockSpec((None, tr, lanes), lambda q, i, c_ref: (q, i, 0)))
    return pl.pallas_call(body, name=name, grid_spec=grid_spec, out_shape=SDS(r4.shape, F32),
                          compiler_params=_params(("parallel", "parallel")))(c_idx, g4, r4)


def _sum_chips(r4, name):
    _, rows, lanes = r4.shape
    tr = _row_tile(rows)

    def body(a_ref, b_ref, c_ref, d_ref, o_ref):
        o_ref[...] = ((a_ref[...] + b_ref[...]) + c_ref[...]) + d_ref[...]

    specs = [pl.BlockSpec((None, tr, lanes), lambda i, q=q: (q, i, 0)) for q in range(N_CHIPS)]
    return pl.pallas_call(body, name=name, grid=(rows // tr,), in_specs=specs,
                          out_specs=pl.BlockSpec((tr, lanes), lambda i: (i, 0)), out_shape=SDS((rows, lanes), F32),
                          compiler_params=_params(("parallel",)))(r4, r4, r4, r4)


def _adamw(wf, gf, mf, vf, name):
    rows, lanes = wf.shape
    tr = _row_tile(rows)
    c1 = 1.0 / (1.0 - ADAM_B1 ** ADAM_STEP)
    c2 = 1.0 / (1.0 - ADAM_B2 ** ADAM_STEP)

    def body(w_ref, g_ref, m_ref, v_ref, d_ref, nm_ref, nv_ref):
        gv = g_ref[...]
        m = ADAM_B1 * m_ref[...] + (1.0 - ADAM_B1) * gv
        v = ADAM_B2 * v_ref[...] + (1.0 - ADAM_B2) * (gv * gv)
        d_ref[...] = -ADAM_LR * ((m * c1) / (jnp.sqrt(v * c2) + ADAM_EPS) + ADAM_WD * w_ref[...])
        nm_ref[...] = m
        nv_ref[...] = v

    spec = pl.BlockSpec((tr, lanes), lambda i: (i, 0))
    return pl.pallas_call(body, name=name, grid=(rows // tr,), in_specs=[spec] * 4, out_specs=[spec] * 3,
                          out_shape=[SDS((rows, lanes), F32)] * 3, compiler_params=_params(("parallel",)))(wf, gf, mf, vf)


BIG = ("ffn1_w_gate", "ffn1_w_up", "ffn1_w_down", "ffn2_w_gate", "ffn2_w_up", "ffn2_w_down", "w_out")
TRANSPOSED = ("ffn1_w_gate", "ffn1_w_up", "ffn2_w_gate", "ffn2_w_up")
PACKED = ("w_in", "rwkv_w2", "rwkv_a2", "rwkv_g2")
SMALL_SHAPES = {"ffn1_norm": (1, D_MODEL), "mix_norm": (1, D_MODEL), "hgrn_lb_logits": (2, W_A),
                "hgrn_out_norm": (1, W_A), "rwkv_shift_mu": (1, N_RWKV_COLS), "rwkv_w0": (1, W_B),
                "rwkv_a0": (1, W_B), "rwkv_k_k": (1, W_B), "rwkv_k_a": (1, W_B),
                "rwkv_r_k": (1, HB_HEADS, HB_DIM), "rwkv_gn_w": (1, W_B), "rwkv_gn_b": (1, W_B),
                "ffn2_norm": (1, D_MODEL), "final_norm": (D_MODEL,)}
PACK_ELEMS = sum(_numel(_shard_shape(n)) for n in PACKED) + sum(_numel(SMALL_SHAPES[n]) for n in SMALL)
PACK_ROWS = -(-PACK_ELEMS // (16 * LANES)) * 16


def _to_rows(name, shard):
    return shard[0].T if name in TRANSPOSED else shard[0]


def _from_rows(name, rows):
    return (rows.T if name in TRANSPOSED else rows)[None]


def _pack(sharded, small):
    flat = jnp.concatenate([sharded[n].reshape(-1) for n in PACKED] + [small[n].reshape(-1) for n in SMALL])
    return jnp.pad(flat, (0, PACK_ROWS * LANES - flat.shape[0])).reshape(PACK_ROWS, LANES)


def _unpack(packed):
    flat, out, off = packed.reshape(-1), {}, 0
    for n in PACKED:
        shp = _shard_shape(n)
        out[n] = flat[off:off + _numel(shp)].reshape((1,) + shp)
        off += _numel(shp)
    for n in SMALL:
        shp = SMALL_SHAPES[n]
        out[n] = flat[off:off + _numel(shp)].reshape(shp)
        off += _numel(shp)
    return out


def _quarter(full, name, q):
    shape, ax = SHARDED_SHAPES[name]
    w = shape[ax] // N_CHIPS
    return lax.slice_in_dim(full, q * w, (q + 1) * w, axis=ax)


def kernel(x, ffn1_norm, ffn1_w_gate, ffn1_w_up, ffn1_w_down, mix_norm, w_in, hgrn_lb_logits, hgrn_out_norm, rwkv_shift_mu, rwkv_w0, rwkv_w2, rwkv_a0, rwkv_a2, rwkv_g2, rwkv_k_k, rwkv_k_a, rwkv_r_k, rwkv_gn_w, rwkv_gn_b, w_out, ffn2_norm, ffn2_w_gate, ffn2_w_up, ffn2_w_down, final_norm, loss_target, m_ffn1_norm, m_ffn1_w_gate, m_ffn1_w_up, m_ffn1_w_down, m_mix_norm, m_w_in, m_hgrn_lb_logits, m_hgrn_out_norm, m_rwkv_shift_mu, m_rwkv_w0, m_rwkv_w2, m_rwkv_a0, m_rwkv_a2, m_rwkv_g2, m_rwkv_k_k, m_rwkv_k_a, m_rwkv_r_k, m_rwkv_gn_w, m_rwkv_gn_b, m_w_out, m_ffn2_norm, m_ffn2_w_gate, m_ffn2_w_up, m_ffn2_w_down, m_final_norm, v_ffn1_norm, v_ffn1_w_gate, v_ffn1_w_up, v_ffn1_w_down, v_mix_norm, v_w_in, v_hgrn_lb_logits, v_hgrn_out_norm, v_rwkv_shift_mu, v_rwkv_w0, v_rwkv_w2, v_rwkv_a0, v_rwkv_a2, v_rwkv_g2, v_rwkv_k_k, v_rwkv_k_a, v_rwkv_r_k, v_rwkv_gn_w, v_rwkv_gn_b, v_w_out, v_ffn2_norm, v_ffn2_w_gate, v_ffn2_w_up, v_ffn2_w_down, v_final_norm):
    args = dict(locals())
    wts = {n: args[n] for n in ALL_WEIGHTS}
    moms = {n: args["m_" + n] for n in ALL_WEIGHTS}
    vars_ = {n: args["v_" + n] for n in ALL_WEIGHTS}

    small_w = {n: wts[n] for n in SMALL}
    shards = [_to_rows(n, wts[n]).astype(BF16) for n in BIG] + [_pack(wts, small_w).astype(BF16)]
    gathered = _gather_weights(shards)
    rows_of = dict(zip(BIG, gathered[:-1]))
    packs = gathered[-1].reshape(N_CHIPS, PACK_ROWS * LANES)
    full, off = {}, 0
    for n in PACKED:
        shp = _shard_shape(n)
        full[n] = jnp.concatenate([packs[q, off:off + _numel(shp)].reshape(shp) for q in range(N_CHIPS)], axis=1)
        off += _numel(shp)

    w = {}
    for tag in ("ffn1", "ffn2"):
        w[f"{tag}_wgt"] = rows_of[f"{tag}_w_gate"].reshape(D_FF, D_MODEL)
        w[f"{tag}_wut"] = rows_of[f"{tag}_w_up"].reshape(D_FF, D_MODEL)
        w[f"{tag}_wd"] = rows_of[f"{tag}_w_down"].reshape(D_FF, D_MODEL)
        w[f"{tag}_norm"] = wts[f"{tag}_norm"]
    w["w_in_h"] = full["w_in"][:, :N_HGRN_COLS]
    w["w_in_r"] = jnp.pad(full["w_in"][:, N_HGRN_COLS:], ((0, 0), (0, N_RWKV_PAD - N_RWKV_COLS)))
    w_out_full = rows_of["w_out"].reshape(D_MODEL, D_MODEL)
    w["w_out_a"], w["w_out_b"] = w_out_full[:W_A], w_out_full[W_A:]
    zrow = lambda nrow: jnp.zeros((nrow, W_B), BF16)
    w["w2_pad"] = jnp.concatenate([full["rwkv_w2"], zrow(LORA_PAD - 32)], axis=0)
    w["a2_pad"] = jnp.concatenate([zrow(32), full["rwkv_a2"], zrow(LORA_PAD - 64)], axis=0)
    w["g2_pad"] = jnp.concatenate([zrow(64), full["rwkv_g2"], zrow(LORA_PAD - 160)], axis=0)
    w["mix_norm"] = mix_norm
    w["lb0"], w["lb1"] = hgrn_lb_logits[0:1], hgrn_lb_logits[1:2]
    w["hgrn_out_norm"] = hgrn_out_norm
    w["mu_pad"] = jnp.pad(rwkv_shift_mu, ((0, 0), (0, N_RWKV_PAD - N_RWKV_COLS)))
    for n in ("rwkv_w0", "rwkv_a0", "rwkv_k_k", "rwkv_k_a", "rwkv_gn_w", "rwkv_gn_b"):
        w[n] = wts[n]
    w["rwkv_r_k"] = rwkv_r_k.reshape(1, W_B)
    w["final_norm"] = final_norm.reshape(1, D_MODEL)

    loss_slab, grad_x, g = _local_step(x[0], loss_target[0], w)
    loss = lax.psum(loss_slab[0, 0], ("x", "y", "c"))

    grows = {
        "ffn1_w_gate": g["ffn1_wgt"], "ffn1_w_up": g["ffn1_wut"], "ffn1_w_down": g["ffn1_wd"],
        "ffn2_w_gate": g["ffn2_wgt"], "ffn2_w_up": g["ffn2_wut"], "ffn2_w_down": g["ffn2_wd"],
        "w_out": jnp.concatenate([g["w_out_a"], g["w_out_b"]], axis=0),
    }
    gfull = {
        "w_in": jnp.concatenate([g["w_in_h"], g["w_in_r"][:, :N_RWKV_COLS]], axis=1),
        "rwkv_w2": g["w2_pad"][0:32], "rwkv_a2": g["a2_pad"][32:64], "rwkv_g2": g["g2_pad"][64:160],
    }
    gsmall = {
        "ffn1_norm": g["ffn1_norm"], "mix_norm": g["mix_norm"],
        "hgrn_lb_logits": jnp.concatenate([g["lb0"], g["lb1"]], axis=0), "hgrn_out_norm": g["hgrn_out_norm"],
        "rwkv_shift_mu": g["mu_pad"][:, :N_RWKV_COLS], "rwkv_w0": g["rwkv_w0"], "rwkv_a0": g["rwkv_a0"],
        "rwkv_k_k": g["rwkv_k_k"], "rwkv_k_a": g["rwkv_k_a"], "rwkv_r_k": g["rwkv_r_k"],
        "rwkv_gn_w": g["rwkv_gn_w"], "rwkv_gn_b": g["rwkv_gn_b"], "ffn2_norm": g["ffn2_norm"],
        "final_norm": g["final_norm"],
    }
    gs = [grows[n].reshape(N_CHIPS, -1, LANES) for n in BIG]
    gs.append(jnp.stack([_pack({n: _quarter(gfull[n], n, q) for n in PACKED}, gsmall) for q in range(N_CHIPS)]))
    names = list(BIG) + ["packed"]
    c_idx = lax.axis_index("c").astype(jnp.int32).reshape(1)
    r1 = _sibling_exchange(gs)
    s4 = [_add_halves(gt, rt, c_idx, f"grad_add_halves_{n}") for gt, rt, n in zip(gs, r1, names)]
    r2 = _chip_exchange(s4)
    halves = [_sum_chips(rt, f"grad_sum_chips_{n}") for rt, n in zip(r2, names)]
    gq = [z.reshape(-1, LANES) for z in _sibling_allgather(halves)]

    def rows_list(d):
        return [_to_rows(n, d[n]) for n in BIG] + [_pack(d, {n: d[n] for n in SMALL})]

    outs = [_adamw(wt, gt, mt, vt, f"adamw_{n}")
            for wt, gt, mt, vt, n in zip(rows_list(wts), gq, rows_list(moms), rows_list(vars_), names)]
    results = []
    for k in range(4):
        per = [gq[i] if k == 0 else outs[i][k - 1] for i in range(len(names))]
        d = {n: _from_rows(n, z) for n, z in zip(BIG, per[:-1])}
        d.update(_unpack(per[-1]))
        results.append(d)
    return (loss, grad_x[None], *[r[n] for r in results for n in ALL_WEIGHTS])
```

```python
import functools

import jax
import jax.numpy as jnp
from jax import lax
from jax.experimental import pallas as pl
from jax.experimental.pallas import tpu as pltpu

F32 = jnp.float32
BF16 = jnp.bfloat16
SDS = jax.ShapeDtypeStruct
MESH = pl.DeviceIdType.MESH

D_MODEL = 1024
D_FF = 2816
W_A = 512
W_B = 512
HA_HEADS, HA_DIM = 4, 128
HB_HEADS, HB_DIM = 8, 64
HGRN_CHUNK = 64
RWKV_CHUNK = 16
RWKV_GROUP = 4
N_HGRN_COLS = 4 * W_A
N_RWKV_COLS = 3 * W_B + 32 + 32 + 96
N_RWKV_PAD = 1792
LORA_PAD = 256
NORM_EPS = 1e-6
RWKV_GN_EPS = 64e-5
L2_EPS = 1e-12
ADAM_LR, ADAM_B1, ADAM_B2, ADAM_EPS, ADAM_WD, ADAM_STEP = 0.001, 0.9, 0.999, 1e-8, 0.01, 10

N_CHIPS = 4
VMEM_LIMIT_V7X = 56 * 1024 * 1024
LANES = 1024

SHARDED_SHAPES = {
    "ffn1_w_gate": ((D_MODEL, D_FF), 1), "ffn1_w_up": ((D_MODEL, D_FF), 1), "ffn1_w_down": ((D_FF, D_MODEL), 0),
    "w_in": ((D_MODEL, N_HGRN_COLS + N_RWKV_COLS), 1), "rwkv_w2": ((32, W_B), 1), "rwkv_a2": ((32, W_B), 1),
    "rwkv_g2": ((96, W_B), 1), "w_out": ((D_MODEL, D_MODEL), 0),
    "ffn2_w_gate": ((D_MODEL, D_FF), 1), "ffn2_w_up": ((D_MODEL, D_FF), 1), "ffn2_w_down": ((D_FF, D_MODEL), 0),
}
SMALL = ("ffn1_norm", "mix_norm", "hgrn_lb_logits", "hgrn_out_norm", "rwkv_shift_mu", "rwkv_w0", "rwkv_a0",
         "rwkv_k_k", "rwkv_k_a", "rwkv_r_k", "rwkv_gn_w", "rwkv_gn_b", "ffn2_norm", "final_norm")
ALL_WEIGHTS = ("ffn1_norm", "ffn1_w_gate", "ffn1_w_up", "ffn1_w_down", "mix_norm", "w_in", "hgrn_lb_logits",
               "hgrn_out_norm", "rwkv_shift_mu", "rwkv_w0", "rwkv_w2", "rwkv_a0", "rwkv_a2", "rwkv_g2", "rwkv_k_k",
               "rwkv_k_a", "rwkv_r_k", "rwkv_gn_w", "rwkv_gn_b", "w_out", "ffn2_norm", "ffn2_w_gate", "ffn2_w_up",
               "ffn2_w_down", "final_norm")


def _shard_shape(name):
    shape, ax = SHARDED_SHAPES[name]
    return tuple(s // N_CHIPS if i == ax else s for i, s in enumerate(shape))


def _numel(shape):
    n = 1
    for s in shape:
        n *= s
    return n


def _params(sem=None):
    return pltpu.CompilerParams(dimension_semantics=sem, vmem_limit_bytes=VMEM_LIMIT_V7X)


def _dg(x, y, cx, cy, hi):
    dn = (((cx,), (cy,)), ((), ()))
    if hi:
        return lax.dot_general(x.astype(F32), y.astype(F32), dn, precision=lax.Precision.HIGHEST,
                               preferred_element_type=F32)
    return lax.dot_general(x.astype(BF16), y.astype(BF16), dn, preferred_element_type=F32)


def _make_mm(hi):
    @jax.custom_vjp
    def nn(x, y):
        return _dg(x, y, 1, 0, hi)

    @jax.custom_vjp
    def nt(x, y):
        return _dg(x, y, 1, 1, hi)

    @jax.custom_vjp
    def tn(x, y):
        return _dg(x, y, 0, 0, hi)

    nn.defvjp(lambda x, y: (nn(x, y), (x, y)), lambda r, g: (nt(g, r[1]), tn(r[0], g)))
    nt.defvjp(lambda x, y: (nt(x, y), (x, y)), lambda r, g: (nn(g, r[1]), tn(g, r[0])))
    tn.defvjp(lambda x, y: (tn(x, y), (x, y)), lambda r, g: (nt(r[1], g), nn(r[0], g)))
    return nn, nt, tn


_nn, _nt, _tn = _make_mm(False)
_nn_hi, _nt_hi, _tn_hi = _make_mm(True)


def _sigmoid(x):
    return 1.0 / (1.0 + jnp.exp(-x))


def _silu(x):
    return x * _sigmoid(x)


def _softplus(z):
    return jnp.maximum(z, 0.0) + jnp.log(1.0 + jnp.exp(-jnp.abs(z)))


def _mm(a, b, *, ta=False, tb=False, tm, tn, tk, name, out_dtype=F32, res=None, scale=None):
    m = a.shape[1] if ta else a.shape[0]
    kdim = a.shape[0] if ta else a.shape[1]
    n = b.shape[0] if tb else b.shape[1]
    assert (b.shape[1] if tb else b.shape[0]) == kdim
    tm, tn, tk = min(tm, m), min(tn, n), min(tk, kdim)
    assert m % tm == 0 and n % tn == 0 and kdim % tk == 0, (name, m, n, kdim)
    nk = kdim // tk
    a_spec = pl.BlockSpec((tk, tm), lambda i, j, k: (k, i)) if ta else pl.BlockSpec((tm, tk), lambda i, j, k: (i, k))
    b_spec = pl.BlockSpec((tn, tk), lambda i, j, k: (j, k)) if tb else pl.BlockSpec((tk, tn), lambda i, j, k: (k, j))
    o_spec = pl.BlockSpec((tm, tn), lambda i, j, k: (i, j))
    ca, cb = (0 if ta else 1), (1 if tb else 0)

    def body(*refs):
        if res is not None:
            a_ref, b_ref, r_ref, o_ref, acc_ref = refs
        else:
            a_ref, b_ref, o_ref, acc_ref = refs
        k = pl.program_id(2)

        @pl.when(k == 0)
        def _():
            acc_ref[...] = jnp.zeros_like(acc_ref)

        acc_ref[...] += _dg(a_ref[...], b_ref[...], ca, cb, False)

        @pl.when(k == nk - 1)
        def _():
            acc = acc_ref[...]
            if scale is not None:
                acc = acc * scale
            if res is not None:
                acc = r_ref[...] + acc
            o_ref[...] = acc.astype(out_dtype)

    in_specs = [a_spec, b_spec] + ([o_spec] if res is not None else [])
    args = (a, b) + ((res,) if res is not None else ())
    return pl.pallas_call(
        body, name=name, grid=(m // tm, n // tn, nk), in_specs=in_specs, out_specs=o_spec,
        out_shape=SDS((m, n), out_dtype), scratch_shapes=[pltpu.VMEM((tm, tn), F32)],
        compiler_params=_params(("parallel", "parallel", "arbitrary")))(*args)


def _row_spec(x, tm):
    if isinstance(x, tuple):
        arr, w, j = x
        return arr, pl.BlockSpec((tm, w), lambda i, j=j: (i, j))
    return x, pl.BlockSpec((tm, x.shape[1]), lambda i: (i, 0))


def _par_spec(p):
    if isinstance(p, tuple):
        arr, w, j = p
        return arr, pl.BlockSpec((arr.shape[0], w), lambda i, j=j: (0, j))
    return p, pl.BlockSpec(p.shape, lambda i: (0, 0))


def _store_groups(refs, groups, vals):
    for ref, idxs in zip(refs, groups):
        off = 0
        for ix in idxs:
            v = vals[ix]
            ref[:, off:off + v.shape[1]] = v.astype(ref.dtype)
            off += v.shape[1]


SUBLANES = 8


def _x_plan(xs, tm, t):
    arrays, specs, plan = [], [], []
    nb = tm // SUBLANES
    for x in xs:
        if isinstance(x, tuple) and isinstance(x[0], str):
            kind, arr, w, j = x
            if kind == "prev":
                halo = lambda i, j=j: (jnp.maximum(i * nb - 1, 0), j)
            else:
                halo = lambda i, j=j: (jnp.minimum((i + 1) * nb, t // SUBLANES - 1), j)
            arrays += [arr, arr]
            specs += [pl.BlockSpec((tm, w), lambda i, j=j: (i, j)), pl.BlockSpec((SUBLANES, w), halo)]
            plan.append((kind, 2, w))
        else:
            arr, spec = _row_spec(x, tm)
            arrays.append(arr)
            specs.append(spec)
            plan.append(("plain", 1, spec.block_shape[1]))
    return arrays, specs, plan


def _x_vals(refs, plan, tm, nt):
    vals, k = [], 0
    i = pl.program_id(0)
    rows = lax.broadcasted_iota(jnp.int32, (tm, 1), 0)
    for kind, n, _ in plan:
        main = refs[k][...].astype(F32)
        if kind == "prev":
            edge = jnp.where(i == 0, 0.0, refs[k + 1][SUBLANES - 1:SUBLANES, :].astype(F32))
            main = jnp.where(rows == 0, edge, pltpu.roll(main, 1, 0))
        elif kind == "next":
            edge = jnp.where(i == nt - 1, 0.0, refs[k + 1][0:1, :].astype(F32))
            main = jnp.where(rows == tm - 1, edge, pltpu.roll(main, tm - 1, 0))
        vals.append(main)
        k += n
    return vals


def _tile_rows(xs, tm):
    arr = xs[0]
    if isinstance(arr, tuple):
        arr = arr[1] if isinstance(arr[0], str) else arr[0]
    return min(tm, arr.shape[0]), arr.shape[0]


def _rowwise(f, xs, params, out_groups, out_dtypes, *, tm, name):
    tm, t = _tile_rows(xs, tm)
    nt = t // tm
    xa, xspecs, plan = _x_plan(xs, tm, t)
    pa, pspecs = (zip(*[_par_spec(p) for p in params]) if params else ((), ()))
    nxr, npar = len(xa), len(pa)
    x_sds = [SDS((tm, w), F32) for _, _, w in plan]
    p_sds = [SDS(s.block_shape, F32) for s in pspecs]
    outs_sds = jax.eval_shape(lambda *vals: f(*vals), *x_sds, *p_sds)
    widths = [sum(outs_sds[ix].shape[1] for ix in idxs) for idxs in out_groups]

    def body(*refs):
        vals = _x_vals(refs[:nxr], plan, tm, nt) + [r[...].astype(F32) for r in refs[nxr:nxr + npar]]
        outs = f(*vals)
        _store_groups(refs[nxr + npar:], out_groups, outs)

    return pl.pallas_call(
        body, name=name, grid=(nt,), in_specs=list(xspecs) + list(pspecs),
        out_specs=[pl.BlockSpec((tm, w), lambda i: (i, 0)) for w in widths],
        out_shape=[SDS((t, w), dt) for w, dt in zip(widths, out_dtypes)],
        compiler_params=_params(("parallel",)))(*xa, *pa)


def _rowwise_bwd(f, xs, params, cots, *, x_grad, p_grad, dx_groups, dx_dtypes, tm, name, extra=None):
    tm, t = _tile_rows(xs, tm)
    nt = t // tm
    xa, xspecs, plan = _x_plan(xs, tm, t)
    pa, pspecs = (zip(*[_par_spec(p) for p in params]) if params else ((), ()))
    ca, cspecs = zip(*[_row_spec(c, tm) for c in cots])
    extra = extra or {}
    ekeys = sorted(extra)
    ea, especs = (zip(*[_row_spec(extra[k], tm) for k in ekeys]) if ekeys else ((), ()))
    nx, nxr, npar, nc, ne = len(plan), len(xa), len(pa), len(ca), len(ea)
    gx = [i for i in range(nx) if x_grad[i]]
    gp = [i for i in range(npar) if p_grad[i]]
    widths = [sum(plan[gx[ix]][2] for ix in idxs) for idxs in dx_groups]
    ng = len(dx_groups)

    def body(*refs):
        ins = refs[:nxr + npar + nc + ne]
        outs = refs[nxr + npar + nc + ne:]
        vals = _x_vals(ins[:nxr], plan, tm, nt) + [r[...].astype(F32) for r in ins[nxr:nxr + npar]]
        cvals = tuple(r[...].astype(F32) for r in ins[nxr + npar:nxr + npar + nc])
        evals = [r[...].astype(F32) for r in ins[nxr + npar + nc:]]
        diff_idx = gx + [nx + i for i in gp]

        def g(*dargs):
            full = list(vals)
            for ix, v in zip(diff_idx, dargs):
                full[ix] = v
            return tuple(f(*full))

        _, vjp = jax.vjp(g, *[vals[ix] for ix in diff_idx])
        grads = vjp(cvals)
        dxs = list(grads[:len(gx)])
        for k, ev in zip(ekeys, evals):
            dxs[k] = dxs[k] + ev
        _store_groups(outs[:ng], dx_groups, dxs)
        i = pl.program_id(0)
        for ref, gval in zip(outs[ng:], grads[len(gx):]):
            @pl.when(i == 0)
            def _(ref=ref):
                ref[...] = jnp.zeros_like(ref)
            ref[...] += gval

    dp_specs = [pl.BlockSpec(pspecs[i].block_shape, lambda i: (0, 0)) for i in gp]
    dp_shapes = [SDS(pspecs[i].block_shape, F32) for i in gp]
    return pl.pallas_call(
        body, name=name, grid=(nt,), in_specs=list(xspecs) + list(pspecs) + list(cspecs) + list(especs),
        out_specs=[pl.BlockSpec((tm, w), lambda i: (i, 0)) for w in widths] + dp_specs,
        out_shape=[SDS((t, w), dt) for w, dt in zip(widths, dx_dtypes)] + dp_shapes,
        compiler_params=_params(("arbitrary",)))(*xa, *pa, *ca, *ea)


def _rms_f(x, g):
    return (x * lax.rsqrt(jnp.mean(x * x, axis=-1, keepdims=True) + NORM_EPS) * g,)


def _group_sum(x, ones_bd):
    return _nn_hi(x, ones_bd)


def _rwkv_prep_f(r, k, v, lo, rp, kp, vp, lop, mu_r, mu_k, mu_v, mu_lo, w0, w2p, a0, a2p, g2p, k_k, k_a, ones_bd):
    r = r + mu_r * (rp - r)
    k = k + mu_k * (kp - k)
    v = v + mu_v * (vp - v)
    lo = lo + mu_lo * (lop - lo)
    w_log = -_softplus(-(w0 + _nn(jnp.tanh(lo), w2p))) - 0.5
    lw = -jnp.exp(w_log)
    a_g = _sigmoid(a0 + _nn(lo, a2p))
    g = _nn(_sigmoid(lo), g2p)
    kk = k * k_k
    kk = kk / jnp.maximum(jnp.sqrt(_group_sum(kk * kk, ones_bd)), L2_EPS)
    k2 = k * (1.0 + (a_g - 1.0) * k_a)
    return r, lw, k2, v, -kk, kk * a_g, g


def _rwkv_post_f(y, r, k2, v, g, r_k, gn_w, gn_b, ones_bd):
    inv_n = 1.0 / HB_DIM
    mean = _group_sum(y, ones_bd) * inv_n
    yc = y - mean
    var = _group_sum(yc * yc, ones_bd) * inv_n
    yn = yc * lax.rsqrt(var + RWKV_GN_EPS) * gn_w + gn_b
    bonus = _group_sum(r * k2 * r_k, ones_bd) * v
    return ((yn + bonus) * g,)


def _tri(c, strict=False):
    ii = lax.broadcasted_iota(jnp.int32, (c, c), 0)
    jj = lax.broadcasted_iota(jnp.int32, (c, c), 1)
    return (jj < ii) if strict else (jj <= ii)


def _hgrn_head(st0, q_a, f_a, i_a, g_a, l0, l1, onorm):
    c = q_a.shape[0]
    mx = jnp.maximum(l0, l1)
    e0, e1 = jnp.exp(l0 - mx), jnp.exp(l1 - mx)
    lb = e0 / (e0 + e1)
    forget = lb + (1.0 - lb) * _sigmoid(f_a)
    q = _silu(q_a)
    kk = 1.0 - forget
    lf = jnp.log(forget)
    incl = _tri(c)
    bcum = _nn_hi(incl.astype(F32), lf)
    rows = lax.broadcasted_iota(jnp.int32, (c, 1), 0)
    bref = jnp.sum(jnp.where(rows <= c // 2, lf, 0.0), axis=0, keepdims=True)
    blast = jnp.sum(lf, axis=0, keepdims=True)
    scores = jnp.where(incl, _nt(q * jnp.exp(bcum - bref), kk * jnp.exp(bref - bcum)), 0.0)
    o = _nn(scores, i_a) + _nt(q * jnp.exp(bcum), st0)
    st1 = st0 * jnp.exp(blast) + _tn(i_a, kk * jnp.exp(blast - bcum))
    o = o * lax.rsqrt(jnp.mean(o * o, axis=-1, keepdims=True) + NORM_EPS)
    return o * onorm * _silu(g_a), st1


def _hgrn_fwd(p_h, l0, l1, onorm):
    t = p_h.shape[0]
    c, n = HGRN_CHUNK, p_h.shape[0] // HGRN_CHUNK

    def body(q_ref, f_ref, i_ref, g_ref, l0_ref, l1_ref, on_ref, o_ref, hs_ref, st_ref):
        @pl.when(pl.program_id(0) == 0)
        def _():
            st_ref[...] = jnp.zeros_like(st_ref)

        hs_ref[0] = st_ref[...]
        for h in range(HA_HEADS):
            sl = slice(h * HA_DIM, (h + 1) * HA_DIM)
            o, st1 = _hgrn_head(st_ref[h], q_ref[:, sl], f_ref[:, sl], i_ref[:, sl], g_ref[:, sl],
                                l0_ref[:, sl], l1_ref[:, sl], on_ref[:, sl])
            o_ref[:, sl] = o
            st_ref[h] = st1

    col = lambda j: pl.BlockSpec((c, W_A), lambda i, j=j: (i, j))
    par = pl.BlockSpec((1, W_A), lambda i: (0, 0))
    return pl.pallas_call(
        body, name="hgrn_fwd", grid=(n,), in_specs=[col(0), col(1), col(2), col(3), par, par, par],
        out_specs=[pl.BlockSpec((c, W_A), lambda i: (i, 0)),
                   pl.BlockSpec((1, HA_HEADS, HA_DIM, HA_DIM), lambda i: (i, 0, 0, 0))],
        out_shape=[SDS((t, W_A), F32), SDS((n, HA_HEADS, HA_DIM, HA_DIM), F32)],
        scratch_shapes=[pltpu.VMEM((HA_HEADS, HA_DIM, HA_DIM), F32)],
        compiler_params=_params(("arbitrary",)))(p_h, p_h, p_h, p_h, l0, l1, onorm)


def _hgrn_bwd(p_h, l0, l1, onorm, hs, do, do_col):
    t = p_h.shape[0]
    c, n = HGRN_CHUNK, p_h.shape[0] // HGRN_CHUNK

    def body(q_ref, f_ref, i_ref, g_ref, l0_ref, l1_ref, on_ref, hs_ref, do_ref,
             dp_ref, dl0_ref, dl1_ref, don_ref, dst_ref):
        @pl.when(pl.program_id(0) == 0)
        def _():
            dst_ref[...] = jnp.zeros_like(dst_ref)
            dl0_ref[...] = jnp.zeros_like(dl0_ref)
            dl1_ref[...] = jnp.zeros_like(dl1_ref)
            don_ref[...] = jnp.zeros_like(don_ref)

        for h in range(HA_HEADS):
            sl = slice(h * HA_DIM, (h + 1) * HA_DIM)
            args = (hs_ref[0, h], q_ref[:, sl], f_ref[:, sl], i_ref[:, sl], g_ref[:, sl],
                    l0_ref[:, sl], l1_ref[:, sl], on_ref[:, sl])
            _, vjp = jax.vjp(_hgrn_head, *args)
            dst0, dq, df, di, dg, dl0, dl1, don = vjp((do_ref[:, sl], dst_ref[h]))
            for j, dv in enumerate((dq, df, di, dg)):
                dp_ref[:, j * W_A + h * HA_DIM:j * W_A + (h + 1) * HA_DIM] = dv
            dl0_ref[:, sl] += dl0
            dl1_ref[:, sl] += dl1
            don_ref[:, sl] += don
            dst_ref[h] = dst0

    col = lambda j: pl.BlockSpec((c, W_A), lambda i, j=j: (n - 1 - i, j))
    par = pl.BlockSpec((1, W_A), lambda i: (0, 0))
    return pl.pallas_call(
        body, name="hgrn_bwd", grid=(n,),
        in_specs=[col(0), col(1), col(2), col(3), par, par, par,
                  pl.BlockSpec((1, HA_HEADS, HA_DIM, HA_DIM), lambda i: (n - 1 - i, 0, 0, 0)),
                  pl.BlockSpec((c, W_A), lambda i: (n - 1 - i, do_col))],
        out_specs=[pl.BlockSpec((c, N_HGRN_COLS), lambda i: (n - 1 - i, 0)), par, par, par],
        out_shape=[SDS((t, N_HGRN_COLS), F32), SDS((1, W_A), F32), SDS((1, W_A), F32), SDS((1, W_A), F32)],
        scratch_shapes=[pltpu.VMEM((HA_HEADS, HA_DIM, HA_DIM), F32)],
        compiler_params=_params(("arbitrary",)))(p_h, p_h, p_h, p_h, l0, l1, onorm, hs, do)


def _rwkv_step(s0, r, lw, k, v, a, b):
    nh, nj = len(r), len(r[0])
    c = r[0][0].shape[0]
    pairs = [(j, h) for j in range(nj) for h in range(nh)]
    every = lambda fn: {p: fn(p) for p in pairs}
    at_ = lambda d: (lambda p: d[p[1]][p[0]])
    r_, lw_, k_, v_, a_, b_ = (at_(z) for z in (r, lw, k, v, a, b))
    incl, strict = _tri(c), _tri(c, strict=True)
    incl_f = incl.astype(F32)
    eye = (lax.broadcasted_iota(jnp.int32, (c, c), 0) == lax.broadcasted_iota(jnp.int32, (c, c), 1)).astype(F32)

    gam = every(lambda p: _nn_hi(incl_f, lw_(p)))
    gtot = every(lambda p: jnp.sum(lw_(p), axis=0, keepdims=True))
    at = every(lambda p: a_(p) * jnp.exp(gam[p] - lw_(p)))
    rt = every(lambda p: r_(p) * jnp.exp(gam[p]))
    eneg = every(lambda p: jnp.exp(-gam[p]))
    bt = every(lambda p: b_(p) * eneg[p])
    kt = every(lambda p: k_(p) * eneg[p])
    edec = every(lambda p: jnp.exp(gtot[p] - gam[p]))
    bdec = every(lambda p: b_(p) * edec[p])
    kdec = every(lambda p: k_(p) * edec[p])
    a_ab = every(lambda p: jnp.where(strict, _nt(at[p], bt[p]), 0.0))
    a_ak = every(lambda p: jnp.where(strict, _nt(at[p], kt[p]), 0.0))
    a_rb = every(lambda p: jnp.where(incl, _nt(rt[p], bt[p]), 0.0))
    a_rk = every(lambda p: jnp.where(incl, _nt(rt[p], kt[p]), 0.0))
    tinv = every(lambda p: eye + a_ab[p])
    pw = a_ab
    span = 2
    while span < c:
        pw = every(lambda p, pw=pw: _nn_hi(pw[p], pw[p]))
        tinv = every(lambda p, pw=pw, tinv=tinv: tinv[p] + _nn_hi(pw[p], tinv[p]))
        span *= 2
    akv = every(lambda p: _nn(a_ak[p], v_(p)))
    w1 = every(lambda p: _nn_hi(tinv[p], at[p]))
    u0 = every(lambda p: _nn_hi(tinv[p], akv[p]))
    r1 = every(lambda p: rt[p] + _nn(a_rb[p], w1[p]))
    y0 = every(lambda p: _nn(a_rb[p], u0[p]) + _nn(a_rk[p], v_(p)))
    mm = every(lambda p: _tn(w1[p], bdec[p]))
    zz = every(lambda p: _tn(u0[p], bdec[p]) + _tn(v_(p), kdec[p]))
    gdec = every(lambda p: jnp.exp(gtot[p]))

    s = list(s0)
    y = [[None] * nj for _ in range(nh)]
    for j in range(nj):
        for h in range(nh):
            y[h][j] = _nt(r1[(j, h)], s[h]) + y0[(j, h)]
        s = [s[h] * gdec[(j, h)] + _nn(s[h], mm[(j, h)]) + zz[(j, h)] for h in range(nh)]
    return y, s


def _rwkv_blocks(ref, nj, c):
    return [[ref[h, j * c:(j + 1) * c, :] for j in range(nj)] for h in range(HB_HEADS)]


def _rwkv_fwd(seqs):
    t = seqs[0].shape[1]
    c, nj = RWKV_CHUNK, RWKV_GROUP
    n = t // (c * nj)

    def body(r_ref, lw_ref, k_ref, v_ref, a_ref, b_ref, y_ref, hs_ref, st_ref):
        @pl.when(pl.program_id(0) == 0)
        def _():
            st_ref[...] = jnp.zeros_like(st_ref)

        hs_ref[0] = st_ref[...]
        s0 = [st_ref[h] for h in range(HB_HEADS)]
        y, s1 = _rwkv_step(s0, *[_rwkv_blocks(ref, nj, c) for ref in (r_ref, lw_ref, k_ref, v_ref, a_ref, b_ref)])
        for h in range(HB_HEADS):
            for j in range(nj):
                y_ref[h, j * c:(j + 1) * c, :] = y[h][j]
            st_ref[h] = s1[h]

    seq = pl.BlockSpec((HB_HEADS, c * nj, HB_DIM), lambda i: (0, i, 0))
    return pl.pallas_call(
        body, name="rwkv_fwd", grid=(n,), in_specs=[seq] * 6,
        out_specs=[seq, pl.BlockSpec((1, HB_HEADS, HB_DIM, HB_DIM), lambda i: (i, 0, 0, 0))],
        out_shape=[SDS((HB_HEADS, t, HB_DIM), F32), SDS((n, HB_HEADS, HB_DIM, HB_DIM), F32)],
        scratch_shapes=[pltpu.VMEM((HB_HEADS, HB_DIM, HB_DIM), F32)],
        compiler_params=_params(("arbitrary",)))(*seqs)


def _rwkv_bwd(seqs, hs, dy):
    t = seqs[0].shape[1]
    c, nj = RWKV_CHUNK, RWKV_GROUP
    n = t // (c * nj)

    def body(r_ref, lw_ref, k_ref, v_ref, a_ref, b_ref, hs_ref, dy_ref,
             dr_ref, dlw_ref, dk_ref, dv_ref, da_ref, db_ref, dst_ref):
        @pl.when(pl.program_id(0) == 0)
        def _():
            dst_ref[...] = jnp.zeros_like(dst_ref)

        s0 = [hs_ref[0, h] for h in range(HB_HEADS)]
        seq_vals = [_rwkv_blocks(ref, nj, c) for ref in (r_ref, lw_ref, k_ref, v_ref, a_ref, b_ref)]
        _, vjp = jax.vjp(_rwkv_step, s0, *seq_vals)
        grads = vjp((_rwkv_blocks(dy_ref, nj, c), [dst_ref[h] for h in range(HB_HEADS)]))
        for ref, gr in zip((dr_ref, dlw_ref, dk_ref, dv_ref, da_ref, db_ref), grads[1:]):
            for h in range(HB_HEADS):
                for j in range(nj):
                    ref[h, j * c:(j + 1) * c, :] = gr[h][j]
        for h in range(HB_HEADS):
            dst_ref[h] = grads[0][h]

    seq = pl.BlockSpec((HB_HEADS, c * nj, HB_DIM), lambda i: (0, n - 1 - i, 0))
    return pl.pallas_call(
        body, name="rwkv_bwd", grid=(n,),
        in_specs=[seq] * 6 + [pl.BlockSpec((1, HB_HEADS, HB_DIM, HB_DIM), lambda i: (n - 1 - i, 0, 0, 0)), seq],
        out_specs=[seq] * 6, out_shape=[SDS((HB_HEADS, t, HB_DIM), F32)] * 6,
        scratch_shapes=[pltpu.VMEM((HB_HEADS, HB_DIM, HB_DIM), F32)],
        compiler_params=_params(("arbitrary",)))(*seqs, hs, dy)


def _final_loss(x3, fnorm, target, *, tm):
    t, d = x3.shape

    def body(x_ref, g_ref, t_ref, dx_ref, dg_ref, loss_ref):
        @pl.when(pl.program_id(0) == 0)
        def _():
            dg_ref[...] = jnp.zeros_like(dg_ref)
            loss_ref[...] = jnp.zeros_like(loss_ref)

        x, g = x_ref[...], g_ref[...]
        rinv = lax.rsqrt(jnp.mean(x * x, axis=-1, keepdims=True) + NORM_EPS)
        xh = x * rinv
        diff = xh * g - t_ref[...]
        loss_ref[...] += 0.5 * jnp.sum(jnp.mean(diff * diff, axis=-1, keepdims=True))
        dy = diff * (1.0 / d)
        dg_ref[...] += jnp.sum(dy * xh, axis=0, keepdims=True)
        dxh = dy * g
        dx_ref[...] = rinv * (dxh - xh * jnp.mean(dxh * xh, axis=-1, keepdims=True))

    row = pl.BlockSpec((tm, d), lambda i: (i, 0))
    return pl.pallas_call(
        body, name="final_loss", grid=(t // tm,), in_specs=[row, pl.BlockSpec((1, d), lambda i: (0, 0)), row],
        out_specs=[row, pl.BlockSpec((1, d), lambda i: (0, 0)), pl.BlockSpec((8, 128), lambda i: (0, 0))],
        out_shape=[SDS((t, d), F32), SDS((1, d), F32), SDS((8, 128), F32)],
        compiler_params=_params(("arbitrary",)))(x3, fnorm, target)


def _gate_up_act(h, wgt, wut, *, tm, tn, name):
    t, d = h.shape
    tm = min(tm, t)

    def body(h_ref, g_ref, u_ref, a_out, u_out, act_out):
        hv = h_ref[...]
        a = _dg(hv, g_ref[...], 1, 1, False)
        u = _dg(hv, u_ref[...], 1, 1, False)
        a_out[...] = a
        u_out[...] = u
        act_out[...] = (_silu(a) * u).astype(act_out.dtype)

    wspec = pl.BlockSpec((tn, d), lambda i, j: (j, 0))
    ospec = pl.BlockSpec((tm, tn), lambda i, j: (i, j))
    return pl.pallas_call(
        body, name=name, grid=(t // tm, D_FF // tn), in_specs=[pl.BlockSpec((tm, d), lambda i, j: (i, 0)), wspec, wspec],
        out_specs=[ospec, ospec, ospec], out_shape=[SDS((t, D_FF), F32), SDS((t, D_FF), F32), SDS((t, D_FF), BF16)],
        compiler_params=_params(("parallel", "parallel")))(h, wgt, wut)


def _dact_swiglu(dout, wd, a, u, *, tm, tn, name):
    t, d = dout.shape
    tm = min(tm, t)

    def body(d_ref, w_ref, a_ref, u_ref, da_out, du_out):
        dact = 0.5 * _dg(d_ref[...], w_ref[...], 1, 1, False)
        av, uv = a_ref[...], u_ref[...]
        s = _sigmoid(av)
        da_out[...] = (dact * uv * (s * (1.0 + av * (1.0 - s)))).astype(da_out.dtype)
        du_out[...] = (dact * (av * s)).astype(du_out.dtype)

    tile = pl.BlockSpec((tm, tn), lambda i, j: (i, j))
    return pl.pallas_call(
        body, name=name, grid=(t // tm, D_FF // tn),
        in_specs=[pl.BlockSpec((tm, d), lambda i, j: (i, 0)), pl.BlockSpec((tn, d), lambda i, j: (j, 0)), tile, tile],
        out_specs=[tile, tile], out_shape=[SDS((t, D_FF), BF16), SDS((t, D_FF), BF16)],
        compiler_params=_params(("parallel", "parallel")))(dout, wd, a, u)


def _ffn_fwd(x, norm, wgt, wut, wd, tag):
    h, = _rowwise(_rms_f, [x], [norm], [[0]], [BF16], tm=512, name=f"{tag}_rms")
    a, u, act = _gate_up_act(h, wgt, wut, tm=512, tn=D_FF // 2, name=f"{tag}_gate_up")
    out = _mm(act, wd, tm=512, tn=D_MODEL, tk=D_FF // 2, name=f"{tag}_down", res=x, scale=0.5)
    return out, (h, a, u, act)


def _ffn_bwd(dout, x, norm, wgt, wut, wd, saved, tag):
    h, a, u, act = saved
    da, du = _dact_swiglu(dout, wd, a, u, tm=512, tn=D_FF // 2, name=f"{tag}_dact")
    dwd = _mm(act, dout, ta=True, tm=D_FF // 2, tn=D_MODEL, tk=512, name=f"{tag}_dwd", scale=0.5)
    dwgt = _mm(da, h, ta=True, tm=D_FF // 2, tn=D_MODEL, tk=512, name=f"{tag}_dwg")
    dwut = _mm(du, h, ta=True, tm=D_FF // 2, tn=D_MODEL, tk=512, name=f"{tag}_dwu")
    dh = _mm(da, wgt, tm=512, tn=D_MODEL, tk=D_FF // 2, name=f"{tag}_dh_g")
    dh = _mm(du, wut, tm=512, tn=D_MODEL, tk=D_FF // 2, name=f"{tag}_dh_u", res=dh)
    dx, dnorm = _rowwise_bwd(_rms_f, [x], [norm], [dh], x_grad=[True], p_grad=[True], dx_groups=[[0]],
                             dx_dtypes=[F32], tm=256, name=f"{tag}_drms", extra={0: dout})
    return dx, dnorm, dwgt, dwut, dwd


def _to_heads(z):
    return z.reshape(z.shape[0], HB_HEADS, HB_DIM).transpose(1, 0, 2)


def _from_heads(z):
    return z.transpose(1, 0, 2).reshape(z.shape[1], W_B)


def _local_step(x, target, w):
    ones_bd = jnp.kron(jnp.eye(HB_HEADS, dtype=F32), jnp.ones((HB_DIM, HB_DIM), F32))
    g = {}
    x1, ffn1_saved = _ffn_fwd(x, w["ffn1_norm"], w["ffn1_wgt"], w["ffn1_wut"], w["ffn1_wd"], "ffn1")
    hm, = _rowwise(_rms_f, [x1], [w["mix_norm"]], [[0]], [BF16], tm=512, name="mix_rms")
    p_h = _mm(hm, w["w_in_h"], tm=512, tn=512, tk=D_MODEL, name="inproj_h")
    p_r = _mm(hm, w["w_in_r"], tm=512, tn=N_RWKV_PAD // 2, tk=D_MODEL, name="inproj_r")
    o_a, hgrn_states = _hgrn_fwd(p_h, w["lb0"], w["lb1"], w["hgrn_out_norm"])

    mu = w["mu_pad"]
    prep_xs = [(p_r, W_B, 0), (p_r, W_B, 1), (p_r, W_B, 2), (p_r, LORA_PAD, 6),
               ("prev", p_r, W_B, 0), ("prev", p_r, W_B, 1), ("prev", p_r, W_B, 2), ("prev", p_r, LORA_PAD, 6)]
    prep_ps = [(mu, W_B, 0), (mu, W_B, 1), (mu, W_B, 2), (mu, LORA_PAD, 6), w["rwkv_w0"], w["w2_pad"], w["rwkv_a0"],
               w["a2_pad"], w["g2_pad"], w["rwkv_k_k"], w["rwkv_k_a"], ones_bd]
    prep_f = _rwkv_prep_f
    r, lw, k2, v, a_vec, b_vec, gate = _rowwise(prep_f, prep_xs, prep_ps, [[0], [1], [2], [3], [4], [5], [6]],
                                                [F32] * 7, tm=256, name="rwkv_prep")
    seqs = [_to_heads(z) for z in (r, lw, k2, v, a_vec, b_vec)]
    y_h, rwkv_states = _rwkv_fwd(seqs)
    y = _from_heads(y_h)
    post_f = _rwkv_post_f
    post_xs = [y, r, k2, v, gate]
    post_ps = [w["rwkv_r_k"], w["rwkv_gn_w"], w["rwkv_gn_b"], ones_bd]
    o_b, = _rowwise(post_f, post_xs, post_ps, [[0]], [F32], tm=256, name="rwkv_post")
    x2 = _mm(o_a, w["w_out_a"], tm=512, tn=D_MODEL, tk=W_A, name="outproj_a", res=x1)
    x2 = _mm(o_b, w["w_out_b"], tm=512, tn=D_MODEL, tk=W_B, name="outproj_b", res=x2)
    x3, ffn2_saved = _ffn_fwd(x2, w["ffn2_norm"], w["ffn2_wgt"], w["ffn2_wut"], w["ffn2_wd"], "ffn2")
    dx3, g["final_norm"], loss = _final_loss(x3, w["final_norm"], target, tm=256)

    dx2, g["ffn2_norm"], g["ffn2_wgt"], g["ffn2_wut"], g["ffn2_wd"] = _ffn_bwd(
        dx3, x2, w["ffn2_norm"], w["ffn2_wgt"], w["ffn2_wut"], w["ffn2_wd"], ffn2_saved, "ffn2")
    do_a = _mm(dx2, w["w_out_a"], tb=True, tm=512, tn=W_A, tk=D_MODEL, name="outproj_do_a")
    do_b = _mm(dx2, w["w_out_b"], tb=True, tm=512, tn=W_B, tk=D_MODEL, name="outproj_do_b")
    g["w_out_a"] = _mm(o_a, dx2, ta=True, tm=W_A, tn=D_MODEL, tk=512, name="outproj_dw_a")
    g["w_out_b"] = _mm(o_b, dx2, ta=True, tm=W_B, tn=D_MODEL, tk=512, name="outproj_dw_b")

    dp_h, g["lb0"], g["lb1"], g["hgrn_out_norm"] = _hgrn_bwd(p_h, w["lb0"], w["lb1"], w["hgrn_out_norm"],
                                                             hgrn_states, do_a, 0)
    post_out = _rowwise_bwd(post_f, post_xs, post_ps, [do_b], x_grad=[True] * 5, p_grad=[True] * 3 + [False],
                            dx_groups=[[0], [1], [2], [3], [4]], dx_dtypes=[F32] * 5, tm=256, name="rwkv_post_bwd")
    dy, dr1, dk1, dv1, dgate, g["rwkv_r_k"], g["rwkv_gn_w"], g["rwkv_gn_b"] = post_out
    dseq = _rwkv_bwd(seqs, rwkv_states, _to_heads(dy))
    dr2, dlw, dk2, dv2, da_vec, db_vec = [_from_heads(z) for z in dseq]

    def prep2_f(*vals):
        r_, lw_, k2_, v_, a_, b_, g_ = prep_f(*vals)
        return r_, lw_, k2_, v_, a_, b_, g_, r_, k2_, v_

    prep_out = _rowwise_bwd(prep2_f, prep_xs, prep_ps, [dr2, dlw, dk2, dv2, da_vec, db_vec, dgate, dr1, dk1, dv1],
                            x_grad=[True] * 8, p_grad=[True] * 11 + [False], dx_groups=[[0, 1, 2, 3], [4, 5, 6, 7]],
                            dx_dtypes=[F32, F32], tm=256, name="rwkv_prep_bwd")
    dpr_main, dpr_prev = prep_out[0], prep_out[1]
    (dmu_r, dmu_k, dmu_v, dmu_lo, g["rwkv_w0"], g["w2_pad"], g["rwkv_a0"], g["a2_pad"], g["g2_pad"],
     g["rwkv_k_k"], g["rwkv_k_a"]) = prep_out[2:]
    g["mu_pad"] = jnp.concatenate([dmu_r, dmu_k, dmu_v, dmu_lo], axis=1)
    dp_r, = _rowwise(lambda u_, s_: (u_ + s_,), [dpr_main, ("next", dpr_prev, N_RWKV_PAD, 0)], [], [[0]], [F32],
                     tm=512, name="rwkv_dp_sum")
    dhm = _mm(dp_h, w["w_in_h"], tb=True, tm=512, tn=D_MODEL, tk=D_MODEL, name="inproj_dh_h")
    dhm = _mm(dp_r, w["w_in_r"], tb=True, tm=512, tn=D_MODEL, tk=N_RWKV_PAD // 2, name="inproj_dh_r", res=dhm)
    g["w_in_h"] = _mm(hm, dp_h, ta=True, tm=D_MODEL, tn=D_MODEL, tk=512, name="inproj_dw_h")
    g["w_in_r"] = _mm(hm, dp_r, ta=True, tm=D_MODEL, tn=N_RWKV_PAD // 2, tk=512, name="inproj_dw_r")
    dx1, g["mix_norm"] = _rowwise_bwd(_rms_f, [x1], [w["mix_norm"]], [dhm], x_grad=[True], p_grad=[True],
                                      dx_groups=[[0]], dx_dtypes=[F32], tm=256, name="mix_drms", extra={0: dx2})
    dx0, g["ffn1_norm"], g["ffn1_wgt"], g["ffn1_wut"], g["ffn1_wd"] = _ffn_bwd(
        dx1, x, w["ffn1_norm"], w["ffn1_wgt"], w["ffn1_wut"], w["ffn1_wd"], ffn1_saved, "ffn1")
    return loss, dx0, g


HBM_SPEC = pl.BlockSpec(memory_space=pl.ANY)


def _chips(x, y):
    return [(1 - x, y), (x, 1 - y), (1 - x, 1 - y)]


def _gather_weights(bufs):
    n = len(bufs)

    def body(*refs):
        outs = refs[n:2 * n]
        ici_send, ici_recv, d2d_send, d2d_recv = refs[2 * n:]
        x, y, c = lax.axis_index("x"), lax.axis_index("y"), lax.axis_index("c")
        me = 2 * x + y

        def half(t, slot, hc):
            hr = bufs[t].shape[1] // 2
            return outs[t].at[slot, pl.ds(pl.multiple_of(hc * hr, 16), hr), :]

        def ici(t, j, slot, px, py):
            return pltpu.make_async_remote_copy(src_ref=half(t, slot, c), dst_ref=half(t, slot, c),
                                                send_sem=ici_send.at[3 * t + j], recv_sem=ici_recv.at[3 * t + j],
                                                device_id=(px, py, c), device_id_type=MESH)

        def d2d(t, j, slot, hc):
            return pltpu.make_async_remote_copy(src_ref=half(t, slot, hc), dst_ref=half(t, slot, hc),
                                                send_sem=d2d_send.at[3 * t + j], recv_sem=d2d_recv.at[3 * t + j],
                                                device_id=(x, y, 1 - c), device_id_type=MESH)

        sends = [ici(t, j, me, px, py) for t in range(n) for j, (px, py) in enumerate(_chips(x, y))]
        for cp in sends:
            cp.start()
        passed = []
        for t in range(n):
            for j, (px, py) in enumerate(_chips(x, y)):
                ici(t, j, 2 * px + py, px, py).wait_recv()
                cp = d2d(t, j, 2 * px + py, c)
                cp.start()
                passed.append(cp)
        for t in range(n):
            for j, (px, py) in enumerate(_chips(x, y)):
                d2d(t, j, 2 * px + py, 1 - c).wait_recv()
        for cp in sends + passed:
            cp.wait_send()

    return pl.pallas_call(
        body, name="gather_weights", in_specs=[HBM_SPEC] * n, out_specs=[HBM_SPEC] * n,
        out_shape=[SDS(b.shape, b.dtype) for b in bufs], input_output_aliases={t: t for t in range(n)},
        scratch_shapes=[pltpu.SemaphoreType.DMA((3 * n,))] * 4,
    )(*bufs)


def _sibling_exchange(gs):
    n = len(gs)

    def body(*refs):
        ins, outs = refs[:n], refs[n:2 * n]
        send_sems, recv_sems = refs[2 * n:]
        x, y, c = lax.axis_index("x"), lax.axis_index("y"), lax.axis_index("c")
        cps = []
        for t in range(n):
            hr = gs[t].shape[1] // 2
            src = ins[t].at[:, pl.ds(pl.multiple_of((1 - c) * hr, SUBLANES), hr), :]
            cps.append(pltpu.make_async_remote_copy(src_ref=src, dst_ref=outs[t], send_sem=send_sems.at[t],
                                                    recv_sem=recv_sems.at[t], device_id=(x, y, 1 - c),
                                                    device_id_type=MESH))
        for cp in cps:
            cp.start()
        for cp in cps:
            cp.wait()

    return pl.pallas_call(
        body, name="grad_sibling_exchange", in_specs=[HBM_SPEC] * n, out_specs=[HBM_SPEC] * n,
        out_shape=[SDS((N_CHIPS, g.shape[1] // 2, g.shape[2]), g.dtype) for g in gs],
        scratch_shapes=[pltpu.SemaphoreType.DMA((n,)), pltpu.SemaphoreType.DMA((n,))],
    )(*gs)


def _chip_exchange(ss):
    n = len(ss)

    def body(*refs):
        ins, outs = refs[:n], refs[n:2 * n]
        send_sems, recv_sems = refs[2 * n:]
        x, y, c = lax.axis_index("x"), lax.axis_index("y"), lax.axis_index("c")
        me = 2 * x + y

        def copy(t, j, px, py, src_slot, dst_slot):
            return pltpu.make_async_remote_copy(src_ref=ins[t].at[src_slot], dst_ref=outs[t].at[dst_slot],
                                                send_sem=send_sems.at[3 * t + j], recv_sem=recv_sems.at[3 * t + j],
                                                device_id=(px, py, c), device_id_type=MESH)

        sends = [copy(t, j, px, py, 2 * px + py, me) for t in range(n) for j, (px, py) in enumerate(_chips(x, y))]
        for cp in sends:
            cp.start()
        for t in range(n):
            for j, (px, py) in enumerate(_chips(x, y)):
                copy(t, j, px, py, me, 2 * px + py).wait_recv()
        for cp in sends:
            cp.wait_send()

    return pl.pallas_call(
        body, name="grad_chip_exchange", in_specs=[HBM_SPEC] * n, out_specs=[HBM_SPEC] * n,
        out_shape=[SDS(s.shape, s.dtype) for s in ss],
        scratch_shapes=[pltpu.SemaphoreType.DMA((3 * n,)), pltpu.SemaphoreType.DMA((3 * n,))],
    )(*ss)


def _sibling_swap(fs):
    n = len(fs)

    def body(*refs):
        ins, outs = refs[:n], refs[n:2 * n]
        send_sems, recv_sems = refs[2 * n:]
        x, y, c = lax.axis_index("x"), lax.axis_index("y"), lax.axis_index("c")
        cps = [pltpu.make_async_remote_copy(src_ref=ins[t], dst_ref=outs[t], send_sem=send_sems.at[t],
                                            recv_sem=recv_sems.at[t], device_id=(x, y, 1 - c), device_id_type=MESH)
               for t in range(n)]
        for cp in cps:
            cp.start()
        for cp in cps:
            cp.wait()

    return pl.pallas_call(
        body, name="grad_sibling_swap", in_specs=[HBM_SPEC] * n, out_specs=[HBM_SPEC] * n,
        out_shape=[SDS(f.shape, f.dtype) for f in fs],
        scratch_shapes=[pltpu.SemaphoreType.DMA((n,)), pltpu.SemaphoreType.DMA((n,))],
    )(*fs)


def _row_tile(rows, cap=512):
    best = SUBLANES
    for tr in range(SUBLANES, min(rows, cap) + 1, SUBLANES):
        if rows % tr == 0:
            best = tr
    return best


def _add_halves(g4, r4, c_idx, name):
    _, hr, lanes = r4.shape
    tr = _row_tile(hr)
    nb = hr // tr

    def body(c_ref, a_ref, b_ref, o_ref):
        o_ref[...] = (a_ref[...] + b_ref[...]).astype(o_ref.dtype)

    grid_spec = pltpu.PrefetchScalarGridSpec(
        num_scalar_prefetch=1, grid=(N_CHIPS, nb),
        in_specs=[pl.BlockSpec((None, tr, lanes), lambda q, i, c_ref: (q, c_ref[0] * nb + i, 0)),
                  pl.BlockSpec((None, tr, lanes), lambda q, i, c_ref: (q, i, 0))],
        out_specs=pl.BlockSpec((None, tr, lanes), lambda q, i, c_ref: (q, i, 0)))
    return pl.pallas_call(body, name=name, grid_spec=grid_spec, out_shape=SDS(r4.shape, BF16),
                          compiler_params=_params(("parallel", "parallel")))(c_idx, g4, r4)


def _sum_chips(r4, s4, me_idx, name):
    _, rows, lanes = r4.shape
    tr = _row_tile(rows)

    def body(me_ref, a_ref, b_ref, c_ref, d_ref, own_ref, o_ref):
        own = own_ref[...].astype(F32)
        p = [jnp.where(me_ref[0] == q, own, ref[...].astype(F32)) for q, ref in enumerate((a_ref, b_ref, c_ref, d_ref))]
        o_ref[...] = ((p[0] + p[1]) + p[2]) + p[3]

    other = lambda q: (lambda i, me_ref: (jnp.where(me_ref[0] == q, (q + 1) % N_CHIPS, q), i, 0))
    grid_spec = pltpu.PrefetchScalarGridSpec(
        num_scalar_prefetch=1, grid=(rows // tr,),
        in_specs=[pl.BlockSpec((None, tr, lanes), other(q)) for q in range(N_CHIPS)]
        + [pl.BlockSpec((None, tr, lanes), lambda i, me_ref: (me_ref[0], i, 0))],
        out_specs=pl.BlockSpec((tr, lanes), lambda i, me_ref: (i, 0)))
    return pl.pallas_call(body, name=name, grid_spec=grid_spec, out_shape=SDS((rows, lanes), F32),
                          compiler_params=_params(("parallel",)))(me_idx, r4, r4, r4, r4, s4)


def _adamw(wf, g_own, g_other, mf, vf, c_idx, name):
    rows, lanes = wf.shape
    hr = rows // 2
    tr = _row_tile(hr)
    nb = hr // tr
    c1 = 1.0 / (1.0 - ADAM_B1 ** ADAM_STEP)
    c2 = 1.0 / (1.0 - ADAM_B2 ** ADAM_STEP)

    def body(c_ref, w_ref, go_ref, gx_ref, m_ref, v_ref, g_ref, d_ref, nm_ref, nv_ref):
        gv = jnp.where(pl.program_id(0) == c_ref[0], go_ref[...], gx_ref[...])
        m = ADAM_B1 * m_ref[...] + (1.0 - ADAM_B1) * gv
        v = ADAM_B2 * v_ref[...] + (1.0 - ADAM_B2) * (gv * gv)
        g_ref[...] = gv
        d_ref[...] = -ADAM_LR * ((m * c1) / (jnp.sqrt(v * c2) + ADAM_EPS) + ADAM_WD * w_ref[...])
        nm_ref[...] = m
        nv_ref[...] = v

    full = pl.BlockSpec((tr, lanes), lambda h, i, c_ref: (h * nb + i, 0))
    half = pl.BlockSpec((tr, lanes), lambda h, i, c_ref: (i, 0))
    grid_spec = pltpu.PrefetchScalarGridSpec(num_scalar_prefetch=1, grid=(2, nb),
                                             in_specs=[full, half, half, full, full], out_specs=[full] * 4)
    return pl.pallas_call(body, name=name, grid_spec=grid_spec, out_shape=[SDS((rows, lanes), F32)] * 4,
                          compiler_params=_params(("parallel", "parallel")))(c_idx, wf, g_own, g_other, mf, vf)


BIG = ("ffn1_w_gate", "ffn1_w_up", "ffn1_w_down", "ffn2_w_gate", "ffn2_w_up", "ffn2_w_down", "w_out")
TRANSPOSED = ("ffn1_w_gate", "ffn1_w_up", "ffn2_w_gate", "ffn2_w_up")
PACKED = ("w_in", "rwkv_w2", "rwkv_a2", "rwkv_g2")
SMALL_SHAPES = {"ffn1_norm": (1, D_MODEL), "mix_norm": (1, D_MODEL), "hgrn_lb_logits": (2, W_A),
                "hgrn_out_norm": (1, W_A), "rwkv_shift_mu": (1, N_RWKV_COLS), "rwkv_w0": (1, W_B),
                "rwkv_a0": (1, W_B), "rwkv_k_k": (1, W_B), "rwkv_k_a": (1, W_B),
                "rwkv_r_k": (1, HB_HEADS, HB_DIM), "rwkv_gn_w": (1, W_B), "rwkv_gn_b": (1, W_B),
                "ffn2_norm": (1, D_MODEL), "final_norm": (D_MODEL,)}
PACK_ELEMS = sum(_numel(_shard_shape(n)) for n in PACKED) + sum(_numel(SMALL_SHAPES[n]) for n in SMALL)
PACK_ROWS = -(-PACK_ELEMS // (32 * LANES)) * 32


def _to_rows(name, shard):
    return shard[0].T if name in TRANSPOSED else shard[0]


def _from_rows(name, rows):
    return (rows.T if name in TRANSPOSED else rows)[None]


def _pack(sharded, small):
    flat = jnp.concatenate([sharded[n].reshape(-1) for n in PACKED] + [small[n].reshape(-1) for n in SMALL])
    return jnp.pad(flat, (0, PACK_ROWS * LANES - flat.shape[0])).reshape(PACK_ROWS, LANES)


def _unpack(packed):
    flat, out, off = packed.reshape(-1), {}, 0
    for n in PACKED:
        shp = _shard_shape(n)
        out[n] = flat[off:off + _numel(shp)].reshape((1,) + shp)
        off += _numel(shp)
    for n in SMALL:
        shp = SMALL_SHAPES[n]
        out[n] = flat[off:off + _numel(shp)].reshape(shp)
        off += _numel(shp)
    return out


def _quarter(full, name, q):
    shape, ax = SHARDED_SHAPES[name]
    w = shape[ax] // N_CHIPS
    return lax.slice_in_dim(full, q * w, (q + 1) * w, axis=ax)


def kernel(x, ffn1_norm, ffn1_w_gate, ffn1_w_up, ffn1_w_down, mix_norm, w_in, hgrn_lb_logits, hgrn_out_norm, rwkv_shift_mu, rwkv_w0, rwkv_w2, rwkv_a0, rwkv_a2, rwkv_g2, rwkv_k_k, rwkv_k_a, rwkv_r_k, rwkv_gn_w, rwkv_gn_b, w_out, ffn2_norm, ffn2_w_gate, ffn2_w_up, ffn2_w_down, final_norm, loss_target, m_ffn1_norm, m_ffn1_w_gate, m_ffn1_w_up, m_ffn1_w_down, m_mix_norm, m_w_in, m_hgrn_lb_logits, m_hgrn_out_norm, m_rwkv_shift_mu, m_rwkv_w0, m_rwkv_w2, m_rwkv_a0, m_rwkv_a2, m_rwkv_g2, m_rwkv_k_k, m_rwkv_k_a, m_rwkv_r_k, m_rwkv_gn_w, m_rwkv_gn_b, m_w_out, m_ffn2_norm, m_ffn2_w_gate, m_ffn2_w_up, m_ffn2_w_down, m_final_norm, v_ffn1_norm, v_ffn1_w_gate, v_ffn1_w_up, v_ffn1_w_down, v_mix_norm, v_w_in, v_hgrn_lb_logits, v_hgrn_out_norm, v_rwkv_shift_mu, v_rwkv_w0, v_rwkv_w2, v_rwkv_a0, v_rwkv_a2, v_rwkv_g2, v_rwkv_k_k, v_rwkv_k_a, v_rwkv_r_k, v_rwkv_gn_w, v_rwkv_gn_b, v_w_out, v_ffn2_norm, v_ffn2_w_gate, v_ffn2_w_up, v_ffn2_w_down, v_final_norm):
    args = dict(locals())
    wts = {n: args[n] for n in ALL_WEIGHTS}
    moms = {n: args["m_" + n] for n in ALL_WEIGHTS}
    vars_ = {n: args["v_" + n] for n in ALL_WEIGHTS}

    small_w = {n: wts[n] for n in SMALL}
    shards = [_to_rows(n, wts[n]).astype(BF16) for n in BIG] + [_pack(wts, small_w).astype(BF16)]
    me = 2 * lax.axis_index("x") + lax.axis_index("y")
    gathered = _gather_weights([lax.dynamic_update_slice(jnp.zeros((N_CHIPS,) + s.shape, BF16), s[None], (me, 0, 0))
                                for s in shards])
    rows_of = dict(zip(BIG, gathered[:-1]))
    packs = gathered[-1].reshape(N_CHIPS, PACK_ROWS * LANES)
    full, off = {}, 0
    for n in PACKED:
        shp = _shard_shape(n)
        full[n] = jnp.concatenate([packs[q, off:off + _numel(shp)].reshape(shp) for q in range(N_CHIPS)], axis=1)
        off += _numel(shp)

    w = {}
    for tag in ("ffn1", "ffn2"):
        w[f"{tag}_wgt"] = rows_of[f"{tag}_w_gate"].reshape(D_FF, D_MODEL)
        w[f"{tag}_wut"] = rows_of[f"{tag}_w_up"].reshape(D_FF, D_MODEL)
        w[f"{tag}_wd"] = rows_of[f"{tag}_w_down"].reshape(D_FF, D_MODEL)
        w[f"{tag}_norm"] = wts[f"{tag}_norm"]
    w["w_in_h"] = full["w_in"][:, :N_HGRN_COLS]
    w["w_in_r"] = jnp.pad(full["w_in"][:, N_HGRN_COLS:], ((0, 0), (0, N_RWKV_PAD - N_RWKV_COLS)))
    w_out_full = rows_of["w_out"].reshape(D_MODEL, D_MODEL)
    w["w_out_a"], w["w_out_b"] = w_out_full[:W_A], w_out_full[W_A:]
    zrow = lambda nrow: jnp.zeros((nrow, W_B), BF16)
    w["w2_pad"] = jnp.concatenate([full["rwkv_w2"], zrow(LORA_PAD - 32)], axis=0)
    w["a2_pad"] = jnp.concatenate([zrow(32), full["rwkv_a2"], zrow(LORA_PAD - 64)], axis=0)
    w["g2_pad"] = jnp.concatenate([zrow(64), full["rwkv_g2"], zrow(LORA_PAD - 160)], axis=0)
    w["mix_norm"] = mix_norm
    w["lb0"], w["lb1"] = hgrn_lb_logits[0:1], hgrn_lb_logits[1:2]
    w["hgrn_out_norm"] = hgrn_out_norm
    w["mu_pad"] = jnp.pad(rwkv_shift_mu, ((0, 0), (0, N_RWKV_PAD - N_RWKV_COLS)))
    for n in ("rwkv_w0", "rwkv_a0", "rwkv_k_k", "rwkv_k_a", "rwkv_gn_w", "rwkv_gn_b"):
        w[n] = wts[n]
    w["rwkv_r_k"] = rwkv_r_k.reshape(1, W_B)
    w["final_norm"] = final_norm.reshape(1, D_MODEL)

    loss_slab, grad_x, g = _local_step(x[0], loss_target[0], w)
    loss = lax.psum(loss_slab[0, 0], ("x", "y", "c"))

    grows = {
        "ffn1_w_gate": g["ffn1_wgt"], "ffn1_w_up": g["ffn1_wut"], "ffn1_w_down": g["ffn1_wd"],
        "ffn2_w_gate": g["ffn2_wgt"], "ffn2_w_up": g["ffn2_wut"], "ffn2_w_down": g["ffn2_wd"],
        "w_out": jnp.concatenate([g["w_out_a"], g["w_out_b"]], axis=0),
    }
    gfull = {
        "w_in": jnp.concatenate([g["w_in_h"], g["w_in_r"][:, :N_RWKV_COLS]], axis=1),
        "rwkv_w2": g["w2_pad"][0:32], "rwkv_a2": g["a2_pad"][32:64], "rwkv_g2": g["g2_pad"][64:160],
    }
    gsmall = {
        "ffn1_norm": g["ffn1_norm"], "mix_norm": g["mix_norm"],
        "hgrn_lb_logits": jnp.concatenate([g["lb0"], g["lb1"]], axis=0), "hgrn_out_norm": g["hgrn_out_norm"],
        "rwkv_shift_mu": g["mu_pad"][:, :N_RWKV_COLS], "rwkv_w0": g["rwkv_w0"], "rwkv_a0": g["rwkv_a0"],
        "rwkv_k_k": g["rwkv_k_k"], "rwkv_k_a": g["rwkv_k_a"], "rwkv_r_k": g["rwkv_r_k"],
        "rwkv_gn_w": g["rwkv_gn_w"], "rwkv_gn_b": g["rwkv_gn_b"], "ffn2_norm": g["ffn2_norm"],
        "final_norm": g["final_norm"],
    }
    gs = [grows[n].reshape(N_CHIPS, -1, LANES) for n in BIG]
    gs.append(jnp.stack([_pack({n: _quarter(gfull[n], n, q) for n in PACKED}, gsmall) for q in range(N_CHIPS)]))
    names = list(BIG) + ["packed"]
    c_idx = lax.axis_index("c").astype(jnp.int32).reshape(1)
    me_idx = me.astype(jnp.int32).reshape(1)
    r1 = _sibling_exchange(gs)
    s4 = [_add_halves(gt, rt, c_idx, f"grad_add_halves_{n}") for gt, rt, n in zip(gs, r1, names)]
    r2 = _chip_exchange(s4)
    own = [_sum_chips(rt, st, me_idx, f"grad_sum_chips_{n}") for rt, st, n in zip(r2, s4, names)]
    other = _sibling_swap(own)

    def rows_list(d):
        return [_to_rows(n, d[n]) for n in BIG] + [_pack(d, {n: d[n] for n in SMALL})]

    outs = [_adamw(wt, go, gx, mt, vt, c_idx, f"adamw_{n}")
            for wt, go, gx, mt, vt, n in zip(rows_list(wts), own, other, rows_list(moms), rows_list(vars_), names)]
    results = []
    for k in range(4):
        per = [outs[i][k] for i in range(len(names))]
        d = {n: _from_rows(n, z) for n, z in zip(BIG, per[:-1])}
        d.update(_unpack(per[-1]))
        results.append(d)
    return (loss, grad_x[None], *[r[n] for r in results for n in ALL_WEIGHTS])
```

```python
import functools

import jax
import jax.numpy as jnp
from jax import lax
from jax.experimental import pallas as pl
from jax.experimental.pallas import tpu as pltpu

F32 = jnp.float32
BF16 = jnp.bfloat16
SDS = jax.ShapeDtypeStruct
MESH = pl.DeviceIdType.MESH

D_MODEL = 1024
D_FF = 2816
W_A = 512
W_B = 512
HA_HEADS, HA_DIM = 4, 128
HB_HEADS, HB_DIM = 8, 64
HGRN_CHUNK = 64
RWKV_CHUNK = 16
RWKV_GROUP = 4
N_HGRN_COLS = 4 * W_A
N_RWKV_COLS = 3 * W_B + 32 + 32 + 96
N_RWKV_PAD = 1792
LORA_PAD = 256
NORM_EPS = 1e-6
RWKV_GN_EPS = 64e-5
L2_EPS = 1e-12
ADAM_LR, ADAM_B1, ADAM_B2, ADAM_EPS, ADAM_WD, ADAM_STEP = 0.001, 0.9, 0.999, 1e-8, 0.01, 10

N_CHIPS = 4
VMEM_LIMIT_V7X = 56 * 1024 * 1024
LANES = 1024

SHARDED_SHAPES = {
    "ffn1_w_gate": ((D_MODEL, D_FF), 1), "ffn1_w_up": ((D_MODEL, D_FF), 1), "ffn1_w_down": ((D_FF, D_MODEL), 0),
    "w_in": ((D_MODEL, N_HGRN_COLS + N_RWKV_COLS), 1), "rwkv_w2": ((32, W_B), 1), "rwkv_a2": ((32, W_B), 1),
    "rwkv_g2": ((96, W_B), 1), "w_out": ((D_MODEL, D_MODEL), 0),
    "ffn2_w_gate": ((D_MODEL, D_FF), 1), "ffn2_w_up": ((D_MODEL, D_FF), 1), "ffn2_w_down": ((D_FF, D_MODEL), 0),
}
SMALL = ("ffn1_norm", "mix_norm", "hgrn_lb_logits", "hgrn_out_norm", "rwkv_shift_mu", "rwkv_w0", "rwkv_a0",
         "rwkv_k_k", "rwkv_k_a", "rwkv_r_k", "rwkv_gn_w", "rwkv_gn_b", "ffn2_norm", "final_norm")
ALL_WEIGHTS = ("ffn1_norm", "ffn1_w_gate", "ffn1_w_up", "ffn1_w_down", "mix_norm", "w_in", "hgrn_lb_logits",
               "hgrn_out_norm", "rwkv_shift_mu", "rwkv_w0", "rwkv_w2", "rwkv_a0", "rwkv_a2", "rwkv_g2", "rwkv_k_k",
               "rwkv_k_a", "rwkv_r_k", "rwkv_gn_w", "rwkv_gn_b", "w_out", "ffn2_norm", "ffn2_w_gate", "ffn2_w_up",
               "ffn2_w_down", "final_norm")


def _shard_shape(name):
    shape, ax = SHARDED_SHAPES[name]
    return tuple(s // N_CHIPS if i == ax else s for i, s in enumerate(shape))


def _numel(shape):
    n = 1
    for s in shape:
        n *= s
    return n


def _params(sem=None):
    return pltpu.CompilerParams(dimension_semantics=sem, vmem_limit_bytes=VMEM_LIMIT_V7X)


def _split2(x):
    hi = x.astype(BF16)
    return hi, (x.astype(F32) - hi.astype(F32)).astype(BF16)


def _dg(x, y, cx, cy, hi):
    dn = (((cx,), (cy,)), ((), ()))
    if hi is True:
        return lax.dot_general(x.astype(F32), y.astype(F32), dn, precision=lax.Precision.HIGHEST,
                               preferred_element_type=F32)
    dot = lambda p, q: lax.dot_general(p, q, dn, preferred_element_type=F32)
    if hi == "x3":
        (xh, xl), (yh, yl) = _split2(x), _split2(y)
        return dot(xh, yh) + (dot(xh, yl) + dot(xl, yh))
    return dot(x.astype(BF16), y.astype(BF16))


def _make_mm(hi):
    @jax.custom_vjp
    def nn(x, y):
        return _dg(x, y, 1, 0, hi)

    @jax.custom_vjp
    def nt(x, y):
        return _dg(x, y, 1, 1, hi)

    @jax.custom_vjp
    def tn(x, y):
        return _dg(x, y, 0, 0, hi)

    nn.defvjp(lambda x, y: (nn(x, y), (x, y)), lambda r, g: (nt(g, r[1]), tn(r[0], g)))
    nt.defvjp(lambda x, y: (nt(x, y), (x, y)), lambda r, g: (nn(g, r[1]), tn(g, r[0])))
    tn.defvjp(lambda x, y: (tn(x, y), (x, y)), lambda r, g: (nt(r[1], g), nn(r[0], g)))
    return nn, nt, tn


_nn, _nt, _tn = _make_mm(False)
_nn_hi, _nt_hi, _tn_hi = _make_mm(True)
_nn_x3, _nt_x3, _tn_x3 = _make_mm("x3")


def _tri_apply(x, transpose):
    c = x.shape[0]
    tri = (lax.broadcasted_iota(jnp.int32, (c, c), 1) <= lax.broadcasted_iota(jnp.int32, (c, c), 0)).astype(BF16)
    dn = (((0 if transpose else 1,), (0,)), ((), ()))
    p1 = x.astype(BF16)
    r1 = x - p1.astype(F32)
    p2 = r1.astype(BF16)
    p3 = (r1 - p2.astype(F32)).astype(BF16)
    dot = lambda p: lax.dot_general(tri, p, dn, preferred_element_type=F32)
    return dot(p1) + (dot(p2) + dot(p3))


@jax.custom_vjp
def _cumsum_rows(x):
    return _tri_apply(x, False)


_cumsum_rows.defvjp(lambda x: (_tri_apply(x, False), None), lambda _, g: (_tri_apply(g, True),))


def _sigmoid(x):
    return 1.0 / (1.0 + jnp.exp(-x))


def _silu(x):
    return x * _sigmoid(x)


def _softplus(z):
    return jnp.maximum(z, 0.0) + jnp.log(1.0 + jnp.exp(-jnp.abs(z)))


def _mm(a, b, *, ta=False, tb=False, tm, tn, tk, name, out_dtype=F32, res=None, scale=None):
    m = a.shape[1] if ta else a.shape[0]
    kdim = a.shape[0] if ta else a.shape[1]
    n = b.shape[0] if tb else b.shape[1]
    assert (b.shape[1] if tb else b.shape[0]) == kdim
    tm, tn, tk = min(tm, m), min(tn, n), min(tk, kdim)
    assert m % tm == 0 and n % tn == 0 and kdim % tk == 0, (name, m, n, kdim)
    nk = kdim // tk
    a_spec = pl.BlockSpec((tk, tm), lambda i, j, k: (k, i)) if ta else pl.BlockSpec((tm, tk), lambda i, j, k: (i, k))
    b_spec = pl.BlockSpec((tn, tk), lambda i, j, k: (j, k)) if tb else pl.BlockSpec((tk, tn), lambda i, j, k: (k, j))
    o_spec = pl.BlockSpec((tm, tn), lambda i, j, k: (i, j))
    ca, cb = (0 if ta else 1), (1 if tb else 0)

    def body(*refs):
        if res is not None:
            a_ref, b_ref, r_ref, o_ref, acc_ref = refs
        else:
            a_ref, b_ref, o_ref, acc_ref = refs
        k = pl.program_id(2)

        @pl.when(k == 0)
        def _():
            acc_ref[...] = jnp.zeros_like(acc_ref)

        acc_ref[...] += _dg(a_ref[...], b_ref[...], ca, cb, False)

        @pl.when(k == nk - 1)
        def _():
            acc = acc_ref[...]
            if scale is not None:
                acc = acc * scale
            if res is not None:
                acc = r_ref[...] + acc
            o_ref[...] = acc.astype(out_dtype)

    in_specs = [a_spec, b_spec] + ([o_spec] if res is not None else [])
    args = (a, b) + ((res,) if res is not None else ())
    return pl.pallas_call(
        body, name=name, grid=(m // tm, n // tn, nk), in_specs=in_specs, out_specs=o_spec,
        out_shape=SDS((m, n), out_dtype), scratch_shapes=[pltpu.VMEM((tm, tn), F32)],
        compiler_params=_params(("parallel", "parallel", "arbitrary")))(*args)


def _row_spec(x, tm):
    if isinstance(x, tuple):
        arr, w, j = x
        return arr, pl.BlockSpec((tm, w), lambda i, j=j: (i, j))
    return x, pl.BlockSpec((tm, x.shape[1]), lambda i: (i, 0))


def _par_spec(p):
    if isinstance(p, tuple):
        arr, w, j = p
        return arr, pl.BlockSpec((arr.shape[0], w), lambda i, j=j: (0, j))
    return p, pl.BlockSpec(p.shape, lambda i: (0, 0))


def _store_groups(refs, groups, vals):
    for ref, idxs in zip(refs, groups):
        off = 0
        for ix in idxs:
            v = vals[ix]
            ref[:, off:off + v.shape[1]] = v.astype(ref.dtype)
            off += v.shape[1]


SUBLANES = 8


def _x_plan(xs, tm, t):
    arrays, specs, plan = [], [], []
    nb = tm // SUBLANES
    for x in xs:
        if isinstance(x, tuple) and isinstance(x[0], str):
            kind, arr, w, j = x
            if kind == "prev":
                halo = lambda i, j=j: (jnp.maximum(i * nb - 1, 0), j)
            else:
                halo = lambda i, j=j: (jnp.minimum((i + 1) * nb, t // SUBLANES - 1), j)
            arrays += [arr, arr]
            specs += [pl.BlockSpec((tm, w), lambda i, j=j: (i, j)), pl.BlockSpec((SUBLANES, w), halo)]
            plan.append((kind, 2, w))
        else:
            arr, spec = _row_spec(x, tm)
            arrays.append(arr)
            specs.append(spec)
            plan.append(("plain", 1, spec.block_shape[1]))
    return arrays, specs, plan


def _x_vals(refs, plan, tm, nt):
    vals, k = [], 0
    i = pl.program_id(0)
    rows = lax.broadcasted_iota(jnp.int32, (tm, 1), 0)
    for kind, n, _ in plan:
        main = refs[k][...].astype(F32)
        if kind == "prev":
            edge = jnp.where(i == 0, 0.0, refs[k + 1][SUBLANES - 1:SUBLANES, :].astype(F32))
            main = jnp.where(rows == 0, edge, pltpu.roll(main, 1, 0))
        elif kind == "next":
            edge = jnp.where(i == nt - 1, 0.0, refs[k + 1][0:1, :].astype(F32))
            main = jnp.where(rows == tm - 1, edge, pltpu.roll(main, tm - 1, 0))
        vals.append(main)
        k += n
    return vals


def _tile_rows(xs, tm):
    arr = xs[0]
    if isinstance(arr, tuple):
        arr = arr[1] if isinstance(arr[0], str) else arr[0]
    return min(tm, arr.shape[0]), arr.shape[0]


def _rowwise(f, xs, params, out_groups, out_dtypes, *, tm, name):
    tm, t = _tile_rows(xs, tm)
    nt = t // tm
    xa, xspecs, plan = _x_plan(xs, tm, t)
    pa, pspecs = (zip(*[_par_spec(p) for p in params]) if params else ((), ()))
    nxr, npar = len(xa), len(pa)
    x_sds = [SDS((tm, w), F32) for _, _, w in plan]
    p_sds = [SDS(s.block_shape, F32) for s in pspecs]
    outs_sds = jax.eval_shape(lambda *vals: f(*vals), *x_sds, *p_sds)
    widths = [sum(outs_sds[ix].shape[1] for ix in idxs) for idxs in out_groups]

    def body(*refs):
        vals = _x_vals(refs[:nxr], plan, tm, nt) + [r[...].astype(F32) for r in refs[nxr:nxr + npar]]
        outs = f(*vals)
        _store_groups(refs[nxr + npar:], out_groups, outs)

    return pl.pallas_call(
        body, name=name, grid=(nt,), in_specs=list(xspecs) + list(pspecs),
        out_specs=[pl.BlockSpec((tm, w), lambda i: (i, 0)) for w in widths],
        out_shape=[SDS((t, w), dt) for w, dt in zip(widths, out_dtypes)],
        compiler_params=_params(("parallel",)))(*xa, *pa)


def _rowwise_bwd(f, xs, params, cots, *, x_grad, p_grad, dx_groups, dx_dtypes, tm, name, extra=None):
    tm, t = _tile_rows(xs, tm)
    nt = t // tm
    xa, xspecs, plan = _x_plan(xs, tm, t)
    pa, pspecs = (zip(*[_par_spec(p) for p in params]) if params else ((), ()))
    ca, cspecs = zip(*[_row_spec(c, tm) for c in cots])
    extra = extra or {}
    ekeys = sorted(extra)
    ea, especs = (zip(*[_row_spec(extra[k], tm) for k in ekeys]) if ekeys else ((), ()))
    nx, nxr, npar, nc, ne = len(plan), len(xa), len(pa), len(ca), len(ea)
    gx = [i for i in range(nx) if x_grad[i]]
    gp = [i for i in range(npar) if p_grad[i]]
    widths = [sum(plan[gx[ix]][2] for ix in idxs) for idxs in dx_groups]
    ng = len(dx_groups)

    def body(*refs):
        ins = refs[:nxr + npar + nc + ne]
        outs = refs[nxr + npar + nc + ne:]
        vals = _x_vals(ins[:nxr], plan, tm, nt) + [r[...].astype(F32) for r in ins[nxr:nxr + npar]]
        cvals = tuple(r[...].astype(F32) for r in ins[nxr + npar:nxr + npar + nc])
        evals = [r[...].astype(F32) for r in ins[nxr + npar + nc:]]
        diff_idx = gx + [nx + i for i in gp]

        def g(*dargs):
            full = list(vals)
            for ix, v in zip(diff_idx, dargs):
                full[ix] = v
            return tuple(f(*full))

        _, vjp = jax.vjp(g, *[vals[ix] for ix in diff_idx])
        grads = vjp(cvals)
        dxs = list(grads[:len(gx)])
        for k, ev in zip(ekeys, evals):
            dxs[k] = dxs[k] + ev
        _store_groups(outs[:ng], dx_groups, dxs)
        i = pl.program_id(0)
        for ref, gval in zip(outs[ng:], grads[len(gx):]):
            @pl.when(i == 0)
            def _(ref=ref):
                ref[...] = jnp.zeros_like(ref)
            ref[...] += gval

    dp_specs = [pl.BlockSpec(pspecs[i].block_shape, lambda i: (0, 0)) for i in gp]
    dp_shapes = [SDS(pspecs[i].block_shape, F32) for i in gp]
    return pl.pallas_call(
        body, name=name, grid=(nt,), in_specs=list(xspecs) + list(pspecs) + list(cspecs) + list(especs),
        out_specs=[pl.BlockSpec((tm, w), lambda i: (i, 0)) for w in widths] + dp_specs,
        out_shape=[SDS((t, w), dt) for w, dt in zip(widths, dx_dtypes)] + dp_shapes,
        compiler_params=_params(("arbitrary",)))(*xa, *pa, *ca, *ea)


def _rms_f(x, g):
    return (x * lax.rsqrt(jnp.mean(x * x, axis=-1, keepdims=True) + NORM_EPS) * g,)


def _group_sum(x, ones_bd):
    return _nn_hi(x, ones_bd)


def _rwkv_prep_f(r, k, v, lo, rp, kp, vp, lop, mu_r, mu_k, mu_v, mu_lo, w0, w2p, a0, a2p, g2p, k_k, k_a, ones_bd):
    r = r + mu_r * (rp - r)
    k = k + mu_k * (kp - k)
    v = v + mu_v * (vp - v)
    lo = lo + mu_lo * (lop - lo)
    w_log = -_softplus(-(w0 + _nn(jnp.tanh(lo), w2p))) - 0.5
    lw = -jnp.exp(w_log)
    a_g = _sigmoid(a0 + _nn(lo, a2p))
    g = _nn(_sigmoid(lo), g2p)
    kk = k * k_k
    kk = kk / jnp.maximum(jnp.sqrt(_group_sum(kk * kk, ones_bd)), L2_EPS)
    k2 = k * (1.0 + (a_g - 1.0) * k_a)
    return r, lw, k2, v, -kk, kk * a_g, g


def _rwkv_post_f(y, r, k2, v, g, r_k, gn_w, gn_b, ones_bd):
    inv_n = 1.0 / HB_DIM
    mean = _group_sum(y, ones_bd) * inv_n
    yc = y - mean
    var = _group_sum(yc * yc, ones_bd) * inv_n
    yn = yc * lax.rsqrt(var + RWKV_GN_EPS) * gn_w + gn_b
    bonus = _group_sum(r * k2 * r_k, ones_bd) * v
    return ((yn + bonus) * g,)


def _tri(c, strict=False):
    ii = lax.broadcasted_iota(jnp.int32, (c, c), 0)
    jj = lax.broadcasted_iota(jnp.int32, (c, c), 1)
    return (jj < ii) if strict else (jj <= ii)


def _hgrn_head(st0, q_a, f_a, i_a, g_a, l0, l1, onorm):
    c = q_a.shape[0]
    mx = jnp.maximum(l0, l1)
    e0, e1 = jnp.exp(l0 - mx), jnp.exp(l1 - mx)
    lb = e0 / (e0 + e1)
    forget = lb + (1.0 - lb) * _sigmoid(f_a)
    q = _silu(q_a)
    kk = 1.0 - forget
    lf = jnp.log(forget)
    incl = _tri(c)
    bcum = _nn_hi(incl.astype(F32), lf)
    rows = lax.broadcasted_iota(jnp.int32, (c, 1), 0)
    bref = jnp.sum(jnp.where(rows <= c // 2, lf, 0.0), axis=0, keepdims=True)
    blast = jnp.sum(lf, axis=0, keepdims=True)
    scores = jnp.where(incl, _nt(q * jnp.exp(bcum - bref), kk * jnp.exp(bref - bcum)), 0.0)
    o = _nn(scores, i_a) + _nt(q * jnp.exp(bcum), st0)
    st1 = st0 * jnp.exp(blast) + _tn(i_a, kk * jnp.exp(blast - bcum))
    o = o * lax.rsqrt(jnp.mean(o * o, axis=-1, keepdims=True) + NORM_EPS)
    return o * onorm * _silu(g_a), st1


def _hgrn_fwd(p_h, l0, l1, onorm):
    t = p_h.shape[0]
    c, n = HGRN_CHUNK, p_h.shape[0] // HGRN_CHUNK

    def body(q_ref, f_ref, i_ref, g_ref, l0_ref, l1_ref, on_ref, o_ref, hs_ref, st_ref):
        @pl.when(pl.program_id(0) == 0)
        def _():
            st_ref[...] = jnp.zeros_like(st_ref)

        hs_ref[0] = st_ref[...]
        for h in range(HA_HEADS):
            sl = slice(h * HA_DIM, (h + 1) * HA_DIM)
            o, st1 = _hgrn_head(st_ref[h], q_ref[:, sl], f_ref[:, sl], i_ref[:, sl], g_ref[:, sl],
                                l0_ref[:, sl], l1_ref[:, sl], on_ref[:, sl])
            o_ref[:, sl] = o
            st_ref[h] = st1

    col = lambda j: pl.BlockSpec((c, W_A), lambda i, j=j: (i, j))
    par = pl.BlockSpec((1, W_A), lambda i: (0, 0))
    return pl.pallas_call(
        body, name="hgrn_fwd", grid=(n,), in_specs=[col(0), col(1), col(2), col(3), par, par, par],
        out_specs=[pl.BlockSpec((c, W_A), lambda i: (i, 0)),
                   pl.BlockSpec((1, HA_HEADS, HA_DIM, HA_DIM), lambda i: (i, 0, 0, 0))],
        out_shape=[SDS((t, W_A), F32), SDS((n, HA_HEADS, HA_DIM, HA_DIM), F32)],
        scratch_shapes=[pltpu.VMEM((HA_HEADS, HA_DIM, HA_DIM), F32)],
        compiler_params=_params(("arbitrary",)))(p_h, p_h, p_h, p_h, l0, l1, onorm)


def _hgrn_bwd(p_h, l0, l1, onorm, hs, do, do_col):
    t = p_h.shape[0]
    c, n = HGRN_CHUNK, p_h.shape[0] // HGRN_CHUNK

    def body(q_ref, f_ref, i_ref, g_ref, l0_ref, l1_ref, on_ref, hs_ref, do_ref,
             dp_ref, dl0_ref, dl1_ref, don_ref, dst_ref):
        @pl.when(pl.program_id(0) == 0)
        def _():
            dst_ref[...] = jnp.zeros_like(dst_ref)
            dl0_ref[...] = jnp.zeros_like(dl0_ref)
            dl1_ref[...] = jnp.zeros_like(dl1_ref)
            don_ref[...] = jnp.zeros_like(don_ref)

        for h in range(HA_HEADS):
            sl = slice(h * HA_DIM, (h + 1) * HA_DIM)
            args = (hs_ref[0, h], q_ref[:, sl], f_ref[:, sl], i_ref[:, sl], g_ref[:, sl],
                    l0_ref[:, sl], l1_ref[:, sl], on_ref[:, sl])
            _, vjp = jax.vjp(_hgrn_head, *args)
            dst0, dq, df, di, dg, dl0, dl1, don = vjp((do_ref[:, sl], dst_ref[h]))
            for j, dv in enumerate((dq, df, di, dg)):
                dp_ref[:, j * W_A + h * HA_DIM:j * W_A + (h + 1) * HA_DIM] = dv
            dl0_ref[:, sl] += dl0
            dl1_ref[:, sl] += dl1
            don_ref[:, sl] += don
            dst_ref[h] = dst0

    col = lambda j: pl.BlockSpec((c, W_A), lambda i, j=j: (n - 1 - i, j))
    par = pl.BlockSpec((1, W_A), lambda i: (0, 0))
    return pl.pallas_call(
        body, name="hgrn_bwd", grid=(n,),
        in_specs=[col(0), col(1), col(2), col(3), par, par, par,
                  pl.BlockSpec((1, HA_HEADS, HA_DIM, HA_DIM), lambda i: (n - 1 - i, 0, 0, 0)),
                  pl.BlockSpec((c, W_A), lambda i: (n - 1 - i, do_col))],
        out_specs=[pl.BlockSpec((c, N_HGRN_COLS), lambda i: (n - 1 - i, 0)), par, par, par],
        out_shape=[SDS((t, N_HGRN_COLS), F32), SDS((1, W_A), F32), SDS((1, W_A), F32), SDS((1, W_A), F32)],
        scratch_shapes=[pltpu.VMEM((HA_HEADS, HA_DIM, HA_DIM), F32)],
        compiler_params=_params(("arbitrary",)))(p_h, p_h, p_h, p_h, l0, l1, onorm, hs, do)


HB_PAIRS = HB_HEADS // 2
PAIR_W = 2 * HB_DIM


def _head_lane_masks():
    lane = lax.broadcasted_iota(jnp.int32, (1, PAIR_W), 1)
    return (lane < HB_DIM).astype(F32), (lane >= HB_DIM).astype(F32)


@jax.custom_vjp
def _stack_heads(x):
    m0, m1 = _head_lane_masks()
    return jnp.concatenate([x * m0, x * m1], axis=0)


def _stack_heads_bwd(_, g):
    m0, m1 = _head_lane_masks()
    c = g.shape[0] // 2
    return (g[:c] * m0 + g[c:] * m1,)


_stack_heads.defvjp(lambda x: (_stack_heads(x), None), _stack_heads_bwd)


@jax.custom_vjp
def _unstack_heads(ys):
    c = ys.shape[0] // 2
    return ys[:c] + ys[c:]


_unstack_heads.defvjp(lambda ys: (_unstack_heads(ys), None), lambda _, g: (_stack_heads(g),))


def _same_head_block(c):
    ii = lax.broadcasted_iota(jnp.int32, (2 * c, 2 * c), 0)
    jj = lax.broadcasted_iota(jnp.int32, (2 * c, 2 * c), 1)
    same = (ii < c) == (jj < c)
    return same & (jj <= ii), same & (jj < ii), (ii == jj).astype(F32)


def _rwkv_step(s0, r, lw, k, v, a, b):
    npair, nj = len(r), len(r[0])
    c = r[0][0].shape[0]
    combos = [(j, p) for j in range(nj) for p in range(npair)]
    every = lambda fn: {q: fn(q) for q in combos}
    at_ = lambda d: (lambda q: d[q[1]][q[0]])
    r_, lw_, k_, v_, a_, b_ = (at_(z) for z in (r, lw, k, v, a, b))
    incl, strict, eye = _same_head_block(c)

    gam = every(lambda q: _cumsum_rows(lw_(q)))
    gtot = every(lambda q: jnp.sum(lw_(q), axis=0, keepdims=True))
    eneg = every(lambda q: jnp.exp(-gam[q]))
    edec = every(lambda q: jnp.exp(gtot[q] - gam[q]))
    at = every(lambda q: _stack_heads(a_(q) * jnp.exp(gam[q] - lw_(q))))
    rt = every(lambda q: _stack_heads(r_(q) * jnp.exp(gam[q])))
    bt = every(lambda q: _stack_heads(b_(q) * eneg[q]))
    kt = every(lambda q: _stack_heads(k_(q) * eneg[q]))
    bdec = every(lambda q: _stack_heads(b_(q) * edec[q]))
    kdec = every(lambda q: _stack_heads(k_(q) * edec[q]))
    vs = every(lambda q: _stack_heads(v_(q)))
    a_ab = every(lambda q: jnp.where(strict, _nt(at[q], bt[q]), 0.0))
    a_ak = every(lambda q: jnp.where(strict, _nt(at[q], kt[q]), 0.0))
    a_rb = every(lambda q: jnp.where(incl, _nt(rt[q], bt[q]), 0.0))
    a_rk = every(lambda q: jnp.where(incl, _nt(rt[q], kt[q]), 0.0))
    tinv = every(lambda q: eye + a_ab[q])
    pw = a_ab
    span = 2
    while span < c:
        pw = every(lambda q, pw=pw: _nn_x3(pw[q], pw[q]))
        tinv = every(lambda q, pw=pw, tinv=tinv: tinv[q] + _nn_x3(pw[q], tinv[q]))
        span *= 2
    akv = every(lambda q: _nn(a_ak[q], vs[q]))
    w1 = every(lambda q: _nn_x3(tinv[q], at[q]))
    u0 = every(lambda q: _nn_x3(tinv[q], akv[q]))
    r1 = every(lambda q: rt[q] + _nn(a_rb[q], w1[q]))
    y0 = every(lambda q: _nn(a_rb[q], u0[q]) + _nn(a_rk[q], vs[q]))
    mm = every(lambda q: _tn(w1[q], bdec[q]))
    zz = every(lambda q: _tn(u0[q], bdec[q]) + _tn(vs[q], kdec[q]))
    gdec = every(lambda q: jnp.exp(gtot[q]))

    s = list(s0)
    y = [[None] * nj for _ in range(npair)]
    for j in range(nj):
        for p in range(npair):
            y[p][j] = _unstack_heads(_nt(r1[(j, p)], s[p]) + y0[(j, p)])
        s = [s[p] * gdec[(j, p)] + _nn(s[p], mm[(j, p)]) + zz[(j, p)] for p in range(npair)]
    return y, s


def _rwkv_blocks(ref, nj, c):
    return [[ref[j * c:(j + 1) * c, p * PAIR_W:(p + 1) * PAIR_W] for j in range(nj)] for p in range(HB_PAIRS)]


def _rwkv_fwd(seqs):
    t = seqs[0].shape[0]
    c, nj = RWKV_CHUNK, RWKV_GROUP
    n = t // (c * nj)

    def body(r_ref, lw_ref, k_ref, v_ref, a_ref, b_ref, y_ref, hs_ref, st_ref):
        @pl.when(pl.program_id(0) == 0)
        def _():
            st_ref[...] = jnp.zeros_like(st_ref)

        hs_ref[0] = st_ref[...]
        s0 = [st_ref[p] for p in range(HB_PAIRS)]
        y, s1 = _rwkv_step(s0, *[_rwkv_blocks(ref, nj, c) for ref in (r_ref, lw_ref, k_ref, v_ref, a_ref, b_ref)])
        for p in range(HB_PAIRS):
            for j in range(nj):
                y_ref[j * c:(j + 1) * c, p * PAIR_W:(p + 1) * PAIR_W] = y[p][j]
            st_ref[p] = s1[p]

    seq = pl.BlockSpec((c * nj, W_B), lambda i: (i, 0))
    return pl.pallas_call(
        body, name="rwkv_fwd", grid=(n,), in_specs=[seq] * 6,
        out_specs=[seq, pl.BlockSpec((1, HB_PAIRS, PAIR_W, PAIR_W), lambda i: (i, 0, 0, 0))],
        out_shape=[SDS((t, W_B), F32), SDS((n, HB_PAIRS, PAIR_W, PAIR_W), F32)],
        scratch_shapes=[pltpu.VMEM((HB_PAIRS, PAIR_W, PAIR_W), F32)],
        compiler_params=_params(("arbitrary",)))(*seqs)


def _rwkv_bwd(seqs, hs, dy):
    t = seqs[0].shape[0]
    c, nj = RWKV_CHUNK, RWKV_GROUP
    n = t // (c * nj)

    def body(r_ref, lw_ref, k_ref, v_ref, a_ref, b_ref, hs_ref, dy_ref,
             dr_ref, dlw_ref, dk_ref, dv_ref, da_ref, db_ref, dst_ref):
        @pl.when(pl.program_id(0) == 0)
        def _():
            dst_ref[...] = jnp.zeros_like(dst_ref)

        s0 = [hs_ref[0, p] for p in range(HB_PAIRS)]
        seq_vals = [_rwkv_blocks(ref, nj, c) for ref in (r_ref, lw_ref, k_ref, v_ref, a_ref, b_ref)]
        _, vjp = jax.vjp(_rwkv_step, s0, *seq_vals)
        grads = vjp((_rwkv_blocks(dy_ref, nj, c), [dst_ref[p] for p in range(HB_PAIRS)]))
        for ref, gr in zip((dr_ref, dlw_ref, dk_ref, dv_ref, da_ref, db_ref), grads[1:]):
            for p in range(HB_PAIRS):
                for j in range(nj):
                    ref[j * c:(j + 1) * c, p * PAIR_W:(p + 1) * PAIR_W] = gr[p][j]
        m0, m1 = _head_lane_masks()
        rows0 = (lax.broadcasted_iota(jnp.int32, (PAIR_W, 1), 0) < HB_DIM).astype(F32)
        blocks = rows0 * m0 + (1.0 - rows0) * m1
        for p in range(HB_PAIRS):
            dst_ref[p] = grads[0][p] * blocks

    seq = pl.BlockSpec((c * nj, W_B), lambda i: (n - 1 - i, 0))
    return pl.pallas_call(
        body, name="rwkv_bwd", grid=(n,),
        in_specs=[seq] * 6 + [pl.BlockSpec((1, HB_PAIRS, PAIR_W, PAIR_W), lambda i: (n - 1 - i, 0, 0, 0)), seq],
        out_specs=[seq] * 6, out_shape=[SDS((t, W_B), F32)] * 6,
        scratch_shapes=[pltpu.VMEM((HB_PAIRS, PAIR_W, PAIR_W), F32)],
        compiler_params=_params(("arbitrary",)))(*seqs, hs, dy)


def _final_loss(x3, fnorm, target, *, tm):
    t, d = x3.shape

    def body(x_ref, g_ref, t_ref, dx_ref, dg_ref, loss_ref):
        @pl.when(pl.program_id(0) == 0)
        def _():
            dg_ref[...] = jnp.zeros_like(dg_ref)
            loss_ref[...] = jnp.zeros_like(loss_ref)

        x, g = x_ref[...], g_ref[...]
        rinv = lax.rsqrt(jnp.mean(x * x, axis=-1, keepdims=True) + NORM_EPS)
        xh = x * rinv
        diff = xh * g - t_ref[...]
        loss_ref[...] += 0.5 * jnp.sum(jnp.mean(diff * diff, axis=-1, keepdims=True))
        dy = diff * (1.0 / d)
        dg_ref[...] += jnp.sum(dy * xh, axis=0, keepdims=True)
        dxh = dy * g
        dx_ref[...] = rinv * (dxh - xh * jnp.mean(dxh * xh, axis=-1, keepdims=True))

    row = pl.BlockSpec((tm, d), lambda i: (i, 0))
    return pl.pallas_call(
        body, name="final_loss", grid=(t // tm,), in_specs=[row, pl.BlockSpec((1, d), lambda i: (0, 0)), row],
        out_specs=[row, pl.BlockSpec((1, d), lambda i: (0, 0)), pl.BlockSpec((8, 128), lambda i: (0, 0))],
        out_shape=[SDS((t, d), F32), SDS((1, d), F32), SDS((8, 128), F32)],
        compiler_params=_params(("arbitrary",)))(x3, fnorm, target)


def _gate_up_act(h, wgt, wut, *, tm, tn, name):
    t, d = h.shape
    tm = min(tm, t)

    def body(h_ref, g_ref, u_ref, a_out, u_out, act_out):
        hv = h_ref[...]
        a = _dg(hv, g_ref[...], 1, 1, False)
        u = _dg(hv, u_ref[...], 1, 1, False)
        a_out[...] = a
        u_out[...] = u
        act_out[...] = (_silu(a) * u).astype(act_out.dtype)

    wspec = pl.BlockSpec((tn, d), lambda i, j: (j, 0))
    ospec = pl.BlockSpec((tm, tn), lambda i, j: (i, j))
    return pl.pallas_call(
        body, name=name, grid=(t // tm, D_FF // tn), in_specs=[pl.BlockSpec((tm, d), lambda i, j: (i, 0)), wspec, wspec],
        out_specs=[ospec, ospec, ospec], out_shape=[SDS((t, D_FF), F32), SDS((t, D_FF), F32), SDS((t, D_FF), BF16)],
        compiler_params=_params(("parallel", "parallel")))(h, wgt, wut)


def _dact_swiglu(dout, wd, a, u, *, tm, tn, name):
    t, d = dout.shape
    tm = min(tm, t)

    def body(d_ref, w_ref, a_ref, u_ref, da_out, du_out):
        dact = 0.5 * _dg(d_ref[...], w_ref[...], 1, 1, False)
        av, uv = a_ref[...], u_ref[...]
        s = _sigmoid(av)
        da_out[...] = (dact * uv * (s * (1.0 + av * (1.0 - s)))).astype(da_out.dtype)
        du_out[...] = (dact * (av * s)).astype(du_out.dtype)

    tile = pl.BlockSpec((tm, tn), lambda i, j: (i, j))
    return pl.pallas_call(
        body, name=name, grid=(t // tm, D_FF // tn),
        in_specs=[pl.BlockSpec((tm, d), lambda i, j: (i, 0)), pl.BlockSpec((tn, d), lambda i, j: (j, 0)), tile, tile],
        out_specs=[tile, tile], out_shape=[SDS((t, D_FF), BF16), SDS((t, D_FF), BF16)],
        compiler_params=_params(("parallel", "parallel")))(dout, wd, a, u)


def _ffn_fwd(x, norm, wgt, wut, wd, tag):
    h, = _rowwise(_rms_f, [x], [norm], [[0]], [BF16], tm=512, name=f"{tag}_rms")
    a, u, act = _gate_up_act(h, wgt, wut, tm=512, tn=D_FF // 2, name=f"{tag}_gate_up")
    out = _mm(act, wd, tm=512, tn=D_MODEL, tk=D_FF // 2, name=f"{tag}_down", res=x, scale=0.5)
    return out, (h, a, u, act)


def _ffn_bwd(dout, x, norm, wgt, wut, wd, saved, tag):
    h, a, u, act = saved
    da, du = _dact_swiglu(dout, wd, a, u, tm=512, tn=D_FF // 2, name=f"{tag}_dact")
    dwd = _mm(act, dout, ta=True, tm=D_FF // 2, tn=D_MODEL, tk=512, name=f"{tag}_dwd", scale=0.5)
    dwgt = _mm(da, h, ta=True, tm=D_FF // 2, tn=D_MODEL, tk=512, name=f"{tag}_dwg")
    dwut = _mm(du, h, ta=True, tm=D_FF // 2, tn=D_MODEL, tk=512, name=f"{tag}_dwu")
    dh = _mm(da, wgt, tm=512, tn=D_MODEL, tk=D_FF // 2, name=f"{tag}_dh_g")
    dh = _mm(du, wut, tm=512, tn=D_MODEL, tk=D_FF // 2, name=f"{tag}_dh_u", res=dh)
    dx, dnorm = _rowwise_bwd(_rms_f, [x], [norm], [dh], x_grad=[True], p_grad=[True], dx_groups=[[0]],
                             dx_dtypes=[F32], tm=256, name=f"{tag}_drms", extra={0: dout})
    return dx, dnorm, dwgt, dwut, dwd


def _local_step(x, target, w):
    ones_bd = jnp.kron(jnp.eye(HB_HEADS, dtype=F32), jnp.ones((HB_DIM, HB_DIM), F32))
    g = {}
    x1, ffn1_saved = _ffn_fwd(x, w["ffn1_norm"], w["ffn1_wgt"], w["ffn1_wut"], w["ffn1_wd"], "ffn1")
    hm, = _rowwise(_rms_f, [x1], [w["mix_norm"]], [[0]], [BF16], tm=512, name="mix_rms")
    p_h = _mm(hm, w["w_in_h"], tm=512, tn=512, tk=D_MODEL, name="inproj_h")
    p_r = _mm(hm, w["w_in_r"], tm=512, tn=N_RWKV_PAD // 2, tk=D_MODEL, name="inproj_r")
    o_a, hgrn_states = _hgrn_fwd(p_h, w["lb0"], w["lb1"], w["hgrn_out_norm"])

    mu = w["mu_pad"]
    prep_xs = [(p_r, W_B, 0), (p_r, W_B, 1), (p_r, W_B, 2), (p_r, LORA_PAD, 6),
               ("prev", p_r, W_B, 0), ("prev", p_r, W_B, 1), ("prev", p_r, W_B, 2), ("prev", p_r, LORA_PAD, 6)]
    prep_ps = [(mu, W_B, 0), (mu, W_B, 1), (mu, W_B, 2), (mu, LORA_PAD, 6), w["rwkv_w0"], w["w2_pad"], w["rwkv_a0"],
               w["a2_pad"], w["g2_pad"], w["rwkv_k_k"], w["rwkv_k_a"], ones_bd]
    prep_f = _rwkv_prep_f
    r, lw, k2, v, a_vec, b_vec, gate = _rowwise(prep_f, prep_xs, prep_ps, [[0], [1], [2], [3], [4], [5], [6]],
                                                [F32] * 7, tm=256, name="rwkv_prep")
    seqs = [r, lw, k2, v, a_vec, b_vec]
    y, rwkv_states = _rwkv_fwd(seqs)
    post_f = _rwkv_post_f
    post_xs = [y, r, k2, v, gate]
    post_ps = [w["rwkv_r_k"], w["rwkv_gn_w"], w["rwkv_gn_b"], ones_bd]
    o_b, = _rowwise(post_f, post_xs, post_ps, [[0]], [F32], tm=256, name="rwkv_post")
    x2 = _mm(o_a, w["w_out_a"], tm=512, tn=D_MODEL, tk=W_A, name="outproj_a", res=x1)
    x2 = _mm(o_b, w["w_out_b"], tm=512, tn=D_MODEL, tk=W_B, name="outproj_b", res=x2)
    x3, ffn2_saved = _ffn_fwd(x2, w["ffn2_norm"], w["ffn2_wgt"], w["ffn2_wut"], w["ffn2_wd"], "ffn2")
    dx3, g["final_norm"], loss = _final_loss(x3, w["final_norm"], target, tm=256)

    dx2, g["ffn2_norm"], g["ffn2_wgt"], g["ffn2_wut"], g["ffn2_wd"] = _ffn_bwd(
        dx3, x2, w["ffn2_norm"], w["ffn2_wgt"], w["ffn2_wut"], w["ffn2_wd"], ffn2_saved, "ffn2")
    do_a = _mm(dx2, w["w_out_a"], tb=True, tm=512, tn=W_A, tk=D_MODEL, name="outproj_do_a")
    do_b = _mm(dx2, w["w_out_b"], tb=True, tm=512, tn=W_B, tk=D_MODEL, name="outproj_do_b")
    g["w_out_a"] = _mm(o_a, dx2, ta=True, tm=W_A, tn=D_MODEL, tk=512, name="outproj_dw_a")
    g["w_out_b"] = _mm(o_b, dx2, ta=True, tm=W_B, tn=D_MODEL, tk=512, name="outproj_dw_b")

    dp_h, g["lb0"], g["lb1"], g["hgrn_out_norm"] = _hgrn_bwd(p_h, w["lb0"], w["lb1"], w["hgrn_out_norm"],
                                                             hgrn_states, do_a, 0)
    post_out = _rowwise_bwd(post_f, post_xs, post_ps, [do_b], x_grad=[True] * 5, p_grad=[True] * 3 + [False],
                            dx_groups=[[0], [1], [2], [3], [4]], dx_dtypes=[F32] * 5, tm=256, name="rwkv_post_bwd")
    dy, dr1, dk1, dv1, dgate, g["rwkv_r_k"], g["rwkv_gn_w"], g["rwkv_gn_b"] = post_out
    dr2, dlw, dk2, dv2, da_vec, db_vec = _rwkv_bwd(seqs, rwkv_states, dy)

    def prep2_f(*vals):
        r_, lw_, k2_, v_, a_, b_, g_ = prep_f(*vals)
        return r_, lw_, k2_, v_, a_, b_, g_, r_, k2_, v_

    prep_out = _rowwise_bwd(prep2_f, prep_xs, prep_ps, [dr2, dlw, dk2, dv2, da_vec, db_vec, dgate, dr1, dk1, dv1],
                            x_grad=[True] * 8, p_grad=[True] * 11 + [False], dx_groups=[[0, 1, 2, 3], [4, 5, 6, 7]],
                            dx_dtypes=[F32, F32], tm=256, name="rwkv_prep_bwd")
    dpr_main, dpr_prev = prep_out[0], prep_out[1]
    (dmu_r, dmu_k, dmu_v, dmu_lo, g["rwkv_w0"], g["w2_pad"], g["rwkv_a0"], g["a2_pad"], g["g2_pad"],
     g["rwkv_k_k"], g["rwkv_k_a"]) = prep_out[2:]
    g["mu_pad"] = jnp.concatenate([dmu_r, dmu_k, dmu_v, dmu_lo], axis=1)
    dp_r, = _rowwise(lambda u_, s_: (u_ + s_,), [dpr_main, ("next", dpr_prev, N_RWKV_PAD, 0)], [], [[0]], [F32],
                     tm=512, name="rwkv_dp_sum")
    dhm = _mm(dp_h, w["w_in_h"], tb=True, tm=512, tn=D_MODEL, tk=D_MODEL, name="inproj_dh_h")
    dhm = _mm(dp_r, w["w_in_r"], tb=True, tm=512, tn=D_MODEL, tk=N_RWKV_PAD // 2, name="inproj_dh_r", res=dhm)
    g["w_in_h"] = _mm(hm, dp_h, ta=True, tm=D_MODEL, tn=D_MODEL, tk=512, name="inproj_dw_h")
    g["w_in_r"] = _mm(hm, dp_r, ta=True, tm=D_MODEL, tn=N_RWKV_PAD // 2, tk=512, name="inproj_dw_r")
    dx1, g["mix_norm"] = _rowwise_bwd(_rms_f, [x1], [w["mix_norm"]], [dhm], x_grad=[True], p_grad=[True],
                                      dx_groups=[[0]], dx_dtypes=[F32], tm=256, name="mix_drms", extra={0: dx2})
    dx0, g["ffn1_norm"], g["ffn1_wgt"], g["ffn1_wut"], g["ffn1_wd"] = _ffn_bwd(
        dx1, x, w["ffn1_norm"], w["ffn1_wgt"], w["ffn1_wut"], w["ffn1_wd"], ffn1_saved, "ffn1")
    return loss, dx0, g


HBM_SPEC = pl.BlockSpec(memory_space=pl.ANY)


def _chips(x, y):
    return [(1 - x, y), (x, 1 - y), (1 - x, 1 - y)]


def _gather_weights(bufs):
    n = len(bufs)

    def body(*refs):
        outs = refs[n:2 * n]
        ici_send, ici_recv, d2d_send, d2d_recv = refs[2 * n:]
        x, y, c = lax.axis_index("x"), lax.axis_index("y"), lax.axis_index("c")
        me = 2 * x + y

        def half(t, slot, hc):
            hr = bufs[t].shape[1] // 2
            return outs[t].at[slot, pl.ds(pl.multiple_of(hc * hr, 16), hr), :]

        def ici(t, j, slot, px, py):
            return pltpu.make_async_remote_copy(src_ref=half(t, slot, c), dst_ref=half(t, slot, c),
                                                send_sem=ici_send.at[3 * t + j], recv_sem=ici_recv.at[3 * t + j],
                                                device_id=(px, py, c), device_id_type=MESH)

        def d2d(t, j, slot, hc):
            return pltpu.make_async_remote_copy(src_ref=half(t, slot, hc), dst_ref=half(t, slot, hc),
                                                send_sem=d2d_send.at[3 * t + j], recv_sem=d2d_recv.at[3 * t + j],
                                                device_id=(x, y, 1 - c), device_id_type=MESH)

        sends = [ici(t, j, me, px, py) for t in range(n) for j, (px, py) in enumerate(_chips(x, y))]
        for cp in sends:
            cp.start()
        passed = []
        for t in range(n):
            for j, (px, py) in enumerate(_chips(x, y)):
                ici(t, j, 2 * px + py, px, py).wait_recv()
                cp = d2d(t, j, 2 * px + py, c)
                cp.start()
                passed.append(cp)
        for t in range(n):
            for j, (px, py) in enumerate(_chips(x, y)):
                d2d(t, j, 2 * px + py, 1 - c).wait_recv()
        for cp in sends + passed:
            cp.wait_send()

    return pl.pallas_call(
        body, name="gather_weights", in_specs=[HBM_SPEC] * n, out_specs=[HBM_SPEC] * n,
        out_shape=[SDS(b.shape, b.dtype) for b in bufs], input_output_aliases={t: t for t in range(n)},
        scratch_shapes=[pltpu.SemaphoreType.DMA((3 * n,))] * 4,
    )(*bufs)


def _sibling_exchange(gs):
    n = len(gs)

    def body(*refs):
        ins, outs = refs[:n], refs[n:2 * n]
        send_sems, recv_sems = refs[2 * n:]
        x, y, c = lax.axis_index("x"), lax.axis_index("y"), lax.axis_index("c")
        cps = []
        for t in range(n):
            hr = gs[t].shape[1] // 2
            src = ins[t].at[:, pl.ds(pl.multiple_of((1 - c) * hr, SUBLANES), hr), :]
            cps.append(pltpu.make_async_remote_copy(src_ref=src, dst_ref=outs[t], send_sem=send_sems.at[t],
                                                    recv_sem=recv_sems.at[t], device_id=(x, y, 1 - c),
                                                    device_id_type=MESH))
        for cp in cps:
            cp.start()
        for cp in cps:
            cp.wait()

    return pl.pallas_call(
        body, name="grad_sibling_exchange", in_specs=[HBM_SPEC] * n, out_specs=[HBM_SPEC] * n,
        out_shape=[SDS((N_CHIPS, g.shape[1] // 2, g.shape[2]), g.dtype) for g in gs],
        scratch_shapes=[pltpu.SemaphoreType.DMA((n,)), pltpu.SemaphoreType.DMA((n,))],
    )(*gs)


def _chip_exchange(ss):
    n = len(ss)

    def body(*refs):
        ins, outs = refs[:n], refs[n:2 * n]
        send_sems, recv_sems = refs[2 * n:]
        x, y, c = lax.axis_index("x"), lax.axis_index("y"), lax.axis_index("c")
        me = 2 * x + y

        def copy(t, j, px, py, src_slot, dst_slot):
            return pltpu.make_async_remote_copy(src_ref=ins[t].at[src_slot], dst_ref=outs[t].at[dst_slot],
                                                send_sem=send_sems.at[3 * t + j], recv_sem=recv_sems.at[3 * t + j],
                                                device_id=(px, py, c), device_id_type=MESH)

        sends = [copy(t, j, px, py, 2 * px + py, me) for t in range(n) for j, (px, py) in enumerate(_chips(x, y))]
        for cp in sends:
            cp.start()
        for t in range(n):
            for j, (px, py) in enumerate(_chips(x, y)):
                copy(t, j, px, py, me, 2 * px + py).wait_recv()
        for cp in sends:
            cp.wait_send()

    return pl.pallas_call(
        body, name="grad_chip_exchange", in_specs=[HBM_SPEC] * n, out_specs=[HBM_SPEC] * n,
        out_shape=[SDS(s.shape, s.dtype) for s in ss],
        scratch_shapes=[pltpu.SemaphoreType.DMA((3 * n,)), pltpu.SemaphoreType.DMA((3 * n,))],
    )(*ss)


def _sibling_swap(fs):
    n = len(fs)

    def body(*refs):
        ins, outs = refs[:n], refs[n:2 * n]
        send_sems, recv_sems = refs[2 * n:]
        x, y, c = lax.axis_index("x"), lax.axis_index("y"), lax.axis_index("c")
        cps = [pltpu.make_async_remote_copy(src_ref=ins[t], dst_ref=outs[t], send_sem=send_sems.at[t],
                                            recv_sem=recv_sems.at[t], device_id=(x, y, 1 - c), device_id_type=MESH)
               for t in range(n)]
        for cp in cps:
            cp.start()
        for cp in cps:
            cp.wait()

    return pl.pallas_call(
        body, name="grad_sibling_swap", in_specs=[HBM_SPEC] * n, out_specs=[HBM_SPEC] * n,
        out_shape=[SDS(f.shape, f.dtype) for f in fs],
        scratch_shapes=[pltpu.SemaphoreType.DMA((n,)), pltpu.SemaphoreType.DMA((n,))],
    )(*fs)


def _row_tile(rows, cap=512):
    best = SUBLANES
    for tr in range(SUBLANES, min(rows, cap) + 1, SUBLANES):
        if rows % tr == 0:
            best = tr
    return best


def _add_halves(g4, r4, c_idx, name):
    _, hr, lanes = r4.shape
    tr = _row_tile(hr)
    nb = hr // tr

    def body(c_ref, a_ref, b_ref, o_ref):
        o_ref[...] = (a_ref[...] + b_ref[...]).astype(o_ref.dtype)

    grid_spec = pltpu.PrefetchScalarGridSpec(
        num_scalar_prefetch=1, grid=(N_CHIPS, nb),
        in_specs=[pl.BlockSpec((None, tr, lanes), lambda q, i, c_ref: (q, c_ref[0] * nb + i, 0)),
                  pl.BlockSpec((None, tr, lanes), lambda q, i, c_ref: (q, i, 0))],
        out_specs=pl.BlockSpec((None, tr, lanes), lambda q, i, c_ref: (q, i, 0)))
    return pl.pallas_call(body, name=name, grid_spec=grid_spec, out_shape=SDS(r4.shape, BF16),
                          compiler_params=_params(("parallel", "parallel")))(c_idx, g4, r4)


def _sum_chips(r4, s4, me_idx, name):
    _, rows, lanes = r4.shape
    tr = _row_tile(rows)

    def body(me_ref, a_ref, b_ref, c_ref, d_ref, own_ref, o_ref):
        own = own_ref[...].astype(F32)
        p = [jnp.where(me_ref[0] == q, own, ref[...].astype(F32)) for q, ref in enumerate((a_ref, b_ref, c_ref, d_ref))]
        o_ref[...] = ((p[0] + p[1]) + p[2]) + p[3]

    other = lambda q: (lambda i, me_ref: (jnp.where(me_ref[0] == q, (q + 1) % N_CHIPS, q), i, 0))
    grid_spec = pltpu.PrefetchScalarGridSpec(
        num_scalar_prefetch=1, grid=(rows // tr,),
        in_specs=[pl.BlockSpec((None, tr, lanes), other(q)) for q in range(N_CHIPS)]
        + [pl.BlockSpec((None, tr, lanes), lambda i, me_ref: (me_ref[0], i, 0))],
        out_specs=pl.BlockSpec((tr, lanes), lambda i, me_ref: (i, 0)))
    return pl.pallas_call(body, name=name, grid_spec=grid_spec, out_shape=SDS((rows, lanes), F32),
                          compiler_params=_params(("parallel",)))(me_idx, r4, r4, r4, r4, s4)


def _adamw(wf, g_own, g_other, mf, vf, c_idx, name):
    rows, lanes = wf.shape
    hr = rows // 2
    tr = _row_tile(hr)
    nb = hr // tr
    c1 = 1.0 / (1.0 - ADAM_B1 ** ADAM_STEP)
    c2 = 1.0 / (1.0 - ADAM_B2 ** ADAM_STEP)

    def body(c_ref, w_ref, go_ref, gx_ref, m_ref, v_ref, g_ref, d_ref, nm_ref, nv_ref):
        gv = jnp.where(pl.program_id(0) == c_ref[0], go_ref[...], gx_ref[...])
        m = ADAM_B1 * m_ref[...] + (1.0 - ADAM_B1) * gv
        v = ADAM_B2 * v_ref[...] + (1.0 - ADAM_B2) * (gv * gv)
        g_ref[...] = gv
        d_ref[...] = -ADAM_LR * ((m * c1) / (jnp.sqrt(v * c2) + ADAM_EPS) + ADAM_WD * w_ref[...])
        nm_ref[...] = m
        nv_ref[...] = v

    full = pl.BlockSpec((tr, lanes), lambda h, i, c_ref: (h * nb + i, 0))
    half = pl.BlockSpec((tr, lanes), lambda h, i, c_ref: (i, 0))
    grid_spec = pltpu.PrefetchScalarGridSpec(num_scalar_prefetch=1, grid=(2, nb),
                                             in_specs=[full, half, half, full, full], out_specs=[full] * 4)
    return pl.pallas_call(body, name=name, grid_spec=grid_spec, out_shape=[SDS((rows, lanes), F32)] * 4,
                          compiler_params=_params(("parallel", "parallel")))(c_idx, wf, g_own, g_other, mf, vf)


BIG = ("ffn1_w_gate", "ffn1_w_up", "ffn1_w_down", "ffn2_w_gate", "ffn2_w_up", "ffn2_w_down", "w_out")
TRANSPOSED = ("ffn1_w_gate", "ffn1_w_up", "ffn2_w_gate", "ffn2_w_up")
PACKED = ("w_in", "rwkv_w2", "rwkv_a2", "rwkv_g2")
SMALL_SHAPES = {"ffn1_norm": (1, D_MODEL), "mix_norm": (1, D_MODEL), "hgrn_lb_logits": (2, W_A),
                "hgrn_out_norm": (1, W_A), "rwkv_shift_mu": (1, N_RWKV_COLS), "rwkv_w0": (1, W_B),
                "rwkv_a0": (1, W_B), "rwkv_k_k": (1, W_B), "rwkv_k_a": (1, W_B),
                "rwkv_r_k": (1, HB_HEADS, HB_DIM), "rwkv_gn_w": (1, W_B), "rwkv_gn_b": (1, W_B),
                "ffn2_norm": (1, D_MODEL), "final_norm": (D_MODEL,)}
PACK_ELEMS = sum(_numel(_shard_shape(n)) for n in PACKED) + sum(_numel(SMALL_SHAPES[n]) for n in SMALL)
PACK_ROWS = -(-PACK_ELEMS // (32 * LANES)) * 32


def _to_rows(name, shard):
    return shard[0].T if name in TRANSPOSED else shard[0]


def _from_rows(name, rows):
    return (rows.T if name in TRANSPOSED else rows)[None]


def _pack(sharded, small):
    flat = jnp.concatenate([sharded[n].reshape(-1) for n in PACKED] + [small[n].reshape(-1) for n in SMALL])
    return jnp.pad(flat, (0, PACK_ROWS * LANES - flat.shape[0])).reshape(PACK_ROWS, LANES)


def _unpack(packed):
    flat, out, off = packed.reshape(-1), {}, 0
    for n in PACKED:
        shp = _shard_shape(n)
        out[n] = flat[off:off + _numel(shp)].reshape((1,) + shp)
        off += _numel(shp)
    for n in SMALL:
        shp = SMALL_SHAPES[n]
        out[n] = flat[off:off + _numel(shp)].reshape(shp)
        off += _numel(shp)
    return out


def _quarter(full, name, q):
    shape, ax = SHARDED_SHAPES[name]
    w = shape[ax] // N_CHIPS
    return lax.slice_in_dim(full, q * w, (q + 1) * w, axis=ax)


def kernel(x, ffn1_norm, ffn1_w_gate, ffn1_w_up, ffn1_w_down, mix_norm, w_in, hgrn_lb_logits, hgrn_out_norm, rwkv_shift_mu, rwkv_w0, rwkv_w2, rwkv_a0, rwkv_a2, rwkv_g2, rwkv_k_k, rwkv_k_a, rwkv_r_k, rwkv_gn_w, rwkv_gn_b, w_out, ffn2_norm, ffn2_w_gate, ffn2_w_up, ffn2_w_down, final_norm, loss_target, m_ffn1_norm, m_ffn1_w_gate, m_ffn1_w_up, m_ffn1_w_down, m_mix_norm, m_w_in, m_hgrn_lb_logits, m_hgrn_out_norm, m_rwkv_shift_mu, m_rwkv_w0, m_rwkv_w2, m_rwkv_a0, m_rwkv_a2, m_rwkv_g2, m_rwkv_k_k, m_rwkv_k_a, m_rwkv_r_k, m_rwkv_gn_w, m_rwkv_gn_b, m_w_out, m_ffn2_norm, m_ffn2_w_gate, m_ffn2_w_up, m_ffn2_w_down, m_final_norm, v_ffn1_norm, v_ffn1_w_gate, v_ffn1_w_up, v_ffn1_w_down, v_mix_norm, v_w_in, v_hgrn_lb_logits, v_hgrn_out_norm, v_rwkv_shift_mu, v_rwkv_w0, v_rwkv_w2, v_rwkv_a0, v_rwkv_a2, v_rwkv_g2, v_rwkv_k_k, v_rwkv_k_a, v_rwkv_r_k, v_rwkv_gn_w, v_rwkv_gn_b, v_w_out, v_ffn2_norm, v_ffn2_w_gate, v_ffn2_w_up, v_ffn2_w_down, v_final_norm):
    args = dict(locals())
    wts = {n: args[n] for n in ALL_WEIGHTS}
    moms = {n: args["m_" + n] for n in ALL_WEIGHTS}
    vars_ = {n: args["v_" + n] for n in ALL_WEIGHTS}

    small_w = {n: wts[n] for n in SMALL}
    shards = [_to_rows(n, wts[n]).astype(BF16) for n in BIG] + [_pack(wts, small_w).astype(BF16)]
    me = 2 * lax.axis_index("x") + lax.axis_index("y")
    gathered = _gather_weights([lax.dynamic_update_slice(jnp.zeros((N_CHIPS,) + s.shape, BF16), s[None], (me, 0, 0))
                                for s in shards])
    rows_of = dict(zip(BIG, gathered[:-1]))
    packs = gathered[-1].reshape(N_CHIPS, PACK_ROWS * LANES)
    full, off = {}, 0
    for n in PACKED:
        shp = _shard_shape(n)
        full[n] = jnp.concatenate([packs[q, off:off + _numel(shp)].reshape(shp) for q in range(N_CHIPS)], axis=1)
        off += _numel(shp)

    w = {}
    for tag in ("ffn1", "ffn2"):
        w[f"{tag}_wgt"] = rows_of[f"{tag}_w_gate"].reshape(D_FF, D_MODEL)
        w[f"{tag}_wut"] = rows_of[f"{tag}_w_up"].reshape(D_FF, D_MODEL)
        w[f"{tag}_wd"] = rows_of[f"{tag}_w_down"].reshape(D_FF, D_MODEL)
        w[f"{tag}_norm"] = wts[f"{tag}_norm"]
    w["w_in_h"] = full["w_in"][:, :N_HGRN_COLS]
    w["w_in_r"] = jnp.pad(full["w_in"][:, N_HGRN_COLS:], ((0, 0), (0, N_RWKV_PAD - N_RWKV_COLS)))
    w_out_full = rows_of["w_out"].reshape(D_MODEL, D_MODEL)
    w["w_out_a"], w["w_out_b"] = w_out_full[:W_A], w_out_full[W_A:]
    zrow = lambda nrow: jnp.zeros((nrow, W_B), BF16)
    w["w2_pad"] = jnp.concatenate([full["rwkv_w2"], zrow(LORA_PAD - 32)], axis=0)
    w["a2_pad"] = jnp.concatenate([zrow(32), full["rwkv_a2"], zrow(LORA_PAD - 64)], axis=0)
    w["g2_pad"] = jnp.concatenate([zrow(64), full["rwkv_g2"], zrow(LORA_PAD - 160)], axis=0)
    w["mix_norm"] = mix_norm
    w["lb0"], w["lb1"] = hgrn_lb_logits[0:1], hgrn_lb_logits[1:2]
    w["hgrn_out_norm"] = hgrn_out_norm
    w["mu_pad"] = jnp.pad(rwkv_shift_mu, ((0, 0), (0, N_RWKV_PAD - N_RWKV_COLS)))
    for n in ("rwkv_w0", "rwkv_a0", "rwkv_k_k", "rwkv_k_a", "rwkv_gn_w", "rwkv_gn_b"):
        w[n] = wts[n]
    w["rwkv_r_k"] = rwkv_r_k.reshape(1, W_B)
    w["final_norm"] = final_norm.reshape(1, D_MODEL)

    loss_slab, grad_x, g = _local_step(x[0], loss_target[0], w)
    loss = lax.psum(loss_slab[0, 0], ("x", "y", "c"))

    grows = {
        "ffn1_w_gate": g["ffn1_wgt"], "ffn1_w_up": g["ffn1_wut"], "ffn1_w_down": g["ffn1_wd"],
        "ffn2_w_gate": g["ffn2_wgt"], "ffn2_w_up": g["ffn2_wut"], "ffn2_w_down": g["ffn2_wd"],
        "w_out": jnp.concatenate([g["w_out_a"], g["w_out_b"]], axis=0),
    }
    gfull = {
        "w_in": jnp.concatenate([g["w_in_h"], g["w_in_r"][:, :N_RWKV_COLS]], axis=1),
        "rwkv_w2": g["w2_pad"][0:32], "rwkv_a2": g["a2_pad"][32:64], "rwkv_g2": g["g2_pad"][64:160],
    }
    gsmall = {
        "ffn1_norm": g["ffn1_norm"], "mix_norm": g["mix_norm"],
        "hgrn_lb_logits": jnp.concatenate([g["lb0"], g["lb1"]], axis=0), "hgrn_out_norm": g["hgrn_out_norm"],
        "rwkv_shift_mu": g["mu_pad"][:, :N_RWKV_COLS], "rwkv_w0": g["rwkv_w0"], "rwkv_a0": g["rwkv_a0"],
        "rwkv_k_k": g["rwkv_k_k"], "rwkv_k_a": g["rwkv_k_a"], "rwkv_r_k": g["rwkv_r_k"],
        "rwkv_gn_w": g["rwkv_gn_w"], "rwkv_gn_b": g["rwkv_gn_b"], "ffn2_norm": g["ffn2_norm"],
        "final_norm": g["final_norm"],
    }
    gs = [grows[n].reshape(N_CHIPS, -1, LANES) for n in BIG]
    gs.append(jnp.stack([_pack({n: _quarter(gfull[n], n, q) for n in PACKED}, gsmall) for q in range(N_CHIPS)]))
    names = list(BIG) + ["packed"]
    c_idx = lax.axis_index("c").astype(jnp.int32).reshape(1)
    me_idx = me.astype(jnp.int32).reshape(1)
    r1 = _sibling_exchange(gs)
    s4 = [_add_halves(gt, rt, c_idx, f"grad_add_halves_{n}") for gt, rt, n in zip(gs, r1, names)]
    r2 = _chip_exchange(s4)
    own = [_sum_chips(rt, st, me_idx, f"grad_sum_chips_{n}") for rt, st, n in zip(r2, s4, names)]
    other = _sibling_swap(own)

    def rows_list(d):
        return [_to_rows(n, d[n]) for n in BIG] + [_pack(d, {n: d[n] for n in SMALL})]

    outs = [_adamw(wt, go, gx, mt, vt, c_idx, f"adamw_{n}")
            for wt, go, gx, mt, vt, n in zip(rows_list(wts), own, other, rows_list(moms), rows_list(vars_), names)]
    results = []
    for k in range(4):
        per = [outs[i][k] for i in range(len(names))]
        d = {n: _from_rows(n, z) for n, z in zip(BIG, per[:-1])}
        d.update(_unpack(per[-1]))
        results.append(d)
    return (loss, grad_x[None], *[r[n] for r in results for n in ALL_WEIGHTS])
```

```python
import functools

import jax
import jax.numpy as jnp
from jax import lax
from jax.experimental import pallas as pl
from jax.experimental.pallas import tpu as pltpu

F32 = jnp.float32
BF16 = jnp.bfloat16
SDS = jax.ShapeDtypeStruct
MESH = pl.DeviceIdType.MESH

D_MODEL = 1024
D_FF = 2816
W_A = 512
W_B = 512
HA_HEADS, HA_DIM = 4, 128
HB_HEADS, HB_DIM = 8, 64
HGRN_CHUNK = 64
HGRN_GROUP = 2
RWKV_CHUNK = 16
RWKV_GROUP = 4
N_HGRN_COLS = 4 * W_A
N_RWKV_COLS = 3 * W_B + 32 + 32 + 96
N_RWKV_PAD = 1792
LORA_PAD = 256
NORM_EPS = 1e-6
RWKV_GN_EPS = 64e-5
L2_EPS = 1e-12
ADAM_LR, ADAM_B1, ADAM_B2, ADAM_EPS, ADAM_WD, ADAM_STEP = 0.001, 0.9, 0.999, 1e-8, 0.01, 10

N_CHIPS = 4
VMEM_LIMIT_V7X = 56 * 1024 * 1024
LANES = 1024

SHARDED_SHAPES = {
    "ffn1_w_gate": ((D_MODEL, D_FF), 1), "ffn1_w_up": ((D_MODEL, D_FF), 1), "ffn1_w_down": ((D_FF, D_MODEL), 0),
    "w_in": ((D_MODEL, N_HGRN_COLS + N_RWKV_COLS), 1), "rwkv_w2": ((32, W_B), 1), "rwkv_a2": ((32, W_B), 1),
    "rwkv_g2": ((96, W_B), 1), "w_out": ((D_MODEL, D_MODEL), 0),
    "ffn2_w_gate": ((D_MODEL, D_FF), 1), "ffn2_w_up": ((D_MODEL, D_FF), 1), "ffn2_w_down": ((D_FF, D_MODEL), 0),
}
SMALL = ("ffn1_norm", "mix_norm", "hgrn_lb_logits", "hgrn_out_norm", "rwkv_shift_mu", "rwkv_w0", "rwkv_a0",
         "rwkv_k_k", "rwkv_k_a", "rwkv_r_k", "rwkv_gn_w", "rwkv_gn_b", "ffn2_norm", "final_norm")
ALL_WEIGHTS = ("ffn1_norm", "ffn1_w_gate", "ffn1_w_up", "ffn1_w_down", "mix_norm", "w_in", "hgrn_lb_logits",
               "hgrn_out_norm", "rwkv_shift_mu", "rwkv_w0", "rwkv_w2", "rwkv_a0", "rwkv_a2", "rwkv_g2", "rwkv_k_k",
               "rwkv_k_a", "rwkv_r_k", "rwkv_gn_w", "rwkv_gn_b", "w_out", "ffn2_norm", "ffn2_w_gate", "ffn2_w_up",
               "ffn2_w_down", "final_norm")


def _shard_shape(name):
    shape, ax = SHARDED_SHAPES[name]
    return tuple(s // N_CHIPS if i == ax else s for i, s in enumerate(shape))


def _numel(shape):
    n = 1
    for s in shape:
        n *= s
    return n


def _params(sem=None):
    return pltpu.CompilerParams(dimension_semantics=sem, vmem_limit_bytes=VMEM_LIMIT_V7X)


def _split2(x):
    hi = x.astype(BF16)
    return hi, (x.astype(F32) - hi.astype(F32)).astype(BF16)


def _dg(x, y, cx, cy, hi):
    dn = (((cx,), (cy,)), ((), ()))
    if hi is True:
        return lax.dot_general(x.astype(F32), y.astype(F32), dn, precision=lax.Precision.HIGHEST,
                               preferred_element_type=F32)
    dot = lambda p, q: lax.dot_general(p, q, dn, preferred_element_type=F32)
    if hi == "x3":
        (xh, xl), (yh, yl) = _split2(x), _split2(y)
        return dot(xh, yh) + (dot(xh, yl) + dot(xl, yh))
    return dot(x.astype(BF16), y.astype(BF16))


def _make_mm(hi):
    @jax.custom_vjp
    def nn(x, y):
        return _dg(x, y, 1, 0, hi)

    @jax.custom_vjp
    def nt(x, y):
        return _dg(x, y, 1, 1, hi)

    @jax.custom_vjp
    def tn(x, y):
        return _dg(x, y, 0, 0, hi)

    nn.defvjp(lambda x, y: (nn(x, y), (x, y)), lambda r, g: (nt(g, r[1]), tn(r[0], g)))
    nt.defvjp(lambda x, y: (nt(x, y), (x, y)), lambda r, g: (nn(g, r[1]), tn(g, r[0])))
    tn.defvjp(lambda x, y: (tn(x, y), (x, y)), lambda r, g: (nt(r[1], g), nn(r[0], g)))
    return nn, nt, tn


_nn, _nt, _tn = _make_mm(False)
_nn_hi, _nt_hi, _tn_hi = _make_mm(True)
_nn_x3, _nt_x3, _tn_x3 = _make_mm("x3")


def _tri_apply(x, transpose):
    c = x.shape[0]
    tri = (lax.broadcasted_iota(jnp.int32, (c, c), 1) <= lax.broadcasted_iota(jnp.int32, (c, c), 0)).astype(BF16)
    dn = (((0 if transpose else 1,), (0,)), ((), ()))
    p1 = x.astype(BF16)
    r1 = x - p1.astype(F32)
    p2 = r1.astype(BF16)
    p3 = (r1 - p2.astype(F32)).astype(BF16)
    dot = lambda p: lax.dot_general(tri, p, dn, preferred_element_type=F32)
    return dot(p1) + (dot(p2) + dot(p3))


@jax.custom_vjp
def _cumsum_rows(x):
    return _tri_apply(x, False)


_cumsum_rows.defvjp(lambda x: (_tri_apply(x, False), None), lambda _, g: (_tri_apply(g, True),))


def _sigmoid(x):
    return 1.0 / (1.0 + jnp.exp(-x))


def _silu(x):
    return x * _sigmoid(x)


def _softplus(z):
    return jnp.maximum(z, 0.0) + jnp.log(1.0 + jnp.exp(-jnp.abs(z)))


def _mm(a, b, *, ta=False, tb=False, tm, tn, tk, name, out_dtype=F32, res=None, scale=None):
    m = a.shape[1] if ta else a.shape[0]
    kdim = a.shape[0] if ta else a.shape[1]
    n = b.shape[0] if tb else b.shape[1]
    assert (b.shape[1] if tb else b.shape[0]) == kdim
    tm, tn, tk = min(tm, m), min(tn, n), min(tk, kdim)
    assert m % tm == 0 and n % tn == 0 and kdim % tk == 0, (name, m, n, kdim)
    nk = kdim // tk
    a_spec = pl.BlockSpec((tk, tm), lambda i, j, k: (k, i)) if ta else pl.BlockSpec((tm, tk), lambda i, j, k: (i, k))
    b_spec = pl.BlockSpec((tn, tk), lambda i, j, k: (j, k)) if tb else pl.BlockSpec((tk, tn), lambda i, j, k: (k, j))
    o_spec = pl.BlockSpec((tm, tn), lambda i, j, k: (i, j))
    ca, cb = (0 if ta else 1), (1 if tb else 0)

    def body(*refs):
        if res is not None:
            a_ref, b_ref, r_ref, o_ref, acc_ref = refs
        else:
            a_ref, b_ref, o_ref, acc_ref = refs
        k = pl.program_id(2)

        @pl.when(k == 0)
        def _():
            acc_ref[...] = jnp.zeros_like(acc_ref)

        acc_ref[...] += _dg(a_ref[...], b_ref[...], ca, cb, False)

        @pl.when(k == nk - 1)
        def _():
            acc = acc_ref[...]
            if scale is not None:
                acc = acc * scale
            if res is not None:
                acc = r_ref[...] + acc
            o_ref[...] = acc.astype(out_dtype)

    in_specs = [a_spec, b_spec] + ([o_spec] if res is not None else [])
    args = (a, b) + ((res,) if res is not None else ())
    return pl.pallas_call(
        body, name=name, grid=(m // tm, n // tn, nk), in_specs=in_specs, out_specs=o_spec,
        out_shape=SDS((m, n), out_dtype), scratch_shapes=[pltpu.VMEM((tm, tn), F32)],
        compiler_params=_params(("parallel", "parallel", "arbitrary")))(*args)


def _row_spec(x, tm):
    if isinstance(x, tuple):
        arr, w, j = x
        return arr, pl.BlockSpec((tm, w), lambda i, j=j: (i, j))
    return x, pl.BlockSpec((tm, x.shape[1]), lambda i: (i, 0))


def _par_spec(p):
    if isinstance(p, tuple):
        arr, w, j = p
        return arr, pl.BlockSpec((arr.shape[0], w), lambda i, j=j: (0, j))
    return p, pl.BlockSpec(p.shape, lambda i: (0, 0))


def _store_groups(refs, groups, vals):
    for ref, idxs in zip(refs, groups):
        off = 0
        for ix in idxs:
            v = vals[ix]
            ref[:, off:off + v.shape[1]] = v.astype(ref.dtype)
            off += v.shape[1]


SUBLANES = 8


def _x_plan(xs, tm, t):
    arrays, specs, plan = [], [], []
    nb = tm // SUBLANES
    for x in xs:
        if isinstance(x, tuple) and isinstance(x[0], str):
            kind, arr, w, j = x
            if kind == "prev":
                halo = lambda i, j=j: (jnp.maximum(i * nb - 1, 0), j)
            else:
                halo = lambda i, j=j: (jnp.minimum((i + 1) * nb, t // SUBLANES - 1), j)
            arrays += [arr, arr]
            specs += [pl.BlockSpec((tm, w), lambda i, j=j: (i, j)), pl.BlockSpec((SUBLANES, w), halo)]
            plan.append((kind, 2, w))
        else:
            arr, spec = _row_spec(x, tm)
            arrays.append(arr)
            specs.append(spec)
            plan.append(("plain", 1, spec.block_shape[1]))
    return arrays, specs, plan


def _x_vals(refs, plan, tm, nt):
    vals, k = [], 0
    i = pl.program_id(0)
    rows = lax.broadcasted_iota(jnp.int32, (tm, 1), 0)
    for kind, n, _ in plan:
        main = refs[k][...].astype(F32)
        if kind == "prev":
            edge = jnp.where(i == 0, 0.0, refs[k + 1][SUBLANES - 1:SUBLANES, :].astype(F32))
            main = jnp.where(rows == 0, edge, pltpu.roll(main, 1, 0))
        elif kind == "next":
            edge = jnp.where(i == nt - 1, 0.0, refs[k + 1][0:1, :].astype(F32))
            main = jnp.where(rows == tm - 1, edge, pltpu.roll(main, tm - 1, 0))
        vals.append(main)
        k += n
    return vals


def _tile_rows(xs, tm):
    arr = xs[0]
    if isinstance(arr, tuple):
        arr = arr[1] if isinstance(arr[0], str) else arr[0]
    return min(tm, arr.shape[0]), arr.shape[0]


def _rowwise(f, xs, params, out_groups, out_dtypes, *, tm, name):
    tm, t = _tile_rows(xs, tm)
    nt = t // tm
    xa, xspecs, plan = _x_plan(xs, tm, t)
    pa, pspecs = (zip(*[_par_spec(p) for p in params]) if params else ((), ()))
    nxr, npar = len(xa), len(pa)
    x_sds = [SDS((tm, w), F32) for _, _, w in plan]
    p_sds = [SDS(s.block_shape, F32) for s in pspecs]
    outs_sds = jax.eval_shape(lambda *vals: f(*vals), *x_sds, *p_sds)
    widths = [sum(outs_sds[ix].shape[1] for ix in idxs) for idxs in out_groups]

    def body(*refs):
        vals = _x_vals(refs[:nxr], plan, tm, nt) + [r[...].astype(F32) for r in refs[nxr:nxr + npar]]
        outs = f(*vals)
        _store_groups(refs[nxr + npar:], out_groups, outs)

    return pl.pallas_call(
        body, name=name, grid=(nt,), in_specs=list(xspecs) + list(pspecs),
        out_specs=[pl.BlockSpec((tm, w), lambda i: (i, 0)) for w in widths],
        out_shape=[SDS((t, w), dt) for w, dt in zip(widths, out_dtypes)],
        compiler_params=_params(("parallel",)))(*xa, *pa)


def _rowwise_bwd(f, xs, params, cots, *, x_grad, p_grad, dx_groups, dx_dtypes, tm, name, extra=None):
    tm, t = _tile_rows(xs, tm)
    nt = t // tm
    xa, xspecs, plan = _x_plan(xs, tm, t)
    pa, pspecs = (zip(*[_par_spec(p) for p in params]) if params else ((), ()))
    ca, cspecs = zip(*[_row_spec(c, tm) for c in cots])
    extra = extra or {}
    ekeys = sorted(extra)
    ea, especs = (zip(*[_row_spec(extra[k], tm) for k in ekeys]) if ekeys else ((), ()))
    nx, nxr, npar, nc, ne = len(plan), len(xa), len(pa), len(ca), len(ea)
    gx = [i for i in range(nx) if x_grad[i]]
    gp = [i for i in range(npar) if p_grad[i]]
    widths = [sum(plan[gx[ix]][2] for ix in idxs) for idxs in dx_groups]
    ng = len(dx_groups)

    def body(*refs):
        ins = refs[:nxr + npar + nc + ne]
        outs = refs[nxr + npar + nc + ne:]
        vals = _x_vals(ins[:nxr], plan, tm, nt) + [r[...].astype(F32) for r in ins[nxr:nxr + npar]]
        cvals = tuple(r[...].astype(F32) for r in ins[nxr + npar:nxr + npar + nc])
        evals = [r[...].astype(F32) for r in ins[nxr + npar + nc:]]
        diff_idx = gx + [nx + i for i in gp]

        def g(*dargs):
            full = list(vals)
            for ix, v in zip(diff_idx, dargs):
                full[ix] = v
            return tuple(f(*full))

        _, vjp = jax.vjp(g, *[vals[ix] for ix in diff_idx])
        grads = vjp(cvals)
        dxs = list(grads[:len(gx)])
        for k, ev in zip(ekeys, evals):
            dxs[k] = dxs[k] + ev
        _store_groups(outs[:ng], dx_groups, dxs)
        i = pl.program_id(0)
        for ref, gval in zip(outs[ng:], grads[len(gx):]):
            @pl.when(i == 0)
            def _(ref=ref):
                ref[...] = jnp.zeros_like(ref)
            ref[...] += gval

    dp_specs = [pl.BlockSpec(pspecs[i].block_shape, lambda i: (0, 0)) for i in gp]
    dp_shapes = [SDS(pspecs[i].block_shape, F32) for i in gp]
    return pl.pallas_call(
        body, name=name, grid=(nt,), in_specs=list(xspecs) + list(pspecs) + list(cspecs) + list(especs),
        out_specs=[pl.BlockSpec((tm, w), lambda i: (i, 0)) for w in widths] + dp_specs,
        out_shape=[SDS((t, w), dt) for w, dt in zip(widths, dx_dtypes)] + dp_shapes,
        compiler_params=_params(("arbitrary",)))(*xa, *pa, *ca, *ea)


def _rms_f(x, g):
    return (x * lax.rsqrt(jnp.mean(x * x, axis=-1, keepdims=True) + NORM_EPS) * g,)


def _three_pieces(x):
    p1 = x.astype(BF16)
    r1 = x - p1.astype(F32)
    p2 = r1.astype(BF16)
    return p1, p2, (r1 - p2.astype(F32)).astype(BF16)


def _group_sum_impl(x, ones_bd):
    p1, p2, p3 = _three_pieces(x)
    dot = lambda p: lax.dot_general(p, ones_bd.astype(BF16), (((1,), (0,)), ((), ())), preferred_element_type=F32)
    return dot(p1) + (dot(p2) + dot(p3))


@jax.custom_vjp
def _group_sum(x, ones_bd):
    return _group_sum_impl(x, ones_bd)


_group_sum.defvjp(lambda x, o: (_group_sum_impl(x, o), o),
                  lambda o, g: (_group_sum_impl(g, o), jnp.zeros_like(o)))


def _rwkv_prep_f(r, k, v, lo, rp, kp, vp, lop, mu_r, mu_k, mu_v, mu_lo, w0, w2p, a0, a2p, g2p, k_k, k_a, ones_bd):
    r = r + mu_r * (rp - r)
    k = k + mu_k * (kp - k)
    v = v + mu_v * (vp - v)
    lo = lo + mu_lo * (lop - lo)
    w_log = -_softplus(-(w0 + _nn(jnp.tanh(lo), w2p))) - 0.5
    lw = -jnp.exp(w_log)
    a_g = _sigmoid(a0 + _nn(lo, a2p))
    g = _nn(_sigmoid(lo), g2p)
    kk = k * k_k
    kk = kk / jnp.maximum(jnp.sqrt(_group_sum(kk * kk, ones_bd)), L2_EPS)
    k2 = k * (1.0 + (a_g - 1.0) * k_a)
    return r, lw, k2, v, -kk, kk * a_g, g


def _rwkv_post_f(y, r, k2, v, g, r_k, gn_w, gn_b, ones_bd):
    inv_n = 1.0 / HB_DIM
    mean = _group_sum(y, ones_bd) * inv_n
    yc = y - mean
    var = _group_sum(yc * yc, ones_bd) * inv_n
    yn = yc * lax.rsqrt(var + RWKV_GN_EPS) * gn_w + gn_b
    bonus = _group_sum(r * k2 * r_k, ones_bd) * v
    return ((yn + bonus) * g,)


def _tri(c, strict=False):
    ii = lax.broadcasted_iota(jnp.int32, (c, c), 0)
    jj = lax.broadcasted_iota(jnp.int32, (c, c), 1)
    return (jj < ii) if strict else (jj <= ii)


def _hgrn_step(st0, q_a, f_a, i_a, g_a, l0, l1, onorm):
    nh, nj = len(q_a), len(q_a[0])
    c = q_a[0][0].shape[0]
    combos = [(j, h) for j in range(nj) for h in range(nh)]
    every = lambda fn: {q: fn(q) for q in combos}
    at_ = lambda d: (lambda q: d[q[1]][q[0]])
    qa_, fa_, ia_, ga_ = (at_(z) for z in (q_a, f_a, i_a, g_a))
    incl = _tri(c)
    rows = lax.broadcasted_iota(jnp.int32, (c, 1), 0)
    lb = []
    for h in range(nh):
        mx = jnp.maximum(l0[h], l1[h])
        e0, e1 = jnp.exp(l0[h] - mx), jnp.exp(l1[h] - mx)
        lb.append(e0 / (e0 + e1))
    forget = every(lambda q: lb[q[1]] + (1.0 - lb[q[1]]) * _sigmoid(fa_(q)))
    qs = every(lambda q: _silu(qa_(q)))
    kk = every(lambda q: 1.0 - forget[q])
    lf = every(lambda q: jnp.log(forget[q]))
    bcum = every(lambda q: _cumsum_rows(lf[q]))
    bref = every(lambda q: jnp.sum(jnp.where(rows <= c // 2, lf[q], 0.0), axis=0, keepdims=True))
    blast = every(lambda q: jnp.sum(lf[q], axis=0, keepdims=True))
    scores = every(lambda q: jnp.where(incl, _nt(qs[q] * jnp.exp(bcum[q] - bref[q]),
                                                 kk[q] * jnp.exp(bref[q] - bcum[q])), 0.0))
    intra = every(lambda q: _nn(scores[q], ia_(q)))
    qb = every(lambda q: qs[q] * jnp.exp(bcum[q]))
    upd = every(lambda q: _tn(ia_(q), kk[q] * jnp.exp(blast[q] - bcum[q])))
    dec = every(lambda q: jnp.exp(blast[q]))
    st = list(st0)
    o = {}
    for j in range(nj):
        for h in range(nh):
            o[(j, h)] = intra[(j, h)] + _nt(qb[(j, h)], st[h])
        st = [st[h] * dec[(j, h)] + upd[(j, h)] for h in range(nh)]
    out = every(lambda q: o[q] * lax.rsqrt(jnp.mean(o[q] * o[q], axis=-1, keepdims=True) + NORM_EPS)
                * onorm[q[1]] * _silu(ga_(q)))
    return [[out[(j, h)] for j in range(nj)] for h in range(nh)], st


def _hgrn_blocks(ref, nj, c):
    return [[ref[j * c:(j + 1) * c, h * HA_DIM:(h + 1) * HA_DIM] for j in range(nj)] for h in range(HA_HEADS)]


def _hgrn_cols(ref):
    return [ref[:, h * HA_DIM:(h + 1) * HA_DIM] for h in range(HA_HEADS)]


def _hgrn_fwd(p_h, l0, l1, onorm):
    t = p_h.shape[0]
    cc, nj = HGRN_CHUNK, HGRN_GROUP
    c = cc * nj
    n = t // c

    def body(q_ref, f_ref, i_ref, g_ref, l0_ref, l1_ref, on_ref, o_ref, hs_ref, st_ref):
        @pl.when(pl.program_id(0) == 0)
        def _():
            st_ref[...] = jnp.zeros_like(st_ref)

        hs_ref[0] = st_ref[...]
        o, st1 = _hgrn_step([st_ref[h] for h in range(HA_HEADS)],
                            *[_hgrn_blocks(ref, nj, cc) for ref in (q_ref, f_ref, i_ref, g_ref)],
                            _hgrn_cols(l0_ref), _hgrn_cols(l1_ref), _hgrn_cols(on_ref))
        for h in range(HA_HEADS):
            for j in range(nj):
                o_ref[j * cc:(j + 1) * cc, h * HA_DIM:(h + 1) * HA_DIM] = o[h][j]
            st_ref[h] = st1[h]

    col = lambda j: pl.BlockSpec((c, W_A), lambda i, j=j: (i, j))
    par = pl.BlockSpec((1, W_A), lambda i: (0, 0))
    return pl.pallas_call(
        body, name="hgrn_fwd", grid=(n,), in_specs=[col(0), col(1), col(2), col(3), par, par, par],
        out_specs=[pl.BlockSpec((c, W_A), lambda i: (i, 0)),
                   pl.BlockSpec((1, HA_HEADS, HA_DIM, HA_DIM), lambda i: (i, 0, 0, 0))],
        out_shape=[SDS((t, W_A), F32), SDS((n, HA_HEADS, HA_DIM, HA_DIM), F32)],
        scratch_shapes=[pltpu.VMEM((HA_HEADS, HA_DIM, HA_DIM), F32)],
        compiler_params=_params(("arbitrary",)))(p_h, p_h, p_h, p_h, l0, l1, onorm)


def _hgrn_bwd(p_h, l0, l1, onorm, hs, do, do_col):
    t = p_h.shape[0]
    cc, nj = HGRN_CHUNK, HGRN_GROUP
    c = cc * nj
    n = t // c

    def body(q_ref, f_ref, i_ref, g_ref, l0_ref, l1_ref, on_ref, hs_ref, do_ref,
             dp_ref, dl0_ref, dl1_ref, don_ref, dst_ref):
        @pl.when(pl.program_id(0) == 0)
        def _():
            dst_ref[...] = jnp.zeros_like(dst_ref)
            dl0_ref[...] = jnp.zeros_like(dl0_ref)
            dl1_ref[...] = jnp.zeros_like(dl1_ref)
            don_ref[...] = jnp.zeros_like(don_ref)

        args = ([hs_ref[0, h] for h in range(HA_HEADS)],
                *[_hgrn_blocks(ref, nj, cc) for ref in (q_ref, f_ref, i_ref, g_ref)],
                _hgrn_cols(l0_ref), _hgrn_cols(l1_ref), _hgrn_cols(on_ref))
        _, vjp = jax.vjp(_hgrn_step, *args)
        dst0, dq, df, di, dg, dl0, dl1, don = vjp((_hgrn_blocks(do_ref, nj, cc),
                                                   [dst_ref[h] for h in range(HA_HEADS)]))
        for h in range(HA_HEADS):
            sl = slice(h * HA_DIM, (h + 1) * HA_DIM)
            for k, dv in enumerate((dq, df, di, dg)):
                for j in range(nj):
                    dp_ref[j * cc:(j + 1) * cc, k * W_A + h * HA_DIM:k * W_A + (h + 1) * HA_DIM] = dv[h][j]
            dl0_ref[:, sl] += dl0[h]
            dl1_ref[:, sl] += dl1[h]
            don_ref[:, sl] += don[h]
            dst_ref[h] = dst0[h]

    col = lambda j: pl.BlockSpec((c, W_A), lambda i, j=j: (n - 1 - i, j))
    par = pl.BlockSpec((1, W_A), lambda i: (0, 0))
    return pl.pallas_call(
        body, name="hgrn_bwd", grid=(n,),
        in_specs=[col(0), col(1), col(2), col(3), par, par, par,
                  pl.BlockSpec((1, HA_HEADS, HA_DIM, HA_DIM), lambda i: (n - 1 - i, 0, 0, 0)),
                  pl.BlockSpec((c, W_A), lambda i: (n - 1 - i, do_col))],
        out_specs=[pl.BlockSpec((c, N_HGRN_COLS), lambda i: (n - 1 - i, 0)), par, par, par],
        out_shape=[SDS((t, N_HGRN_COLS), F32), SDS((1, W_A), F32), SDS((1, W_A), F32), SDS((1, W_A), F32)],
        scratch_shapes=[pltpu.VMEM((HA_HEADS, HA_DIM, HA_DIM), F32)],
        compiler_params=_params(("arbitrary",)))(p_h, p_h, p_h, p_h, l0, l1, onorm, hs, do)


HB_PAIRS = HB_HEADS // 2
PAIR_W = 2 * HB_DIM


def _head_lane_masks():
    lane = lax.broadcasted_iota(jnp.int32, (1, PAIR_W), 1)
    return (lane < HB_DIM).astype(F32), (lane >= HB_DIM).astype(F32)


@jax.custom_vjp
def _stack_heads(x):
    m0, m1 = _head_lane_masks()
    return jnp.concatenate([x * m0, x * m1], axis=0)


def _stack_heads_bwd(_, g):
    m0, m1 = _head_lane_masks()
    c = g.shape[0] // 2
    return (g[:c] * m0 + g[c:] * m1,)


_stack_heads.defvjp(lambda x: (_stack_heads(x), None), _stack_heads_bwd)


@jax.custom_vjp
def _unstack_heads(ys):
    c = ys.shape[0] // 2
    return ys[:c] + ys[c:]


_unstack_heads.defvjp(lambda ys: (_unstack_heads(ys), None), lambda _, g: (_stack_heads(g),))


def _same_head_block(c):
    ii = lax.broadcasted_iota(jnp.int32, (2 * c, 2 * c), 0)
    jj = lax.broadcasted_iota(jnp.int32, (2 * c, 2 * c), 1)
    same = (ii < c) == (jj < c)
    return same & (jj <= ii), same & (jj < ii), (ii == jj).astype(F32)


def _rwkv_step(s0, r, lw, k, v, a, b):
    npair, nj = len(r), len(r[0])
    c = r[0][0].shape[0]
    combos = [(j, p) for j in range(nj) for p in range(npair)]
    every = lambda fn: {q: fn(q) for q in combos}
    at_ = lambda d: (lambda q: d[q[1]][q[0]])
    r_, lw_, k_, v_, a_, b_ = (at_(z) for z in (r, lw, k, v, a, b))
    incl, strict, eye = _same_head_block(c)

    gam = every(lambda q: _cumsum_rows(lw_(q)))
    gtot = every(lambda q: jnp.sum(lw_(q), axis=0, keepdims=True))
    eneg = every(lambda q: jnp.exp(-gam[q]))
    edec = every(lambda q: jnp.exp(gtot[q] - gam[q]))
    at = every(lambda q: _stack_heads(a_(q) * jnp.exp(gam[q] - lw_(q))))
    rt = every(lambda q: _stack_heads(r_(q) * jnp.exp(gam[q])))
    bt = every(lambda q: _stack_heads(b_(q) * eneg[q]))
    kt = every(lambda q: _stack_heads(k_(q) * eneg[q]))
    bdec = every(lambda q: _stack_heads(b_(q) * edec[q]))
    kdec = every(lambda q: _stack_heads(k_(q) * edec[q]))
    vs = every(lambda q: _stack_heads(v_(q)))
    a_ab = every(lambda q: jnp.where(strict, _nt(at[q], bt[q]), 0.0))
    a_ak = every(lambda q: jnp.where(strict, _nt(at[q], kt[q]), 0.0))
    a_rb = every(lambda q: jnp.where(incl, _nt(rt[q], bt[q]), 0.0))
    a_rk = every(lambda q: jnp.where(incl, _nt(rt[q], kt[q]), 0.0))
    tinv = every(lambda q: eye + a_ab[q])
    pw = a_ab
    span = 2
    while span < c:
        pw = every(lambda q, pw=pw: _nn_x3(pw[q], pw[q]))
        tinv = every(lambda q, pw=pw, tinv=tinv: tinv[q] + _nn_x3(pw[q], tinv[q]))
        span *= 2
    akv = every(lambda q: _nn(a_ak[q], vs[q]))
    w1 = every(lambda q: _nn_x3(tinv[q], at[q]))
    u0 = every(lambda q: _nn_x3(tinv[q], akv[q]))
    r1 = every(lambda q: rt[q] + _nn(a_rb[q], w1[q]))
    y0 = every(lambda q: _nn(a_rb[q], u0[q]) + _nn(a_rk[q], vs[q]))
    mm = every(lambda q: _tn(w1[q], bdec[q]))
    zz = every(lambda q: _tn(u0[q], bdec[q]) + _tn(vs[q], kdec[q]))
    gdec = every(lambda q: jnp.exp(gtot[q]))

    s = list(s0)
    y = [[None] * nj for _ in range(npair)]
    for j in range(nj):
        for p in range(npair):
            y[p][j] = _unstack_heads(_nt(r1[(j, p)], s[p]) + y0[(j, p)])
        s = [s[p] * gdec[(j, p)] + _nn(s[p], mm[(j, p)]) + zz[(j, p)] for p in range(npair)]
    return y, s


def _rwkv_blocks(ref, nj, c):
    return [[ref[j * c:(j + 1) * c, p * PAIR_W:(p + 1) * PAIR_W] for j in range(nj)] for p in range(HB_PAIRS)]


def _rwkv_fwd(seqs):
    t = seqs[0].shape[0]
    c, nj = RWKV_CHUNK, RWKV_GROUP
    n = t // (c * nj)

    def body(r_ref, lw_ref, k_ref, v_ref, a_ref, b_ref, y_ref, hs_ref, st_ref):
        @pl.when(pl.program_id(0) == 0)
        def _():
            st_ref[...] = jnp.zeros_like(st_ref)

        hs_ref[0] = st_ref[...]
        s0 = [st_ref[p] for p in range(HB_PAIRS)]
        y, s1 = _rwkv_step(s0, *[_rwkv_blocks(ref, nj, c) for ref in (r_ref, lw_ref, k_ref, v_ref, a_ref, b_ref)])
        for p in range(HB_PAIRS):
            for j in range(nj):
                y_ref[j * c:(j + 1) * c, p * PAIR_W:(p + 1) * PAIR_W] = y[p][j]
            st_ref[p] = s1[p]

    seq = pl.BlockSpec((c * nj, W_B), lambda i: (i, 0))
    return pl.pallas_call(
        body, name="rwkv_fwd", grid=(n,), in_specs=[seq] * 6,
        out_specs=[seq, pl.BlockSpec((1, HB_PAIRS, PAIR_W, PAIR_W), lambda i: (i, 0, 0, 0))],
        out_shape=[SDS((t, W_B), F32), SDS((n, HB_PAIRS, PAIR_W, PAIR_W), F32)],
        scratch_shapes=[pltpu.VMEM((HB_PAIRS, PAIR_W, PAIR_W), F32)],
        compiler_params=_params(("arbitrary",)))(*seqs)


def _rwkv_bwd(seqs, hs, dy):
    t = seqs[0].shape[0]
    c, nj = RWKV_CHUNK, RWKV_GROUP
    n = t // (c * nj)

    def body(r_ref, lw_ref, k_ref, v_ref, a_ref, b_ref, hs_ref, dy_ref,
             dr_ref, dlw_ref, dk_ref, dv_ref, da_ref, db_ref, dst_ref):
        @pl.when(pl.program_id(0) == 0)
        def _():
            dst_ref[...] = jnp.zeros_like(dst_ref)

        s0 = [hs_ref[0, p] for p in range(HB_PAIRS)]
        seq_vals = [_rwkv_blocks(ref, nj, c) for ref in (r_ref, lw_ref, k_ref, v_ref, a_ref, b_ref)]
        _, vjp = jax.vjp(_rwkv_step, s0, *seq_vals)
        grads = vjp((_rwkv_blocks(dy_ref, nj, c), [dst_ref[p] for p in range(HB_PAIRS)]))
        for ref, gr in zip((dr_ref, dlw_ref, dk_ref, dv_ref, da_ref, db_ref), grads[1:]):
            for p in range(HB_PAIRS):
                for j in range(nj):
                    ref[j * c:(j + 1) * c, p * PAIR_W:(p + 1) * PAIR_W] = gr[p][j]
        m0, m1 = _head_lane_masks()
        rows0 = (lax.broadcasted_iota(jnp.int32, (PAIR_W, 1), 0) < HB_DIM).astype(F32)
        blocks = rows0 * m0 + (1.0 - rows0) * m1
        for p in range(HB_PAIRS):
            dst_ref[p] = grads[0][p] * blocks

    seq = pl.BlockSpec((c * nj, W_B), lambda i: (n - 1 - i, 0))
    return pl.pallas_call(
        body, name="rwkv_bwd", grid=(n,),
        in_specs=[seq] * 6 + [pl.BlockSpec((1, HB_PAIRS, PAIR_W, PAIR_W), lambda i: (n - 1 - i, 0, 0, 0)), seq],
        out_specs=[seq] * 6, out_shape=[SDS((t, W_B), F32)] * 6,
        scratch_shapes=[pltpu.VMEM((HB_PAIRS, PAIR_W, PAIR_W), F32)],
        compiler_params=_params(("arbitrary",)))(*seqs, hs, dy)


def _final_loss(x3, fnorm, target, *, tm):
    t, d = x3.shape

    def body(x_ref, g_ref, t_ref, dx_ref, dg_ref, loss_ref):
        @pl.when(pl.program_id(0) == 0)
        def _():
            dg_ref[...] = jnp.zeros_like(dg_ref)
            loss_ref[...] = jnp.zeros_like(loss_ref)

        x, g = x_ref[...], g_ref[...]
        rinv = lax.rsqrt(jnp.mean(x * x, axis=-1, keepdims=True) + NORM_EPS)
        xh = x * rinv
        diff = xh * g - t_ref[...]
        loss_ref[...] += 0.5 * jnp.sum(jnp.mean(diff * diff, axis=-1, keepdims=True))
        dy = diff * (1.0 / d)
        dg_ref[...] += jnp.sum(dy * xh, axis=0, keepdims=True)
        dxh = dy * g
        dx_ref[...] = rinv * (dxh - xh * jnp.mean(dxh * xh, axis=-1, keepdims=True))

    row = pl.BlockSpec((tm, d), lambda i: (i, 0))
    return pl.pallas_call(
        body, name="final_loss", grid=(t // tm,), in_specs=[row, pl.BlockSpec((1, d), lambda i: (0, 0)), row],
        out_specs=[row, pl.BlockSpec((1, d), lambda i: (0, 0)), pl.BlockSpec((8, 128), lambda i: (0, 0))],
        out_shape=[SDS((t, d), F32), SDS((1, d), F32), SDS((8, 128), F32)],
        compiler_params=_params(("arbitrary",)))(x3, fnorm, target)


def _gate_up_act(h, wgt, wut, *, tm, tn, name):
    t, d = h.shape
    tm = min(tm, t)

    def body(h_ref, g_ref, u_ref, a_out, u_out, act_out):
        hv = h_ref[...]
        a = _dg(hv, g_ref[...], 1, 1, False)
        u = _dg(hv, u_ref[...], 1, 1, False)
        a_out[...] = a
        u_out[...] = u
        act_out[...] = (_silu(a) * u).astype(act_out.dtype)

    wspec = pl.BlockSpec((tn, d), lambda i, j: (j, 0))
    ospec = pl.BlockSpec((tm, tn), lambda i, j: (i, j))
    return pl.pallas_call(
        body, name=name, grid=(t // tm, D_FF // tn), in_specs=[pl.BlockSpec((tm, d), lambda i, j: (i, 0)), wspec, wspec],
        out_specs=[ospec, ospec, ospec], out_shape=[SDS((t, D_FF), F32), SDS((t, D_FF), F32), SDS((t, D_FF), BF16)],
        compiler_params=_params(("parallel", "parallel")))(h, wgt, wut)


def _dact_swiglu(dout, wd, a, u, *, tm, tn, name):
    t, d = dout.shape
    tm = min(tm, t)

    def body(d_ref, w_ref, a_ref, u_ref, da_out, du_out):
        dact = 0.5 * _dg(d_ref[...], w_ref[...], 1, 1, False)
        av, uv = a_ref[...], u_ref[...]
        s = _sigmoid(av)
        da_out[...] = (dact * uv * (s * (1.0 + av * (1.0 - s)))).astype(da_out.dtype)
        du_out[...] = (dact * (av * s)).astype(du_out.dtype)

    tile = pl.BlockSpec((tm, tn), lambda i, j: (i, j))
    return pl.pallas_call(
        body, name=name, grid=(t // tm, D_FF // tn),
        in_specs=[pl.BlockSpec((tm, d), lambda i, j: (i, 0)), pl.BlockSpec((tn, d), lambda i, j: (j, 0)), tile, tile],
        out_specs=[tile, tile], out_shape=[SDS((t, D_FF), BF16), SDS((t, D_FF), BF16)],
        compiler_params=_params(("parallel", "parallel")))(dout, wd, a, u)


def _ffn_fwd(x, norm, wgt, wut, wd, tag):
    h, = _rowwise(_rms_f, [x], [norm], [[0]], [BF16], tm=512, name=f"{tag}_rms")
    a, u, act = _gate_up_act(h, wgt, wut, tm=512, tn=D_FF // 2, name=f"{tag}_gate_up")
    out = _mm(act, wd, tm=512, tn=D_MODEL, tk=D_FF // 2, name=f"{tag}_down", res=x, scale=0.5)
    return out, (h, a, u, act)


def _ffn_bwd(dout, x, norm, wgt, wut, wd, saved, tag):
    h, a, u, act = saved
    da, du = _dact_swiglu(dout, wd, a, u, tm=512, tn=D_FF // 2, name=f"{tag}_dact")
    dwd = _mm(act, dout, ta=True, tm=D_FF // 2, tn=D_MODEL, tk=512, name=f"{tag}_dwd", scale=0.5)
    dwgt = _mm(da, h, ta=True, tm=D_FF // 2, tn=D_MODEL, tk=512, name=f"{tag}_dwg")
    dwut = _mm(du, h, ta=True, tm=D_FF // 2, tn=D_MODEL, tk=512, name=f"{tag}_dwu")
    dh = _mm(da, wgt, tm=512, tn=D_MODEL, tk=D_FF // 2, name=f"{tag}_dh_g")
    dh = _mm(du, wut, tm=512, tn=D_MODEL, tk=D_FF // 2, name=f"{tag}_dh_u", res=dh)
    dx, dnorm = _rowwise_bwd(_rms_f, [x], [norm], [dh], x_grad=[True], p_grad=[True], dx_groups=[[0]],
                             dx_dtypes=[F32], tm=256, name=f"{tag}_drms", extra={0: dout})
    return dx, dnorm, dwgt, dwut, dwd


def _local_step(x, target, w):
    ones_bd = jnp.kron(jnp.eye(HB_HEADS, dtype=F32), jnp.ones((HB_DIM, HB_DIM), F32))
    g = {}
    x1, ffn1_saved = _ffn_fwd(x, w["ffn1_norm"], w["ffn1_wgt"], w["ffn1_wut"], w["ffn1_wd"], "ffn1")
    hm, = _rowwise(_rms_f, [x1], [w["mix_norm"]], [[0]], [BF16], tm=512, name="mix_rms")
    p_h = _mm(hm, w["w_in_h"], tm=512, tn=512, tk=D_MODEL, name="inproj_h")
    p_r = _mm(hm, w["w_in_r"], tm=512, tn=N_RWKV_PAD // 2, tk=D_MODEL, name="inproj_r")
    o_a, hgrn_states = _hgrn_fwd(p_h, w["lb0"], w["lb1"], w["hgrn_out_norm"])

    mu = w["mu_pad"]
    prep_xs = [(p_r, W_B, 0), (p_r, W_B, 1), (p_r, W_B, 2), (p_r, LORA_PAD, 6),
               ("prev", p_r, W_B, 0), ("prev", p_r, W_B, 1), ("prev", p_r, W_B, 2), ("prev", p_r, LORA_PAD, 6)]
    prep_ps = [(mu, W_B, 0), (mu, W_B, 1), (mu, W_B, 2), (mu, LORA_PAD, 6), w["rwkv_w0"], w["w2_pad"], w["rwkv_a0"],
               w["a2_pad"], w["g2_pad"], w["rwkv_k_k"], w["rwkv_k_a"], ones_bd]
    prep_f = _rwkv_prep_f
    r, lw, k2, v, a_vec, b_vec, gate = _rowwise(prep_f, prep_xs, prep_ps, [[0], [1], [2], [3], [4], [5], [6]],
                                                [F32] * 7, tm=256, name="rwkv_prep")
    seqs = [r, lw, k2, v, a_vec, b_vec]
    y, rwkv_states = _rwkv_fwd(seqs)
    post_f = _rwkv_post_f
    post_xs = [y, r, k2, v, gate]
    post_ps = [w["rwkv_r_k"], w["rwkv_gn_w"], w["rwkv_gn_b"], ones_bd]
    o_b, = _rowwise(post_f, post_xs, post_ps, [[0]], [F32], tm=256, name="rwkv_post")
    x2 = _mm(o_a, w["w_out_a"], tm=512, tn=D_MODEL, tk=W_A, name="outproj_a", res=x1)
    x2 = _mm(o_b, w["w_out_b"], tm=512, tn=D_MODEL, tk=W_B, name="outproj_b", res=x2)
    x3, ffn2_saved = _ffn_fwd(x2, w["ffn2_norm"], w["ffn2_wgt"], w["ffn2_wut"], w["ffn2_wd"], "ffn2")
    dx3, g["final_norm"], loss = _final_loss(x3, w["final_norm"], target, tm=256)

    dx2, g["ffn2_norm"], g["ffn2_wgt"], g["ffn2_wut"], g["ffn2_wd"] = _ffn_bwd(
        dx3, x2, w["ffn2_norm"], w["ffn2_wgt"], w["ffn2_wut"], w["ffn2_wd"], ffn2_saved, "ffn2")
    do_a = _mm(dx2, w["w_out_a"], tb=True, tm=512, tn=W_A, tk=D_MODEL, name="outproj_do_a")
    do_b = _mm(dx2, w["w_out_b"], tb=True, tm=512, tn=W_B, tk=D_MODEL, name="outproj_do_b")
    g["w_out_a"] = _mm(o_a, dx2, ta=True, tm=W_A, tn=D_MODEL, tk=512, name="outproj_dw_a")
    g["w_out_b"] = _mm(o_b, dx2, ta=True, tm=W_B, tn=D_MODEL, tk=512, name="outproj_dw_b")

    dp_h, g["lb0"], g["lb1"], g["hgrn_out_norm"] = _hgrn_bwd(p_h, w["lb0"], w["lb1"], w["hgrn_out_norm"],
                                                             hgrn_states, do_a, 0)
    post_out = _rowwise_bwd(post_f, post_xs, post_ps, [do_b], x_grad=[True] * 5, p_grad=[True] * 3 + [False],
                            dx_groups=[[0], [1], [2], [3], [4]], dx_dtypes=[F32] * 5, tm=256, name="rwkv_post_bwd")
    dy, dr1, dk1, dv1, dgate, g["rwkv_r_k"], g["rwkv_gn_w"], g["rwkv_gn_b"] = post_out
    dr2, dlw, dk2, dv2, da_vec, db_vec = _rwkv_bwd(seqs, rwkv_states, dy)

    def prep2_f(*vals):
        r_, lw_, k2_, v_, a_, b_, g_ = prep_f(*vals)
        return r_, lw_, k2_, v_, a_, b_, g_, r_, k2_, v_

    prep_out = _rowwise_bwd(prep2_f, prep_xs, prep_ps, [dr2, dlw, dk2, dv2, da_vec, db_vec, dgate, dr1, dk1, dv1],
                            x_grad=[True] * 8, p_grad=[True] * 11 + [False], dx_groups=[[0, 1, 2, 3], [4, 5, 6, 7]],
                            dx_dtypes=[F32, F32], tm=256, name="rwkv_prep_bwd")
    dpr_main, dpr_prev = prep_out[0], prep_out[1]
    (dmu_r, dmu_k, dmu_v, dmu_lo, g["rwkv_w0"], g["w2_pad"], g["rwkv_a0"], g["a2_pad"], g["g2_pad"],
     g["rwkv_k_k"], g["rwkv_k_a"]) = prep_out[2:]
    g["mu_pad"] = jnp.concatenate([dmu_r, dmu_k, dmu_v, dmu_lo], axis=1)
    dp_r, = _rowwise(lambda u_, s_: (u_ + s_,), [dpr_main, ("next", dpr_prev, N_RWKV_PAD, 0)], [], [[0]], [F32],
                     tm=512, name="rwkv_dp_sum")
    dhm = _mm(dp_h, w["w_in_h"], tb=True, tm=512, tn=D_MODEL, tk=D_MODEL, name="inproj_dh_h")
    dhm = _mm(dp_r, w["w_in_r"], tb=True, tm=512, tn=D_MODEL, tk=N_RWKV_PAD // 2, name="inproj_dh_r", res=dhm)
    g["w_in_h"] = _mm(hm, dp_h, ta=True, tm=D_MODEL, tn=D_MODEL, tk=512, name="inproj_dw_h")
    g["w_in_r"] = _mm(hm, dp_r, ta=True, tm=D_MODEL, tn=N_RWKV_PAD // 2, tk=512, name="inproj_dw_r")
    dx1, g["mix_norm"] = _rowwise_bwd(_rms_f, [x1], [w["mix_norm"]], [dhm], x_grad=[True], p_grad=[True],
                                      dx_groups=[[0]], dx_dtypes=[F32], tm=256, name="mix_drms", extra={0: dx2})
    dx0, g["ffn1_norm"], g["ffn1_wgt"], g["ffn1_wut"], g["ffn1_wd"] = _ffn_bwd(
        dx1, x, w["ffn1_norm"], w["ffn1_wgt"], w["ffn1_wut"], w["ffn1_wd"], ffn1_saved, "ffn1")
    return loss, dx0, g


HBM_SPEC = pl.BlockSpec(memory_space=pl.ANY)


def _chips(x, y):
    return [(1 - x, y), (x, 1 - y), (1 - x, 1 - y)]


def _gather_weights(bufs):
    n = len(bufs)

    def body(*refs):
        outs = refs[n:2 * n]
        ici_send, ici_recv, d2d_send, d2d_recv = refs[2 * n:]
        x, y, c = lax.axis_index("x"), lax.axis_index("y"), lax.axis_index("c")
        me = 2 * x + y

        def half(t, slot, hc):
            hr = bufs[t].shape[1] // 2
            return outs[t].at[slot, pl.ds(pl.multiple_of(hc * hr, 16), hr), :]

        def ici(t, j, slot, px, py):
            return pltpu.make_async_remote_copy(src_ref=half(t, slot, c), dst_ref=half(t, slot, c),
                                                send_sem=ici_send.at[3 * t + j], recv_sem=ici_recv.at[3 * t + j],
                                                device_id=(px, py, c), device_id_type=MESH)

        def d2d(t, j, slot, hc):
            return pltpu.make_async_remote_copy(src_ref=half(t, slot, hc), dst_ref=half(t, slot, hc),
                                                send_sem=d2d_send.at[3 * t + j], recv_sem=d2d_recv.at[3 * t + j],
                                                device_id=(x, y, 1 - c), device_id_type=MESH)

        sends = [ici(t, j, me, px, py) for t in range(n) for j, (px, py) in enumerate(_chips(x, y))]
        for cp in sends:
            cp.start()
        passed = []
        for t in range(n):
            for j, (px, py) in enumerate(_chips(x, y)):
                ici(t, j, 2 * px + py, px, py).wait_recv()
                cp = d2d(t, j, 2 * px + py, c)
                cp.start()
                passed.append(cp)
        for t in range(n):
            for j, (px, py) in enumerate(_chips(x, y)):
                d2d(t, j, 2 * px + py, 1 - c).wait_recv()
        for cp in sends + passed:
            cp.wait_send()

    return pl.pallas_call(
        body, name="gather_weights", in_specs=[HBM_SPEC] * n, out_specs=[HBM_SPEC] * n,
        out_shape=[SDS(b.shape, b.dtype) for b in bufs], input_output_aliases={t: t for t in range(n)},
        scratch_shapes=[pltpu.SemaphoreType.DMA((3 * n,))] * 4,
    )(*bufs)


def _sibling_exchange(gs):
    n = len(gs)

    def body(*refs):
        ins, outs = refs[:n], refs[n:2 * n]
        send_sems, recv_sems = refs[2 * n:]
        x, y, c = lax.axis_index("x"), lax.axis_index("y"), lax.axis_index("c")
        cps = []
        for t in range(n):
            hr = gs[t].shape[1] // 2
            src = ins[t].at[:, pl.ds(pl.multiple_of((1 - c) * hr, SUBLANES), hr), :]
            cps.append(pltpu.make_async_remote_copy(src_ref=src, dst_ref=outs[t], send_sem=send_sems.at[t],
                                                    recv_sem=recv_sems.at[t], device_id=(x, y, 1 - c),
                                                    device_id_type=MESH))
        for cp in cps:
            cp.start()
        for cp in cps:
            cp.wait()

    return pl.pallas_call(
        body, name="grad_sibling_exchange", in_specs=[HBM_SPEC] * n, out_specs=[HBM_SPEC] * n,
        out_shape=[SDS((N_CHIPS, g.shape[1] // 2, g.shape[2]), g.dtype) for g in gs],
        scratch_shapes=[pltpu.SemaphoreType.DMA((n,)), pltpu.SemaphoreType.DMA((n,))],
    )(*gs)


def _chip_exchange(ss):
    n = len(ss)

    def body(*refs):
        ins, outs = refs[:n], refs[n:2 * n]
        send_sems, recv_sems = refs[2 * n:]
        x, y, c = lax.axis_index("x"), lax.axis_index("y"), lax.axis_index("c")
        me = 2 * x + y

        def copy(t, j, px, py, src_slot, dst_slot):
            return pltpu.make_async_remote_copy(src_ref=ins[t].at[src_slot], dst_ref=outs[t].at[dst_slot],
                                                send_sem=send_sems.at[3 * t + j], recv_sem=recv_sems.at[3 * t + j],
                                                device_id=(px, py, c), device_id_type=MESH)

        sends = [copy(t, j, px, py, 2 * px + py, me) for t in range(n) for j, (px, py) in enumerate(_chips(x, y))]
        for cp in sends:
            cp.start()
        for t in range(n):
            for j, (px, py) in enumerate(_chips(x, y)):
                copy(t, j, px, py, me, 2 * px + py).wait_recv()
        for cp in sends:
            cp.wait_send()

    return pl.pallas_call(
        body, name="grad_chip_exchange", in_specs=[HBM_SPEC] * n, out_specs=[HBM_SPEC] * n,
        out_shape=[SDS(s.shape, s.dtype) for s in ss],
        scratch_shapes=[pltpu.SemaphoreType.DMA((3 * n,)), pltpu.SemaphoreType.DMA((3 * n,))],
    )(*ss)


def _sibling_swap(fs):
    n = len(fs)

    def body(*refs):
        ins, outs = refs[:n], refs[n:2 * n]
        send_sems, recv_sems = refs[2 * n:]
        x, y, c = lax.axis_index("x"), lax.axis_index("y"), lax.axis_index("c")
        cps = [pltpu.make_async_remote_copy(src_ref=ins[t], dst_ref=outs[t], send_sem=send_sems.at[t],
                                            recv_sem=recv_sems.at[t], device_id=(x, y, 1 - c), device_id_type=MESH)
               for t in range(n)]
        for cp in cps:
            cp.start()
        for cp in cps:
            cp.wait()

    return pl.pallas_call(
        body, name="grad_sibling_swap", in_specs=[HBM_SPEC] * n, out_specs=[HBM_SPEC] * n,
        out_shape=[SDS(f.shape, f.dtype) for f in fs],
        scratch_shapes=[pltpu.SemaphoreType.DMA((n,)), pltpu.SemaphoreType.DMA((n,))],
    )(*fs)


def _row_tile(rows, cap=512):
    best = SUBLANES
    for tr in range(SUBLANES, min(rows, cap) + 1, SUBLANES):
        if rows % tr == 0:
            best = tr
    return best


def _add_halves(g4, r4, c_idx, name):
    _, hr, lanes = r4.shape
    tr = _row_tile(hr)
    nb = hr // tr

    def body(c_ref, a_ref, b_ref, o_ref):
        o_ref[...] = (a_ref[...] + b_ref[...]).astype(o_ref.dtype)

    grid_spec = pltpu.PrefetchScalarGridSpec(
        num_scalar_prefetch=1, grid=(N_CHIPS, nb),
        in_specs=[pl.BlockSpec((None, tr, lanes), lambda q, i, c_ref: (q, c_ref[0] * nb + i, 0)),
                  pl.BlockSpec((None, tr, lanes), lambda q, i, c_ref: (q, i, 0))],
        out_specs=pl.BlockSpec((None, tr, lanes), lambda q, i, c_ref: (q, i, 0)))
    return pl.pallas_call(body, name=name, grid_spec=grid_spec, out_shape=SDS(r4.shape, BF16),
                          compiler_params=_params(("parallel", "parallel")))(c_idx, g4, r4)


def _sum_chips(r4, s4, me_idx, name):
    _, rows, lanes = r4.shape
    tr = _row_tile(rows)

    def body(me_ref, a_ref, b_ref, c_ref, d_ref, own_ref, o_ref):
        own = own_ref[...].astype(F32)
        p = [jnp.where(me_ref[0] == q, own, ref[...].astype(F32)) for q, ref in enumerate((a_ref, b_ref, c_ref, d_ref))]
        o_ref[...] = ((p[0] + p[1]) + p[2]) + p[3]

    other = lambda q: (lambda i, me_ref: (jnp.where(me_ref[0] == q, (q + 1) % N_CHIPS, q), i, 0))
    grid_spec = pltpu.PrefetchScalarGridSpec(
        num_scalar_prefetch=1, grid=(rows // tr,),
        in_specs=[pl.BlockSpec((None, tr, lanes), other(q)) for q in range(N_CHIPS)]
        + [pl.BlockSpec((None, tr, lanes), lambda i, me_ref: (me_ref[0], i, 0))],
        out_specs=pl.BlockSpec((tr, lanes), lambda i, me_ref: (i, 0)))
    return pl.pallas_call(body, name=name, grid_spec=grid_spec, out_shape=SDS((rows, lanes), F32),
                          compiler_params=_params(("parallel",)))(me_idx, r4, r4, r4, r4, s4)


def _adamw(wf, g_own, g_other, mf, vf, c_idx, name):
    rows, lanes = wf.shape
    hr = rows // 2
    tr = _row_tile(hr)
    nb = hr // tr
    c1 = 1.0 / (1.0 - ADAM_B1 ** ADAM_STEP)
    c2 = 1.0 / (1.0 - ADAM_B2 ** ADAM_STEP)

    def body(c_ref, w_ref, go_ref, gx_ref, m_ref, v_ref, g_ref, d_ref, nm_ref, nv_ref):
        gv = jnp.where(pl.program_id(0) == c_ref[0], go_ref[...], gx_ref[...])
        m = ADAM_B1 * m_ref[...] + (1.0 - ADAM_B1) * gv
        v = ADAM_B2 * v_ref[...] + (1.0 - ADAM_B2) * (gv * gv)
        g_ref[...] = gv
        d_ref[...] = -ADAM_LR * ((m * c1) / (jnp.sqrt(v * c2) + ADAM_EPS) + ADAM_WD * w_ref[...])
        nm_ref[...] = m
        nv_ref[...] = v

    full = pl.BlockSpec((tr, lanes), lambda h, i, c_ref: (h * nb + i, 0))
    half = pl.BlockSpec((tr, lanes), lambda h, i, c_ref: (i, 0))
    grid_spec = pltpu.PrefetchScalarGridSpec(num_scalar_prefetch=1, grid=(2, nb),
                                             in_specs=[full, half, half, full, full], out_specs=[full] * 4)
    return pl.pallas_call(body, name=name, grid_spec=grid_spec, out_shape=[SDS((rows, lanes), F32)] * 4,
                          compiler_params=_params(("parallel", "parallel")))(c_idx, wf, g_own, g_other, mf, vf)


BIG = ("ffn1_w_gate", "ffn1_w_up", "ffn1_w_down", "ffn2_w_gate", "ffn2_w_up", "ffn2_w_down", "w_out", "w_in")
TRANSPOSED = ("ffn1_w_gate", "ffn1_w_up", "ffn2_w_gate", "ffn2_w_up")
PACKED = ("rwkv_w2", "rwkv_a2", "rwkv_g2")
SMALL_SHAPES = {"ffn1_norm": (1, D_MODEL), "mix_norm": (1, D_MODEL), "hgrn_lb_logits": (2, W_A),
                "hgrn_out_norm": (1, W_A), "rwkv_shift_mu": (1, N_RWKV_COLS), "rwkv_w0": (1, W_B),
                "rwkv_a0": (1, W_B), "rwkv_k_k": (1, W_B), "rwkv_k_a": (1, W_B),
                "rwkv_r_k": (1, HB_HEADS, HB_DIM), "rwkv_gn_w": (1, W_B), "rwkv_gn_b": (1, W_B),
                "ffn2_norm": (1, D_MODEL), "final_norm": (D_MODEL,)}
PACK_ELEMS = sum(_numel(_shard_shape(n)) for n in PACKED) + sum(_numel(SMALL_SHAPES[n]) for n in SMALL)
PACK_ROWS = -(-PACK_ELEMS // (32 * LANES)) * 32


def _to_rows(name, shard):
    return shard[0].T if name in TRANSPOSED else shard[0]


def _from_rows(name, rows):
    return (rows.T if name in TRANSPOSED else rows)[None]


def _pack(sharded, small):
    flat = jnp.concatenate([sharded[n].reshape(-1) for n in PACKED] + [small[n].reshape(-1) for n in SMALL])
    return jnp.pad(flat, (0, PACK_ROWS * LANES - flat.shape[0])).reshape(PACK_ROWS, LANES)


def _unpack(packed):
    flat, out, off = packed.reshape(-1), {}, 0
    for n in PACKED:
        shp = _shard_shape(n)
        out[n] = flat[off:off + _numel(shp)].reshape((1,) + shp)
        off += _numel(shp)
    for n in SMALL:
        shp = SMALL_SHAPES[n]
        out[n] = flat[off:off + _numel(shp)].reshape(shp)
        off += _numel(shp)
    return out


def _quarter(full, name, q):
    shape, ax = SHARDED_SHAPES[name]
    w = shape[ax] // N_CHIPS
    return lax.slice_in_dim(full, q * w, (q + 1) * w, axis=ax)


def kernel(x, ffn1_norm, ffn1_w_gate, ffn1_w_up, ffn1_w_down, mix_norm, w_in, hgrn_lb_logits, hgrn_out_norm, rwkv_shift_mu, rwkv_w0, rwkv_w2, rwkv_a0, rwkv_a2, rwkv_g2, rwkv_k_k, rwkv_k_a, rwkv_r_k, rwkv_gn_w, rwkv_gn_b, w_out, ffn2_norm, ffn2_w_gate, ffn2_w_up, ffn2_w_down, final_norm, loss_target, m_ffn1_norm, m_ffn1_w_gate, m_ffn1_w_up, m_ffn1_w_down, m_mix_norm, m_w_in, m_hgrn_lb_logits, m_hgrn_out_norm, m_rwkv_shift_mu, m_rwkv_w0, m_rwkv_w2, m_rwkv_a0, m_rwkv_a2, m_rwkv_g2, m_rwkv_k_k, m_rwkv_k_a, m_rwkv_r_k, m_rwkv_gn_w, m_rwkv_gn_b, m_w_out, m_ffn2_norm, m_ffn2_w_gate, m_ffn2_w_up, m_ffn2_w_down, m_final_norm, v_ffn1_norm, v_ffn1_w_gate, v_ffn1_w_up, v_ffn1_w_down, v_mix_norm, v_w_in, v_hgrn_lb_logits, v_hgrn_out_norm, v_rwkv_shift_mu, v_rwkv_w0, v_rwkv_w2, v_rwkv_a0, v_rwkv_a2, v_rwkv_g2, v_rwkv_k_k, v_rwkv_k_a, v_rwkv_r_k, v_rwkv_gn_w, v_rwkv_gn_b, v_w_out, v_ffn2_norm, v_ffn2_w_gate, v_ffn2_w_up, v_ffn2_w_down, v_final_norm):
    args = dict(locals())
    wts = {n: args[n] for n in ALL_WEIGHTS}
    moms = {n: args["m_" + n] for n in ALL_WEIGHTS}
    vars_ = {n: args["v_" + n] for n in ALL_WEIGHTS}

    small_w = {n: wts[n] for n in SMALL}
    shards = [_to_rows(n, wts[n]).astype(BF16) for n in BIG] + [_pack(wts, small_w).astype(BF16)]
    me = 2 * lax.axis_index("x") + lax.axis_index("y")
    gathered = _gather_weights([lax.dynamic_update_slice(jnp.zeros((N_CHIPS,) + s.shape, BF16), s[None], (me, 0, 0))
                                for s in shards])
    rows_of = dict(zip(BIG, gathered[:-1]))
    packs = gathered[-1].reshape(N_CHIPS, PACK_ROWS * LANES)
    full, off = {}, 0
    for n in PACKED:
        shp = _shard_shape(n)
        full[n] = jnp.concatenate([packs[q, off:off + _numel(shp)].reshape(shp) for q in range(N_CHIPS)], axis=1)
        off += _numel(shp)

    w = {}
    for tag in ("ffn1", "ffn2"):
        w[f"{tag}_wgt"] = rows_of[f"{tag}_w_gate"].reshape(D_FF, D_MODEL)
        w[f"{tag}_wut"] = rows_of[f"{tag}_w_up"].reshape(D_FF, D_MODEL)
        w[f"{tag}_wd"] = rows_of[f"{tag}_w_down"].reshape(D_FF, D_MODEL)
        w[f"{tag}_norm"] = wts[f"{tag}_norm"]
    w_in_full = jnp.concatenate([rows_of["w_in"][q] for q in range(N_CHIPS)], axis=1)
    w["w_in_h"] = w_in_full[:, :N_HGRN_COLS]
    w["w_in_r"] = jnp.pad(w_in_full[:, N_HGRN_COLS:], ((0, 0), (0, N_RWKV_PAD - N_RWKV_COLS)))
    w_out_full = rows_of["w_out"].reshape(D_MODEL, D_MODEL)
    w["w_out_a"], w["w_out_b"] = w_out_full[:W_A], w_out_full[W_A:]
    zrow = lambda nrow: jnp.zeros((nrow, W_B), BF16)
    w["w2_pad"] = jnp.concatenate([full["rwkv_w2"], zrow(LORA_PAD - 32)], axis=0)
    w["a2_pad"] = jnp.concatenate([zrow(32), full["rwkv_a2"], zrow(LORA_PAD - 64)], axis=0)
    w["g2_pad"] = jnp.concatenate([zrow(64), full["rwkv_g2"], zrow(LORA_PAD - 160)], axis=0)
    w["mix_norm"] = mix_norm
    w["lb0"], w["lb1"] = hgrn_lb_logits[0:1], hgrn_lb_logits[1:2]
    w["hgrn_out_norm"] = hgrn_out_norm
    w["mu_pad"] = jnp.pad(rwkv_shift_mu, ((0, 0), (0, N_RWKV_PAD - N_RWKV_COLS)))
    for n in ("rwkv_w0", "rwkv_a0", "rwkv_k_k", "rwkv_k_a", "rwkv_gn_w", "rwkv_gn_b"):
        w[n] = wts[n]
    w["rwkv_r_k"] = rwkv_r_k.reshape(1, W_B)
    w["final_norm"] = final_norm.reshape(1, D_MODEL)

    loss_slab, grad_x, g = _local_step(x[0], loss_target[0], w)
    loss = lax.psum(loss_slab[0, 0], ("x", "y", "c"))

    grows = {
        "ffn1_w_gate": g["ffn1_wgt"], "ffn1_w_up": g["ffn1_wut"], "ffn1_w_down": g["ffn1_wd"],
        "ffn2_w_gate": g["ffn2_wgt"], "ffn2_w_up": g["ffn2_wut"], "ffn2_w_down": g["ffn2_wd"],
        "w_out": jnp.concatenate([g["w_out_a"], g["w_out_b"]], axis=0),
    }
    g_w_in = jnp.concatenate([g["w_in_h"], g["w_in_r"][:, :N_RWKV_COLS]], axis=1)
    gfull = {
        "rwkv_w2": g["w2_pad"][0:32], "rwkv_a2": g["a2_pad"][32:64], "rwkv_g2": g["g2_pad"][64:160],
    }
    gsmall = {
        "ffn1_norm": g["ffn1_norm"], "mix_norm": g["mix_norm"],
        "hgrn_lb_logits": jnp.concatenate([g["lb0"], g["lb1"]], axis=0), "hgrn_out_norm": g["hgrn_out_norm"],
        "rwkv_shift_mu": g["mu_pad"][:, :N_RWKV_COLS], "rwkv_w0": g["rwkv_w0"], "rwkv_a0": g["rwkv_a0"],
        "rwkv_k_k": g["rwkv_k_k"], "rwkv_k_a": g["rwkv_k_a"], "rwkv_r_k": g["rwkv_r_k"],
        "rwkv_gn_w": g["rwkv_gn_w"], "rwkv_gn_b": g["rwkv_gn_b"], "ffn2_norm": g["ffn2_norm"],
        "final_norm": g["final_norm"],
    }
    gs = [grows[n].reshape(N_CHIPS, -1, D_MODEL) for n in BIG if n != "w_in"]
    gs.append(jnp.stack([_quarter(g_w_in, "w_in", q) for q in range(N_CHIPS)]))
    gs.append(jnp.stack([_pack({n: _quarter(gfull[n], n, q) for n in PACKED}, gsmall) for q in range(N_CHIPS)]))
    names = list(BIG) + ["packed"]
    c_idx = lax.axis_index("c").astype(jnp.int32).reshape(1)
    me_idx = me.astype(jnp.int32).reshape(1)
    r1 = _sibling_exchange(gs)
    s4 = [_add_halves(gt, rt, c_idx, f"grad_add_halves_{n}") for gt, rt, n in zip(gs, r1, names)]
    r2 = _chip_exchange(s4)
    own = [_sum_chips(rt, st, me_idx, f"grad_sum_chips_{n}") for rt, st, n in zip(r2, s4, names)]
    other = _sibling_swap(own)

    def rows_list(d):
        return [_to_rows(n, d[n]) for n in BIG] + [_pack(d, {n: d[n] for n in SMALL})]

    outs = [_adamw(wt, go, gx, mt, vt, c_idx, f"adamw_{n}")
            for wt, go, gx, mt, vt, n in zip(rows_list(wts), own, other, rows_list(moms), rows_list(vars_), names)]
    results = []
    for k in range(4):
        per = [outs[i][k] for i in range(len(names))]
        d = {n: _from_rows(n, z) for n, z in zip(BIG, per[:-1])}
        d.update(_unpack(per[-1]))
        results.append(d)
    return (loss, grad_x[None], *[r[n] for r in results for n in ALL_WEIGHTS])
```

```python
import functools

import jax
import jax.numpy as jnp
from jax import lax
from jax.experimental import pallas as pl
from jax.experimental.pallas import tpu as pltpu

F32 = jnp.float32
BF16 = jnp.bfloat16
SDS = jax.ShapeDtypeStruct
MESH = pl.DeviceIdType.MESH

D_MODEL = 1024
D_FF = 2816
W_A = 512
W_B = 512
HA_HEADS, HA_DIM = 4, 128
HB_HEADS, HB_DIM = 8, 64
HGRN_CHUNK = 64
HGRN_GROUP = 2
RWKV_CHUNK = 16
RWKV_GROUP = 4
N_HGRN_COLS = 4 * W_A
N_RWKV_COLS = 3 * W_B + 32 + 32 + 96
N_RWKV_PAD = 1792
LORA_PAD = 256
NORM_EPS = 1e-6
RWKV_GN_EPS = 64e-5
L2_EPS = 1e-12
ADAM_LR, ADAM_B1, ADAM_B2, ADAM_EPS, ADAM_WD, ADAM_STEP = 0.001, 0.9, 0.999, 1e-8, 0.01, 10

N_CHIPS = 4
VMEM_LIMIT_V7X = 56 * 1024 * 1024
LANES = 1024

SHARDED_SHAPES = {
    "ffn1_w_gate": ((D_MODEL, D_FF), 1), "ffn1_w_up": ((D_MODEL, D_FF), 1), "ffn1_w_down": ((D_FF, D_MODEL), 0),
    "w_in": ((D_MODEL, N_HGRN_COLS + N_RWKV_COLS), 1), "rwkv_w2": ((32, W_B), 1), "rwkv_a2": ((32, W_B), 1),
    "rwkv_g2": ((96, W_B), 1), "w_out": ((D_MODEL, D_MODEL), 0),
    "ffn2_w_gate": ((D_MODEL, D_FF), 1), "ffn2_w_up": ((D_MODEL, D_FF), 1), "ffn2_w_down": ((D_FF, D_MODEL), 0),
}
SMALL = ("ffn1_norm", "mix_norm", "hgrn_lb_logits", "hgrn_out_norm", "rwkv_shift_mu", "rwkv_w0", "rwkv_a0",
         "rwkv_k_k", "rwkv_k_a", "rwkv_r_k", "rwkv_gn_w", "rwkv_gn_b", "ffn2_norm", "final_norm")
ALL_WEIGHTS = ("ffn1_norm", "ffn1_w_gate", "ffn1_w_up", "ffn1_w_down", "mix_norm", "w_in", "hgrn_lb_logits",
               "hgrn_out_norm", "rwkv_shift_mu", "rwkv_w0", "rwkv_w2", "rwkv_a0", "rwkv_a2", "rwkv_g2", "rwkv_k_k",
               "rwkv_k_a", "rwkv_r_k", "rwkv_gn_w", "rwkv_gn_b", "w_out", "ffn2_norm", "ffn2_w_gate", "ffn2_w_up",
               "ffn2_w_down", "final_norm")


def _shard_shape(name):
    shape, ax = SHARDED_SHAPES[name]
    return tuple(s // N_CHIPS if i == ax else s for i, s in enumerate(shape))


def _numel(shape):
    n = 1
    for s in shape:
        n *= s
    return n


def _params(sem=None):
    return pltpu.CompilerParams(dimension_semantics=sem, vmem_limit_bytes=VMEM_LIMIT_V7X)


def _split2(x):
    hi = x.astype(BF16)
    return hi, (x.astype(F32) - hi.astype(F32)).astype(BF16)


def _dg(x, y, cx, cy, hi):
    dn = (((cx,), (cy,)), ((), ()))
    dot = lambda p, q: lax.dot_general(p, q, dn, preferred_element_type=F32)
    if hi == "x3":
        (xh, xl), (yh, yl) = _split2(x), _split2(y)
        return dot(xh, yh) + (dot(xh, yl) + dot(xl, yh))
    return dot(x.astype(BF16), y.astype(BF16))


def _make_mm(hi, cotangent_forms=None):
    @jax.custom_vjp
    def nn(x, y):
        return _dg(x, y, 1, 0, hi)

    @jax.custom_vjp
    def nt(x, y):
        return _dg(x, y, 1, 1, hi)

    @jax.custom_vjp
    def tn(x, y):
        return _dg(x, y, 0, 0, hi)

    bnn, bnt, btn = cotangent_forms or (nn, nt, tn)
    nn.defvjp(lambda x, y: (nn(x, y), (x, y)), lambda r, g: (bnt(g, r[1]), btn(r[0], g)))
    nt.defvjp(lambda x, y: (nt(x, y), (x, y)), lambda r, g: (bnn(g, r[1]), btn(g, r[0])))
    tn.defvjp(lambda x, y: (tn(x, y), (x, y)), lambda r, g: (bnt(r[1], g), bnn(r[0], g)))
    return nn, nt, tn


_nn, _nt, _tn = _make_mm(False)
_nn_x3, _nt_x3, _tn_x3 = _make_mm("x3", (_nn, _nt, _tn))


def _tri_apply(x, transpose):
    c = x.shape[0]
    tri = (lax.broadcasted_iota(jnp.int32, (c, c), 1) <= lax.broadcasted_iota(jnp.int32, (c, c), 0)).astype(BF16)
    dn = (((0 if transpose else 1,), (0,)), ((), ()))
    p1 = x.astype(BF16)
    r1 = x - p1.astype(F32)
    p2 = r1.astype(BF16)
    p3 = (r1 - p2.astype(F32)).astype(BF16)
    dot = lambda p: lax.dot_general(tri, p, dn, preferred_element_type=F32)
    return dot(p1) + (dot(p2) + dot(p3))


@jax.custom_vjp
def _cumsum_rows(x):
    return _tri_apply(x, False)


_cumsum_rows.defvjp(lambda x: (_tri_apply(x, False), None), lambda _, g: (_tri_apply(g, True),))


def _sigmoid(x):
    return 1.0 / (1.0 + jnp.exp(-x))


def _silu(x):
    return x * _sigmoid(x)


def _softplus(z):
    return jnp.maximum(z, 0.0) + jnp.log(1.0 + jnp.exp(-jnp.abs(z)))


def _mm(a, b, *, ta=False, tb=False, tm, tn, tk, name, out_dtype=F32, res=None, scale=None):
    m = a.shape[1] if ta else a.shape[0]
    kdim = a.shape[0] if ta else a.shape[1]
    n = b.shape[0] if tb else b.shape[1]
    assert (b.shape[1] if tb else b.shape[0]) == kdim
    tm, tn, tk = min(tm, m), min(tn, n), min(tk, kdim)
    assert m % tm == 0 and n % tn == 0 and kdim % tk == 0, (name, m, n, kdim)
    nk = kdim // tk
    a_spec = pl.BlockSpec((tk, tm), lambda i, j, k: (k, i)) if ta else pl.BlockSpec((tm, tk), lambda i, j, k: (i, k))
    b_spec = pl.BlockSpec((tn, tk), lambda i, j, k: (j, k)) if tb else pl.BlockSpec((tk, tn), lambda i, j, k: (k, j))
    o_spec = pl.BlockSpec((tm, tn), lambda i, j, k: (i, j))
    ca, cb = (0 if ta else 1), (1 if tb else 0)

    def body(*refs):
        if res is not None:
            a_ref, b_ref, r_ref, o_ref, acc_ref = refs
        else:
            a_ref, b_ref, o_ref, acc_ref = refs
        k = pl.program_id(2)

        @pl.when(k == 0)
        def _():
            acc_ref[...] = jnp.zeros_like(acc_ref)

        acc_ref[...] += _dg(a_ref[...], b_ref[...], ca, cb, False)

        @pl.when(k == nk - 1)
        def _():
            acc = acc_ref[...]
            if scale is not None:
                acc = acc * scale
            if res is not None:
                acc = r_ref[...] + acc
            o_ref[...] = acc.astype(out_dtype)

    in_specs = [a_spec, b_spec] + ([o_spec] if res is not None else [])
    args = (a, b) + ((res,) if res is not None else ())
    return pl.pallas_call(
        body, name=name, grid=(m // tm, n // tn, nk), in_specs=in_specs, out_specs=o_spec,
        out_shape=SDS((m, n), out_dtype), scratch_shapes=[pltpu.VMEM((tm, tn), F32)],
        compiler_params=_params(("parallel", "parallel", "arbitrary")))(*args)


def _row_spec(x, tm):
    if isinstance(x, tuple):
        arr, w, j = x
        return arr, pl.BlockSpec((tm, w), lambda i, j=j: (i, j))
    return x, pl.BlockSpec((tm, x.shape[1]), lambda i: (i, 0))


def _par_spec(p):
    if isinstance(p, tuple):
        arr, w, j = p
        return arr, pl.BlockSpec((arr.shape[0], w), lambda i, j=j: (0, j))
    return p, pl.BlockSpec(p.shape, lambda i: (0, 0))


def _store_groups(refs, groups, vals):
    for ref, idxs in zip(refs, groups):
        off = 0
        for ix in idxs:
            v = vals[ix]
            ref[:, off:off + v.shape[1]] = v.astype(ref.dtype)
            off += v.shape[1]


SUBLANES = 8


def _x_plan(xs, tm, t):
    arrays, specs, plan = [], [], []
    nb = tm // SUBLANES
    for x in xs:
        if isinstance(x, tuple) and isinstance(x[0], str):
            kind, arr, w, j = x
            if kind == "prev":
                halo = lambda i, j=j: (jnp.maximum(i * nb - 1, 0), j)
            else:
                halo = lambda i, j=j: (jnp.minimum((i + 1) * nb, t // SUBLANES - 1), j)
            arrays += [arr, arr]
            specs += [pl.BlockSpec((tm, w), lambda i, j=j: (i, j)), pl.BlockSpec((SUBLANES, w), halo)]
            plan.append((kind, 2, w))
        else:
            arr, spec = _row_spec(x, tm)
            arrays.append(arr)
            specs.append(spec)
            plan.append(("plain", 1, spec.block_shape[1]))
    return arrays, specs, plan


def _x_vals(refs, plan, tm, nt):
    vals, k = [], 0
    i = pl.program_id(0)
    rows = lax.broadcasted_iota(jnp.int32, (tm, 1), 0)
    for kind, n, _ in plan:
        main = refs[k][...].astype(F32)
        if kind == "prev":
            edge = jnp.where(i == 0, 0.0, refs[k + 1][SUBLANES - 1:SUBLANES, :].astype(F32))
            main = jnp.where(rows == 0, edge, pltpu.roll(main, 1, 0))
        elif kind == "next":
            edge = jnp.where(i == nt - 1, 0.0, refs[k + 1][0:1, :].astype(F32))
            main = jnp.where(rows == tm - 1, edge, pltpu.roll(main, tm - 1, 0))
        vals.append(main)
        k += n
    return vals


def _tile_rows(xs, tm):
    arr = xs[0]
    if isinstance(arr, tuple):
        arr = arr[1] if isinstance(arr[0], str) else arr[0]
    return min(tm, arr.shape[0]), arr.shape[0]


def _rowwise(f, xs, params, out_groups, out_dtypes, *, tm, name):
    tm, t = _tile_rows(xs, tm)
    nt = t // tm
    xa, xspecs, plan = _x_plan(xs, tm, t)
    pa, pspecs = (zip(*[_par_spec(p) for p in params]) if params else ((), ()))
    nxr, npar = len(xa), len(pa)
    x_sds = [SDS((tm, w), F32) for _, _, w in plan]
    p_sds = [SDS(s.block_shape, F32) for s in pspecs]
    outs_sds = jax.eval_shape(lambda *vals: f(*vals), *x_sds, *p_sds)
    widths = [sum(outs_sds[ix].shape[1] for ix in idxs) for idxs in out_groups]

    def body(*refs):
        vals = _x_vals(refs[:nxr], plan, tm, nt) + [r[...].astype(F32) for r in refs[nxr:nxr + npar]]
        outs = f(*vals)
        _store_groups(refs[nxr + npar:], out_groups, outs)

    return pl.pallas_call(
        body, name=name, grid=(nt,), in_specs=list(xspecs) + list(pspecs),
        out_specs=[pl.BlockSpec((tm, w), lambda i: (i, 0)) for w in widths],
        out_shape=[SDS((t, w), dt) for w, dt in zip(widths, out_dtypes)],
        compiler_params=_params(("parallel",)))(*xa, *pa)


def _rowwise_bwd(f, xs, params, cots, *, x_grad, p_grad, dx_groups, dx_dtypes, tm, name, extra=None):
    tm, t = _tile_rows(xs, tm)
    nt = t // tm
    xa, xspecs, plan = _x_plan(xs, tm, t)
    pa, pspecs = (zip(*[_par_spec(p) for p in params]) if params else ((), ()))
    ca, cspecs = zip(*[_row_spec(c, tm) for c in cots])
    extra = extra or {}
    ekeys = sorted(extra)
    ea, especs = (zip(*[_row_spec(extra[k], tm) for k in ekeys]) if ekeys else ((), ()))
    nx, nxr, npar, nc, ne = len(plan), len(xa), len(pa), len(ca), len(ea)
    gx = [i for i in range(nx) if x_grad[i]]
    gp = [i for i in range(npar) if p_grad[i]]
    widths = [sum(plan[gx[ix]][2] for ix in idxs) for idxs in dx_groups]
    ng = len(dx_groups)

    def body(*refs):
        ins = refs[:nxr + npar + nc + ne]
        outs = refs[nxr + npar + nc + ne:]
        vals = _x_vals(ins[:nxr], plan, tm, nt) + [r[...].astype(F32) for r in ins[nxr:nxr + npar]]
        cvals = tuple(r[...].astype(F32) for r in ins[nxr + npar:nxr + npar + nc])
        evals = [r[...].astype(F32) for r in ins[nxr + npar + nc:]]
        diff_idx = gx + [nx + i for i in gp]

        def g(*dargs):
            full = list(vals)
            for ix, v in zip(diff_idx, dargs):
                full[ix] = v
            return tuple(f(*full))

        _, vjp = jax.vjp(g, *[vals[ix] for ix in diff_idx])
        grads = vjp(cvals)
        dxs = list(grads[:len(gx)])
        for k, ev in zip(ekeys, evals):
            dxs[k] = dxs[k] + ev
        _store_groups(outs[:ng], dx_groups, dxs)
        i = pl.program_id(0)
        for ref, gval in zip(outs[ng:], grads[len(gx):]):
            @pl.when(i == 0)
            def _(ref=ref):
                ref[...] = jnp.zeros_like(ref)
            ref[...] += gval

    dp_specs = [pl.BlockSpec(pspecs[i].block_shape, lambda i: (0, 0)) for i in gp]
    dp_shapes = [SDS(pspecs[i].block_shape, F32) for i in gp]
    return pl.pallas_call(
        body, name=name, grid=(nt,), in_specs=list(xspecs) + list(pspecs) + list(cspecs) + list(especs),
        out_specs=[pl.BlockSpec((tm, w), lambda i: (i, 0)) for w in widths] + dp_specs,
        out_shape=[SDS((t, w), dt) for w, dt in zip(widths, dx_dtypes)] + dp_shapes,
        compiler_params=_params(("arbitrary",)))(*xa, *pa, *ca, *ea)


def _rms_f(x, g):
    return (x * lax.rsqrt(jnp.mean(x * x, axis=-1, keepdims=True) + NORM_EPS) * g,)


def _three_pieces(x):
    p1 = x.astype(BF16)
    r1 = x - p1.astype(F32)
    p2 = r1.astype(BF16)
    return p1, p2, (r1 - p2.astype(F32)).astype(BF16)


def _group_sum_impl(x, ones_bd):
    p1, p2, p3 = _three_pieces(x)
    dot = lambda p: lax.dot_general(p, ones_bd.astype(BF16), (((1,), (0,)), ((), ())), preferred_element_type=F32)
    return dot(p1) + (dot(p2) + dot(p3))


@jax.custom_vjp
def _group_sum(x, ones_bd):
    return _group_sum_impl(x, ones_bd)


_group_sum.defvjp(lambda x, o: (_group_sum_impl(x, o), o),
                  lambda o, g: (_group_sum_impl(g, o), jnp.zeros_like(o)))


def _rwkv_prep_f(r, k, v, lo, rp, kp, vp, lop, mu_r, mu_k, mu_v, mu_lo, w0, w2p, a0, a2p, g2p, k_k, k_a, ones_bd):
    r = r + mu_r * (rp - r)
    k = k + mu_k * (kp - k)
    v = v + mu_v * (vp - v)
    lo = lo + mu_lo * (lop - lo)
    w_log = -_softplus(-(w0 + _nn(jnp.tanh(lo), w2p))) - 0.5
    lw = -jnp.exp(w_log)
    a_g = _sigmoid(a0 + _nn(lo, a2p))
    g = _nn(_sigmoid(lo), g2p)
    kk = k * k_k
    kk = kk / jnp.maximum(jnp.sqrt(_group_sum(kk * kk, ones_bd)), L2_EPS)
    k2 = k * (1.0 + (a_g - 1.0) * k_a)
    return r, lw, k2, v, -kk, kk * a_g, g


def _rwkv_post_f(y, r, k2, v, g, r_k, gn_w, gn_b, ones_bd):
    inv_n = 1.0 / HB_DIM
    mean = _group_sum(y, ones_bd) * inv_n
    yc = y - mean
    var = _group_sum(yc * yc, ones_bd) * inv_n
    yn = yc * lax.rsqrt(var + RWKV_GN_EPS) * gn_w + gn_b
    bonus = _group_sum(r * k2 * r_k, ones_bd) * v
    return ((yn + bonus) * g,)


def _tri(c, strict=False):
    ii = lax.broadcasted_iota(jnp.int32, (c, c), 0)
    jj = lax.broadcasted_iota(jnp.int32, (c, c), 1)
    return (jj < ii) if strict else (jj <= ii)


def _hgrn_step(st0, q_a, f_a, i_a, g_a, l0, l1, onorm):
    nh, nj = len(q_a), len(q_a[0])
    c = q_a[0][0].shape[0]
    combos = [(j, h) for j in range(nj) for h in range(nh)]
    every = lambda fn: {q: fn(q) for q in combos}
    at_ = lambda d: (lambda q: d[q[1]][q[0]])
    qa_, fa_, ia_, ga_ = (at_(z) for z in (q_a, f_a, i_a, g_a))
    incl = _tri(c)
    rows = lax.broadcasted_iota(jnp.int32, (c, 1), 0)
    lb = []
    for h in range(nh):
        mx = jnp.maximum(l0[h], l1[h])
        e0, e1 = jnp.exp(l0[h] - mx), jnp.exp(l1[h] - mx)
        lb.append(e0 / (e0 + e1))
    forget = every(lambda q: lb[q[1]] + (1.0 - lb[q[1]]) * _sigmoid(fa_(q)))
    qs = every(lambda q: _silu(qa_(q)))
    kk = every(lambda q: 1.0 - forget[q])
    lf = every(lambda q: jnp.log(forget[q]))
    bcum = every(lambda q: _cumsum_rows(lf[q]))
    bref = every(lambda q: jnp.sum(jnp.where(rows <= c // 2, lf[q], 0.0), axis=0, keepdims=True))
    blast = every(lambda q: jnp.sum(lf[q], axis=0, keepdims=True))
    scores = every(lambda q: jnp.where(incl, _nt(qs[q] * jnp.exp(bcum[q] - bref[q]),
                                                 kk[q] * jnp.exp(bref[q] - bcum[q])), 0.0))
    intra = every(lambda q: _nn(scores[q], ia_(q)))
    qb = every(lambda q: qs[q] * jnp.exp(bcum[q]))
    upd = every(lambda q: _tn(ia_(q), kk[q] * jnp.exp(blast[q] - bcum[q])))
    dec = every(lambda q: jnp.exp(blast[q]))
    st = list(st0)
    o = {}
    for j in range(nj):
        for h in range(nh):
            o[(j, h)] = intra[(j, h)] + _nt(qb[(j, h)], st[h])
        st = [st[h] * dec[(j, h)] + upd[(j, h)] for h in range(nh)]
    out = every(lambda q: o[q] * lax.rsqrt(jnp.mean(o[q] * o[q], axis=-1, keepdims=True) + NORM_EPS)
                * onorm[q[1]] * _silu(ga_(q)))
    return [[out[(j, h)] for j in range(nj)] for h in range(nh)], st


def _hgrn_blocks(ref, nj, c):
    return [[ref[j * c:(j + 1) * c, h * HA_DIM:(h + 1) * HA_DIM] for j in range(nj)] for h in range(HA_HEADS)]


def _hgrn_cols(ref):
    return [ref[:, h * HA_DIM:(h + 1) * HA_DIM] for h in range(HA_HEADS)]


def _hgrn_fwd(p_h, l0, l1, onorm):
    t = p_h.shape[0]
    cc, nj = HGRN_CHUNK, HGRN_GROUP
    c = cc * nj
    n = t // c

    def body(q_ref, f_ref, i_ref, g_ref, l0_ref, l1_ref, on_ref, o_ref, hs_ref, st_ref):
        @pl.when(pl.program_id(0) == 0)
        def _():
            st_ref[...] = jnp.zeros_like(st_ref)

        hs_ref[0] = st_ref[...]
        o, st1 = _hgrn_step([st_ref[h] for h in range(HA_HEADS)],
                            *[_hgrn_blocks(ref, nj, cc) for ref in (q_ref, f_ref, i_ref, g_ref)],
                            _hgrn_cols(l0_ref), _hgrn_cols(l1_ref), _hgrn_cols(on_ref))
        for h in range(HA_HEADS):
            for j in range(nj):
                o_ref[j * cc:(j + 1) * cc, h * HA_DIM:(h + 1) * HA_DIM] = o[h][j]
            st_ref[h] = st1[h]

    col = lambda j: pl.BlockSpec((c, W_A), lambda i, j=j: (i, j))
    par = pl.BlockSpec((1, W_A), lambda i: (0, 0))
    return pl.pallas_call(
        body, name="hgrn_fwd", grid=(n,), in_specs=[col(0), col(1), col(2), col(3), par, par, par],
        out_specs=[pl.BlockSpec((c, W_A), lambda i: (i, 0)),
                   pl.BlockSpec((1, HA_HEADS, HA_DIM, HA_DIM), lambda i: (i, 0, 0, 0))],
        out_shape=[SDS((t, W_A), F32), SDS((n, HA_HEADS, HA_DIM, HA_DIM), F32)],
        scratch_shapes=[pltpu.VMEM((HA_HEADS, HA_DIM, HA_DIM), F32)],
        compiler_params=_params(("arbitrary",)))(p_h, p_h, p_h, p_h, l0, l1, onorm)


def _hgrn_bwd(p_h, l0, l1, onorm, hs, do, do_col):
    t = p_h.shape[0]
    cc, nj = HGRN_CHUNK, HGRN_GROUP
    c = cc * nj
    n = t // c

    def body(q_ref, f_ref, i_ref, g_ref, l0_ref, l1_ref, on_ref, hs_ref, do_ref,
             dp_ref, dl0_ref, dl1_ref, don_ref, dst_ref):
        @pl.when(pl.program_id(0) == 0)
        def _():
            dst_ref[...] = jnp.zeros_like(dst_ref)
            dl0_ref[...] = jnp.zeros_like(dl0_ref)
            dl1_ref[...] = jnp.zeros_like(dl1_ref)
            don_ref[...] = jnp.zeros_like(don_ref)

        args = ([hs_ref[0, h] for h in range(HA_HEADS)],
                *[_hgrn_blocks(ref, nj, cc) for ref in (q_ref, f_ref, i_ref, g_ref)],
                _hgrn_cols(l0_ref), _hgrn_cols(l1_ref), _hgrn_cols(on_ref))
        _, vjp = jax.vjp(_hgrn_step, *args)
        dst0, dq, df, di, dg, dl0, dl1, don = vjp((_hgrn_blocks(do_ref, nj, cc),
                                                   [dst_ref[h] for h in range(HA_HEADS)]))
        for h in range(HA_HEADS):
            sl = slice(h * HA_DIM, (h + 1) * HA_DIM)
            for k, dv in enumerate((dq, df, di, dg)):
                for j in range(nj):
                    dp_ref[j * cc:(j + 1) * cc, k * W_A + h * HA_DIM:k * W_A + (h + 1) * HA_DIM] = dv[h][j]
            dl0_ref[:, sl] += dl0[h]
            dl1_ref[:, sl] += dl1[h]
            don_ref[:, sl] += don[h]
            dst_ref[h] = dst0[h]

    col = lambda j: pl.BlockSpec((c, W_A), lambda i, j=j: (n - 1 - i, j))
    par = pl.BlockSpec((1, W_A), lambda i: (0, 0))
    return pl.pallas_call(
        body, name="hgrn_bwd", grid=(n,),
        in_specs=[col(0), col(1), col(2), col(3), par, par, par,
                  pl.BlockSpec((1, HA_HEADS, HA_DIM, HA_DIM), lambda i: (n - 1 - i, 0, 0, 0)),
                  pl.BlockSpec((c, W_A), lambda i: (n - 1 - i, do_col))],
        out_specs=[pl.BlockSpec((c, N_HGRN_COLS), lambda i: (n - 1 - i, 0)), par, par, par],
        out_shape=[SDS((t, N_HGRN_COLS), F32), SDS((1, W_A), F32), SDS((1, W_A), F32), SDS((1, W_A), F32)],
        scratch_shapes=[pltpu.VMEM((HA_HEADS, HA_DIM, HA_DIM), F32)],
        compiler_params=_params(("arbitrary",)))(p_h, p_h, p_h, p_h, l0, l1, onorm, hs, do)


HB_PAIRS = HB_HEADS // 2
PAIR_W = 2 * HB_DIM


def _head_lane_masks():
    lane = lax.broadcasted_iota(jnp.int32, (1, PAIR_W), 1)
    return (lane < HB_DIM).astype(F32), (lane >= HB_DIM).astype(F32)


@jax.custom_vjp
def _stack_heads(x):
    m0, m1 = _head_lane_masks()
    return jnp.concatenate([x * m0, x * m1], axis=0)


def _stack_heads_bwd(_, g):
    m0, m1 = _head_lane_masks()
    c = g.shape[0] // 2
    return (g[:c] * m0 + g[c:] * m1,)


_stack_heads.defvjp(lambda x: (_stack_heads(x), None), _stack_heads_bwd)


@jax.custom_vjp
def _unstack_heads(ys):
    c = ys.shape[0] // 2
    return ys[:c] + ys[c:]


_unstack_heads.defvjp(lambda ys: (_unstack_heads(ys), None), lambda _, g: (_stack_heads(g),))


def _same_head_block(c):
    ii = lax.broadcasted_iota(jnp.int32, (2 * c, 2 * c), 0)
    jj = lax.broadcasted_iota(jnp.int32, (2 * c, 2 * c), 1)
    same = (ii < c) == (jj < c)
    return same & (jj <= ii), same & (jj < ii), (ii == jj).astype(F32)


def _rwkv_step(s0, r, lw, k, v, a, b):
    npair, nj = len(r), len(r[0])
    c = r[0][0].shape[0]
    combos = [(j, p) for j in range(nj) for p in range(npair)]
    every = lambda fn: {q: fn(q) for q in combos}
    at_ = lambda d: (lambda q: d[q[1]][q[0]])
    r_, lw_, k_, v_, a_, b_ = (at_(z) for z in (r, lw, k, v, a, b))
    incl, strict, eye = _same_head_block(c)

    gam = every(lambda q: _cumsum_rows(lw_(q)))
    gtot = every(lambda q: jnp.sum(lw_(q), axis=0, keepdims=True))
    eneg = every(lambda q: jnp.exp(-gam[q]))
    edec = every(lambda q: jnp.exp(gtot[q] - gam[q]))
    at = every(lambda q: _stack_heads(a_(q) * jnp.exp(gam[q] - lw_(q))))
    rt = every(lambda q: _stack_heads(r_(q) * jnp.exp(gam[q])))
    bt = every(lambda q: _stack_heads(b_(q) * eneg[q]))
    kt = every(lambda q: _stack_heads(k_(q) * eneg[q]))
    bdec = every(lambda q: _stack_heads(b_(q) * edec[q]))
    kdec = every(lambda q: _stack_heads(k_(q) * edec[q]))
    vs = every(lambda q: _stack_heads(v_(q)))
    a_ab = every(lambda q: jnp.where(strict, _nt(at[q], bt[q]), 0.0))
    a_ak = every(lambda q: jnp.where(strict, _nt(at[q], kt[q]), 0.0))
    a_rb = every(lambda q: jnp.where(incl, _nt(rt[q], bt[q]), 0.0))
    a_rk = every(lambda q: jnp.where(incl, _nt(rt[q], kt[q]), 0.0))
    tinv = every(lambda q: eye + a_ab[q])
    pw = a_ab
    span = 2
    while span < c:
        pw = every(lambda q, pw=pw: _nn_x3(pw[q], pw[q]))
        tinv = every(lambda q, pw=pw, tinv=tinv: tinv[q] + _nn_x3(pw[q], tinv[q]))
        span *= 2
    akv = every(lambda q: _nn(a_ak[q], vs[q]))
    w1 = every(lambda q: _nn_x3(tinv[q], at[q]))
    u0 = every(lambda q: _nn_x3(tinv[q], akv[q]))
    r1 = every(lambda q: rt[q] + _nn(a_rb[q], w1[q]))
    y0 = every(lambda q: _nn(a_rb[q], u0[q]) + _nn(a_rk[q], vs[q]))
    mm = every(lambda q: _tn(w1[q], bdec[q]))
    zz = every(lambda q: _tn(u0[q], bdec[q]) + _tn(vs[q], kdec[q]))
    gdec = every(lambda q: jnp.exp(gtot[q]))

    s = list(s0)
    y = [[None] * nj for _ in range(npair)]
    for j in range(nj):
        for p in range(npair):
            y[p][j] = _unstack_heads(_nt(r1[(j, p)], s[p]) + y0[(j, p)])
        s = [s[p] * gdec[(j, p)] + _nn(s[p], mm[(j, p)]) + zz[(j, p)] for p in range(npair)]
    return y, s


def _rwkv_blocks(ref, nj, c):
    return [[ref[j * c:(j + 1) * c, p * PAIR_W:(p + 1) * PAIR_W] for j in range(nj)] for p in range(HB_PAIRS)]


def _rwkv_fwd(seqs):
    t = seqs[0].shape[0]
    c, nj = RWKV_CHUNK, RWKV_GROUP
    n = t // (c * nj)

    def body(r_ref, lw_ref, k_ref, v_ref, a_ref, b_ref, y_ref, hs_ref, st_ref):
        @pl.when(pl.program_id(0) == 0)
        def _():
            st_ref[...] = jnp.zeros_like(st_ref)

        hs_ref[0] = st_ref[...]
        s0 = [st_ref[p] for p in range(HB_PAIRS)]
        y, s1 = _rwkv_step(s0, *[_rwkv_blocks(ref, nj, c) for ref in (r_ref, lw_ref, k_ref, v_ref, a_ref, b_ref)])
        for p in range(HB_PAIRS):
            for j in range(nj):
                y_ref[j * c:(j + 1) * c, p * PAIR_W:(p + 1) * PAIR_W] = y[p][j]
            st_ref[p] = s1[p]

    seq = pl.BlockSpec((c * nj, W_B), lambda i: (i, 0))
    return pl.pallas_call(
        body, name="rwkv_fwd", grid=(n,), in_specs=[seq] * 6,
        out_specs=[seq, pl.BlockSpec((1, HB_PAIRS, PAIR_W, PAIR_W), lambda i: (i, 0, 0, 0))],
        out_shape=[SDS((t, W_B), F32), SDS((n, HB_PAIRS, PAIR_W, PAIR_W), F32)],
        scratch_shapes=[pltpu.VMEM((HB_PAIRS, PAIR_W, PAIR_W), F32)],
        compiler_params=_params(("arbitrary",)))(*seqs)


def _rwkv_bwd(seqs, hs, dy):
    t = seqs[0].shape[0]
    c, nj = RWKV_CHUNK, RWKV_GROUP
    n = t // (c * nj)

    def body(r_ref, lw_ref, k_ref, v_ref, a_ref, b_ref, hs_ref, dy_ref,
             dr_ref, dlw_ref, dk_ref, dv_ref, da_ref, db_ref, dst_ref):
        @pl.when(pl.program_id(0) == 0)
        def _():
            dst_ref[...] = jnp.zeros_like(dst_ref)

        s0 = [hs_ref[0, p] for p in range(HB_PAIRS)]
        seq_vals = [_rwkv_blocks(ref, nj, c) for ref in (r_ref, lw_ref, k_ref, v_ref, a_ref, b_ref)]
        _, vjp = jax.vjp(_rwkv_step, s0, *seq_vals)
        grads = vjp((_rwkv_blocks(dy_ref, nj, c), [dst_ref[p] for p in range(HB_PAIRS)]))
        for ref, gr in zip((dr_ref, dlw_ref, dk_ref, dv_ref, da_ref, db_ref), grads[1:]):
            for p in range(HB_PAIRS):
                for j in range(nj):
                    ref[j * c:(j + 1) * c, p * PAIR_W:(p + 1) * PAIR_W] = gr[p][j]
        m0, m1 = _head_lane_masks()
        rows0 = (lax.broadcasted_iota(jnp.int32, (PAIR_W, 1), 0) < HB_DIM).astype(F32)
        blocks = rows0 * m0 + (1.0 - rows0) * m1
        for p in range(HB_PAIRS):
            dst_ref[p] = grads[0][p] * blocks

    seq = pl.BlockSpec((c * nj, W_B), lambda i: (n - 1 - i, 0))
    return pl.pallas_call(
        body, name="rwkv_bwd", grid=(n,),
        in_specs=[seq] * 6 + [pl.BlockSpec((1, HB_PAIRS, PAIR_W, PAIR_W), lambda i: (n - 1 - i, 0, 0, 0)), seq],
        out_specs=[seq] * 6, out_shape=[SDS((t, W_B), F32)] * 6,
        scratch_shapes=[pltpu.VMEM((HB_PAIRS, PAIR_W, PAIR_W), F32)],
        compiler_params=_params(("arbitrary",)))(*seqs, hs, dy)


def _final_loss(x3, fnorm, target, *, tm):
    t, d = x3.shape

    def body(x_ref, g_ref, t_ref, dx_ref, dg_ref, loss_ref):
        @pl.when(pl.program_id(0) == 0)
        def _():
            dg_ref[...] = jnp.zeros_like(dg_ref)
            loss_ref[...] = jnp.zeros_like(loss_ref)

        x, g = x_ref[...], g_ref[...]
        rinv = lax.rsqrt(jnp.mean(x * x, axis=-1, keepdims=True) + NORM_EPS)
        xh = x * rinv
        diff = xh * g - t_ref[...]
        loss_ref[...] += 0.5 * jnp.sum(jnp.mean(diff * diff, axis=-1, keepdims=True))
        dy = diff * (1.0 / d)
        dg_ref[...] += jnp.sum(dy * xh, axis=0, keepdims=True)
        dxh = dy * g
        dx_ref[...] = rinv * (dxh - xh * jnp.mean(dxh * xh, axis=-1, keepdims=True))

    row = pl.BlockSpec((tm, d), lambda i: (i, 0))
    return pl.pallas_call(
        body, name="final_loss", grid=(t // tm,), in_specs=[row, pl.BlockSpec((1, d), lambda i: (0, 0)), row],
        out_specs=[row, pl.BlockSpec((1, d), lambda i: (0, 0)), pl.BlockSpec((8, 128), lambda i: (0, 0))],
        out_shape=[SDS((t, d), F32), SDS((1, d), F32), SDS((8, 128), F32)],
        compiler_params=_params(("arbitrary",)))(x3, fnorm, target)


def _gate_up_act(h, wgt, wut, *, tm, tn, name):
    t, d = h.shape
    tm = min(tm, t)

    def body(h_ref, g_ref, u_ref, a_out, u_out, act_out):
        hv = h_ref[...]
        a = _dg(hv, g_ref[...], 1, 1, False)
        u = _dg(hv, u_ref[...], 1, 1, False)
        a_out[...] = a
        u_out[...] = u
        act_out[...] = (_silu(a) * u).astype(act_out.dtype)

    wspec = pl.BlockSpec((tn, d), lambda i, j: (j, 0))
    ospec = pl.BlockSpec((tm, tn), lambda i, j: (i, j))
    return pl.pallas_call(
        body, name=name, grid=(t // tm, D_FF // tn), in_specs=[pl.BlockSpec((tm, d), lambda i, j: (i, 0)), wspec, wspec],
        out_specs=[ospec, ospec, ospec], out_shape=[SDS((t, D_FF), F32), SDS((t, D_FF), F32), SDS((t, D_FF), BF16)],
        compiler_params=_params(("parallel", "parallel")))(h, wgt, wut)


def _dact_swiglu(dout, wd, a, u, *, tm, tn, name):
    t, d = dout.shape
    tm = min(tm, t)

    def body(d_ref, w_ref, a_ref, u_ref, da_out, du_out):
        dact = 0.5 * _dg(d_ref[...], w_ref[...], 1, 1, False)
        av, uv = a_ref[...], u_ref[...]
        s = _sigmoid(av)
        da_out[...] = (dact * uv * (s * (1.0 + av * (1.0 - s)))).astype(da_out.dtype)
        du_out[...] = (dact * (av * s)).astype(du_out.dtype)

    tile = pl.BlockSpec((tm, tn), lambda i, j: (i, j))
    return pl.pallas_call(
        body, name=name, grid=(t // tm, D_FF // tn),
        in_specs=[pl.BlockSpec((tm, d), lambda i, j: (i, 0)), pl.BlockSpec((tn, d), lambda i, j: (j, 0)), tile, tile],
        out_specs=[tile, tile], out_shape=[SDS((t, D_FF), BF16), SDS((t, D_FF), BF16)],
        compiler_params=_params(("parallel", "parallel")))(dout, wd, a, u)


def _ffn_fwd(x, norm, wgt, wut, wd, tag):
    h, = _rowwise(_rms_f, [x], [norm], [[0]], [BF16], tm=512, name=f"{tag}_rms")
    a, u, act = _gate_up_act(h, wgt, wut, tm=2048, tn=256, name=f"{tag}_gate_up")
    out = _mm(act, wd, tm=1024, tn=512, tk=D_FF, name=f"{tag}_down", res=x, scale=0.5)
    return out, (h, a, u, act)


def _ffn_bwd(dout, x, norm, wgt, wut, wd, saved, tag):
    h, a, u, act = saved
    da, du = _dact_swiglu(dout, wd, a, u, tm=2048, tn=256, name=f"{tag}_dact")
    dwd = _mm(act, dout, ta=True, tm=D_FF // 2, tn=D_MODEL, tk=1024, name=f"{tag}_dwd", scale=0.5)
    dwgt = _mm(da, h, ta=True, tm=D_FF // 2, tn=D_MODEL, tk=1024, name=f"{tag}_dwg")
    dwut = _mm(du, h, ta=True, tm=D_FF // 2, tn=D_MODEL, tk=1024, name=f"{tag}_dwu")
    dh = _mm(da, wgt, tm=1024, tn=512, tk=D_FF, name=f"{tag}_dh_g")
    dh = _mm(du, wut, tm=1024, tn=512, tk=D_FF, name=f"{tag}_dh_u", res=dh)
    dx, dnorm = _rowwise_bwd(_rms_f, [x], [norm], [dh], x_grad=[True], p_grad=[True], dx_groups=[[0]],
                             dx_dtypes=[F32], tm=256, name=f"{tag}_drms", extra={0: dout})
    return dx, dnorm, dwgt, dwut, dwd


def _local_step(x, target, w):
    ones_bd = jnp.kron(jnp.eye(HB_HEADS, dtype=F32), jnp.ones((HB_DIM, HB_DIM), F32))
    g = {}
    x1, ffn1_saved = _ffn_fwd(x, w["ffn1_norm"], w["ffn1_wgt"], w["ffn1_wut"], w["ffn1_wd"], "ffn1")
    hm, = _rowwise(_rms_f, [x1], [w["mix_norm"]], [[0]], [BF16], tm=512, name="mix_rms")
    p_h = _mm(hm, w["w_in_h"], tm=2048, tn=256, tk=D_MODEL, name="inproj_h")
    p_r = _mm(hm, w["w_in_r"], tm=2048, tn=256, tk=D_MODEL, name="inproj_r")
    o_a, hgrn_states = _hgrn_fwd(p_h, w["lb0"], w["lb1"], w["hgrn_out_norm"])

    mu = w["mu_pad"]
    prep_xs = [(p_r, W_B, 0), (p_r, W_B, 1), (p_r, W_B, 2), (p_r, LORA_PAD, 6),
               ("prev", p_r, W_B, 0), ("prev", p_r, W_B, 1), ("prev", p_r, W_B, 2), ("prev", p_r, LORA_PAD, 6)]
    prep_ps = [(mu, W_B, 0), (mu, W_B, 1), (mu, W_B, 2), (mu, LORA_PAD, 6), w["rwkv_w0"], w["w2_pad"], w["rwkv_a0"],
               w["a2_pad"], w["g2_pad"], w["rwkv_k_k"], w["rwkv_k_a"], ones_bd]
    prep_f = _rwkv_prep_f
    r, lw, k2, v, a_vec, b_vec, gate = _rowwise(prep_f, prep_xs, prep_ps, [[0], [1], [2], [3], [4], [5], [6]],
                                                [F32] * 7, tm=256, name="rwkv_prep")
    seqs = [r, lw, k2, v, a_vec, b_vec]
    y, rwkv_states = _rwkv_fwd(seqs)
    post_f = _rwkv_post_f
    post_xs = [y, r, k2, v, gate]
    post_ps = [w["rwkv_r_k"], w["rwkv_gn_w"], w["rwkv_gn_b"], ones_bd]
    o_b, = _rowwise(post_f, post_xs, post_ps, [[0]], [F32], tm=256, name="rwkv_post")
    x2 = _mm(o_a, w["w_out_a"], tm=2048, tn=256, tk=W_A, name="outproj_a", res=x1)
    x2 = _mm(o_b, w["w_out_b"], tm=2048, tn=256, tk=W_B, name="outproj_b", res=x2)
    x3, ffn2_saved = _ffn_fwd(x2, w["ffn2_norm"], w["ffn2_wgt"], w["ffn2_wut"], w["ffn2_wd"], "ffn2")
    dx3, g["final_norm"], loss = _final_loss(x3, w["final_norm"], target, tm=256)

    dx2, g["ffn2_norm"], g["ffn2_wgt"], g["ffn2_wut"], g["ffn2_wd"] = _ffn_bwd(
        dx3, x2, w["ffn2_norm"], w["ffn2_wgt"], w["ffn2_wut"], w["ffn2_wd"], ffn2_saved, "ffn2")
    do_a = _mm(dx2, w["w_out_a"], tb=True, tm=2048, tn=256, tk=D_MODEL, name="outproj_do_a")
    do_b = _mm(dx2, w["w_out_b"], tb=True, tm=2048, tn=256, tk=D_MODEL, name="outproj_do_b")
    g["w_out_a"] = _mm(o_a, dx2, ta=True, tm=W_A, tn=D_MODEL, tk=1024, name="outproj_dw_a")
    g["w_out_b"] = _mm(o_b, dx2, ta=True, tm=W_B, tn=D_MODEL, tk=1024, name="outproj_dw_b")

    dp_h, g["lb0"], g["lb1"], g["hgrn_out_norm"] = _hgrn_bwd(p_h, w["lb0"], w["lb1"], w["hgrn_out_norm"],
                                                             hgrn_states, do_a, 0)
    post_out = _rowwise_bwd(post_f, post_xs, post_ps, [do_b], x_grad=[True] * 5, p_grad=[True] * 3 + [False],
                            dx_groups=[[0], [1], [2], [3], [4]], dx_dtypes=[F32] * 5, tm=256, name="rwkv_post_bwd")
    dy, dr1, dk1, dv1, dgate, g["rwkv_r_k"], g["rwkv_gn_w"], g["rwkv_gn_b"] = post_out
    dr2, dlw, dk2, dv2, da_vec, db_vec = _rwkv_bwd(seqs, rwkv_states, dy)

    def prep2_f(*vals):
        r_, lw_, k2_, v_, a_, b_, g_ = prep_f(*vals)
        return r_, lw_, k2_, v_, a_, b_, g_, r_, k2_, v_

    prep_out = _rowwise_bwd(prep2_f, prep_xs, prep_ps, [dr2, dlw, dk2, dv2, da_vec, db_vec, dgate, dr1, dk1, dv1],
                            x_grad=[True] * 8, p_grad=[True] * 11 + [False], dx_groups=[[0, 1, 2, 3], [4, 5, 6, 7]],
                            dx_dtypes=[F32, F32], tm=256, name="rwkv_prep_bwd")
    dpr_main, dpr_prev = prep_out[0], prep_out[1]
    (dmu_r, dmu_k, dmu_v, dmu_lo, g["rwkv_w0"], g["w2_pad"], g["rwkv_a0"], g["a2_pad"], g["g2_pad"],
     g["rwkv_k_k"], g["rwkv_k_a"]) = prep_out[2:]
    g["mu_pad"] = jnp.concatenate([dmu_r, dmu_k, dmu_v, dmu_lo], axis=1)
    dp_r, = _rowwise(lambda u_, s_: (u_ + s_,), [dpr_main, ("next", dpr_prev, N_RWKV_PAD, 0)], [], [[0]], [F32],
                     tm=512, name="rwkv_dp_sum")
    dhm = _mm(dp_h, w["w_in_h"], tb=True, tm=1024, tn=512, tk=N_HGRN_COLS, name="inproj_dh_h")
    dhm = _mm(dp_r, w["w_in_r"], tb=True, tm=1024, tn=512, tk=N_RWKV_PAD, name="inproj_dh_r", res=dhm)
    g["w_in_h"] = _mm(hm, dp_h, ta=True, tm=D_MODEL, tn=D_MODEL, tk=1024, name="inproj_dw_h")
    g["w_in_r"] = _mm(hm, dp_r, ta=True, tm=D_MODEL, tn=N_RWKV_PAD // 2, tk=1024, name="inproj_dw_r")
    dx1, g["mix_norm"] = _rowwise_bwd(_rms_f, [x1], [w["mix_norm"]], [dhm], x_grad=[True], p_grad=[True],
                                      dx_groups=[[0]], dx_dtypes=[F32], tm=256, name="mix_drms", extra={0: dx2})
    dx0, g["ffn1_norm"], g["ffn1_wgt"], g["ffn1_wut"], g["ffn1_wd"] = _ffn_bwd(
        dx1, x, w["ffn1_norm"], w["ffn1_wgt"], w["ffn1_wut"], w["ffn1_wd"], ffn1_saved, "ffn1")
    return loss, dx0, g


HBM_SPEC = pl.BlockSpec(memory_space=pl.ANY)


def _chips(x, y):
    return [(1 - x, y), (x, 1 - y), (1 - x, 1 - y)]


def _gather_weights(bufs):
    n = len(bufs)

    def body(*refs):
        outs = refs[n:2 * n]
        ici_send, ici_recv, d2d_send, d2d_recv = refs[2 * n:]
        x, y, c = lax.axis_index("x"), lax.axis_index("y"), lax.axis_index("c")
        me = 2 * x + y

        def half(t, slot, hc):
            hr = bufs[t].shape[1] // 2
            return outs[t].at[slot, pl.ds(pl.multiple_of(hc * hr, 16), hr), :]

        def ici(t, j, slot, px, py):
            return pltpu.make_async_remote_copy(src_ref=half(t, slot, c), dst_ref=half(t, slot, c),
                                                send_sem=ici_send.at[3 * t + j], recv_sem=ici_recv.at[3 * t + j],
                                                device_id=(px, py, c), device_id_type=MESH)

        def d2d(t, j, slot, hc):
            return pltpu.make_async_remote_copy(src_ref=half(t, slot, hc), dst_ref=half(t, slot, hc),
                                                send_sem=d2d_send.at[3 * t + j], recv_sem=d2d_recv.at[3 * t + j],
                                                device_id=(x, y, 1 - c), device_id_type=MESH)

        sends = [ici(t, j, me, px, py) for t in range(n) for j, (px, py) in enumerate(_chips(x, y))]
        for cp in sends:
            cp.start()
        passed = []
        for t in range(n):
            for j, (px, py) in enumerate(_chips(x, y)):
                ici(t, j, 2 * px + py, px, py).wait_recv()
                cp = d2d(t, j, 2 * px + py, c)
                cp.start()
                passed.append(cp)
        for t in range(n):
            for j, (px, py) in enumerate(_chips(x, y)):
                d2d(t, j, 2 * px + py, 1 - c).wait_recv()
        for cp in sends + passed:
            cp.wait_send()

    return pl.pallas_call(
        body, name="gather_weights", in_specs=[HBM_SPEC] * n, out_specs=[HBM_SPEC] * n,
        out_shape=[SDS(b.shape, b.dtype) for b in bufs], input_output_aliases={t: t for t in range(n)},
        scratch_shapes=[pltpu.SemaphoreType.DMA((3 * n,))] * 4,
    )(*bufs)


def _sibling_exchange(gs):
    n = len(gs)

    def body(*refs):
        ins, outs = refs[:n], refs[n:2 * n]
        send_sems, recv_sems = refs[2 * n:]
        x, y, c = lax.axis_index("x"), lax.axis_index("y"), lax.axis_index("c")
        cps = []
        for t in range(n):
            hr = gs[t].shape[1] // 2
            src = ins[t].at[:, pl.ds(pl.multiple_of((1 - c) * hr, SUBLANES), hr), :]
            cps.append(pltpu.make_async_remote_copy(src_ref=src, dst_ref=outs[t], send_sem=send_sems.at[t],
                                                    recv_sem=recv_sems.at[t], device_id=(x, y, 1 - c),
                                                    device_id_type=MESH))
        for cp in cps:
            cp.start()
        for cp in cps:
            cp.wait()

    return pl.pallas_call(
        body, name="grad_sibling_exchange", in_specs=[HBM_SPEC] * n, out_specs=[HBM_SPEC] * n,
        out_shape=[SDS((N_CHIPS, g.shape[1] // 2, g.shape[2]), g.dtype) for g in gs],
        scratch_shapes=[pltpu.SemaphoreType.DMA((n,)), pltpu.SemaphoreType.DMA((n,))],
    )(*gs)


def _chip_exchange(ss):
    n = len(ss)

    def body(*refs):
        ins, outs = refs[:n], refs[n:2 * n]
        send_sems, recv_sems = refs[2 * n:]
        x, y, c = lax.axis_index("x"), lax.axis_index("y"), lax.axis_index("c")
        me = 2 * x + y

        def copy(t, j, px, py, src_slot, dst_slot):
            return pltpu.make_async_remote_copy(src_ref=ins[t].at[src_slot], dst_ref=outs[t].at[dst_slot],
                                                send_sem=send_sems.at[3 * t + j], recv_sem=recv_sems.at[3 * t + j],
                                                device_id=(px, py, c), device_id_type=MESH)

        sends = [copy(t, j, px, py, 2 * px + py, me) for t in range(n) for j, (px, py) in enumerate(_chips(x, y))]
        for cp in sends:
            cp.start()
        for t in range(n):
            for j, (px, py) in enumerate(_chips(x, y)):
                copy(t, j, px, py, me, 2 * px + py).wait_recv()
        for cp in sends:
            cp.wait_send()

    return pl.pallas_call(
        body, name="grad_chip_exchange", in_specs=[HBM_SPEC] * n, out_specs=[HBM_SPEC] * n,
        out_shape=[SDS(s.shape, s.dtype) for s in ss],
        scratch_shapes=[pltpu.SemaphoreType.DMA((3 * n,)), pltpu.SemaphoreType.DMA((3 * n,))],
    )(*ss)


def _sibling_swap(fs):
    n = len(fs)

    def body(*refs):
        ins, outs = refs[:n], refs[n:2 * n]
        send_sems, recv_sems = refs[2 * n:]
        x, y, c = lax.axis_index("x"), lax.axis_index("y"), lax.axis_index("c")
        cps = [pltpu.make_async_remote_copy(src_ref=ins[t], dst_ref=outs[t], send_sem=send_sems.at[t],
                                            recv_sem=recv_sems.at[t], device_id=(x, y, 1 - c), device_id_type=MESH)
               for t in range(n)]
        for cp in cps:
            cp.start()
        for cp in cps:
            cp.wait()

    return pl.pallas_call(
        body, name="grad_sibling_swap", in_specs=[HBM_SPEC] * n, out_specs=[HBM_SPEC] * n,
        out_shape=[SDS(f.shape, f.dtype) for f in fs],
        scratch_shapes=[pltpu.SemaphoreType.DMA((n,)), pltpu.SemaphoreType.DMA((n,))],
    )(*fs)


def _row_tile(rows, cap=512):
    best = SUBLANES
    for tr in range(SUBLANES, min(rows, cap) + 1, SUBLANES):
        if rows % tr == 0:
            best = tr
    return best


def _add_halves(g4, r4, c_idx, name):
    _, hr, lanes = r4.shape
    tr = _row_tile(hr)
    nb = hr // tr

    def body(c_ref, a_ref, b_ref, o_ref):
        o_ref[...] = (a_ref[...] + b_ref[...]).astype(o_ref.dtype)

    grid_spec = pltpu.PrefetchScalarGridSpec(
        num_scalar_prefetch=1, grid=(N_CHIPS, nb),
        in_specs=[pl.BlockSpec((None, tr, lanes), lambda q, i, c_ref: (q, c_ref[0] * nb + i, 0)),
                  pl.BlockSpec((None, tr, lanes), lambda q, i, c_ref: (q, i, 0))],
        out_specs=pl.BlockSpec((None, tr, lanes), lambda q, i, c_ref: (q, i, 0)))
    return pl.pallas_call(body, name=name, grid_spec=grid_spec, out_shape=SDS(r4.shape, BF16),
                          compiler_params=_params(("parallel", "parallel")))(c_idx, g4, r4)


def _sum_chips(r4, s4, me_idx, name):
    _, rows, lanes = r4.shape
    tr = _row_tile(rows)

    def body(me_ref, a_ref, b_ref, c_ref, d_ref, own_ref, o_ref):
        own = own_ref[...].astype(F32)
        p = [jnp.where(me_ref[0] == q, own, ref[...].astype(F32)) for q, ref in enumerate((a_ref, b_ref, c_ref, d_ref))]
        o_ref[...] = ((p[0] + p[1]) + p[2]) + p[3]

    other = lambda q: (lambda i, me_ref: (jnp.where(me_ref[0] == q, (q + 1) % N_CHIPS, q), i, 0))
    grid_spec = pltpu.PrefetchScalarGridSpec(
        num_scalar_prefetch=1, grid=(rows // tr,),
        in_specs=[pl.BlockSpec((None, tr, lanes), other(q)) for q in range(N_CHIPS)]
        + [pl.BlockSpec((None, tr, lanes), lambda i, me_ref: (me_ref[0], i, 0))],
        out_specs=pl.BlockSpec((tr, lanes), lambda i, me_ref: (i, 0)))
    return pl.pallas_call(body, name=name, grid_spec=grid_spec, out_shape=SDS((rows, lanes), F32),
                          compiler_params=_params(("parallel",)))(me_idx, r4, r4, r4, r4, s4)


def _adamw(wf, g_own, g_other, mf, vf, c_idx, name):
    rows, lanes = wf.shape
    hr = rows // 2
    tr = _row_tile(hr)
    nb = hr // tr
    c1 = 1.0 / (1.0 - ADAM_B1 ** ADAM_STEP)
    c2 = 1.0 / (1.0 - ADAM_B2 ** ADAM_STEP)

    def body(c_ref, w_ref, go_ref, gx_ref, m_ref, v_ref, g_ref, d_ref, nm_ref, nv_ref):
        gv = jnp.where(pl.program_id(0) == c_ref[0], go_ref[...], gx_ref[...])
        m = ADAM_B1 * m_ref[...] + (1.0 - ADAM_B1) * gv
        v = ADAM_B2 * v_ref[...] + (1.0 - ADAM_B2) * (gv * gv)
        g_ref[...] = gv
        d_ref[...] = -ADAM_LR * ((m * c1) / (jnp.sqrt(v * c2) + ADAM_EPS) + ADAM_WD * w_ref[...])
        nm_ref[...] = m
        nv_ref[...] = v

    full = pl.BlockSpec((tr, lanes), lambda h, i, c_ref: (h * nb + i, 0))
    half = pl.BlockSpec((tr, lanes), lambda h, i, c_ref: (i, 0))
    grid_spec = pltpu.PrefetchScalarGridSpec(num_scalar_prefetch=1, grid=(2, nb),
                                             in_specs=[full, half, half, full, full], out_specs=[full] * 4)
    return pl.pallas_call(body, name=name, grid_spec=grid_spec, out_shape=[SDS((rows, lanes), F32)] * 4,
                          compiler_params=_params(("parallel", "parallel")))(c_idx, wf, g_own, g_other, mf, vf)


BIG = ("ffn1_w_gate", "ffn1_w_up", "ffn1_w_down", "ffn2_w_gate", "ffn2_w_up", "ffn2_w_down", "w_out", "w_in")
TRANSPOSED = ("ffn1_w_gate", "ffn1_w_up", "ffn2_w_gate", "ffn2_w_up")
PACKED = ("rwkv_w2", "rwkv_a2", "rwkv_g2")
SMALL_SHAPES = {"ffn1_norm": (1, D_MODEL), "mix_norm": (1, D_MODEL), "hgrn_lb_logits": (2, W_A),
                "hgrn_out_norm": (1, W_A), "rwkv_shift_mu": (1, N_RWKV_COLS), "rwkv_w0": (1, W_B),
                "rwkv_a0": (1, W_B), "rwkv_k_k": (1, W_B), "rwkv_k_a": (1, W_B),
                "rwkv_r_k": (1, HB_HEADS, HB_DIM), "rwkv_gn_w": (1, W_B), "rwkv_gn_b": (1, W_B),
                "ffn2_norm": (1, D_MODEL), "final_norm": (D_MODEL,)}
PACK_ELEMS = sum(_numel(_shard_shape(n)) for n in PACKED) + sum(_numel(SMALL_SHAPES[n]) for n in SMALL)
PACK_ROWS = -(-PACK_ELEMS // (32 * LANES)) * 32


def _to_rows(name, shard):
    return shard[0].T if name in TRANSPOSED else shard[0]


def _from_rows(name, rows):
    return (rows.T if name in TRANSPOSED else rows)[None]


def _pack(sharded, small):
    flat = jnp.concatenate([sharded[n].reshape(-1) for n in PACKED] + [small[n].reshape(-1) for n in SMALL])
    return jnp.pad(flat, (0, PACK_ROWS * LANES - flat.shape[0])).reshape(PACK_ROWS, LANES)


def _unpack(packed):
    flat, out, off = packed.reshape(-1), {}, 0
    for n in PACKED:
        shp = _shard_shape(n)
        out[n] = flat[off:off + _numel(shp)].reshape((1,) + shp)
        off += _numel(shp)
    for n in SMALL:
        shp = SMALL_SHAPES[n]
        out[n] = flat[off:off + _numel(shp)].reshape(shp)
        off += _numel(shp)
    return out


def _quarter(full, name, q):
    shape, ax = SHARDED_SHAPES[name]
    w = shape[ax] // N_CHIPS
    return lax.slice_in_dim(full, q * w, (q + 1) * w, axis=ax)


def kernel(x, ffn1_norm, ffn1_w_gate, ffn1_w_up, ffn1_w_down, mix_norm, w_in, hgrn_lb_logits, hgrn_out_norm, rwkv_shift_mu, rwkv_w0, rwkv_w2, rwkv_a0, rwkv_a2, rwkv_g2, rwkv_k_k, rwkv_k_a, rwkv_r_k, rwkv_gn_w, rwkv_gn_b, w_out, ffn2_norm, ffn2_w_gate, ffn2_w_up, ffn2_w_down, final_norm, loss_target, m_ffn1_norm, m_ffn1_w_gate, m_ffn1_w_up, m_ffn1_w_down, m_mix_norm, m_w_in, m_hgrn_lb_logits, m_hgrn_out_norm, m_rwkv_shift_mu, m_rwkv_w0, m_rwkv_w2, m_rwkv_a0, m_rwkv_a2, m_rwkv_g2, m_rwkv_k_k, m_rwkv_k_a, m_rwkv_r_k, m_rwkv_gn_w, m_rwkv_gn_b, m_w_out, m_ffn2_norm, m_ffn2_w_gate, m_ffn2_w_up, m_ffn2_w_down, m_final_norm, v_ffn1_norm, v_ffn1_w_gate, v_ffn1_w_up, v_ffn1_w_down, v_mix_norm, v_w_in, v_hgrn_lb_logits, v_hgrn_out_norm, v_rwkv_shift_mu, v_rwkv_w0, v_rwkv_w2, v_rwkv_a0, v_rwkv_a2, v_rwkv_g2, v_rwkv_k_k, v_rwkv_k_a, v_rwkv_r_k, v_rwkv_gn_w, v_rwkv_gn_b, v_w_out, v_ffn2_norm, v_ffn2_w_gate, v_ffn2_w_up, v_ffn2_w_down, v_final_norm):
    args = dict(locals())
    wts = {n: args[n] for n in ALL_WEIGHTS}
    moms = {n: args["m_" + n] for n in ALL_WEIGHTS}
    vars_ = {n: args["v_" + n] for n in ALL_WEIGHTS}

    small_w = {n: wts[n] for n in SMALL}
    shards = [_to_rows(n, wts[n]).astype(BF16) for n in BIG] + [_pack(wts, small_w).astype(BF16)]
    me = 2 * lax.axis_index("x") + lax.axis_index("y")
    gathered = _gather_weights([lax.dynamic_update_slice(jnp.zeros((N_CHIPS,) + s.shape, BF16), s[None], (me, 0, 0))
                                for s in shards])
    rows_of = dict(zip(BIG, gathered[:-1]))
    packs = gathered[-1].reshape(N_CHIPS, PACK_ROWS * LANES)
    full, off = {}, 0
    for n in PACKED:
        shp = _shard_shape(n)
        full[n] = jnp.concatenate([packs[q, off:off + _numel(shp)].reshape(shp) for q in range(N_CHIPS)], axis=1)
        off += _numel(shp)

    w = {}
    for tag in ("ffn1", "ffn2"):
        w[f"{tag}_wgt"] = rows_of[f"{tag}_w_gate"].reshape(D_FF, D_MODEL)
        w[f"{tag}_wut"] = rows_of[f"{tag}_w_up"].reshape(D_FF, D_MODEL)
        w[f"{tag}_wd"] = rows_of[f"{tag}_w_down"].reshape(D_FF, D_MODEL)
        w[f"{tag}_norm"] = wts[f"{tag}_norm"]
    w_in_full = jnp.concatenate([rows_of["w_in"][q] for q in range(N_CHIPS)], axis=1)
    w["w_in_h"] = w_in_full[:, :N_HGRN_COLS]
    w["w_in_r"] = jnp.pad(w_in_full[:, N_HGRN_COLS:], ((0, 0), (0, N_RWKV_PAD - N_RWKV_COLS)))
    w_out_full = rows_of["w_out"].reshape(D_MODEL, D_MODEL)
    w["w_out_a"], w["w_out_b"] = w_out_full[:W_A], w_out_full[W_A:]
    zrow = lambda nrow: jnp.zeros((nrow, W_B), BF16)
    w["w2_pad"] = jnp.concatenate([full["rwkv_w2"], zrow(LORA_PAD - 32)], axis=0)
    w["a2_pad"] = jnp.concatenate([zrow(32), full["rwkv_a2"], zrow(LORA_PAD - 64)], axis=0)
    w["g2_pad"] = jnp.concatenate([zrow(64), full["rwkv_g2"], zrow(LORA_PAD - 160)], axis=0)
    w["mix_norm"] = mix_norm
    w["lb0"], w["lb1"] = hgrn_lb_logits[0:1], hgrn_lb_logits[1:2]
    w["hgrn_out_norm"] = hgrn_out_norm
    w["mu_pad"] = jnp.pad(rwkv_shift_mu, ((0, 0), (0, N_RWKV_PAD - N_RWKV_COLS)))
    for n in ("rwkv_w0", "rwkv_a0", "rwkv_k_k", "rwkv_k_a", "rwkv_gn_w", "rwkv_gn_b"):
        w[n] = wts[n]
    w["rwkv_r_k"] = rwkv_r_k.reshape(1, W_B)
    w["final_norm"] = final_norm.reshape(1, D_MODEL)

    loss_slab, grad_x, g = _local_step(x[0], loss_target[0], w)
    loss = lax.psum(loss_slab[0, 0], ("x", "y", "c"))

    grows = {
        "ffn1_w_gate": g["ffn1_wgt"], "ffn1_w_up": g["ffn1_wut"], "ffn1_w_down": g["ffn1_wd"],
        "ffn2_w_gate": g["ffn2_wgt"], "ffn2_w_up": g["ffn2_wut"], "ffn2_w_down": g["ffn2_wd"],
        "w_out": jnp.concatenate([g["w_out_a"], g["w_out_b"]], axis=0),
    }
    g_w_in = jnp.concatenate([g["w_in_h"], g["w_in_r"][:, :N_RWKV_COLS]], axis=1)
    gfull = {
        "rwkv_w2": g["w2_pad"][0:32], "rwkv_a2": g["a2_pad"][32:64], "rwkv_g2": g["g2_pad"][64:160],
    }
    gsmall = {
        "ffn1_norm": g["ffn1_norm"], "mix_norm": g["mix_norm"],
        "hgrn_lb_logits": jnp.concatenate([g["lb0"], g["lb1"]], axis=0), "hgrn_out_norm": g["hgrn_out_norm"],
        "rwkv_shift_mu": g["mu_pad"][:, :N_RWKV_COLS], "rwkv_w0": g["rwkv_w0"], "rwkv_a0": g["rwkv_a0"],
        "rwkv_k_k": g["rwkv_k_k"], "rwkv_k_a": g["rwkv_k_a"], "rwkv_r_k": g["rwkv_r_k"],
        "rwkv_gn_w": g["rwkv_gn_w"], "rwkv_gn_b": g["rwkv_gn_b"], "ffn2_norm": g["ffn2_norm"],
        "final_norm": g["final_norm"],
    }
    gs = [grows[n].reshape(N_CHIPS, -1, D_MODEL) for n in BIG if n != "w_in"]
    gs.append(jnp.stack([_quarter(g_w_in, "w_in", q) for q in range(N_CHIPS)]))
    gs.append(jnp.stack([_pack({n: _quarter(gfull[n], n, q) for n in PACKED}, gsmall) for q in range(N_CHIPS)]))
    names = list(BIG) + ["packed"]
    c_idx = lax.axis_index("c").astype(jnp.int32).reshape(1)
    me_idx = me.astype(jnp.int32).reshape(1)
    r1 = _sibling_exchange(gs)
    s4 = [_add_halves(gt, rt, c_idx, f"grad_add_halves_{n}") for gt, rt, n in zip(gs, r1, names)]
    r2 = _chip_exchange(s4)
    own = [_sum_chips(rt, st, me_idx, f"grad_sum_chips_{n}") for rt, st, n in zip(r2, s4, names)]
    other = _sibling_swap(own)

    def rows_list(d):
        return [_to_rows(n, d[n]) for n in BIG] + [_pack(d, {n: d[n] for n in SMALL})]

    outs = [_adamw(wt, go, gx, mt, vt, c_idx, f"adamw_{n}")
            for wt, go, gx, mt, vt, n in zip(rows_list(wts), own, other, rows_list(moms), rows_list(vars_), names)]
    results = []
    for k in range(4):
        per = [outs[i][k] for i in range(len(names))]
        d = {n: _from_rows(n, z) for n, z in zip(BIG, per[:-1])}
        d.update(_unpack(per[-1]))
        results.append(d)
    return (loss, grad_x[None], *[r[n] for r in results for n in ALL_WEIGHTS])
```

```python
import collections
import functools

import jax
import jax.numpy as jnp
from jax import lax
from jax.experimental import pallas as pl
from jax.experimental.pallas import tpu as pltpu

F32 = jnp.float32
BF16 = jnp.bfloat16
SDS = jax.ShapeDtypeStruct
MESH = pl.DeviceIdType.MESH

D_MODEL = 1024
D_FF = 2816
W_A = 512
W_B = 512
HA_HEADS, HA_DIM = 4, 128
HB_HEADS, HB_DIM = 8, 64
HGRN_CHUNK = 64
HGRN_GROUP = 2
RWKV_CHUNK = 16
RWKV_GROUP = 4
N_HGRN_COLS = 4 * W_A
N_RWKV_COLS = 3 * W_B + 32 + 32 + 96
N_RWKV_PAD = 1792
LORA_PAD = 256
NORM_EPS = 1e-6
RWKV_GN_EPS = 64e-5
L2_EPS = 1e-12
ADAM_LR, ADAM_B1, ADAM_B2, ADAM_EPS, ADAM_WD, ADAM_STEP = 0.001, 0.9, 0.999, 1e-8, 0.01, 10

N_CHIPS = 4
VMEM_LIMIT_V7X = 56 * 1024 * 1024
LANES = 1024

SHARDED_SHAPES = {
    "ffn1_w_gate": ((D_MODEL, D_FF), 1), "ffn1_w_up": ((D_MODEL, D_FF), 1), "ffn1_w_down": ((D_FF, D_MODEL), 0),
    "w_in": ((D_MODEL, N_HGRN_COLS + N_RWKV_COLS), 1), "rwkv_w2": ((32, W_B), 1), "rwkv_a2": ((32, W_B), 1),
    "rwkv_g2": ((96, W_B), 1), "w_out": ((D_MODEL, D_MODEL), 0),
    "ffn2_w_gate": ((D_MODEL, D_FF), 1), "ffn2_w_up": ((D_MODEL, D_FF), 1), "ffn2_w_down": ((D_FF, D_MODEL), 0),
}
SMALL = ("ffn1_norm", "mix_norm", "hgrn_lb_logits", "hgrn_out_norm", "rwkv_shift_mu", "rwkv_w0", "rwkv_a0",
         "rwkv_k_k", "rwkv_k_a", "rwkv_r_k", "rwkv_gn_w", "rwkv_gn_b", "ffn2_norm", "final_norm")
ALL_WEIGHTS = ("ffn1_norm", "ffn1_w_gate", "ffn1_w_up", "ffn1_w_down", "mix_norm", "w_in", "hgrn_lb_logits",
               "hgrn_out_norm", "rwkv_shift_mu", "rwkv_w0", "rwkv_w2", "rwkv_a0", "rwkv_a2", "rwkv_g2", "rwkv_k_k",
               "rwkv_k_a", "rwkv_r_k", "rwkv_gn_w", "rwkv_gn_b", "w_out", "ffn2_norm", "ffn2_w_gate", "ffn2_w_up",
               "ffn2_w_down", "final_norm")


def _shard_shape(name):
    shape, ax = SHARDED_SHAPES[name]
    return tuple(s // N_CHIPS if i == ax else s for i, s in enumerate(shape))


def _numel(shape):
    n = 1
    for s in shape:
        n *= s
    return n


def _params(sem=None):
    return pltpu.CompilerParams(dimension_semantics=sem, vmem_limit_bytes=VMEM_LIMIT_V7X)


def _split2(x):
    hi = x.astype(BF16)
    return hi, (x.astype(F32) - hi.astype(F32)).astype(BF16)


def _dg(x, y, cx, cy, hi):
    dn = (((cx,), (cy,)), ((), ()))
    dot = lambda p, q: lax.dot_general(p, q, dn, preferred_element_type=F32)
    if hi == "x3":
        (xh, xl), (yh, yl) = _split2(x), _split2(y)
        return dot(xh, yh) + (dot(xh, yl) + dot(xl, yh))
    return dot(x.astype(BF16), y.astype(BF16))


def _make_mm(hi, cotangent_forms=None):
    @jax.custom_vjp
    def nn(x, y):
        return _dg(x, y, 1, 0, hi)

    @jax.custom_vjp
    def nt(x, y):
        return _dg(x, y, 1, 1, hi)

    @jax.custom_vjp
    def tn(x, y):
        return _dg(x, y, 0, 0, hi)

    bnn, bnt, btn = cotangent_forms or (nn, nt, tn)
    nn.defvjp(lambda x, y: (nn(x, y), (x, y)), lambda r, g: (bnt(g, r[1]), btn(r[0], g)))
    nt.defvjp(lambda x, y: (nt(x, y), (x, y)), lambda r, g: (bnn(g, r[1]), btn(g, r[0])))
    tn.defvjp(lambda x, y: (tn(x, y), (x, y)), lambda r, g: (bnt(r[1], g), bnn(r[0], g)))
    return nn, nt, tn


_nn, _nt, _tn = _make_mm(False)
_nn_x3, _nt_x3, _tn_x3 = _make_mm("x3", (_nn, _nt, _tn))


def _tri_apply(x, transpose):
    c = x.shape[0]
    tri = (lax.broadcasted_iota(jnp.int32, (c, c), 1) <= lax.broadcasted_iota(jnp.int32, (c, c), 0)).astype(BF16)
    dn = (((0 if transpose else 1,), (0,)), ((), ()))
    p1 = x.astype(BF16)
    r1 = x - p1.astype(F32)
    p2 = r1.astype(BF16)
    p3 = (r1 - p2.astype(F32)).astype(BF16)
    dot = lambda p: lax.dot_general(tri, p, dn, preferred_element_type=F32)
    return dot(p1) + (dot(p2) + dot(p3))


@jax.custom_vjp
def _cumsum_rows(x):
    return _tri_apply(x, False)


_cumsum_rows.defvjp(lambda x: (_tri_apply(x, False), None), lambda _, g: (_tri_apply(g, True),))


def _sigmoid(x):
    return 1.0 / (1.0 + jnp.exp(-x))


def _silu(x):
    return x * _sigmoid(x)


def _softplus(z):
    return jnp.maximum(z, 0.0) + jnp.log(1.0 + jnp.exp(-jnp.abs(z)))


def _mm(a, b, *, ta=False, tb=False, tm, tn, tk, name, out_dtype=F32, res=None, scale=None):
    m = a.shape[1] if ta else a.shape[0]
    kdim = a.shape[0] if ta else a.shape[1]
    n = b.shape[0] if tb else b.shape[1]
    assert (b.shape[1] if tb else b.shape[0]) == kdim
    tm, tn, tk = min(tm, m), min(tn, n), min(tk, kdim)
    assert m % tm == 0 and n % tn == 0 and kdim % tk == 0, (name, m, n, kdim)
    nk = kdim // tk
    a_spec = pl.BlockSpec((tk, tm), lambda i, j, k: (k, i)) if ta else pl.BlockSpec((tm, tk), lambda i, j, k: (i, k))
    b_spec = pl.BlockSpec((tn, tk), lambda i, j, k: (j, k)) if tb else pl.BlockSpec((tk, tn), lambda i, j, k: (k, j))
    o_spec = pl.BlockSpec((tm, tn), lambda i, j, k: (i, j))
    ca, cb = (0 if ta else 1), (1 if tb else 0)

    def body(*refs):
        if res is not None:
            a_ref, b_ref, r_ref, o_ref, acc_ref = refs
        else:
            a_ref, b_ref, o_ref, acc_ref = refs
        k = pl.program_id(2)

        @pl.when(k == 0)
        def _():
            acc_ref[...] = jnp.zeros_like(acc_ref)

        acc_ref[...] += _dg(a_ref[...], b_ref[...], ca, cb, False)

        @pl.when(k == nk - 1)
        def _():
            acc = acc_ref[...]
            if scale is not None:
                acc = acc * scale
            if res is not None:
                acc = r_ref[...] + acc
            o_ref[...] = acc.astype(out_dtype)

    in_specs = [a_spec, b_spec] + ([o_spec] if res is not None else [])
    args = (a, b) + ((res,) if res is not None else ())
    return pl.pallas_call(
        body, name=name, grid=(m // tm, n // tn, nk), in_specs=in_specs, out_specs=o_spec,
        out_shape=SDS((m, n), out_dtype), scratch_shapes=[pltpu.VMEM((tm, tn), F32)],
        compiler_params=_params(("parallel", "parallel", "arbitrary")))(*args)


def _row_spec(x, tm):
    if isinstance(x, tuple):
        arr, w, j = x
        return arr, pl.BlockSpec((tm, w), lambda i, j=j: (i, j))
    return x, pl.BlockSpec((tm, x.shape[1]), lambda i: (i, 0))


def _par_spec(p):
    if isinstance(p, tuple):
        arr, w, j = p
        return arr, pl.BlockSpec((arr.shape[0], w), lambda i, j=j: (0, j))
    return p, pl.BlockSpec(p.shape, lambda i: (0, 0))


def _store_groups(refs, groups, vals):
    for ref, idxs in zip(refs, groups):
        off = 0
        for ix in idxs:
            v = vals[ix]
            ref[:, off:off + v.shape[1]] = v.astype(ref.dtype)
            off += v.shape[1]


SUBLANES = 8


def _x_plan(xs, tm, t):
    arrays, specs, plan = [], [], []
    nb = tm // SUBLANES
    for x in xs:
        if isinstance(x, tuple) and isinstance(x[0], str):
            kind, arr, w, j = x
            if kind == "prev":
                halo = lambda i, j=j: (jnp.maximum(i * nb - 1, 0), j)
            else:
                halo = lambda i, j=j: (jnp.minimum((i + 1) * nb, t // SUBLANES - 1), j)
            arrays += [arr, arr]
            specs += [pl.BlockSpec((tm, w), lambda i, j=j: (i, j)), pl.BlockSpec((SUBLANES, w), halo)]
            plan.append((kind, 2, w))
        else:
            arr, spec = _row_spec(x, tm)
            arrays.append(arr)
            specs.append(spec)
            plan.append(("plain", 1, spec.block_shape[1]))
    return arrays, specs, plan


def _x_vals(refs, plan, tm, nt):
    vals, k = [], 0
    i = pl.program_id(0)
    rows = lax.broadcasted_iota(jnp.int32, (tm, 1), 0)
    for kind, n, _ in plan:
        main = refs[k][...].astype(F32)
        if kind == "prev":
            edge = jnp.where(i == 0, 0.0, refs[k + 1][SUBLANES - 1:SUBLANES, :].astype(F32))
            main = jnp.where(rows == 0, edge, pltpu.roll(main, 1, 0))
        elif kind == "next":
            edge = jnp.where(i == nt - 1, 0.0, refs[k + 1][0:1, :].astype(F32))
            main = jnp.where(rows == tm - 1, edge, pltpu.roll(main, tm - 1, 0))
        vals.append(main)
        k += n
    return vals


def _tile_rows(xs, tm):
    arr = xs[0]
    if isinstance(arr, tuple):
        arr = arr[1] if isinstance(arr[0], str) else arr[0]
    return min(tm, arr.shape[0]), arr.shape[0]


def _rowwise(f, xs, params, out_groups, out_dtypes, *, tm, name):
    tm, t = _tile_rows(xs, tm)
    nt = t // tm
    xa, xspecs, plan = _x_plan(xs, tm, t)
    pa, pspecs = (zip(*[_par_spec(p) for p in params]) if params else ((), ()))
    nxr, npar = len(xa), len(pa)
    x_sds = [SDS((tm, w), F32) for _, _, w in plan]
    p_sds = [SDS(s.block_shape, F32) for s in pspecs]
    outs_sds = jax.eval_shape(lambda *vals: f(*vals), *x_sds, *p_sds)
    widths = [sum(outs_sds[ix].shape[1] for ix in idxs) for idxs in out_groups]

    def body(*refs):
        vals = _x_vals(refs[:nxr], plan, tm, nt) + [r[...].astype(F32) for r in refs[nxr:nxr + npar]]
        outs = f(*vals)
        _store_groups(refs[nxr + npar:], out_groups, outs)

    return pl.pallas_call(
        body, name=name, grid=(nt,), in_specs=list(xspecs) + list(pspecs),
        out_specs=[pl.BlockSpec((tm, w), lambda i: (i, 0)) for w in widths],
        out_shape=[SDS((t, w), dt) for w, dt in zip(widths, out_dtypes)],
        compiler_params=_params(("parallel",)))(*xa, *pa)


def _rowwise_bwd(f, xs, params, cots, *, x_grad, p_grad, dx_groups, dx_dtypes, tm, name, extra=None):
    tm, t = _tile_rows(xs, tm)
    nt = t // tm
    xa, xspecs, plan = _x_plan(xs, tm, t)
    pa, pspecs = (zip(*[_par_spec(p) for p in params]) if params else ((), ()))
    ca, cspecs = zip(*[_row_spec(c, tm) for c in cots])
    extra = extra or {}
    ekeys = sorted(extra)
    ea, especs = (zip(*[_row_spec(extra[k], tm) for k in ekeys]) if ekeys else ((), ()))
    nx, nxr, npar, nc, ne = len(plan), len(xa), len(pa), len(ca), len(ea)
    gx = [i for i in range(nx) if x_grad[i]]
    gp = [i for i in range(npar) if p_grad[i]]
    widths = [sum(plan[gx[ix]][2] for ix in idxs) for idxs in dx_groups]
    ng = len(dx_groups)

    def body(*refs):
        ins = refs[:nxr + npar + nc + ne]
        outs = refs[nxr + npar + nc + ne:]
        vals = _x_vals(ins[:nxr], plan, tm, nt) + [r[...].astype(F32) for r in ins[nxr:nxr + npar]]
        cvals = tuple(r[...].astype(F32) for r in ins[nxr + npar:nxr + npar + nc])
        evals = [r[...].astype(F32) for r in ins[nxr + npar + nc:]]
        diff_idx = gx + [nx + i for i in gp]

        def g(*dargs):
            full = list(vals)
            for ix, v in zip(diff_idx, dargs):
                full[ix] = v
            return tuple(f(*full))

        _, vjp = jax.vjp(g, *[vals[ix] for ix in diff_idx])
        grads = vjp(cvals)
        dxs = list(grads[:len(gx)])
        for k, ev in zip(ekeys, evals):
            dxs[k] = dxs[k] + ev
        _store_groups(outs[:ng], dx_groups, dxs)
        i = pl.program_id(0)
        for ref, gval in zip(outs[ng:], grads[len(gx):]):
            @pl.when(i == 0)
            def _(ref=ref):
                ref[...] = jnp.zeros_like(ref)
            ref[...] += gval

    dp_specs = [pl.BlockSpec(pspecs[i].block_shape, lambda i: (0, 0)) for i in gp]
    dp_shapes = [SDS(pspecs[i].block_shape, F32) for i in gp]
    return pl.pallas_call(
        body, name=name, grid=(nt,), in_specs=list(xspecs) + list(pspecs) + list(cspecs) + list(especs),
        out_specs=[pl.BlockSpec((tm, w), lambda i: (i, 0)) for w in widths] + dp_specs,
        out_shape=[SDS((t, w), dt) for w, dt in zip(widths, dx_dtypes)] + dp_shapes,
        compiler_params=_params(("arbitrary",)))(*xa, *pa, *ca, *ea)


def _rms_f(x, g):
    return (x * lax.rsqrt(jnp.mean(x * x, axis=-1, keepdims=True) + NORM_EPS) * g,)


def _three_pieces(x):
    p1 = x.astype(BF16)
    r1 = x - p1.astype(F32)
    p2 = r1.astype(BF16)
    return p1, p2, (r1 - p2.astype(F32)).astype(BF16)


def _group_sum_impl(x, ones_bd):
    p1, p2, p3 = _three_pieces(x)
    dot = lambda p: lax.dot_general(p, ones_bd.astype(BF16), (((1,), (0,)), ((), ())), preferred_element_type=F32)
    return dot(p1) + (dot(p2) + dot(p3))


@jax.custom_vjp
def _group_sum(x, ones_bd):
    return _group_sum_impl(x, ones_bd)


_group_sum.defvjp(lambda x, o: (_group_sum_impl(x, o), o),
                  lambda o, g: (_group_sum_impl(g, o), jnp.zeros_like(o)))


def _rwkv_prep_f(r, k, v, lo, rp, kp, vp, lop, mu_r, mu_k, mu_v, mu_lo, w0, w2p, a0, a2p, g2p, k_k, k_a, ones_bd):
    r = r + mu_r * (rp - r)
    k = k + mu_k * (kp - k)
    v = v + mu_v * (vp - v)
    lo = lo + mu_lo * (lop - lo)
    w_log = -_softplus(-(w0 + _nn(jnp.tanh(lo), w2p))) - 0.5
    lw = -jnp.exp(w_log)
    a_g = _sigmoid(a0 + _nn(lo, a2p))
    g = _nn(_sigmoid(lo), g2p)
    kk = k * k_k
    kk = kk / jnp.maximum(jnp.sqrt(_group_sum(kk * kk, ones_bd)), L2_EPS)
    k2 = k * (1.0 + (a_g - 1.0) * k_a)
    return r, lw, k2, v, -kk, kk * a_g, g


def _rwkv_post_f(y, r, k2, v, g, r_k, gn_w, gn_b, ones_bd):
    inv_n = 1.0 / HB_DIM
    mean = _group_sum(y, ones_bd) * inv_n
    yc = y - mean
    var = _group_sum(yc * yc, ones_bd) * inv_n
    yn = yc * lax.rsqrt(var + RWKV_GN_EPS) * gn_w + gn_b
    bonus = _group_sum(r * k2 * r_k, ones_bd) * v
    return ((yn + bonus) * g,)


def _tri(c, strict=False):
    ii = lax.broadcasted_iota(jnp.int32, (c, c), 0)
    jj = lax.broadcasted_iota(jnp.int32, (c, c), 1)
    return (jj < ii) if strict else (jj <= ii)


def _hgrn_step(st0, q_a, f_a, i_a, g_a, l0, l1, onorm):
    nh, nj = len(q_a), len(q_a[0])
    c = q_a[0][0].shape[0]
    combos = [(j, h) for j in range(nj) for h in range(nh)]
    every = lambda fn: {q: fn(q) for q in combos}
    at_ = lambda d: (lambda q: d[q[1]][q[0]])
    qa_, fa_, ia_, ga_ = (at_(z) for z in (q_a, f_a, i_a, g_a))
    incl = _tri(c)
    rows = lax.broadcasted_iota(jnp.int32, (c, 1), 0)
    lb = []
    for h in range(nh):
        mx = jnp.maximum(l0[h], l1[h])
        e0, e1 = jnp.exp(l0[h] - mx), jnp.exp(l1[h] - mx)
        lb.append(e0 / (e0 + e1))
    forget = every(lambda q: lb[q[1]] + (1.0 - lb[q[1]]) * _sigmoid(fa_(q)))
    qs = every(lambda q: _silu(qa_(q)))
    kk = every(lambda q: 1.0 - forget[q])
    lf = every(lambda q: jnp.log(forget[q]))
    bcum = every(lambda q: _cumsum_rows(lf[q]))
    bref = every(lambda q: jnp.sum(jnp.where(rows <= c // 2, lf[q], 0.0), axis=0, keepdims=True))
    blast = every(lambda q: jnp.sum(lf[q], axis=0, keepdims=True))
    scores = every(lambda q: jnp.where(incl, _nt(qs[q] * jnp.exp(bcum[q] - bref[q]),
                                                 kk[q] * jnp.exp(bref[q] - bcum[q])), 0.0))
    intra = every(lambda q: _nn(scores[q], ia_(q)))
    qb = every(lambda q: qs[q] * jnp.exp(bcum[q]))
    upd = every(lambda q: _tn(ia_(q), kk[q] * jnp.exp(blast[q] - bcum[q])))
    dec = every(lambda q: jnp.exp(blast[q]))
    st = list(st0)
    o = {}
    for j in range(nj):
        for h in range(nh):
            o[(j, h)] = intra[(j, h)] + _nt(qb[(j, h)], st[h])
        st = [st[h] * dec[(j, h)] + upd[(j, h)] for h in range(nh)]
    out = every(lambda q: o[q] * lax.rsqrt(jnp.mean(o[q] * o[q], axis=-1, keepdims=True) + NORM_EPS)
                * onorm[q[1]] * _silu(ga_(q)))
    return [[out[(j, h)] for j in range(nj)] for h in range(nh)], st


def _hgrn_blocks(ref, nj, c):
    return [[ref[j * c:(j + 1) * c, h * HA_DIM:(h + 1) * HA_DIM] for j in range(nj)] for h in range(HA_HEADS)]


def _hgrn_cols(ref):
    return [ref[:, h * HA_DIM:(h + 1) * HA_DIM] for h in range(HA_HEADS)]


def _hgrn_fwd(p_h, l0, l1, onorm):
    t = p_h.shape[0]
    cc, nj = HGRN_CHUNK, HGRN_GROUP
    c = cc * nj
    n = t // c

    def body(q_ref, f_ref, i_ref, g_ref, l0_ref, l1_ref, on_ref, o_ref, hs_ref, st_ref):
        @pl.when(pl.program_id(0) == 0)
        def _():
            st_ref[...] = jnp.zeros_like(st_ref)

        hs_ref[0] = st_ref[...]
        o, st1 = _hgrn_step([st_ref[h] for h in range(HA_HEADS)],
                            *[_hgrn_blocks(ref, nj, cc) for ref in (q_ref, f_ref, i_ref, g_ref)],
                            _hgrn_cols(l0_ref), _hgrn_cols(l1_ref), _hgrn_cols(on_ref))
        for h in range(HA_HEADS):
            for j in range(nj):
                o_ref[j * cc:(j + 1) * cc, h * HA_DIM:(h + 1) * HA_DIM] = o[h][j]
            st_ref[h] = st1[h]

    col = lambda j: pl.BlockSpec((c, W_A), lambda i, j=j: (i, j))
    par = pl.BlockSpec((1, W_A), lambda i: (0, 0))
    return pl.pallas_call(
        body, name="hgrn_fwd", grid=(n,), in_specs=[col(0), col(1), col(2), col(3), par, par, par],
        out_specs=[pl.BlockSpec((c, W_A), lambda i: (i, 0)),
                   pl.BlockSpec((1, HA_HEADS, HA_DIM, HA_DIM), lambda i: (i, 0, 0, 0))],
        out_shape=[SDS((t, W_A), F32), SDS((n, HA_HEADS, HA_DIM, HA_DIM), F32)],
        scratch_shapes=[pltpu.VMEM((HA_HEADS, HA_DIM, HA_DIM), F32)],
        compiler_params=_params(("arbitrary",)))(p_h, p_h, p_h, p_h, l0, l1, onorm)


def _hgrn_bwd(p_h, l0, l1, onorm, hs, do, do_col, comm=None):
    t = p_h.shape[0]
    cc, nj = HGRN_CHUNK, HGRN_GROUP
    c = cc * nj
    n = t // c

    def body(q_ref, f_ref, i_ref, g_ref, l0_ref, l1_ref, on_ref, hs_ref, do_ref,
             dp_ref, dl0_ref, dl1_ref, don_ref, dst_ref):
        @pl.when(pl.program_id(0) == 0)
        def _():
            dst_ref[...] = jnp.zeros_like(dst_ref)
            dl0_ref[...] = jnp.zeros_like(dl0_ref)
            dl1_ref[...] = jnp.zeros_like(dl1_ref)
            don_ref[...] = jnp.zeros_like(don_ref)

        args = ([hs_ref[0, h] for h in range(HA_HEADS)],
                *[_hgrn_blocks(ref, nj, cc) for ref in (q_ref, f_ref, i_ref, g_ref)],
                _hgrn_cols(l0_ref), _hgrn_cols(l1_ref), _hgrn_cols(on_ref))
        _, vjp = jax.vjp(_hgrn_step, *args)
        dst0, dq, df, di, dg, dl0, dl1, don = vjp((_hgrn_blocks(do_ref, nj, cc),
                                                   [dst_ref[h] for h in range(HA_HEADS)]))
        for h in range(HA_HEADS):
            sl = slice(h * HA_DIM, (h + 1) * HA_DIM)
            for k, dv in enumerate((dq, df, di, dg)):
                for j in range(nj):
                    dp_ref[j * cc:(j + 1) * cc, k * W_A + h * HA_DIM:k * W_A + (h + 1) * HA_DIM] = dv[h][j]
            dl0_ref[:, sl] += dl0[h]
            dl1_ref[:, sl] += dl1[h]
            don_ref[:, sl] += don[h]
            dst_ref[h] = dst0[h]

    col = lambda j: pl.BlockSpec((c, W_A), lambda i, j=j: (n - 1 - i, j))
    par = pl.BlockSpec((1, W_A), lambda i: (0, 0))
    return _hosting_call(
        body, comm, name="hgrn_bwd", grid=(n,),
        in_specs=[col(0), col(1), col(2), col(3), par, par, par,
                  pl.BlockSpec((1, HA_HEADS, HA_DIM, HA_DIM), lambda i: (n - 1 - i, 0, 0, 0)),
                  pl.BlockSpec((c, W_A), lambda i: (n - 1 - i, do_col))],
        out_specs=[pl.BlockSpec((c, N_HGRN_COLS), lambda i: (n - 1 - i, 0)), par, par, par],
        out_shape=[SDS((t, N_HGRN_COLS), F32), SDS((1, W_A), F32), SDS((1, W_A), F32), SDS((1, W_A), F32)],
        scratch_shapes=[pltpu.VMEM((HA_HEADS, HA_DIM, HA_DIM), F32)],
        args=(p_h, p_h, p_h, p_h, l0, l1, onorm, hs, do))


HB_PAIRS = HB_HEADS // 2
PAIR_W = 2 * HB_DIM


def _head_lane_masks():
    lane = lax.broadcasted_iota(jnp.int32, (1, PAIR_W), 1)
    return (lane < HB_DIM).astype(F32), (lane >= HB_DIM).astype(F32)


@jax.custom_vjp
def _stack_heads(x):
    m0, m1 = _head_lane_masks()
    return jnp.concatenate([x * m0, x * m1], axis=0)


def _stack_heads_bwd(_, g):
    m0, m1 = _head_lane_masks()
    c = g.shape[0] // 2
    return (g[:c] * m0 + g[c:] * m1,)


_stack_heads.defvjp(lambda x: (_stack_heads(x), None), _stack_heads_bwd)


@jax.custom_vjp
def _unstack_heads(ys):
    c = ys.shape[0] // 2
    return ys[:c] + ys[c:]


_unstack_heads.defvjp(lambda ys: (_unstack_heads(ys), None), lambda _, g: (_stack_heads(g),))


def _same_head_block(c):
    ii = lax.broadcasted_iota(jnp.int32, (2 * c, 2 * c), 0)
    jj = lax.broadcasted_iota(jnp.int32, (2 * c, 2 * c), 1)
    same = (ii < c) == (jj < c)
    return same & (jj <= ii), same & (jj < ii), (ii == jj).astype(F32)


def _rwkv_step(s0, r, lw, k, v, a, b):
    npair, nj = len(r), len(r[0])
    c = r[0][0].shape[0]
    combos = [(j, p) for j in range(nj) for p in range(npair)]
    every = lambda fn: {q: fn(q) for q in combos}
    at_ = lambda d: (lambda q: d[q[1]][q[0]])
    r_, lw_, k_, v_, a_, b_ = (at_(z) for z in (r, lw, k, v, a, b))
    incl, strict, eye = _same_head_block(c)

    gam = every(lambda q: _cumsum_rows(lw_(q)))
    gtot = every(lambda q: jnp.sum(lw_(q), axis=0, keepdims=True))
    eneg = every(lambda q: jnp.exp(-gam[q]))
    edec = every(lambda q: jnp.exp(gtot[q] - gam[q]))
    at = every(lambda q: _stack_heads(a_(q) * jnp.exp(gam[q] - lw_(q))))
    rt = every(lambda q: _stack_heads(r_(q) * jnp.exp(gam[q])))
    bt = every(lambda q: _stack_heads(b_(q) * eneg[q]))
    kt = every(lambda q: _stack_heads(k_(q) * eneg[q]))
    bdec = every(lambda q: _stack_heads(b_(q) * edec[q]))
    kdec = every(lambda q: _stack_heads(k_(q) * edec[q]))
    vs = every(lambda q: _stack_heads(v_(q)))
    a_ab = every(lambda q: jnp.where(strict, _nt(at[q], bt[q]), 0.0))
    a_ak = every(lambda q: jnp.where(strict, _nt(at[q], kt[q]), 0.0))
    a_rb = every(lambda q: jnp.where(incl, _nt(rt[q], bt[q]), 0.0))
    a_rk = every(lambda q: jnp.where(incl, _nt(rt[q], kt[q]), 0.0))
    tinv = every(lambda q: eye + a_ab[q])
    pw = a_ab
    span = 2
    while span < c:
        pw = every(lambda q, pw=pw: _nn_x3(pw[q], pw[q]))
        tinv = every(lambda q, pw=pw, tinv=tinv: tinv[q] + _nn_x3(pw[q], tinv[q]))
        span *= 2
    akv = every(lambda q: _nn(a_ak[q], vs[q]))
    w1 = every(lambda q: _nn_x3(tinv[q], at[q]))
    u0 = every(lambda q: _nn_x3(tinv[q], akv[q]))
    r1 = every(lambda q: rt[q] + _nn(a_rb[q], w1[q]))
    y0 = every(lambda q: _nn(a_rb[q], u0[q]) + _nn(a_rk[q], vs[q]))
    mm = every(lambda q: _tn(w1[q], bdec[q]))
    zz = every(lambda q: _tn(u0[q], bdec[q]) + _tn(vs[q], kdec[q]))
    gdec = every(lambda q: jnp.exp(gtot[q]))

    s = list(s0)
    y = [[None] * nj for _ in range(npair)]
    for j in range(nj):
        for p in range(npair):
            y[p][j] = _unstack_heads(_nt(r1[(j, p)], s[p]) + y0[(j, p)])
        s = [s[p] * gdec[(j, p)] + _nn(s[p], mm[(j, p)]) + zz[(j, p)] for p in range(npair)]
    return y, s


def _rwkv_blocks(ref, nj, c):
    return [[ref[j * c:(j + 1) * c, p * PAIR_W:(p + 1) * PAIR_W] for j in range(nj)] for p in range(HB_PAIRS)]


def _rwkv_fwd(seqs, comm=None):
    t = seqs[0].shape[0]
    c, nj = RWKV_CHUNK, RWKV_GROUP
    n = t // (c * nj)

    def body(r_ref, lw_ref, k_ref, v_ref, a_ref, b_ref, y_ref, hs_ref, st_ref):
        @pl.when(pl.program_id(0) == 0)
        def _():
            st_ref[...] = jnp.zeros_like(st_ref)

        hs_ref[0] = st_ref[...]
        s0 = [st_ref[p] for p in range(HB_PAIRS)]
        y, s1 = _rwkv_step(s0, *[_rwkv_blocks(ref, nj, c) for ref in (r_ref, lw_ref, k_ref, v_ref, a_ref, b_ref)])
        for p in range(HB_PAIRS):
            for j in range(nj):
                y_ref[j * c:(j + 1) * c, p * PAIR_W:(p + 1) * PAIR_W] = y[p][j]
            st_ref[p] = s1[p]

    seq = pl.BlockSpec((c * nj, W_B), lambda i: (i, 0))
    return _hosting_call(
        body, comm, name="rwkv_fwd", grid=(n,), in_specs=[seq] * 6,
        out_specs=[seq, pl.BlockSpec((1, HB_PAIRS, PAIR_W, PAIR_W), lambda i: (i, 0, 0, 0))],
        out_shape=[SDS((t, W_B), F32), SDS((n, HB_PAIRS, PAIR_W, PAIR_W), F32)],
        scratch_shapes=[pltpu.VMEM((HB_PAIRS, PAIR_W, PAIR_W), F32)], args=tuple(seqs))


def _rwkv_bwd(seqs, hs, dy, comm=None):
    t = seqs[0].shape[0]
    c, nj = RWKV_CHUNK, RWKV_GROUP
    n = t // (c * nj)

    def body(r_ref, lw_ref, k_ref, v_ref, a_ref, b_ref, hs_ref, dy_ref,
             dr_ref, dlw_ref, dk_ref, dv_ref, da_ref, db_ref, dst_ref):
        @pl.when(pl.program_id(0) == 0)
        def _():
            dst_ref[...] = jnp.zeros_like(dst_ref)

        s0 = [hs_ref[0, p] for p in range(HB_PAIRS)]
        seq_vals = [_rwkv_blocks(ref, nj, c) for ref in (r_ref, lw_ref, k_ref, v_ref, a_ref, b_ref)]
        _, vjp = jax.vjp(_rwkv_step, s0, *seq_vals)
        grads = vjp((_rwkv_blocks(dy_ref, nj, c), [dst_ref[p] for p in range(HB_PAIRS)]))
        for ref, gr in zip((dr_ref, dlw_ref, dk_ref, dv_ref, da_ref, db_ref), grads[1:]):
            for p in range(HB_PAIRS):
                for j in range(nj):
                    ref[j * c:(j + 1) * c, p * PAIR_W:(p + 1) * PAIR_W] = gr[p][j]
        m0, m1 = _head_lane_masks()
        rows0 = (lax.broadcasted_iota(jnp.int32, (PAIR_W, 1), 0) < HB_DIM).astype(F32)
        blocks = rows0 * m0 + (1.0 - rows0) * m1
        for p in range(HB_PAIRS):
            dst_ref[p] = grads[0][p] * blocks

    seq = pl.BlockSpec((c * nj, W_B), lambda i: (n - 1 - i, 0))
    return _hosting_call(
        body, comm, name="rwkv_bwd", grid=(n,),
        in_specs=[seq] * 6 + [pl.BlockSpec((1, HB_PAIRS, PAIR_W, PAIR_W), lambda i: (n - 1 - i, 0, 0, 0)), seq],
        out_specs=[seq] * 6, out_shape=[SDS((t, W_B), F32)] * 6,
        scratch_shapes=[pltpu.VMEM((HB_PAIRS, PAIR_W, PAIR_W), F32)], args=(*seqs, hs, dy))


def _final_loss(x3, fnorm, target, *, tm):
    t, d = x3.shape

    def body(x_ref, g_ref, t_ref, dx_ref, dg_ref, loss_ref):
        @pl.when(pl.program_id(0) == 0)
        def _():
            dg_ref[...] = jnp.zeros_like(dg_ref)
            loss_ref[...] = jnp.zeros_like(loss_ref)

        x, g = x_ref[...], g_ref[...]
        rinv = lax.rsqrt(jnp.mean(x * x, axis=-1, keepdims=True) + NORM_EPS)
        xh = x * rinv
        diff = xh * g - t_ref[...]
        loss_ref[...] += 0.5 * jnp.sum(jnp.mean(diff * diff, axis=-1, keepdims=True))
        dy = diff * (1.0 / d)
        dg_ref[...] += jnp.sum(dy * xh, axis=0, keepdims=True)
        dxh = dy * g
        dx_ref[...] = rinv * (dxh - xh * jnp.mean(dxh * xh, axis=-1, keepdims=True))

    row = pl.BlockSpec((tm, d), lambda i: (i, 0))
    return pl.pallas_call(
        body, name="final_loss", grid=(t // tm,), in_specs=[row, pl.BlockSpec((1, d), lambda i: (0, 0)), row],
        out_specs=[row, pl.BlockSpec((1, d), lambda i: (0, 0)), pl.BlockSpec((8, 128), lambda i: (0, 0))],
        out_shape=[SDS((t, d), F32), SDS((1, d), F32), SDS((8, 128), F32)],
        compiler_params=_params(("arbitrary",)))(x3, fnorm, target)


def _gate_up_act(h, wgt, wut, *, tm, tn, name, comm=None):
    t, d = h.shape
    tm = min(tm, t)

    def body(h_ref, g_ref, u_ref, a_out, u_out, act_out):
        hv = h_ref[...]
        a = _dg(hv, g_ref[...], 1, 1, False)
        u = _dg(hv, u_ref[...], 1, 1, False)
        a_out[...] = a
        u_out[...] = u
        act_out[...] = (_silu(a) * u).astype(act_out.dtype)

    wspec = pl.BlockSpec((tn, d), lambda i, j: (j, 0))
    ospec = pl.BlockSpec((tm, tn), lambda i, j: (i, j))
    return _hosting_call(
        body, comm, name=name, grid=(t // tm, D_FF // tn),
        in_specs=[pl.BlockSpec((tm, d), lambda i, j: (i, 0)), wspec, wspec], out_specs=[ospec, ospec, ospec],
        out_shape=[SDS((t, D_FF), F32), SDS((t, D_FF), F32), SDS((t, D_FF), BF16)], scratch_shapes=[],
        args=(h, wgt, wut))


def _dact_swiglu(dout, wd, a, u, *, tm, tn, name):
    t, d = dout.shape
    tm = min(tm, t)

    def body(d_ref, w_ref, a_ref, u_ref, da_out, du_out):
        dact = 0.5 * _dg(d_ref[...], w_ref[...], 1, 1, False)
        av, uv = a_ref[...], u_ref[...]
        s = _sigmoid(av)
        da_out[...] = (dact * uv * (s * (1.0 + av * (1.0 - s)))).astype(da_out.dtype)
        du_out[...] = (dact * (av * s)).astype(du_out.dtype)

    tile = pl.BlockSpec((tm, tn), lambda i, j: (i, j))
    return pl.pallas_call(
        body, name=name, grid=(t // tm, D_FF // tn),
        in_specs=[pl.BlockSpec((tm, d), lambda i, j: (i, 0)), pl.BlockSpec((tn, d), lambda i, j: (j, 0)), tile, tile],
        out_specs=[tile, tile], out_shape=[SDS((t, D_FF), BF16), SDS((t, D_FF), BF16)],
        compiler_params=_params(("parallel", "parallel")))(dout, wd, a, u)


class _Plan:
    def __init__(self):
        self.comm_of, self.after = {}, {}

    def comm(self, name, g):
        return self.comm_of[name](g) if name in self.comm_of else None

    def done(self, name, results, w):
        if name in self.after:
            self.after[name](results, w)


def _ffn_fwd(x, norm, wgt, wut, wd, tag, comm=None):
    h, = _rowwise(_rms_f, [x], [norm], [[0]], [BF16], tm=512, name=f"{tag}_rms")
    (a, u, act), carried = _gate_up_act(h, wgt, wut, tm=2048, tn=256, name=f"{tag}_gate_up", comm=comm)
    out = _mm(act, wd, tm=1024, tn=512, tk=D_FF, name=f"{tag}_down", res=x, scale=0.5)
    return out, (h, a, u, act), carried


def _ffn_bwd(dout, x, norm, wgt, wut, wd, saved, tag):
    h, a, u, act = saved
    da, du = _dact_swiglu(dout, wd, a, u, tm=2048, tn=256, name=f"{tag}_dact")
    dwd = _mm(act, dout, ta=True, tm=D_FF // 2, tn=D_MODEL, tk=1024, name=f"{tag}_dwd", scale=0.5)
    dwgt = _mm(da, h, ta=True, tm=D_FF // 2, tn=D_MODEL, tk=1024, name=f"{tag}_dwg")
    dwut = _mm(du, h, ta=True, tm=D_FF // 2, tn=D_MODEL, tk=1024, name=f"{tag}_dwu")
    dh = _mm(da, wgt, tm=1024, tn=512, tk=D_FF, name=f"{tag}_dh_g")
    dh = _mm(du, wut, tm=1024, tn=512, tk=D_FF, name=f"{tag}_dh_u", res=dh)
    dx, dnorm = _rowwise_bwd(_rms_f, [x], [norm], [dh], x_grad=[True], p_grad=[True], dx_groups=[[0]],
                             dx_dtypes=[F32], tm=256, name=f"{tag}_drms", extra={0: dout})
    return dx, dnorm, dwgt, dwut, dwd


def _local_step(x, target, w, plan=None):
    plan = plan or _Plan()
    ones_bd = jnp.kron(jnp.eye(HB_HEADS, dtype=F32), jnp.ones((HB_DIM, HB_DIM), F32))
    g = {}
    x1, ffn1_saved, carried = _ffn_fwd(x, w["ffn1_norm"], w["ffn1_wgt"], w["ffn1_wut"], w["ffn1_wd"], "ffn1",
                                       comm=plan.comm("ffn1_gate_up", g))
    plan.done("ffn1_gate_up", carried, w)
    hm, = _rowwise(_rms_f, [x1], [w["mix_norm"]], [[0]], [BF16], tm=512, name="mix_rms")
    p_h = _mm(hm, w["w_in_h"], tm=2048, tn=256, tk=D_MODEL, name="inproj_h")
    p_r = _mm(hm, w["w_in_r"], tm=2048, tn=256, tk=D_MODEL, name="inproj_r")
    o_a, hgrn_states = _hgrn_fwd(p_h, w["lb0"], w["lb1"], w["hgrn_out_norm"])

    mu = w["mu_pad"]
    prep_xs = [(p_r, W_B, 0), (p_r, W_B, 1), (p_r, W_B, 2), (p_r, LORA_PAD, 6),
               ("prev", p_r, W_B, 0), ("prev", p_r, W_B, 1), ("prev", p_r, W_B, 2), ("prev", p_r, LORA_PAD, 6)]
    prep_ps = [(mu, W_B, 0), (mu, W_B, 1), (mu, W_B, 2), (mu, LORA_PAD, 6), w["rwkv_w0"], w["w2_pad"], w["rwkv_a0"],
               w["a2_pad"], w["g2_pad"], w["rwkv_k_k"], w["rwkv_k_a"], ones_bd]
    prep_f = _rwkv_prep_f
    r, lw, k2, v, a_vec, b_vec, gate = _rowwise(prep_f, prep_xs, prep_ps, [[0], [1], [2], [3], [4], [5], [6]],
                                                [F32] * 7, tm=256, name="rwkv_prep")
    seqs = [r, lw, k2, v, a_vec, b_vec]
    (y, rwkv_states), carried = _rwkv_fwd(seqs, comm=plan.comm("rwkv_fwd", g))
    plan.done("rwkv_fwd", carried, w)
    post_f = _rwkv_post_f
    post_xs = [y, r, k2, v, gate]
    post_ps = [w["rwkv_r_k"], w["rwkv_gn_w"], w["rwkv_gn_b"], ones_bd]
    o_b, = _rowwise(post_f, post_xs, post_ps, [[0]], [F32], tm=256, name="rwkv_post")
    x2 = _mm(o_a, w["w_out_a"], tm=2048, tn=256, tk=W_A, name="outproj_a", res=x1)
    x2 = _mm(o_b, w["w_out_b"], tm=2048, tn=256, tk=W_B, name="outproj_b", res=x2)
    x3, ffn2_saved, _ = _ffn_fwd(x2, w["ffn2_norm"], w["ffn2_wgt"], w["ffn2_wut"], w["ffn2_wd"], "ffn2")
    dx3, g["final_norm"], loss = _final_loss(x3, w["final_norm"], target, tm=256)

    dx2, g["ffn2_norm"], g["ffn2_wgt"], g["ffn2_wut"], g["ffn2_wd"] = _ffn_bwd(
        dx3, x2, w["ffn2_norm"], w["ffn2_wgt"], w["ffn2_wut"], w["ffn2_wd"], ffn2_saved, "ffn2")
    do_a = _mm(dx2, w["w_out_a"], tb=True, tm=2048, tn=256, tk=D_MODEL, name="outproj_do_a")
    do_b = _mm(dx2, w["w_out_b"], tb=True, tm=2048, tn=256, tk=D_MODEL, name="outproj_do_b")
    g["w_out_a"] = _mm(o_a, dx2, ta=True, tm=W_A, tn=D_MODEL, tk=1024, name="outproj_dw_a")
    g["w_out_b"] = _mm(o_b, dx2, ta=True, tm=W_B, tn=D_MODEL, tk=1024, name="outproj_dw_b")

    (dp_h, g["lb0"], g["lb1"], g["hgrn_out_norm"]), carried = _hgrn_bwd(
        p_h, w["lb0"], w["lb1"], w["hgrn_out_norm"], hgrn_states, do_a, 0, comm=plan.comm("hgrn_bwd", g))
    plan.done("hgrn_bwd", carried, w)
    post_out = _rowwise_bwd(post_f, post_xs, post_ps, [do_b], x_grad=[True] * 5, p_grad=[True] * 3 + [False],
                            dx_groups=[[0], [1], [2], [3], [4]], dx_dtypes=[F32] * 5, tm=256, name="rwkv_post_bwd")
    dy, dr1, dk1, dv1, dgate, g["rwkv_r_k"], g["rwkv_gn_w"], g["rwkv_gn_b"] = post_out
    (dr2, dlw, dk2, dv2, da_vec, db_vec), carried = _rwkv_bwd(seqs, rwkv_states, dy, comm=plan.comm("rwkv_bwd", g))
    plan.done("rwkv_bwd", carried, w)

    def prep2_f(*vals):
        r_, lw_, k2_, v_, a_, b_, g_ = prep_f(*vals)
        return r_, lw_, k2_, v_, a_, b_, g_, r_, k2_, v_

    prep_out = _rowwise_bwd(prep2_f, prep_xs, prep_ps, [dr2, dlw, dk2, dv2, da_vec, db_vec, dgate, dr1, dk1, dv1],
                            x_grad=[True] * 8, p_grad=[True] * 11 + [False], dx_groups=[[0, 1, 2, 3], [4, 5, 6, 7]],
                            dx_dtypes=[F32, F32], tm=256, name="rwkv_prep_bwd")
    dpr_main, dpr_prev = prep_out[0], prep_out[1]
    (dmu_r, dmu_k, dmu_v, dmu_lo, g["rwkv_w0"], g["w2_pad"], g["rwkv_a0"], g["a2_pad"], g["g2_pad"],
     g["rwkv_k_k"], g["rwkv_k_a"]) = prep_out[2:]
    g["mu_pad"] = jnp.concatenate([dmu_r, dmu_k, dmu_v, dmu_lo], axis=1)
    dp_r, = _rowwise(lambda u_, s_: (u_ + s_,), [dpr_main, ("next", dpr_prev, N_RWKV_PAD, 0)], [], [[0]], [F32],
                     tm=512, name="rwkv_dp_sum")
    dhm = _mm(dp_h, w["w_in_h"], tb=True, tm=1024, tn=512, tk=N_HGRN_COLS, name="inproj_dh_h")
    dhm = _mm(dp_r, w["w_in_r"], tb=True, tm=1024, tn=512, tk=N_RWKV_PAD, name="inproj_dh_r", res=dhm)
    g["w_in_h"] = _mm(hm, dp_h, ta=True, tm=D_MODEL, tn=D_MODEL, tk=1024, name="inproj_dw_h")
    g["w_in_r"] = _mm(hm, dp_r, ta=True, tm=D_MODEL, tn=N_RWKV_PAD // 2, tk=1024, name="inproj_dw_r")
    dx1, g["mix_norm"] = _rowwise_bwd(_rms_f, [x1], [w["mix_norm"]], [dhm], x_grad=[True], p_grad=[True],
                                      dx_groups=[[0]], dx_dtypes=[F32], tm=256, name="mix_drms", extra={0: dx2})
    dx0, g["ffn1_norm"], g["ffn1_wgt"], g["ffn1_wut"], g["ffn1_wd"] = _ffn_bwd(
        dx1, x, w["ffn1_norm"], w["ffn1_wgt"], w["ffn1_wut"], w["ffn1_wd"], ffn1_saved, "ffn1")
    return loss, dx0, g


HBM_SPEC = pl.BlockSpec(memory_space=pl.ANY)

Comm = collections.namedtuple("Comm", "arrays out_shapes aliased sem_shapes start finish")


def _run_comm(comm, name):
    n = len(comm.arrays)

    def body(*refs):
        ins, outs, sems = refs[:n], refs[n:2 * n], refs[2 * n:]
        comm.start(ins, outs, sems)
        comm.finish(ins, outs, sems)

    return pl.pallas_call(
        body, name=name, in_specs=[HBM_SPEC] * n, out_specs=[HBM_SPEC] * n, out_shape=list(comm.out_shapes),
        input_output_aliases={t: t for t in range(n)} if comm.aliased else {},
        scratch_shapes=list(comm.sem_shapes))(*comm.arrays)


def _hosting_call(body, comm, *, name, grid, in_specs, out_specs, out_shape, scratch_shapes, args):
    sem = ("arbitrary",) * len(grid)
    if comm is None:
        res = pl.pallas_call(body, name=name, grid=grid, in_specs=in_specs, out_specs=out_specs, out_shape=out_shape,
                             scratch_shapes=scratch_shapes, compiler_params=_params(sem))(*args)
        return list(res), []
    ni, no, ns, nc = len(in_specs), len(out_specs), len(scratch_shapes), len(comm.arrays)

    def wrapped(*refs):
        ins, cins = refs[:ni], refs[ni:ni + nc]
        outs, couts = refs[ni + nc:ni + nc + no], refs[ni + nc + no:ni + 2 * nc + no]
        scr, sems = refs[ni + 2 * nc + no:ni + 2 * nc + no + ns], refs[ni + 2 * nc + no + ns:]
        first = functools.reduce(jnp.logical_and, [pl.program_id(k) == 0 for k in range(len(grid))])
        last = functools.reduce(jnp.logical_and, [pl.program_id(k) == grid[k] - 1 for k in range(len(grid))])

        @pl.when(first)
        def _():
            comm.start(cins, couts, sems)

        body(*ins, *outs, *scr)

        @pl.when(last)
        def _():
            comm.finish(cins, couts, sems)

    res = pl.pallas_call(
        wrapped, name=name, grid=grid, in_specs=list(in_specs) + [HBM_SPEC] * nc,
        out_specs=list(out_specs) + [HBM_SPEC] * nc, out_shape=list(out_shape) + list(comm.out_shapes),
        scratch_shapes=list(scratch_shapes) + list(comm.sem_shapes),
        input_output_aliases={ni + t: no + t for t in range(nc)} if comm.aliased else {},
        compiler_params=_params(sem))(*args, *comm.arrays)
    return list(res[:no]), list(res[no:])


def _chips(x, y):
    return [(1 - x, y), (x, 1 - y), (1 - x, 1 - y)]


def _gather_comm(bufs):
    n = len(bufs)

    def copies(outs, sems):
        ici_send, ici_recv, d2d_send, d2d_recv = sems
        x, y, c = lax.axis_index("x"), lax.axis_index("y"), lax.axis_index("c")

        def half(t, slot, hc):
            hr = bufs[t].shape[1] // 2
            return outs[t].at[slot, pl.ds(pl.multiple_of(hc * hr, 16), hr), :]

        def ici(t, j, slot, px, py):
            return pltpu.make_async_remote_copy(src_ref=half(t, slot, c), dst_ref=half(t, slot, c),
                                                send_sem=ici_send.at[3 * t + j], recv_sem=ici_recv.at[3 * t + j],
                                                device_id=(px, py, c), device_id_type=MESH)

        def d2d(t, j, slot, hc):
            return pltpu.make_async_remote_copy(src_ref=half(t, slot, hc), dst_ref=half(t, slot, hc),
                                                send_sem=d2d_send.at[3 * t + j], recv_sem=d2d_recv.at[3 * t + j],
                                                device_id=(x, y, 1 - c), device_id_type=MESH)

        peers = [(t, j, px, py) for t in range(n) for j, (px, py) in enumerate(_chips(x, y))]
        return ici, d2d, peers, 2 * x + y, c

    def start(ins, outs, sems):
        ici, _, peers, me, _ = copies(outs, sems)
        for t, j, px, py in peers:
            ici(t, j, me, px, py).start()

    def finish(ins, outs, sems):
        ici, d2d, peers, me, c = copies(outs, sems)
        for t, j, px, py in peers:
            ici(t, j, 2 * px + py, px, py).wait_recv()
            d2d(t, j, 2 * px + py, c).start()
        for t, j, px, py in peers:
            d2d(t, j, 2 * px + py, 1 - c).wait_recv()
        for t, j, px, py in peers:
            ici(t, j, me, px, py).wait_send()
            d2d(t, j, 2 * px + py, c).wait_send()

    return Comm(list(bufs), [SDS(b.shape, b.dtype) for b in bufs], True, [pltpu.SemaphoreType.DMA((3 * n,))] * 4,
                start, finish)


def _sibling_exchange_comm(gs):
    n = len(gs)

    def copies(ins, outs, sems):
        x, y, c = lax.axis_index("x"), lax.axis_index("y"), lax.axis_index("c")
        cps = []
        for t in range(n):
            hr = gs[t].shape[1] // 2
            src = ins[t].at[:, pl.ds(pl.multiple_of((1 - c) * hr, SUBLANES), hr), :]
            cps.append(pltpu.make_async_remote_copy(src_ref=src, dst_ref=outs[t], send_sem=sems[0].at[t],
                                                    recv_sem=sems[1].at[t], device_id=(x, y, 1 - c),
                                                    device_id_type=MESH))
        return cps

    def start(ins, outs, sems):
        for cp in copies(ins, outs, sems):
            cp.start()

    def finish(ins, outs, sems):
        for cp in copies(ins, outs, sems):
            cp.wait()

    return Comm(list(gs), [SDS((N_CHIPS, g.shape[1] // 2, g.shape[2]), g.dtype) for g in gs], False,
                [pltpu.SemaphoreType.DMA((n,))] * 2, start, finish)


def _chip_exchange_comm(ss):
    n = len(ss)

    def copies(ins, outs, sems):
        x, y, c = lax.axis_index("x"), lax.axis_index("y"), lax.axis_index("c")
        me = 2 * x + y

        def copy(t, j, px, py, src_slot, dst_slot):
            return pltpu.make_async_remote_copy(src_ref=ins[t].at[src_slot], dst_ref=outs[t].at[dst_slot],
                                                send_sem=sems[0].at[3 * t + j], recv_sem=sems[1].at[3 * t + j],
                                                device_id=(px, py, c), device_id_type=MESH)

        peers = [(t, j, px, py) for t in range(n) for j, (px, py) in enumerate(_chips(x, y))]
        return copy, peers, me

    def start(ins, outs, sems):
        copy, peers, me = copies(ins, outs, sems)
        for t, j, px, py in peers:
            copy(t, j, px, py, 2 * px + py, me).start()

    def finish(ins, outs, sems):
        copy, peers, me = copies(ins, outs, sems)
        for t, j, px, py in peers:
            copy(t, j, px, py, me, 2 * px + py).wait_recv()
        for t, j, px, py in peers:
            copy(t, j, px, py, 2 * px + py, me).wait_send()

    return Comm(list(ss), [SDS(s.shape, s.dtype) for s in ss], False, [pltpu.SemaphoreType.DMA((3 * n,))] * 2,
                start, finish)


def _sibling_swap_comm(fs):
    n = len(fs)

    def copies(ins, outs, sems):
        x, y, c = lax.axis_index("x"), lax.axis_index("y"), lax.axis_index("c")
        return [pltpu.make_async_remote_copy(src_ref=ins[t], dst_ref=outs[t], send_sem=sems[0].at[t],
                                             recv_sem=sems[1].at[t], device_id=(x, y, 1 - c), device_id_type=MESH)
                for t in range(n)]

    def start(ins, outs, sems):
        for cp in copies(ins, outs, sems):
            cp.start()

    def finish(ins, outs, sems):
        for cp in copies(ins, outs, sems):
            cp.wait()

    return Comm(list(fs), [SDS(f.shape, f.dtype) for f in fs], False, [pltpu.SemaphoreType.DMA((n,))] * 2,
                start, finish)


def _row_tile(rows, cap=512):
    best = SUBLANES
    for tr in range(SUBLANES, min(rows, cap) + 1, SUBLANES):
        if rows % tr == 0:
            best = tr
    return best


def _add_halves(g4, r4, c_idx, name):
    _, hr, lanes = r4.shape
    tr = _row_tile(hr)
    nb = hr // tr

    def body(c_ref, a_ref, b_ref, o_ref):
        o_ref[...] = (a_ref[...] + b_ref[...]).astype(o_ref.dtype)

    grid_spec = pltpu.PrefetchScalarGridSpec(
        num_scalar_prefetch=1, grid=(N_CHIPS, nb),
        in_specs=[pl.BlockSpec((None, tr, lanes), lambda q, i, c_ref: (q, c_ref[0] * nb + i, 0)),
                  pl.BlockSpec((None, tr, lanes), lambda q, i, c_ref: (q, i, 0))],
        out_specs=pl.BlockSpec((None, tr, lanes), lambda q, i, c_ref: (q, i, 0)))
    return pl.pallas_call(body, name=name, grid_spec=grid_spec, out_shape=SDS(r4.shape, BF16),
                          compiler_params=_params(("parallel", "parallel")))(c_idx, g4, r4)


def _sum_chips(r4, s4, me_idx, name):
    _, rows, lanes = r4.shape
    tr = _row_tile(rows)

    def body(me_ref, a_ref, b_ref, c_ref, d_ref, own_ref, o_ref):
        own = own_ref[...].astype(F32)
        p = [jnp.where(me_ref[0] == q, own, ref[...].astype(F32)) for q, ref in enumerate((a_ref, b_ref, c_ref, d_ref))]
        o_ref[...] = ((p[0] + p[1]) + p[2]) + p[3]

    other = lambda q: (lambda i, me_ref: (jnp.where(me_ref[0] == q, (q + 1) % N_CHIPS, q), i, 0))
    grid_spec = pltpu.PrefetchScalarGridSpec(
        num_scalar_prefetch=1, grid=(rows // tr,),
        in_specs=[pl.BlockSpec((None, tr, lanes), other(q)) for q in range(N_CHIPS)]
        + [pl.BlockSpec((None, tr, lanes), lambda i, me_ref: (me_ref[0], i, 0))],
        out_specs=pl.BlockSpec((tr, lanes), lambda i, me_ref: (i, 0)))
    return pl.pallas_call(body, name=name, grid_spec=grid_spec, out_shape=SDS((rows, lanes), F32),
                          compiler_params=_params(("parallel",)))(me_idx, r4, r4, r4, r4, s4)


def _adamw(wf, g_own, g_other, mf, vf, c_idx, name):
    rows, lanes = wf.shape
    hr = rows // 2
    tr = _row_tile(hr)
    nb = hr // tr
    c1 = 1.0 / (1.0 - ADAM_B1 ** ADAM_STEP)
    c2 = 1.0 / (1.0 - ADAM_B2 ** ADAM_STEP)

    def body(c_ref, w_ref, go_ref, gx_ref, m_ref, v_ref, g_ref, d_ref, nm_ref, nv_ref):
        gv = jnp.where(pl.program_id(0) == c_ref[0], go_ref[...], gx_ref[...])
        m = ADAM_B1 * m_ref[...] + (1.0 - ADAM_B1) * gv
        v = ADAM_B2 * v_ref[...] + (1.0 - ADAM_B2) * (gv * gv)
        g_ref[...] = gv
        d_ref[...] = -ADAM_LR * ((m * c1) / (jnp.sqrt(v * c2) + ADAM_EPS) + ADAM_WD * w_ref[...])
        nm_ref[...] = m
        nv_ref[...] = v

    full = pl.BlockSpec((tr, lanes), lambda h, i, c_ref: (h * nb + i, 0))
    half = pl.BlockSpec((tr, lanes), lambda h, i, c_ref: (i, 0))
    grid_spec = pltpu.PrefetchScalarGridSpec(num_scalar_prefetch=1, grid=(2, nb),
                                             in_specs=[full, half, half, full, full], out_specs=[full] * 4)
    return pl.pallas_call(body, name=name, grid_spec=grid_spec, out_shape=[SDS((rows, lanes), F32)] * 4,
                          compiler_params=_params(("parallel", "parallel")))(c_idx, wf, g_own, g_other, mf, vf)


BIG = ("ffn1_w_gate", "ffn1_w_up", "ffn1_w_down", "ffn2_w_gate", "ffn2_w_up", "ffn2_w_down", "w_out", "w_in")
TRANSPOSED = ("ffn1_w_gate", "ffn1_w_up", "ffn2_w_gate", "ffn2_w_up")
PACKED = ("rwkv_w2", "rwkv_a2", "rwkv_g2")
SMALL_SHAPES = {"ffn1_norm": (1, D_MODEL), "mix_norm": (1, D_MODEL), "hgrn_lb_logits": (2, W_A),
                "hgrn_out_norm": (1, W_A), "rwkv_shift_mu": (1, N_RWKV_COLS), "rwkv_w0": (1, W_B),
                "rwkv_a0": (1, W_B), "rwkv_k_k": (1, W_B), "rwkv_k_a": (1, W_B),
                "rwkv_r_k": (1, HB_HEADS, HB_DIM), "rwkv_gn_w": (1, W_B), "rwkv_gn_b": (1, W_B),
                "ffn2_norm": (1, D_MODEL), "final_norm": (D_MODEL,)}
PACK_ELEMS = sum(_numel(_shard_shape(n)) for n in PACKED) + sum(_numel(SMALL_SHAPES[n]) for n in SMALL)
PACK_ROWS = -(-PACK_ELEMS // (32 * LANES)) * 32


def _to_rows(name, shard):
    return shard[0].T if name in TRANSPOSED else shard[0]


def _from_rows(name, rows):
    return (rows.T if name in TRANSPOSED else rows)[None]


def _pack(sharded, small):
    flat = jnp.concatenate([sharded[n].reshape(-1) for n in PACKED] + [small[n].reshape(-1) for n in SMALL])
    return jnp.pad(flat, (0, PACK_ROWS * LANES - flat.shape[0])).reshape(PACK_ROWS, LANES)


def _unpack(packed):
    flat, out, off = packed.reshape(-1), {}, 0
    for n in PACKED:
        shp = _shard_shape(n)
        out[n] = flat[off:off + _numel(shp)].reshape((1,) + shp)
        off += _numel(shp)
    for n in SMALL:
        shp = SMALL_SHAPES[n]
        out[n] = flat[off:off + _numel(shp)].reshape(shp)
        off += _numel(shp)
    return out


def _quarter(full, name, q):
    shape, ax = SHARDED_SHAPES[name]
    w = shape[ax] // N_CHIPS
    return lax.slice_in_dim(full, q * w, (q + 1) * w, axis=ax)


def kernel(x, ffn1_norm, ffn1_w_gate, ffn1_w_up, ffn1_w_down, mix_norm, w_in, hgrn_lb_logits, hgrn_out_norm, rwkv_shift_mu, rwkv_w0, rwkv_w2, rwkv_a0, rwkv_a2, rwkv_g2, rwkv_k_k, rwkv_k_a, rwkv_r_k, rwkv_gn_w, rwkv_gn_b, w_out, ffn2_norm, ffn2_w_gate, ffn2_w_up, ffn2_w_down, final_norm, loss_target, m_ffn1_norm, m_ffn1_w_gate, m_ffn1_w_up, m_ffn1_w_down, m_mix_norm, m_w_in, m_hgrn_lb_logits, m_hgrn_out_norm, m_rwkv_shift_mu, m_rwkv_w0, m_rwkv_w2, m_rwkv_a0, m_rwkv_a2, m_rwkv_g2, m_rwkv_k_k, m_rwkv_k_a, m_rwkv_r_k, m_rwkv_gn_w, m_rwkv_gn_b, m_w_out, m_ffn2_norm, m_ffn2_w_gate, m_ffn2_w_up, m_ffn2_w_down, m_final_norm, v_ffn1_norm, v_ffn1_w_gate, v_ffn1_w_up, v_ffn1_w_down, v_mix_norm, v_w_in, v_hgrn_lb_logits, v_hgrn_out_norm, v_rwkv_shift_mu, v_rwkv_w0, v_rwkv_w2, v_rwkv_a0, v_rwkv_a2, v_rwkv_g2, v_rwkv_k_k, v_rwkv_k_a, v_rwkv_r_k, v_rwkv_gn_w, v_rwkv_gn_b, v_w_out, v_ffn2_norm, v_ffn2_w_gate, v_ffn2_w_up, v_ffn2_w_down, v_final_norm):
    args = dict(locals())
    wts = {n: args[n] for n in ALL_WEIGHTS}
    moms = {n: args["m_" + n] for n in ALL_WEIGHTS}
    vars_ = {n: args["v_" + n] for n in ALL_WEIGHTS}

    me = 2 * lax.axis_index("x") + lax.axis_index("y")
    c_idx = lax.axis_index("c").astype(jnp.int32).reshape(1)
    me_idx = me.astype(jnp.int32).reshape(1)
    shard_of = {n: _to_rows(n, wts[n]).astype(BF16) for n in BIG}
    shard_of["packed"] = _pack(wts, {n: wts[n] for n in SMALL}).astype(BF16)
    group = {"ffn1": BIG[0:3], "ffn2": BIG[3:6], "mix": ("w_out", "w_in", "packed")}

    def slot_bufs(names):
        return [lax.dynamic_update_slice(jnp.zeros((N_CHIPS,) + shard_of[n].shape, BF16), shard_of[n][None],
                                         (me, 0, 0)) for n in names]

    def ffn_weights(tag, gathered):
        return {f"{tag}_wgt": gathered[0].reshape(D_FF, D_MODEL), f"{tag}_wut": gathered[1].reshape(D_FF, D_MODEL),
                f"{tag}_wd": gathered[2].reshape(D_FF, D_MODEL)}

    def mixer_weights(gathered):
        w_out_full = gathered[0].reshape(D_MODEL, D_MODEL)
        w_in_full = jnp.concatenate([gathered[1][q] for q in range(N_CHIPS)], axis=1)
        packs = gathered[2].reshape(N_CHIPS, PACK_ROWS * LANES)
        full, off = {}, 0
        for n in PACKED:
            shp = _shard_shape(n)
            full[n] = jnp.concatenate([packs[q, off:off + _numel(shp)].reshape(shp) for q in range(N_CHIPS)], axis=1)
            off += _numel(shp)
        zrow = lambda nrow: jnp.zeros((nrow, W_B), BF16)
        return {"w_out_a": w_out_full[:W_A], "w_out_b": w_out_full[W_A:], "w_in_h": w_in_full[:, :N_HGRN_COLS],
                "w_in_r": jnp.pad(w_in_full[:, N_HGRN_COLS:], ((0, 0), (0, N_RWKV_PAD - N_RWKV_COLS))),
                "w2_pad": jnp.concatenate([full["rwkv_w2"], zrow(LORA_PAD - 32)], axis=0),
                "a2_pad": jnp.concatenate([zrow(32), full["rwkv_a2"], zrow(LORA_PAD - 64)], axis=0),
                "g2_pad": jnp.concatenate([zrow(64), full["rwkv_g2"], zrow(LORA_PAD - 160)], axis=0)}

    plan = _Plan()
    w = ffn_weights("ffn1", _run_comm(_gather_comm(slot_bufs(group["ffn1"])), "gather_ffn1"))
    plan.comm_of["ffn1_gate_up"] = lambda g: _gather_comm(slot_bufs(group["mix"]))
    plan.after["ffn1_gate_up"] = lambda res, w_: w_.update(mixer_weights(res))
    plan.comm_of["rwkv_fwd"] = lambda g: _gather_comm(slot_bufs(group["ffn2"]))
    plan.after["rwkv_fwd"] = lambda res, w_: w_.update(ffn_weights("ffn2", res))
    w["ffn1_norm"], w["ffn2_norm"] = ffn1_norm, ffn2_norm
    w["mix_norm"] = mix_norm
    w["lb0"], w["lb1"] = hgrn_lb_logits[0:1], hgrn_lb_logits[1:2]
    w["hgrn_out_norm"] = hgrn_out_norm
    w["mu_pad"] = jnp.pad(rwkv_shift_mu, ((0, 0), (0, N_RWKV_PAD - N_RWKV_COLS)))
    for n in ("rwkv_w0", "rwkv_a0", "rwkv_k_k", "rwkv_k_a", "rwkv_gn_w", "rwkv_gn_b"):
        w[n] = wts[n]
    w["rwkv_r_k"] = rwkv_r_k.reshape(1, W_B)
    w["final_norm"] = final_norm.reshape(1, D_MODEL)

    def reduce_rows(names, gs):
        r1 = _run_comm(_sibling_exchange_comm(gs), "grad_sibling_exchange")
        s4 = [_add_halves(gt, rt, c_idx, f"grad_add_halves_{n}") for gt, rt, n in zip(gs, r1, names)]
        r2 = _run_comm(_chip_exchange_comm(s4), "grad_chip_exchange")
        return [_sum_chips(rt, st, me_idx, f"grad_sum_chips_{n}") for rt, st, n in zip(r2, s4, names)]

    early = {}
    ffn2_grads = lambda g: [g[k].reshape(N_CHIPS, -1, D_MODEL) for k in ("ffn2_wgt", "ffn2_wut", "ffn2_wd")]

    def after_sibling(res, w_):
        early["s4"] = [_add_halves(gt, rt, c_idx, f"grad_add_halves_{n}")
                       for gt, rt, n in zip(early["gs"], res, group["ffn2"])]

    def after_chips(res, w_):
        early["own"] = [_sum_chips(rt, st, me_idx, f"grad_sum_chips_{n}")
                        for rt, st, n in zip(res, early["s4"], group["ffn2"])]

    def sibling_comm(g):
        early["gs"] = ffn2_grads(g)
        return _sibling_exchange_comm(early["gs"])

    plan.comm_of["hgrn_bwd"] = sibling_comm
    plan.after["hgrn_bwd"] = after_sibling
    plan.comm_of["rwkv_bwd"] = lambda g: _chip_exchange_comm(early["s4"])
    plan.after["rwkv_bwd"] = after_chips
    loss_slab, grad_x, g = _local_step(x[0], loss_target[0], w, plan)
    loss = lax.psum(loss_slab[0, 0], ("x", "y", "c"))

    grows = {
        "ffn1_w_gate": g["ffn1_wgt"], "ffn1_w_up": g["ffn1_wut"], "ffn1_w_down": g["ffn1_wd"],
        "w_out": jnp.concatenate([g["w_out_a"], g["w_out_b"]], axis=0),
    }
    g_w_in = jnp.concatenate([g["w_in_h"], g["w_in_r"][:, :N_RWKV_COLS]], axis=1)
    gfull = {
        "rwkv_w2": g["w2_pad"][0:32], "rwkv_a2": g["a2_pad"][32:64], "rwkv_g2": g["g2_pad"][64:160],
    }
    gsmall = {
        "ffn1_norm": g["ffn1_norm"], "mix_norm": g["mix_norm"],
        "hgrn_lb_logits": jnp.concatenate([g["lb0"], g["lb1"]], axis=0), "hgrn_out_norm": g["hgrn_out_norm"],
        "rwkv_shift_mu": g["mu_pad"][:, :N_RWKV_COLS], "rwkv_w0": g["rwkv_w0"], "rwkv_a0": g["rwkv_a0"],
        "rwkv_k_k": g["rwkv_k_k"], "rwkv_k_a": g["rwkv_k_a"], "rwkv_r_k": g["rwkv_r_k"],
        "rwkv_gn_w": g["rwkv_gn_w"], "rwkv_gn_b": g["rwkv_gn_b"], "ffn2_norm": g["ffn2_norm"],
        "final_norm": g["final_norm"],
    }
    late_names = list(group["ffn1"]) + list(group["mix"])
    gs = [grows[n].reshape(N_CHIPS, -1, D_MODEL) for n in late_names[:4]]
    gs.append(jnp.stack([_quarter(g_w_in, "w_in", q) for q in range(N_CHIPS)]))
    gs.append(jnp.stack([_pack({n: _quarter(gfull[n], n, q) for n in PACKED}, gsmall) for q in range(N_CHIPS)]))
    own_of = dict(zip(late_names, reduce_rows(late_names, gs)))
    own_of.update(zip(group["ffn2"], early["own"]))
    names = list(BIG) + ["packed"]
    own = [own_of[n] for n in names]
    other = _run_comm(_sibling_swap_comm(own), "grad_sibling_swap")

    def rows_list(d):
        return [_to_rows(n, d[n]) for n in BIG] + [_pack(d, {n: d[n] for n in SMALL})]

    outs = [_adamw(wt, go, gx, mt, vt, c_idx, f"adamw_{n}")
            for wt, go, gx, mt, vt, n in zip(rows_list(wts), own, other, rows_list(moms), rows_list(vars_), names)]
    results = []
    for k in range(4):
        per = [outs[i][k] for i in range(len(names))]
        d = {n: _from_rows(n, z) for n, z in zip(BIG, per[:-1])}
        d.update(_unpack(per[-1]))
        results.append(d)
    return (loss, grad_x[None], *[r[n] for r in results for n in ALL_WEIGHTS])
```

```python
import collections
import functools

import jax
import jax.numpy as jnp
from jax import lax
from jax.experimental import pallas as pl
from jax.experimental.pallas import tpu as pltpu

F32 = jnp.float32
BF16 = jnp.bfloat16
SDS = jax.ShapeDtypeStruct
MESH = pl.DeviceIdType.MESH

D_MODEL = 1024
D_FF = 2816
W_A = 512
W_B = 512
HA_HEADS, HA_DIM = 4, 128
HB_HEADS, HB_DIM = 8, 64
HGRN_CHUNK = 64
HGRN_GROUP = 2
RWKV_CHUNK = 16
RWKV_GROUP = 4
N_HGRN_COLS = 4 * W_A
N_RWKV_COLS = 3 * W_B + 32 + 32 + 96
N_RWKV_PAD = 1792
LORA_PAD = 256
NORM_EPS = 1e-6
RWKV_GN_EPS = 64e-5
L2_EPS = 1e-12
ADAM_LR, ADAM_B1, ADAM_B2, ADAM_EPS, ADAM_WD, ADAM_STEP = 0.001, 0.9, 0.999, 1e-8, 0.01, 10

N_CHIPS = 4
VMEM_LIMIT_V7X = 56 * 1024 * 1024
LANES = 1024

SHARDED_SHAPES = {
    "ffn1_w_gate": ((D_MODEL, D_FF), 1), "ffn1_w_up": ((D_MODEL, D_FF), 1), "ffn1_w_down": ((D_FF, D_MODEL), 0),
    "w_in": ((D_MODEL, N_HGRN_COLS + N_RWKV_COLS), 1), "rwkv_w2": ((32, W_B), 1), "rwkv_a2": ((32, W_B), 1),
    "rwkv_g2": ((96, W_B), 1), "w_out": ((D_MODEL, D_MODEL), 0),
    "ffn2_w_gate": ((D_MODEL, D_FF), 1), "ffn2_w_up": ((D_MODEL, D_FF), 1), "ffn2_w_down": ((D_FF, D_MODEL), 0),
}
SMALL = ("ffn1_norm", "mix_norm", "hgrn_lb_logits", "hgrn_out_norm", "rwkv_shift_mu", "rwkv_w0", "rwkv_a0",
         "rwkv_k_k", "rwkv_k_a", "rwkv_r_k", "rwkv_gn_w", "rwkv_gn_b", "ffn2_norm", "final_norm")
ALL_WEIGHTS = ("ffn1_norm", "ffn1_w_gate", "ffn1_w_up", "ffn1_w_down", "mix_norm", "w_in", "hgrn_lb_logits",
               "hgrn_out_norm", "rwkv_shift_mu", "rwkv_w0", "rwkv_w2", "rwkv_a0", "rwkv_a2", "rwkv_g2", "rwkv_k_k",
               "rwkv_k_a", "rwkv_r_k", "rwkv_gn_w", "rwkv_gn_b", "w_out", "ffn2_norm", "ffn2_w_gate", "ffn2_w_up",
               "ffn2_w_down", "final_norm")


def _shard_shape(name):
    shape, ax = SHARDED_SHAPES[name]
    return tuple(s // N_CHIPS if i == ax else s for i, s in enumerate(shape))


def _numel(shape):
    n = 1
    for s in shape:
        n *= s
    return n


def _params(sem=None):
    return pltpu.CompilerParams(dimension_semantics=sem, vmem_limit_bytes=VMEM_LIMIT_V7X)


def _split2(x):
    hi = x.astype(BF16)
    return hi, (x.astype(F32) - hi.astype(F32)).astype(BF16)


def _dg(x, y, cx, cy, hi):
    dn = (((cx,), (cy,)), ((), ()))
    dot = lambda p, q: lax.dot_general(p, q, dn, preferred_element_type=F32)
    if hi == "x3":
        (xh, xl), (yh, yl) = _split2(x), _split2(y)
        return dot(xh, yh) + (dot(xh, yl) + dot(xl, yh))
    return dot(x.astype(BF16), y.astype(BF16))


def _make_mm(hi, cotangent_forms=None):
    @jax.custom_vjp
    def nn(x, y):
        return _dg(x, y, 1, 0, hi)

    @jax.custom_vjp
    def nt(x, y):
        return _dg(x, y, 1, 1, hi)

    @jax.custom_vjp
    def tn(x, y):
        return _dg(x, y, 0, 0, hi)

    bnn, bnt, btn = cotangent_forms or (nn, nt, tn)
    nn.defvjp(lambda x, y: (nn(x, y), (x, y)), lambda r, g: (bnt(g, r[1]), btn(r[0], g)))
    nt.defvjp(lambda x, y: (nt(x, y), (x, y)), lambda r, g: (bnn(g, r[1]), btn(g, r[0])))
    tn.defvjp(lambda x, y: (tn(x, y), (x, y)), lambda r, g: (bnt(r[1], g), bnn(r[0], g)))
    return nn, nt, tn


_nn, _nt, _tn = _make_mm(False)
_nn_x3, _nt_x3, _tn_x3 = _make_mm("x3", (_nn, _nt, _tn))


def _tri_apply(x, transpose):
    c = x.shape[0]
    tri = (lax.broadcasted_iota(jnp.int32, (c, c), 1) <= lax.broadcasted_iota(jnp.int32, (c, c), 0)).astype(BF16)
    dn = (((0 if transpose else 1,), (0,)), ((), ()))
    p1 = x.astype(BF16)
    r1 = x - p1.astype(F32)
    p2 = r1.astype(BF16)
    p3 = (r1 - p2.astype(F32)).astype(BF16)
    dot = lambda p: lax.dot_general(tri, p, dn, preferred_element_type=F32)
    return dot(p1) + (dot(p2) + dot(p3))


@jax.custom_vjp
def _cumsum_rows(x):
    return _tri_apply(x, False)


_cumsum_rows.defvjp(lambda x: (_tri_apply(x, False), None), lambda _, g: (_tri_apply(g, True),))


def _sigmoid(x):
    return 1.0 / (1.0 + jnp.exp(-x))


def _silu(x):
    return x * _sigmoid(x)


def _softplus(z):
    return jnp.maximum(z, 0.0) + jnp.log(1.0 + jnp.exp(-jnp.abs(z)))


def _mm(a, b, *, ta=False, tb=False, tm, tn, tk, name, out_dtype=F32, res=None, scale=None):
    m = a.shape[1] if ta else a.shape[0]
    kdim = a.shape[0] if ta else a.shape[1]
    n = b.shape[0] if tb else b.shape[1]
    assert (b.shape[1] if tb else b.shape[0]) == kdim
    tm, tn, tk = min(tm, m), min(tn, n), min(tk, kdim)
    assert m % tm == 0 and n % tn == 0 and kdim % tk == 0, (name, m, n, kdim)
    nk = kdim // tk
    a_spec = pl.BlockSpec((tk, tm), lambda i, j, k: (k, i)) if ta else pl.BlockSpec((tm, tk), lambda i, j, k: (i, k))
    b_spec = pl.BlockSpec((tn, tk), lambda i, j, k: (j, k)) if tb else pl.BlockSpec((tk, tn), lambda i, j, k: (k, j))
    o_spec = pl.BlockSpec((tm, tn), lambda i, j, k: (i, j))
    ca, cb = (0 if ta else 1), (1 if tb else 0)

    def body(*refs):
        if res is not None:
            a_ref, b_ref, r_ref, o_ref, acc_ref = refs
        else:
            a_ref, b_ref, o_ref, acc_ref = refs
        k = pl.program_id(2)

        @pl.when(k == 0)
        def _():
            acc_ref[...] = jnp.zeros_like(acc_ref)

        acc_ref[...] += _dg(a_ref[...], b_ref[...], ca, cb, False)

        @pl.when(k == nk - 1)
        def _():
            acc = acc_ref[...]
            if scale is not None:
                acc = acc * scale
            if res is not None:
                acc = r_ref[...] + acc
            o_ref[...] = acc.astype(out_dtype)

    in_specs = [a_spec, b_spec] + ([o_spec] if res is not None else [])
    args = (a, b) + ((res,) if res is not None else ())
    return pl.pallas_call(
        body, name=name, grid=(m // tm, n // tn, nk), in_specs=in_specs, out_specs=o_spec,
        out_shape=SDS((m, n), out_dtype), scratch_shapes=[pltpu.VMEM((tm, tn), F32)],
        compiler_params=_params(("parallel", "parallel", "arbitrary")))(*args)


def _row_spec(x, tm):
    if isinstance(x, tuple):
        arr, w, j = x
        return arr, pl.BlockSpec((tm, w), lambda i, j=j: (i, j))
    return x, pl.BlockSpec((tm, x.shape[1]), lambda i: (i, 0))


def _par_spec(p):
    if isinstance(p, tuple):
        arr, w, j = p
        return arr, pl.BlockSpec((arr.shape[0], w), lambda i, j=j: (0, j))
    return p, pl.BlockSpec(p.shape, lambda i: (0, 0))


def _store_groups(refs, groups, vals):
    for ref, idxs in zip(refs, groups):
        off = 0
        for ix in idxs:
            v = vals[ix]
            ref[:, off:off + v.shape[1]] = v.astype(ref.dtype)
            off += v.shape[1]


SUBLANES = 8


def _x_plan(xs, tm, t):
    arrays, specs, plan = [], [], []
    nb = tm // SUBLANES
    for x in xs:
        if isinstance(x, tuple) and isinstance(x[0], str):
            kind, arr, w, j = x
            if kind == "prev":
                halo = lambda i, j=j: (jnp.maximum(i * nb - 1, 0), j)
            else:
                halo = lambda i, j=j: (jnp.minimum((i + 1) * nb, t // SUBLANES - 1), j)
            arrays += [arr, arr]
            specs += [pl.BlockSpec((tm, w), lambda i, j=j: (i, j)), pl.BlockSpec((SUBLANES, w), halo)]
            plan.append((kind, 2, w))
        else:
            arr, spec = _row_spec(x, tm)
            arrays.append(arr)
            specs.append(spec)
            plan.append(("plain", 1, spec.block_shape[1]))
    return arrays, specs, plan


def _x_vals(refs, plan, tm, nt):
    vals, k = [], 0
    i = pl.program_id(0)
    rows = lax.broadcasted_iota(jnp.int32, (tm, 1), 0)
    for kind, n, _ in plan:
        main = refs[k][...].astype(F32)
        if kind == "prev":
            edge = jnp.where(i == 0, 0.0, refs[k + 1][SUBLANES - 1:SUBLANES, :].astype(F32))
            main = jnp.where(rows == 0, edge, pltpu.roll(main, 1, 0))
        elif kind == "next":
            edge = jnp.where(i == nt - 1, 0.0, refs[k + 1][0:1, :].astype(F32))
            main = jnp.where(rows == tm - 1, edge, pltpu.roll(main, tm - 1, 0))
        vals.append(main)
        k += n
    return vals


def _tile_rows(xs, tm):
    arr = xs[0]
    if isinstance(arr, tuple):
        arr = arr[1] if isinstance(arr[0], str) else arr[0]
    return min(tm, arr.shape[0]), arr.shape[0]


def _rowwise(f, xs, params, out_groups, out_dtypes, *, tm, name):
    tm, t = _tile_rows(xs, tm)
    nt = t // tm
    xa, xspecs, plan = _x_plan(xs, tm, t)
    pa, pspecs = (zip(*[_par_spec(p) for p in params]) if params else ((), ()))
    nxr, npar = len(xa), len(pa)
    x_sds = [SDS((tm, w), F32) for _, _, w in plan]
    p_sds = [SDS(s.block_shape, F32) for s in pspecs]
    outs_sds = jax.eval_shape(lambda *vals: f(*vals), *x_sds, *p_sds)
    widths = [sum(outs_sds[ix].shape[1] for ix in idxs) for idxs in out_groups]

    def body(*refs):
        vals = _x_vals(refs[:nxr], plan, tm, nt) + [r[...].astype(F32) for r in refs[nxr:nxr + npar]]
        outs = f(*vals)
        _store_groups(refs[nxr + npar:], out_groups, outs)

    return pl.pallas_call(
        body, name=name, grid=(nt,), in_specs=list(xspecs) + list(pspecs),
        out_specs=[pl.BlockSpec((tm, w), lambda i: (i, 0)) for w in widths],
        out_shape=[SDS((t, w), dt) for w, dt in zip(widths, out_dtypes)],
        compiler_params=_params(("parallel",)))(*xa, *pa)


def _rowwise_bwd(f, xs, params, cots, *, x_grad, p_grad, dx_groups, dx_dtypes, tm, name, extra=None, comm=None):
    tm, t = _tile_rows(xs, tm)
    nt = t // tm
    xa, xspecs, plan = _x_plan(xs, tm, t)
    pa, pspecs = (zip(*[_par_spec(p) for p in params]) if params else ((), ()))
    ca, cspecs = zip(*[_row_spec(c, tm) for c in cots])
    extra = extra or {}
    ekeys = sorted(extra)
    ea, especs = (zip(*[_row_spec(extra[k], tm) for k in ekeys]) if ekeys else ((), ()))
    nx, nxr, npar, nc, ne = len(plan), len(xa), len(pa), len(ca), len(ea)
    gx = [i for i in range(nx) if x_grad[i]]
    gp = [i for i in range(npar) if p_grad[i]]
    widths = [sum(plan[gx[ix]][2] for ix in idxs) for idxs in dx_groups]
    ng = len(dx_groups)

    def body(*refs):
        ins = refs[:nxr + npar + nc + ne]
        outs = refs[nxr + npar + nc + ne:]
        vals = _x_vals(ins[:nxr], plan, tm, nt) + [r[...].astype(F32) for r in ins[nxr:nxr + npar]]
        cvals = tuple(r[...].astype(F32) for r in ins[nxr + npar:nxr + npar + nc])
        evals = [r[...].astype(F32) for r in ins[nxr + npar + nc:]]
        diff_idx = gx + [nx + i for i in gp]

        def g(*dargs):
            full = list(vals)
            for ix, v in zip(diff_idx, dargs):
                full[ix] = v
            return tuple(f(*full))

        _, vjp = jax.vjp(g, *[vals[ix] for ix in diff_idx])
        grads = vjp(cvals)
        dxs = list(grads[:len(gx)])
        for k, ev in zip(ekeys, evals):
            dxs[k] = dxs[k] + ev
        _store_groups(outs[:ng], dx_groups, dxs)
        i = pl.program_id(0)
        for ref, gval in zip(outs[ng:], grads[len(gx):]):
            @pl.when(i == 0)
            def _(ref=ref):
                ref[...] = jnp.zeros_like(ref)
            ref[...] += gval

    dp_specs = [pl.BlockSpec(pspecs[i].block_shape, lambda i: (0, 0)) for i in gp]
    dp_shapes = [SDS(pspecs[i].block_shape, F32) for i in gp]
    res, carried = _hosting_call(
        body, comm, name=name, grid=(nt,), in_specs=list(xspecs) + list(pspecs) + list(cspecs) + list(especs),
        out_specs=[pl.BlockSpec((tm, w), lambda i: (i, 0)) for w in widths] + dp_specs,
        out_shape=[SDS((t, w), dt) for w, dt in zip(widths, dx_dtypes)] + dp_shapes, scratch_shapes=[],
        args=(*xa, *pa, *ca, *ea))
    return res if comm is None else (res, carried)


def _rms_f(x, g):
    return (x * lax.rsqrt(jnp.mean(x * x, axis=-1, keepdims=True) + NORM_EPS) * g,)


def _three_pieces(x):
    p1 = x.astype(BF16)
    r1 = x - p1.astype(F32)
    p2 = r1.astype(BF16)
    return p1, p2, (r1 - p2.astype(F32)).astype(BF16)


def _group_sum_impl(x, ones_bd):
    p1, p2, p3 = _three_pieces(x)
    dot = lambda p: lax.dot_general(p, ones_bd.astype(BF16), (((1,), (0,)), ((), ())), preferred_element_type=F32)
    return dot(p1) + (dot(p2) + dot(p3))


@jax.custom_vjp
def _group_sum(x, ones_bd):
    return _group_sum_impl(x, ones_bd)


_group_sum.defvjp(lambda x, o: (_group_sum_impl(x, o), o),
                  lambda o, g: (_group_sum_impl(g, o), jnp.zeros_like(o)))


def _rwkv_prep_f(r, k, v, lo, rp, kp, vp, lop, mu_r, mu_k, mu_v, mu_lo, w0, w2p, a0, a2p, g2p, k_k, k_a, ones_bd):
    r = r + mu_r * (rp - r)
    k = k + mu_k * (kp - k)
    v = v + mu_v * (vp - v)
    lo = lo + mu_lo * (lop - lo)
    w_log = -_softplus(-(w0 + _nn(jnp.tanh(lo), w2p))) - 0.5
    lw = -jnp.exp(w_log)
    a_g = _sigmoid(a0 + _nn(lo, a2p))
    g = _nn(_sigmoid(lo), g2p)
    kk = k * k_k
    kk = kk / jnp.maximum(jnp.sqrt(_group_sum(kk * kk, ones_bd)), L2_EPS)
    k2 = k * (1.0 + (a_g - 1.0) * k_a)
    return r, lw, k2, v, -kk, kk * a_g, g


def _rwkv_post_f(y, r, k2, v, g, r_k, gn_w, gn_b, ones_bd):
    inv_n = 1.0 / HB_DIM
    mean = _group_sum(y, ones_bd) * inv_n
    yc = y - mean
    var = _group_sum(yc * yc, ones_bd) * inv_n
    yn = yc * lax.rsqrt(var + RWKV_GN_EPS) * gn_w + gn_b
    bonus = _group_sum(r * k2 * r_k, ones_bd) * v
    return ((yn + bonus) * g,)


def _tri(c, strict=False):
    ii = lax.broadcasted_iota(jnp.int32, (c, c), 0)
    jj = lax.broadcasted_iota(jnp.int32, (c, c), 1)
    return (jj < ii) if strict else (jj <= ii)


def _hgrn_step(st0, q_a, f_a, i_a, g_a, l0, l1, onorm):
    nh, nj = len(q_a), len(q_a[0])
    c = q_a[0][0].shape[0]
    combos = [(j, h) for j in range(nj) for h in range(nh)]
    every = lambda fn: {q: fn(q) for q in combos}
    at_ = lambda d: (lambda q: d[q[1]][q[0]])
    qa_, fa_, ia_, ga_ = (at_(z) for z in (q_a, f_a, i_a, g_a))
    incl = _tri(c)
    rows = lax.broadcasted_iota(jnp.int32, (c, 1), 0)
    lb = []
    for h in range(nh):
        mx = jnp.maximum(l0[h], l1[h])
        e0, e1 = jnp.exp(l0[h] - mx), jnp.exp(l1[h] - mx)
        lb.append(e0 / (e0 + e1))
    forget = every(lambda q: lb[q[1]] + (1.0 - lb[q[1]]) * _sigmoid(fa_(q)))
    qs = every(lambda q: _silu(qa_(q)))
    kk = every(lambda q: 1.0 - forget[q])
    lf = every(lambda q: jnp.log(forget[q]))
    bcum = every(lambda q: _cumsum_rows(lf[q]))
    bref = every(lambda q: jnp.sum(jnp.where(rows <= c // 2, lf[q], 0.0), axis=0, keepdims=True))
    blast = every(lambda q: jnp.sum(lf[q], axis=0, keepdims=True))
    scores = every(lambda q: jnp.where(incl, _nt(qs[q] * jnp.exp(bcum[q] - bref[q]),
                                                 kk[q] * jnp.exp(bref[q] - bcum[q])), 0.0))
    intra = every(lambda q: _nn(scores[q], ia_(q)))
    qb = every(lambda q: qs[q] * jnp.exp(bcum[q]))
    upd = every(lambda q: _tn(ia_(q), kk[q] * jnp.exp(blast[q] - bcum[q])))
    dec = every(lambda q: jnp.exp(blast[q]))
    st = list(st0)
    o = {}
    for j in range(nj):
        for h in range(nh):
            o[(j, h)] = intra[(j, h)] + _nt(qb[(j, h)], st[h])
        st = [st[h] * dec[(j, h)] + upd[(j, h)] for h in range(nh)]
    out = every(lambda q: o[q] * lax.rsqrt(jnp.mean(o[q] * o[q], axis=-1, keepdims=True) + NORM_EPS)
                * onorm[q[1]] * _silu(ga_(q)))
    return [[out[(j, h)] for j in range(nj)] for h in range(nh)], st


def _hgrn_blocks(ref, nj, c):
    return [[ref[j * c:(j + 1) * c, h * HA_DIM:(h + 1) * HA_DIM] for j in range(nj)] for h in range(HA_HEADS)]


def _hgrn_cols(ref):
    return [ref[:, h * HA_DIM:(h + 1) * HA_DIM] for h in range(HA_HEADS)]


def _hgrn_fwd(p_h, l0, l1, onorm):
    t = p_h.shape[0]
    cc, nj = HGRN_CHUNK, HGRN_GROUP
    c = cc * nj
    n = t // c

    def body(q_ref, f_ref, i_ref, g_ref, l0_ref, l1_ref, on_ref, o_ref, hs_ref, st_ref):
        @pl.when(pl.program_id(0) == 0)
        def _():
            st_ref[...] = jnp.zeros_like(st_ref)

        hs_ref[0] = st_ref[...]
        o, st1 = _hgrn_step([st_ref[h] for h in range(HA_HEADS)],
                            *[_hgrn_blocks(ref, nj, cc) for ref in (q_ref, f_ref, i_ref, g_ref)],
                            _hgrn_cols(l0_ref), _hgrn_cols(l1_ref), _hgrn_cols(on_ref))
        for h in range(HA_HEADS):
            for j in range(nj):
                o_ref[j * cc:(j + 1) * cc, h * HA_DIM:(h + 1) * HA_DIM] = o[h][j]
            st_ref[h] = st1[h]

    col = lambda j: pl.BlockSpec((c, W_A), lambda i, j=j: (i, j))
    par = pl.BlockSpec((1, W_A), lambda i: (0, 0))
    return pl.pallas_call(
        body, name="hgrn_fwd", grid=(n,), in_specs=[col(0), col(1), col(2), col(3), par, par, par],
        out_specs=[pl.BlockSpec((c, W_A), lambda i: (i, 0)),
                   pl.BlockSpec((1, HA_HEADS, HA_DIM, HA_DIM), lambda i: (i, 0, 0, 0))],
        out_shape=[SDS((t, W_A), F32), SDS((n, HA_HEADS, HA_DIM, HA_DIM), F32)],
        scratch_shapes=[pltpu.VMEM((HA_HEADS, HA_DIM, HA_DIM), F32)],
        compiler_params=_params(("arbitrary",)))(p_h, p_h, p_h, p_h, l0, l1, onorm)


def _hgrn_bwd(p_h, l0, l1, onorm, hs, do, do_col, comm=None):
    t = p_h.shape[0]
    cc, nj = HGRN_CHUNK, HGRN_GROUP
    c = cc * nj
    n = t // c

    def body(q_ref, f_ref, i_ref, g_ref, l0_ref, l1_ref, on_ref, hs_ref, do_ref,
             dp_ref, dl0_ref, dl1_ref, don_ref, dst_ref):
        @pl.when(pl.program_id(0) == 0)
        def _():
            dst_ref[...] = jnp.zeros_like(dst_ref)
            dl0_ref[...] = jnp.zeros_like(dl0_ref)
            dl1_ref[...] = jnp.zeros_like(dl1_ref)
            don_ref[...] = jnp.zeros_like(don_ref)

        args = ([hs_ref[0, h] for h in range(HA_HEADS)],
                *[_hgrn_blocks(ref, nj, cc) for ref in (q_ref, f_ref, i_ref, g_ref)],
                _hgrn_cols(l0_ref), _hgrn_cols(l1_ref), _hgrn_cols(on_ref))
        _, vjp = jax.vjp(_hgrn_step, *args)
        dst0, dq, df, di, dg, dl0, dl1, don = vjp((_hgrn_blocks(do_ref, nj, cc),
                                                   [dst_ref[h] for h in range(HA_HEADS)]))
        for h in range(HA_HEADS):
            sl = slice(h * HA_DIM, (h + 1) * HA_DIM)
            for k, dv in enumerate((dq, df, di, dg)):
                for j in range(nj):
                    dp_ref[j * cc:(j + 1) * cc, k * W_A + h * HA_DIM:k * W_A + (h + 1) * HA_DIM] = dv[h][j]
            dl0_ref[:, sl] += dl0[h]
            dl1_ref[:, sl] += dl1[h]
            don_ref[:, sl] += don[h]
            dst_ref[h] = dst0[h]

    col = lambda j: pl.BlockSpec((c, W_A), lambda i, j=j: (n - 1 - i, j))
    par = pl.BlockSpec((1, W_A), lambda i: (0, 0))
    return _hosting_call(
        body, comm, name="hgrn_bwd", grid=(n,),
        in_specs=[col(0), col(1), col(2), col(3), par, par, par,
                  pl.BlockSpec((1, HA_HEADS, HA_DIM, HA_DIM), lambda i: (n - 1 - i, 0, 0, 0)),
                  pl.BlockSpec((c, W_A), lambda i: (n - 1 - i, do_col))],
        out_specs=[pl.BlockSpec((c, N_HGRN_COLS), lambda i: (n - 1 - i, 0)), par, par, par],
        out_shape=[SDS((t, N_HGRN_COLS), F32), SDS((1, W_A), F32), SDS((1, W_A), F32), SDS((1, W_A), F32)],
        scratch_shapes=[pltpu.VMEM((HA_HEADS, HA_DIM, HA_DIM), F32)],
        args=(p_h, p_h, p_h, p_h, l0, l1, onorm, hs, do))


HB_PAIRS = HB_HEADS // 2
PAIR_W = 2 * HB_DIM


def _head_lane_masks():
    lane = lax.broadcasted_iota(jnp.int32, (1, PAIR_W), 1)
    return (lane < HB_DIM).astype(F32), (lane >= HB_DIM).astype(F32)


@jax.custom_vjp
def _stack_heads(x):
    m0, m1 = _head_lane_masks()
    return jnp.concatenate([x * m0, x * m1], axis=0)


def _stack_heads_bwd(_, g):
    m0, m1 = _head_lane_masks()
    c = g.shape[0] // 2
    return (g[:c] * m0 + g[c:] * m1,)


_stack_heads.defvjp(lambda x: (_stack_heads(x), None), _stack_heads_bwd)


@jax.custom_vjp
def _unstack_heads(ys):
    c = ys.shape[0] // 2
    return ys[:c] + ys[c:]


_unstack_heads.defvjp(lambda ys: (_unstack_heads(ys), None), lambda _, g: (_stack_heads(g),))


def _same_head_block(c):
    ii = lax.broadcasted_iota(jnp.int32, (2 * c, 2 * c), 0)
    jj = lax.broadcasted_iota(jnp.int32, (2 * c, 2 * c), 1)
    same = (ii < c) == (jj < c)
    return same & (jj <= ii), same & (jj < ii), (ii == jj).astype(F32)


@jax.custom_vjp
def _rows_join(top, bottom):
    return jnp.concatenate([top, bottom], axis=0)


def _rows_join_bwd(n_top, g):
    return g[:n_top], g[n_top:]


_rows_join.defvjp(lambda top, bottom: (_rows_join(top, bottom), top.shape[0]), _rows_join_bwd)


def _rows_split_impl(x, n_top):
    return x[:n_top], x[n_top:]


_rows_split = jax.custom_vjp(_rows_split_impl, nondiff_argnums=(1,))
_rows_split.defvjp(lambda x, n_top: (_rows_split_impl(x, n_top), None),
                   lambda n_top, _, g: (jnp.concatenate([g[0], g[1]], axis=0),))


def _rwkv_step(s0, r, lw, k, v, a, b):
    npair, nj = len(r), len(r[0])
    c = r[0][0].shape[0]
    combos = [(j, p) for j in range(nj) for p in range(npair)]
    every = lambda fn: {q: fn(q) for q in combos}
    at_ = lambda d: (lambda q: d[q[1]][q[0]])
    r_, lw_, k_, v_, a_, b_ = (at_(z) for z in (r, lw, k, v, a, b))
    incl, strict, eye = _same_head_block(c)

    gam = every(lambda q: _cumsum_rows(lw_(q)))
    gtot = every(lambda q: jnp.sum(lw_(q), axis=0, keepdims=True))
    eneg = every(lambda q: jnp.exp(-gam[q]))
    edec = every(lambda q: jnp.exp(gtot[q] - gam[q]))
    at = every(lambda q: _stack_heads(a_(q) * jnp.exp(gam[q] - lw_(q))))
    rt = every(lambda q: _stack_heads(r_(q) * jnp.exp(gam[q])))
    bt = every(lambda q: _stack_heads(b_(q) * eneg[q]))
    kt = every(lambda q: _stack_heads(k_(q) * eneg[q]))
    bdec = every(lambda q: _stack_heads(b_(q) * edec[q]))
    kdec = every(lambda q: _stack_heads(k_(q) * edec[q]))
    vs = every(lambda q: _stack_heads(v_(q)))
    a_ab = every(lambda q: jnp.where(strict, _nt(at[q], bt[q]), 0.0))
    a_ak = every(lambda q: jnp.where(strict, _nt(at[q], kt[q]), 0.0))
    a_rb = every(lambda q: jnp.where(incl, _nt(rt[q], bt[q]), 0.0))
    a_rk = every(lambda q: jnp.where(incl, _nt(rt[q], kt[q]), 0.0))
    tinv = every(lambda q: eye + a_ab[q])
    pw = a_ab
    span = 2
    while span < c:
        pw = every(lambda q, pw=pw: _nn_x3(pw[q], pw[q]))
        tinv = every(lambda q, pw=pw, tinv=tinv: tinv[q] + _nn_x3(pw[q], tinv[q]))
        span *= 2
    akv = every(lambda q: _nn(a_ak[q], vs[q]))
    w1 = every(lambda q: _nn_x3(tinv[q], at[q]))
    u0 = every(lambda q: _nn_x3(tinv[q], akv[q]))
    wr = every(lambda q: _rows_join(w1[q], rt[q]))
    bk = every(lambda q: _rows_join(bdec[q], kdec[q]))
    yv = every(lambda q: _nn(a_rk[q], vs[q]))
    gdec = every(lambda q: jnp.exp(gtot[q]))

    s = list(s0)
    y = [[None] * nj for _ in range(npair)]
    for j in range(nj):
        both = {p: _rows_split(_nt(wr[(j, p)], s[p]), 2 * c) for p in range(npair)}
        u = {p: both[p][0] + u0[(j, p)] for p in range(npair)}
        for p in range(npair):
            y[p][j] = _unstack_heads(both[p][1] + _nn(a_rb[(j, p)], u[p]) + yv[(j, p)])
        s = [s[p] * gdec[(j, p)] + _tn(_rows_join(u[p], vs[(j, p)]), bk[(j, p)]) for p in range(npair)]
    return y, s


def _rwkv_blocks(ref, nj, c):
    return [[ref[j * c:(j + 1) * c, p * PAIR_W:(p + 1) * PAIR_W] for j in range(nj)] for p in range(HB_PAIRS)]


def _rwkv_fwd(seqs, comm=None):
    t = seqs[0].shape[0]
    c, nj = RWKV_CHUNK, RWKV_GROUP
    n = t // (c * nj)

    def body(r_ref, lw_ref, k_ref, v_ref, a_ref, b_ref, y_ref, hs_ref, st_ref):
        @pl.when(pl.program_id(0) == 0)
        def _():
            st_ref[...] = jnp.zeros_like(st_ref)

        hs_ref[0] = st_ref[...]
        s0 = [st_ref[p] for p in range(HB_PAIRS)]
        y, s1 = _rwkv_step(s0, *[_rwkv_blocks(ref, nj, c) for ref in (r_ref, lw_ref, k_ref, v_ref, a_ref, b_ref)])
        for p in range(HB_PAIRS):
            for j in range(nj):
                y_ref[j * c:(j + 1) * c, p * PAIR_W:(p + 1) * PAIR_W] = y[p][j]
            st_ref[p] = s1[p]

    seq = pl.BlockSpec((c * nj, W_B), lambda i: (i, 0))
    return _hosting_call(
        body, comm, name="rwkv_fwd", grid=(n,), in_specs=[seq] * 6,
        out_specs=[seq, pl.BlockSpec((1, HB_PAIRS, PAIR_W, PAIR_W), lambda i: (i, 0, 0, 0))],
        out_shape=[SDS((t, W_B), F32), SDS((n, HB_PAIRS, PAIR_W, PAIR_W), F32)],
        scratch_shapes=[pltpu.VMEM((HB_PAIRS, PAIR_W, PAIR_W), F32)], args=tuple(seqs))


def _rwkv_bwd(seqs, hs, dy, comm=None):
    t = seqs[0].shape[0]
    c, nj = RWKV_CHUNK, RWKV_GROUP
    n = t // (c * nj)

    def body(r_ref, lw_ref, k_ref, v_ref, a_ref, b_ref, hs_ref, dy_ref,
             dr_ref, dlw_ref, dk_ref, dv_ref, da_ref, db_ref, dst_ref):
        @pl.when(pl.program_id(0) == 0)
        def _():
            dst_ref[...] = jnp.zeros_like(dst_ref)

        s0 = [hs_ref[0, p] for p in range(HB_PAIRS)]
        seq_vals = [_rwkv_blocks(ref, nj, c) for ref in (r_ref, lw_ref, k_ref, v_ref, a_ref, b_ref)]
        _, vjp = jax.vjp(_rwkv_step, s0, *seq_vals)
        grads = vjp((_rwkv_blocks(dy_ref, nj, c), [dst_ref[p] for p in range(HB_PAIRS)]))
        for ref, gr in zip((dr_ref, dlw_ref, dk_ref, dv_ref, da_ref, db_ref), grads[1:]):
            for p in range(HB_PAIRS):
                for j in range(nj):
                    ref[j * c:(j + 1) * c, p * PAIR_W:(p + 1) * PAIR_W] = gr[p][j]
        m0, m1 = _head_lane_masks()
        rows0 = (lax.broadcasted_iota(jnp.int32, (PAIR_W, 1), 0) < HB_DIM).astype(F32)
        blocks = rows0 * m0 + (1.0 - rows0) * m1
        for p in range(HB_PAIRS):
            dst_ref[p] = grads[0][p] * blocks

    seq = pl.BlockSpec((c * nj, W_B), lambda i: (n - 1 - i, 0))
    return _hosting_call(
        body, comm, name="rwkv_bwd", grid=(n,),
        in_specs=[seq] * 6 + [pl.BlockSpec((1, HB_PAIRS, PAIR_W, PAIR_W), lambda i: (n - 1 - i, 0, 0, 0)), seq],
        out_specs=[seq] * 6, out_shape=[SDS((t, W_B), F32)] * 6,
        scratch_shapes=[pltpu.VMEM((HB_PAIRS, PAIR_W, PAIR_W), F32)], args=(*seqs, hs, dy))


def _final_loss(x3, fnorm, target, *, tm):
    t, d = x3.shape

    def body(x_ref, g_ref, t_ref, dx_ref, dg_ref, loss_ref):
        @pl.when(pl.program_id(0) == 0)
        def _():
            dg_ref[...] = jnp.zeros_like(dg_ref)
            loss_ref[...] = jnp.zeros_like(loss_ref)

        x, g = x_ref[...], g_ref[...]
        rinv = lax.rsqrt(jnp.mean(x * x, axis=-1, keepdims=True) + NORM_EPS)
        xh = x * rinv
        diff = xh * g - t_ref[...]
        loss_ref[...] += 0.5 * jnp.sum(jnp.mean(diff * diff, axis=-1, keepdims=True))
        dy = diff * (1.0 / d)
        dg_ref[...] += jnp.sum(dy * xh, axis=0, keepdims=True)
        dxh = dy * g
        dx_ref[...] = rinv * (dxh - xh * jnp.mean(dxh * xh, axis=-1, keepdims=True))

    row = pl.BlockSpec((tm, d), lambda i: (i, 0))
    return pl.pallas_call(
        body, name="final_loss", grid=(t // tm,), in_specs=[row, pl.BlockSpec((1, d), lambda i: (0, 0)), row],
        out_specs=[row, pl.BlockSpec((1, d), lambda i: (0, 0)), pl.BlockSpec((8, 128), lambda i: (0, 0))],
        out_shape=[SDS((t, d), F32), SDS((1, d), F32), SDS((8, 128), F32)],
        compiler_params=_params(("arbitrary",)))(x3, fnorm, target)


def _gate_up_act(h, wgt, wut, *, tm, tn, name, comm=None):
    t, d = h.shape
    tm = min(tm, t)

    def body(h_ref, g_ref, u_ref, a_out, u_out, act_out):
        hv = h_ref[...]
        a = _dg(hv, g_ref[...], 1, 1, False)
        u = _dg(hv, u_ref[...], 1, 1, False)
        a_out[...] = a
        u_out[...] = u
        act_out[...] = (_silu(a) * u).astype(act_out.dtype)

    wspec = pl.BlockSpec((tn, d), lambda i, j: (j, 0))
    ospec = pl.BlockSpec((tm, tn), lambda i, j: (i, j))
    return _hosting_call(
        body, comm, name=name, grid=(t // tm, D_FF // tn),
        in_specs=[pl.BlockSpec((tm, d), lambda i, j: (i, 0)), wspec, wspec], out_specs=[ospec, ospec, ospec],
        out_shape=[SDS((t, D_FF), F32), SDS((t, D_FF), F32), SDS((t, D_FF), BF16)], scratch_shapes=[],
        args=(h, wgt, wut))


def _dact_swiglu(dout, wd, a, u, *, tm, tn, name, comm=None):
    t, d = dout.shape
    tm = min(tm, t)

    def body(d_ref, w_ref, a_ref, u_ref, da_out, du_out):
        dact = 0.5 * _dg(d_ref[...], w_ref[...], 1, 1, False)
        av, uv = a_ref[...], u_ref[...]
        s = _sigmoid(av)
        da_out[...] = (dact * uv * (s * (1.0 + av * (1.0 - s)))).astype(da_out.dtype)
        du_out[...] = (dact * (av * s)).astype(du_out.dtype)

    tile = pl.BlockSpec((tm, tn), lambda i, j: (i, j))
    return _hosting_call(
        body, comm, name=name, grid=(t // tm, D_FF // tn),
        in_specs=[pl.BlockSpec((tm, d), lambda i, j: (i, 0)), pl.BlockSpec((tn, d), lambda i, j: (j, 0)), tile, tile],
        out_specs=[tile, tile], out_shape=[SDS((t, D_FF), BF16), SDS((t, D_FF), BF16)], scratch_shapes=[],
        args=(dout, wd, a, u))


class _Plan:
    def __init__(self):
        self.comm_of, self.after = {}, {}

    def comm(self, name, g):
        return self.comm_of[name](g) if name in self.comm_of else None

    def done(self, name, results, w):
        if name in self.after:
            self.after[name](results, w)


def _ffn_fwd(x, norm, wgt, wut, wd, tag, comm=None):
    h, = _rowwise(_rms_f, [x], [norm], [[0]], [BF16], tm=512, name=f"{tag}_rms")
    (a, u, act), carried = _gate_up_act(h, wgt, wut, tm=2048, tn=256, name=f"{tag}_gate_up", comm=comm)
    out = _mm(act, wd, tm=1024, tn=512, tk=D_FF, name=f"{tag}_down", res=x, scale=0.5)
    return out, (h, a, u, act), carried


def _ffn_bwd(dout, x, norm, wgt, wut, wd, saved, tag, comm=None):
    h, a, u, act = saved
    (da, du), carried = _dact_swiglu(dout, wd, a, u, tm=2048, tn=256, name=f"{tag}_dact", comm=comm)
    dwd = _mm(act, dout, ta=True, tm=D_FF // 2, tn=D_MODEL, tk=1024, name=f"{tag}_dwd", scale=0.5)
    dwgt = _mm(da, h, ta=True, tm=D_FF // 2, tn=D_MODEL, tk=1024, name=f"{tag}_dwg")
    dwut = _mm(du, h, ta=True, tm=D_FF // 2, tn=D_MODEL, tk=1024, name=f"{tag}_dwu")
    dh = _mm(da, wgt, tm=1024, tn=512, tk=D_FF, name=f"{tag}_dh_g")
    dh = _mm(du, wut, tm=1024, tn=512, tk=D_FF, name=f"{tag}_dh_u", res=dh)
    dx, dnorm = _rowwise_bwd(_rms_f, [x], [norm], [dh], x_grad=[True], p_grad=[True], dx_groups=[[0]],
                             dx_dtypes=[F32], tm=256, name=f"{tag}_drms", extra={0: dout})
    return dx, dnorm, dwgt, dwut, dwd, carried


def _local_step(x, target, w, plan=None):
    plan = plan or _Plan()
    ones_bd = jnp.kron(jnp.eye(HB_HEADS, dtype=F32), jnp.ones((HB_DIM, HB_DIM), F32))
    g = {}
    x1, ffn1_saved, carried = _ffn_fwd(x, w["ffn1_norm"], w["ffn1_wgt"], w["ffn1_wut"], w["ffn1_wd"], "ffn1",
                                       comm=plan.comm("ffn1_gate_up", g))
    plan.done("ffn1_gate_up", carried, w)
    hm, = _rowwise(_rms_f, [x1], [w["mix_norm"]], [[0]], [BF16], tm=512, name="mix_rms")
    p_h = _mm(hm, w["w_in_h"], tm=2048, tn=256, tk=D_MODEL, name="inproj_h")
    p_r = _mm(hm, w["w_in_r"], tm=2048, tn=256, tk=D_MODEL, name="inproj_r")
    o_a, hgrn_states = _hgrn_fwd(p_h, w["lb0"], w["lb1"], w["hgrn_out_norm"])

    mu = w["mu_pad"]
    prep_xs = [(p_r, W_B, 0), (p_r, W_B, 1), (p_r, W_B, 2), (p_r, LORA_PAD, 6),
               ("prev", p_r, W_B, 0), ("prev", p_r, W_B, 1), ("prev", p_r, W_B, 2), ("prev", p_r, LORA_PAD, 6)]
    prep_ps = [(mu, W_B, 0), (mu, W_B, 1), (mu, W_B, 2), (mu, LORA_PAD, 6), w["rwkv_w0"], w["w2_pad"], w["rwkv_a0"],
               w["a2_pad"], w["g2_pad"], w["rwkv_k_k"], w["rwkv_k_a"], ones_bd]
    prep_f = _rwkv_prep_f
    r, lw, k2, v, a_vec, b_vec, gate = _rowwise(prep_f, prep_xs, prep_ps, [[0], [1], [2], [3], [4], [5], [6]],
                                                [F32] * 7, tm=256, name="rwkv_prep")
    seqs = [r, lw, k2, v, a_vec, b_vec]
    (y, rwkv_states), carried = _rwkv_fwd(seqs, comm=plan.comm("rwkv_fwd", g))
    plan.done("rwkv_fwd", carried, w)
    post_f = _rwkv_post_f
    post_xs = [y, r, k2, v, gate]
    post_ps = [w["rwkv_r_k"], w["rwkv_gn_w"], w["rwkv_gn_b"], ones_bd]
    o_b, = _rowwise(post_f, post_xs, post_ps, [[0]], [F32], tm=256, name="rwkv_post")
    x2 = _mm(o_a, w["w_out_a"], tm=2048, tn=256, tk=W_A, name="outproj_a", res=x1)
    x2 = _mm(o_b, w["w_out_b"], tm=2048, tn=256, tk=W_B, name="outproj_b", res=x2)
    x3, ffn2_saved, _ = _ffn_fwd(x2, w["ffn2_norm"], w["ffn2_wgt"], w["ffn2_wut"], w["ffn2_wd"], "ffn2")
    dx3, g["final_norm"], loss = _final_loss(x3, w["final_norm"], target, tm=256)

    dx2, g["ffn2_norm"], g["ffn2_wgt"], g["ffn2_wut"], g["ffn2_wd"], _ = _ffn_bwd(
        dx3, x2, w["ffn2_norm"], w["ffn2_wgt"], w["ffn2_wut"], w["ffn2_wd"], ffn2_saved, "ffn2")
    do_a = _mm(dx2, w["w_out_a"], tb=True, tm=2048, tn=256, tk=D_MODEL, name="outproj_do_a")
    do_b = _mm(dx2, w["w_out_b"], tb=True, tm=2048, tn=256, tk=D_MODEL, name="outproj_do_b")
    g["w_out_a"] = _mm(o_a, dx2, ta=True, tm=W_A, tn=D_MODEL, tk=1024, name="outproj_dw_a")
    g["w_out_b"] = _mm(o_b, dx2, ta=True, tm=W_B, tn=D_MODEL, tk=1024, name="outproj_dw_b")

    (dp_h, g["lb0"], g["lb1"], g["hgrn_out_norm"]), carried = _hgrn_bwd(
        p_h, w["lb0"], w["lb1"], w["hgrn_out_norm"], hgrn_states, do_a, 0, comm=plan.comm("hgrn_bwd", g))
    plan.done("hgrn_bwd", carried, w)
    post_out = _rowwise_bwd(post_f, post_xs, post_ps, [do_b], x_grad=[True] * 5, p_grad=[True] * 3 + [False],
                            dx_groups=[[0], [1], [2], [3], [4]], dx_dtypes=[F32] * 5, tm=256, name="rwkv_post_bwd")
    dy, dr1, dk1, dv1, dgate, g["rwkv_r_k"], g["rwkv_gn_w"], g["rwkv_gn_b"] = post_out
    (dr2, dlw, dk2, dv2, da_vec, db_vec), carried = _rwkv_bwd(seqs, rwkv_states, dy, comm=plan.comm("rwkv_bwd", g))
    plan.done("rwkv_bwd", carried, w)

    def prep2_f(*vals):
        r_, lw_, k2_, v_, a_, b_, g_ = prep_f(*vals)
        return r_, lw_, k2_, v_, a_, b_, g_, r_, k2_, v_

    prep_out = _rowwise_bwd(prep2_f, prep_xs, prep_ps, [dr2, dlw, dk2, dv2, da_vec, db_vec, dgate, dr1, dk1, dv1],
                            x_grad=[True] * 8, p_grad=[True] * 11 + [False], dx_groups=[[0, 1, 2, 3], [4, 5, 6, 7]],
                            dx_dtypes=[F32, F32], tm=256, name="rwkv_prep_bwd")
    dpr_main, dpr_prev = prep_out[0], prep_out[1]
    (dmu_r, dmu_k, dmu_v, dmu_lo, g["rwkv_w0"], g["w2_pad"], g["rwkv_a0"], g["a2_pad"], g["g2_pad"],
     g["rwkv_k_k"], g["rwkv_k_a"]) = prep_out[2:]
    g["mu_pad"] = jnp.concatenate([dmu_r, dmu_k, dmu_v, dmu_lo], axis=1)
    dp_r, = _rowwise(lambda u_, s_: (u_ + s_,), [dpr_main, ("next", dpr_prev, N_RWKV_PAD, 0)], [], [[0]], [F32],
                     tm=512, name="rwkv_dp_sum")
    dhm = _mm(dp_h, w["w_in_h"], tb=True, tm=1024, tn=512, tk=N_HGRN_COLS, name="inproj_dh_h")
    dhm = _mm(dp_r, w["w_in_r"], tb=True, tm=1024, tn=512, tk=N_RWKV_PAD, name="inproj_dh_r", res=dhm)
    g["w_in_h"] = _mm(hm, dp_h, ta=True, tm=D_MODEL, tn=D_MODEL, tk=1024, name="inproj_dw_h")
    g["w_in_r"] = _mm(hm, dp_r, ta=True, tm=D_MODEL, tn=N_RWKV_PAD // 2, tk=1024, name="inproj_dw_r")
    mix_comm = plan.comm("mix_drms", g)
    mix_out = _rowwise_bwd(_rms_f, [x1], [w["mix_norm"]], [dhm], x_grad=[True], p_grad=[True], dx_groups=[[0]],
                           dx_dtypes=[F32], tm=256, name="mix_drms", extra={0: dx2}, comm=mix_comm)
    (dx1, g["mix_norm"]), carried = mix_out if mix_comm is not None else (mix_out, [])
    plan.done("mix_drms", carried, w)
    dx0, g["ffn1_norm"], g["ffn1_wgt"], g["ffn1_wut"], g["ffn1_wd"], carried = _ffn_bwd(
        dx1, x, w["ffn1_norm"], w["ffn1_wgt"], w["ffn1_wut"], w["ffn1_wd"], ffn1_saved, "ffn1",
        comm=plan.comm("ffn1_dact", g))
    plan.done("ffn1_dact", carried, w)
    return loss, dx0, g


HBM_SPEC = pl.BlockSpec(memory_space=pl.ANY)

Comm = collections.namedtuple("Comm", "arrays out_shapes aliased sem_shapes start finish")


def _run_comm(comm, name):
    n = len(comm.arrays)

    def body(*refs):
        ins, outs, sems = refs[:n], refs[n:2 * n], refs[2 * n:]
        comm.start(ins, outs, sems)
        comm.finish(ins, outs, sems)

    return pl.pallas_call(
        body, name=name, in_specs=[HBM_SPEC] * n, out_specs=[HBM_SPEC] * n, out_shape=list(comm.out_shapes),
        input_output_aliases={t: t for t in range(n)} if comm.aliased else {},
        scratch_shapes=list(comm.sem_shapes))(*comm.arrays)


def _hosting_call(body, comm, *, name, grid, in_specs, out_specs, out_shape, scratch_shapes, args):
    sem = ("arbitrary",) * len(grid)
    if comm is None:
        res = pl.pallas_call(body, name=name, grid=grid, in_specs=in_specs, out_specs=out_specs, out_shape=out_shape,
                             scratch_shapes=scratch_shapes, compiler_params=_params(sem))(*args)
        return list(res), []
    ni, no, ns, nc = len(in_specs), len(out_specs), len(scratch_shapes), len(comm.arrays)

    def wrapped(*refs):
        ins, cins = refs[:ni], refs[ni:ni + nc]
        outs, couts = refs[ni + nc:ni + nc + no], refs[ni + nc + no:ni + 2 * nc + no]
        scr, sems = refs[ni + 2 * nc + no:ni + 2 * nc + no + ns], refs[ni + 2 * nc + no + ns:]
        first = functools.reduce(jnp.logical_and, [pl.program_id(k) == 0 for k in range(len(grid))])
        last = functools.reduce(jnp.logical_and, [pl.program_id(k) == grid[k] - 1 for k in range(len(grid))])

        @pl.when(first)
        def _():
            comm.start(cins, couts, sems)

        body(*ins, *outs, *scr)

        @pl.when(last)
        def _():
            comm.finish(cins, couts, sems)

    res = pl.pallas_call(
        wrapped, name=name, grid=grid, in_specs=list(in_specs) + [HBM_SPEC] * nc,
        out_specs=list(out_specs) + [HBM_SPEC] * nc, out_shape=list(out_shape) + list(comm.out_shapes),
        scratch_shapes=list(scratch_shapes) + list(comm.sem_shapes),
        input_output_aliases={ni + t: no + t for t in range(nc)} if comm.aliased else {},
        compiler_params=_params(sem))(*args, *comm.arrays)
    return list(res[:no]), list(res[no:])


def _chips(x, y):
    return [(1 - x, y), (x, 1 - y), (1 - x, 1 - y)]


def _gather_comm(bufs):
    n = len(bufs)

    def copies(outs, sems):
        ici_send, ici_recv, d2d_send, d2d_recv = sems
        x, y, c = lax.axis_index("x"), lax.axis_index("y"), lax.axis_index("c")

        def half(t, slot, hc):
            hr = bufs[t].shape[1] // 2
            return outs[t].at[slot, pl.ds(pl.multiple_of(hc * hr, 16), hr), :]

        def ici(t, j, slot, px, py):
            return pltpu.make_async_remote_copy(src_ref=half(t, slot, c), dst_ref=half(t, slot, c),
                                                send_sem=ici_send.at[3 * t + j], recv_sem=ici_recv.at[3 * t + j],
                                                device_id=(px, py, c), device_id_type=MESH)

        def d2d(t, j, slot, hc):
            return pltpu.make_async_remote_copy(src_ref=half(t, slot, hc), dst_ref=half(t, slot, hc),
                                                send_sem=d2d_send.at[3 * t + j], recv_sem=d2d_recv.at[3 * t + j],
                                                device_id=(x, y, 1 - c), device_id_type=MESH)

        peers = [(t, j, px, py) for t in range(n) for j, (px, py) in enumerate(_chips(x, y))]
        return ici, d2d, peers, 2 * x + y, c

    def start(ins, outs, sems):
        ici, _, peers, me, _ = copies(outs, sems)
        for t, j, px, py in peers:
            ici(t, j, me, px, py).start()

    def finish(ins, outs, sems):
        ici, d2d, peers, me, c = copies(outs, sems)
        for t, j, px, py in peers:
            ici(t, j, 2 * px + py, px, py).wait_recv()
            d2d(t, j, 2 * px + py, c).start()
        for t, j, px, py in peers:
            d2d(t, j, 2 * px + py, 1 - c).wait_recv()
        for t, j, px, py in peers:
            ici(t, j, me, px, py).wait_send()
            d2d(t, j, 2 * px + py, c).wait_send()

    return Comm(list(bufs), [SDS(b.shape, b.dtype) for b in bufs], True, [pltpu.SemaphoreType.DMA((3 * n,))] * 4,
                start, finish)


def _sibling_exchange_comm(gs):
    n = len(gs)

    def copies(ins, outs, sems):
        x, y, c = lax.axis_index("x"), lax.axis_index("y"), lax.axis_index("c")
        cps = []
        for t in range(n):
            hr = gs[t].shape[1] // 2
            src = ins[t].at[:, pl.ds(pl.multiple_of((1 - c) * hr, SUBLANES), hr), :]
            cps.append(pltpu.make_async_remote_copy(src_ref=src, dst_ref=outs[t], send_sem=sems[0].at[t],
                                                    recv_sem=sems[1].at[t], device_id=(x, y, 1 - c),
                                                    device_id_type=MESH))
        return cps

    def start(ins, outs, sems):
        for cp in copies(ins, outs, sems):
            cp.start()

    def finish(ins, outs, sems):
        for cp in copies(ins, outs, sems):
            cp.wait()

    return Comm(list(gs), [SDS((N_CHIPS, g.shape[1] // 2, g.shape[2]), g.dtype) for g in gs], False,
                [pltpu.SemaphoreType.DMA((n,))] * 2, start, finish)


def _chip_exchange_comm(ss):
    n = len(ss)

    def copies(ins, outs, sems):
        x, y, c = lax.axis_index("x"), lax.axis_index("y"), lax.axis_index("c")
        me = 2 * x + y

        def copy(t, j, px, py, src_slot, dst_slot):
            return pltpu.make_async_remote_copy(src_ref=ins[t].at[src_slot], dst_ref=outs[t].at[dst_slot],
                                                send_sem=sems[0].at[3 * t + j], recv_sem=sems[1].at[3 * t + j],
                                                device_id=(px, py, c), device_id_type=MESH)

        peers = [(t, j, px, py) for t in range(n) for j, (px, py) in enumerate(_chips(x, y))]
        return copy, peers, me

    def start(ins, outs, sems):
        copy, peers, me = copies(ins, outs, sems)
        for t, j, px, py in peers:
            copy(t, j, px, py, 2 * px + py, me).start()

    def finish(ins, outs, sems):
        copy, peers, me = copies(ins, outs, sems)
        for t, j, px, py in peers:
            copy(t, j, px, py, me, 2 * px + py).wait_recv()
        for t, j, px, py in peers:
            copy(t, j, px, py, 2 * px + py, me).wait_send()

    return Comm(list(ss), [SDS(s.shape, s.dtype) for s in ss], False, [pltpu.SemaphoreType.DMA((3 * n,))] * 2,
                start, finish)


def _sibling_swap_comm(fs):
    n = len(fs)

    def copies(ins, outs, sems):
        x, y, c = lax.axis_index("x"), lax.axis_index("y"), lax.axis_index("c")
        return [pltpu.make_async_remote_copy(src_ref=ins[t], dst_ref=outs[t], send_sem=sems[0].at[t],
                                             recv_sem=sems[1].at[t], device_id=(x, y, 1 - c), device_id_type=MESH)
                for t in range(n)]

    def start(ins, outs, sems):
        for cp in copies(ins, outs, sems):
            cp.start()

    def finish(ins, outs, sems):
        for cp in copies(ins, outs, sems):
            cp.wait()

    return Comm(list(fs), [SDS(f.shape, f.dtype) for f in fs], False, [pltpu.SemaphoreType.DMA((n,))] * 2,
                start, finish)


def _row_tile(rows, cap=512):
    best = SUBLANES
    for tr in range(SUBLANES, min(rows, cap) + 1, SUBLANES):
        if rows % tr == 0:
            best = tr
    return best


def _add_halves(g4, r4, c_idx, name):
    _, hr, lanes = r4.shape
    tr = _row_tile(hr)
    nb = hr // tr

    def body(c_ref, a_ref, b_ref, o_ref):
        o_ref[...] = (a_ref[...] + b_ref[...]).astype(o_ref.dtype)

    grid_spec = pltpu.PrefetchScalarGridSpec(
        num_scalar_prefetch=1, grid=(N_CHIPS, nb),
        in_specs=[pl.BlockSpec((None, tr, lanes), lambda q, i, c_ref: (q, c_ref[0] * nb + i, 0)),
                  pl.BlockSpec((None, tr, lanes), lambda q, i, c_ref: (q, i, 0))],
        out_specs=pl.BlockSpec((None, tr, lanes), lambda q, i, c_ref: (q, i, 0)))
    return pl.pallas_call(body, name=name, grid_spec=grid_spec, out_shape=SDS(r4.shape, BF16),
                          compiler_params=_params(("parallel", "parallel")))(c_idx, g4, r4)


def _sum_chips(r4, s4, me_idx, name):
    _, rows, lanes = r4.shape
    tr = _row_tile(rows)

    def body(me_ref, a_ref, b_ref, c_ref, d_ref, own_ref, o_ref):
        own = own_ref[...].astype(F32)
        p = [jnp.where(me_ref[0] == q, own, ref[...].astype(F32)) for q, ref in enumerate((a_ref, b_ref, c_ref, d_ref))]
        o_ref[...] = ((p[0] + p[1]) + p[2]) + p[3]

    other = lambda q: (lambda i, me_ref: (jnp.where(me_ref[0] == q, (q + 1) % N_CHIPS, q), i, 0))
    grid_spec = pltpu.PrefetchScalarGridSpec(
        num_scalar_prefetch=1, grid=(rows // tr,),
        in_specs=[pl.BlockSpec((None, tr, lanes), other(q)) for q in range(N_CHIPS)]
        + [pl.BlockSpec((None, tr, lanes), lambda i, me_ref: (me_ref[0], i, 0))],
        out_specs=pl.BlockSpec((tr, lanes), lambda i, me_ref: (i, 0)))
    return pl.pallas_call(body, name=name, grid_spec=grid_spec, out_shape=SDS((rows, lanes), F32),
                          compiler_params=_params(("parallel",)))(me_idx, r4, r4, r4, r4, s4)


def _adamw(wf, g_own, g_other, mf, vf, c_idx, name):
    rows, lanes = wf.shape
    hr = rows // 2
    tr = _row_tile(hr)
    nb = hr // tr
    c1 = 1.0 / (1.0 - ADAM_B1 ** ADAM_STEP)
    c2 = 1.0 / (1.0 - ADAM_B2 ** ADAM_STEP)

    def body(c_ref, w_ref, go_ref, gx_ref, m_ref, v_ref, g_ref, d_ref, nm_ref, nv_ref):
        gv = jnp.where(pl.program_id(0) == c_ref[0], go_ref[...], gx_ref[...])
        m = ADAM_B1 * m_ref[...] + (1.0 - ADAM_B1) * gv
        v = ADAM_B2 * v_ref[...] + (1.0 - ADAM_B2) * (gv * gv)
        g_ref[...] = gv
        d_ref[...] = -ADAM_LR * ((m * c1) / (jnp.sqrt(v * c2) + ADAM_EPS) + ADAM_WD * w_ref[...])
        nm_ref[...] = m
        nv_ref[...] = v

    full = pl.BlockSpec((tr, lanes), lambda h, i, c_ref: (h * nb + i, 0))
    half = pl.BlockSpec((tr, lanes), lambda h, i, c_ref: (i, 0))
    grid_spec = pltpu.PrefetchScalarGridSpec(num_scalar_prefetch=1, grid=(2, nb),
                                             in_specs=[full, half, half, full, full], out_specs=[full] * 4)
    return pl.pallas_call(body, name=name, grid_spec=grid_spec, out_shape=[SDS((rows, lanes), F32)] * 4,
                          compiler_params=_params(("parallel", "parallel")))(c_idx, wf, g_own, g_other, mf, vf)


BIG = ("ffn1_w_gate", "ffn1_w_up", "ffn1_w_down", "ffn2_w_gate", "ffn2_w_up", "ffn2_w_down", "w_out", "w_in")
TRANSPOSED = ("ffn1_w_gate", "ffn1_w_up", "ffn2_w_gate", "ffn2_w_up")
PACKED = ("rwkv_w2", "rwkv_a2", "rwkv_g2")
SMALL_SHAPES = {"ffn1_norm": (1, D_MODEL), "mix_norm": (1, D_MODEL), "hgrn_lb_logits": (2, W_A),
                "hgrn_out_norm": (1, W_A), "rwkv_shift_mu": (1, N_RWKV_COLS), "rwkv_w0": (1, W_B),
                "rwkv_a0": (1, W_B), "rwkv_k_k": (1, W_B), "rwkv_k_a": (1, W_B),
                "rwkv_r_k": (1, HB_HEADS, HB_DIM), "rwkv_gn_w": (1, W_B), "rwkv_gn_b": (1, W_B),
                "ffn2_norm": (1, D_MODEL), "final_norm": (D_MODEL,)}
PACK_ELEMS = sum(_numel(_shard_shape(n)) for n in PACKED) + sum(_numel(SMALL_SHAPES[n]) for n in SMALL)
PACK_ROWS = -(-PACK_ELEMS // (32 * LANES)) * 32


def _to_rows(name, shard):
    return shard[0].T if name in TRANSPOSED else shard[0]


def _from_rows(name, rows):
    return (rows.T if name in TRANSPOSED else rows)[None]


def _pack(sharded, small):
    flat = jnp.concatenate([sharded[n].reshape(-1) for n in PACKED] + [small[n].reshape(-1) for n in SMALL])
    return jnp.pad(flat, (0, PACK_ROWS * LANES - flat.shape[0])).reshape(PACK_ROWS, LANES)


def _unpack(packed):
    flat, out, off = packed.reshape(-1), {}, 0
    for n in PACKED:
        shp = _shard_shape(n)
        out[n] = flat[off:off + _numel(shp)].reshape((1,) + shp)
        off += _numel(shp)
    for n in SMALL:
        shp = SMALL_SHAPES[n]
        out[n] = flat[off:off + _numel(shp)].reshape(shp)
        off += _numel(shp)
    return out


def _quarter(full, name, q):
    shape, ax = SHARDED_SHAPES[name]
    w = shape[ax] // N_CHIPS
    return lax.slice_in_dim(full, q * w, (q + 1) * w, axis=ax)


def kernel(x, ffn1_norm, ffn1_w_gate, ffn1_w_up, ffn1_w_down, mix_norm, w_in, hgrn_lb_logits, hgrn_out_norm, rwkv_shift_mu, rwkv_w0, rwkv_w2, rwkv_a0, rwkv_a2, rwkv_g2, rwkv_k_k, rwkv_k_a, rwkv_r_k, rwkv_gn_w, rwkv_gn_b, w_out, ffn2_norm, ffn2_w_gate, ffn2_w_up, ffn2_w_down, final_norm, loss_target, m_ffn1_norm, m_ffn1_w_gate, m_ffn1_w_up, m_ffn1_w_down, m_mix_norm, m_w_in, m_hgrn_lb_logits, m_hgrn_out_norm, m_rwkv_shift_mu, m_rwkv_w0, m_rwkv_w2, m_rwkv_a0, m_rwkv_a2, m_rwkv_g2, m_rwkv_k_k, m_rwkv_k_a, m_rwkv_r_k, m_rwkv_gn_w, m_rwkv_gn_b, m_w_out, m_ffn2_norm, m_ffn2_w_gate, m_ffn2_w_up, m_ffn2_w_down, m_final_norm, v_ffn1_norm, v_ffn1_w_gate, v_ffn1_w_up, v_ffn1_w_down, v_mix_norm, v_w_in, v_hgrn_lb_logits, v_hgrn_out_norm, v_rwkv_shift_mu, v_rwkv_w0, v_rwkv_w2, v_rwkv_a0, v_rwkv_a2, v_rwkv_g2, v_rwkv_k_k, v_rwkv_k_a, v_rwkv_r_k, v_rwkv_gn_w, v_rwkv_gn_b, v_w_out, v_ffn2_norm, v_ffn2_w_gate, v_ffn2_w_up, v_ffn2_w_down, v_final_norm):
    args = dict(locals())
    wts = {n: args[n] for n in ALL_WEIGHTS}
    moms = {n: args["m_" + n] for n in ALL_WEIGHTS}
    vars_ = {n: args["v_" + n] for n in ALL_WEIGHTS}

    me = 2 * lax.axis_index("x") + lax.axis_index("y")
    c_idx = lax.axis_index("c").astype(jnp.int32).reshape(1)
    me_idx = me.astype(jnp.int32).reshape(1)
    shard_of = {n: _to_rows(n, wts[n]).astype(BF16) for n in BIG}
    shard_of["packed"] = _pack(wts, {n: wts[n] for n in SMALL}).astype(BF16)
    group = {"ffn1": BIG[0:3], "ffn2": BIG[3:6], "mix": ("w_out", "w_in", "packed")}

    def slot_bufs(names):
        return [lax.dynamic_update_slice(jnp.zeros((N_CHIPS,) + shard_of[n].shape, BF16), shard_of[n][None],
                                         (me, 0, 0)) for n in names]

    def ffn_weights(tag, gathered):
        return {f"{tag}_wgt": gathered[0].reshape(D_FF, D_MODEL), f"{tag}_wut": gathered[1].reshape(D_FF, D_MODEL),
                f"{tag}_wd": gathered[2].reshape(D_FF, D_MODEL)}

    def mixer_weights(gathered):
        w_out_full = gathered[0].reshape(D_MODEL, D_MODEL)
        w_in_full = jnp.concatenate([gathered[1][q] for q in range(N_CHIPS)], axis=1)
        packs = gathered[2].reshape(N_CHIPS, PACK_ROWS * LANES)
        full, off = {}, 0
        for n in PACKED:
            shp = _shard_shape(n)
            full[n] = jnp.concatenate([packs[q, off:off + _numel(shp)].reshape(shp) for q in range(N_CHIPS)], axis=1)
            off += _numel(shp)
        zrow = lambda nrow: jnp.zeros((nrow, W_B), BF16)
        return {"w_out_a": w_out_full[:W_A], "w_out_b": w_out_full[W_A:], "w_in_h": w_in_full[:, :N_HGRN_COLS],
                "w_in_r": jnp.pad(w_in_full[:, N_HGRN_COLS:], ((0, 0), (0, N_RWKV_PAD - N_RWKV_COLS))),
                "w2_pad": jnp.concatenate([full["rwkv_w2"], zrow(LORA_PAD - 32)], axis=0),
                "a2_pad": jnp.concatenate([zrow(32), full["rwkv_a2"], zrow(LORA_PAD - 64)], axis=0),
                "g2_pad": jnp.concatenate([zrow(64), full["rwkv_g2"], zrow(LORA_PAD - 160)], axis=0)}

    plan = _Plan()
    w = ffn_weights("ffn1", _run_comm(_gather_comm(slot_bufs(group["ffn1"])), "gather_ffn1"))
    plan.comm_of["ffn1_gate_up"] = lambda g: _gather_comm(slot_bufs(group["mix"]))
    plan.after["ffn1_gate_up"] = lambda res, w_: w_.update(mixer_weights(res))
    plan.comm_of["rwkv_fwd"] = lambda g: _gather_comm(slot_bufs(group["ffn2"]))
    plan.after["rwkv_fwd"] = lambda res, w_: w_.update(ffn_weights("ffn2", res))
    w["ffn1_norm"], w["ffn2_norm"] = ffn1_norm, ffn2_norm
    w["mix_norm"] = mix_norm
    w["lb0"], w["lb1"] = hgrn_lb_logits[0:1], hgrn_lb_logits[1:2]
    w["hgrn_out_norm"] = hgrn_out_norm
    w["mu_pad"] = jnp.pad(rwkv_shift_mu, ((0, 0), (0, N_RWKV_PAD - N_RWKV_COLS)))
    for n in ("rwkv_w0", "rwkv_a0", "rwkv_k_k", "rwkv_k_a", "rwkv_gn_w", "rwkv_gn_b"):
        w[n] = wts[n]
    w["rwkv_r_k"] = rwkv_r_k.reshape(1, W_B)
    w["final_norm"] = final_norm.reshape(1, D_MODEL)

    def reduce_rows(names, gs):
        r1 = _run_comm(_sibling_exchange_comm(gs), "grad_sibling_exchange")
        s4 = [_add_halves(gt, rt, c_idx, f"grad_add_halves_{n}") for gt, rt, n in zip(gs, r1, names)]
        r2 = _run_comm(_chip_exchange_comm(s4), "grad_chip_exchange")
        return [_sum_chips(rt, st, me_idx, f"grad_sum_chips_{n}") for rt, st, n in zip(r2, s4, names)]

    early = {}

    def reduce_early(names, grads_of, sibling_host, chips_host):
        def sibling_comm(g):
            early[names, "gs"] = grads_of(g)
            return _sibling_exchange_comm(early[names, "gs"])

        def after_sibling(res, w_):
            early[names, "s4"] = [_add_halves(gt, rt, c_idx, f"grad_add_halves_{n}")
                                  for gt, rt, n in zip(early[names, "gs"], res, names)]

        def after_chips(res, w_):
            early.update(zip(names, [_sum_chips(rt, st, me_idx, f"grad_sum_chips_{n}")
                                     for rt, st, n in zip(res, early[names, "s4"], names)]))

        plan.comm_of[sibling_host], plan.after[sibling_host] = sibling_comm, after_sibling
        plan.comm_of[chips_host] = lambda g: _chip_exchange_comm(early[names, "s4"])
        plan.after[chips_host] = after_chips

    def proj_grads(g):
        g_w_in = jnp.concatenate([g["w_in_h"], g["w_in_r"][:, :N_RWKV_COLS]], axis=1)
        return [jnp.concatenate([g["w_out_a"], g["w_out_b"]], axis=0).reshape(N_CHIPS, -1, D_MODEL),
                jnp.stack([_quarter(g_w_in, "w_in", q) for q in range(N_CHIPS)])]

    reduce_early(group["ffn2"], lambda g: [g[k].reshape(N_CHIPS, -1, D_MODEL) for k in ("ffn2_wgt", "ffn2_wut", "ffn2_wd")],
                 "hgrn_bwd", "rwkv_bwd")
    reduce_early(("w_out", "w_in"), proj_grads, "mix_drms", "ffn1_dact")
    loss_slab, grad_x, g = _local_step(x[0], loss_target[0], w, plan)
    loss = lax.psum(loss_slab[0, 0], ("x", "y", "c"))

    grows = {"ffn1_w_gate": g["ffn1_wgt"], "ffn1_w_up": g["ffn1_wut"], "ffn1_w_down": g["ffn1_wd"]}
    gfull = {
        "rwkv_w2": g["w2_pad"][0:32], "rwkv_a2": g["a2_pad"][32:64], "rwkv_g2": g["g2_pad"][64:160],
    }
    gsmall = {
        "ffn1_norm": g["ffn1_norm"], "mix_norm": g["mix_norm"],
        "hgrn_lb_logits": jnp.concatenate([g["lb0"], g["lb1"]], axis=0), "hgrn_out_norm": g["hgrn_out_norm"],
        "rwkv_shift_mu": g["mu_pad"][:, :N_RWKV_COLS], "rwkv_w0": g["rwkv_w0"], "rwkv_a0": g["rwkv_a0"],
        "rwkv_k_k": g["rwkv_k_k"], "rwkv_k_a": g["rwkv_k_a"], "rwkv_r_k": g["rwkv_r_k"],
        "rwkv_gn_w": g["rwkv_gn_w"], "rwkv_gn_b": g["rwkv_gn_b"], "ffn2_norm": g["ffn2_norm"],
        "final_norm": g["final_norm"],
    }
    late_names = list(group["ffn1"]) + ["packed"]
    gs = [grows[n].reshape(N_CHIPS, -1, D_MODEL) for n in group["ffn1"]]
    gs.append(jnp.stack([_pack({n: _quarter(gfull[n], n, q) for n in PACKED}, gsmall) for q in range(N_CHIPS)]))
    own_of = dict(zip(late_names, reduce_rows(late_names, gs)))
    names = list(BIG) + ["packed"]
    own = [own_of[n] if n in own_of else early[n] for n in names]
    other = _run_comm(_sibling_swap_comm(own), "grad_sibling_swap")

    def rows_list(d):
        return [_to_rows(n, d[n]) for n in BIG] + [_pack(d, {n: d[n] for n in SMALL})]

    outs = [_adamw(wt, go, gx, mt, vt, c_idx, f"adamw_{n}")
            for wt, go, gx, mt, vt, n in zip(rows_list(wts), own, other, rows_list(moms), rows_list(vars_), names)]
    results = []
    for k in range(4):
        per = [outs[i][k] for i in range(len(names))]
        d = {n: _from_rows(n, z) for n, z in zip(BIG, per[:-1])}
        d.update(_unpack(per[-1]))
        results.append(d)
    return (loss, grad_x[None], *[r[n] for r in results for n in ALL_WEIGHTS])
```

```python
import collections
import functools

import jax
import jax.numpy as jnp
from jax import lax
from jax.experimental import pallas as pl
from jax.experimental.pallas import tpu as pltpu

F32 = jnp.float32
BF16 = jnp.bfloat16
SDS = jax.ShapeDtypeStruct
MESH = pl.DeviceIdType.MESH

D_MODEL = 1024
D_FF = 2816
W_A = 512
W_B = 512
HA_HEADS, HA_DIM = 4, 128
HB_HEADS, HB_DIM = 8, 64
HGRN_CHUNK = 64
HGRN_GROUP = 2
RWKV_CHUNK = 16
RWKV_GROUP = 4
N_HGRN_COLS = 4 * W_A
N_RWKV_COLS = 3 * W_B + 32 + 32 + 96
N_RWKV_PAD = 1792
LORA_PAD = 256
NORM_EPS = 1e-6
RWKV_GN_EPS = 64e-5
L2_EPS = 1e-12
ADAM_LR, ADAM_B1, ADAM_B2, ADAM_EPS, ADAM_WD, ADAM_STEP = 0.001, 0.9, 0.999, 1e-8, 0.01, 10

N_CHIPS = 4
VMEM_LIMIT_V7X = 56 * 1024 * 1024
LANES = 1024

SHARDED_SHAPES = {
    "ffn1_w_gate": ((D_MODEL, D_FF), 1), "ffn1_w_up": ((D_MODEL, D_FF), 1), "ffn1_w_down": ((D_FF, D_MODEL), 0),
    "w_in": ((D_MODEL, N_HGRN_COLS + N_RWKV_COLS), 1), "rwkv_w2": ((32, W_B), 1), "rwkv_a2": ((32, W_B), 1),
    "rwkv_g2": ((96, W_B), 1), "w_out": ((D_MODEL, D_MODEL), 0),
    "ffn2_w_gate": ((D_MODEL, D_FF), 1), "ffn2_w_up": ((D_MODEL, D_FF), 1), "ffn2_w_down": ((D_FF, D_MODEL), 0),
}
SMALL = ("ffn1_norm", "mix_norm", "hgrn_lb_logits", "hgrn_out_norm", "rwkv_shift_mu", "rwkv_w0", "rwkv_a0",
         "rwkv_k_k", "rwkv_k_a", "rwkv_r_k", "rwkv_gn_w", "rwkv_gn_b", "ffn2_norm", "final_norm")
ALL_WEIGHTS = ("ffn1_norm", "ffn1_w_gate", "ffn1_w_up", "ffn1_w_down", "mix_norm", "w_in", "hgrn_lb_logits",
               "hgrn_out_norm", "rwkv_shift_mu", "rwkv_w0", "rwkv_w2", "rwkv_a0", "rwkv_a2", "rwkv_g2", "rwkv_k_k",
               "rwkv_k_a", "rwkv_r_k", "rwkv_gn_w", "rwkv_gn_b", "w_out", "ffn2_norm", "ffn2_w_gate", "ffn2_w_up",
               "ffn2_w_down", "final_norm")


def _shard_shape(name):
    shape, ax = SHARDED_SHAPES[name]
    return tuple(s // N_CHIPS if i == ax else s for i, s in enumerate(shape))


def _numel(shape):
    n = 1
    for s in shape:
        n *= s
    return n


def _params(sem=None):
    return pltpu.CompilerParams(dimension_semantics=sem, vmem_limit_bytes=VMEM_LIMIT_V7X)


def _split2(x):
    hi = x.astype(BF16)
    return hi, (x.astype(F32) - hi.astype(F32)).astype(BF16)


def _dg(x, y, cx, cy, hi):
    dn = (((cx,), (cy,)), ((), ()))
    dot = lambda p, q: lax.dot_general(p, q, dn, preferred_element_type=F32)
    if hi == "x3":
        (xh, xl), (yh, yl) = _split2(x), _split2(y)
        return dot(xh, yh) + (dot(xh, yl) + dot(xl, yh))
    return dot(x.astype(BF16), y.astype(BF16))


def _make_mm(hi, cotangent_forms=None):
    @jax.custom_vjp
    def nn(x, y):
        return _dg(x, y, 1, 0, hi)

    @jax.custom_vjp
    def nt(x, y):
        return _dg(x, y, 1, 1, hi)

    @jax.custom_vjp
    def tn(x, y):
        return _dg(x, y, 0, 0, hi)

    bnn, bnt, btn = cotangent_forms or (nn, nt, tn)
    nn.defvjp(lambda x, y: (nn(x, y), (x, y)), lambda r, g: (bnt(g, r[1]), btn(r[0], g)))
    nt.defvjp(lambda x, y: (nt(x, y), (x, y)), lambda r, g: (bnn(g, r[1]), btn(g, r[0])))
    tn.defvjp(lambda x, y: (tn(x, y), (x, y)), lambda r, g: (bnt(r[1], g), bnn(r[0], g)))
    return nn, nt, tn


_nn, _nt, _tn = _make_mm(False)
_nn_x3, _nt_x3, _tn_x3 = _make_mm("x3", (_nn, _nt, _tn))


def _tri_apply(x, transpose):
    c = x.shape[0]
    tri = (lax.broadcasted_iota(jnp.int32, (c, c), 1) <= lax.broadcasted_iota(jnp.int32, (c, c), 0)).astype(BF16)
    dn = (((0 if transpose else 1,), (0,)), ((), ()))
    p1 = x.astype(BF16)
    r1 = x - p1.astype(F32)
    p2 = r1.astype(BF16)
    p3 = (r1 - p2.astype(F32)).astype(BF16)
    dot = lambda p: lax.dot_general(tri, p, dn, preferred_element_type=F32)
    return dot(p1) + (dot(p2) + dot(p3))


@jax.custom_vjp
def _cumsum_rows(x):
    return _tri_apply(x, False)


_cumsum_rows.defvjp(lambda x: (_tri_apply(x, False), None), lambda _, g: (_tri_apply(g, True),))


def _sigmoid(x):
    return 1.0 / (1.0 + jnp.exp(-x))


def _silu(x):
    return x * _sigmoid(x)


def _softplus(z):
    return jnp.maximum(z, 0.0) + jnp.log(1.0 + jnp.exp(-jnp.abs(z)))


def _mm(a, b, *, ta=False, tb=False, tm, tn, tk, name, out_dtype=F32, res=None, scale=None):
    m = a.shape[1] if ta else a.shape[0]
    kdim = a.shape[0] if ta else a.shape[1]
    n = b.shape[0] if tb else b.shape[1]
    assert (b.shape[1] if tb else b.shape[0]) == kdim
    tm, tn, tk = min(tm, m), min(tn, n), min(tk, kdim)
    assert m % tm == 0 and n % tn == 0 and kdim % tk == 0, (name, m, n, kdim)
    nk = kdim // tk
    a_spec = pl.BlockSpec((tk, tm), lambda i, j, k: (k, i)) if ta else pl.BlockSpec((tm, tk), lambda i, j, k: (i, k))
    b_spec = pl.BlockSpec((tn, tk), lambda i, j, k: (j, k)) if tb else pl.BlockSpec((tk, tn), lambda i, j, k: (k, j))
    o_spec = pl.BlockSpec((tm, tn), lambda i, j, k: (i, j))
    ca, cb = (0 if ta else 1), (1 if tb else 0)

    def body(*refs):
        if res is not None:
            a_ref, b_ref, r_ref, o_ref, acc_ref = refs
        else:
            a_ref, b_ref, o_ref, acc_ref = refs
        k = pl.program_id(2)

        @pl.when(k == 0)
        def _():
            acc_ref[...] = jnp.zeros_like(acc_ref)

        acc_ref[...] += _dg(a_ref[...], b_ref[...], ca, cb, False)

        @pl.when(k == nk - 1)
        def _():
            acc = acc_ref[...]
            if scale is not None:
                acc = acc * scale
            if res is not None:
                acc = r_ref[...] + acc
            o_ref[...] = acc.astype(out_dtype)

    in_specs = [a_spec, b_spec] + ([o_spec] if res is not None else [])
    args = (a, b) + ((res,) if res is not None else ())
    return pl.pallas_call(
        body, name=name, grid=(m // tm, n // tn, nk), in_specs=in_specs, out_specs=o_spec,
        out_shape=SDS((m, n), out_dtype), scratch_shapes=[pltpu.VMEM((tm, tn), F32)],
        compiler_params=_params(("parallel", "parallel", "arbitrary")))(*args)


def _row_spec(x, tm):
    if isinstance(x, tuple):
        arr, w, j = x
        return arr, pl.BlockSpec((tm, w), lambda i, j=j: (i, j))
    return x, pl.BlockSpec((tm, x.shape[1]), lambda i: (i, 0))


def _par_spec(p):
    if isinstance(p, tuple):
        arr, w, j = p
        return arr, pl.BlockSpec((arr.shape[0], w), lambda i, j=j: (0, j))
    return p, pl.BlockSpec(p.shape, lambda i: (0, 0))


def _store_groups(refs, groups, vals):
    for ref, idxs in zip(refs, groups):
        off = 0
        for ix in idxs:
            v = vals[ix]
            ref[:, off:off + v.shape[1]] = v.astype(ref.dtype)
            off += v.shape[1]


SUBLANES = 8


def _x_plan(xs, tm, t):
    arrays, specs, plan = [], [], []
    nb = tm // SUBLANES
    for x in xs:
        if isinstance(x, tuple) and isinstance(x[0], str):
            kind, arr, w, j = x
            if kind == "prev":
                halo = lambda i, j=j: (jnp.maximum(i * nb - 1, 0), j)
            else:
                halo = lambda i, j=j: (jnp.minimum((i + 1) * nb, t // SUBLANES - 1), j)
            arrays += [arr, arr]
            specs += [pl.BlockSpec((tm, w), lambda i, j=j: (i, j)), pl.BlockSpec((SUBLANES, w), halo)]
            plan.append((kind, 2, w))
        else:
            arr, spec = _row_spec(x, tm)
            arrays.append(arr)
            specs.append(spec)
            plan.append(("plain", 1, spec.block_shape[1]))
    return arrays, specs, plan


def _x_vals(refs, plan, tm, nt):
    vals, k = [], 0
    i = pl.program_id(0)
    rows = lax.broadcasted_iota(jnp.int32, (tm, 1), 0)
    for kind, n, _ in plan:
        main = refs[k][...].astype(F32)
        if kind == "prev":
            edge = jnp.where(i == 0, 0.0, refs[k + 1][SUBLANES - 1:SUBLANES, :].astype(F32))
            main = jnp.where(rows == 0, edge, pltpu.roll(main, 1, 0))
        elif kind == "next":
            edge = jnp.where(i == nt - 1, 0.0, refs[k + 1][0:1, :].astype(F32))
            main = jnp.where(rows == tm - 1, edge, pltpu.roll(main, tm - 1, 0))
        vals.append(main)
        k += n
    return vals


def _tile_rows(xs, tm):
    arr = xs[0]
    if isinstance(arr, tuple):
        arr = arr[1] if isinstance(arr[0], str) else arr[0]
    return min(tm, arr.shape[0]), arr.shape[0]


def _rowwise(f, xs, params, out_groups, out_dtypes, *, tm, name):
    tm, t = _tile_rows(xs, tm)
    nt = t // tm
    xa, xspecs, plan = _x_plan(xs, tm, t)
    pa, pspecs = (zip(*[_par_spec(p) for p in params]) if params else ((), ()))
    nxr, npar = len(xa), len(pa)
    x_sds = [SDS((tm, w), F32) for _, _, w in plan]
    p_sds = [SDS(s.block_shape, F32) for s in pspecs]
    outs_sds = jax.eval_shape(lambda *vals: f(*vals), *x_sds, *p_sds)
    widths = [sum(outs_sds[ix].shape[1] for ix in idxs) for idxs in out_groups]

    def body(*refs):
        vals = _x_vals(refs[:nxr], plan, tm, nt) + [r[...].astype(F32) for r in refs[nxr:nxr + npar]]
        outs = f(*vals)
        _store_groups(refs[nxr + npar:], out_groups, outs)

    return pl.pallas_call(
        body, name=name, grid=(nt,), in_specs=list(xspecs) + list(pspecs),
        out_specs=[pl.BlockSpec((tm, w), lambda i: (i, 0)) for w in widths],
        out_shape=[SDS((t, w), dt) for w, dt in zip(widths, out_dtypes)],
        compiler_params=_params(("parallel",)))(*xa, *pa)


def _rowwise_bwd(f, xs, params, cots, *, x_grad, p_grad, dx_groups, dx_dtypes, tm, name, extra=None, comm=None):
    tm, t = _tile_rows(xs, tm)
    nt = t // tm
    xa, xspecs, plan = _x_plan(xs, tm, t)
    pa, pspecs = (zip(*[_par_spec(p) for p in params]) if params else ((), ()))
    ca, cspecs = zip(*[_row_spec(c, tm) for c in cots])
    extra = extra or {}
    ekeys = sorted(extra)
    ea, especs = (zip(*[_row_spec(extra[k], tm) for k in ekeys]) if ekeys else ((), ()))
    nx, nxr, npar, nc, ne = len(plan), len(xa), len(pa), len(ca), len(ea)
    gx = [i for i in range(nx) if x_grad[i]]
    gp = [i for i in range(npar) if p_grad[i]]
    widths = [sum(plan[gx[ix]][2] for ix in idxs) for idxs in dx_groups]
    ng = len(dx_groups)

    def body(*refs):
        ins = refs[:nxr + npar + nc + ne]
        outs = refs[nxr + npar + nc + ne:]
        vals = _x_vals(ins[:nxr], plan, tm, nt) + [r[...].astype(F32) for r in ins[nxr:nxr + npar]]
        cvals = tuple(r[...].astype(F32) for r in ins[nxr + npar:nxr + npar + nc])
        evals = [r[...].astype(F32) for r in ins[nxr + npar + nc:]]
        diff_idx = gx + [nx + i for i in gp]

        def g(*dargs):
            full = list(vals)
            for ix, v in zip(diff_idx, dargs):
                full[ix] = v
            return tuple(f(*full))

        _, vjp = jax.vjp(g, *[vals[ix] for ix in diff_idx])
        grads = vjp(cvals)
        dxs = list(grads[:len(gx)])
        for k, ev in zip(ekeys, evals):
            dxs[k] = dxs[k] + ev
        _store_groups(outs[:ng], dx_groups, dxs)
        i = pl.program_id(0)
        for ref, gval in zip(outs[ng:], grads[len(gx):]):
            @pl.when(i == 0)
            def _(ref=ref):
                ref[...] = jnp.zeros_like(ref)
            ref[...] += gval

    dp_specs = [pl.BlockSpec(pspecs[i].block_shape, lambda i: (0, 0)) for i in gp]
    dp_shapes = [SDS(pspecs[i].block_shape, F32) for i in gp]
    res, carried = _hosting_call(
        body, comm, name=name, grid=(nt,), in_specs=list(xspecs) + list(pspecs) + list(cspecs) + list(especs),
        out_specs=[pl.BlockSpec((tm, w), lambda i: (i, 0)) for w in widths] + dp_specs,
        out_shape=[SDS((t, w), dt) for w, dt in zip(widths, dx_dtypes)] + dp_shapes, scratch_shapes=[],
        args=(*xa, *pa, *ca, *ea))
    return res if comm is None else (res, carried)


def _rms_f(x, g):
    return (x * lax.rsqrt(jnp.mean(x * x, axis=-1, keepdims=True) + NORM_EPS) * g,)


def _three_pieces(x):
    p1 = x.astype(BF16)
    r1 = x - p1.astype(F32)
    p2 = r1.astype(BF16)
    return p1, p2, (r1 - p2.astype(F32)).astype(BF16)


def _group_sum_impl(x, ones_bd):
    p1, p2, p3 = _three_pieces(x)
    dot = lambda p: lax.dot_general(p, ones_bd.astype(BF16), (((1,), (0,)), ((), ())), preferred_element_type=F32)
    return dot(p1) + (dot(p2) + dot(p3))


@jax.custom_vjp
def _group_sum(x, ones_bd):
    return _group_sum_impl(x, ones_bd)


_group_sum.defvjp(lambda x, o: (_group_sum_impl(x, o), o),
                  lambda o, g: (_group_sum_impl(g, o), jnp.zeros_like(o)))


def _rwkv_prep_f(r, k, v, lo, rp, kp, vp, lop, mu_r, mu_k, mu_v, mu_lo, w0, w2p, a0, a2p, g2p, k_k, k_a, ones_bd):
    r = r + mu_r * (rp - r)
    k = k + mu_k * (kp - k)
    v = v + mu_v * (vp - v)
    lo = lo + mu_lo * (lop - lo)
    w_log = -_softplus(-(w0 + _nn(jnp.tanh(lo), w2p))) - 0.5
    lw = -jnp.exp(w_log)
    a_g = _sigmoid(a0 + _nn(lo, a2p))
    g = _nn(_sigmoid(lo), g2p)
    kk = k * k_k
    kk = kk / jnp.maximum(jnp.sqrt(_group_sum(kk * kk, ones_bd)), L2_EPS)
    k2 = k * (1.0 + (a_g - 1.0) * k_a)
    return r, lw, k2, v, -kk, kk * a_g, g


def _rwkv_post_f(y, r, k2, v, g, r_k, gn_w, gn_b, ones_bd):
    inv_n = 1.0 / HB_DIM
    mean = _group_sum(y, ones_bd) * inv_n
    yc = y - mean
    var = _group_sum(yc * yc, ones_bd) * inv_n
    yn = yc * lax.rsqrt(var + RWKV_GN_EPS) * gn_w + gn_b
    bonus = _group_sum(r * k2 * r_k, ones_bd) * v
    return ((yn + bonus) * g,)


def _tri(c, strict=False):
    ii = lax.broadcasted_iota(jnp.int32, (c, c), 0)
    jj = lax.broadcasted_iota(jnp.int32, (c, c), 1)
    return (jj < ii) if strict else (jj <= ii)


def _hgrn_step(st0, q_a, f_a, i_a, g_a, l0, l1, onorm):
    nh, nj = len(q_a), len(q_a[0])
    c = q_a[0][0].shape[0]
    combos = [(j, h) for j in range(nj) for h in range(nh)]
    every = lambda fn: {q: fn(q) for q in combos}
    at_ = lambda d: (lambda q: d[q[1]][q[0]])
    qa_, fa_, ia_, ga_ = (at_(z) for z in (q_a, f_a, i_a, g_a))
    incl = _tri(c)
    rows = lax.broadcasted_iota(jnp.int32, (c, 1), 0)
    lb = []
    for h in range(nh):
        mx = jnp.maximum(l0[h], l1[h])
        e0, e1 = jnp.exp(l0[h] - mx), jnp.exp(l1[h] - mx)
        lb.append(e0 / (e0 + e1))
    forget = every(lambda q: lb[q[1]] + (1.0 - lb[q[1]]) * _sigmoid(fa_(q)))
    qs = every(lambda q: _silu(qa_(q)))
    kk = every(lambda q: 1.0 - forget[q])
    lf = every(lambda q: jnp.log(forget[q]))
    bcum = every(lambda q: _cumsum_rows(lf[q]))
    bref = every(lambda q: jnp.sum(jnp.where(rows <= c // 2, lf[q], 0.0), axis=0, keepdims=True))
    blast = every(lambda q: jnp.sum(lf[q], axis=0, keepdims=True))
    scores = every(lambda q: jnp.where(incl, _nt(qs[q] * jnp.exp(bcum[q] - bref[q]),
                                                 kk[q] * jnp.exp(bref[q] - bcum[q])), 0.0))
    intra = every(lambda q: _nn(scores[q], ia_(q)))
    qb = every(lambda q: qs[q] * jnp.exp(bcum[q]))
    upd = every(lambda q: _tn(ia_(q), kk[q] * jnp.exp(blast[q] - bcum[q])))
    dec = every(lambda q: jnp.exp(blast[q]))
    st = list(st0)
    o = {}
    for j in range(nj):
        for h in range(nh):
            o[(j, h)] = intra[(j, h)] + _nt(qb[(j, h)], st[h])
        st = [st[h] * dec[(j, h)] + upd[(j, h)] for h in range(nh)]
    out = every(lambda q: o[q] * lax.rsqrt(jnp.mean(o[q] * o[q], axis=-1, keepdims=True) + NORM_EPS)
                * onorm[q[1]] * _silu(ga_(q)))
    return [[out[(j, h)] for j in range(nj)] for h in range(nh)], st


def _hgrn_blocks(ref, nj, c):
    return [[ref[j * c:(j + 1) * c, h * HA_DIM:(h + 1) * HA_DIM] for j in range(nj)] for h in range(HA_HEADS)]


def _hgrn_cols(ref):
    return [ref[:, h * HA_DIM:(h + 1) * HA_DIM] for h in range(HA_HEADS)]


def _hgrn_fwd(p_h, l0, l1, onorm):
    t = p_h.shape[0]
    cc, nj = HGRN_CHUNK, HGRN_GROUP
    c = cc * nj
    n = t // c

    def body(q_ref, f_ref, i_ref, g_ref, l0_ref, l1_ref, on_ref, o_ref, hs_ref, st_ref):
        @pl.when(pl.program_id(0) == 0)
        def _():
            st_ref[...] = jnp.zeros_like(st_ref)

        hs_ref[0] = st_ref[...]
        o, st1 = _hgrn_step([st_ref[h] for h in range(HA_HEADS)],
                            *[_hgrn_blocks(ref, nj, cc) for ref in (q_ref, f_ref, i_ref, g_ref)],
                            _hgrn_cols(l0_ref), _hgrn_cols(l1_ref), _hgrn_cols(on_ref))
        for h in range(HA_HEADS):
            for j in range(nj):
                o_ref[j * cc:(j + 1) * cc, h * HA_DIM:(h + 1) * HA_DIM] = o[h][j]
            st_ref[h] = st1[h]

    col = lambda j: pl.BlockSpec((c, W_A), lambda i, j=j: (i, j))
    par = pl.BlockSpec((1, W_A), lambda i: (0, 0))
    return pl.pallas_call(
        body, name="hgrn_fwd", grid=(n,), in_specs=[col(0), col(1), col(2), col(3), par, par, par],
        out_specs=[pl.BlockSpec((c, W_A), lambda i: (i, 0)),
                   pl.BlockSpec((1, HA_HEADS, HA_DIM, HA_DIM), lambda i: (i, 0, 0, 0))],
        out_shape=[SDS((t, W_A), F32), SDS((n, HA_HEADS, HA_DIM, HA_DIM), F32)],
        scratch_shapes=[pltpu.VMEM((HA_HEADS, HA_DIM, HA_DIM), F32)],
        compiler_params=_params(("arbitrary",)))(p_h, p_h, p_h, p_h, l0, l1, onorm)


def _hgrn_bwd(p_h, l0, l1, onorm, hs, do, do_col, comm=None):
    t = p_h.shape[0]
    cc, nj = HGRN_CHUNK, HGRN_GROUP
    c = cc * nj
    n = t // c

    def body(q_ref, f_ref, i_ref, g_ref, l0_ref, l1_ref, on_ref, hs_ref, do_ref,
             dp_ref, dl0_ref, dl1_ref, don_ref, dst_ref):
        @pl.when(pl.program_id(0) == 0)
        def _():
            dst_ref[...] = jnp.zeros_like(dst_ref)
            dl0_ref[...] = jnp.zeros_like(dl0_ref)
            dl1_ref[...] = jnp.zeros_like(dl1_ref)
            don_ref[...] = jnp.zeros_like(don_ref)

        args = ([hs_ref[0, h] for h in range(HA_HEADS)],
                *[_hgrn_blocks(ref, nj, cc) for ref in (q_ref, f_ref, i_ref, g_ref)],
                _hgrn_cols(l0_ref), _hgrn_cols(l1_ref), _hgrn_cols(on_ref))
        _, vjp = jax.vjp(_hgrn_step, *args)
        dst0, dq, df, di, dg, dl0, dl1, don = vjp((_hgrn_blocks(do_ref, nj, cc),
                                                   [dst_ref[h] for h in range(HA_HEADS)]))
        for h in range(HA_HEADS):
            sl = slice(h * HA_DIM, (h + 1) * HA_DIM)
            for k, dv in enumerate((dq, df, di, dg)):
                for j in range(nj):
                    dp_ref[j * cc:(j + 1) * cc, k * W_A + h * HA_DIM:k * W_A + (h + 1) * HA_DIM] = dv[h][j]
            dl0_ref[:, sl] += dl0[h]
            dl1_ref[:, sl] += dl1[h]
            don_ref[:, sl] += don[h]
            dst_ref[h] = dst0[h]

    col = lambda j: pl.BlockSpec((c, W_A), lambda i, j=j: (n - 1 - i, j))
    par = pl.BlockSpec((1, W_A), lambda i: (0, 0))
    return _hosting_call(
        body, comm, name="hgrn_bwd", grid=(n,),
        in_specs=[col(0), col(1), col(2), col(3), par, par, par,
                  pl.BlockSpec((1, HA_HEADS, HA_DIM, HA_DIM), lambda i: (n - 1 - i, 0, 0, 0)),
                  pl.BlockSpec((c, W_A), lambda i: (n - 1 - i, do_col))],
        out_specs=[pl.BlockSpec((c, N_HGRN_COLS), lambda i: (n - 1 - i, 0)), par, par, par],
        out_shape=[SDS((t, N_HGRN_COLS), F32), SDS((1, W_A), F32), SDS((1, W_A), F32), SDS((1, W_A), F32)],
        scratch_shapes=[pltpu.VMEM((HA_HEADS, HA_DIM, HA_DIM), F32)],
        args=(p_h, p_h, p_h, p_h, l0, l1, onorm, hs, do))


HB_PAIRS = HB_HEADS // 2
PAIR_W = 2 * HB_DIM


def _head_lane_masks():
    lane = lax.broadcasted_iota(jnp.int32, (1, PAIR_W), 1)
    return (lane < HB_DIM).astype(F32), (lane >= HB_DIM).astype(F32)


@jax.custom_vjp
def _stack_heads(x):
    m0, m1 = _head_lane_masks()
    return jnp.concatenate([x * m0, x * m1], axis=0)


def _stack_heads_bwd(_, g):
    m0, m1 = _head_lane_masks()
    c = g.shape[0] // 2
    return (g[:c] * m0 + g[c:] * m1,)


_stack_heads.defvjp(lambda x: (_stack_heads(x), None), _stack_heads_bwd)


@jax.custom_vjp
def _unstack_heads(ys):
    c = ys.shape[0] // 2
    return ys[:c] + ys[c:]


_unstack_heads.defvjp(lambda ys: (_unstack_heads(ys), None), lambda _, g: (_stack_heads(g),))


def _same_head_block(c):
    ii = lax.broadcasted_iota(jnp.int32, (2 * c, 2 * c), 0)
    jj = lax.broadcasted_iota(jnp.int32, (2 * c, 2 * c), 1)
    same = (ii < c) == (jj < c)
    return same & (jj <= ii), same & (jj < ii), (ii == jj).astype(F32)


@jax.custom_vjp
def _rows_join(top, bottom):
    return jnp.concatenate([top, bottom], axis=0)


def _rows_join_bwd(n_top, g):
    return g[:n_top], g[n_top:]


_rows_join.defvjp(lambda top, bottom: (_rows_join(top, bottom), top.shape[0]), _rows_join_bwd)


def _rows_split_impl(x, n_top):
    return x[:n_top], x[n_top:]


_rows_split = jax.custom_vjp(_rows_split_impl, nondiff_argnums=(1,))
_rows_split.defvjp(lambda x, n_top: (_rows_split_impl(x, n_top), None),
                   lambda n_top, _, g: (jnp.concatenate([g[0], g[1]], axis=0),))


def _rwkv_step(s0, r, lw, k, v, a, b):
    npair, nj = len(r), len(r[0])
    c = r[0][0].shape[0]
    combos = [(j, p) for j in range(nj) for p in range(npair)]
    every = lambda fn: {q: fn(q) for q in combos}
    at_ = lambda d: (lambda q: d[q[1]][q[0]])
    r_, lw_, k_, v_, a_, b_ = (at_(z) for z in (r, lw, k, v, a, b))
    incl, strict, eye = _same_head_block(c)

    gam = every(lambda q: _cumsum_rows(lw_(q)))
    gtot = every(lambda q: jnp.sum(lw_(q), axis=0, keepdims=True))
    eneg = every(lambda q: jnp.exp(-gam[q]))
    edec = every(lambda q: jnp.exp(gtot[q] - gam[q]))
    at = every(lambda q: _stack_heads(a_(q) * jnp.exp(gam[q] - lw_(q))))
    rt = every(lambda q: _stack_heads(r_(q) * jnp.exp(gam[q])))
    bt = every(lambda q: _stack_heads(b_(q) * eneg[q]))
    kt = every(lambda q: _stack_heads(k_(q) * eneg[q]))
    bdec = every(lambda q: _stack_heads(b_(q) * edec[q]))
    kdec = every(lambda q: _stack_heads(k_(q) * edec[q]))
    vs = every(lambda q: _stack_heads(v_(q)))
    a_ab = every(lambda q: jnp.where(strict, _nt(at[q], bt[q]), 0.0))
    a_ak = every(lambda q: jnp.where(strict, _nt(at[q], kt[q]), 0.0))
    a_rb = every(lambda q: jnp.where(incl, _nt(rt[q], bt[q]), 0.0))
    a_rk = every(lambda q: jnp.where(incl, _nt(rt[q], kt[q]), 0.0))
    tinv = every(lambda q: eye + a_ab[q])
    pw = a_ab
    span = 2
    while span < c:
        pw = every(lambda q, pw=pw: _nn_x3(pw[q], pw[q]))
        tinv = every(lambda q, pw=pw, tinv=tinv: tinv[q] + _nn_x3(pw[q], tinv[q]))
        span *= 2
    akv = every(lambda q: _nn(a_ak[q], vs[q]))
    w1 = every(lambda q: _nn_x3(tinv[q], at[q]))
    u0 = every(lambda q: _nn_x3(tinv[q], akv[q]))
    wr = every(lambda q: _rows_join(w1[q], rt[q]))
    bk = every(lambda q: _rows_join(bdec[q], kdec[q]))
    yv = every(lambda q: _nn(a_rk[q], vs[q]))
    gdec = every(lambda q: jnp.exp(gtot[q]))

    s = list(s0)
    y = [[None] * nj for _ in range(npair)]
    for j in range(nj):
        both = {p: _rows_split(_nt(wr[(j, p)], s[p]), 2 * c) for p in range(npair)}
        u = {p: both[p][0] + u0[(j, p)] for p in range(npair)}
        for p in range(npair):
            y[p][j] = _unstack_heads(both[p][1] + _nn(a_rb[(j, p)], u[p]) + yv[(j, p)])
        s = [s[p] * gdec[(j, p)] + _tn(_rows_join(u[p], vs[(j, p)]), bk[(j, p)]) for p in range(npair)]
    return y, s


def _rwkv_blocks(ref, nj, c):
    return [[ref[j * c:(j + 1) * c, p * PAIR_W:(p + 1) * PAIR_W] for j in range(nj)] for p in range(HB_PAIRS)]


def _rwkv_fwd(seqs, comm=None):
    t = seqs[0].shape[0]
    c, nj = RWKV_CHUNK, RWKV_GROUP
    n = t // (c * nj)

    def body(r_ref, lw_ref, k_ref, v_ref, a_ref, b_ref, y_ref, hs_ref, st_ref):
        @pl.when(pl.program_id(0) == 0)
        def _():
            st_ref[...] = jnp.zeros_like(st_ref)

        hs_ref[0] = st_ref[...]
        s0 = [st_ref[p] for p in range(HB_PAIRS)]
        y, s1 = _rwkv_step(s0, *[_rwkv_blocks(ref, nj, c) for ref in (r_ref, lw_ref, k_ref, v_ref, a_ref, b_ref)])
        for p in range(HB_PAIRS):
            for j in range(nj):
                y_ref[j * c:(j + 1) * c, p * PAIR_W:(p + 1) * PAIR_W] = y[p][j]
            st_ref[p] = s1[p]

    seq = pl.BlockSpec((c * nj, W_B), lambda i: (i, 0))
    return _hosting_call(
        body, comm, name="rwkv_fwd", grid=(n,), in_specs=[seq] * 6,
        out_specs=[seq, pl.BlockSpec((1, HB_PAIRS, PAIR_W, PAIR_W), lambda i: (i, 0, 0, 0))],
        out_shape=[SDS((t, W_B), F32), SDS((n, HB_PAIRS, PAIR_W, PAIR_W), F32)],
        scratch_shapes=[pltpu.VMEM((HB_PAIRS, PAIR_W, PAIR_W), F32)], args=tuple(seqs))


def _rwkv_bwd(seqs, hs, dy, comm=None):
    t = seqs[0].shape[0]
    c, nj = RWKV_CHUNK, RWKV_GROUP
    n = t // (c * nj)

    def body(r_ref, lw_ref, k_ref, v_ref, a_ref, b_ref, hs_ref, dy_ref,
             dr_ref, dlw_ref, dk_ref, dv_ref, da_ref, db_ref, dst_ref):
        @pl.when(pl.program_id(0) == 0)
        def _():
            dst_ref[...] = jnp.zeros_like(dst_ref)

        s0 = [hs_ref[0, p] for p in range(HB_PAIRS)]
        seq_vals = [_rwkv_blocks(ref, nj, c) for ref in (r_ref, lw_ref, k_ref, v_ref, a_ref, b_ref)]
        _, vjp = jax.vjp(_rwkv_step, s0, *seq_vals)
        grads = vjp((_rwkv_blocks(dy_ref, nj, c), [dst_ref[p] for p in range(HB_PAIRS)]))
        for ref, gr in zip((dr_ref, dlw_ref, dk_ref, dv_ref, da_ref, db_ref), grads[1:]):
            for p in range(HB_PAIRS):
                for j in range(nj):
                    ref[j * c:(j + 1) * c, p * PAIR_W:(p + 1) * PAIR_W] = gr[p][j]
        m0, m1 = _head_lane_masks()
        rows0 = (lax.broadcasted_iota(jnp.int32, (PAIR_W, 1), 0) < HB_DIM).astype(F32)
        blocks = rows0 * m0 + (1.0 - rows0) * m1
        for p in range(HB_PAIRS):
            dst_ref[p] = grads[0][p] * blocks

    seq = pl.BlockSpec((c * nj, W_B), lambda i: (n - 1 - i, 0))
    return _hosting_call(
        body, comm, name="rwkv_bwd", grid=(n,),
        in_specs=[seq] * 6 + [pl.BlockSpec((1, HB_PAIRS, PAIR_W, PAIR_W), lambda i: (n - 1 - i, 0, 0, 0)), seq],
        out_specs=[seq] * 6, out_shape=[SDS((t, W_B), F32)] * 6,
        scratch_shapes=[pltpu.VMEM((HB_PAIRS, PAIR_W, PAIR_W), F32)], args=(*seqs, hs, dy))


def _final_loss(x3, fnorm, target, *, tm):
    t, d = x3.shape

    def body(x_ref, g_ref, t_ref, dx_ref, dg_ref, loss_ref):
        @pl.when(pl.program_id(0) == 0)
        def _():
            dg_ref[...] = jnp.zeros_like(dg_ref)
            loss_ref[...] = jnp.zeros_like(loss_ref)

        x, g = x_ref[...], g_ref[...]
        rinv = lax.rsqrt(jnp.mean(x * x, axis=-1, keepdims=True) + NORM_EPS)
        xh = x * rinv
        diff = xh * g - t_ref[...]
        loss_ref[...] += 0.5 * jnp.sum(jnp.mean(diff * diff, axis=-1, keepdims=True))
        dy = diff * (1.0 / d)
        dg_ref[...] += jnp.sum(dy * xh, axis=0, keepdims=True)
        dxh = dy * g
        dx_ref[...] = rinv * (dxh - xh * jnp.mean(dxh * xh, axis=-1, keepdims=True))

    row = pl.BlockSpec((tm, d), lambda i: (i, 0))
    return pl.pallas_call(
        body, name="final_loss", grid=(t // tm,), in_specs=[row, pl.BlockSpec((1, d), lambda i: (0, 0)), row],
        out_specs=[row, pl.BlockSpec((1, d), lambda i: (0, 0)), pl.BlockSpec((8, 128), lambda i: (0, 0))],
        out_shape=[SDS((t, d), F32), SDS((1, d), F32), SDS((8, 128), F32)],
        compiler_params=_params(("arbitrary",)))(x3, fnorm, target)


def _gate_up_act(h, wgt, wut, *, tm, tn, name, comm=None):
    t, d = h.shape
    tm = min(tm, t)

    def body(h_ref, g_ref, u_ref, a_out, u_out, act_out):
        hv = h_ref[...]
        a = _dg(hv, g_ref[...], 1, 1, False)
        u = _dg(hv, u_ref[...], 1, 1, False)
        a_out[...] = a.astype(a_out.dtype)
        u_out[...] = u.astype(u_out.dtype)
        act_out[...] = (_silu(a) * u).astype(act_out.dtype)

    wspec = pl.BlockSpec((tn, d), lambda i, j: (j, 0))
    ospec = pl.BlockSpec((tm, tn), lambda i, j: (i, j))
    return _hosting_call(
        body, comm, name=name, grid=(t // tm, D_FF // tn),
        in_specs=[pl.BlockSpec((tm, d), lambda i, j: (i, 0)), wspec, wspec], out_specs=[ospec, ospec, ospec],
        out_shape=[SDS((t, D_FF), BF16), SDS((t, D_FF), BF16), SDS((t, D_FF), BF16)], scratch_shapes=[],
        args=(h, wgt, wut))


def _dact_swiglu(dout, wd, a, u, *, tm, tn, name, comm=None):
    t, d = dout.shape
    tm = min(tm, t)

    def body(d_ref, w_ref, a_ref, u_ref, da_out, du_out):
        dact = 0.5 * _dg(d_ref[...], w_ref[...], 1, 1, False)
        av, uv = a_ref[...].astype(F32), u_ref[...].astype(F32)
        s = _sigmoid(av)
        da_out[...] = (dact * uv * (s * (1.0 + av * (1.0 - s)))).astype(da_out.dtype)
        du_out[...] = (dact * (av * s)).astype(du_out.dtype)

    tile = pl.BlockSpec((tm, tn), lambda i, j: (i, j))
    return _hosting_call(
        body, comm, name=name, grid=(t // tm, D_FF // tn),
        in_specs=[pl.BlockSpec((tm, d), lambda i, j: (i, 0)), pl.BlockSpec((tn, d), lambda i, j: (j, 0)), tile, tile],
        out_specs=[tile, tile], out_shape=[SDS((t, D_FF), BF16), SDS((t, D_FF), BF16)], scratch_shapes=[],
        args=(dout, wd, a, u))


class _Plan:
    def __init__(self):
        self.comm_of, self.after = {}, {}

    def comm(self, name, g):
        return self.comm_of[name](g) if name in self.comm_of else None

    def done(self, name, results, w):
        if name in self.after:
            self.after[name](results, w)


def _ffn_fwd(x, w, tag, plan, g):
    h, = _rowwise(_rms_f, [x], [w[f"{tag}_norm"]], [[0]], [BF16], tm=512, name=f"{tag}_rms")
    (a, u, act), carried = _gate_up_act(h, w[f"{tag}_wgt"], w[f"{tag}_wut"], tm=2048, tn=256, name=f"{tag}_gate_up",
                                        comm=plan.comm(f"{tag}_gate_up", g))
    plan.done(f"{tag}_gate_up", carried, w)
    out = _mm(act, w[f"{tag}_wd"], tm=1024, tn=512, tk=D_FF, name=f"{tag}_down", res=x, scale=0.5)
    return out, (h, a, u, act)


def _ffn_bwd(dout, x, norm, wgt, wut, wd, saved, tag, comm=None):
    h, a, u, act = saved
    (da, du), carried = _dact_swiglu(dout, wd, a, u, tm=2048, tn=256, name=f"{tag}_dact", comm=comm)
    dwd = _mm(act, dout, ta=True, tm=D_FF // 2, tn=D_MODEL, tk=1024, name=f"{tag}_dwd", scale=0.5)
    dwgt = _mm(da, h, ta=True, tm=D_FF // 2, tn=D_MODEL, tk=1024, name=f"{tag}_dwg")
    dwut = _mm(du, h, ta=True, tm=D_FF // 2, tn=D_MODEL, tk=1024, name=f"{tag}_dwu")
    dh = _mm(da, wgt, tm=1024, tn=512, tk=D_FF, name=f"{tag}_dh_g")
    dh = _mm(du, wut, tm=1024, tn=512, tk=D_FF, name=f"{tag}_dh_u", res=dh)
    dx, dnorm = _rowwise_bwd(_rms_f, [x], [norm], [dh], x_grad=[True], p_grad=[True], dx_groups=[[0]],
                             dx_dtypes=[F32], tm=512, name=f"{tag}_drms", extra={0: dout})
    return dx, dnorm, dwgt, dwut, dwd, carried


def _local_step(x, target, w, plan=None):
    plan = plan or _Plan()
    ones_bd = jnp.kron(jnp.eye(HB_HEADS, dtype=F32), jnp.ones((HB_DIM, HB_DIM), F32))
    g = {}
    x1, ffn1_saved = _ffn_fwd(x, w, "ffn1", plan, g)
    hm, = _rowwise(_rms_f, [x1], [w["mix_norm"]], [[0]], [BF16], tm=512, name="mix_rms")
    p_h = _mm(hm, w["w_in_h"], tm=2048, tn=256, tk=D_MODEL, name="inproj_h")
    p_r = _mm(hm, w["w_in_r"], tm=2048, tn=256, tk=D_MODEL, name="inproj_r")
    o_a, hgrn_states = _hgrn_fwd(p_h, w["lb0"], w["lb1"], w["hgrn_out_norm"])

    mu = w["mu_pad"]
    prep_xs = [(p_r, W_B, 0), (p_r, W_B, 1), (p_r, W_B, 2), (p_r, LORA_PAD, 6),
               ("prev", p_r, W_B, 0), ("prev", p_r, W_B, 1), ("prev", p_r, W_B, 2), ("prev", p_r, LORA_PAD, 6)]
    prep_ps = [(mu, W_B, 0), (mu, W_B, 1), (mu, W_B, 2), (mu, LORA_PAD, 6), w["rwkv_w0"], w["w2_pad"], w["rwkv_a0"],
               w["a2_pad"], w["g2_pad"], w["rwkv_k_k"], w["rwkv_k_a"], ones_bd]
    prep_f = _rwkv_prep_f
    r, lw, k2, v, a_vec, b_vec, gate = _rowwise(prep_f, prep_xs, prep_ps, [[0], [1], [2], [3], [4], [5], [6]],
                                                [F32] * 7, tm=256, name="rwkv_prep")
    seqs = [r, lw, k2, v, a_vec, b_vec]
    (y, rwkv_states), carried = _rwkv_fwd(seqs, comm=plan.comm("rwkv_fwd", g))
    plan.done("rwkv_fwd", carried, w)
    post_f = _rwkv_post_f
    post_xs = [y, r, k2, v, gate]
    post_ps = [w["rwkv_r_k"], w["rwkv_gn_w"], w["rwkv_gn_b"], ones_bd]
    o_b, = _rowwise(post_f, post_xs, post_ps, [[0]], [F32], tm=256, name="rwkv_post")
    x2 = _mm(o_a, w["w_out_a"], tm=2048, tn=256, tk=W_A, name="outproj_a", res=x1)
    x2 = _mm(o_b, w["w_out_b"], tm=2048, tn=256, tk=W_B, name="outproj_b", res=x2)
    x3, ffn2_saved = _ffn_fwd(x2, w, "ffn2", plan, g)
    dx3, g["final_norm"], loss = _final_loss(x3, w["final_norm"], target, tm=256)

    dx2, g["ffn2_norm"], g["ffn2_wgt"], g["ffn2_wut"], g["ffn2_wd"], _ = _ffn_bwd(
        dx3, x2, w["ffn2_norm"], w["ffn2_wgt"], w["ffn2_wut"], w["ffn2_wd"], ffn2_saved, "ffn2")
    do_a = _mm(dx2, w["w_out_a"], tb=True, tm=2048, tn=256, tk=D_MODEL, name="outproj_do_a")
    do_b = _mm(dx2, w["w_out_b"], tb=True, tm=2048, tn=256, tk=D_MODEL, name="outproj_do_b")
    g["w_out_a"] = _mm(o_a, dx2, ta=True, tm=W_A, tn=D_MODEL, tk=1024, name="outproj_dw_a")
    g["w_out_b"] = _mm(o_b, dx2, ta=True, tm=W_B, tn=D_MODEL, tk=1024, name="outproj_dw_b")

    (dp_h, g["lb0"], g["lb1"], g["hgrn_out_norm"]), carried = _hgrn_bwd(
        p_h, w["lb0"], w["lb1"], w["hgrn_out_norm"], hgrn_states, do_a, 0, comm=plan.comm("hgrn_bwd", g))
    plan.done("hgrn_bwd", carried, w)
    post_out = _rowwise_bwd(post_f, post_xs, post_ps, [do_b], x_grad=[True] * 5, p_grad=[True] * 3 + [False],
                            dx_groups=[[0], [1], [2], [3], [4]], dx_dtypes=[F32] * 5, tm=256, name="rwkv_post_bwd")
    dy, dr1, dk1, dv1, dgate, g["rwkv_r_k"], g["rwkv_gn_w"], g["rwkv_gn_b"] = post_out
    (dr2, dlw, dk2, dv2, da_vec, db_vec), carried = _rwkv_bwd(seqs, rwkv_states, dy, comm=plan.comm("rwkv_bwd", g))
    plan.done("rwkv_bwd", carried, w)

    def prep2_f(*vals):
        r_, lw_, k2_, v_, a_, b_, g_ = prep_f(*vals)
        return r_, lw_, k2_, v_, a_, b_, g_, r_, k2_, v_

    prep_out = _rowwise_bwd(prep2_f, prep_xs, prep_ps, [dr2, dlw, dk2, dv2, da_vec, db_vec, dgate, dr1, dk1, dv1],
                            x_grad=[True] * 8, p_grad=[True] * 11 + [False], dx_groups=[[0, 1, 2, 3], [4, 5, 6, 7]],
                            dx_dtypes=[F32, F32], tm=256, name="rwkv_prep_bwd")
    dpr_main, dpr_prev = prep_out[0], prep_out[1]
    (dmu_r, dmu_k, dmu_v, dmu_lo, g["rwkv_w0"], g["w2_pad"], g["rwkv_a0"], g["a2_pad"], g["g2_pad"],
     g["rwkv_k_k"], g["rwkv_k_a"]) = prep_out[2:]
    g["mu_pad"] = jnp.concatenate([dmu_r, dmu_k, dmu_v, dmu_lo], axis=1)
    dp_r, = _rowwise(lambda u_, s_: (u_ + s_,), [dpr_main, ("next", dpr_prev, N_RWKV_PAD, 0)], [], [[0]], [F32],
                     tm=512, name="rwkv_dp_sum")
    dhm = _mm(dp_h, w["w_in_h"], tb=True, tm=1024, tn=512, tk=N_HGRN_COLS, name="inproj_dh_h")
    dhm = _mm(dp_r, w["w_in_r"], tb=True, tm=1024, tn=512, tk=N_RWKV_PAD, name="inproj_dh_r", res=dhm)
    g["w_in_h"] = _mm(hm, dp_h, ta=True, tm=D_MODEL, tn=D_MODEL, tk=1024, name="inproj_dw_h")
    g["w_in_r"] = _mm(hm, dp_r, ta=True, tm=D_MODEL, tn=N_RWKV_PAD // 2, tk=1024, name="inproj_dw_r")
    mix_comm = plan.comm("mix_drms", g)
    mix_out = _rowwise_bwd(_rms_f, [x1], [w["mix_norm"]], [dhm], x_grad=[True], p_grad=[True], dx_groups=[[0]],
                           dx_dtypes=[F32], tm=512, name="mix_drms", extra={0: dx2}, comm=mix_comm)
    (dx1, g["mix_norm"]), carried = mix_out if mix_comm is not None else (mix_out, [])
    plan.done("mix_drms", carried, w)
    dx0, g["ffn1_norm"], g["ffn1_wgt"], g["ffn1_wut"], g["ffn1_wd"], carried = _ffn_bwd(
        dx1, x, w["ffn1_norm"], w["ffn1_wgt"], w["ffn1_wut"], w["ffn1_wd"], ffn1_saved, "ffn1",
        comm=plan.comm("ffn1_dact", g))
    plan.done("ffn1_dact", carried, w)
    return loss, dx0, g


HBM_SPEC = pl.BlockSpec(memory_space=pl.ANY)

Comm = collections.namedtuple("Comm", "arrays out_shapes aliased sem_shapes start finish")


def _run_comm(comm, name):
    n = len(comm.arrays)

    def body(*refs):
        ins, outs, sems = refs[:n], refs[n:2 * n], refs[2 * n:]
        comm.start(ins, outs, sems)
        comm.finish(ins, outs, sems)

    return pl.pallas_call(
        body, name=name, in_specs=[HBM_SPEC] * n, out_specs=[HBM_SPEC] * n, out_shape=list(comm.out_shapes),
        input_output_aliases={t: t for t in range(n)} if comm.aliased else {},
        scratch_shapes=list(comm.sem_shapes))(*comm.arrays)


def _hosting_call(body, comm, *, name, grid, in_specs, out_specs, out_shape, scratch_shapes, args):
    sem = ("arbitrary",) * len(grid)
    if comm is None:
        res = pl.pallas_call(body, name=name, grid=grid, in_specs=in_specs, out_specs=out_specs, out_shape=out_shape,
                             scratch_shapes=scratch_shapes, compiler_params=_params(sem))(*args)
        return list(res), []
    ni, no, ns, nc = len(in_specs), len(out_specs), len(scratch_shapes), len(comm.arrays)

    def wrapped(*refs):
        ins, cins = refs[:ni], refs[ni:ni + nc]
        outs, couts = refs[ni + nc:ni + nc + no], refs[ni + nc + no:ni + 2 * nc + no]
        scr, sems = refs[ni + 2 * nc + no:ni + 2 * nc + no + ns], refs[ni + 2 * nc + no + ns:]
        first = functools.reduce(jnp.logical_and, [pl.program_id(k) == 0 for k in range(len(grid))])
        last = functools.reduce(jnp.logical_and, [pl.program_id(k) == grid[k] - 1 for k in range(len(grid))])

        @pl.when(first)
        def _():
            comm.start(cins, couts, sems)

        body(*ins, *outs, *scr)

        @pl.when(last)
        def _():
            comm.finish(cins, couts, sems)

    res = pl.pallas_call(
        wrapped, name=name, grid=grid, in_specs=list(in_specs) + [HBM_SPEC] * nc,
        out_specs=list(out_specs) + [HBM_SPEC] * nc, out_shape=list(out_shape) + list(comm.out_shapes),
        scratch_shapes=list(scratch_shapes) + list(comm.sem_shapes),
        input_output_aliases={ni + t: no + t for t in range(nc)} if comm.aliased else {},
        compiler_params=_params(sem))(*args, *comm.arrays)
    return list(res[:no]), list(res[no:])


def _chips(x, y):
    return [(1 - x, y), (x, 1 - y), (1 - x, 1 - y)]


def _gather_comm(bufs):
    n = len(bufs)

    def copies(outs, sems):
        ici_send, ici_recv, d2d_send, d2d_recv = sems
        x, y, c = lax.axis_index("x"), lax.axis_index("y"), lax.axis_index("c")

        def half(t, slot, hc):
            hr = bufs[t].shape[1] // 2
            return outs[t].at[slot, pl.ds(pl.multiple_of(hc * hr, 16), hr), :]

        def ici(t, j, slot, px, py):
            return pltpu.make_async_remote_copy(src_ref=half(t, slot, c), dst_ref=half(t, slot, c),
                                                send_sem=ici_send.at[3 * t + j], recv_sem=ici_recv.at[3 * t + j],
                                                device_id=(px, py, c), device_id_type=MESH)

        def d2d(t, j, slot, hc):
            return pltpu.make_async_remote_copy(src_ref=half(t, slot, hc), dst_ref=half(t, slot, hc),
                                                send_sem=d2d_send.at[3 * t + j], recv_sem=d2d_recv.at[3 * t + j],
                                                device_id=(x, y, 1 - c), device_id_type=MESH)

        peers = [(t, j, px, py) for t in range(n) for j, (px, py) in enumerate(_chips(x, y))]
        return ici, d2d, peers, 2 * x + y, c

    def start(ins, outs, sems):
        ici, _, peers, me, _ = copies(outs, sems)
        for t, j, px, py in peers:
            ici(t, j, me, px, py).start()

    def finish(ins, outs, sems):
        ici, d2d, peers, me, c = copies(outs, sems)
        for t, j, px, py in peers:
            ici(t, j, 2 * px + py, px, py).wait_recv()
            d2d(t, j, 2 * px + py, c).start()
        for t, j, px, py in peers:
            d2d(t, j, 2 * px + py, 1 - c).wait_recv()
        for t, j, px, py in peers:
            ici(t, j, me, px, py).wait_send()
            d2d(t, j, 2 * px + py, c).wait_send()

    return Comm(list(bufs), [SDS(b.shape, b.dtype) for b in bufs], True, [pltpu.SemaphoreType.DMA((3 * n,))] * 4,
                start, finish)


def _sibling_exchange_comm(gs):
    n = len(gs)

    def copies(ins, outs, sems):
        x, y, c = lax.axis_index("x"), lax.axis_index("y"), lax.axis_index("c")
        cps = []
        for t in range(n):
            hr = gs[t].shape[1] // 2
            src = ins[t].at[:, pl.ds(pl.multiple_of((1 - c) * hr, SUBLANES), hr), :]
            cps.append(pltpu.make_async_remote_copy(src_ref=src, dst_ref=outs[t], send_sem=sems[0].at[t],
                                                    recv_sem=sems[1].at[t], device_id=(x, y, 1 - c),
                                                    device_id_type=MESH))
        return cps

    def start(ins, outs, sems):
        for cp in copies(ins, outs, sems):
            cp.start()

    def finish(ins, outs, sems):
        for cp in copies(ins, outs, sems):
            cp.wait()

    return Comm(list(gs), [SDS((N_CHIPS, g.shape[1] // 2, g.shape[2]), g.dtype) for g in gs], False,
                [pltpu.SemaphoreType.DMA((n,))] * 2, start, finish)


def _chip_exchange_comm(ss):
    n = len(ss)

    def copies(ins, outs, sems):
        x, y, c = lax.axis_index("x"), lax.axis_index("y"), lax.axis_index("c")
        me = 2 * x + y

        def copy(t, j, px, py, src_slot, dst_slot):
            return pltpu.make_async_remote_copy(src_ref=ins[t].at[src_slot], dst_ref=outs[t].at[dst_slot],
                                                send_sem=sems[0].at[3 * t + j], recv_sem=sems[1].at[3 * t + j],
                                                device_id=(px, py, c), device_id_type=MESH)

        peers = [(t, j, px, py) for t in range(n) for j, (px, py) in enumerate(_chips(x, y))]
        return copy, peers, me

    def start(ins, outs, sems):
        copy, peers, me = copies(ins, outs, sems)
        for t, j, px, py in peers:
            copy(t, j, px, py, 2 * px + py, me).start()

    def finish(ins, outs, sems):
        copy, peers, me = copies(ins, outs, sems)
        for t, j, px, py in peers:
            copy(t, j, px, py, me, 2 * px + py).wait_recv()
        for t, j, px, py in peers:
            copy(t, j, px, py, 2 * px + py, me).wait_send()

    return Comm(list(ss), [SDS(s.shape, s.dtype) for s in ss], False, [pltpu.SemaphoreType.DMA((3 * n,))] * 2,
                start, finish)


def _sibling_swap_comm(fs):
    n = len(fs)

    def copies(ins, outs, sems):
        x, y, c = lax.axis_index("x"), lax.axis_index("y"), lax.axis_index("c")
        return [pltpu.make_async_remote_copy(src_ref=ins[t], dst_ref=outs[t], send_sem=sems[0].at[t],
                                             recv_sem=sems[1].at[t], device_id=(x, y, 1 - c), device_id_type=MESH)
                for t in range(n)]

    def start(ins, outs, sems):
        for cp in copies(ins, outs, sems):
            cp.start()

    def finish(ins, outs, sems):
        for cp in copies(ins, outs, sems):
            cp.wait()

    return Comm(list(fs), [SDS(f.shape, f.dtype) for f in fs], False, [pltpu.SemaphoreType.DMA((n,))] * 2,
                start, finish)


def _row_tile(rows, cap=512):
    best = SUBLANES
    for tr in range(SUBLANES, min(rows, cap) + 1, SUBLANES):
        if rows % tr == 0:
            best = tr
    return best


def _add_halves(g4, r4, c_idx, name):
    _, hr, lanes = r4.shape
    tr = _row_tile(hr)
    nb = hr // tr

    def body(c_ref, a_ref, b_ref, o_ref):
        o_ref[...] = (a_ref[...] + b_ref[...]).astype(o_ref.dtype)

    grid_spec = pltpu.PrefetchScalarGridSpec(
        num_scalar_prefetch=1, grid=(N_CHIPS, nb),
        in_specs=[pl.BlockSpec((None, tr, lanes), lambda q, i, c_ref: (q, c_ref[0] * nb + i, 0)),
                  pl.BlockSpec((None, tr, lanes), lambda q, i, c_ref: (q, i, 0))],
        out_specs=pl.BlockSpec((None, tr, lanes), lambda q, i, c_ref: (q, i, 0)))
    return pl.pallas_call(body, name=name, grid_spec=grid_spec, out_shape=SDS(r4.shape, BF16),
                          compiler_params=_params(("parallel", "parallel")))(c_idx, g4, r4)


def _sum_chips(r4, s4, me_idx, name):
    _, rows, lanes = r4.shape
    tr = _row_tile(rows)

    def body(me_ref, a_ref, b_ref, c_ref, d_ref, own_ref, o_ref):
        own = own_ref[...].astype(F32)
        p = [jnp.where(me_ref[0] == q, own, ref[...].astype(F32)) for q, ref in enumerate((a_ref, b_ref, c_ref, d_ref))]
        o_ref[...] = ((p[0] + p[1]) + p[2]) + p[3]

    other = lambda q: (lambda i, me_ref: (jnp.where(me_ref[0] == q, (q + 1) % N_CHIPS, q), i, 0))
    grid_spec = pltpu.PrefetchScalarGridSpec(
        num_scalar_prefetch=1, grid=(rows // tr,),
        in_specs=[pl.BlockSpec((None, tr, lanes), other(q)) for q in range(N_CHIPS)]
        + [pl.BlockSpec((None, tr, lanes), lambda i, me_ref: (me_ref[0], i, 0))],
        out_specs=pl.BlockSpec((tr, lanes), lambda i, me_ref: (i, 0)))
    return pl.pallas_call(body, name=name, grid_spec=grid_spec, out_shape=SDS((rows, lanes), F32),
                          compiler_params=_params(("parallel",)))(me_idx, r4, r4, r4, r4, s4)


def _adamw(wf, g_own, g_other, mf, vf, c_idx, name):
    rows, lanes = wf.shape
    hr = rows // 2
    tr = _row_tile(hr)
    nb = hr // tr
    c1 = 1.0 / (1.0 - ADAM_B1 ** ADAM_STEP)
    c2 = 1.0 / (1.0 - ADAM_B2 ** ADAM_STEP)

    def body(c_ref, w_ref, go_ref, gx_ref, m_ref, v_ref, g_ref, d_ref, nm_ref, nv_ref):
        gv = jnp.where(pl.program_id(0) == c_ref[0], go_ref[...], gx_ref[...])
        m = ADAM_B1 * m_ref[...] + (1.0 - ADAM_B1) * gv
        v = ADAM_B2 * v_ref[...] + (1.0 - ADAM_B2) * (gv * gv)
        g_ref[...] = gv
        d_ref[...] = -ADAM_LR * ((m * c1) / (jnp.sqrt(v * c2) + ADAM_EPS) + ADAM_WD * w_ref[...])
        nm_ref[...] = m
        nv_ref[...] = v

    full = pl.BlockSpec((tr, lanes), lambda h, i, c_ref: (h * nb + i, 0))
    half = pl.BlockSpec((tr, lanes), lambda h, i, c_ref: (i, 0))
    grid_spec = pltpu.PrefetchScalarGridSpec(num_scalar_prefetch=1, grid=(2, nb),
                                             in_specs=[full, half, half, full, full], out_specs=[full] * 4)
    return pl.pallas_call(body, name=name, grid_spec=grid_spec, out_shape=[SDS((rows, lanes), F32)] * 4,
                          compiler_params=_params(("parallel", "parallel")))(c_idx, wf, g_own, g_other, mf, vf)


BIG = ("ffn1_w_gate", "ffn1_w_up", "ffn1_w_down", "ffn2_w_gate", "ffn2_w_up", "ffn2_w_down", "w_out", "w_in")
TRANSPOSED = ("ffn1_w_gate", "ffn1_w_up", "ffn2_w_gate", "ffn2_w_up")
PACKED = ("rwkv_w2", "rwkv_a2", "rwkv_g2")
SMALL_SHAPES = {"ffn1_norm": (1, D_MODEL), "mix_norm": (1, D_MODEL), "hgrn_lb_logits": (2, W_A),
                "hgrn_out_norm": (1, W_A), "rwkv_shift_mu": (1, N_RWKV_COLS), "rwkv_w0": (1, W_B),
                "rwkv_a0": (1, W_B), "rwkv_k_k": (1, W_B), "rwkv_k_a": (1, W_B),
                "rwkv_r_k": (1, HB_HEADS, HB_DIM), "rwkv_gn_w": (1, W_B), "rwkv_gn_b": (1, W_B),
                "ffn2_norm": (1, D_MODEL), "final_norm": (D_MODEL,)}
PACK_ELEMS = sum(_numel(_shard_shape(n)) for n in PACKED) + sum(_numel(SMALL_SHAPES[n]) for n in SMALL)
PACK_ROWS = -(-PACK_ELEMS // (32 * LANES)) * 32


def _to_rows(name, shard):
    return shard[0].T if name in TRANSPOSED else shard[0]


def _from_rows(name, rows):
    return (rows.T if name in TRANSPOSED else rows)[None]


def _pack(sharded, small):
    flat = jnp.concatenate([sharded[n].reshape(-1) for n in PACKED] + [small[n].reshape(-1) for n in SMALL])
    return jnp.pad(flat, (0, PACK_ROWS * LANES - flat.shape[0])).reshape(PACK_ROWS, LANES)


def _unpack(packed):
    flat, out, off = packed.reshape(-1), {}, 0
    for n in PACKED:
        shp = _shard_shape(n)
        out[n] = flat[off:off + _numel(shp)].reshape((1,) + shp)
        off += _numel(shp)
    for n in SMALL:
        shp = SMALL_SHAPES[n]
        out[n] = flat[off:off + _numel(shp)].reshape(shp)
        off += _numel(shp)
    return out


def _quarter(full, name, q):
    shape, ax = SHARDED_SHAPES[name]
    w = shape[ax] // N_CHIPS
    return lax.slice_in_dim(full, q * w, (q + 1) * w, axis=ax)


def kernel(x, ffn1_norm, ffn1_w_gate, ffn1_w_up, ffn1_w_down, mix_norm, w_in, hgrn_lb_logits, hgrn_out_norm, rwkv_shift_mu, rwkv_w0, rwkv_w2, rwkv_a0, rwkv_a2, rwkv_g2, rwkv_k_k, rwkv_k_a, rwkv_r_k, rwkv_gn_w, rwkv_gn_b, w_out, ffn2_norm, ffn2_w_gate, ffn2_w_up, ffn2_w_down, final_norm, loss_target, m_ffn1_norm, m_ffn1_w_gate, m_ffn1_w_up, m_ffn1_w_down, m_mix_norm, m_w_in, m_hgrn_lb_logits, m_hgrn_out_norm, m_rwkv_shift_mu, m_rwkv_w0, m_rwkv_w2, m_rwkv_a0, m_rwkv_a2, m_rwkv_g2, m_rwkv_k_k, m_rwkv_k_a, m_rwkv_r_k, m_rwkv_gn_w, m_rwkv_gn_b, m_w_out, m_ffn2_norm, m_ffn2_w_gate, m_ffn2_w_up, m_ffn2_w_down, m_final_norm, v_ffn1_norm, v_ffn1_w_gate, v_ffn1_w_up, v_ffn1_w_down, v_mix_norm, v_w_in, v_hgrn_lb_logits, v_hgrn_out_norm, v_rwkv_shift_mu, v_rwkv_w0, v_rwkv_w2, v_rwkv_a0, v_rwkv_a2, v_rwkv_g2, v_rwkv_k_k, v_rwkv_k_a, v_rwkv_r_k, v_rwkv_gn_w, v_rwkv_gn_b, v_w_out, v_ffn2_norm, v_ffn2_w_gate, v_ffn2_w_up, v_ffn2_w_down, v_final_norm):
    args = dict(locals())
    wts = {n: args[n] for n in ALL_WEIGHTS}
    moms = {n: args["m_" + n] for n in ALL_WEIGHTS}
    vars_ = {n: args["v_" + n] for n in ALL_WEIGHTS}

    me = 2 * lax.axis_index("x") + lax.axis_index("y")
    c_idx = lax.axis_index("c").astype(jnp.int32).reshape(1)
    me_idx = me.astype(jnp.int32).reshape(1)
    shard_of = {n: _to_rows(n, wts[n]).astype(BF16) for n in BIG}
    shard_of["packed"] = _pack(wts, {n: wts[n] for n in SMALL}).astype(BF16)
    group = {"ffn1": BIG[0:3], "ffn2": BIG[3:6], "mix": ("w_out", "w_in", "packed")}

    def slot_bufs(names):
        return [lax.dynamic_update_slice(jnp.zeros((N_CHIPS,) + shard_of[n].shape, BF16), shard_of[n][None],
                                         (me, 0, 0)) for n in names]

    def ffn_weights(tag, gathered):
        return {f"{tag}_wgt": gathered[0].reshape(D_FF, D_MODEL), f"{tag}_wut": gathered[1].reshape(D_FF, D_MODEL),
                f"{tag}_wd": gathered[2].reshape(D_FF, D_MODEL)}

    def mixer_weights(gathered):
        w_out_full = gathered[0].reshape(D_MODEL, D_MODEL)
        w_in_full = jnp.concatenate([gathered[1][q] for q in range(N_CHIPS)], axis=1)
        packs = gathered[2].reshape(N_CHIPS, PACK_ROWS * LANES)
        full, off = {}, 0
        for n in PACKED:
            shp = _shard_shape(n)
            full[n] = jnp.concatenate([packs[q, off:off + _numel(shp)].reshape(shp) for q in range(N_CHIPS)], axis=1)
            off += _numel(shp)
        zrow = lambda nrow: jnp.zeros((nrow, W_B), BF16)
        return {"w_out_a": w_out_full[:W_A], "w_out_b": w_out_full[W_A:], "w_in_h": w_in_full[:, :N_HGRN_COLS],
                "w_in_r": jnp.pad(w_in_full[:, N_HGRN_COLS:], ((0, 0), (0, N_RWKV_PAD - N_RWKV_COLS))),
                "w2_pad": jnp.concatenate([full["rwkv_w2"], zrow(LORA_PAD - 32)], axis=0),
                "a2_pad": jnp.concatenate([zrow(32), full["rwkv_a2"], zrow(LORA_PAD - 64)], axis=0),
                "g2_pad": jnp.concatenate([zrow(64), full["rwkv_g2"], zrow(LORA_PAD - 160)], axis=0)}

    plan = _Plan()
    first = _run_comm(_gather_comm(slot_bufs(group["ffn1"][:2])), "gather_ffn1")
    w = {"ffn1_wgt": first[0].reshape(D_FF, D_MODEL), "ffn1_wut": first[1].reshape(D_FF, D_MODEL)}

    def after_gate_up(res, w_):
        w_["ffn1_wd"] = res[0].reshape(D_FF, D_MODEL)
        w_.update(mixer_weights(res[1:]))

    plan.comm_of["ffn1_gate_up"] = lambda g: _gather_comm(slot_bufs(group["ffn1"][2:] + group["mix"]))
    plan.after["ffn1_gate_up"] = after_gate_up
    plan.comm_of["rwkv_fwd"] = lambda g: _gather_comm(slot_bufs(group["ffn2"]))
    plan.after["rwkv_fwd"] = lambda res, w_: w_.update(ffn_weights("ffn2", res))
    w["ffn1_norm"], w["ffn2_norm"] = ffn1_norm, ffn2_norm
    w["mix_norm"] = mix_norm
    w["lb0"], w["lb1"] = hgrn_lb_logits[0:1], hgrn_lb_logits[1:2]
    w["hgrn_out_norm"] = hgrn_out_norm
    w["mu_pad"] = jnp.pad(rwkv_shift_mu, ((0, 0), (0, N_RWKV_PAD - N_RWKV_COLS)))
    for n in ("rwkv_w0", "rwkv_a0", "rwkv_k_k", "rwkv_k_a", "rwkv_gn_w", "rwkv_gn_b"):
        w[n] = wts[n]
    w["rwkv_r_k"] = rwkv_r_k.reshape(1, W_B)
    w["final_norm"] = final_norm.reshape(1, D_MODEL)

    def reduce_rows(names, gs):
        r1 = _run_comm(_sibling_exchange_comm(gs), "grad_sibling_exchange")
        s4 = [_add_halves(gt, rt, c_idx, f"grad_add_halves_{n}") for gt, rt, n in zip(gs, r1, names)]
        r2 = _run_comm(_chip_exchange_comm(s4), "grad_chip_exchange")
        return [_sum_chips(rt, st, me_idx, f"grad_sum_chips_{n}") for rt, st, n in zip(r2, s4, names)]

    early = {}

    def reduce_early(names, grads_of, sibling_host, chips_host):
        def sibling_comm(g):
            early[names, "gs"] = grads_of(g)
            return _sibling_exchange_comm(early[names, "gs"])

        def after_sibling(res, w_):
            early[names, "s4"] = [_add_halves(gt, rt, c_idx, f"grad_add_halves_{n}")
                                  for gt, rt, n in zip(early[names, "gs"], res, names)]

        def after_chips(res, w_):
            early.update(zip(names, [_sum_chips(rt, st, me_idx, f"grad_sum_chips_{n}")
                                     for rt, st, n in zip(res, early[names, "s4"], names)]))

        plan.comm_of[sibling_host], plan.after[sibling_host] = sibling_comm, after_sibling
        plan.comm_of[chips_host] = lambda g: _chip_exchange_comm(early[names, "s4"])
        plan.after[chips_host] = after_chips

    def proj_grads(g):
        g_w_in = jnp.concatenate([g["w_in_h"], g["w_in_r"][:, :N_RWKV_COLS]], axis=1)
        return [jnp.concatenate([g["w_out_a"], g["w_out_b"]], axis=0).reshape(N_CHIPS, -1, D_MODEL),
                jnp.stack([_quarter(g_w_in, "w_in", q) for q in range(N_CHIPS)])]

    reduce_early(group["ffn2"], lambda g: [g[k].reshape(N_CHIPS, -1, D_MODEL) for k in ("ffn2_wgt", "ffn2_wut", "ffn2_wd")],
                 "hgrn_bwd", "rwkv_bwd")
    reduce_early(("w_out", "w_in"), proj_grads, "mix_drms", "ffn1_dact")
    loss_slab, grad_x, g = _local_step(x[0], loss_target[0], w, plan)
    loss = lax.psum(loss_slab[0, 0], ("x", "y", "c"))

    grows = {"ffn1_w_gate": g["ffn1_wgt"], "ffn1_w_up": g["ffn1_wut"], "ffn1_w_down": g["ffn1_wd"]}
    gfull = {
        "rwkv_w2": g["w2_pad"][0:32], "rwkv_a2": g["a2_pad"][32:64], "rwkv_g2": g["g2_pad"][64:160],
    }
    gsmall = {
        "ffn1_norm": g["ffn1_norm"], "mix_norm": g["mix_norm"],
        "hgrn_lb_logits": jnp.concatenate([g["lb0"], g["lb1"]], axis=0), "hgrn_out_norm": g["hgrn_out_norm"],
        "rwkv_shift_mu": g["mu_pad"][:, :N_RWKV_COLS], "rwkv_w0": g["rwkv_w0"], "rwkv_a0": g["rwkv_a0"],
        "rwkv_k_k": g["rwkv_k_k"], "rwkv_k_a": g["rwkv_k_a"], "rwkv_r_k": g["rwkv_r_k"],
        "rwkv_gn_w": g["rwkv_gn_w"], "rwkv_gn_b": g["rwkv_gn_b"], "ffn2_norm": g["ffn2_norm"],
        "final_norm": g["final_norm"],
    }
    late_names = list(group["ffn1"]) + ["packed"]
    gs = [grows[n].reshape(N_CHIPS, -1, D_MODEL) for n in group["ffn1"]]
    gs.append(jnp.stack([_pack({n: _quarter(gfull[n], n, q) for n in PACKED}, gsmall) for q in range(N_CHIPS)]))
    own_of = dict(zip(late_names, reduce_rows(late_names, gs)))
    names = list(BIG) + ["packed"]
    own = [own_of[n] if n in own_of else early[n] for n in names]
    other = _run_comm(_sibling_swap_comm(own), "grad_sibling_swap")

    def rows_list(d):
        return [_to_rows(n, d[n]) for n in BIG] + [_pack(d, {n: d[n] for n in SMALL})]

    outs = [_adamw(wt, go, gx, mt, vt, c_idx, f"adamw_{n}")
            for wt, go, gx, mt, vt, n in zip(rows_list(wts), own, other, rows_list(moms), rows_list(vars_), names)]
    results = []
    for k in range(4):
        per = [outs[i][k] for i in range(len(names))]
        d = {n: _from_rows(n, z) for n, z in zip(BIG, per[:-1])}
        d.update(_unpack(per[-1]))
        results.append(d)
    return (loss, grad_x[None], *[r[n] for r in results for n in ALL_WEIGHTS])
```

```python
import collections
import functools

import jax
import jax.numpy as jnp
from jax import lax
from jax.experimental import pallas as pl
from jax.experimental.pallas import tpu as pltpu

F32 = jnp.float32
BF16 = jnp.bfloat16
SDS = jax.ShapeDtypeStruct
MESH = pl.DeviceIdType.MESH

D_MODEL = 1024
D_FF = 2816
W_A = 512
W_B = 512
HA_HEADS, HA_DIM = 4, 128
HB_HEADS, HB_DIM = 8, 64
HGRN_CHUNK = 64
HGRN_GROUP = 2
RWKV_CHUNK = 16
RWKV_GROUP = 4
N_HGRN_COLS = 4 * W_A
N_RWKV_COLS = 3 * W_B + 32 + 32 + 96
N_RWKV_PAD = 1792
LORA_PAD = 256
NORM_EPS = 1e-6
RWKV_GN_EPS = 64e-5
L2_EPS = 1e-12
ADAM_LR, ADAM_B1, ADAM_B2, ADAM_EPS, ADAM_WD, ADAM_STEP = 0.001, 0.9, 0.999, 1e-8, 0.01, 10

N_CHIPS = 4
VMEM_LIMIT_V7X = 56 * 1024 * 1024
LANES = 1024

SHARDED_SHAPES = {
    "ffn1_w_gate": ((D_MODEL, D_FF), 1), "ffn1_w_up": ((D_MODEL, D_FF), 1), "ffn1_w_down": ((D_FF, D_MODEL), 0),
    "w_in": ((D_MODEL, N_HGRN_COLS + N_RWKV_COLS), 1), "rwkv_w2": ((32, W_B), 1), "rwkv_a2": ((32, W_B), 1),
    "rwkv_g2": ((96, W_B), 1), "w_out": ((D_MODEL, D_MODEL), 0),
    "ffn2_w_gate": ((D_MODEL, D_FF), 1), "ffn2_w_up": ((D_MODEL, D_FF), 1), "ffn2_w_down": ((D_FF, D_MODEL), 0),
}
SMALL = ("ffn1_norm", "mix_norm", "hgrn_lb_logits", "hgrn_out_norm", "rwkv_shift_mu", "rwkv_w0", "rwkv_a0",
         "rwkv_k_k", "rwkv_k_a", "rwkv_r_k", "rwkv_gn_w", "rwkv_gn_b", "ffn2_norm", "final_norm")
ALL_WEIGHTS = ("ffn1_norm", "ffn1_w_gate", "ffn1_w_up", "ffn1_w_down", "mix_norm", "w_in", "hgrn_lb_logits",
               "hgrn_out_norm", "rwkv_shift_mu", "rwkv_w0", "rwkv_w2", "rwkv_a0", "rwkv_a2", "rwkv_g2", "rwkv_k_k",
               "rwkv_k_a", "rwkv_r_k", "rwkv_gn_w", "rwkv_gn_b", "w_out", "ffn2_norm", "ffn2_w_gate", "ffn2_w_up",
               "ffn2_w_down", "final_norm")


def _shard_shape(name):
    shape, ax = SHARDED_SHAPES[name]
    return tuple(s // N_CHIPS if i == ax else s for i, s in enumerate(shape))


def _numel(shape):
    n = 1
    for s in shape:
        n *= s
    return n


def _params(sem=None):
    return pltpu.CompilerParams(dimension_semantics=sem, vmem_limit_bytes=VMEM_LIMIT_V7X)


def _split2(x):
    hi = x.astype(BF16)
    return hi, (x.astype(F32) - hi.astype(F32)).astype(BF16)


def _dg(x, y, cx, cy, hi):
    dn = (((cx,), (cy,)), ((), ()))
    dot = lambda p, q: lax.dot_general(p, q, dn, preferred_element_type=F32)
    if hi == "x3":
        (xh, xl), (yh, yl) = _split2(x), _split2(y)
        return dot(xh, yh) + (dot(xh, yl) + dot(xl, yh))
    return dot(x.astype(BF16), y.astype(BF16))


def _make_mm(hi, cotangent_forms=None):
    @jax.custom_vjp
    def nn(x, y):
        return _dg(x, y, 1, 0, hi)

    @jax.custom_vjp
    def nt(x, y):
        return _dg(x, y, 1, 1, hi)

    @jax.custom_vjp
    def tn(x, y):
        return _dg(x, y, 0, 0, hi)

    bnn, bnt, btn = cotangent_forms or (nn, nt, tn)
    nn.defvjp(lambda x, y: (nn(x, y), (x, y)), lambda r, g: (bnt(g, r[1]), btn(r[0], g)))
    nt.defvjp(lambda x, y: (nt(x, y), (x, y)), lambda r, g: (bnn(g, r[1]), btn(g, r[0])))
    tn.defvjp(lambda x, y: (tn(x, y), (x, y)), lambda r, g: (bnt(r[1], g), bnn(r[0], g)))
    return nn, nt, tn


_nn, _nt, _tn = _make_mm(False)
_nn_x3, _nt_x3, _tn_x3 = _make_mm("x3", (_nn, _nt, _tn))


def _tri_apply(x, transpose):
    c = x.shape[0]
    tri = (lax.broadcasted_iota(jnp.int32, (c, c), 1) <= lax.broadcasted_iota(jnp.int32, (c, c), 0)).astype(BF16)
    dn = (((0 if transpose else 1,), (0,)), ((), ()))
    p1 = x.astype(BF16)
    r1 = x - p1.astype(F32)
    p2 = r1.astype(BF16)
    p3 = (r1 - p2.astype(F32)).astype(BF16)
    dot = lambda p: lax.dot_general(tri, p, dn, preferred_element_type=F32)
    return dot(p1) + (dot(p2) + dot(p3))


@jax.custom_vjp
def _cumsum_rows(x):
    return _tri_apply(x, False)


_cumsum_rows.defvjp(lambda x: (_tri_apply(x, False), None), lambda _, g: (_tri_apply(g, True),))


def _sigmoid(x):
    return 1.0 / (1.0 + jnp.exp(-x))


def _silu(x):
    return x * _sigmoid(x)


def _softplus(z):
    return jnp.maximum(z, 0.0) + jnp.log(1.0 + jnp.exp(-jnp.abs(z)))


def _mm(a, b, *, ta=False, tb=False, tm, tn, tk, name, out_dtype=F32, res=None, scale=None, comm=None):
    m = a.shape[1] if ta else a.shape[0]
    kdim = a.shape[0] if ta else a.shape[1]
    n = b.shape[0] if tb else b.shape[1]
    assert (b.shape[1] if tb else b.shape[0]) == kdim
    tm, tn, tk = min(tm, m), min(tn, n), min(tk, kdim)
    assert m % tm == 0 and n % tn == 0 and kdim % tk == 0, (name, m, n, kdim)
    nk = kdim // tk
    a_spec = pl.BlockSpec((tk, tm), lambda i, j, k: (k, i)) if ta else pl.BlockSpec((tm, tk), lambda i, j, k: (i, k))
    b_spec = pl.BlockSpec((tn, tk), lambda i, j, k: (j, k)) if tb else pl.BlockSpec((tk, tn), lambda i, j, k: (k, j))
    o_spec = pl.BlockSpec((tm, tn), lambda i, j, k: (i, j))
    ca, cb = (0 if ta else 1), (1 if tb else 0)

    def body(*refs):
        if res is not None:
            a_ref, b_ref, r_ref, o_ref, acc_ref = refs
        else:
            a_ref, b_ref, o_ref, acc_ref = refs
        k = pl.program_id(2)

        @pl.when(k == 0)
        def _():
            acc_ref[...] = jnp.zeros_like(acc_ref)

        acc_ref[...] += _dg(a_ref[...], b_ref[...], ca, cb, False)

        @pl.when(k == nk - 1)
        def _():
            acc = acc_ref[...]
            if scale is not None:
                acc = acc * scale
            if res is not None:
                acc = r_ref[...] + acc
            o_ref[...] = acc.astype(out_dtype)

    in_specs = [a_spec, b_spec] + ([o_spec] if res is not None else [])
    args = (a, b) + ((res,) if res is not None else ())
    if comm is None:
        return pl.pallas_call(
            body, name=name, grid=(m // tm, n // tn, nk), in_specs=in_specs, out_specs=o_spec,
            out_shape=SDS((m, n), out_dtype), scratch_shapes=[pltpu.VMEM((tm, tn), F32)],
            compiler_params=_params(("parallel", "parallel", "arbitrary")))(*args)
    (out,), carried = _hosting_call(
        body, comm, name=name, grid=(m // tm, n // tn, nk), in_specs=in_specs, out_specs=[o_spec],
        out_shape=[SDS((m, n), out_dtype)], scratch_shapes=[pltpu.VMEM((tm, tn), F32)], args=args)
    return out, carried


def _row_spec(x, tm):
    if isinstance(x, tuple):
        arr, w, j = x
        return arr, pl.BlockSpec((tm, w), lambda i, j=j: (i, j))
    return x, pl.BlockSpec((tm, x.shape[1]), lambda i: (i, 0))


def _par_spec(p):
    if isinstance(p, tuple):
        arr, w, j = p
        return arr, pl.BlockSpec((arr.shape[0], w), lambda i, j=j: (0, j))
    return p, pl.BlockSpec(p.shape, lambda i: (0, 0))


def _store_groups(refs, groups, vals):
    for ref, idxs in zip(refs, groups):
        off = 0
        for ix in idxs:
            v = vals[ix]
            ref[:, off:off + v.shape[1]] = v.astype(ref.dtype)
            off += v.shape[1]


SUBLANES = 8


def _x_plan(xs, tm, t):
    arrays, specs, plan = [], [], []
    nb = tm // SUBLANES
    for x in xs:
        if isinstance(x, tuple) and isinstance(x[0], str):
            kind, arr, w, j = x
            if kind == "prev":
                halo = lambda i, j=j: (jnp.maximum(i * nb - 1, 0), j)
            else:
                halo = lambda i, j=j: (jnp.minimum((i + 1) * nb, t // SUBLANES - 1), j)
            arrays += [arr, arr]
            specs += [pl.BlockSpec((tm, w), lambda i, j=j: (i, j)), pl.BlockSpec((SUBLANES, w), halo)]
            plan.append((kind, 2, w))
        else:
            arr, spec = _row_spec(x, tm)
            arrays.append(arr)
            specs.append(spec)
            plan.append(("plain", 1, spec.block_shape[1]))
    return arrays, specs, plan


def _x_vals(refs, plan, tm, nt):
    vals, k = [], 0
    i = pl.program_id(0)
    rows = lax.broadcasted_iota(jnp.int32, (tm, 1), 0)
    for kind, n, _ in plan:
        main = refs[k][...].astype(F32)
        if kind == "prev":
            edge = jnp.where(i == 0, 0.0, refs[k + 1][SUBLANES - 1:SUBLANES, :].astype(F32))
            main = jnp.where(rows == 0, edge, pltpu.roll(main, 1, 0))
        elif kind == "next":
            edge = jnp.where(i == nt - 1, 0.0, refs[k + 1][0:1, :].astype(F32))
            main = jnp.where(rows == tm - 1, edge, pltpu.roll(main, tm - 1, 0))
        vals.append(main)
        k += n
    return vals


def _tile_rows(xs, tm):
    arr = xs[0]
    if isinstance(arr, tuple):
        arr = arr[1] if isinstance(arr[0], str) else arr[0]
    return min(tm, arr.shape[0]), arr.shape[0]


def _rowwise(f, xs, params, out_groups, out_dtypes, *, tm, name):
    tm, t = _tile_rows(xs, tm)
    nt = t // tm
    xa, xspecs, plan = _x_plan(xs, tm, t)
    pa, pspecs = (zip(*[_par_spec(p) for p in params]) if params else ((), ()))
    nxr, npar = len(xa), len(pa)
    x_sds = [SDS((tm, w), F32) for _, _, w in plan]
    p_sds = [SDS(s.block_shape, F32) for s in pspecs]
    outs_sds = jax.eval_shape(lambda *vals: f(*vals), *x_sds, *p_sds)
    widths = [sum(outs_sds[ix].shape[1] for ix in idxs) for idxs in out_groups]

    def body(*refs):
        vals = _x_vals(refs[:nxr], plan, tm, nt) + [r[...].astype(F32) for r in refs[nxr:nxr + npar]]
        outs = f(*vals)
        _store_groups(refs[nxr + npar:], out_groups, outs)

    return pl.pallas_call(
        body, name=name, grid=(nt,), in_specs=list(xspecs) + list(pspecs),
        out_specs=[pl.BlockSpec((tm, w), lambda i: (i, 0)) for w in widths],
        out_shape=[SDS((t, w), dt) for w, dt in zip(widths, out_dtypes)],
        compiler_params=_params(("parallel",)))(*xa, *pa)


def _rowwise_bwd(f, xs, params, cots, *, x_grad, p_grad, dx_groups, dx_dtypes, tm, name, extra=None, comm=None):
    tm, t = _tile_rows(xs, tm)
    nt = t // tm
    xa, xspecs, plan = _x_plan(xs, tm, t)
    pa, pspecs = (zip(*[_par_spec(p) for p in params]) if params else ((), ()))
    ca, cspecs = zip(*[_row_spec(c, tm) for c in cots])
    extra = extra or {}
    ekeys = sorted(extra)
    ea, especs = (zip(*[_row_spec(extra[k], tm) for k in ekeys]) if ekeys else ((), ()))
    nx, nxr, npar, nc, ne = len(plan), len(xa), len(pa), len(ca), len(ea)
    gx = [i for i in range(nx) if x_grad[i]]
    gp = [i for i in range(npar) if p_grad[i]]
    widths = [sum(plan[gx[ix]][2] for ix in idxs) for idxs in dx_groups]
    ng = len(dx_groups)

    def body(*refs):
        ins = refs[:nxr + npar + nc + ne]
        outs = refs[nxr + npar + nc + ne:]
        vals = _x_vals(ins[:nxr], plan, tm, nt) + [r[...].astype(F32) for r in ins[nxr:nxr + npar]]
        cvals = tuple(r[...].astype(F32) for r in ins[nxr + npar:nxr + npar + nc])
        evals = [r[...].astype(F32) for r in ins[nxr + npar + nc:]]
        diff_idx = gx + [nx + i for i in gp]

        def g(*dargs):
            full = list(vals)
            for ix, v in zip(diff_idx, dargs):
                full[ix] = v
            return tuple(f(*full))

        _, vjp = jax.vjp(g, *[vals[ix] for ix in diff_idx])
        grads = vjp(cvals)
        dxs = list(grads[:len(gx)])
        for k, ev in zip(ekeys, evals):
            dxs[k] = dxs[k] + ev
        _store_groups(outs[:ng], dx_groups, dxs)
        i = pl.program_id(0)
        for ref, gval in zip(outs[ng:], grads[len(gx):]):
            @pl.when(i == 0)
            def _(ref=ref):
                ref[...] = jnp.zeros_like(ref)
            ref[...] += gval

    dp_specs = [pl.BlockSpec(pspecs[i].block_shape, lambda i: (0, 0)) for i in gp]
    dp_shapes = [SDS(pspecs[i].block_shape, F32) for i in gp]
    res, carried = _hosting_call(
        body, comm, name=name, grid=(nt,), in_specs=list(xspecs) + list(pspecs) + list(cspecs) + list(especs),
        out_specs=[pl.BlockSpec((tm, w), lambda i: (i, 0)) for w in widths] + dp_specs,
        out_shape=[SDS((t, w), dt) for w, dt in zip(widths, dx_dtypes)] + dp_shapes, scratch_shapes=[],
        args=(*xa, *pa, *ca, *ea))
    return res if comm is None else (res, carried)


def _rms_f(x, g):
    return (x * lax.rsqrt(jnp.mean(x * x, axis=-1, keepdims=True) + NORM_EPS) * g,)


def _three_pieces(x):
    p1 = x.astype(BF16)
    r1 = x - p1.astype(F32)
    p2 = r1.astype(BF16)
    return p1, p2, (r1 - p2.astype(F32)).astype(BF16)


def _group_sum_impl(x, ones_bd):
    p1, p2, p3 = _three_pieces(x)
    dot = lambda p: lax.dot_general(p, ones_bd.astype(BF16), (((1,), (0,)), ((), ())), preferred_element_type=F32)
    return dot(p1) + (dot(p2) + dot(p3))


@jax.custom_vjp
def _group_sum(x, ones_bd):
    return _group_sum_impl(x, ones_bd)


_group_sum.defvjp(lambda x, o: (_group_sum_impl(x, o), o),
                  lambda o, g: (_group_sum_impl(g, o), jnp.zeros_like(o)))


def _rwkv_prep_f(r, k, v, lo, rp, kp, vp, lop, mu_r, mu_k, mu_v, mu_lo, w0, w2p, a0, a2p, g2p, k_k, k_a, ones_bd):
    r = r + mu_r * (rp - r)
    k = k + mu_k * (kp - k)
    v = v + mu_v * (vp - v)
    lo = lo + mu_lo * (lop - lo)
    w_log = -_softplus(-(w0 + _nn(jnp.tanh(lo), w2p))) - 0.5
    lw = -jnp.exp(w_log)
    a_g = _sigmoid(a0 + _nn(lo, a2p))
    g = _nn(_sigmoid(lo), g2p)
    kk = k * k_k
    kk = kk / jnp.maximum(jnp.sqrt(_group_sum(kk * kk, ones_bd)), L2_EPS)
    k2 = k * (1.0 + (a_g - 1.0) * k_a)
    return r, lw, k2, v, -kk, kk * a_g, g


def _rwkv_post_f(y, r, k2, v, g, r_k, gn_w, gn_b, ones_bd):
    inv_n = 1.0 / HB_DIM
    mean = _group_sum(y, ones_bd) * inv_n
    yc = y - mean
    var = _group_sum(yc * yc, ones_bd) * inv_n
    yn = yc * lax.rsqrt(var + RWKV_GN_EPS) * gn_w + gn_b
    bonus = _group_sum(r * k2 * r_k, ones_bd) * v
    return ((yn + bonus) * g,)


def _tri(c, strict=False):
    ii = lax.broadcasted_iota(jnp.int32, (c, c), 0)
    jj = lax.broadcasted_iota(jnp.int32, (c, c), 1)
    return (jj < ii) if strict else (jj <= ii)


def _hgrn_step(st0, q_a, f_a, i_a, g_a, l0, l1, onorm):
    nh, nj = len(q_a), len(q_a[0])
    c = q_a[0][0].shape[0]
    combos = [(j, h) for j in range(nj) for h in range(nh)]
    every = lambda fn: {q: fn(q) for q in combos}
    at_ = lambda d: (lambda q: d[q[1]][q[0]])
    qa_, fa_, ia_, ga_ = (at_(z) for z in (q_a, f_a, i_a, g_a))
    incl = _tri(c)
    rows = lax.broadcasted_iota(jnp.int32, (c, 1), 0)
    lb = []
    for h in range(nh):
        mx = jnp.maximum(l0[h], l1[h])
        e0, e1 = jnp.exp(l0[h] - mx), jnp.exp(l1[h] - mx)
        lb.append(e0 / (e0 + e1))
    forget = every(lambda q: lb[q[1]] + (1.0 - lb[q[1]]) * _sigmoid(fa_(q)))
    qs = every(lambda q: _silu(qa_(q)))
    kk = every(lambda q: 1.0 - forget[q])
    lf = every(lambda q: jnp.log(forget[q]))
    bcum = every(lambda q: _cumsum_rows(lf[q]))
    bref = every(lambda q: jnp.sum(jnp.where(rows <= c // 2, lf[q], 0.0), axis=0, keepdims=True))
    blast = every(lambda q: jnp.sum(lf[q], axis=0, keepdims=True))
    scores = every(lambda q: jnp.where(incl, _nt(qs[q] * jnp.exp(bcum[q] - bref[q]),
                                                 kk[q] * jnp.exp(bref[q] - bcum[q])), 0.0))
    intra = every(lambda q: _nn(scores[q], ia_(q)))
    qb = every(lambda q: qs[q] * jnp.exp(bcum[q]))
    upd = every(lambda q: _tn(ia_(q), kk[q] * jnp.exp(blast[q] - bcum[q])))
    dec = every(lambda q: jnp.exp(blast[q]))
    st = list(st0)
    o = {}
    for j in range(nj):
        for h in range(nh):
            o[(j, h)] = intra[(j, h)] + _nt(qb[(j, h)], st[h])
        st = [st[h] * dec[(j, h)] + upd[(j, h)] for h in range(nh)]
    out = every(lambda q: o[q] * lax.rsqrt(jnp.mean(o[q] * o[q], axis=-1, keepdims=True) + NORM_EPS)
                * onorm[q[1]] * _silu(ga_(q)))
    return [[out[(j, h)] for j in range(nj)] for h in range(nh)], st


def _hgrn_blocks(ref, nj, c):
    return [[ref[j * c:(j + 1) * c, h * HA_DIM:(h + 1) * HA_DIM] for j in range(nj)] for h in range(HA_HEADS)]


def _hgrn_cols(ref):
    return [ref[:, h * HA_DIM:(h + 1) * HA_DIM] for h in range(HA_HEADS)]


def _hgrn_fwd(p_h, l0, l1, onorm):
    t = p_h.shape[0]
    cc, nj = HGRN_CHUNK, HGRN_GROUP
    c = cc * nj
    n = t // c

    def body(q_ref, f_ref, i_ref, g_ref, l0_ref, l1_ref, on_ref, o_ref, hs_ref, st_ref):
        @pl.when(pl.program_id(0) == 0)
        def _():
            st_ref[...] = jnp.zeros_like(st_ref)

        hs_ref[0] = st_ref[...]
        o, st1 = _hgrn_step([st_ref[h] for h in range(HA_HEADS)],
                            *[_hgrn_blocks(ref, nj, cc) for ref in (q_ref, f_ref, i_ref, g_ref)],
                            _hgrn_cols(l0_ref), _hgrn_cols(l1_ref), _hgrn_cols(on_ref))
        for h in range(HA_HEADS):
            for j in range(nj):
                o_ref[j * cc:(j + 1) * cc, h * HA_DIM:(h + 1) * HA_DIM] = o[h][j]
            st_ref[h] = st1[h]

    col = lambda j: pl.BlockSpec((c, W_A), lambda i, j=j: (i, j))
    par = pl.BlockSpec((1, W_A), lambda i: (0, 0))
    return pl.pallas_call(
        body, name="hgrn_fwd", grid=(n,), in_specs=[col(0), col(1), col(2), col(3), par, par, par],
        out_specs=[pl.BlockSpec((c, W_A), lambda i: (i, 0)),
                   pl.BlockSpec((1, HA_HEADS, HA_DIM, HA_DIM), lambda i: (i, 0, 0, 0))],
        out_shape=[SDS((t, W_A), F32), SDS((n, HA_HEADS, HA_DIM, HA_DIM), F32)],
        scratch_shapes=[pltpu.VMEM((HA_HEADS, HA_DIM, HA_DIM), F32)],
        compiler_params=_params(("arbitrary",)))(p_h, p_h, p_h, p_h, l0, l1, onorm)


def _hgrn_bwd(p_h, l0, l1, onorm, hs, do, do_col, comm=None):
    t = p_h.shape[0]
    cc, nj = HGRN_CHUNK, HGRN_GROUP
    c = cc * nj
    n = t // c

    def body(q_ref, f_ref, i_ref, g_ref, l0_ref, l1_ref, on_ref, hs_ref, do_ref,
             dp_ref, dl0_ref, dl1_ref, don_ref, dst_ref):
        @pl.when(pl.program_id(0) == 0)
        def _():
            dst_ref[...] = jnp.zeros_like(dst_ref)
            dl0_ref[...] = jnp.zeros_like(dl0_ref)
            dl1_ref[...] = jnp.zeros_like(dl1_ref)
            don_ref[...] = jnp.zeros_like(don_ref)

        args = ([hs_ref[0, h] for h in range(HA_HEADS)],
                *[_hgrn_blocks(ref, nj, cc) for ref in (q_ref, f_ref, i_ref, g_ref)],
                _hgrn_cols(l0_ref), _hgrn_cols(l1_ref), _hgrn_cols(on_ref))
        _, vjp = jax.vjp(_hgrn_step, *args)
        dst0, dq, df, di, dg, dl0, dl1, don = vjp((_hgrn_blocks(do_ref, nj, cc),
                                                   [dst_ref[h] for h in range(HA_HEADS)]))
        for h in range(HA_HEADS):
            sl = slice(h * HA_DIM, (h + 1) * HA_DIM)
            for k, dv in enumerate((dq, df, di, dg)):
                for j in range(nj):
                    dp_ref[j * cc:(j + 1) * cc, k * W_A + h * HA_DIM:k * W_A + (h + 1) * HA_DIM] = dv[h][j]
            dl0_ref[:, sl] += dl0[h]
            dl1_ref[:, sl] += dl1[h]
            don_ref[:, sl] += don[h]
            dst_ref[h] = dst0[h]

    col = lambda j: pl.BlockSpec((c, W_A), lambda i, j=j: (n - 1 - i, j))
    par = pl.BlockSpec((1, W_A), lambda i: (0, 0))
    return _hosting_call(
        body, comm, name="hgrn_bwd", grid=(n,),
        in_specs=[col(0), col(1), col(2), col(3), par, par, par,
                  pl.BlockSpec((1, HA_HEADS, HA_DIM, HA_DIM), lambda i: (n - 1 - i, 0, 0, 0)),
                  pl.BlockSpec((c, W_A), lambda i: (n - 1 - i, do_col))],
        out_specs=[pl.BlockSpec((c, N_HGRN_COLS), lambda i: (n - 1 - i, 0)), par, par, par],
        out_shape=[SDS((t, N_HGRN_COLS), F32), SDS((1, W_A), F32), SDS((1, W_A), F32), SDS((1, W_A), F32)],
        scratch_shapes=[pltpu.VMEM((HA_HEADS, HA_DIM, HA_DIM), F32)],
        args=(p_h, p_h, p_h, p_h, l0, l1, onorm, hs, do))


HB_PAIRS = HB_HEADS // 2
PAIR_W = 2 * HB_DIM


def _head_lane_masks():
    lane = lax.broadcasted_iota(jnp.int32, (1, PAIR_W), 1)
    return (lane < HB_DIM).astype(F32), (lane >= HB_DIM).astype(F32)


@jax.custom_vjp
def _stack_heads(x):
    m0, m1 = _head_lane_masks()
    return jnp.concatenate([x * m0, x * m1], axis=0)


def _stack_heads_bwd(_, g):
    m0, m1 = _head_lane_masks()
    c = g.shape[0] // 2
    return (g[:c] * m0 + g[c:] * m1,)


_stack_heads.defvjp(lambda x: (_stack_heads(x), None), _stack_heads_bwd)


@jax.custom_vjp
def _unstack_heads(ys):
    c = ys.shape[0] // 2
    return ys[:c] + ys[c:]


_unstack_heads.defvjp(lambda ys: (_unstack_heads(ys), None), lambda _, g: (_stack_heads(g),))


def _same_head_block(c):
    ii = lax.broadcasted_iota(jnp.int32, (2 * c, 2 * c), 0)
    jj = lax.broadcasted_iota(jnp.int32, (2 * c, 2 * c), 1)
    same = (ii < c) == (jj < c)
    return same & (jj <= ii), same & (jj < ii), (ii == jj).astype(F32)


@jax.custom_vjp
def _rows_join(top, bottom):
    return jnp.concatenate([top, bottom], axis=0)


def _rows_join_bwd(n_top, g):
    return g[:n_top], g[n_top:]


_rows_join.defvjp(lambda top, bottom: (_rows_join(top, bottom), top.shape[0]), _rows_join_bwd)


def _rows_split_impl(x, n_top):
    return x[:n_top], x[n_top:]


_rows_split = jax.custom_vjp(_rows_split_impl, nondiff_argnums=(1,))
_rows_split.defvjp(lambda x, n_top: (_rows_split_impl(x, n_top), None),
                   lambda n_top, _, g: (jnp.concatenate([g[0], g[1]], axis=0),))


def _rwkv_step(s0, r, lw, k, v, a, b):
    npair, nj = len(r), len(r[0])
    c = r[0][0].shape[0]
    combos = [(j, p) for j in range(nj) for p in range(npair)]
    every = lambda fn: {q: fn(q) for q in combos}
    at_ = lambda d: (lambda q: d[q[1]][q[0]])
    r_, lw_, k_, v_, a_, b_ = (at_(z) for z in (r, lw, k, v, a, b))
    incl, strict, eye = _same_head_block(c)

    gam = every(lambda q: _cumsum_rows(lw_(q)))
    gtot = every(lambda q: jnp.sum(lw_(q), axis=0, keepdims=True))
    eneg = every(lambda q: jnp.exp(-gam[q]))
    edec = every(lambda q: jnp.exp(gtot[q] - gam[q]))
    at = every(lambda q: _stack_heads(a_(q) * jnp.exp(gam[q] - lw_(q))))
    rt = every(lambda q: _stack_heads(r_(q) * jnp.exp(gam[q])))
    bt = every(lambda q: _stack_heads(b_(q) * eneg[q]))
    kt = every(lambda q: _stack_heads(k_(q) * eneg[q]))
    bdec = every(lambda q: _stack_heads(b_(q) * edec[q]))
    kdec = every(lambda q: _stack_heads(k_(q) * edec[q]))
    vs = every(lambda q: _stack_heads(v_(q)))
    a_ab = every(lambda q: jnp.where(strict, _nt(at[q], bt[q]), 0.0))
    a_ak = every(lambda q: jnp.where(strict, _nt(at[q], kt[q]), 0.0))
    a_rb = every(lambda q: jnp.where(incl, _nt(rt[q], bt[q]), 0.0))
    a_rk = every(lambda q: jnp.where(incl, _nt(rt[q], kt[q]), 0.0))
    tinv = every(lambda q: eye + a_ab[q])
    pw = a_ab
    span = 2
    while span < c:
        pw = every(lambda q, pw=pw: _nn_x3(pw[q], pw[q]))
        tinv = every(lambda q, pw=pw, tinv=tinv: tinv[q] + _nn_x3(pw[q], tinv[q]))
        span *= 2
    akv = every(lambda q: _nn(a_ak[q], vs[q]))
    w1 = every(lambda q: _nn_x3(tinv[q], at[q]))
    u0 = every(lambda q: _nn_x3(tinv[q], akv[q]))
    wr = every(lambda q: _rows_join(w1[q], rt[q]))
    bk = every(lambda q: _rows_join(bdec[q], kdec[q]))
    yv = every(lambda q: _nn(a_rk[q], vs[q]))
    gdec = every(lambda q: jnp.exp(gtot[q]))

    s = list(s0)
    y = [[None] * nj for _ in range(npair)]
    for j in range(nj):
        both = {p: _rows_split(_nt(wr[(j, p)], s[p]), 2 * c) for p in range(npair)}
        u = {p: both[p][0] + u0[(j, p)] for p in range(npair)}
        for p in range(npair):
            y[p][j] = _unstack_heads(both[p][1] + _nn(a_rb[(j, p)], u[p]) + yv[(j, p)])
        s = [s[p] * gdec[(j, p)] + _tn(_rows_join(u[p], vs[(j, p)]), bk[(j, p)]) for p in range(npair)]
    return y, s


def _rwkv_blocks(ref, nj, c):
    return [[ref[j * c:(j + 1) * c, p * PAIR_W:(p + 1) * PAIR_W] for j in range(nj)] for p in range(HB_PAIRS)]


def _rwkv_fwd(seqs, comm=None):
    t = seqs[0].shape[0]
    c, nj = RWKV_CHUNK, RWKV_GROUP
    n = t // (c * nj)

    def body(r_ref, lw_ref, k_ref, v_ref, a_ref, b_ref, y_ref, hs_ref, st_ref):
        @pl.when(pl.program_id(0) == 0)
        def _():
            st_ref[...] = jnp.zeros_like(st_ref)

        hs_ref[0] = st_ref[...]
        s0 = [st_ref[p] for p in range(HB_PAIRS)]
        y, s1 = _rwkv_step(s0, *[_rwkv_blocks(ref, nj, c) for ref in (r_ref, lw_ref, k_ref, v_ref, a_ref, b_ref)])
        for p in range(HB_PAIRS):
            for j in range(nj):
                y_ref[j * c:(j + 1) * c, p * PAIR_W:(p + 1) * PAIR_W] = y[p][j]
            st_ref[p] = s1[p]

    seq = pl.BlockSpec((c * nj, W_B), lambda i: (i, 0))
    return _hosting_call(
        body, comm, name="rwkv_fwd", grid=(n,), in_specs=[seq] * 6,
        out_specs=[seq, pl.BlockSpec((1, HB_PAIRS, PAIR_W, PAIR_W), lambda i: (i, 0, 0, 0))],
        out_shape=[SDS((t, W_B), F32), SDS((n, HB_PAIRS, PAIR_W, PAIR_W), F32)],
        scratch_shapes=[pltpu.VMEM((HB_PAIRS, PAIR_W, PAIR_W), F32)], args=tuple(seqs))


def _rwkv_bwd(seqs, hs, dy, comm=None):
    t = seqs[0].shape[0]
    c, nj = RWKV_CHUNK, RWKV_GROUP
    n = t // (c * nj)

    def body(r_ref, lw_ref, k_ref, v_ref, a_ref, b_ref, hs_ref, dy_ref,
             dr_ref, dlw_ref, dk_ref, dv_ref, da_ref, db_ref, dst_ref):
        @pl.when(pl.program_id(0) == 0)
        def _():
            dst_ref[...] = jnp.zeros_like(dst_ref)

        s0 = [hs_ref[0, p] for p in range(HB_PAIRS)]
        seq_vals = [_rwkv_blocks(ref, nj, c) for ref in (r_ref, lw_ref, k_ref, v_ref, a_ref, b_ref)]
        _, vjp = jax.vjp(_rwkv_step, s0, *seq_vals)
        grads = vjp((_rwkv_blocks(dy_ref, nj, c), [dst_ref[p] for p in range(HB_PAIRS)]))
        for ref, gr in zip((dr_ref, dlw_ref, dk_ref, dv_ref, da_ref, db_ref), grads[1:]):
            for p in range(HB_PAIRS):
                for j in range(nj):
                    ref[j * c:(j + 1) * c, p * PAIR_W:(p + 1) * PAIR_W] = gr[p][j]
        m0, m1 = _head_lane_masks()
        rows0 = (lax.broadcasted_iota(jnp.int32, (PAIR_W, 1), 0) < HB_DIM).astype(F32)
        blocks = rows0 * m0 + (1.0 - rows0) * m1
        for p in range(HB_PAIRS):
            dst_ref[p] = grads[0][p] * blocks

    seq = pl.BlockSpec((c * nj, W_B), lambda i: (n - 1 - i, 0))
    return _hosting_call(
        body, comm, name="rwkv_bwd", grid=(n,),
        in_specs=[seq] * 6 + [pl.BlockSpec((1, HB_PAIRS, PAIR_W, PAIR_W), lambda i: (n - 1 - i, 0, 0, 0)), seq],
        out_specs=[seq] * 6, out_shape=[SDS((t, W_B), F32)] * 6,
        scratch_shapes=[pltpu.VMEM((HB_PAIRS, PAIR_W, PAIR_W), F32)], args=(*seqs, hs, dy))


def _final_loss(x3, fnorm, target, *, tm):
    t, d = x3.shape

    def body(x_ref, g_ref, t_ref, dx_ref, dg_ref, loss_ref):
        @pl.when(pl.program_id(0) == 0)
        def _():
            dg_ref[...] = jnp.zeros_like(dg_ref)
            loss_ref[...] = jnp.zeros_like(loss_ref)

        x, g = x_ref[...], g_ref[...]
        rinv = lax.rsqrt(jnp.mean(x * x, axis=-1, keepdims=True) + NORM_EPS)
        xh = x * rinv
        diff = xh * g - t_ref[...]
        loss_ref[...] += 0.5 * jnp.sum(jnp.mean(diff * diff, axis=-1, keepdims=True))
        dy = diff * (1.0 / d)
        dg_ref[...] += jnp.sum(dy * xh, axis=0, keepdims=True)
        dxh = dy * g
        dx_ref[...] = rinv * (dxh - xh * jnp.mean(dxh * xh, axis=-1, keepdims=True))

    row = pl.BlockSpec((tm, d), lambda i: (i, 0))
    return pl.pallas_call(
        body, name="final_loss", grid=(t // tm,), in_specs=[row, pl.BlockSpec((1, d), lambda i: (0, 0)), row],
        out_specs=[row, pl.BlockSpec((1, d), lambda i: (0, 0)), pl.BlockSpec((8, 128), lambda i: (0, 0))],
        out_shape=[SDS((t, d), F32), SDS((1, d), F32), SDS((8, 128), F32)],
        compiler_params=_params(("arbitrary",)))(x3, fnorm, target)


def _gate_up_act(h, wgt, wut, *, tm, tn, name, comm=None):
    t, d = h.shape
    tm = min(tm, t)

    def body(h_ref, g_ref, u_ref, a_out, u_out, act_out):
        hv = h_ref[...]
        a = _dg(hv, g_ref[...], 1, 1, False)
        u = _dg(hv, u_ref[...], 1, 1, False)
        a_out[...] = a.astype(a_out.dtype)
        u_out[...] = u.astype(u_out.dtype)
        act_out[...] = (_silu(a) * u).astype(act_out.dtype)

    wspec = pl.BlockSpec((tn, d), lambda i, j: (j, 0))
    ospec = pl.BlockSpec((tm, tn), lambda i, j: (i, j))
    return _hosting_call(
        body, comm, name=name, grid=(t // tm, D_FF // tn),
        in_specs=[pl.BlockSpec((tm, d), lambda i, j: (i, 0)), wspec, wspec], out_specs=[ospec, ospec, ospec],
        out_shape=[SDS((t, D_FF), BF16), SDS((t, D_FF), BF16), SDS((t, D_FF), BF16)], scratch_shapes=[],
        args=(h, wgt, wut))


def _dact_swiglu(dout, wd, a, u, *, tm, tn, name, comm=None):
    t, d = dout.shape
    tm = min(tm, t)

    def body(d_ref, w_ref, a_ref, u_ref, da_out, du_out):
        dact = 0.5 * _dg(d_ref[...], w_ref[...], 1, 1, False)
        av, uv = a_ref[...].astype(F32), u_ref[...].astype(F32)
        s = _sigmoid(av)
        da_out[...] = (dact * uv * (s * (1.0 + av * (1.0 - s)))).astype(da_out.dtype)
        du_out[...] = (dact * (av * s)).astype(du_out.dtype)

    tile = pl.BlockSpec((tm, tn), lambda i, j: (i, j))
    return _hosting_call(
        body, comm, name=name, grid=(t // tm, D_FF // tn),
        in_specs=[pl.BlockSpec((tm, d), lambda i, j: (i, 0)), pl.BlockSpec((tn, d), lambda i, j: (j, 0)), tile, tile],
        out_specs=[tile, tile], out_shape=[SDS((t, D_FF), BF16), SDS((t, D_FF), BF16)], scratch_shapes=[],
        args=(dout, wd, a, u))


class _Plan:
    def __init__(self):
        self.entries, self.counts = collections.defaultdict(list), {}

    def carry(self, host, comm_of, after):
        self.entries[host].append((comm_of, after))

    def comm(self, host, g):
        comms = [comm_of(g) for comm_of, _ in self.entries.get(host, [])]
        self.counts[host] = [len(c.arrays) for c in comms]
        return functools.reduce(_join_comms, comms) if comms else None

    def done(self, host, results, w):
        start = 0
        for (_, after), n in zip(self.entries.get(host, []), self.counts.get(host, [])):
            after(results[start:start + n], w)
            start += n


def _ffn_fwd(x, w, tag, plan, g):
    h, = _rowwise(_rms_f, [x], [w[f"{tag}_norm"]], [[0]], [BF16], tm=512, name=f"{tag}_rms")
    (a, u, act), carried = _gate_up_act(h, w[f"{tag}_wgt"], w[f"{tag}_wut"], tm=2048, tn=256, name=f"{tag}_gate_up",
                                        comm=plan.comm(f"{tag}_gate_up", g))
    plan.done(f"{tag}_gate_up", carried, w)
    out = _mm(act, w[f"{tag}_wd"], tm=1024, tn=512, tk=D_FF, name=f"{tag}_down", res=x, scale=0.5)
    return out, (h, a, u, act)


def _ffn_bwd(dout, x, w, saved, tag, plan, g):
    h, a, u, act = saved

    def carrying(fn, host, *args, **kwargs):
        comm = plan.comm(host, g)
        res = fn(*args, name=host, comm=comm, **kwargs)
        out, carried = res if comm is not None else (res, [])
        plan.done(host, carried, w)
        return out

    (da, du), carried = _dact_swiglu(dout, w[f"{tag}_wd"], a, u, tm=2048, tn=256, name=f"{tag}_dact",
                                     comm=plan.comm(f"{tag}_dact", g))
    plan.done(f"{tag}_dact", carried, w)
    g[f"{tag}_wd"] = _mm(act, dout, ta=True, tm=D_FF // 2, tn=D_MODEL, tk=1024, name=f"{tag}_dwd", scale=0.5)
    g[f"{tag}_wgt"] = _mm(da, h, ta=True, tm=D_FF // 2, tn=D_MODEL, tk=1024, name=f"{tag}_dwg")
    g[f"{tag}_wut"] = carrying(_mm, f"{tag}_dwu", du, h, ta=True, tm=D_FF // 2, tn=D_MODEL, tk=1024)
    dh = carrying(_mm, f"{tag}_dh_g", da, w[f"{tag}_wgt"], tm=1024, tn=512, tk=D_FF)
    dh = carrying(_mm, f"{tag}_dh_u", du, w[f"{tag}_wut"], tm=1024, tn=512, tk=D_FF, res=dh)
    dx, g[f"{tag}_norm"] = _rowwise_bwd(_rms_f, [x], [w[f"{tag}_norm"]], [dh], x_grad=[True], p_grad=[True],
                                        dx_groups=[[0]], dx_dtypes=[F32], tm=512, name=f"{tag}_drms",
                                        extra={0: dout})
    return dx


def _local_step(x, target, w, plan=None):
    plan = plan or _Plan()
    ones_bd = jnp.kron(jnp.eye(HB_HEADS, dtype=F32), jnp.ones((HB_DIM, HB_DIM), F32))
    g = {}
    x1, ffn1_saved = _ffn_fwd(x, w, "ffn1", plan, g)
    hm, = _rowwise(_rms_f, [x1], [w["mix_norm"]], [[0]], [BF16], tm=512, name="mix_rms")
    p_h = _mm(hm, w["w_in_h"], tm=2048, tn=256, tk=D_MODEL, name="inproj_h")
    p_r = _mm(hm, w["w_in_r"], tm=2048, tn=256, tk=D_MODEL, name="inproj_r")
    o_a, hgrn_states = _hgrn_fwd(p_h, w["lb0"], w["lb1"], w["hgrn_out_norm"])

    mu = w["mu_pad"]
    prep_xs = [(p_r, W_B, 0), (p_r, W_B, 1), (p_r, W_B, 2), (p_r, LORA_PAD, 6),
               ("prev", p_r, W_B, 0), ("prev", p_r, W_B, 1), ("prev", p_r, W_B, 2), ("prev", p_r, LORA_PAD, 6)]
    prep_ps = [(mu, W_B, 0), (mu, W_B, 1), (mu, W_B, 2), (mu, LORA_PAD, 6), w["rwkv_w0"], w["w2_pad"], w["rwkv_a0"],
               w["a2_pad"], w["g2_pad"], w["rwkv_k_k"], w["rwkv_k_a"], ones_bd]
    prep_f = _rwkv_prep_f
    r, lw, k2, v, a_vec, b_vec, gate = _rowwise(prep_f, prep_xs, prep_ps, [[0], [1], [2], [3], [4], [5], [6]],
                                                [F32] * 7, tm=256, name="rwkv_prep")
    seqs = [r, lw, k2, v, a_vec, b_vec]
    (y, rwkv_states), carried = _rwkv_fwd(seqs, comm=plan.comm("rwkv_fwd", g))
    plan.done("rwkv_fwd", carried, w)
    post_f = _rwkv_post_f
    post_xs = [y, r, k2, v, gate]
    post_ps = [w["rwkv_r_k"], w["rwkv_gn_w"], w["rwkv_gn_b"], ones_bd]
    o_b, = _rowwise(post_f, post_xs, post_ps, [[0]], [F32], tm=256, name="rwkv_post")
    x2 = _mm(o_a, w["w_out_a"], tm=2048, tn=256, tk=W_A, name="outproj_a", res=x1)
    x2 = _mm(o_b, w["w_out_b"], tm=2048, tn=256, tk=W_B, name="outproj_b", res=x2)
    x3, ffn2_saved = _ffn_fwd(x2, w, "ffn2", plan, g)
    dx3, g["final_norm"], loss = _final_loss(x3, w["final_norm"], target, tm=256)

    dx2 = _ffn_bwd(dx3, x2, w, ffn2_saved, "ffn2", plan, g)
    do_a = _mm(dx2, w["w_out_a"], tb=True, tm=2048, tn=256, tk=D_MODEL, name="outproj_do_a")
    do_b = _mm(dx2, w["w_out_b"], tb=True, tm=2048, tn=256, tk=D_MODEL, name="outproj_do_b")
    g["w_out_a"] = _mm(o_a, dx2, ta=True, tm=W_A, tn=D_MODEL, tk=1024, name="outproj_dw_a")
    g["w_out_b"] = _mm(o_b, dx2, ta=True, tm=W_B, tn=D_MODEL, tk=1024, name="outproj_dw_b")

    (dp_h, g["lb0"], g["lb1"], g["hgrn_out_norm"]), carried = _hgrn_bwd(
        p_h, w["lb0"], w["lb1"], w["hgrn_out_norm"], hgrn_states, do_a, 0, comm=plan.comm("hgrn_bwd", g))
    plan.done("hgrn_bwd", carried, w)
    post_out = _rowwise_bwd(post_f, post_xs, post_ps, [do_b], x_grad=[True] * 5, p_grad=[True] * 3 + [False],
                            dx_groups=[[0], [1], [2], [3], [4]], dx_dtypes=[F32] * 5, tm=256, name="rwkv_post_bwd")
    dy, dr1, dk1, dv1, dgate, g["rwkv_r_k"], g["rwkv_gn_w"], g["rwkv_gn_b"] = post_out
    (dr2, dlw, dk2, dv2, da_vec, db_vec), carried = _rwkv_bwd(seqs, rwkv_states, dy, comm=plan.comm("rwkv_bwd", g))
    plan.done("rwkv_bwd", carried, w)

    def prep2_f(*vals):
        r_, lw_, k2_, v_, a_, b_, g_ = prep_f(*vals)
        return r_, lw_, k2_, v_, a_, b_, g_, r_, k2_, v_

    prep_out = _rowwise_bwd(prep2_f, prep_xs, prep_ps, [dr2, dlw, dk2, dv2, da_vec, db_vec, dgate, dr1, dk1, dv1],
                            x_grad=[True] * 8, p_grad=[True] * 11 + [False], dx_groups=[[0, 1, 2, 3], [4, 5, 6, 7]],
                            dx_dtypes=[F32, F32], tm=256, name="rwkv_prep_bwd")
    dpr_main, dpr_prev = prep_out[0], prep_out[1]
    (dmu_r, dmu_k, dmu_v, dmu_lo, g["rwkv_w0"], g["w2_pad"], g["rwkv_a0"], g["a2_pad"], g["g2_pad"],
     g["rwkv_k_k"], g["rwkv_k_a"]) = prep_out[2:]
    g["mu_pad"] = jnp.concatenate([dmu_r, dmu_k, dmu_v, dmu_lo], axis=1)
    dp_r, = _rowwise(lambda u_, s_: (u_ + s_,), [dpr_main, ("next", dpr_prev, N_RWKV_PAD, 0)], [], [[0]], [F32],
                     tm=512, name="rwkv_dp_sum")
    dhm = _mm(dp_h, w["w_in_h"], tb=True, tm=1024, tn=512, tk=N_HGRN_COLS, name="inproj_dh_h")
    dhm = _mm(dp_r, w["w_in_r"], tb=True, tm=1024, tn=512, tk=N_RWKV_PAD, name="inproj_dh_r", res=dhm)
    g["w_in_h"] = _mm(hm, dp_h, ta=True, tm=D_MODEL, tn=D_MODEL, tk=1024, name="inproj_dw_h")
    g["w_in_r"] = _mm(hm, dp_r, ta=True, tm=D_MODEL, tn=N_RWKV_PAD // 2, tk=1024, name="inproj_dw_r")
    mix_comm = plan.comm("mix_drms", g)
    mix_out = _rowwise_bwd(_rms_f, [x1], [w["mix_norm"]], [dhm], x_grad=[True], p_grad=[True], dx_groups=[[0]],
                           dx_dtypes=[F32], tm=512, name="mix_drms", extra={0: dx2}, comm=mix_comm)
    (dx1, g["mix_norm"]), carried = mix_out if mix_comm is not None else (mix_out, [])
    plan.done("mix_drms", carried, w)
    dx0 = _ffn_bwd(dx1, x, w, ffn1_saved, "ffn1", plan, g)
    return loss, dx0, g


HBM_SPEC = pl.BlockSpec(memory_space=pl.ANY)

Comm = collections.namedtuple("Comm", "arrays out_shapes aliased sem_shapes start finish")


def _join_comms(first, second):
    assert first.aliased == second.aliased
    n, s = len(first.arrays), len(first.sem_shapes)

    def start(ins, outs, sems):
        first.start(ins[:n], outs[:n], sems[:s])
        second.start(ins[n:], outs[n:], sems[s:])

    def finish(ins, outs, sems):
        first.finish(ins[:n], outs[:n], sems[:s])
        second.finish(ins[n:], outs[n:], sems[s:])

    return Comm(list(first.arrays) + list(second.arrays), list(first.out_shapes) + list(second.out_shapes),
                first.aliased, list(first.sem_shapes) + list(second.sem_shapes), start, finish)


def _run_comm(comm, name):
    n = len(comm.arrays)

    def body(*refs):
        ins, outs, sems = refs[:n], refs[n:2 * n], refs[2 * n:]
        comm.start(ins, outs, sems)
        comm.finish(ins, outs, sems)

    return pl.pallas_call(
        body, name=name, in_specs=[HBM_SPEC] * n, out_specs=[HBM_SPEC] * n, out_shape=list(comm.out_shapes),
        input_output_aliases={t: t for t in range(n)} if comm.aliased else {},
        scratch_shapes=list(comm.sem_shapes))(*comm.arrays)


def _hosting_call(body, comm, *, name, grid, in_specs, out_specs, out_shape, scratch_shapes, args):
    sem = ("arbitrary",) * len(grid)
    if comm is None:
        res = pl.pallas_call(body, name=name, grid=grid, in_specs=in_specs, out_specs=out_specs, out_shape=out_shape,
                             scratch_shapes=scratch_shapes, compiler_params=_params(sem))(*args)
        return list(res), []
    ni, no, ns, nc = len(in_specs), len(out_specs), len(scratch_shapes), len(comm.arrays)

    def wrapped(*refs):
        ins, cins = refs[:ni], refs[ni:ni + nc]
        outs, couts = refs[ni + nc:ni + nc + no], refs[ni + nc + no:ni + 2 * nc + no]
        scr, sems = refs[ni + 2 * nc + no:ni + 2 * nc + no + ns], refs[ni + 2 * nc + no + ns:]
        first = functools.reduce(jnp.logical_and, [pl.program_id(k) == 0 for k in range(len(grid))])
        last = functools.reduce(jnp.logical_and, [pl.program_id(k) == grid[k] - 1 for k in range(len(grid))])

        @pl.when(first)
        def _():
            comm.start(cins, couts, sems)

        body(*ins, *outs, *scr)

        @pl.when(last)
        def _():
            comm.finish(cins, couts, sems)

    res = pl.pallas_call(
        wrapped, name=name, grid=grid, in_specs=list(in_specs) + [HBM_SPEC] * nc,
        out_specs=list(out_specs) + [HBM_SPEC] * nc, out_shape=list(out_shape) + list(comm.out_shapes),
        scratch_shapes=list(scratch_shapes) + list(comm.sem_shapes),
        input_output_aliases={ni + t: no + t for t in range(nc)} if comm.aliased else {},
        compiler_params=_params(sem))(*args, *comm.arrays)
    return list(res[:no]), list(res[no:])


def _chips(x, y):
    return [(1 - x, y), (x, 1 - y), (1 - x, 1 - y)]


def _gather_comm(bufs):
    n = len(bufs)

    def copies(outs, sems):
        ici_send, ici_recv, d2d_send, d2d_recv = sems
        x, y, c = lax.axis_index("x"), lax.axis_index("y"), lax.axis_index("c")

        def half(t, slot, hc):
            hr = bufs[t].shape[1] // 2
            return outs[t].at[slot, pl.ds(pl.multiple_of(hc * hr, 16), hr), :]

        def ici(t, j, slot, px, py):
            return pltpu.make_async_remote_copy(src_ref=half(t, slot, c), dst_ref=half(t, slot, c),
                                                send_sem=ici_send.at[3 * t + j], recv_sem=ici_recv.at[3 * t + j],
                                                device_id=(px, py, c), device_id_type=MESH)

        def d2d(t, j, slot, hc):
            return pltpu.make_async_remote_copy(src_ref=half(t, slot, hc), dst_ref=half(t, slot, hc),
                                                send_sem=d2d_send.at[3 * t + j], recv_sem=d2d_recv.at[3 * t + j],
                                                device_id=(x, y, 1 - c), device_id_type=MESH)

        peers = [(t, j, px, py) for t in range(n) for j, (px, py) in enumerate(_chips(x, y))]
        return ici, d2d, peers, 2 * x + y, c

    def start(ins, outs, sems):
        ici, _, peers, me, _ = copies(outs, sems)
        for t, j, px, py in peers:
            ici(t, j, me, px, py).start()

    def finish(ins, outs, sems):
        ici, d2d, peers, me, c = copies(outs, sems)
        for t, j, px, py in peers:
            ici(t, j, 2 * px + py, px, py).wait_recv()
            d2d(t, j, 2 * px + py, c).start()
        for t, j, px, py in peers:
            d2d(t, j, 2 * px + py, 1 - c).wait_recv()
        for t, j, px, py in peers:
            ici(t, j, me, px, py).wait_send()
            d2d(t, j, 2 * px + py, c).wait_send()

    return Comm(list(bufs), [SDS(b.shape, b.dtype) for b in bufs], True, [pltpu.SemaphoreType.DMA((3 * n,))] * 4,
                start, finish)


def _sibling_exchange_comm(gs):
    n = len(gs)

    def copies(ins, outs, sems):
        x, y, c = lax.axis_index("x"), lax.axis_index("y"), lax.axis_index("c")
        cps = []
        for t in range(n):
            hr = gs[t].shape[1] // 2
            src = ins[t].at[:, pl.ds(pl.multiple_of((1 - c) * hr, SUBLANES), hr), :]
            cps.append(pltpu.make_async_remote_copy(src_ref=src, dst_ref=outs[t], send_sem=sems[0].at[t],
                                                    recv_sem=sems[1].at[t], device_id=(x, y, 1 - c),
                                                    device_id_type=MESH))
        return cps

    def start(ins, outs, sems):
        for cp in copies(ins, outs, sems):
            cp.start()

    def finish(ins, outs, sems):
        for cp in copies(ins, outs, sems):
            cp.wait()

    return Comm(list(gs), [SDS((N_CHIPS, g.shape[1] // 2, g.shape[2]), g.dtype) for g in gs], False,
                [pltpu.SemaphoreType.DMA((n,))] * 2, start, finish)


def _chip_exchange_comm(ss):
    n = len(ss)

    def copies(ins, outs, sems):
        x, y, c = lax.axis_index("x"), lax.axis_index("y"), lax.axis_index("c")
        me = 2 * x + y

        def copy(t, j, px, py, src_slot, dst_slot):
            return pltpu.make_async_remote_copy(src_ref=ins[t].at[src_slot], dst_ref=outs[t].at[dst_slot],
                                                send_sem=sems[0].at[3 * t + j], recv_sem=sems[1].at[3 * t + j],
                                                device_id=(px, py, c), device_id_type=MESH)

        peers = [(t, j, px, py) for t in range(n) for j, (px, py) in enumerate(_chips(x, y))]
        return copy, peers, me

    def start(ins, outs, sems):
        copy, peers, me = copies(ins, outs, sems)
        for t, j, px, py in peers:
            copy(t, j, px, py, 2 * px + py, me).start()

    def finish(ins, outs, sems):
        copy, peers, me = copies(ins, outs, sems)
        for t, j, px, py in peers:
            copy(t, j, px, py, me, 2 * px + py).wait_recv()
        for t, j, px, py in peers:
            copy(t, j, px, py, 2 * px + py, me).wait_send()

    return Comm(list(ss), [SDS(s.shape, s.dtype) for s in ss], False, [pltpu.SemaphoreType.DMA((3 * n,))] * 2,
                start, finish)


def _sibling_swap_comm(fs):
    n = len(fs)

    def copies(ins, outs, sems):
        x, y, c = lax.axis_index("x"), lax.axis_index("y"), lax.axis_index("c")
        return [pltpu.make_async_remote_copy(src_ref=ins[t], dst_ref=outs[t], send_sem=sems[0].at[t],
                                             recv_sem=sems[1].at[t], device_id=(x, y, 1 - c), device_id_type=MESH)
                for t in range(n)]

    def start(ins, outs, sems):
        for cp in copies(ins, outs, sems):
            cp.start()

    def finish(ins, outs, sems):
        for cp in copies(ins, outs, sems):
            cp.wait()

    return Comm(list(fs), [SDS(f.shape, f.dtype) for f in fs], False, [pltpu.SemaphoreType.DMA((n,))] * 2,
                start, finish)


def _row_tile(rows, cap=512):
    best = SUBLANES
    for tr in range(SUBLANES, min(rows, cap) + 1, SUBLANES):
        if rows % tr == 0:
            best = tr
    return best


def _add_halves(g4, r4, c_idx, name):
    _, hr, lanes = r4.shape
    tr = _row_tile(hr)
    nb = hr // tr

    def body(c_ref, a_ref, b_ref, o_ref):
        o_ref[...] = (a_ref[...] + b_ref[...]).astype(o_ref.dtype)

    grid_spec = pltpu.PrefetchScalarGridSpec(
        num_scalar_prefetch=1, grid=(N_CHIPS, nb),
        in_specs=[pl.BlockSpec((None, tr, lanes), lambda q, i, c_ref: (q, c_ref[0] * nb + i, 0)),
                  pl.BlockSpec((None, tr, lanes), lambda q, i, c_ref: (q, i, 0))],
        out_specs=pl.BlockSpec((None, tr, lanes), lambda q, i, c_ref: (q, i, 0)))
    return pl.pallas_call(body, name=name, grid_spec=grid_spec, out_shape=SDS(r4.shape, BF16),
                          compiler_params=_params(("parallel", "parallel")))(c_idx, g4, r4)


def _sum_chips(r4, s4, me_idx, name):
    _, rows, lanes = r4.shape
    tr = _row_tile(rows)

    def body(me_ref, a_ref, b_ref, c_ref, d_ref, own_ref, o_ref):
        own = own_ref[...].astype(F32)
        p = [jnp.where(me_ref[0] == q, own, ref[...].astype(F32)) for q, ref in enumerate((a_ref, b_ref, c_ref, d_ref))]
        o_ref[...] = ((p[0] + p[1]) + p[2]) + p[3]

    other = lambda q: (lambda i, me_ref: (jnp.where(me_ref[0] == q, (q + 1) % N_CHIPS, q), i, 0))
    grid_spec = pltpu.PrefetchScalarGridSpec(
        num_scalar_prefetch=1, grid=(rows // tr,),
        in_specs=[pl.BlockSpec((None, tr, lanes), other(q)) for q in range(N_CHIPS)]
        + [pl.BlockSpec((None, tr, lanes), lambda i, me_ref: (me_ref[0], i, 0))],
        out_specs=pl.BlockSpec((tr, lanes), lambda i, me_ref: (i, 0)))
    return pl.pallas_call(body, name=name, grid_spec=grid_spec, out_shape=SDS((rows, lanes), F32),
                          compiler_params=_params(("parallel",)))(me_idx, r4, r4, r4, r4, s4)


def _adamw(wf, g_own, g_other, mf, vf, c_idx, name):
    rows, lanes = wf.shape
    hr = rows // 2
    tr = _row_tile(hr)
    nb = hr // tr
    c1 = 1.0 / (1.0 - ADAM_B1 ** ADAM_STEP)
    c2 = 1.0 / (1.0 - ADAM_B2 ** ADAM_STEP)

    def body(c_ref, w_ref, go_ref, gx_ref, m_ref, v_ref, g_ref, d_ref, nm_ref, nv_ref):
        gv = jnp.where(pl.program_id(0) == c_ref[0], go_ref[...], gx_ref[...])
        m = ADAM_B1 * m_ref[...] + (1.0 - ADAM_B1) * gv
        v = ADAM_B2 * v_ref[...] + (1.0 - ADAM_B2) * (gv * gv)
        g_ref[...] = gv
        d_ref[...] = -ADAM_LR * ((m * c1) / (jnp.sqrt(v * c2) + ADAM_EPS) + ADAM_WD * w_ref[...])
        nm_ref[...] = m
        nv_ref[...] = v

    full = pl.BlockSpec((tr, lanes), lambda h, i, c_ref: (h * nb + i, 0))
    half = pl.BlockSpec((tr, lanes), lambda h, i, c_ref: (i, 0))
    grid_spec = pltpu.PrefetchScalarGridSpec(num_scalar_prefetch=1, grid=(2, nb),
                                             in_specs=[full, half, half, full, full], out_specs=[full] * 4)
    return pl.pallas_call(body, name=name, grid_spec=grid_spec, out_shape=[SDS((rows, lanes), F32)] * 4,
                          compiler_params=_params(("parallel", "parallel")))(c_idx, wf, g_own, g_other, mf, vf)


BIG = ("ffn1_w_gate", "ffn1_w_up", "ffn1_w_down", "ffn2_w_gate", "ffn2_w_up", "ffn2_w_down", "w_out", "w_in")
TRANSPOSED = ("ffn1_w_gate", "ffn1_w_up", "ffn2_w_gate", "ffn2_w_up")
PACKED = ("rwkv_w2", "rwkv_a2", "rwkv_g2")
SMALL_SHAPES = {"ffn1_norm": (1, D_MODEL), "mix_norm": (1, D_MODEL), "hgrn_lb_logits": (2, W_A),
                "hgrn_out_norm": (1, W_A), "rwkv_shift_mu": (1, N_RWKV_COLS), "rwkv_w0": (1, W_B),
                "rwkv_a0": (1, W_B), "rwkv_k_k": (1, W_B), "rwkv_k_a": (1, W_B),
                "rwkv_r_k": (1, HB_HEADS, HB_DIM), "rwkv_gn_w": (1, W_B), "rwkv_gn_b": (1, W_B),
                "ffn2_norm": (1, D_MODEL), "final_norm": (D_MODEL,)}
PACK_ELEMS = sum(_numel(_shard_shape(n)) for n in PACKED) + sum(_numel(SMALL_SHAPES[n]) for n in SMALL)
PACK_ROWS = -(-PACK_ELEMS // (32 * LANES)) * 32


def _to_rows(name, shard):
    return shard[0].T if name in TRANSPOSED else shard[0]


def _from_rows(name, rows):
    return (rows.T if name in TRANSPOSED else rows)[None]


def _pack(sharded, small):
    flat = jnp.concatenate([sharded[n].reshape(-1) for n in PACKED] + [small[n].reshape(-1) for n in SMALL])
    return jnp.pad(flat, (0, PACK_ROWS * LANES - flat.shape[0])).reshape(PACK_ROWS, LANES)


def _unpack(packed):
    flat, out, off = packed.reshape(-1), {}, 0
    for n in PACKED:
        shp = _shard_shape(n)
        out[n] = flat[off:off + _numel(shp)].reshape((1,) + shp)
        off += _numel(shp)
    for n in SMALL:
        shp = SMALL_SHAPES[n]
        out[n] = flat[off:off + _numel(shp)].reshape(shp)
        off += _numel(shp)
    return out


def _quarter(full, name, q):
    shape, ax = SHARDED_SHAPES[name]
    w = shape[ax] // N_CHIPS
    return lax.slice_in_dim(full, q * w, (q + 1) * w, axis=ax)


def kernel(x, ffn1_norm, ffn1_w_gate, ffn1_w_up, ffn1_w_down, mix_norm, w_in, hgrn_lb_logits, hgrn_out_norm, rwkv_shift_mu, rwkv_w0, rwkv_w2, rwkv_a0, rwkv_a2, rwkv_g2, rwkv_k_k, rwkv_k_a, rwkv_r_k, rwkv_gn_w, rwkv_gn_b, w_out, ffn2_norm, ffn2_w_gate, ffn2_w_up, ffn2_w_down, final_norm, loss_target, m_ffn1_norm, m_ffn1_w_gate, m_ffn1_w_up, m_ffn1_w_down, m_mix_norm, m_w_in, m_hgrn_lb_logits, m_hgrn_out_norm, m_rwkv_shift_mu, m_rwkv_w0, m_rwkv_w2, m_rwkv_a0, m_rwkv_a2, m_rwkv_g2, m_rwkv_k_k, m_rwkv_k_a, m_rwkv_r_k, m_rwkv_gn_w, m_rwkv_gn_b, m_w_out, m_ffn2_norm, m_ffn2_w_gate, m_ffn2_w_up, m_ffn2_w_down, m_final_norm, v_ffn1_norm, v_ffn1_w_gate, v_ffn1_w_up, v_ffn1_w_down, v_mix_norm, v_w_in, v_hgrn_lb_logits, v_hgrn_out_norm, v_rwkv_shift_mu, v_rwkv_w0, v_rwkv_w2, v_rwkv_a0, v_rwkv_a2, v_rwkv_g2, v_rwkv_k_k, v_rwkv_k_a, v_rwkv_r_k, v_rwkv_gn_w, v_rwkv_gn_b, v_w_out, v_ffn2_norm, v_ffn2_w_gate, v_ffn2_w_up, v_ffn2_w_down, v_final_norm):
    args = dict(locals())
    wts = {n: args[n] for n in ALL_WEIGHTS}
    moms = {n: args["m_" + n] for n in ALL_WEIGHTS}
    vars_ = {n: args["v_" + n] for n in ALL_WEIGHTS}

    me = 2 * lax.axis_index("x") + lax.axis_index("y")
    c_idx = lax.axis_index("c").astype(jnp.int32).reshape(1)
    me_idx = me.astype(jnp.int32).reshape(1)
    shard_of = {n: _to_rows(n, wts[n]).astype(BF16) for n in BIG}
    shard_of["packed"] = _pack(wts, {n: wts[n] for n in SMALL}).astype(BF16)
    group = {"ffn1": BIG[0:3], "ffn2": BIG[3:6], "mix": ("w_out", "w_in", "packed")}

    def slot_bufs(names):
        return [lax.dynamic_update_slice(jnp.zeros((N_CHIPS,) + shard_of[n].shape, BF16), shard_of[n][None],
                                         (me, 0, 0)) for n in names]

    def ffn_weights(tag, gathered):
        return {f"{tag}_wgt": gathered[0].reshape(D_FF, D_MODEL), f"{tag}_wut": gathered[1].reshape(D_FF, D_MODEL),
                f"{tag}_wd": gathered[2].reshape(D_FF, D_MODEL)}

    def mixer_weights(gathered):
        w_out_full = gathered[0].reshape(D_MODEL, D_MODEL)
        w_in_full = jnp.concatenate([gathered[1][q] for q in range(N_CHIPS)], axis=1)
        packs = gathered[2].reshape(N_CHIPS, PACK_ROWS * LANES)
        full, off = {}, 0
        for n in PACKED:
            shp = _shard_shape(n)
            full[n] = jnp.concatenate([packs[q, off:off + _numel(shp)].reshape(shp) for q in range(N_CHIPS)], axis=1)
            off += _numel(shp)
        zrow = lambda nrow: jnp.zeros((nrow, W_B), BF16)
        return {"w_out_a": w_out_full[:W_A], "w_out_b": w_out_full[W_A:], "w_in_h": w_in_full[:, :N_HGRN_COLS],
                "w_in_r": jnp.pad(w_in_full[:, N_HGRN_COLS:], ((0, 0), (0, N_RWKV_PAD - N_RWKV_COLS))),
                "w2_pad": jnp.concatenate([full["rwkv_w2"], zrow(LORA_PAD - 32)], axis=0),
                "a2_pad": jnp.concatenate([zrow(32), full["rwkv_a2"], zrow(LORA_PAD - 64)], axis=0),
                "g2_pad": jnp.concatenate([zrow(64), full["rwkv_g2"], zrow(LORA_PAD - 160)], axis=0)}

    plan = _Plan()
    first = _run_comm(_gather_comm(slot_bufs(group["ffn1"][:2])), "gather_ffn1")
    w = {"ffn1_wgt": first[0].reshape(D_FF, D_MODEL), "ffn1_wut": first[1].reshape(D_FF, D_MODEL)}

    def after_gate_up(res, w_):
        w_["ffn1_wd"] = res[0].reshape(D_FF, D_MODEL)
        w_.update(mixer_weights(res[1:]))

    plan.carry("ffn1_gate_up", lambda g: _gather_comm(slot_bufs(group["ffn1"][2:] + group["mix"])), after_gate_up)
    plan.carry("rwkv_fwd", lambda g: _gather_comm(slot_bufs(group["ffn2"])),
               lambda res, w_: w_.update(ffn_weights("ffn2", res)))
    w["ffn1_norm"], w["ffn2_norm"] = ffn1_norm, ffn2_norm
    w["mix_norm"] = mix_norm
    w["lb0"], w["lb1"] = hgrn_lb_logits[0:1], hgrn_lb_logits[1:2]
    w["hgrn_out_norm"] = hgrn_out_norm
    w["mu_pad"] = jnp.pad(rwkv_shift_mu, ((0, 0), (0, N_RWKV_PAD - N_RWKV_COLS)))
    for n in ("rwkv_w0", "rwkv_a0", "rwkv_k_k", "rwkv_k_a", "rwkv_gn_w", "rwkv_gn_b"):
        w[n] = wts[n]
    w["rwkv_r_k"] = rwkv_r_k.reshape(1, W_B)
    w["final_norm"] = final_norm.reshape(1, D_MODEL)

    def reduce_rows(names, gs):
        r1 = _run_comm(_sibling_exchange_comm(gs), "grad_sibling_exchange")
        s4 = [_add_halves(gt, rt, c_idx, f"grad_add_halves_{n}") for gt, rt, n in zip(gs, r1, names)]
        r2 = _run_comm(_chip_exchange_comm(s4), "grad_chip_exchange")
        return [_sum_chips(rt, st, me_idx, f"grad_sum_chips_{n}") for rt, st, n in zip(r2, s4, names)]

    early = {}

    def reduce_early(names, grads_of, sibling_host, chips_host):
        def sibling_comm(g):
            early[names, "gs"] = grads_of(g)
            return _sibling_exchange_comm(early[names, "gs"])

        def after_sibling(res, w_):
            early[names, "s4"] = [_add_halves(gt, rt, c_idx, f"grad_add_halves_{n}")
                                  for gt, rt, n in zip(early[names, "gs"], res, names)]

        def after_chips(res, w_):
            early.update(zip(names, [_sum_chips(rt, st, me_idx, f"grad_sum_chips_{n}")
                                     for rt, st, n in zip(res, early[names, "s4"], names)]))

        plan.carry(sibling_host, sibling_comm, after_sibling)
        plan.carry(chips_host, lambda g: _chip_exchange_comm(early[names, "s4"]), after_chips)

    def proj_grads(g):
        g_w_in = jnp.concatenate([g["w_in_h"], g["w_in_r"][:, :N_RWKV_COLS]], axis=1)
        return [jnp.concatenate([g["w_out_a"], g["w_out_b"]], axis=0).reshape(N_CHIPS, -1, D_MODEL),
                jnp.stack([_quarter(g_w_in, "w_in", q) for q in range(N_CHIPS)])]

    rows_of = lambda keys: (lambda g: [g[k].reshape(N_CHIPS, -1, D_MODEL) for k in keys])
    reduce_early(group["ffn2"], rows_of(("ffn2_wgt", "ffn2_wut", "ffn2_wd")), "hgrn_bwd", "rwkv_bwd")
    reduce_early(("w_out", "w_in"), proj_grads, "mix_drms", "ffn1_dact")
    reduce_early(("ffn1_w_down", "ffn1_w_gate"), rows_of(("ffn1_wd", "ffn1_wgt")), "ffn1_dwu", "ffn1_dh_g")
    reduce_early(("ffn1_w_up",), rows_of(("ffn1_wut",)), "ffn1_dh_g", "ffn1_dh_u")
    loss_slab, grad_x, g = _local_step(x[0], loss_target[0], w, plan)
    loss = lax.psum(loss_slab[0, 0], ("x", "y", "c"))

    gfull = {
        "rwkv_w2": g["w2_pad"][0:32], "rwkv_a2": g["a2_pad"][32:64], "rwkv_g2": g["g2_pad"][64:160],
    }
    gsmall = {
        "ffn1_norm": g["ffn1_norm"], "mix_norm": g["mix_norm"],
        "hgrn_lb_logits": jnp.concatenate([g["lb0"], g["lb1"]], axis=0), "hgrn_out_norm": g["hgrn_out_norm"],
        "rwkv_shift_mu": g["mu_pad"][:, :N_RWKV_COLS], "rwkv_w0": g["rwkv_w0"], "rwkv_a0": g["rwkv_a0"],
        "rwkv_k_k": g["rwkv_k_k"], "rwkv_k_a": g["rwkv_k_a"], "rwkv_r_k": g["rwkv_r_k"],
        "rwkv_gn_w": g["rwkv_gn_w"], "rwkv_gn_b": g["rwkv_gn_b"], "ffn2_norm": g["ffn2_norm"],
        "final_norm": g["final_norm"],
    }
    packed = jnp.stack([_pack({n: _quarter(gfull[n], n, q) for n in PACKED}, gsmall) for q in range(N_CHIPS)])
    early["packed"], = reduce_rows(["packed"], [packed])
    names = list(BIG) + ["packed"]
    own = [early[n] for n in names]
    other = _run_comm(_sibling_swap_comm(own), "grad_sibling_swap")

    def rows_list(d):
        return [_to_rows(n, d[n]) for n in BIG] + [_pack(d, {n: d[n] for n in SMALL})]

    outs = [_adamw(wt, go, gx, mt, vt, c_idx, f"adamw_{n}")
            for wt, go, gx, mt, vt, n in zip(rows_list(wts), own, other, rows_list(moms), rows_list(vars_), names)]
    results = []
    for k in range(4):
        per = [outs[i][k] for i in range(len(names))]
        d = {n: _from_rows(n, z) for n, z in zip(BIG, per[:-1])}
        d.update(_unpack(per[-1]))
        results.append(d)
    return (loss, grad_x[None], *[r[n] for r in results for n in ALL_WEIGHTS])
```

```python
import collections
import functools

import jax
import jax.numpy as jnp
from jax import lax
from jax.experimental import pallas as pl
from jax.experimental.pallas import tpu as pltpu

F32 = jnp.float32
BF16 = jnp.bfloat16
SDS = jax.ShapeDtypeStruct
MESH = pl.DeviceIdType.MESH

D_MODEL = 1024
D_FF = 2816
W_A = 512
W_B = 512
HA_HEADS, HA_DIM = 4, 128
HB_HEADS, HB_DIM = 8, 64
HGRN_CHUNK = 64
HGRN_GROUP = 2
RWKV_CHUNK = 16
RWKV_GROUP = 4
N_HGRN_COLS = 4 * W_A
N_RWKV_COLS = 3 * W_B + 32 + 32 + 96
N_RWKV_PAD = 1792
LORA_PAD = 256
NORM_EPS = 1e-6
RWKV_GN_EPS = 64e-5
L2_EPS = 1e-12
ADAM_LR, ADAM_B1, ADAM_B2, ADAM_EPS, ADAM_WD, ADAM_STEP = 0.001, 0.9, 0.999, 1e-8, 0.01, 10

N_CHIPS = 4
VMEM_LIMIT_V7X = 56 * 1024 * 1024
LANES = 1024

SHARDED_SHAPES = {
    "ffn1_w_gate": ((D_MODEL, D_FF), 1), "ffn1_w_up": ((D_MODEL, D_FF), 1), "ffn1_w_down": ((D_FF, D_MODEL), 0),
    "w_in": ((D_MODEL, N_HGRN_COLS + N_RWKV_COLS), 1), "rwkv_w2": ((32, W_B), 1), "rwkv_a2": ((32, W_B), 1),
    "rwkv_g2": ((96, W_B), 1), "w_out": ((D_MODEL, D_MODEL), 0),
    "ffn2_w_gate": ((D_MODEL, D_FF), 1), "ffn2_w_up": ((D_MODEL, D_FF), 1), "ffn2_w_down": ((D_FF, D_MODEL), 0),
}
SMALL = ("ffn1_norm", "mix_norm", "hgrn_lb_logits", "hgrn_out_norm", "rwkv_shift_mu", "rwkv_w0", "rwkv_a0",
         "rwkv_k_k", "rwkv_k_a", "rwkv_r_k", "rwkv_gn_w", "rwkv_gn_b", "ffn2_norm", "final_norm")
ALL_WEIGHTS = ("ffn1_norm", "ffn1_w_gate", "ffn1_w_up", "ffn1_w_down", "mix_norm", "w_in", "hgrn_lb_logits",
               "hgrn_out_norm", "rwkv_shift_mu", "rwkv_w0", "rwkv_w2", "rwkv_a0", "rwkv_a2", "rwkv_g2", "rwkv_k_k",
               "rwkv_k_a", "rwkv_r_k", "rwkv_gn_w", "rwkv_gn_b", "w_out", "ffn2_norm", "ffn2_w_gate", "ffn2_w_up",
               "ffn2_w_down", "final_norm")


def _shard_shape(name):
    shape, ax = SHARDED_SHAPES[name]
    return tuple(s // N_CHIPS if i == ax else s for i, s in enumerate(shape))


def _numel(shape):
    n = 1
    for s in shape:
        n *= s
    return n


def _params(sem=None):
    return pltpu.CompilerParams(dimension_semantics=sem, vmem_limit_bytes=VMEM_LIMIT_V7X)


def _split2(x):
    hi = x.astype(BF16)
    return hi, (x.astype(F32) - hi.astype(F32)).astype(BF16)


def _dg(x, y, cx, cy, hi):
    dn = (((cx,), (cy,)), ((), ()))
    dot = lambda p, q: lax.dot_general(p, q, dn, preferred_element_type=F32)
    if hi == "x3":
        (xh, xl), (yh, yl) = _split2(x), _split2(y)
        return dot(xh, yh) + (dot(xh, yl) + dot(xl, yh))
    return dot(x.astype(BF16), y.astype(BF16))


def _make_mm(hi, cotangent_forms=None):
    @jax.custom_vjp
    def nn(x, y):
        return _dg(x, y, 1, 0, hi)

    @jax.custom_vjp
    def nt(x, y):
        return _dg(x, y, 1, 1, hi)

    @jax.custom_vjp
    def tn(x, y):
        return _dg(x, y, 0, 0, hi)

    bnn, bnt, btn = cotangent_forms or (nn, nt, tn)
    nn.defvjp(lambda x, y: (nn(x, y), (x, y)), lambda r, g: (bnt(g, r[1]), btn(r[0], g)))
    nt.defvjp(lambda x, y: (nt(x, y), (x, y)), lambda r, g: (bnn(g, r[1]), btn(g, r[0])))
    tn.defvjp(lambda x, y: (tn(x, y), (x, y)), lambda r, g: (bnt(r[1], g), bnn(r[0], g)))
    return nn, nt, tn


_nn, _nt, _tn = _make_mm(False)
_nn_x3, _nt_x3, _tn_x3 = _make_mm("x3", (_nn, _nt, _tn))


def _tri_apply(x, transpose):
    c = x.shape[0]
    tri = (lax.broadcasted_iota(jnp.int32, (c, c), 1) <= lax.broadcasted_iota(jnp.int32, (c, c), 0)).astype(BF16)
    dn = (((0 if transpose else 1,), (0,)), ((), ()))
    p1 = x.astype(BF16)
    r1 = x - p1.astype(F32)
    p2 = r1.astype(BF16)
    p3 = (r1 - p2.astype(F32)).astype(BF16)
    dot = lambda p: lax.dot_general(tri, p, dn, preferred_element_type=F32)
    return dot(p1) + (dot(p2) + dot(p3))


@jax.custom_vjp
def _cumsum_rows(x):
    return _tri_apply(x, False)


_cumsum_rows.defvjp(lambda x: (_tri_apply(x, False), None), lambda _, g: (_tri_apply(g, True),))


def _sigmoid(x):
    return 1.0 / (1.0 + jnp.exp(-x))


def _silu(x):
    return x * _sigmoid(x)


def _softplus(z):
    return jnp.maximum(z, 0.0) + jnp.log(1.0 + jnp.exp(-jnp.abs(z)))


def _mm(a, b, *, ta=False, tb=False, tm, tn, tk, name, out_dtype=F32, res=None, scale=None, comm=None):
    m = a.shape[1] if ta else a.shape[0]
    kdim = a.shape[0] if ta else a.shape[1]
    n = b.shape[0] if tb else b.shape[1]
    assert (b.shape[1] if tb else b.shape[0]) == kdim
    tm, tn, tk = min(tm, m), min(tn, n), min(tk, kdim)
    assert m % tm == 0 and n % tn == 0 and kdim % tk == 0, (name, m, n, kdim)
    nk = kdim // tk
    a_spec = pl.BlockSpec((tk, tm), lambda i, j, k: (k, i)) if ta else pl.BlockSpec((tm, tk), lambda i, j, k: (i, k))
    b_spec = pl.BlockSpec((tn, tk), lambda i, j, k: (j, k)) if tb else pl.BlockSpec((tk, tn), lambda i, j, k: (k, j))
    o_spec = pl.BlockSpec((tm, tn), lambda i, j, k: (i, j))
    ca, cb = (0 if ta else 1), (1 if tb else 0)

    def body(*refs):
        if res is not None:
            a_ref, b_ref, r_ref, o_ref, acc_ref = refs
        else:
            a_ref, b_ref, o_ref, acc_ref = refs
        k = pl.program_id(2)

        @pl.when(k == 0)
        def _():
            acc_ref[...] = jnp.zeros_like(acc_ref)

        acc_ref[...] += _dg(a_ref[...], b_ref[...], ca, cb, False)

        @pl.when(k == nk - 1)
        def _():
            acc = acc_ref[...]
            if scale is not None:
                acc = acc * scale
            if res is not None:
                acc = r_ref[...] + acc
            o_ref[...] = acc.astype(out_dtype)

    in_specs = [a_spec, b_spec] + ([o_spec] if res is not None else [])
    args = (a, b) + ((res,) if res is not None else ())
    if comm is None:
        return pl.pallas_call(
            body, name=name, grid=(m // tm, n // tn, nk), in_specs=in_specs, out_specs=o_spec,
            out_shape=SDS((m, n), out_dtype), scratch_shapes=[pltpu.VMEM((tm, tn), F32)],
            compiler_params=_params(("parallel", "parallel", "arbitrary")))(*args)
    (out,), carried = _hosting_call(
        body, comm, name=name, grid=(m // tm, n // tn, nk), in_specs=in_specs, out_specs=[o_spec],
        out_shape=[SDS((m, n), out_dtype)], scratch_shapes=[pltpu.VMEM((tm, tn), F32)], args=args)
    return out, carried


def _row_spec(x, tm):
    if isinstance(x, tuple):
        arr, w, j = x
        return arr, pl.BlockSpec((tm, w), lambda i, j=j: (i, j))
    return x, pl.BlockSpec((tm, x.shape[1]), lambda i: (i, 0))


def _par_spec(p):
    if isinstance(p, tuple):
        arr, w, j = p
        return arr, pl.BlockSpec((arr.shape[0], w), lambda i, j=j: (0, j))
    return p, pl.BlockSpec(p.shape, lambda i: (0, 0))


def _store_groups(refs, groups, vals):
    for ref, idxs in zip(refs, groups):
        off = 0
        for ix in idxs:
            v = vals[ix]
            ref[:, off:off + v.shape[1]] = v.astype(ref.dtype)
            off += v.shape[1]


SUBLANES = 8


def _x_plan(xs, tm, t):
    arrays, specs, plan = [], [], []
    nb = tm // SUBLANES
    for x in xs:
        if isinstance(x, tuple) and isinstance(x[0], str):
            kind, arr, w, j = x
            if kind == "prev":
                halo = lambda i, j=j: (jnp.maximum(i * nb - 1, 0), j)
            else:
                halo = lambda i, j=j: (jnp.minimum((i + 1) * nb, t // SUBLANES - 1), j)
            arrays += [arr, arr]
            specs += [pl.BlockSpec((tm, w), lambda i, j=j: (i, j)), pl.BlockSpec((SUBLANES, w), halo)]
            plan.append((kind, 2, w))
        else:
            arr, spec = _row_spec(x, tm)
            arrays.append(arr)
            specs.append(spec)
            plan.append(("plain", 1, spec.block_shape[1]))
    return arrays, specs, plan


def _x_vals(refs, plan, tm, nt):
    vals, k = [], 0
    i = pl.program_id(0)
    rows = lax.broadcasted_iota(jnp.int32, (tm, 1), 0)
    for kind, n, _ in plan:
        main = refs[k][...].astype(F32)
        if kind == "prev":
            edge = jnp.where(i == 0, 0.0, refs[k + 1][SUBLANES - 1:SUBLANES, :].astype(F32))
            main = jnp.where(rows == 0, edge, pltpu.roll(main, 1, 0))
        elif kind == "next":
            edge = jnp.where(i == nt - 1, 0.0, refs[k + 1][0:1, :].astype(F32))
            main = jnp.where(rows == tm - 1, edge, pltpu.roll(main, tm - 1, 0))
        vals.append(main)
        k += n
    return vals


def _tile_rows(xs, tm):
    arr = xs[0]
    if isinstance(arr, tuple):
        arr = arr[1] if isinstance(arr[0], str) else arr[0]
    return min(tm, arr.shape[0]), arr.shape[0]


def _rowwise(f, xs, params, out_groups, out_dtypes, *, tm, name):
    tm, t = _tile_rows(xs, tm)
    nt = t // tm
    xa, xspecs, plan = _x_plan(xs, tm, t)
    pa, pspecs = (zip(*[_par_spec(p) for p in params]) if params else ((), ()))
    nxr, npar = len(xa), len(pa)
    x_sds = [SDS((tm, w), F32) for _, _, w in plan]
    p_sds = [SDS(s.block_shape, F32) for s in pspecs]
    outs_sds = jax.eval_shape(lambda *vals: f(*vals), *x_sds, *p_sds)
    widths = [sum(outs_sds[ix].shape[1] for ix in idxs) for idxs in out_groups]

    def body(*refs):
        vals = _x_vals(refs[:nxr], plan, tm, nt) + [r[...].astype(F32) for r in refs[nxr:nxr + npar]]
        outs = f(*vals)
        _store_groups(refs[nxr + npar:], out_groups, outs)

    return pl.pallas_call(
        body, name=name, grid=(nt,), in_specs=list(xspecs) + list(pspecs),
        out_specs=[pl.BlockSpec((tm, w), lambda i: (i, 0)) for w in widths],
        out_shape=[SDS((t, w), dt) for w, dt in zip(widths, out_dtypes)],
        compiler_params=_params(("parallel",)))(*xa, *pa)


def _rowwise_bwd(f, xs, params, cots, *, x_grad, p_grad, dx_groups, dx_dtypes, tm, name, extra=None, comm=None):
    tm, t = _tile_rows(xs, tm)
    nt = t // tm
    xa, xspecs, plan = _x_plan(xs, tm, t)
    pa, pspecs = (zip(*[_par_spec(p) for p in params]) if params else ((), ()))
    ca, cspecs = zip(*[_row_spec(c, tm) for c in cots])
    extra = extra or {}
    ekeys = sorted(extra)
    ea, especs = (zip(*[_row_spec(extra[k], tm) for k in ekeys]) if ekeys else ((), ()))
    nx, nxr, npar, nc, ne = len(plan), len(xa), len(pa), len(ca), len(ea)
    gx = [i for i in range(nx) if x_grad[i]]
    gp = [i for i in range(npar) if p_grad[i]]
    widths = [sum(plan[gx[ix]][2] for ix in idxs) for idxs in dx_groups]
    ng = len(dx_groups)

    def body(*refs):
        ins = refs[:nxr + npar + nc + ne]
        outs = refs[nxr + npar + nc + ne:]
        vals = _x_vals(ins[:nxr], plan, tm, nt) + [r[...].astype(F32) for r in ins[nxr:nxr + npar]]
        cvals = tuple(r[...].astype(F32) for r in ins[nxr + npar:nxr + npar + nc])
        evals = [r[...].astype(F32) for r in ins[nxr + npar + nc:]]
        diff_idx = gx + [nx + i for i in gp]

        def g(*dargs):
            full = list(vals)
            for ix, v in zip(diff_idx, dargs):
                full[ix] = v
            return tuple(f(*full))

        _, vjp = jax.vjp(g, *[vals[ix] for ix in diff_idx])
        grads = vjp(cvals)
        dxs = list(grads[:len(gx)])
        for k, ev in zip(ekeys, evals):
            dxs[k] = dxs[k] + ev
        _store_groups(outs[:ng], dx_groups, dxs)
        i = pl.program_id(0)
        for ref, gval in zip(outs[ng:], grads[len(gx):]):
            @pl.when(i == 0)
            def _(ref=ref):
                ref[...] = jnp.zeros_like(ref)
            ref[...] += gval

    dp_specs = [pl.BlockSpec(pspecs[i].block_shape, lambda i: (0, 0)) for i in gp]
    dp_shapes = [SDS(pspecs[i].block_shape, F32) for i in gp]
    res, carried = _hosting_call(
        body, comm, name=name, grid=(nt,), in_specs=list(xspecs) + list(pspecs) + list(cspecs) + list(especs),
        out_specs=[pl.BlockSpec((tm, w), lambda i: (i, 0)) for w in widths] + dp_specs,
        out_shape=[SDS((t, w), dt) for w, dt in zip(widths, dx_dtypes)] + dp_shapes, scratch_shapes=[],
        args=(*xa, *pa, *ca, *ea))
    return res if comm is None else (res, carried)


def _rms_f(x, g):
    return (x * lax.rsqrt(jnp.mean(x * x, axis=-1, keepdims=True) + NORM_EPS) * g,)


def _three_pieces(x):
    p1 = x.astype(BF16)
    r1 = x - p1.astype(F32)
    p2 = r1.astype(BF16)
    return p1, p2, (r1 - p2.astype(F32)).astype(BF16)


def _group_sum_impl(x, ones_bd):
    p1, p2, p3 = _three_pieces(x)
    dot = lambda p: lax.dot_general(p, ones_bd.astype(BF16), (((1,), (0,)), ((), ())), preferred_element_type=F32)
    return dot(p1) + (dot(p2) + dot(p3))


@jax.custom_vjp
def _group_sum(x, ones_bd):
    return _group_sum_impl(x, ones_bd)


_group_sum.defvjp(lambda x, o: (_group_sum_impl(x, o), o),
                  lambda o, g: (_group_sum_impl(g, o), jnp.zeros_like(o)))


def _rwkv_prep_f(r, k, v, lo, rp, kp, vp, lop, mu_r, mu_k, mu_v, mu_lo, w0, w2p, a0, a2p, g2p, k_k, k_a, ones_bd):
    r = r + mu_r * (rp - r)
    k = k + mu_k * (kp - k)
    v = v + mu_v * (vp - v)
    lo = lo + mu_lo * (lop - lo)
    w_log = -_softplus(-(w0 + _nn(jnp.tanh(lo), w2p))) - 0.5
    lw = -jnp.exp(w_log)
    a_g = _sigmoid(a0 + _nn(lo, a2p))
    g = _nn(_sigmoid(lo), g2p)
    kk = k * k_k
    kk = kk / jnp.maximum(jnp.sqrt(_group_sum(kk * kk, ones_bd)), L2_EPS)
    k2 = k * (1.0 + (a_g - 1.0) * k_a)
    return r, lw, k2, v, -kk, kk * a_g, g


def _rwkv_post_f(y, r, k2, v, g, r_k, gn_w, gn_b, ones_bd):
    inv_n = 1.0 / HB_DIM
    mean = _group_sum(y, ones_bd) * inv_n
    yc = y - mean
    var = _group_sum(yc * yc, ones_bd) * inv_n
    yn = yc * lax.rsqrt(var + RWKV_GN_EPS) * gn_w + gn_b
    bonus = _group_sum(r * k2 * r_k, ones_bd) * v
    return ((yn + bonus) * g,)


def _tri(c, strict=False):
    ii = lax.broadcasted_iota(jnp.int32, (c, c), 0)
    jj = lax.broadcasted_iota(jnp.int32, (c, c), 1)
    return (jj < ii) if strict else (jj <= ii)


def _hgrn_step(st0, q_a, f_a, i_a, g_a, l0, l1, onorm):
    nh, nj = len(q_a), len(q_a[0])
    c = q_a[0][0].shape[0]
    combos = [(j, h) for j in range(nj) for h in range(nh)]
    every = lambda fn: {q: fn(q) for q in combos}
    at_ = lambda d: (lambda q: d[q[1]][q[0]])
    qa_, fa_, ia_, ga_ = (at_(z) for z in (q_a, f_a, i_a, g_a))
    incl = _tri(c)
    rows = lax.broadcasted_iota(jnp.int32, (c, 1), 0)
    lb = []
    for h in range(nh):
        mx = jnp.maximum(l0[h], l1[h])
        e0, e1 = jnp.exp(l0[h] - mx), jnp.exp(l1[h] - mx)
        lb.append(e0 / (e0 + e1))
    forget = every(lambda q: lb[q[1]] + (1.0 - lb[q[1]]) * _sigmoid(fa_(q)))
    qs = every(lambda q: _silu(qa_(q)))
    kk = every(lambda q: 1.0 - forget[q])
    lf = every(lambda q: jnp.log(forget[q]))
    bcum = every(lambda q: _cumsum_rows(lf[q]))
    bref = every(lambda q: jnp.sum(jnp.where(rows <= c // 2, lf[q], 0.0), axis=0, keepdims=True))
    blast = every(lambda q: jnp.sum(lf[q], axis=0, keepdims=True))
    scores = every(lambda q: jnp.where(incl, _nt(qs[q] * jnp.exp(bcum[q] - bref[q]),
                                                 kk[q] * jnp.exp(bref[q] - bcum[q])), 0.0))
    intra = every(lambda q: _nn(scores[q], ia_(q)))
    qb = every(lambda q: qs[q] * jnp.exp(bcum[q]))
    upd = every(lambda q: _tn(ia_(q), kk[q] * jnp.exp(blast[q] - bcum[q])))
    dec = every(lambda q: jnp.exp(blast[q]))
    st = list(st0)
    o = {}
    for j in range(nj):
        for h in range(nh):
            o[(j, h)] = intra[(j, h)] + _nt(qb[(j, h)], st[h])
        st = [st[h] * dec[(j, h)] + upd[(j, h)] for h in range(nh)]
    out = every(lambda q: o[q] * lax.rsqrt(jnp.mean(o[q] * o[q], axis=-1, keepdims=True) + NORM_EPS)
                * onorm[q[1]] * _silu(ga_(q)))
    return [[out[(j, h)] for j in range(nj)] for h in range(nh)], st


def _hgrn_blocks(ref, nj, c):
    return [[ref[j * c:(j + 1) * c, h * HA_DIM:(h + 1) * HA_DIM] for j in range(nj)] for h in range(HA_HEADS)]


def _hgrn_cols(ref):
    return [ref[:, h * HA_DIM:(h + 1) * HA_DIM] for h in range(HA_HEADS)]


def _hgrn_fwd(p_h, l0, l1, onorm):
    t = p_h.shape[0]
    cc, nj = HGRN_CHUNK, HGRN_GROUP
    c = cc * nj
    n = t // c

    def body(q_ref, f_ref, i_ref, g_ref, l0_ref, l1_ref, on_ref, o_ref, hs_ref, st_ref):
        @pl.when(pl.program_id(0) == 0)
        def _():
            st_ref[...] = jnp.zeros_like(st_ref)

        hs_ref[0] = st_ref[...]
        o, st1 = _hgrn_step([st_ref[h] for h in range(HA_HEADS)],
                            *[_hgrn_blocks(ref, nj, cc) for ref in (q_ref, f_ref, i_ref, g_ref)],
                            _hgrn_cols(l0_ref), _hgrn_cols(l1_ref), _hgrn_cols(on_ref))
        for h in range(HA_HEADS):
            for j in range(nj):
                o_ref[j * cc:(j + 1) * cc, h * HA_DIM:(h + 1) * HA_DIM] = o[h][j]
            st_ref[h] = st1[h]

    col = lambda j: pl.BlockSpec((c, W_A), lambda i, j=j: (i, j))
    par = pl.BlockSpec((1, W_A), lambda i: (0, 0))
    return pl.pallas_call(
        body, name="hgrn_fwd", grid=(n,), in_specs=[col(0), col(1), col(2), col(3), par, par, par],
        out_specs=[pl.BlockSpec((c, W_A), lambda i: (i, 0)),
                   pl.BlockSpec((1, HA_HEADS, HA_DIM, HA_DIM), lambda i: (i, 0, 0, 0))],
        out_shape=[SDS((t, W_A), F32), SDS((n, HA_HEADS, HA_DIM, HA_DIM), F32)],
        scratch_shapes=[pltpu.VMEM((HA_HEADS, HA_DIM, HA_DIM), F32)],
        compiler_params=_params(("arbitrary",)))(p_h, p_h, p_h, p_h, l0, l1, onorm)


def _hgrn_bwd(p_h, l0, l1, onorm, hs, do, do_col, comm=None):
    t = p_h.shape[0]
    cc, nj = HGRN_CHUNK, HGRN_GROUP
    c = cc * nj
    n = t // c

    def body(q_ref, f_ref, i_ref, g_ref, l0_ref, l1_ref, on_ref, hs_ref, do_ref,
             dp_ref, dl0_ref, dl1_ref, don_ref, dst_ref):
        @pl.when(pl.program_id(0) == 0)
        def _():
            dst_ref[...] = jnp.zeros_like(dst_ref)
            dl0_ref[...] = jnp.zeros_like(dl0_ref)
            dl1_ref[...] = jnp.zeros_like(dl1_ref)
            don_ref[...] = jnp.zeros_like(don_ref)

        args = ([hs_ref[0, h] for h in range(HA_HEADS)],
                *[_hgrn_blocks(ref, nj, cc) for ref in (q_ref, f_ref, i_ref, g_ref)],
                _hgrn_cols(l0_ref), _hgrn_cols(l1_ref), _hgrn_cols(on_ref))
        _, vjp = jax.vjp(_hgrn_step, *args)
        dst0, dq, df, di, dg, dl0, dl1, don = vjp((_hgrn_blocks(do_ref, nj, cc),
                                                   [dst_ref[h] for h in range(HA_HEADS)]))
        for h in range(HA_HEADS):
            sl = slice(h * HA_DIM, (h + 1) * HA_DIM)
            for k, dv in enumerate((dq, df, di, dg)):
                for j in range(nj):
                    dp_ref[j * cc:(j + 1) * cc, k * W_A + h * HA_DIM:k * W_A + (h + 1) * HA_DIM] = dv[h][j]
            dl0_ref[:, sl] += dl0[h]
            dl1_ref[:, sl] += dl1[h]
            don_ref[:, sl] += don[h]
            dst_ref[h] = dst0[h]

    col = lambda j: pl.BlockSpec((c, W_A), lambda i, j=j: (n - 1 - i, j))
    par = pl.BlockSpec((1, W_A), lambda i: (0, 0))
    return _hosting_call(
        body, comm, name="hgrn_bwd", grid=(n,),
        in_specs=[col(0), col(1), col(2), col(3), par, par, par,
                  pl.BlockSpec((1, HA_HEADS, HA_DIM, HA_DIM), lambda i: (n - 1 - i, 0, 0, 0)),
                  pl.BlockSpec((c, W_A), lambda i: (n - 1 - i, do_col))],
        out_specs=[pl.BlockSpec((c, N_HGRN_COLS), lambda i: (n - 1 - i, 0)), par, par, par],
        out_shape=[SDS((t, N_HGRN_COLS), F32), SDS((1, W_A), F32), SDS((1, W_A), F32), SDS((1, W_A), F32)],
        scratch_shapes=[pltpu.VMEM((HA_HEADS, HA_DIM, HA_DIM), F32)],
        args=(p_h, p_h, p_h, p_h, l0, l1, onorm, hs, do))


HB_PAIRS = HB_HEADS // 2
PAIR_W = 2 * HB_DIM


def _head_lane_masks():
    lane = lax.broadcasted_iota(jnp.int32, (1, PAIR_W), 1)
    return (lane < HB_DIM).astype(F32), (lane >= HB_DIM).astype(F32)


@jax.custom_vjp
def _stack_heads(x):
    m0, m1 = _head_lane_masks()
    return jnp.concatenate([x * m0, x * m1], axis=0)


def _stack_heads_bwd(_, g):
    m0, m1 = _head_lane_masks()
    c = g.shape[0] // 2
    return (g[:c] * m0 + g[c:] * m1,)


_stack_heads.defvjp(lambda x: (_stack_heads(x), None), _stack_heads_bwd)


@jax.custom_vjp
def _unstack_heads(ys):
    c = ys.shape[0] // 2
    return ys[:c] + ys[c:]


_unstack_heads.defvjp(lambda ys: (_unstack_heads(ys), None), lambda _, g: (_stack_heads(g),))


def _same_head_block(c):
    ii = lax.broadcasted_iota(jnp.int32, (2 * c, 2 * c), 0)
    jj = lax.broadcasted_iota(jnp.int32, (2 * c, 2 * c), 1)
    same = (ii < c) == (jj < c)
    return same & (jj <= ii), same & (jj < ii), (ii == jj).astype(F32)


@jax.custom_vjp
def _rows_join(top, bottom):
    return jnp.concatenate([top, bottom], axis=0)


def _rows_join_bwd(n_top, g):
    return g[:n_top], g[n_top:]


_rows_join.defvjp(lambda top, bottom: (_rows_join(top, bottom), top.shape[0]), _rows_join_bwd)


def _rows_split_impl(x, n_top):
    return x[:n_top], x[n_top:]


_rows_split = jax.custom_vjp(_rows_split_impl, nondiff_argnums=(1,))
_rows_split.defvjp(lambda x, n_top: (_rows_split_impl(x, n_top), None),
                   lambda n_top, _, g: (jnp.concatenate([g[0], g[1]], axis=0),))


def _rwkv_step(s0, r, lw, k, v, a, b):
    npair, nj = len(r), len(r[0])
    c = r[0][0].shape[0]
    combos = [(j, p) for j in range(nj) for p in range(npair)]
    every = lambda fn: {q: fn(q) for q in combos}
    at_ = lambda d: (lambda q: d[q[1]][q[0]])
    r_, lw_, k_, v_, a_, b_ = (at_(z) for z in (r, lw, k, v, a, b))
    incl, strict, eye = _same_head_block(c)

    gam = every(lambda q: _cumsum_rows(lw_(q)))
    gtot = every(lambda q: jnp.sum(lw_(q), axis=0, keepdims=True))
    eneg = every(lambda q: jnp.exp(-gam[q]))
    edec = every(lambda q: jnp.exp(gtot[q] - gam[q]))
    at = every(lambda q: _stack_heads(a_(q) * jnp.exp(gam[q] - lw_(q))))
    rt = every(lambda q: _stack_heads(r_(q) * jnp.exp(gam[q])))
    bt = every(lambda q: _stack_heads(b_(q) * eneg[q]))
    kt = every(lambda q: _stack_heads(k_(q) * eneg[q]))
    bdec = every(lambda q: _stack_heads(b_(q) * edec[q]))
    kdec = every(lambda q: _stack_heads(k_(q) * edec[q]))
    vs = every(lambda q: _stack_heads(v_(q)))
    a_ab = every(lambda q: jnp.where(strict, _nt(at[q], bt[q]), 0.0))
    a_ak = every(lambda q: jnp.where(strict, _nt(at[q], kt[q]), 0.0))
    a_rb = every(lambda q: jnp.where(incl, _nt(rt[q], bt[q]), 0.0))
    a_rk = every(lambda q: jnp.where(incl, _nt(rt[q], kt[q]), 0.0))
    tinv = every(lambda q: eye + a_ab[q])
    pw = a_ab
    span = 2
    while span < c:
        pw = every(lambda q, pw=pw: _nn_x3(pw[q], pw[q]))
        tinv = every(lambda q, pw=pw, tinv=tinv: tinv[q] + _nn_x3(pw[q], tinv[q]))
        span *= 2
    akv = every(lambda q: _nn(a_ak[q], vs[q]))
    w1 = every(lambda q: _nn_x3(tinv[q], at[q]))
    u0 = every(lambda q: _nn_x3(tinv[q], akv[q]))
    wr = every(lambda q: _rows_join(w1[q], rt[q]))
    bk = every(lambda q: _rows_join(bdec[q], kdec[q]))
    yv = every(lambda q: _nn(a_rk[q], vs[q]))
    gdec = every(lambda q: jnp.exp(gtot[q]))

    s = list(s0)
    y = [[None] * nj for _ in range(npair)]
    for j in range(nj):
        both = {p: _rows_split(_nt(wr[(j, p)], s[p]), 2 * c) for p in range(npair)}
        u = {p: both[p][0] + u0[(j, p)] for p in range(npair)}
        for p in range(npair):
            y[p][j] = _unstack_heads(both[p][1] + _nn(a_rb[(j, p)], u[p]) + yv[(j, p)])
        s = [s[p] * gdec[(j, p)] + _tn(_rows_join(u[p], vs[(j, p)]), bk[(j, p)]) for p in range(npair)]
    return y, s


def _rwkv_blocks(ref, nj, c):
    return [[ref[j * c:(j + 1) * c, p * PAIR_W:(p + 1) * PAIR_W] for j in range(nj)] for p in range(HB_PAIRS)]


def _rwkv_fwd(seqs, comm=None):
    t = seqs[0].shape[0]
    c, nj = RWKV_CHUNK, RWKV_GROUP
    n = t // (c * nj)

    def body(r_ref, lw_ref, k_ref, v_ref, a_ref, b_ref, y_ref, hs_ref, st_ref):
        @pl.when(pl.program_id(0) == 0)
        def _():
            st_ref[...] = jnp.zeros_like(st_ref)

        hs_ref[0] = st_ref[...]
        s0 = [st_ref[p] for p in range(HB_PAIRS)]
        y, s1 = _rwkv_step(s0, *[_rwkv_blocks(ref, nj, c) for ref in (r_ref, lw_ref, k_ref, v_ref, a_ref, b_ref)])
        for p in range(HB_PAIRS):
            for j in range(nj):
                y_ref[j * c:(j + 1) * c, p * PAIR_W:(p + 1) * PAIR_W] = y[p][j]
            st_ref[p] = s1[p]

    seq = pl.BlockSpec((c * nj, W_B), lambda i: (i, 0))
    return _hosting_call(
        body, comm, name="rwkv_fwd", grid=(n,), in_specs=[seq] * 6,
        out_specs=[seq, pl.BlockSpec((1, HB_PAIRS, PAIR_W, PAIR_W), lambda i: (i, 0, 0, 0))],
        out_shape=[SDS((t, W_B), F32), SDS((n, HB_PAIRS, PAIR_W, PAIR_W), F32)],
        scratch_shapes=[pltpu.VMEM((HB_PAIRS, PAIR_W, PAIR_W), F32)], args=tuple(seqs))


def _rwkv_bwd(seqs, hs, dy, comm=None):
    t = seqs[0].shape[0]
    c, nj = RWKV_CHUNK, RWKV_GROUP
    n = t // (c * nj)

    def body(r_ref, lw_ref, k_ref, v_ref, a_ref, b_ref, hs_ref, dy_ref,
             dr_ref, dlw_ref, dk_ref, dv_ref, da_ref, db_ref, dst_ref):
        @pl.when(pl.program_id(0) == 0)
        def _():
            dst_ref[...] = jnp.zeros_like(dst_ref)

        s0 = [hs_ref[0, p] for p in range(HB_PAIRS)]
        seq_vals = [_rwkv_blocks(ref, nj, c) for ref in (r_ref, lw_ref, k_ref, v_ref, a_ref, b_ref)]
        _, vjp = jax.vjp(_rwkv_step, s0, *seq_vals)
        grads = vjp((_rwkv_blocks(dy_ref, nj, c), [dst_ref[p] for p in range(HB_PAIRS)]))
        for ref, gr in zip((dr_ref, dlw_ref, dk_ref, dv_ref, da_ref, db_ref), grads[1:]):
            for p in range(HB_PAIRS):
                for j in range(nj):
                    ref[j * c:(j + 1) * c, p * PAIR_W:(p + 1) * PAIR_W] = gr[p][j]
        m0, m1 = _head_lane_masks()
        rows0 = (lax.broadcasted_iota(jnp.int32, (PAIR_W, 1), 0) < HB_DIM).astype(F32)
        blocks = rows0 * m0 + (1.0 - rows0) * m1
        for p in range(HB_PAIRS):
            dst_ref[p] = grads[0][p] * blocks

    seq = pl.BlockSpec((c * nj, W_B), lambda i: (n - 1 - i, 0))
    return _hosting_call(
        body, comm, name="rwkv_bwd", grid=(n,),
        in_specs=[seq] * 6 + [pl.BlockSpec((1, HB_PAIRS, PAIR_W, PAIR_W), lambda i: (n - 1 - i, 0, 0, 0)), seq],
        out_specs=[seq] * 6, out_shape=[SDS((t, W_B), F32)] * 6,
        scratch_shapes=[pltpu.VMEM((HB_PAIRS, PAIR_W, PAIR_W), F32)], args=(*seqs, hs, dy))


def _final_loss(x3, fnorm, target, *, tm):
    t, d = x3.shape

    def body(x_ref, g_ref, t_ref, dx_ref, dg_ref, loss_ref):
        @pl.when(pl.program_id(0) == 0)
        def _():
            dg_ref[...] = jnp.zeros_like(dg_ref)
            loss_ref[...] = jnp.zeros_like(loss_ref)

        x, g = x_ref[...], g_ref[...]
        rinv = lax.rsqrt(jnp.mean(x * x, axis=-1, keepdims=True) + NORM_EPS)
        xh = x * rinv
        diff = xh * g - t_ref[...]
        loss_ref[...] += 0.5 * jnp.sum(jnp.mean(diff * diff, axis=-1, keepdims=True))
        dy = diff * (1.0 / d)
        dg_ref[...] += jnp.sum(dy * xh, axis=0, keepdims=True)
        dxh = dy * g
        dx_ref[...] = rinv * (dxh - xh * jnp.mean(dxh * xh, axis=-1, keepdims=True))

    row = pl.BlockSpec((tm, d), lambda i: (i, 0))
    return pl.pallas_call(
        body, name="final_loss", grid=(t // tm,), in_specs=[row, pl.BlockSpec((1, d), lambda i: (0, 0)), row],
        out_specs=[row, pl.BlockSpec((1, d), lambda i: (0, 0)), pl.BlockSpec((8, 128), lambda i: (0, 0))],
        out_shape=[SDS((t, d), F32), SDS((1, d), F32), SDS((8, 128), F32)],
        compiler_params=_params(("arbitrary",)))(x3, fnorm, target)


def _gate_up_act(h, wgt, wut, *, tm, tn, name, comm=None):
    t, d = h.shape
    tm = min(tm, t)

    def body(h_ref, g_ref, u_ref, a_out, u_out, act_out):
        hv = h_ref[...]
        a = _dg(hv, g_ref[...], 1, 1, False)
        u = _dg(hv, u_ref[...], 1, 1, False)
        a_out[...] = a.astype(a_out.dtype)
        u_out[...] = u.astype(u_out.dtype)
        act_out[...] = (_silu(a) * u).astype(act_out.dtype)

    wspec = pl.BlockSpec((tn, d), lambda i, j: (j, 0))
    ospec = pl.BlockSpec((tm, tn), lambda i, j: (i, j))
    return _hosting_call(
        body, comm, name=name, grid=(t // tm, D_FF // tn),
        in_specs=[pl.BlockSpec((tm, d), lambda i, j: (i, 0)), wspec, wspec], out_specs=[ospec, ospec, ospec],
        out_shape=[SDS((t, D_FF), BF16), SDS((t, D_FF), BF16), SDS((t, D_FF), BF16)], scratch_shapes=[],
        args=(h, wgt, wut))


def _dact_swiglu(dout, wd, a, u, *, tm, tn, name, comm=None):
    t, d = dout.shape
    tm = min(tm, t)

    def body(d_ref, w_ref, a_ref, u_ref, da_out, du_out):
        dact = 0.5 * _dg(d_ref[...], w_ref[...], 1, 1, False)
        av, uv = a_ref[...].astype(F32), u_ref[...].astype(F32)
        s = _sigmoid(av)
        da_out[...] = (dact * uv * (s * (1.0 + av * (1.0 - s)))).astype(da_out.dtype)
        du_out[...] = (dact * (av * s)).astype(du_out.dtype)

    tile = pl.BlockSpec((tm, tn), lambda i, j: (i, j))
    return _hosting_call(
        body, comm, name=name, grid=(t // tm, D_FF // tn),
        in_specs=[pl.BlockSpec((tm, d), lambda i, j: (i, 0)), pl.BlockSpec((tn, d), lambda i, j: (j, 0)), tile, tile],
        out_specs=[tile, tile], out_shape=[SDS((t, D_FF), BF16), SDS((t, D_FF), BF16)], scratch_shapes=[],
        args=(dout, wd, a, u))


class _Plan:
    def __init__(self):
        self.entries, self.counts = collections.defaultdict(list), {}

    def carry(self, host, comm_of, after):
        self.entries[host].append((comm_of, after))

    def comm(self, host, g):
        comms = [comm_of(g) for comm_of, _ in self.entries.get(host, [])]
        self.counts[host] = [len(c.arrays) for c in comms]
        return functools.reduce(_join_comms, comms) if comms else None

    def done(self, host, results, w):
        start = 0
        for (_, after), n in zip(self.entries.get(host, []), self.counts.get(host, [])):
            after(results[start:start + n], w)
            start += n


def _ffn_fwd(x, w, tag, plan, g):
    h, = _rowwise(_rms_f, [x], [w[f"{tag}_norm"]], [[0]], [BF16], tm=512, name=f"{tag}_rms")
    (a, u, act), carried = _gate_up_act(h, w[f"{tag}_wgt"], w[f"{tag}_wut"], tm=2048, tn=256, name=f"{tag}_gate_up",
                                        comm=plan.comm(f"{tag}_gate_up", g))
    plan.done(f"{tag}_gate_up", carried, w)
    comm = plan.comm(f"{tag}_down", g)
    out = _mm(act, w[f"{tag}_wd"], tm=1024, tn=512, tk=D_FF, name=f"{tag}_down", res=x, scale=0.5, comm=comm)
    if comm is not None:
        out, carried = out
        plan.done(f"{tag}_down", carried, w)
    return out, (h, a, u, act)


def _ffn_bwd(dout, x, w, saved, tag, plan, g):
    h, a, u, act = saved

    def carrying(fn, host, *args, **kwargs):
        comm = plan.comm(host, g)
        res = fn(*args, name=host, comm=comm, **kwargs)
        out, carried = res if comm is not None else (res, [])
        plan.done(host, carried, w)
        return out

    (da, du), carried = _dact_swiglu(dout, w[f"{tag}_wd"], a, u, tm=2048, tn=256, name=f"{tag}_dact",
                                     comm=plan.comm(f"{tag}_dact", g))
    plan.done(f"{tag}_dact", carried, w)
    g[f"{tag}_wd"] = _mm(act, dout, ta=True, tm=D_FF // 2, tn=D_MODEL, tk=1024, name=f"{tag}_dwd", scale=0.5)
    g[f"{tag}_wgt"] = carrying(_mm, f"{tag}_dwg", da, h, ta=True, tm=D_FF // 2, tn=D_MODEL, tk=1024)
    g[f"{tag}_wut"] = carrying(_mm, f"{tag}_dwu", du, h, ta=True, tm=D_FF // 2, tn=D_MODEL, tk=1024)
    dh = carrying(_mm, f"{tag}_dh_g", da, w[f"{tag}_wgt"], tm=1024, tn=512, tk=D_FF)
    dh = carrying(_mm, f"{tag}_dh_u", du, w[f"{tag}_wut"], tm=1024, tn=512, tk=D_FF, res=dh)
    dx, g[f"{tag}_norm"] = _rowwise_bwd(_rms_f, [x], [w[f"{tag}_norm"]], [dh], x_grad=[True], p_grad=[True],
                                        dx_groups=[[0]], dx_dtypes=[F32], tm=512, name=f"{tag}_drms",
                                        extra={0: dout})
    return dx


def _local_step(x, target, w, plan=None):
    plan = plan or _Plan()
    ones_bd = jnp.kron(jnp.eye(HB_HEADS, dtype=F32), jnp.ones((HB_DIM, HB_DIM), F32))
    g = {}
    x1, ffn1_saved = _ffn_fwd(x, w, "ffn1", plan, g)
    hm, = _rowwise(_rms_f, [x1], [w["mix_norm"]], [[0]], [BF16], tm=512, name="mix_rms")
    p_h = _mm(hm, w["w_in_h"], tm=2048, tn=256, tk=D_MODEL, name="inproj_h")
    p_r = _mm(hm, w["w_in_r"], tm=2048, tn=256, tk=D_MODEL, name="inproj_r")
    o_a, hgrn_states = _hgrn_fwd(p_h, w["lb0"], w["lb1"], w["hgrn_out_norm"])

    mu = w["mu_pad"]
    prep_xs = [(p_r, W_B, 0), (p_r, W_B, 1), (p_r, W_B, 2), (p_r, LORA_PAD, 6),
               ("prev", p_r, W_B, 0), ("prev", p_r, W_B, 1), ("prev", p_r, W_B, 2), ("prev", p_r, LORA_PAD, 6)]
    prep_ps = [(mu, W_B, 0), (mu, W_B, 1), (mu, W_B, 2), (mu, LORA_PAD, 6), w["rwkv_w0"], w["w2_pad"], w["rwkv_a0"],
               w["a2_pad"], w["g2_pad"], w["rwkv_k_k"], w["rwkv_k_a"], ones_bd]
    prep_f = _rwkv_prep_f
    r, lw, k2, v, a_vec, b_vec, gate = _rowwise(prep_f, prep_xs, prep_ps, [[0], [1], [2], [3], [4], [5], [6]],
                                                [F32] * 7, tm=256, name="rwkv_prep")
    seqs = [r, lw, k2, v, a_vec, b_vec]
    (y, rwkv_states), carried = _rwkv_fwd(seqs, comm=plan.comm("rwkv_fwd", g))
    plan.done("rwkv_fwd", carried, w)
    post_f = _rwkv_post_f
    post_xs = [y, r, k2, v, gate]
    post_ps = [w["rwkv_r_k"], w["rwkv_gn_w"], w["rwkv_gn_b"], ones_bd]
    o_b, = _rowwise(post_f, post_xs, post_ps, [[0]], [F32], tm=256, name="rwkv_post")
    x2 = _mm(o_a, w["w_out_a"], tm=2048, tn=256, tk=W_A, name="outproj_a", res=x1)
    x2 = _mm(o_b, w["w_out_b"], tm=2048, tn=256, tk=W_B, name="outproj_b", res=x2)
    x3, ffn2_saved = _ffn_fwd(x2, w, "ffn2", plan, g)
    dx3, g["final_norm"], loss = _final_loss(x3, w["final_norm"], target, tm=256)

    dx2 = _ffn_bwd(dx3, x2, w, ffn2_saved, "ffn2", plan, g)
    do_a = _mm(dx2, w["w_out_a"], tb=True, tm=2048, tn=256, tk=D_MODEL, name="outproj_do_a")
    do_b = _mm(dx2, w["w_out_b"], tb=True, tm=2048, tn=256, tk=D_MODEL, name="outproj_do_b")
    g["w_out_a"] = _mm(o_a, dx2, ta=True, tm=W_A, tn=D_MODEL, tk=1024, name="outproj_dw_a")
    g["w_out_b"] = _mm(o_b, dx2, ta=True, tm=W_B, tn=D_MODEL, tk=1024, name="outproj_dw_b")

    (dp_h, g["lb0"], g["lb1"], g["hgrn_out_norm"]), carried = _hgrn_bwd(
        p_h, w["lb0"], w["lb1"], w["hgrn_out_norm"], hgrn_states, do_a, 0, comm=plan.comm("hgrn_bwd", g))
    plan.done("hgrn_bwd", carried, w)
    post_out = _rowwise_bwd(post_f, post_xs, post_ps, [do_b], x_grad=[True] * 5, p_grad=[True] * 3 + [False],
                            dx_groups=[[0], [1], [2], [3], [4]], dx_dtypes=[F32] * 5, tm=256, name="rwkv_post_bwd")
    dy, dr1, dk1, dv1, dgate, g["rwkv_r_k"], g["rwkv_gn_w"], g["rwkv_gn_b"] = post_out
    (dr2, dlw, dk2, dv2, da_vec, db_vec), carried = _rwkv_bwd(seqs, rwkv_states, dy, comm=plan.comm("rwkv_bwd", g))
    plan.done("rwkv_bwd", carried, w)

    def prep2_f(*vals):
        r_, lw_, k2_, v_, a_, b_, g_ = prep_f(*vals)
        return r_, lw_, k2_, v_, a_, b_, g_, r_, k2_, v_

    prep_out = _rowwise_bwd(prep2_f, prep_xs, prep_ps, [dr2, dlw, dk2, dv2, da_vec, db_vec, dgate, dr1, dk1, dv1],
                            x_grad=[True] * 8, p_grad=[True] * 11 + [False], dx_groups=[[0, 1, 2, 3], [4, 5, 6, 7]],
                            dx_dtypes=[F32, F32], tm=256, name="rwkv_prep_bwd")
    dpr_main, dpr_prev = prep_out[0], prep_out[1]
    (dmu_r, dmu_k, dmu_v, dmu_lo, g["rwkv_w0"], g["w2_pad"], g["rwkv_a0"], g["a2_pad"], g["g2_pad"],
     g["rwkv_k_k"], g["rwkv_k_a"]) = prep_out[2:]
    g["mu_pad"] = jnp.concatenate([dmu_r, dmu_k, dmu_v, dmu_lo], axis=1)
    dp_r, = _rowwise(lambda u_, s_: (u_ + s_,), [dpr_main, ("next", dpr_prev, N_RWKV_PAD, 0)], [], [[0]], [F32],
                     tm=512, name="rwkv_dp_sum")
    dhm = _mm(dp_h, w["w_in_h"], tb=True, tm=1024, tn=512, tk=N_HGRN_COLS, name="inproj_dh_h")
    dhm = _mm(dp_r, w["w_in_r"], tb=True, tm=1024, tn=512, tk=N_RWKV_PAD, name="inproj_dh_r", res=dhm)
    g["w_in_h"] = _mm(hm, dp_h, ta=True, tm=D_MODEL, tn=D_MODEL, tk=1024, name="inproj_dw_h")
    g["w_in_r"] = _mm(hm, dp_r, ta=True, tm=D_MODEL, tn=N_RWKV_PAD // 2, tk=1024, name="inproj_dw_r")
    mix_comm = plan.comm("mix_drms", g)
    mix_out = _rowwise_bwd(_rms_f, [x1], [w["mix_norm"]], [dhm], x_grad=[True], p_grad=[True], dx_groups=[[0]],
                           dx_dtypes=[F32], tm=512, name="mix_drms", extra={0: dx2}, comm=mix_comm)
    (dx1, g["mix_norm"]), carried = mix_out if mix_comm is not None else (mix_out, [])
    plan.done("mix_drms", carried, w)
    dx0 = _ffn_bwd(dx1, x, w, ffn1_saved, "ffn1", plan, g)
    return loss, dx0, g


HBM_SPEC = pl.BlockSpec(memory_space=pl.ANY)

Comm = collections.namedtuple("Comm", "arrays out_shapes aliased sem_shapes start finish")


def _join_comms(first, second):
    assert first.aliased == second.aliased
    n, s = len(first.arrays), len(first.sem_shapes)

    def start(ins, outs, sems):
        first.start(ins[:n], outs[:n], sems[:s])
        second.start(ins[n:], outs[n:], sems[s:])

    def finish(ins, outs, sems):
        first.finish(ins[:n], outs[:n], sems[:s])
        second.finish(ins[n:], outs[n:], sems[s:])

    return Comm(list(first.arrays) + list(second.arrays), list(first.out_shapes) + list(second.out_shapes),
                first.aliased, list(first.sem_shapes) + list(second.sem_shapes), start, finish)


def _run_comm(comm, name):
    n = len(comm.arrays)

    def body(*refs):
        ins, outs, sems = refs[:n], refs[n:2 * n], refs[2 * n:]
        comm.start(ins, outs, sems)
        comm.finish(ins, outs, sems)

    return pl.pallas_call(
        body, name=name, in_specs=[HBM_SPEC] * n, out_specs=[HBM_SPEC] * n, out_shape=list(comm.out_shapes),
        input_output_aliases={t: t for t in range(n)} if comm.aliased else {},
        scratch_shapes=list(comm.sem_shapes))(*comm.arrays)


def _hosting_call(body, comm, *, name, grid, in_specs, out_specs, out_shape, scratch_shapes, args):
    sem = ("arbitrary",) * len(grid)
    if comm is None:
        res = pl.pallas_call(body, name=name, grid=grid, in_specs=in_specs, out_specs=out_specs, out_shape=out_shape,
                             scratch_shapes=scratch_shapes, compiler_params=_params(sem))(*args)
        return list(res), []
    ni, no, ns, nc = len(in_specs), len(out_specs), len(scratch_shapes), len(comm.arrays)

    def wrapped(*refs):
        ins, cins = refs[:ni], refs[ni:ni + nc]
        outs, couts = refs[ni + nc:ni + nc + no], refs[ni + nc + no:ni + 2 * nc + no]
        scr, sems = refs[ni + 2 * nc + no:ni + 2 * nc + no + ns], refs[ni + 2 * nc + no + ns:]
        first = functools.reduce(jnp.logical_and, [pl.program_id(k) == 0 for k in range(len(grid))])
        last = functools.reduce(jnp.logical_and, [pl.program_id(k) == grid[k] - 1 for k in range(len(grid))])

        @pl.when(first)
        def _():
            comm.start(cins, couts, sems)

        body(*ins, *outs, *scr)

        @pl.when(last)
        def _():
            comm.finish(cins, couts, sems)

    res = pl.pallas_call(
        wrapped, name=name, grid=grid, in_specs=list(in_specs) + [HBM_SPEC] * nc,
        out_specs=list(out_specs) + [HBM_SPEC] * nc, out_shape=list(out_shape) + list(comm.out_shapes),
        scratch_shapes=list(scratch_shapes) + list(comm.sem_shapes),
        input_output_aliases={ni + t: no + t for t in range(nc)} if comm.aliased else {},
        compiler_params=_params(sem))(*args, *comm.arrays)
    return list(res[:no]), list(res[no:])


def _chips(x, y):
    return [(1 - x, y), (x, 1 - y), (1 - x, 1 - y)]


def _gather_comm(bufs):
    n = len(bufs)

    def copies(outs, sems):
        ici_send, ici_recv, d2d_send, d2d_recv = sems
        x, y, c = lax.axis_index("x"), lax.axis_index("y"), lax.axis_index("c")

        def half(t, slot, hc):
            hr = bufs[t].shape[1] // 2
            return outs[t].at[slot, pl.ds(pl.multiple_of(hc * hr, 16), hr), :]

        def ici(t, j, slot, px, py):
            return pltpu.make_async_remote_copy(src_ref=half(t, slot, c), dst_ref=half(t, slot, c),
                                                send_sem=ici_send.at[3 * t + j], recv_sem=ici_recv.at[3 * t + j],
                                                device_id=(px, py, c), device_id_type=MESH)

        def d2d(t, j, slot, hc):
            return pltpu.make_async_remote_copy(src_ref=half(t, slot, hc), dst_ref=half(t, slot, hc),
                                                send_sem=d2d_send.at[3 * t + j], recv_sem=d2d_recv.at[3 * t + j],
                                                device_id=(x, y, 1 - c), device_id_type=MESH)

        peers = [(t, j, px, py) for t in range(n) for j, (px, py) in enumerate(_chips(x, y))]
        return ici, d2d, peers, 2 * x + y, c

    def start(ins, outs, sems):
        ici, _, peers, me, _ = copies(outs, sems)
        for t, j, px, py in peers:
            ici(t, j, me, px, py).start()

    def finish(ins, outs, sems):
        ici, d2d, peers, me, c = copies(outs, sems)
        for t, j, px, py in peers:
            ici(t, j, 2 * px + py, px, py).wait_recv()
            d2d(t, j, 2 * px + py, c).start()
        for t, j, px, py in peers:
            d2d(t, j, 2 * px + py, 1 - c).wait_recv()
        for t, j, px, py in peers:
            ici(t, j, me, px, py).wait_send()
            d2d(t, j, 2 * px + py, c).wait_send()

    return Comm(list(bufs), [SDS(b.shape, b.dtype) for b in bufs], True, [pltpu.SemaphoreType.DMA((3 * n,))] * 4,
                start, finish)


def _sibling_exchange_comm(gs):
    n = len(gs)

    def copies(ins, outs, sems):
        x, y, c = lax.axis_index("x"), lax.axis_index("y"), lax.axis_index("c")
        cps = []
        for t in range(n):
            hr = gs[t].shape[1] // 2
            src = ins[t].at[:, pl.ds(pl.multiple_of((1 - c) * hr, SUBLANES), hr), :]
            cps.append(pltpu.make_async_remote_copy(src_ref=src, dst_ref=outs[t], send_sem=sems[0].at[t],
                                                    recv_sem=sems[1].at[t], device_id=(x, y, 1 - c),
                                                    device_id_type=MESH))
        return cps

    def start(ins, outs, sems):
        for cp in copies(ins, outs, sems):
            cp.start()

    def finish(ins, outs, sems):
        for cp in copies(ins, outs, sems):
            cp.wait()

    return Comm(list(gs), [SDS((N_CHIPS, g.shape[1] // 2, g.shape[2]), g.dtype) for g in gs], False,
                [pltpu.SemaphoreType.DMA((n,))] * 2, start, finish)


def _chip_exchange_comm(ss):
    n = len(ss)

    def copies(ins, outs, sems):
        x, y, c = lax.axis_index("x"), lax.axis_index("y"), lax.axis_index("c")
        me = 2 * x + y

        def copy(t, j, px, py, src_slot, dst_slot):
            return pltpu.make_async_remote_copy(src_ref=ins[t].at[src_slot], dst_ref=outs[t].at[dst_slot],
                                                send_sem=sems[0].at[3 * t + j], recv_sem=sems[1].at[3 * t + j],
                                                device_id=(px, py, c), device_id_type=MESH)

        peers = [(t, j, px, py) for t in range(n) for j, (px, py) in enumerate(_chips(x, y))]
        return copy, peers, me

    def start(ins, outs, sems):
        copy, peers, me = copies(ins, outs, sems)
        for t, j, px, py in peers:
            copy(t, j, px, py, 2 * px + py, me).start()

    def finish(ins, outs, sems):
        copy, peers, me = copies(ins, outs, sems)
        for t, j, px, py in peers:
            copy(t, j, px, py, me, 2 * px + py).wait_recv()
        for t, j, px, py in peers:
            copy(t, j, px, py, 2 * px + py, me).wait_send()

    return Comm(list(ss), [SDS(s.shape, s.dtype) for s in ss], False, [pltpu.SemaphoreType.DMA((3 * n,))] * 2,
                start, finish)


def _sibling_swap_comm(fs):
    n = len(fs)

    def copies(ins, outs, sems):
        x, y, c = lax.axis_index("x"), lax.axis_index("y"), lax.axis_index("c")
        return [pltpu.make_async_remote_copy(src_ref=ins[t], dst_ref=outs[t], send_sem=sems[0].at[t],
                                             recv_sem=sems[1].at[t], device_id=(x, y, 1 - c), device_id_type=MESH)
                for t in range(n)]

    def start(ins, outs, sems):
        for cp in copies(ins, outs, sems):
            cp.start()

    def finish(ins, outs, sems):
        for cp in copies(ins, outs, sems):
            cp.wait()

    return Comm(list(fs), [SDS(f.shape, f.dtype) for f in fs], False, [pltpu.SemaphoreType.DMA((n,))] * 2,
                start, finish)


def _row_tile(rows, cap=512):
    best = SUBLANES
    for tr in range(SUBLANES, min(rows, cap) + 1, SUBLANES):
        if rows % tr == 0:
            best = tr
    return best


def _add_halves(g4, r4, c_idx, name):
    _, hr, lanes = r4.shape
    tr = _row_tile(hr)
    nb = hr // tr

    def body(c_ref, a_ref, b_ref, o_ref):
        o_ref[...] = (a_ref[...] + b_ref[...]).astype(o_ref.dtype)

    grid_spec = pltpu.PrefetchScalarGridSpec(
        num_scalar_prefetch=1, grid=(N_CHIPS, nb),
        in_specs=[pl.BlockSpec((None, tr, lanes), lambda q, i, c_ref: (q, c_ref[0] * nb + i, 0)),
                  pl.BlockSpec((None, tr, lanes), lambda q, i, c_ref: (q, i, 0))],
        out_specs=pl.BlockSpec((None, tr, lanes), lambda q, i, c_ref: (q, i, 0)))
    return pl.pallas_call(body, name=name, grid_spec=grid_spec, out_shape=SDS(r4.shape, BF16),
                          compiler_params=_params(("parallel", "parallel")))(c_idx, g4, r4)


def _sum_chips(r4, s4, me_idx, name):
    _, rows, lanes = r4.shape
    tr = _row_tile(rows)

    def body(me_ref, a_ref, b_ref, c_ref, d_ref, own_ref, o_ref):
        own = own_ref[...].astype(F32)
        p = [jnp.where(me_ref[0] == q, own, ref[...].astype(F32)) for q, ref in enumerate((a_ref, b_ref, c_ref, d_ref))]
        o_ref[...] = ((p[0] + p[1]) + p[2]) + p[3]

    other = lambda q: (lambda i, me_ref: (jnp.where(me_ref[0] == q, (q + 1) % N_CHIPS, q), i, 0))
    grid_spec = pltpu.PrefetchScalarGridSpec(
        num_scalar_prefetch=1, grid=(rows // tr,),
        in_specs=[pl.BlockSpec((None, tr, lanes), other(q)) for q in range(N_CHIPS)]
        + [pl.BlockSpec((None, tr, lanes), lambda i, me_ref: (me_ref[0], i, 0))],
        out_specs=pl.BlockSpec((tr, lanes), lambda i, me_ref: (i, 0)))
    return pl.pallas_call(body, name=name, grid_spec=grid_spec, out_shape=SDS((rows, lanes), F32),
                          compiler_params=_params(("parallel",)))(me_idx, r4, r4, r4, r4, s4)


def _adamw(wf, g_own, g_other, mf, vf, c_idx, name):
    rows, lanes = wf.shape
    hr = rows // 2
    tr = _row_tile(hr)
    nb = hr // tr
    c1 = 1.0 / (1.0 - ADAM_B1 ** ADAM_STEP)
    c2 = 1.0 / (1.0 - ADAM_B2 ** ADAM_STEP)

    def body(c_ref, w_ref, go_ref, gx_ref, m_ref, v_ref, g_ref, d_ref, nm_ref, nv_ref):
        gv = jnp.where(pl.program_id(0) == c_ref[0], go_ref[...], gx_ref[...])
        m = ADAM_B1 * m_ref[...] + (1.0 - ADAM_B1) * gv
        v = ADAM_B2 * v_ref[...] + (1.0 - ADAM_B2) * (gv * gv)
        g_ref[...] = gv
        d_ref[...] = -ADAM_LR * ((m * c1) / (jnp.sqrt(v * c2) + ADAM_EPS) + ADAM_WD * w_ref[...])
        nm_ref[...] = m
        nv_ref[...] = v

    full = pl.BlockSpec((tr, lanes), lambda h, i, c_ref: (h * nb + i, 0))
    half = pl.BlockSpec((tr, lanes), lambda h, i, c_ref: (i, 0))
    grid_spec = pltpu.PrefetchScalarGridSpec(num_scalar_prefetch=1, grid=(2, nb),
                                             in_specs=[full, half, half, full, full], out_specs=[full] * 4)
    return pl.pallas_call(body, name=name, grid_spec=grid_spec, out_shape=[SDS((rows, lanes), F32)] * 4,
                          compiler_params=_params(("parallel", "parallel")))(c_idx, wf, g_own, g_other, mf, vf)


BIG = ("ffn1_w_gate", "ffn1_w_up", "ffn1_w_down", "ffn2_w_gate", "ffn2_w_up", "ffn2_w_down", "w_out", "w_in")
TRANSPOSED = ("ffn1_w_gate", "ffn1_w_up", "ffn2_w_gate", "ffn2_w_up")
PACKED = ("rwkv_w2", "rwkv_a2", "rwkv_g2")
SMALL_SHAPES = {"ffn1_norm": (1, D_MODEL), "mix_norm": (1, D_MODEL), "hgrn_lb_logits": (2, W_A),
                "hgrn_out_norm": (1, W_A), "rwkv_shift_mu": (1, N_RWKV_COLS), "rwkv_w0": (1, W_B),
                "rwkv_a0": (1, W_B), "rwkv_k_k": (1, W_B), "rwkv_k_a": (1, W_B),
                "rwkv_r_k": (1, HB_HEADS, HB_DIM), "rwkv_gn_w": (1, W_B), "rwkv_gn_b": (1, W_B),
                "ffn2_norm": (1, D_MODEL), "final_norm": (D_MODEL,)}
PACK_ELEMS = sum(_numel(_shard_shape(n)) for n in PACKED) + sum(_numel(SMALL_SHAPES[n]) for n in SMALL)
PACK_ROWS = -(-PACK_ELEMS // (32 * LANES)) * 32


def _to_rows(name, shard):
    return shard[0].T if name in TRANSPOSED else shard[0]


def _from_rows(name, rows):
    return (rows.T if name in TRANSPOSED else rows)[None]


def _pack(sharded, small):
    flat = jnp.concatenate([sharded[n].reshape(-1) for n in PACKED] + [small[n].reshape(-1) for n in SMALL])
    return jnp.pad(flat, (0, PACK_ROWS * LANES - flat.shape[0])).reshape(PACK_ROWS, LANES)


def _unpack(packed):
    flat, out, off = packed.reshape(-1), {}, 0
    for n in PACKED:
        shp = _shard_shape(n)
        out[n] = flat[off:off + _numel(shp)].reshape((1,) + shp)
        off += _numel(shp)
    for n in SMALL:
        shp = SMALL_SHAPES[n]
        out[n] = flat[off:off + _numel(shp)].reshape(shp)
        off += _numel(shp)
    return out


def _quarter(full, name, q):
    shape, ax = SHARDED_SHAPES[name]
    w = shape[ax] // N_CHIPS
    return lax.slice_in_dim(full, q * w, (q + 1) * w, axis=ax)


def kernel(x, ffn1_norm, ffn1_w_gate, ffn1_w_up, ffn1_w_down, mix_norm, w_in, hgrn_lb_logits, hgrn_out_norm, rwkv_shift_mu, rwkv_w0, rwkv_w2, rwkv_a0, rwkv_a2, rwkv_g2, rwkv_k_k, rwkv_k_a, rwkv_r_k, rwkv_gn_w, rwkv_gn_b, w_out, ffn2_norm, ffn2_w_gate, ffn2_w_up, ffn2_w_down, final_norm, loss_target, m_ffn1_norm, m_ffn1_w_gate, m_ffn1_w_up, m_ffn1_w_down, m_mix_norm, m_w_in, m_hgrn_lb_logits, m_hgrn_out_norm, m_rwkv_shift_mu, m_rwkv_w0, m_rwkv_w2, m_rwkv_a0, m_rwkv_a2, m_rwkv_g2, m_rwkv_k_k, m_rwkv_k_a, m_rwkv_r_k, m_rwkv_gn_w, m_rwkv_gn_b, m_w_out, m_ffn2_norm, m_ffn2_w_gate, m_ffn2_w_up, m_ffn2_w_down, m_final_norm, v_ffn1_norm, v_ffn1_w_gate, v_ffn1_w_up, v_ffn1_w_down, v_mix_norm, v_w_in, v_hgrn_lb_logits, v_hgrn_out_norm, v_rwkv_shift_mu, v_rwkv_w0, v_rwkv_w2, v_rwkv_a0, v_rwkv_a2, v_rwkv_g2, v_rwkv_k_k, v_rwkv_k_a, v_rwkv_r_k, v_rwkv_gn_w, v_rwkv_gn_b, v_w_out, v_ffn2_norm, v_ffn2_w_gate, v_ffn2_w_up, v_ffn2_w_down, v_final_norm):
    args = dict(locals())
    wts = {n: args[n] for n in ALL_WEIGHTS}
    moms = {n: args["m_" + n] for n in ALL_WEIGHTS}
    vars_ = {n: args["v_" + n] for n in ALL_WEIGHTS}

    me = 2 * lax.axis_index("x") + lax.axis_index("y")
    c_idx = lax.axis_index("c").astype(jnp.int32).reshape(1)
    me_idx = me.astype(jnp.int32).reshape(1)
    shard_of = {n: _to_rows(n, wts[n]).astype(BF16) for n in BIG}
    shard_of["packed"] = _pack(wts, {n: wts[n] for n in SMALL}).astype(BF16)
    group = {"ffn1": BIG[0:3], "ffn2": BIG[3:6]}

    def slot_bufs(names):
        return [lax.dynamic_update_slice(jnp.zeros((N_CHIPS,) + shard_of[n].shape, BF16), shard_of[n][None],
                                         (me, 0, 0)) for n in names]

    def ffn_weights(tag, gathered):
        return {f"{tag}_wgt": gathered[0].reshape(D_FF, D_MODEL), f"{tag}_wut": gathered[1].reshape(D_FF, D_MODEL),
                f"{tag}_wd": gathered[2].reshape(D_FF, D_MODEL)}

    def w_in_weights(gathered):
        w_in_full = jnp.concatenate([gathered[0][q] for q in range(N_CHIPS)], axis=1)
        return {"w_in_h": w_in_full[:, :N_HGRN_COLS],
                "w_in_r": jnp.pad(w_in_full[:, N_HGRN_COLS:], ((0, 0), (0, N_RWKV_PAD - N_RWKV_COLS)))}

    def mixer_weights(gathered):
        w_out_full = gathered[0].reshape(D_MODEL, D_MODEL)
        packs = gathered[1].reshape(N_CHIPS, PACK_ROWS * LANES)
        full, off = {}, 0
        for n in PACKED:
            shp = _shard_shape(n)
            full[n] = jnp.concatenate([packs[q, off:off + _numel(shp)].reshape(shp) for q in range(N_CHIPS)], axis=1)
            off += _numel(shp)
        zrow = lambda nrow: jnp.zeros((nrow, W_B), BF16)
        return {"w_out_a": w_out_full[:W_A], "w_out_b": w_out_full[W_A:],
                "w2_pad": jnp.concatenate([full["rwkv_w2"], zrow(LORA_PAD - 32)], axis=0),
                "a2_pad": jnp.concatenate([zrow(32), full["rwkv_a2"], zrow(LORA_PAD - 64)], axis=0),
                "g2_pad": jnp.concatenate([zrow(64), full["rwkv_g2"], zrow(LORA_PAD - 160)], axis=0)}

    plan = _Plan()
    first = _run_comm(_gather_comm(slot_bufs(group["ffn1"][:2])), "gather_ffn1")
    w = {"ffn1_wgt": first[0].reshape(D_FF, D_MODEL), "ffn1_wut": first[1].reshape(D_FF, D_MODEL)}

    def after_gate_up(res, w_):
        w_["ffn1_wd"] = res[0].reshape(D_FF, D_MODEL)
        w_.update(mixer_weights(res[1:]))

    plan.carry("ffn1_gate_up", lambda g: _gather_comm(slot_bufs(("ffn1_w_down", "w_out", "packed"))), after_gate_up)
    plan.carry("ffn1_down", lambda g: _gather_comm(slot_bufs(("w_in",))), lambda res, w_: w_.update(w_in_weights(res)))
    plan.carry("rwkv_fwd", lambda g: _gather_comm(slot_bufs(group["ffn2"])),
               lambda res, w_: w_.update(ffn_weights("ffn2", res)))
    w["ffn1_norm"], w["ffn2_norm"] = ffn1_norm, ffn2_norm
    w["mix_norm"] = mix_norm
    w["lb0"], w["lb1"] = hgrn_lb_logits[0:1], hgrn_lb_logits[1:2]
    w["hgrn_out_norm"] = hgrn_out_norm
    w["mu_pad"] = jnp.pad(rwkv_shift_mu, ((0, 0), (0, N_RWKV_PAD - N_RWKV_COLS)))
    for n in ("rwkv_w0", "rwkv_a0", "rwkv_k_k", "rwkv_k_a", "rwkv_gn_w", "rwkv_gn_b"):
        w[n] = wts[n]
    w["rwkv_r_k"] = rwkv_r_k.reshape(1, W_B)
    w["final_norm"] = final_norm.reshape(1, D_MODEL)

    def reduce_rows(names, gs):
        r1 = _run_comm(_sibling_exchange_comm(gs), "grad_sibling_exchange")
        s4 = [_add_halves(gt, rt, c_idx, f"grad_add_halves_{n}") for gt, rt, n in zip(gs, r1, names)]
        r2 = _run_comm(_chip_exchange_comm(s4), "grad_chip_exchange")
        return [_sum_chips(rt, st, me_idx, f"grad_sum_chips_{n}") for rt, st, n in zip(r2, s4, names)]

    early = {}

    def reduce_early(names, grads_of, sibling_host, chips_host):
        def sibling_comm(g):
            early[names, "gs"] = grads_of(g)
            return _sibling_exchange_comm(early[names, "gs"])

        def after_sibling(res, w_):
            early[names, "s4"] = [_add_halves(gt, rt, c_idx, f"grad_add_halves_{n}")
                                  for gt, rt, n in zip(early[names, "gs"], res, names)]

        def after_chips(res, w_):
            early.update(zip(names, [_sum_chips(rt, st, me_idx, f"grad_sum_chips_{n}")
                                     for rt, st, n in zip(res, early[names, "s4"], names)]))

        plan.carry(sibling_host, sibling_comm, after_sibling)
        plan.carry(chips_host, lambda g: _chip_exchange_comm(early[names, "s4"]), after_chips)

    def proj_grads(g):
        g_w_in = jnp.concatenate([g["w_in_h"], g["w_in_r"][:, :N_RWKV_COLS]], axis=1)
        return [jnp.concatenate([g["w_out_a"], g["w_out_b"]], axis=0).reshape(N_CHIPS, -1, D_MODEL),
                jnp.stack([_quarter(g_w_in, "w_in", q) for q in range(N_CHIPS)])]

    rows_of = lambda keys: (lambda g: [g[k].reshape(N_CHIPS, -1, D_MODEL) for k in keys])
    reduce_early(group["ffn2"], rows_of(("ffn2_wgt", "ffn2_wut", "ffn2_wd")), "hgrn_bwd", "rwkv_bwd")
    reduce_early(("w_out", "w_in"), proj_grads, "mix_drms", "ffn1_dact")
    reduce_early(("ffn1_w_down",), rows_of(("ffn1_wd",)), "ffn1_dwg", "ffn1_dwu")
    reduce_early(("ffn1_w_gate",), rows_of(("ffn1_wgt",)), "ffn1_dwu", "ffn1_dh_g")
    reduce_early(("ffn1_w_up",), rows_of(("ffn1_wut",)), "ffn1_dh_g", "ffn1_dh_u")
    loss_slab, grad_x, g = _local_step(x[0], loss_target[0], w, plan)
    loss = lax.psum(loss_slab[0, 0], ("x", "y", "c"))

    gfull = {
        "rwkv_w2": g["w2_pad"][0:32], "rwkv_a2": g["a2_pad"][32:64], "rwkv_g2": g["g2_pad"][64:160],
    }
    gsmall = {
        "ffn1_norm": g["ffn1_norm"], "mix_norm": g["mix_norm"],
        "hgrn_lb_logits": jnp.concatenate([g["lb0"], g["lb1"]], axis=0), "hgrn_out_norm": g["hgrn_out_norm"],
        "rwkv_shift_mu": g["mu_pad"][:, :N_RWKV_COLS], "rwkv_w0": g["rwkv_w0"], "rwkv_a0": g["rwkv_a0"],
        "rwkv_k_k": g["rwkv_k_k"], "rwkv_k_a": g["rwkv_k_a"], "rwkv_r_k": g["rwkv_r_k"],
        "rwkv_gn_w": g["rwkv_gn_w"], "rwkv_gn_b": g["rwkv_gn_b"], "ffn2_norm": g["ffn2_norm"],
        "final_norm": g["final_norm"],
    }
    packed = jnp.stack([_pack({n: _quarter(gfull[n], n, q) for n in PACKED}, gsmall) for q in range(N_CHIPS)])
    early["packed"], = reduce_rows(["packed"], [packed])
    names = list(BIG) + ["packed"]
    own = [early[n] for n in names]
    other = _run_comm(_sibling_swap_comm(own), "grad_sibling_swap")

    def rows_list(d):
        return [_to_rows(n, d[n]) for n in BIG] + [_pack(d, {n: d[n] for n in SMALL})]

    outs = [_adamw(wt, go, gx, mt, vt, c_idx, f"adamw_{n}")
            for wt, go, gx, mt, vt, n in zip(rows_list(wts), own, other, rows_list(moms), rows_list(vars_), names)]
    results = []
    for k in range(4):
        per = [outs[i][k] for i in range(len(names))]
        d = {n: _from_rows(n, z) for n, z in zip(BIG, per[:-1])}
        d.update(_unpack(per[-1]))
        results.append(d)
    return (loss, grad_x[None], *[r[n] for r in results for n in ALL_WEIGHTS])
```

```python
import collections
import functools

import jax
import jax.numpy as jnp
from jax import lax
from jax.experimental import pallas as pl
from jax.experimental.pallas import tpu as pltpu

F32 = jnp.float32
BF16 = jnp.bfloat16
SDS = jax.ShapeDtypeStruct
MESH = pl.DeviceIdType.MESH

D_MODEL = 1024
D_FF = 2816
W_A = 512
W_B = 512
HA_HEADS, HA_DIM = 4, 128
HB_HEADS, HB_DIM = 8, 64
HGRN_CHUNK = 64
HGRN_GROUP = 2
RWKV_CHUNK = 16
RWKV_GROUP = 4
N_HGRN_COLS = 4 * W_A
N_RWKV_COLS = 3 * W_B + 32 + 32 + 96
N_RWKV_PAD = 1792
LORA_PAD = 256
NORM_EPS = 1e-6
RWKV_GN_EPS = 64e-5
L2_EPS = 1e-12
ADAM_LR, ADAM_B1, ADAM_B2, ADAM_EPS, ADAM_WD, ADAM_STEP = 0.001, 0.9, 0.999, 1e-8, 0.01, 10

N_CHIPS = 4
VMEM_LIMIT_V7X = 56 * 1024 * 1024
LANES = 1024

SHARDED_SHAPES = {
    "ffn1_w_gate": ((D_MODEL, D_FF), 1), "ffn1_w_up": ((D_MODEL, D_FF), 1), "ffn1_w_down": ((D_FF, D_MODEL), 0),
    "w_in": ((D_MODEL, N_HGRN_COLS + N_RWKV_COLS), 1), "rwkv_w2": ((32, W_B), 1), "rwkv_a2": ((32, W_B), 1),
    "rwkv_g2": ((96, W_B), 1), "w_out": ((D_MODEL, D_MODEL), 0),
    "ffn2_w_gate": ((D_MODEL, D_FF), 1), "ffn2_w_up": ((D_MODEL, D_FF), 1), "ffn2_w_down": ((D_FF, D_MODEL), 0),
}
SMALL = ("ffn1_norm", "mix_norm", "hgrn_lb_logits", "hgrn_out_norm", "rwkv_shift_mu", "rwkv_w0", "rwkv_a0",
         "rwkv_k_k", "rwkv_k_a", "rwkv_r_k", "rwkv_gn_w", "rwkv_gn_b", "ffn2_norm", "final_norm")
ALL_WEIGHTS = ("ffn1_norm", "ffn1_w_gate", "ffn1_w_up", "ffn1_w_down", "mix_norm", "w_in", "hgrn_lb_logits",
               "hgrn_out_norm", "rwkv_shift_mu", "rwkv_w0", "rwkv_w2", "rwkv_a0", "rwkv_a2", "rwkv_g2", "rwkv_k_k",
               "rwkv_k_a", "rwkv_r_k", "rwkv_gn_w", "rwkv_gn_b", "w_out", "ffn2_norm", "ffn2_w_gate", "ffn2_w_up",
               "ffn2_w_down", "final_norm")


def _shard_shape(name):
    shape, ax = SHARDED_SHAPES[name]
    return tuple(s // N_CHIPS if i == ax else s for i, s in enumerate(shape))


def _numel(shape):
    n = 1
    for s in shape:
        n *= s
    return n


def _params(sem=None):
    return pltpu.CompilerParams(dimension_semantics=sem, vmem_limit_bytes=VMEM_LIMIT_V7X)


def _split2(x):
    hi = x.astype(BF16)
    return hi, (x.astype(F32) - hi.astype(F32)).astype(BF16)


def _dg(x, y, cx, cy, hi):
    dn = (((cx,), (cy,)), ((), ()))
    dot = lambda p, q: lax.dot_general(p, q, dn, preferred_element_type=F32)
    if hi == "x3":
        (xh, xl), (yh, yl) = _split2(x), _split2(y)
        return dot(xh, yh) + (dot(xh, yl) + dot(xl, yh))
    return dot(x.astype(BF16), y.astype(BF16))


def _make_mm(hi, cotangent_forms=None):
    @jax.custom_vjp
    def nn(x, y):
        return _dg(x, y, 1, 0, hi)

    @jax.custom_vjp
    def nt(x, y):
        return _dg(x, y, 1, 1, hi)

    @jax.custom_vjp
    def tn(x, y):
        return _dg(x, y, 0, 0, hi)

    bnn, bnt, btn = cotangent_forms or (nn, nt, tn)
    nn.defvjp(lambda x, y: (nn(x, y), (x, y)), lambda r, g: (bnt(g, r[1]), btn(r[0], g)))
    nt.defvjp(lambda x, y: (nt(x, y), (x, y)), lambda r, g: (bnn(g, r[1]), btn(g, r[0])))
    tn.defvjp(lambda x, y: (tn(x, y), (x, y)), lambda r, g: (bnt(r[1], g), bnn(r[0], g)))
    return nn, nt, tn


_nn, _nt, _tn = _make_mm(False)
_nn_x3, _nt_x3, _tn_x3 = _make_mm("x3", (_nn, _nt, _tn))


def _tri_apply(x, transpose):
    c = x.shape[0]
    tri = (lax.broadcasted_iota(jnp.int32, (c, c), 1) <= lax.broadcasted_iota(jnp.int32, (c, c), 0)).astype(BF16)
    dn = (((0 if transpose else 1,), (0,)), ((), ()))
    p1 = x.astype(BF16)
    r1 = x - p1.astype(F32)
    p2 = r1.astype(BF16)
    p3 = (r1 - p2.astype(F32)).astype(BF16)
    dot = lambda p: lax.dot_general(tri, p, dn, preferred_element_type=F32)
    return dot(p1) + (dot(p2) + dot(p3))


@jax.custom_vjp
def _cumsum_rows(x):
    return _tri_apply(x, False)


_cumsum_rows.defvjp(lambda x: (_tri_apply(x, False), None), lambda _, g: (_tri_apply(g, True),))


def _sigmoid(x):
    return 1.0 / (1.0 + jnp.exp(-x))


def _silu(x):
    return x * _sigmoid(x)


def _softplus(z):
    return jnp.maximum(z, 0.0) + jnp.log(1.0 + jnp.exp(-jnp.abs(z)))


def _mm(a, b, *, ta=False, tb=False, tm, tn, tk, name, out_dtype=F32, res=None, scale=None, comm=None):
    m = a.shape[1] if ta else a.shape[0]
    kdim = a.shape[0] if ta else a.shape[1]
    n = b.shape[0] if tb else b.shape[1]
    assert (b.shape[1] if tb else b.shape[0]) == kdim
    tm, tn, tk = min(tm, m), min(tn, n), min(tk, kdim)
    assert m % tm == 0 and n % tn == 0 and kdim % tk == 0, (name, m, n, kdim)
    nk = kdim // tk
    a_spec = pl.BlockSpec((tk, tm), lambda i, j, k: (k, i)) if ta else pl.BlockSpec((tm, tk), lambda i, j, k: (i, k))
    b_spec = pl.BlockSpec((tn, tk), lambda i, j, k: (j, k)) if tb else pl.BlockSpec((tk, tn), lambda i, j, k: (k, j))
    o_spec = pl.BlockSpec((tm, tn), lambda i, j, k: (i, j))
    ca, cb = (0 if ta else 1), (1 if tb else 0)

    def body(*refs):
        if res is not None:
            a_ref, b_ref, r_ref, o_ref, acc_ref = refs
        else:
            a_ref, b_ref, o_ref, acc_ref = refs
        k = pl.program_id(2)

        @pl.when(k == 0)
        def _():
            acc_ref[...] = jnp.zeros_like(acc_ref)

        acc_ref[...] += _dg(a_ref[...], b_ref[...], ca, cb, False)

        @pl.when(k == nk - 1)
        def _():
            acc = acc_ref[...]
            if scale is not None:
                acc = acc * scale
            if res is not None:
                acc = r_ref[...] + acc
            o_ref[...] = acc.astype(out_dtype)

    in_specs = [a_spec, b_spec] + ([o_spec] if res is not None else [])
    args = (a, b) + ((res,) if res is not None else ())
    if comm is None:
        return pl.pallas_call(
            body, name=name, grid=(m // tm, n // tn, nk), in_specs=in_specs, out_specs=o_spec,
            out_shape=SDS((m, n), out_dtype), scratch_shapes=[pltpu.VMEM((tm, tn), F32)],
            compiler_params=_params(("parallel", "parallel", "arbitrary")))(*args)
    (out,), carried = _hosting_call(
        body, comm, name=name, grid=(m // tm, n // tn, nk), in_specs=in_specs, out_specs=[o_spec],
        out_shape=[SDS((m, n), out_dtype)], scratch_shapes=[pltpu.VMEM((tm, tn), F32)], args=args)
    return out, carried


def _row_spec(x, tm):
    if isinstance(x, tuple):
        arr, w, j = x
        return arr, pl.BlockSpec((tm, w), lambda i, j=j: (i, j))
    return x, pl.BlockSpec((tm, x.shape[1]), lambda i: (i, 0))


def _par_spec(p):
    if isinstance(p, tuple):
        arr, w, j = p
        return arr, pl.BlockSpec((arr.shape[0], w), lambda i, j=j: (0, j))
    return p, pl.BlockSpec(p.shape, lambda i: (0, 0))


def _store_groups(refs, groups, vals):
    for ref, idxs in zip(refs, groups):
        off = 0
        for ix in idxs:
            v = vals[ix]
            ref[:, off:off + v.shape[1]] = v.astype(ref.dtype)
            off += v.shape[1]


SUBLANES = 8


def _x_plan(xs, tm, t):
    arrays, specs, plan = [], [], []
    nb = tm // SUBLANES
    for x in xs:
        if isinstance(x, tuple) and isinstance(x[0], str):
            kind, arr, w, j = x
            if kind == "prev":
                halo = lambda i, j=j: (jnp.maximum(i * nb - 1, 0), j)
            else:
                halo = lambda i, j=j: (jnp.minimum((i + 1) * nb, t // SUBLANES - 1), j)
            arrays += [arr, arr]
            specs += [pl.BlockSpec((tm, w), lambda i, j=j: (i, j)), pl.BlockSpec((SUBLANES, w), halo)]
            plan.append((kind, 2, w))
        else:
            arr, spec = _row_spec(x, tm)
            arrays.append(arr)
            specs.append(spec)
            plan.append(("plain", 1, spec.block_shape[1]))
    return arrays, specs, plan


def _x_vals(refs, plan, tm, nt):
    vals, k = [], 0
    i = pl.program_id(0)
    rows = lax.broadcasted_iota(jnp.int32, (tm, 1), 0)
    for kind, n, _ in plan:
        main = refs[k][...].astype(F32)
        if kind == "prev":
            edge = jnp.where(i == 0, 0.0, refs[k + 1][SUBLANES - 1:SUBLANES, :].astype(F32))
            main = jnp.where(rows == 0, edge, pltpu.roll(main, 1, 0))
        elif kind == "next":
            edge = jnp.where(i == nt - 1, 0.0, refs[k + 1][0:1, :].astype(F32))
            main = jnp.where(rows == tm - 1, edge, pltpu.roll(main, tm - 1, 0))
        vals.append(main)
        k += n
    return vals


def _tile_rows(xs, tm):
    arr = xs[0]
    if isinstance(arr, tuple):
        arr = arr[1] if isinstance(arr[0], str) else arr[0]
    return min(tm, arr.shape[0]), arr.shape[0]


def _rowwise(f, xs, params, out_groups, out_dtypes, *, tm, name, comm=None):
    tm, t = _tile_rows(xs, tm)
    nt = t // tm
    xa, xspecs, plan = _x_plan(xs, tm, t)
    pa, pspecs = (zip(*[_par_spec(p) for p in params]) if params else ((), ()))
    nxr, npar = len(xa), len(pa)
    x_sds = [SDS((tm, w), F32) for _, _, w in plan]
    p_sds = [SDS(s.block_shape, F32) for s in pspecs]
    outs_sds = jax.eval_shape(lambda *vals: f(*vals), *x_sds, *p_sds)
    widths = [sum(outs_sds[ix].shape[1] for ix in idxs) for idxs in out_groups]

    def body(*refs):
        vals = _x_vals(refs[:nxr], plan, tm, nt) + [r[...].astype(F32) for r in refs[nxr:nxr + npar]]
        outs = f(*vals)
        _store_groups(refs[nxr + npar:], out_groups, outs)

    res, carried = _hosting_call(
        body, comm, name=name, grid=(nt,), in_specs=list(xspecs) + list(pspecs),
        out_specs=[pl.BlockSpec((tm, w), lambda i: (i, 0)) for w in widths],
        out_shape=[SDS((t, w), dt) for w, dt in zip(widths, out_dtypes)], scratch_shapes=[], args=(*xa, *pa))
    return res if comm is None else (res, carried)


def _rowwise_bwd(f, xs, params, cots, *, x_grad, p_grad, dx_groups, dx_dtypes, tm, name, extra=None, comm=None):
    tm, t = _tile_rows(xs, tm)
    nt = t // tm
    xa, xspecs, plan = _x_plan(xs, tm, t)
    pa, pspecs = (zip(*[_par_spec(p) for p in params]) if params else ((), ()))
    ca, cspecs = zip(*[_row_spec(c, tm) for c in cots])
    extra = extra or {}
    ekeys = sorted(extra)
    ea, especs = (zip(*[_row_spec(extra[k], tm) for k in ekeys]) if ekeys else ((), ()))
    nx, nxr, npar, nc, ne = len(plan), len(xa), len(pa), len(ca), len(ea)
    gx = [i for i in range(nx) if x_grad[i]]
    gp = [i for i in range(npar) if p_grad[i]]
    widths = [sum(plan[gx[ix]][2] for ix in idxs) for idxs in dx_groups]
    ng = len(dx_groups)

    def body(*refs):
        ins = refs[:nxr + npar + nc + ne]
        outs = refs[nxr + npar + nc + ne:]
        vals = _x_vals(ins[:nxr], plan, tm, nt) + [r[...].astype(F32) for r in ins[nxr:nxr + npar]]
        cvals = tuple(r[...].astype(F32) for r in ins[nxr + npar:nxr + npar + nc])
        evals = [r[...].astype(F32) for r in ins[nxr + npar + nc:]]
        diff_idx = gx + [nx + i for i in gp]

        def g(*dargs):
            full = list(vals)
            for ix, v in zip(diff_idx, dargs):
                full[ix] = v
            return tuple(f(*full))

        _, vjp = jax.vjp(g, *[vals[ix] for ix in diff_idx])
        grads = vjp(cvals)
        dxs = list(grads[:len(gx)])
        for k, ev in zip(ekeys, evals):
            dxs[k] = dxs[k] + ev
        _store_groups(outs[:ng], dx_groups, dxs)
        i = pl.program_id(0)
        for ref, gval in zip(outs[ng:], grads[len(gx):]):
            @pl.when(i == 0)
            def _(ref=ref):
                ref[...] = jnp.zeros_like(ref)
            ref[...] += gval

    dp_specs = [pl.BlockSpec(pspecs[i].block_shape, lambda i: (0, 0)) for i in gp]
    dp_shapes = [SDS(pspecs[i].block_shape, F32) for i in gp]
    res, carried = _hosting_call(
        body, comm, name=name, grid=(nt,), in_specs=list(xspecs) + list(pspecs) + list(cspecs) + list(especs),
        out_specs=[pl.BlockSpec((tm, w), lambda i: (i, 0)) for w in widths] + dp_specs,
        out_shape=[SDS((t, w), dt) for w, dt in zip(widths, dx_dtypes)] + dp_shapes, scratch_shapes=[],
        args=(*xa, *pa, *ca, *ea))
    return res if comm is None else (res, carried)


def _rms_f(x, g):
    return (x * lax.rsqrt(jnp.mean(x * x, axis=-1, keepdims=True) + NORM_EPS) * g,)


def _three_pieces(x):
    p1 = x.astype(BF16)
    r1 = x - p1.astype(F32)
    p2 = r1.astype(BF16)
    return p1, p2, (r1 - p2.astype(F32)).astype(BF16)


def _group_sum_impl(x, ones_bd):
    p1, p2, p3 = _three_pieces(x)
    dot = lambda p: lax.dot_general(p, ones_bd.astype(BF16), (((1,), (0,)), ((), ())), preferred_element_type=F32)
    return dot(p1) + (dot(p2) + dot(p3))


@jax.custom_vjp
def _group_sum(x, ones_bd):
    return _group_sum_impl(x, ones_bd)


_group_sum.defvjp(lambda x, o: (_group_sum_impl(x, o), o),
                  lambda o, g: (_group_sum_impl(g, o), jnp.zeros_like(o)))


def _rwkv_prep_f(r, k, v, lo, rp, kp, vp, lop, mu_r, mu_k, mu_v, mu_lo, w0, w2p, a0, a2p, g2p, k_k, k_a, ones_bd):
    r = r + mu_r * (rp - r)
    k = k + mu_k * (kp - k)
    v = v + mu_v * (vp - v)
    lo = lo + mu_lo * (lop - lo)
    w_log = -_softplus(-(w0 + _nn(jnp.tanh(lo), w2p))) - 0.5
    lw = -jnp.exp(w_log)
    a_g = _sigmoid(a0 + _nn(lo, a2p))
    g = _nn(_sigmoid(lo), g2p)
    kk = k * k_k
    kk = kk / jnp.maximum(jnp.sqrt(_group_sum(kk * kk, ones_bd)), L2_EPS)
    k2 = k * (1.0 + (a_g - 1.0) * k_a)
    return r, lw, k2, v, -kk, kk * a_g, g


def _rwkv_post_f(y, r, k2, v, g, r_k, gn_w, gn_b, ones_bd):
    inv_n = 1.0 / HB_DIM
    mean = _group_sum(y, ones_bd) * inv_n
    yc = y - mean
    var = _group_sum(yc * yc, ones_bd) * inv_n
    yn = yc * lax.rsqrt(var + RWKV_GN_EPS) * gn_w + gn_b
    bonus = _group_sum(r * k2 * r_k, ones_bd) * v
    return ((yn + bonus) * g,)


def _tri(c, strict=False):
    ii = lax.broadcasted_iota(jnp.int32, (c, c), 0)
    jj = lax.broadcasted_iota(jnp.int32, (c, c), 1)
    return (jj < ii) if strict else (jj <= ii)


def _hgrn_step(st0, q_a, f_a, i_a, g_a, l0, l1, onorm):
    nh, nj = len(q_a), len(q_a[0])
    c = q_a[0][0].shape[0]
    combos = [(j, h) for j in range(nj) for h in range(nh)]
    every = lambda fn: {q: fn(q) for q in combos}
    at_ = lambda d: (lambda q: d[q[1]][q[0]])
    qa_, fa_, ia_, ga_ = (at_(z) for z in (q_a, f_a, i_a, g_a))
    incl = _tri(c)
    rows = lax.broadcasted_iota(jnp.int32, (c, 1), 0)
    lb = []
    for h in range(nh):
        mx = jnp.maximum(l0[h], l1[h])
        e0, e1 = jnp.exp(l0[h] - mx), jnp.exp(l1[h] - mx)
        lb.append(e0 / (e0 + e1))
    forget = every(lambda q: lb[q[1]] + (1.0 - lb[q[1]]) * _sigmoid(fa_(q)))
    qs = every(lambda q: _silu(qa_(q)))
    kk = every(lambda q: 1.0 - forget[q])
    lf = every(lambda q: jnp.log(forget[q]))
    bcum = every(lambda q: _cumsum_rows(lf[q]))
    bref = every(lambda q: jnp.sum(jnp.where(rows <= c // 2, lf[q], 0.0), axis=0, keepdims=True))
    blast = every(lambda q: jnp.sum(lf[q], axis=0, keepdims=True))
    scores = every(lambda q: jnp.where(incl, _nt(qs[q] * jnp.exp(bcum[q] - bref[q]),
                                                 kk[q] * jnp.exp(bref[q] - bcum[q])), 0.0))
    intra = every(lambda q: _nn(scores[q], ia_(q)))
    qb = every(lambda q: qs[q] * jnp.exp(bcum[q]))
    upd = every(lambda q: _tn(ia_(q), kk[q] * jnp.exp(blast[q] - bcum[q])))
    dec = every(lambda q: jnp.exp(blast[q]))
    st = list(st0)
    o = {}
    for j in range(nj):
        for h in range(nh):
            o[(j, h)] = intra[(j, h)] + _nt(qb[(j, h)], st[h])
        st = [st[h] * dec[(j, h)] + upd[(j, h)] for h in range(nh)]
    out = every(lambda q: o[q] * lax.rsqrt(jnp.mean(o[q] * o[q], axis=-1, keepdims=True) + NORM_EPS)
                * onorm[q[1]] * _silu(ga_(q)))
    return [[out[(j, h)] for j in range(nj)] for h in range(nh)], st


def _hgrn_blocks(ref, nj, c):
    return [[ref[j * c:(j + 1) * c, h * HA_DIM:(h + 1) * HA_DIM] for j in range(nj)] for h in range(HA_HEADS)]


def _hgrn_cols(ref):
    return [ref[:, h * HA_DIM:(h + 1) * HA_DIM] for h in range(HA_HEADS)]


def _hgrn_fwd(p_h, l0, l1, onorm):
    t = p_h.shape[0]
    cc, nj = HGRN_CHUNK, HGRN_GROUP
    c = cc * nj
    n = t // c

    def body(q_ref, f_ref, i_ref, g_ref, l0_ref, l1_ref, on_ref, o_ref, hs_ref, st_ref):
        @pl.when(pl.program_id(0) == 0)
        def _():
            st_ref[...] = jnp.zeros_like(st_ref)

        hs_ref[0] = st_ref[...]
        o, st1 = _hgrn_step([st_ref[h] for h in range(HA_HEADS)],
                            *[_hgrn_blocks(ref, nj, cc) for ref in (q_ref, f_ref, i_ref, g_ref)],
                            _hgrn_cols(l0_ref), _hgrn_cols(l1_ref), _hgrn_cols(on_ref))
        for h in range(HA_HEADS):
            for j in range(nj):
                o_ref[j * cc:(j + 1) * cc, h * HA_DIM:(h + 1) * HA_DIM] = o[h][j]
            st_ref[h] = st1[h]

    col = lambda j: pl.BlockSpec((c, W_A), lambda i, j=j: (i, j))
    par = pl.BlockSpec((1, W_A), lambda i: (0, 0))
    return pl.pallas_call(
        body, name="hgrn_fwd", grid=(n,), in_specs=[col(0), col(1), col(2), col(3), par, par, par],
        out_specs=[pl.BlockSpec((c, W_A), lambda i: (i, 0)),
                   pl.BlockSpec((1, HA_HEADS, HA_DIM, HA_DIM), lambda i: (i, 0, 0, 0))],
        out_shape=[SDS((t, W_A), F32), SDS((n, HA_HEADS, HA_DIM, HA_DIM), F32)],
        scratch_shapes=[pltpu.VMEM((HA_HEADS, HA_DIM, HA_DIM), F32)],
        compiler_params=_params(("arbitrary",)))(p_h, p_h, p_h, p_h, l0, l1, onorm)


def _hgrn_bwd(p_h, l0, l1, onorm, hs, do, do_col, comm=None):
    t = p_h.shape[0]
    cc, nj = HGRN_CHUNK, HGRN_GROUP
    c = cc * nj
    n = t // c

    def body(q_ref, f_ref, i_ref, g_ref, l0_ref, l1_ref, on_ref, hs_ref, do_ref,
             dp_ref, dl0_ref, dl1_ref, don_ref, dst_ref):
        @pl.when(pl.program_id(0) == 0)
        def _():
            dst_ref[...] = jnp.zeros_like(dst_ref)
            dl0_ref[...] = jnp.zeros_like(dl0_ref)
            dl1_ref[...] = jnp.zeros_like(dl1_ref)
            don_ref[...] = jnp.zeros_like(don_ref)

        args = ([hs_ref[0, h] for h in range(HA_HEADS)],
                *[_hgrn_blocks(ref, nj, cc) for ref in (q_ref, f_ref, i_ref, g_ref)],
                _hgrn_cols(l0_ref), _hgrn_cols(l1_ref), _hgrn_cols(on_ref))
        _, vjp = jax.vjp(_hgrn_step, *args)
        dst0, dq, df, di, dg, dl0, dl1, don = vjp((_hgrn_blocks(do_ref, nj, cc),
                                                   [dst_ref[h] for h in range(HA_HEADS)]))
        for h in range(HA_HEADS):
            sl = slice(h * HA_DIM, (h + 1) * HA_DIM)
            for k, dv in enumerate((dq, df, di, dg)):
                for j in range(nj):
                    dp_ref[j * cc:(j + 1) * cc, k * W_A + h * HA_DIM:k * W_A + (h + 1) * HA_DIM] = dv[h][j]
            dl0_ref[:, sl] += dl0[h]
            dl1_ref[:, sl] += dl1[h]
            don_ref[:, sl] += don[h]
            dst_ref[h] = dst0[h]

    col = lambda j: pl.BlockSpec((c, W_A), lambda i, j=j: (n - 1 - i, j))
    par = pl.BlockSpec((1, W_A), lambda i: (0, 0))
    return _hosting_call(
        body, comm, name="hgrn_bwd", grid=(n,),
        in_specs=[col(0), col(1), col(2), col(3), par, par, par,
                  pl.BlockSpec((1, HA_HEADS, HA_DIM, HA_DIM), lambda i: (n - 1 - i, 0, 0, 0)),
                  pl.BlockSpec((c, W_A), lambda i: (n - 1 - i, do_col))],
        out_specs=[pl.BlockSpec((c, N_HGRN_COLS), lambda i: (n - 1 - i, 0)), par, par, par],
        out_shape=[SDS((t, N_HGRN_COLS), F32), SDS((1, W_A), F32), SDS((1, W_A), F32), SDS((1, W_A), F32)],
        scratch_shapes=[pltpu.VMEM((HA_HEADS, HA_DIM, HA_DIM), F32)],
        args=(p_h, p_h, p_h, p_h, l0, l1, onorm, hs, do))


HB_PAIRS = HB_HEADS // 2
PAIR_W = 2 * HB_DIM


def _head_lane_masks():
    lane = lax.broadcasted_iota(jnp.int32, (1, PAIR_W), 1)
    return (lane < HB_DIM).astype(F32), (lane >= HB_DIM).astype(F32)


@jax.custom_vjp
def _stack_heads(x):
    m0, m1 = _head_lane_masks()
    return jnp.concatenate([x * m0, x * m1], axis=0)


def _stack_heads_bwd(_, g):
    m0, m1 = _head_lane_masks()
    c = g.shape[0] // 2
    return (g[:c] * m0 + g[c:] * m1,)


_stack_heads.defvjp(lambda x: (_stack_heads(x), None), _stack_heads_bwd)


@jax.custom_vjp
def _unstack_heads(ys):
    c = ys.shape[0] // 2
    return ys[:c] + ys[c:]


_unstack_heads.defvjp(lambda ys: (_unstack_heads(ys), None), lambda _, g: (_stack_heads(g),))


def _same_head_block(c):
    ii = lax.broadcasted_iota(jnp.int32, (2 * c, 2 * c), 0)
    jj = lax.broadcasted_iota(jnp.int32, (2 * c, 2 * c), 1)
    same = (ii < c) == (jj < c)
    return same & (jj <= ii), same & (jj < ii), (ii == jj).astype(F32)


@jax.custom_vjp
def _rows_join(top, bottom):
    return jnp.concatenate([top, bottom], axis=0)


def _rows_join_bwd(n_top, g):
    return g[:n_top], g[n_top:]


_rows_join.defvjp(lambda top, bottom: (_rows_join(top, bottom), top.shape[0]), _rows_join_bwd)


def _rows_split_impl(x, n_top):
    return x[:n_top], x[n_top:]


_rows_split = jax.custom_vjp(_rows_split_impl, nondiff_argnums=(1,))
_rows_split.defvjp(lambda x, n_top: (_rows_split_impl(x, n_top), None),
                   lambda n_top, _, g: (jnp.concatenate([g[0], g[1]], axis=0),))


def _rwkv_step(s0, r, lw, k, v, a, b):
    npair, nj = len(r), len(r[0])
    c = r[0][0].shape[0]
    combos = [(j, p) for j in range(nj) for p in range(npair)]
    every = lambda fn: {q: fn(q) for q in combos}
    at_ = lambda d: (lambda q: d[q[1]][q[0]])
    r_, lw_, k_, v_, a_, b_ = (at_(z) for z in (r, lw, k, v, a, b))
    incl, strict, eye = _same_head_block(c)

    gam = every(lambda q: _cumsum_rows(lw_(q)))
    gtot = every(lambda q: jnp.sum(lw_(q), axis=0, keepdims=True))
    eneg = every(lambda q: jnp.exp(-gam[q]))
    edec = every(lambda q: jnp.exp(gtot[q] - gam[q]))
    at = every(lambda q: _stack_heads(a_(q) * jnp.exp(gam[q] - lw_(q))))
    rt = every(lambda q: _stack_heads(r_(q) * jnp.exp(gam[q])))
    bt = every(lambda q: _stack_heads(b_(q) * eneg[q]))
    kt = every(lambda q: _stack_heads(k_(q) * eneg[q]))
    bdec = every(lambda q: _stack_heads(b_(q) * edec[q]))
    kdec = every(lambda q: _stack_heads(k_(q) * edec[q]))
    vs = every(lambda q: _stack_heads(v_(q)))
    a_ab = every(lambda q: jnp.where(strict, _nt(at[q], bt[q]), 0.0))
    a_ak = every(lambda q: jnp.where(strict, _nt(at[q], kt[q]), 0.0))
    a_rb = every(lambda q: jnp.where(incl, _nt(rt[q], bt[q]), 0.0))
    a_rk = every(lambda q: jnp.where(incl, _nt(rt[q], kt[q]), 0.0))
    tinv = every(lambda q: eye + a_ab[q])
    pw = a_ab
    span = 2
    while span < c:
        pw = every(lambda q, pw=pw: _nn_x3(pw[q], pw[q]))
        tinv = every(lambda q, pw=pw, tinv=tinv: tinv[q] + _nn_x3(pw[q], tinv[q]))
        span *= 2
    akv = every(lambda q: _nn(a_ak[q], vs[q]))
    w1 = every(lambda q: _nn_x3(tinv[q], at[q]))
    u0 = every(lambda q: _nn_x3(tinv[q], akv[q]))
    wr = every(lambda q: _rows_join(w1[q], rt[q]))
    bk = every(lambda q: _rows_join(bdec[q], kdec[q]))
    yv = every(lambda q: _nn(a_rk[q], vs[q]))
    gdec = every(lambda q: jnp.exp(gtot[q]))

    s = list(s0)
    y = [[None] * nj for _ in range(npair)]
    for j in range(nj):
        both = {p: _rows_split(_nt(wr[(j, p)], s[p]), 2 * c) for p in range(npair)}
        u = {p: both[p][0] + u0[(j, p)] for p in range(npair)}
        for p in range(npair):
            y[p][j] = _unstack_heads(both[p][1] + _nn(a_rb[(j, p)], u[p]) + yv[(j, p)])
        s = [s[p] * gdec[(j, p)] + _tn(_rows_join(u[p], vs[(j, p)]), bk[(j, p)]) for p in range(npair)]
    return y, s


def _rwkv_blocks(ref, nj, c):
    return [[ref[j * c:(j + 1) * c, p * PAIR_W:(p + 1) * PAIR_W] for j in range(nj)] for p in range(HB_PAIRS)]


def _rwkv_fwd(seqs, comm=None):
    t = seqs[0].shape[0]
    c, nj = RWKV_CHUNK, RWKV_GROUP
    n = t // (c * nj)

    def body(r_ref, lw_ref, k_ref, v_ref, a_ref, b_ref, y_ref, hs_ref, st_ref):
        @pl.when(pl.program_id(0) == 0)
        def _():
            st_ref[...] = jnp.zeros_like(st_ref)

        hs_ref[0] = st_ref[...]
        s0 = [st_ref[p] for p in range(HB_PAIRS)]
        y, s1 = _rwkv_step(s0, *[_rwkv_blocks(ref, nj, c) for ref in (r_ref, lw_ref, k_ref, v_ref, a_ref, b_ref)])
        for p in range(HB_PAIRS):
            for j in range(nj):
                y_ref[j * c:(j + 1) * c, p * PAIR_W:(p + 1) * PAIR_W] = y[p][j]
            st_ref[p] = s1[p]

    seq = pl.BlockSpec((c * nj, W_B), lambda i: (i, 0))
    return _hosting_call(
        body, comm, name="rwkv_fwd", grid=(n,), in_specs=[seq] * 6,
        out_specs=[seq, pl.BlockSpec((1, HB_PAIRS, PAIR_W, PAIR_W), lambda i: (i, 0, 0, 0))],
        out_shape=[SDS((t, W_B), F32), SDS((n, HB_PAIRS, PAIR_W, PAIR_W), F32)],
        scratch_shapes=[pltpu.VMEM((HB_PAIRS, PAIR_W, PAIR_W), F32)], args=tuple(seqs))


def _rwkv_bwd(seqs, hs, dy, comm=None):
    t = seqs[0].shape[0]
    c, nj = RWKV_CHUNK, RWKV_GROUP
    n = t // (c * nj)

    def body(r_ref, lw_ref, k_ref, v_ref, a_ref, b_ref, hs_ref, dy_ref,
             dr_ref, dlw_ref, dk_ref, dv_ref, da_ref, db_ref, dst_ref):
        @pl.when(pl.program_id(0) == 0)
        def _():
            dst_ref[...] = jnp.zeros_like(dst_ref)

        s0 = [hs_ref[0, p] for p in range(HB_PAIRS)]
        seq_vals = [_rwkv_blocks(ref, nj, c) for ref in (r_ref, lw_ref, k_ref, v_ref, a_ref, b_ref)]
        _, vjp = jax.vjp(_rwkv_step, s0, *seq_vals)
        grads = vjp((_rwkv_blocks(dy_ref, nj, c), [dst_ref[p] for p in range(HB_PAIRS)]))
        for ref, gr in zip((dr_ref, dlw_ref, dk_ref, dv_ref, da_ref, db_ref), grads[1:]):
            for p in range(HB_PAIRS):
                for j in range(nj):
                    ref[j * c:(j + 1) * c, p * PAIR_W:(p + 1) * PAIR_W] = gr[p][j]
        m0, m1 = _head_lane_masks()
        rows0 = (lax.broadcasted_iota(jnp.int32, (PAIR_W, 1), 0) < HB_DIM).astype(F32)
        blocks = rows0 * m0 + (1.0 - rows0) * m1
        for p in range(HB_PAIRS):
            dst_ref[p] = grads[0][p] * blocks

    seq = pl.BlockSpec((c * nj, W_B), lambda i: (n - 1 - i, 0))
    return _hosting_call(
        body, comm, name="rwkv_bwd", grid=(n,),
        in_specs=[seq] * 6 + [pl.BlockSpec((1, HB_PAIRS, PAIR_W, PAIR_W), lambda i: (n - 1 - i, 0, 0, 0)), seq],
        out_specs=[seq] * 6, out_shape=[SDS((t, W_B), F32)] * 6,
        scratch_shapes=[pltpu.VMEM((HB_PAIRS, PAIR_W, PAIR_W), F32)], args=(*seqs, hs, dy))


def _final_loss(x3, fnorm, target, *, tm):
    t, d = x3.shape

    def body(x_ref, g_ref, t_ref, dx_ref, dg_ref, loss_ref):
        @pl.when(pl.program_id(0) == 0)
        def _():
            dg_ref[...] = jnp.zeros_like(dg_ref)
            loss_ref[...] = jnp.zeros_like(loss_ref)

        x, g = x_ref[...], g_ref[...]
        rinv = lax.rsqrt(jnp.mean(x * x, axis=-1, keepdims=True) + NORM_EPS)
        xh = x * rinv
        diff = xh * g - t_ref[...]
        loss_ref[...] += 0.5 * jnp.sum(jnp.mean(diff * diff, axis=-1, keepdims=True))
        dy = diff * (1.0 / d)
        dg_ref[...] += jnp.sum(dy * xh, axis=0, keepdims=True)
        dxh = dy * g
        dx_ref[...] = rinv * (dxh - xh * jnp.mean(dxh * xh, axis=-1, keepdims=True))

    row = pl.BlockSpec((tm, d), lambda i: (i, 0))
    return pl.pallas_call(
        body, name="final_loss", grid=(t // tm,), in_specs=[row, pl.BlockSpec((1, d), lambda i: (0, 0)), row],
        out_specs=[row, pl.BlockSpec((1, d), lambda i: (0, 0)), pl.BlockSpec((8, 128), lambda i: (0, 0))],
        out_shape=[SDS((t, d), F32), SDS((1, d), F32), SDS((8, 128), F32)],
        compiler_params=_params(("arbitrary",)))(x3, fnorm, target)


def _gate_up_act(h, wgt, wut, *, tm, tn, name, comm=None):
    t, d = h.shape
    tm = min(tm, t)

    def body(h_ref, g_ref, u_ref, a_out, u_out, act_out):
        hv = h_ref[...]
        a = _dg(hv, g_ref[...], 1, 1, False)
        u = _dg(hv, u_ref[...], 1, 1, False)
        a_out[...] = a.astype(a_out.dtype)
        u_out[...] = u.astype(u_out.dtype)
        act_out[...] = (_silu(a) * u).astype(act_out.dtype)

    wspec = pl.BlockSpec((tn, d), lambda i, j: (j, 0))
    ospec = pl.BlockSpec((tm, tn), lambda i, j: (i, j))
    return _hosting_call(
        body, comm, name=name, grid=(t // tm, D_FF // tn),
        in_specs=[pl.BlockSpec((tm, d), lambda i, j: (i, 0)), wspec, wspec], out_specs=[ospec, ospec, ospec],
        out_shape=[SDS((t, D_FF), BF16), SDS((t, D_FF), BF16), SDS((t, D_FF), BF16)], scratch_shapes=[],
        args=(h, wgt, wut))


def _dact_swiglu(dout, wd, a, u, *, tm, tn, name, comm=None):
    t, d = dout.shape
    tm = min(tm, t)

    def body(d_ref, w_ref, a_ref, u_ref, da_out, du_out):
        dact = 0.5 * _dg(d_ref[...], w_ref[...], 1, 1, False)
        av, uv = a_ref[...].astype(F32), u_ref[...].astype(F32)
        s = _sigmoid(av)
        da_out[...] = (dact * uv * (s * (1.0 + av * (1.0 - s)))).astype(da_out.dtype)
        du_out[...] = (dact * (av * s)).astype(du_out.dtype)

    tile = pl.BlockSpec((tm, tn), lambda i, j: (i, j))
    return _hosting_call(
        body, comm, name=name, grid=(t // tm, D_FF // tn),
        in_specs=[pl.BlockSpec((tm, d), lambda i, j: (i, 0)), pl.BlockSpec((tn, d), lambda i, j: (j, 0)), tile, tile],
        out_specs=[tile, tile], out_shape=[SDS((t, D_FF), BF16), SDS((t, D_FF), BF16)], scratch_shapes=[],
        args=(dout, wd, a, u))


class _Plan:
    def __init__(self):
        self.entries, self.counts = collections.defaultdict(list), {}

    def carry(self, host, comm_of, after):
        self.entries[host].append((comm_of, after))

    def comm(self, host, g):
        comms = [comm_of(g) for comm_of, _ in self.entries.get(host, [])]
        self.counts[host] = [len(c.arrays) for c in comms]
        return functools.reduce(_join_comms, comms) if comms else None

    def done(self, host, results, w):
        start = 0
        for (_, after), n in zip(self.entries.get(host, []), self.counts.get(host, [])):
            after(results[start:start + n], w)
            start += n


def _ffn_fwd(x, w, tag, plan, g):
    comm = plan.comm(f"{tag}_rms", g)
    res = _rowwise(_rms_f, [x], [w[f"{tag}_norm"]], [[0]], [BF16], tm=512, name=f"{tag}_rms", comm=comm)
    (h,), carried = res if comm is not None else (res, [])
    plan.done(f"{tag}_rms", carried, w)
    (a, u, act), carried = _gate_up_act(h, w[f"{tag}_wgt"], w[f"{tag}_wut"], tm=2048, tn=256, name=f"{tag}_gate_up",
                                        comm=plan.comm(f"{tag}_gate_up", g))
    plan.done(f"{tag}_gate_up", carried, w)
    comm = plan.comm(f"{tag}_down", g)
    out = _mm(act, w[f"{tag}_wd"], tm=1024, tn=D_MODEL, tk=D_FF, name=f"{tag}_down", res=x, scale=0.5, comm=comm)
    if comm is not None:
        out, carried = out
        plan.done(f"{tag}_down", carried, w)
    return out, (h, a, u, act)


def _ffn_bwd(dout, x, w, saved, tag, plan, g):
    h, a, u, act = saved

    def carrying(fn, host, *args, **kwargs):
        comm = plan.comm(host, g)
        res = fn(*args, name=host, comm=comm, **kwargs)
        out, carried = res if comm is not None else (res, [])
        plan.done(host, carried, w)
        return out

    (da, du), carried = _dact_swiglu(dout, w[f"{tag}_wd"], a, u, tm=2048, tn=256, name=f"{tag}_dact",
                                     comm=plan.comm(f"{tag}_dact", g))
    plan.done(f"{tag}_dact", carried, w)
    g[f"{tag}_wd"] = _mm(act, dout, ta=True, tm=D_FF // 2, tn=D_MODEL, tk=1024, name=f"{tag}_dwd", scale=0.5)
    g[f"{tag}_wgt"] = carrying(_mm, f"{tag}_dwg", da, h, ta=True, tm=D_FF // 2, tn=D_MODEL, tk=1024)
    g[f"{tag}_wut"] = carrying(_mm, f"{tag}_dwu", du, h, ta=True, tm=D_FF // 2, tn=D_MODEL, tk=1024)
    dh = carrying(_mm, f"{tag}_dh_g", da, w[f"{tag}_wgt"], tm=1024, tn=D_MODEL, tk=D_FF)
    dh = carrying(_mm, f"{tag}_dh_u", du, w[f"{tag}_wut"], tm=1024, tn=D_MODEL, tk=D_FF, res=dh)
    dx, g[f"{tag}_norm"] = _rowwise_bwd(_rms_f, [x], [w[f"{tag}_norm"]], [dh], x_grad=[True], p_grad=[True],
                                        dx_groups=[[0]], dx_dtypes=[F32], tm=512, name=f"{tag}_drms",
                                        extra={0: dout})
    return dx


def _local_step(x, target, w, plan=None):
    plan = plan or _Plan()
    ones_bd = jnp.kron(jnp.eye(HB_HEADS, dtype=F32), jnp.ones((HB_DIM, HB_DIM), F32))
    g = {}
    x1, ffn1_saved = _ffn_fwd(x, w, "ffn1", plan, g)
    hm, = _rowwise(_rms_f, [x1], [w["mix_norm"]], [[0]], [BF16], tm=512, name="mix_rms")
    p_h = _mm(hm, w["w_in_h"], tm=2048, tn=256, tk=D_MODEL, name="inproj_h")
    p_r = _mm(hm, w["w_in_r"], tm=2048, tn=256, tk=D_MODEL, name="inproj_r")
    o_a, hgrn_states = _hgrn_fwd(p_h, w["lb0"], w["lb1"], w["hgrn_out_norm"])

    mu = w["mu_pad"]
    prep_xs = [(p_r, W_B, 0), (p_r, W_B, 1), (p_r, W_B, 2), (p_r, LORA_PAD, 6),
               ("prev", p_r, W_B, 0), ("prev", p_r, W_B, 1), ("prev", p_r, W_B, 2), ("prev", p_r, LORA_PAD, 6)]
    prep_ps = [(mu, W_B, 0), (mu, W_B, 1), (mu, W_B, 2), (mu, LORA_PAD, 6), w["rwkv_w0"], w["w2_pad"], w["rwkv_a0"],
               w["a2_pad"], w["g2_pad"], w["rwkv_k_k"], w["rwkv_k_a"], ones_bd]
    prep_f = _rwkv_prep_f
    r, lw, k2, v, a_vec, b_vec, gate = _rowwise(prep_f, prep_xs, prep_ps, [[0], [1], [2], [3], [4], [5], [6]],
                                                [F32] * 7, tm=256, name="rwkv_prep")
    seqs = [r, lw, k2, v, a_vec, b_vec]
    (y, rwkv_states), carried = _rwkv_fwd(seqs, comm=plan.comm("rwkv_fwd", g))
    plan.done("rwkv_fwd", carried, w)
    post_f = _rwkv_post_f
    post_xs = [y, r, k2, v, gate]
    post_ps = [w["rwkv_r_k"], w["rwkv_gn_w"], w["rwkv_gn_b"], ones_bd]
    o_b, = _rowwise(post_f, post_xs, post_ps, [[0]], [F32], tm=256, name="rwkv_post")
    x2 = _mm(o_a, w["w_out_a"], tm=2048, tn=256, tk=W_A, name="outproj_a", res=x1)
    x2 = _mm(o_b, w["w_out_b"], tm=2048, tn=256, tk=W_B, name="outproj_b", res=x2)
    x3, ffn2_saved = _ffn_fwd(x2, w, "ffn2", plan, g)
    dx3, g["final_norm"], loss = _final_loss(x3, w["final_norm"], target, tm=256)

    dx2 = _ffn_bwd(dx3, x2, w, ffn2_saved, "ffn2", plan, g)
    do_a = _mm(dx2, w["w_out_a"], tb=True, tm=2048, tn=256, tk=D_MODEL, name="outproj_do_a")
    do_b = _mm(dx2, w["w_out_b"], tb=True, tm=2048, tn=256, tk=D_MODEL, name="outproj_do_b")
    g["w_out_a"] = _mm(o_a, dx2, ta=True, tm=W_A, tn=D_MODEL, tk=1024, name="outproj_dw_a")
    g["w_out_b"] = _mm(o_b, dx2, ta=True, tm=W_B, tn=D_MODEL, tk=1024, name="outproj_dw_b")

    (dp_h, g["lb0"], g["lb1"], g["hgrn_out_norm"]), carried = _hgrn_bwd(
        p_h, w["lb0"], w["lb1"], w["hgrn_out_norm"], hgrn_states, do_a, 0, comm=plan.comm("hgrn_bwd", g))
    plan.done("hgrn_bwd", carried, w)
    post_out = _rowwise_bwd(post_f, post_xs, post_ps, [do_b], x_grad=[True] * 5, p_grad=[True] * 3 + [False],
                            dx_groups=[[0], [1], [2], [3], [4]], dx_dtypes=[F32] * 5, tm=256, name="rwkv_post_bwd")
    dy, dr1, dk1, dv1, dgate, g["rwkv_r_k"], g["rwkv_gn_w"], g["rwkv_gn_b"] = post_out
    (dr2, dlw, dk2, dv2, da_vec, db_vec), carried = _rwkv_bwd(seqs, rwkv_states, dy, comm=plan.comm("rwkv_bwd", g))
    plan.done("rwkv_bwd", carried, w)

    def prep2_f(*vals):
        r_, lw_, k2_, v_, a_, b_, g_ = prep_f(*vals)
        return r_, lw_, k2_, v_, a_, b_, g_, r_, k2_, v_

    prep_out = _rowwise_bwd(prep2_f, prep_xs, prep_ps, [dr2, dlw, dk2, dv2, da_vec, db_vec, dgate, dr1, dk1, dv1],
                            x_grad=[True] * 8, p_grad=[True] * 11 + [False], dx_groups=[[0, 1, 2, 3], [4, 5, 6, 7]],
                            dx_dtypes=[F32, F32], tm=256, name="rwkv_prep_bwd")
    dpr_main, dpr_prev = prep_out[0], prep_out[1]
    (dmu_r, dmu_k, dmu_v, dmu_lo, g["rwkv_w0"], g["w2_pad"], g["rwkv_a0"], g["a2_pad"], g["g2_pad"],
     g["rwkv_k_k"], g["rwkv_k_a"]) = prep_out[2:]
    g["mu_pad"] = jnp.concatenate([dmu_r, dmu_k, dmu_v, dmu_lo], axis=1)
    dp_r, = _rowwise(lambda u_, s_: (u_ + s_,), [dpr_main, ("next", dpr_prev, N_RWKV_PAD, 0)], [], [[0]], [F32],
                     tm=512, name="rwkv_dp_sum")
    dhm = _mm(dp_h, w["w_in_h"], tb=True, tm=1024, tn=D_MODEL, tk=N_HGRN_COLS, name="inproj_dh_h")
    dhm = _mm(dp_r, w["w_in_r"], tb=True, tm=1024, tn=D_MODEL, tk=N_RWKV_PAD, name="inproj_dh_r", res=dhm)
    g["w_in_h"] = _mm(hm, dp_h, ta=True, tm=D_MODEL, tn=D_MODEL, tk=1024, name="inproj_dw_h")
    g["w_in_r"] = _mm(hm, dp_r, ta=True, tm=D_MODEL, tn=N_RWKV_PAD // 2, tk=1024, name="inproj_dw_r")
    mix_comm = plan.comm("mix_drms", g)
    mix_out = _rowwise_bwd(_rms_f, [x1], [w["mix_norm"]], [dhm], x_grad=[True], p_grad=[True], dx_groups=[[0]],
                           dx_dtypes=[F32], tm=512, name="mix_drms", extra={0: dx2}, comm=mix_comm)
    (dx1, g["mix_norm"]), carried = mix_out if mix_comm is not None else (mix_out, [])
    plan.done("mix_drms", carried, w)
    dx0 = _ffn_bwd(dx1, x, w, ffn1_saved, "ffn1", plan, g)
    return loss, dx0, g


HBM_SPEC = pl.BlockSpec(memory_space=pl.ANY)

Comm = collections.namedtuple("Comm", "arrays out_shapes aliased sem_shapes start finish")


def _join_comms(first, second):
    assert first.aliased == second.aliased
    n, s = len(first.arrays), len(first.sem_shapes)

    def start(ins, outs, sems):
        first.start(ins[:n], outs[:n], sems[:s])
        second.start(ins[n:], outs[n:], sems[s:])

    def finish(ins, outs, sems):
        first.finish(ins[:n], outs[:n], sems[:s])
        second.finish(ins[n:], outs[n:], sems[s:])

    return Comm(list(first.arrays) + list(second.arrays), list(first.out_shapes) + list(second.out_shapes),
                first.aliased, list(first.sem_shapes) + list(second.sem_shapes), start, finish)


def _run_comm(comm, name):
    n = len(comm.arrays)

    def body(*refs):
        ins, outs, sems = refs[:n], refs[n:2 * n], refs[2 * n:]
        comm.start(ins, outs, sems)
        comm.finish(ins, outs, sems)

    return pl.pallas_call(
        body, name=name, in_specs=[HBM_SPEC] * n, out_specs=[HBM_SPEC] * n, out_shape=list(comm.out_shapes),
        input_output_aliases={t: t for t in range(n)} if comm.aliased else {},
        scratch_shapes=list(comm.sem_shapes))(*comm.arrays)


def _hosting_call(body, comm, *, name, grid, in_specs, out_specs, out_shape, scratch_shapes, args):
    sem = ("arbitrary",) * len(grid)
    if comm is None:
        res = pl.pallas_call(body, name=name, grid=grid, in_specs=in_specs, out_specs=out_specs, out_shape=out_shape,
                             scratch_shapes=scratch_shapes, compiler_params=_params(sem))(*args)
        return list(res), []
    ni, no, ns, nc = len(in_specs), len(out_specs), len(scratch_shapes), len(comm.arrays)

    def wrapped(*refs):
        ins, cins = refs[:ni], refs[ni:ni + nc]
        outs, couts = refs[ni + nc:ni + nc + no], refs[ni + nc + no:ni + 2 * nc + no]
        scr, sems = refs[ni + 2 * nc + no:ni + 2 * nc + no + ns], refs[ni + 2 * nc + no + ns:]
        first = functools.reduce(jnp.logical_and, [pl.program_id(k) == 0 for k in range(len(grid))])
        last = functools.reduce(jnp.logical_and, [pl.program_id(k) == grid[k] - 1 for k in range(len(grid))])

        @pl.when(first)
        def _():
            comm.start(cins, couts, sems)

        body(*ins, *outs, *scr)

        @pl.when(last)
        def _():
            comm.finish(cins, couts, sems)

    res = pl.pallas_call(
        wrapped, name=name, grid=grid, in_specs=list(in_specs) + [HBM_SPEC] * nc,
        out_specs=list(out_specs) + [HBM_SPEC] * nc, out_shape=list(out_shape) + list(comm.out_shapes),
        scratch_shapes=list(scratch_shapes) + list(comm.sem_shapes),
        input_output_aliases={ni + t: no + t for t in range(nc)} if comm.aliased else {},
        compiler_params=_params(sem))(*args, *comm.arrays)
    return list(res[:no]), list(res[no:])


def _chips(x, y):
    return [(1 - x, y), (x, 1 - y), (1 - x, 1 - y)]


def _gather_comm(bufs):
    n = len(bufs)

    def copies(outs, sems):
        ici_send, ici_recv, d2d_send, d2d_recv = sems
        x, y, c = lax.axis_index("x"), lax.axis_index("y"), lax.axis_index("c")

        def half(t, slot, hc):
            hr = bufs[t].shape[1] // 2
            return outs[t].at[slot, pl.ds(pl.multiple_of(hc * hr, 16), hr), :]

        def ici(t, j, slot, px, py):
            return pltpu.make_async_remote_copy(src_ref=half(t, slot, c), dst_ref=half(t, slot, c),
                                                send_sem=ici_send.at[3 * t + j], recv_sem=ici_recv.at[3 * t + j],
                                                device_id=(px, py, c), device_id_type=MESH)

        def d2d(t, j, slot, hc):
            return pltpu.make_async_remote_copy(src_ref=half(t, slot, hc), dst_ref=half(t, slot, hc),
                                                send_sem=d2d_send.at[3 * t + j], recv_sem=d2d_recv.at[3 * t + j],
                                                device_id=(x, y, 1 - c), device_id_type=MESH)

        peers = [(t, j, px, py) for t in range(n) for j, (px, py) in enumerate(_chips(x, y))]
        return ici, d2d, peers, 2 * x + y, c

    def start(ins, outs, sems):
        ici, _, peers, me, _ = copies(outs, sems)
        for t, j, px, py in peers:
            ici(t, j, me, px, py).start()

    def finish(ins, outs, sems):
        ici, d2d, peers, me, c = copies(outs, sems)
        for t, j, px, py in peers:
            ici(t, j, 2 * px + py, px, py).wait_recv()
            d2d(t, j, 2 * px + py, c).start()
        for t, j, px, py in peers:
            d2d(t, j, 2 * px + py, 1 - c).wait_recv()
        for t, j, px, py in peers:
            ici(t, j, me, px, py).wait_send()
            d2d(t, j, 2 * px + py, c).wait_send()

    return Comm(list(bufs), [SDS(b.shape, b.dtype) for b in bufs], True, [pltpu.SemaphoreType.DMA((3 * n,))] * 4,
                start, finish)


def _sibling_exchange_comm(gs):
    n = len(gs)

    def copies(ins, outs, sems):
        x, y, c = lax.axis_index("x"), lax.axis_index("y"), lax.axis_index("c")
        cps = []
        for t in range(n):
            hr = gs[t].shape[1] // 2
            src = ins[t].at[:, pl.ds(pl.multiple_of((1 - c) * hr, SUBLANES), hr), :]
            cps.append(pltpu.make_async_remote_copy(src_ref=src, dst_ref=outs[t], send_sem=sems[0].at[t],
                                                    recv_sem=sems[1].at[t], device_id=(x, y, 1 - c),
                                                    device_id_type=MESH))
        return cps

    def start(ins, outs, sems):
        for cp in copies(ins, outs, sems):
            cp.start()

    def finish(ins, outs, sems):
        for cp in copies(ins, outs, sems):
            cp.wait()

    return Comm(list(gs), [SDS((N_CHIPS, g.shape[1] // 2, g.shape[2]), g.dtype) for g in gs], False,
                [pltpu.SemaphoreType.DMA((n,))] * 2, start, finish)


def _chip_exchange_comm(ss):
    n = len(ss)

    def copies(ins, outs, sems):
        x, y, c = lax.axis_index("x"), lax.axis_index("y"), lax.axis_index("c")
        me = 2 * x + y

        def copy(t, j, px, py, src_slot, dst_slot):
            return pltpu.make_async_remote_copy(src_ref=ins[t].at[src_slot], dst_ref=outs[t].at[dst_slot],
                                                send_sem=sems[0].at[3 * t + j], recv_sem=sems[1].at[3 * t + j],
                                                device_id=(px, py, c), device_id_type=MESH)

        peers = [(t, j, px, py) for t in range(n) for j, (px, py) in enumerate(_chips(x, y))]
        return copy, peers, me

    def start(ins, outs, sems):
        copy, peers, me = copies(ins, outs, sems)
        for t, j, px, py in peers:
            copy(t, j, px, py, 2 * px + py, me).start()

    def finish(ins, outs, sems):
        copy, peers, me = copies(ins, outs, sems)
        for t, j, px, py in peers:
            copy(t, j, px, py, me, 2 * px + py).wait_recv()
        for t, j, px, py in peers:
            copy(t, j, px, py, 2 * px + py, me).wait_send()

    return Comm(list(ss), [SDS(s.shape, s.dtype) for s in ss], False, [pltpu.SemaphoreType.DMA((3 * n,))] * 2,
                start, finish)


def _sibling_swap_comm(fs):
    n = len(fs)

    def copies(ins, outs, sems):
        x, y, c = lax.axis_index("x"), lax.axis_index("y"), lax.axis_index("c")
        return [pltpu.make_async_remote_copy(src_ref=ins[t], dst_ref=outs[t], send_sem=sems[0].at[t],
                                             recv_sem=sems[1].at[t], device_id=(x, y, 1 - c), device_id_type=MESH)
                for t in range(n)]

    def start(ins, outs, sems):
        for cp in copies(ins, outs, sems):
            cp.start()

    def finish(ins, outs, sems):
        for cp in copies(ins, outs, sems):
            cp.wait()

    return Comm(list(fs), [SDS(f.shape, f.dtype) for f in fs], False, [pltpu.SemaphoreType.DMA((n,))] * 2,
                start, finish)


def _row_tile(rows, cap=512):
    best = SUBLANES
    for tr in range(SUBLANES, min(rows, cap) + 1, SUBLANES):
        if rows % tr == 0:
            best = tr
    return best


def _add_halves(g4, r4, c_idx, name):
    _, hr, lanes = r4.shape
    tr = _row_tile(hr)
    nb = hr // tr

    def body(c_ref, a_ref, b_ref, o_ref):
        o_ref[...] = (a_ref[...] + b_ref[...]).astype(o_ref.dtype)

    grid_spec = pltpu.PrefetchScalarGridSpec(
        num_scalar_prefetch=1, grid=(N_CHIPS, nb),
        in_specs=[pl.BlockSpec((None, tr, lanes), lambda q, i, c_ref: (q, c_ref[0] * nb + i, 0)),
                  pl.BlockSpec((None, tr, lanes), lambda q, i, c_ref: (q, i, 0))],
        out_specs=pl.BlockSpec((None, tr, lanes), lambda q, i, c_ref: (q, i, 0)))
    return pl.pallas_call(body, name=name, grid_spec=grid_spec, out_shape=SDS(r4.shape, BF16),
                          compiler_params=_params(("parallel", "parallel")))(c_idx, g4, r4)


def _sum_chips(r4, s4, me_idx, name):
    _, rows, lanes = r4.shape
    tr = _row_tile(rows)

    def body(me_ref, a_ref, b_ref, c_ref, d_ref, own_ref, o_ref):
        own = own_ref[...].astype(F32)
        p = [jnp.where(me_ref[0] == q, own, ref[...].astype(F32)) for q, ref in enumerate((a_ref, b_ref, c_ref, d_ref))]
        o_ref[...] = ((p[0] + p[1]) + p[2]) + p[3]

    other = lambda q: (lambda i, me_ref: (jnp.where(me_ref[0] == q, (q + 1) % N_CHIPS, q), i, 0))
    grid_spec = pltpu.PrefetchScalarGridSpec(
        num_scalar_prefetch=1, grid=(rows // tr,),
        in_specs=[pl.BlockSpec((None, tr, lanes), other(q)) for q in range(N_CHIPS)]
        + [pl.BlockSpec((None, tr, lanes), lambda i, me_ref: (me_ref[0], i, 0))],
        out_specs=pl.BlockSpec((tr, lanes), lambda i, me_ref: (i, 0)))
    return pl.pallas_call(body, name=name, grid_spec=grid_spec, out_shape=SDS((rows, lanes), F32),
                          compiler_params=_params(("parallel",)))(me_idx, r4, r4, r4, r4, s4)


def _adamw(wf, g_own, g_other, mf, vf, c_idx, name):
    rows, lanes = wf.shape
    hr = rows // 2
    tr = _row_tile(hr)
    nb = hr // tr
    c1 = 1.0 / (1.0 - ADAM_B1 ** ADAM_STEP)
    c2 = 1.0 / (1.0 - ADAM_B2 ** ADAM_STEP)

    def body(c_ref, w_ref, go_ref, gx_ref, m_ref, v_ref, g_ref, d_ref, nm_ref, nv_ref):
        gv = jnp.where(pl.program_id(0) == c_ref[0], go_ref[...], gx_ref[...])
        m = ADAM_B1 * m_ref[...] + (1.0 - ADAM_B1) * gv
        v = ADAM_B2 * v_ref[...] + (1.0 - ADAM_B2) * (gv * gv)
        g_ref[...] = gv
        d_ref[...] = -ADAM_LR * ((m * c1) / (jnp.sqrt(v * c2) + ADAM_EPS) + ADAM_WD * w_ref[...])
        nm_ref[...] = m
        nv_ref[...] = v

    full = pl.BlockSpec((tr, lanes), lambda h, i, c_ref: (h * nb + i, 0))
    half = pl.BlockSpec((tr, lanes), lambda h, i, c_ref: (i, 0))
    grid_spec = pltpu.PrefetchScalarGridSpec(num_scalar_prefetch=1, grid=(2, nb),
                                             in_specs=[full, half, half, full, full], out_specs=[full] * 4)
    return pl.pallas_call(body, name=name, grid_spec=grid_spec, out_shape=[SDS((rows, lanes), F32)] * 4,
                          compiler_params=_params(("parallel", "parallel")))(c_idx, wf, g_own, g_other, mf, vf)


BIG = ("ffn1_w_gate", "ffn1_w_up", "ffn1_w_down", "ffn2_w_gate", "ffn2_w_up", "ffn2_w_down", "w_out", "w_in")
TRANSPOSED = ("ffn1_w_gate", "ffn1_w_up", "ffn2_w_gate", "ffn2_w_up")
PACKED = ("rwkv_w2", "rwkv_a2", "rwkv_g2")
SMALL_SHAPES = {"ffn1_norm": (1, D_MODEL), "mix_norm": (1, D_MODEL), "hgrn_lb_logits": (2, W_A),
                "hgrn_out_norm": (1, W_A), "rwkv_shift_mu": (1, N_RWKV_COLS), "rwkv_w0": (1, W_B),
                "rwkv_a0": (1, W_B), "rwkv_k_k": (1, W_B), "rwkv_k_a": (1, W_B),
                "rwkv_r_k": (1, HB_HEADS, HB_DIM), "rwkv_gn_w": (1, W_B), "rwkv_gn_b": (1, W_B),
                "ffn2_norm": (1, D_MODEL), "final_norm": (D_MODEL,)}
PACK_ELEMS = sum(_numel(_shard_shape(n)) for n in PACKED) + sum(_numel(SMALL_SHAPES[n]) for n in SMALL)
PACK_ROWS = -(-PACK_ELEMS // (32 * LANES)) * 32


def _to_rows(name, shard):
    return shard[0].T if name in TRANSPOSED else shard[0]


def _from_rows(name, rows):
    return (rows.T if name in TRANSPOSED else rows)[None]


def _pack(sharded, small):
    flat = jnp.concatenate([sharded[n].reshape(-1) for n in PACKED] + [small[n].reshape(-1) for n in SMALL])
    return jnp.pad(flat, (0, PACK_ROWS * LANES - flat.shape[0])).reshape(PACK_ROWS, LANES)


def _unpack(packed):
    flat, out, off = packed.reshape(-1), {}, 0
    for n in PACKED:
        shp = _shard_shape(n)
        out[n] = flat[off:off + _numel(shp)].reshape((1,) + shp)
        off += _numel(shp)
    for n in SMALL:
        shp = SMALL_SHAPES[n]
        out[n] = flat[off:off + _numel(shp)].reshape(shp)
        off += _numel(shp)
    return out


def _quarter(full, name, q):
    shape, ax = SHARDED_SHAPES[name]
    w = shape[ax] // N_CHIPS
    return lax.slice_in_dim(full, q * w, (q + 1) * w, axis=ax)


def kernel(x, ffn1_norm, ffn1_w_gate, ffn1_w_up, ffn1_w_down, mix_norm, w_in, hgrn_lb_logits, hgrn_out_norm, rwkv_shift_mu, rwkv_w0, rwkv_w2, rwkv_a0, rwkv_a2, rwkv_g2, rwkv_k_k, rwkv_k_a, rwkv_r_k, rwkv_gn_w, rwkv_gn_b, w_out, ffn2_norm, ffn2_w_gate, ffn2_w_up, ffn2_w_down, final_norm, loss_target, m_ffn1_norm, m_ffn1_w_gate, m_ffn1_w_up, m_ffn1_w_down, m_mix_norm, m_w_in, m_hgrn_lb_logits, m_hgrn_out_norm, m_rwkv_shift_mu, m_rwkv_w0, m_rwkv_w2, m_rwkv_a0, m_rwkv_a2, m_rwkv_g2, m_rwkv_k_k, m_rwkv_k_a, m_rwkv_r_k, m_rwkv_gn_w, m_rwkv_gn_b, m_w_out, m_ffn2_norm, m_ffn2_w_gate, m_ffn2_w_up, m_ffn2_w_down, m_final_norm, v_ffn1_norm, v_ffn1_w_gate, v_ffn1_w_up, v_ffn1_w_down, v_mix_norm, v_w_in, v_hgrn_lb_logits, v_hgrn_out_norm, v_rwkv_shift_mu, v_rwkv_w0, v_rwkv_w2, v_rwkv_a0, v_rwkv_a2, v_rwkv_g2, v_rwkv_k_k, v_rwkv_k_a, v_rwkv_r_k, v_rwkv_gn_w, v_rwkv_gn_b, v_w_out, v_ffn2_norm, v_ffn2_w_gate, v_ffn2_w_up, v_ffn2_w_down, v_final_norm):
    args = dict(locals())
    wts = {n: args[n] for n in ALL_WEIGHTS}
    moms = {n: args["m_" + n] for n in ALL_WEIGHTS}
    vars_ = {n: args["v_" + n] for n in ALL_WEIGHTS}

    me = 2 * lax.axis_index("x") + lax.axis_index("y")
    c_idx = lax.axis_index("c").astype(jnp.int32).reshape(1)
    me_idx = me.astype(jnp.int32).reshape(1)
    shard_of = {n: _to_rows(n, wts[n]).astype(BF16) for n in BIG}
    shard_of["packed"] = _pack(wts, {n: wts[n] for n in SMALL}).astype(BF16)
    group = {"ffn1": BIG[0:3], "ffn2": BIG[3:6]}

    def slot_bufs(names):
        return [lax.dynamic_update_slice(jnp.zeros((N_CHIPS,) + shard_of[n].shape, BF16), shard_of[n][None],
                                         (me, 0, 0)) for n in names]

    def ffn_weights(tag, gathered):
        return {f"{tag}_wgt": gathered[0].reshape(D_FF, D_MODEL), f"{tag}_wut": gathered[1].reshape(D_FF, D_MODEL),
                f"{tag}_wd": gathered[2].reshape(D_FF, D_MODEL)}

    def w_in_weights(gathered):
        w_in_full = jnp.concatenate([gathered[0][q] for q in range(N_CHIPS)], axis=1)
        return {"w_in_h": w_in_full[:, :N_HGRN_COLS],
                "w_in_r": jnp.pad(w_in_full[:, N_HGRN_COLS:], ((0, 0), (0, N_RWKV_PAD - N_RWKV_COLS)))}

    def mixer_weights(gathered):
        w_out_full = gathered[0].reshape(D_MODEL, D_MODEL)
        packs = gathered[1].reshape(N_CHIPS, PACK_ROWS * LANES)
        full, off = {}, 0
        for n in PACKED:
            shp = _shard_shape(n)
            full[n] = jnp.concatenate([packs[q, off:off + _numel(shp)].reshape(shp) for q in range(N_CHIPS)], axis=1)
            off += _numel(shp)
        zrow = lambda nrow: jnp.zeros((nrow, W_B), BF16)
        return {"w_out_a": w_out_full[:W_A], "w_out_b": w_out_full[W_A:],
                "w2_pad": jnp.concatenate([full["rwkv_w2"], zrow(LORA_PAD - 32)], axis=0),
                "a2_pad": jnp.concatenate([zrow(32), full["rwkv_a2"], zrow(LORA_PAD - 64)], axis=0),
                "g2_pad": jnp.concatenate([zrow(64), full["rwkv_g2"], zrow(LORA_PAD - 160)], axis=0)}

    plan = _Plan()
    w = {}
    plan.carry("ffn1_rms", lambda g: _gather_comm(slot_bufs(group["ffn1"][:2])),
               lambda res, w_: w_.update({"ffn1_wgt": res[0].reshape(D_FF, D_MODEL),
                                          "ffn1_wut": res[1].reshape(D_FF, D_MODEL)}))

    def after_gate_up(res, w_):
        w_["ffn1_wd"] = res[0].reshape(D_FF, D_MODEL)
        w_.update(mixer_weights(res[1:]))

    plan.carry("ffn1_gate_up", lambda g: _gather_comm(slot_bufs(("ffn1_w_down", "w_out", "packed"))), after_gate_up)
    plan.carry("ffn1_down", lambda g: _gather_comm(slot_bufs(("w_in",))), lambda res, w_: w_.update(w_in_weights(res)))
    plan.carry("rwkv_fwd", lambda g: _gather_comm(slot_bufs(group["ffn2"])),
               lambda res, w_: w_.update(ffn_weights("ffn2", res)))
    w["ffn1_norm"], w["ffn2_norm"] = ffn1_norm, ffn2_norm
    w["mix_norm"] = mix_norm
    w["lb0"], w["lb1"] = hgrn_lb_logits[0:1], hgrn_lb_logits[1:2]
    w["hgrn_out_norm"] = hgrn_out_norm
    w["mu_pad"] = jnp.pad(rwkv_shift_mu, ((0, 0), (0, N_RWKV_PAD - N_RWKV_COLS)))
    for n in ("rwkv_w0", "rwkv_a0", "rwkv_k_k", "rwkv_k_a", "rwkv_gn_w", "rwkv_gn_b"):
        w[n] = wts[n]
    w["rwkv_r_k"] = rwkv_r_k.reshape(1, W_B)
    w["final_norm"] = final_norm.reshape(1, D_MODEL)

    def reduce_rows(names, gs):
        r1 = _run_comm(_sibling_exchange_comm(gs), "grad_sibling_exchange")
        s4 = [_add_halves(gt, rt, c_idx, f"grad_add_halves_{n}") for gt, rt, n in zip(gs, r1, names)]
        r2 = _run_comm(_chip_exchange_comm(s4), "grad_chip_exchange")
        return [_sum_chips(rt, st, me_idx, f"grad_sum_chips_{n}") for rt, st, n in zip(r2, s4, names)]

    early = {}

    def reduce_early(names, grads_of, sibling_host, chips_host):
        def sibling_comm(g):
            early[names, "gs"] = grads_of(g)
            return _sibling_exchange_comm(early[names, "gs"])

        def after_sibling(res, w_):
            early[names, "s4"] = [_add_halves(gt, rt, c_idx, f"grad_add_halves_{n}")
                                  for gt, rt, n in zip(early[names, "gs"], res, names)]

        def after_chips(res, w_):
            early.update(zip(names, [_sum_chips(rt, st, me_idx, f"grad_sum_chips_{n}")
                                     for rt, st, n in zip(res, early[names, "s4"], names)]))

        plan.carry(sibling_host, sibling_comm, after_sibling)
        plan.carry(chips_host, lambda g: _chip_exchange_comm(early[names, "s4"]), after_chips)

    def proj_grads(g):
        g_w_in = jnp.concatenate([g["w_in_h"], g["w_in_r"][:, :N_RWKV_COLS]], axis=1)
        return [jnp.concatenate([g["w_out_a"], g["w_out_b"]], axis=0).reshape(N_CHIPS, -1, D_MODEL),
                jnp.stack([_quarter(g_w_in, "w_in", q) for q in range(N_CHIPS)])]

    rows_of = lambda keys: (lambda g: [g[k].reshape(N_CHIPS, -1, D_MODEL) for k in keys])
    reduce_early(group["ffn2"], rows_of(("ffn2_wgt", "ffn2_wut", "ffn2_wd")), "hgrn_bwd", "rwkv_bwd")
    reduce_early(("w_out", "w_in"), proj_grads, "mix_drms", "ffn1_dact")
    reduce_early(("ffn1_w_down",), rows_of(("ffn1_wd",)), "ffn1_dwg", "ffn1_dwu")
    reduce_early(("ffn1_w_gate",), rows_of(("ffn1_wgt",)), "ffn1_dwu", "ffn1_dh_g")
    reduce_early(("ffn1_w_up",), rows_of(("ffn1_wut",)), "ffn1_dh_g", "ffn1_dh_u")
    loss_slab, grad_x, g = _local_step(x[0], loss_target[0], w, plan)
    loss = lax.psum(loss_slab[0, 0], ("x", "y", "c"))

    gfull = {
        "rwkv_w2": g["w2_pad"][0:32], "rwkv_a2": g["a2_pad"][32:64], "rwkv_g2": g["g2_pad"][64:160],
    }
    gsmall = {
        "ffn1_norm": g["ffn1_norm"], "mix_norm": g["mix_norm"],
        "hgrn_lb_logits": jnp.concatenate([g["lb0"], g["lb1"]], axis=0), "hgrn_out_norm": g["hgrn_out_norm"],
        "rwkv_shift_mu": g["mu_pad"][:, :N_RWKV_COLS], "rwkv_w0": g["rwkv_w0"], "rwkv_a0": g["rwkv_a0"],
        "rwkv_k_k": g["rwkv_k_k"], "rwkv_k_a": g["rwkv_k_a"], "rwkv_r_k": g["rwkv_r_k"],
        "rwkv_gn_w": g["rwkv_gn_w"], "rwkv_gn_b": g["rwkv_gn_b"], "ffn2_norm": g["ffn2_norm"],
        "final_norm": g["final_norm"],
    }
    packed = jnp.stack([_pack({n: _quarter(gfull[n], n, q) for n in PACKED}, gsmall) for q in range(N_CHIPS)])
    early["packed"], = reduce_rows(["packed"], [packed])
    names = list(BIG) + ["packed"]
    own = [early[n] for n in names]
    other = _run_comm(_sibling_swap_comm(own), "grad_sibling_swap")

    def rows_list(d):
        return [_to_rows(n, d[n]) for n in BIG] + [_pack(d, {n: d[n] for n in SMALL})]

    outs = [_adamw(wt, go, gx, mt, vt, c_idx, f"adamw_{n}")
            for wt, go, gx, mt, vt, n in zip(rows_list(wts), own, other, rows_list(moms), rows_list(vars_), names)]
    results = []
    for k in range(4):
        per = [outs[i][k] for i in range(len(names))]
        d = {n: _from_rows(n, z) for n, z in zip(BIG, per[:-1])}
        d.update(_unpack(per[-1]))
        results.append(d)
    return (loss, grad_x[None], *[r[n] for r in results for n in ALL_WEIGHTS])
```

```python
import collections
import functools

import jax
import jax.numpy as jnp
from jax import lax
from jax.experimental import pallas as pl
from jax.experimental.pallas import tpu as pltpu

F32 = jnp.float32
BF16 = jnp.bfloat16
SDS = jax.ShapeDtypeStruct
MESH = pl.DeviceIdType.MESH

D_MODEL = 1024
D_FF = 2816
W_A = 512
W_B = 512
HA_HEADS, HA_DIM = 4, 128
HB_HEADS, HB_DIM = 8, 64
HGRN_CHUNK = 64
HGRN_GROUP = 2
RWKV_CHUNK = 16
RWKV_GROUP = 4
N_HGRN_COLS = 4 * W_A
N_RWKV_COLS = 3 * W_B + 32 + 32 + 96
N_RWKV_PAD = 1792
LORA_PAD = 256
NORM_EPS = 1e-6
RWKV_GN_EPS = 64e-5
L2_EPS = 1e-12
ADAM_LR, ADAM_B1, ADAM_B2, ADAM_EPS, ADAM_WD, ADAM_STEP = 0.001, 0.9, 0.999, 1e-8, 0.01, 10

N_CHIPS = 4
VMEM_LIMIT_V7X = 56 * 1024 * 1024
LANES = 1024

SHARDED_SHAPES = {
    "ffn1_w_gate": ((D_MODEL, D_FF), 1), "ffn1_w_up": ((D_MODEL, D_FF), 1), "ffn1_w_down": ((D_FF, D_MODEL), 0),
    "w_in": ((D_MODEL, N_HGRN_COLS + N_RWKV_COLS), 1), "rwkv_w2": ((32, W_B), 1), "rwkv_a2": ((32, W_B), 1),
    "rwkv_g2": ((96, W_B), 1), "w_out": ((D_MODEL, D_MODEL), 0),
    "ffn2_w_gate": ((D_MODEL, D_FF), 1), "ffn2_w_up": ((D_MODEL, D_FF), 1), "ffn2_w_down": ((D_FF, D_MODEL), 0),
}
SMALL = ("ffn1_norm", "mix_norm", "hgrn_lb_logits", "hgrn_out_norm", "rwkv_shift_mu", "rwkv_w0", "rwkv_a0",
         "rwkv_k_k", "rwkv_k_a", "rwkv_r_k", "rwkv_gn_w", "rwkv_gn_b", "ffn2_norm", "final_norm")
ALL_WEIGHTS = ("ffn1_norm", "ffn1_w_gate", "ffn1_w_up", "ffn1_w_down", "mix_norm", "w_in", "hgrn_lb_logits",
               "hgrn_out_norm", "rwkv_shift_mu", "rwkv_w0", "rwkv_w2", "rwkv_a0", "rwkv_a2", "rwkv_g2", "rwkv_k_k",
               "rwkv_k_a", "rwkv_r_k", "rwkv_gn_w", "rwkv_gn_b", "w_out", "ffn2_norm", "ffn2_w_gate", "ffn2_w_up",
               "ffn2_w_down", "final_norm")


def _shard_shape(name):
    shape, ax = SHARDED_SHAPES[name]
    return tuple(s // N_CHIPS if i == ax else s for i, s in enumerate(shape))


def _numel(shape):
    n = 1
    for s in shape:
        n *= s
    return n


def _params(sem=None):
    return pltpu.CompilerParams(dimension_semantics=sem, vmem_limit_bytes=VMEM_LIMIT_V7X)


def _split2(x):
    hi = x.astype(BF16)
    return hi, (x.astype(F32) - hi.astype(F32)).astype(BF16)


def _dg(x, y, cx, cy, hi):
    dn = (((cx,), (cy,)), ((), ()))
    dot = lambda p, q: lax.dot_general(p, q, dn, preferred_element_type=F32)
    if hi == "x3":
        (xh, xl), (yh, yl) = _split2(x), _split2(y)
        return dot(xh, yh) + (dot(xh, yl) + dot(xl, yh))
    return dot(x.astype(BF16), y.astype(BF16))


def _make_mm(hi, cotangent_forms=None):
    @jax.custom_vjp
    def nn(x, y):
        return _dg(x, y, 1, 0, hi)

    @jax.custom_vjp
    def nt(x, y):
        return _dg(x, y, 1, 1, hi)

    @jax.custom_vjp
    def tn(x, y):
        return _dg(x, y, 0, 0, hi)

    bnn, bnt, btn = cotangent_forms or (nn, nt, tn)
    nn.defvjp(lambda x, y: (nn(x, y), (x, y)), lambda r, g: (bnt(g, r[1]), btn(r[0], g)))
    nt.defvjp(lambda x, y: (nt(x, y), (x, y)), lambda r, g: (bnn(g, r[1]), btn(g, r[0])))
    tn.defvjp(lambda x, y: (tn(x, y), (x, y)), lambda r, g: (bnt(r[1], g), bnn(r[0], g)))
    return nn, nt, tn


_nn, _nt, _tn = _make_mm(False)
_nn_x3, _nt_x3, _tn_x3 = _make_mm("x3", (_nn, _nt, _tn))


def _tri_apply(x, transpose):
    c = x.shape[0]
    tri = (lax.broadcasted_iota(jnp.int32, (c, c), 1) <= lax.broadcasted_iota(jnp.int32, (c, c), 0)).astype(BF16)
    dn = (((0 if transpose else 1,), (0,)), ((), ()))
    p1 = x.astype(BF16)
    r1 = x - p1.astype(F32)
    p2 = r1.astype(BF16)
    p3 = (r1 - p2.astype(F32)).astype(BF16)
    dot = lambda p: lax.dot_general(tri, p, dn, preferred_element_type=F32)
    return dot(p1) + (dot(p2) + dot(p3))


@jax.custom_vjp
def _cumsum_rows(x):
    return _tri_apply(x, False)


_cumsum_rows.defvjp(lambda x: (_tri_apply(x, False), None), lambda _, g: (_tri_apply(g, True),))


def _sigmoid(x):
    return 1.0 / (1.0 + jnp.exp(-x))


def _silu(x):
    return x * _sigmoid(x)


def _softplus(z):
    return jnp.maximum(z, 0.0) + jnp.log(1.0 + jnp.exp(-jnp.abs(z)))


def _mm(a, b, *, ta=False, tb=False, tm, tn, tk, name, out_dtype=F32, res=None, scale=None, comm=None):
    m = a.shape[1] if ta else a.shape[0]
    kdim = a.shape[0] if ta else a.shape[1]
    n = b.shape[0] if tb else b.shape[1]
    assert (b.shape[1] if tb else b.shape[0]) == kdim
    tm, tn, tk = min(tm, m), min(tn, n), min(tk, kdim)
    assert m % tm == 0 and n % tn == 0 and kdim % tk == 0, (name, m, n, kdim)
    nk = kdim // tk
    a_spec = pl.BlockSpec((tk, tm), lambda i, j, k: (k, i)) if ta else pl.BlockSpec((tm, tk), lambda i, j, k: (i, k))
    b_spec = pl.BlockSpec((tn, tk), lambda i, j, k: (j, k)) if tb else pl.BlockSpec((tk, tn), lambda i, j, k: (k, j))
    o_spec = pl.BlockSpec((tm, tn), lambda i, j, k: (i, j))
    ca, cb = (0 if ta else 1), (1 if tb else 0)

    def body(*refs):
        if res is not None:
            a_ref, b_ref, r_ref, o_ref, acc_ref = refs
        else:
            a_ref, b_ref, o_ref, acc_ref = refs
        k = pl.program_id(2)

        @pl.when(k == 0)
        def _():
            acc_ref[...] = jnp.zeros_like(acc_ref)

        acc_ref[...] += _dg(a_ref[...], b_ref[...], ca, cb, False)

        @pl.when(k == nk - 1)
        def _():
            acc = acc_ref[...]
            if scale is not None:
                acc = acc * scale
            if res is not None:
                acc = r_ref[...] + acc
            o_ref[...] = acc.astype(out_dtype)

    in_specs = [a_spec, b_spec] + ([o_spec] if res is not None else [])
    args = (a, b) + ((res,) if res is not None else ())
    if comm is None:
        return pl.pallas_call(
            body, name=name, grid=(m // tm, n // tn, nk), in_specs=in_specs, out_specs=o_spec,
            out_shape=SDS((m, n), out_dtype), scratch_shapes=[pltpu.VMEM((tm, tn), F32)],
            compiler_params=_params(("parallel", "parallel", "arbitrary")))(*args)
    (out,), carried = _hosting_call(
        body, comm, name=name, grid=(m // tm, n // tn, nk), in_specs=in_specs, out_specs=[o_spec],
        out_shape=[SDS((m, n), out_dtype)], scratch_shapes=[pltpu.VMEM((tm, tn), F32)], args=args)
    return out, carried


def _row_spec(x, tm):
    if isinstance(x, tuple):
        arr, w, j = x
        return arr, pl.BlockSpec((tm, w), lambda i, j=j: (i, j))
    return x, pl.BlockSpec((tm, x.shape[1]), lambda i: (i, 0))


def _par_spec(p):
    if isinstance(p, tuple):
        arr, w, j = p
        return arr, pl.BlockSpec((arr.shape[0], w), lambda i, j=j: (0, j))
    return p, pl.BlockSpec(p.shape, lambda i: (0, 0))


def _store_groups(refs, groups, vals):
    for ref, idxs in zip(refs, groups):
        off = 0
        for ix in idxs:
            v = vals[ix]
            ref[:, off:off + v.shape[1]] = v.astype(ref.dtype)
            off += v.shape[1]


SUBLANES = 8


def _x_plan(xs, tm, t):
    arrays, specs, plan = [], [], []
    nb = tm // SUBLANES
    for x in xs:
        if isinstance(x, tuple) and isinstance(x[0], str):
            kind, arr, w, j = x
            if kind == "prev":
                halo = lambda i, j=j: (jnp.maximum(i * nb - 1, 0), j)
            else:
                halo = lambda i, j=j: (jnp.minimum((i + 1) * nb, t // SUBLANES - 1), j)
            arrays += [arr, arr]
            specs += [pl.BlockSpec((tm, w), lambda i, j=j: (i, j)), pl.BlockSpec((SUBLANES, w), halo)]
            plan.append((kind, 2, w))
        else:
            arr, spec = _row_spec(x, tm)
            arrays.append(arr)
            specs.append(spec)
            plan.append(("plain", 1, spec.block_shape[1]))
    return arrays, specs, plan


def _x_vals(refs, plan, tm, nt):
    vals, k = [], 0
    i = pl.program_id(0)
    rows = lax.broadcasted_iota(jnp.int32, (tm, 1), 0)
    for kind, n, _ in plan:
        main = refs[k][...].astype(F32)
        if kind == "prev":
            edge = jnp.where(i == 0, 0.0, refs[k + 1][SUBLANES - 1:SUBLANES, :].astype(F32))
            main = jnp.where(rows == 0, edge, pltpu.roll(main, 1, 0))
        elif kind == "next":
            edge = jnp.where(i == nt - 1, 0.0, refs[k + 1][0:1, :].astype(F32))
            main = jnp.where(rows == tm - 1, edge, pltpu.roll(main, tm - 1, 0))
        vals.append(main)
        k += n
    return vals


def _tile_rows(xs, tm):
    arr = xs[0]
    if isinstance(arr, tuple):
        arr = arr[1] if isinstance(arr[0], str) else arr[0]
    return min(tm, arr.shape[0]), arr.shape[0]


def _rowwise(f, xs, params, out_groups, out_dtypes, *, tm, name, comm=None):
    tm, t = _tile_rows(xs, tm)
    nt = t // tm
    xa, xspecs, plan = _x_plan(xs, tm, t)
    pa, pspecs = (zip(*[_par_spec(p) for p in params]) if params else ((), ()))
    nxr, npar = len(xa), len(pa)
    x_sds = [SDS((tm, w), F32) for _, _, w in plan]
    p_sds = [SDS(s.block_shape, F32) for s in pspecs]
    outs_sds = jax.eval_shape(lambda *vals: f(*vals), *x_sds, *p_sds)
    widths = [sum(outs_sds[ix].shape[1] for ix in idxs) for idxs in out_groups]

    def body(*refs):
        vals = _x_vals(refs[:nxr], plan, tm, nt) + [r[...].astype(F32) for r in refs[nxr:nxr + npar]]
        outs = f(*vals)
        _store_groups(refs[nxr + npar:], out_groups, outs)

    res, carried = _hosting_call(
        body, comm, name=name, grid=(nt,), in_specs=list(xspecs) + list(pspecs),
        out_specs=[pl.BlockSpec((tm, w), lambda i: (i, 0)) for w in widths],
        out_shape=[SDS((t, w), dt) for w, dt in zip(widths, out_dtypes)], scratch_shapes=[], args=(*xa, *pa))
    return res if comm is None else (res, carried)


def _rowwise_bwd(f, xs, params, cots, *, x_grad, p_grad, dx_groups, dx_dtypes, tm, name, extra=None, comm=None):
    tm, t = _tile_rows(xs, tm)
    nt = t // tm
    xa, xspecs, plan = _x_plan(xs, tm, t)
    pa, pspecs = (zip(*[_par_spec(p) for p in params]) if params else ((), ()))
    ca, cspecs = zip(*[_row_spec(c, tm) for c in cots])
    extra = extra or {}
    ekeys = sorted(extra)
    ea, especs = (zip(*[_row_spec(extra[k], tm) for k in ekeys]) if ekeys else ((), ()))
    nx, nxr, npar, nc, ne = len(plan), len(xa), len(pa), len(ca), len(ea)
    gx = [i for i in range(nx) if x_grad[i]]
    gp = [i for i in range(npar) if p_grad[i]]
    widths = [sum(plan[gx[ix]][2] for ix in idxs) for idxs in dx_groups]
    ng = len(dx_groups)

    def body(*refs):
        ins = refs[:nxr + npar + nc + ne]
        outs = refs[nxr + npar + nc + ne:]
        vals = _x_vals(ins[:nxr], plan, tm, nt) + [r[...].astype(F32) for r in ins[nxr:nxr + npar]]
        cvals = tuple(r[...].astype(F32) for r in ins[nxr + npar:nxr + npar + nc])
        evals = [r[...].astype(F32) for r in ins[nxr + npar + nc:]]
        diff_idx = gx + [nx + i for i in gp]

        def g(*dargs):
            full = list(vals)
            for ix, v in zip(diff_idx, dargs):
                full[ix] = v
            return tuple(f(*full))

        _, vjp = jax.vjp(g, *[vals[ix] for ix in diff_idx])
        grads = vjp(cvals)
        dxs = list(grads[:len(gx)])
        for k, ev in zip(ekeys, evals):
            dxs[k] = dxs[k] + ev
        _store_groups(outs[:ng], dx_groups, dxs)
        i = pl.program_id(0)
        for ref, gval in zip(outs[ng:], grads[len(gx):]):
            @pl.when(i == 0)
            def _(ref=ref):
                ref[...] = jnp.zeros_like(ref)
            ref[...] += gval

    dp_specs = [pl.BlockSpec(pspecs[i].block_shape, lambda i: (0, 0)) for i in gp]
    dp_shapes = [SDS(pspecs[i].block_shape, F32) for i in gp]
    res, carried = _hosting_call(
        body, comm, name=name, grid=(nt,), in_specs=list(xspecs) + list(pspecs) + list(cspecs) + list(especs),
        out_specs=[pl.BlockSpec((tm, w), lambda i: (i, 0)) for w in widths] + dp_specs,
        out_shape=[SDS((t, w), dt) for w, dt in zip(widths, dx_dtypes)] + dp_shapes, scratch_shapes=[],
        args=(*xa, *pa, *ca, *ea))
    return res if comm is None else (res, carried)


def _rms_f(x, g):
    return (x * lax.rsqrt(jnp.mean(x * x, axis=-1, keepdims=True) + NORM_EPS) * g,)


def _three_pieces(x):
    p1 = x.astype(BF16)
    r1 = x - p1.astype(F32)
    p2 = r1.astype(BF16)
    return p1, p2, (r1 - p2.astype(F32)).astype(BF16)


def _group_sum_impl(x, ones_bd):
    p1, p2, p3 = _three_pieces(x)
    dot = lambda p: lax.dot_general(p, ones_bd.astype(BF16), (((1,), (0,)), ((), ())), preferred_element_type=F32)
    return dot(p1) + (dot(p2) + dot(p3))


@jax.custom_vjp
def _group_sum(x, ones_bd):
    return _group_sum_impl(x, ones_bd)


_group_sum.defvjp(lambda x, o: (_group_sum_impl(x, o), o),
                  lambda o, g: (_group_sum_impl(g, o), jnp.zeros_like(o)))


def _rwkv_prep_f(r, k, v, lo, rp, kp, vp, lop, mu_r, mu_k, mu_v, mu_lo, w0, w2p, a0, a2p, g2p, k_k, k_a, ones_bd):
    r = r + mu_r * (rp - r)
    k = k + mu_k * (kp - k)
    v = v + mu_v * (vp - v)
    lo = lo + mu_lo * (lop - lo)
    w_log = -_softplus(-(w0 + _nn(jnp.tanh(lo), w2p))) - 0.5
    lw = -jnp.exp(w_log)
    a_g = _sigmoid(a0 + _nn(lo, a2p))
    g = _nn(_sigmoid(lo), g2p)
    kk = k * k_k
    kk = kk / jnp.maximum(jnp.sqrt(_group_sum(kk * kk, ones_bd)), L2_EPS)
    k2 = k * (1.0 + (a_g - 1.0) * k_a)
    return r, lw, k2, v, -kk, kk * a_g, g


def _rwkv_post_f(y, r, k2, v, g, r_k, gn_w, gn_b, ones_bd):
    inv_n = 1.0 / HB_DIM
    mean = _group_sum(y, ones_bd) * inv_n
    yc = y - mean
    var = _group_sum(yc * yc, ones_bd) * inv_n
    yn = yc * lax.rsqrt(var + RWKV_GN_EPS) * gn_w + gn_b
    bonus = _group_sum(r * k2 * r_k, ones_bd) * v
    return ((yn + bonus) * g,)


def _tri(c, strict=False):
    ii = lax.broadcasted_iota(jnp.int32, (c, c), 0)
    jj = lax.broadcasted_iota(jnp.int32, (c, c), 1)
    return (jj < ii) if strict else (jj <= ii)


def _hgrn_step(st0, q_a, f_a, i_a, g_a, l0, l1, onorm):
    nh, nj = len(q_a), len(q_a[0])
    c = q_a[0][0].shape[0]
    combos = [(j, h) for j in range(nj) for h in range(nh)]
    every = lambda fn: {q: fn(q) for q in combos}
    at_ = lambda d: (lambda q: d[q[1]][q[0]])
    qa_, fa_, ia_, ga_ = (at_(z) for z in (q_a, f_a, i_a, g_a))
    incl = _tri(c)
    rows = lax.broadcasted_iota(jnp.int32, (c, 1), 0)
    lb = []
    for h in range(nh):
        mx = jnp.maximum(l0[h], l1[h])
        e0, e1 = jnp.exp(l0[h] - mx), jnp.exp(l1[h] - mx)
        lb.append(e0 / (e0 + e1))
    forget = every(lambda q: lb[q[1]] + (1.0 - lb[q[1]]) * _sigmoid(fa_(q)))
    qs = every(lambda q: _silu(qa_(q)))
    kk = every(lambda q: 1.0 - forget[q])
    lf = every(lambda q: jnp.log(forget[q]))
    bcum = every(lambda q: _cumsum_rows(lf[q]))
    bref = every(lambda q: jnp.sum(jnp.where(rows <= c // 2, lf[q], 0.0), axis=0, keepdims=True))
    blast = every(lambda q: jnp.sum(lf[q], axis=0, keepdims=True))
    scores = every(lambda q: jnp.where(incl, _nt(qs[q] * jnp.exp(bcum[q] - bref[q]),
                                                 kk[q] * jnp.exp(bref[q] - bcum[q])), 0.0))
    intra = every(lambda q: _nn(scores[q], ia_(q)))
    qb = every(lambda q: qs[q] * jnp.exp(bcum[q]))
    upd = every(lambda q: _tn(ia_(q), kk[q] * jnp.exp(blast[q] - bcum[q])))
    dec = every(lambda q: jnp.exp(blast[q]))
    st = list(st0)
    o = {}
    for j in range(nj):
        for h in range(nh):
            o[(j, h)] = intra[(j, h)] + _nt(qb[(j, h)], st[h])
        st = [st[h] * dec[(j, h)] + upd[(j, h)] for h in range(nh)]
    out = every(lambda q: o[q] * lax.rsqrt(jnp.mean(o[q] * o[q], axis=-1, keepdims=True) + NORM_EPS)
                * onorm[q[1]] * _silu(ga_(q)))
    return [[out[(j, h)] for j in range(nj)] for h in range(nh)], st


def _hgrn_blocks(ref, nj, c):
    return [[ref[j * c:(j + 1) * c, h * HA_DIM:(h + 1) * HA_DIM] for j in range(nj)] for h in range(HA_HEADS)]


def _hgrn_cols(ref):
    return [ref[:, h * HA_DIM:(h + 1) * HA_DIM] for h in range(HA_HEADS)]


def _hgrn_fwd(p_h, l0, l1, onorm):
    t = p_h.shape[0]
    cc, nj = HGRN_CHUNK, HGRN_GROUP
    c = cc * nj
    n = t // c

    def body(q_ref, f_ref, i_ref, g_ref, l0_ref, l1_ref, on_ref, o_ref, hs_ref, st_ref):
        @pl.when(pl.program_id(0) == 0)
        def _():
            st_ref[...] = jnp.zeros_like(st_ref)

        hs_ref[0] = st_ref[...]
        o, st1 = _hgrn_step([st_ref[h] for h in range(HA_HEADS)],
                            *[_hgrn_blocks(ref, nj, cc) for ref in (q_ref, f_ref, i_ref, g_ref)],
                            _hgrn_cols(l0_ref), _hgrn_cols(l1_ref), _hgrn_cols(on_ref))
        for h in range(HA_HEADS):
            for j in range(nj):
                o_ref[j * cc:(j + 1) * cc, h * HA_DIM:(h + 1) * HA_DIM] = o[h][j]
            st_ref[h] = st1[h]

    col = lambda j: pl.BlockSpec((c, W_A), lambda i, j=j: (i, j))
    par = pl.BlockSpec((1, W_A), lambda i: (0, 0))
    return pl.pallas_call(
        body, name="hgrn_fwd", grid=(n,), in_specs=[col(0), col(1), col(2), col(3), par, par, par],
        out_specs=[pl.BlockSpec((c, W_A), lambda i: (i, 0)),
                   pl.BlockSpec((1, HA_HEADS, HA_DIM, HA_DIM), lambda i: (i, 0, 0, 0))],
        out_shape=[SDS((t, W_A), F32), SDS((n, HA_HEADS, HA_DIM, HA_DIM), F32)],
        scratch_shapes=[pltpu.VMEM((HA_HEADS, HA_DIM, HA_DIM), F32)],
        compiler_params=_params(("arbitrary",)))(p_h, p_h, p_h, p_h, l0, l1, onorm)


def _hgrn_bwd(p_h, l0, l1, onorm, hs, do, do_col, comm=None):
    t = p_h.shape[0]
    cc, nj = HGRN_CHUNK, HGRN_GROUP
    c = cc * nj
    n = t // c

    def body(q_ref, f_ref, i_ref, g_ref, l0_ref, l1_ref, on_ref, hs_ref, do_ref,
             dp_ref, dl0_ref, dl1_ref, don_ref, dst_ref):
        @pl.when(pl.program_id(0) == 0)
        def _():
            dst_ref[...] = jnp.zeros_like(dst_ref)
            dl0_ref[...] = jnp.zeros_like(dl0_ref)
            dl1_ref[...] = jnp.zeros_like(dl1_ref)
            don_ref[...] = jnp.zeros_like(don_ref)

        args = ([hs_ref[0, h] for h in range(HA_HEADS)],
                *[_hgrn_blocks(ref, nj, cc) for ref in (q_ref, f_ref, i_ref, g_ref)],
                _hgrn_cols(l0_ref), _hgrn_cols(l1_ref), _hgrn_cols(on_ref))
        _, vjp = jax.vjp(_hgrn_step, *args)
        dst0, dq, df, di, dg, dl0, dl1, don = vjp((_hgrn_blocks(do_ref, nj, cc),
                                                   [dst_ref[h] for h in range(HA_HEADS)]))
        for h in range(HA_HEADS):
            sl = slice(h * HA_DIM, (h + 1) * HA_DIM)
            for k, dv in enumerate((dq, df, di, dg)):
                for j in range(nj):
                    dp_ref[j * cc:(j + 1) * cc, k * W_A + h * HA_DIM:k * W_A + (h + 1) * HA_DIM] = dv[h][j]
            dl0_ref[:, sl] += dl0[h]
            dl1_ref[:, sl] += dl1[h]
            don_ref[:, sl] += don[h]
            dst_ref[h] = dst0[h]

    col = lambda j: pl.BlockSpec((c, W_A), lambda i, j=j: (n - 1 - i, j))
    par = pl.BlockSpec((1, W_A), lambda i: (0, 0))
    return _hosting_call(
        body, comm, name="hgrn_bwd", grid=(n,),
        in_specs=[col(0), col(1), col(2), col(3), par, par, par,
                  pl.BlockSpec((1, HA_HEADS, HA_DIM, HA_DIM), lambda i: (n - 1 - i, 0, 0, 0)),
                  pl.BlockSpec((c, W_A), lambda i: (n - 1 - i, do_col))],
        out_specs=[pl.BlockSpec((c, N_HGRN_COLS), lambda i: (n - 1 - i, 0)), par, par, par],
        out_shape=[SDS((t, N_HGRN_COLS), F32), SDS((1, W_A), F32), SDS((1, W_A), F32), SDS((1, W_A), F32)],
        scratch_shapes=[pltpu.VMEM((HA_HEADS, HA_DIM, HA_DIM), F32)],
        args=(p_h, p_h, p_h, p_h, l0, l1, onorm, hs, do))


HB_PAIRS = HB_HEADS // 2
PAIR_W = 2 * HB_DIM


def _head_lane_masks():
    lane = lax.broadcasted_iota(jnp.int32, (1, PAIR_W), 1)
    return (lane < HB_DIM).astype(F32), (lane >= HB_DIM).astype(F32)


@jax.custom_vjp
def _stack_heads(x):
    m0, m1 = _head_lane_masks()
    return jnp.concatenate([x * m0, x * m1], axis=0)


def _stack_heads_bwd(_, g):
    m0, m1 = _head_lane_masks()
    c = g.shape[0] // 2
    return (g[:c] * m0 + g[c:] * m1,)


_stack_heads.defvjp(lambda x: (_stack_heads(x), None), _stack_heads_bwd)


@jax.custom_vjp
def _unstack_heads(ys):
    c = ys.shape[0] // 2
    return ys[:c] + ys[c:]


_unstack_heads.defvjp(lambda ys: (_unstack_heads(ys), None), lambda _, g: (_stack_heads(g),))


def _same_head_block(c):
    ii = lax.broadcasted_iota(jnp.int32, (2 * c, 2 * c), 0)
    jj = lax.broadcasted_iota(jnp.int32, (2 * c, 2 * c), 1)
    same = (ii < c) == (jj < c)
    return same & (jj <= ii), same & (jj < ii), (ii == jj).astype(F32)


@jax.custom_vjp
def _rows_join(top, bottom):
    return jnp.concatenate([top, bottom], axis=0)


def _rows_join_bwd(n_top, g):
    return g[:n_top], g[n_top:]


_rows_join.defvjp(lambda top, bottom: (_rows_join(top, bottom), top.shape[0]), _rows_join_bwd)


def _rows_split_impl(x, n_top):
    return x[:n_top], x[n_top:]


_rows_split = jax.custom_vjp(_rows_split_impl, nondiff_argnums=(1,))
_rows_split.defvjp(lambda x, n_top: (_rows_split_impl(x, n_top), None),
                   lambda n_top, _, g: (jnp.concatenate([g[0], g[1]], axis=0),))


def _rwkv_step(s0, r, lw, k, v, a, b):
    npair, nj = len(r), len(r[0])
    c = r[0][0].shape[0]
    combos = [(j, p) for j in range(nj) for p in range(npair)]
    every = lambda fn: {q: fn(q) for q in combos}
    at_ = lambda d: (lambda q: d[q[1]][q[0]])
    r_, lw_, k_, v_, a_, b_ = (at_(z) for z in (r, lw, k, v, a, b))
    incl, strict, eye = _same_head_block(c)

    gam = every(lambda q: _cumsum_rows(lw_(q)))
    gtot = every(lambda q: jnp.sum(lw_(q), axis=0, keepdims=True))
    eneg = every(lambda q: jnp.exp(-gam[q]))
    edec = every(lambda q: jnp.exp(gtot[q] - gam[q]))
    at = every(lambda q: _stack_heads(a_(q) * jnp.exp(gam[q] - lw_(q))))
    rt = every(lambda q: _stack_heads(r_(q) * jnp.exp(gam[q])))
    bt = every(lambda q: _stack_heads(b_(q) * eneg[q]))
    kt = every(lambda q: _stack_heads(k_(q) * eneg[q]))
    bdec = every(lambda q: _stack_heads(b_(q) * edec[q]))
    kdec = every(lambda q: _stack_heads(k_(q) * edec[q]))
    vs = every(lambda q: _stack_heads(v_(q)))
    a_ab = every(lambda q: jnp.where(strict, _nt(at[q], bt[q]), 0.0))
    a_ak = every(lambda q: jnp.where(strict, _nt(at[q], kt[q]), 0.0))
    a_rb = every(lambda q: jnp.where(incl, _nt(rt[q], bt[q]), 0.0))
    a_rk = every(lambda q: jnp.where(incl, _nt(rt[q], kt[q]), 0.0))
    tinv = every(lambda q: eye + a_ab[q])
    pw = a_ab
    span = 2
    while span < c:
        pw = every(lambda q, pw=pw: _nn_x3(pw[q], pw[q]))
        tinv = every(lambda q, pw=pw, tinv=tinv: tinv[q] + _nn_x3(pw[q], tinv[q]))
        span *= 2
    akv = every(lambda q: _nn(a_ak[q], vs[q]))
    w1 = every(lambda q: _nn(tinv[q], at[q]))
    u0 = every(lambda q: _nn(tinv[q], akv[q]))
    wr = every(lambda q: _rows_join(w1[q], rt[q]))
    bk = every(lambda q: _rows_join(bdec[q], kdec[q]))
    yv = every(lambda q: _nn(a_rk[q], vs[q]))
    gdec = every(lambda q: jnp.exp(gtot[q]))

    s = list(s0)
    y = [[None] * nj for _ in range(npair)]
    for j in range(nj):
        both = {p: _rows_split(_nt(wr[(j, p)], s[p]), 2 * c) for p in range(npair)}
        u = {p: both[p][0] + u0[(j, p)] for p in range(npair)}
        for p in range(npair):
            y[p][j] = _unstack_heads(both[p][1] + _nn(a_rb[(j, p)], u[p]) + yv[(j, p)])
        s = [s[p] * gdec[(j, p)] + _tn(_rows_join(u[p], vs[(j, p)]), bk[(j, p)]) for p in range(npair)]
    return y, s


def _rwkv_blocks(ref, nj, c):
    return [[ref[j * c:(j + 1) * c, p * PAIR_W:(p + 1) * PAIR_W] for j in range(nj)] for p in range(HB_PAIRS)]


def _rwkv_fwd(seqs, comm=None):
    t = seqs[0].shape[0]
    c, nj = RWKV_CHUNK, RWKV_GROUP
    n = t // (c * nj)

    def body(r_ref, lw_ref, k_ref, v_ref, a_ref, b_ref, y_ref, hs_ref, st_ref):
        @pl.when(pl.program_id(0) == 0)
        def _():
            st_ref[...] = jnp.zeros_like(st_ref)

        hs_ref[0] = st_ref[...]
        s0 = [st_ref[p] for p in range(HB_PAIRS)]
        y, s1 = _rwkv_step(s0, *[_rwkv_blocks(ref, nj, c) for ref in (r_ref, lw_ref, k_ref, v_ref, a_ref, b_ref)])
        for p in range(HB_PAIRS):
            for j in range(nj):
                y_ref[j * c:(j + 1) * c, p * PAIR_W:(p + 1) * PAIR_W] = y[p][j]
            st_ref[p] = s1[p]

    seq = pl.BlockSpec((c * nj, W_B), lambda i: (i, 0))
    return _hosting_call(
        body, comm, name="rwkv_fwd", grid=(n,), in_specs=[seq] * 6,
        out_specs=[seq, pl.BlockSpec((1, HB_PAIRS, PAIR_W, PAIR_W), lambda i: (i, 0, 0, 0))],
        out_shape=[SDS((t, W_B), F32), SDS((n, HB_PAIRS, PAIR_W, PAIR_W), F32)],
        scratch_shapes=[pltpu.VMEM((HB_PAIRS, PAIR_W, PAIR_W), F32)], args=tuple(seqs))


def _rwkv_bwd(seqs, hs, dy, comm=None):
    t = seqs[0].shape[0]
    c, nj = RWKV_CHUNK, RWKV_GROUP
    n = t // (c * nj)

    def body(r_ref, lw_ref, k_ref, v_ref, a_ref, b_ref, hs_ref, dy_ref,
             dr_ref, dlw_ref, dk_ref, dv_ref, da_ref, db_ref, dst_ref):
        @pl.when(pl.program_id(0) == 0)
        def _():
            dst_ref[...] = jnp.zeros_like(dst_ref)

        s0 = [hs_ref[0, p] for p in range(HB_PAIRS)]
        seq_vals = [_rwkv_blocks(ref, nj, c) for ref in (r_ref, lw_ref, k_ref, v_ref, a_ref, b_ref)]
        _, vjp = jax.vjp(_rwkv_step, s0, *seq_vals)
        grads = vjp((_rwkv_blocks(dy_ref, nj, c), [dst_ref[p] for p in range(HB_PAIRS)]))
        for ref, gr in zip((dr_ref, dlw_ref, dk_ref, dv_ref, da_ref, db_ref), grads[1:]):
            for p in range(HB_PAIRS):
                for j in range(nj):
                    ref[j * c:(j + 1) * c, p * PAIR_W:(p + 1) * PAIR_W] = gr[p][j]
        m0, m1 = _head_lane_masks()
        rows0 = (lax.broadcasted_iota(jnp.int32, (PAIR_W, 1), 0) < HB_DIM).astype(F32)
        blocks = rows0 * m0 + (1.0 - rows0) * m1
        for p in range(HB_PAIRS):
            dst_ref[p] = grads[0][p] * blocks

    seq = pl.BlockSpec((c * nj, W_B), lambda i: (n - 1 - i, 0))
    return _hosting_call(
        body, comm, name="rwkv_bwd", grid=(n,),
        in_specs=[seq] * 6 + [pl.BlockSpec((1, HB_PAIRS, PAIR_W, PAIR_W), lambda i: (n - 1 - i, 0, 0, 0)), seq],
        out_specs=[seq] * 6, out_shape=[SDS((t, W_B), F32)] * 6,
        scratch_shapes=[pltpu.VMEM((HB_PAIRS, PAIR_W, PAIR_W), F32)], args=(*seqs, hs, dy))


def _final_loss(x3, fnorm, target, *, tm):
    t, d = x3.shape

    def body(x_ref, g_ref, t_ref, dx_ref, dg_ref, loss_ref):
        @pl.when(pl.program_id(0) == 0)
        def _():
            dg_ref[...] = jnp.zeros_like(dg_ref)
            loss_ref[...] = jnp.zeros_like(loss_ref)

        x, g = x_ref[...], g_ref[...]
        rinv = lax.rsqrt(jnp.mean(x * x, axis=-1, keepdims=True) + NORM_EPS)
        xh = x * rinv
        diff = xh * g - t_ref[...]
        loss_ref[...] += 0.5 * jnp.sum(jnp.mean(diff * diff, axis=-1, keepdims=True))
        dy = diff * (1.0 / d)
        dg_ref[...] += jnp.sum(dy * xh, axis=0, keepdims=True)
        dxh = dy * g
        dx_ref[...] = rinv * (dxh - xh * jnp.mean(dxh * xh, axis=-1, keepdims=True))

    row = pl.BlockSpec((tm, d), lambda i: (i, 0))
    return pl.pallas_call(
        body, name="final_loss", grid=(t // tm,), in_specs=[row, pl.BlockSpec((1, d), lambda i: (0, 0)), row],
        out_specs=[row, pl.BlockSpec((1, d), lambda i: (0, 0)), pl.BlockSpec((8, 128), lambda i: (0, 0))],
        out_shape=[SDS((t, d), F32), SDS((1, d), F32), SDS((8, 128), F32)],
        compiler_params=_params(("arbitrary",)))(x3, fnorm, target)


def _gate_up_act(h, wgt, wut, *, tm, tn, name, comm=None):
    t, d = h.shape
    tm = min(tm, t)

    def body(h_ref, g_ref, u_ref, a_out, u_out, act_out):
        hv = h_ref[...]
        a = _dg(hv, g_ref[...], 1, 1, False)
        u = _dg(hv, u_ref[...], 1, 1, False)
        a_out[...] = a.astype(a_out.dtype)
        u_out[...] = u.astype(u_out.dtype)
        act_out[...] = (_silu(a) * u).astype(act_out.dtype)

    wspec = pl.BlockSpec((tn, d), lambda i, j: (j, 0))
    ospec = pl.BlockSpec((tm, tn), lambda i, j: (i, j))
    return _hosting_call(
        body, comm, name=name, grid=(t // tm, D_FF // tn),
        in_specs=[pl.BlockSpec((tm, d), lambda i, j: (i, 0)), wspec, wspec], out_specs=[ospec, ospec, ospec],
        out_shape=[SDS((t, D_FF), BF16), SDS((t, D_FF), BF16), SDS((t, D_FF), BF16)], scratch_shapes=[],
        args=(h, wgt, wut))


def _dact_swiglu(dout, wd, a, u, *, tm, tn, name, comm=None):
    t, d = dout.shape
    tm = min(tm, t)

    def body(d_ref, w_ref, a_ref, u_ref, da_out, du_out):
        dact = 0.5 * _dg(d_ref[...], w_ref[...], 1, 1, False)
        av, uv = a_ref[...].astype(F32), u_ref[...].astype(F32)
        s = _sigmoid(av)
        da_out[...] = (dact * uv * (s * (1.0 + av * (1.0 - s)))).astype(da_out.dtype)
        du_out[...] = (dact * (av * s)).astype(du_out.dtype)

    tile = pl.BlockSpec((tm, tn), lambda i, j: (i, j))
    return _hosting_call(
        body, comm, name=name, grid=(t // tm, D_FF // tn),
        in_specs=[pl.BlockSpec((tm, d), lambda i, j: (i, 0)), pl.BlockSpec((tn, d), lambda i, j: (j, 0)), tile, tile],
        out_specs=[tile, tile], out_shape=[SDS((t, D_FF), BF16), SDS((t, D_FF), BF16)], scratch_shapes=[],
        args=(dout, wd, a, u))


class _Plan:
    def __init__(self):
        self.entries, self.counts = collections.defaultdict(list), {}

    def carry(self, host, comm_of, after):
        self.entries[host].append((comm_of, after))

    def comm(self, host, g):
        comms = [comm_of(g) for comm_of, _ in self.entries.get(host, [])]
        self.counts[host] = [len(c.arrays) for c in comms]
        return functools.reduce(_join_comms, comms) if comms else None

    def done(self, host, results, w):
        start = 0
        for (_, after), n in zip(self.entries.get(host, []), self.counts.get(host, [])):
            after(results[start:start + n], w)
            start += n


def _ffn_fwd(x, w, tag, plan, g):
    comm = plan.comm(f"{tag}_rms", g)
    res = _rowwise(_rms_f, [x], [w[f"{tag}_norm"]], [[0]], [BF16], tm=512, name=f"{tag}_rms", comm=comm)
    (h,), carried = res if comm is not None else (res, [])
    plan.done(f"{tag}_rms", carried, w)
    (a, u, act), carried = _gate_up_act(h, w[f"{tag}_wgt"], w[f"{tag}_wut"], tm=2048, tn=256, name=f"{tag}_gate_up",
                                        comm=plan.comm(f"{tag}_gate_up", g))
    plan.done(f"{tag}_gate_up", carried, w)
    comm = plan.comm(f"{tag}_down", g)
    out = _mm(act, w[f"{tag}_wd"], tm=1024, tn=D_MODEL, tk=D_FF, name=f"{tag}_down", res=x, scale=0.5, comm=comm)
    if comm is not None:
        out, carried = out
        plan.done(f"{tag}_down", carried, w)
    return out, (h, a, u, act)


def _ffn_bwd(dout, x, w, saved, tag, plan, g):
    h, a, u, act = saved

    def carrying(fn, host, *args, **kwargs):
        comm = plan.comm(host, g)
        res = fn(*args, name=host, comm=comm, **kwargs)
        out, carried = res if comm is not None else (res, [])
        plan.done(host, carried, w)
        return out

    (da, du), carried = _dact_swiglu(dout, w[f"{tag}_wd"], a, u, tm=2048, tn=256, name=f"{tag}_dact",
                                     comm=plan.comm(f"{tag}_dact", g))
    plan.done(f"{tag}_dact", carried, w)
    g[f"{tag}_wd"] = _mm(act, dout, ta=True, tm=D_FF // 2, tn=D_MODEL, tk=1024, name=f"{tag}_dwd", scale=0.5)
    g[f"{tag}_wgt"] = carrying(_mm, f"{tag}_dwg", da, h, ta=True, tm=D_FF // 2, tn=D_MODEL, tk=1024)
    g[f"{tag}_wut"] = carrying(_mm, f"{tag}_dwu", du, h, ta=True, tm=D_FF // 2, tn=D_MODEL, tk=1024)
    dh = carrying(_mm, f"{tag}_dh_g", da, w[f"{tag}_wgt"], tm=1024, tn=D_MODEL, tk=D_FF)
    dh = carrying(_mm, f"{tag}_dh_u", du, w[f"{tag}_wut"], tm=1024, tn=D_MODEL, tk=D_FF, res=dh)
    dx, g[f"{tag}_norm"] = _rowwise_bwd(_rms_f, [x], [w[f"{tag}_norm"]], [dh], x_grad=[True], p_grad=[True],
                                        dx_groups=[[0]], dx_dtypes=[F32], tm=512, name=f"{tag}_drms",
                                        extra={0: dout})
    return dx


def _local_step(x, target, w, plan=None):
    plan = plan or _Plan()
    ones_bd = jnp.kron(jnp.eye(HB_HEADS, dtype=F32), jnp.ones((HB_DIM, HB_DIM), F32))
    g = {}
    x1, ffn1_saved = _ffn_fwd(x, w, "ffn1", plan, g)
    hm, = _rowwise(_rms_f, [x1], [w["mix_norm"]], [[0]], [BF16], tm=512, name="mix_rms")
    p_h = _mm(hm, w["w_in_h"], tm=2048, tn=256, tk=D_MODEL, name="inproj_h")
    p_r = _mm(hm, w["w_in_r"], tm=2048, tn=256, tk=D_MODEL, name="inproj_r")
    o_a, hgrn_states = _hgrn_fwd(p_h, w["lb0"], w["lb1"], w["hgrn_out_norm"])

    mu = w["mu_pad"]
    prep_xs = [(p_r, W_B, 0), (p_r, W_B, 1), (p_r, W_B, 2), (p_r, LORA_PAD, 6),
               ("prev", p_r, W_B, 0), ("prev", p_r, W_B, 1), ("prev", p_r, W_B, 2), ("prev", p_r, LORA_PAD, 6)]
    prep_ps = [(mu, W_B, 0), (mu, W_B, 1), (mu, W_B, 2), (mu, LORA_PAD, 6), w["rwkv_w0"], w["w2_pad"], w["rwkv_a0"],
               w["a2_pad"], w["g2_pad"], w["rwkv_k_k"], w["rwkv_k_a"], ones_bd]
    prep_f = _rwkv_prep_f
    r, lw, k2, v, a_vec, b_vec, gate = _rowwise(prep_f, prep_xs, prep_ps, [[0], [1], [2], [3], [4], [5], [6]],
                                                [F32] * 7, tm=256, name="rwkv_prep")
    seqs = [r, lw, k2, v, a_vec, b_vec]
    (y, rwkv_states), carried = _rwkv_fwd(seqs, comm=plan.comm("rwkv_fwd", g))
    plan.done("rwkv_fwd", carried, w)
    post_f = _rwkv_post_f
    post_xs = [y, r, k2, v, gate]
    post_ps = [w["rwkv_r_k"], w["rwkv_gn_w"], w["rwkv_gn_b"], ones_bd]
    o_b, = _rowwise(post_f, post_xs, post_ps, [[0]], [F32], tm=256, name="rwkv_post")
    x2 = _mm(o_a, w["w_out_a"], tm=2048, tn=256, tk=W_A, name="outproj_a", res=x1)
    x2 = _mm(o_b, w["w_out_b"], tm=2048, tn=256, tk=W_B, name="outproj_b", res=x2)
    x3, ffn2_saved = _ffn_fwd(x2, w, "ffn2", plan, g)
    dx3, g["final_norm"], loss = _final_loss(x3, w["final_norm"], target, tm=256)

    dx2 = _ffn_bwd(dx3, x2, w, ffn2_saved, "ffn2", plan, g)
    do_a = _mm(dx2, w["w_out_a"], tb=True, tm=2048, tn=256, tk=D_MODEL, name="outproj_do_a")
    do_b = _mm(dx2, w["w_out_b"], tb=True, tm=2048, tn=256, tk=D_MODEL, name="outproj_do_b")
    g["w_out_a"] = _mm(o_a, dx2, ta=True, tm=W_A, tn=D_MODEL, tk=1024, name="outproj_dw_a")
    g["w_out_b"] = _mm(o_b, dx2, ta=True, tm=W_B, tn=D_MODEL, tk=1024, name="outproj_dw_b")

    (dp_h, g["lb0"], g["lb1"], g["hgrn_out_norm"]), carried = _hgrn_bwd(
        p_h, w["lb0"], w["lb1"], w["hgrn_out_norm"], hgrn_states, do_a, 0, comm=plan.comm("hgrn_bwd", g))
    plan.done("hgrn_bwd", carried, w)
    post_out = _rowwise_bwd(post_f, post_xs, post_ps, [do_b], x_grad=[True] * 5, p_grad=[True] * 3 + [False],
                            dx_groups=[[0], [1], [2], [3], [4]], dx_dtypes=[F32] * 5, tm=256, name="rwkv_post_bwd")
    dy, dr1, dk1, dv1, dgate, g["rwkv_r_k"], g["rwkv_gn_w"], g["rwkv_gn_b"] = post_out
    (dr2, dlw, dk2, dv2, da_vec, db_vec), carried = _rwkv_bwd(seqs, rwkv_states, dy, comm=plan.comm("rwkv_bwd", g))
    plan.done("rwkv_bwd", carried, w)

    def prep2_f(*vals):
        r_, lw_, k2_, v_, a_, b_, g_ = prep_f(*vals)
        return r_, lw_, k2_, v_, a_, b_, g_, r_, k2_, v_

    prep_out = _rowwise_bwd(prep2_f, prep_xs, prep_ps, [dr2, dlw, dk2, dv2, da_vec, db_vec, dgate, dr1, dk1, dv1],
                            x_grad=[True] * 8, p_grad=[True] * 11 + [False], dx_groups=[[0, 1, 2, 3], [4, 5, 6, 7]],
                            dx_dtypes=[F32, F32], tm=256, name="rwkv_prep_bwd")
    dpr_main, dpr_prev = prep_out[0], prep_out[1]
    (dmu_r, dmu_k, dmu_v, dmu_lo, g["rwkv_w0"], g["w2_pad"], g["rwkv_a0"], g["a2_pad"], g["g2_pad"],
     g["rwkv_k_k"], g["rwkv_k_a"]) = prep_out[2:]
    g["mu_pad"] = jnp.concatenate([dmu_r, dmu_k, dmu_v, dmu_lo], axis=1)
    dp_r, = _rowwise(lambda u_, s_: (u_ + s_,), [dpr_main, ("next", dpr_prev, N_RWKV_PAD, 0)], [], [[0]], [F32],
                     tm=512, name="rwkv_dp_sum")
    dhm = _mm(dp_h, w["w_in_h"], tb=True, tm=1024, tn=D_MODEL, tk=N_HGRN_COLS, name="inproj_dh_h")
    dhm = _mm(dp_r, w["w_in_r"], tb=True, tm=1024, tn=D_MODEL, tk=N_RWKV_PAD, name="inproj_dh_r", res=dhm)
    g["w_in_h"] = _mm(hm, dp_h, ta=True, tm=D_MODEL, tn=D_MODEL, tk=1024, name="inproj_dw_h")
    g["w_in_r"] = _mm(hm, dp_r, ta=True, tm=D_MODEL, tn=N_RWKV_PAD // 2, tk=1024, name="inproj_dw_r")
    mix_comm = plan.comm("mix_drms", g)
    mix_out = _rowwise_bwd(_rms_f, [x1], [w["mix_norm"]], [dhm], x_grad=[True], p_grad=[True], dx_groups=[[0]],
                           dx_dtypes=[F32], tm=512, name="mix_drms", extra={0: dx2}, comm=mix_comm)
    (dx1, g["mix_norm"]), carried = mix_out if mix_comm is not None else (mix_out, [])
    plan.done("mix_drms", carried, w)
    dx0 = _ffn_bwd(dx1, x, w, ffn1_saved, "ffn1", plan, g)
    return loss, dx0, g


HBM_SPEC = pl.BlockSpec(memory_space=pl.ANY)

Comm = collections.namedtuple("Comm", "arrays out_shapes aliased sem_shapes start finish")


def _join_comms(first, second):
    assert first.aliased == second.aliased
    n, s = len(first.arrays), len(first.sem_shapes)

    def start(ins, outs, sems):
        first.start(ins[:n], outs[:n], sems[:s])
        second.start(ins[n:], outs[n:], sems[s:])

    def finish(ins, outs, sems):
        first.finish(ins[:n], outs[:n], sems[:s])
        second.finish(ins[n:], outs[n:], sems[s:])

    return Comm(list(first.arrays) + list(second.arrays), list(first.out_shapes) + list(second.out_shapes),
                first.aliased, list(first.sem_shapes) + list(second.sem_shapes), start, finish)


def _run_comm(comm, name):
    n = len(comm.arrays)

    def body(*refs):
        ins, outs, sems = refs[:n], refs[n:2 * n], refs[2 * n:]
        comm.start(ins, outs, sems)
        comm.finish(ins, outs, sems)

    return pl.pallas_call(
        body, name=name, in_specs=[HBM_SPEC] * n, out_specs=[HBM_SPEC] * n, out_shape=list(comm.out_shapes),
        input_output_aliases={t: t for t in range(n)} if comm.aliased else {},
        scratch_shapes=list(comm.sem_shapes))(*comm.arrays)


def _hosting_call(body, comm, *, name, grid, in_specs, out_specs, out_shape, scratch_shapes, args):
    sem = ("arbitrary",) * len(grid)
    if comm is None:
        res = pl.pallas_call(body, name=name, grid=grid, in_specs=in_specs, out_specs=out_specs, out_shape=out_shape,
                             scratch_shapes=scratch_shapes, compiler_params=_params(sem))(*args)
        return list(res), []
    ni, no, ns, nc = len(in_specs), len(out_specs), len(scratch_shapes), len(comm.arrays)

    def wrapped(*refs):
        ins, cins = refs[:ni], refs[ni:ni + nc]
        outs, couts = refs[ni + nc:ni + nc + no], refs[ni + nc + no:ni + 2 * nc + no]
        scr, sems = refs[ni + 2 * nc + no:ni + 2 * nc + no + ns], refs[ni + 2 * nc + no + ns:]
        first = functools.reduce(jnp.logical_and, [pl.program_id(k) == 0 for k in range(len(grid))])
        last = functools.reduce(jnp.logical_and, [pl.program_id(k) == grid[k] - 1 for k in range(len(grid))])

        @pl.when(first)
        def _():
            comm.start(cins, couts, sems)

        body(*ins, *outs, *scr)

        @pl.when(last)
        def _():
            comm.finish(cins, couts, sems)

    res = pl.pallas_call(
        wrapped, name=name, grid=grid, in_specs=list(in_specs) + [HBM_SPEC] * nc,
        out_specs=list(out_specs) + [HBM_SPEC] * nc, out_shape=list(out_shape) + list(comm.out_shapes),
        scratch_shapes=list(scratch_shapes) + list(comm.sem_shapes),
        input_output_aliases={ni + t: no + t for t in range(nc)} if comm.aliased else {},
        compiler_params=_params(sem))(*args, *comm.arrays)
    return list(res[:no]), list(res[no:])


def _chips(x, y):
    return [(1 - x, y), (x, 1 - y), (1 - x, 1 - y)]


def _gather_comm(bufs):
    n = len(bufs)

    def copies(outs, sems):
        ici_send, ici_recv, d2d_send, d2d_recv = sems
        x, y, c = lax.axis_index("x"), lax.axis_index("y"), lax.axis_index("c")

        def half(t, slot, hc):
            hr = bufs[t].shape[1] // 2
            return outs[t].at[slot, pl.ds(pl.multiple_of(hc * hr, 16), hr), :]

        def ici(t, j, slot, px, py):
            return pltpu.make_async_remote_copy(src_ref=half(t, slot, c), dst_ref=half(t, slot, c),
                                                send_sem=ici_send.at[3 * t + j], recv_sem=ici_recv.at[3 * t + j],
                                                device_id=(px, py, c), device_id_type=MESH)

        def d2d(t, j, slot, hc):
            return pltpu.make_async_remote_copy(src_ref=half(t, slot, hc), dst_ref=half(t, slot, hc),
                                                send_sem=d2d_send.at[3 * t + j], recv_sem=d2d_recv.at[3 * t + j],
                                                device_id=(x, y, 1 - c), device_id_type=MESH)

        peers = [(t, j, px, py) for t in range(n) for j, (px, py) in enumerate(_chips(x, y))]
        return ici, d2d, peers, 2 * x + y, c

    def start(ins, outs, sems):
        ici, _, peers, me, _ = copies(outs, sems)
        for t, j, px, py in peers:
            ici(t, j, me, px, py).start()

    def finish(ins, outs, sems):
        ici, d2d, peers, me, c = copies(outs, sems)
        for t, j, px, py in peers:
            ici(t, j, 2 * px + py, px, py).wait_recv()
            d2d(t, j, 2 * px + py, c).start()
        for t, j, px, py in peers:
            d2d(t, j, 2 * px + py, 1 - c).wait_recv()
        for t, j, px, py in peers:
            ici(t, j, me, px, py).wait_send()
            d2d(t, j, 2 * px + py, c).wait_send()

    return Comm(list(bufs), [SDS(b.shape, b.dtype) for b in bufs], True, [pltpu.SemaphoreType.DMA((3 * n,))] * 4,
                start, finish)


def _sibling_exchange_comm(gs):
    n = len(gs)

    def copies(ins, outs, sems):
        x, y, c = lax.axis_index("x"), lax.axis_index("y"), lax.axis_index("c")
        cps = []
        for t in range(n):
            hr = gs[t].shape[1] // 2
            src = ins[t].at[:, pl.ds(pl.multiple_of((1 - c) * hr, SUBLANES), hr), :]
            cps.append(pltpu.make_async_remote_copy(src_ref=src, dst_ref=outs[t], send_sem=sems[0].at[t],
                                                    recv_sem=sems[1].at[t], device_id=(x, y, 1 - c),
                                                    device_id_type=MESH))
        return cps

    def start(ins, outs, sems):
        for cp in copies(ins, outs, sems):
            cp.start()

    def finish(ins, outs, sems):
        for cp in copies(ins, outs, sems):
            cp.wait()

    return Comm(list(gs), [SDS((N_CHIPS, g.shape[1] // 2, g.shape[2]), g.dtype) for g in gs], False,
                [pltpu.SemaphoreType.DMA((n,))] * 2, start, finish)


def _chip_exchange_comm(ss):
    n = len(ss)

    def copies(ins, outs, sems):
        x, y, c = lax.axis_index("x"), lax.axis_index("y"), lax.axis_index("c")
        me = 2 * x + y

        def copy(t, j, px, py, src_slot, dst_slot):
            return pltpu.make_async_remote_copy(src_ref=ins[t].at[src_slot], dst_ref=outs[t].at[dst_slot],
                                                send_sem=sems[0].at[3 * t + j], recv_sem=sems[1].at[3 * t + j],
                                                device_id=(px, py, c), device_id_type=MESH)

        peers = [(t, j, px, py) for t in range(n) for j, (px, py) in enumerate(_chips(x, y))]
        return copy, peers, me

    def start(ins, outs, sems):
        copy, peers, me = copies(ins, outs, sems)
        for t, j, px, py in peers:
            copy(t, j, px, py, 2 * px + py, me).start()

    def finish(ins, outs, sems):
        copy, peers, me = copies(ins, outs, sems)
        for t, j, px, py in peers:
            copy(t, j, px, py, me, 2 * px + py).wait_recv()
        for t, j, px, py in peers:
            copy(t, j, px, py, 2 * px + py, me).wait_send()

    return Comm(list(ss), [SDS(s.shape, s.dtype) for s in ss], False, [pltpu.SemaphoreType.DMA((3 * n,))] * 2,
                start, finish)


def _sibling_swap_comm(fs):
    n = len(fs)

    def copies(ins, outs, sems):
        x, y, c = lax.axis_index("x"), lax.axis_index("y"), lax.axis_index("c")
        return [pltpu.make_async_remote_copy(src_ref=ins[t], dst_ref=outs[t], send_sem=sems[0].at[t],
                                             recv_sem=sems[1].at[t], device_id=(x, y, 1 - c), device_id_type=MESH)
                for t in range(n)]

    def start(ins, outs, sems):
        for cp in copies(ins, outs, sems):
            cp.start()

    def finish(ins, outs, sems):
        for cp in copies(ins, outs, sems):
            cp.wait()

    return Comm(list(fs), [SDS(f.shape, f.dtype) for f in fs], False, [pltpu.SemaphoreType.DMA((n,))] * 2,
                start, finish)


def _row_tile(rows, cap=512):
    best = SUBLANES
    for tr in range(SUBLANES, min(rows, cap) + 1, SUBLANES):
        if rows % tr == 0:
            best = tr
    return best


def _add_halves(g4, r4, c_idx, name):
    _, hr, lanes = r4.shape
    tr = _row_tile(hr)
    nb = hr // tr

    def body(c_ref, a_ref, b_ref, o_ref):
        o_ref[...] = (a_ref[...] + b_ref[...]).astype(o_ref.dtype)

    grid_spec = pltpu.PrefetchScalarGridSpec(
        num_scalar_prefetch=1, grid=(N_CHIPS, nb),
        in_specs=[pl.BlockSpec((None, tr, lanes), lambda q, i, c_ref: (q, c_ref[0] * nb + i, 0)),
                  pl.BlockSpec((None, tr, lanes), lambda q, i, c_ref: (q, i, 0))],
        out_specs=pl.BlockSpec((None, tr, lanes), lambda q, i, c_ref: (q, i, 0)))
    return pl.pallas_call(body, name=name, grid_spec=grid_spec, out_shape=SDS(r4.shape, BF16),
                          compiler_params=_params(("parallel", "parallel")))(c_idx, g4, r4)


def _sum_chips(r4, s4, me_idx, name):
    _, rows, lanes = r4.shape
    tr = _row_tile(rows)

    def body(me_ref, a_ref, b_ref, c_ref, d_ref, own_ref, o_ref):
        own = own_ref[...].astype(F32)
        p = [jnp.where(me_ref[0] == q, own, ref[...].astype(F32)) for q, ref in enumerate((a_ref, b_ref, c_ref, d_ref))]
        o_ref[...] = ((p[0] + p[1]) + p[2]) + p[3]

    other = lambda q: (lambda i, me_ref: (jnp.where(me_ref[0] == q, (q + 1) % N_CHIPS, q), i, 0))
    grid_spec = pltpu.PrefetchScalarGridSpec(
        num_scalar_prefetch=1, grid=(rows // tr,),
        in_specs=[pl.BlockSpec((None, tr, lanes), other(q)) for q in range(N_CHIPS)]
        + [pl.BlockSpec((None, tr, lanes), lambda i, me_ref: (me_ref[0], i, 0))],
        out_specs=pl.BlockSpec((tr, lanes), lambda i, me_ref: (i, 0)))
    return pl.pallas_call(body, name=name, grid_spec=grid_spec, out_shape=SDS((rows, lanes), F32),
                          compiler_params=_params(("parallel",)))(me_idx, r4, r4, r4, r4, s4)


def _adamw(wf, g_own, g_other, mf, vf, c_idx, name):
    rows, lanes = wf.shape
    hr = rows // 2
    tr = _row_tile(hr)
    nb = hr // tr
    c1 = 1.0 / (1.0 - ADAM_B1 ** ADAM_STEP)
    c2 = 1.0 / (1.0 - ADAM_B2 ** ADAM_STEP)

    def body(c_ref, w_ref, go_ref, gx_ref, m_ref, v_ref, g_ref, d_ref, nm_ref, nv_ref):
        gv = jnp.where(pl.program_id(0) == c_ref[0], go_ref[...], gx_ref[...])
        m = ADAM_B1 * m_ref[...] + (1.0 - ADAM_B1) * gv
        v = ADAM_B2 * v_ref[...] + (1.0 - ADAM_B2) * (gv * gv)
        g_ref[...] = gv
        d_ref[...] = -ADAM_LR * ((m * c1) / (jnp.sqrt(v * c2) + ADAM_EPS) + ADAM_WD * w_ref[...])
        nm_ref[...] = m
        nv_ref[...] = v

    full = pl.BlockSpec((tr, lanes), lambda h, i, c_ref: (h * nb + i, 0))
    half = pl.BlockSpec((tr, lanes), lambda h, i, c_ref: (i, 0))
    grid_spec = pltpu.PrefetchScalarGridSpec(num_scalar_prefetch=1, grid=(2, nb),
                                             in_specs=[full, half, half, full, full], out_specs=[full] * 4)
    return pl.pallas_call(body, name=name, grid_spec=grid_spec, out_shape=[SDS((rows, lanes), F32)] * 4,
                          compiler_params=_params(("parallel", "parallel")))(c_idx, wf, g_own, g_other, mf, vf)


BIG = ("ffn1_w_gate", "ffn1_w_up", "ffn1_w_down", "ffn2_w_gate", "ffn2_w_up", "ffn2_w_down", "w_out", "w_in")
TRANSPOSED = ("ffn1_w_gate", "ffn1_w_up", "ffn2_w_gate", "ffn2_w_up")
PACKED = ("rwkv_w2", "rwkv_a2", "rwkv_g2")
SMALL_SHAPES = {"ffn1_norm": (1, D_MODEL), "mix_norm": (1, D_MODEL), "hgrn_lb_logits": (2, W_A),
                "hgrn_out_norm": (1, W_A), "rwkv_shift_mu": (1, N_RWKV_COLS), "rwkv_w0": (1, W_B),
                "rwkv_a0": (1, W_B), "rwkv_k_k": (1, W_B), "rwkv_k_a": (1, W_B),
                "rwkv_r_k": (1, HB_HEADS, HB_DIM), "rwkv_gn_w": (1, W_B), "rwkv_gn_b": (1, W_B),
                "ffn2_norm": (1, D_MODEL), "final_norm": (D_MODEL,)}
PACK_ELEMS = sum(_numel(_shard_shape(n)) for n in PACKED) + sum(_numel(SMALL_SHAPES[n]) for n in SMALL)
PACK_ROWS = -(-PACK_ELEMS // (32 * LANES)) * 32


def _to_rows(name, shard):
    return shard[0].T if name in TRANSPOSED else shard[0]


def _from_rows(name, rows):
    return (rows.T if name in TRANSPOSED else rows)[None]


def _pack(sharded, small):
    flat = jnp.concatenate([sharded[n].reshape(-1) for n in PACKED] + [small[n].reshape(-1) for n in SMALL])
    return jnp.pad(flat, (0, PACK_ROWS * LANES - flat.shape[0])).reshape(PACK_ROWS, LANES)


def _unpack(packed):
    flat, out, off = packed.reshape(-1), {}, 0
    for n in PACKED:
        shp = _shard_shape(n)
        out[n] = flat[off:off + _numel(shp)].reshape((1,) + shp)
        off += _numel(shp)
    for n in SMALL:
        shp = SMALL_SHAPES[n]
        out[n] = flat[off:off + _numel(shp)].reshape(shp)
        off += _numel(shp)
    return out


def _quarter(full, name, q):
    shape, ax = SHARDED_SHAPES[name]
    w = shape[ax] // N_CHIPS
    return lax.slice_in_dim(full, q * w, (q + 1) * w, axis=ax)


def kernel(x, ffn1_norm, ffn1_w_gate, ffn1_w_up, ffn1_w_down, mix_norm, w_in, hgrn_lb_logits, hgrn_out_norm, rwkv_shift_mu, rwkv_w0, rwkv_w2, rwkv_a0, rwkv_a2, rwkv_g2, rwkv_k_k, rwkv_k_a, rwkv_r_k, rwkv_gn_w, rwkv_gn_b, w_out, ffn2_norm, ffn2_w_gate, ffn2_w_up, ffn2_w_down, final_norm, loss_target, m_ffn1_norm, m_ffn1_w_gate, m_ffn1_w_up, m_ffn1_w_down, m_mix_norm, m_w_in, m_hgrn_lb_logits, m_hgrn_out_norm, m_rwkv_shift_mu, m_rwkv_w0, m_rwkv_w2, m_rwkv_a0, m_rwkv_a2, m_rwkv_g2, m_rwkv_k_k, m_rwkv_k_a, m_rwkv_r_k, m_rwkv_gn_w, m_rwkv_gn_b, m_w_out, m_ffn2_norm, m_ffn2_w_gate, m_ffn2_w_up, m_ffn2_w_down, m_final_norm, v_ffn1_norm, v_ffn1_w_gate, v_ffn1_w_up, v_ffn1_w_down, v_mix_norm, v_w_in, v_hgrn_lb_logits, v_hgrn_out_norm, v_rwkv_shift_mu, v_rwkv_w0, v_rwkv_w2, v_rwkv_a0, v_rwkv_a2, v_rwkv_g2, v_rwkv_k_k, v_rwkv_k_a, v_rwkv_r_k, v_rwkv_gn_w, v_rwkv_gn_b, v_w_out, v_ffn2_norm, v_ffn2_w_gate, v_ffn2_w_up, v_ffn2_w_down, v_final_norm):
    args = dict(locals())
    wts = {n: args[n] for n in ALL_WEIGHTS}
    moms = {n: args["m_" + n] for n in ALL_WEIGHTS}
    vars_ = {n: args["v_" + n] for n in ALL_WEIGHTS}

    me = 2 * lax.axis_index("x") + lax.axis_index("y")
    c_idx = lax.axis_index("c").astype(jnp.int32).reshape(1)
    me_idx = me.astype(jnp.int32).reshape(1)
    shard_of = {n: _to_rows(n, wts[n]).astype(BF16) for n in BIG}
    shard_of["packed"] = _pack(wts, {n: wts[n] for n in SMALL}).astype(BF16)
    group = {"ffn1": BIG[0:3], "ffn2": BIG[3:6]}

    def slot_bufs(names):
        return [lax.dynamic_update_slice(jnp.zeros((N_CHIPS,) + shard_of[n].shape, BF16), shard_of[n][None],
                                         (me, 0, 0)) for n in names]

    def ffn_weights(tag, gathered):
        return {f"{tag}_wgt": gathered[0].reshape(D_FF, D_MODEL), f"{tag}_wut": gathered[1].reshape(D_FF, D_MODEL),
                f"{tag}_wd": gathered[2].reshape(D_FF, D_MODEL)}

    def w_in_weights(gathered):
        w_in_full = jnp.concatenate([gathered[0][q] for q in range(N_CHIPS)], axis=1)
        return {"w_in_h": w_in_full[:, :N_HGRN_COLS],
                "w_in_r": jnp.pad(w_in_full[:, N_HGRN_COLS:], ((0, 0), (0, N_RWKV_PAD - N_RWKV_COLS)))}

    def mixer_weights(gathered):
        w_out_full = gathered[0].reshape(D_MODEL, D_MODEL)
        packs = gathered[1].reshape(N_CHIPS, PACK_ROWS * LANES)
        full, off = {}, 0
        for n in PACKED:
            shp = _shard_shape(n)
            full[n] = jnp.concatenate([packs[q, off:off + _numel(shp)].reshape(shp) for q in range(N_CHIPS)], axis=1)
            off += _numel(shp)
        zrow = lambda nrow: jnp.zeros((nrow, W_B), BF16)
        return {"w_out_a": w_out_full[:W_A], "w_out_b": w_out_full[W_A:],
                "w2_pad": jnp.concatenate([full["rwkv_w2"], zrow(LORA_PAD - 32)], axis=0),
                "a2_pad": jnp.concatenate([zrow(32), full["rwkv_a2"], zrow(LORA_PAD - 64)], axis=0),
                "g2_pad": jnp.concatenate([zrow(64), full["rwkv_g2"], zrow(LORA_PAD - 160)], axis=0)}

    plan = _Plan()
    w = {}
    plan.carry("ffn1_rms", lambda g: _gather_comm(slot_bufs(group["ffn1"][:2])),
               lambda res, w_: w_.update({"ffn1_wgt": res[0].reshape(D_FF, D_MODEL),
                                          "ffn1_wut": res[1].reshape(D_FF, D_MODEL)}))

    def after_gate_up(res, w_):
        w_["ffn1_wd"] = res[0].reshape(D_FF, D_MODEL)
        w_.update(w_in_weights(res[1:]))

    plan.carry("ffn1_gate_up", lambda g: _gather_comm(slot_bufs(("ffn1_w_down", "w_in"))), after_gate_up)
    plan.carry("ffn1_down", lambda g: _gather_comm(slot_bufs(("w_out", "packed"))),
               lambda res, w_: w_.update(mixer_weights(res)))
    plan.carry("rwkv_fwd", lambda g: _gather_comm(slot_bufs(group["ffn2"])),
               lambda res, w_: w_.update(ffn_weights("ffn2", res)))
    w["ffn1_norm"], w["ffn2_norm"] = ffn1_norm, ffn2_norm
    w["mix_norm"] = mix_norm
    w["lb0"], w["lb1"] = hgrn_lb_logits[0:1], hgrn_lb_logits[1:2]
    w["hgrn_out_norm"] = hgrn_out_norm
    w["mu_pad"] = jnp.pad(rwkv_shift_mu, ((0, 0), (0, N_RWKV_PAD - N_RWKV_COLS)))
    for n in ("rwkv_w0", "rwkv_a0", "rwkv_k_k", "rwkv_k_a", "rwkv_gn_w", "rwkv_gn_b"):
        w[n] = wts[n]
    w["rwkv_r_k"] = rwkv_r_k.reshape(1, W_B)
    w["final_norm"] = final_norm.reshape(1, D_MODEL)

    def reduce_rows(names, gs):
        r1 = _run_comm(_sibling_exchange_comm(gs), "grad_sibling_exchange")
        s4 = [_add_halves(gt, rt, c_idx, f"grad_add_halves_{n}") for gt, rt, n in zip(gs, r1, names)]
        r2 = _run_comm(_chip_exchange_comm(s4), "grad_chip_exchange")
        return [_sum_chips(rt, st, me_idx, f"grad_sum_chips_{n}") for rt, st, n in zip(r2, s4, names)]

    early = {}

    def reduce_early(names, grads_of, sibling_host, chips_host):
        def sibling_comm(g):
            early[names, "gs"] = grads_of(g)
            return _sibling_exchange_comm(early[names, "gs"])

        def after_sibling(res, w_):
            early[names, "s4"] = [_add_halves(gt, rt, c_idx, f"grad_add_halves_{n}")
                                  for gt, rt, n in zip(early[names, "gs"], res, names)]

        def after_chips(res, w_):
            early.update(zip(names, [_sum_chips(rt, st, me_idx, f"grad_sum_chips_{n}")
                                     for rt, st, n in zip(res, early[names, "s4"], names)]))

        plan.carry(sibling_host, sibling_comm, after_sibling)
        plan.carry(chips_host, lambda g: _chip_exchange_comm(early[names, "s4"]), after_chips)

    def proj_grads(g):
        g_w_in = jnp.concatenate([g["w_in_h"], g["w_in_r"][:, :N_RWKV_COLS]], axis=1)
        return [jnp.concatenate([g["w_out_a"], g["w_out_b"]], axis=0).reshape(N_CHIPS, -1, D_MODEL),
                jnp.stack([_quarter(g_w_in, "w_in", q) for q in range(N_CHIPS)])]

    rows_of = lambda keys: (lambda g: [g[k].reshape(N_CHIPS, -1, D_MODEL) for k in keys])
    reduce_early(group["ffn2"], rows_of(("ffn2_wgt", "ffn2_wut", "ffn2_wd")), "hgrn_bwd", "rwkv_bwd")
    reduce_early(("w_out", "w_in"), proj_grads, "mix_drms", "ffn1_dact")
    reduce_early(("ffn1_w_down",), rows_of(("ffn1_wd",)), "ffn1_dwg", "ffn1_dwu")
    reduce_early(("ffn1_w_gate",), rows_of(("ffn1_wgt",)), "ffn1_dwu", "ffn1_dh_g")
    reduce_early(("ffn1_w_up",), rows_of(("ffn1_wut",)), "ffn1_dh_g", "ffn1_dh_u")
    loss_slab, grad_x, g = _local_step(x[0], loss_target[0], w, plan)
    loss = lax.psum(loss_slab[0, 0], ("x", "y", "c"))

    gfull = {
        "rwkv_w2": g["w2_pad"][0:32], "rwkv_a2": g["a2_pad"][32:64], "rwkv_g2": g["g2_pad"][64:160],
    }
    gsmall = {
        "ffn1_norm": g["ffn1_norm"], "mix_norm": g["mix_norm"],
        "hgrn_lb_logits": jnp.concatenate([g["lb0"], g["lb1"]], axis=0), "hgrn_out_norm": g["hgrn_out_norm"],
        "rwkv_shift_mu": g["mu_pad"][:, :N_RWKV_COLS], "rwkv_w0": g["rwkv_w0"], "rwkv_a0": g["rwkv_a0"],
        "rwkv_k_k": g["rwkv_k_k"], "rwkv_k_a": g["rwkv_k_a"], "rwkv_r_k": g["rwkv_r_k"],
        "rwkv_gn_w": g["rwkv_gn_w"], "rwkv_gn_b": g["rwkv_gn_b"], "ffn2_norm": g["ffn2_norm"],
        "final_norm": g["final_norm"],
    }
    packed = jnp.stack([_pack({n: _quarter(gfull[n], n, q) for n in PACKED}, gsmall) for q in range(N_CHIPS)])
    early["packed"], = reduce_rows(["packed"], [packed])
    names = list(BIG) + ["packed"]
    own = [early[n] for n in names]
    other = _run_comm(_sibling_swap_comm(own), "grad_sibling_swap")

    def rows_list(d):
        return [_to_rows(n, d[n]) for n in BIG] + [_pack(d, {n: d[n] for n in SMALL})]

    outs = [_adamw(wt, go, gx, mt, vt, c_idx, f"adamw_{n}")
            for wt, go, gx, mt, vt, n in zip(rows_list(wts), own, other, rows_list(moms), rows_list(vars_), names)]
    results = []
    for k in range(4):
        per = [outs[i][k] for i in range(len(names))]
        d = {n: _from_rows(n, z) for n, z in zip(BIG, per[:-1])}
        d.update(_unpack(per[-1]))
        results.append(d)
    return (loss, grad_x[None], *[r[n] for r in results for n in ALL_WEIGHTS])
```

```python
import collections
import functools

import jax
import jax.numpy as jnp
from jax import lax
from jax.experimental import pallas as pl
from jax.experimental.pallas import tpu as pltpu

F32 = jnp.float32
BF16 = jnp.bfloat16
SDS = jax.ShapeDtypeStruct
MESH = pl.DeviceIdType.MESH

D_MODEL = 1024
D_FF = 2816
W_A = 512
W_B = 512
HA_HEADS, HA_DIM = 4, 128
HB_HEADS, HB_DIM = 8, 64
HGRN_CHUNK = 64
HGRN_GROUP = 2
RWKV_CHUNK = 16
RWKV_GROUP = 4
N_HGRN_COLS = 4 * W_A
N_RWKV_COLS = 3 * W_B + 32 + 32 + 96
N_RWKV_PAD = 1792
LORA_PAD = 256
NORM_EPS = 1e-6
RWKV_GN_EPS = 64e-5
L2_EPS = 1e-12
ADAM_LR, ADAM_B1, ADAM_B2, ADAM_EPS, ADAM_WD, ADAM_STEP = 0.001, 0.9, 0.999, 1e-8, 0.01, 10

N_CHIPS = 4
VMEM_LIMIT_V7X = 56 * 1024 * 1024
LANES = 1024

SHARDED_SHAPES = {
    "ffn1_w_gate": ((D_MODEL, D_FF), 1), "ffn1_w_up": ((D_MODEL, D_FF), 1), "ffn1_w_down": ((D_FF, D_MODEL), 0),
    "w_in": ((D_MODEL, N_HGRN_COLS + N_RWKV_COLS), 1), "rwkv_w2": ((32, W_B), 1), "rwkv_a2": ((32, W_B), 1),
    "rwkv_g2": ((96, W_B), 1), "w_out": ((D_MODEL, D_MODEL), 0),
    "ffn2_w_gate": ((D_MODEL, D_FF), 1), "ffn2_w_up": ((D_MODEL, D_FF), 1), "ffn2_w_down": ((D_FF, D_MODEL), 0),
}
SMALL = ("ffn1_norm", "mix_norm", "hgrn_lb_logits", "hgrn_out_norm", "rwkv_shift_mu", "rwkv_w0", "rwkv_a0",
         "rwkv_k_k", "rwkv_k_a", "rwkv_r_k", "rwkv_gn_w", "rwkv_gn_b", "ffn2_norm", "final_norm")
ALL_WEIGHTS = ("ffn1_norm", "ffn1_w_gate", "ffn1_w_up", "ffn1_w_down", "mix_norm", "w_in", "hgrn_lb_logits",
               "hgrn_out_norm", "rwkv_shift_mu", "rwkv_w0", "rwkv_w2", "rwkv_a0", "rwkv_a2", "rwkv_g2", "rwkv_k_k",
               "rwkv_k_a", "rwkv_r_k", "rwkv_gn_w", "rwkv_gn_b", "w_out", "ffn2_norm", "ffn2_w_gate", "ffn2_w_up",
               "ffn2_w_down", "final_norm")


def _shard_shape(name):
    shape, ax = SHARDED_SHAPES[name]
    return tuple(s // N_CHIPS if i == ax else s for i, s in enumerate(shape))


def _numel(shape):
    n = 1
    for s in shape:
        n *= s
    return n


def _params(sem=None):
    return pltpu.CompilerParams(dimension_semantics=sem, vmem_limit_bytes=VMEM_LIMIT_V7X)


def _split2(x):
    hi = x.astype(BF16)
    return hi, (x.astype(F32) - hi.astype(F32)).astype(BF16)


def _dg(x, y, cx, cy, hi):
    dn = (((cx,), (cy,)), ((), ()))
    dot = lambda p, q: lax.dot_general(p, q, dn, preferred_element_type=F32)
    if hi == "x3":
        (xh, xl), (yh, yl) = _split2(x), _split2(y)
        return dot(xh, yh) + (dot(xh, yl) + dot(xl, yh))
    return dot(x.astype(BF16), y.astype(BF16))


def _make_mm(hi, cotangent_forms=None):
    @jax.custom_vjp
    def nn(x, y):
        return _dg(x, y, 1, 0, hi)

    @jax.custom_vjp
    def nt(x, y):
        return _dg(x, y, 1, 1, hi)

    @jax.custom_vjp
    def tn(x, y):
        return _dg(x, y, 0, 0, hi)

    bnn, bnt, btn = cotangent_forms or (nn, nt, tn)
    nn.defvjp(lambda x, y: (nn(x, y), (x, y)), lambda r, g: (bnt(g, r[1]), btn(r[0], g)))
    nt.defvjp(lambda x, y: (nt(x, y), (x, y)), lambda r, g: (bnn(g, r[1]), btn(g, r[0])))
    tn.defvjp(lambda x, y: (tn(x, y), (x, y)), lambda r, g: (bnt(r[1], g), bnn(r[0], g)))
    return nn, nt, tn


_nn, _nt, _tn = _make_mm(False)
_nn_x3, _nt_x3, _tn_x3 = _make_mm("x3", (_nn, _nt, _tn))


def _tri_apply(x, transpose):
    c = x.shape[0]
    tri = (lax.broadcasted_iota(jnp.int32, (c, c), 1) <= lax.broadcasted_iota(jnp.int32, (c, c), 0)).astype(BF16)
    dn = (((0 if transpose else 1,), (0,)), ((), ()))
    p1, p2 = _split2(x)
    dot = lambda p: lax.dot_general(tri, p, dn, preferred_element_type=F32)
    return dot(p1) + dot(p2)


@jax.custom_vjp
def _cumsum_rows(x):
    return _tri_apply(x, False)


_cumsum_rows.defvjp(lambda x: (_tri_apply(x, False), None), lambda _, g: (_tri_apply(g, True),))


def _sigmoid(x):
    return 1.0 / (1.0 + jnp.exp(-x))


def _silu(x):
    return x * _sigmoid(x)


def _softplus(z):
    return jnp.maximum(z, 0.0) + jnp.log(1.0 + jnp.exp(-jnp.abs(z)))


def _mm(a, b, *, ta=False, tb=False, tm, tn, tk, name, out_dtype=F32, res=None, scale=None, comm=None):
    m = a.shape[1] if ta else a.shape[0]
    kdim = a.shape[0] if ta else a.shape[1]
    n = b.shape[0] if tb else b.shape[1]
    assert (b.shape[1] if tb else b.shape[0]) == kdim
    tm, tn, tk = min(tm, m), min(tn, n), min(tk, kdim)
    assert m % tm == 0 and n % tn == 0 and kdim % tk == 0, (name, m, n, kdim)
    nk = kdim // tk
    a_spec = pl.BlockSpec((tk, tm), lambda i, j, k: (k, i)) if ta else pl.BlockSpec((tm, tk), lambda i, j, k: (i, k))
    b_spec = pl.BlockSpec((tn, tk), lambda i, j, k: (j, k)) if tb else pl.BlockSpec((tk, tn), lambda i, j, k: (k, j))
    o_spec = pl.BlockSpec((tm, tn), lambda i, j, k: (i, j))
    ca, cb = (0 if ta else 1), (1 if tb else 0)

    def body(*refs):
        if res is not None:
            a_ref, b_ref, r_ref, o_ref, acc_ref = refs
        else:
            a_ref, b_ref, o_ref, acc_ref = refs
        k = pl.program_id(2)

        @pl.when(k == 0)
        def _():
            acc_ref[...] = jnp.zeros_like(acc_ref)

        acc_ref[...] += _dg(a_ref[...], b_ref[...], ca, cb, False)

        @pl.when(k == nk - 1)
        def _():
            acc = acc_ref[...]
            if scale is not None:
                acc = acc * scale
            if res is not None:
                acc = r_ref[...] + acc
            o_ref[...] = acc.astype(out_dtype)

    in_specs = [a_spec, b_spec] + ([o_spec] if res is not None else [])
    args = (a, b) + ((res,) if res is not None else ())
    if comm is None:
        return pl.pallas_call(
            body, name=name, grid=(m // tm, n // tn, nk), in_specs=in_specs, out_specs=o_spec,
            out_shape=SDS((m, n), out_dtype), scratch_shapes=[pltpu.VMEM((tm, tn), F32)],
            compiler_params=_params(("parallel", "parallel", "arbitrary")))(*args)
    (out,), carried = _hosting_call(
        body, comm, name=name, grid=(m // tm, n // tn, nk), in_specs=in_specs, out_specs=[o_spec],
        out_shape=[SDS((m, n), out_dtype)], scratch_shapes=[pltpu.VMEM((tm, tn), F32)], args=args)
    return out, carried


def _row_spec(x, tm):
    if isinstance(x, tuple):
        arr, w, j = x
        return arr, pl.BlockSpec((tm, w), lambda i, j=j: (i, j))
    return x, pl.BlockSpec((tm, x.shape[1]), lambda i: (i, 0))


def _par_spec(p):
    if isinstance(p, tuple):
        arr, w, j = p
        return arr, pl.BlockSpec((arr.shape[0], w), lambda i, j=j: (0, j))
    return p, pl.BlockSpec(p.shape, lambda i: (0, 0))


def _store_groups(refs, groups, vals):
    for ref, idxs in zip(refs, groups):
        off = 0
        for ix in idxs:
            v = vals[ix]
            ref[:, off:off + v.shape[1]] = v.astype(ref.dtype)
            off += v.shape[1]


SUBLANES = 8


def _x_plan(xs, tm, t):
    arrays, specs, plan = [], [], []
    nb = tm // SUBLANES
    for x in xs:
        if isinstance(x, tuple) and isinstance(x[0], str):
            kind, arr, w, j = x
            if kind == "prev":
                halo = lambda i, j=j: (jnp.maximum(i * nb - 1, 0), j)
            else:
                halo = lambda i, j=j: (jnp.minimum((i + 1) * nb, t // SUBLANES - 1), j)
            arrays += [arr, arr]
            specs += [pl.BlockSpec((tm, w), lambda i, j=j: (i, j)), pl.BlockSpec((SUBLANES, w), halo)]
            plan.append((kind, 2, w))
        else:
            arr, spec = _row_spec(x, tm)
            arrays.append(arr)
            specs.append(spec)
            plan.append(("plain", 1, spec.block_shape[1]))
    return arrays, specs, plan


def _x_vals(refs, plan, tm, nt):
    vals, k = [], 0
    i = pl.program_id(0)
    rows = lax.broadcasted_iota(jnp.int32, (tm, 1), 0)
    for kind, n, _ in plan:
        main = refs[k][...].astype(F32)
        if kind == "prev":
            edge = jnp.where(i == 0, 0.0, refs[k + 1][SUBLANES - 1:SUBLANES, :].astype(F32))
            main = jnp.where(rows == 0, edge, pltpu.roll(main, 1, 0))
        elif kind == "next":
            edge = jnp.where(i == nt - 1, 0.0, refs[k + 1][0:1, :].astype(F32))
            main = jnp.where(rows == tm - 1, edge, pltpu.roll(main, tm - 1, 0))
        vals.append(main)
        k += n
    return vals


def _tile_rows(xs, tm):
    arr = xs[0]
    if isinstance(arr, tuple):
        arr = arr[1] if isinstance(arr[0], str) else arr[0]
    return min(tm, arr.shape[0]), arr.shape[0]


def _rowwise(f, xs, params, out_groups, out_dtypes, *, tm, name, comm=None):
    tm, t = _tile_rows(xs, tm)
    nt = t // tm
    xa, xspecs, plan = _x_plan(xs, tm, t)
    pa, pspecs = (zip(*[_par_spec(p) for p in params]) if params else ((), ()))
    nxr, npar = len(xa), len(pa)
    x_sds = [SDS((tm, w), F32) for _, _, w in plan]
    p_sds = [SDS(s.block_shape, F32) for s in pspecs]
    outs_sds = jax.eval_shape(lambda *vals: f(*vals), *x_sds, *p_sds)
    widths = [sum(outs_sds[ix].shape[1] for ix in idxs) for idxs in out_groups]

    def body(*refs):
        vals = _x_vals(refs[:nxr], plan, tm, nt) + [r[...].astype(F32) for r in refs[nxr:nxr + npar]]
        outs = f(*vals)
        _store_groups(refs[nxr + npar:], out_groups, outs)

    res, carried = _hosting_call(
        body, comm, name=name, grid=(nt,), in_specs=list(xspecs) + list(pspecs),
        out_specs=[pl.BlockSpec((tm, w), lambda i: (i, 0)) for w in widths],
        out_shape=[SDS((t, w), dt) for w, dt in zip(widths, out_dtypes)], scratch_shapes=[], args=(*xa, *pa))
    return res if comm is None else (res, carried)


def _rowwise_bwd(f, xs, params, cots, *, x_grad, p_grad, dx_groups, dx_dtypes, tm, name, extra=None, comm=None):
    tm, t = _tile_rows(xs, tm)
    nt = t // tm
    xa, xspecs, plan = _x_plan(xs, tm, t)
    pa, pspecs = (zip(*[_par_spec(p) for p in params]) if params else ((), ()))
    ca, cspecs = zip(*[_row_spec(c, tm) for c in cots])
    extra = extra or {}
    ekeys = sorted(extra)
    ea, especs = (zip(*[_row_spec(extra[k], tm) for k in ekeys]) if ekeys else ((), ()))
    nx, nxr, npar, nc, ne = len(plan), len(xa), len(pa), len(ca), len(ea)
    gx = [i for i in range(nx) if x_grad[i]]
    gp = [i for i in range(npar) if p_grad[i]]
    widths = [sum(plan[gx[ix]][2] for ix in idxs) for idxs in dx_groups]
    ng = len(dx_groups)

    def body(*refs):
        ins = refs[:nxr + npar + nc + ne]
        outs = refs[nxr + npar + nc + ne:]
        vals = _x_vals(ins[:nxr], plan, tm, nt) + [r[...].astype(F32) for r in ins[nxr:nxr + npar]]
        cvals = tuple(r[...].astype(F32) for r in ins[nxr + npar:nxr + npar + nc])
        evals = [r[...].astype(F32) for r in ins[nxr + npar + nc:]]
        diff_idx = gx + [nx + i for i in gp]

        def g(*dargs):
            full = list(vals)
            for ix, v in zip(diff_idx, dargs):
                full[ix] = v
            return tuple(f(*full))

        _, vjp = jax.vjp(g, *[vals[ix] for ix in diff_idx])
        grads = vjp(cvals)
        dxs = list(grads[:len(gx)])
        for k, ev in zip(ekeys, evals):
            dxs[k] = dxs[k] + ev
        _store_groups(outs[:ng], dx_groups, dxs)
        i = pl.program_id(0)
        for ref, gval in zip(outs[ng:], grads[len(gx):]):
            @pl.when(i == 0)
            def _(ref=ref):
                ref[...] = jnp.zeros_like(ref)
            ref[...] += gval

    dp_specs = [pl.BlockSpec(pspecs[i].block_shape, lambda i: (0, 0)) for i in gp]
    dp_shapes = [SDS(pspecs[i].block_shape, F32) for i in gp]
    res, carried = _hosting_call(
        body, comm, name=name, grid=(nt,), in_specs=list(xspecs) + list(pspecs) + list(cspecs) + list(especs),
        out_specs=[pl.BlockSpec((tm, w), lambda i: (i, 0)) for w in widths] + dp_specs,
        out_shape=[SDS((t, w), dt) for w, dt in zip(widths, dx_dtypes)] + dp_shapes, scratch_shapes=[],
        args=(*xa, *pa, *ca, *ea))
    return res if comm is None else (res, carried)


def _rms_f(x, g):
    return (x * lax.rsqrt(jnp.mean(x * x, axis=-1, keepdims=True) + NORM_EPS) * g,)


def _group_sum_impl(x, ones_bd):
    p1, p2 = _split2(x)
    dot = lambda p: lax.dot_general(p, ones_bd.astype(BF16), (((1,), (0,)), ((), ())), preferred_element_type=F32)
    return dot(p1) + dot(p2)


@jax.custom_vjp
def _group_sum(x, ones_bd):
    return _group_sum_impl(x, ones_bd)


_group_sum.defvjp(lambda x, o: (_group_sum_impl(x, o), o),
                  lambda o, g: (_group_sum_impl(g, o), jnp.zeros_like(o)))


def _rwkv_prep_f(r, k, v, lo, rp, kp, vp, lop, mu_r, mu_k, mu_v, mu_lo, w0, w2p, a0, a2p, g2p, k_k, k_a, ones_bd):
    r = r + mu_r * (rp - r)
    k = k + mu_k * (kp - k)
    v = v + mu_v * (vp - v)
    lo = lo + mu_lo * (lop - lo)
    w_log = -_softplus(-(w0 + _nn(jnp.tanh(lo), w2p))) - 0.5
    lw = -jnp.exp(w_log)
    a_g = _sigmoid(a0 + _nn(lo, a2p))
    g = _nn(_sigmoid(lo), g2p)
    kk = k * k_k
    kk = kk / jnp.maximum(jnp.sqrt(_group_sum(kk * kk, ones_bd)), L2_EPS)
    k2 = k * (1.0 + (a_g - 1.0) * k_a)
    return r, lw, k2, v, -kk, kk * a_g, g


def _rwkv_post_f(y, r, k2, v, g, r_k, gn_w, gn_b, ones_bd):
    inv_n = 1.0 / HB_DIM
    mean = _group_sum(y, ones_bd) * inv_n
    yc = y - mean
    var = _group_sum(yc * yc, ones_bd) * inv_n
    yn = yc * lax.rsqrt(var + RWKV_GN_EPS) * gn_w + gn_b
    bonus = _group_sum(r * k2 * r_k, ones_bd) * v
    return ((yn + bonus) * g,)


def _tri(c, strict=False):
    ii = lax.broadcasted_iota(jnp.int32, (c, c), 0)
    jj = lax.broadcasted_iota(jnp.int32, (c, c), 1)
    return (jj < ii) if strict else (jj <= ii)


def _hgrn_step(st0, q_a, f_a, i_a, g_a, l0, l1, onorm):
    nh, nj = len(q_a), len(q_a[0])
    c = q_a[0][0].shape[0]
    combos = [(j, h) for j in range(nj) for h in range(nh)]
    every = lambda fn: {q: fn(q) for q in combos}
    at_ = lambda d: (lambda q: d[q[1]][q[0]])
    qa_, fa_, ia_, ga_ = (at_(z) for z in (q_a, f_a, i_a, g_a))
    incl = _tri(c)
    rows = lax.broadcasted_iota(jnp.int32, (c, 1), 0)
    lb = []
    for h in range(nh):
        mx = jnp.maximum(l0[h], l1[h])
        e0, e1 = jnp.exp(l0[h] - mx), jnp.exp(l1[h] - mx)
        lb.append(e0 / (e0 + e1))
    forget = every(lambda q: lb[q[1]] + (1.0 - lb[q[1]]) * _sigmoid(fa_(q)))
    qs = every(lambda q: _silu(qa_(q)))
    kk = every(lambda q: 1.0 - forget[q])
    lf = every(lambda q: jnp.log(forget[q]))
    bcum = every(lambda q: _cumsum_rows(lf[q]))
    bref = every(lambda q: jnp.sum(jnp.where(rows <= c // 2, lf[q], 0.0), axis=0, keepdims=True))
    blast = every(lambda q: jnp.sum(lf[q], axis=0, keepdims=True))
    scores = every(lambda q: jnp.where(incl, _nt(qs[q] * jnp.exp(bcum[q] - bref[q]),
                                                 kk[q] * jnp.exp(bref[q] - bcum[q])), 0.0))
    intra = every(lambda q: _nn(scores[q], ia_(q)))
    qb = every(lambda q: qs[q] * jnp.exp(bcum[q]))
    upd = every(lambda q: _tn(ia_(q), kk[q] * jnp.exp(blast[q] - bcum[q])))
    dec = every(lambda q: jnp.exp(blast[q]))
    st = list(st0)
    o = {}
    for j in range(nj):
        for h in range(nh):
            o[(j, h)] = intra[(j, h)] + _nt(qb[(j, h)], st[h])
        st = [st[h] * dec[(j, h)] + upd[(j, h)] for h in range(nh)]
    out = every(lambda q: o[q] * lax.rsqrt(jnp.mean(o[q] * o[q], axis=-1, keepdims=True) + NORM_EPS)
                * onorm[q[1]] * _silu(ga_(q)))
    return [[out[(j, h)] for j in range(nj)] for h in range(nh)], st


def _hgrn_blocks(ref, nj, c):
    return [[ref[j * c:(j + 1) * c, h * HA_DIM:(h + 1) * HA_DIM] for j in range(nj)] for h in range(HA_HEADS)]


def _hgrn_cols(ref):
    return [ref[:, h * HA_DIM:(h + 1) * HA_DIM] for h in range(HA_HEADS)]


def _hgrn_fwd(p_h, l0, l1, onorm):
    t = p_h.shape[0]
    cc, nj = HGRN_CHUNK, HGRN_GROUP
    c = cc * nj
    n = t // c

    def body(q_ref, f_ref, i_ref, g_ref, l0_ref, l1_ref, on_ref, o_ref, hs_ref, st_ref):
        @pl.when(pl.program_id(0) == 0)
        def _():
            st_ref[...] = jnp.zeros_like(st_ref)

        hs_ref[0] = st_ref[...]
        o, st1 = _hgrn_step([st_ref[h] for h in range(HA_HEADS)],
                            *[_hgrn_blocks(ref, nj, cc) for ref in (q_ref, f_ref, i_ref, g_ref)],
                            _hgrn_cols(l0_ref), _hgrn_cols(l1_ref), _hgrn_cols(on_ref))
        for h in range(HA_HEADS):
            for j in range(nj):
                o_ref[j * cc:(j + 1) * cc, h * HA_DIM:(h + 1) * HA_DIM] = o[h][j]
            st_ref[h] = st1[h]

    col = lambda j: pl.BlockSpec((c, W_A), lambda i, j=j: (i, j))
    par = pl.BlockSpec((1, W_A), lambda i: (0, 0))
    return pl.pallas_call(
        body, name="hgrn_fwd", grid=(n,), in_specs=[col(0), col(1), col(2), col(3), par, par, par],
        out_specs=[pl.BlockSpec((c, W_A), lambda i: (i, 0)),
                   pl.BlockSpec((1, HA_HEADS, HA_DIM, HA_DIM), lambda i: (i, 0, 0, 0))],
        out_shape=[SDS((t, W_A), F32), SDS((n, HA_HEADS, HA_DIM, HA_DIM), F32)],
        scratch_shapes=[pltpu.VMEM((HA_HEADS, HA_DIM, HA_DIM), F32)],
        compiler_params=_params(("arbitrary",)))(p_h, p_h, p_h, p_h, l0, l1, onorm)


def _hgrn_bwd(p_h, l0, l1, onorm, hs, do, do_col, comm=None):
    t = p_h.shape[0]
    cc, nj = HGRN_CHUNK, HGRN_GROUP
    c = cc * nj
    n = t // c

    def body(q_ref, f_ref, i_ref, g_ref, l0_ref, l1_ref, on_ref, hs_ref, do_ref,
             dp_ref, dl0_ref, dl1_ref, don_ref, dst_ref):
        @pl.when(pl.program_id(0) == 0)
        def _():
            dst_ref[...] = jnp.zeros_like(dst_ref)
            dl0_ref[...] = jnp.zeros_like(dl0_ref)
            dl1_ref[...] = jnp.zeros_like(dl1_ref)
            don_ref[...] = jnp.zeros_like(don_ref)

        args = ([hs_ref[0, h] for h in range(HA_HEADS)],
                *[_hgrn_blocks(ref, nj, cc) for ref in (q_ref, f_ref, i_ref, g_ref)],
                _hgrn_cols(l0_ref), _hgrn_cols(l1_ref), _hgrn_cols(on_ref))
        _, vjp = jax.vjp(_hgrn_step, *args)
        dst0, dq, df, di, dg, dl0, dl1, don = vjp((_hgrn_blocks(do_ref, nj, cc),
                                                   [dst_ref[h] for h in range(HA_HEADS)]))
        for h in range(HA_HEADS):
            sl = slice(h * HA_DIM, (h + 1) * HA_DIM)
            for k, dv in enumerate((dq, df, di, dg)):
                for j in range(nj):
                    dp_ref[j * cc:(j + 1) * cc, k * W_A + h * HA_DIM:k * W_A + (h + 1) * HA_DIM] = dv[h][j]
            dl0_ref[:, sl] += dl0[h]
            dl1_ref[:, sl] += dl1[h]
            don_ref[:, sl] += don[h]
            dst_ref[h] = dst0[h]

    col = lambda j: pl.BlockSpec((c, W_A), lambda i, j=j: (n - 1 - i, j))
    par = pl.BlockSpec((1, W_A), lambda i: (0, 0))
    return _hosting_call(
        body, comm, name="hgrn_bwd", grid=(n,),
        in_specs=[col(0), col(1), col(2), col(3), par, par, par,
                  pl.BlockSpec((1, HA_HEADS, HA_DIM, HA_DIM), lambda i: (n - 1 - i, 0, 0, 0)),
                  pl.BlockSpec((c, W_A), lambda i: (n - 1 - i, do_col))],
        out_specs=[pl.BlockSpec((c, N_HGRN_COLS), lambda i: (n - 1 - i, 0)), par, par, par],
        out_shape=[SDS((t, N_HGRN_COLS), F32), SDS((1, W_A), F32), SDS((1, W_A), F32), SDS((1, W_A), F32)],
        scratch_shapes=[pltpu.VMEM((HA_HEADS, HA_DIM, HA_DIM), F32)],
        args=(p_h, p_h, p_h, p_h, l0, l1, onorm, hs, do))


HB_PAIRS = HB_HEADS // 2
PAIR_W = 2 * HB_DIM


def _head_lane_masks():
    lane = lax.broadcasted_iota(jnp.int32, (1, PAIR_W), 1)
    return (lane < HB_DIM).astype(F32), (lane >= HB_DIM).astype(F32)


@jax.custom_vjp
def _stack_heads(x):
    m0, m1 = _head_lane_masks()
    return jnp.concatenate([x * m0, x * m1], axis=0)


def _stack_heads_bwd(_, g):
    m0, m1 = _head_lane_masks()
    c = g.shape[0] // 2
    return (g[:c] * m0 + g[c:] * m1,)


_stack_heads.defvjp(lambda x: (_stack_heads(x), None), _stack_heads_bwd)


@jax.custom_vjp
def _unstack_heads(ys):
    c = ys.shape[0] // 2
    return ys[:c] + ys[c:]


_unstack_heads.defvjp(lambda ys: (_unstack_heads(ys), None), lambda _, g: (_stack_heads(g),))


def _same_head_block(c):
    ii = lax.broadcasted_iota(jnp.int32, (2 * c, 2 * c), 0)
    jj = lax.broadcasted_iota(jnp.int32, (2 * c, 2 * c), 1)
    same = (ii < c) == (jj < c)
    return same & (jj <= ii), same & (jj < ii), (ii == jj).astype(F32)


@jax.custom_vjp
def _rows_join(top, bottom):
    return jnp.concatenate([top, bottom], axis=0)


def _rows_join_bwd(n_top, g):
    return g[:n_top], g[n_top:]


_rows_join.defvjp(lambda top, bottom: (_rows_join(top, bottom), top.shape[0]), _rows_join_bwd)


def _rows_split_impl(x, n_top):
    return x[:n_top], x[n_top:]


_rows_split = jax.custom_vjp(_rows_split_impl, nondiff_argnums=(1,))
_rows_split.defvjp(lambda x, n_top: (_rows_split_impl(x, n_top), None),
                   lambda n_top, _, g: (jnp.concatenate([g[0], g[1]], axis=0),))


def _rwkv_step(s0, r, lw, k, v, a, b):
    npair, nj = len(r), len(r[0])
    c = r[0][0].shape[0]
    combos = [(j, p) for j in range(nj) for p in range(npair)]
    every = lambda fn: {q: fn(q) for q in combos}
    at_ = lambda d: (lambda q: d[q[1]][q[0]])
    r_, lw_, k_, v_, a_, b_ = (at_(z) for z in (r, lw, k, v, a, b))
    incl, strict, eye = _same_head_block(c)

    gam = every(lambda q: _cumsum_rows(lw_(q)))
    gtot = every(lambda q: jnp.sum(lw_(q), axis=0, keepdims=True))
    eneg = every(lambda q: jnp.exp(-gam[q]))
    edec = every(lambda q: jnp.exp(gtot[q] - gam[q]))
    at = every(lambda q: _stack_heads(a_(q) * jnp.exp(gam[q] - lw_(q))))
    rt = every(lambda q: _stack_heads(r_(q) * jnp.exp(gam[q])))
    bt = every(lambda q: _stack_heads(b_(q) * eneg[q]))
    kt = every(lambda q: _stack_heads(k_(q) * eneg[q]))
    bdec = every(lambda q: _stack_heads(b_(q) * edec[q]))
    kdec = every(lambda q: _stack_heads(k_(q) * edec[q]))
    vs = every(lambda q: _stack_heads(v_(q)))
    a_ab = every(lambda q: jnp.where(strict, _nt(at[q], bt[q]), 0.0))
    a_ak = every(lambda q: jnp.where(strict, _nt(at[q], kt[q]), 0.0))
    a_rb = every(lambda q: jnp.where(incl, _nt(rt[q], bt[q]), 0.0))
    a_rk = every(lambda q: jnp.where(incl, _nt(rt[q], kt[q]), 0.0))
    tinv = every(lambda q: eye + a_ab[q])
    pw = a_ab
    span = 2
    while span < c:
        pw = every(lambda q, pw=pw: _nn_x3(pw[q], pw[q]))
        tinv = every(lambda q, pw=pw, tinv=tinv: tinv[q] + _nn_x3(pw[q], tinv[q]))
        span *= 2
    akv = every(lambda q: _nn(a_ak[q], vs[q]))
    w1 = every(lambda q: _nn(tinv[q], at[q]))
    u0 = every(lambda q: _nn(tinv[q], akv[q]))
    wr = every(lambda q: _rows_join(w1[q], rt[q]))
    bk = every(lambda q: _rows_join(bdec[q], kdec[q]))
    yv = every(lambda q: _nn(a_rk[q], vs[q]))
    gdec = every(lambda q: jnp.exp(gtot[q]))

    s = list(s0)
    y = [[None] * nj for _ in range(npair)]
    for j in range(nj):
        both = {p: _rows_split(_nt(wr[(j, p)], s[p]), 2 * c) for p in range(npair)}
        u = {p: both[p][0] + u0[(j, p)] for p in range(npair)}
        for p in range(npair):
            y[p][j] = _unstack_heads(both[p][1] + _nn(a_rb[(j, p)], u[p]) + yv[(j, p)])
        s = [s[p] * gdec[(j, p)] + _tn(_rows_join(u[p], vs[(j, p)]), bk[(j, p)]) for p in range(npair)]
    return y, s


def _rwkv_blocks(ref, nj, c):
    return [[ref[j * c:(j + 1) * c, p * PAIR_W:(p + 1) * PAIR_W] for j in range(nj)] for p in range(HB_PAIRS)]


def _rwkv_fwd(seqs, comm=None):
    t = seqs[0].shape[0]
    c, nj = RWKV_CHUNK, RWKV_GROUP
    n = t // (c * nj)

    def body(r_ref, lw_ref, k_ref, v_ref, a_ref, b_ref, y_ref, hs_ref, st_ref):
        @pl.when(pl.program_id(0) == 0)
        def _():
            st_ref[...] = jnp.zeros_like(st_ref)

        hs_ref[0] = st_ref[...]
        s0 = [st_ref[p] for p in range(HB_PAIRS)]
        y, s1 = _rwkv_step(s0, *[_rwkv_blocks(ref, nj, c) for ref in (r_ref, lw_ref, k_ref, v_ref, a_ref, b_ref)])
        for p in range(HB_PAIRS):
            for j in range(nj):
                y_ref[j * c:(j + 1) * c, p * PAIR_W:(p + 1) * PAIR_W] = y[p][j]
            st_ref[p] = s1[p]

    seq = pl.BlockSpec((c * nj, W_B), lambda i: (i, 0))
    return _hosting_call(
        body, comm, name="rwkv_fwd", grid=(n,), in_specs=[seq] * 6,
        out_specs=[seq, pl.BlockSpec((1, HB_PAIRS, PAIR_W, PAIR_W), lambda i: (i, 0, 0, 0))],
        out_shape=[SDS((t, W_B), F32), SDS((n, HB_PAIRS, PAIR_W, PAIR_W), F32)],
        scratch_shapes=[pltpu.VMEM((HB_PAIRS, PAIR_W, PAIR_W), F32)], args=tuple(seqs))


def _rwkv_bwd(seqs, hs, dy, comm=None):
    t = seqs[0].shape[0]
    c, nj = RWKV_CHUNK, RWKV_GROUP
    n = t // (c * nj)

    def body(r_ref, lw_ref, k_ref, v_ref, a_ref, b_ref, hs_ref, dy_ref,
             dr_ref, dlw_ref, dk_ref, dv_ref, da_ref, db_ref, dst_ref):
        @pl.when(pl.program_id(0) == 0)
        def _():
            dst_ref[...] = jnp.zeros_like(dst_ref)

        s0 = [hs_ref[0, p] for p in range(HB_PAIRS)]
        seq_vals = [_rwkv_blocks(ref, nj, c) for ref in (r_ref, lw_ref, k_ref, v_ref, a_ref, b_ref)]
        _, vjp = jax.vjp(_rwkv_step, s0, *seq_vals)
        grads = vjp((_rwkv_blocks(dy_ref, nj, c), [dst_ref[p] for p in range(HB_PAIRS)]))
        for ref, gr in zip((dr_ref, dlw_ref, dk_ref, dv_ref, da_ref, db_ref), grads[1:]):
            for p in range(HB_PAIRS):
                for j in range(nj):
                    ref[j * c:(j + 1) * c, p * PAIR_W:(p + 1) * PAIR_W] = gr[p][j]
        m0, m1 = _head_lane_masks()
        rows0 = (lax.broadcasted_iota(jnp.int32, (PAIR_W, 1), 0) < HB_DIM).astype(F32)
        blocks = rows0 * m0 + (1.0 - rows0) * m1
        for p in range(HB_PAIRS):
            dst_ref[p] = grads[0][p] * blocks

    seq = pl.BlockSpec((c * nj, W_B), lambda i: (n - 1 - i, 0))
    return _hosting_call(
        body, comm, name="rwkv_bwd", grid=(n,),
        in_specs=[seq] * 6 + [pl.BlockSpec((1, HB_PAIRS, PAIR_W, PAIR_W), lambda i: (n - 1 - i, 0, 0, 0)), seq],
        out_specs=[seq] * 6, out_shape=[SDS((t, W_B), F32)] * 6,
        scratch_shapes=[pltpu.VMEM((HB_PAIRS, PAIR_W, PAIR_W), F32)], args=(*seqs, hs, dy))


def _final_loss(x3, fnorm, target, *, tm):
    t, d = x3.shape

    def body(x_ref, g_ref, t_ref, dx_ref, dg_ref, loss_ref):
        @pl.when(pl.program_id(0) == 0)
        def _():
            dg_ref[...] = jnp.zeros_like(dg_ref)
            loss_ref[...] = jnp.zeros_like(loss_ref)

        x, g = x_ref[...], g_ref[...]
        rinv = lax.rsqrt(jnp.mean(x * x, axis=-1, keepdims=True) + NORM_EPS)
        xh = x * rinv
        diff = xh * g - t_ref[...]
        loss_ref[...] += 0.5 * jnp.sum(jnp.mean(diff * diff, axis=-1, keepdims=True))
        dy = diff * (1.0 / d)
        dg_ref[...] += jnp.sum(dy * xh, axis=0, keepdims=True)
        dxh = dy * g
        dx_ref[...] = rinv * (dxh - xh * jnp.mean(dxh * xh, axis=-1, keepdims=True))

    row = pl.BlockSpec((tm, d), lambda i: (i, 0))
    return pl.pallas_call(
        body, name="final_loss", grid=(t // tm,), in_specs=[row, pl.BlockSpec((1, d), lambda i: (0, 0)), row],
        out_specs=[row, pl.BlockSpec((1, d), lambda i: (0, 0)), pl.BlockSpec((8, 128), lambda i: (0, 0))],
        out_shape=[SDS((t, d), F32), SDS((1, d), F32), SDS((8, 128), F32)],
        compiler_params=_params(("arbitrary",)))(x3, fnorm, target)


def _gate_up_act(h, wgt, wut, *, tm, tn, name, comm=None):
    t, d = h.shape
    tm = min(tm, t)

    def body(h_ref, g_ref, u_ref, a_out, u_out, act_out):
        hv = h_ref[...]
        a = _dg(hv, g_ref[...], 1, 1, False)
        u = _dg(hv, u_ref[...], 1, 1, False)
        a_out[...] = a.astype(a_out.dtype)
        u_out[...] = u.astype(u_out.dtype)
        act_out[...] = (_silu(a) * u).astype(act_out.dtype)

    wspec = pl.BlockSpec((tn, d), lambda i, j: (j, 0))
    ospec = pl.BlockSpec((tm, tn), lambda i, j: (i, j))
    return _hosting_call(
        body, comm, name=name, grid=(t // tm, D_FF // tn),
        in_specs=[pl.BlockSpec((tm, d), lambda i, j: (i, 0)), wspec, wspec], out_specs=[ospec, ospec, ospec],
        out_shape=[SDS((t, D_FF), BF16), SDS((t, D_FF), BF16), SDS((t, D_FF), BF16)], scratch_shapes=[],
        args=(h, wgt, wut))


def _dact_swiglu(dout, wd, a, u, *, tm, tn, name, comm=None):
    t, d = dout.shape
    tm = min(tm, t)

    def body(d_ref, w_ref, a_ref, u_ref, da_out, du_out):
        dact = 0.5 * _dg(d_ref[...], w_ref[...], 1, 1, False)
        av, uv = a_ref[...].astype(F32), u_ref[...].astype(F32)
        s = _sigmoid(av)
        da_out[...] = (dact * uv * (s * (1.0 + av * (1.0 - s)))).astype(da_out.dtype)
        du_out[...] = (dact * (av * s)).astype(du_out.dtype)

    tile = pl.BlockSpec((tm, tn), lambda i, j: (i, j))
    return _hosting_call(
        body, comm, name=name, grid=(t // tm, D_FF // tn),
        in_specs=[pl.BlockSpec((tm, d), lambda i, j: (i, 0)), pl.BlockSpec((tn, d), lambda i, j: (j, 0)), tile, tile],
        out_specs=[tile, tile], out_shape=[SDS((t, D_FF), BF16), SDS((t, D_FF), BF16)], scratch_shapes=[],
        args=(dout, wd, a, u))


class _Plan:
    def __init__(self):
        self.entries, self.counts = collections.defaultdict(list), {}

    def carry(self, host, comm_of, after):
        self.entries[host].append((comm_of, after))

    def comm(self, host, g):
        comms = [comm_of(g) for comm_of, _ in self.entries.get(host, [])]
        self.counts[host] = [len(c.arrays) for c in comms]
        return functools.reduce(_join_comms, comms) if comms else None

    def done(self, host, results, w):
        start = 0
        for (_, after), n in zip(self.entries.get(host, []), self.counts.get(host, [])):
            after(results[start:start + n], w)
            start += n


def _ffn_fwd(x, w, tag, plan, g):
    comm = plan.comm(f"{tag}_rms", g)
    res = _rowwise(_rms_f, [x], [w[f"{tag}_norm"]], [[0]], [BF16], tm=512, name=f"{tag}_rms", comm=comm)
    (h,), carried = res if comm is not None else (res, [])
    plan.done(f"{tag}_rms", carried, w)
    (a, u, act), carried = _gate_up_act(h, w[f"{tag}_wgt"], w[f"{tag}_wut"], tm=2048, tn=256, name=f"{tag}_gate_up",
                                        comm=plan.comm(f"{tag}_gate_up", g))
    plan.done(f"{tag}_gate_up", carried, w)
    comm = plan.comm(f"{tag}_down", g)
    out = _mm(act, w[f"{tag}_wd"], tm=1024, tn=D_MODEL, tk=D_FF, name=f"{tag}_down", res=x, scale=0.5, comm=comm)
    if comm is not None:
        out, carried = out
        plan.done(f"{tag}_down", carried, w)
    return out, (h, a, u, act)


def _ffn_bwd(dout, x, w, saved, tag, plan, g):
    h, a, u, act = saved

    def carrying(fn, host, *args, **kwargs):
        comm = plan.comm(host, g)
        res = fn(*args, name=host, comm=comm, **kwargs)
        out, carried = res if comm is not None else (res, [])
        plan.done(host, carried, w)
        return out

    (da, du), carried = _dact_swiglu(dout, w[f"{tag}_wd"], a, u, tm=2048, tn=256, name=f"{tag}_dact",
                                     comm=plan.comm(f"{tag}_dact", g))
    plan.done(f"{tag}_dact", carried, w)
    g[f"{tag}_wd"] = _mm(act, dout, ta=True, tm=D_FF // 2, tn=D_MODEL, tk=1024, name=f"{tag}_dwd", scale=0.5)
    g[f"{tag}_wgt"] = carrying(_mm, f"{tag}_dwg", da, h, ta=True, tm=D_FF // 2, tn=D_MODEL, tk=1024)
    g[f"{tag}_wut"] = carrying(_mm, f"{tag}_dwu", du, h, ta=True, tm=D_FF // 2, tn=D_MODEL, tk=1024)
    dh = carrying(_mm, f"{tag}_dh_g", da, w[f"{tag}_wgt"], tm=1024, tn=D_MODEL, tk=D_FF)
    dh = carrying(_mm, f"{tag}_dh_u", du, w[f"{tag}_wut"], tm=1024, tn=D_MODEL, tk=D_FF, res=dh)
    dx, g[f"{tag}_norm"] = _rowwise_bwd(_rms_f, [x], [w[f"{tag}_norm"]], [dh], x_grad=[True], p_grad=[True],
                                        dx_groups=[[0]], dx_dtypes=[F32], tm=512, name=f"{tag}_drms",
                                        extra={0: dout})
    return dx


def _local_step(x, target, w, plan=None):
    plan = plan or _Plan()
    ones_bd = jnp.kron(jnp.eye(HB_HEADS, dtype=F32), jnp.ones((HB_DIM, HB_DIM), F32))
    g = {}
    x1, ffn1_saved = _ffn_fwd(x, w, "ffn1", plan, g)
    hm, = _rowwise(_rms_f, [x1], [w["mix_norm"]], [[0]], [BF16], tm=512, name="mix_rms")
    p_h = _mm(hm, w["w_in_h"], tm=2048, tn=256, tk=D_MODEL, name="inproj_h")
    p_r = _mm(hm, w["w_in_r"], tm=2048, tn=256, tk=D_MODEL, name="inproj_r")
    o_a, hgrn_states = _hgrn_fwd(p_h, w["lb0"], w["lb1"], w["hgrn_out_norm"])

    mu = w["mu_pad"]
    prep_xs = [(p_r, W_B, 0), (p_r, W_B, 1), (p_r, W_B, 2), (p_r, LORA_PAD, 6),
               ("prev", p_r, W_B, 0), ("prev", p_r, W_B, 1), ("prev", p_r, W_B, 2), ("prev", p_r, LORA_PAD, 6)]
    prep_ps = [(mu, W_B, 0), (mu, W_B, 1), (mu, W_B, 2), (mu, LORA_PAD, 6), w["rwkv_w0"], w["w2_pad"], w["rwkv_a0"],
               w["a2_pad"], w["g2_pad"], w["rwkv_k_k"], w["rwkv_k_a"], ones_bd]
    prep_f = _rwkv_prep_f
    r, lw, k2, v, a_vec, b_vec, gate = _rowwise(prep_f, prep_xs, prep_ps, [[0], [1], [2], [3], [4], [5], [6]],
                                                [F32] * 7, tm=256, name="rwkv_prep")
    seqs = [r, lw, k2, v, a_vec, b_vec]
    (y, rwkv_states), carried = _rwkv_fwd(seqs, comm=plan.comm("rwkv_fwd", g))
    plan.done("rwkv_fwd", carried, w)
    post_f = _rwkv_post_f
    post_xs = [y, r, k2, v, gate]
    post_ps = [w["rwkv_r_k"], w["rwkv_gn_w"], w["rwkv_gn_b"], ones_bd]
    o_b, = _rowwise(post_f, post_xs, post_ps, [[0]], [F32], tm=256, name="rwkv_post")
    x2 = _mm(o_a, w["w_out_a"], tm=2048, tn=256, tk=W_A, name="outproj_a", res=x1)
    x2 = _mm(o_b, w["w_out_b"], tm=2048, tn=256, tk=W_B, name="outproj_b", res=x2)
    x3, ffn2_saved = _ffn_fwd(x2, w, "ffn2", plan, g)
    dx3, g["final_norm"], loss = _final_loss(x3, w["final_norm"], target, tm=256)

    dx2 = _ffn_bwd(dx3, x2, w, ffn2_saved, "ffn2", plan, g)
    do_a = _mm(dx2, w["w_out_a"], tb=True, tm=2048, tn=256, tk=D_MODEL, name="outproj_do_a")
    do_b = _mm(dx2, w["w_out_b"], tb=True, tm=2048, tn=256, tk=D_MODEL, name="outproj_do_b")
    g["w_out_a"] = _mm(o_a, dx2, ta=True, tm=W_A, tn=D_MODEL, tk=1024, name="outproj_dw_a")
    g["w_out_b"] = _mm(o_b, dx2, ta=True, tm=W_B, tn=D_MODEL, tk=1024, name="outproj_dw_b")

    (dp_h, g["lb0"], g["lb1"], g["hgrn_out_norm"]), carried = _hgrn_bwd(
        p_h, w["lb0"], w["lb1"], w["hgrn_out_norm"], hgrn_states, do_a, 0, comm=plan.comm("hgrn_bwd", g))
    plan.done("hgrn_bwd", carried, w)
    post_out = _rowwise_bwd(post_f, post_xs, post_ps, [do_b], x_grad=[True] * 5, p_grad=[True] * 3 + [False],
                            dx_groups=[[0], [1], [2], [3], [4]], dx_dtypes=[F32] * 5, tm=256, name="rwkv_post_bwd")
    dy, dr1, dk1, dv1, dgate, g["rwkv_r_k"], g["rwkv_gn_w"], g["rwkv_gn_b"] = post_out
    (dr2, dlw, dk2, dv2, da_vec, db_vec), carried = _rwkv_bwd(seqs, rwkv_states, dy, comm=plan.comm("rwkv_bwd", g))
    plan.done("rwkv_bwd", carried, w)

    def prep2_f(*vals):
        r_, lw_, k2_, v_, a_, b_, g_ = prep_f(*vals)
        return r_, lw_, k2_, v_, a_, b_, g_, r_, k2_, v_

    prep_out = _rowwise_bwd(prep2_f, prep_xs, prep_ps, [dr2, dlw, dk2, dv2, da_vec, db_vec, dgate, dr1, dk1, dv1],
                            x_grad=[True] * 8, p_grad=[True] * 11 + [False], dx_groups=[[0, 1, 2, 3], [4, 5, 6, 7]],
                            dx_dtypes=[F32, F32], tm=256, name="rwkv_prep_bwd")
    dpr_main, dpr_prev = prep_out[0], prep_out[1]
    (dmu_r, dmu_k, dmu_v, dmu_lo, g["rwkv_w0"], g["w2_pad"], g["rwkv_a0"], g["a2_pad"], g["g2_pad"],
     g["rwkv_k_k"], g["rwkv_k_a"]) = prep_out[2:]
    g["mu_pad"] = jnp.concatenate([dmu_r, dmu_k, dmu_v, dmu_lo], axis=1)
    dp_r, = _rowwise(lambda u_, s_: (u_ + s_,), [dpr_main, ("next", dpr_prev, N_RWKV_PAD, 0)], [], [[0]], [F32],
                     tm=512, name="rwkv_dp_sum")
    dhm = _mm(dp_h, w["w_in_h"], tb=True, tm=1024, tn=D_MODEL, tk=N_HGRN_COLS, name="inproj_dh_h")
    dhm = _mm(dp_r, w["w_in_r"], tb=True, tm=1024, tn=D_MODEL, tk=N_RWKV_PAD, name="inproj_dh_r", res=dhm)
    g["w_in_h"] = _mm(hm, dp_h, ta=True, tm=D_MODEL, tn=D_MODEL, tk=1024, name="inproj_dw_h")
    g["w_in_r"] = _mm(hm, dp_r, ta=True, tm=D_MODEL, tn=N_RWKV_PAD // 2, tk=1024, name="inproj_dw_r")
    mix_comm = plan.comm("mix_drms", g)
    mix_out = _rowwise_bwd(_rms_f, [x1], [w["mix_norm"]], [dhm], x_grad=[True], p_grad=[True], dx_groups=[[0]],
                           dx_dtypes=[F32], tm=512, name="mix_drms", extra={0: dx2}, comm=mix_comm)
    (dx1, g["mix_norm"]), carried = mix_out if mix_comm is not None else (mix_out, [])
    plan.done("mix_drms", carried, w)
    dx0 = _ffn_bwd(dx1, x, w, ffn1_saved, "ffn1", plan, g)
    return loss, dx0, g


HBM_SPEC = pl.BlockSpec(memory_space=pl.ANY)

Comm = collections.namedtuple("Comm", "arrays out_shapes aliased sem_shapes start finish")


def _join_comms(first, second):
    assert first.aliased == second.aliased
    n, s = len(first.arrays), len(first.sem_shapes)

    def start(ins, outs, sems):
        first.start(ins[:n], outs[:n], sems[:s])
        second.start(ins[n:], outs[n:], sems[s:])

    def finish(ins, outs, sems):
        first.finish(ins[:n], outs[:n], sems[:s])
        second.finish(ins[n:], outs[n:], sems[s:])

    return Comm(list(first.arrays) + list(second.arrays), list(first.out_shapes) + list(second.out_shapes),
                first.aliased, list(first.sem_shapes) + list(second.sem_shapes), start, finish)


def _run_comm(comm, name):
    n = len(comm.arrays)

    def body(*refs):
        ins, outs, sems = refs[:n], refs[n:2 * n], refs[2 * n:]
        comm.start(ins, outs, sems)
        comm.finish(ins, outs, sems)

    return pl.pallas_call(
        body, name=name, in_specs=[HBM_SPEC] * n, out_specs=[HBM_SPEC] * n, out_shape=list(comm.out_shapes),
        input_output_aliases={t: t for t in range(n)} if comm.aliased else {},
        scratch_shapes=list(comm.sem_shapes))(*comm.arrays)


def _hosting_call(body, comm, *, name, grid, in_specs, out_specs, out_shape, scratch_shapes, args):
    sem = ("arbitrary",) * len(grid)
    if comm is None:
        res = pl.pallas_call(body, name=name, grid=grid, in_specs=in_specs, out_specs=out_specs, out_shape=out_shape,
                             scratch_shapes=scratch_shapes, compiler_params=_params(sem))(*args)
        return list(res), []
    ni, no, ns, nc = len(in_specs), len(out_specs), len(scratch_shapes), len(comm.arrays)

    def wrapped(*refs):
        ins, cins = refs[:ni], refs[ni:ni + nc]
        outs, couts = refs[ni + nc:ni + nc + no], refs[ni + nc + no:ni + 2 * nc + no]
        scr, sems = refs[ni + 2 * nc + no:ni + 2 * nc + no + ns], refs[ni + 2 * nc + no + ns:]
        first = functools.reduce(jnp.logical_and, [pl.program_id(k) == 0 for k in range(len(grid))])
        last = functools.reduce(jnp.logical_and, [pl.program_id(k) == grid[k] - 1 for k in range(len(grid))])

        @pl.when(first)
        def _():
            comm.start(cins, couts, sems)

        body(*ins, *outs, *scr)

        @pl.when(last)
        def _():
            comm.finish(cins, couts, sems)

    res = pl.pallas_call(
        wrapped, name=name, grid=grid, in_specs=list(in_specs) + [HBM_SPEC] * nc,
        out_specs=list(out_specs) + [HBM_SPEC] * nc, out_shape=list(out_shape) + list(comm.out_shapes),
        scratch_shapes=list(scratch_shapes) + list(comm.sem_shapes),
        input_output_aliases={ni + t: no + t for t in range(nc)} if comm.aliased else {},
        compiler_params=_params(sem))(*args, *comm.arrays)
    return list(res[:no]), list(res[no:])


def _chips(x, y):
    return [(1 - x, y), (x, 1 - y), (1 - x, 1 - y)]


def _gather_comm(bufs):
    n = len(bufs)

    def copies(outs, sems):
        ici_send, ici_recv, d2d_send, d2d_recv = sems
        x, y, c = lax.axis_index("x"), lax.axis_index("y"), lax.axis_index("c")

        def half(t, slot, hc):
            hr = bufs[t].shape[1] // 2
            return outs[t].at[slot, pl.ds(pl.multiple_of(hc * hr, 16), hr), :]

        def ici(t, j, slot, px, py):
            return pltpu.make_async_remote_copy(src_ref=half(t, slot, c), dst_ref=half(t, slot, c),
                                                send_sem=ici_send.at[3 * t + j], recv_sem=ici_recv.at[3 * t + j],
                                                device_id=(px, py, c), device_id_type=MESH)

        def d2d(t, j, slot, hc):
            return pltpu.make_async_remote_copy(src_ref=half(t, slot, hc), dst_ref=half(t, slot, hc),
                                                send_sem=d2d_send.at[3 * t + j], recv_sem=d2d_recv.at[3 * t + j],
                                                device_id=(x, y, 1 - c), device_id_type=MESH)

        peers = [(t, j, px, py) for t in range(n) for j, (px, py) in enumerate(_chips(x, y))]
        return ici, d2d, peers, 2 * x + y, c

    def start(ins, outs, sems):
        ici, _, peers, me, _ = copies(outs, sems)
        for t, j, px, py in peers:
            ici(t, j, me, px, py).start()

    def finish(ins, outs, sems):
        ici, d2d, peers, me, c = copies(outs, sems)
        for t, j, px, py in peers:
            ici(t, j, 2 * px + py, px, py).wait_recv()
            d2d(t, j, 2 * px + py, c).start()
        for t, j, px, py in peers:
            d2d(t, j, 2 * px + py, 1 - c).wait_recv()
        for t, j, px, py in peers:
            ici(t, j, me, px, py).wait_send()
            d2d(t, j, 2 * px + py, c).wait_send()

    return Comm(list(bufs), [SDS(b.shape, b.dtype) for b in bufs], True, [pltpu.SemaphoreType.DMA((3 * n,))] * 4,
                start, finish)


def _sibling_exchange_comm(gs):
    n = len(gs)

    def copies(ins, outs, sems):
        x, y, c = lax.axis_index("x"), lax.axis_index("y"), lax.axis_index("c")
        cps = []
        for t in range(n):
            hr = gs[t].shape[1] // 2
            src = ins[t].at[:, pl.ds(pl.multiple_of((1 - c) * hr, SUBLANES), hr), :]
            cps.append(pltpu.make_async_remote_copy(src_ref=src, dst_ref=outs[t], send_sem=sems[0].at[t],
                                                    recv_sem=sems[1].at[t], device_id=(x, y, 1 - c),
                                                    device_id_type=MESH))
        return cps

    def start(ins, outs, sems):
        for cp in copies(ins, outs, sems):
            cp.start()

    def finish(ins, outs, sems):
        for cp in copies(ins, outs, sems):
            cp.wait()

    return Comm(list(gs), [SDS((N_CHIPS, g.shape[1] // 2, g.shape[2]), g.dtype) for g in gs], False,
                [pltpu.SemaphoreType.DMA((n,))] * 2, start, finish)


def _chip_exchange_comm(ss):
    n = len(ss)

    def copies(ins, outs, sems):
        x, y, c = lax.axis_index("x"), lax.axis_index("y"), lax.axis_index("c")
        me = 2 * x + y

        def copy(t, j, px, py, src_slot, dst_slot):
            return pltpu.make_async_remote_copy(src_ref=ins[t].at[src_slot], dst_ref=outs[t].at[dst_slot],
                                                send_sem=sems[0].at[3 * t + j], recv_sem=sems[1].at[3 * t + j],
                                                device_id=(px, py, c), device_id_type=MESH)

        peers = [(t, j, px, py) for t in range(n) for j, (px, py) in enumerate(_chips(x, y))]
        return copy, peers, me

    def start(ins, outs, sems):
        copy, peers, me = copies(ins, outs, sems)
        for t, j, px, py in peers:
            copy(t, j, px, py, 2 * px + py, me).start()

    def finish(ins, outs, sems):
        copy, peers, me = copies(ins, outs, sems)
        for t, j, px, py in peers:
            copy(t, j, px, py, me, 2 * px + py).wait_recv()
        for t, j, px, py in peers:
            copy(t, j, px, py, 2 * px + py, me).wait_send()

    return Comm(list(ss), [SDS(s.shape, s.dtype) for s in ss], False, [pltpu.SemaphoreType.DMA((3 * n,))] * 2,
                start, finish)


def _sibling_swap_comm(fs):
    n = len(fs)

    def copies(ins, outs, sems):
        x, y, c = lax.axis_index("x"), lax.axis_index("y"), lax.axis_index("c")
        return [pltpu.make_async_remote_copy(src_ref=ins[t], dst_ref=outs[t], send_sem=sems[0].at[t],
                                             recv_sem=sems[1].at[t], device_id=(x, y, 1 - c), device_id_type=MESH)
                for t in range(n)]

    def start(ins, outs, sems):
        for cp in copies(ins, outs, sems):
            cp.start()

    def finish(ins, outs, sems):
        for cp in copies(ins, outs, sems):
            cp.wait()

    return Comm(list(fs), [SDS(f.shape, f.dtype) for f in fs], False, [pltpu.SemaphoreType.DMA((n,))] * 2,
                start, finish)


def _row_tile(rows, cap=512):
    best = SUBLANES
    for tr in range(SUBLANES, min(rows, cap) + 1, SUBLANES):
        if rows % tr == 0:
            best = tr
    return best


def _add_halves(g4, r4, c_idx, name):
    _, hr, lanes = r4.shape
    tr = _row_tile(hr)
    nb = hr // tr

    def body(c_ref, a_ref, b_ref, o_ref):
        o_ref[...] = (a_ref[...] + b_ref[...]).astype(o_ref.dtype)

    grid_spec = pltpu.PrefetchScalarGridSpec(
        num_scalar_prefetch=1, grid=(N_CHIPS, nb),
        in_specs=[pl.BlockSpec((None, tr, lanes), lambda q, i, c_ref: (q, c_ref[0] * nb + i, 0)),
                  pl.BlockSpec((None, tr, lanes), lambda q, i, c_ref: (q, i, 0))],
        out_specs=pl.BlockSpec((None, tr, lanes), lambda q, i, c_ref: (q, i, 0)))
    return pl.pallas_call(body, name=name, grid_spec=grid_spec, out_shape=SDS(r4.shape, BF16),
                          compiler_params=_params(("parallel", "parallel")))(c_idx, g4, r4)


def _sum_chips(r4, s4, me_idx, name):
    _, rows, lanes = r4.shape
    tr = _row_tile(rows)

    def body(me_ref, a_ref, b_ref, c_ref, d_ref, own_ref, o_ref):
        own = own_ref[...].astype(F32)
        p = [jnp.where(me_ref[0] == q, own, ref[...].astype(F32)) for q, ref in enumerate((a_ref, b_ref, c_ref, d_ref))]
        o_ref[...] = ((p[0] + p[1]) + p[2]) + p[3]

    other = lambda q: (lambda i, me_ref: (jnp.where(me_ref[0] == q, (q + 1) % N_CHIPS, q), i, 0))
    grid_spec = pltpu.PrefetchScalarGridSpec(
        num_scalar_prefetch=1, grid=(rows // tr,),
        in_specs=[pl.BlockSpec((None, tr, lanes), other(q)) for q in range(N_CHIPS)]
        + [pl.BlockSpec((None, tr, lanes), lambda i, me_ref: (me_ref[0], i, 0))],
        out_specs=pl.BlockSpec((tr, lanes), lambda i, me_ref: (i, 0)))
    return pl.pallas_call(body, name=name, grid_spec=grid_spec, out_shape=SDS((rows, lanes), F32),
                          compiler_params=_params(("parallel",)))(me_idx, r4, r4, r4, r4, s4)


def _adamw(wf, g_own, g_other, mf, vf, c_idx, name):
    rows, lanes = wf.shape
    hr = rows // 2
    tr = _row_tile(hr)
    nb = hr // tr
    c1 = 1.0 / (1.0 - ADAM_B1 ** ADAM_STEP)
    c2 = 1.0 / (1.0 - ADAM_B2 ** ADAM_STEP)

    def body(c_ref, w_ref, go_ref, gx_ref, m_ref, v_ref, g_ref, d_ref, nm_ref, nv_ref):
        gv = jnp.where(pl.program_id(0) == c_ref[0], go_ref[...], gx_ref[...])
        m = ADAM_B1 * m_ref[...] + (1.0 - ADAM_B1) * gv
        v = ADAM_B2 * v_ref[...] + (1.0 - ADAM_B2) * (gv * gv)
        g_ref[...] = gv
        d_ref[...] = -ADAM_LR * ((m * c1) / (jnp.sqrt(v * c2) + ADAM_EPS) + ADAM_WD * w_ref[...])
        nm_ref[...] = m
        nv_ref[...] = v

    full = pl.BlockSpec((tr, lanes), lambda h, i, c_ref: (h * nb + i, 0))
    half = pl.BlockSpec((tr, lanes), lambda h, i, c_ref: (i, 0))
    grid_spec = pltpu.PrefetchScalarGridSpec(num_scalar_prefetch=1, grid=(2, nb),
                                             in_specs=[full, half, half, full, full], out_specs=[full] * 4)
    return pl.pallas_call(body, name=name, grid_spec=grid_spec, out_shape=[SDS((rows, lanes), F32)] * 4,
                          compiler_params=_params(("parallel", "parallel")))(c_idx, wf, g_own, g_other, mf, vf)


def _adamw_whole(wf, gf, mf, vf, name):
    rows, lanes = wf.shape
    tr = _row_tile(rows)
    c1 = 1.0 / (1.0 - ADAM_B1 ** ADAM_STEP)
    c2 = 1.0 / (1.0 - ADAM_B2 ** ADAM_STEP)

    def body(w_ref, g_ref, m_ref, v_ref, d_ref, nm_ref, nv_ref):
        gv = g_ref[...]
        m = ADAM_B1 * m_ref[...] + (1.0 - ADAM_B1) * gv
        v = ADAM_B2 * v_ref[...] + (1.0 - ADAM_B2) * (gv * gv)
        d_ref[...] = -ADAM_LR * ((m * c1) / (jnp.sqrt(v * c2) + ADAM_EPS) + ADAM_WD * w_ref[...])
        nm_ref[...] = m
        nv_ref[...] = v

    spec = pl.BlockSpec((tr, lanes), lambda i: (i, 0))
    return pl.pallas_call(body, name=name, grid=(rows // tr,), in_specs=[spec] * 4, out_specs=[spec] * 3,
                          out_shape=[SDS((rows, lanes), F32)] * 3, compiler_params=_params(("parallel",)))(wf, gf, mf, vf)


BIG = ("ffn1_w_gate", "ffn1_w_up", "ffn1_w_down", "ffn2_w_gate", "ffn2_w_up", "ffn2_w_down", "w_out", "w_in")
TRANSPOSED = ("ffn1_w_gate", "ffn1_w_up", "ffn2_w_gate", "ffn2_w_up")
PACKED = ("rwkv_w2", "rwkv_a2", "rwkv_g2")
SMALL_SHAPES = {"ffn1_norm": (1, D_MODEL), "mix_norm": (1, D_MODEL), "hgrn_lb_logits": (2, W_A),
                "hgrn_out_norm": (1, W_A), "rwkv_shift_mu": (1, N_RWKV_COLS), "rwkv_w0": (1, W_B),
                "rwkv_a0": (1, W_B), "rwkv_k_k": (1, W_B), "rwkv_k_a": (1, W_B),
                "rwkv_r_k": (1, HB_HEADS, HB_DIM), "rwkv_gn_w": (1, W_B), "rwkv_gn_b": (1, W_B),
                "ffn2_norm": (1, D_MODEL), "final_norm": (D_MODEL,)}
PACK_ELEMS = sum(_numel(_shard_shape(n)) for n in PACKED) + sum(_numel(SMALL_SHAPES[n]) for n in SMALL)
PACK_ROWS = -(-PACK_ELEMS // (32 * LANES)) * 32


def _to_rows(name, shard):
    return shard[0].T if name in TRANSPOSED else shard[0]


def _pack(sharded, small):
    flat = jnp.concatenate([sharded[n].reshape(-1) for n in PACKED] + [small[n].reshape(-1) for n in SMALL])
    return jnp.pad(flat, (0, PACK_ROWS * LANES - flat.shape[0])).reshape(PACK_ROWS, LANES)


def _unpack(packed):
    flat, out, off = packed.reshape(-1), {}, 0
    for n in PACKED:
        shp = _shard_shape(n)
        out[n] = flat[off:off + _numel(shp)].reshape((1,) + shp)
        off += _numel(shp)
    for n in SMALL:
        shp = SMALL_SHAPES[n]
        out[n] = flat[off:off + _numel(shp)].reshape(shp)
        off += _numel(shp)
    return out


def _quarter(full, name, q):
    shape, ax = SHARDED_SHAPES[name]
    w = shape[ax] // N_CHIPS
    return lax.slice_in_dim(full, q * w, (q + 1) * w, axis=ax)


def kernel(x, ffn1_norm, ffn1_w_gate, ffn1_w_up, ffn1_w_down, mix_norm, w_in, hgrn_lb_logits, hgrn_out_norm, rwkv_shift_mu, rwkv_w0, rwkv_w2, rwkv_a0, rwkv_a2, rwkv_g2, rwkv_k_k, rwkv_k_a, rwkv_r_k, rwkv_gn_w, rwkv_gn_b, w_out, ffn2_norm, ffn2_w_gate, ffn2_w_up, ffn2_w_down, final_norm, loss_target, m_ffn1_norm, m_ffn1_w_gate, m_ffn1_w_up, m_ffn1_w_down, m_mix_norm, m_w_in, m_hgrn_lb_logits, m_hgrn_out_norm, m_rwkv_shift_mu, m_rwkv_w0, m_rwkv_w2, m_rwkv_a0, m_rwkv_a2, m_rwkv_g2, m_rwkv_k_k, m_rwkv_k_a, m_rwkv_r_k, m_rwkv_gn_w, m_rwkv_gn_b, m_w_out, m_ffn2_norm, m_ffn2_w_gate, m_ffn2_w_up, m_ffn2_w_down, m_final_norm, v_ffn1_norm, v_ffn1_w_gate, v_ffn1_w_up, v_ffn1_w_down, v_mix_norm, v_w_in, v_hgrn_lb_logits, v_hgrn_out_norm, v_rwkv_shift_mu, v_rwkv_w0, v_rwkv_w2, v_rwkv_a0, v_rwkv_a2, v_rwkv_g2, v_rwkv_k_k, v_rwkv_k_a, v_rwkv_r_k, v_rwkv_gn_w, v_rwkv_gn_b, v_w_out, v_ffn2_norm, v_ffn2_w_gate, v_ffn2_w_up, v_ffn2_w_down, v_final_norm):
    args = dict(locals())
    wts = {n: args[n] for n in ALL_WEIGHTS}
    moms = {n: args["m_" + n] for n in ALL_WEIGHTS}
    vars_ = {n: args["v_" + n] for n in ALL_WEIGHTS}

    me = 2 * lax.axis_index("x") + lax.axis_index("y")
    c_idx = lax.axis_index("c").astype(jnp.int32).reshape(1)
    me_idx = me.astype(jnp.int32).reshape(1)
    shard_of = {n: _to_rows(n, wts[n]).astype(BF16) for n in BIG}
    shard_of["packed"] = _pack(wts, {n: wts[n] for n in SMALL}).astype(BF16)
    group = {"ffn1": BIG[0:3], "ffn2": BIG[3:6]}

    def slot_bufs(names):
        return [lax.dynamic_update_slice(jnp.zeros((N_CHIPS,) + shard_of[n].shape, BF16), shard_of[n][None],
                                         (me, 0, 0)) for n in names]

    def ffn_weights(tag, gathered):
        return {f"{tag}_wgt": gathered[0].reshape(D_FF, D_MODEL), f"{tag}_wut": gathered[1].reshape(D_FF, D_MODEL),
                f"{tag}_wd": gathered[2].reshape(D_FF, D_MODEL)}

    def w_in_weights(gathered):
        w_in_full = jnp.concatenate([gathered[0][q] for q in range(N_CHIPS)], axis=1)
        return {"w_in_h": w_in_full[:, :N_HGRN_COLS],
                "w_in_r": jnp.pad(w_in_full[:, N_HGRN_COLS:], ((0, 0), (0, N_RWKV_PAD - N_RWKV_COLS)))}

    def mixer_weights(gathered):
        w_out_full = gathered[0].reshape(D_MODEL, D_MODEL)
        packs = gathered[1].reshape(N_CHIPS, PACK_ROWS * LANES)
        full, off = {}, 0
        for n in PACKED:
            shp = _shard_shape(n)
            full[n] = jnp.concatenate([packs[q, off:off + _numel(shp)].reshape(shp) for q in range(N_CHIPS)], axis=1)
            off += _numel(shp)
        zrow = lambda nrow: jnp.zeros((nrow, W_B), BF16)
        return {"w_out_a": w_out_full[:W_A], "w_out_b": w_out_full[W_A:],
                "w2_pad": jnp.concatenate([full["rwkv_w2"], zrow(LORA_PAD - 32)], axis=0),
                "a2_pad": jnp.concatenate([zrow(32), full["rwkv_a2"], zrow(LORA_PAD - 64)], axis=0),
                "g2_pad": jnp.concatenate([zrow(64), full["rwkv_g2"], zrow(LORA_PAD - 160)], axis=0)}

    plan = _Plan()
    w = {}
    plan.carry("ffn1_rms", lambda g: _gather_comm(slot_bufs(group["ffn1"][:2])),
               lambda res, w_: w_.update({"ffn1_wgt": res[0].reshape(D_FF, D_MODEL),
                                          "ffn1_wut": res[1].reshape(D_FF, D_MODEL)}))

    def after_gate_up(res, w_):
        w_["ffn1_wd"] = res[0].reshape(D_FF, D_MODEL)
        w_.update(w_in_weights(res[1:]))

    plan.carry("ffn1_gate_up", lambda g: _gather_comm(slot_bufs(("ffn1_w_down", "w_in"))), after_gate_up)
    plan.carry("ffn1_down", lambda g: _gather_comm(slot_bufs(("w_out", "packed"))),
               lambda res, w_: w_.update(mixer_weights(res)))
    plan.carry("rwkv_fwd", lambda g: _gather_comm(slot_bufs(group["ffn2"])),
               lambda res, w_: w_.update(ffn_weights("ffn2", res)))
    w["ffn1_norm"], w["ffn2_norm"] = ffn1_norm, ffn2_norm
    w["mix_norm"] = mix_norm
    w["lb0"], w["lb1"] = hgrn_lb_logits[0:1], hgrn_lb_logits[1:2]
    w["hgrn_out_norm"] = hgrn_out_norm
    w["mu_pad"] = jnp.pad(rwkv_shift_mu, ((0, 0), (0, N_RWKV_PAD - N_RWKV_COLS)))
    for n in ("rwkv_w0", "rwkv_a0", "rwkv_k_k", "rwkv_k_a", "rwkv_gn_w", "rwkv_gn_b"):
        w[n] = wts[n]
    w["rwkv_r_k"] = rwkv_r_k.reshape(1, W_B)
    w["final_norm"] = final_norm.reshape(1, D_MODEL)

    def reduce_rows(names, gs):
        r1 = _run_comm(_sibling_exchange_comm(gs), "grad_sibling_exchange")
        s4 = [_add_halves(gt, rt, c_idx, f"grad_add_halves_{n}") for gt, rt, n in zip(gs, r1, names)]
        r2 = _run_comm(_chip_exchange_comm(s4), "grad_chip_exchange")
        return [_sum_chips(rt, st, me_idx, f"grad_sum_chips_{n}") for rt, st, n in zip(r2, s4, names)]

    early = {}

    def reduce_early(names, grads_of, sibling_host, chips_host):
        def sibling_comm(g):
            early[names, "gs"] = grads_of(g)
            return _sibling_exchange_comm(early[names, "gs"])

        def after_sibling(res, w_):
            early[names, "s4"] = [_add_halves(gt, rt, c_idx, f"grad_add_halves_{n}")
                                  for gt, rt, n in zip(early[names, "gs"], res, names)]

        def after_chips(res, w_):
            early.update(zip(names, [_sum_chips(rt, st, me_idx, f"grad_sum_chips_{n}")
                                     for rt, st, n in zip(res, early[names, "s4"], names)]))

        plan.carry(sibling_host, sibling_comm, after_sibling)
        plan.carry(chips_host, lambda g: _chip_exchange_comm(early[names, "s4"]), after_chips)

    def proj_grads(g):
        g_w_in = jnp.concatenate([g["w_in_h"], g["w_in_r"][:, :N_RWKV_COLS]], axis=1)
        return [jnp.concatenate([g["w_out_a"], g["w_out_b"]], axis=0).reshape(N_CHIPS, -1, D_MODEL),
                jnp.stack([_quarter(g_w_in, "w_in", q) for q in range(N_CHIPS)])]

    rows_of = lambda keys: (lambda g: [g[k].reshape(N_CHIPS, -1, D_MODEL) for k in keys])
    reduce_early(group["ffn2"], rows_of(("ffn2_wgt", "ffn2_wut", "ffn2_wd")), "hgrn_bwd", "rwkv_bwd")
    reduce_early(("w_out", "w_in"), proj_grads, "mix_drms", "ffn1_dact")
    reduce_early(("ffn1_w_down",), rows_of(("ffn1_wd",)), "ffn1_dwg", "ffn1_dwu")
    reduce_early(("ffn1_w_gate",), rows_of(("ffn1_wgt",)), "ffn1_dwu", "ffn1_dh_g")
    reduce_early(("ffn1_w_up",), rows_of(("ffn1_wut",)), "ffn1_dh_g", "ffn1_dh_u")
    loss_slab, grad_x, g = _local_step(x[0], loss_target[0], w, plan)
    loss = lax.psum(loss_slab[0, 0], ("x", "y", "c"))

    gfull = {
        "rwkv_w2": g["w2_pad"][0:32], "rwkv_a2": g["a2_pad"][32:64], "rwkv_g2": g["g2_pad"][64:160],
    }
    gsmall = {
        "ffn1_norm": g["ffn1_norm"], "mix_norm": g["mix_norm"],
        "hgrn_lb_logits": jnp.concatenate([g["lb0"], g["lb1"]], axis=0), "hgrn_out_norm": g["hgrn_out_norm"],
        "rwkv_shift_mu": g["mu_pad"][:, :N_RWKV_COLS], "rwkv_w0": g["rwkv_w0"], "rwkv_a0": g["rwkv_a0"],
        "rwkv_k_k": g["rwkv_k_k"], "rwkv_k_a": g["rwkv_k_a"], "rwkv_r_k": g["rwkv_r_k"],
        "rwkv_gn_w": g["rwkv_gn_w"], "rwkv_gn_b": g["rwkv_gn_b"], "ffn2_norm": g["ffn2_norm"],
        "final_norm": g["final_norm"],
    }
    packed = jnp.stack([_pack({n: _quarter(gfull[n], n, q) for n in PACKED}, gsmall) for q in range(N_CHIPS)])
    early["packed"], = reduce_rows(["packed"], [packed])
    names = list(BIG) + ["packed"]
    own = [early[n] for n in names]
    other = _run_comm(_sibling_swap_comm(own), "grad_sibling_swap")

    core = lax.axis_index("c")
    results = [{}, {}, {}, {}]
    for n, go, gx in zip(names, own, other):
        if n in TRANSPOSED:
            gf = jnp.where(core == 0, jnp.concatenate([go.T, gx.T], axis=1), jnp.concatenate([gx.T, go.T], axis=1))
            outs = [gf, *_adamw_whole(wts[n][0], gf, moms[n][0], vars_[n][0], f"adamw_{n}")]
            for res, z in zip(results, outs):
                res[n] = z[None]
        elif n == "packed":
            pack_of = lambda d: _pack(d, {k: d[k] for k in SMALL})
            outs = _adamw(pack_of(wts), go, gx, pack_of(moms), pack_of(vars_), c_idx, "adamw_packed")
            for res, z in zip(results, outs):
                res.update(_unpack(z))
        else:
            outs = _adamw(wts[n][0], go, gx, moms[n][0], vars_[n][0], c_idx, f"adamw_{n}")
            for res, z in zip(results, outs):
                res[n] = z[None]
    return (loss, grad_x[None], *[r[n] for r in results for n in ALL_WEIGHTS])
```

```python
import collections
import functools

import jax
import jax.numpy as jnp
from jax import lax
from jax.experimental import pallas as pl
from jax.experimental.pallas import tpu as pltpu

F32 = jnp.float32
BF16 = jnp.bfloat16
SDS = jax.ShapeDtypeStruct
MESH = pl.DeviceIdType.MESH

D_MODEL = 1024
D_FF = 2816
W_A = 512
W_B = 512
HA_HEADS, HA_DIM = 4, 128
HB_HEADS, HB_DIM = 8, 64
HGRN_CHUNK = 64
HGRN_GROUP = 2
RWKV_CHUNK = 16
RWKV_GROUP = 4
N_HGRN_COLS = 4 * W_A
N_RWKV_COLS = 3 * W_B + 32 + 32 + 96
N_RWKV_PAD = 1792
LORA_PAD = 256
NORM_EPS = 1e-6
RWKV_GN_EPS = 64e-5
L2_EPS = 1e-12
ADAM_LR, ADAM_B1, ADAM_B2, ADAM_EPS, ADAM_WD, ADAM_STEP = 0.001, 0.9, 0.999, 1e-8, 0.01, 10

N_CHIPS = 4
VMEM_LIMIT_V7X = 56 * 1024 * 1024
LANES = 1024

SHARDED_SHAPES = {
    "ffn1_w_gate": ((D_MODEL, D_FF), 1), "ffn1_w_up": ((D_MODEL, D_FF), 1), "ffn1_w_down": ((D_FF, D_MODEL), 0),
    "w_in": ((D_MODEL, N_HGRN_COLS + N_RWKV_COLS), 1), "rwkv_w2": ((32, W_B), 1), "rwkv_a2": ((32, W_B), 1),
    "rwkv_g2": ((96, W_B), 1), "w_out": ((D_MODEL, D_MODEL), 0),
    "ffn2_w_gate": ((D_MODEL, D_FF), 1), "ffn2_w_up": ((D_MODEL, D_FF), 1), "ffn2_w_down": ((D_FF, D_MODEL), 0),
}
SMALL = ("ffn1_norm", "mix_norm", "hgrn_lb_logits", "hgrn_out_norm", "rwkv_shift_mu", "rwkv_w0", "rwkv_a0",
         "rwkv_k_k", "rwkv_k_a", "rwkv_r_k", "rwkv_gn_w", "rwkv_gn_b", "ffn2_norm", "final_norm")
ALL_WEIGHTS = ("ffn1_norm", "ffn1_w_gate", "ffn1_w_up", "ffn1_w_down", "mix_norm", "w_in", "hgrn_lb_logits",
               "hgrn_out_norm", "rwkv_shift_mu", "rwkv_w0", "rwkv_w2", "rwkv_a0", "rwkv_a2", "rwkv_g2", "rwkv_k_k",
               "rwkv_k_a", "rwkv_r_k", "rwkv_gn_w", "rwkv_gn_b", "w_out", "ffn2_norm", "ffn2_w_gate", "ffn2_w_up",
               "ffn2_w_down", "final_norm")


def _shard_shape(name):
    shape, ax = SHARDED_SHAPES[name]
    return tuple(s // N_CHIPS if i == ax else s for i, s in enumerate(shape))


def _numel(shape):
    n = 1
    for s in shape:
        n *= s
    return n


def _params(sem=None):
    return pltpu.CompilerParams(dimension_semantics=sem, vmem_limit_bytes=VMEM_LIMIT_V7X)


def _split2(x):
    hi = x.astype(BF16)
    return hi, (x.astype(F32) - hi.astype(F32)).astype(BF16)


def _dg(x, y, cx, cy, hi):
    dn = (((cx,), (cy,)), ((), ()))
    dot = lambda p, q: lax.dot_general(p, q, dn, preferred_element_type=F32)
    if hi == "x3":
        (xh, xl), (yh, yl) = _split2(x), _split2(y)
        return dot(xh, yh) + (dot(xh, yl) + dot(xl, yh))
    return dot(x.astype(BF16), y.astype(BF16))


def _make_mm(hi, cotangent_forms=None):
    @jax.custom_vjp
    def nn(x, y):
        return _dg(x, y, 1, 0, hi)

    @jax.custom_vjp
    def nt(x, y):
        return _dg(x, y, 1, 1, hi)

    @jax.custom_vjp
    def tn(x, y):
        return _dg(x, y, 0, 0, hi)

    bnn, bnt, btn = cotangent_forms or (nn, nt, tn)
    nn.defvjp(lambda x, y: (nn(x, y), (x, y)), lambda r, g: (bnt(g, r[1]), btn(r[0], g)))
    nt.defvjp(lambda x, y: (nt(x, y), (x, y)), lambda r, g: (bnn(g, r[1]), btn(g, r[0])))
    tn.defvjp(lambda x, y: (tn(x, y), (x, y)), lambda r, g: (bnt(r[1], g), bnn(r[0], g)))
    return nn, nt, tn


_nn, _nt, _tn = _make_mm(False)
_nn_x3, _nt_x3, _tn_x3 = _make_mm("x3", (_nn, _nt, _tn))


def _tri_apply(x, transpose):
    c = x.shape[0]
    tri = (lax.broadcasted_iota(jnp.int32, (c, c), 1) <= lax.broadcasted_iota(jnp.int32, (c, c), 0)).astype(BF16)
    dn = (((0 if transpose else 1,), (0,)), ((), ()))
    p1, p2 = _split2(x)
    dot = lambda p: lax.dot_general(tri, p, dn, preferred_element_type=F32)
    return dot(p1) + dot(p2)


@jax.custom_vjp
def _cumsum_rows(x):
    return _tri_apply(x, False)


_cumsum_rows.defvjp(lambda x: (_tri_apply(x, False), None), lambda _, g: (_tri_apply(g, True),))


def _sigmoid(x):
    return 1.0 / (1.0 + jnp.exp(-x))


def _silu(x):
    return x * _sigmoid(x)


def _softplus(z):
    return jnp.maximum(z, 0.0) + jnp.log(1.0 + jnp.exp(-jnp.abs(z)))


def _mm(a, b, *, ta=False, tb=False, tm, tn, tk, name, out_dtype=F32, res=None, scale=None, comm=None):
    m = a.shape[1] if ta else a.shape[0]
    kdim = a.shape[0] if ta else a.shape[1]
    n = b.shape[0] if tb else b.shape[1]
    assert (b.shape[1] if tb else b.shape[0]) == kdim
    tm, tn, tk = min(tm, m), min(tn, n), min(tk, kdim)
    assert m % tm == 0 and n % tn == 0 and kdim % tk == 0, (name, m, n, kdim)
    nk = kdim // tk
    a_spec = pl.BlockSpec((tk, tm), lambda i, j, k: (k, i)) if ta else pl.BlockSpec((tm, tk), lambda i, j, k: (i, k))
    b_spec = pl.BlockSpec((tn, tk), lambda i, j, k: (j, k)) if tb else pl.BlockSpec((tk, tn), lambda i, j, k: (k, j))
    o_spec = pl.BlockSpec((tm, tn), lambda i, j, k: (i, j))
    ca, cb = (0 if ta else 1), (1 if tb else 0)

    def body(*refs):
        if res is not None:
            a_ref, b_ref, r_ref, o_ref, acc_ref = refs
        else:
            a_ref, b_ref, o_ref, acc_ref = refs
        k = pl.program_id(2)

        @pl.when(k == 0)
        def _():
            acc_ref[...] = jnp.zeros_like(acc_ref)

        acc_ref[...] += _dg(a_ref[...], b_ref[...], ca, cb, False)

        @pl.when(k == nk - 1)
        def _():
            acc = acc_ref[...]
            if scale is not None:
                acc = acc * scale
            if res is not None:
                acc = r_ref[...] + acc
            o_ref[...] = acc.astype(out_dtype)

    in_specs = [a_spec, b_spec] + ([o_spec] if res is not None else [])
    args = (a, b) + ((res,) if res is not None else ())
    if comm is None:
        return pl.pallas_call(
            body, name=name, grid=(m // tm, n // tn, nk), in_specs=in_specs, out_specs=o_spec,
            out_shape=SDS((m, n), out_dtype), scratch_shapes=[pltpu.VMEM((tm, tn), F32)],
            compiler_params=_params(("parallel", "parallel", "arbitrary")))(*args)
    (out,), carried = _hosting_call(
        body, comm, name=name, grid=(m // tm, n // tn, nk), in_specs=in_specs, out_specs=[o_spec],
        out_shape=[SDS((m, n), out_dtype)], scratch_shapes=[pltpu.VMEM((tm, tn), F32)], args=args)
    return out, carried


def _row_spec(x, tm):
    if isinstance(x, tuple):
        arr, w, j = x
        return arr, pl.BlockSpec((tm, w), lambda i, j=j: (i, j))
    return x, pl.BlockSpec((tm, x.shape[1]), lambda i: (i, 0))


def _par_spec(p):
    if isinstance(p, tuple):
        arr, w, j = p
        return arr, pl.BlockSpec((arr.shape[0], w), lambda i, j=j: (0, j))
    return p, pl.BlockSpec(p.shape, lambda i: (0, 0))


def _store_groups(refs, groups, vals):
    for ref, idxs in zip(refs, groups):
        off = 0
        for ix in idxs:
            v = vals[ix]
            ref[:, off:off + v.shape[1]] = v.astype(ref.dtype)
            off += v.shape[1]


SUBLANES = 8


def _x_plan(xs, tm, t):
    arrays, specs, plan = [], [], []
    nb = tm // SUBLANES
    for x in xs:
        if isinstance(x, tuple) and isinstance(x[0], str):
            kind, arr, w, j = x
            if kind == "prev":
                halo = lambda i, j=j: (jnp.maximum(i * nb - 1, 0), j)
            else:
                halo = lambda i, j=j: (jnp.minimum((i + 1) * nb, t // SUBLANES - 1), j)
            arrays += [arr, arr]
            specs += [pl.BlockSpec((tm, w), lambda i, j=j: (i, j)), pl.BlockSpec((SUBLANES, w), halo)]
            plan.append((kind, 2, w))
        else:
            arr, spec = _row_spec(x, tm)
            arrays.append(arr)
            specs.append(spec)
            plan.append(("plain", 1, spec.block_shape[1]))
    return arrays, specs, plan


def _x_vals(refs, plan, tm, nt):
    vals, k = [], 0
    i = pl.program_id(0)
    rows = lax.broadcasted_iota(jnp.int32, (tm, 1), 0)
    for kind, n, _ in plan:
        main = refs[k][...].astype(F32)
        if kind == "prev":
            edge = jnp.where(i == 0, 0.0, refs[k + 1][SUBLANES - 1:SUBLANES, :].astype(F32))
            main = jnp.where(rows == 0, edge, pltpu.roll(main, 1, 0))
        elif kind == "next":
            edge = jnp.where(i == nt - 1, 0.0, refs[k + 1][0:1, :].astype(F32))
            main = jnp.where(rows == tm - 1, edge, pltpu.roll(main, tm - 1, 0))
        vals.append(main)
        k += n
    return vals


def _tile_rows(xs, tm):
    arr = xs[0]
    if isinstance(arr, tuple):
        arr = arr[1] if isinstance(arr[0], str) else arr[0]
    return min(tm, arr.shape[0]), arr.shape[0]


def _rowwise(f, xs, params, out_groups, out_dtypes, *, tm, name, comm=None):
    tm, t = _tile_rows(xs, tm)
    nt = t // tm
    xa, xspecs, plan = _x_plan(xs, tm, t)
    pa, pspecs = (zip(*[_par_spec(p) for p in params]) if params else ((), ()))
    nxr, npar = len(xa), len(pa)
    x_sds = [SDS((tm, w), F32) for _, _, w in plan]
    p_sds = [SDS(s.block_shape, F32) for s in pspecs]
    outs_sds = jax.eval_shape(lambda *vals: f(*vals), *x_sds, *p_sds)
    widths = [sum(outs_sds[ix].shape[1] for ix in idxs) for idxs in out_groups]

    def body(*refs):
        vals = _x_vals(refs[:nxr], plan, tm, nt) + [r[...].astype(F32) for r in refs[nxr:nxr + npar]]
        outs = f(*vals)
        _store_groups(refs[nxr + npar:], out_groups, outs)

    res, carried = _hosting_call(
        body, comm, name=name, grid=(nt,), in_specs=list(xspecs) + list(pspecs),
        out_specs=[pl.BlockSpec((tm, w), lambda i: (i, 0)) for w in widths],
        out_shape=[SDS((t, w), dt) for w, dt in zip(widths, out_dtypes)], scratch_shapes=[], args=(*xa, *pa))
    return res if comm is None else (res, carried)


def _rowwise_bwd(f, xs, params, cots, *, x_grad, p_grad, dx_groups, dx_dtypes, tm, name, extra=None, comm=None):
    tm, t = _tile_rows(xs, tm)
    nt = t // tm
    xa, xspecs, plan = _x_plan(xs, tm, t)
    pa, pspecs = (zip(*[_par_spec(p) for p in params]) if params else ((), ()))
    ca, cspecs = zip(*[_row_spec(c, tm) for c in cots])
    extra = extra or {}
    ekeys = sorted(extra)
    ea, especs = (zip(*[_row_spec(extra[k], tm) for k in ekeys]) if ekeys else ((), ()))
    nx, nxr, npar, nc, ne = len(plan), len(xa), len(pa), len(ca), len(ea)
    gx = [i for i in range(nx) if x_grad[i]]
    gp = [i for i in range(npar) if p_grad[i]]
    widths = [sum(plan[gx[ix]][2] for ix in idxs) for idxs in dx_groups]
    ng = len(dx_groups)

    def body(*refs):
        ins = refs[:nxr + npar + nc + ne]
        outs = refs[nxr + npar + nc + ne:]
        vals = _x_vals(ins[:nxr], plan, tm, nt) + [r[...].astype(F32) for r in ins[nxr:nxr + npar]]
        cvals = tuple(r[...].astype(F32) for r in ins[nxr + npar:nxr + npar + nc])
        evals = [r[...].astype(F32) for r in ins[nxr + npar + nc:]]
        diff_idx = gx + [nx + i for i in gp]

        def g(*dargs):
            full = list(vals)
            for ix, v in zip(diff_idx, dargs):
                full[ix] = v
            return tuple(f(*full))

        _, vjp = jax.vjp(g, *[vals[ix] for ix in diff_idx])
        grads = vjp(cvals)
        dxs = list(grads[:len(gx)])
        for k, ev in zip(ekeys, evals):
            dxs[k] = dxs[k] + ev
        _store_groups(outs[:ng], dx_groups, dxs)
        i = pl.program_id(0)
        for ref, gval in zip(outs[ng:], grads[len(gx):]):
            @pl.when(i == 0)
            def _(ref=ref):
                ref[...] = jnp.zeros_like(ref)
            ref[...] += gval

    dp_specs = [pl.BlockSpec(pspecs[i].block_shape, lambda i: (0, 0)) for i in gp]
    dp_shapes = [SDS(pspecs[i].block_shape, F32) for i in gp]
    res, carried = _hosting_call(
        body, comm, name=name, grid=(nt,), in_specs=list(xspecs) + list(pspecs) + list(cspecs) + list(especs),
        out_specs=[pl.BlockSpec((tm, w), lambda i: (i, 0)) for w in widths] + dp_specs,
        out_shape=[SDS((t, w), dt) for w, dt in zip(widths, dx_dtypes)] + dp_shapes, scratch_shapes=[],
        args=(*xa, *pa, *ca, *ea))
    return res if comm is None else (res, carried)


def _rms_f(x, g):
    return (x * lax.rsqrt(jnp.mean(x * x, axis=-1, keepdims=True) + NORM_EPS) * g,)


def _group_sum_impl(x, ones_bd):
    p1, p2 = _split2(x)
    dot = lambda p: lax.dot_general(p, ones_bd.astype(BF16), (((1,), (0,)), ((), ())), preferred_element_type=F32)
    return dot(p1) + dot(p2)


@jax.custom_vjp
def _group_sum(x, ones_bd):
    return _group_sum_impl(x, ones_bd)


_group_sum.defvjp(lambda x, o: (_group_sum_impl(x, o), o),
                  lambda o, g: (_group_sum_impl(g, o), jnp.zeros_like(o)))


def _rwkv_prep_f(r, k, v, lo, rp, kp, vp, lop, mu_r, mu_k, mu_v, mu_lo, w0, w2p, a0, a2p, g2p, k_k, k_a, ones_bd):
    r = r + mu_r * (rp - r)
    k = k + mu_k * (kp - k)
    v = v + mu_v * (vp - v)
    lo = lo + mu_lo * (lop - lo)
    w_log = -_softplus(-(w0 + _nn(jnp.tanh(lo), w2p))) - 0.5
    lw = -jnp.exp(w_log)
    a_g = _sigmoid(a0 + _nn(lo, a2p))
    g = _nn(_sigmoid(lo), g2p)
    kk = k * k_k
    kk = kk / jnp.maximum(jnp.sqrt(_group_sum(kk * kk, ones_bd)), L2_EPS)
    k2 = k * (1.0 + (a_g - 1.0) * k_a)
    return r, lw, k2, v, -kk, kk * a_g, g


def _rwkv_post_f(y, r, k2, v, g, r_k, gn_w, gn_b, ones_bd):
    inv_n = 1.0 / HB_DIM
    mean = _group_sum(y, ones_bd) * inv_n
    yc = y - mean
    var = _group_sum(yc * yc, ones_bd) * inv_n
    yn = yc * lax.rsqrt(var + RWKV_GN_EPS) * gn_w + gn_b
    bonus = _group_sum(r * k2 * r_k, ones_bd) * v
    return ((yn + bonus) * g,)


def _tri(c, strict=False):
    ii = lax.broadcasted_iota(jnp.int32, (c, c), 0)
    jj = lax.broadcasted_iota(jnp.int32, (c, c), 1)
    return (jj < ii) if strict else (jj <= ii)


def _hgrn_step(st0, q_a, f_a, i_a, g_a, l0, l1, onorm):
    nh, nj = len(q_a), len(q_a[0])
    c = q_a[0][0].shape[0]
    combos = [(j, h) for j in range(nj) for h in range(nh)]
    every = lambda fn: {q: fn(q) for q in combos}
    at_ = lambda d: (lambda q: d[q[1]][q[0]])
    qa_, fa_, ia_, ga_ = (at_(z) for z in (q_a, f_a, i_a, g_a))
    incl = _tri(c)
    rows = lax.broadcasted_iota(jnp.int32, (c, 1), 0)
    lb = []
    for h in range(nh):
        mx = jnp.maximum(l0[h], l1[h])
        e0, e1 = jnp.exp(l0[h] - mx), jnp.exp(l1[h] - mx)
        lb.append(e0 / (e0 + e1))
    forget = every(lambda q: lb[q[1]] + (1.0 - lb[q[1]]) * _sigmoid(fa_(q)))
    qs = every(lambda q: _silu(qa_(q)))
    kk = every(lambda q: 1.0 - forget[q])
    lf = every(lambda q: jnp.log(forget[q]))
    bcum = every(lambda q: _cumsum_rows(lf[q]))
    bref = every(lambda q: jnp.sum(jnp.where(rows <= c // 2, lf[q], 0.0), axis=0, keepdims=True))
    blast = every(lambda q: jnp.sum(lf[q], axis=0, keepdims=True))
    scores = every(lambda q: jnp.where(incl, _nt(qs[q] * jnp.exp(bcum[q] - bref[q]),
                                                 kk[q] * jnp.exp(bref[q] - bcum[q])), 0.0))
    intra = every(lambda q: _nn(scores[q], ia_(q)))
    qb = every(lambda q: qs[q] * jnp.exp(bcum[q]))
    upd = every(lambda q: _tn(ia_(q), kk[q] * jnp.exp(blast[q] - bcum[q])))
    dec = every(lambda q: jnp.exp(blast[q]))
    st = list(st0)
    o = {}
    for j in range(nj):
        for h in range(nh):
            o[(j, h)] = intra[(j, h)] + _nt(qb[(j, h)], st[h])
        st = [st[h] * dec[(j, h)] + upd[(j, h)] for h in range(nh)]
    out = every(lambda q: o[q] * lax.rsqrt(jnp.mean(o[q] * o[q], axis=-1, keepdims=True) + NORM_EPS)
                * onorm[q[1]] * _silu(ga_(q)))
    return [[out[(j, h)] for j in range(nj)] for h in range(nh)], st


def _hgrn_blocks(ref, nj, c):
    return [[ref[j * c:(j + 1) * c, h * HA_DIM:(h + 1) * HA_DIM] for j in range(nj)] for h in range(HA_HEADS)]


def _hgrn_cols(ref):
    return [ref[:, h * HA_DIM:(h + 1) * HA_DIM] for h in range(HA_HEADS)]


def _hgrn_fwd(p_h, l0, l1, onorm):
    t = p_h.shape[0]
    cc, nj = HGRN_CHUNK, HGRN_GROUP
    c = cc * nj
    n = t // c

    def body(q_ref, f_ref, i_ref, g_ref, l0_ref, l1_ref, on_ref, o_ref, hs_ref, st_ref):
        @pl.when(pl.program_id(0) == 0)
        def _():
            st_ref[...] = jnp.zeros_like(st_ref)

        hs_ref[0] = st_ref[...]
        o, st1 = _hgrn_step([st_ref[h] for h in range(HA_HEADS)],
                            *[_hgrn_blocks(ref, nj, cc) for ref in (q_ref, f_ref, i_ref, g_ref)],
                            _hgrn_cols(l0_ref), _hgrn_cols(l1_ref), _hgrn_cols(on_ref))
        for h in range(HA_HEADS):
            for j in range(nj):
                o_ref[j * cc:(j + 1) * cc, h * HA_DIM:(h + 1) * HA_DIM] = o[h][j]
            st_ref[h] = st1[h]

    col = lambda j: pl.BlockSpec((c, W_A), lambda i, j=j: (i, j))
    par = pl.BlockSpec((1, W_A), lambda i: (0, 0))
    return pl.pallas_call(
        body, name="hgrn_fwd", grid=(n,), in_specs=[col(0), col(1), col(2), col(3), par, par, par],
        out_specs=[pl.BlockSpec((c, W_A), lambda i: (i, 0)),
                   pl.BlockSpec((1, HA_HEADS, HA_DIM, HA_DIM), lambda i: (i, 0, 0, 0))],
        out_shape=[SDS((t, W_A), F32), SDS((n, HA_HEADS, HA_DIM, HA_DIM), F32)],
        scratch_shapes=[pltpu.VMEM((HA_HEADS, HA_DIM, HA_DIM), F32)],
        compiler_params=_params(("arbitrary",)))(p_h, p_h, p_h, p_h, l0, l1, onorm)


def _hgrn_bwd(p_h, l0, l1, onorm, hs, do, do_col, comm=None):
    t = p_h.shape[0]
    cc, nj = HGRN_CHUNK, HGRN_GROUP
    c = cc * nj
    n = t // c

    def body(q_ref, f_ref, i_ref, g_ref, l0_ref, l1_ref, on_ref, hs_ref, do_ref,
             dp_ref, dl0_ref, dl1_ref, don_ref, dst_ref):
        @pl.when(pl.program_id(0) == 0)
        def _():
            dst_ref[...] = jnp.zeros_like(dst_ref)
            dl0_ref[...] = jnp.zeros_like(dl0_ref)
            dl1_ref[...] = jnp.zeros_like(dl1_ref)
            don_ref[...] = jnp.zeros_like(don_ref)

        args = ([hs_ref[0, h] for h in range(HA_HEADS)],
                *[_hgrn_blocks(ref, nj, cc) for ref in (q_ref, f_ref, i_ref, g_ref)],
                _hgrn_cols(l0_ref), _hgrn_cols(l1_ref), _hgrn_cols(on_ref))
        _, vjp = jax.vjp(_hgrn_step, *args)
        dst0, dq, df, di, dg, dl0, dl1, don = vjp((_hgrn_blocks(do_ref, nj, cc),
                                                   [dst_ref[h] for h in range(HA_HEADS)]))
        for h in range(HA_HEADS):
            sl = slice(h * HA_DIM, (h + 1) * HA_DIM)
            for k, dv in enumerate((dq, df, di, dg)):
                for j in range(nj):
                    dp_ref[j * cc:(j + 1) * cc, k * W_A + h * HA_DIM:k * W_A + (h + 1) * HA_DIM] = dv[h][j]
            dl0_ref[:, sl] += dl0[h]
            dl1_ref[:, sl] += dl1[h]
            don_ref[:, sl] += don[h]
            dst_ref[h] = dst0[h]

    col = lambda j: pl.BlockSpec((c, W_A), lambda i, j=j: (n - 1 - i, j))
    par = pl.BlockSpec((1, W_A), lambda i: (0, 0))
    return _hosting_call(
        body, comm, name="hgrn_bwd", grid=(n,),
        in_specs=[col(0), col(1), col(2), col(3), par, par, par,
                  pl.BlockSpec((1, HA_HEADS, HA_DIM, HA_DIM), lambda i: (n - 1 - i, 0, 0, 0)),
                  pl.BlockSpec((c, W_A), lambda i: (n - 1 - i, do_col))],
        out_specs=[pl.BlockSpec((c, N_HGRN_COLS), lambda i: (n - 1 - i, 0)), par, par, par],
        out_shape=[SDS((t, N_HGRN_COLS), F32), SDS((1, W_A), F32), SDS((1, W_A), F32), SDS((1, W_A), F32)],
        scratch_shapes=[pltpu.VMEM((HA_HEADS, HA_DIM, HA_DIM), F32)],
        args=(p_h, p_h, p_h, p_h, l0, l1, onorm, hs, do))


HB_PAIRS = HB_HEADS // 2
PAIR_W = 2 * HB_DIM


def _head_lane_masks():
    lane = lax.broadcasted_iota(jnp.int32, (1, PAIR_W), 1)
    return (lane < HB_DIM).astype(F32), (lane >= HB_DIM).astype(F32)


@jax.custom_vjp
def _stack_heads(x):
    m0, m1 = _head_lane_masks()
    return jnp.concatenate([x * m0, x * m1], axis=0)


def _stack_heads_bwd(_, g):
    m0, m1 = _head_lane_masks()
    c = g.shape[0] // 2
    return (g[:c] * m0 + g[c:] * m1,)


_stack_heads.defvjp(lambda x: (_stack_heads(x), None), _stack_heads_bwd)


@jax.custom_vjp
def _unstack_heads(ys):
    c = ys.shape[0] // 2
    return ys[:c] + ys[c:]


_unstack_heads.defvjp(lambda ys: (_unstack_heads(ys), None), lambda _, g: (_stack_heads(g),))


def _same_head_block(c):
    ii = lax.broadcasted_iota(jnp.int32, (2 * c, 2 * c), 0)
    jj = lax.broadcasted_iota(jnp.int32, (2 * c, 2 * c), 1)
    same = (ii < c) == (jj < c)
    return same & (jj <= ii), same & (jj < ii), (ii == jj).astype(F32)


@jax.custom_vjp
def _rows_join(top, bottom):
    return jnp.concatenate([top, bottom], axis=0)


def _rows_join_bwd(n_top, g):
    return g[:n_top], g[n_top:]


_rows_join.defvjp(lambda top, bottom: (_rows_join(top, bottom), top.shape[0]), _rows_join_bwd)


def _rows_split_impl(x, n_top):
    return x[:n_top], x[n_top:]


_rows_split = jax.custom_vjp(_rows_split_impl, nondiff_argnums=(1,))
_rows_split.defvjp(lambda x, n_top: (_rows_split_impl(x, n_top), None),
                   lambda n_top, _, g: (jnp.concatenate([g[0], g[1]], axis=0),))


def _rwkv_step(s0, r, lw, k, v, a, b):
    npair, nj = len(r), len(r[0])
    c = r[0][0].shape[0]
    combos = [(j, p) for j in range(nj) for p in range(npair)]
    every = lambda fn: {q: fn(q) for q in combos}
    at_ = lambda d: (lambda q: d[q[1]][q[0]])
    r_, lw_, k_, v_, a_, b_ = (at_(z) for z in (r, lw, k, v, a, b))
    incl, strict, eye = _same_head_block(c)

    gam = every(lambda q: _cumsum_rows(lw_(q)))
    gtot = every(lambda q: jnp.sum(lw_(q), axis=0, keepdims=True))
    eneg = every(lambda q: jnp.exp(-gam[q]))
    edec = every(lambda q: jnp.exp(gtot[q] - gam[q]))
    at = every(lambda q: _stack_heads(a_(q) * jnp.exp(gam[q] - lw_(q))))
    rt = every(lambda q: _stack_heads(r_(q) * jnp.exp(gam[q])))
    bt = every(lambda q: _stack_heads(b_(q) * eneg[q]))
    kt = every(lambda q: _stack_heads(k_(q) * eneg[q]))
    bdec = every(lambda q: _stack_heads(b_(q) * edec[q]))
    kdec = every(lambda q: _stack_heads(k_(q) * edec[q]))
    vs = every(lambda q: _stack_heads(v_(q)))
    a_ab = every(lambda q: jnp.where(strict, _nt(at[q], bt[q]), 0.0))
    a_ak = every(lambda q: jnp.where(strict, _nt(at[q], kt[q]), 0.0))
    a_rb = every(lambda q: jnp.where(incl, _nt(rt[q], bt[q]), 0.0))
    a_rk = every(lambda q: jnp.where(incl, _nt(rt[q], kt[q]), 0.0))
    tinv = every(lambda q: eye + a_ab[q])
    pw = a_ab
    span = 2
    while span < c:
        pw = every(lambda q, pw=pw: _nn_x3(pw[q], pw[q]))
        tinv = every(lambda q, pw=pw, tinv=tinv: tinv[q] + _nn_x3(pw[q], tinv[q]))
        span *= 2
    akv = every(lambda q: _nn(a_ak[q], vs[q]))
    w1 = every(lambda q: _nn(tinv[q], at[q]))
    u0 = every(lambda q: _nn(tinv[q], akv[q]))
    wr = every(lambda q: _rows_join(w1[q], rt[q]))
    bk = every(lambda q: _rows_join(bdec[q], kdec[q]))
    yv = every(lambda q: _nn(a_rk[q], vs[q]))
    gdec = every(lambda q: jnp.exp(gtot[q]))

    s = list(s0)
    y = [[None] * nj for _ in range(npair)]
    for j in range(nj):
        both = {p: _rows_split(_nt(wr[(j, p)], s[p]), 2 * c) for p in range(npair)}
        u = {p: both[p][0] + u0[(j, p)] for p in range(npair)}
        for p in range(npair):
            y[p][j] = _unstack_heads(both[p][1] + _nn(a_rb[(j, p)], u[p]) + yv[(j, p)])
        s = [s[p] * gdec[(j, p)] + _tn(_rows_join(u[p], vs[(j, p)]), bk[(j, p)]) for p in range(npair)]
    return y, s


def _rwkv_blocks(ref, nj, c):
    return [[ref[j * c:(j + 1) * c, p * PAIR_W:(p + 1) * PAIR_W] for j in range(nj)] for p in range(HB_PAIRS)]


def _rwkv_fwd(seqs, comm=None):
    t = seqs[0].shape[0]
    c, nj = RWKV_CHUNK, RWKV_GROUP
    n = t // (c * nj)

    def body(r_ref, lw_ref, k_ref, v_ref, a_ref, b_ref, y_ref, hs_ref, st_ref):
        @pl.when(pl.program_id(0) == 0)
        def _():
            st_ref[...] = jnp.zeros_like(st_ref)

        hs_ref[0] = st_ref[...]
        s0 = [st_ref[p] for p in range(HB_PAIRS)]
        y, s1 = _rwkv_step(s0, *[_rwkv_blocks(ref, nj, c) for ref in (r_ref, lw_ref, k_ref, v_ref, a_ref, b_ref)])
        for p in range(HB_PAIRS):
            for j in range(nj):
                y_ref[j * c:(j + 1) * c, p * PAIR_W:(p + 1) * PAIR_W] = y[p][j]
            st_ref[p] = s1[p]

    seq = pl.BlockSpec((c * nj, W_B), lambda i: (i, 0))
    return _hosting_call(
        body, comm, name="rwkv_fwd", grid=(n,), in_specs=[seq] * 6,
        out_specs=[seq, pl.BlockSpec((1, HB_PAIRS, PAIR_W, PAIR_W), lambda i: (i, 0, 0, 0))],
        out_shape=[SDS((t, W_B), F32), SDS((n, HB_PAIRS, PAIR_W, PAIR_W), F32)],
        scratch_shapes=[pltpu.VMEM((HB_PAIRS, PAIR_W, PAIR_W), F32)], args=tuple(seqs))


def _rwkv_bwd(seqs, hs, dy, comm=None):
    t = seqs[0].shape[0]
    c, nj = RWKV_CHUNK, RWKV_GROUP
    n = t // (c * nj)

    def body(r_ref, lw_ref, k_ref, v_ref, a_ref, b_ref, hs_ref, dy_ref,
             dr_ref, dlw_ref, dk_ref, dv_ref, da_ref, db_ref, dst_ref):
        @pl.when(pl.program_id(0) == 0)
        def _():
            dst_ref[...] = jnp.zeros_like(dst_ref)

        s0 = [hs_ref[0, p] for p in range(HB_PAIRS)]
        seq_vals = [_rwkv_blocks(ref, nj, c) for ref in (r_ref, lw_ref, k_ref, v_ref, a_ref, b_ref)]
        _, vjp = jax.vjp(_rwkv_step, s0, *seq_vals)
        grads = vjp((_rwkv_blocks(dy_ref, nj, c), [dst_ref[p] for p in range(HB_PAIRS)]))
        for ref, gr in zip((dr_ref, dlw_ref, dk_ref, dv_ref, da_ref, db_ref), grads[1:]):
            for p in range(HB_PAIRS):
                for j in range(nj):
                    ref[j * c:(j + 1) * c, p * PAIR_W:(p + 1) * PAIR_W] = gr[p][j]
        m0, m1 = _head_lane_masks()
        rows0 = (lax.broadcasted_iota(jnp.int32, (PAIR_W, 1), 0) < HB_DIM).astype(F32)
        blocks = rows0 * m0 + (1.0 - rows0) * m1
        for p in range(HB_PAIRS):
            dst_ref[p] = grads[0][p] * blocks

    seq = pl.BlockSpec((c * nj, W_B), lambda i: (n - 1 - i, 0))
    return _hosting_call(
        body, comm, name="rwkv_bwd", grid=(n,),
        in_specs=[seq] * 6 + [pl.BlockSpec((1, HB_PAIRS, PAIR_W, PAIR_W), lambda i: (n - 1 - i, 0, 0, 0)), seq],
        out_specs=[seq] * 6, out_shape=[SDS((t, W_B), F32)] * 6,
        scratch_shapes=[pltpu.VMEM((HB_PAIRS, PAIR_W, PAIR_W), F32)], args=(*seqs, hs, dy))


def _final_loss(x3, fnorm, target, *, tm):
    t, d = x3.shape

    def body(x_ref, g_ref, t_ref, dx_ref, dg_ref, loss_ref):
        @pl.when(pl.program_id(0) == 0)
        def _():
            dg_ref[...] = jnp.zeros_like(dg_ref)
            loss_ref[...] = jnp.zeros_like(loss_ref)

        x, g = x_ref[...], g_ref[...]
        rinv = lax.rsqrt(jnp.mean(x * x, axis=-1, keepdims=True) + NORM_EPS)
        xh = x * rinv
        diff = xh * g - t_ref[...]
        loss_ref[...] += 0.5 * jnp.sum(jnp.mean(diff * diff, axis=-1, keepdims=True))
        dy = diff * (1.0 / d)
        dg_ref[...] += jnp.sum(dy * xh, axis=0, keepdims=True)
        dxh = dy * g
        dx_ref[...] = rinv * (dxh - xh * jnp.mean(dxh * xh, axis=-1, keepdims=True))

    row = pl.BlockSpec((tm, d), lambda i: (i, 0))
    return pl.pallas_call(
        body, name="final_loss", grid=(t // tm,), in_specs=[row, pl.BlockSpec((1, d), lambda i: (0, 0)), row],
        out_specs=[row, pl.BlockSpec((1, d), lambda i: (0, 0)), pl.BlockSpec((8, 128), lambda i: (0, 0))],
        out_shape=[SDS((t, d), F32), SDS((1, d), F32), SDS((8, 128), F32)],
        compiler_params=_params(("arbitrary",)))(x3, fnorm, target)


def _gate_up_act(h, wgt, wut, *, tm, tn, name, comm=None):
    t, d = h.shape
    tm = min(tm, t)

    def body(h_ref, g_ref, u_ref, a_out, u_out, act_out):
        hv = h_ref[...]
        a = _dg(hv, g_ref[...], 1, 1, False)
        u = _dg(hv, u_ref[...], 1, 1, False)
        a_out[...] = a.astype(a_out.dtype)
        u_out[...] = u.astype(u_out.dtype)
        act_out[...] = (_silu(a) * u).astype(act_out.dtype)

    wspec = pl.BlockSpec((tn, d), lambda i, j: (j, 0))
    ospec = pl.BlockSpec((tm, tn), lambda i, j: (i, j))
    return _hosting_call(
        body, comm, name=name, grid=(t // tm, D_FF // tn),
        in_specs=[pl.BlockSpec((tm, d), lambda i, j: (i, 0)), wspec, wspec], out_specs=[ospec, ospec, ospec],
        out_shape=[SDS((t, D_FF), BF16), SDS((t, D_FF), BF16), SDS((t, D_FF), BF16)], scratch_shapes=[],
        args=(h, wgt, wut))


def _dact_swiglu(dout, wd, a, u, *, tm, tn, name, comm=None):
    t, d = dout.shape
    tm = min(tm, t)

    def body(d_ref, w_ref, a_ref, u_ref, da_out, du_out):
        dact = 0.5 * _dg(d_ref[...], w_ref[...], 1, 1, False)
        av, uv = a_ref[...].astype(F32), u_ref[...].astype(F32)
        s = _sigmoid(av)
        da_out[...] = (dact * uv * (s * (1.0 + av * (1.0 - s)))).astype(da_out.dtype)
        du_out[...] = (dact * (av * s)).astype(du_out.dtype)

    tile = pl.BlockSpec((tm, tn), lambda i, j: (i, j))
    return _hosting_call(
        body, comm, name=name, grid=(t // tm, D_FF // tn),
        in_specs=[pl.BlockSpec((tm, d), lambda i, j: (i, 0)), pl.BlockSpec((tn, d), lambda i, j: (j, 0)), tile, tile],
        out_specs=[tile, tile], out_shape=[SDS((t, D_FF), BF16), SDS((t, D_FF), BF16)], scratch_shapes=[],
        args=(dout, wd, a, u))


class _Plan:
    def __init__(self):
        self.entries, self.counts = collections.defaultdict(list), {}

    def carry(self, host, comm_of, after):
        self.entries[host].append((comm_of, after))

    def comm(self, host, g):
        comms = [comm_of(g) for comm_of, _ in self.entries.get(host, [])]
        self.counts[host] = [len(c.arrays) for c in comms]
        return functools.reduce(_join_comms, comms) if comms else None

    def done(self, host, results, w):
        start = 0
        for (_, after), n in zip(self.entries.get(host, []), self.counts.get(host, [])):
            after(results[start:start + n], w)
            start += n


def _ffn_fwd(x, w, tag, plan, g):
    comm = plan.comm(f"{tag}_rms", g)
    res = _rowwise(_rms_f, [x], [w[f"{tag}_norm"]], [[0]], [BF16], tm=512, name=f"{tag}_rms", comm=comm)
    (h,), carried = res if comm is not None else (res, [])
    plan.done(f"{tag}_rms", carried, w)
    (a, u, act), carried = _gate_up_act(h, w[f"{tag}_wgt"], w[f"{tag}_wut"], tm=2048, tn=256, name=f"{tag}_gate_up",
                                        comm=plan.comm(f"{tag}_gate_up", g))
    plan.done(f"{tag}_gate_up", carried, w)
    comm = plan.comm(f"{tag}_down", g)
    out = _mm(act, w[f"{tag}_wd"], tm=1024, tn=D_MODEL, tk=D_FF, name=f"{tag}_down", res=x, scale=0.5, comm=comm)
    if comm is not None:
        out, carried = out
        plan.done(f"{tag}_down", carried, w)
    return out, (h, a, u, act)


def _ffn_bwd(dout, x, w, saved, tag, plan, g):
    h, a, u, act = saved

    def carrying(fn, host, *args, **kwargs):
        comm = plan.comm(host, g)
        res = fn(*args, name=host, comm=comm, **kwargs)
        out, carried = res if comm is not None else (res, [])
        plan.done(host, carried, w)
        return out

    (da, du), carried = _dact_swiglu(dout, w[f"{tag}_wd"], a, u, tm=2048, tn=256, name=f"{tag}_dact",
                                     comm=plan.comm(f"{tag}_dact", g))
    plan.done(f"{tag}_dact", carried, w)
    g[f"{tag}_wd"] = _mm(act, dout, ta=True, tm=D_FF // 2, tn=D_MODEL, tk=1024, name=f"{tag}_dwd", scale=0.5)
    g[f"{tag}_wgt"] = carrying(_mm, f"{tag}_dwg", da, h, ta=True, tm=D_FF // 2, tn=D_MODEL, tk=1024)
    g[f"{tag}_wut"] = carrying(_mm, f"{tag}_dwu", du, h, ta=True, tm=D_FF // 2, tn=D_MODEL, tk=1024)
    dh = carrying(_mm, f"{tag}_dh_g", da, w[f"{tag}_wgt"], tm=1024, tn=D_MODEL, tk=D_FF)
    dh = carrying(_mm, f"{tag}_dh_u", du, w[f"{tag}_wut"], tm=1024, tn=D_MODEL, tk=D_FF, res=dh)
    dx, g[f"{tag}_norm"] = _rowwise_bwd(_rms_f, [x], [w[f"{tag}_norm"]], [dh], x_grad=[True], p_grad=[True],
                                        dx_groups=[[0]], dx_dtypes=[F32], tm=512, name=f"{tag}_drms",
                                        extra={0: dout})
    return dx


def _local_step(x, target, w, plan=None):
    plan = plan or _Plan()
    ones_bd = jnp.kron(jnp.eye(HB_HEADS, dtype=F32), jnp.ones((HB_DIM, HB_DIM), F32))
    g = {}
    x1, ffn1_saved = _ffn_fwd(x, w, "ffn1", plan, g)
    hm, = _rowwise(_rms_f, [x1], [w["mix_norm"]], [[0]], [BF16], tm=512, name="mix_rms")
    p_h = _mm(hm, w["w_in_h"], tm=2048, tn=256, tk=D_MODEL, name="inproj_h")
    p_r = _mm(hm, w["w_in_r"], tm=2048, tn=256, tk=D_MODEL, name="inproj_r")
    o_a, hgrn_states = _hgrn_fwd(p_h, w["lb0"], w["lb1"], w["hgrn_out_norm"])

    mu = w["mu_pad"]
    prep_xs = [(p_r, W_B, 0), (p_r, W_B, 1), (p_r, W_B, 2), (p_r, LORA_PAD, 6),
               ("prev", p_r, W_B, 0), ("prev", p_r, W_B, 1), ("prev", p_r, W_B, 2), ("prev", p_r, LORA_PAD, 6)]
    prep_ps = [(mu, W_B, 0), (mu, W_B, 1), (mu, W_B, 2), (mu, LORA_PAD, 6), w["rwkv_w0"], w["w2_pad"], w["rwkv_a0"],
               w["a2_pad"], w["g2_pad"], w["rwkv_k_k"], w["rwkv_k_a"], ones_bd]
    prep_f = _rwkv_prep_f
    r, lw, k2, v, a_vec, b_vec, gate = _rowwise(prep_f, prep_xs, prep_ps, [[0], [1], [2], [3], [4], [5], [6]],
                                                [F32] * 7, tm=256, name="rwkv_prep")
    seqs = [r, lw, k2, v, a_vec, b_vec]
    (y, rwkv_states), carried = _rwkv_fwd(seqs, comm=plan.comm("rwkv_fwd", g))
    plan.done("rwkv_fwd", carried, w)
    post_f = _rwkv_post_f
    post_xs = [y, r, k2, v, gate]
    post_ps = [w["rwkv_r_k"], w["rwkv_gn_w"], w["rwkv_gn_b"], ones_bd]
    o_b, = _rowwise(post_f, post_xs, post_ps, [[0]], [F32], tm=256, name="rwkv_post")
    x2 = _mm(o_a, w["w_out_a"], tm=2048, tn=256, tk=W_A, name="outproj_a", res=x1)
    x2 = _mm(o_b, w["w_out_b"], tm=2048, tn=256, tk=W_B, name="outproj_b", res=x2)
    x3, ffn2_saved = _ffn_fwd(x2, w, "ffn2", plan, g)
    dx3, g["final_norm"], loss = _final_loss(x3, w["final_norm"], target, tm=256)

    dx2 = _ffn_bwd(dx3, x2, w, ffn2_saved, "ffn2", plan, g)
    do_a = _mm(dx2, w["w_out_a"], tb=True, tm=2048, tn=256, tk=D_MODEL, name="outproj_do_a")
    do_b = _mm(dx2, w["w_out_b"], tb=True, tm=2048, tn=256, tk=D_MODEL, name="outproj_do_b")
    g["w_out_a"] = _mm(o_a, dx2, ta=True, tm=W_A, tn=D_MODEL, tk=1024, name="outproj_dw_a")
    g["w_out_b"] = _mm(o_b, dx2, ta=True, tm=W_B, tn=D_MODEL, tk=1024, name="outproj_dw_b")

    (dp_h, g["lb0"], g["lb1"], g["hgrn_out_norm"]), carried = _hgrn_bwd(
        p_h, w["lb0"], w["lb1"], w["hgrn_out_norm"], hgrn_states, do_a, 0, comm=plan.comm("hgrn_bwd", g))
    plan.done("hgrn_bwd", carried, w)
    post_out = _rowwise_bwd(post_f, post_xs, post_ps, [do_b], x_grad=[True] * 5, p_grad=[True] * 3 + [False],
                            dx_groups=[[0], [1], [2], [3], [4]], dx_dtypes=[F32] * 5, tm=256, name="rwkv_post_bwd")
    dy, dr1, dk1, dv1, dgate, g["rwkv_r_k"], g["rwkv_gn_w"], g["rwkv_gn_b"] = post_out
    (dr2, dlw, dk2, dv2, da_vec, db_vec), carried = _rwkv_bwd(seqs, rwkv_states, dy, comm=plan.comm("rwkv_bwd", g))
    plan.done("rwkv_bwd", carried, w)

    def prep2_f(*vals):
        r_, lw_, k2_, v_, a_, b_, g_ = prep_f(*vals)
        return r_, lw_, k2_, v_, a_, b_, g_, r_, k2_, v_

    prep_out = _rowwise_bwd(prep2_f, prep_xs, prep_ps, [dr2, dlw, dk2, dv2, da_vec, db_vec, dgate, dr1, dk1, dv1],
                            x_grad=[True] * 8, p_grad=[True] * 11 + [False], dx_groups=[[0, 1, 2, 3], [4, 5, 6, 7]],
                            dx_dtypes=[F32, F32], tm=256, name="rwkv_prep_bwd")
    dpr_main, dpr_prev = prep_out[0], prep_out[1]
    (dmu_r, dmu_k, dmu_v, dmu_lo, g["rwkv_w0"], g["w2_pad"], g["rwkv_a0"], g["a2_pad"], g["g2_pad"],
     g["rwkv_k_k"], g["rwkv_k_a"]) = prep_out[2:]
    g["mu_pad"] = jnp.concatenate([dmu_r, dmu_k, dmu_v, dmu_lo], axis=1)
    dp_r, = _rowwise(lambda u_, s_: (u_ + s_,), [dpr_main, ("next", dpr_prev, N_RWKV_PAD, 0)], [], [[0]], [F32],
                     tm=512, name="rwkv_dp_sum")
    dhm = _mm(dp_h, w["w_in_h"], tb=True, tm=1024, tn=D_MODEL, tk=N_HGRN_COLS, name="inproj_dh_h")
    dhm = _mm(dp_r, w["w_in_r"], tb=True, tm=1024, tn=D_MODEL, tk=N_RWKV_PAD, name="inproj_dh_r", res=dhm)
    g["w_in_h"] = _mm(hm, dp_h, ta=True, tm=D_MODEL, tn=D_MODEL, tk=1024, name="inproj_dw_h")
    g["w_in_r"] = _mm(hm, dp_r, ta=True, tm=D_MODEL, tn=N_RWKV_PAD // 2, tk=1024, name="inproj_dw_r")
    mix_comm = plan.comm("mix_drms", g)
    mix_out = _rowwise_bwd(_rms_f, [x1], [w["mix_norm"]], [dhm], x_grad=[True], p_grad=[True], dx_groups=[[0]],
                           dx_dtypes=[F32], tm=512, name="mix_drms", extra={0: dx2}, comm=mix_comm)
    (dx1, g["mix_norm"]), carried = mix_out if mix_comm is not None else (mix_out, [])
    plan.done("mix_drms", carried, w)
    dx0 = _ffn_bwd(dx1, x, w, ffn1_saved, "ffn1", plan, g)
    return loss, dx0, g


HBM_SPEC = pl.BlockSpec(memory_space=pl.ANY)

Comm = collections.namedtuple("Comm", "arrays out_shapes aliased sem_shapes start finish")


def _join_comms(first, second):
    assert first.aliased == second.aliased
    n, s = len(first.arrays), len(first.sem_shapes)

    def start(ins, outs, sems):
        first.start(ins[:n], outs[:n], sems[:s])
        second.start(ins[n:], outs[n:], sems[s:])

    def finish(ins, outs, sems):
        first.finish(ins[:n], outs[:n], sems[:s])
        second.finish(ins[n:], outs[n:], sems[s:])

    return Comm(list(first.arrays) + list(second.arrays), list(first.out_shapes) + list(second.out_shapes),
                first.aliased, list(first.sem_shapes) + list(second.sem_shapes), start, finish)


def _run_comm(comm, name):
    n = len(comm.arrays)

    def body(*refs):
        ins, outs, sems = refs[:n], refs[n:2 * n], refs[2 * n:]
        comm.start(ins, outs, sems)
        comm.finish(ins, outs, sems)

    return pl.pallas_call(
        body, name=name, in_specs=[HBM_SPEC] * n, out_specs=[HBM_SPEC] * n, out_shape=list(comm.out_shapes),
        input_output_aliases={t: t for t in range(n)} if comm.aliased else {},
        scratch_shapes=list(comm.sem_shapes))(*comm.arrays)


def _hosting_call(body, comm, *, name, grid, in_specs, out_specs, out_shape, scratch_shapes, args):
    sem = ("arbitrary",) * len(grid)
    if comm is None:
        res = pl.pallas_call(body, name=name, grid=grid, in_specs=in_specs, out_specs=out_specs, out_shape=out_shape,
                             scratch_shapes=scratch_shapes, compiler_params=_params(sem))(*args)
        return list(res), []
    ni, no, ns, nc = len(in_specs), len(out_specs), len(scratch_shapes), len(comm.arrays)

    def wrapped(*refs):
        ins, cins = refs[:ni], refs[ni:ni + nc]
        outs, couts = refs[ni + nc:ni + nc + no], refs[ni + nc + no:ni + 2 * nc + no]
        scr, sems = refs[ni + 2 * nc + no:ni + 2 * nc + no + ns], refs[ni + 2 * nc + no + ns:]
        first = functools.reduce(jnp.logical_and, [pl.program_id(k) == 0 for k in range(len(grid))])
        last = functools.reduce(jnp.logical_and, [pl.program_id(k) == grid[k] - 1 for k in range(len(grid))])

        @pl.when(first)
        def _():
            comm.start(cins, couts, sems)

        body(*ins, *outs, *scr)

        @pl.when(last)
        def _():
            comm.finish(cins, couts, sems)

    res = pl.pallas_call(
        wrapped, name=name, grid=grid, in_specs=list(in_specs) + [HBM_SPEC] * nc,
        out_specs=list(out_specs) + [HBM_SPEC] * nc, out_shape=list(out_shape) + list(comm.out_shapes),
        scratch_shapes=list(scratch_shapes) + list(comm.sem_shapes),
        input_output_aliases={ni + t: no + t for t in range(nc)} if comm.aliased else {},
        compiler_params=_params(sem))(*args, *comm.arrays)
    return list(res[:no]), list(res[no:])


def _chips(x, y):
    return [(1 - x, y), (x, 1 - y), (1 - x, 1 - y)]


def _gather_comm(bufs):
    n = len(bufs)

    def copies(outs, sems):
        ici_send, ici_recv, d2d_send, d2d_recv = sems
        x, y, c = lax.axis_index("x"), lax.axis_index("y"), lax.axis_index("c")

        def half(t, slot, hc):
            hr = bufs[t].shape[1] // 2
            return outs[t].at[slot, pl.ds(pl.multiple_of(hc * hr, 16), hr), :]

        def ici(t, j, slot, px, py):
            return pltpu.make_async_remote_copy(src_ref=half(t, slot, c), dst_ref=half(t, slot, c),
                                                send_sem=ici_send.at[3 * t + j], recv_sem=ici_recv.at[3 * t + j],
                                                device_id=(px, py, c), device_id_type=MESH)

        def d2d(t, j, slot, hc):
            return pltpu.make_async_remote_copy(src_ref=half(t, slot, hc), dst_ref=half(t, slot, hc),
                                                send_sem=d2d_send.at[3 * t + j], recv_sem=d2d_recv.at[3 * t + j],
                                                device_id=(x, y, 1 - c), device_id_type=MESH)

        peers = [(t, j, px, py) for t in range(n) for j, (px, py) in enumerate(_chips(x, y))]
        return ici, d2d, peers, 2 * x + y, c

    def start(ins, outs, sems):
        ici, _, peers, me, _ = copies(outs, sems)
        for t, j, px, py in peers:
            ici(t, j, me, px, py).start()

    def finish(ins, outs, sems):
        ici, d2d, peers, me, c = copies(outs, sems)
        for t, j, px, py in peers:
            ici(t, j, 2 * px + py, px, py).wait_recv()
            d2d(t, j, 2 * px + py, c).start()
        for t, j, px, py in peers:
            d2d(t, j, 2 * px + py, 1 - c).wait_recv()
        for t, j, px, py in peers:
            ici(t, j, me, px, py).wait_send()
            d2d(t, j, 2 * px + py, c).wait_send()

    return Comm(list(bufs), [SDS(b.shape, b.dtype) for b in bufs], True, [pltpu.SemaphoreType.DMA((3 * n,))] * 4,
                start, finish)


def _sibling_exchange_comm(gs):
    n = len(gs)

    def copies(ins, outs, sems):
        x, y, c = lax.axis_index("x"), lax.axis_index("y"), lax.axis_index("c")
        cps = []
        for t in range(n):
            hr = gs[t].shape[1] // 2
            src = ins[t].at[:, pl.ds(pl.multiple_of((1 - c) * hr, SUBLANES), hr), :]
            cps.append(pltpu.make_async_remote_copy(src_ref=src, dst_ref=outs[t], send_sem=sems[0].at[t],
                                                    recv_sem=sems[1].at[t], device_id=(x, y, 1 - c),
                                                    device_id_type=MESH))
        return cps

    def start(ins, outs, sems):
        for cp in copies(ins, outs, sems):
            cp.start()

    def finish(ins, outs, sems):
        for cp in copies(ins, outs, sems):
            cp.wait()

    return Comm(list(gs), [SDS((N_CHIPS, g.shape[1] // 2, g.shape[2]), g.dtype) for g in gs], False,
                [pltpu.SemaphoreType.DMA((n,))] * 2, start, finish)


def _chip_exchange_comm(ss):
    n = len(ss)

    def copies(ins, outs, sems):
        x, y, c = lax.axis_index("x"), lax.axis_index("y"), lax.axis_index("c")
        me = 2 * x + y

        def copy(t, j, px, py, src_slot, dst_slot):
            return pltpu.make_async_remote_copy(src_ref=ins[t].at[src_slot], dst_ref=outs[t].at[dst_slot],
                                                send_sem=sems[0].at[3 * t + j], recv_sem=sems[1].at[3 * t + j],
                                                device_id=(px, py, c), device_id_type=MESH)

        peers = [(t, j, px, py) for t in range(n) for j, (px, py) in enumerate(_chips(x, y))]
        return copy, peers, me

    def start(ins, outs, sems):
        copy, peers, me = copies(ins, outs, sems)
        for t, j, px, py in peers:
            copy(t, j, px, py, 2 * px + py, me).start()

    def finish(ins, outs, sems):
        copy, peers, me = copies(ins, outs, sems)
        for t, j, px, py in peers:
            copy(t, j, px, py, me, 2 * px + py).wait_recv()
        for t, j, px, py in peers:
            copy(t, j, px, py, 2 * px + py, me).wait_send()

    return Comm(list(ss), [SDS(s.shape, s.dtype) for s in ss], False, [pltpu.SemaphoreType.DMA((3 * n,))] * 2,
                start, finish)


def _sibling_swap_comm(fs):
    n = len(fs)

    def copies(ins, outs, sems):
        x, y, c = lax.axis_index("x"), lax.axis_index("y"), lax.axis_index("c")
        return [pltpu.make_async_remote_copy(src_ref=ins[t], dst_ref=outs[t], send_sem=sems[0].at[t],
                                             recv_sem=sems[1].at[t], device_id=(x, y, 1 - c), device_id_type=MESH)
                for t in range(n)]

    def start(ins, outs, sems):
        for cp in copies(ins, outs, sems):
            cp.start()

    def finish(ins, outs, sems):
        for cp in copies(ins, outs, sems):
            cp.wait()

    return Comm(list(fs), [SDS(f.shape, f.dtype) for f in fs], False, [pltpu.SemaphoreType.DMA((n,))] * 2,
                start, finish)


def _row_tile(rows, cap=512):
    best = SUBLANES
    for tr in range(SUBLANES, min(rows, cap) + 1, SUBLANES):
        if rows % tr == 0:
            best = tr
    return best


def _add_halves(g4, r4, c_idx, name):
    _, hr, lanes = r4.shape
    tr = _row_tile(hr)
    nb = hr // tr

    def body(c_ref, a_ref, b_ref, o_ref):
        o_ref[...] = (a_ref[...] + b_ref[...]).astype(o_ref.dtype)

    grid_spec = pltpu.PrefetchScalarGridSpec(
        num_scalar_prefetch=1, grid=(N_CHIPS, nb),
        in_specs=[pl.BlockSpec((None, tr, lanes), lambda q, i, c_ref: (q, c_ref[0] * nb + i, 0)),
                  pl.BlockSpec((None, tr, lanes), lambda q, i, c_ref: (q, i, 0))],
        out_specs=pl.BlockSpec((None, tr, lanes), lambda q, i, c_ref: (q, i, 0)))
    return pl.pallas_call(body, name=name, grid_spec=grid_spec, out_shape=SDS(r4.shape, BF16),
                          compiler_params=_params(("parallel", "parallel")))(c_idx, g4, r4)


def _sum_chips(r4, s4, me_idx, name):
    _, rows, lanes = r4.shape
    tr = _row_tile(rows)

    def body(me_ref, a_ref, b_ref, c_ref, d_ref, own_ref, o_ref):
        own = own_ref[...].astype(F32)
        p = [jnp.where(me_ref[0] == q, own, ref[...].astype(F32)) for q, ref in enumerate((a_ref, b_ref, c_ref, d_ref))]
        o_ref[...] = ((p[0] + p[1]) + p[2]) + p[3]

    other = lambda q: (lambda i, me_ref: (jnp.where(me_ref[0] == q, (q + 1) % N_CHIPS, q), i, 0))
    grid_spec = pltpu.PrefetchScalarGridSpec(
        num_scalar_prefetch=1, grid=(rows // tr,),
        in_specs=[pl.BlockSpec((None, tr, lanes), other(q)) for q in range(N_CHIPS)]
        + [pl.BlockSpec((None, tr, lanes), lambda i, me_ref: (me_ref[0], i, 0))],
        out_specs=pl.BlockSpec((tr, lanes), lambda i, me_ref: (i, 0)))
    return pl.pallas_call(body, name=name, grid_spec=grid_spec, out_shape=SDS((rows, lanes), F32),
                          compiler_params=_params(("parallel",)))(me_idx, r4, r4, r4, r4, s4)


def _adamw(wf, g_own, g_other, mf, vf, c_idx, name):
    rows, lanes = wf.shape
    hr = rows // 2
    tr = _row_tile(hr)
    nb = hr // tr
    c1 = 1.0 / (1.0 - ADAM_B1 ** ADAM_STEP)
    c2 = 1.0 / (1.0 - ADAM_B2 ** ADAM_STEP)

    def body(c_ref, w_ref, go_ref, gx_ref, m_ref, v_ref, g_ref, d_ref, nm_ref, nv_ref):
        gv = jnp.where(pl.program_id(0) == c_ref[0], go_ref[...], gx_ref[...])
        m = ADAM_B1 * m_ref[...] + (1.0 - ADAM_B1) * gv
        v = ADAM_B2 * v_ref[...] + (1.0 - ADAM_B2) * (gv * gv)
        g_ref[...] = gv
        d_ref[...] = -ADAM_LR * ((m * c1) / (jnp.sqrt(v * c2) + ADAM_EPS) + ADAM_WD * w_ref[...])
        nm_ref[...] = m
        nv_ref[...] = v

    full = pl.BlockSpec((tr, lanes), lambda h, i, c_ref: (h * nb + i, 0))
    half = pl.BlockSpec((tr, lanes), lambda h, i, c_ref: (i, 0))
    grid_spec = pltpu.PrefetchScalarGridSpec(num_scalar_prefetch=1, grid=(2, nb),
                                             in_specs=[full, half, half, full, full], out_specs=[full] * 4)
    return pl.pallas_call(body, name=name, grid_spec=grid_spec, out_shape=[SDS((rows, lanes), F32)] * 4,
                          compiler_params=_params(("parallel", "parallel")))(c_idx, wf, g_own, g_other, mf, vf)


BIG = ("ffn1_w_gate", "ffn1_w_up", "ffn1_w_down", "ffn2_w_gate", "ffn2_w_up", "ffn2_w_down", "w_out", "w_in")
TRANSPOSED = ("ffn1_w_gate", "ffn1_w_up", "ffn2_w_gate", "ffn2_w_up")
PACKED = ("rwkv_w2", "rwkv_a2", "rwkv_g2")
SMALL_SHAPES = {"ffn1_norm": (1, D_MODEL), "mix_norm": (1, D_MODEL), "hgrn_lb_logits": (2, W_A),
                "hgrn_out_norm": (1, W_A), "rwkv_shift_mu": (1, N_RWKV_COLS), "rwkv_w0": (1, W_B),
                "rwkv_a0": (1, W_B), "rwkv_k_k": (1, W_B), "rwkv_k_a": (1, W_B),
                "rwkv_r_k": (1, HB_HEADS, HB_DIM), "rwkv_gn_w": (1, W_B), "rwkv_gn_b": (1, W_B),
                "ffn2_norm": (1, D_MODEL), "final_norm": (D_MODEL,)}
PACK_ELEMS = sum(_numel(_shard_shape(n)) for n in PACKED) + sum(_numel(SMALL_SHAPES[n]) for n in SMALL)
PACK_ROWS = -(-PACK_ELEMS // (32 * LANES)) * 32


def _to_rows(name, shard):
    return shard[0].T if name in TRANSPOSED else shard[0]


def _from_rows(name, rows):
    return (rows.T if name in TRANSPOSED else rows)[None]


def _pack(sharded, small):
    flat = jnp.concatenate([sharded[n].reshape(-1) for n in PACKED] + [small[n].reshape(-1) for n in SMALL])
    return jnp.pad(flat, (0, PACK_ROWS * LANES - flat.shape[0])).reshape(PACK_ROWS, LANES)


def _unpack(packed):
    flat, out, off = packed.reshape(-1), {}, 0
    for n in PACKED:
        shp = _shard_shape(n)
        out[n] = flat[off:off + _numel(shp)].reshape((1,) + shp)
        off += _numel(shp)
    for n in SMALL:
        shp = SMALL_SHAPES[n]
        out[n] = flat[off:off + _numel(shp)].reshape(shp)
        off += _numel(shp)
    return out


def _quarter(full, name, q):
    shape, ax = SHARDED_SHAPES[name]
    w = shape[ax] // N_CHIPS
    return lax.slice_in_dim(full, q * w, (q + 1) * w, axis=ax)


def kernel(x, ffn1_norm, ffn1_w_gate, ffn1_w_up, ffn1_w_down, mix_norm, w_in, hgrn_lb_logits, hgrn_out_norm, rwkv_shift_mu, rwkv_w0, rwkv_w2, rwkv_a0, rwkv_a2, rwkv_g2, rwkv_k_k, rwkv_k_a, rwkv_r_k, rwkv_gn_w, rwkv_gn_b, w_out, ffn2_norm, ffn2_w_gate, ffn2_w_up, ffn2_w_down, final_norm, loss_target, m_ffn1_norm, m_ffn1_w_gate, m_ffn1_w_up, m_ffn1_w_down, m_mix_norm, m_w_in, m_hgrn_lb_logits, m_hgrn_out_norm, m_rwkv_shift_mu, m_rwkv_w0, m_rwkv_w2, m_rwkv_a0, m_rwkv_a2, m_rwkv_g2, m_rwkv_k_k, m_rwkv_k_a, m_rwkv_r_k, m_rwkv_gn_w, m_rwkv_gn_b, m_w_out, m_ffn2_norm, m_ffn2_w_gate, m_ffn2_w_up, m_ffn2_w_down, m_final_norm, v_ffn1_norm, v_ffn1_w_gate, v_ffn1_w_up, v_ffn1_w_down, v_mix_norm, v_w_in, v_hgrn_lb_logits, v_hgrn_out_norm, v_rwkv_shift_mu, v_rwkv_w0, v_rwkv_w2, v_rwkv_a0, v_rwkv_a2, v_rwkv_g2, v_rwkv_k_k, v_rwkv_k_a, v_rwkv_r_k, v_rwkv_gn_w, v_rwkv_gn_b, v_w_out, v_ffn2_norm, v_ffn2_w_gate, v_ffn2_w_up, v_ffn2_w_down, v_final_norm):
    args = dict(locals())
    wts = {n: args[n] for n in ALL_WEIGHTS}
    moms = {n: args["m_" + n] for n in ALL_WEIGHTS}
    vars_ = {n: args["v_" + n] for n in ALL_WEIGHTS}

    me = 2 * lax.axis_index("x") + lax.axis_index("y")
    c_idx = lax.axis_index("c").astype(jnp.int32).reshape(1)
    me_idx = me.astype(jnp.int32).reshape(1)
    shard_of = {n: _to_rows(n, wts[n]).astype(BF16) for n in BIG}
    shard_of["packed"] = _pack(wts, {n: wts[n] for n in SMALL}).astype(BF16)
    group = {"ffn1": BIG[0:3], "ffn2": BIG[3:6]}

    def slot_bufs(names):
        return [lax.dynamic_update_slice(jnp.zeros((N_CHIPS,) + shard_of[n].shape, BF16), shard_of[n][None],
                                         (me, 0, 0)) for n in names]

    def ffn_weights(tag, gathered):
        return {f"{tag}_wgt": gathered[0].reshape(D_FF, D_MODEL), f"{tag}_wut": gathered[1].reshape(D_FF, D_MODEL),
                f"{tag}_wd": gathered[2].reshape(D_FF, D_MODEL)}

    def w_in_weights(gathered):
        w_in_full = jnp.concatenate([gathered[0][q] for q in range(N_CHIPS)], axis=1)
        return {"w_in_h": w_in_full[:, :N_HGRN_COLS],
                "w_in_r": jnp.pad(w_in_full[:, N_HGRN_COLS:], ((0, 0), (0, N_RWKV_PAD - N_RWKV_COLS)))}

    def mixer_weights(gathered):
        w_out_full = gathered[0].reshape(D_MODEL, D_MODEL)
        packs = gathered[1].reshape(N_CHIPS, PACK_ROWS * LANES)
        full, off = {}, 0
        for n in PACKED:
            shp = _shard_shape(n)
            full[n] = jnp.concatenate([packs[q, off:off + _numel(shp)].reshape(shp) for q in range(N_CHIPS)], axis=1)
            off += _numel(shp)
        zrow = lambda nrow: jnp.zeros((nrow, W_B), BF16)
        return {"w_out_a": w_out_full[:W_A], "w_out_b": w_out_full[W_A:],
                "w2_pad": jnp.concatenate([full["rwkv_w2"], zrow(LORA_PAD - 32)], axis=0),
                "a2_pad": jnp.concatenate([zrow(32), full["rwkv_a2"], zrow(LORA_PAD - 64)], axis=0),
                "g2_pad": jnp.concatenate([zrow(64), full["rwkv_g2"], zrow(LORA_PAD - 160)], axis=0)}

    plan = _Plan()
    w = {}
    plan.carry("ffn1_rms", lambda g: _gather_comm(slot_bufs(group["ffn1"][:2])),
               lambda res, w_: w_.update({"ffn1_wgt": res[0].reshape(D_FF, D_MODEL),
                                          "ffn1_wut": res[1].reshape(D_FF, D_MODEL)}))

    def after_gate_up(res, w_):
        w_["ffn1_wd"] = res[0].reshape(D_FF, D_MODEL)
        w_.update(w_in_weights(res[1:]))

    plan.carry("ffn1_gate_up", lambda g: _gather_comm(slot_bufs(("ffn1_w_down", "w_in"))), after_gate_up)
    plan.carry("ffn1_down", lambda g: _gather_comm(slot_bufs(("w_out", "packed"))),
               lambda res, w_: w_.update(mixer_weights(res)))
    plan.carry("rwkv_fwd", lambda g: _gather_comm(slot_bufs(group["ffn2"])),
               lambda res, w_: w_.update(ffn_weights("ffn2", res)))
    w["ffn1_norm"], w["ffn2_norm"] = ffn1_norm, ffn2_norm
    w["mix_norm"] = mix_norm
    w["lb0"], w["lb1"] = hgrn_lb_logits[0:1], hgrn_lb_logits[1:2]
    w["hgrn_out_norm"] = hgrn_out_norm
    w["mu_pad"] = jnp.pad(rwkv_shift_mu, ((0, 0), (0, N_RWKV_PAD - N_RWKV_COLS)))
    for n in ("rwkv_w0", "rwkv_a0", "rwkv_k_k", "rwkv_k_a", "rwkv_gn_w", "rwkv_gn_b"):
        w[n] = wts[n]
    w["rwkv_r_k"] = rwkv_r_k.reshape(1, W_B)
    w["final_norm"] = final_norm.reshape(1, D_MODEL)

    def reduce_rows(names, gs):
        r1 = _run_comm(_sibling_exchange_comm(gs), "grad_sibling_exchange")
        s4 = [_add_halves(gt, rt, c_idx, f"grad_add_halves_{n}") for gt, rt, n in zip(gs, r1, names)]
        r2 = _run_comm(_chip_exchange_comm(s4), "grad_chip_exchange")
        return [_sum_chips(rt, st, me_idx, f"grad_sum_chips_{n}") for rt, st, n in zip(r2, s4, names)]

    early = {}

    def reduce_early(names, grads_of, sibling_host, chips_host):
        def sibling_comm(g):
            early[names, "gs"] = grads_of(g)
            return _sibling_exchange_comm(early[names, "gs"])

        def after_sibling(res, w_):
            early[names, "s4"] = [_add_halves(gt, rt, c_idx, f"grad_add_halves_{n}")
                                  for gt, rt, n in zip(early[names, "gs"], res, names)]

        def after_chips(res, w_):
            early.update(zip(names, [_sum_chips(rt, st, me_idx, f"grad_sum_chips_{n}")
                                     for rt, st, n in zip(res, early[names, "s4"], names)]))

        plan.carry(sibling_host, sibling_comm, after_sibling)
        plan.carry(chips_host, lambda g: _chip_exchange_comm(early[names, "s4"]), after_chips)

    def proj_grads(g):
        g_w_in = jnp.concatenate([g["w_in_h"], g["w_in_r"][:, :N_RWKV_COLS]], axis=1)
        return [jnp.concatenate([g["w_out_a"], g["w_out_b"]], axis=0).reshape(N_CHIPS, -1, D_MODEL),
                jnp.stack([_quarter(g_w_in, "w_in", q) for q in range(N_CHIPS)])]

    rows_of = lambda keys: (lambda g: [g[k].reshape(N_CHIPS, -1, D_MODEL) for k in keys])
    reduce_early(group["ffn2"], rows_of(("ffn2_wgt", "ffn2_wut", "ffn2_wd")), "hgrn_bwd", "rwkv_bwd")
    reduce_early(("w_out", "w_in"), proj_grads, "mix_drms", "ffn1_dact")
    reduce_early(("ffn1_w_down",), rows_of(("ffn1_wd",)), "ffn1_dwg", "ffn1_dwu")
    reduce_early(("ffn1_w_gate",), rows_of(("ffn1_wgt",)), "ffn1_dwu", "ffn1_dh_g")
    reduce_early(("ffn1_w_up",), rows_of(("ffn1_wut",)), "ffn1_dh_g", "ffn1_dh_u")
    loss_slab, grad_x, g = _local_step(x[0], loss_target[0], w, plan)
    loss = lax.psum(loss_slab[0, 0], ("x", "y", "c"))

    gfull = {
        "rwkv_w2": g["w2_pad"][0:32], "rwkv_a2": g["a2_pad"][32:64], "rwkv_g2": g["g2_pad"][64:160],
    }
    gsmall = {
        "ffn1_norm": g["ffn1_norm"], "mix_norm": g["mix_norm"],
        "hgrn_lb_logits": jnp.concatenate([g["lb0"], g["lb1"]], axis=0), "hgrn_out_norm": g["hgrn_out_norm"],
        "rwkv_shift_mu": g["mu_pad"][:, :N_RWKV_COLS], "rwkv_w0": g["rwkv_w0"], "rwkv_a0": g["rwkv_a0"],
        "rwkv_k_k": g["rwkv_k_k"], "rwkv_k_a": g["rwkv_k_a"], "rwkv_r_k": g["rwkv_r_k"],
        "rwkv_gn_w": g["rwkv_gn_w"], "rwkv_gn_b": g["rwkv_gn_b"], "ffn2_norm": g["ffn2_norm"],
        "final_norm": g["final_norm"],
    }
    packed = jnp.stack([_pack({n: _quarter(gfull[n], n, q) for n in PACKED}, gsmall) for q in range(N_CHIPS)])
    early["packed"], = reduce_rows(["packed"], [packed])
    names = list(BIG) + ["packed"]
    own = [early[n] for n in names]
    other = _run_comm(_sibling_swap_comm(own), "grad_sibling_swap")

    def rows_list(d):
        return [_to_rows(n, d[n]) for n in BIG] + [_pack(d, {n: d[n] for n in SMALL})]

    outs = [_adamw(wt, go, gx, mt, vt, c_idx, f"adamw_{n}")
            for wt, go, gx, mt, vt, n in zip(rows_list(wts), own, other, rows_list(moms), rows_list(vars_), names)]
    results = []
    for k in range(4):
        per = [outs[i][k] for i in range(len(names))]
        d = {n: _from_rows(n, z) for n, z in zip(BIG, per[:-1])}
        d.update(_unpack(per[-1]))
        results.append(d)
    return (loss, grad_x[None], *[r[n] for r in results for n in ALL_WEIGHTS])
```

```python
import collections
import functools

import jax
import jax.numpy as jnp
from jax import lax
from jax.experimental import pallas as pl
from jax.experimental.pallas import tpu as pltpu

F32 = jnp.float32
BF16 = jnp.bfloat16
SDS = jax.ShapeDtypeStruct
MESH = pl.DeviceIdType.MESH

D_MODEL = 1024
D_FF = 2816
W_A = 512
W_B = 512
HA_HEADS, HA_DIM = 4, 128
HB_HEADS, HB_DIM = 8, 64
HGRN_CHUNK = 64
HGRN_GROUP = 2
RWKV_CHUNK = 16
RWKV_GROUP = 4
N_HGRN_COLS = 4 * W_A
N_RWKV_COLS = 3 * W_B + 32 + 32 + 96
N_RWKV_PAD = 1792
LORA_PAD = 256
NORM_EPS = 1e-6
RWKV_GN_EPS = 64e-5
L2_EPS = 1e-12
ADAM_LR, ADAM_B1, ADAM_B2, ADAM_EPS, ADAM_WD, ADAM_STEP = 0.001, 0.9, 0.999, 1e-8, 0.01, 10

N_CHIPS = 4
VMEM_LIMIT_V7X = 56 * 1024 * 1024
LANES = 1024

SHARDED_SHAPES = {
    "ffn1_w_gate": ((D_MODEL, D_FF), 1), "ffn1_w_up": ((D_MODEL, D_FF), 1), "ffn1_w_down": ((D_FF, D_MODEL), 0),
    "w_in": ((D_MODEL, N_HGRN_COLS + N_RWKV_COLS), 1), "rwkv_w2": ((32, W_B), 1), "rwkv_a2": ((32, W_B), 1),
    "rwkv_g2": ((96, W_B), 1), "w_out": ((D_MODEL, D_MODEL), 0),
    "ffn2_w_gate": ((D_MODEL, D_FF), 1), "ffn2_w_up": ((D_MODEL, D_FF), 1), "ffn2_w_down": ((D_FF, D_MODEL), 0),
}
SMALL = ("ffn1_norm", "mix_norm", "hgrn_lb_logits", "hgrn_out_norm", "rwkv_shift_mu", "rwkv_w0", "rwkv_a0",
         "rwkv_k_k", "rwkv_k_a", "rwkv_r_k", "rwkv_gn_w", "rwkv_gn_b", "ffn2_norm", "final_norm")
ALL_WEIGHTS = ("ffn1_norm", "ffn1_w_gate", "ffn1_w_up", "ffn1_w_down", "mix_norm", "w_in", "hgrn_lb_logits",
               "hgrn_out_norm", "rwkv_shift_mu", "rwkv_w0", "rwkv_w2", "rwkv_a0", "rwkv_a2", "rwkv_g2", "rwkv_k_k",
               "rwkv_k_a", "rwkv_r_k", "rwkv_gn_w", "rwkv_gn_b", "w_out", "ffn2_norm", "ffn2_w_gate", "ffn2_w_up",
               "ffn2_w_down", "final_norm")


def _shard_shape(name):
    shape, ax = SHARDED_SHAPES[name]
    return tuple(s // N_CHIPS if i == ax else s for i, s in enumerate(shape))


def _numel(shape):
    n = 1
    for s in shape:
        n *= s
    return n


def _params(sem=None):
    return pltpu.CompilerParams(dimension_semantics=sem, vmem_limit_bytes=VMEM_LIMIT_V7X)


def _split2(x):
    hi = x.astype(BF16)
    return hi, (x.astype(F32) - hi.astype(F32)).astype(BF16)


def _dg(x, y, cx, cy, hi):
    dn = (((cx,), (cy,)), ((), ()))
    dot = lambda p, q: lax.dot_general(p, q, dn, preferred_element_type=F32)
    if hi == "x3":
        (xh, xl), (yh, yl) = _split2(x), _split2(y)
        return dot(xh, yh) + (dot(xh, yl) + dot(xl, yh))
    return dot(x.astype(BF16), y.astype(BF16))


def _make_mm(hi, cotangent_forms=None):
    @jax.custom_vjp
    def nn(x, y):
        return _dg(x, y, 1, 0, hi)

    @jax.custom_vjp
    def nt(x, y):
        return _dg(x, y, 1, 1, hi)

    @jax.custom_vjp
    def tn(x, y):
        return _dg(x, y, 0, 0, hi)

    bnn, bnt, btn = cotangent_forms or (nn, nt, tn)
    nn.defvjp(lambda x, y: (nn(x, y), (x, y)), lambda r, g: (bnt(g, r[1]), btn(r[0], g)))
    nt.defvjp(lambda x, y: (nt(x, y), (x, y)), lambda r, g: (bnn(g, r[1]), btn(g, r[0])))
    tn.defvjp(lambda x, y: (tn(x, y), (x, y)), lambda r, g: (bnt(r[1], g), bnn(r[0], g)))
    return nn, nt, tn


_nn, _nt, _tn = _make_mm(False)
_nn_x3, _nt_x3, _tn_x3 = _make_mm("x3", (_nn, _nt, _tn))


def _tri_apply(x, transpose):
    c = x.shape[0]
    tri = (lax.broadcasted_iota(jnp.int32, (c, c), 1) <= lax.broadcasted_iota(jnp.int32, (c, c), 0)).astype(BF16)
    dn = (((0 if transpose else 1,), (0,)), ((), ()))
    p1, p2 = _split2(x)
    dot = lambda p: lax.dot_general(tri, p, dn, preferred_element_type=F32)
    return dot(p1) + dot(p2)


@jax.custom_vjp
def _cumsum_rows(x):
    return _tri_apply(x, False)


_cumsum_rows.defvjp(lambda x: (_tri_apply(x, False), None), lambda _, g: (_tri_apply(g, True),))


def _sigmoid(x):
    return 1.0 / (1.0 + jnp.exp(-x))


def _silu(x):
    return x * _sigmoid(x)


def _softplus(z):
    return jnp.maximum(z, 0.0) + jnp.log(1.0 + jnp.exp(-jnp.abs(z)))


def _mm(a, b, *, ta=False, tb=False, tm, tn, tk, name, out_dtype=F32, res=None, scale=None, comm=None):
    m = a.shape[1] if ta else a.shape[0]
    kdim = a.shape[0] if ta else a.shape[1]
    n = b.shape[0] if tb else b.shape[1]
    assert (b.shape[1] if tb else b.shape[0]) == kdim
    tm, tn, tk = min(tm, m), min(tn, n), min(tk, kdim)
    assert m % tm == 0 and n % tn == 0 and kdim % tk == 0, (name, m, n, kdim)
    nk = kdim // tk
    a_spec = pl.BlockSpec((tk, tm), lambda i, j, k: (k, i)) if ta else pl.BlockSpec((tm, tk), lambda i, j, k: (i, k))
    b_spec = pl.BlockSpec((tn, tk), lambda i, j, k: (j, k)) if tb else pl.BlockSpec((tk, tn), lambda i, j, k: (k, j))
    o_spec = pl.BlockSpec((tm, tn), lambda i, j, k: (i, j))
    ca, cb = (0 if ta else 1), (1 if tb else 0)

    def body(*refs):
        if res is not None:
            a_ref, b_ref, r_ref, o_ref, acc_ref = refs
        else:
            a_ref, b_ref, o_ref, acc_ref = refs
        k = pl.program_id(2)

        @pl.when(k == 0)
        def _():
            acc_ref[...] = jnp.zeros_like(acc_ref)

        acc_ref[...] += _dg(a_ref[...], b_ref[...], ca, cb, False)

        @pl.when(k == nk - 1)
        def _():
            acc = acc_ref[...]
            if scale is not None:
                acc = acc * scale
            if res is not None:
                acc = r_ref[...] + acc
            o_ref[...] = acc.astype(out_dtype)

    in_specs = [a_spec, b_spec] + ([o_spec] if res is not None else [])
    args = (a, b) + ((res,) if res is not None else ())
    if comm is None:
        return pl.pallas_call(
            body, name=name, grid=(m // tm, n // tn, nk), in_specs=in_specs, out_specs=o_spec,
            out_shape=SDS((m, n), out_dtype), scratch_shapes=[pltpu.VMEM((tm, tn), F32)],
            compiler_params=_params(("parallel", "parallel", "arbitrary")))(*args)
    (out,), carried = _hosting_call(
        body, comm, name=name, grid=(m // tm, n // tn, nk), in_specs=in_specs, out_specs=[o_spec],
        out_shape=[SDS((m, n), out_dtype)], scratch_shapes=[pltpu.VMEM((tm, tn), F32)], args=args)
    return out, carried


def _row_spec(x, tm):
    if isinstance(x, tuple):
        arr, w, j = x
        return arr, pl.BlockSpec((tm, w), lambda i, j=j: (i, j))
    return x, pl.BlockSpec((tm, x.shape[1]), lambda i: (i, 0))


def _par_spec(p):
    if isinstance(p, tuple):
        arr, w, j = p
        return arr, pl.BlockSpec((arr.shape[0], w), lambda i, j=j: (0, j))
    return p, pl.BlockSpec(p.shape, lambda i: (0, 0))


def _store_groups(refs, groups, vals):
    for ref, idxs in zip(refs, groups):
        off = 0
        for ix in idxs:
            v = vals[ix]
            ref[:, off:off + v.shape[1]] = v.astype(ref.dtype)
            off += v.shape[1]


SUBLANES = 8


def _x_plan(xs, tm, t):
    arrays, specs, plan = [], [], []
    nb = tm // SUBLANES
    for x in xs:
        if isinstance(x, tuple) and isinstance(x[0], str):
            kind, arr, w, j = x
            if kind == "prev":
                halo = lambda i, j=j: (jnp.maximum(i * nb - 1, 0), j)
            else:
                halo = lambda i, j=j: (jnp.minimum((i + 1) * nb, t // SUBLANES - 1), j)
            arrays += [arr, arr]
            specs += [pl.BlockSpec((tm, w), lambda i, j=j: (i, j)), pl.BlockSpec((SUBLANES, w), halo)]
            plan.append((kind, 2, w))
        else:
            arr, spec = _row_spec(x, tm)
            arrays.append(arr)
            specs.append(spec)
            plan.append(("plain", 1, spec.block_shape[1]))
    return arrays, specs, plan


def _x_vals(refs, plan, tm, nt):
    vals, k = [], 0
    i = pl.program_id(0)
    rows = lax.broadcasted_iota(jnp.int32, (tm, 1), 0)
    for kind, n, _ in plan:
        main = refs[k][...].astype(F32)
        if kind == "prev":
            edge = jnp.where(i == 0, 0.0, refs[k + 1][SUBLANES - 1:SUBLANES, :].astype(F32))
            main = jnp.where(rows == 0, edge, pltpu.roll(main, 1, 0))
        elif kind == "next":
            edge = jnp.where(i == nt - 1, 0.0, refs[k + 1][0:1, :].astype(F32))
            main = jnp.where(rows == tm - 1, edge, pltpu.roll(main, tm - 1, 0))
        vals.append(main)
        k += n
    return vals


def _tile_rows(xs, tm):
    arr = xs[0]
    if isinstance(arr, tuple):
        arr = arr[1] if isinstance(arr[0], str) else arr[0]
    return min(tm, arr.shape[0]), arr.shape[0]


def _rowwise(f, xs, params, out_groups, out_dtypes, *, tm, name, comm=None):
    tm, t = _tile_rows(xs, tm)
    nt = t // tm
    xa, xspecs, plan = _x_plan(xs, tm, t)
    pa, pspecs = (zip(*[_par_spec(p) for p in params]) if params else ((), ()))
    nxr, npar = len(xa), len(pa)
    x_sds = [SDS((tm, w), F32) for _, _, w in plan]
    p_sds = [SDS(s.block_shape, F32) for s in pspecs]
    outs_sds = jax.eval_shape(lambda *vals: f(*vals), *x_sds, *p_sds)
    widths = [sum(outs_sds[ix].shape[1] for ix in idxs) for idxs in out_groups]

    def body(*refs):
        vals = _x_vals(refs[:nxr], plan, tm, nt) + [r[...].astype(F32) for r in refs[nxr:nxr + npar]]
        outs = f(*vals)
        _store_groups(refs[nxr + npar:], out_groups, outs)

    res, carried = _hosting_call(
        body, comm, name=name, grid=(nt,), in_specs=list(xspecs) + list(pspecs),
        out_specs=[pl.BlockSpec((tm, w), lambda i: (i, 0)) for w in widths],
        out_shape=[SDS((t, w), dt) for w, dt in zip(widths, out_dtypes)], scratch_shapes=[], args=(*xa, *pa))
    return res if comm is None else (res, carried)


def _rowwise_bwd(f, xs, params, cots, *, x_grad, p_grad, dx_groups, dx_dtypes, tm, name, extra=None, comm=None):
    tm, t = _tile_rows(xs, tm)
    nt = t // tm
    xa, xspecs, plan = _x_plan(xs, tm, t)
    pa, pspecs = (zip(*[_par_spec(p) for p in params]) if params else ((), ()))
    ca, cspecs = zip(*[_row_spec(c, tm) for c in cots])
    extra = extra or {}
    ekeys = sorted(extra)
    ea, especs = (zip(*[_row_spec(extra[k], tm) for k in ekeys]) if ekeys else ((), ()))
    nx, nxr, npar, nc, ne = len(plan), len(xa), len(pa), len(ca), len(ea)
    gx = [i for i in range(nx) if x_grad[i]]
    gp = [i for i in range(npar) if p_grad[i]]
    widths = [sum(plan[gx[ix]][2] for ix in idxs) for idxs in dx_groups]
    ng = len(dx_groups)

    def body(*refs):
        ins = refs[:nxr + npar + nc + ne]
        outs = refs[nxr + npar + nc + ne:]
        vals = _x_vals(ins[:nxr], plan, tm, nt) + [r[...].astype(F32) for r in ins[nxr:nxr + npar]]
        cvals = tuple(r[...].astype(F32) for r in ins[nxr + npar:nxr + npar + nc])
        evals = [r[...].astype(F32) for r in ins[nxr + npar + nc:]]
        diff_idx = gx + [nx + i for i in gp]

        def g(*dargs):
            full = list(vals)
            for ix, v in zip(diff_idx, dargs):
                full[ix] = v
            return tuple(f(*full))

        _, vjp = jax.vjp(g, *[vals[ix] for ix in diff_idx])
        grads = vjp(cvals)
        dxs = list(grads[:len(gx)])
        for k, ev in zip(ekeys, evals):
            dxs[k] = dxs[k] + ev
        _store_groups(outs[:ng], dx_groups, dxs)
        i = pl.program_id(0)
        for ref, gval in zip(outs[ng:], grads[len(gx):]):
            @pl.when(i == 0)
            def _(ref=ref):
                ref[...] = jnp.zeros_like(ref)
            ref[...] += gval

    dp_specs = [pl.BlockSpec(pspecs[i].block_shape, lambda i: (0, 0)) for i in gp]
    dp_shapes = [SDS(pspecs[i].block_shape, F32) for i in gp]
    res, carried = _hosting_call(
        body, comm, name=name, grid=(nt,), in_specs=list(xspecs) + list(pspecs) + list(cspecs) + list(especs),
        out_specs=[pl.BlockSpec((tm, w), lambda i: (i, 0)) for w in widths] + dp_specs,
        out_shape=[SDS((t, w), dt) for w, dt in zip(widths, dx_dtypes)] + dp_shapes, scratch_shapes=[],
        args=(*xa, *pa, *ca, *ea))
    return res if comm is None else (res, carried)


def _rms_f(x, g):
    return (x * lax.rsqrt(jnp.mean(x * x, axis=-1, keepdims=True) + NORM_EPS) * g,)


def _group_sum_impl(x, ones_bd):
    p1, p2 = _split2(x)
    dot = lambda p: lax.dot_general(p, ones_bd.astype(BF16), (((1,), (0,)), ((), ())), preferred_element_type=F32)
    return dot(p1) + dot(p2)


@jax.custom_vjp
def _group_sum(x, ones_bd):
    return _group_sum_impl(x, ones_bd)


_group_sum.defvjp(lambda x, o: (_group_sum_impl(x, o), o),
                  lambda o, g: (_group_sum_impl(g, o), jnp.zeros_like(o)))


def _rwkv_prep_f(r, k, v, lo, rp, kp, vp, lop, mu_r, mu_k, mu_v, mu_lo, w0, w2p, a0, a2p, g2p, k_k, k_a, ones_bd):
    r = r + mu_r * (rp - r)
    k = k + mu_k * (kp - k)
    v = v + mu_v * (vp - v)
    lo = lo + mu_lo * (lop - lo)
    w_log = -_softplus(-(w0 + _nn(jnp.tanh(lo), w2p))) - 0.5
    lw = -jnp.exp(w_log)
    a_g = _sigmoid(a0 + _nn(lo, a2p))
    g = _nn(_sigmoid(lo), g2p)
    kk = k * k_k
    kk = kk / jnp.maximum(jnp.sqrt(_group_sum(kk * kk, ones_bd)), L2_EPS)
    k2 = k * (1.0 + (a_g - 1.0) * k_a)
    return r, lw, k2, v, -kk, kk * a_g, g


def _rwkv_post_f(y, r, k2, v, g, r_k, gn_w, gn_b, ones_bd):
    inv_n = 1.0 / HB_DIM
    mean = _group_sum(y, ones_bd) * inv_n
    yc = y - mean
    var = _group_sum(yc * yc, ones_bd) * inv_n
    yn = yc * lax.rsqrt(var + RWKV_GN_EPS) * gn_w + gn_b
    bonus = _group_sum(r * k2 * r_k, ones_bd) * v
    return ((yn + bonus) * g,)


def _tri(c, strict=False):
    ii = lax.broadcasted_iota(jnp.int32, (c, c), 0)
    jj = lax.broadcasted_iota(jnp.int32, (c, c), 1)
    return (jj < ii) if strict else (jj <= ii)


def _hgrn_step(st0, q_a, f_a, i_a, g_a, l0, l1, onorm):
    nh, nj = len(q_a), len(q_a[0])
    c = q_a[0][0].shape[0]
    combos = [(j, h) for j in range(nj) for h in range(nh)]
    every = lambda fn: {q: fn(q) for q in combos}
    at_ = lambda d: (lambda q: d[q[1]][q[0]])
    qa_, fa_, ia_, ga_ = (at_(z) for z in (q_a, f_a, i_a, g_a))
    incl = _tri(c)
    rows = lax.broadcasted_iota(jnp.int32, (c, 1), 0)
    lb = []
    for h in range(nh):
        mx = jnp.maximum(l0[h], l1[h])
        e0, e1 = jnp.exp(l0[h] - mx), jnp.exp(l1[h] - mx)
        lb.append(e0 / (e0 + e1))
    forget = every(lambda q: lb[q[1]] + (1.0 - lb[q[1]]) * _sigmoid(fa_(q)))
    qs = every(lambda q: _silu(qa_(q)))
    kk = every(lambda q: 1.0 - forget[q])
    lf = every(lambda q: jnp.log(forget[q]))
    bcum = every(lambda q: _cumsum_rows(lf[q]))
    bref = every(lambda q: jnp.sum(jnp.where(rows <= c // 2, lf[q], 0.0), axis=0, keepdims=True))
    blast = every(lambda q: jnp.sum(lf[q], axis=0, keepdims=True))
    scores = every(lambda q: jnp.where(incl, _nt(qs[q] * jnp.exp(bcum[q] - bref[q]),
                                                 kk[q] * jnp.exp(bref[q] - bcum[q])), 0.0))
    intra = every(lambda q: _nn(scores[q], ia_(q)))
    qb = every(lambda q: qs[q] * jnp.exp(bcum[q]))
    upd = every(lambda q: _tn(ia_(q), kk[q] * jnp.exp(blast[q] - bcum[q])))
    dec = every(lambda q: jnp.exp(blast[q]))
    st = list(st0)
    o = {}
    for j in range(nj):
        for h in range(nh):
            o[(j, h)] = intra[(j, h)] + _nt(qb[(j, h)], st[h])
        st = [st[h] * dec[(j, h)] + upd[(j, h)] for h in range(nh)]
    out = every(lambda q: o[q] * lax.rsqrt(jnp.mean(o[q] * o[q], axis=-1, keepdims=True) + NORM_EPS)
                * onorm[q[1]] * _silu(ga_(q)))
    return [[out[(j, h)] for j in range(nj)] for h in range(nh)], st


def _hgrn_blocks(ref, nj, c):
    return [[ref[j * c:(j + 1) * c, h * HA_DIM:(h + 1) * HA_DIM] for j in range(nj)] for h in range(HA_HEADS)]


def _hgrn_cols(ref):
    return [ref[:, h * HA_DIM:(h + 1) * HA_DIM] for h in range(HA_HEADS)]


def _hgrn_fwd(p_h, l0, l1, onorm):
    t = p_h.shape[0]
    cc, nj = HGRN_CHUNK, HGRN_GROUP
    c = cc * nj
    n = t // c

    def body(q_ref, f_ref, i_ref, g_ref, l0_ref, l1_ref, on_ref, o_ref, hs_ref, st_ref):
        @pl.when(pl.program_id(0) == 0)
        def _():
            st_ref[...] = jnp.zeros_like(st_ref)

        hs_ref[0] = st_ref[...]
        o, st1 = _hgrn_step([st_ref[h] for h in range(HA_HEADS)],
                            *[_hgrn_blocks(ref, nj, cc) for ref in (q_ref, f_ref, i_ref, g_ref)],
                            _hgrn_cols(l0_ref), _hgrn_cols(l1_ref), _hgrn_cols(on_ref))
        for h in range(HA_HEADS):
            for j in range(nj):
                o_ref[j * cc:(j + 1) * cc, h * HA_DIM:(h + 1) * HA_DIM] = o[h][j]
            st_ref[h] = st1[h]

    col = lambda j: pl.BlockSpec((c, W_A), lambda i, j=j: (i, j))
    par = pl.BlockSpec((1, W_A), lambda i: (0, 0))
    return pl.pallas_call(
        body, name="hgrn_fwd", grid=(n,), in_specs=[col(0), col(1), col(2), col(3), par, par, par],
        out_specs=[pl.BlockSpec((c, W_A), lambda i: (i, 0)),
                   pl.BlockSpec((1, HA_HEADS, HA_DIM, HA_DIM), lambda i: (i, 0, 0, 0))],
        out_shape=[SDS((t, W_A), F32), SDS((n, HA_HEADS, HA_DIM, HA_DIM), F32)],
        scratch_shapes=[pltpu.VMEM((HA_HEADS, HA_DIM, HA_DIM), F32)],
        compiler_params=_params(("arbitrary",)))(p_h, p_h, p_h, p_h, l0, l1, onorm)


def _hgrn_bwd(p_h, l0, l1, onorm, hs, do, do_col, comm=None):
    t = p_h.shape[0]
    cc, nj = HGRN_CHUNK, HGRN_GROUP
    c = cc * nj
    n = t // c

    def body(q_ref, f_ref, i_ref, g_ref, l0_ref, l1_ref, on_ref, hs_ref, do_ref,
             dp_ref, dl0_ref, dl1_ref, don_ref, dst_ref):
        @pl.when(pl.program_id(0) == 0)
        def _():
            dst_ref[...] = jnp.zeros_like(dst_ref)
            dl0_ref[...] = jnp.zeros_like(dl0_ref)
            dl1_ref[...] = jnp.zeros_like(dl1_ref)
            don_ref[...] = jnp.zeros_like(don_ref)

        args = ([hs_ref[0, h] for h in range(HA_HEADS)],
                *[_hgrn_blocks(ref, nj, cc) for ref in (q_ref, f_ref, i_ref, g_ref)],
                _hgrn_cols(l0_ref), _hgrn_cols(l1_ref), _hgrn_cols(on_ref))
        _, vjp = jax.vjp(_hgrn_step, *args)
        dst0, dq, df, di, dg, dl0, dl1, don = vjp((_hgrn_blocks(do_ref, nj, cc),
                                                   [dst_ref[h] for h in range(HA_HEADS)]))
        for h in range(HA_HEADS):
            sl = slice(h * HA_DIM, (h + 1) * HA_DIM)
            for k, dv in enumerate((dq, df, di, dg)):
                for j in range(nj):
                    dp_ref[j * cc:(j + 1) * cc, k * W_A + h * HA_DIM:k * W_A + (h + 1) * HA_DIM] = dv[h][j]
            dl0_ref[:, sl] += dl0[h]
            dl1_ref[:, sl] += dl1[h]
            don_ref[:, sl] += don[h]
            dst_ref[h] = dst0[h]

    col = lambda j: pl.BlockSpec((c, W_A), lambda i, j=j: (n - 1 - i, j))
    par = pl.BlockSpec((1, W_A), lambda i: (0, 0))
    return _hosting_call(
        body, comm, name="hgrn_bwd", grid=(n,),
        in_specs=[col(0), col(1), col(2), col(3), par, par, par,
                  pl.BlockSpec((1, HA_HEADS, HA_DIM, HA_DIM), lambda i: (n - 1 - i, 0, 0, 0)),
                  pl.BlockSpec((c, W_A), lambda i: (n - 1 - i, do_col))],
        out_specs=[pl.BlockSpec((c, N_HGRN_COLS), lambda i: (n - 1 - i, 0)), par, par, par],
        out_shape=[SDS((t, N_HGRN_COLS), F32), SDS((1, W_A), F32), SDS((1, W_A), F32), SDS((1, W_A), F32)],
        scratch_shapes=[pltpu.VMEM((HA_HEADS, HA_DIM, HA_DIM), F32)],
        args=(p_h, p_h, p_h, p_h, l0, l1, onorm, hs, do))


HB_PAIRS = HB_HEADS // 2
PAIR_W = 2 * HB_DIM


def _head_lane_masks():
    lane = lax.broadcasted_iota(jnp.int32, (1, PAIR_W), 1)
    return (lane < HB_DIM).astype(F32), (lane >= HB_DIM).astype(F32)


@jax.custom_vjp
def _stack_heads(x):
    m0, m1 = _head_lane_masks()
    return jnp.concatenate([x * m0, x * m1], axis=0)


def _stack_heads_bwd(_, g):
    m0, m1 = _head_lane_masks()
    c = g.shape[0] // 2
    return (g[:c] * m0 + g[c:] * m1,)


_stack_heads.defvjp(lambda x: (_stack_heads(x), None), _stack_heads_bwd)


@jax.custom_vjp
def _unstack_heads(ys):
    c = ys.shape[0] // 2
    return ys[:c] + ys[c:]


_unstack_heads.defvjp(lambda ys: (_unstack_heads(ys), None), lambda _, g: (_stack_heads(g),))


def _same_head_block(c):
    ii = lax.broadcasted_iota(jnp.int32, (2 * c, 2 * c), 0)
    jj = lax.broadcasted_iota(jnp.int32, (2 * c, 2 * c), 1)
    same = (ii < c) == (jj < c)
    return same & (jj <= ii), same & (jj < ii), (ii == jj).astype(F32)


@jax.custom_vjp
def _rows_join(top, bottom):
    return jnp.concatenate([top, bottom], axis=0)


def _rows_join_bwd(n_top, g):
    return g[:n_top], g[n_top:]


_rows_join.defvjp(lambda top, bottom: (_rows_join(top, bottom), top.shape[0]), _rows_join_bwd)


def _rows_split_impl(x, n_top):
    return x[:n_top], x[n_top:]


_rows_split = jax.custom_vjp(_rows_split_impl, nondiff_argnums=(1,))
_rows_split.defvjp(lambda x, n_top: (_rows_split_impl(x, n_top), None),
                   lambda n_top, _, g: (jnp.concatenate([g[0], g[1]], axis=0),))


def _rwkv_step(s0, r, lw, k, v, a, b):
    npair, nj = len(r), len(r[0])
    c = r[0][0].shape[0]
    combos = [(j, p) for j in range(nj) for p in range(npair)]
    every = lambda fn: {q: fn(q) for q in combos}
    at_ = lambda d: (lambda q: d[q[1]][q[0]])
    r_, lw_, k_, v_, a_, b_ = (at_(z) for z in (r, lw, k, v, a, b))
    incl, strict, eye = _same_head_block(c)

    gam = every(lambda q: _cumsum_rows(lw_(q)))
    gtot = every(lambda q: jnp.sum(lw_(q), axis=0, keepdims=True))
    eneg = every(lambda q: jnp.exp(-gam[q]))
    edec = every(lambda q: jnp.exp(gtot[q] - gam[q]))
    at = every(lambda q: _stack_heads(a_(q) * jnp.exp(gam[q] - lw_(q))))
    rt = every(lambda q: _stack_heads(r_(q) * jnp.exp(gam[q])))
    bt = every(lambda q: _stack_heads(b_(q) * eneg[q]))
    kt = every(lambda q: _stack_heads(k_(q) * eneg[q]))
    bdec = every(lambda q: _stack_heads(b_(q) * edec[q]))
    kdec = every(lambda q: _stack_heads(k_(q) * edec[q]))
    vs = every(lambda q: _stack_heads(v_(q)))
    a_ab = every(lambda q: jnp.where(strict, _nt(at[q], bt[q]), 0.0))
    a_ak = every(lambda q: jnp.where(strict, _nt(at[q], kt[q]), 0.0))
    a_rb = every(lambda q: jnp.where(incl, _nt(rt[q], bt[q]), 0.0))
    a_rk = every(lambda q: jnp.where(incl, _nt(rt[q], kt[q]), 0.0))
    tinv = every(lambda q: eye + a_ab[q])
    pw = a_ab
    span = 2
    while span < c:
        pw = every(lambda q, pw=pw: _nn_x3(pw[q], pw[q]))
        tinv = every(lambda q, pw=pw, tinv=tinv: tinv[q] + _nn_x3(pw[q], tinv[q]))
        span *= 2
    akv = every(lambda q: _nn(a_ak[q], vs[q]))
    w1 = every(lambda q: _nn(tinv[q], at[q]))
    u0 = every(lambda q: _nn(tinv[q], akv[q]))
    wr = every(lambda q: _rows_join(w1[q], rt[q]))
    bk = every(lambda q: _rows_join(bdec[q], kdec[q]))
    yv = every(lambda q: _nn(a_rk[q], vs[q]))
    gdec = every(lambda q: jnp.exp(gtot[q]))

    s = list(s0)
    y = [[None] * nj for _ in range(npair)]
    for j in range(nj):
        both = {p: _rows_split(_nt(wr[(j, p)], s[p]), 2 * c) for p in range(npair)}
        u = {p: both[p][0] + u0[(j, p)] for p in range(npair)}
        for p in range(npair):
            y[p][j] = _unstack_heads(both[p][1] + _nn(a_rb[(j, p)], u[p]) + yv[(j, p)])
        s = [s[p] * gdec[(j, p)] + _tn(_rows_join(u[p], vs[(j, p)]), bk[(j, p)]) for p in range(npair)]
    return y, s


def _rwkv_blocks(ref, nj, c):
    return [[ref[j * c:(j + 1) * c, p * PAIR_W:(p + 1) * PAIR_W] for j in range(nj)] for p in range(HB_PAIRS)]


def _rwkv_fwd(seqs, comm=None):
    t = seqs[0].shape[0]
    c, nj = RWKV_CHUNK, RWKV_GROUP
    n = t // (c * nj)

    def body(r_ref, lw_ref, k_ref, v_ref, a_ref, b_ref, y_ref, hs_ref, st_ref):
        @pl.when(pl.program_id(0) == 0)
        def _():
            st_ref[...] = jnp.zeros_like(st_ref)

        hs_ref[0] = st_ref[...]
        s0 = [st_ref[p] for p in range(HB_PAIRS)]
        y, s1 = _rwkv_step(s0, *[_rwkv_blocks(ref, nj, c) for ref in (r_ref, lw_ref, k_ref, v_ref, a_ref, b_ref)])
        for p in range(HB_PAIRS):
            for j in range(nj):
                y_ref[j * c:(j + 1) * c, p * PAIR_W:(p + 1) * PAIR_W] = y[p][j]
            st_ref[p] = s1[p]

    seq = pl.BlockSpec((c * nj, W_B), lambda i: (i, 0))
    return _hosting_call(
        body, comm, name="rwkv_fwd", grid=(n,), in_specs=[seq] * 6,
        out_specs=[seq, pl.BlockSpec((1, HB_PAIRS, PAIR_W, PAIR_W), lambda i: (i, 0, 0, 0))],
        out_shape=[SDS((t, W_B), F32), SDS((n, HB_PAIRS, PAIR_W, PAIR_W), F32)],
        scratch_shapes=[pltpu.VMEM((HB_PAIRS, PAIR_W, PAIR_W), F32)], args=tuple(seqs))


def _rwkv_bwd(seqs, hs, dy, comm=None):
    t = seqs[0].shape[0]
    c, nj = RWKV_CHUNK, RWKV_GROUP
    n = t // (c * nj)

    def body(r_ref, lw_ref, k_ref, v_ref, a_ref, b_ref, hs_ref, dy_ref,
             dr_ref, dlw_ref, dk_ref, dv_ref, da_ref, db_ref, dst_ref):
        @pl.when(pl.program_id(0) == 0)
        def _():
            dst_ref[...] = jnp.zeros_like(dst_ref)

        s0 = [hs_ref[0, p] for p in range(HB_PAIRS)]
        seq_vals = [_rwkv_blocks(ref, nj, c) for ref in (r_ref, lw_ref, k_ref, v_ref, a_ref, b_ref)]
        _, vjp = jax.vjp(_rwkv_step, s0, *seq_vals)
        grads = vjp((_rwkv_blocks(dy_ref, nj, c), [dst_ref[p] for p in range(HB_PAIRS)]))
        for ref, gr in zip((dr_ref, dlw_ref, dk_ref, dv_ref, da_ref, db_ref), grads[1:]):
            for p in range(HB_PAIRS):
                for j in range(nj):
                    ref[j * c:(j + 1) * c, p * PAIR_W:(p + 1) * PAIR_W] = gr[p][j]
        m0, m1 = _head_lane_masks()
        rows0 = (lax.broadcasted_iota(jnp.int32, (PAIR_W, 1), 0) < HB_DIM).astype(F32)
        blocks = rows0 * m0 + (1.0 - rows0) * m1
        for p in range(HB_PAIRS):
            dst_ref[p] = grads[0][p] * blocks

    seq = pl.BlockSpec((c * nj, W_B), lambda i: (n - 1 - i, 0))
    return _hosting_call(
        body, comm, name="rwkv_bwd", grid=(n,),
        in_specs=[seq] * 6 + [pl.BlockSpec((1, HB_PAIRS, PAIR_W, PAIR_W), lambda i: (n - 1 - i, 0, 0, 0)), seq],
        out_specs=[seq] * 6, out_shape=[SDS((t, W_B), F32)] * 6,
        scratch_shapes=[pltpu.VMEM((HB_PAIRS, PAIR_W, PAIR_W), F32)], args=(*seqs, hs, dy))


def _final_loss(x3, fnorm, target, *, tm):
    t, d = x3.shape

    def body(x_ref, g_ref, t_ref, dx_ref, dg_ref, loss_ref):
        @pl.when(pl.program_id(0) == 0)
        def _():
            dg_ref[...] = jnp.zeros_like(dg_ref)
            loss_ref[...] = jnp.zeros_like(loss_ref)

        x, g = x_ref[...], g_ref[...]
        rinv = lax.rsqrt(jnp.mean(x * x, axis=-1, keepdims=True) + NORM_EPS)
        xh = x * rinv
        diff = xh * g - t_ref[...]
        loss_ref[...] += 0.5 * jnp.sum(jnp.mean(diff * diff, axis=-1, keepdims=True))
        dy = diff * (1.0 / d)
        dg_ref[...] += jnp.sum(dy * xh, axis=0, keepdims=True)
        dxh = dy * g
        dx_ref[...] = rinv * (dxh - xh * jnp.mean(dxh * xh, axis=-1, keepdims=True))

    row = pl.BlockSpec((tm, d), lambda i: (i, 0))
    return pl.pallas_call(
        body, name="final_loss", grid=(t // tm,), in_specs=[row, pl.BlockSpec((1, d), lambda i: (0, 0)), row],
        out_specs=[row, pl.BlockSpec((1, d), lambda i: (0, 0)), pl.BlockSpec((8, 128), lambda i: (0, 0))],
        out_shape=[SDS((t, d), F32), SDS((1, d), F32), SDS((8, 128), F32)],
        compiler_params=_params(("arbitrary",)))(x3, fnorm, target)


def _gate_up_act(h, wgt, wut, *, tm, tn, name, comm=None):
    t, d = h.shape
    tm = min(tm, t)

    def body(h_ref, g_ref, u_ref, da_out, du_out, act_out):
        hv = h_ref[...]
        a = _dg(hv, g_ref[...], 1, 1, False)
        u = _dg(hv, u_ref[...], 1, 1, False)
        s = _sigmoid(a)
        silu = a * s
        da_out[...] = (u * (s * (1.0 + a * (1.0 - s)))).astype(da_out.dtype)
        du_out[...] = silu.astype(du_out.dtype)
        act_out[...] = (silu * u).astype(act_out.dtype)

    wspec = pl.BlockSpec((tn, d), lambda i, j: (j, 0))
    ospec = pl.BlockSpec((tm, tn), lambda i, j: (i, j))
    return _hosting_call(
        body, comm, name=name, grid=(t // tm, D_FF // tn),
        in_specs=[pl.BlockSpec((tm, d), lambda i, j: (i, 0)), wspec, wspec], out_specs=[ospec, ospec, ospec],
        out_shape=[SDS((t, D_FF), BF16), SDS((t, D_FF), BF16), SDS((t, D_FF), BF16)], scratch_shapes=[],
        args=(h, wgt, wut))


def _dact_swiglu(dout, wd, act_da, act_du, *, tm, tn, name, comm=None):
    t, d = dout.shape
    tm = min(tm, t)

    def body(d_ref, w_ref, fa_ref, fu_ref, da_out, du_out):
        dact = 0.5 * _dg(d_ref[...], w_ref[...], 1, 1, False)
        da_out[...] = (dact * fa_ref[...].astype(F32)).astype(da_out.dtype)
        du_out[...] = (dact * fu_ref[...].astype(F32)).astype(du_out.dtype)

    tile = pl.BlockSpec((tm, tn), lambda i, j: (i, j))
    return _hosting_call(
        body, comm, name=name, grid=(t // tm, D_FF // tn),
        in_specs=[pl.BlockSpec((tm, d), lambda i, j: (i, 0)), pl.BlockSpec((tn, d), lambda i, j: (j, 0)), tile, tile],
        out_specs=[tile, tile], out_shape=[SDS((t, D_FF), BF16), SDS((t, D_FF), BF16)], scratch_shapes=[],
        args=(dout, wd, act_da, act_du))


class _Plan:
    def __init__(self):
        self.entries, self.counts = collections.defaultdict(list), {}

    def carry(self, host, comm_of, after):
        self.entries[host].append((comm_of, after))

    def comm(self, host, g):
        comms = [comm_of(g) for comm_of, _ in self.entries.get(host, [])]
        self.counts[host] = [len(c.arrays) for c in comms]
        return functools.reduce(_join_comms, comms) if comms else None

    def done(self, host, results, w):
        start = 0
        for (_, after), n in zip(self.entries.get(host, []), self.counts.get(host, [])):
            after(results[start:start + n], w)
            start += n


def _ffn_fwd(x, w, tag, plan, g):
    comm = plan.comm(f"{tag}_rms", g)
    res = _rowwise(_rms_f, [x], [w[f"{tag}_norm"]], [[0]], [BF16], tm=512, name=f"{tag}_rms", comm=comm)
    (h,), carried = res if comm is not None else (res, [])
    plan.done(f"{tag}_rms", carried, w)
    (fa, fu, act), carried = _gate_up_act(h, w[f"{tag}_wgt"], w[f"{tag}_wut"], tm=2048, tn=256, name=f"{tag}_gate_up",
                                        comm=plan.comm(f"{tag}_gate_up", g))
    plan.done(f"{tag}_gate_up", carried, w)
    comm = plan.comm(f"{tag}_down", g)
    out = _mm(act, w[f"{tag}_wd"], tm=1024, tn=D_MODEL, tk=D_FF, name=f"{tag}_down", res=x, scale=0.5, comm=comm)
    if comm is not None:
        out, carried = out
        plan.done(f"{tag}_down", carried, w)
    return out, (h, fa, fu, act)


def _ffn_bwd(dout, x, w, saved, tag, plan, g):
    h, fa, fu, act = saved

    def carrying(fn, host, *args, **kwargs):
        comm = plan.comm(host, g)
        res = fn(*args, name=host, comm=comm, **kwargs)
        out, carried = res if comm is not None else (res, [])
        plan.done(host, carried, w)
        return out

    (da, du), carried = _dact_swiglu(dout, w[f"{tag}_wd"], fa, fu, tm=2048, tn=256, name=f"{tag}_dact",
                                     comm=plan.comm(f"{tag}_dact", g))
    plan.done(f"{tag}_dact", carried, w)
    g[f"{tag}_wd"] = _mm(act, dout, ta=True, tm=D_FF // 2, tn=D_MODEL, tk=1024, name=f"{tag}_dwd", scale=0.5)
    g[f"{tag}_wgt"] = carrying(_mm, f"{tag}_dwg", da, h, ta=True, tm=D_FF // 2, tn=D_MODEL, tk=1024)
    g[f"{tag}_wut"] = carrying(_mm, f"{tag}_dwu", du, h, ta=True, tm=D_FF // 2, tn=D_MODEL, tk=1024)
    dh = carrying(_mm, f"{tag}_dh_g", da, w[f"{tag}_wgt"], tm=1024, tn=D_MODEL, tk=D_FF)
    dh = carrying(_mm, f"{tag}_dh_u", du, w[f"{tag}_wut"], tm=1024, tn=D_MODEL, tk=D_FF, res=dh)
    dx, g[f"{tag}_norm"] = _rowwise_bwd(_rms_f, [x], [w[f"{tag}_norm"]], [dh], x_grad=[True], p_grad=[True],
                                        dx_groups=[[0]], dx_dtypes=[F32], tm=512, name=f"{tag}_drms",
                                        extra={0: dout})
    return dx


def _local_step(x, target, w, plan=None):
    plan = plan or _Plan()
    ones_bd = jnp.kron(jnp.eye(HB_HEADS, dtype=F32), jnp.ones((HB_DIM, HB_DIM), F32))
    g = {}
    x1, ffn1_saved = _ffn_fwd(x, w, "ffn1", plan, g)
    hm, = _rowwise(_rms_f, [x1], [w["mix_norm"]], [[0]], [BF16], tm=512, name="mix_rms")
    p_h = _mm(hm, w["w_in_h"], tm=2048, tn=256, tk=D_MODEL, name="inproj_h")
    p_r = _mm(hm, w["w_in_r"], tm=2048, tn=256, tk=D_MODEL, name="inproj_r")
    o_a, hgrn_states = _hgrn_fwd(p_h, w["lb0"], w["lb1"], w["hgrn_out_norm"])

    mu = w["mu_pad"]
    prep_xs = [(p_r, W_B, 0), (p_r, W_B, 1), (p_r, W_B, 2), (p_r, LORA_PAD, 6),
               ("prev", p_r, W_B, 0), ("prev", p_r, W_B, 1), ("prev", p_r, W_B, 2), ("prev", p_r, LORA_PAD, 6)]
    prep_ps = [(mu, W_B, 0), (mu, W_B, 1), (mu, W_B, 2), (mu, LORA_PAD, 6), w["rwkv_w0"], w["w2_pad"], w["rwkv_a0"],
               w["a2_pad"], w["g2_pad"], w["rwkv_k_k"], w["rwkv_k_a"], ones_bd]
    prep_f = _rwkv_prep_f
    r, lw, k2, v, a_vec, b_vec, gate = _rowwise(prep_f, prep_xs, prep_ps, [[0], [1], [2], [3], [4], [5], [6]],
                                                [F32] * 7, tm=256, name="rwkv_prep")
    seqs = [r, lw, k2, v, a_vec, b_vec]
    (y, rwkv_states), carried = _rwkv_fwd(seqs, comm=plan.comm("rwkv_fwd", g))
    plan.done("rwkv_fwd", carried, w)
    post_f = _rwkv_post_f
    post_xs = [y, r, k2, v, gate]
    post_ps = [w["rwkv_r_k"], w["rwkv_gn_w"], w["rwkv_gn_b"], ones_bd]
    o_b, = _rowwise(post_f, post_xs, post_ps, [[0]], [F32], tm=256, name="rwkv_post")
    x2 = _mm(o_a, w["w_out_a"], tm=2048, tn=256, tk=W_A, name="outproj_a", res=x1)
    x2 = _mm(o_b, w["w_out_b"], tm=2048, tn=256, tk=W_B, name="outproj_b", res=x2)
    x3, ffn2_saved = _ffn_fwd(x2, w, "ffn2", plan, g)
    dx3, g["final_norm"], loss = _final_loss(x3, w["final_norm"], target, tm=256)

    dx2 = _ffn_bwd(dx3, x2, w, ffn2_saved, "ffn2", plan, g)
    do_a = _mm(dx2, w["w_out_a"], tb=True, tm=2048, tn=256, tk=D_MODEL, name="outproj_do_a")
    do_b = _mm(dx2, w["w_out_b"], tb=True, tm=2048, tn=256, tk=D_MODEL, name="outproj_do_b")
    g["w_out_a"] = _mm(o_a, dx2, ta=True, tm=W_A, tn=D_MODEL, tk=1024, name="outproj_dw_a")
    g["w_out_b"] = _mm(o_b, dx2, ta=True, tm=W_B, tn=D_MODEL, tk=1024, name="outproj_dw_b")

    (dp_h, g["lb0"], g["lb1"], g["hgrn_out_norm"]), carried = _hgrn_bwd(
        p_h, w["lb0"], w["lb1"], w["hgrn_out_norm"], hgrn_states, do_a, 0, comm=plan.comm("hgrn_bwd", g))
    plan.done("hgrn_bwd", carried, w)
    post_out = _rowwise_bwd(post_f, post_xs, post_ps, [do_b], x_grad=[True] * 5, p_grad=[True] * 3 + [False],
                            dx_groups=[[0], [1], [2], [3], [4]], dx_dtypes=[F32] * 5, tm=256, name="rwkv_post_bwd")
    dy, dr1, dk1, dv1, dgate, g["rwkv_r_k"], g["rwkv_gn_w"], g["rwkv_gn_b"] = post_out
    (dr2, dlw, dk2, dv2, da_vec, db_vec), carried = _rwkv_bwd(seqs, rwkv_states, dy, comm=plan.comm("rwkv_bwd", g))
    plan.done("rwkv_bwd", carried, w)

    def prep2_f(*vals):
        r_, lw_, k2_, v_, a_, b_, g_ = prep_f(*vals)
        return r_, lw_, k2_, v_, a_, b_, g_, r_, k2_, v_

    prep_out = _rowwise_bwd(prep2_f, prep_xs, prep_ps, [dr2, dlw, dk2, dv2, da_vec, db_vec, dgate, dr1, dk1, dv1],
                            x_grad=[True] * 8, p_grad=[True] * 11 + [False], dx_groups=[[0, 1, 2, 3], [4, 5, 6, 7]],
                            dx_dtypes=[F32, F32], tm=256, name="rwkv_prep_bwd")
    dpr_main, dpr_prev = prep_out[0], prep_out[1]
    (dmu_r, dmu_k, dmu_v, dmu_lo, g["rwkv_w0"], g["w2_pad"], g["rwkv_a0"], g["a2_pad"], g["g2_pad"],
     g["rwkv_k_k"], g["rwkv_k_a"]) = prep_out[2:]
    g["mu_pad"] = jnp.concatenate([dmu_r, dmu_k, dmu_v, dmu_lo], axis=1)
    dp_r, = _rowwise(lambda u_, s_: (u_ + s_,), [dpr_main, ("next", dpr_prev, N_RWKV_PAD, 0)], [], [[0]], [F32],
                     tm=512, name="rwkv_dp_sum")
    dhm = _mm(dp_h, w["w_in_h"], tb=True, tm=1024, tn=D_MODEL, tk=N_HGRN_COLS, name="inproj_dh_h")
    dhm = _mm(dp_r, w["w_in_r"], tb=True, tm=1024, tn=D_MODEL, tk=N_RWKV_PAD, name="inproj_dh_r", res=dhm)
    g["w_in_h"] = _mm(hm, dp_h, ta=True, tm=D_MODEL, tn=D_MODEL, tk=1024, name="inproj_dw_h")
    g["w_in_r"] = _mm(hm, dp_r, ta=True, tm=D_MODEL, tn=N_RWKV_PAD // 2, tk=1024, name="inproj_dw_r")
    mix_comm = plan.comm("mix_drms", g)
    mix_out = _rowwise_bwd(_rms_f, [x1], [w["mix_norm"]], [dhm], x_grad=[True], p_grad=[True], dx_groups=[[0]],
                           dx_dtypes=[F32], tm=512, name="mix_drms", extra={0: dx2}, comm=mix_comm)
    (dx1, g["mix_norm"]), carried = mix_out if mix_comm is not None else (mix_out, [])
    plan.done("mix_drms", carried, w)
    dx0 = _ffn_bwd(dx1, x, w, ffn1_saved, "ffn1", plan, g)
    return loss, dx0, g


HBM_SPEC = pl.BlockSpec(memory_space=pl.ANY)

Comm = collections.namedtuple("Comm", "arrays out_shapes aliased sem_shapes start finish")


def _join_comms(first, second):
    assert first.aliased == second.aliased
    n, s = len(first.arrays), len(first.sem_shapes)

    def start(ins, outs, sems):
        first.start(ins[:n], outs[:n], sems[:s])
        second.start(ins[n:], outs[n:], sems[s:])

    def finish(ins, outs, sems):
        first.finish(ins[:n], outs[:n], sems[:s])
        second.finish(ins[n:], outs[n:], sems[s:])

    return Comm(list(first.arrays) + list(second.arrays), list(first.out_shapes) + list(second.out_shapes),
                first.aliased, list(first.sem_shapes) + list(second.sem_shapes), start, finish)


def _run_comm(comm, name):
    n = len(comm.arrays)

    def body(*refs):
        ins, outs, sems = refs[:n], refs[n:2 * n], refs[2 * n:]
        comm.start(ins, outs, sems)
        comm.finish(ins, outs, sems)

    return pl.pallas_call(
        body, name=name, in_specs=[HBM_SPEC] * n, out_specs=[HBM_SPEC] * n, out_shape=list(comm.out_shapes),
        input_output_aliases={t: t for t in range(n)} if comm.aliased else {},
        scratch_shapes=list(comm.sem_shapes))(*comm.arrays)


def _hosting_call(body, comm, *, name, grid, in_specs, out_specs, out_shape, scratch_shapes, args):
    sem = ("arbitrary",) * len(grid)
    if comm is None:
        res = pl.pallas_call(body, name=name, grid=grid, in_specs=in_specs, out_specs=out_specs, out_shape=out_shape,
                             scratch_shapes=scratch_shapes, compiler_params=_params(sem))(*args)
        return list(res), []
    ni, no, ns, nc = len(in_specs), len(out_specs), len(scratch_shapes), len(comm.arrays)

    def wrapped(*refs):
        ins, cins = refs[:ni], refs[ni:ni + nc]
        outs, couts = refs[ni + nc:ni + nc + no], refs[ni + nc + no:ni + 2 * nc + no]
        scr, sems = refs[ni + 2 * nc + no:ni + 2 * nc + no + ns], refs[ni + 2 * nc + no + ns:]
        first = functools.reduce(jnp.logical_and, [pl.program_id(k) == 0 for k in range(len(grid))])
        last = functools.reduce(jnp.logical_and, [pl.program_id(k) == grid[k] - 1 for k in range(len(grid))])

        @pl.when(first)
        def _():
            comm.start(cins, couts, sems)

        body(*ins, *outs, *scr)

        @pl.when(last)
        def _():
            comm.finish(cins, couts, sems)

    res = pl.pallas_call(
        wrapped, name=name, grid=grid, in_specs=list(in_specs) + [HBM_SPEC] * nc,
        out_specs=list(out_specs) + [HBM_SPEC] * nc, out_shape=list(out_shape) + list(comm.out_shapes),
        scratch_shapes=list(scratch_shapes) + list(comm.sem_shapes),
        input_output_aliases={ni + t: no + t for t in range(nc)} if comm.aliased else {},
        compiler_params=_params(sem))(*args, *comm.arrays)
    return list(res[:no]), list(res[no:])


def _chips(x, y):
    return [(1 - x, y), (x, 1 - y), (1 - x, 1 - y)]


def _gather_comm(bufs):
    n = len(bufs)

    def copies(outs, sems):
        ici_send, ici_recv, d2d_send, d2d_recv = sems
        x, y, c = lax.axis_index("x"), lax.axis_index("y"), lax.axis_index("c")

        def half(t, slot, hc):
            hr = bufs[t].shape[1] // 2
            return outs[t].at[slot, pl.ds(pl.multiple_of(hc * hr, 16), hr), :]

        def ici(t, j, slot, px, py):
            return pltpu.make_async_remote_copy(src_ref=half(t, slot, c), dst_ref=half(t, slot, c),
                                                send_sem=ici_send.at[3 * t + j], recv_sem=ici_recv.at[3 * t + j],
                                                device_id=(px, py, c), device_id_type=MESH)

        def d2d(t, j, slot, hc):
            return pltpu.make_async_remote_copy(src_ref=half(t, slot, hc), dst_ref=half(t, slot, hc),
                                                send_sem=d2d_send.at[3 * t + j], recv_sem=d2d_recv.at[3 * t + j],
                                                device_id=(x, y, 1 - c), device_id_type=MESH)

        peers = [(t, j, px, py) for t in range(n) for j, (px, py) in enumerate(_chips(x, y))]
        return ici, d2d, peers, 2 * x + y, c

    def start(ins, outs, sems):
        ici, _, peers, me, _ = copies(outs, sems)
        for t, j, px, py in peers:
            ici(t, j, me, px, py).start()

    def finish(ins, outs, sems):
        ici, d2d, peers, me, c = copies(outs, sems)
        for t, j, px, py in peers:
            ici(t, j, 2 * px + py, px, py).wait_recv()
            d2d(t, j, 2 * px + py, c).start()
        for t, j, px, py in peers:
            d2d(t, j, 2 * px + py, 1 - c).wait_recv()
        for t, j, px, py in peers:
            ici(t, j, me, px, py).wait_send()
            d2d(t, j, 2 * px + py, c).wait_send()

    return Comm(list(bufs), [SDS(b.shape, b.dtype) for b in bufs], True, [pltpu.SemaphoreType.DMA((3 * n,))] * 4,
                start, finish)


def _sibling_exchange_comm(gs):
    n = len(gs)

    def copies(ins, outs, sems):
        x, y, c = lax.axis_index("x"), lax.axis_index("y"), lax.axis_index("c")
        cps = []
        for t in range(n):
            hr = gs[t].shape[1] // 2
            src = ins[t].at[:, pl.ds(pl.multiple_of((1 - c) * hr, SUBLANES), hr), :]
            cps.append(pltpu.make_async_remote_copy(src_ref=src, dst_ref=outs[t], send_sem=sems[0].at[t],
                                                    recv_sem=sems[1].at[t], device_id=(x, y, 1 - c),
                                                    device_id_type=MESH))
        return cps

    def start(ins, outs, sems):
        for cp in copies(ins, outs, sems):
            cp.start()

    def finish(ins, outs, sems):
        for cp in copies(ins, outs, sems):
            cp.wait()

    return Comm(list(gs), [SDS((N_CHIPS, g.shape[1] // 2, g.shape[2]), g.dtype) for g in gs], False,
                [pltpu.SemaphoreType.DMA((n,))] * 2, start, finish)


def _chip_exchange_comm(ss):
    n = len(ss)

    def copies(ins, outs, sems):
        x, y, c = lax.axis_index("x"), lax.axis_index("y"), lax.axis_index("c")
        me = 2 * x + y

        def copy(t, j, px, py, src_slot, dst_slot):
            return pltpu.make_async_remote_copy(src_ref=ins[t].at[src_slot], dst_ref=outs[t].at[dst_slot],
                                                send_sem=sems[0].at[3 * t + j], recv_sem=sems[1].at[3 * t + j],
                                                device_id=(px, py, c), device_id_type=MESH)

        peers = [(t, j, px, py) for t in range(n) for j, (px, py) in enumerate(_chips(x, y))]
        return copy, peers, me

    def start(ins, outs, sems):
        copy, peers, me = copies(ins, outs, sems)
        for t, j, px, py in peers:
            copy(t, j, px, py, 2 * px + py, me).start()

    def finish(ins, outs, sems):
        copy, peers, me = copies(ins, outs, sems)
        for t, j, px, py in peers:
            copy(t, j, px, py, me, 2 * px + py).wait_recv()
        for t, j, px, py in peers:
            copy(t, j, px, py, 2 * px + py, me).wait_send()

    return Comm(list(ss), [SDS(s.shape, s.dtype) for s in ss], False, [pltpu.SemaphoreType.DMA((3 * n,))] * 2,
                start, finish)


def _sibling_swap_comm(fs):
    n = len(fs)

    def copies(ins, outs, sems):
        x, y, c = lax.axis_index("x"), lax.axis_index("y"), lax.axis_index("c")
        return [pltpu.make_async_remote_copy(src_ref=ins[t], dst_ref=outs[t], send_sem=sems[0].at[t],
                                             recv_sem=sems[1].at[t], device_id=(x, y, 1 - c), device_id_type=MESH)
                for t in range(n)]

    def start(ins, outs, sems):
        for cp in copies(ins, outs, sems):
            cp.start()

    def finish(ins, outs, sems):
        for cp in copies(ins, outs, sems):
            cp.wait()

    return Comm(list(fs), [SDS(f.shape, f.dtype) for f in fs], False, [pltpu.SemaphoreType.DMA((n,))] * 2,
                start, finish)


def _row_tile(rows, cap=512):
    best = SUBLANES
    for tr in range(SUBLANES, min(rows, cap) + 1, SUBLANES):
        if rows % tr == 0:
            best = tr
    return best


def _add_halves(g4, r4, c_idx, name):
    _, hr, lanes = r4.shape
    tr = _row_tile(hr)
    nb = hr // tr

    def body(c_ref, a_ref, b_ref, o_ref):
        o_ref[...] = (a_ref[...] + b_ref[...]).astype(o_ref.dtype)

    grid_spec = pltpu.PrefetchScalarGridSpec(
        num_scalar_prefetch=1, grid=(N_CHIPS, nb),
        in_specs=[pl.BlockSpec((None, tr, lanes), lambda q, i, c_ref: (q, c_ref[0] * nb + i, 0)),
                  pl.BlockSpec((None, tr, lanes), lambda q, i, c_ref: (q, i, 0))],
        out_specs=pl.BlockSpec((None, tr, lanes), lambda q, i, c_ref: (q, i, 0)))
    return pl.pallas_call(body, name=name, grid_spec=grid_spec, out_shape=SDS(r4.shape, BF16),
                          compiler_params=_params(("parallel", "parallel")))(c_idx, g4, r4)


def _sum_chips(r4, s4, me_idx, name):
    _, rows, lanes = r4.shape
    tr = _row_tile(rows)

    def body(me_ref, a_ref, b_ref, c_ref, d_ref, own_ref, o_ref):
        own = own_ref[...].astype(F32)
        p = [jnp.where(me_ref[0] == q, own, ref[...].astype(F32)) for q, ref in enumerate((a_ref, b_ref, c_ref, d_ref))]
        o_ref[...] = ((p[0] + p[1]) + p[2]) + p[3]

    other = lambda q: (lambda i, me_ref: (jnp.where(me_ref[0] == q, (q + 1) % N_CHIPS, q), i, 0))
    grid_spec = pltpu.PrefetchScalarGridSpec(
        num_scalar_prefetch=1, grid=(rows // tr,),
        in_specs=[pl.BlockSpec((None, tr, lanes), other(q)) for q in range(N_CHIPS)]
        + [pl.BlockSpec((None, tr, lanes), lambda i, me_ref: (me_ref[0], i, 0))],
        out_specs=pl.BlockSpec((tr, lanes), lambda i, me_ref: (i, 0)))
    return pl.pallas_call(body, name=name, grid_spec=grid_spec, out_shape=SDS((rows, lanes), F32),
                          compiler_params=_params(("parallel",)))(me_idx, r4, r4, r4, r4, s4)


def _adamw(wf, g_own, g_other, mf, vf, c_idx, name):
    rows, lanes = wf.shape
    hr = rows // 2
    tr = _row_tile(hr)
    nb = hr // tr
    c1 = 1.0 / (1.0 - ADAM_B1 ** ADAM_STEP)
    c2 = 1.0 / (1.0 - ADAM_B2 ** ADAM_STEP)

    def body(c_ref, w_ref, go_ref, gx_ref, m_ref, v_ref, g_ref, d_ref, nm_ref, nv_ref):
        gv = jnp.where(pl.program_id(0) == c_ref[0], go_ref[...], gx_ref[...])
        m = ADAM_B1 * m_ref[...] + (1.0 - ADAM_B1) * gv
        v = ADAM_B2 * v_ref[...] + (1.0 - ADAM_B2) * (gv * gv)
        g_ref[...] = gv
        d_ref[...] = -ADAM_LR * ((m * c1) / (jnp.sqrt(v * c2) + ADAM_EPS) + ADAM_WD * w_ref[...])
        nm_ref[...] = m
        nv_ref[...] = v

    full = pl.BlockSpec((tr, lanes), lambda h, i, c_ref: (h * nb + i, 0))
    half = pl.BlockSpec((tr, lanes), lambda h, i, c_ref: (i, 0))
    grid_spec = pltpu.PrefetchScalarGridSpec(num_scalar_prefetch=1, grid=(2, nb),
                                             in_specs=[full, half, half, full, full], out_specs=[full] * 4)
    return pl.pallas_call(body, name=name, grid_spec=grid_spec, out_shape=[SDS((rows, lanes), F32)] * 4,
                          compiler_params=_params(("parallel", "parallel")))(c_idx, wf, g_own, g_other, mf, vf)


BIG = ("ffn1_w_gate", "ffn1_w_up", "ffn1_w_down", "ffn2_w_gate", "ffn2_w_up", "ffn2_w_down", "w_out", "w_in")
TRANSPOSED = ("ffn1_w_gate", "ffn1_w_up", "ffn2_w_gate", "ffn2_w_up")
PACKED = ("rwkv_w2", "rwkv_a2", "rwkv_g2")
SMALL_SHAPES = {"ffn1_norm": (1, D_MODEL), "mix_norm": (1, D_MODEL), "hgrn_lb_logits": (2, W_A),
                "hgrn_out_norm": (1, W_A), "rwkv_shift_mu": (1, N_RWKV_COLS), "rwkv_w0": (1, W_B),
                "rwkv_a0": (1, W_B), "rwkv_k_k": (1, W_B), "rwkv_k_a": (1, W_B),
                "rwkv_r_k": (1, HB_HEADS, HB_DIM), "rwkv_gn_w": (1, W_B), "rwkv_gn_b": (1, W_B),
                "ffn2_norm": (1, D_MODEL), "final_norm": (D_MODEL,)}
PACK_ELEMS = sum(_numel(_shard_shape(n)) for n in PACKED) + sum(_numel(SMALL_SHAPES[n]) for n in SMALL)
PACK_ROWS = -(-PACK_ELEMS // (32 * LANES)) * 32


def _to_rows(name, shard):
    return shard[0].T if name in TRANSPOSED else shard[0]


def _from_rows(name, rows):
    return (rows.T if name in TRANSPOSED else rows)[None]


def _pack(sharded, small):
    flat = jnp.concatenate([sharded[n].reshape(-1) for n in PACKED] + [small[n].reshape(-1) for n in SMALL])
    return jnp.pad(flat, (0, PACK_ROWS * LANES - flat.shape[0])).reshape(PACK_ROWS, LANES)


def _unpack(packed):
    flat, out, off = packed.reshape(-1), {}, 0
    for n in PACKED:
        shp = _shard_shape(n)
        out[n] = flat[off:off + _numel(shp)].reshape((1,) + shp)
        off += _numel(shp)
    for n in SMALL:
        shp = SMALL_SHAPES[n]
        out[n] = flat[off:off + _numel(shp)].reshape(shp)
        off += _numel(shp)
    return out


def _quarter(full, name, q):
    shape, ax = SHARDED_SHAPES[name]
    w = shape[ax] // N_CHIPS
    return lax.slice_in_dim(full, q * w, (q + 1) * w, axis=ax)


def kernel(x, ffn1_norm, ffn1_w_gate, ffn1_w_up, ffn1_w_down, mix_norm, w_in, hgrn_lb_logits, hgrn_out_norm, rwkv_shift_mu, rwkv_w0, rwkv_w2, rwkv_a0, rwkv_a2, rwkv_g2, rwkv_k_k, rwkv_k_a, rwkv_r_k, rwkv_gn_w, rwkv_gn_b, w_out, ffn2_norm, ffn2_w_gate, ffn2_w_up, ffn2_w_down, final_norm, loss_target, m_ffn1_norm, m_ffn1_w_gate, m_ffn1_w_up, m_ffn1_w_down, m_mix_norm, m_w_in, m_hgrn_lb_logits, m_hgrn_out_norm, m_rwkv_shift_mu, m_rwkv_w0, m_rwkv_w2, m_rwkv_a0, m_rwkv_a2, m_rwkv_g2, m_rwkv_k_k, m_rwkv_k_a, m_rwkv_r_k, m_rwkv_gn_w, m_rwkv_gn_b, m_w_out, m_ffn2_norm, m_ffn2_w_gate, m_ffn2_w_up, m_ffn2_w_down, m_final_norm, v_ffn1_norm, v_ffn1_w_gate, v_ffn1_w_up, v_ffn1_w_down, v_mix_norm, v_w_in, v_hgrn_lb_logits, v_hgrn_out_norm, v_rwkv_shift_mu, v_rwkv_w0, v_rwkv_w2, v_rwkv_a0, v_rwkv_a2, v_rwkv_g2, v_rwkv_k_k, v_rwkv_k_a, v_rwkv_r_k, v_rwkv_gn_w, v_rwkv_gn_b, v_w_out, v_ffn2_norm, v_ffn2_w_gate, v_ffn2_w_up, v_ffn2_w_down, v_final_norm):
    args = dict(locals())
    wts = {n: args[n] for n in ALL_WEIGHTS}
    moms = {n: args["m_" + n] for n in ALL_WEIGHTS}
    vars_ = {n: args["v_" + n] for n in ALL_WEIGHTS}

    me = 2 * lax.axis_index("x") + lax.axis_index("y")
    c_idx = lax.axis_index("c").astype(jnp.int32).reshape(1)
    me_idx = me.astype(jnp.int32).reshape(1)
    shard_of = {n: _to_rows(n, wts[n]).astype(BF16) for n in BIG}
    shard_of["packed"] = _pack(wts, {n: wts[n] for n in SMALL}).astype(BF16)
    group = {"ffn1": BIG[0:3], "ffn2": BIG[3:6]}

    def slot_bufs(names):
        return [lax.dynamic_update_slice(jnp.zeros((N_CHIPS,) + shard_of[n].shape, BF16), shard_of[n][None],
                                         (me, 0, 0)) for n in names]

    def ffn_weights(tag, gathered):
        return {f"{tag}_wgt": gathered[0].reshape(D_FF, D_MODEL), f"{tag}_wut": gathered[1].reshape(D_FF, D_MODEL),
                f"{tag}_wd": gathered[2].reshape(D_FF, D_MODEL)}

    def w_in_weights(gathered):
        w_in_full = jnp.concatenate([gathered[0][q] for q in range(N_CHIPS)], axis=1)
        return {"w_in_h": w_in_full[:, :N_HGRN_COLS],
                "w_in_r": jnp.pad(w_in_full[:, N_HGRN_COLS:], ((0, 0), (0, N_RWKV_PAD - N_RWKV_COLS)))}

    def mixer_weights(gathered):
        w_out_full = gathered[0].reshape(D_MODEL, D_MODEL)
        packs = gathered[1].reshape(N_CHIPS, PACK_ROWS * LANES)
        full, off = {}, 0
        for n in PACKED:
            shp = _shard_shape(n)
            full[n] = jnp.concatenate([packs[q, off:off + _numel(shp)].reshape(shp) for q in range(N_CHIPS)], axis=1)
            off += _numel(shp)
        zrow = lambda nrow: jnp.zeros((nrow, W_B), BF16)
        return {"w_out_a": w_out_full[:W_A], "w_out_b": w_out_full[W_A:],
                "w2_pad": jnp.concatenate([full["rwkv_w2"], zrow(LORA_PAD - 32)], axis=0),
                "a2_pad": jnp.concatenate([zrow(32), full["rwkv_a2"], zrow(LORA_PAD - 64)], axis=0),
                "g2_pad": jnp.concatenate([zrow(64), full["rwkv_g2"], zrow(LORA_PAD - 160)], axis=0)}

    plan = _Plan()
    w = {}
    plan.carry("ffn1_rms", lambda g: _gather_comm(slot_bufs(group["ffn1"][:2])),
               lambda res, w_: w_.update({"ffn1_wgt": res[0].reshape(D_FF, D_MODEL),
                                          "ffn1_wut": res[1].reshape(D_FF, D_MODEL)}))

    def after_gate_up(res, w_):
        w_["ffn1_wd"] = res[0].reshape(D_FF, D_MODEL)
        w_.update(w_in_weights(res[1:]))

    plan.carry("ffn1_gate_up", lambda g: _gather_comm(slot_bufs(("ffn1_w_down", "w_in"))), after_gate_up)
    plan.carry("ffn1_down", lambda g: _gather_comm(slot_bufs(("w_out", "packed"))),
               lambda res, w_: w_.update(mixer_weights(res)))
    plan.carry("rwkv_fwd", lambda g: _gather_comm(slot_bufs(group["ffn2"])),
               lambda res, w_: w_.update(ffn_weights("ffn2", res)))
    w["ffn1_norm"], w["ffn2_norm"] = ffn1_norm, ffn2_norm
    w["mix_norm"] = mix_norm
    w["lb0"], w["lb1"] = hgrn_lb_logits[0:1], hgrn_lb_logits[1:2]
    w["hgrn_out_norm"] = hgrn_out_norm
    w["mu_pad"] = jnp.pad(rwkv_shift_mu, ((0, 0), (0, N_RWKV_PAD - N_RWKV_COLS)))
    for n in ("rwkv_w0", "rwkv_a0", "rwkv_k_k", "rwkv_k_a", "rwkv_gn_w", "rwkv_gn_b"):
        w[n] = wts[n]
    w["rwkv_r_k"] = rwkv_r_k.reshape(1, W_B)
    w["final_norm"] = final_norm.reshape(1, D_MODEL)

    def reduce_rows(names, gs):
        r1 = _run_comm(_sibling_exchange_comm(gs), "grad_sibling_exchange")
        s4 = [_add_halves(gt, rt, c_idx, f"grad_add_halves_{n}") for gt, rt, n in zip(gs, r1, names)]
        r2 = _run_comm(_chip_exchange_comm(s4), "grad_chip_exchange")
        return [_sum_chips(rt, st, me_idx, f"grad_sum_chips_{n}") for rt, st, n in zip(r2, s4, names)]

    early = {}

    def reduce_early(names, grads_of, sibling_host, chips_host):
        def sibling_comm(g):
            early[names, "gs"] = grads_of(g)
            return _sibling_exchange_comm(early[names, "gs"])

        def after_sibling(res, w_):
            early[names, "s4"] = [_add_halves(gt, rt, c_idx, f"grad_add_halves_{n}")
                                  for gt, rt, n in zip(early[names, "gs"], res, names)]

        def after_chips(res, w_):
            early.update(zip(names, [_sum_chips(rt, st, me_idx, f"grad_sum_chips_{n}")
                                     for rt, st, n in zip(res, early[names, "s4"], names)]))

        plan.carry(sibling_host, sibling_comm, after_sibling)
        plan.carry(chips_host, lambda g: _chip_exchange_comm(early[names, "s4"]), after_chips)

    def proj_grads(g):
        g_w_in = jnp.concatenate([g["w_in_h"], g["w_in_r"][:, :N_RWKV_COLS]], axis=1)
        return [jnp.concatenate([g["w_out_a"], g["w_out_b"]], axis=0).reshape(N_CHIPS, -1, D_MODEL),
                jnp.stack([_quarter(g_w_in, "w_in", q) for q in range(N_CHIPS)])]

    rows_of = lambda keys: (lambda g: [g[k].reshape(N_CHIPS, -1, D_MODEL) for k in keys])
    reduce_early(group["ffn2"], rows_of(("ffn2_wgt", "ffn2_wut", "ffn2_wd")), "hgrn_bwd", "rwkv_bwd")
    reduce_early(("w_out", "w_in"), proj_grads, "mix_drms", "ffn1_dact")
    reduce_early(("ffn1_w_down",), rows_of(("ffn1_wd",)), "ffn1_dwg", "ffn1_dwu")
    reduce_early(("ffn1_w_gate",), rows_of(("ffn1_wgt",)), "ffn1_dwu", "ffn1_dh_g")
    reduce_early(("ffn1_w_up",), rows_of(("ffn1_wut",)), "ffn1_dh_g", "ffn1_dh_u")
    loss_slab, grad_x, g = _local_step(x[0], loss_target[0], w, plan)
    loss = lax.psum(loss_slab[0, 0], ("x", "y", "c"))

    gfull = {
        "rwkv_w2": g["w2_pad"][0:32], "rwkv_a2": g["a2_pad"][32:64], "rwkv_g2": g["g2_pad"][64:160],
    }
    gsmall = {
        "ffn1_norm": g["ffn1_norm"], "mix_norm": g["mix_norm"],
        "hgrn_lb_logits": jnp.concatenate([g["lb0"], g["lb1"]], axis=0), "hgrn_out_norm": g["hgrn_out_norm"],
        "rwkv_shift_mu": g["mu_pad"][:, :N_RWKV_COLS], "rwkv_w0": g["rwkv_w0"], "rwkv_a0": g["rwkv_a0"],
        "rwkv_k_k": g["rwkv_k_k"], "rwkv_k_a": g["rwkv_k_a"], "rwkv_r_k": g["rwkv_r_k"],
        "rwkv_gn_w": g["rwkv_gn_w"], "rwkv_gn_b": g["rwkv_gn_b"], "ffn2_norm": g["ffn2_norm"],
        "final_norm": g["final_norm"],
    }
    packed = jnp.stack([_pack({n: _quarter(gfull[n], n, q) for n in PACKED}, gsmall) for q in range(N_CHIPS)])
    early["packed"], = reduce_rows(["packed"], [packed])
    names = list(BIG) + ["packed"]
    own = [early[n] for n in names]
    other = _run_comm(_sibling_swap_comm(own), "grad_sibling_swap")

    def rows_list(d):
        return [_to_rows(n, d[n]) for n in BIG] + [_pack(d, {n: d[n] for n in SMALL})]

    outs = [_adamw(wt, go, gx, mt, vt, c_idx, f"adamw_{n}")
            for wt, go, gx, mt, vt, n in zip(rows_list(wts), own, other, rows_list(moms), rows_list(vars_), names)]
    results = []
    for k in range(4):
        per = [outs[i][k] for i in range(len(names))]
        d = {n: _from_rows(n, z) for n, z in zip(BIG, per[:-1])}
        d.update(_unpack(per[-1]))
        results.append(d)
    return (loss, grad_x[None], *[r[n] for r in results for n in ALL_WEIGHTS])
```

```python
import collections
import functools

import jax
import jax.numpy as jnp
from jax import lax
from jax.experimental import pallas as pl
from jax.experimental.pallas import tpu as pltpu

F32 = jnp.float32
BF16 = jnp.bfloat16
SDS = jax.ShapeDtypeStruct
MESH = pl.DeviceIdType.MESH

D_MODEL = 1024
D_FF = 2816
W_A = 512
W_B = 512
HA_HEADS, HA_DIM = 4, 128
HB_HEADS, HB_DIM = 8, 64
HGRN_CHUNK = 64
HGRN_GROUP = 4
RWKV_CHUNK = 16
RWKV_GROUP = 8
N_HGRN_COLS = 4 * W_A
N_RWKV_COLS = 3 * W_B + 32 + 32 + 96
N_RWKV_PAD = 1792
LORA_PAD = 256
NORM_EPS = 1e-6
RWKV_GN_EPS = 64e-5
L2_EPS = 1e-12
ADAM_LR, ADAM_B1, ADAM_B2, ADAM_EPS, ADAM_WD, ADAM_STEP = 0.001, 0.9, 0.999, 1e-8, 0.01, 10

N_CHIPS = 4
VMEM_LIMIT_V7X = 56 * 1024 * 1024
LANES = 1024

SHARDED_SHAPES = {
    "ffn1_w_gate": ((D_MODEL, D_FF), 1), "ffn1_w_up": ((D_MODEL, D_FF), 1), "ffn1_w_down": ((D_FF, D_MODEL), 0),
    "w_in": ((D_MODEL, N_HGRN_COLS + N_RWKV_COLS), 1), "rwkv_w2": ((32, W_B), 1), "rwkv_a2": ((32, W_B), 1),
    "rwkv_g2": ((96, W_B), 1), "w_out": ((D_MODEL, D_MODEL), 0),
    "ffn2_w_gate": ((D_MODEL, D_FF), 1), "ffn2_w_up": ((D_MODEL, D_FF), 1), "ffn2_w_down": ((D_FF, D_MODEL), 0),
}
SMALL = ("ffn1_norm", "mix_norm", "hgrn_lb_logits", "hgrn_out_norm", "rwkv_shift_mu", "rwkv_w0", "rwkv_a0",
         "rwkv_k_k", "rwkv_k_a", "rwkv_r_k", "rwkv_gn_w", "rwkv_gn_b", "ffn2_norm", "final_norm")
ALL_WEIGHTS = ("ffn1_norm", "ffn1_w_gate", "ffn1_w_up", "ffn1_w_down", "mix_norm", "w_in", "hgrn_lb_logits",
               "hgrn_out_norm", "rwkv_shift_mu", "rwkv_w0", "rwkv_w2", "rwkv_a0", "rwkv_a2", "rwkv_g2", "rwkv_k_k",
               "rwkv_k_a", "rwkv_r_k", "rwkv_gn_w", "rwkv_gn_b", "w_out", "ffn2_norm", "ffn2_w_gate", "ffn2_w_up",
               "ffn2_w_down", "final_norm")


def _shard_shape(name):
    shape, ax = SHARDED_SHAPES[name]
    return tuple(s // N_CHIPS if i == ax else s for i, s in enumerate(shape))


def _numel(shape):
    n = 1
    for s in shape:
        n *= s
    return n


def _params(sem=None):
    return pltpu.CompilerParams(dimension_semantics=sem, vmem_limit_bytes=VMEM_LIMIT_V7X)


def _split2(x):
    hi = x.astype(BF16)
    return hi, (x.astype(F32) - hi.astype(F32)).astype(BF16)


def _dg(x, y, cx, cy, hi):
    dn = (((cx,), (cy,)), ((), ()))
    dot = lambda p, q: lax.dot_general(p, q, dn, preferred_element_type=F32)
    if hi == "x3":
        (xh, xl), (yh, yl) = _split2(x), _split2(y)
        return dot(xh, yh) + (dot(xh, yl) + dot(xl, yh))
    return dot(x.astype(BF16), y.astype(BF16))


def _make_mm(hi, cotangent_forms=None):
    @jax.custom_vjp
    def nn(x, y):
        return _dg(x, y, 1, 0, hi)

    @jax.custom_vjp
    def nt(x, y):
        return _dg(x, y, 1, 1, hi)

    @jax.custom_vjp
    def tn(x, y):
        return _dg(x, y, 0, 0, hi)

    bnn, bnt, btn = cotangent_forms or (nn, nt, tn)
    nn.defvjp(lambda x, y: (nn(x, y), (x, y)), lambda r, g: (bnt(g, r[1]), btn(r[0], g)))
    nt.defvjp(lambda x, y: (nt(x, y), (x, y)), lambda r, g: (bnn(g, r[1]), btn(g, r[0])))
    tn.defvjp(lambda x, y: (tn(x, y), (x, y)), lambda r, g: (bnt(r[1], g), bnn(r[0], g)))
    return nn, nt, tn


_nn, _nt, _tn = _make_mm(False)
_nn_x3, _nt_x3, _tn_x3 = _make_mm("x3", (_nn, _nt, _tn))


def _tri_apply(x, transpose):
    c = x.shape[0]
    tri = (lax.broadcasted_iota(jnp.int32, (c, c), 1) <= lax.broadcasted_iota(jnp.int32, (c, c), 0)).astype(BF16)
    dn = (((0 if transpose else 1,), (0,)), ((), ()))
    p1, p2 = _split2(x)
    dot = lambda p: lax.dot_general(tri, p, dn, preferred_element_type=F32)
    return dot(p1) + dot(p2)


@jax.custom_vjp
def _cumsum_rows(x):
    return _tri_apply(x, False)


_cumsum_rows.defvjp(lambda x: (_tri_apply(x, False), None), lambda _, g: (_tri_apply(g, True),))


def _sigmoid(x):
    return 1.0 / (1.0 + jnp.exp(-x))


def _silu(x):
    return x * _sigmoid(x)


def _softplus(z):
    return jnp.maximum(z, 0.0) + jnp.log(1.0 + jnp.exp(-jnp.abs(z)))


def _mm(a, b, *, ta=False, tb=False, tm, tn, tk, name, out_dtype=F32, res=None, scale=None, comm=None):
    m = a.shape[1] if ta else a.shape[0]
    kdim = a.shape[0] if ta else a.shape[1]
    n = b.shape[0] if tb else b.shape[1]
    assert (b.shape[1] if tb else b.shape[0]) == kdim
    tm, tn, tk = min(tm, m), min(tn, n), min(tk, kdim)
    assert m % tm == 0 and n % tn == 0 and kdim % tk == 0, (name, m, n, kdim)
    nk = kdim // tk
    a_spec = pl.BlockSpec((tk, tm), lambda i, j, k: (k, i)) if ta else pl.BlockSpec((tm, tk), lambda i, j, k: (i, k))
    b_spec = pl.BlockSpec((tn, tk), lambda i, j, k: (j, k)) if tb else pl.BlockSpec((tk, tn), lambda i, j, k: (k, j))
    o_spec = pl.BlockSpec((tm, tn), lambda i, j, k: (i, j))
    ca, cb = (0 if ta else 1), (1 if tb else 0)

    def body(*refs):
        if res is not None:
            a_ref, b_ref, r_ref, o_ref, acc_ref = refs
        else:
            a_ref, b_ref, o_ref, acc_ref = refs
        k = pl.program_id(2)

        @pl.when(k == 0)
        def _():
            acc_ref[...] = jnp.zeros_like(acc_ref)

        acc_ref[...] += _dg(a_ref[...], b_ref[...], ca, cb, False)

        @pl.when(k == nk - 1)
        def _():
            acc = acc_ref[...]
            if scale is not None:
                acc = acc * scale
            if res is not None:
                acc = r_ref[...] + acc
            o_ref[...] = acc.astype(out_dtype)

    in_specs = [a_spec, b_spec] + ([o_spec] if res is not None else [])
    args = (a, b) + ((res,) if res is not None else ())
    if comm is None:
        return pl.pallas_call(
            body, name=name, grid=(m // tm, n // tn, nk), in_specs=in_specs, out_specs=o_spec,
            out_shape=SDS((m, n), out_dtype), scratch_shapes=[pltpu.VMEM((tm, tn), F32)],
            compiler_params=_params(("parallel", "parallel", "arbitrary")))(*args)
    (out,), carried = _hosting_call(
        body, comm, name=name, grid=(m // tm, n // tn, nk), in_specs=in_specs, out_specs=[o_spec],
        out_shape=[SDS((m, n), out_dtype)], scratch_shapes=[pltpu.VMEM((tm, tn), F32)], args=args)
    return out, carried


def _row_spec(x, tm):
    if isinstance(x, tuple):
        arr, w, j = x
        return arr, pl.BlockSpec((tm, w), lambda i, j=j: (i, j))
    return x, pl.BlockSpec((tm, x.shape[1]), lambda i: (i, 0))


def _par_spec(p):
    if isinstance(p, tuple):
        arr, w, j = p
        return arr, pl.BlockSpec((arr.shape[0], w), lambda i, j=j: (0, j))
    return p, pl.BlockSpec(p.shape, lambda i: (0, 0))


def _store_groups(refs, groups, vals):
    for ref, idxs in zip(refs, groups):
        off = 0
        for ix in idxs:
            v = vals[ix]
            ref[:, off:off + v.shape[1]] = v.astype(ref.dtype)
            off += v.shape[1]


SUBLANES = 8


def _x_plan(xs, tm, t):
    arrays, specs, plan = [], [], []
    nb = tm // SUBLANES
    for x in xs:
        if isinstance(x, tuple) and isinstance(x[0], str):
            kind, arr, w, j = x
            if kind == "prev":
                halo = lambda i, j=j: (jnp.maximum(i * nb - 1, 0), j)
            else:
                halo = lambda i, j=j: (jnp.minimum((i + 1) * nb, t // SUBLANES - 1), j)
            arrays += [arr, arr]
            specs += [pl.BlockSpec((tm, w), lambda i, j=j: (i, j)), pl.BlockSpec((SUBLANES, w), halo)]
            plan.append((kind, 2, w))
        else:
            arr, spec = _row_spec(x, tm)
            arrays.append(arr)
            specs.append(spec)
            plan.append(("plain", 1, spec.block_shape[1]))
    return arrays, specs, plan


def _x_vals(refs, plan, tm, nt):
    vals, k = [], 0
    i = pl.program_id(0)
    rows = lax.broadcasted_iota(jnp.int32, (tm, 1), 0)
    for kind, n, _ in plan:
        main = refs[k][...].astype(F32)
        if kind == "prev":
            edge = jnp.where(i == 0, 0.0, refs[k + 1][SUBLANES - 1:SUBLANES, :].astype(F32))
            main = jnp.where(rows == 0, edge, pltpu.roll(main, 1, 0))
        elif kind == "next":
            edge = jnp.where(i == nt - 1, 0.0, refs[k + 1][0:1, :].astype(F32))
            main = jnp.where(rows == tm - 1, edge, pltpu.roll(main, tm - 1, 0))
        vals.append(main)
        k += n
    return vals


def _tile_rows(xs, tm):
    arr = xs[0]
    if isinstance(arr, tuple):
        arr = arr[1] if isinstance(arr[0], str) else arr[0]
    return min(tm, arr.shape[0]), arr.shape[0]


def _rowwise(f, xs, params, out_groups, out_dtypes, *, tm, name, comm=None):
    tm, t = _tile_rows(xs, tm)
    nt = t // tm
    xa, xspecs, plan = _x_plan(xs, tm, t)
    pa, pspecs = (zip(*[_par_spec(p) for p in params]) if params else ((), ()))
    nxr, npar = len(xa), len(pa)
    x_sds = [SDS((tm, w), F32) for _, _, w in plan]
    p_sds = [SDS(s.block_shape, F32) for s in pspecs]
    outs_sds = jax.eval_shape(lambda *vals: f(*vals), *x_sds, *p_sds)
    widths = [sum(outs_sds[ix].shape[1] for ix in idxs) for idxs in out_groups]

    def body(*refs):
        vals = _x_vals(refs[:nxr], plan, tm, nt) + [r[...].astype(F32) for r in refs[nxr:nxr + npar]]
        outs = f(*vals)
        _store_groups(refs[nxr + npar:], out_groups, outs)

    res, carried = _hosting_call(
        body, comm, name=name, grid=(nt,), in_specs=list(xspecs) + list(pspecs),
        out_specs=[pl.BlockSpec((tm, w), lambda i: (i, 0)) for w in widths],
        out_shape=[SDS((t, w), dt) for w, dt in zip(widths, out_dtypes)], scratch_shapes=[], args=(*xa, *pa))
    return res if comm is None else (res, carried)


def _rowwise_bwd(f, xs, params, cots, *, x_grad, p_grad, dx_groups, dx_dtypes, tm, name, extra=None, comm=None):
    tm, t = _tile_rows(xs, tm)
    nt = t // tm
    xa, xspecs, plan = _x_plan(xs, tm, t)
    pa, pspecs = (zip(*[_par_spec(p) for p in params]) if params else ((), ()))
    ca, cspecs = zip(*[_row_spec(c, tm) for c in cots])
    extra = extra or {}
    ekeys = sorted(extra)
    ea, especs = (zip(*[_row_spec(extra[k], tm) for k in ekeys]) if ekeys else ((), ()))
    nx, nxr, npar, nc, ne = len(plan), len(xa), len(pa), len(ca), len(ea)
    gx = [i for i in range(nx) if x_grad[i]]
    gp = [i for i in range(npar) if p_grad[i]]
    widths = [sum(plan[gx[ix]][2] for ix in idxs) for idxs in dx_groups]
    ng = len(dx_groups)

    def body(*refs):
        ins = refs[:nxr + npar + nc + ne]
        outs = refs[nxr + npar + nc + ne:]
        vals = _x_vals(ins[:nxr], plan, tm, nt) + [r[...].astype(F32) for r in ins[nxr:nxr + npar]]
        cvals = tuple(r[...].astype(F32) for r in ins[nxr + npar:nxr + npar + nc])
        evals = [r[...].astype(F32) for r in ins[nxr + npar + nc:]]
        diff_idx = gx + [nx + i for i in gp]

        def g(*dargs):
            full = list(vals)
            for ix, v in zip(diff_idx, dargs):
                full[ix] = v
            return tuple(f(*full))

        _, vjp = jax.vjp(g, *[vals[ix] for ix in diff_idx])
        grads = vjp(cvals)
        dxs = list(grads[:len(gx)])
        for k, ev in zip(ekeys, evals):
            dxs[k] = dxs[k] + ev
        _store_groups(outs[:ng], dx_groups, dxs)
        i = pl.program_id(0)
        for ref, gval in zip(outs[ng:], grads[len(gx):]):
            @pl.when(i == 0)
            def _(ref=ref):
                ref[...] = jnp.zeros_like(ref)
            ref[...] += gval

    dp_specs = [pl.BlockSpec(pspecs[i].block_shape, lambda i: (0, 0)) for i in gp]
    dp_shapes = [SDS(pspecs[i].block_shape, F32) for i in gp]
    res, carried = _hosting_call(
        body, comm, name=name, grid=(nt,), in_specs=list(xspecs) + list(pspecs) + list(cspecs) + list(especs),
        out_specs=[pl.BlockSpec((tm, w), lambda i: (i, 0)) for w in widths] + dp_specs,
        out_shape=[SDS((t, w), dt) for w, dt in zip(widths, dx_dtypes)] + dp_shapes, scratch_shapes=[],
        args=(*xa, *pa, *ca, *ea))
    return res if comm is None else (res, carried)


def _rms_f(x, g):
    return (x * lax.rsqrt(jnp.mean(x * x, axis=-1, keepdims=True) + NORM_EPS) * g,)


def _group_sum_impl(x, ones_bd):
    p1, p2 = _split2(x)
    dot = lambda p: lax.dot_general(p, ones_bd.astype(BF16), (((1,), (0,)), ((), ())), preferred_element_type=F32)
    return dot(p1) + dot(p2)


@jax.custom_vjp
def _group_sum(x, ones_bd):
    return _group_sum_impl(x, ones_bd)


_group_sum.defvjp(lambda x, o: (_group_sum_impl(x, o), o),
                  lambda o, g: (_group_sum_impl(g, o), jnp.zeros_like(o)))


def _rwkv_prep_f(r, k, v, lo, rp, kp, vp, lop, mu_r, mu_k, mu_v, mu_lo, w0, w2p, a0, a2p, g2p, k_k, k_a, ones_bd):
    r = r + mu_r * (rp - r)
    k = k + mu_k * (kp - k)
    v = v + mu_v * (vp - v)
    lo = lo + mu_lo * (lop - lo)
    w_log = -_softplus(-(w0 + _nn(jnp.tanh(lo), w2p))) - 0.5
    lw = -jnp.exp(w_log)
    a_g = _sigmoid(a0 + _nn(lo, a2p))
    g = _nn(_sigmoid(lo), g2p)
    kk = k * k_k
    kk = kk / jnp.maximum(jnp.sqrt(_group_sum(kk * kk, ones_bd)), L2_EPS)
    k2 = k * (1.0 + (a_g - 1.0) * k_a)
    return r, lw, k2, v, -kk, kk * a_g, g


def _rwkv_post_f(y, r, k2, v, g, r_k, gn_w, gn_b, ones_bd):
    inv_n = 1.0 / HB_DIM
    mean = _group_sum(y, ones_bd) * inv_n
    yc = y - mean
    var = _group_sum(yc * yc, ones_bd) * inv_n
    yn = yc * lax.rsqrt(var + RWKV_GN_EPS) * gn_w + gn_b
    bonus = _group_sum(r * k2 * r_k, ones_bd) * v
    return ((yn + bonus) * g,)


def _tri(c, strict=False):
    ii = lax.broadcasted_iota(jnp.int32, (c, c), 0)
    jj = lax.broadcasted_iota(jnp.int32, (c, c), 1)
    return (jj < ii) if strict else (jj <= ii)


def _hgrn_step(st0, q_a, f_a, i_a, g_a, l0, l1, onorm):
    nh, nj = len(q_a), len(q_a[0])
    c = q_a[0][0].shape[0]
    combos = [(j, h) for j in range(nj) for h in range(nh)]
    every = lambda fn: {q: fn(q) for q in combos}
    at_ = lambda d: (lambda q: d[q[1]][q[0]])
    qa_, fa_, ia_, ga_ = (at_(z) for z in (q_a, f_a, i_a, g_a))
    incl = _tri(c)
    rows = lax.broadcasted_iota(jnp.int32, (c, 1), 0)
    lb = []
    for h in range(nh):
        mx = jnp.maximum(l0[h], l1[h])
        e0, e1 = jnp.exp(l0[h] - mx), jnp.exp(l1[h] - mx)
        lb.append(e0 / (e0 + e1))
    forget = every(lambda q: lb[q[1]] + (1.0 - lb[q[1]]) * _sigmoid(fa_(q)))
    qs = every(lambda q: _silu(qa_(q)))
    kk = every(lambda q: 1.0 - forget[q])
    lf = every(lambda q: jnp.log(forget[q]))
    bcum = every(lambda q: _cumsum_rows(lf[q]))
    bref = every(lambda q: jnp.sum(jnp.where(rows <= c // 2, lf[q], 0.0), axis=0, keepdims=True))
    blast = every(lambda q: jnp.sum(lf[q], axis=0, keepdims=True))
    scores = every(lambda q: jnp.where(incl, _nt(qs[q] * jnp.exp(bcum[q] - bref[q]),
                                                 kk[q] * jnp.exp(bref[q] - bcum[q])), 0.0))
    intra = every(lambda q: _nn(scores[q], ia_(q)))
    qb = every(lambda q: qs[q] * jnp.exp(bcum[q]))
    upd = every(lambda q: _tn(ia_(q), kk[q] * jnp.exp(blast[q] - bcum[q])))
    dec = every(lambda q: jnp.exp(blast[q]))
    st = list(st0)
    o = {}
    for j in range(nj):
        for h in range(nh):
            o[(j, h)] = intra[(j, h)] + _nt(qb[(j, h)], st[h])
        st = [st[h] * dec[(j, h)] + upd[(j, h)] for h in range(nh)]
    out = every(lambda q: o[q] * lax.rsqrt(jnp.mean(o[q] * o[q], axis=-1, keepdims=True) + NORM_EPS)
                * onorm[q[1]] * _silu(ga_(q)))
    return [[out[(j, h)] for j in range(nj)] for h in range(nh)], st


def _hgrn_blocks(ref, nj, c):
    return [[ref[j * c:(j + 1) * c, h * HA_DIM:(h + 1) * HA_DIM] for j in range(nj)] for h in range(HA_HEADS)]


def _hgrn_cols(ref):
    return [ref[:, h * HA_DIM:(h + 1) * HA_DIM] for h in range(HA_HEADS)]


def _hgrn_fwd(p_h, l0, l1, onorm):
    t = p_h.shape[0]
    cc, nj = HGRN_CHUNK, HGRN_GROUP
    c = cc * nj
    n = t // c

    def body(q_ref, f_ref, i_ref, g_ref, l0_ref, l1_ref, on_ref, o_ref, hs_ref, st_ref):
        @pl.when(pl.program_id(0) == 0)
        def _():
            st_ref[...] = jnp.zeros_like(st_ref)

        hs_ref[0] = st_ref[...]
        o, st1 = _hgrn_step([st_ref[h] for h in range(HA_HEADS)],
                            *[_hgrn_blocks(ref, nj, cc) for ref in (q_ref, f_ref, i_ref, g_ref)],
                            _hgrn_cols(l0_ref), _hgrn_cols(l1_ref), _hgrn_cols(on_ref))
        for h in range(HA_HEADS):
            for j in range(nj):
                o_ref[j * cc:(j + 1) * cc, h * HA_DIM:(h + 1) * HA_DIM] = o[h][j]
            st_ref[h] = st1[h]

    col = lambda j: pl.BlockSpec((c, W_A), lambda i, j=j: (i, j))
    par = pl.BlockSpec((1, W_A), lambda i: (0, 0))
    return pl.pallas_call(
        body, name="hgrn_fwd", grid=(n,), in_specs=[col(0), col(1), col(2), col(3), par, par, par],
        out_specs=[pl.BlockSpec((c, W_A), lambda i: (i, 0)),
                   pl.BlockSpec((1, HA_HEADS, HA_DIM, HA_DIM), lambda i: (i, 0, 0, 0))],
        out_shape=[SDS((t, W_A), F32), SDS((n, HA_HEADS, HA_DIM, HA_DIM), F32)],
        scratch_shapes=[pltpu.VMEM((HA_HEADS, HA_DIM, HA_DIM), F32)],
        compiler_params=_params(("arbitrary",)))(p_h, p_h, p_h, p_h, l0, l1, onorm)


def _hgrn_bwd(p_h, l0, l1, onorm, hs, do, do_col, comm=None):
    t = p_h.shape[0]
    cc, nj = HGRN_CHUNK, HGRN_GROUP
    c = cc * nj
    n = t // c

    def body(q_ref, f_ref, i_ref, g_ref, l0_ref, l1_ref, on_ref, hs_ref, do_ref,
             dp_ref, dl0_ref, dl1_ref, don_ref, dst_ref):
        @pl.when(pl.program_id(0) == 0)
        def _():
            dst_ref[...] = jnp.zeros_like(dst_ref)
            dl0_ref[...] = jnp.zeros_like(dl0_ref)
            dl1_ref[...] = jnp.zeros_like(dl1_ref)
            don_ref[...] = jnp.zeros_like(don_ref)

        args = ([hs_ref[0, h] for h in range(HA_HEADS)],
                *[_hgrn_blocks(ref, nj, cc) for ref in (q_ref, f_ref, i_ref, g_ref)],
                _hgrn_cols(l0_ref), _hgrn_cols(l1_ref), _hgrn_cols(on_ref))
        _, vjp = jax.vjp(_hgrn_step, *args)
        dst0, dq, df, di, dg, dl0, dl1, don = vjp((_hgrn_blocks(do_ref, nj, cc),
                                                   [dst_ref[h] for h in range(HA_HEADS)]))
        for h in range(HA_HEADS):
            sl = slice(h * HA_DIM, (h + 1) * HA_DIM)
            for k, dv in enumerate((dq, df, di, dg)):
                for j in range(nj):
                    dp_ref[j * cc:(j + 1) * cc, k * W_A + h * HA_DIM:k * W_A + (h + 1) * HA_DIM] = dv[h][j]
            dl0_ref[:, sl] += dl0[h]
            dl1_ref[:, sl] += dl1[h]
            don_ref[:, sl] += don[h]
            dst_ref[h] = dst0[h]

    col = lambda j: pl.BlockSpec((c, W_A), lambda i, j=j: (n - 1 - i, j))
    par = pl.BlockSpec((1, W_A), lambda i: (0, 0))
    return _hosting_call(
        body, comm, name="hgrn_bwd", grid=(n,),
        in_specs=[col(0), col(1), col(2), col(3), par, par, par,
                  pl.BlockSpec((1, HA_HEADS, HA_DIM, HA_DIM), lambda i: (n - 1 - i, 0, 0, 0)),
                  pl.BlockSpec((c, W_A), lambda i: (n - 1 - i, do_col))],
        out_specs=[pl.BlockSpec((c, N_HGRN_COLS), lambda i: (n - 1 - i, 0)), par, par, par],
        out_shape=[SDS((t, N_HGRN_COLS), F32), SDS((1, W_A), F32), SDS((1, W_A), F32), SDS((1, W_A), F32)],
        scratch_shapes=[pltpu.VMEM((HA_HEADS, HA_DIM, HA_DIM), F32)],
        args=(p_h, p_h, p_h, p_h, l0, l1, onorm, hs, do))


HB_PAIRS = HB_HEADS // 2
PAIR_W = 2 * HB_DIM


def _head_lane_masks():
    lane = lax.broadcasted_iota(jnp.int32, (1, PAIR_W), 1)
    return (lane < HB_DIM).astype(F32), (lane >= HB_DIM).astype(F32)


@jax.custom_vjp
def _stack_heads(x):
    m0, m1 = _head_lane_masks()
    return jnp.concatenate([x * m0, x * m1], axis=0)


def _stack_heads_bwd(_, g):
    m0, m1 = _head_lane_masks()
    c = g.shape[0] // 2
    return (g[:c] * m0 + g[c:] * m1,)


_stack_heads.defvjp(lambda x: (_stack_heads(x), None), _stack_heads_bwd)


@jax.custom_vjp
def _unstack_heads(ys):
    c = ys.shape[0] // 2
    return ys[:c] + ys[c:]


_unstack_heads.defvjp(lambda ys: (_unstack_heads(ys), None), lambda _, g: (_stack_heads(g),))


def _same_head_block(c):
    ii = lax.broadcasted_iota(jnp.int32, (2 * c, 2 * c), 0)
    jj = lax.broadcasted_iota(jnp.int32, (2 * c, 2 * c), 1)
    same = (ii < c) == (jj < c)
    return same & (jj <= ii), same & (jj < ii), (ii == jj).astype(F32)


@jax.custom_vjp
def _rows_join(top, bottom):
    return jnp.concatenate([top, bottom], axis=0)


def _rows_join_bwd(n_top, g):
    return g[:n_top], g[n_top:]


_rows_join.defvjp(lambda top, bottom: (_rows_join(top, bottom), top.shape[0]), _rows_join_bwd)


def _rows_split_impl(x, n_top):
    return x[:n_top], x[n_top:]


_rows_split = jax.custom_vjp(_rows_split_impl, nondiff_argnums=(1,))
_rows_split.defvjp(lambda x, n_top: (_rows_split_impl(x, n_top), None),
                   lambda n_top, _, g: (jnp.concatenate([g[0], g[1]], axis=0),))


def _rwkv_step(s0, r, lw, k, v, a, b):
    npair, nj = len(r), len(r[0])
    c = r[0][0].shape[0]
    combos = [(j, p) for j in range(nj) for p in range(npair)]
    every = lambda fn: {q: fn(q) for q in combos}
    at_ = lambda d: (lambda q: d[q[1]][q[0]])
    r_, lw_, k_, v_, a_, b_ = (at_(z) for z in (r, lw, k, v, a, b))
    incl, strict, eye = _same_head_block(c)

    gam = every(lambda q: _cumsum_rows(lw_(q)))
    gtot = every(lambda q: jnp.sum(lw_(q), axis=0, keepdims=True))
    eneg = every(lambda q: jnp.exp(-gam[q]))
    edec = every(lambda q: jnp.exp(gtot[q] - gam[q]))
    at = every(lambda q: _stack_heads(a_(q) * jnp.exp(gam[q] - lw_(q))))
    rt = every(lambda q: _stack_heads(r_(q) * jnp.exp(gam[q])))
    bt = every(lambda q: _stack_heads(b_(q) * eneg[q]))
    kt = every(lambda q: _stack_heads(k_(q) * eneg[q]))
    bdec = every(lambda q: _stack_heads(b_(q) * edec[q]))
    kdec = every(lambda q: _stack_heads(k_(q) * edec[q]))
    vs = every(lambda q: _stack_heads(v_(q)))
    a_ab = every(lambda q: jnp.where(strict, _nt(at[q], bt[q]), 0.0))
    a_ak = every(lambda q: jnp.where(strict, _nt(at[q], kt[q]), 0.0))
    a_rb = every(lambda q: jnp.where(incl, _nt(rt[q], bt[q]), 0.0))
    a_rk = every(lambda q: jnp.where(incl, _nt(rt[q], kt[q]), 0.0))
    tinv = every(lambda q: eye + a_ab[q])
    pw = a_ab
    span = 2
    while span < c:
        pw = every(lambda q, pw=pw: _nn_x3(pw[q], pw[q]))
        tinv = every(lambda q, pw=pw, tinv=tinv: tinv[q] + _nn_x3(pw[q], tinv[q]))
        span *= 2
    akv = every(lambda q: _nn(a_ak[q], vs[q]))
    w1 = every(lambda q: _nn(tinv[q], at[q]))
    u0 = every(lambda q: _nn(tinv[q], akv[q]))
    wr = every(lambda q: _rows_join(w1[q], rt[q]))
    bk = every(lambda q: _rows_join(bdec[q], kdec[q]))
    yv = every(lambda q: _nn(a_rk[q], vs[q]))
    gdec = every(lambda q: jnp.exp(gtot[q]))

    s = list(s0)
    y = [[None] * nj for _ in range(npair)]
    for j in range(nj):
        both = {p: _rows_split(_nt(wr[(j, p)], s[p]), 2 * c) for p in range(npair)}
        u = {p: both[p][0] + u0[(j, p)] for p in range(npair)}
        for p in range(npair):
            y[p][j] = _unstack_heads(both[p][1] + _nn(a_rb[(j, p)], u[p]) + yv[(j, p)])
        s = [s[p] * gdec[(j, p)] + _tn(_rows_join(u[p], vs[(j, p)]), bk[(j, p)]) for p in range(npair)]
    return y, s


def _rwkv_blocks(ref, nj, c):
    return [[ref[j * c:(j + 1) * c, p * PAIR_W:(p + 1) * PAIR_W] for j in range(nj)] for p in range(HB_PAIRS)]


def _rwkv_fwd(seqs, comm=None):
    t = seqs[0].shape[0]
    c, nj = RWKV_CHUNK, RWKV_GROUP
    n = t // (c * nj)

    def body(r_ref, lw_ref, k_ref, v_ref, a_ref, b_ref, y_ref, hs_ref, st_ref):
        @pl.when(pl.program_id(0) == 0)
        def _():
            st_ref[...] = jnp.zeros_like(st_ref)

        hs_ref[0] = st_ref[...]
        s0 = [st_ref[p] for p in range(HB_PAIRS)]
        y, s1 = _rwkv_step(s0, *[_rwkv_blocks(ref, nj, c) for ref in (r_ref, lw_ref, k_ref, v_ref, a_ref, b_ref)])
        for p in range(HB_PAIRS):
            for j in range(nj):
                y_ref[j * c:(j + 1) * c, p * PAIR_W:(p + 1) * PAIR_W] = y[p][j]
            st_ref[p] = s1[p]

    seq = pl.BlockSpec((c * nj, W_B), lambda i: (i, 0))
    return _hosting_call(
        body, comm, name="rwkv_fwd", grid=(n,), in_specs=[seq] * 6,
        out_specs=[seq, pl.BlockSpec((1, HB_PAIRS, PAIR_W, PAIR_W), lambda i: (i, 0, 0, 0))],
        out_shape=[SDS((t, W_B), F32), SDS((n, HB_PAIRS, PAIR_W, PAIR_W), F32)],
        scratch_shapes=[pltpu.VMEM((HB_PAIRS, PAIR_W, PAIR_W), F32)], args=tuple(seqs))


def _rwkv_bwd(seqs, hs, dy, comm=None):
    t = seqs[0].shape[0]
    c, nj = RWKV_CHUNK, RWKV_GROUP
    n = t // (c * nj)

    def body(r_ref, lw_ref, k_ref, v_ref, a_ref, b_ref, hs_ref, dy_ref,
             dr_ref, dlw_ref, dk_ref, dv_ref, da_ref, db_ref, dst_ref):
        @pl.when(pl.program_id(0) == 0)
        def _():
            dst_ref[...] = jnp.zeros_like(dst_ref)

        s0 = [hs_ref[0, p] for p in range(HB_PAIRS)]
        seq_vals = [_rwkv_blocks(ref, nj, c) for ref in (r_ref, lw_ref, k_ref, v_ref, a_ref, b_ref)]
        _, vjp = jax.vjp(_rwkv_step, s0, *seq_vals)
        grads = vjp((_rwkv_blocks(dy_ref, nj, c), [dst_ref[p] for p in range(HB_PAIRS)]))
        for ref, gr in zip((dr_ref, dlw_ref, dk_ref, dv_ref, da_ref, db_ref), grads[1:]):
            for p in range(HB_PAIRS):
                for j in range(nj):
                    ref[j * c:(j + 1) * c, p * PAIR_W:(p + 1) * PAIR_W] = gr[p][j]
        m0, m1 = _head_lane_masks()
        rows0 = (lax.broadcasted_iota(jnp.int32, (PAIR_W, 1), 0) < HB_DIM).astype(F32)
        blocks = rows0 * m0 + (1.0 - rows0) * m1
        for p in range(HB_PAIRS):
            dst_ref[p] = grads[0][p] * blocks

    seq = pl.BlockSpec((c * nj, W_B), lambda i: (n - 1 - i, 0))
    return _hosting_call(
        body, comm, name="rwkv_bwd", grid=(n,),
        in_specs=[seq] * 6 + [pl.BlockSpec((1, HB_PAIRS, PAIR_W, PAIR_W), lambda i: (n - 1 - i, 0, 0, 0)), seq],
        out_specs=[seq] * 6, out_shape=[SDS((t, W_B), F32)] * 6,
        scratch_shapes=[pltpu.VMEM((HB_PAIRS, PAIR_W, PAIR_W), F32)], args=(*seqs, hs, dy))


def _final_loss(x3, fnorm, target, *, tm):
    t, d = x3.shape

    def body(x_ref, g_ref, t_ref, dx_ref, dg_ref, loss_ref):
        @pl.when(pl.program_id(0) == 0)
        def _():
            dg_ref[...] = jnp.zeros_like(dg_ref)
            loss_ref[...] = jnp.zeros_like(loss_ref)

        x, g = x_ref[...], g_ref[...]
        rinv = lax.rsqrt(jnp.mean(x * x, axis=-1, keepdims=True) + NORM_EPS)
        xh = x * rinv
        diff = xh * g - t_ref[...]
        loss_ref[...] += 0.5 * jnp.sum(jnp.mean(diff * diff, axis=-1, keepdims=True))
        dy = diff * (1.0 / d)
        dg_ref[...] += jnp.sum(dy * xh, axis=0, keepdims=True)
        dxh = dy * g
        dx_ref[...] = rinv * (dxh - xh * jnp.mean(dxh * xh, axis=-1, keepdims=True))

    row = pl.BlockSpec((tm, d), lambda i: (i, 0))
    return pl.pallas_call(
        body, name="final_loss", grid=(t // tm,), in_specs=[row, pl.BlockSpec((1, d), lambda i: (0, 0)), row],
        out_specs=[row, pl.BlockSpec((1, d), lambda i: (0, 0)), pl.BlockSpec((8, 128), lambda i: (0, 0))],
        out_shape=[SDS((t, d), F32), SDS((1, d), F32), SDS((8, 128), F32)],
        compiler_params=_params(("arbitrary",)))(x3, fnorm, target)


def _gate_up_act(h, wgt, wut, *, tm, tn, name, comm=None):
    t, d = h.shape
    tm = min(tm, t)

    def body(h_ref, g_ref, u_ref, a_out, u_out, act_out):
        hv = h_ref[...]
        a = _dg(hv, g_ref[...], 1, 1, False)
        u = _dg(hv, u_ref[...], 1, 1, False)
        a_out[...] = a.astype(a_out.dtype)
        u_out[...] = u.astype(u_out.dtype)
        act_out[...] = (_silu(a) * u).astype(act_out.dtype)

    wspec = pl.BlockSpec((tn, d), lambda i, j: (j, 0))
    ospec = pl.BlockSpec((tm, tn), lambda i, j: (i, j))
    return _hosting_call(
        body, comm, name=name, grid=(t // tm, D_FF // tn),
        in_specs=[pl.BlockSpec((tm, d), lambda i, j: (i, 0)), wspec, wspec], out_specs=[ospec, ospec, ospec],
        out_shape=[SDS((t, D_FF), BF16), SDS((t, D_FF), BF16), SDS((t, D_FF), BF16)], scratch_shapes=[],
        args=(h, wgt, wut))


def _dact_swiglu(dout, wd, a, u, *, tm, tn, name, comm=None):
    t, d = dout.shape
    tm = min(tm, t)

    def body(d_ref, w_ref, a_ref, u_ref, da_out, du_out):
        dact = 0.5 * _dg(d_ref[...], w_ref[...], 1, 1, False)
        av, uv = a_ref[...].astype(F32), u_ref[...].astype(F32)
        s = _sigmoid(av)
        da_out[...] = (dact * uv * (s * (1.0 + av * (1.0 - s)))).astype(da_out.dtype)
        du_out[...] = (dact * (av * s)).astype(du_out.dtype)

    tile = pl.BlockSpec((tm, tn), lambda i, j: (i, j))
    return _hosting_call(
        body, comm, name=name, grid=(t // tm, D_FF // tn),
        in_specs=[pl.BlockSpec((tm, d), lambda i, j: (i, 0)), pl.BlockSpec((tn, d), lambda i, j: (j, 0)), tile, tile],
        out_specs=[tile, tile], out_shape=[SDS((t, D_FF), BF16), SDS((t, D_FF), BF16)], scratch_shapes=[],
        args=(dout, wd, a, u))


class _Plan:
    def __init__(self):
        self.entries, self.counts = collections.defaultdict(list), {}

    def carry(self, host, comm_of, after):
        self.entries[host].append((comm_of, after))

    def comm(self, host, g):
        comms = [comm_of(g) for comm_of, _ in self.entries.get(host, [])]
        self.counts[host] = [len(c.arrays) for c in comms]
        return functools.reduce(_join_comms, comms) if comms else None

    def done(self, host, results, w):
        start = 0
        for (_, after), n in zip(self.entries.get(host, []), self.counts.get(host, [])):
            after(results[start:start + n], w)
            start += n


def _ffn_fwd(x, w, tag, plan, g):
    comm = plan.comm(f"{tag}_rms", g)
    res = _rowwise(_rms_f, [x], [w[f"{tag}_norm"]], [[0]], [BF16], tm=512, name=f"{tag}_rms", comm=comm)
    (h,), carried = res if comm is not None else (res, [])
    plan.done(f"{tag}_rms", carried, w)
    (a, u, act), carried = _gate_up_act(h, w[f"{tag}_wgt"], w[f"{tag}_wut"], tm=2048, tn=256, name=f"{tag}_gate_up",
                                        comm=plan.comm(f"{tag}_gate_up", g))
    plan.done(f"{tag}_gate_up", carried, w)
    comm = plan.comm(f"{tag}_down", g)
    out = _mm(act, w[f"{tag}_wd"], tm=1024, tn=D_MODEL, tk=D_FF, name=f"{tag}_down", res=x, scale=0.5, comm=comm)
    if comm is not None:
        out, carried = out
        plan.done(f"{tag}_down", carried, w)
    return out, (h, a, u, act)


def _ffn_bwd(dout, x, w, saved, tag, plan, g):
    h, a, u, act = saved

    def carrying(fn, host, *args, **kwargs):
        comm = plan.comm(host, g)
        res = fn(*args, name=host, comm=comm, **kwargs)
        out, carried = res if comm is not None else (res, [])
        plan.done(host, carried, w)
        return out

    (da, du), carried = _dact_swiglu(dout, w[f"{tag}_wd"], a, u, tm=2048, tn=256, name=f"{tag}_dact",
                                     comm=plan.comm(f"{tag}_dact", g))
    plan.done(f"{tag}_dact", carried, w)
    g[f"{tag}_wd"] = _mm(act, dout, ta=True, tm=D_FF // 2, tn=D_MODEL, tk=1024, name=f"{tag}_dwd", scale=0.5)
    g[f"{tag}_wgt"] = carrying(_mm, f"{tag}_dwg", da, h, ta=True, tm=D_FF // 2, tn=D_MODEL, tk=1024)
    g[f"{tag}_wut"] = carrying(_mm, f"{tag}_dwu", du, h, ta=True, tm=D_FF // 2, tn=D_MODEL, tk=1024)
    dh = carrying(_mm, f"{tag}_dh_g", da, w[f"{tag}_wgt"], tm=1024, tn=D_MODEL, tk=D_FF)
    dh = carrying(_mm, f"{tag}_dh_u", du, w[f"{tag}_wut"], tm=1024, tn=D_MODEL, tk=D_FF, res=dh)
    dx, g[f"{tag}_norm"] = _rowwise_bwd(_rms_f, [x], [w[f"{tag}_norm"]], [dh], x_grad=[True], p_grad=[True],
                                        dx_groups=[[0]], dx_dtypes=[F32], tm=512, name=f"{tag}_drms",
                                        extra={0: dout})
    return dx


def _local_step(x, target, w, plan=None):
    plan = plan or _Plan()
    ones_bd = jnp.kron(jnp.eye(HB_HEADS, dtype=F32), jnp.ones((HB_DIM, HB_DIM), F32))
    g = {}
    x1, ffn1_saved = _ffn_fwd(x, w, "ffn1", plan, g)
    hm, = _rowwise(_rms_f, [x1], [w["mix_norm"]], [[0]], [BF16], tm=512, name="mix_rms")
    p_h = _mm(hm, w["w_in_h"], tm=2048, tn=256, tk=D_MODEL, name="inproj_h")
    p_r = _mm(hm, w["w_in_r"], tm=2048, tn=256, tk=D_MODEL, name="inproj_r")
    o_a, hgrn_states = _hgrn_fwd(p_h, w["lb0"], w["lb1"], w["hgrn_out_norm"])

    mu = w["mu_pad"]
    prep_xs = [(p_r, W_B, 0), (p_r, W_B, 1), (p_r, W_B, 2), (p_r, LORA_PAD, 6),
               ("prev", p_r, W_B, 0), ("prev", p_r, W_B, 1), ("prev", p_r, W_B, 2), ("prev", p_r, LORA_PAD, 6)]
    prep_ps = [(mu, W_B, 0), (mu, W_B, 1), (mu, W_B, 2), (mu, LORA_PAD, 6), w["rwkv_w0"], w["w2_pad"], w["rwkv_a0"],
               w["a2_pad"], w["g2_pad"], w["rwkv_k_k"], w["rwkv_k_a"], ones_bd]
    prep_f = _rwkv_prep_f
    r, lw, k2, v, a_vec, b_vec, gate = _rowwise(prep_f, prep_xs, prep_ps, [[0], [1], [2], [3], [4], [5], [6]],
                                                [F32] * 7, tm=256, name="rwkv_prep")
    seqs = [r, lw, k2, v, a_vec, b_vec]
    (y, rwkv_states), carried = _rwkv_fwd(seqs, comm=plan.comm("rwkv_fwd", g))
    plan.done("rwkv_fwd", carried, w)
    post_f = _rwkv_post_f
    post_xs = [y, r, k2, v, gate]
    post_ps = [w["rwkv_r_k"], w["rwkv_gn_w"], w["rwkv_gn_b"], ones_bd]
    o_b, = _rowwise(post_f, post_xs, post_ps, [[0]], [F32], tm=256, name="rwkv_post")
    x2 = _mm(o_a, w["w_out_a"], tm=2048, tn=256, tk=W_A, name="outproj_a", res=x1)
    x2 = _mm(o_b, w["w_out_b"], tm=2048, tn=256, tk=W_B, name="outproj_b", res=x2)
    x3, ffn2_saved = _ffn_fwd(x2, w, "ffn2", plan, g)
    dx3, g["final_norm"], loss = _final_loss(x3, w["final_norm"], target, tm=256)

    dx2 = _ffn_bwd(dx3, x2, w, ffn2_saved, "ffn2", plan, g)
    do_a = _mm(dx2, w["w_out_a"], tb=True, tm=2048, tn=256, tk=D_MODEL, name="outproj_do_a")
    do_b = _mm(dx2, w["w_out_b"], tb=True, tm=2048, tn=256, tk=D_MODEL, name="outproj_do_b")
    g["w_out_a"] = _mm(o_a, dx2, ta=True, tm=W_A, tn=D_MODEL, tk=1024, name="outproj_dw_a")
    g["w_out_b"] = _mm(o_b, dx2, ta=True, tm=W_B, tn=D_MODEL, tk=1024, name="outproj_dw_b")

    (dp_h, g["lb0"], g["lb1"], g["hgrn_out_norm"]), carried = _hgrn_bwd(
        p_h, w["lb0"], w["lb1"], w["hgrn_out_norm"], hgrn_states, do_a, 0, comm=plan.comm("hgrn_bwd", g))
    plan.done("hgrn_bwd", carried, w)
    post_out = _rowwise_bwd(post_f, post_xs, post_ps, [do_b], x_grad=[True] * 5, p_grad=[True] * 3 + [False],
                            dx_groups=[[0], [1], [2], [3], [4]], dx_dtypes=[F32] * 5, tm=256, name="rwkv_post_bwd")
    dy, dr1, dk1, dv1, dgate, g["rwkv_r_k"], g["rwkv_gn_w"], g["rwkv_gn_b"] = post_out
    (dr2, dlw, dk2, dv2, da_vec, db_vec), carried = _rwkv_bwd(seqs, rwkv_states, dy, comm=plan.comm("rwkv_bwd", g))
    plan.done("rwkv_bwd", carried, w)

    def prep2_f(*vals):
        r_, lw_, k2_, v_, a_, b_, g_ = prep_f(*vals)
        return r_, lw_, k2_, v_, a_, b_, g_, r_, k2_, v_

    prep_out = _rowwise_bwd(prep2_f, prep_xs, prep_ps, [dr2, dlw, dk2, dv2, da_vec, db_vec, dgate, dr1, dk1, dv1],
                            x_grad=[True] * 8, p_grad=[True] * 11 + [False], dx_groups=[[0, 1, 2, 3], [4, 5, 6, 7]],
                            dx_dtypes=[F32, F32], tm=256, name="rwkv_prep_bwd")
    dpr_main, dpr_prev = prep_out[0], prep_out[1]
    (dmu_r, dmu_k, dmu_v, dmu_lo, g["rwkv_w0"], g["w2_pad"], g["rwkv_a0"], g["a2_pad"], g["g2_pad"],
     g["rwkv_k_k"], g["rwkv_k_a"]) = prep_out[2:]
    g["mu_pad"] = jnp.concatenate([dmu_r, dmu_k, dmu_v, dmu_lo], axis=1)
    dp_r, = _rowwise(lambda u_, s_: (u_ + s_,), [dpr_main, ("next", dpr_prev, N_RWKV_PAD, 0)], [], [[0]], [F32],
                     tm=512, name="rwkv_dp_sum")
    dhm = _mm(dp_h, w["w_in_h"], tb=True, tm=1024, tn=D_MODEL, tk=N_HGRN_COLS, name="inproj_dh_h")
    dhm = _mm(dp_r, w["w_in_r"], tb=True, tm=1024, tn=D_MODEL, tk=N_RWKV_PAD, name="inproj_dh_r", res=dhm)
    g["w_in_h"] = _mm(hm, dp_h, ta=True, tm=D_MODEL, tn=D_MODEL, tk=1024, name="inproj_dw_h")
    g["w_in_r"] = _mm(hm, dp_r, ta=True, tm=D_MODEL, tn=N_RWKV_PAD // 2, tk=1024, name="inproj_dw_r")
    mix_comm = plan.comm("mix_drms", g)
    mix_out = _rowwise_bwd(_rms_f, [x1], [w["mix_norm"]], [dhm], x_grad=[True], p_grad=[True], dx_groups=[[0]],
                           dx_dtypes=[F32], tm=512, name="mix_drms", extra={0: dx2}, comm=mix_comm)
    (dx1, g["mix_norm"]), carried = mix_out if mix_comm is not None else (mix_out, [])
    plan.done("mix_drms", carried, w)
    dx0 = _ffn_bwd(dx1, x, w, ffn1_saved, "ffn1", plan, g)
    return loss, dx0, g


HBM_SPEC = pl.BlockSpec(memory_space=pl.ANY)

Comm = collections.namedtuple("Comm", "arrays out_shapes aliased sem_shapes start finish")


def _join_comms(first, second):
    assert first.aliased == second.aliased
    n, s = len(first.arrays), len(first.sem_shapes)

    def start(ins, outs, sems):
        first.start(ins[:n], outs[:n], sems[:s])
        second.start(ins[n:], outs[n:], sems[s:])

    def finish(ins, outs, sems):
        first.finish(ins[:n], outs[:n], sems[:s])
        second.finish(ins[n:], outs[n:], sems[s:])

    return Comm(list(first.arrays) + list(second.arrays), list(first.out_shapes) + list(second.out_shapes),
                first.aliased, list(first.sem_shapes) + list(second.sem_shapes), start, finish)


def _run_comm(comm, name):
    n = len(comm.arrays)

    def body(*refs):
        ins, outs, sems = refs[:n], refs[n:2 * n], refs[2 * n:]
        comm.start(ins, outs, sems)
        comm.finish(ins, outs, sems)

    return pl.pallas_call(
        body, name=name, in_specs=[HBM_SPEC] * n, out_specs=[HBM_SPEC] * n, out_shape=list(comm.out_shapes),
        input_output_aliases={t: t for t in range(n)} if comm.aliased else {},
        scratch_shapes=list(comm.sem_shapes))(*comm.arrays)


def _hosting_call(body, comm, *, name, grid, in_specs, out_specs, out_shape, scratch_shapes, args):
    sem = ("arbitrary",) * len(grid)
    if comm is None:
        res = pl.pallas_call(body, name=name, grid=grid, in_specs=in_specs, out_specs=out_specs, out_shape=out_shape,
                             scratch_shapes=scratch_shapes, compiler_params=_params(sem))(*args)
        return list(res), []
    ni, no, ns, nc = len(in_specs), len(out_specs), len(scratch_shapes), len(comm.arrays)

    def wrapped(*refs):
        ins, cins = refs[:ni], refs[ni:ni + nc]
        outs, couts = refs[ni + nc:ni + nc + no], refs[ni + nc + no:ni + 2 * nc + no]
        scr, sems = refs[ni + 2 * nc + no:ni + 2 * nc + no + ns], refs[ni + 2 * nc + no + ns:]
        first = functools.reduce(jnp.logical_and, [pl.program_id(k) == 0 for k in range(len(grid))])
        last = functools.reduce(jnp.logical_and, [pl.program_id(k) == grid[k] - 1 for k in range(len(grid))])

        @pl.when(first)
        def _():
            comm.start(cins, couts, sems)

        body(*ins, *outs, *scr)

        @pl.when(last)
        def _():
            comm.finish(cins, couts, sems)

    res = pl.pallas_call(
        wrapped, name=name, grid=grid, in_specs=list(in_specs) + [HBM_SPEC] * nc,
        out_specs=list(out_specs) + [HBM_SPEC] * nc, out_shape=list(out_shape) + list(comm.out_shapes),
        scratch_shapes=list(scratch_shapes) + list(comm.sem_shapes),
        input_output_aliases={ni + t: no + t for t in range(nc)} if comm.aliased else {},
        compiler_params=_params(sem))(*args, *comm.arrays)
    return list(res[:no]), list(res[no:])


def _chips(x, y):
    return [(1 - x, y), (x, 1 - y), (1 - x, 1 - y)]


def _gather_comm(bufs):
    n = len(bufs)

    def copies(outs, sems):
        ici_send, ici_recv, d2d_send, d2d_recv = sems
        x, y, c = lax.axis_index("x"), lax.axis_index("y"), lax.axis_index("c")

        def half(t, slot, hc):
            hr = bufs[t].shape[1] // 2
            return outs[t].at[slot, pl.ds(pl.multiple_of(hc * hr, 16), hr), :]

        def ici(t, j, slot, px, py):
            return pltpu.make_async_remote_copy(src_ref=half(t, slot, c), dst_ref=half(t, slot, c),
                                                send_sem=ici_send.at[3 * t + j], recv_sem=ici_recv.at[3 * t + j],
                                                device_id=(px, py, c), device_id_type=MESH)

        def d2d(t, j, slot, hc):
            return pltpu.make_async_remote_copy(src_ref=half(t, slot, hc), dst_ref=half(t, slot, hc),
                                                send_sem=d2d_send.at[3 * t + j], recv_sem=d2d_recv.at[3 * t + j],
                                                device_id=(x, y, 1 - c), device_id_type=MESH)

        peers = [(t, j, px, py) for t in range(n) for j, (px, py) in enumerate(_chips(x, y))]
        return ici, d2d, peers, 2 * x + y, c

    def start(ins, outs, sems):
        ici, _, peers, me, _ = copies(outs, sems)
        for t, j, px, py in peers:
            ici(t, j, me, px, py).start()

    def finish(ins, outs, sems):
        ici, d2d, peers, me, c = copies(outs, sems)
        for t, j, px, py in peers:
            ici(t, j, 2 * px + py, px, py).wait_recv()
            d2d(t, j, 2 * px + py, c).start()
        for t, j, px, py in peers:
            d2d(t, j, 2 * px + py, 1 - c).wait_recv()
        for t, j, px, py in peers:
            ici(t, j, me, px, py).wait_send()
            d2d(t, j, 2 * px + py, c).wait_send()

    return Comm(list(bufs), [SDS(b.shape, b.dtype) for b in bufs], True, [pltpu.SemaphoreType.DMA((3 * n,))] * 4,
                start, finish)


def _sibling_exchange_comm(gs):
    n = len(gs)

    def copies(ins, outs, sems):
        x, y, c = lax.axis_index("x"), lax.axis_index("y"), lax.axis_index("c")
        cps = []
        for t in range(n):
            hr = gs[t].shape[1] // 2
            src = ins[t].at[:, pl.ds(pl.multiple_of((1 - c) * hr, SUBLANES), hr), :]
            cps.append(pltpu.make_async_remote_copy(src_ref=src, dst_ref=outs[t], send_sem=sems[0].at[t],
                                                    recv_sem=sems[1].at[t], device_id=(x, y, 1 - c),
                                                    device_id_type=MESH))
        return cps

    def start(ins, outs, sems):
        for cp in copies(ins, outs, sems):
            cp.start()

    def finish(ins, outs, sems):
        for cp in copies(ins, outs, sems):
            cp.wait()

    return Comm(list(gs), [SDS((N_CHIPS, g.shape[1] // 2, g.shape[2]), g.dtype) for g in gs], False,
                [pltpu.SemaphoreType.DMA((n,))] * 2, start, finish)


def _chip_exchange_comm(ss):
    n = len(ss)

    def copies(ins, outs, sems):
        x, y, c = lax.axis_index("x"), lax.axis_index("y"), lax.axis_index("c")
        me = 2 * x + y

        def copy(t, j, px, py, src_slot, dst_slot):
            return pltpu.make_async_remote_copy(src_ref=ins[t].at[src_slot], dst_ref=outs[t].at[dst_slot],
                                                send_sem=sems[0].at[3 * t + j], recv_sem=sems[1].at[3 * t + j],
                                                device_id=(px, py, c), device_id_type=MESH)

        peers = [(t, j, px, py) for t in range(n) for j, (px, py) in enumerate(_chips(x, y))]
        return copy, peers, me

    def start(ins, outs, sems):
        copy, peers, me = copies(ins, outs, sems)
        for t, j, px, py in peers:
            copy(t, j, px, py, 2 * px + py, me).start()

    def finish(ins, outs, sems):
        copy, peers, me = copies(ins, outs, sems)
        for t, j, px, py in peers:
            copy(t, j, px, py, me, 2 * px + py).wait_recv()
        for t, j, px, py in peers:
            copy(t, j, px, py, 2 * px + py, me).wait_send()

    return Comm(list(ss), [SDS(s.shape, s.dtype) for s in ss], False, [pltpu.SemaphoreType.DMA((3 * n,))] * 2,
                start, finish)


def _sibling_swap_comm(fs):
    n = len(fs)

    def copies(ins, outs, sems):
        x, y, c = lax.axis_index("x"), lax.axis_index("y"), lax.axis_index("c")
        return [pltpu.make_async_remote_copy(src_ref=ins[t], dst_ref=outs[t], send_sem=sems[0].at[t],
                                             recv_sem=sems[1].at[t], device_id=(x, y, 1 - c), device_id_type=MESH)
                for t in range(n)]

    def start(ins, outs, sems):
        for cp in copies(ins, outs, sems):
            cp.start()

    def finish(ins, outs, sems):
        for cp in copies(ins, outs, sems):
            cp.wait()

    return Comm(list(fs), [SDS(f.shape, f.dtype) for f in fs], False, [pltpu.SemaphoreType.DMA((n,))] * 2,
                start, finish)


def _row_tile(rows, cap=512):
    best = SUBLANES
    for tr in range(SUBLANES, min(rows, cap) + 1, SUBLANES):
        if rows % tr == 0:
            best = tr
    return best


def _add_halves(g4, r4, c_idx, name):
    _, hr, lanes = r4.shape
    tr = _row_tile(hr)
    nb = hr // tr

    def body(c_ref, a_ref, b_ref, o_ref):
        o_ref[...] = (a_ref[...] + b_ref[...]).astype(o_ref.dtype)

    grid_spec = pltpu.PrefetchScalarGridSpec(
        num_scalar_prefetch=1, grid=(N_CHIPS, nb),
        in_specs=[pl.BlockSpec((None, tr, lanes), lambda q, i, c_ref: (q, c_ref[0] * nb + i, 0)),
                  pl.BlockSpec((None, tr, lanes), lambda q, i, c_ref: (q, i, 0))],
        out_specs=pl.BlockSpec((None, tr, lanes), lambda q, i, c_ref: (q, i, 0)))
    return pl.pallas_call(body, name=name, grid_spec=grid_spec, out_shape=SDS(r4.shape, BF16),
                          compiler_params=_params(("parallel", "parallel")))(c_idx, g4, r4)


def _sum_chips(r4, s4, me_idx, name):
    _, rows, lanes = r4.shape
    tr = _row_tile(rows)

    def body(me_ref, a_ref, b_ref, c_ref, d_ref, own_ref, o_ref):
        own = own_ref[...].astype(F32)
        p = [jnp.where(me_ref[0] == q, own, ref[...].astype(F32)) for q, ref in enumerate((a_ref, b_ref, c_ref, d_ref))]
        o_ref[...] = ((p[0] + p[1]) + p[2]) + p[3]

    other = lambda q: (lambda i, me_ref: (jnp.where(me_ref[0] == q, (q + 1) % N_CHIPS, q), i, 0))
    grid_spec = pltpu.PrefetchScalarGridSpec(
        num_scalar_prefetch=1, grid=(rows // tr,),
        in_specs=[pl.BlockSpec((None, tr, lanes), other(q)) for q in range(N_CHIPS)]
        + [pl.BlockSpec((None, tr, lanes), lambda i, me_ref: (me_ref[0], i, 0))],
        out_specs=pl.BlockSpec((tr, lanes), lambda i, me_ref: (i, 0)))
    return pl.pallas_call(body, name=name, grid_spec=grid_spec, out_shape=SDS((rows, lanes), F32),
                          compiler_params=_params(("parallel",)))(me_idx, r4, r4, r4, r4, s4)


def _adamw(wf, g_own, g_other, mf, vf, c_idx, name):
    rows, lanes = wf.shape
    hr = rows // 2
    tr = _row_tile(hr)
    nb = hr // tr
    c1 = 1.0 / (1.0 - ADAM_B1 ** ADAM_STEP)
    c2 = 1.0 / (1.0 - ADAM_B2 ** ADAM_STEP)

    def body(c_ref, w_ref, go_ref, gx_ref, m_ref, v_ref, g_ref, d_ref, nm_ref, nv_ref):
        gv = jnp.where(pl.program_id(0) == c_ref[0], go_ref[...], gx_ref[...])
        m = ADAM_B1 * m_ref[...] + (1.0 - ADAM_B1) * gv
        v = ADAM_B2 * v_ref[...] + (1.0 - ADAM_B2) * (gv * gv)
        g_ref[...] = gv
        d_ref[...] = -ADAM_LR * ((m * c1) / (jnp.sqrt(v * c2) + ADAM_EPS) + ADAM_WD * w_ref[...])
        nm_ref[...] = m
        nv_ref[...] = v

    full = pl.BlockSpec((tr, lanes), lambda h, i, c_ref: (h * nb + i, 0))
    half = pl.BlockSpec((tr, lanes), lambda h, i, c_ref: (i, 0))
    grid_spec = pltpu.PrefetchScalarGridSpec(num_scalar_prefetch=1, grid=(2, nb),
                                             in_specs=[full, half, half, full, full], out_specs=[full] * 4)
    return pl.pallas_call(body, name=name, grid_spec=grid_spec, out_shape=[SDS((rows, lanes), F32)] * 4,
                          compiler_params=_params(("parallel", "parallel")))(c_idx, wf, g_own, g_other, mf, vf)


BIG = ("ffn1_w_gate", "ffn1_w_up", "ffn1_w_down", "ffn2_w_gate", "ffn2_w_up", "ffn2_w_down", "w_out", "w_in")
TRANSPOSED = ("ffn1_w_gate", "ffn1_w_up", "ffn2_w_gate", "ffn2_w_up")
PACKED = ("rwkv_w2", "rwkv_a2", "rwkv_g2")
SMALL_SHAPES = {"ffn1_norm": (1, D_MODEL), "mix_norm": (1, D_MODEL), "hgrn_lb_logits": (2, W_A),
                "hgrn_out_norm": (1, W_A), "rwkv_shift_mu": (1, N_RWKV_COLS), "rwkv_w0": (1, W_B),
                "rwkv_a0": (1, W_B), "rwkv_k_k": (1, W_B), "rwkv_k_a": (1, W_B),
                "rwkv_r_k": (1, HB_HEADS, HB_DIM), "rwkv_gn_w": (1, W_B), "rwkv_gn_b": (1, W_B),
                "ffn2_norm": (1, D_MODEL), "final_norm": (D_MODEL,)}
PACK_ELEMS = sum(_numel(_shard_shape(n)) for n in PACKED) + sum(_numel(SMALL_SHAPES[n]) for n in SMALL)
PACK_ROWS = -(-PACK_ELEMS // (32 * LANES)) * 32


def _to_rows(name, shard):
    return shard[0].T if name in TRANSPOSED else shard[0]


def _from_rows(name, rows):
    return (rows.T if name in TRANSPOSED else rows)[None]


def _pack(sharded, small):
    flat = jnp.concatenate([sharded[n].reshape(-1) for n in PACKED] + [small[n].reshape(-1) for n in SMALL])
    return jnp.pad(flat, (0, PACK_ROWS * LANES - flat.shape[0])).reshape(PACK_ROWS, LANES)


def _unpack(packed):
    flat, out, off = packed.reshape(-1), {}, 0
    for n in PACKED:
        shp = _shard_shape(n)
        out[n] = flat[off:off + _numel(shp)].reshape((1,) + shp)
        off += _numel(shp)
    for n in SMALL:
        shp = SMALL_SHAPES[n]
        out[n] = flat[off:off + _numel(shp)].reshape(shp)
        off += _numel(shp)
    return out


def _quarter(full, name, q):
    shape, ax = SHARDED_SHAPES[name]
    w = shape[ax] // N_CHIPS
    return lax.slice_in_dim(full, q * w, (q + 1) * w, axis=ax)


def kernel(x, ffn1_norm, ffn1_w_gate, ffn1_w_up, ffn1_w_down, mix_norm, w_in, hgrn_lb_logits, hgrn_out_norm, rwkv_shift_mu, rwkv_w0, rwkv_w2, rwkv_a0, rwkv_a2, rwkv_g2, rwkv_k_k, rwkv_k_a, rwkv_r_k, rwkv_gn_w, rwkv_gn_b, w_out, ffn2_norm, ffn2_w_gate, ffn2_w_up, ffn2_w_down, final_norm, loss_target, m_ffn1_norm, m_ffn1_w_gate, m_ffn1_w_up, m_ffn1_w_down, m_mix_norm, m_w_in, m_hgrn_lb_logits, m_hgrn_out_norm, m_rwkv_shift_mu, m_rwkv_w0, m_rwkv_w2, m_rwkv_a0, m_rwkv_a2, m_rwkv_g2, m_rwkv_k_k, m_rwkv_k_a, m_rwkv_r_k, m_rwkv_gn_w, m_rwkv_gn_b, m_w_out, m_ffn2_norm, m_ffn2_w_gate, m_ffn2_w_up, m_ffn2_w_down, m_final_norm, v_ffn1_norm, v_ffn1_w_gate, v_ffn1_w_up, v_ffn1_w_down, v_mix_norm, v_w_in, v_hgrn_lb_logits, v_hgrn_out_norm, v_rwkv_shift_mu, v_rwkv_w0, v_rwkv_w2, v_rwkv_a0, v_rwkv_a2, v_rwkv_g2, v_rwkv_k_k, v_rwkv_k_a, v_rwkv_r_k, v_rwkv_gn_w, v_rwkv_gn_b, v_w_out, v_ffn2_norm, v_ffn2_w_gate, v_ffn2_w_up, v_ffn2_w_down, v_final_norm):
    args = dict(locals())
    wts = {n: args[n] for n in ALL_WEIGHTS}
    moms = {n: args["m_" + n] for n in ALL_WEIGHTS}
    vars_ = {n: args["v_" + n] for n in ALL_WEIGHTS}

    me = 2 * lax.axis_index("x") + lax.axis_index("y")
    c_idx = lax.axis_index("c").astype(jnp.int32).reshape(1)
    me_idx = me.astype(jnp.int32).reshape(1)
    shard_of = {n: _to_rows(n, wts[n]).astype(BF16) for n in BIG}
    shard_of["packed"] = _pack(wts, {n: wts[n] for n in SMALL}).astype(BF16)
    group = {"ffn1": BIG[0:3], "ffn2": BIG[3:6]}

    def slot_bufs(names):
        return [lax.dynamic_update_slice(jnp.zeros((N_CHIPS,) + shard_of[n].shape, BF16), shard_of[n][None],
                                         (me, 0, 0)) for n in names]

    def ffn_weights(tag, gathered):
        return {f"{tag}_wgt": gathered[0].reshape(D_FF, D_MODEL), f"{tag}_wut": gathered[1].reshape(D_FF, D_MODEL),
                f"{tag}_wd": gathered[2].reshape(D_FF, D_MODEL)}

    def w_in_weights(gathered):
        w_in_full = jnp.concatenate([gathered[0][q] for q in range(N_CHIPS)], axis=1)
        return {"w_in_h": w_in_full[:, :N_HGRN_COLS],
                "w_in_r": jnp.pad(w_in_full[:, N_HGRN_COLS:], ((0, 0), (0, N_RWKV_PAD - N_RWKV_COLS)))}

    def mixer_weights(gathered):
        w_out_full = gathered[0].reshape(D_MODEL, D_MODEL)
        packs = gathered[1].reshape(N_CHIPS, PACK_ROWS * LANES)
        full, off = {}, 0
        for n in PACKED:
            shp = _shard_shape(n)
            full[n] = jnp.concatenate([packs[q, off:off + _numel(shp)].reshape(shp) for q in range(N_CHIPS)], axis=1)
            off += _numel(shp)
        zrow = lambda nrow: jnp.zeros((nrow, W_B), BF16)
        return {"w_out_a": w_out_full[:W_A], "w_out_b": w_out_full[W_A:],
                "w2_pad": jnp.concatenate([full["rwkv_w2"], zrow(LORA_PAD - 32)], axis=0),
                "a2_pad": jnp.concatenate([zrow(32), full["rwkv_a2"], zrow(LORA_PAD - 64)], axis=0),
                "g2_pad": jnp.concatenate([zrow(64), full["rwkv_g2"], zrow(LORA_PAD - 160)], axis=0)}

    plan = _Plan()
    w = {}
    plan.carry("ffn1_rms", lambda g: _gather_comm(slot_bufs(group["ffn1"][:2])),
               lambda res, w_: w_.update({"ffn1_wgt": res[0].reshape(D_FF, D_MODEL),
                                          "ffn1_wut": res[1].reshape(D_FF, D_MODEL)}))

    def after_gate_up(res, w_):
        w_["ffn1_wd"] = res[0].reshape(D_FF, D_MODEL)
        w_.update(w_in_weights(res[1:]))

    plan.carry("ffn1_gate_up", lambda g: _gather_comm(slot_bufs(("ffn1_w_down", "w_in"))), after_gate_up)
    plan.carry("ffn1_down", lambda g: _gather_comm(slot_bufs(("w_out", "packed"))),
               lambda res, w_: w_.update(mixer_weights(res)))
    plan.carry("rwkv_fwd", lambda g: _gather_comm(slot_bufs(group["ffn2"])),
               lambda res, w_: w_.update(ffn_weights("ffn2", res)))
    w["ffn1_norm"], w["ffn2_norm"] = ffn1_norm, ffn2_norm
    w["mix_norm"] = mix_norm
    w["lb0"], w["lb1"] = hgrn_lb_logits[0:1], hgrn_lb_logits[1:2]
    w["hgrn_out_norm"] = hgrn_out_norm
    w["mu_pad"] = jnp.pad(rwkv_shift_mu, ((0, 0), (0, N_RWKV_PAD - N_RWKV_COLS)))
    for n in ("rwkv_w0", "rwkv_a0", "rwkv_k_k", "rwkv_k_a", "rwkv_gn_w", "rwkv_gn_b"):
        w[n] = wts[n]
    w["rwkv_r_k"] = rwkv_r_k.reshape(1, W_B)
    w["final_norm"] = final_norm.reshape(1, D_MODEL)

    def reduce_rows(names, gs):
        r1 = _run_comm(_sibling_exchange_comm(gs), "grad_sibling_exchange")
        s4 = [_add_halves(gt, rt, c_idx, f"grad_add_halves_{n}") for gt, rt, n in zip(gs, r1, names)]
        r2 = _run_comm(_chip_exchange_comm(s4), "grad_chip_exchange")
        return [_sum_chips(rt, st, me_idx, f"grad_sum_chips_{n}") for rt, st, n in zip(r2, s4, names)]

    early = {}

    def reduce_early(names, grads_of, sibling_host, chips_host):
        def sibling_comm(g):
            early[names, "gs"] = grads_of(g)
            return _sibling_exchange_comm(early[names, "gs"])

        def after_sibling(res, w_):
            early[names, "s4"] = [_add_halves(gt, rt, c_idx, f"grad_add_halves_{n}")
                                  for gt, rt, n in zip(early[names, "gs"], res, names)]

        def after_chips(res, w_):
            early.update(zip(names, [_sum_chips(rt, st, me_idx, f"grad_sum_chips_{n}")
                                     for rt, st, n in zip(res, early[names, "s4"], names)]))

        plan.carry(sibling_host, sibling_comm, after_sibling)
        plan.carry(chips_host, lambda g: _chip_exchange_comm(early[names, "s4"]), after_chips)

    def proj_grads(g):
        g_w_in = jnp.concatenate([g["w_in_h"], g["w_in_r"][:, :N_RWKV_COLS]], axis=1)
        return [jnp.concatenate([g["w_out_a"], g["w_out_b"]], axis=0).reshape(N_CHIPS, -1, D_MODEL),
                jnp.stack([_quarter(g_w_in, "w_in", q) for q in range(N_CHIPS)])]

    rows_of = lambda keys: (lambda g: [g[k].reshape(N_CHIPS, -1, D_MODEL) for k in keys])
    reduce_early(group["ffn2"], rows_of(("ffn2_wgt", "ffn2_wut", "ffn2_wd")), "hgrn_bwd", "rwkv_bwd")
    reduce_early(("w_out", "w_in"), proj_grads, "mix_drms", "ffn1_dact")
    reduce_early(("ffn1_w_down",), rows_of(("ffn1_wd",)), "ffn1_dwg", "ffn1_dwu")
    reduce_early(("ffn1_w_gate",), rows_of(("ffn1_wgt",)), "ffn1_dwu", "ffn1_dh_g")
    reduce_early(("ffn1_w_up",), rows_of(("ffn1_wut",)), "ffn1_dh_g", "ffn1_dh_u")
    loss_slab, grad_x, g = _local_step(x[0], loss_target[0], w, plan)
    loss = lax.psum(loss_slab[0, 0], ("x", "y", "c"))

    gfull = {
        "rwkv_w2": g["w2_pad"][0:32], "rwkv_a2": g["a2_pad"][32:64], "rwkv_g2": g["g2_pad"][64:160],
    }
    gsmall = {
        "ffn1_norm": g["ffn1_norm"], "mix_norm": g["mix_norm"],
        "hgrn_lb_logits": jnp.concatenate([g["lb0"], g["lb1"]], axis=0), "hgrn_out_norm": g["hgrn_out_norm"],
        "rwkv_shift_mu": g["mu_pad"][:, :N_RWKV_COLS], "rwkv_w0": g["rwkv_w0"], "rwkv_a0": g["rwkv_a0"],
        "rwkv_k_k": g["rwkv_k_k"], "rwkv_k_a": g["rwkv_k_a"], "rwkv_r_k": g["rwkv_r_k"],
        "rwkv_gn_w": g["rwkv_gn_w"], "rwkv_gn_b": g["rwkv_gn_b"], "ffn2_norm": g["ffn2_norm"],
        "final_norm": g["final_norm"],
    }
    packed = jnp.stack([_pack({n: _quarter(gfull[n], n, q) for n in PACKED}, gsmall) for q in range(N_CHIPS)])
    early["packed"], = reduce_rows(["packed"], [packed])
    names = list(BIG) + ["packed"]
    own = [early[n] for n in names]
    other = _run_comm(_sibling_swap_comm(own), "grad_sibling_swap")

    def rows_list(d):
        return [_to_rows(n, d[n]) for n in BIG] + [_pack(d, {n: d[n] for n in SMALL})]

    outs = [_adamw(wt, go, gx, mt, vt, c_idx, f"adamw_{n}")
            for wt, go, gx, mt, vt, n in zip(rows_list(wts), own, other, rows_list(moms), rows_list(vars_), names)]
    results = []
    for k in range(4):
        per = [outs[i][k] for i in range(len(names))]
        d = {n: _from_rows(n, z) for n, z in zip(BIG, per[:-1])}
        d.update(_unpack(per[-1]))
        results.append(d)
    return (loss, grad_x[None], *[r[n] for r in results for n in ALL_WEIGHTS])
```

```python
import collections
import functools

import jax
import jax.numpy as jnp
from jax import lax
from jax.experimental import pallas as pl
from jax.experimental.pallas import tpu as pltpu

F32 = jnp.float32
BF16 = jnp.bfloat16
SDS = jax.ShapeDtypeStruct
MESH = pl.DeviceIdType.MESH

D_MODEL = 1024
D_FF = 2816
W_A = 512
W_B = 512
HA_HEADS, HA_DIM = 4, 128
HB_HEADS, HB_DIM = 8, 64
HGRN_CHUNK = 64
HGRN_GROUP = 4
RWKV_CHUNK = 16
RWKV_GROUP = 8
N_HGRN_COLS = 4 * W_A
N_RWKV_COLS = 3 * W_B + 32 + 32 + 96
N_RWKV_PAD = 1792
LORA_PAD = 256
NORM_EPS = 1e-6
RWKV_GN_EPS = 64e-5
L2_EPS = 1e-12
ADAM_LR, ADAM_B1, ADAM_B2, ADAM_EPS, ADAM_WD, ADAM_STEP = 0.001, 0.9, 0.999, 1e-8, 0.01, 10

N_CHIPS = 4
VMEM_LIMIT_V7X = 56 * 1024 * 1024
LANES = 1024

SHARDED_SHAPES = {
    "ffn1_w_gate": ((D_MODEL, D_FF), 1), "ffn1_w_up": ((D_MODEL, D_FF), 1), "ffn1_w_down": ((D_FF, D_MODEL), 0),
    "w_in": ((D_MODEL, N_HGRN_COLS + N_RWKV_COLS), 1), "rwkv_w2": ((32, W_B), 1), "rwkv_a2": ((32, W_B), 1),
    "rwkv_g2": ((96, W_B), 1), "w_out": ((D_MODEL, D_MODEL), 0),
    "ffn2_w_gate": ((D_MODEL, D_FF), 1), "ffn2_w_up": ((D_MODEL, D_FF), 1), "ffn2_w_down": ((D_FF, D_MODEL), 0),
}
SMALL = ("ffn1_norm", "mix_norm", "hgrn_lb_logits", "hgrn_out_norm", "rwkv_shift_mu", "rwkv_w0", "rwkv_a0",
         "rwkv_k_k", "rwkv_k_a", "rwkv_r_k", "rwkv_gn_w", "rwkv_gn_b", "ffn2_norm", "final_norm")
ALL_WEIGHTS = ("ffn1_norm", "ffn1_w_gate", "ffn1_w_up", "ffn1_w_down", "mix_norm", "w_in", "hgrn_lb_logits",
               "hgrn_out_norm", "rwkv_shift_mu", "rwkv_w0", "rwkv_w2", "rwkv_a0", "rwkv_a2", "rwkv_g2", "rwkv_k_k",
               "rwkv_k_a", "rwkv_r_k", "rwkv_gn_w", "rwkv_gn_b", "w_out", "ffn2_norm", "ffn2_w_gate", "ffn2_w_up",
               "ffn2_w_down", "final_norm")


def _shard_shape(name):
    shape, ax = SHARDED_SHAPES[name]
    return tuple(s // N_CHIPS if i == ax else s for i, s in enumerate(shape))


def _numel(shape):
    n = 1
    for s in shape:
        n *= s
    return n


def _params(sem=None):
    return pltpu.CompilerParams(dimension_semantics=sem, vmem_limit_bytes=VMEM_LIMIT_V7X)


def _split2(x):
    hi = x.astype(BF16)
    return hi, (x.astype(F32) - hi.astype(F32)).astype(BF16)


def _dg(x, y, cx, cy, hi):
    dn = (((cx,), (cy,)), ((), ()))
    dot = lambda p, q: lax.dot_general(p, q, dn, preferred_element_type=F32)
    if hi == "x3":
        (xh, xl), (yh, yl) = _split2(x), _split2(y)
        return dot(xh, yh) + (dot(xh, yl) + dot(xl, yh))
    return dot(x.astype(BF16), y.astype(BF16))


def _make_mm(hi, cotangent_forms=None):
    @jax.custom_vjp
    def nn(x, y):
        return _dg(x, y, 1, 0, hi)

    @jax.custom_vjp
    def nt(x, y):
        return _dg(x, y, 1, 1, hi)

    @jax.custom_vjp
    def tn(x, y):
        return _dg(x, y, 0, 0, hi)

    bnn, bnt, btn = cotangent_forms or (nn, nt, tn)
    nn.defvjp(lambda x, y: (nn(x, y), (x, y)), lambda r, g: (bnt(g, r[1]), btn(r[0], g)))
    nt.defvjp(lambda x, y: (nt(x, y), (x, y)), lambda r, g: (bnn(g, r[1]), btn(g, r[0])))
    tn.defvjp(lambda x, y: (tn(x, y), (x, y)), lambda r, g: (bnt(r[1], g), bnn(r[0], g)))
    return nn, nt, tn


_nn, _nt, _tn = _make_mm(False)
_nn_x3, _nt_x3, _tn_x3 = _make_mm("x3", (_nn, _nt, _tn))


def _tri_apply(x, transpose):
    c = x.shape[0]
    tri = (lax.broadcasted_iota(jnp.int32, (c, c), 1) <= lax.broadcasted_iota(jnp.int32, (c, c), 0)).astype(BF16)
    dn = (((0 if transpose else 1,), (0,)), ((), ()))
    p1, p2 = _split2(x)
    dot = lambda p: lax.dot_general(tri, p, dn, preferred_element_type=F32)
    return dot(p1) + dot(p2)


@jax.custom_vjp
def _cumsum_rows(x):
    return _tri_apply(x, False)


_cumsum_rows.defvjp(lambda x: (_tri_apply(x, False), None), lambda _, g: (_tri_apply(g, True),))


def _sigmoid(x):
    return 1.0 / (1.0 + jnp.exp(-x))


def _silu(x):
    return x * _sigmoid(x)


def _softplus(z):
    return jnp.maximum(z, 0.0) + jnp.log(1.0 + jnp.exp(-jnp.abs(z)))


def _mm(a, b, *, ta=False, tb=False, tm, tn, tk, name, out_dtype=F32, res=None, scale=None, comm=None):
    m = a.shape[1] if ta else a.shape[0]
    kdim = a.shape[0] if ta else a.shape[1]
    n = b.shape[0] if tb else b.shape[1]
    assert (b.shape[1] if tb else b.shape[0]) == kdim
    tm, tn, tk = min(tm, m), min(tn, n), min(tk, kdim)
    assert m % tm == 0 and n % tn == 0 and kdim % tk == 0, (name, m, n, kdim)
    nk = kdim // tk
    a_spec = pl.BlockSpec((tk, tm), lambda i, j, k: (k, i)) if ta else pl.BlockSpec((tm, tk), lambda i, j, k: (i, k))
    b_spec = pl.BlockSpec((tn, tk), lambda i, j, k: (j, k)) if tb else pl.BlockSpec((tk, tn), lambda i, j, k: (k, j))
    o_spec = pl.BlockSpec((tm, tn), lambda i, j, k: (i, j))
    ca, cb = (0 if ta else 1), (1 if tb else 0)

    def body(*refs):
        if res is not None:
            a_ref, b_ref, r_ref, o_ref, acc_ref = refs
        else:
            a_ref, b_ref, o_ref, acc_ref = refs
        k = pl.program_id(2)

        @pl.when(k == 0)
        def _():
            acc_ref[...] = jnp.zeros_like(acc_ref)

        acc_ref[...] += _dg(a_ref[...], b_ref[...], ca, cb, False)

        @pl.when(k == nk - 1)
        def _():
            acc = acc_ref[...]
            if scale is not None:
                acc = acc * scale
            if res is not None:
                acc = r_ref[...] + acc
            o_ref[...] = acc.astype(out_dtype)

    in_specs = [a_spec, b_spec] + ([o_spec] if res is not None else [])
    args = (a, b) + ((res,) if res is not None else ())
    if comm is None:
        return pl.pallas_call(
            body, name=name, grid=(m // tm, n // tn, nk), in_specs=in_specs, out_specs=o_spec,
            out_shape=SDS((m, n), out_dtype), scratch_shapes=[pltpu.VMEM((tm, tn), F32)],
            compiler_params=_params(("parallel", "parallel", "arbitrary")))(*args)
    (out,), carried = _hosting_call(
        body, comm, name=name, grid=(m // tm, n // tn, nk), in_specs=in_specs, out_specs=[o_spec],
        out_shape=[SDS((m, n), out_dtype)], scratch_shapes=[pltpu.VMEM((tm, tn), F32)], args=args)
    return out, carried


def _row_spec(x, tm, tile_of=lambda i: i):
    if isinstance(x, tuple):
        arr, w, j = x
        return arr, pl.BlockSpec((tm, w), lambda i, j=j: (tile_of(i), j))
    return x, pl.BlockSpec((tm, x.shape[1]), lambda i: (tile_of(i), 0))


def _par_spec(p):
    if isinstance(p, tuple):
        arr, w, j = p
        return arr, pl.BlockSpec((arr.shape[0], w), lambda i, j=j: (0, j))
    return p, pl.BlockSpec(p.shape, lambda i: (0, 0))


def _store_groups(refs, groups, vals):
    for ref, idxs in zip(refs, groups):
        off = 0
        for ix in idxs:
            v = vals[ix]
            ref[:, off:off + v.shape[1]] = v.astype(ref.dtype)
            off += v.shape[1]


SUBLANES = 8


def _x_plan(xs, tm, t, tile_of=lambda i: i):
    arrays, specs, plan = [], [], []
    nb = tm // SUBLANES
    for x in xs:
        if isinstance(x, tuple) and isinstance(x[0], str):
            kind, arr, w, j = x
            if kind == "prev":
                halo = lambda i, j=j: (jnp.maximum(tile_of(i) * nb - 1, 0), j)
            else:
                halo = lambda i, j=j: (jnp.minimum((tile_of(i) + 1) * nb, t // SUBLANES - 1), j)
            arrays += [arr, arr]
            specs += [pl.BlockSpec((tm, w), lambda i, j=j: (tile_of(i), j)), pl.BlockSpec((SUBLANES, w), halo)]
            plan.append((kind, 2, w))
        else:
            arr, spec = _row_spec(x, tm, tile_of)
            arrays.append(arr)
            specs.append(spec)
            plan.append(("plain", 1, spec.block_shape[1]))
    return arrays, specs, plan


def _x_vals(refs, plan, tm, nt, tile_of=lambda i: i):
    vals, k = [], 0
    i = tile_of(pl.program_id(0))
    rows = lax.broadcasted_iota(jnp.int32, (tm, 1), 0)
    for kind, n, _ in plan:
        main = refs[k][...].astype(F32)
        if kind == "prev":
            edge = jnp.where(i == 0, 0.0, refs[k + 1][SUBLANES - 1:SUBLANES, :].astype(F32))
            main = jnp.where(rows == 0, edge, pltpu.roll(main, 1, 0))
        elif kind == "next":
            edge = jnp.where(i == nt - 1, 0.0, refs[k + 1][0:1, :].astype(F32))
            main = jnp.where(rows == tm - 1, edge, pltpu.roll(main, tm - 1, 0))
        vals.append(main)
        k += n
    return vals


def _tile_rows(xs, tm):
    arr = xs[0]
    if isinstance(arr, tuple):
        arr = arr[1] if isinstance(arr[0], str) else arr[0]
    return min(tm, arr.shape[0]), arr.shape[0]


def _rowwise(f, xs, params, out_groups, out_dtypes, *, tm, name, comm=None):
    tm, t = _tile_rows(xs, tm)
    nt = t // tm
    xa, xspecs, plan = _x_plan(xs, tm, t)
    pa, pspecs = (zip(*[_par_spec(p) for p in params]) if params else ((), ()))
    nxr, npar = len(xa), len(pa)
    x_sds = [SDS((tm, w), F32) for _, _, w in plan]
    p_sds = [SDS(s.block_shape, F32) for s in pspecs]
    outs_sds = jax.eval_shape(lambda *vals: f(*vals), *x_sds, *p_sds)
    widths = [sum(outs_sds[ix].shape[1] for ix in idxs) for idxs in out_groups]

    def body(*refs):
        vals = _x_vals(refs[:nxr], plan, tm, nt) + [r[...].astype(F32) for r in refs[nxr:nxr + npar]]
        outs = f(*vals)
        _store_groups(refs[nxr + npar:], out_groups, outs)

    res, carried = _hosting_call(
        body, comm, name=name, grid=(nt,), in_specs=list(xspecs) + list(pspecs),
        out_specs=[pl.BlockSpec((tm, w), lambda i: (i, 0)) for w in widths],
        out_shape=[SDS((t, w), dt) for w, dt in zip(widths, out_dtypes)], scratch_shapes=[], args=(*xa, *pa))
    return res if comm is None else (res, carried)


def _rowwise_bwd(f, xs, params, cots, *, x_grad, p_grad, dx_groups, dx_dtypes, tm, name, extra=None, comm=None,
                 fold_next=None):
    tm, t = _tile_rows(xs, tm)
    nt = t // tm
    tile_of = (lambda i: nt - 1 - i) if fold_next else (lambda i: i)
    xa, xspecs, plan = _x_plan(xs, tm, t, tile_of)
    pa, pspecs = (zip(*[_par_spec(p) for p in params]) if params else ((), ()))
    ca, cspecs = zip(*[_row_spec(c, tm, tile_of) for c in cots])
    extra = extra or {}
    ekeys = sorted(extra)
    ea, especs = (zip(*[_row_spec(extra[k], tm, tile_of) for k in ekeys]) if ekeys else ((), ()))
    nx, nxr, npar, nc, ne = len(plan), len(xa), len(pa), len(ca), len(ea)
    gx = [i for i in range(nx) if x_grad[i]]
    gp = [i for i in range(npar) if p_grad[i]]
    all_widths = [sum(plan[gx[ix]][2] for ix in idxs) for idxs in dx_groups]
    emitted = [k for k in range(len(dx_groups)) if not (fold_next and k == fold_next[1])]
    widths = [all_widths[k] for k in emitted]
    ng = len(emitted)

    def body(*refs):
        ins = refs[:nxr + npar + nc + ne]
        outs = refs[nxr + npar + nc + ne:]
        vals = _x_vals(ins[:nxr], plan, tm, nt, tile_of) + [r[...].astype(F32) for r in ins[nxr:nxr + npar]]
        cvals = tuple(r[...].astype(F32) for r in ins[nxr + npar:nxr + npar + nc])
        evals = [r[...].astype(F32) for r in ins[nxr + npar + nc:]]
        diff_idx = gx + [nx + i for i in gp]

        def g(*dargs):
            full = list(vals)
            for ix, v in zip(diff_idx, dargs):
                full[ix] = v
            return tuple(f(*full))

        _, vjp = jax.vjp(g, *[vals[ix] for ix in diff_idx])
        grads = vjp(cvals)
        dxs = list(grads[:len(gx)])
        for k, ev in zip(ekeys, evals):
            dxs[k] = dxs[k] + ev
        _store_groups(outs[:ng], [dx_groups[k] for k in emitted], dxs)
        i = pl.program_id(0)
        if fold_next:
            main_ref, carry_ref = outs[emitted.index(fold_next[0])], refs[-1]
            rows = lax.broadcasted_iota(jnp.int32, (tm, 1), 0)
            off = 0
            for ix in dx_groups[fold_next[1]]:
                piece = dxs[ix]
                cols = slice(off, off + piece.shape[1])
                edge = jnp.where(i == 0, 0.0, carry_ref[0:1, cols])
                main_ref[:, cols] += jnp.where(rows == tm - 1, edge, pltpu.roll(piece, tm - 1, 0))
                carry_ref[:, cols] = piece[:SUBLANES]
                off += piece.shape[1]
        for ref, gval in zip(outs[ng:ng + len(gp)], grads[len(gx):]):
            @pl.when(i == 0)
            def _(ref=ref):
                ref[...] = jnp.zeros_like(ref)
            ref[...] += gval

    dp_specs = [pl.BlockSpec(pspecs[i].block_shape, lambda i: (0, 0)) for i in gp]
    dp_shapes = [SDS(pspecs[i].block_shape, F32) for i in gp]
    scratch = [pltpu.VMEM((SUBLANES, all_widths[fold_next[1]]), F32)] if fold_next else []
    res, carried = _hosting_call(
        body, comm, name=name, grid=(nt,), in_specs=list(xspecs) + list(pspecs) + list(cspecs) + list(especs),
        out_specs=[pl.BlockSpec((tm, w), lambda i: (tile_of(i), 0)) for w in widths] + dp_specs,
        out_shape=[SDS((t, w), dt) for w, dt in zip(widths, dx_dtypes)] + dp_shapes, scratch_shapes=scratch,
        args=(*xa, *pa, *ca, *ea))
    return res if comm is None else (res, carried)


def _rms_f(x, g):
    return (x * lax.rsqrt(jnp.mean(x * x, axis=-1, keepdims=True) + NORM_EPS) * g,)


def _group_sum_impl(x, ones_bd):
    p1, p2 = _split2(x)
    dot = lambda p: lax.dot_general(p, ones_bd.astype(BF16), (((1,), (0,)), ((), ())), preferred_element_type=F32)
    return dot(p1) + dot(p2)


@jax.custom_vjp
def _group_sum(x, ones_bd):
    return _group_sum_impl(x, ones_bd)


_group_sum.defvjp(lambda x, o: (_group_sum_impl(x, o), o),
                  lambda o, g: (_group_sum_impl(g, o), jnp.zeros_like(o)))


def _rwkv_prep_f(r, k, v, lo, rp, kp, vp, lop, mu_r, mu_k, mu_v, mu_lo, w0, w2p, a0, a2p, g2p, k_k, k_a, ones_bd):
    r = r + mu_r * (rp - r)
    k = k + mu_k * (kp - k)
    v = v + mu_v * (vp - v)
    lo = lo + mu_lo * (lop - lo)
    w_log = -_softplus(-(w0 + _nn(jnp.tanh(lo), w2p))) - 0.5
    lw = -jnp.exp(w_log)
    a_g = _sigmoid(a0 + _nn(lo, a2p))
    g = _nn(_sigmoid(lo), g2p)
    kk = k * k_k
    kk = kk / jnp.maximum(jnp.sqrt(_group_sum(kk * kk, ones_bd)), L2_EPS)
    k2 = k * (1.0 + (a_g - 1.0) * k_a)
    return r, lw, k2, v, -kk, kk * a_g, g


def _rwkv_post_f(y, r, k2, v, g, r_k, gn_w, gn_b, ones_bd):
    inv_n = 1.0 / HB_DIM
    mean = _group_sum(y, ones_bd) * inv_n
    yc = y - mean
    var = _group_sum(yc * yc, ones_bd) * inv_n
    yn = yc * lax.rsqrt(var + RWKV_GN_EPS) * gn_w + gn_b
    bonus = _group_sum(r * k2 * r_k, ones_bd) * v
    return ((yn + bonus) * g,)


def _tri(c, strict=False):
    ii = lax.broadcasted_iota(jnp.int32, (c, c), 0)
    jj = lax.broadcasted_iota(jnp.int32, (c, c), 1)
    return (jj < ii) if strict else (jj <= ii)


def _hgrn_step(st0, q_a, f_a, i_a, g_a, l0, l1, onorm):
    nh, nj = len(q_a), len(q_a[0])
    c = q_a[0][0].shape[0]
    combos = [(j, h) for j in range(nj) for h in range(nh)]
    every = lambda fn: {q: fn(q) for q in combos}
    at_ = lambda d: (lambda q: d[q[1]][q[0]])
    qa_, fa_, ia_, ga_ = (at_(z) for z in (q_a, f_a, i_a, g_a))
    incl = _tri(c)
    rows = lax.broadcasted_iota(jnp.int32, (c, 1), 0)
    lb = []
    for h in range(nh):
        mx = jnp.maximum(l0[h], l1[h])
        e0, e1 = jnp.exp(l0[h] - mx), jnp.exp(l1[h] - mx)
        lb.append(e0 / (e0 + e1))
    forget = every(lambda q: lb[q[1]] + (1.0 - lb[q[1]]) * _sigmoid(fa_(q)))
    qs = every(lambda q: _silu(qa_(q)))
    kk = every(lambda q: 1.0 - forget[q])
    lf = every(lambda q: jnp.log(forget[q]))
    bcum = every(lambda q: _cumsum_rows(lf[q]))
    bref = every(lambda q: jnp.sum(jnp.where(rows <= c // 2, lf[q], 0.0), axis=0, keepdims=True))
    blast = every(lambda q: jnp.sum(lf[q], axis=0, keepdims=True))
    scores = every(lambda q: jnp.where(incl, _nt(qs[q] * jnp.exp(bcum[q] - bref[q]),
                                                 kk[q] * jnp.exp(bref[q] - bcum[q])), 0.0))
    intra = every(lambda q: _nn(scores[q], ia_(q)))
    qb = every(lambda q: qs[q] * jnp.exp(bcum[q]))
    upd = every(lambda q: _tn(ia_(q), kk[q] * jnp.exp(blast[q] - bcum[q])))
    dec = every(lambda q: jnp.exp(blast[q]))
    st = list(st0)
    o = {}
    for j in range(nj):
        for h in range(nh):
            o[(j, h)] = intra[(j, h)] + _nt(qb[(j, h)], st[h])
        st = [st[h] * dec[(j, h)] + upd[(j, h)] for h in range(nh)]
    out = every(lambda q: o[q] * lax.rsqrt(jnp.mean(o[q] * o[q], axis=-1, keepdims=True) + NORM_EPS)
                * onorm[q[1]] * _silu(ga_(q)))
    return [[out[(j, h)] for j in range(nj)] for h in range(nh)], st


def _hgrn_blocks(ref, nj, c):
    return [[ref[j * c:(j + 1) * c, h * HA_DIM:(h + 1) * HA_DIM] for j in range(nj)] for h in range(HA_HEADS)]


def _hgrn_cols(ref):
    return [ref[:, h * HA_DIM:(h + 1) * HA_DIM] for h in range(HA_HEADS)]


def _hgrn_fwd(p_h, l0, l1, onorm):
    t = p_h.shape[0]
    cc, nj = HGRN_CHUNK, HGRN_GROUP
    c = cc * nj
    n = t // c

    def body(q_ref, f_ref, i_ref, g_ref, l0_ref, l1_ref, on_ref, o_ref, hs_ref, st_ref):
        @pl.when(pl.program_id(0) == 0)
        def _():
            st_ref[...] = jnp.zeros_like(st_ref)

        hs_ref[0] = st_ref[...]
        o, st1 = _hgrn_step([st_ref[h] for h in range(HA_HEADS)],
                            *[_hgrn_blocks(ref, nj, cc) for ref in (q_ref, f_ref, i_ref, g_ref)],
                            _hgrn_cols(l0_ref), _hgrn_cols(l1_ref), _hgrn_cols(on_ref))
        for h in range(HA_HEADS):
            for j in range(nj):
                o_ref[j * cc:(j + 1) * cc, h * HA_DIM:(h + 1) * HA_DIM] = o[h][j]
            st_ref[h] = st1[h]

    col = lambda j: pl.BlockSpec((c, W_A), lambda i, j=j: (i, j))
    par = pl.BlockSpec((1, W_A), lambda i: (0, 0))
    return pl.pallas_call(
        body, name="hgrn_fwd", grid=(n,), in_specs=[col(0), col(1), col(2), col(3), par, par, par],
        out_specs=[pl.BlockSpec((c, W_A), lambda i: (i, 0)),
                   pl.BlockSpec((1, HA_HEADS, HA_DIM, HA_DIM), lambda i: (i, 0, 0, 0))],
        out_shape=[SDS((t, W_A), F32), SDS((n, HA_HEADS, HA_DIM, HA_DIM), F32)],
        scratch_shapes=[pltpu.VMEM((HA_HEADS, HA_DIM, HA_DIM), F32)],
        compiler_params=_params(("arbitrary",)))(p_h, p_h, p_h, p_h, l0, l1, onorm)


def _hgrn_bwd(p_h, l0, l1, onorm, hs, do, do_col, comm=None):
    t = p_h.shape[0]
    cc, nj = HGRN_CHUNK, HGRN_GROUP
    c = cc * nj
    n = t // c

    def body(q_ref, f_ref, i_ref, g_ref, l0_ref, l1_ref, on_ref, hs_ref, do_ref,
             dp_ref, dl0_ref, dl1_ref, don_ref, dst_ref):
        @pl.when(pl.program_id(0) == 0)
        def _():
            dst_ref[...] = jnp.zeros_like(dst_ref)
            dl0_ref[...] = jnp.zeros_like(dl0_ref)
            dl1_ref[...] = jnp.zeros_like(dl1_ref)
            don_ref[...] = jnp.zeros_like(don_ref)

        args = ([hs_ref[0, h] for h in range(HA_HEADS)],
                *[_hgrn_blocks(ref, nj, cc) for ref in (q_ref, f_ref, i_ref, g_ref)],
                _hgrn_cols(l0_ref), _hgrn_cols(l1_ref), _hgrn_cols(on_ref))
        _, vjp = jax.vjp(_hgrn_step, *args)
        dst0, dq, df, di, dg, dl0, dl1, don = vjp((_hgrn_blocks(do_ref, nj, cc),
                                                   [dst_ref[h] for h in range(HA_HEADS)]))
        for h in range(HA_HEADS):
            sl = slice(h * HA_DIM, (h + 1) * HA_DIM)
            for k, dv in enumerate((dq, df, di, dg)):
                for j in range(nj):
                    dp_ref[j * cc:(j + 1) * cc, k * W_A + h * HA_DIM:k * W_A + (h + 1) * HA_DIM] = dv[h][j]
            dl0_ref[:, sl] += dl0[h]
            dl1_ref[:, sl] += dl1[h]
            don_ref[:, sl] += don[h]
            dst_ref[h] = dst0[h]

    col = lambda j: pl.BlockSpec((c, W_A), lambda i, j=j: (n - 1 - i, j))
    par = pl.BlockSpec((1, W_A), lambda i: (0, 0))
    return _hosting_call(
        body, comm, name="hgrn_bwd", grid=(n,),
        in_specs=[col(0), col(1), col(2), col(3), par, par, par,
                  pl.BlockSpec((1, HA_HEADS, HA_DIM, HA_DIM), lambda i: (n - 1 - i, 0, 0, 0)),
                  pl.BlockSpec((c, W_A), lambda i: (n - 1 - i, do_col))],
        out_specs=[pl.BlockSpec((c, N_HGRN_COLS), lambda i: (n - 1 - i, 0)), par, par, par],
        out_shape=[SDS((t, N_HGRN_COLS), F32), SDS((1, W_A), F32), SDS((1, W_A), F32), SDS((1, W_A), F32)],
        scratch_shapes=[pltpu.VMEM((HA_HEADS, HA_DIM, HA_DIM), F32)],
        args=(p_h, p_h, p_h, p_h, l0, l1, onorm, hs, do))


HB_PAIRS = HB_HEADS // 2
PAIR_W = 2 * HB_DIM


def _head_lane_masks():
    lane = lax.broadcasted_iota(jnp.int32, (1, PAIR_W), 1)
    return (lane < HB_DIM).astype(F32), (lane >= HB_DIM).astype(F32)


@jax.custom_vjp
def _stack_heads(x):
    m0, m1 = _head_lane_masks()
    return jnp.concatenate([x * m0, x * m1], axis=0)


def _stack_heads_bwd(_, g):
    m0, m1 = _head_lane_masks()
    c = g.shape[0] // 2
    return (g[:c] * m0 + g[c:] * m1,)


_stack_heads.defvjp(lambda x: (_stack_heads(x), None), _stack_heads_bwd)


@jax.custom_vjp
def _unstack_heads(ys):
    c = ys.shape[0] // 2
    return ys[:c] + ys[c:]


_unstack_heads.defvjp(lambda ys: (_unstack_heads(ys), None), lambda _, g: (_stack_heads(g),))


def _same_head_block(c):
    ii = lax.broadcasted_iota(jnp.int32, (2 * c, 2 * c), 0)
    jj = lax.broadcasted_iota(jnp.int32, (2 * c, 2 * c), 1)
    same = (ii < c) == (jj < c)
    return same & (jj <= ii), same & (jj < ii), (ii == jj).astype(F32)


@jax.custom_vjp
def _rows_join(top, bottom):
    return jnp.concatenate([top, bottom], axis=0)


def _rows_join_bwd(n_top, g):
    return g[:n_top], g[n_top:]


_rows_join.defvjp(lambda top, bottom: (_rows_join(top, bottom), top.shape[0]), _rows_join_bwd)


def _rows_split_impl(x, n_top):
    return x[:n_top], x[n_top:]


_rows_split = jax.custom_vjp(_rows_split_impl, nondiff_argnums=(1,))
_rows_split.defvjp(lambda x, n_top: (_rows_split_impl(x, n_top), None),
                   lambda n_top, _, g: (jnp.concatenate([g[0], g[1]], axis=0),))


def _rwkv_step(s0, r, lw, k, v, a, b):
    npair, nj = len(r), len(r[0])
    c = r[0][0].shape[0]
    combos = [(j, p) for j in range(nj) for p in range(npair)]
    every = lambda fn: {q: fn(q) for q in combos}
    at_ = lambda d: (lambda q: d[q[1]][q[0]])
    r_, lw_, k_, v_, a_, b_ = (at_(z) for z in (r, lw, k, v, a, b))
    incl, strict, eye = _same_head_block(c)

    gam = every(lambda q: _cumsum_rows(lw_(q)))
    gtot = every(lambda q: jnp.sum(lw_(q), axis=0, keepdims=True))
    eneg = every(lambda q: jnp.exp(-gam[q]))
    edec = every(lambda q: jnp.exp(gtot[q] - gam[q]))
    at = every(lambda q: _stack_heads(a_(q) * jnp.exp(gam[q] - lw_(q))))
    rt = every(lambda q: _stack_heads(r_(q) * jnp.exp(gam[q])))
    bt = every(lambda q: _stack_heads(b_(q) * eneg[q]))
    kt = every(lambda q: _stack_heads(k_(q) * eneg[q]))
    bdec = every(lambda q: _stack_heads(b_(q) * edec[q]))
    kdec = every(lambda q: _stack_heads(k_(q) * edec[q]))
    vs = every(lambda q: _stack_heads(v_(q)))
    a_ab = every(lambda q: jnp.where(strict, _nt(at[q], bt[q]), 0.0))
    a_ak = every(lambda q: jnp.where(strict, _nt(at[q], kt[q]), 0.0))
    a_rb = every(lambda q: jnp.where(incl, _nt(rt[q], bt[q]), 0.0))
    a_rk = every(lambda q: jnp.where(incl, _nt(rt[q], kt[q]), 0.0))
    tinv = every(lambda q: eye + a_ab[q])
    pw = a_ab
    span = 2
    while span < c:
        pw = every(lambda q, pw=pw: _nn_x3(pw[q], pw[q]))
        tinv = every(lambda q, pw=pw, tinv=tinv: tinv[q] + _nn_x3(pw[q], tinv[q]))
        span *= 2
    akv = every(lambda q: _nn(a_ak[q], vs[q]))
    w1 = every(lambda q: _nn(tinv[q], at[q]))
    u0 = every(lambda q: _nn(tinv[q], akv[q]))
    wr = every(lambda q: _rows_join(w1[q], rt[q]))
    bk = every(lambda q: _rows_join(bdec[q], kdec[q]))
    yv = every(lambda q: _nn(a_rk[q], vs[q]))
    gdec = every(lambda q: jnp.exp(gtot[q]))

    s = list(s0)
    y = [[None] * nj for _ in range(npair)]
    for j in range(nj):
        both = {p: _rows_split(_nt(wr[(j, p)], s[p]), 2 * c) for p in range(npair)}
        u = {p: both[p][0] + u0[(j, p)] for p in range(npair)}
        for p in range(npair):
            y[p][j] = _unstack_heads(both[p][1] + _nn(a_rb[(j, p)], u[p]) + yv[(j, p)])
        s = [s[p] * gdec[(j, p)] + _tn(_rows_join(u[p], vs[(j, p)]), bk[(j, p)]) for p in range(npair)]
    return y, s


def _rwkv_blocks(ref, nj, c):
    return [[ref[j * c:(j + 1) * c, p * PAIR_W:(p + 1) * PAIR_W] for j in range(nj)] for p in range(HB_PAIRS)]


def _rwkv_fwd(seqs, comm=None):
    t = seqs[0].shape[0]
    c, nj = RWKV_CHUNK, RWKV_GROUP
    n = t // (c * nj)

    def body(r_ref, lw_ref, k_ref, v_ref, a_ref, b_ref, y_ref, hs_ref, st_ref):
        @pl.when(pl.program_id(0) == 0)
        def _():
            st_ref[...] = jnp.zeros_like(st_ref)

        hs_ref[0] = st_ref[...]
        s0 = [st_ref[p] for p in range(HB_PAIRS)]
        y, s1 = _rwkv_step(s0, *[_rwkv_blocks(ref, nj, c) for ref in (r_ref, lw_ref, k_ref, v_ref, a_ref, b_ref)])
        for p in range(HB_PAIRS):
            for j in range(nj):
                y_ref[j * c:(j + 1) * c, p * PAIR_W:(p + 1) * PAIR_W] = y[p][j]
            st_ref[p] = s1[p]

    seq = pl.BlockSpec((c * nj, W_B), lambda i: (i, 0))
    return _hosting_call(
        body, comm, name="rwkv_fwd", grid=(n,), in_specs=[seq] * 6,
        out_specs=[seq, pl.BlockSpec((1, HB_PAIRS, PAIR_W, PAIR_W), lambda i: (i, 0, 0, 0))],
        out_shape=[SDS((t, W_B), F32), SDS((n, HB_PAIRS, PAIR_W, PAIR_W), F32)],
        scratch_shapes=[pltpu.VMEM((HB_PAIRS, PAIR_W, PAIR_W), F32)], args=tuple(seqs))


def _rwkv_bwd(seqs, hs, dy, comm=None):
    t = seqs[0].shape[0]
    c, nj = RWKV_CHUNK, RWKV_GROUP
    n = t // (c * nj)

    def body(r_ref, lw_ref, k_ref, v_ref, a_ref, b_ref, hs_ref, dy_ref,
             dr_ref, dlw_ref, dk_ref, dv_ref, da_ref, db_ref, dst_ref):
        @pl.when(pl.program_id(0) == 0)
        def _():
            dst_ref[...] = jnp.zeros_like(dst_ref)

        s0 = [hs_ref[0, p] for p in range(HB_PAIRS)]
        seq_vals = [_rwkv_blocks(ref, nj, c) for ref in (r_ref, lw_ref, k_ref, v_ref, a_ref, b_ref)]
        _, vjp = jax.vjp(_rwkv_step, s0, *seq_vals)
        grads = vjp((_rwkv_blocks(dy_ref, nj, c), [dst_ref[p] for p in range(HB_PAIRS)]))
        for ref, gr in zip((dr_ref, dlw_ref, dk_ref, dv_ref, da_ref, db_ref), grads[1:]):
            for p in range(HB_PAIRS):
                for j in range(nj):
                    ref[j * c:(j + 1) * c, p * PAIR_W:(p + 1) * PAIR_W] = gr[p][j]
        m0, m1 = _head_lane_masks()
        rows0 = (lax.broadcasted_iota(jnp.int32, (PAIR_W, 1), 0) < HB_DIM).astype(F32)
        blocks = rows0 * m0 + (1.0 - rows0) * m1
        for p in range(HB_PAIRS):
            dst_ref[p] = grads[0][p] * blocks

    seq = pl.BlockSpec((c * nj, W_B), lambda i: (n - 1 - i, 0))
    return _hosting_call(
        body, comm, name="rwkv_bwd", grid=(n,),
        in_specs=[seq] * 6 + [pl.BlockSpec((1, HB_PAIRS, PAIR_W, PAIR_W), lambda i: (n - 1 - i, 0, 0, 0)), seq],
        out_specs=[seq] * 6, out_shape=[SDS((t, W_B), F32)] * 6,
        scratch_shapes=[pltpu.VMEM((HB_PAIRS, PAIR_W, PAIR_W), F32)], args=(*seqs, hs, dy))


def _final_loss(x3, fnorm, target, *, tm):
    t, d = x3.shape

    def body(x_ref, g_ref, t_ref, dx_ref, dg_ref, loss_ref):
        @pl.when(pl.program_id(0) == 0)
        def _():
            dg_ref[...] = jnp.zeros_like(dg_ref)
            loss_ref[...] = jnp.zeros_like(loss_ref)

        x, g = x_ref[...], g_ref[...]
        rinv = lax.rsqrt(jnp.mean(x * x, axis=-1, keepdims=True) + NORM_EPS)
        xh = x * rinv
        diff = xh * g - t_ref[...]
        loss_ref[...] += 0.5 * jnp.sum(jnp.mean(diff * diff, axis=-1, keepdims=True))
        dy = diff * (1.0 / d)
        dg_ref[...] += jnp.sum(dy * xh, axis=0, keepdims=True)
        dxh = dy * g
        dx_ref[...] = rinv * (dxh - xh * jnp.mean(dxh * xh, axis=-1, keepdims=True))

    row = pl.BlockSpec((tm, d), lambda i: (i, 0))
    return pl.pallas_call(
        body, name="final_loss", grid=(t // tm,), in_specs=[row, pl.BlockSpec((1, d), lambda i: (0, 0)), row],
        out_specs=[row, pl.BlockSpec((1, d), lambda i: (0, 0)), pl.BlockSpec((8, 128), lambda i: (0, 0))],
        out_shape=[SDS((t, d), F32), SDS((1, d), F32), SDS((8, 128), F32)],
        compiler_params=_params(("arbitrary",)))(x3, fnorm, target)


def _gate_up_act(h, wgt, wut, *, tm, tn, name, comm=None):
    t, d = h.shape
    tm = min(tm, t)

    def body(h_ref, g_ref, u_ref, a_out, u_out, act_out):
        hv = h_ref[...]
        a = _dg(hv, g_ref[...], 1, 1, False)
        u = _dg(hv, u_ref[...], 1, 1, False)
        a_out[...] = a.astype(a_out.dtype)
        u_out[...] = u.astype(u_out.dtype)
        act_out[...] = (_silu(a) * u).astype(act_out.dtype)

    wspec = pl.BlockSpec((tn, d), lambda i, j: (j, 0))
    ospec = pl.BlockSpec((tm, tn), lambda i, j: (i, j))
    return _hosting_call(
        body, comm, name=name, grid=(t // tm, D_FF // tn),
        in_specs=[pl.BlockSpec((tm, d), lambda i, j: (i, 0)), wspec, wspec], out_specs=[ospec, ospec, ospec],
        out_shape=[SDS((t, D_FF), BF16), SDS((t, D_FF), BF16), SDS((t, D_FF), BF16)], scratch_shapes=[],
        args=(h, wgt, wut))


def _dact_swiglu(dout, wd, a, u, *, tm, tn, name, comm=None):
    t, d = dout.shape
    tm = min(tm, t)

    def body(d_ref, w_ref, a_ref, u_ref, da_out, du_out):
        dact = 0.5 * _dg(d_ref[...], w_ref[...], 1, 1, False)
        av, uv = a_ref[...].astype(F32), u_ref[...].astype(F32)
        s = _sigmoid(av)
        da_out[...] = (dact * uv * (s * (1.0 + av * (1.0 - s)))).astype(da_out.dtype)
        du_out[...] = (dact * (av * s)).astype(du_out.dtype)

    tile = pl.BlockSpec((tm, tn), lambda i, j: (i, j))
    return _hosting_call(
        body, comm, name=name, grid=(t // tm, D_FF // tn),
        in_specs=[pl.BlockSpec((tm, d), lambda i, j: (i, 0)), pl.BlockSpec((tn, d), lambda i, j: (j, 0)), tile, tile],
        out_specs=[tile, tile], out_shape=[SDS((t, D_FF), BF16), SDS((t, D_FF), BF16)], scratch_shapes=[],
        args=(dout, wd, a, u))


class _Plan:
    def __init__(self):
        self.entries, self.counts = collections.defaultdict(list), {}

    def carry(self, host, comm_of, after):
        self.entries[host].append((comm_of, after))

    def comm(self, host, g):
        comms = [comm_of(g) for comm_of, _ in self.entries.get(host, [])]
        self.counts[host] = [len(c.arrays) for c in comms]
        return functools.reduce(_join_comms, comms) if comms else None

    def done(self, host, results, w):
        start = 0
        for (_, after), n in zip(self.entries.get(host, []), self.counts.get(host, [])):
            after(results[start:start + n], w)
            start += n


def _ffn_fwd(x, w, tag, plan, g):
    comm = plan.comm(f"{tag}_rms", g)
    res = _rowwise(_rms_f, [x], [w[f"{tag}_norm"]], [[0]], [BF16], tm=512, name=f"{tag}_rms", comm=comm)
    (h,), carried = res if comm is not None else (res, [])
    plan.done(f"{tag}_rms", carried, w)
    (a, u, act), carried = _gate_up_act(h, w[f"{tag}_wgt"], w[f"{tag}_wut"], tm=2048, tn=256, name=f"{tag}_gate_up",
                                        comm=plan.comm(f"{tag}_gate_up", g))
    plan.done(f"{tag}_gate_up", carried, w)
    comm = plan.comm(f"{tag}_down", g)
    out = _mm(act, w[f"{tag}_wd"], tm=1024, tn=D_MODEL, tk=D_FF, name=f"{tag}_down", res=x, scale=0.5, comm=comm)
    if comm is not None:
        out, carried = out
        plan.done(f"{tag}_down", carried, w)
    return out, (h, a, u, act)


def _ffn_bwd(dout, x, w, saved, tag, plan, g):
    h, a, u, act = saved

    def carrying(fn, host, *args, **kwargs):
        comm = plan.comm(host, g)
        res = fn(*args, name=host, comm=comm, **kwargs)
        out, carried = res if comm is not None else (res, [])
        plan.done(host, carried, w)
        return out

    (da, du), carried = _dact_swiglu(dout, w[f"{tag}_wd"], a, u, tm=2048, tn=256, name=f"{tag}_dact",
                                     comm=plan.comm(f"{tag}_dact", g))
    plan.done(f"{tag}_dact", carried, w)
    g[f"{tag}_wd"] = _mm(act, dout, ta=True, tm=D_FF // 2, tn=D_MODEL, tk=1024, name=f"{tag}_dwd", scale=0.5)
    g[f"{tag}_wgt"] = carrying(_mm, f"{tag}_dwg", da, h, ta=True, tm=D_FF // 2, tn=D_MODEL, tk=1024)
    g[f"{tag}_wut"] = carrying(_mm, f"{tag}_dwu", du, h, ta=True, tm=D_FF // 2, tn=D_MODEL, tk=1024)
    dh = carrying(_mm, f"{tag}_dh_g", da, w[f"{tag}_wgt"], tm=1024, tn=D_MODEL, tk=D_FF)
    dh = carrying(_mm, f"{tag}_dh_u", du, w[f"{tag}_wut"], tm=1024, tn=D_MODEL, tk=D_FF, res=dh)
    dx, g[f"{tag}_norm"] = _rowwise_bwd(_rms_f, [x], [w[f"{tag}_norm"]], [dh], x_grad=[True], p_grad=[True],
                                        dx_groups=[[0]], dx_dtypes=[F32], tm=512, name=f"{tag}_drms",
                                        extra={0: dout})
    return dx


def _local_step(x, target, w, plan=None):
    plan = plan or _Plan()
    ones_bd = jnp.kron(jnp.eye(HB_HEADS, dtype=F32), jnp.ones((HB_DIM, HB_DIM), F32))
    g = {}
    x1, ffn1_saved = _ffn_fwd(x, w, "ffn1", plan, g)
    hm, = _rowwise(_rms_f, [x1], [w["mix_norm"]], [[0]], [BF16], tm=512, name="mix_rms")
    p_h = _mm(hm, w["w_in_h"], tm=2048, tn=256, tk=D_MODEL, name="inproj_h")
    p_r = _mm(hm, w["w_in_r"], tm=2048, tn=256, tk=D_MODEL, name="inproj_r")
    o_a, hgrn_states = _hgrn_fwd(p_h, w["lb0"], w["lb1"], w["hgrn_out_norm"])

    mu = w["mu_pad"]
    prep_xs = [(p_r, W_B, 0), (p_r, W_B, 1), (p_r, W_B, 2), (p_r, LORA_PAD, 6),
               ("prev", p_r, W_B, 0), ("prev", p_r, W_B, 1), ("prev", p_r, W_B, 2), ("prev", p_r, LORA_PAD, 6)]
    prep_ps = [(mu, W_B, 0), (mu, W_B, 1), (mu, W_B, 2), (mu, LORA_PAD, 6), w["rwkv_w0"], w["w2_pad"], w["rwkv_a0"],
               w["a2_pad"], w["g2_pad"], w["rwkv_k_k"], w["rwkv_k_a"], ones_bd]
    prep_f = _rwkv_prep_f
    r, lw, k2, v, a_vec, b_vec, gate = _rowwise(prep_f, prep_xs, prep_ps, [[0], [1], [2], [3], [4], [5], [6]],
                                                [F32] * 7, tm=256, name="rwkv_prep")
    seqs = [r, lw, k2, v, a_vec, b_vec]
    (y, rwkv_states), carried = _rwkv_fwd(seqs, comm=plan.comm("rwkv_fwd", g))
    plan.done("rwkv_fwd", carried, w)
    post_f = _rwkv_post_f
    post_xs = [y, r, k2, v, gate]
    post_ps = [w["rwkv_r_k"], w["rwkv_gn_w"], w["rwkv_gn_b"], ones_bd]
    o_b, = _rowwise(post_f, post_xs, post_ps, [[0]], [F32], tm=256, name="rwkv_post")
    x2 = _mm(o_a, w["w_out_a"], tm=2048, tn=256, tk=W_A, name="outproj_a", res=x1)
    x2 = _mm(o_b, w["w_out_b"], tm=2048, tn=256, tk=W_B, name="outproj_b", res=x2)
    x3, ffn2_saved = _ffn_fwd(x2, w, "ffn2", plan, g)
    dx3, g["final_norm"], loss = _final_loss(x3, w["final_norm"], target, tm=256)

    dx2 = _ffn_bwd(dx3, x2, w, ffn2_saved, "ffn2", plan, g)
    do_a = _mm(dx2, w["w_out_a"], tb=True, tm=2048, tn=256, tk=D_MODEL, name="outproj_do_a")
    do_b = _mm(dx2, w["w_out_b"], tb=True, tm=2048, tn=256, tk=D_MODEL, name="outproj_do_b")
    g["w_out_a"] = _mm(o_a, dx2, ta=True, tm=W_A, tn=D_MODEL, tk=1024, name="outproj_dw_a")
    g["w_out_b"] = _mm(o_b, dx2, ta=True, tm=W_B, tn=D_MODEL, tk=1024, name="outproj_dw_b")

    (dp_h, g["lb0"], g["lb1"], g["hgrn_out_norm"]), carried = _hgrn_bwd(
        p_h, w["lb0"], w["lb1"], w["hgrn_out_norm"], hgrn_states, do_a, 0, comm=plan.comm("hgrn_bwd", g))
    plan.done("hgrn_bwd", carried, w)
    post_out = _rowwise_bwd(post_f, post_xs, post_ps, [do_b], x_grad=[True] * 5, p_grad=[True] * 3 + [False],
                            dx_groups=[[0], [1], [2], [3], [4]], dx_dtypes=[F32] * 5, tm=256, name="rwkv_post_bwd")
    dy, dr1, dk1, dv1, dgate, g["rwkv_r_k"], g["rwkv_gn_w"], g["rwkv_gn_b"] = post_out
    (dr2, dlw, dk2, dv2, da_vec, db_vec), carried = _rwkv_bwd(seqs, rwkv_states, dy, comm=plan.comm("rwkv_bwd", g))
    plan.done("rwkv_bwd", carried, w)

    def prep2_f(*vals):
        r_, lw_, k2_, v_, a_, b_, g_ = prep_f(*vals)
        return r_, lw_, k2_, v_, a_, b_, g_, r_, k2_, v_

    prep_out = _rowwise_bwd(prep2_f, prep_xs, prep_ps, [dr2, dlw, dk2, dv2, da_vec, db_vec, dgate, dr1, dk1, dv1],
                            x_grad=[True] * 8, p_grad=[True] * 11 + [False], dx_groups=[[0, 1, 2, 3], [4, 5, 6, 7]],
                            dx_dtypes=[F32], tm=256, name="rwkv_prep_bwd", fold_next=(0, 1))
    dp_r = prep_out[0]
    (dmu_r, dmu_k, dmu_v, dmu_lo, g["rwkv_w0"], g["w2_pad"], g["rwkv_a0"], g["a2_pad"], g["g2_pad"],
     g["rwkv_k_k"], g["rwkv_k_a"]) = prep_out[1:]
    g["mu_pad"] = jnp.concatenate([dmu_r, dmu_k, dmu_v, dmu_lo], axis=1)
    dhm = _mm(dp_h, w["w_in_h"], tb=True, tm=1024, tn=D_MODEL, tk=N_HGRN_COLS, name="inproj_dh_h")
    dhm = _mm(dp_r, w["w_in_r"], tb=True, tm=1024, tn=D_MODEL, tk=N_RWKV_PAD, name="inproj_dh_r", res=dhm)
    g["w_in_h"] = _mm(hm, dp_h, ta=True, tm=D_MODEL, tn=D_MODEL, tk=1024, name="inproj_dw_h")
    g["w_in_r"] = _mm(hm, dp_r, ta=True, tm=D_MODEL, tn=N_RWKV_PAD // 2, tk=1024, name="inproj_dw_r")
    mix_comm = plan.comm("mix_drms", g)
    mix_out = _rowwise_bwd(_rms_f, [x1], [w["mix_norm"]], [dhm], x_grad=[True], p_grad=[True], dx_groups=[[0]],
                           dx_dtypes=[F32], tm=512, name="mix_drms", extra={0: dx2}, comm=mix_comm)
    (dx1, g["mix_norm"]), carried = mix_out if mix_comm is not None else (mix_out, [])
    plan.done("mix_drms", carried, w)
    dx0 = _ffn_bwd(dx1, x, w, ffn1_saved, "ffn1", plan, g)
    return loss, dx0, g


HBM_SPEC = pl.BlockSpec(memory_space=pl.ANY)

Comm = collections.namedtuple("Comm", "arrays out_shapes aliased sem_shapes start finish")


def _join_comms(first, second):
    assert first.aliased == second.aliased
    n, s = len(first.arrays), len(first.sem_shapes)

    def start(ins, outs, sems):
        first.start(ins[:n], outs[:n], sems[:s])
        second.start(ins[n:], outs[n:], sems[s:])

    def finish(ins, outs, sems):
        first.finish(ins[:n], outs[:n], sems[:s])
        second.finish(ins[n:], outs[n:], sems[s:])

    return Comm(list(first.arrays) + list(second.arrays), list(first.out_shapes) + list(second.out_shapes),
                first.aliased, list(first.sem_shapes) + list(second.sem_shapes), start, finish)


def _run_comm(comm, name):
    n = len(comm.arrays)

    def body(*refs):
        ins, outs, sems = refs[:n], refs[n:2 * n], refs[2 * n:]
        comm.start(ins, outs, sems)
        comm.finish(ins, outs, sems)

    return pl.pallas_call(
        body, name=name, in_specs=[HBM_SPEC] * n, out_specs=[HBM_SPEC] * n, out_shape=list(comm.out_shapes),
        input_output_aliases={t: t for t in range(n)} if comm.aliased else {},
        scratch_shapes=list(comm.sem_shapes))(*comm.arrays)


def _hosting_call(body, comm, *, name, grid, in_specs, out_specs, out_shape, scratch_shapes, args):
    sem = ("arbitrary",) * len(grid)
    if comm is None:
        res = pl.pallas_call(body, name=name, grid=grid, in_specs=in_specs, out_specs=out_specs, out_shape=out_shape,
                             scratch_shapes=scratch_shapes, compiler_params=_params(sem))(*args)
        return list(res), []
    ni, no, ns, nc = len(in_specs), len(out_specs), len(scratch_shapes), len(comm.arrays)

    def wrapped(*refs):
        ins, cins = refs[:ni], refs[ni:ni + nc]
        outs, couts = refs[ni + nc:ni + nc + no], refs[ni + nc + no:ni + 2 * nc + no]
        scr, sems = refs[ni + 2 * nc + no:ni + 2 * nc + no + ns], refs[ni + 2 * nc + no + ns:]
        first = functools.reduce(jnp.logical_and, [pl.program_id(k) == 0 for k in range(len(grid))])
        last = functools.reduce(jnp.logical_and, [pl.program_id(k) == grid[k] - 1 for k in range(len(grid))])

        @pl.when(first)
        def _():
            comm.start(cins, couts, sems)

        body(*ins, *outs, *scr)

        @pl.when(last)
        def _():
            comm.finish(cins, couts, sems)

    res = pl.pallas_call(
        wrapped, name=name, grid=grid, in_specs=list(in_specs) + [HBM_SPEC] * nc,
        out_specs=list(out_specs) + [HBM_SPEC] * nc, out_shape=list(out_shape) + list(comm.out_shapes),
        scratch_shapes=list(scratch_shapes) + list(comm.sem_shapes),
        input_output_aliases={ni + t: no + t for t in range(nc)} if comm.aliased else {},
        compiler_params=_params(sem))(*args, *comm.arrays)
    return list(res[:no]), list(res[no:])


def _chips(x, y):
    return [(1 - x, y), (x, 1 - y), (1 - x, 1 - y)]


def _gather_comm(bufs):
    n = len(bufs)

    def copies(outs, sems):
        ici_send, ici_recv, d2d_send, d2d_recv = sems
        x, y, c = lax.axis_index("x"), lax.axis_index("y"), lax.axis_index("c")

        def half(t, slot, hc):
            hr = bufs[t].shape[1] // 2
            return outs[t].at[slot, pl.ds(pl.multiple_of(hc * hr, 16), hr), :]

        def ici(t, j, slot, px, py):
            return pltpu.make_async_remote_copy(src_ref=half(t, slot, c), dst_ref=half(t, slot, c),
                                                send_sem=ici_send.at[3 * t + j], recv_sem=ici_recv.at[3 * t + j],
                                                device_id=(px, py, c), device_id_type=MESH)

        def d2d(t, j, slot, hc):
            return pltpu.make_async_remote_copy(src_ref=half(t, slot, hc), dst_ref=half(t, slot, hc),
                                                send_sem=d2d_send.at[3 * t + j], recv_sem=d2d_recv.at[3 * t + j],
                                                device_id=(x, y, 1 - c), device_id_type=MESH)

        peers = [(t, j, px, py) for t in range(n) for j, (px, py) in enumerate(_chips(x, y))]
        return ici, d2d, peers, 2 * x + y, c

    def start(ins, outs, sems):
        ici, _, peers, me, _ = copies(outs, sems)
        for t, j, px, py in peers:
            ici(t, j, me, px, py).start()

    def finish(ins, outs, sems):
        ici, d2d, peers, me, c = copies(outs, sems)
        for t, j, px, py in peers:
            ici(t, j, 2 * px + py, px, py).wait_recv()
            d2d(t, j, 2 * px + py, c).start()
        for t, j, px, py in peers:
            d2d(t, j, 2 * px + py, 1 - c).wait_recv()
        for t, j, px, py in peers:
            ici(t, j, me, px, py).wait_send()
            d2d(t, j, 2 * px + py, c).wait_send()

    return Comm(list(bufs), [SDS(b.shape, b.dtype) for b in bufs], True, [pltpu.SemaphoreType.DMA((3 * n,))] * 4,
                start, finish)


def _sibling_exchange_comm(gs):
    n = len(gs)

    def copies(ins, outs, sems):
        x, y, c = lax.axis_index("x"), lax.axis_index("y"), lax.axis_index("c")
        cps = []
        for t in range(n):
            hr = gs[t].shape[1] // 2
            src = ins[t].at[:, pl.ds(pl.multiple_of((1 - c) * hr, SUBLANES), hr), :]
            cps.append(pltpu.make_async_remote_copy(src_ref=src, dst_ref=outs[t], send_sem=sems[0].at[t],
                                                    recv_sem=sems[1].at[t], device_id=(x, y, 1 - c),
                                                    device_id_type=MESH))
        return cps

    def start(ins, outs, sems):
        for cp in copies(ins, outs, sems):
            cp.start()

    def finish(ins, outs, sems):
        for cp in copies(ins, outs, sems):
            cp.wait()

    return Comm(list(gs), [SDS((N_CHIPS, g.shape[1] // 2, g.shape[2]), g.dtype) for g in gs], False,
                [pltpu.SemaphoreType.DMA((n,))] * 2, start, finish)


def _chip_exchange_comm(ss):
    n = len(ss)

    def copies(ins, outs, sems):
        x, y, c = lax.axis_index("x"), lax.axis_index("y"), lax.axis_index("c")
        me = 2 * x + y

        def copy(t, j, px, py, src_slot, dst_slot):
            return pltpu.make_async_remote_copy(src_ref=ins[t].at[src_slot], dst_ref=outs[t].at[dst_slot],
                                                send_sem=sems[0].at[3 * t + j], recv_sem=sems[1].at[3 * t + j],
                                                device_id=(px, py, c), device_id_type=MESH)

        peers = [(t, j, px, py) for t in range(n) for j, (px, py) in enumerate(_chips(x, y))]
        return copy, peers, me

    def start(ins, outs, sems):
        copy, peers, me = copies(ins, outs, sems)
        for t, j, px, py in peers:
            copy(t, j, px, py, 2 * px + py, me).start()

    def finish(ins, outs, sems):
        copy, peers, me = copies(ins, outs, sems)
        for t, j, px, py in peers:
            copy(t, j, px, py, me, 2 * px + py).wait_recv()
        for t, j, px, py in peers:
            copy(t, j, px, py, 2 * px + py, me).wait_send()

    return Comm(list(ss), [SDS(s.shape, s.dtype) for s in ss], False, [pltpu.SemaphoreType.DMA((3 * n,))] * 2,
                start, finish)


def _sibling_swap_comm(fs):
    n = len(fs)

    def copies(ins, outs, sems):
        x, y, c = lax.axis_index("x"), lax.axis_index("y"), lax.axis_index("c")
        return [pltpu.make_async_remote_copy(src_ref=ins[t], dst_ref=outs[t], send_sem=sems[0].at[t],
                                             recv_sem=sems[1].at[t], device_id=(x, y, 1 - c), device_id_type=MESH)
                for t in range(n)]

    def start(ins, outs, sems):
        for cp in copies(ins, outs, sems):
            cp.start()

    def finish(ins, outs, sems):
        for cp in copies(ins, outs, sems):
            cp.wait()

    return Comm(list(fs), [SDS(f.shape, f.dtype) for f in fs], False, [pltpu.SemaphoreType.DMA((n,))] * 2,
                start, finish)


def _row_tile(rows, cap=512):
    best = SUBLANES
    for tr in range(SUBLANES, min(rows, cap) + 1, SUBLANES):
        if rows % tr == 0:
            best = tr
    return best


def _add_halves(g4, r4, c_idx, name):
    _, hr, lanes = r4.shape
    tr = _row_tile(hr)
    nb = hr // tr

    def body(c_ref, a_ref, b_ref, o_ref):
        o_ref[...] = (a_ref[...] + b_ref[...]).astype(o_ref.dtype)

    grid_spec = pltpu.PrefetchScalarGridSpec(
        num_scalar_prefetch=1, grid=(N_CHIPS, nb),
        in_specs=[pl.BlockSpec((None, tr, lanes), lambda q, i, c_ref: (q, c_ref[0] * nb + i, 0)),
                  pl.BlockSpec((None, tr, lanes), lambda q, i, c_ref: (q, i, 0))],
        out_specs=pl.BlockSpec((None, tr, lanes), lambda q, i, c_ref: (q, i, 0)))
    return pl.pallas_call(body, name=name, grid_spec=grid_spec, out_shape=SDS(r4.shape, BF16),
                          compiler_params=_params(("parallel", "parallel")))(c_idx, g4, r4)


def _sum_chips(r4, s4, me_idx, name):
    _, rows, lanes = r4.shape
    tr = _row_tile(rows)

    def body(me_ref, a_ref, b_ref, c_ref, d_ref, own_ref, o_ref):
        own = own_ref[...].astype(F32)
        p = [jnp.where(me_ref[0] == q, own, ref[...].astype(F32)) for q, ref in enumerate((a_ref, b_ref, c_ref, d_ref))]
        o_ref[...] = ((p[0] + p[1]) + p[2]) + p[3]

    other = lambda q: (lambda i, me_ref: (jnp.where(me_ref[0] == q, (q + 1) % N_CHIPS, q), i, 0))
    grid_spec = pltpu.PrefetchScalarGridSpec(
        num_scalar_prefetch=1, grid=(rows // tr,),
        in_specs=[pl.BlockSpec((None, tr, lanes), other(q)) for q in range(N_CHIPS)]
        + [pl.BlockSpec((None, tr, lanes), lambda i, me_ref: (me_ref[0], i, 0))],
        out_specs=pl.BlockSpec((tr, lanes), lambda i, me_ref: (i, 0)))
    return pl.pallas_call(body, name=name, grid_spec=grid_spec, out_shape=SDS((rows, lanes), F32),
                          compiler_params=_params(("parallel",)))(me_idx, r4, r4, r4, r4, s4)


def _adamw(wf, g_own, g_other, mf, vf, c_idx, name):
    rows, lanes = wf.shape
    hr = rows // 2
    tr = _row_tile(hr)
    nb = hr // tr
    c1 = 1.0 / (1.0 - ADAM_B1 ** ADAM_STEP)
    c2 = 1.0 / (1.0 - ADAM_B2 ** ADAM_STEP)

    def body(c_ref, w_ref, go_ref, gx_ref, m_ref, v_ref, g_ref, d_ref, nm_ref, nv_ref):
        gv = jnp.where(pl.program_id(0) == c_ref[0], go_ref[...], gx_ref[...])
        m = ADAM_B1 * m_ref[...] + (1.0 - ADAM_B1) * gv
        v = ADAM_B2 * v_ref[...] + (1.0 - ADAM_B2) * (gv * gv)
        g_ref[...] = gv
        d_ref[...] = -ADAM_LR * ((m * c1) / (jnp.sqrt(v * c2) + ADAM_EPS) + ADAM_WD * w_ref[...])
        nm_ref[...] = m
        nv_ref[...] = v

    full = pl.BlockSpec((tr, lanes), lambda h, i, c_ref: (h * nb + i, 0))
    half = pl.BlockSpec((tr, lanes), lambda h, i, c_ref: (i, 0))
    grid_spec = pltpu.PrefetchScalarGridSpec(num_scalar_prefetch=1, grid=(2, nb),
                                             in_specs=[full, half, half, full, full], out_specs=[full] * 4)
    return pl.pallas_call(body, name=name, grid_spec=grid_spec, out_shape=[SDS((rows, lanes), F32)] * 4,
                          compiler_params=_params(("parallel", "parallel")))(c_idx, wf, g_own, g_other, mf, vf)


BIG = ("ffn1_w_gate", "ffn1_w_up", "ffn1_w_down", "ffn2_w_gate", "ffn2_w_up", "ffn2_w_down", "w_out", "w_in")
TRANSPOSED = ("ffn1_w_gate", "ffn1_w_up", "ffn2_w_gate", "ffn2_w_up")
PACKED = ("rwkv_w2", "rwkv_a2", "rwkv_g2")
SMALL_SHAPES = {"ffn1_norm": (1, D_MODEL), "mix_norm": (1, D_MODEL), "hgrn_lb_logits": (2, W_A),
                "hgrn_out_norm": (1, W_A), "rwkv_shift_mu": (1, N_RWKV_COLS), "rwkv_w0": (1, W_B),
                "rwkv_a0": (1, W_B), "rwkv_k_k": (1, W_B), "rwkv_k_a": (1, W_B),
                "rwkv_r_k": (1, HB_HEADS, HB_DIM), "rwkv_gn_w": (1, W_B), "rwkv_gn_b": (1, W_B),
                "ffn2_norm": (1, D_MODEL), "final_norm": (D_MODEL,)}
PACK_ELEMS = sum(_numel(_shard_shape(n)) for n in PACKED) + sum(_numel(SMALL_SHAPES[n]) for n in SMALL)
PACK_ROWS = -(-PACK_ELEMS // (32 * LANES)) * 32


def _to_rows(name, shard):
    return shard[0].T if name in TRANSPOSED else shard[0]


def _from_rows(name, rows):
    return (rows.T if name in TRANSPOSED else rows)[None]


def _pack(sharded, small):
    flat = jnp.concatenate([sharded[n].reshape(-1) for n in PACKED] + [small[n].reshape(-1) for n in SMALL])
    return jnp.pad(flat, (0, PACK_ROWS * LANES - flat.shape[0])).reshape(PACK_ROWS, LANES)


def _unpack(packed):
    flat, out, off = packed.reshape(-1), {}, 0
    for n in PACKED:
        shp = _shard_shape(n)
        out[n] = flat[off:off + _numel(shp)].reshape((1,) + shp)
        off += _numel(shp)
    for n in SMALL:
        shp = SMALL_SHAPES[n]
        out[n] = flat[off:off + _numel(shp)].reshape(shp)
        off += _numel(shp)
    return out


def _quarter(full, name, q):
    shape, ax = SHARDED_SHAPES[name]
    w = shape[ax] // N_CHIPS
    return lax.slice_in_dim(full, q * w, (q + 1) * w, axis=ax)


def kernel(x, ffn1_norm, ffn1_w_gate, ffn1_w_up, ffn1_w_down, mix_norm, w_in, hgrn_lb_logits, hgrn_out_norm, rwkv_shift_mu, rwkv_w0, rwkv_w2, rwkv_a0, rwkv_a2, rwkv_g2, rwkv_k_k, rwkv_k_a, rwkv_r_k, rwkv_gn_w, rwkv_gn_b, w_out, ffn2_norm, ffn2_w_gate, ffn2_w_up, ffn2_w_down, final_norm, loss_target, m_ffn1_norm, m_ffn1_w_gate, m_ffn1_w_up, m_ffn1_w_down, m_mix_norm, m_w_in, m_hgrn_lb_logits, m_hgrn_out_norm, m_rwkv_shift_mu, m_rwkv_w0, m_rwkv_w2, m_rwkv_a0, m_rwkv_a2, m_rwkv_g2, m_rwkv_k_k, m_rwkv_k_a, m_rwkv_r_k, m_rwkv_gn_w, m_rwkv_gn_b, m_w_out, m_ffn2_norm, m_ffn2_w_gate, m_ffn2_w_up, m_ffn2_w_down, m_final_norm, v_ffn1_norm, v_ffn1_w_gate, v_ffn1_w_up, v_ffn1_w_down, v_mix_norm, v_w_in, v_hgrn_lb_logits, v_hgrn_out_norm, v_rwkv_shift_mu, v_rwkv_w0, v_rwkv_w2, v_rwkv_a0, v_rwkv_a2, v_rwkv_g2, v_rwkv_k_k, v_rwkv_k_a, v_rwkv_r_k, v_rwkv_gn_w, v_rwkv_gn_b, v_w_out, v_ffn2_norm, v_ffn2_w_gate, v_ffn2_w_up, v_ffn2_w_down, v_final_norm):
    args = dict(locals())
    wts = {n: args[n] for n in ALL_WEIGHTS}
    moms = {n: args["m_" + n] for n in ALL_WEIGHTS}
    vars_ = {n: args["v_" + n] for n in ALL_WEIGHTS}

    me = 2 * lax.axis_index("x") + lax.axis_index("y")
    c_idx = lax.axis_index("c").astype(jnp.int32).reshape(1)
    me_idx = me.astype(jnp.int32).reshape(1)
    shard_of = {n: _to_rows(n, wts[n]).astype(BF16) for n in BIG}
    shard_of["packed"] = _pack(wts, {n: wts[n] for n in SMALL}).astype(BF16)
    group = {"ffn1": BIG[0:3], "ffn2": BIG[3:6]}

    def slot_bufs(names):
        return [lax.dynamic_update_slice(jnp.zeros((N_CHIPS,) + shard_of[n].shape, BF16), shard_of[n][None],
                                         (me, 0, 0)) for n in names]

    def ffn_weights(tag, gathered):
        return {f"{tag}_wgt": gathered[0].reshape(D_FF, D_MODEL), f"{tag}_wut": gathered[1].reshape(D_FF, D_MODEL),
                f"{tag}_wd": gathered[2].reshape(D_FF, D_MODEL)}

    def w_in_weights(gathered):
        w_in_full = jnp.concatenate([gathered[0][q] for q in range(N_CHIPS)], axis=1)
        return {"w_in_h": w_in_full[:, :N_HGRN_COLS],
                "w_in_r": jnp.pad(w_in_full[:, N_HGRN_COLS:], ((0, 0), (0, N_RWKV_PAD - N_RWKV_COLS)))}

    def mixer_weights(gathered):
        w_out_full = gathered[0].reshape(D_MODEL, D_MODEL)
        packs = gathered[1].reshape(N_CHIPS, PACK_ROWS * LANES)
        full, off = {}, 0
        for n in PACKED:
            shp = _shard_shape(n)
            full[n] = jnp.concatenate([packs[q, off:off + _numel(shp)].reshape(shp) for q in range(N_CHIPS)], axis=1)
            off += _numel(shp)
        zrow = lambda nrow: jnp.zeros((nrow, W_B), BF16)
        return {"w_out_a": w_out_full[:W_A], "w_out_b": w_out_full[W_A:],
                "w2_pad": jnp.concatenate([full["rwkv_w2"], zrow(LORA_PAD - 32)], axis=0),
                "a2_pad": jnp.concatenate([zrow(32), full["rwkv_a2"], zrow(LORA_PAD - 64)], axis=0),
                "g2_pad": jnp.concatenate([zrow(64), full["rwkv_g2"], zrow(LORA_PAD - 160)], axis=0)}

    plan = _Plan()
    w = {}
    plan.carry("ffn1_rms", lambda g: _gather_comm(slot_bufs(group["ffn1"][:2])),
               lambda res, w_: w_.update({"ffn1_wgt": res[0].reshape(D_FF, D_MODEL),
                                          "ffn1_wut": res[1].reshape(D_FF, D_MODEL)}))

    def after_gate_up(res, w_):
        w_["ffn1_wd"] = res[0].reshape(D_FF, D_MODEL)
        w_.update(w_in_weights(res[1:]))

    plan.carry("ffn1_gate_up", lambda g: _gather_comm(slot_bufs(("ffn1_w_down", "w_in"))), after_gate_up)
    plan.carry("ffn1_down", lambda g: _gather_comm(slot_bufs(("w_out", "packed"))),
               lambda res, w_: w_.update(mixer_weights(res)))
    plan.carry("rwkv_fwd", lambda g: _gather_comm(slot_bufs(group["ffn2"])),
               lambda res, w_: w_.update(ffn_weights("ffn2", res)))
    w["ffn1_norm"], w["ffn2_norm"] = ffn1_norm, ffn2_norm
    w["mix_norm"] = mix_norm
    w["lb0"], w["lb1"] = hgrn_lb_logits[0:1], hgrn_lb_logits[1:2]
    w["hgrn_out_norm"] = hgrn_out_norm
    w["mu_pad"] = jnp.pad(rwkv_shift_mu, ((0, 0), (0, N_RWKV_PAD - N_RWKV_COLS)))
    for n in ("rwkv_w0", "rwkv_a0", "rwkv_k_k", "rwkv_k_a", "rwkv_gn_w", "rwkv_gn_b"):
        w[n] = wts[n]
    w["rwkv_r_k"] = rwkv_r_k.reshape(1, W_B)
    w["final_norm"] = final_norm.reshape(1, D_MODEL)

    def reduce_rows(names, gs):
        r1 = _run_comm(_sibling_exchange_comm(gs), "grad_sibling_exchange")
        s4 = [_add_halves(gt, rt, c_idx, f"grad_add_halves_{n}") for gt, rt, n in zip(gs, r1, names)]
        r2 = _run_comm(_chip_exchange_comm(s4), "grad_chip_exchange")
        return [_sum_chips(rt, st, me_idx, f"grad_sum_chips_{n}") for rt, st, n in zip(r2, s4, names)]

    early = {}

    def reduce_early(names, grads_of, sibling_host, chips_host):
        def sibling_comm(g):
            early[names, "gs"] = grads_of(g)
            return _sibling_exchange_comm(early[names, "gs"])

        def after_sibling(res, w_):
            early[names, "s4"] = [_add_halves(gt, rt, c_idx, f"grad_add_halves_{n}")
                                  for gt, rt, n in zip(early[names, "gs"], res, names)]

        def after_chips(res, w_):
            early.update(zip(names, [_sum_chips(rt, st, me_idx, f"grad_sum_chips_{n}")
                                     for rt, st, n in zip(res, early[names, "s4"], names)]))

        plan.carry(sibling_host, sibling_comm, after_sibling)
        plan.carry(chips_host, lambda g: _chip_exchange_comm(early[names, "s4"]), after_chips)

    def proj_grads(g):
        g_w_in = jnp.concatenate([g["w_in_h"], g["w_in_r"][:, :N_RWKV_COLS]], axis=1)
        return [jnp.concatenate([g["w_out_a"], g["w_out_b"]], axis=0).reshape(N_CHIPS, -1, D_MODEL),
                jnp.stack([_quarter(g_w_in, "w_in", q) for q in range(N_CHIPS)])]

    rows_of = lambda keys: (lambda g: [g[k].reshape(N_CHIPS, -1, D_MODEL) for k in keys])
    reduce_early(group["ffn2"], rows_of(("ffn2_wgt", "ffn2_wut", "ffn2_wd")), "hgrn_bwd", "rwkv_bwd")
    reduce_early(("w_out", "w_in"), proj_grads, "mix_drms", "ffn1_dact")
    reduce_early(("ffn1_w_down",), rows_of(("ffn1_wd",)), "ffn1_dwg", "ffn1_dwu")
    reduce_early(("ffn1_w_gate",), rows_of(("ffn1_wgt",)), "ffn1_dwu", "ffn1_dh_g")
    reduce_early(("ffn1_w_up",), rows_of(("ffn1_wut",)), "ffn1_dh_g", "ffn1_dh_u")
    loss_slab, grad_x, g = _local_step(x[0], loss_target[0], w, plan)
    loss = lax.psum(loss_slab[0, 0], ("x", "y", "c"))

    gfull = {
        "rwkv_w2": g["w2_pad"][0:32], "rwkv_a2": g["a2_pad"][32:64], "rwkv_g2": g["g2_pad"][64:160],
    }
    gsmall = {
        "ffn1_norm": g["ffn1_norm"], "mix_norm": g["mix_norm"],
        "hgrn_lb_logits": jnp.concatenate([g["lb0"], g["lb1"]], axis=0), "hgrn_out_norm": g["hgrn_out_norm"],
        "rwkv_shift_mu": g["mu_pad"][:, :N_RWKV_COLS], "rwkv_w0": g["rwkv_w0"], "rwkv_a0": g["rwkv_a0"],
        "rwkv_k_k": g["rwkv_k_k"], "rwkv_k_a": g["rwkv_k_a"], "rwkv_r_k": g["rwkv_r_k"],
        "rwkv_gn_w": g["rwkv_gn_w"], "rwkv_gn_b": g["rwkv_gn_b"], "ffn2_norm": g["ffn2_norm"],
        "final_norm": g["final_norm"],
    }
    packed = jnp.stack([_pack({n: _quarter(gfull[n], n, q) for n in PACKED}, gsmall) for q in range(N_CHIPS)])
    early["packed"], = reduce_rows(["packed"], [packed])
    names = list(BIG) + ["packed"]
    own = [early[n] for n in names]
    other = _run_comm(_sibling_swap_comm(own), "grad_sibling_swap")

    def rows_list(d):
        return [_to_rows(n, d[n]) for n in BIG] + [_pack(d, {n: d[n] for n in SMALL})]

    outs = [_adamw(wt, go, gx, mt, vt, c_idx, f"adamw_{n}")
            for wt, go, gx, mt, vt, n in zip(rows_list(wts), own, other, rows_list(moms), rows_list(vars_), names)]
    results = []
    for k in range(4):
        per = [outs[i][k] for i in range(len(names))]
        d = {n: _from_rows(n, z) for n, z in zip(BIG, per[:-1])}
        d.update(_unpack(per[-1]))
        results.append(d)
    return (loss, grad_x[None], *[r[n] for r in results for n in ALL_WEIGHTS])
```

```python
import collections
import functools

import jax
import jax.numpy as jnp
from jax import lax
from jax.experimental import pallas as pl
from jax.experimental.pallas import tpu as pltpu

F32 = jnp.float32
BF16 = jnp.bfloat16
SDS = jax.ShapeDtypeStruct
MESH = pl.DeviceIdType.MESH

D_MODEL = 1024
D_FF = 2816
W_A = 512
W_B = 512
HA_HEADS, HA_DIM = 4, 128
HB_HEADS, HB_DIM = 8, 64
HGRN_CHUNK = 64
HGRN_GROUP = 4
RWKV_CHUNK = 16
RWKV_GROUP = 8
N_HGRN_COLS = 4 * W_A
N_RWKV_COLS = 3 * W_B + 32 + 32 + 96
N_RWKV_PAD = 1792
LORA_PAD = 256
NORM_EPS = 1e-6
RWKV_GN_EPS = 64e-5
L2_EPS = 1e-12
ADAM_LR, ADAM_B1, ADAM_B2, ADAM_EPS, ADAM_WD, ADAM_STEP = 0.001, 0.9, 0.999, 1e-8, 0.01, 10

N_CHIPS = 4
VMEM_LIMIT_V7X = 56 * 1024 * 1024
LANES = 1024

SHARDED_SHAPES = {
    "ffn1_w_gate": ((D_MODEL, D_FF), 1), "ffn1_w_up": ((D_MODEL, D_FF), 1), "ffn1_w_down": ((D_FF, D_MODEL), 0),
    "w_in": ((D_MODEL, N_HGRN_COLS + N_RWKV_COLS), 1), "rwkv_w2": ((32, W_B), 1), "rwkv_a2": ((32, W_B), 1),
    "rwkv_g2": ((96, W_B), 1), "w_out": ((D_MODEL, D_MODEL), 0),
    "ffn2_w_gate": ((D_MODEL, D_FF), 1), "ffn2_w_up": ((D_MODEL, D_FF), 1), "ffn2_w_down": ((D_FF, D_MODEL), 0),
}
SMALL = ("ffn1_norm", "mix_norm", "hgrn_lb_logits", "hgrn_out_norm", "rwkv_shift_mu", "rwkv_w0", "rwkv_a0",
         "rwkv_k_k", "rwkv_k_a", "rwkv_r_k", "rwkv_gn_w", "rwkv_gn_b", "ffn2_norm", "final_norm")
ALL_WEIGHTS = ("ffn1_norm", "ffn1_w_gate", "ffn1_w_up", "ffn1_w_down", "mix_norm", "w_in", "hgrn_lb_logits",
               "hgrn_out_norm", "rwkv_shift_mu", "rwkv_w0", "rwkv_w2", "rwkv_a0", "rwkv_a2", "rwkv_g2", "rwkv_k_k",
               "rwkv_k_a", "rwkv_r_k", "rwkv_gn_w", "rwkv_gn_b", "w_out", "ffn2_norm", "ffn2_w_gate", "ffn2_w_up",
               "ffn2_w_down", "final_norm")


def _shard_shape(name):
    shape, ax = SHARDED_SHAPES[name]
    return tuple(s // N_CHIPS if i == ax else s for i, s in enumerate(shape))


def _numel(shape):
    n = 1
    for s in shape:
        n *= s
    return n


def _params(sem=None):
    return pltpu.CompilerParams(dimension_semantics=sem, vmem_limit_bytes=VMEM_LIMIT_V7X)


def _split2(x):
    hi = x.astype(BF16)
    return hi, (x.astype(F32) - hi.astype(F32)).astype(BF16)


def _dg(x, y, cx, cy, hi):
    dn = (((cx,), (cy,)), ((), ()))
    dot = lambda p, q: lax.dot_general(p, q, dn, preferred_element_type=F32)
    if hi == "x3":
        (xh, xl), (yh, yl) = _split2(x), _split2(y)
        return dot(xh, yh) + (dot(xh, yl) + dot(xl, yh))
    return dot(x.astype(BF16), y.astype(BF16))


def _make_mm(hi, cotangent_forms=None):
    @jax.custom_vjp
    def nn(x, y):
        return _dg(x, y, 1, 0, hi)

    @jax.custom_vjp
    def nt(x, y):
        return _dg(x, y, 1, 1, hi)

    @jax.custom_vjp
    def tn(x, y):
        return _dg(x, y, 0, 0, hi)

    bnn, bnt, btn = cotangent_forms or (nn, nt, tn)
    nn.defvjp(lambda x, y: (nn(x, y), (x, y)), lambda r, g: (bnt(g, r[1]), btn(r[0], g)))
    nt.defvjp(lambda x, y: (nt(x, y), (x, y)), lambda r, g: (bnn(g, r[1]), btn(g, r[0])))
    tn.defvjp(lambda x, y: (tn(x, y), (x, y)), lambda r, g: (bnt(r[1], g), bnn(r[0], g)))
    return nn, nt, tn


_nn, _nt, _tn = _make_mm(False)
_nn_x3, _nt_x3, _tn_x3 = _make_mm("x3", (_nn, _nt, _tn))


def _tri_apply(x, transpose):
    c = x.shape[0]
    tri = (lax.broadcasted_iota(jnp.int32, (c, c), 1) <= lax.broadcasted_iota(jnp.int32, (c, c), 0)).astype(BF16)
    dn = (((0 if transpose else 1,), (0,)), ((), ()))
    p1, p2 = _split2(x)
    dot = lambda p: lax.dot_general(tri, p, dn, preferred_element_type=F32)
    return dot(p1) + dot(p2)


@jax.custom_vjp
def _cumsum_rows(x):
    return _tri_apply(x, False)


_cumsum_rows.defvjp(lambda x: (_tri_apply(x, False), None), lambda _, g: (_tri_apply(g, True),))


def _sigmoid(x):
    return 1.0 / (1.0 + jnp.exp(-x))


def _silu(x):
    return x * _sigmoid(x)


def _softplus(z):
    return jnp.maximum(z, 0.0) + jnp.log(1.0 + jnp.exp(-jnp.abs(z)))


def _mm(a, b, *, ta=False, tb=False, tm, tn, tk, name, out_dtype=F32, res=None, scale=None, comm=None):
    m = a.shape[1] if ta else a.shape[0]
    kdim = a.shape[0] if ta else a.shape[1]
    n = b.shape[0] if tb else b.shape[1]
    assert (b.shape[1] if tb else b.shape[0]) == kdim
    tm, tn, tk = min(tm, m), min(tn, n), min(tk, kdim)
    assert m % tm == 0 and n % tn == 0 and kdim % tk == 0, (name, m, n, kdim)
    nk = kdim // tk
    a_spec = pl.BlockSpec((tk, tm), lambda i, j, k: (k, i)) if ta else pl.BlockSpec((tm, tk), lambda i, j, k: (i, k))
    b_spec = pl.BlockSpec((tn, tk), lambda i, j, k: (j, k)) if tb else pl.BlockSpec((tk, tn), lambda i, j, k: (k, j))
    o_spec = pl.BlockSpec((tm, tn), lambda i, j, k: (i, j))
    ca, cb = (0 if ta else 1), (1 if tb else 0)

    def body(*refs):
        if res is not None:
            a_ref, b_ref, r_ref, o_ref, acc_ref = refs
        else:
            a_ref, b_ref, o_ref, acc_ref = refs
        k = pl.program_id(2)

        @pl.when(k == 0)
        def _():
            acc_ref[...] = jnp.zeros_like(acc_ref)

        acc_ref[...] += _dg(a_ref[...], b_ref[...], ca, cb, False)

        @pl.when(k == nk - 1)
        def _():
            acc = acc_ref[...]
            if scale is not None:
                acc = acc * scale
            if res is not None:
                acc = r_ref[...] + acc
            o_ref[...] = acc.astype(out_dtype)

    in_specs = [a_spec, b_spec] + ([o_spec] if res is not None else [])
    args = (a, b) + ((res,) if res is not None else ())
    if comm is None:
        return pl.pallas_call(
            body, name=name, grid=(m // tm, n // tn, nk), in_specs=in_specs, out_specs=o_spec,
            out_shape=SDS((m, n), out_dtype), scratch_shapes=[pltpu.VMEM((tm, tn), F32)],
            compiler_params=_params(("parallel", "parallel", "arbitrary")))(*args)
    (out,), carried = _hosting_call(
        body, comm, name=name, grid=(m // tm, n // tn, nk), in_specs=in_specs, out_specs=[o_spec],
        out_shape=[SDS((m, n), out_dtype)], scratch_shapes=[pltpu.VMEM((tm, tn), F32)], args=args)
    return out, carried


def _row_spec(x, tm, tile_of=lambda i: i):
    if isinstance(x, tuple):
        arr, w, j = x
        return arr, pl.BlockSpec((tm, w), lambda i, j=j: (tile_of(i), j))
    return x, pl.BlockSpec((tm, x.shape[1]), lambda i: (tile_of(i), 0))


def _par_spec(p):
    if isinstance(p, tuple):
        arr, w, j = p
        return arr, pl.BlockSpec((arr.shape[0], w), lambda i, j=j: (0, j))
    return p, pl.BlockSpec(p.shape, lambda i: (0, 0))


def _store_groups(refs, groups, vals):
    for ref, idxs in zip(refs, groups):
        off = 0
        for ix in idxs:
            v = vals[ix]
            ref[:, off:off + v.shape[1]] = v.astype(ref.dtype)
            off += v.shape[1]


SUBLANES = 8


def _x_plan(xs, tm, t, tile_of=lambda i: i):
    arrays, specs, plan = [], [], []
    nb = tm // SUBLANES
    for x in xs:
        if isinstance(x, tuple) and isinstance(x[0], str):
            kind, arr, w, j = x
            if kind == "prev":
                halo = lambda i, j=j: (jnp.maximum(tile_of(i) * nb - 1, 0), j)
            else:
                halo = lambda i, j=j: (jnp.minimum((tile_of(i) + 1) * nb, t // SUBLANES - 1), j)
            arrays += [arr, arr]
            specs += [pl.BlockSpec((tm, w), lambda i, j=j: (tile_of(i), j)), pl.BlockSpec((SUBLANES, w), halo)]
            plan.append((kind, 2, w))
        else:
            arr, spec = _row_spec(x, tm, tile_of)
            arrays.append(arr)
            specs.append(spec)
            plan.append(("plain", 1, spec.block_shape[1]))
    return arrays, specs, plan


def _x_vals(refs, plan, tm, nt, tile_of=lambda i: i):
    vals, k = [], 0
    i = tile_of(pl.program_id(0))
    rows = lax.broadcasted_iota(jnp.int32, (tm, 1), 0)
    for kind, n, _ in plan:
        main = refs[k][...].astype(F32)
        if kind == "prev":
            edge = jnp.where(i == 0, 0.0, refs[k + 1][SUBLANES - 1:SUBLANES, :].astype(F32))
            main = jnp.where(rows == 0, edge, pltpu.roll(main, 1, 0))
        elif kind == "next":
            edge = jnp.where(i == nt - 1, 0.0, refs[k + 1][0:1, :].astype(F32))
            main = jnp.where(rows == tm - 1, edge, pltpu.roll(main, tm - 1, 0))
        vals.append(main)
        k += n
    return vals


def _tile_rows(xs, tm):
    arr = xs[0]
    if isinstance(arr, tuple):
        arr = arr[1] if isinstance(arr[0], str) else arr[0]
    return min(tm, arr.shape[0]), arr.shape[0]


def _rowwise(f, xs, params, out_groups, out_dtypes, *, tm, name, comm=None):
    tm, t = _tile_rows(xs, tm)
    nt = t // tm
    xa, xspecs, plan = _x_plan(xs, tm, t)
    pa, pspecs = (zip(*[_par_spec(p) for p in params]) if params else ((), ()))
    nxr, npar = len(xa), len(pa)
    x_sds = [SDS((tm, w), F32) for _, _, w in plan]
    p_sds = [SDS(s.block_shape, F32) for s in pspecs]
    outs_sds = jax.eval_shape(lambda *vals: f(*vals), *x_sds, *p_sds)
    widths = [sum(outs_sds[ix].shape[1] for ix in idxs) for idxs in out_groups]

    def body(*refs):
        vals = _x_vals(refs[:nxr], plan, tm, nt) + [r[...].astype(F32) for r in refs[nxr:nxr + npar]]
        outs = f(*vals)
        _store_groups(refs[nxr + npar:], out_groups, outs)

    res, carried = _hosting_call(
        body, comm, name=name, grid=(nt,), in_specs=list(xspecs) + list(pspecs),
        out_specs=[pl.BlockSpec((tm, w), lambda i: (i, 0)) for w in widths],
        out_shape=[SDS((t, w), dt) for w, dt in zip(widths, out_dtypes)], scratch_shapes=[], args=(*xa, *pa))
    return res if comm is None else (res, carried)


def _rowwise_bwd(f, xs, params, cots, *, x_grad, p_grad, dx_groups, dx_dtypes, tm, name, extra=None, comm=None,
                 fold_next=None):
    tm, t = _tile_rows(xs, tm)
    nt = t // tm
    tile_of = (lambda i: nt - 1 - i) if fold_next else (lambda i: i)
    xa, xspecs, plan = _x_plan(xs, tm, t, tile_of)
    pa, pspecs = (zip(*[_par_spec(p) for p in params]) if params else ((), ()))
    ca, cspecs = zip(*[_row_spec(c, tm, tile_of) for c in cots])
    extra = extra or {}
    ekeys = sorted(extra)
    ea, especs = (zip(*[_row_spec(extra[k], tm, tile_of) for k in ekeys]) if ekeys else ((), ()))
    nx, nxr, npar, nc, ne = len(plan), len(xa), len(pa), len(ca), len(ea)
    gx = [i for i in range(nx) if x_grad[i]]
    gp = [i for i in range(npar) if p_grad[i]]
    all_widths = [sum(plan[gx[ix]][2] for ix in idxs) for idxs in dx_groups]
    emitted = [k for k in range(len(dx_groups)) if not (fold_next and k == fold_next[1])]
    widths = [all_widths[k] for k in emitted]
    ng = len(emitted)

    def body(*refs):
        ins = refs[:nxr + npar + nc + ne]
        outs = refs[nxr + npar + nc + ne:]
        vals = _x_vals(ins[:nxr], plan, tm, nt, tile_of) + [r[...].astype(F32) for r in ins[nxr:nxr + npar]]
        cvals = tuple(r[...].astype(F32) for r in ins[nxr + npar:nxr + npar + nc])
        evals = [r[...].astype(F32) for r in ins[nxr + npar + nc:]]
        diff_idx = gx + [nx + i for i in gp]

        def g(*dargs):
            full = list(vals)
            for ix, v in zip(diff_idx, dargs):
                full[ix] = v
            return tuple(f(*full))

        _, vjp = jax.vjp(g, *[vals[ix] for ix in diff_idx])
        grads = vjp(cvals)
        dxs = list(grads[:len(gx)])
        for k, ev in zip(ekeys, evals):
            dxs[k] = dxs[k] + ev
        _store_groups(outs[:ng], [dx_groups[k] for k in emitted], dxs)
        i = pl.program_id(0)
        if fold_next:
            main_ref, carry_ref = outs[emitted.index(fold_next[0])], refs[-1]
            rows = lax.broadcasted_iota(jnp.int32, (tm, 1), 0)
            off = 0
            for ix in dx_groups[fold_next[1]]:
                piece = dxs[ix]
                cols = slice(off, off + piece.shape[1])
                edge = jnp.where(i == 0, 0.0, carry_ref[0:1, cols])
                main_ref[:, cols] += jnp.where(rows == tm - 1, edge, pltpu.roll(piece, tm - 1, 0))
                carry_ref[:, cols] = piece[:SUBLANES]
                off += piece.shape[1]
        for ref, gval in zip(outs[ng:ng + len(gp)], grads[len(gx):]):
            @pl.when(i == 0)
            def _(ref=ref):
                ref[...] = jnp.zeros_like(ref)
            ref[...] += gval

    dp_specs = [pl.BlockSpec(pspecs[i].block_shape, lambda i: (0, 0)) for i in gp]
    dp_shapes = [SDS(pspecs[i].block_shape, F32) for i in gp]
    scratch = [pltpu.VMEM((SUBLANES, all_widths[fold_next[1]]), F32)] if fold_next else []
    res, carried = _hosting_call(
        body, comm, name=name, grid=(nt,), in_specs=list(xspecs) + list(pspecs) + list(cspecs) + list(especs),
        out_specs=[pl.BlockSpec((tm, w), lambda i: (tile_of(i), 0)) for w in widths] + dp_specs,
        out_shape=[SDS((t, w), dt) for w, dt in zip(widths, dx_dtypes)] + dp_shapes, scratch_shapes=scratch,
        args=(*xa, *pa, *ca, *ea))
    return res if comm is None else (res, carried)


def _rms_f(x, g):
    return (x * lax.rsqrt(jnp.mean(x * x, axis=-1, keepdims=True) + NORM_EPS) * g,)


def _group_sum_impl(x, ones_bd):
    p1, p2 = _split2(x)
    dot = lambda p: lax.dot_general(p, ones_bd.astype(BF16), (((1,), (0,)), ((), ())), preferred_element_type=F32)
    return dot(p1) + dot(p2)


@jax.custom_vjp
def _group_sum(x, ones_bd):
    return _group_sum_impl(x, ones_bd)


_group_sum.defvjp(lambda x, o: (_group_sum_impl(x, o), o),
                  lambda o, g: (_group_sum_impl(g, o), jnp.zeros_like(o)))


def _rwkv_prep_f(r, k, v, lo, rp, kp, vp, lop, mu_r, mu_k, mu_v, mu_lo, w0, w2p, a0, a2p, g2p, k_k, k_a, ones_bd):
    r = r + mu_r * (rp - r)
    k = k + mu_k * (kp - k)
    v = v + mu_v * (vp - v)
    lo = lo + mu_lo * (lop - lo)
    w_log = -_softplus(-(w0 + _nn(jnp.tanh(lo), w2p))) - 0.5
    lw = -jnp.exp(w_log)
    a_g = _sigmoid(a0 + _nn(lo, a2p))
    g = _nn(_sigmoid(lo), g2p)
    kk = k * k_k
    kk = kk / jnp.maximum(jnp.sqrt(_group_sum(kk * kk, ones_bd)), L2_EPS)
    k2 = k * (1.0 + (a_g - 1.0) * k_a)
    return r, lw, k2, v, -kk, kk * a_g, g


def _rwkv_post_f(y, r, k2, v, g, r_k, gn_w, gn_b, ones_bd):
    inv_n = 1.0 / HB_DIM
    mean = _group_sum(y, ones_bd) * inv_n
    yc = y - mean
    var = _group_sum(yc * yc, ones_bd) * inv_n
    yn = yc * lax.rsqrt(var + RWKV_GN_EPS) * gn_w + gn_b
    bonus = _group_sum(r * k2 * r_k, ones_bd) * v
    return ((yn + bonus) * g,)


def _tri(c, strict=False):
    ii = lax.broadcasted_iota(jnp.int32, (c, c), 0)
    jj = lax.broadcasted_iota(jnp.int32, (c, c), 1)
    return (jj < ii) if strict else (jj <= ii)


def _hgrn_step(st0, q_a, f_a, i_a, g_a, l0, l1, onorm):
    nh, nj = len(q_a), len(q_a[0])
    c = q_a[0][0].shape[0]
    combos = [(j, h) for j in range(nj) for h in range(nh)]
    every = lambda fn: {q: fn(q) for q in combos}
    at_ = lambda d: (lambda q: d[q[1]][q[0]])
    qa_, fa_, ia_, ga_ = (at_(z) for z in (q_a, f_a, i_a, g_a))
    incl = _tri(c)
    rows = lax.broadcasted_iota(jnp.int32, (c, 1), 0)
    lb = []
    for h in range(nh):
        mx = jnp.maximum(l0[h], l1[h])
        e0, e1 = jnp.exp(l0[h] - mx), jnp.exp(l1[h] - mx)
        lb.append(e0 / (e0 + e1))
    forget = every(lambda q: lb[q[1]] + (1.0 - lb[q[1]]) * _sigmoid(fa_(q)))
    qs = every(lambda q: _silu(qa_(q)))
    kk = every(lambda q: 1.0 - forget[q])
    lf = every(lambda q: jnp.log(forget[q]))
    bcum = every(lambda q: _cumsum_rows(lf[q]))
    bref = every(lambda q: jnp.sum(jnp.where(rows <= c // 2, lf[q], 0.0), axis=0, keepdims=True))
    blast = every(lambda q: jnp.sum(lf[q], axis=0, keepdims=True))
    scores = every(lambda q: jnp.where(incl, _nt(qs[q] * jnp.exp(bcum[q] - bref[q]),
                                                 kk[q] * jnp.exp(bref[q] - bcum[q])), 0.0))
    intra = every(lambda q: _nn(scores[q], ia_(q)))
    qb = every(lambda q: qs[q] * jnp.exp(bcum[q]))
    upd = every(lambda q: _tn(ia_(q), kk[q] * jnp.exp(blast[q] - bcum[q])))
    dec = every(lambda q: jnp.exp(blast[q]))
    st = list(st0)
    o = {}
    for j in range(nj):
        for h in range(nh):
            o[(j, h)] = intra[(j, h)] + _nt(qb[(j, h)], st[h])
        st = [st[h] * dec[(j, h)] + upd[(j, h)] for h in range(nh)]
    out = every(lambda q: o[q] * lax.rsqrt(jnp.mean(o[q] * o[q], axis=-1, keepdims=True) + NORM_EPS)
                * onorm[q[1]] * _silu(ga_(q)))
    return [[out[(j, h)] for j in range(nj)] for h in range(nh)], st


def _hgrn_blocks(ref, nj, c):
    return [[ref[j * c:(j + 1) * c, h * HA_DIM:(h + 1) * HA_DIM] for j in range(nj)] for h in range(HA_HEADS)]


def _hgrn_cols(ref):
    return [ref[:, h * HA_DIM:(h + 1) * HA_DIM] for h in range(HA_HEADS)]


def _hgrn_fwd(p_h, l0, l1, onorm):
    t = p_h.shape[0]
    cc, nj = HGRN_CHUNK, HGRN_GROUP
    c = cc * nj
    n = t // c

    def body(q_ref, f_ref, i_ref, g_ref, l0_ref, l1_ref, on_ref, o_ref, hs_ref, st_ref):
        @pl.when(pl.program_id(0) == 0)
        def _():
            st_ref[...] = jnp.zeros_like(st_ref)

        hs_ref[0] = st_ref[...]
        o, st1 = _hgrn_step([st_ref[h] for h in range(HA_HEADS)],
                            *[_hgrn_blocks(ref, nj, cc) for ref in (q_ref, f_ref, i_ref, g_ref)],
                            _hgrn_cols(l0_ref), _hgrn_cols(l1_ref), _hgrn_cols(on_ref))
        for h in range(HA_HEADS):
            for j in range(nj):
                o_ref[j * cc:(j + 1) * cc, h * HA_DIM:(h + 1) * HA_DIM] = o[h][j]
            st_ref[h] = st1[h]

    col = lambda j: pl.BlockSpec((c, W_A), lambda i, j=j: (i, j))
    par = pl.BlockSpec((1, W_A), lambda i: (0, 0))
    return pl.pallas_call(
        body, name="hgrn_fwd", grid=(n,), in_specs=[col(0), col(1), col(2), col(3), par, par, par],
        out_specs=[pl.BlockSpec((c, W_A), lambda i: (i, 0)),
                   pl.BlockSpec((1, HA_HEADS, HA_DIM, HA_DIM), lambda i: (i, 0, 0, 0))],
        out_shape=[SDS((t, W_A), F32), SDS((n, HA_HEADS, HA_DIM, HA_DIM), F32)],
        scratch_shapes=[pltpu.VMEM((HA_HEADS, HA_DIM, HA_DIM), F32)],
        compiler_params=_params(("arbitrary",)))(p_h, p_h, p_h, p_h, l0, l1, onorm)


def _hgrn_bwd(p_h, l0, l1, onorm, hs, do, do_col, comm=None):
    t = p_h.shape[0]
    cc, nj = HGRN_CHUNK, HGRN_GROUP
    c = cc * nj
    n = t // c

    def body(q_ref, f_ref, i_ref, g_ref, l0_ref, l1_ref, on_ref, hs_ref, do_ref,
             dp_ref, dl0_ref, dl1_ref, don_ref, dst_ref):
        @pl.when(pl.program_id(0) == 0)
        def _():
            dst_ref[...] = jnp.zeros_like(dst_ref)
            dl0_ref[...] = jnp.zeros_like(dl0_ref)
            dl1_ref[...] = jnp.zeros_like(dl1_ref)
            don_ref[...] = jnp.zeros_like(don_ref)

        args = ([hs_ref[0, h] for h in range(HA_HEADS)],
                *[_hgrn_blocks(ref, nj, cc) for ref in (q_ref, f_ref, i_ref, g_ref)],
                _hgrn_cols(l0_ref), _hgrn_cols(l1_ref), _hgrn_cols(on_ref))
        _, vjp = jax.vjp(_hgrn_step, *args)
        dst0, dq, df, di, dg, dl0, dl1, don = vjp((_hgrn_blocks(do_ref, nj, cc),
                                                   [dst_ref[h] for h in range(HA_HEADS)]))
        for h in range(HA_HEADS):
            sl = slice(h * HA_DIM, (h + 1) * HA_DIM)
            for k, dv in enumerate((dq, df, di, dg)):
                for j in range(nj):
                    dp_ref[j * cc:(j + 1) * cc, k * W_A + h * HA_DIM:k * W_A + (h + 1) * HA_DIM] = dv[h][j]
            dl0_ref[:, sl] += dl0[h]
            dl1_ref[:, sl] += dl1[h]
            don_ref[:, sl] += don[h]
            dst_ref[h] = dst0[h]

    col = lambda j: pl.BlockSpec((c, W_A), lambda i, j=j: (n - 1 - i, j))
    par = pl.BlockSpec((1, W_A), lambda i: (0, 0))
    return _hosting_call(
        body, comm, name="hgrn_bwd", grid=(n,),
        in_specs=[col(0), col(1), col(2), col(3), par, par, par,
                  pl.BlockSpec((1, HA_HEADS, HA_DIM, HA_DIM), lambda i: (n - 1 - i, 0, 0, 0)),
                  pl.BlockSpec((c, W_A), lambda i: (n - 1 - i, do_col))],
        out_specs=[pl.BlockSpec((c, N_HGRN_COLS), lambda i: (n - 1 - i, 0)), par, par, par],
        out_shape=[SDS((t, N_HGRN_COLS), F32), SDS((1, W_A), F32), SDS((1, W_A), F32), SDS((1, W_A), F32)],
        scratch_shapes=[pltpu.VMEM((HA_HEADS, HA_DIM, HA_DIM), F32)],
        args=(p_h, p_h, p_h, p_h, l0, l1, onorm, hs, do))


HB_PAIRS = HB_HEADS // 2
PAIR_W = 2 * HB_DIM


def _head_lane_masks():
    lane = lax.broadcasted_iota(jnp.int32, (1, PAIR_W), 1)
    return (lane < HB_DIM).astype(F32), (lane >= HB_DIM).astype(F32)


@jax.custom_vjp
def _stack_heads(x):
    m0, m1 = _head_lane_masks()
    return jnp.concatenate([x * m0, x * m1], axis=0)


def _stack_heads_bwd(_, g):
    m0, m1 = _head_lane_masks()
    c = g.shape[0] // 2
    return (g[:c] * m0 + g[c:] * m1,)


_stack_heads.defvjp(lambda x: (_stack_heads(x), None), _stack_heads_bwd)


@jax.custom_vjp
def _unstack_heads(ys):
    c = ys.shape[0] // 2
    return ys[:c] + ys[c:]


_unstack_heads.defvjp(lambda ys: (_unstack_heads(ys), None), lambda _, g: (_stack_heads(g),))


def _same_head_block(c):
    ii = lax.broadcasted_iota(jnp.int32, (2 * c, 2 * c), 0)
    jj = lax.broadcasted_iota(jnp.int32, (2 * c, 2 * c), 1)
    same = (ii < c) == (jj < c)
    return same & (jj <= ii), same & (jj < ii), (ii == jj).astype(F32)


@jax.custom_vjp
def _rows_join(top, bottom):
    return jnp.concatenate([top, bottom], axis=0)


def _rows_join_bwd(n_top, g):
    return g[:n_top], g[n_top:]


_rows_join.defvjp(lambda top, bottom: (_rows_join(top, bottom), top.shape[0]), _rows_join_bwd)


def _rows_split_impl(x, n_top):
    return x[:n_top], x[n_top:]


_rows_split = jax.custom_vjp(_rows_split_impl, nondiff_argnums=(1,))
_rows_split.defvjp(lambda x, n_top: (_rows_split_impl(x, n_top), None),
                   lambda n_top, _, g: (jnp.concatenate([g[0], g[1]], axis=0),))


def _rwkv_step(s0, r, lw, k, v, a, b):
    npair, nj = len(r), len(r[0])
    c = r[0][0].shape[0]
    combos = [(j, p) for j in range(nj) for p in range(npair)]
    every = lambda fn: {q: fn(q) for q in combos}
    at_ = lambda d: (lambda q: d[q[1]][q[0]])
    r_, lw_, k_, v_, a_, b_ = (at_(z) for z in (r, lw, k, v, a, b))
    incl, strict, eye = _same_head_block(c)

    gam = every(lambda q: _cumsum_rows(lw_(q)))
    gtot = every(lambda q: jnp.sum(lw_(q), axis=0, keepdims=True))
    eneg = every(lambda q: jnp.exp(-gam[q]))
    edec = every(lambda q: jnp.exp(gtot[q] - gam[q]))
    at = every(lambda q: _stack_heads(a_(q) * jnp.exp(gam[q] - lw_(q))))
    rt = every(lambda q: _stack_heads(r_(q) * jnp.exp(gam[q])))
    bt = every(lambda q: _stack_heads(b_(q) * eneg[q]))
    kt = every(lambda q: _stack_heads(k_(q) * eneg[q]))
    bdec = every(lambda q: _stack_heads(b_(q) * edec[q]))
    kdec = every(lambda q: _stack_heads(k_(q) * edec[q]))
    vs = every(lambda q: _stack_heads(v_(q)))
    a_ab = every(lambda q: jnp.where(strict, _nt(at[q], bt[q]), 0.0))
    a_ak = every(lambda q: jnp.where(strict, _nt(at[q], kt[q]), 0.0))
    a_rb = every(lambda q: jnp.where(incl, _nt(rt[q], bt[q]), 0.0))
    a_rk = every(lambda q: jnp.where(incl, _nt(rt[q], kt[q]), 0.0))
    tinv = every(lambda q: eye + a_ab[q])
    pw = a_ab
    span = 2
    while span < c:
        pw = every(lambda q, pw=pw: _nn_x3(pw[q], pw[q]))
        tinv = every(lambda q, pw=pw, tinv=tinv: tinv[q] + _nn_x3(pw[q], tinv[q]))
        span *= 2
    akv = every(lambda q: _nn(a_ak[q], vs[q]))
    w1 = every(lambda q: _nn(tinv[q], at[q]))
    u0 = every(lambda q: _nn(tinv[q], akv[q]))
    wr = every(lambda q: _rows_join(w1[q], rt[q]))
    bk = every(lambda q: _rows_join(bdec[q], kdec[q]))
    yv = every(lambda q: _nn(a_rk[q], vs[q]))
    gdec = every(lambda q: jnp.exp(gtot[q]))

    s = list(s0)
    y = [[None] * nj for _ in range(npair)]
    for j in range(nj):
        both = {p: _rows_split(_nt(wr[(j, p)], s[p]), 2 * c) for p in range(npair)}
        u = {p: both[p][0] + u0[(j, p)] for p in range(npair)}
        for p in range(npair):
            y[p][j] = _unstack_heads(both[p][1] + _nn(a_rb[(j, p)], u[p]) + yv[(j, p)])
        s = [s[p] * gdec[(j, p)] + _tn(_rows_join(u[p], vs[(j, p)]), bk[(j, p)]) for p in range(npair)]
    return y, s


def _rwkv_blocks(ref, nj, c):
    return [[ref[j * c:(j + 1) * c, p * PAIR_W:(p + 1) * PAIR_W] for j in range(nj)] for p in range(HB_PAIRS)]


def _rwkv_fwd(seqs, gate, post_params, comm=None):
    t = seqs[0].shape[0]
    c, nj = RWKV_CHUNK, RWKV_GROUP
    n = t // (c * nj)

    def body(r_ref, lw_ref, k_ref, v_ref, a_ref, b_ref, g_ref, rk_ref, gw_ref, gb_ref, ones_ref,
             y_ref, ob_ref, hs_ref, st_ref):
        @pl.when(pl.program_id(0) == 0)
        def _():
            st_ref[...] = jnp.zeros_like(st_ref)

        hs_ref[0] = st_ref[...]
        s0 = [st_ref[p] for p in range(HB_PAIRS)]
        y, s1 = _rwkv_step(s0, *[_rwkv_blocks(ref, nj, c) for ref in (r_ref, lw_ref, k_ref, v_ref, a_ref, b_ref)])
        for p in range(HB_PAIRS):
            for j in range(nj):
                y_ref[j * c:(j + 1) * c, p * PAIR_W:(p + 1) * PAIR_W] = y[p][j]
            st_ref[p] = s1[p]
        ob_ref[...], = _rwkv_post_f(y_ref[...], r_ref[...], k_ref[...], v_ref[...], g_ref[...], rk_ref[...],
                                    gw_ref[...], gb_ref[...], ones_ref[...])

    seq = pl.BlockSpec((c * nj, W_B), lambda i: (i, 0))
    whole = lambda z: pl.BlockSpec(z.shape, lambda i: (0, 0))
    return _hosting_call(
        body, comm, name="rwkv_fwd", grid=(n,), in_specs=[seq] * 7 + [whole(z) for z in post_params],
        out_specs=[seq, seq, pl.BlockSpec((1, HB_PAIRS, PAIR_W, PAIR_W), lambda i: (i, 0, 0, 0))],
        out_shape=[SDS((t, W_B), F32), SDS((t, W_B), F32), SDS((n, HB_PAIRS, PAIR_W, PAIR_W), F32)],
        scratch_shapes=[pltpu.VMEM((HB_PAIRS, PAIR_W, PAIR_W), F32)], args=(*seqs, gate, *post_params))


def _rwkv_bwd(seqs, gate, post_params, y, hs, dob, comm=None):
    t = seqs[0].shape[0]
    c, nj = RWKV_CHUNK, RWKV_GROUP
    n = t // (c * nj)

    def body(r_ref, lw_ref, k_ref, v_ref, a_ref, b_ref, g_ref, rk_ref, gw_ref, gb_ref, ones_ref, y_ref, hs_ref,
             dob_ref, dr_ref, dlw_ref, dk_ref, dv_ref, da_ref, db_ref, dg_ref, drk_ref, dgw_ref, dgb_ref,
             dst_ref, dy_ref):
        @pl.when(pl.program_id(0) == 0)
        def _():
            dst_ref[...] = jnp.zeros_like(dst_ref)
            drk_ref[...] = jnp.zeros_like(drk_ref)
            dgw_ref[...] = jnp.zeros_like(dgw_ref)
            dgb_ref[...] = jnp.zeros_like(dgb_ref)

        ones = ones_ref[...]
        post = lambda yv, rv, kv, vv, gv, rk, gw, gb: _rwkv_post_f(yv, rv, kv, vv, gv, rk, gw, gb, ones)[0]
        _, post_vjp = jax.vjp(post, y_ref[...], r_ref[...], k_ref[...], v_ref[...], g_ref[...], rk_ref[...],
                              gw_ref[...], gb_ref[...])
        dy_ref[...], dr1, dk1, dv1, dg_ref[...], drk, dgw, dgb = post_vjp(dob_ref[...])
        drk_ref[...] += drk
        dgw_ref[...] += dgw
        dgb_ref[...] += dgb

        s0 = [hs_ref[0, p] for p in range(HB_PAIRS)]
        seq_vals = [_rwkv_blocks(ref, nj, c) for ref in (r_ref, lw_ref, k_ref, v_ref, a_ref, b_ref)]
        _, vjp = jax.vjp(_rwkv_step, s0, *seq_vals)
        grads = vjp((_rwkv_blocks(dy_ref, nj, c), [dst_ref[p] for p in range(HB_PAIRS)]))
        for ref, gr in zip((dr_ref, dlw_ref, dk_ref, dv_ref, da_ref, db_ref), grads[1:]):
            for p in range(HB_PAIRS):
                for j in range(nj):
                    ref[j * c:(j + 1) * c, p * PAIR_W:(p + 1) * PAIR_W] = gr[p][j]
        dr_ref[...] += dr1
        dk_ref[...] += dk1
        dv_ref[...] += dv1
        m0, m1 = _head_lane_masks()
        rows0 = (lax.broadcasted_iota(jnp.int32, (PAIR_W, 1), 0) < HB_DIM).astype(F32)
        blocks = rows0 * m0 + (1.0 - rows0) * m1
        for p in range(HB_PAIRS):
            dst_ref[p] = grads[0][p] * blocks

    seq = pl.BlockSpec((c * nj, W_B), lambda i: (n - 1 - i, 0))
    whole = lambda z: pl.BlockSpec(z.shape, lambda i: (0, 0))
    par = pl.BlockSpec((1, W_B), lambda i: (0, 0))
    return _hosting_call(
        body, comm, name="rwkv_bwd", grid=(n,),
        in_specs=[seq] * 7 + [whole(z) for z in post_params]
        + [seq, pl.BlockSpec((1, HB_PAIRS, PAIR_W, PAIR_W), lambda i: (n - 1 - i, 0, 0, 0)), seq],
        out_specs=[seq] * 7 + [par] * 3, out_shape=[SDS((t, W_B), F32)] * 7 + [SDS((1, W_B), F32)] * 3,
        scratch_shapes=[pltpu.VMEM((HB_PAIRS, PAIR_W, PAIR_W), F32), pltpu.VMEM((c * nj, W_B), F32)],
        args=(*seqs, gate, *post_params, y, hs, dob))


def _final_loss(x3, fnorm, target, *, tm):
    t, d = x3.shape

    def body(x_ref, g_ref, t_ref, dx_ref, dg_ref, loss_ref):
        @pl.when(pl.program_id(0) == 0)
        def _():
            dg_ref[...] = jnp.zeros_like(dg_ref)
            loss_ref[...] = jnp.zeros_like(loss_ref)

        x, g = x_ref[...], g_ref[...]
        rinv = lax.rsqrt(jnp.mean(x * x, axis=-1, keepdims=True) + NORM_EPS)
        xh = x * rinv
        diff = xh * g - t_ref[...]
        loss_ref[...] += 0.5 * jnp.sum(jnp.mean(diff * diff, axis=-1, keepdims=True))
        dy = diff * (1.0 / d)
        dg_ref[...] += jnp.sum(dy * xh, axis=0, keepdims=True)
        dxh = dy * g
        dx_ref[...] = rinv * (dxh - xh * jnp.mean(dxh * xh, axis=-1, keepdims=True))

    row = pl.BlockSpec((tm, d), lambda i: (i, 0))
    return pl.pallas_call(
        body, name="final_loss", grid=(t // tm,), in_specs=[row, pl.BlockSpec((1, d), lambda i: (0, 0)), row],
        out_specs=[row, pl.BlockSpec((1, d), lambda i: (0, 0)), pl.BlockSpec((8, 128), lambda i: (0, 0))],
        out_shape=[SDS((t, d), F32), SDS((1, d), F32), SDS((8, 128), F32)],
        compiler_params=_params(("arbitrary",)))(x3, fnorm, target)


def _gate_up_act(h, wgt, wut, *, tm, tn, name, comm=None):
    t, d = h.shape
    tm = min(tm, t)

    def body(h_ref, g_ref, u_ref, a_out, u_out, act_out):
        hv = h_ref[...]
        a = _dg(hv, g_ref[...], 1, 1, False)
        u = _dg(hv, u_ref[...], 1, 1, False)
        a_out[...] = a.astype(a_out.dtype)
        u_out[...] = u.astype(u_out.dtype)
        act_out[...] = (_silu(a) * u).astype(act_out.dtype)

    wspec = pl.BlockSpec((tn, d), lambda i, j: (j, 0))
    ospec = pl.BlockSpec((tm, tn), lambda i, j: (i, j))
    return _hosting_call(
        body, comm, name=name, grid=(t // tm, D_FF // tn),
        in_specs=[pl.BlockSpec((tm, d), lambda i, j: (i, 0)), wspec, wspec], out_specs=[ospec, ospec, ospec],
        out_shape=[SDS((t, D_FF), BF16), SDS((t, D_FF), BF16), SDS((t, D_FF), BF16)], scratch_shapes=[],
        args=(h, wgt, wut))


def _dact_swiglu(dout, wd, a, u, *, tm, tn, name, comm=None):
    t, d = dout.shape
    tm = min(tm, t)

    def body(d_ref, w_ref, a_ref, u_ref, da_out, du_out):
        dact = 0.5 * _dg(d_ref[...], w_ref[...], 1, 1, False)
        av, uv = a_ref[...].astype(F32), u_ref[...].astype(F32)
        s = _sigmoid(av)
        da_out[...] = (dact * uv * (s * (1.0 + av * (1.0 - s)))).astype(da_out.dtype)
        du_out[...] = (dact * (av * s)).astype(du_out.dtype)

    tile = pl.BlockSpec((tm, tn), lambda i, j: (i, j))
    return _hosting_call(
        body, comm, name=name, grid=(t // tm, D_FF // tn),
        in_specs=[pl.BlockSpec((tm, d), lambda i, j: (i, 0)), pl.BlockSpec((tn, d), lambda i, j: (j, 0)), tile, tile],
        out_specs=[tile, tile], out_shape=[SDS((t, D_FF), BF16), SDS((t, D_FF), BF16)], scratch_shapes=[],
        args=(dout, wd, a, u))


class _Plan:
    def __init__(self):
        self.entries, self.counts = collections.defaultdict(list), {}

    def carry(self, host, comm_of, after):
        self.entries[host].append((comm_of, after))

    def comm(self, host, g):
        comms = [comm_of(g) for comm_of, _ in self.entries.get(host, [])]
        self.counts[host] = [len(c.arrays) for c in comms]
        return functools.reduce(_join_comms, comms) if comms else None

    def done(self, host, results, w):
        start = 0
        for (_, after), n in zip(self.entries.get(host, []), self.counts.get(host, [])):
            after(results[start:start + n], w)
            start += n


def _ffn_fwd(x, w, tag, plan, g):
    comm = plan.comm(f"{tag}_rms", g)
    res = _rowwise(_rms_f, [x], [w[f"{tag}_norm"]], [[0]], [BF16], tm=512, name=f"{tag}_rms", comm=comm)
    (h,), carried = res if comm is not None else (res, [])
    plan.done(f"{tag}_rms", carried, w)
    (a, u, act), carried = _gate_up_act(h, w[f"{tag}_wgt"], w[f"{tag}_wut"], tm=2048, tn=256, name=f"{tag}_gate_up",
                                        comm=plan.comm(f"{tag}_gate_up", g))
    plan.done(f"{tag}_gate_up", carried, w)
    comm = plan.comm(f"{tag}_down", g)
    out = _mm(act, w[f"{tag}_wd"], tm=1024, tn=D_MODEL, tk=D_FF, name=f"{tag}_down", res=x, scale=0.5, comm=comm)
    if comm is not None:
        out, carried = out
        plan.done(f"{tag}_down", carried, w)
    return out, (h, a, u, act)


def _ffn_bwd(dout, x, w, saved, tag, plan, g):
    h, a, u, act = saved

    def carrying(fn, host, *args, **kwargs):
        comm = plan.comm(host, g)
        res = fn(*args, name=host, comm=comm, **kwargs)
        out, carried = res if comm is not None else (res, [])
        plan.done(host, carried, w)
        return out

    (da, du), carried = _dact_swiglu(dout, w[f"{tag}_wd"], a, u, tm=2048, tn=256, name=f"{tag}_dact",
                                     comm=plan.comm(f"{tag}_dact", g))
    plan.done(f"{tag}_dact", carried, w)
    g[f"{tag}_wd"] = _mm(act, dout, ta=True, tm=D_FF // 2, tn=D_MODEL, tk=1024, name=f"{tag}_dwd", scale=0.5)
    g[f"{tag}_wgt"] = carrying(_mm, f"{tag}_dwg", da, h, ta=True, tm=D_FF // 2, tn=D_MODEL, tk=1024)
    g[f"{tag}_wut"] = carrying(_mm, f"{tag}_dwu", du, h, ta=True, tm=D_FF // 2, tn=D_MODEL, tk=1024)
    dh = carrying(_mm, f"{tag}_dh_g", da, w[f"{tag}_wgt"], tm=1024, tn=D_MODEL, tk=D_FF)
    dh = carrying(_mm, f"{tag}_dh_u", du, w[f"{tag}_wut"], tm=1024, tn=D_MODEL, tk=D_FF, res=dh)
    dx, g[f"{tag}_norm"] = _rowwise_bwd(_rms_f, [x], [w[f"{tag}_norm"]], [dh], x_grad=[True], p_grad=[True],
                                        dx_groups=[[0]], dx_dtypes=[F32], tm=512, name=f"{tag}_drms",
                                        extra={0: dout})
    return dx


def _local_step(x, target, w, plan=None):
    plan = plan or _Plan()
    ones_bd = jnp.kron(jnp.eye(HB_HEADS, dtype=F32), jnp.ones((HB_DIM, HB_DIM), F32))
    g = {}
    x1, ffn1_saved = _ffn_fwd(x, w, "ffn1", plan, g)
    hm, = _rowwise(_rms_f, [x1], [w["mix_norm"]], [[0]], [BF16], tm=512, name="mix_rms")
    p_h = _mm(hm, w["w_in_h"], tm=2048, tn=256, tk=D_MODEL, name="inproj_h")
    p_r = _mm(hm, w["w_in_r"], tm=2048, tn=256, tk=D_MODEL, name="inproj_r")
    o_a, hgrn_states = _hgrn_fwd(p_h, w["lb0"], w["lb1"], w["hgrn_out_norm"])

    mu = w["mu_pad"]
    prep_xs = [(p_r, W_B, 0), (p_r, W_B, 1), (p_r, W_B, 2), (p_r, LORA_PAD, 6),
               ("prev", p_r, W_B, 0), ("prev", p_r, W_B, 1), ("prev", p_r, W_B, 2), ("prev", p_r, LORA_PAD, 6)]
    prep_ps = [(mu, W_B, 0), (mu, W_B, 1), (mu, W_B, 2), (mu, LORA_PAD, 6), w["rwkv_w0"], w["w2_pad"], w["rwkv_a0"],
               w["a2_pad"], w["g2_pad"], w["rwkv_k_k"], w["rwkv_k_a"], ones_bd]
    prep_f = _rwkv_prep_f
    r, lw, k2, v, a_vec, b_vec, gate = _rowwise(prep_f, prep_xs, prep_ps, [[0], [1], [2], [3], [4], [5], [6]],
                                                [F32] * 7, tm=256, name="rwkv_prep")
    seqs = [r, lw, k2, v, a_vec, b_vec]
    post_ps = [w["rwkv_r_k"], w["rwkv_gn_w"], w["rwkv_gn_b"], ones_bd]
    (y, o_b, rwkv_states), carried = _rwkv_fwd(seqs, gate, post_ps, comm=plan.comm("rwkv_fwd", g))
    plan.done("rwkv_fwd", carried, w)
    x2 = _mm(o_a, w["w_out_a"], tm=2048, tn=256, tk=W_A, name="outproj_a", res=x1)
    x2 = _mm(o_b, w["w_out_b"], tm=2048, tn=256, tk=W_B, name="outproj_b", res=x2)
    x3, ffn2_saved = _ffn_fwd(x2, w, "ffn2", plan, g)
    dx3, g["final_norm"], loss = _final_loss(x3, w["final_norm"], target, tm=256)

    dx2 = _ffn_bwd(dx3, x2, w, ffn2_saved, "ffn2", plan, g)
    do_a = _mm(dx2, w["w_out_a"], tb=True, tm=2048, tn=256, tk=D_MODEL, name="outproj_do_a")
    do_b = _mm(dx2, w["w_out_b"], tb=True, tm=2048, tn=256, tk=D_MODEL, name="outproj_do_b")
    g["w_out_a"] = _mm(o_a, dx2, ta=True, tm=W_A, tn=D_MODEL, tk=1024, name="outproj_dw_a")
    g["w_out_b"] = _mm(o_b, dx2, ta=True, tm=W_B, tn=D_MODEL, tk=1024, name="outproj_dw_b")

    (dp_h, g["lb0"], g["lb1"], g["hgrn_out_norm"]), carried = _hgrn_bwd(
        p_h, w["lb0"], w["lb1"], w["hgrn_out_norm"], hgrn_states, do_a, 0, comm=plan.comm("hgrn_bwd", g))
    plan.done("hgrn_bwd", carried, w)
    rwkv_grads, carried = _rwkv_bwd(seqs, gate, post_ps, y, rwkv_states, do_b, comm=plan.comm("rwkv_bwd", g))
    plan.done("rwkv_bwd", carried, w)
    dr, dlw, dk2, dv, da_vec, db_vec, dgate, g["rwkv_r_k"], g["rwkv_gn_w"], g["rwkv_gn_b"] = rwkv_grads
    prep_out = _rowwise_bwd(prep_f, prep_xs, prep_ps, [dr, dlw, dk2, dv, da_vec, db_vec, dgate],
                            x_grad=[True] * 8, p_grad=[True] * 11 + [False], dx_groups=[[0, 1, 2, 3], [4, 5, 6, 7]],
                            dx_dtypes=[F32], tm=256, name="rwkv_prep_bwd", fold_next=(0, 1))
    dp_r = prep_out[0]
    (dmu_r, dmu_k, dmu_v, dmu_lo, g["rwkv_w0"], g["w2_pad"], g["rwkv_a0"], g["a2_pad"], g["g2_pad"],
     g["rwkv_k_k"], g["rwkv_k_a"]) = prep_out[1:]
    g["mu_pad"] = jnp.concatenate([dmu_r, dmu_k, dmu_v, dmu_lo], axis=1)
    dhm = _mm(dp_h, w["w_in_h"], tb=True, tm=1024, tn=D_MODEL, tk=N_HGRN_COLS, name="inproj_dh_h")
    dhm = _mm(dp_r, w["w_in_r"], tb=True, tm=1024, tn=D_MODEL, tk=N_RWKV_PAD, name="inproj_dh_r", res=dhm)
    g["w_in_h"] = _mm(hm, dp_h, ta=True, tm=D_MODEL, tn=D_MODEL, tk=1024, name="inproj_dw_h")
    g["w_in_r"] = _mm(hm, dp_r, ta=True, tm=D_MODEL, tn=N_RWKV_PAD // 2, tk=1024, name="inproj_dw_r")
    mix_comm = plan.comm("mix_drms", g)
    mix_out = _rowwise_bwd(_rms_f, [x1], [w["mix_norm"]], [dhm], x_grad=[True], p_grad=[True], dx_groups=[[0]],
                           dx_dtypes=[F32], tm=512, name="mix_drms", extra={0: dx2}, comm=mix_comm)
    (dx1, g["mix_norm"]), carried = mix_out if mix_comm is not None else (mix_out, [])
    plan.done("mix_drms", carried, w)
    dx0 = _ffn_bwd(dx1, x, w, ffn1_saved, "ffn1", plan, g)
    return loss, dx0, g


HBM_SPEC = pl.BlockSpec(memory_space=pl.ANY)

Comm = collections.namedtuple("Comm", "arrays out_shapes aliased sem_shapes start finish")


def _join_comms(first, second):
    assert first.aliased == second.aliased
    n, s = len(first.arrays), len(first.sem_shapes)

    def start(ins, outs, sems):
        first.start(ins[:n], outs[:n], sems[:s])
        second.start(ins[n:], outs[n:], sems[s:])

    def finish(ins, outs, sems):
        first.finish(ins[:n], outs[:n], sems[:s])
        second.finish(ins[n:], outs[n:], sems[s:])

    return Comm(list(first.arrays) + list(second.arrays), list(first.out_shapes) + list(second.out_shapes),
                first.aliased, list(first.sem_shapes) + list(second.sem_shapes), start, finish)


def _run_comm(comm, name):
    n = len(comm.arrays)

    def body(*refs):
        ins, outs, sems = refs[:n], refs[n:2 * n], refs[2 * n:]
        comm.start(ins, outs, sems)
        comm.finish(ins, outs, sems)

    return pl.pallas_call(
        body, name=name, in_specs=[HBM_SPEC] * n, out_specs=[HBM_SPEC] * n, out_shape=list(comm.out_shapes),
        input_output_aliases={t: t for t in range(n)} if comm.aliased else {},
        scratch_shapes=list(comm.sem_shapes))(*comm.arrays)


def _hosting_call(body, comm, *, name, grid, in_specs, out_specs, out_shape, scratch_shapes, args):
    sem = ("arbitrary",) * len(grid)
    if comm is None:
        res = pl.pallas_call(body, name=name, grid=grid, in_specs=in_specs, out_specs=out_specs, out_shape=out_shape,
                             scratch_shapes=scratch_shapes, compiler_params=_params(sem))(*args)
        return list(res), []
    ni, no, ns, nc = len(in_specs), len(out_specs), len(scratch_shapes), len(comm.arrays)

    def wrapped(*refs):
        ins, cins = refs[:ni], refs[ni:ni + nc]
        outs, couts = refs[ni + nc:ni + nc + no], refs[ni + nc + no:ni + 2 * nc + no]
        scr, sems = refs[ni + 2 * nc + no:ni + 2 * nc + no + ns], refs[ni + 2 * nc + no + ns:]
        first = functools.reduce(jnp.logical_and, [pl.program_id(k) == 0 for k in range(len(grid))])
        last = functools.reduce(jnp.logical_and, [pl.program_id(k) == grid[k] - 1 for k in range(len(grid))])

        @pl.when(first)
        def _():
            comm.start(cins, couts, sems)

        body(*ins, *outs, *scr)

        @pl.when(last)
        def _():
            comm.finish(cins, couts, sems)

    res = pl.pallas_call(
        wrapped, name=name, grid=grid, in_specs=list(in_specs) + [HBM_SPEC] * nc,
        out_specs=list(out_specs) + [HBM_SPEC] * nc, out_shape=list(out_shape) + list(comm.out_shapes),
        scratch_shapes=list(scratch_shapes) + list(comm.sem_shapes),
        input_output_aliases={ni + t: no + t for t in range(nc)} if comm.aliased else {},
        compiler_params=_params(sem))(*args, *comm.arrays)
    return list(res[:no]), list(res[no:])


def _chips(x, y):
    return [(1 - x, y), (x, 1 - y), (1 - x, 1 - y)]


def _gather_comm(bufs):
    n = len(bufs)

    def copies(outs, sems):
        ici_send, ici_recv, d2d_send, d2d_recv = sems
        x, y, c = lax.axis_index("x"), lax.axis_index("y"), lax.axis_index("c")

        def half(t, slot, hc):
            hr = bufs[t].shape[1] // 2
            return outs[t].at[slot, pl.ds(pl.multiple_of(hc * hr, 16), hr), :]

        def ici(t, j, slot, px, py):
            return pltpu.make_async_remote_copy(src_ref=half(t, slot, c), dst_ref=half(t, slot, c),
                                                send_sem=ici_send.at[3 * t + j], recv_sem=ici_recv.at[3 * t + j],
                                                device_id=(px, py, c), device_id_type=MESH)

        def d2d(t, j, slot, hc):
            return pltpu.make_async_remote_copy(src_ref=half(t, slot, hc), dst_ref=half(t, slot, hc),
                                                send_sem=d2d_send.at[3 * t + j], recv_sem=d2d_recv.at[3 * t + j],
                                                device_id=(x, y, 1 - c), device_id_type=MESH)

        peers = [(t, j, px, py) for t in range(n) for j, (px, py) in enumerate(_chips(x, y))]
        return ici, d2d, peers, 2 * x + y, c

    def start(ins, outs, sems):
        ici, _, peers, me, _ = copies(outs, sems)
        for t, j, px, py in peers:
            ici(t, j, me, px, py).start()

    def finish(ins, outs, sems):
        ici, d2d, peers, me, c = copies(outs, sems)
        for t, j, px, py in peers:
            ici(t, j, 2 * px + py, px, py).wait_recv()
            d2d(t, j, 2 * px + py, c).start()
        for t, j, px, py in peers:
            d2d(t, j, 2 * px + py, 1 - c).wait_recv()
        for t, j, px, py in peers:
            ici(t, j, me, px, py).wait_send()
            d2d(t, j, 2 * px + py, c).wait_send()

    return Comm(list(bufs), [SDS(b.shape, b.dtype) for b in bufs], True, [pltpu.SemaphoreType.DMA((3 * n,))] * 4,
                start, finish)


def _sibling_exchange_comm(gs):
    n = len(gs)

    def copies(ins, outs, sems):
        x, y, c = lax.axis_index("x"), lax.axis_index("y"), lax.axis_index("c")
        cps = []
        for t in range(n):
            hr = gs[t].shape[1] // 2
            src = ins[t].at[:, pl.ds(pl.multiple_of((1 - c) * hr, SUBLANES), hr), :]
            cps.append(pltpu.make_async_remote_copy(src_ref=src, dst_ref=outs[t], send_sem=sems[0].at[t],
                                                    recv_sem=sems[1].at[t], device_id=(x, y, 1 - c),
                                                    device_id_type=MESH))
        return cps

    def start(ins, outs, sems):
        for cp in copies(ins, outs, sems):
            cp.start()

    def finish(ins, outs, sems):
        for cp in copies(ins, outs, sems):
            cp.wait()

    return Comm(list(gs), [SDS((N_CHIPS, g.shape[1] // 2, g.shape[2]), g.dtype) for g in gs], False,
                [pltpu.SemaphoreType.DMA((n,))] * 2, start, finish)


def _chip_exchange_comm(ss):
    n = len(ss)

    def copies(ins, outs, sems):
        x, y, c = lax.axis_index("x"), lax.axis_index("y"), lax.axis_index("c")
        me = 2 * x + y

        def copy(t, j, px, py, src_slot, dst_slot):
            return pltpu.make_async_remote_copy(src_ref=ins[t].at[src_slot], dst_ref=outs[t].at[dst_slot],
                                                send_sem=sems[0].at[3 * t + j], recv_sem=sems[1].at[3 * t + j],
                                                device_id=(px, py, c), device_id_type=MESH)

        peers = [(t, j, px, py) for t in range(n) for j, (px, py) in enumerate(_chips(x, y))]
        return copy, peers, me

    def start(ins, outs, sems):
        copy, peers, me = copies(ins, outs, sems)
        for t, j, px, py in peers:
            copy(t, j, px, py, 2 * px + py, me).start()

    def finish(ins, outs, sems):
        copy, peers, me = copies(ins, outs, sems)
        for t, j, px, py in peers:
            copy(t, j, px, py, me, 2 * px + py).wait_recv()
        for t, j, px, py in peers:
            copy(t, j, px, py, 2 * px + py, me).wait_send()

    return Comm(list(ss), [SDS(s.shape, s.dtype) for s in ss], False, [pltpu.SemaphoreType.DMA((3 * n,))] * 2,
                start, finish)


def _sibling_swap_comm(fs):
    n = len(fs)

    def copies(ins, outs, sems):
        x, y, c = lax.axis_index("x"), lax.axis_index("y"), lax.axis_index("c")
        return [pltpu.make_async_remote_copy(src_ref=ins[t], dst_ref=outs[t], send_sem=sems[0].at[t],
                                             recv_sem=sems[1].at[t], device_id=(x, y, 1 - c), device_id_type=MESH)
                for t in range(n)]

    def start(ins, outs, sems):
        for cp in copies(ins, outs, sems):
            cp.start()

    def finish(ins, outs, sems):
        for cp in copies(ins, outs, sems):
            cp.wait()

    return Comm(list(fs), [SDS(f.shape, f.dtype) for f in fs], False, [pltpu.SemaphoreType.DMA((n,))] * 2,
                start, finish)


def _row_tile(rows, cap=512):
    best = SUBLANES
    for tr in range(SUBLANES, min(rows, cap) + 1, SUBLANES):
        if rows % tr == 0:
            best = tr
    return best


def _add_halves(g4, r4, c_idx, name):
    _, hr, lanes = r4.shape
    tr = _row_tile(hr)
    nb = hr // tr

    def body(c_ref, a_ref, b_ref, o_ref):
        o_ref[...] = (a_ref[...] + b_ref[...]).astype(o_ref.dtype)

    grid_spec = pltpu.PrefetchScalarGridSpec(
        num_scalar_prefetch=1, grid=(N_CHIPS, nb),
        in_specs=[pl.BlockSpec((None, tr, lanes), lambda q, i, c_ref: (q, c_ref[0] * nb + i, 0)),
                  pl.BlockSpec((None, tr, lanes), lambda q, i, c_ref: (q, i, 0))],
        out_specs=pl.BlockSpec((None, tr, lanes), lambda q, i, c_ref: (q, i, 0)))
    return pl.pallas_call(body, name=name, grid_spec=grid_spec, out_shape=SDS(r4.shape, BF16),
                          compiler_params=_params(("parallel", "parallel")))(c_idx, g4, r4)


def _sum_chips(r4, s4, me_idx, name):
    _, rows, lanes = r4.shape
    tr = _row_tile(rows)

    def body(me_ref, a_ref, b_ref, c_ref, d_ref, own_ref, o_ref):
        own = own_ref[...].astype(F32)
        p = [jnp.where(me_ref[0] == q, own, ref[...].astype(F32)) for q, ref in enumerate((a_ref, b_ref, c_ref, d_ref))]
        o_ref[...] = ((p[0] + p[1]) + p[2]) + p[3]

    other = lambda q: (lambda i, me_ref: (jnp.where(me_ref[0] == q, (q + 1) % N_CHIPS, q), i, 0))
    grid_spec = pltpu.PrefetchScalarGridSpec(
        num_scalar_prefetch=1, grid=(rows // tr,),
        in_specs=[pl.BlockSpec((None, tr, lanes), other(q)) for q in range(N_CHIPS)]
        + [pl.BlockSpec((None, tr, lanes), lambda i, me_ref: (me_ref[0], i, 0))],
        out_specs=pl.BlockSpec((tr, lanes), lambda i, me_ref: (i, 0)))
    return pl.pallas_call(body, name=name, grid_spec=grid_spec, out_shape=SDS((rows, lanes), F32),
                          compiler_params=_params(("parallel",)))(me_idx, r4, r4, r4, r4, s4)


def _adamw(wf, g_own, g_other, mf, vf, c_idx, name):
    rows, lanes = wf.shape
    hr = rows // 2
    tr = _row_tile(hr)
    nb = hr // tr
    c1 = 1.0 / (1.0 - ADAM_B1 ** ADAM_STEP)
    c2 = 1.0 / (1.0 - ADAM_B2 ** ADAM_STEP)

    def body(c_ref, w_ref, go_ref, gx_ref, m_ref, v_ref, g_ref, d_ref, nm_ref, nv_ref):
        gv = jnp.where(pl.program_id(0) == c_ref[0], go_ref[...], gx_ref[...])
        m = ADAM_B1 * m_ref[...] + (1.0 - ADAM_B1) * gv
        v = ADAM_B2 * v_ref[...] + (1.0 - ADAM_B2) * (gv * gv)
        g_ref[...] = gv
        d_ref[...] = -ADAM_LR * ((m * c1) / (jnp.sqrt(v * c2) + ADAM_EPS) + ADAM_WD * w_ref[...])
        nm_ref[...] = m
        nv_ref[...] = v

    full = pl.BlockSpec((tr, lanes), lambda h, i, c_ref: (h * nb + i, 0))
    half = pl.BlockSpec((tr, lanes), lambda h, i, c_ref: (i, 0))
    grid_spec = pltpu.PrefetchScalarGridSpec(num_scalar_prefetch=1, grid=(2, nb),
                                             in_specs=[full, half, half, full, full], out_specs=[full] * 4)
    return pl.pallas_call(body, name=name, grid_spec=grid_spec, out_shape=[SDS((rows, lanes), F32)] * 4,
                          compiler_params=_params(("parallel", "parallel")))(c_idx, wf, g_own, g_other, mf, vf)


BIG = ("ffn1_w_gate", "ffn1_w_up", "ffn1_w_down", "ffn2_w_gate", "ffn2_w_up", "ffn2_w_down", "w_out", "w_in")
TRANSPOSED = ("ffn1_w_gate", "ffn1_w_up", "ffn2_w_gate", "ffn2_w_up")
PACKED = ("rwkv_w2", "rwkv_a2", "rwkv_g2")
SMALL_SHAPES = {"ffn1_norm": (1, D_MODEL), "mix_norm": (1, D_MODEL), "hgrn_lb_logits": (2, W_A),
                "hgrn_out_norm": (1, W_A), "rwkv_shift_mu": (1, N_RWKV_COLS), "rwkv_w0": (1, W_B),
                "rwkv_a0": (1, W_B), "rwkv_k_k": (1, W_B), "rwkv_k_a": (1, W_B),
                "rwkv_r_k": (1, HB_HEADS, HB_DIM), "rwkv_gn_w": (1, W_B), "rwkv_gn_b": (1, W_B),
                "ffn2_norm": (1, D_MODEL), "final_norm": (D_MODEL,)}
PACK_ELEMS = sum(_numel(_shard_shape(n)) for n in PACKED) + sum(_numel(SMALL_SHAPES[n]) for n in SMALL)
PACK_ROWS = -(-PACK_ELEMS // (32 * LANES)) * 32


def _to_rows(name, shard):
    return shard[0].T if name in TRANSPOSED else shard[0]


def _from_rows(name, rows):
    return (rows.T if name in TRANSPOSED else rows)[None]


def _pack(sharded, small):
    flat = jnp.concatenate([sharded[n].reshape(-1) for n in PACKED] + [small[n].reshape(-1) for n in SMALL])
    return jnp.pad(flat, (0, PACK_ROWS * LANES - flat.shape[0])).reshape(PACK_ROWS, LANES)


def _unpack(packed):
    flat, out, off = packed.reshape(-1), {}, 0
    for n in PACKED:
        shp = _shard_shape(n)
        out[n] = flat[off:off + _numel(shp)].reshape((1,) + shp)
        off += _numel(shp)
    for n in SMALL:
        shp = SMALL_SHAPES[n]
        out[n] = flat[off:off + _numel(shp)].reshape(shp)
        off += _numel(shp)
    return out


def _quarter(full, name, q):
    shape, ax = SHARDED_SHAPES[name]
    w = shape[ax] // N_CHIPS
    return lax.slice_in_dim(full, q * w, (q + 1) * w, axis=ax)


def kernel(x, ffn1_norm, ffn1_w_gate, ffn1_w_up, ffn1_w_down, mix_norm, w_in, hgrn_lb_logits, hgrn_out_norm, rwkv_shift_mu, rwkv_w0, rwkv_w2, rwkv_a0, rwkv_a2, rwkv_g2, rwkv_k_k, rwkv_k_a, rwkv_r_k, rwkv_gn_w, rwkv_gn_b, w_out, ffn2_norm, ffn2_w_gate, ffn2_w_up, ffn2_w_down, final_norm, loss_target, m_ffn1_norm, m_ffn1_w_gate, m_ffn1_w_up, m_ffn1_w_down, m_mix_norm, m_w_in, m_hgrn_lb_logits, m_hgrn_out_norm, m_rwkv_shift_mu, m_rwkv_w0, m_rwkv_w2, m_rwkv_a0, m_rwkv_a2, m_rwkv_g2, m_rwkv_k_k, m_rwkv_k_a, m_rwkv_r_k, m_rwkv_gn_w, m_rwkv_gn_b, m_w_out, m_ffn2_norm, m_ffn2_w_gate, m_ffn2_w_up, m_ffn2_w_down, m_final_norm, v_ffn1_norm, v_ffn1_w_gate, v_ffn1_w_up, v_ffn1_w_down, v_mix_norm, v_w_in, v_hgrn_lb_logits, v_hgrn_out_norm, v_rwkv_shift_mu, v_rwkv_w0, v_rwkv_w2, v_rwkv_a0, v_rwkv_a2, v_rwkv_g2, v_rwkv_k_k, v_rwkv_k_a, v_rwkv_r_k, v_rwkv_gn_w, v_rwkv_gn_b, v_w_out, v_ffn2_norm, v_ffn2_w_gate, v_ffn2_w_up, v_ffn2_w_down, v_final_norm):
    args = dict(locals())
    wts = {n: args[n] for n in ALL_WEIGHTS}
    moms = {n: args["m_" + n] for n in ALL_WEIGHTS}
    vars_ = {n: args["v_" + n] for n in ALL_WEIGHTS}

    me = 2 * lax.axis_index("x") + lax.axis_index("y")
    c_idx = lax.axis_index("c").astype(jnp.int32).reshape(1)
    me_idx = me.astype(jnp.int32).reshape(1)
    shard_of = {n: _to_rows(n, wts[n]).astype(BF16) for n in BIG}
    shard_of["packed"] = _pack(wts, {n: wts[n] for n in SMALL}).astype(BF16)
    group = {"ffn1": BIG[0:3], "ffn2": BIG[3:6]}

    def slot_bufs(names):
        return [lax.dynamic_update_slice(jnp.zeros((N_CHIPS,) + shard_of[n].shape, BF16), shard_of[n][None],
                                         (me, 0, 0)) for n in names]

    def ffn_weights(tag, gathered):
        return {f"{tag}_wgt": gathered[0].reshape(D_FF, D_MODEL), f"{tag}_wut": gathered[1].reshape(D_FF, D_MODEL),
                f"{tag}_wd": gathered[2].reshape(D_FF, D_MODEL)}

    def w_in_weights(gathered):
        w_in_full = jnp.concatenate([gathered[0][q] for q in range(N_CHIPS)], axis=1)
        return {"w_in_h": w_in_full[:, :N_HGRN_COLS],
                "w_in_r": jnp.pad(w_in_full[:, N_HGRN_COLS:], ((0, 0), (0, N_RWKV_PAD - N_RWKV_COLS)))}

    def mixer_weights(gathered):
        w_out_full = gathered[0].reshape(D_MODEL, D_MODEL)
        packs = gathered[1].reshape(N_CHIPS, PACK_ROWS * LANES)
        full, off = {}, 0
        for n in PACKED:
            shp = _shard_shape(n)
            full[n] = jnp.concatenate([packs[q, off:off + _numel(shp)].reshape(shp) for q in range(N_CHIPS)], axis=1)
            off += _numel(shp)
        zrow = lambda nrow: jnp.zeros((nrow, W_B), BF16)
        return {"w_out_a": w_out_full[:W_A], "w_out_b": w_out_full[W_A:],
                "w2_pad": jnp.concatenate([full["rwkv_w2"], zrow(LORA_PAD - 32)], axis=0),
                "a2_pad": jnp.concatenate([zrow(32), full["rwkv_a2"], zrow(LORA_PAD - 64)], axis=0),
                "g2_pad": jnp.concatenate([zrow(64), full["rwkv_g2"], zrow(LORA_PAD - 160)], axis=0)}

    plan = _Plan()
    w = {}
    plan.carry("ffn1_rms", lambda g: _gather_comm(slot_bufs(group["ffn1"][:2])),
               lambda res, w_: w_.update({"ffn1_wgt": res[0].reshape(D_FF, D_MODEL),
                                          "ffn1_wut": res[1].reshape(D_FF, D_MODEL)}))

    def after_gate_up(res, w_):
        w_["ffn1_wd"] = res[0].reshape(D_FF, D_MODEL)
        w_.update(w_in_weights(res[1:]))

    plan.carry("ffn1_gate_up", lambda g: _gather_comm(slot_bufs(("ffn1_w_down", "w_in"))), after_gate_up)
    plan.carry("ffn1_down", lambda g: _gather_comm(slot_bufs(("w_out", "packed"))),
               lambda res, w_: w_.update(mixer_weights(res)))
    plan.carry("rwkv_fwd", lambda g: _gather_comm(slot_bufs(group["ffn2"])),
               lambda res, w_: w_.update(ffn_weights("ffn2", res)))
    w["ffn1_norm"], w["ffn2_norm"] = ffn1_norm, ffn2_norm
    w["mix_norm"] = mix_norm
    w["lb0"], w["lb1"] = hgrn_lb_logits[0:1], hgrn_lb_logits[1:2]
    w["hgrn_out_norm"] = hgrn_out_norm
    w["mu_pad"] = jnp.pad(rwkv_shift_mu, ((0, 0), (0, N_RWKV_PAD - N_RWKV_COLS)))
    for n in ("rwkv_w0", "rwkv_a0", "rwkv_k_k", "rwkv_k_a", "rwkv_gn_w", "rwkv_gn_b"):
        w[n] = wts[n]
    w["rwkv_r_k"] = rwkv_r_k.reshape(1, W_B)
    w["final_norm"] = final_norm.reshape(1, D_MODEL)

    def reduce_rows(names, gs):
        r1 = _run_comm(_sibling_exchange_comm(gs), "grad_sibling_exchange")
        s4 = [_add_halves(gt, rt, c_idx, f"grad_add_halves_{n}") for gt, rt, n in zip(gs, r1, names)]
        r2 = _run_comm(_chip_exchange_comm(s4), "grad_chip_exchange")
        return [_sum_chips(rt, st, me_idx, f"grad_sum_chips_{n}") for rt, st, n in zip(r2, s4, names)]

    early = {}

    def reduce_early(names, grads_of, sibling_host, chips_host):
        def sibling_comm(g):
            early[names, "gs"] = grads_of(g)
            return _sibling_exchange_comm(early[names, "gs"])

        def after_sibling(res, w_):
            early[names, "s4"] = [_add_halves(gt, rt, c_idx, f"grad_add_halves_{n}")
                                  for gt, rt, n in zip(early[names, "gs"], res, names)]

        def after_chips(res, w_):
            early.update(zip(names, [_sum_chips(rt, st, me_idx, f"grad_sum_chips_{n}")
                                     for rt, st, n in zip(res, early[names, "s4"], names)]))

        plan.carry(sibling_host, sibling_comm, after_sibling)
        plan.carry(chips_host, lambda g: _chip_exchange_comm(early[names, "s4"]), after_chips)

    def proj_grads(g):
        g_w_in = jnp.concatenate([g["w_in_h"], g["w_in_r"][:, :N_RWKV_COLS]], axis=1)
        return [jnp.concatenate([g["w_out_a"], g["w_out_b"]], axis=0).reshape(N_CHIPS, -1, D_MODEL),
                jnp.stack([_quarter(g_w_in, "w_in", q) for q in range(N_CHIPS)])]

    rows_of = lambda keys: (lambda g: [g[k].reshape(N_CHIPS, -1, D_MODEL) for k in keys])
    reduce_early(group["ffn2"], rows_of(("ffn2_wgt", "ffn2_wut", "ffn2_wd")), "hgrn_bwd", "rwkv_bwd")
    reduce_early(("w_out", "w_in"), proj_grads, "mix_drms", "ffn1_dact")
    reduce_early(("ffn1_w_down",), rows_of(("ffn1_wd",)), "ffn1_dwg", "ffn1_dwu")
    reduce_early(("ffn1_w_gate",), rows_of(("ffn1_wgt",)), "ffn1_dwu", "ffn1_dh_g")
    reduce_early(("ffn1_w_up",), rows_of(("ffn1_wut",)), "ffn1_dh_g", "ffn1_dh_u")
    loss_slab, grad_x, g = _local_step(x[0], loss_target[0], w, plan)
    loss = lax.psum(loss_slab[0, 0], ("x", "y", "c"))

    gfull = {
        "rwkv_w2": g["w2_pad"][0:32], "rwkv_a2": g["a2_pad"][32:64], "rwkv_g2": g["g2_pad"][64:160],
    }
    gsmall = {
        "ffn1_norm": g["ffn1_norm"], "mix_norm": g["mix_norm"],
        "hgrn_lb_logits": jnp.concatenate([g["lb0"], g["lb1"]], axis=0), "hgrn_out_norm": g["hgrn_out_norm"],
        "rwkv_shift_mu": g["mu_pad"][:, :N_RWKV_COLS], "rwkv_w0": g["rwkv_w0"], "rwkv_a0": g["rwkv_a0"],
        "rwkv_k_k": g["rwkv_k_k"], "rwkv_k_a": g["rwkv_k_a"], "rwkv_r_k": g["rwkv_r_k"],
        "rwkv_gn_w": g["rwkv_gn_w"], "rwkv_gn_b": g["rwkv_gn_b"], "ffn2_norm": g["ffn2_norm"],
        "final_norm": g["final_norm"],
    }
    packed = jnp.stack([_pack({n: _quarter(gfull[n], n, q) for n in PACKED}, gsmall) for q in range(N_CHIPS)])
    early["packed"], = reduce_rows(["packed"], [packed])
    names = list(BIG) + ["packed"]
    own = [early[n] for n in names]
    other = _run_comm(_sibling_swap_comm(own), "grad_sibling_swap")

    def rows_list(d):
        return [_to_rows(n, d[n]) for n in BIG] + [_pack(d, {n: d[n] for n in SMALL})]

    outs = [_adamw(wt, go, gx, mt, vt, c_idx, f"adamw_{n}")
            for wt, go, gx, mt, vt, n in zip(rows_list(wts), own, other, rows_list(moms), rows_list(vars_), names)]
    results = []
    for k in range(4):
        per = [outs[i][k] for i in range(len(names))]
        d = {n: _from_rows(n, z) for n, z in zip(BIG, per[:-1])}
        d.update(_unpack(per[-1]))
        results.append(d)
    return (loss, grad_x[None], *[r[n] for r in results for n in ALL_WEIGHTS])
```

```python
import collections
import functools

import jax
import jax.numpy as jnp
from jax import lax
from jax.experimental import pallas as pl
from jax.experimental.pallas import tpu as pltpu

F32 = jnp.float32
BF16 = jnp.bfloat16
SDS = jax.ShapeDtypeStruct
MESH = pl.DeviceIdType.MESH

D_MODEL = 1024
D_FF = 2816
W_A = 512
W_B = 512
HA_HEADS, HA_DIM = 4, 128
HB_HEADS, HB_DIM = 8, 64
HGRN_CHUNK = 64
HGRN_GROUP = 8
RWKV_CHUNK = 16
RWKV_GROUP = 8
N_HGRN_COLS = 4 * W_A
N_RWKV_COLS = 3 * W_B + 32 + 32 + 96
N_RWKV_PAD = 1792
LORA_PAD = 256
NORM_EPS = 1e-6
RWKV_GN_EPS = 64e-5
L2_EPS = 1e-12
ADAM_LR, ADAM_B1, ADAM_B2, ADAM_EPS, ADAM_WD, ADAM_STEP = 0.001, 0.9, 0.999, 1e-8, 0.01, 10

N_CHIPS = 4
VMEM_LIMIT_V7X = 56 * 1024 * 1024
LANES = 1024

SHARDED_SHAPES = {
    "ffn1_w_gate": ((D_MODEL, D_FF), 1), "ffn1_w_up": ((D_MODEL, D_FF), 1), "ffn1_w_down": ((D_FF, D_MODEL), 0),
    "w_in": ((D_MODEL, N_HGRN_COLS + N_RWKV_COLS), 1), "rwkv_w2": ((32, W_B), 1), "rwkv_a2": ((32, W_B), 1),
    "rwkv_g2": ((96, W_B), 1), "w_out": ((D_MODEL, D_MODEL), 0),
    "ffn2_w_gate": ((D_MODEL, D_FF), 1), "ffn2_w_up": ((D_MODEL, D_FF), 1), "ffn2_w_down": ((D_FF, D_MODEL), 0),
}
SMALL = ("ffn1_norm", "mix_norm", "hgrn_lb_logits", "hgrn_out_norm", "rwkv_shift_mu", "rwkv_w0", "rwkv_a0",
         "rwkv_k_k", "rwkv_k_a", "rwkv_r_k", "rwkv_gn_w", "rwkv_gn_b", "ffn2_norm", "final_norm")
ALL_WEIGHTS = ("ffn1_norm", "ffn1_w_gate", "ffn1_w_up", "ffn1_w_down", "mix_norm", "w_in", "hgrn_lb_logits",
               "hgrn_out_norm", "rwkv_shift_mu", "rwkv_w0", "rwkv_w2", "rwkv_a0", "rwkv_a2", "rwkv_g2", "rwkv_k_k",
               "rwkv_k_a", "rwkv_r_k", "rwkv_gn_w", "rwkv_gn_b", "w_out", "ffn2_norm", "ffn2_w_gate", "ffn2_w_up",
               "ffn2_w_down", "final_norm")


def _shard_shape(name):
    shape, ax = SHARDED_SHAPES[name]
    return tuple(s // N_CHIPS if i == ax else s for i, s in enumerate(shape))


def _numel(shape):
    n = 1
    for s in shape:
        n *= s
    return n


def _params(sem=None):
    return pltpu.CompilerParams(dimension_semantics=sem, vmem_limit_bytes=VMEM_LIMIT_V7X)


def _split2(x):
    hi = x.astype(BF16)
    return hi, (x.astype(F32) - hi.astype(F32)).astype(BF16)


def _dg(x, y, cx, cy, hi):
    dn = (((cx,), (cy,)), ((), ()))
    dot = lambda p, q: lax.dot_general(p, q, dn, preferred_element_type=F32)
    if hi == "x3":
        (xh, xl), (yh, yl) = _split2(x), _split2(y)
        return dot(xh, yh) + (dot(xh, yl) + dot(xl, yh))
    return dot(x.astype(BF16), y.astype(BF16))


def _make_mm(hi, cotangent_forms=None):
    @jax.custom_vjp
    def nn(x, y):
        return _dg(x, y, 1, 0, hi)

    @jax.custom_vjp
    def nt(x, y):
        return _dg(x, y, 1, 1, hi)

    @jax.custom_vjp
    def tn(x, y):
        return _dg(x, y, 0, 0, hi)

    bnn, bnt, btn = cotangent_forms or (nn, nt, tn)
    nn.defvjp(lambda x, y: (nn(x, y), (x, y)), lambda r, g: (bnt(g, r[1]), btn(r[0], g)))
    nt.defvjp(lambda x, y: (nt(x, y), (x, y)), lambda r, g: (bnn(g, r[1]), btn(g, r[0])))
    tn.defvjp(lambda x, y: (tn(x, y), (x, y)), lambda r, g: (bnt(r[1], g), bnn(r[0], g)))
    return nn, nt, tn


_nn, _nt, _tn = _make_mm(False)
_nn_x3, _nt_x3, _tn_x3 = _make_mm("x3", (_nn, _nt, _tn))


def _tri_apply(x, transpose):
    c = x.shape[0]
    tri = (lax.broadcasted_iota(jnp.int32, (c, c), 1) <= lax.broadcasted_iota(jnp.int32, (c, c), 0)).astype(BF16)
    dn = (((0 if transpose else 1,), (0,)), ((), ()))
    p1, p2 = _split2(x)
    dot = lambda p: lax.dot_general(tri, p, dn, preferred_element_type=F32)
    return dot(p1) + dot(p2)


@jax.custom_vjp
def _cumsum_rows(x):
    return _tri_apply(x, False)


_cumsum_rows.defvjp(lambda x: (_tri_apply(x, False), None), lambda _, g: (_tri_apply(g, True),))


def _sigmoid(x):
    return 1.0 / (1.0 + jnp.exp(-x))


def _silu(x):
    return x * _sigmoid(x)


def _softplus(z):
    return jnp.maximum(z, 0.0) + jnp.log(1.0 + jnp.exp(-jnp.abs(z)))


def _mm(a, b, *, ta=False, tb=False, tm, tn, tk, name, out_dtype=F32, res=None, scale=None, comm=None):
    m = a.shape[1] if ta else a.shape[0]
    kdim = a.shape[0] if ta else a.shape[1]
    n = b.shape[0] if tb else b.shape[1]
    assert (b.shape[1] if tb else b.shape[0]) == kdim
    tm, tn, tk = min(tm, m), min(tn, n), min(tk, kdim)
    assert m % tm == 0 and n % tn == 0 and kdim % tk == 0, (name, m, n, kdim)
    nk = kdim // tk
    a_spec = pl.BlockSpec((tk, tm), lambda i, j, k: (k, i)) if ta else pl.BlockSpec((tm, tk), lambda i, j, k: (i, k))
    b_spec = pl.BlockSpec((tn, tk), lambda i, j, k: (j, k)) if tb else pl.BlockSpec((tk, tn), lambda i, j, k: (k, j))
    o_spec = pl.BlockSpec((tm, tn), lambda i, j, k: (i, j))
    ca, cb = (0 if ta else 1), (1 if tb else 0)

    def body(*refs):
        if res is not None:
            a_ref, b_ref, r_ref, o_ref, acc_ref = refs
        else:
            a_ref, b_ref, o_ref, acc_ref = refs
        k = pl.program_id(2)

        @pl.when(k == 0)
        def _():
            acc_ref[...] = jnp.zeros_like(acc_ref)

        acc_ref[...] += _dg(a_ref[...], b_ref[...], ca, cb, False)

        @pl.when(k == nk - 1)
        def _():
            acc = acc_ref[...]
            if scale is not None:
                acc = acc * scale
            if res is not None:
                acc = r_ref[...] + acc
            o_ref[...] = acc.astype(out_dtype)

    in_specs = [a_spec, b_spec] + ([o_spec] if res is not None else [])
    args = (a, b) + ((res,) if res is not None else ())
    if comm is None:
        return pl.pallas_call(
            body, name=name, grid=(m // tm, n // tn, nk), in_specs=in_specs, out_specs=o_spec,
            out_shape=SDS((m, n), out_dtype), scratch_shapes=[pltpu.VMEM((tm, tn), F32)],
            compiler_params=_params(("parallel", "parallel", "arbitrary")))(*args)
    (out,), carried = _hosting_call(
        body, comm, name=name, grid=(m // tm, n // tn, nk), in_specs=in_specs, out_specs=[o_spec],
        out_shape=[SDS((m, n), out_dtype)], scratch_shapes=[pltpu.VMEM((tm, tn), F32)], args=args)
    return out, carried


def _row_spec(x, tm, tile_of=lambda i: i):
    if isinstance(x, tuple):
        arr, w, j = x
        return arr, pl.BlockSpec((tm, w), lambda i, j=j: (tile_of(i), j))
    return x, pl.BlockSpec((tm, x.shape[1]), lambda i: (tile_of(i), 0))


def _par_spec(p):
    if isinstance(p, tuple):
        arr, w, j = p
        return arr, pl.BlockSpec((arr.shape[0], w), lambda i, j=j: (0, j))
    return p, pl.BlockSpec(p.shape, lambda i: (0, 0))


def _store_groups(refs, groups, vals):
    for ref, idxs in zip(refs, groups):
        off = 0
        for ix in idxs:
            v = vals[ix]
            ref[:, off:off + v.shape[1]] = v.astype(ref.dtype)
            off += v.shape[1]


SUBLANES = 8


def _x_plan(xs, tm, t, tile_of=lambda i: i):
    arrays, specs, plan = [], [], []
    nb = tm // SUBLANES
    for x in xs:
        if isinstance(x, tuple) and isinstance(x[0], str):
            kind, arr, w, j = x
            if kind == "prev":
                halo = lambda i, j=j: (jnp.maximum(tile_of(i) * nb - 1, 0), j)
            else:
                halo = lambda i, j=j: (jnp.minimum((tile_of(i) + 1) * nb, t // SUBLANES - 1), j)
            arrays += [arr, arr]
            specs += [pl.BlockSpec((tm, w), lambda i, j=j: (tile_of(i), j)), pl.BlockSpec((SUBLANES, w), halo)]
            plan.append((kind, 2, w))
        else:
            arr, spec = _row_spec(x, tm, tile_of)
            arrays.append(arr)
            specs.append(spec)
            plan.append(("plain", 1, spec.block_shape[1]))
    return arrays, specs, plan


def _x_vals(refs, plan, tm, nt, tile_of=lambda i: i):
    vals, k = [], 0
    i = tile_of(pl.program_id(0))
    rows = lax.broadcasted_iota(jnp.int32, (tm, 1), 0)
    for kind, n, _ in plan:
        main = refs[k][...].astype(F32)
        if kind == "prev":
            edge = jnp.where(i == 0, 0.0, refs[k + 1][SUBLANES - 1:SUBLANES, :].astype(F32))
            main = jnp.where(rows == 0, edge, pltpu.roll(main, 1, 0))
        elif kind == "next":
            edge = jnp.where(i == nt - 1, 0.0, refs[k + 1][0:1, :].astype(F32))
            main = jnp.where(rows == tm - 1, edge, pltpu.roll(main, tm - 1, 0))
        vals.append(main)
        k += n
    return vals


def _tile_rows(xs, tm):
    arr = xs[0]
    if isinstance(arr, tuple):
        arr = arr[1] if isinstance(arr[0], str) else arr[0]
    return min(tm, arr.shape[0]), arr.shape[0]


def _rowwise(f, xs, params, out_groups, out_dtypes, *, tm, name, comm=None):
    tm, t = _tile_rows(xs, tm)
    nt = t // tm
    xa, xspecs, plan = _x_plan(xs, tm, t)
    pa, pspecs = (zip(*[_par_spec(p) for p in params]) if params else ((), ()))
    nxr, npar = len(xa), len(pa)
    x_sds = [SDS((tm, w), F32) for _, _, w in plan]
    p_sds = [SDS(s.block_shape, F32) for s in pspecs]
    outs_sds = jax.eval_shape(lambda *vals: f(*vals), *x_sds, *p_sds)
    widths = [sum(outs_sds[ix].shape[1] for ix in idxs) for idxs in out_groups]

    def body(*refs):
        vals = _x_vals(refs[:nxr], plan, tm, nt) + [r[...].astype(F32) for r in refs[nxr:nxr + npar]]
        outs = f(*vals)
        _store_groups(refs[nxr + npar:], out_groups, outs)

    res, carried = _hosting_call(
        body, comm, name=name, grid=(nt,), in_specs=list(xspecs) + list(pspecs),
        out_specs=[pl.BlockSpec((tm, w), lambda i: (i, 0)) for w in widths],
        out_shape=[SDS((t, w), dt) for w, dt in zip(widths, out_dtypes)], scratch_shapes=[], args=(*xa, *pa))
    return res if comm is None else (res, carried)


def _rowwise_bwd(f, xs, params, cots, *, x_grad, p_grad, dx_groups, dx_dtypes, tm, name, extra=None, comm=None,
                 fold_next=None):
    tm, t = _tile_rows(xs, tm)
    nt = t // tm
    tile_of = (lambda i: nt - 1 - i) if fold_next else (lambda i: i)
    xa, xspecs, plan = _x_plan(xs, tm, t, tile_of)
    pa, pspecs = (zip(*[_par_spec(p) for p in params]) if params else ((), ()))
    ca, cspecs = zip(*[_row_spec(c, tm, tile_of) for c in cots])
    extra = extra or {}
    ekeys = sorted(extra)
    ea, especs = (zip(*[_row_spec(extra[k], tm, tile_of) for k in ekeys]) if ekeys else ((), ()))
    nx, nxr, npar, nc, ne = len(plan), len(xa), len(pa), len(ca), len(ea)
    gx = [i for i in range(nx) if x_grad[i]]
    gp = [i for i in range(npar) if p_grad[i]]
    all_widths = [sum(plan[gx[ix]][2] for ix in idxs) for idxs in dx_groups]
    emitted = [k for k in range(len(dx_groups)) if not (fold_next and k == fold_next[1])]
    widths = [all_widths[k] for k in emitted]
    ng = len(emitted)

    def body(*refs):
        ins = refs[:nxr + npar + nc + ne]
        outs = refs[nxr + npar + nc + ne:]
        vals = _x_vals(ins[:nxr], plan, tm, nt, tile_of) + [r[...].astype(F32) for r in ins[nxr:nxr + npar]]
        cvals = tuple(r[...].astype(F32) for r in ins[nxr + npar:nxr + npar + nc])
        evals = [r[...].astype(F32) for r in ins[nxr + npar + nc:]]
        diff_idx = gx + [nx + i for i in gp]

        def g(*dargs):
            full = list(vals)
            for ix, v in zip(diff_idx, dargs):
                full[ix] = v
            return tuple(f(*full))

        _, vjp = jax.vjp(g, *[vals[ix] for ix in diff_idx])
        grads = vjp(cvals)
        dxs = list(grads[:len(gx)])
        for k, ev in zip(ekeys, evals):
            dxs[k] = dxs[k] + ev
        _store_groups(outs[:ng], [dx_groups[k] for k in emitted], dxs)
        i = pl.program_id(0)
        if fold_next:
            main_ref, carry_ref = outs[emitted.index(fold_next[0])], refs[-1]
            rows = lax.broadcasted_iota(jnp.int32, (tm, 1), 0)
            off = 0
            for ix in dx_groups[fold_next[1]]:
                piece = dxs[ix]
                cols = slice(off, off + piece.shape[1])
                edge = jnp.where(i == 0, 0.0, carry_ref[0:1, cols])
                main_ref[:, cols] += jnp.where(rows == tm - 1, edge, pltpu.roll(piece, tm - 1, 0))
                carry_ref[:, cols] = piece[:SUBLANES]
                off += piece.shape[1]
        for ref, gval in zip(outs[ng:ng + len(gp)], grads[len(gx):]):
            @pl.when(i == 0)
            def _(ref=ref):
                ref[...] = jnp.zeros_like(ref)
            ref[...] += gval

    dp_specs = [pl.BlockSpec(pspecs[i].block_shape, lambda i: (0, 0)) for i in gp]
    dp_shapes = [SDS(pspecs[i].block_shape, F32) for i in gp]
    scratch = [pltpu.VMEM((SUBLANES, all_widths[fold_next[1]]), F32)] if fold_next else []
    res, carried = _hosting_call(
        body, comm, name=name, grid=(nt,), in_specs=list(xspecs) + list(pspecs) + list(cspecs) + list(especs),
        out_specs=[pl.BlockSpec((tm, w), lambda i: (tile_of(i), 0)) for w in widths] + dp_specs,
        out_shape=[SDS((t, w), dt) for w, dt in zip(widths, dx_dtypes)] + dp_shapes, scratch_shapes=scratch,
        args=(*xa, *pa, *ca, *ea))
    return res if comm is None else (res, carried)


def _rms_f(x, g):
    return (x * lax.rsqrt(jnp.mean(x * x, axis=-1, keepdims=True) + NORM_EPS) * g,)


def _group_sum_impl(x, ones_bd):
    p1, p2 = _split2(x)
    dot = lambda p: lax.dot_general(p, ones_bd.astype(BF16), (((1,), (0,)), ((), ())), preferred_element_type=F32)
    return dot(p1) + dot(p2)


@jax.custom_vjp
def _group_sum(x, ones_bd):
    return _group_sum_impl(x, ones_bd)


_group_sum.defvjp(lambda x, o: (_group_sum_impl(x, o), o),
                  lambda o, g: (_group_sum_impl(g, o), jnp.zeros_like(o)))


def _rwkv_prep_f(r, k, v, lo, rp, kp, vp, lop, mu_r, mu_k, mu_v, mu_lo, w0, w2p, a0, a2p, g2p, k_k, k_a, ones_bd):
    r = r + mu_r * (rp - r)
    k = k + mu_k * (kp - k)
    v = v + mu_v * (vp - v)
    lo = lo + mu_lo * (lop - lo)
    w_log = -_softplus(-(w0 + _nn(jnp.tanh(lo), w2p))) - 0.5
    lw = -jnp.exp(w_log)
    a_g = _sigmoid(a0 + _nn(lo, a2p))
    g = _nn(_sigmoid(lo), g2p)
    kk = k * k_k
    kk = kk / jnp.maximum(jnp.sqrt(_group_sum(kk * kk, ones_bd)), L2_EPS)
    k2 = k * (1.0 + (a_g - 1.0) * k_a)
    return r, lw, k2, v, -kk, kk * a_g, g


def _rwkv_post_f(y, r, k2, v, g, r_k, gn_w, gn_b, ones_bd):
    inv_n = 1.0 / HB_DIM
    mean = _group_sum(y, ones_bd) * inv_n
    yc = y - mean
    var = _group_sum(yc * yc, ones_bd) * inv_n
    yn = yc * lax.rsqrt(var + RWKV_GN_EPS) * gn_w + gn_b
    bonus = _group_sum(r * k2 * r_k, ones_bd) * v
    return ((yn + bonus) * g,)


def _tri(c, strict=False):
    ii = lax.broadcasted_iota(jnp.int32, (c, c), 0)
    jj = lax.broadcasted_iota(jnp.int32, (c, c), 1)
    return (jj < ii) if strict else (jj <= ii)


def _hgrn_step(st0, q_a, f_a, i_a, g_a, l0, l1, onorm):
    nh, nj = len(q_a), len(q_a[0])
    c = q_a[0][0].shape[0]
    combos = [(j, h) for j in range(nj) for h in range(nh)]
    every = lambda fn: {q: fn(q) for q in combos}
    at_ = lambda d: (lambda q: d[q[1]][q[0]])
    qa_, fa_, ia_, ga_ = (at_(z) for z in (q_a, f_a, i_a, g_a))
    incl = _tri(c)
    rows = lax.broadcasted_iota(jnp.int32, (c, 1), 0)
    lb = []
    for h in range(nh):
        mx = jnp.maximum(l0[h], l1[h])
        e0, e1 = jnp.exp(l0[h] - mx), jnp.exp(l1[h] - mx)
        lb.append(e0 / (e0 + e1))
    forget = every(lambda q: lb[q[1]] + (1.0 - lb[q[1]]) * _sigmoid(fa_(q)))
    qs = every(lambda q: _silu(qa_(q)))
    kk = every(lambda q: 1.0 - forget[q])
    lf = every(lambda q: jnp.log(forget[q]))
    bcum = every(lambda q: _cumsum_rows(lf[q]))
    bref = every(lambda q: jnp.sum(jnp.where(rows <= c // 2, lf[q], 0.0), axis=0, keepdims=True))
    blast = every(lambda q: jnp.sum(lf[q], axis=0, keepdims=True))
    scores = every(lambda q: jnp.where(incl, _nt(qs[q] * jnp.exp(bcum[q] - bref[q]),
                                                 kk[q] * jnp.exp(bref[q] - bcum[q])), 0.0))
    intra = every(lambda q: _nn(scores[q], ia_(q)))
    qb = every(lambda q: qs[q] * jnp.exp(bcum[q]))
    upd = every(lambda q: _tn(ia_(q), kk[q] * jnp.exp(blast[q] - bcum[q])))
    dec = every(lambda q: jnp.exp(blast[q]))
    st = list(st0)
    o = {}
    for j in range(nj):
        for h in range(nh):
            o[(j, h)] = intra[(j, h)] + _nt(qb[(j, h)], st[h])
        st = [st[h] * dec[(j, h)] + upd[(j, h)] for h in range(nh)]
    out = every(lambda q: o[q] * lax.rsqrt(jnp.mean(o[q] * o[q], axis=-1, keepdims=True) + NORM_EPS)
                * onorm[q[1]] * _silu(ga_(q)))
    return [[out[(j, h)] for j in range(nj)] for h in range(nh)], st


def _hgrn_blocks(ref, nj, c):
    return [[ref[j * c:(j + 1) * c, h * HA_DIM:(h + 1) * HA_DIM] for j in range(nj)] for h in range(HA_HEADS)]


def _hgrn_cols(ref):
    return [ref[:, h * HA_DIM:(h + 1) * HA_DIM] for h in range(HA_HEADS)]


def _hgrn_fwd(p_h, l0, l1, onorm):
    t = p_h.shape[0]
    cc, nj = HGRN_CHUNK, HGRN_GROUP
    c = cc * nj
    n = t // c

    def body(q_ref, f_ref, i_ref, g_ref, l0_ref, l1_ref, on_ref, o_ref, hs_ref, st_ref):
        @pl.when(pl.program_id(0) == 0)
        def _():
            st_ref[...] = jnp.zeros_like(st_ref)

        hs_ref[0] = st_ref[...]
        o, st1 = _hgrn_step([st_ref[h] for h in range(HA_HEADS)],
                            *[_hgrn_blocks(ref, nj, cc) for ref in (q_ref, f_ref, i_ref, g_ref)],
                            _hgrn_cols(l0_ref), _hgrn_cols(l1_ref), _hgrn_cols(on_ref))
        for h in range(HA_HEADS):
            for j in range(nj):
                o_ref[j * cc:(j + 1) * cc, h * HA_DIM:(h + 1) * HA_DIM] = o[h][j]
            st_ref[h] = st1[h]

    col = lambda j: pl.BlockSpec((c, W_A), lambda i, j=j: (i, j))
    par = pl.BlockSpec((1, W_A), lambda i: (0, 0))
    return pl.pallas_call(
        body, name="hgrn_fwd", grid=(n,), in_specs=[col(0), col(1), col(2), col(3), par, par, par],
        out_specs=[pl.BlockSpec((c, W_A), lambda i: (i, 0)),
                   pl.BlockSpec((1, HA_HEADS, HA_DIM, HA_DIM), lambda i: (i, 0, 0, 0))],
        out_shape=[SDS((t, W_A), F32), SDS((n, HA_HEADS, HA_DIM, HA_DIM), F32)],
        scratch_shapes=[pltpu.VMEM((HA_HEADS, HA_DIM, HA_DIM), F32)],
        compiler_params=_params(("arbitrary",)))(p_h, p_h, p_h, p_h, l0, l1, onorm)


def _hgrn_bwd(p_h, l0, l1, onorm, hs, do, do_col, comm=None):
    t = p_h.shape[0]
    cc, nj = HGRN_CHUNK, HGRN_GROUP
    c = cc * nj
    n = t // c

    def body(q_ref, f_ref, i_ref, g_ref, l0_ref, l1_ref, on_ref, hs_ref, do_ref,
             dp_ref, dl0_ref, dl1_ref, don_ref, dst_ref):
        @pl.when(pl.program_id(0) == 0)
        def _():
            dst_ref[...] = jnp.zeros_like(dst_ref)
            dl0_ref[...] = jnp.zeros_like(dl0_ref)
            dl1_ref[...] = jnp.zeros_like(dl1_ref)
            don_ref[...] = jnp.zeros_like(don_ref)

        args = ([hs_ref[0, h] for h in range(HA_HEADS)],
                *[_hgrn_blocks(ref, nj, cc) for ref in (q_ref, f_ref, i_ref, g_ref)],
                _hgrn_cols(l0_ref), _hgrn_cols(l1_ref), _hgrn_cols(on_ref))
        _, vjp = jax.vjp(_hgrn_step, *args)
        dst0, dq, df, di, dg, dl0, dl1, don = vjp((_hgrn_blocks(do_ref, nj, cc),
                                                   [dst_ref[h] for h in range(HA_HEADS)]))
        for h in range(HA_HEADS):
            sl = slice(h * HA_DIM, (h + 1) * HA_DIM)
            for k, dv in enumerate((dq, df, di, dg)):
                for j in range(nj):
                    dp_ref[j * cc:(j + 1) * cc, k * W_A + h * HA_DIM:k * W_A + (h + 1) * HA_DIM] = dv[h][j]
            dl0_ref[:, sl] += dl0[h]
            dl1_ref[:, sl] += dl1[h]
            don_ref[:, sl] += don[h]
            dst_ref[h] = dst0[h]

    col = lambda j: pl.BlockSpec((c, W_A), lambda i, j=j: (n - 1 - i, j))
    par = pl.BlockSpec((1, W_A), lambda i: (0, 0))
    return _hosting_call(
        body, comm, name="hgrn_bwd", grid=(n,),
        in_specs=[col(0), col(1), col(2), col(3), par, par, par,
                  pl.BlockSpec((1, HA_HEADS, HA_DIM, HA_DIM), lambda i: (n - 1 - i, 0, 0, 0)),
                  pl.BlockSpec((c, W_A), lambda i: (n - 1 - i, do_col))],
        out_specs=[pl.BlockSpec((c, N_HGRN_COLS), lambda i: (n - 1 - i, 0)), par, par, par],
        out_shape=[SDS((t, N_HGRN_COLS), F32), SDS((1, W_A), F32), SDS((1, W_A), F32), SDS((1, W_A), F32)],
        scratch_shapes=[pltpu.VMEM((HA_HEADS, HA_DIM, HA_DIM), F32)],
        args=(p_h, p_h, p_h, p_h, l0, l1, onorm, hs, do))


HB_PAIRS = HB_HEADS // 2
PAIR_W = 2 * HB_DIM


def _head_lane_masks():
    lane = lax.broadcasted_iota(jnp.int32, (1, PAIR_W), 1)
    return (lane < HB_DIM).astype(F32), (lane >= HB_DIM).astype(F32)


@jax.custom_vjp
def _stack_heads(x):
    m0, m1 = _head_lane_masks()
    return jnp.concatenate([x * m0, x * m1], axis=0)


def _stack_heads_bwd(_, g):
    m0, m1 = _head_lane_masks()
    c = g.shape[0] // 2
    return (g[:c] * m0 + g[c:] * m1,)


_stack_heads.defvjp(lambda x: (_stack_heads(x), None), _stack_heads_bwd)


@jax.custom_vjp
def _unstack_heads(ys):
    c = ys.shape[0] // 2
    return ys[:c] + ys[c:]


_unstack_heads.defvjp(lambda ys: (_unstack_heads(ys), None), lambda _, g: (_stack_heads(g),))


def _same_head_block(c):
    ii = lax.broadcasted_iota(jnp.int32, (2 * c, 2 * c), 0)
    jj = lax.broadcasted_iota(jnp.int32, (2 * c, 2 * c), 1)
    same = (ii < c) == (jj < c)
    return same & (jj <= ii), same & (jj < ii), (ii == jj).astype(F32)


@jax.custom_vjp
def _rows_join(top, bottom):
    return jnp.concatenate([top, bottom], axis=0)


def _rows_join_bwd(n_top, g):
    return g[:n_top], g[n_top:]


_rows_join.defvjp(lambda top, bottom: (_rows_join(top, bottom), top.shape[0]), _rows_join_bwd)


def _rows_split_impl(x, n_top):
    return x[:n_top], x[n_top:]


_rows_split = jax.custom_vjp(_rows_split_impl, nondiff_argnums=(1,))
_rows_split.defvjp(lambda x, n_top: (_rows_split_impl(x, n_top), None),
                   lambda n_top, _, g: (jnp.concatenate([g[0], g[1]], axis=0),))


def _rwkv_step(s0, r, lw, k, v, a, b):
    npair, nj = len(r), len(r[0])
    c = r[0][0].shape[0]
    combos = [(j, p) for j in range(nj) for p in range(npair)]
    every = lambda fn: {q: fn(q) for q in combos}
    at_ = lambda d: (lambda q: d[q[1]][q[0]])
    r_, lw_, k_, v_, a_, b_ = (at_(z) for z in (r, lw, k, v, a, b))
    incl, strict, eye = _same_head_block(c)

    gam = every(lambda q: _cumsum_rows(lw_(q)))
    gtot = every(lambda q: jnp.sum(lw_(q), axis=0, keepdims=True))
    eneg = every(lambda q: jnp.exp(-gam[q]))
    edec = every(lambda q: jnp.exp(gtot[q] - gam[q]))
    at = every(lambda q: _stack_heads(a_(q) * jnp.exp(gam[q] - lw_(q))))
    rt = every(lambda q: _stack_heads(r_(q) * jnp.exp(gam[q])))
    bt = every(lambda q: _stack_heads(b_(q) * eneg[q]))
    kt = every(lambda q: _stack_heads(k_(q) * eneg[q]))
    bdec = every(lambda q: _stack_heads(b_(q) * edec[q]))
    kdec = every(lambda q: _stack_heads(k_(q) * edec[q]))
    vs = every(lambda q: _stack_heads(v_(q)))
    a_ab = every(lambda q: jnp.where(strict, _nt(at[q], bt[q]), 0.0))
    a_ak = every(lambda q: jnp.where(strict, _nt(at[q], kt[q]), 0.0))
    a_rb = every(lambda q: jnp.where(incl, _nt(rt[q], bt[q]), 0.0))
    a_rk = every(lambda q: jnp.where(incl, _nt(rt[q], kt[q]), 0.0))
    tinv = every(lambda q: eye + a_ab[q])
    pw = a_ab
    span = 2
    while span < c:
        pw = every(lambda q, pw=pw: _nn_x3(pw[q], pw[q]))
        tinv = every(lambda q, pw=pw, tinv=tinv: tinv[q] + _nn_x3(pw[q], tinv[q]))
        span *= 2
    akv = every(lambda q: _nn(a_ak[q], vs[q]))
    w1 = every(lambda q: _nn(tinv[q], at[q]))
    u0 = every(lambda q: _nn(tinv[q], akv[q]))
    wr = every(lambda q: _rows_join(w1[q], rt[q]))
    bk = every(lambda q: _rows_join(bdec[q], kdec[q]))
    yv = every(lambda q: _nn(a_rk[q], vs[q]))
    gdec = every(lambda q: jnp.exp(gtot[q]))

    s = list(s0)
    y = [[None] * nj for _ in range(npair)]
    for j in range(nj):
        both = {p: _rows_split(_nt(wr[(j, p)], s[p]), 2 * c) for p in range(npair)}
        u = {p: both[p][0] + u0[(j, p)] for p in range(npair)}
        for p in range(npair):
            y[p][j] = _unstack_heads(both[p][1] + _nn(a_rb[(j, p)], u[p]) + yv[(j, p)])
        s = [s[p] * gdec[(j, p)] + _tn(_rows_join(u[p], vs[(j, p)]), bk[(j, p)]) for p in range(npair)]
    return y, s


def _rwkv_blocks(ref, nj, c):
    return [[ref[j * c:(j + 1) * c, p * PAIR_W:(p + 1) * PAIR_W] for j in range(nj)] for p in range(HB_PAIRS)]


def _rwkv_fwd(seqs, comm=None):
    t = seqs[0].shape[0]
    c, nj = RWKV_CHUNK, RWKV_GROUP
    n = t // (c * nj)

    def body(r_ref, lw_ref, k_ref, v_ref, a_ref, b_ref, y_ref, hs_ref, st_ref):
        @pl.when(pl.program_id(0) == 0)
        def _():
            st_ref[...] = jnp.zeros_like(st_ref)

        hs_ref[0] = st_ref[...]
        s0 = [st_ref[p] for p in range(HB_PAIRS)]
        y, s1 = _rwkv_step(s0, *[_rwkv_blocks(ref, nj, c) for ref in (r_ref, lw_ref, k_ref, v_ref, a_ref, b_ref)])
        for p in range(HB_PAIRS):
            for j in range(nj):
                y_ref[j * c:(j + 1) * c, p * PAIR_W:(p + 1) * PAIR_W] = y[p][j]
            st_ref[p] = s1[p]

    seq = pl.BlockSpec((c * nj, W_B), lambda i: (i, 0))
    return _hosting_call(
        body, comm, name="rwkv_fwd", grid=(n,), in_specs=[seq] * 6,
        out_specs=[seq, pl.BlockSpec((1, HB_PAIRS, PAIR_W, PAIR_W), lambda i: (i, 0, 0, 0))],
        out_shape=[SDS((t, W_B), F32), SDS((n, HB_PAIRS, PAIR_W, PAIR_W), F32)],
        scratch_shapes=[pltpu.VMEM((HB_PAIRS, PAIR_W, PAIR_W), F32)], args=tuple(seqs))


def _rwkv_bwd(seqs, hs, dy, comm=None):
    t = seqs[0].shape[0]
    c, nj = RWKV_CHUNK, RWKV_GROUP
    n = t // (c * nj)

    def body(r_ref, lw_ref, k_ref, v_ref, a_ref, b_ref, hs_ref, dy_ref,
             dr_ref, dlw_ref, dk_ref, dv_ref, da_ref, db_ref, dst_ref):
        @pl.when(pl.program_id(0) == 0)
        def _():
            dst_ref[...] = jnp.zeros_like(dst_ref)

        s0 = [hs_ref[0, p] for p in range(HB_PAIRS)]
        seq_vals = [_rwkv_blocks(ref, nj, c) for ref in (r_ref, lw_ref, k_ref, v_ref, a_ref, b_ref)]
        _, vjp = jax.vjp(_rwkv_step, s0, *seq_vals)
        grads = vjp((_rwkv_blocks(dy_ref, nj, c), [dst_ref[p] for p in range(HB_PAIRS)]))
        for ref, gr in zip((dr_ref, dlw_ref, dk_ref, dv_ref, da_ref, db_ref), grads[1:]):
            for p in range(HB_PAIRS):
                for j in range(nj):
                    ref[j * c:(j + 1) * c, p * PAIR_W:(p + 1) * PAIR_W] = gr[p][j]
        m0, m1 = _head_lane_masks()
        rows0 = (lax.broadcasted_iota(jnp.int32, (PAIR_W, 1), 0) < HB_DIM).astype(F32)
        blocks = rows0 * m0 + (1.0 - rows0) * m1
        for p in range(HB_PAIRS):
            dst_ref[p] = grads[0][p] * blocks

    seq = pl.BlockSpec((c * nj, W_B), lambda i: (n - 1 - i, 0))
    return _hosting_call(
        body, comm, name="rwkv_bwd", grid=(n,),
        in_specs=[seq] * 6 + [pl.BlockSpec((1, HB_PAIRS, PAIR_W, PAIR_W), lambda i: (n - 1 - i, 0, 0, 0)), seq],
        out_specs=[seq] * 6, out_shape=[SDS((t, W_B), F32)] * 6,
        scratch_shapes=[pltpu.VMEM((HB_PAIRS, PAIR_W, PAIR_W), F32)], args=(*seqs, hs, dy))


def _final_loss(x3, fnorm, target, *, tm):
    t, d = x3.shape

    def body(x_ref, g_ref, t_ref, dx_ref, dg_ref, loss_ref):
        @pl.when(pl.program_id(0) == 0)
        def _():
            dg_ref[...] = jnp.zeros_like(dg_ref)
            loss_ref[...] = jnp.zeros_like(loss_ref)

        x, g = x_ref[...], g_ref[...]
        rinv = lax.rsqrt(jnp.mean(x * x, axis=-1, keepdims=True) + NORM_EPS)
        xh = x * rinv
        diff = xh * g - t_ref[...]
        loss_ref[...] += 0.5 * jnp.sum(jnp.mean(diff * diff, axis=-1, keepdims=True))
        dy = diff * (1.0 / d)
        dg_ref[...] += jnp.sum(dy * xh, axis=0, keepdims=True)
        dxh = dy * g
        dx_ref[...] = rinv * (dxh - xh * jnp.mean(dxh * xh, axis=-1, keepdims=True))

    row = pl.BlockSpec((tm, d), lambda i: (i, 0))
    return pl.pallas_call(
        body, name="final_loss", grid=(t // tm,), in_specs=[row, pl.BlockSpec((1, d), lambda i: (0, 0)), row],
        out_specs=[row, pl.BlockSpec((1, d), lambda i: (0, 0)), pl.BlockSpec((8, 128), lambda i: (0, 0))],
        out_shape=[SDS((t, d), F32), SDS((1, d), F32), SDS((8, 128), F32)],
        compiler_params=_params(("arbitrary",)))(x3, fnorm, target)


def _gate_up_act(h, wgt, wut, *, tm, tn, name, comm=None):
    t, d = h.shape
    tm = min(tm, t)

    def body(h_ref, g_ref, u_ref, a_out, u_out, act_out):
        hv = h_ref[...]
        a = _dg(hv, g_ref[...], 1, 1, False)
        u = _dg(hv, u_ref[...], 1, 1, False)
        a_out[...] = a.astype(a_out.dtype)
        u_out[...] = u.astype(u_out.dtype)
        act_out[...] = (_silu(a) * u).astype(act_out.dtype)

    wspec = pl.BlockSpec((tn, d), lambda i, j: (j, 0))
    ospec = pl.BlockSpec((tm, tn), lambda i, j: (i, j))
    return _hosting_call(
        body, comm, name=name, grid=(t // tm, D_FF // tn),
        in_specs=[pl.BlockSpec((tm, d), lambda i, j: (i, 0)), wspec, wspec], out_specs=[ospec, ospec, ospec],
        out_shape=[SDS((t, D_FF), BF16), SDS((t, D_FF), BF16), SDS((t, D_FF), BF16)], scratch_shapes=[],
        args=(h, wgt, wut))


def _dact_swiglu(dout, wd, a, u, *, tm, tn, name, comm=None):
    t, d = dout.shape
    tm = min(tm, t)

    def body(d_ref, w_ref, a_ref, u_ref, da_out, du_out):
        dact = 0.5 * _dg(d_ref[...], w_ref[...], 1, 1, False)
        av, uv = a_ref[...].astype(F32), u_ref[...].astype(F32)
        s = _sigmoid(av)
        da_out[...] = (dact * uv * (s * (1.0 + av * (1.0 - s)))).astype(da_out.dtype)
        du_out[...] = (dact * (av * s)).astype(du_out.dtype)

    tile = pl.BlockSpec((tm, tn), lambda i, j: (i, j))
    return _hosting_call(
        body, comm, name=name, grid=(t // tm, D_FF // tn),
        in_specs=[pl.BlockSpec((tm, d), lambda i, j: (i, 0)), pl.BlockSpec((tn, d), lambda i, j: (j, 0)), tile, tile],
        out_specs=[tile, tile], out_shape=[SDS((t, D_FF), BF16), SDS((t, D_FF), BF16)], scratch_shapes=[],
        args=(dout, wd, a, u))


class _Plan:
    def __init__(self):
        self.entries, self.counts = collections.defaultdict(list), {}

    def carry(self, host, comm_of, after):
        self.entries[host].append((comm_of, after))

    def comm(self, host, g):
        comms = [comm_of(g) for comm_of, _ in self.entries.get(host, [])]
        self.counts[host] = [len(c.arrays) for c in comms]
        return functools.reduce(_join_comms, comms) if comms else None

    def done(self, host, results, w):
        start = 0
        for (_, after), n in zip(self.entries.get(host, []), self.counts.get(host, [])):
            after(results[start:start + n], w)
            start += n


def _ffn_fwd(x, w, tag, plan, g):
    comm = plan.comm(f"{tag}_rms", g)
    res = _rowwise(_rms_f, [x], [w[f"{tag}_norm"]], [[0]], [BF16], tm=512, name=f"{tag}_rms", comm=comm)
    (h,), carried = res if comm is not None else (res, [])
    plan.done(f"{tag}_rms", carried, w)
    (a, u, act), carried = _gate_up_act(h, w[f"{tag}_wgt"], w[f"{tag}_wut"], tm=2048, tn=256, name=f"{tag}_gate_up",
                                        comm=plan.comm(f"{tag}_gate_up", g))
    plan.done(f"{tag}_gate_up", carried, w)
    comm = plan.comm(f"{tag}_down", g)
    out = _mm(act, w[f"{tag}_wd"], tm=1024, tn=D_MODEL, tk=D_FF, name=f"{tag}_down", res=x, scale=0.5, comm=comm)
    if comm is not None:
        out, carried = out
        plan.done(f"{tag}_down", carried, w)
    return out, (h, a, u, act)


def _ffn_bwd(dout, x, w, saved, tag, plan, g):
    h, a, u, act = saved

    def carrying(fn, host, *args, **kwargs):
        comm = plan.comm(host, g)
        res = fn(*args, name=host, comm=comm, **kwargs)
        out, carried = res if comm is not None else (res, [])
        plan.done(host, carried, w)
        return out

    (da, du), carried = _dact_swiglu(dout, w[f"{tag}_wd"], a, u, tm=2048, tn=256, name=f"{tag}_dact",
                                     comm=plan.comm(f"{tag}_dact", g))
    plan.done(f"{tag}_dact", carried, w)
    g[f"{tag}_wd"] = _mm(act, dout, ta=True, tm=D_FF // 2, tn=D_MODEL, tk=1024, name=f"{tag}_dwd", scale=0.5)
    g[f"{tag}_wgt"] = carrying(_mm, f"{tag}_dwg", da, h, ta=True, tm=D_FF // 2, tn=D_MODEL, tk=1024)
    g[f"{tag}_wut"] = carrying(_mm, f"{tag}_dwu", du, h, ta=True, tm=D_FF // 2, tn=D_MODEL, tk=1024)
    dh = carrying(_mm, f"{tag}_dh_g", da, w[f"{tag}_wgt"], tm=1024, tn=D_MODEL, tk=D_FF)
    dh = carrying(_mm, f"{tag}_dh_u", du, w[f"{tag}_wut"], tm=1024, tn=D_MODEL, tk=D_FF, res=dh)
    dx, g[f"{tag}_norm"] = _rowwise_bwd(_rms_f, [x], [w[f"{tag}_norm"]], [dh], x_grad=[True], p_grad=[True],
                                        dx_groups=[[0]], dx_dtypes=[F32], tm=512, name=f"{tag}_drms",
                                        extra={0: dout})
    return dx


def _local_step(x, target, w, plan=None):
    plan = plan or _Plan()
    ones_bd = jnp.kron(jnp.eye(HB_HEADS, dtype=F32), jnp.ones((HB_DIM, HB_DIM), F32))
    g = {}
    x1, ffn1_saved = _ffn_fwd(x, w, "ffn1", plan, g)
    hm, = _rowwise(_rms_f, [x1], [w["mix_norm"]], [[0]], [BF16], tm=512, name="mix_rms")
    p_h = _mm(hm, w["w_in_h"], tm=2048, tn=256, tk=D_MODEL, name="inproj_h")
    p_r = _mm(hm, w["w_in_r"], tm=2048, tn=256, tk=D_MODEL, name="inproj_r")
    o_a, hgrn_states = _hgrn_fwd(p_h, w["lb0"], w["lb1"], w["hgrn_out_norm"])

    mu = w["mu_pad"]
    prep_xs = [(p_r, W_B, 0), (p_r, W_B, 1), (p_r, W_B, 2), (p_r, LORA_PAD, 6),
               ("prev", p_r, W_B, 0), ("prev", p_r, W_B, 1), ("prev", p_r, W_B, 2), ("prev", p_r, LORA_PAD, 6)]
    prep_ps = [(mu, W_B, 0), (mu, W_B, 1), (mu, W_B, 2), (mu, LORA_PAD, 6), w["rwkv_w0"], w["w2_pad"], w["rwkv_a0"],
               w["a2_pad"], w["g2_pad"], w["rwkv_k_k"], w["rwkv_k_a"], ones_bd]
    prep_f = _rwkv_prep_f
    r, lw, k2, v, a_vec, b_vec, gate = _rowwise(prep_f, prep_xs, prep_ps, [[0], [1], [2], [3], [4], [5], [6]],
                                                [F32] * 7, tm=256, name="rwkv_prep")
    seqs = [r, lw, k2, v, a_vec, b_vec]
    (y, rwkv_states), carried = _rwkv_fwd(seqs, comm=plan.comm("rwkv_fwd", g))
    plan.done("rwkv_fwd", carried, w)
    post_f = _rwkv_post_f
    post_xs = [y, r, k2, v, gate]
    post_ps = [w["rwkv_r_k"], w["rwkv_gn_w"], w["rwkv_gn_b"], ones_bd]
    o, = _rowwise(lambda o_a_, *rest: (o_a_,) + tuple(post_f(*rest)), [o_a] + post_xs, post_ps, [[0, 1]], [F32],
                  tm=256, name="rwkv_post")
    x2 = _mm(o, w["w_out"], tm=2048, tn=256, tk=D_MODEL, name="outproj", res=x1)
    x3, ffn2_saved = _ffn_fwd(x2, w, "ffn2", plan, g)
    dx3, g["final_norm"], loss = _final_loss(x3, w["final_norm"], target, tm=256)

    dx2 = _ffn_bwd(dx3, x2, w, ffn2_saved, "ffn2", plan, g)
    do = _mm(dx2, w["w_out"], tb=True, tm=2048, tn=256, tk=D_MODEL, name="outproj_do")
    g["w_out"] = _mm(o, dx2, ta=True, tm=D_MODEL, tn=D_MODEL, tk=1024, name="outproj_dw")

    (dp_h, g["lb0"], g["lb1"], g["hgrn_out_norm"]), carried = _hgrn_bwd(
        p_h, w["lb0"], w["lb1"], w["hgrn_out_norm"], hgrn_states, do, 0, comm=plan.comm("hgrn_bwd", g))
    plan.done("hgrn_bwd", carried, w)
    post_out = _rowwise_bwd(post_f, post_xs, post_ps, [(do, W_B, 1)], x_grad=[True] * 5, p_grad=[True] * 3 + [False],
                            dx_groups=[[0], [1], [2], [3], [4]], dx_dtypes=[F32] * 5, tm=256, name="rwkv_post_bwd")
    dy, dr1, dk1, dv1, dgate, g["rwkv_r_k"], g["rwkv_gn_w"], g["rwkv_gn_b"] = post_out
    (dr2, dlw, dk2, dv2, da_vec, db_vec), carried = _rwkv_bwd(seqs, rwkv_states, dy, comm=plan.comm("rwkv_bwd", g))
    plan.done("rwkv_bwd", carried, w)

    def prep2_f(*vals):
        r_, lw_, k2_, v_, a_, b_, g_ = prep_f(*vals)
        return r_, lw_, k2_, v_, a_, b_, g_, r_, k2_, v_

    prep_out = _rowwise_bwd(prep2_f, prep_xs, prep_ps, [dr2, dlw, dk2, dv2, da_vec, db_vec, dgate, dr1, dk1, dv1],
                            x_grad=[True] * 8, p_grad=[True] * 11 + [False], dx_groups=[[0, 1, 2, 3], [4, 5, 6, 7]],
                            dx_dtypes=[F32], tm=256, name="rwkv_prep_bwd", fold_next=(0, 1))
    dp_r = prep_out[0]
    (dmu_r, dmu_k, dmu_v, dmu_lo, g["rwkv_w0"], g["w2_pad"], g["rwkv_a0"], g["a2_pad"], g["g2_pad"],
     g["rwkv_k_k"], g["rwkv_k_a"]) = prep_out[1:]
    g["mu_pad"] = jnp.concatenate([dmu_r, dmu_k, dmu_v, dmu_lo], axis=1)
    dhm = _mm(dp_h, w["w_in_h"], tb=True, tm=1024, tn=D_MODEL, tk=N_HGRN_COLS, name="inproj_dh_h")
    dhm = _mm(dp_r, w["w_in_r"], tb=True, tm=1024, tn=D_MODEL, tk=N_RWKV_PAD, name="inproj_dh_r", res=dhm)
    g["w_in_h"] = _mm(hm, dp_h, ta=True, tm=D_MODEL, tn=D_MODEL, tk=1024, name="inproj_dw_h")
    g["w_in_r"] = _mm(hm, dp_r, ta=True, tm=D_MODEL, tn=N_RWKV_PAD // 2, tk=1024, name="inproj_dw_r")
    mix_comm = plan.comm("mix_drms", g)
    mix_out = _rowwise_bwd(_rms_f, [x1], [w["mix_norm"]], [dhm], x_grad=[True], p_grad=[True], dx_groups=[[0]],
                           dx_dtypes=[F32], tm=512, name="mix_drms", extra={0: dx2}, comm=mix_comm)
    (dx1, g["mix_norm"]), carried = mix_out if mix_comm is not None else (mix_out, [])
    plan.done("mix_drms", carried, w)
    dx0 = _ffn_bwd(dx1, x, w, ffn1_saved, "ffn1", plan, g)
    return loss, dx0, g


HBM_SPEC = pl.BlockSpec(memory_space=pl.ANY)

Comm = collections.namedtuple("Comm", "arrays out_shapes aliased sem_shapes start finish")


def _join_comms(first, second):
    assert first.aliased == second.aliased
    n, s = len(first.arrays), len(first.sem_shapes)

    def start(ins, outs, sems):
        first.start(ins[:n], outs[:n], sems[:s])
        second.start(ins[n:], outs[n:], sems[s:])

    def finish(ins, outs, sems):
        first.finish(ins[:n], outs[:n], sems[:s])
        second.finish(ins[n:], outs[n:], sems[s:])

    return Comm(list(first.arrays) + list(second.arrays), list(first.out_shapes) + list(second.out_shapes),
                first.aliased, list(first.sem_shapes) + list(second.sem_shapes), start, finish)


def _run_comm(comm, name):
    n = len(comm.arrays)

    def body(*refs):
        ins, outs, sems = refs[:n], refs[n:2 * n], refs[2 * n:]
        comm.start(ins, outs, sems)
        comm.finish(ins, outs, sems)

    return pl.pallas_call(
        body, name=name, in_specs=[HBM_SPEC] * n, out_specs=[HBM_SPEC] * n, out_shape=list(comm.out_shapes),
        input_output_aliases={t: t for t in range(n)} if comm.aliased else {},
        scratch_shapes=list(comm.sem_shapes))(*comm.arrays)


def _hosting_call(body, comm, *, name, grid, in_specs, out_specs, out_shape, scratch_shapes, args):
    sem = ("arbitrary",) * len(grid)
    if comm is None:
        res = pl.pallas_call(body, name=name, grid=grid, in_specs=in_specs, out_specs=out_specs, out_shape=out_shape,
                             scratch_shapes=scratch_shapes, compiler_params=_params(sem))(*args)
        return list(res), []
    ni, no, ns, nc = len(in_specs), len(out_specs), len(scratch_shapes), len(comm.arrays)

    def wrapped(*refs):
        ins, cins = refs[:ni], refs[ni:ni + nc]
        outs, couts = refs[ni + nc:ni + nc + no], refs[ni + nc + no:ni + 2 * nc + no]
        scr, sems = refs[ni + 2 * nc + no:ni + 2 * nc + no + ns], refs[ni + 2 * nc + no + ns:]
        first = functools.reduce(jnp.logical_and, [pl.program_id(k) == 0 for k in range(len(grid))])
        last = functools.reduce(jnp.logical_and, [pl.program_id(k) == grid[k] - 1 for k in range(len(grid))])

        @pl.when(first)
        def _():
            comm.start(cins, couts, sems)

        body(*ins, *outs, *scr)

        @pl.when(last)
        def _():
            comm.finish(cins, couts, sems)

    res = pl.pallas_call(
        wrapped, name=name, grid=grid, in_specs=list(in_specs) + [HBM_SPEC] * nc,
        out_specs=list(out_specs) + [HBM_SPEC] * nc, out_shape=list(out_shape) + list(comm.out_shapes),
        scratch_shapes=list(scratch_shapes) + list(comm.sem_shapes),
        input_output_aliases={ni + t: no + t for t in range(nc)} if comm.aliased else {},
        compiler_params=_params(sem))(*args, *comm.arrays)
    return list(res[:no]), list(res[no:])


def _chips(x, y):
    return [(1 - x, y), (x, 1 - y), (1 - x, 1 - y)]


def _gather_comm(bufs):
    n = len(bufs)

    def copies(outs, sems):
        ici_send, ici_recv, d2d_send, d2d_recv = sems
        x, y, c = lax.axis_index("x"), lax.axis_index("y"), lax.axis_index("c")

        def half(t, slot, hc):
            hr = bufs[t].shape[1] // 2
            return outs[t].at[slot, pl.ds(pl.multiple_of(hc * hr, 16), hr), :]

        def ici(t, j, slot, px, py):
            return pltpu.make_async_remote_copy(src_ref=half(t, slot, c), dst_ref=half(t, slot, c),
                                                send_sem=ici_send.at[3 * t + j], recv_sem=ici_recv.at[3 * t + j],
                                                device_id=(px, py, c), device_id_type=MESH)

        def d2d(t, j, slot, hc):
            return pltpu.make_async_remote_copy(src_ref=half(t, slot, hc), dst_ref=half(t, slot, hc),
                                                send_sem=d2d_send.at[3 * t + j], recv_sem=d2d_recv.at[3 * t + j],
                                                device_id=(x, y, 1 - c), device_id_type=MESH)

        peers = [(t, j, px, py) for t in range(n) for j, (px, py) in enumerate(_chips(x, y))]
        return ici, d2d, peers, 2 * x + y, c

    def start(ins, outs, sems):
        ici, _, peers, me, _ = copies(outs, sems)
        for t, j, px, py in peers:
            ici(t, j, me, px, py).start()

    def finish(ins, outs, sems):
        ici, d2d, peers, me, c = copies(outs, sems)
        for t, j, px, py in peers:
            ici(t, j, 2 * px + py, px, py).wait_recv()
            d2d(t, j, 2 * px + py, c).start()
        for t, j, px, py in peers:
            d2d(t, j, 2 * px + py, 1 - c).wait_recv()
        for t, j, px, py in peers:
            ici(t, j, me, px, py).wait_send()
            d2d(t, j, 2 * px + py, c).wait_send()

    return Comm(list(bufs), [SDS(b.shape, b.dtype) for b in bufs], True, [pltpu.SemaphoreType.DMA((3 * n,))] * 4,
                start, finish)


def _sibling_exchange_comm(gs):
    n = len(gs)

    def copies(ins, outs, sems):
        x, y, c = lax.axis_index("x"), lax.axis_index("y"), lax.axis_index("c")
        cps = []
        for t in range(n):
            hr = gs[t].shape[1] // 2
            src = ins[t].at[:, pl.ds(pl.multiple_of((1 - c) * hr, SUBLANES), hr), :]
            cps.append(pltpu.make_async_remote_copy(src_ref=src, dst_ref=outs[t], send_sem=sems[0].at[t],
                                                    recv_sem=sems[1].at[t], device_id=(x, y, 1 - c),
                                                    device_id_type=MESH))
        return cps

    def start(ins, outs, sems):
        for cp in copies(ins, outs, sems):
            cp.start()

    def finish(ins, outs, sems):
        for cp in copies(ins, outs, sems):
            cp.wait()

    return Comm(list(gs), [SDS((N_CHIPS, g.shape[1] // 2, g.shape[2]), g.dtype) for g in gs], False,
                [pltpu.SemaphoreType.DMA((n,))] * 2, start, finish)


def _chip_exchange_comm(ss):
    n = len(ss)

    def copies(ins, outs, sems):
        x, y, c = lax.axis_index("x"), lax.axis_index("y"), lax.axis_index("c")
        me = 2 * x + y

        def copy(t, j, px, py, src_slot, dst_slot):
            return pltpu.make_async_remote_copy(src_ref=ins[t].at[src_slot], dst_ref=outs[t].at[dst_slot],
                                                send_sem=sems[0].at[3 * t + j], recv_sem=sems[1].at[3 * t + j],
                                                device_id=(px, py, c), device_id_type=MESH)

        peers = [(t, j, px, py) for t in range(n) for j, (px, py) in enumerate(_chips(x, y))]
        return copy, peers, me

    def start(ins, outs, sems):
        copy, peers, me = copies(ins, outs, sems)
        for t, j, px, py in peers:
            copy(t, j, px, py, 2 * px + py, me).start()

    def finish(ins, outs, sems):
        copy, peers, me = copies(ins, outs, sems)
        for t, j, px, py in peers:
            copy(t, j, px, py, me, 2 * px + py).wait_recv()
        for t, j, px, py in peers:
            copy(t, j, px, py, 2 * px + py, me).wait_send()

    return Comm(list(ss), [SDS(s.shape, s.dtype) for s in ss], False, [pltpu.SemaphoreType.DMA((3 * n,))] * 2,
                start, finish)


def _sibling_swap_comm(fs):
    n = len(fs)

    def copies(ins, outs, sems):
        x, y, c = lax.axis_index("x"), lax.axis_index("y"), lax.axis_index("c")
        return [pltpu.make_async_remote_copy(src_ref=ins[t], dst_ref=outs[t], send_sem=sems[0].at[t],
                                             recv_sem=sems[1].at[t], device_id=(x, y, 1 - c), device_id_type=MESH)
                for t in range(n)]

    def start(ins, outs, sems):
        for cp in copies(ins, outs, sems):
            cp.start()

    def finish(ins, outs, sems):
        for cp in copies(ins, outs, sems):
            cp.wait()

    return Comm(list(fs), [SDS(f.shape, f.dtype) for f in fs], False, [pltpu.SemaphoreType.DMA((n,))] * 2,
                start, finish)


def _row_tile(rows, cap=512):
    best = SUBLANES
    for tr in range(SUBLANES, min(rows, cap) + 1, SUBLANES):
        if rows % tr == 0:
            best = tr
    return best


def _add_halves(g4, r4, c_idx, name):
    _, hr, lanes = r4.shape
    tr = _row_tile(hr)
    nb = hr // tr

    def body(c_ref, a_ref, b_ref, o_ref):
        o_ref[...] = (a_ref[...] + b_ref[...]).astype(o_ref.dtype)

    grid_spec = pltpu.PrefetchScalarGridSpec(
        num_scalar_prefetch=1, grid=(N_CHIPS, nb),
        in_specs=[pl.BlockSpec((None, tr, lanes), lambda q, i, c_ref: (q, c_ref[0] * nb + i, 0)),
                  pl.BlockSpec((None, tr, lanes), lambda q, i, c_ref: (q, i, 0))],
        out_specs=pl.BlockSpec((None, tr, lanes), lambda q, i, c_ref: (q, i, 0)))
    return pl.pallas_call(body, name=name, grid_spec=grid_spec, out_shape=SDS(r4.shape, BF16),
                          compiler_params=_params(("parallel", "parallel")))(c_idx, g4, r4)


def _sum_chips(r4, s4, me_idx, name):
    _, rows, lanes = r4.shape
    tr = _row_tile(rows)

    def body(me_ref, a_ref, b_ref, c_ref, d_ref, own_ref, o_ref):
        own = own_ref[...].astype(F32)
        p = [jnp.where(me_ref[0] == q, own, ref[...].astype(F32)) for q, ref in enumerate((a_ref, b_ref, c_ref, d_ref))]
        o_ref[...] = ((p[0] + p[1]) + p[2]) + p[3]

    other = lambda q: (lambda i, me_ref: (jnp.where(me_ref[0] == q, (q + 1) % N_CHIPS, q), i, 0))
    grid_spec = pltpu.PrefetchScalarGridSpec(
        num_scalar_prefetch=1, grid=(rows // tr,),
        in_specs=[pl.BlockSpec((None, tr, lanes), other(q)) for q in range(N_CHIPS)]
        + [pl.BlockSpec((None, tr, lanes), lambda i, me_ref: (me_ref[0], i, 0))],
        out_specs=pl.BlockSpec((tr, lanes), lambda i, me_ref: (i, 0)))
    return pl.pallas_call(body, name=name, grid_spec=grid_spec, out_shape=SDS((rows, lanes), F32),
                          compiler_params=_params(("parallel",)))(me_idx, r4, r4, r4, r4, s4)


def _adamw(wf, g_own, g_other, mf, vf, c_idx, name):
    rows, lanes = wf.shape
    hr = rows // 2
    tr = _row_tile(hr)
    nb = hr // tr
    c1 = 1.0 / (1.0 - ADAM_B1 ** ADAM_STEP)
    c2 = 1.0 / (1.0 - ADAM_B2 ** ADAM_STEP)

    def body(c_ref, w_ref, go_ref, gx_ref, m_ref, v_ref, g_ref, d_ref, nm_ref, nv_ref):
        gv = jnp.where(pl.program_id(0) == c_ref[0], go_ref[...], gx_ref[...])
        m = ADAM_B1 * m_ref[...] + (1.0 - ADAM_B1) * gv
        v = ADAM_B2 * v_ref[...] + (1.0 - ADAM_B2) * (gv * gv)
        g_ref[...] = gv
        d_ref[...] = -ADAM_LR * ((m * c1) / (jnp.sqrt(v * c2) + ADAM_EPS) + ADAM_WD * w_ref[...])
        nm_ref[...] = m
        nv_ref[...] = v

    full = pl.BlockSpec((tr, lanes), lambda h, i, c_ref: (h * nb + i, 0))
    half = pl.BlockSpec((tr, lanes), lambda h, i, c_ref: (i, 0))
    grid_spec = pltpu.PrefetchScalarGridSpec(num_scalar_prefetch=1, grid=(2, nb),
                                             in_specs=[full, half, half, full, full], out_specs=[full] * 4)
    return pl.pallas_call(body, name=name, grid_spec=grid_spec, out_shape=[SDS((rows, lanes), F32)] * 4,
                          compiler_params=_params(("parallel", "parallel")))(c_idx, wf, g_own, g_other, mf, vf)


BIG = ("ffn1_w_gate", "ffn1_w_up", "ffn1_w_down", "ffn2_w_gate", "ffn2_w_up", "ffn2_w_down", "w_out", "w_in")
TRANSPOSED = ("ffn1_w_gate", "ffn1_w_up", "ffn2_w_gate", "ffn2_w_up")
PACKED = ("rwkv_w2", "rwkv_a2", "rwkv_g2")
SMALL_SHAPES = {"ffn1_norm": (1, D_MODEL), "mix_norm": (1, D_MODEL), "hgrn_lb_logits": (2, W_A),
                "hgrn_out_norm": (1, W_A), "rwkv_shift_mu": (1, N_RWKV_COLS), "rwkv_w0": (1, W_B),
                "rwkv_a0": (1, W_B), "rwkv_k_k": (1, W_B), "rwkv_k_a": (1, W_B),
                "rwkv_r_k": (1, HB_HEADS, HB_DIM), "rwkv_gn_w": (1, W_B), "rwkv_gn_b": (1, W_B),
                "ffn2_norm": (1, D_MODEL), "final_norm": (D_MODEL,)}
PACK_ELEMS = sum(_numel(_shard_shape(n)) for n in PACKED) + sum(_numel(SMALL_SHAPES[n]) for n in SMALL)
PACK_ROWS = -(-PACK_ELEMS // (32 * LANES)) * 32


def _to_rows(name, shard):
    return shard[0].T if name in TRANSPOSED else shard[0]


def _from_rows(name, rows):
    return (rows.T if name in TRANSPOSED else rows)[None]


def _pack(sharded, small):
    flat = jnp.concatenate([sharded[n].reshape(-1) for n in PACKED] + [small[n].reshape(-1) for n in SMALL])
    return jnp.pad(flat, (0, PACK_ROWS * LANES - flat.shape[0])).reshape(PACK_ROWS, LANES)


def _unpack(packed):
    flat, out, off = packed.reshape(-1), {}, 0
    for n in PACKED:
        shp = _shard_shape(n)
        out[n] = flat[off:off + _numel(shp)].reshape((1,) + shp)
        off += _numel(shp)
    for n in SMALL:
        shp = SMALL_SHAPES[n]
        out[n] = flat[off:off + _numel(shp)].reshape(shp)
        off += _numel(shp)
    return out


def _quarter(full, name, q):
    shape, ax = SHARDED_SHAPES[name]
    w = shape[ax] // N_CHIPS
    return lax.slice_in_dim(full, q * w, (q + 1) * w, axis=ax)


def kernel(x, ffn1_norm, ffn1_w_gate, ffn1_w_up, ffn1_w_down, mix_norm, w_in, hgrn_lb_logits, hgrn_out_norm, rwkv_shift_mu, rwkv_w0, rwkv_w2, rwkv_a0, rwkv_a2, rwkv_g2, rwkv_k_k, rwkv_k_a, rwkv_r_k, rwkv_gn_w, rwkv_gn_b, w_out, ffn2_norm, ffn2_w_gate, ffn2_w_up, ffn2_w_down, final_norm, loss_target, m_ffn1_norm, m_ffn1_w_gate, m_ffn1_w_up, m_ffn1_w_down, m_mix_norm, m_w_in, m_hgrn_lb_logits, m_hgrn_out_norm, m_rwkv_shift_mu, m_rwkv_w0, m_rwkv_w2, m_rwkv_a0, m_rwkv_a2, m_rwkv_g2, m_rwkv_k_k, m_rwkv_k_a, m_rwkv_r_k, m_rwkv_gn_w, m_rwkv_gn_b, m_w_out, m_ffn2_norm, m_ffn2_w_gate, m_ffn2_w_up, m_ffn2_w_down, m_final_norm, v_ffn1_norm, v_ffn1_w_gate, v_ffn1_w_up, v_ffn1_w_down, v_mix_norm, v_w_in, v_hgrn_lb_logits, v_hgrn_out_norm, v_rwkv_shift_mu, v_rwkv_w0, v_rwkv_w2, v_rwkv_a0, v_rwkv_a2, v_rwkv_g2, v_rwkv_k_k, v_rwkv_k_a, v_rwkv_r_k, v_rwkv_gn_w, v_rwkv_gn_b, v_w_out, v_ffn2_norm, v_ffn2_w_gate, v_ffn2_w_up, v_ffn2_w_down, v_final_norm):
    args = dict(locals())
    wts = {n: args[n] for n in ALL_WEIGHTS}
    moms = {n: args["m_" + n] for n in ALL_WEIGHTS}
    vars_ = {n: args["v_" + n] for n in ALL_WEIGHTS}

    me = 2 * lax.axis_index("x") + lax.axis_index("y")
    c_idx = lax.axis_index("c").astype(jnp.int32).reshape(1)
    me_idx = me.astype(jnp.int32).reshape(1)
    shard_of = {n: _to_rows(n, wts[n]).astype(BF16) for n in BIG}
    shard_of["packed"] = _pack(wts, {n: wts[n] for n in SMALL}).astype(BF16)
    group = {"ffn1": BIG[0:3], "ffn2": BIG[3:6]}

    def slot_bufs(names):
        return [lax.dynamic_update_slice(jnp.zeros((N_CHIPS,) + shard_of[n].shape, BF16), shard_of[n][None],
                                         (me, 0, 0)) for n in names]

    def ffn_weights(tag, gathered):
        return {f"{tag}_wgt": gathered[0].reshape(D_FF, D_MODEL), f"{tag}_wut": gathered[1].reshape(D_FF, D_MODEL),
                f"{tag}_wd": gathered[2].reshape(D_FF, D_MODEL)}

    def w_in_weights(gathered):
        w_in_full = jnp.concatenate([gathered[0][q] for q in range(N_CHIPS)], axis=1)
        return {"w_in_h": w_in_full[:, :N_HGRN_COLS],
                "w_in_r": jnp.pad(w_in_full[:, N_HGRN_COLS:], ((0, 0), (0, N_RWKV_PAD - N_RWKV_COLS)))}

    def mixer_weights(gathered):
        w_out_full = gathered[0].reshape(D_MODEL, D_MODEL)
        packs = gathered[1].reshape(N_CHIPS, PACK_ROWS * LANES)
        full, off = {}, 0
        for n in PACKED:
            shp = _shard_shape(n)
            full[n] = jnp.concatenate([packs[q, off:off + _numel(shp)].reshape(shp) for q in range(N_CHIPS)], axis=1)
            off += _numel(shp)
        zrow = lambda nrow: jnp.zeros((nrow, W_B), BF16)
        return {"w_out": w_out_full,
                "w2_pad": jnp.concatenate([full["rwkv_w2"], zrow(LORA_PAD - 32)], axis=0),
                "a2_pad": jnp.concatenate([zrow(32), full["rwkv_a2"], zrow(LORA_PAD - 64)], axis=0),
                "g2_pad": jnp.concatenate([zrow(64), full["rwkv_g2"], zrow(LORA_PAD - 160)], axis=0)}

    plan = _Plan()
    w = {}
    plan.carry("ffn1_rms", lambda g: _gather_comm(slot_bufs(group["ffn1"][:2])),
               lambda res, w_: w_.update({"ffn1_wgt": res[0].reshape(D_FF, D_MODEL),
                                          "ffn1_wut": res[1].reshape(D_FF, D_MODEL)}))

    def after_gate_up(res, w_):
        w_["ffn1_wd"] = res[0].reshape(D_FF, D_MODEL)
        w_.update(w_in_weights(res[1:]))

    plan.carry("ffn1_gate_up", lambda g: _gather_comm(slot_bufs(("ffn1_w_down", "w_in"))), after_gate_up)
    plan.carry("ffn1_down", lambda g: _gather_comm(slot_bufs(("w_out", "packed"))),
               lambda res, w_: w_.update(mixer_weights(res)))
    plan.carry("rwkv_fwd", lambda g: _gather_comm(slot_bufs(group["ffn2"])),
               lambda res, w_: w_.update(ffn_weights("ffn2", res)))
    w["ffn1_norm"], w["ffn2_norm"] = ffn1_norm, ffn2_norm
    w["mix_norm"] = mix_norm
    w["lb0"], w["lb1"] = hgrn_lb_logits[0:1], hgrn_lb_logits[1:2]
    w["hgrn_out_norm"] = hgrn_out_norm
    w["mu_pad"] = jnp.pad(rwkv_shift_mu, ((0, 0), (0, N_RWKV_PAD - N_RWKV_COLS)))
    for n in ("rwkv_w0", "rwkv_a0", "rwkv_k_k", "rwkv_k_a", "rwkv_gn_w", "rwkv_gn_b"):
        w[n] = wts[n]
    w["rwkv_r_k"] = rwkv_r_k.reshape(1, W_B)
    w["final_norm"] = final_norm.reshape(1, D_MODEL)

    def reduce_rows(names, gs):
        r1 = _run_comm(_sibling_exchange_comm(gs), "grad_sibling_exchange")
        s4 = [_add_halves(gt, rt, c_idx, f"grad_add_halves_{n}") for gt, rt, n in zip(gs, r1, names)]
        r2 = _run_comm(_chip_exchange_comm(s4), "grad_chip_exchange")
        return [_sum_chips(rt, st, me_idx, f"grad_sum_chips_{n}") for rt, st, n in zip(r2, s4, names)]

    early = {}

    def reduce_early(names, grads_of, sibling_host, chips_host):
        def sibling_comm(g):
            early[names, "gs"] = grads_of(g)
            return _sibling_exchange_comm(early[names, "gs"])

        def after_sibling(res, w_):
            early[names, "s4"] = [_add_halves(gt, rt, c_idx, f"grad_add_halves_{n}")
                                  for gt, rt, n in zip(early[names, "gs"], res, names)]

        def after_chips(res, w_):
            early.update(zip(names, [_sum_chips(rt, st, me_idx, f"grad_sum_chips_{n}")
                                     for rt, st, n in zip(res, early[names, "s4"], names)]))

        plan.carry(sibling_host, sibling_comm, after_sibling)
        plan.carry(chips_host, lambda g: _chip_exchange_comm(early[names, "s4"]), after_chips)

    def proj_grads(g):
        g_w_in = jnp.concatenate([g["w_in_h"], g["w_in_r"][:, :N_RWKV_COLS]], axis=1)
        return [g["w_out"].reshape(N_CHIPS, -1, D_MODEL),
                jnp.stack([_quarter(g_w_in, "w_in", q) for q in range(N_CHIPS)])]

    rows_of = lambda keys: (lambda g: [g[k].reshape(N_CHIPS, -1, D_MODEL) for k in keys])
    reduce_early(group["ffn2"], rows_of(("ffn2_wgt", "ffn2_wut", "ffn2_wd")), "hgrn_bwd", "rwkv_bwd")
    reduce_early(("w_out", "w_in"), proj_grads, "mix_drms", "ffn1_dact")
    reduce_early(("ffn1_w_down",), rows_of(("ffn1_wd",)), "ffn1_dwg", "ffn1_dwu")
    reduce_early(("ffn1_w_gate",), rows_of(("ffn1_wgt",)), "ffn1_dwu", "ffn1_dh_g")
    reduce_early(("ffn1_w_up",), rows_of(("ffn1_wut",)), "ffn1_dh_g", "ffn1_dh_u")
    loss_slab, grad_x, g = _local_step(x[0], loss_target[0], w, plan)
    loss = lax.psum(loss_slab[0, 0], ("x", "y", "c"))

    gfull = {
        "rwkv_w2": g["w2_pad"][0:32], "rwkv_a2": g["a2_pad"][32:64], "rwkv_g2": g["g2_pad"][64:160],
    }
    gsmall = {
        "ffn1_norm": g["ffn1_norm"], "mix_norm": g["mix_norm"],
        "hgrn_lb_logits": jnp.concatenate([g["lb0"], g["lb1"]], axis=0), "hgrn_out_norm": g["hgrn_out_norm"],
        "rwkv_shift_mu": g["mu_pad"][:, :N_RWKV_COLS], "rwkv_w0": g["rwkv_w0"], "rwkv_a0": g["rwkv_a0"],
        "rwkv_k_k": g["rwkv_k_k"], "rwkv_k_a": g["rwkv_k_a"], "rwkv_r_k": g["rwkv_r_k"],
        "rwkv_gn_w": g["rwkv_gn_w"], "rwkv_gn_b": g["rwkv_gn_b"], "ffn2_norm": g["ffn2_norm"],
        "final_norm": g["final_norm"],
    }
    packed = jnp.stack([_pack({n: _quarter(gfull[n], n, q) for n in PACKED}, gsmall) for q in range(N_CHIPS)])
    early["packed"], = reduce_rows(["packed"], [packed])
    names = list(BIG) + ["packed"]
    own = [early[n] for n in names]
    other = _run_comm(_sibling_swap_comm(own), "grad_sibling_swap")

    def rows_list(d):
        return [_to_rows(n, d[n]) for n in BIG] + [_pack(d, {n: d[n] for n in SMALL})]

    outs = [_adamw(wt, go, gx, mt, vt, c_idx, f"adamw_{n}")
            for wt, go, gx, mt, vt, n in zip(rows_list(wts), own, other, rows_list(moms), rows_list(vars_), names)]
    results = []
    for k in range(4):
        per = [outs[i][k] for i in range(len(names))]
        d = {n: _from_rows(n, z) for n, z in zip(BIG, per[:-1])}
        d.update(_unpack(per[-1]))
        results.append(d)
    return (loss, grad_x[None], *[r[n] for r in results for n in ALL_WEIGHTS])
```

```python
import collections
import functools

import jax
import jax.numpy as jnp
from jax import lax
from jax.experimental import pallas as pl
from jax.experimental.pallas import tpu as pltpu

F32 = jnp.float32
BF16 = jnp.bfloat16
SDS = jax.ShapeDtypeStruct
MESH = pl.DeviceIdType.MESH

D_MODEL = 1024
D_FF = 2816
W_A = 512
W_B = 512
HA_HEADS, HA_DIM = 4, 128
HB_HEADS, HB_DIM = 8, 64
HGRN_CHUNK = 64
HGRN_GROUP = 8
RWKV_CHUNK = 16
RWKV_GROUP = 8
N_HGRN_COLS = 4 * W_A
N_RWKV_COLS = 3 * W_B + 32 + 32 + 96
N_RWKV_PAD = 1792
LORA_PAD = 256
NORM_EPS = 1e-6
RWKV_GN_EPS = 64e-5
L2_EPS = 1e-12
ADAM_LR, ADAM_B1, ADAM_B2, ADAM_EPS, ADAM_WD, ADAM_STEP = 0.001, 0.9, 0.999, 1e-8, 0.01, 10

N_CHIPS = 4
VMEM_LIMIT_V7X = 56 * 1024 * 1024
LANES = 1024

SHARDED_SHAPES = {
    "ffn1_w_gate": ((D_MODEL, D_FF), 1), "ffn1_w_up": ((D_MODEL, D_FF), 1), "ffn1_w_down": ((D_FF, D_MODEL), 0),
    "w_in": ((D_MODEL, N_HGRN_COLS + N_RWKV_COLS), 1), "rwkv_w2": ((32, W_B), 1), "rwkv_a2": ((32, W_B), 1),
    "rwkv_g2": ((96, W_B), 1), "w_out": ((D_MODEL, D_MODEL), 0),
    "ffn2_w_gate": ((D_MODEL, D_FF), 1), "ffn2_w_up": ((D_MODEL, D_FF), 1), "ffn2_w_down": ((D_FF, D_MODEL), 0),
}
SMALL = ("ffn1_norm", "mix_norm", "hgrn_lb_logits", "hgrn_out_norm", "rwkv_shift_mu", "rwkv_w0", "rwkv_a0",
         "rwkv_k_k", "rwkv_k_a", "rwkv_r_k", "rwkv_gn_w", "rwkv_gn_b", "ffn2_norm", "final_norm")
ALL_WEIGHTS = ("ffn1_norm", "ffn1_w_gate", "ffn1_w_up", "ffn1_w_down", "mix_norm", "w_in", "hgrn_lb_logits",
               "hgrn_out_norm", "rwkv_shift_mu", "rwkv_w0", "rwkv_w2", "rwkv_a0", "rwkv_a2", "rwkv_g2", "rwkv_k_k",
               "rwkv_k_a", "rwkv_r_k", "rwkv_gn_w", "rwkv_gn_b", "w_out", "ffn2_norm", "ffn2_w_gate", "ffn2_w_up",
               "ffn2_w_down", "final_norm")


def _shard_shape(name):
    shape, ax = SHARDED_SHAPES[name]
    return tuple(s // N_CHIPS if i == ax else s for i, s in enumerate(shape))


def _numel(shape):
    n = 1
    for s in shape:
        n *= s
    return n


def _params(sem=None):
    return pltpu.CompilerParams(dimension_semantics=sem, vmem_limit_bytes=VMEM_LIMIT_V7X)


def _split2(x):
    hi = x.astype(BF16)
    return hi, (x.astype(F32) - hi.astype(F32)).astype(BF16)


def _dg(x, y, cx, cy, hi):
    dn = (((cx,), (cy,)), ((), ()))
    dot = lambda p, q: lax.dot_general(p, q, dn, preferred_element_type=F32)
    if hi == "x3":
        (xh, xl), (yh, yl) = _split2(x), _split2(y)
        return dot(xh, yh) + (dot(xh, yl) + dot(xl, yh))
    return dot(x.astype(BF16), y.astype(BF16))


def _make_mm(hi, cotangent_forms=None):
    @jax.custom_vjp
    def nn(x, y):
        return _dg(x, y, 1, 0, hi)

    @jax.custom_vjp
    def nt(x, y):
        return _dg(x, y, 1, 1, hi)

    @jax.custom_vjp
    def tn(x, y):
        return _dg(x, y, 0, 0, hi)

    bnn, bnt, btn = cotangent_forms or (nn, nt, tn)
    nn.defvjp(lambda x, y: (nn(x, y), (x, y)), lambda r, g: (bnt(g, r[1]), btn(r[0], g)))
    nt.defvjp(lambda x, y: (nt(x, y), (x, y)), lambda r, g: (bnn(g, r[1]), btn(g, r[0])))
    tn.defvjp(lambda x, y: (tn(x, y), (x, y)), lambda r, g: (bnt(r[1], g), bnn(r[0], g)))
    return nn, nt, tn


_nn, _nt, _tn = _make_mm(False)
_nn_x3, _nt_x3, _tn_x3 = _make_mm("x3", (_nn, _nt, _tn))


def _tri_apply(x, transpose):
    c = x.shape[0]
    tri = (lax.broadcasted_iota(jnp.int32, (c, c), 1) <= lax.broadcasted_iota(jnp.int32, (c, c), 0)).astype(BF16)
    dn = (((0 if transpose else 1,), (0,)), ((), ()))
    p1, p2 = _split2(x)
    dot = lambda p: lax.dot_general(tri, p, dn, preferred_element_type=F32)
    return dot(p1) + dot(p2)


@jax.custom_vjp
def _cumsum_rows(x):
    return _tri_apply(x, False)


_cumsum_rows.defvjp(lambda x: (_tri_apply(x, False), None), lambda _, g: (_tri_apply(g, True),))


def _sigmoid(x):
    return 1.0 / (1.0 + jnp.exp(-x))


def _silu(x):
    return x * _sigmoid(x)


def _softplus(z):
    return jnp.maximum(z, 0.0) + jnp.log(1.0 + jnp.exp(-jnp.abs(z)))


def _mm(a, b, *, ta=False, tb=False, tm, tn, tk, name, out_dtype=F32, res=None, scale=None, comm=None):
    m = a.shape[1] if ta else a.shape[0]
    kdim = a.shape[0] if ta else a.shape[1]
    n = b.shape[0] if tb else b.shape[1]
    assert (b.shape[1] if tb else b.shape[0]) == kdim
    tm, tn, tk = min(tm, m), min(tn, n), min(tk, kdim)
    assert m % tm == 0 and n % tn == 0 and kdim % tk == 0, (name, m, n, kdim)
    nk = kdim // tk
    a_spec = pl.BlockSpec((tk, tm), lambda i, j, k: (k, i)) if ta else pl.BlockSpec((tm, tk), lambda i, j, k: (i, k))
    b_spec = pl.BlockSpec((tn, tk), lambda i, j, k: (j, k)) if tb else pl.BlockSpec((tk, tn), lambda i, j, k: (k, j))
    o_spec = pl.BlockSpec((tm, tn), lambda i, j, k: (i, j))
    ca, cb = (0 if ta else 1), (1 if tb else 0)

    def body(*refs):
        if res is not None:
            a_ref, b_ref, r_ref, o_ref, acc_ref = refs
        else:
            a_ref, b_ref, o_ref, acc_ref = refs
        k = pl.program_id(2)

        @pl.when(k == 0)
        def _():
            acc_ref[...] = jnp.zeros_like(acc_ref)

        acc_ref[...] += _dg(a_ref[...], b_ref[...], ca, cb, False)

        @pl.when(k == nk - 1)
        def _():
            acc = acc_ref[...]
            if scale is not None:
                acc = acc * scale
            if res is not None:
                acc = r_ref[...] + acc
            o_ref[...] = acc.astype(out_dtype)

    in_specs = [a_spec, b_spec] + ([o_spec] if res is not None else [])
    args = (a, b) + ((res,) if res is not None else ())
    if comm is None:
        return pl.pallas_call(
            body, name=name, grid=(m // tm, n // tn, nk), in_specs=in_specs, out_specs=o_spec,
            out_shape=SDS((m, n), out_dtype), scratch_shapes=[pltpu.VMEM((tm, tn), F32)],
            compiler_params=_params(("parallel", "parallel", "arbitrary")))(*args)
    (out,), carried = _hosting_call(
        body, comm, name=name, grid=(m // tm, n // tn, nk), in_specs=in_specs, out_specs=[o_spec],
        out_shape=[SDS((m, n), out_dtype)], scratch_shapes=[pltpu.VMEM((tm, tn), F32)], args=args)
    return out, carried


def _row_spec(x, tm, tile_of=lambda i: i):
    if isinstance(x, tuple):
        arr, w, j = x
        return arr, pl.BlockSpec((tm, w), lambda i, j=j: (tile_of(i), j))
    return x, pl.BlockSpec((tm, x.shape[1]), lambda i: (tile_of(i), 0))


def _par_spec(p):
    if isinstance(p, tuple):
        arr, w, j = p
        return arr, pl.BlockSpec((arr.shape[0], w), lambda i, j=j: (0, j))
    return p, pl.BlockSpec(p.shape, lambda i: (0, 0))


def _store_groups(refs, groups, vals):
    for ref, idxs in zip(refs, groups):
        off = 0
        for ix in idxs:
            v = vals[ix]
            ref[:, off:off + v.shape[1]] = v.astype(ref.dtype)
            off += v.shape[1]


SUBLANES = 8


def _x_plan(xs, tm, t, tile_of=lambda i: i):
    arrays, specs, plan = [], [], []
    nb = tm // SUBLANES
    for x in xs:
        if isinstance(x, tuple) and isinstance(x[0], str):
            kind, arr, w, j = x
            if kind == "prev":
                halo = lambda i, j=j: (jnp.maximum(tile_of(i) * nb - 1, 0), j)
            else:
                halo = lambda i, j=j: (jnp.minimum((tile_of(i) + 1) * nb, t // SUBLANES - 1), j)
            arrays += [arr, arr]
            specs += [pl.BlockSpec((tm, w), lambda i, j=j: (tile_of(i), j)), pl.BlockSpec((SUBLANES, w), halo)]
            plan.append((kind, 2, w))
        else:
            arr, spec = _row_spec(x, tm, tile_of)
            arrays.append(arr)
            specs.append(spec)
            plan.append(("plain", 1, spec.block_shape[1]))
    return arrays, specs, plan


def _x_vals(refs, plan, tm, nt, tile_of=lambda i: i):
    vals, k = [], 0
    i = tile_of(pl.program_id(0))
    rows = lax.broadcasted_iota(jnp.int32, (tm, 1), 0)
    for kind, n, _ in plan:
        main = refs[k][...].astype(F32)
        if kind == "prev":
            edge = jnp.where(i == 0, 0.0, refs[k + 1][SUBLANES - 1:SUBLANES, :].astype(F32))
            main = jnp.where(rows == 0, edge, pltpu.roll(main, 1, 0))
        elif kind == "next":
            edge = jnp.where(i == nt - 1, 0.0, refs[k + 1][0:1, :].astype(F32))
            main = jnp.where(rows == tm - 1, edge, pltpu.roll(main, tm - 1, 0))
        vals.append(main)
        k += n
    return vals


def _tile_rows(xs, tm):
    arr = xs[0]
    if isinstance(arr, tuple):
        arr = arr[1] if isinstance(arr[0], str) else arr[0]
    return min(tm, arr.shape[0]), arr.shape[0]


def _rowwise(f, xs, params, out_groups, out_dtypes, *, tm, name, comm=None):
    tm, t = _tile_rows(xs, tm)
    nt = t // tm
    xa, xspecs, plan = _x_plan(xs, tm, t)
    pa, pspecs = (zip(*[_par_spec(p) for p in params]) if params else ((), ()))
    nxr, npar = len(xa), len(pa)
    x_sds = [SDS((tm, w), F32) for _, _, w in plan]
    p_sds = [SDS(s.block_shape, F32) for s in pspecs]
    outs_sds = jax.eval_shape(lambda *vals: f(*vals), *x_sds, *p_sds)
    widths = [sum(outs_sds[ix].shape[1] for ix in idxs) for idxs in out_groups]

    def body(*refs):
        vals = _x_vals(refs[:nxr], plan, tm, nt) + [r[...].astype(F32) for r in refs[nxr:nxr + npar]]
        outs = f(*vals)
        _store_groups(refs[nxr + npar:], out_groups, outs)

    res, carried = _hosting_call(
        body, comm, name=name, grid=(nt,), in_specs=list(xspecs) + list(pspecs),
        out_specs=[pl.BlockSpec((tm, w), lambda i: (i, 0)) for w in widths],
        out_shape=[SDS((t, w), dt) for w, dt in zip(widths, out_dtypes)], scratch_shapes=[], args=(*xa, *pa))
    return res if comm is None else (res, carried)


def _rowwise_bwd(f, xs, params, cots, *, x_grad, p_grad, dx_groups, dx_dtypes, tm, name, extra=None, comm=None,
                 fold_next=None):
    tm, t = _tile_rows(xs, tm)
    nt = t // tm
    tile_of = (lambda i: nt - 1 - i) if fold_next else (lambda i: i)
    xa, xspecs, plan = _x_plan(xs, tm, t, tile_of)
    pa, pspecs = (zip(*[_par_spec(p) for p in params]) if params else ((), ()))
    ca, cspecs = zip(*[_row_spec(c, tm, tile_of) for c in cots])
    extra = extra or {}
    ekeys = sorted(extra)
    ea, especs = (zip(*[_row_spec(extra[k], tm, tile_of) for k in ekeys]) if ekeys else ((), ()))
    nx, nxr, npar, nc, ne = len(plan), len(xa), len(pa), len(ca), len(ea)
    gx = [i for i in range(nx) if x_grad[i]]
    gp = [i for i in range(npar) if p_grad[i]]
    all_widths = [sum(plan[gx[ix]][2] for ix in idxs) for idxs in dx_groups]
    emitted = [k for k in range(len(dx_groups)) if not (fold_next and k == fold_next[1])]
    widths = [all_widths[k] for k in emitted]
    ng = len(emitted)

    def body(*refs):
        ins = refs[:nxr + npar + nc + ne]
        outs = refs[nxr + npar + nc + ne:]
        vals = _x_vals(ins[:nxr], plan, tm, nt, tile_of) + [r[...].astype(F32) for r in ins[nxr:nxr + npar]]
        cvals = tuple(r[...].astype(F32) for r in ins[nxr + npar:nxr + npar + nc])
        evals = [r[...].astype(F32) for r in ins[nxr + npar + nc:]]
        diff_idx = gx + [nx + i for i in gp]

        def g(*dargs):
            full = list(vals)
            for ix, v in zip(diff_idx, dargs):
                full[ix] = v
            return tuple(f(*full))

        _, vjp = jax.vjp(g, *[vals[ix] for ix in diff_idx])
        grads = vjp(cvals)
        dxs = list(grads[:len(gx)])
        for k, ev in zip(ekeys, evals):
            dxs[k] = dxs[k] + ev
        _store_groups(outs[:ng], [dx_groups[k] for k in emitted], dxs)
        i = pl.program_id(0)
        if fold_next:
            main_ref, carry_ref = outs[emitted.index(fold_next[0])], refs[-1]
            rows = lax.broadcasted_iota(jnp.int32, (tm, 1), 0)
            off = 0
            for ix in dx_groups[fold_next[1]]:
                piece = dxs[ix]
                cols = slice(off, off + piece.shape[1])
                edge = jnp.where(i == 0, 0.0, carry_ref[0:1, cols])
                main_ref[:, cols] += jnp.where(rows == tm - 1, edge, pltpu.roll(piece, tm - 1, 0))
                carry_ref[:, cols] = piece[:SUBLANES]
                off += piece.shape[1]
        for ref, gval in zip(outs[ng:ng + len(gp)], grads[len(gx):]):
            @pl.when(i == 0)
            def _(ref=ref):
                ref[...] = jnp.zeros_like(ref)
            ref[...] += gval

    dp_specs = [pl.BlockSpec(pspecs[i].block_shape, lambda i: (0, 0)) for i in gp]
    dp_shapes = [SDS(pspecs[i].block_shape, F32) for i in gp]
    scratch = [pltpu.VMEM((SUBLANES, all_widths[fold_next[1]]), F32)] if fold_next else []
    res, carried = _hosting_call(
        body, comm, name=name, grid=(nt,), in_specs=list(xspecs) + list(pspecs) + list(cspecs) + list(especs),
        out_specs=[pl.BlockSpec((tm, w), lambda i: (tile_of(i), 0)) for w in widths] + dp_specs,
        out_shape=[SDS((t, w), dt) for w, dt in zip(widths, dx_dtypes)] + dp_shapes, scratch_shapes=scratch,
        args=(*xa, *pa, *ca, *ea))
    return res if comm is None else (res, carried)


def _rms_f(x, g):
    return (x * lax.rsqrt(jnp.mean(x * x, axis=-1, keepdims=True) + NORM_EPS) * g,)


def _group_sum_impl(x, ones_bd):
    p1, p2 = _split2(x)
    dot = lambda p: lax.dot_general(p, ones_bd.astype(BF16), (((1,), (0,)), ((), ())), preferred_element_type=F32)
    return dot(p1) + dot(p2)


@jax.custom_vjp
def _group_sum(x, ones_bd):
    return _group_sum_impl(x, ones_bd)


_group_sum.defvjp(lambda x, o: (_group_sum_impl(x, o), o),
                  lambda o, g: (_group_sum_impl(g, o), jnp.zeros_like(o)))


def _rwkv_prep_f(r, k, v, lo, rp, kp, vp, lop, mu_r, mu_k, mu_v, mu_lo, w0, w2p, a0, a2p, g2p, k_k, k_a, ones_bd):
    r = r + mu_r * (rp - r)
    k = k + mu_k * (kp - k)
    v = v + mu_v * (vp - v)
    lo = lo + mu_lo * (lop - lo)
    w_log = -_softplus(-(w0 + _nn(jnp.tanh(lo), w2p))) - 0.5
    lw = -jnp.exp(w_log)
    a_g = _sigmoid(a0 + _nn(lo, a2p))
    g = _nn(_sigmoid(lo), g2p)
    kk = k * k_k
    kk = kk / jnp.maximum(jnp.sqrt(_group_sum(kk * kk, ones_bd)), L2_EPS)
    k2 = k * (1.0 + (a_g - 1.0) * k_a)
    return r, lw, k2, v, -kk, kk * a_g, g


def _rwkv_post_f(y, r, k2, v, g, r_k, gn_w, gn_b, ones_bd):
    inv_n = 1.0 / HB_DIM
    mean = _group_sum(y, ones_bd) * inv_n
    yc = y - mean
    var = _group_sum(yc * yc, ones_bd) * inv_n
    yn = yc * lax.rsqrt(var + RWKV_GN_EPS) * gn_w + gn_b
    bonus = _group_sum(r * k2 * r_k, ones_bd) * v
    return ((yn + bonus) * g,)


def _tri(c, strict=False):
    ii = lax.broadcasted_iota(jnp.int32, (c, c), 0)
    jj = lax.broadcasted_iota(jnp.int32, (c, c), 1)
    return (jj < ii) if strict else (jj <= ii)


def _hgrn_step(st0, q_a, f_a, i_a, g_a, l0, l1, onorm):
    nh, nj = len(q_a), len(q_a[0])
    c = q_a[0][0].shape[0]
    combos = [(j, h) for j in range(nj) for h in range(nh)]
    every = lambda fn: {q: fn(q) for q in combos}
    at_ = lambda d: (lambda q: d[q[1]][q[0]])
    qa_, fa_, ia_, ga_ = (at_(z) for z in (q_a, f_a, i_a, g_a))
    incl = _tri(c)
    rows = lax.broadcasted_iota(jnp.int32, (c, 1), 0)
    lb = []
    for h in range(nh):
        mx = jnp.maximum(l0[h], l1[h])
        e0, e1 = jnp.exp(l0[h] - mx), jnp.exp(l1[h] - mx)
        lb.append(e0 / (e0 + e1))
    forget = every(lambda q: lb[q[1]] + (1.0 - lb[q[1]]) * _sigmoid(fa_(q)))
    qs = every(lambda q: _silu(qa_(q)))
    kk = every(lambda q: 1.0 - forget[q])
    lf = every(lambda q: jnp.log(forget[q]))
    bcum = every(lambda q: _cumsum_rows(lf[q]))
    bref = every(lambda q: jnp.sum(jnp.where(rows <= c // 2, lf[q], 0.0), axis=0, keepdims=True))
    blast = every(lambda q: jnp.sum(lf[q], axis=0, keepdims=True))
    scores = every(lambda q: jnp.where(incl, _nt(qs[q] * jnp.exp(bcum[q] - bref[q]),
                                                 kk[q] * jnp.exp(bref[q] - bcum[q])), 0.0))
    intra = every(lambda q: _nn(scores[q], ia_(q)))
    qb = every(lambda q: qs[q] * jnp.exp(bcum[q]))
    upd = every(lambda q: _tn(ia_(q), kk[q] * jnp.exp(blast[q] - bcum[q])))
    dec = every(lambda q: jnp.exp(blast[q]))
    st = list(st0)
    o = {}
    for j in range(nj):
        for h in range(nh):
            o[(j, h)] = intra[(j, h)] + _nt(qb[(j, h)], st[h])
        st = [st[h] * dec[(j, h)] + upd[(j, h)] for h in range(nh)]
    out = every(lambda q: o[q] * lax.rsqrt(jnp.mean(o[q] * o[q], axis=-1, keepdims=True) + NORM_EPS)
                * onorm[q[1]] * _silu(ga_(q)))
    return [[out[(j, h)] for j in range(nj)] for h in range(nh)], st


def _hgrn_blocks(ref, nj, c):
    return [[ref[j * c:(j + 1) * c, h * HA_DIM:(h + 1) * HA_DIM] for j in range(nj)] for h in range(HA_HEADS)]


def _hgrn_cols(ref):
    return [ref[:, h * HA_DIM:(h + 1) * HA_DIM] for h in range(HA_HEADS)]


def _hgrn_fwd(p_h, l0, l1, onorm):
    t = p_h.shape[0]
    cc, nj = HGRN_CHUNK, HGRN_GROUP
    c = cc * nj
    n = t // c

    def body(q_ref, f_ref, i_ref, g_ref, l0_ref, l1_ref, on_ref, o_ref, hs_ref, st_ref):
        @pl.when(pl.program_id(0) == 0)
        def _():
            st_ref[...] = jnp.zeros_like(st_ref)

        hs_ref[0] = st_ref[...]
        o, st1 = _hgrn_step([st_ref[h] for h in range(HA_HEADS)],
                            *[_hgrn_blocks(ref, nj, cc) for ref in (q_ref, f_ref, i_ref, g_ref)],
                            _hgrn_cols(l0_ref), _hgrn_cols(l1_ref), _hgrn_cols(on_ref))
        for h in range(HA_HEADS):
            for j in range(nj):
                o_ref[j * cc:(j + 1) * cc, h * HA_DIM:(h + 1) * HA_DIM] = o[h][j]
            st_ref[h] = st1[h]

    col = lambda j: pl.BlockSpec((c, W_A), lambda i, j=j: (i, j))
    par = pl.BlockSpec((1, W_A), lambda i: (0, 0))
    return pl.pallas_call(
        body, name="hgrn_fwd", grid=(n,), in_specs=[col(0), col(1), col(2), col(3), par, par, par],
        out_specs=[pl.BlockSpec((c, W_A), lambda i: (i, 0)),
                   pl.BlockSpec((1, HA_HEADS, HA_DIM, HA_DIM), lambda i: (i, 0, 0, 0))],
        out_shape=[SDS((t, W_A), F32), SDS((n, HA_HEADS, HA_DIM, HA_DIM), F32)],
        scratch_shapes=[pltpu.VMEM((HA_HEADS, HA_DIM, HA_DIM), F32)],
        compiler_params=_params(("arbitrary",)))(p_h, p_h, p_h, p_h, l0, l1, onorm)


def _hgrn_bwd(p_h, l0, l1, onorm, hs, do, do_col, comm=None):
    t = p_h.shape[0]
    cc, nj = HGRN_CHUNK, HGRN_GROUP
    c = cc * nj
    n = t // c

    def body(q_ref, f_ref, i_ref, g_ref, l0_ref, l1_ref, on_ref, hs_ref, do_ref,
             dp_ref, dl0_ref, dl1_ref, don_ref, dst_ref):
        @pl.when(pl.program_id(0) == 0)
        def _():
            dst_ref[...] = jnp.zeros_like(dst_ref)
            dl0_ref[...] = jnp.zeros_like(dl0_ref)
            dl1_ref[...] = jnp.zeros_like(dl1_ref)
            don_ref[...] = jnp.zeros_like(don_ref)

        args = ([hs_ref[0, h] for h in range(HA_HEADS)],
                *[_hgrn_blocks(ref, nj, cc) for ref in (q_ref, f_ref, i_ref, g_ref)],
                _hgrn_cols(l0_ref), _hgrn_cols(l1_ref), _hgrn_cols(on_ref))
        _, vjp = jax.vjp(_hgrn_step, *args)
        dst0, dq, df, di, dg, dl0, dl1, don = vjp((_hgrn_blocks(do_ref, nj, cc),
                                                   [dst_ref[h] for h in range(HA_HEADS)]))
        for h in range(HA_HEADS):
            sl = slice(h * HA_DIM, (h + 1) * HA_DIM)
            for k, dv in enumerate((dq, df, di, dg)):
                for j in range(nj):
                    dp_ref[j * cc:(j + 1) * cc, k * W_A + h * HA_DIM:k * W_A + (h + 1) * HA_DIM] = dv[h][j]
            dl0_ref[:, sl] += dl0[h]
            dl1_ref[:, sl] += dl1[h]
            don_ref[:, sl] += don[h]
            dst_ref[h] = dst0[h]

    col = lambda j: pl.BlockSpec((c, W_A), lambda i, j=j: (n - 1 - i, j))
    par = pl.BlockSpec((1, W_A), lambda i: (0, 0))
    return _hosting_call(
        body, comm, name="hgrn_bwd", grid=(n,),
        in_specs=[col(0), col(1), col(2), col(3), par, par, par,
                  pl.BlockSpec((1, HA_HEADS, HA_DIM, HA_DIM), lambda i: (n - 1 - i, 0, 0, 0)),
                  pl.BlockSpec((c, W_A), lambda i: (n - 1 - i, do_col))],
        out_specs=[pl.BlockSpec((c, N_HGRN_COLS), lambda i: (n - 1 - i, 0)), par, par, par],
        out_shape=[SDS((t, N_HGRN_COLS), F32), SDS((1, W_A), F32), SDS((1, W_A), F32), SDS((1, W_A), F32)],
        scratch_shapes=[pltpu.VMEM((HA_HEADS, HA_DIM, HA_DIM), F32)],
        args=(p_h, p_h, p_h, p_h, l0, l1, onorm, hs, do))


HB_PAIRS = HB_HEADS // 2
PAIR_W = 2 * HB_DIM


def _head_lane_masks():
    lane = lax.broadcasted_iota(jnp.int32, (1, PAIR_W), 1)
    return (lane < HB_DIM).astype(F32), (lane >= HB_DIM).astype(F32)


@jax.custom_vjp
def _stack_heads(x):
    m0, m1 = _head_lane_masks()
    return jnp.concatenate([x * m0, x * m1], axis=0)


def _stack_heads_bwd(_, g):
    m0, m1 = _head_lane_masks()
    c = g.shape[0] // 2
    return (g[:c] * m0 + g[c:] * m1,)


_stack_heads.defvjp(lambda x: (_stack_heads(x), None), _stack_heads_bwd)


@jax.custom_vjp
def _unstack_heads(ys):
    c = ys.shape[0] // 2
    return ys[:c] + ys[c:]


_unstack_heads.defvjp(lambda ys: (_unstack_heads(ys), None), lambda _, g: (_stack_heads(g),))


def _same_head_block(c):
    ii = lax.broadcasted_iota(jnp.int32, (2 * c, 2 * c), 0)
    jj = lax.broadcasted_iota(jnp.int32, (2 * c, 2 * c), 1)
    same = (ii < c) == (jj < c)
    return same & (jj <= ii), same & (jj < ii), (ii == jj).astype(F32)


@jax.custom_vjp
def _rows_join(top, bottom):
    return jnp.concatenate([top, bottom], axis=0)


def _rows_join_bwd(n_top, g):
    return g[:n_top], g[n_top:]


_rows_join.defvjp(lambda top, bottom: (_rows_join(top, bottom), top.shape[0]), _rows_join_bwd)


def _rows_split_impl(x, n_top):
    return x[:n_top], x[n_top:]


_rows_split = jax.custom_vjp(_rows_split_impl, nondiff_argnums=(1,))
_rows_split.defvjp(lambda x, n_top: (_rows_split_impl(x, n_top), None),
                   lambda n_top, _, g: (jnp.concatenate([g[0], g[1]], axis=0),))


def _rwkv_step(s0, r, lw, k, v, a, b):
    npair, nj = len(r), len(r[0])
    c = r[0][0].shape[0]
    combos = [(j, p) for j in range(nj) for p in range(npair)]
    every = lambda fn: {q: fn(q) for q in combos}
    at_ = lambda d: (lambda q: d[q[1]][q[0]])
    r_, lw_, k_, v_, a_, b_ = (at_(z) for z in (r, lw, k, v, a, b))
    incl, strict, eye = _same_head_block(c)

    gam = every(lambda q: _cumsum_rows(lw_(q)))
    gtot = every(lambda q: jnp.sum(lw_(q), axis=0, keepdims=True))
    eneg = every(lambda q: jnp.exp(-gam[q]))
    edec = every(lambda q: jnp.exp(gtot[q] - gam[q]))
    at = every(lambda q: _stack_heads(a_(q) * jnp.exp(gam[q] - lw_(q))))
    rt = every(lambda q: _stack_heads(r_(q) * jnp.exp(gam[q])))
    bt = every(lambda q: _stack_heads(b_(q) * eneg[q]))
    kt = every(lambda q: _stack_heads(k_(q) * eneg[q]))
    bdec = every(lambda q: _stack_heads(b_(q) * edec[q]))
    kdec = every(lambda q: _stack_heads(k_(q) * edec[q]))
    vs = every(lambda q: _stack_heads(v_(q)))
    a_ab = every(lambda q: jnp.where(strict, _nt(at[q], bt[q]), 0.0))
    a_ak = every(lambda q: jnp.where(strict, _nt(at[q], kt[q]), 0.0))
    a_rb = every(lambda q: jnp.where(incl, _nt(rt[q], bt[q]), 0.0))
    a_rk = every(lambda q: jnp.where(incl, _nt(rt[q], kt[q]), 0.0))
    tinv = every(lambda q: eye + a_ab[q])
    pw = a_ab
    span = 2
    while span < c:
        pw = every(lambda q, pw=pw: _nn_x3(pw[q], pw[q]))
        tinv = every(lambda q, pw=pw, tinv=tinv: tinv[q] + _nn_x3(pw[q], tinv[q]))
        span *= 2
    akv = every(lambda q: _nn(a_ak[q], vs[q]))
    w1 = every(lambda q: _nn(tinv[q], at[q]))
    u0 = every(lambda q: _nn(tinv[q], akv[q]))
    wr = every(lambda q: _rows_join(w1[q], rt[q]))
    bk = every(lambda q: _rows_join(bdec[q], kdec[q]))
    yv = every(lambda q: _nn(a_rk[q], vs[q]))
    gdec = every(lambda q: jnp.exp(gtot[q]))

    s = list(s0)
    y = [[None] * nj for _ in range(npair)]
    for j in range(nj):
        both = {p: _rows_split(_nt(wr[(j, p)], s[p]), 2 * c) for p in range(npair)}
        u = {p: both[p][0] + u0[(j, p)] for p in range(npair)}
        for p in range(npair):
            y[p][j] = _unstack_heads(both[p][1] + _nn(a_rb[(j, p)], u[p]) + yv[(j, p)])
        s = [s[p] * gdec[(j, p)] + _tn(_rows_join(u[p], vs[(j, p)]), bk[(j, p)]) for p in range(npair)]
    return y, s


def _rwkv_blocks(ref, nj, c):
    return [[ref[j * c:(j + 1) * c, p * PAIR_W:(p + 1) * PAIR_W] for j in range(nj)] for p in range(HB_PAIRS)]


def _rwkv_fwd(seqs, comm=None):
    t = seqs[0].shape[0]
    c, nj = RWKV_CHUNK, RWKV_GROUP
    n = t // (c * nj)

    def body(r_ref, lw_ref, k_ref, v_ref, a_ref, b_ref, y_ref, hs_ref, st_ref):
        @pl.when(pl.program_id(0) == 0)
        def _():
            st_ref[...] = jnp.zeros_like(st_ref)

        hs_ref[0] = st_ref[...]
        s0 = [st_ref[p] for p in range(HB_PAIRS)]
        y, s1 = _rwkv_step(s0, *[_rwkv_blocks(ref, nj, c) for ref in (r_ref, lw_ref, k_ref, v_ref, a_ref, b_ref)])
        for p in range(HB_PAIRS):
            for j in range(nj):
                y_ref[j * c:(j + 1) * c, p * PAIR_W:(p + 1) * PAIR_W] = y[p][j]
            st_ref[p] = s1[p]

    seq = pl.BlockSpec((c * nj, W_B), lambda i: (i, 0))
    return _hosting_call(
        body, comm, name="rwkv_fwd", grid=(n,), in_specs=[seq] * 6,
        out_specs=[seq, pl.BlockSpec((1, HB_PAIRS, PAIR_W, PAIR_W), lambda i: (i, 0, 0, 0))],
        out_shape=[SDS((t, W_B), F32), SDS((n, HB_PAIRS, PAIR_W, PAIR_W), F32)],
        scratch_shapes=[pltpu.VMEM((HB_PAIRS, PAIR_W, PAIR_W), F32)], args=tuple(seqs))


def _rwkv_bwd(seqs, hs, dy, comm=None):
    t = seqs[0].shape[0]
    c, nj = RWKV_CHUNK, RWKV_GROUP
    n = t // (c * nj)

    def body(r_ref, lw_ref, k_ref, v_ref, a_ref, b_ref, hs_ref, dy_ref,
             dr_ref, dlw_ref, dk_ref, dv_ref, da_ref, db_ref, dst_ref):
        @pl.when(pl.program_id(0) == 0)
        def _():
            dst_ref[...] = jnp.zeros_like(dst_ref)

        s0 = [hs_ref[0, p] for p in range(HB_PAIRS)]
        seq_vals = [_rwkv_blocks(ref, nj, c) for ref in (r_ref, lw_ref, k_ref, v_ref, a_ref, b_ref)]
        _, vjp = jax.vjp(_rwkv_step, s0, *seq_vals)
        grads = vjp((_rwkv_blocks(dy_ref, nj, c), [dst_ref[p] for p in range(HB_PAIRS)]))
        for ref, gr in zip((dr_ref, dlw_ref, dk_ref, dv_ref, da_ref, db_ref), grads[1:]):
            for p in range(HB_PAIRS):
                for j in range(nj):
                    ref[j * c:(j + 1) * c, p * PAIR_W:(p + 1) * PAIR_W] = gr[p][j]
        m0, m1 = _head_lane_masks()
        rows0 = (lax.broadcasted_iota(jnp.int32, (PAIR_W, 1), 0) < HB_DIM).astype(F32)
        blocks = rows0 * m0 + (1.0 - rows0) * m1
        for p in range(HB_PAIRS):
            dst_ref[p] = grads[0][p] * blocks

    seq = pl.BlockSpec((c * nj, W_B), lambda i: (n - 1 - i, 0))
    return _hosting_call(
        body, comm, name="rwkv_bwd", grid=(n,),
        in_specs=[seq] * 6 + [pl.BlockSpec((1, HB_PAIRS, PAIR_W, PAIR_W), lambda i: (n - 1 - i, 0, 0, 0)), seq],
        out_specs=[seq] * 6, out_shape=[SDS((t, W_B), F32)] * 6,
        scratch_shapes=[pltpu.VMEM((HB_PAIRS, PAIR_W, PAIR_W), F32)], args=(*seqs, hs, dy))


def _final_loss(x3, fnorm, target, *, tm):
    t, d = x3.shape

    def body(x_ref, g_ref, t_ref, dx_ref, dg_ref, loss_ref):
        @pl.when(pl.program_id(0) == 0)
        def _():
            dg_ref[...] = jnp.zeros_like(dg_ref)
            loss_ref[...] = jnp.zeros_like(loss_ref)

        x, g = x_ref[...], g_ref[...]
        rinv = lax.rsqrt(jnp.mean(x * x, axis=-1, keepdims=True) + NORM_EPS)
        xh = x * rinv
        diff = xh * g - t_ref[...]
        loss_ref[...] += 0.5 * jnp.sum(jnp.mean(diff * diff, axis=-1, keepdims=True))
        dy = diff * (1.0 / d)
        dg_ref[...] += jnp.sum(dy * xh, axis=0, keepdims=True)
        dxh = dy * g
        dx_ref[...] = rinv * (dxh - xh * jnp.mean(dxh * xh, axis=-1, keepdims=True))

    row = pl.BlockSpec((tm, d), lambda i: (i, 0))
    return pl.pallas_call(
        body, name="final_loss", grid=(t // tm,), in_specs=[row, pl.BlockSpec((1, d), lambda i: (0, 0)), row],
        out_specs=[row, pl.BlockSpec((1, d), lambda i: (0, 0)), pl.BlockSpec((8, 128), lambda i: (0, 0))],
        out_shape=[SDS((t, d), F32), SDS((1, d), F32), SDS((8, 128), F32)],
        compiler_params=_params(("arbitrary",)))(x3, fnorm, target)


def _gate_up_act(h, wgt, wut, *, tm, tn, name, comm=None):
    t, d = h.shape
    tm = min(tm, t)

    def body(h_ref, g_ref, u_ref, a_out, u_out, act_out):
        hv = h_ref[...]
        a = _dg(hv, g_ref[...], 1, 1, False)
        u = _dg(hv, u_ref[...], 1, 1, False)
        a_out[...] = a.astype(a_out.dtype)
        u_out[...] = u.astype(u_out.dtype)
        act_out[...] = (_silu(a) * u).astype(act_out.dtype)

    wspec = pl.BlockSpec((tn, d), lambda i, j: (j, 0))
    ospec = pl.BlockSpec((tm, tn), lambda i, j: (i, j))
    return _hosting_call(
        body, comm, name=name, grid=(t // tm, D_FF // tn),
        in_specs=[pl.BlockSpec((tm, d), lambda i, j: (i, 0)), wspec, wspec], out_specs=[ospec, ospec, ospec],
        out_shape=[SDS((t, D_FF), BF16), SDS((t, D_FF), BF16), SDS((t, D_FF), BF16)], scratch_shapes=[],
        args=(h, wgt, wut))


def _dact_swiglu(dout, wd, a, u, *, tm, tn, name, comm=None):
    t, d = dout.shape
    tm = min(tm, t)

    def body(d_ref, w_ref, a_ref, u_ref, da_out, du_out):
        dact = 0.5 * _dg(d_ref[...], w_ref[...], 1, 1, False)
        av, uv = a_ref[...].astype(F32), u_ref[...].astype(F32)
        s = _sigmoid(av)
        da_out[...] = (dact * uv * (s * (1.0 + av * (1.0 - s)))).astype(da_out.dtype)
        du_out[...] = (dact * (av * s)).astype(du_out.dtype)

    tile = pl.BlockSpec((tm, tn), lambda i, j: (i, j))
    return _hosting_call(
        body, comm, name=name, grid=(t // tm, D_FF // tn),
        in_specs=[pl.BlockSpec((tm, d), lambda i, j: (i, 0)), pl.BlockSpec((tn, d), lambda i, j: (j, 0)), tile, tile],
        out_specs=[tile, tile], out_shape=[SDS((t, D_FF), BF16), SDS((t, D_FF), BF16)], scratch_shapes=[],
        args=(dout, wd, a, u))


class _Plan:
    def __init__(self):
        self.entries, self.counts = collections.defaultdict(list), {}

    def carry(self, host, comm_of, after):
        self.entries[host].append((comm_of, after))

    def comm(self, host, g):
        comms = [comm_of(g) for comm_of, _ in self.entries.get(host, [])]
        self.counts[host] = [len(c.arrays) for c in comms]
        return functools.reduce(_join_comms, comms) if comms else None

    def done(self, host, results, w):
        start = 0
        for (_, after), n in zip(self.entries.get(host, []), self.counts.get(host, [])):
            after(results[start:start + n], w)
            start += n


def _ffn_fwd(x, w, tag, plan, g):
    comm = plan.comm(f"{tag}_rms", g)
    res = _rowwise(_rms_f, [x], [w[f"{tag}_norm"]], [[0]], [BF16], tm=512, name=f"{tag}_rms", comm=comm)
    (h,), carried = res if comm is not None else (res, [])
    plan.done(f"{tag}_rms", carried, w)
    (a, u, act), carried = _gate_up_act(h, w[f"{tag}_wgt"], w[f"{tag}_wut"], tm=2048, tn=256, name=f"{tag}_gate_up",
                                        comm=plan.comm(f"{tag}_gate_up", g))
    plan.done(f"{tag}_gate_up", carried, w)
    comm = plan.comm(f"{tag}_down", g)
    out = _mm(act, w[f"{tag}_wd"], tm=1024, tn=D_MODEL, tk=D_FF, name=f"{tag}_down", res=x, scale=0.5, comm=comm)
    if comm is not None:
        out, carried = out
        plan.done(f"{tag}_down", carried, w)
    return out, (h, a, u, act)


def _ffn_bwd(dout, x, w, saved, tag, plan, g):
    h, a, u, act = saved

    def carrying(fn, host, *args, **kwargs):
        comm = plan.comm(host, g)
        res = fn(*args, name=host, comm=comm, **kwargs)
        out, carried = res if comm is not None else (res, [])
        plan.done(host, carried, w)
        return out

    (da, du), carried = _dact_swiglu(dout, w[f"{tag}_wd"], a, u, tm=2048, tn=256, name=f"{tag}_dact",
                                     comm=plan.comm(f"{tag}_dact", g))
    plan.done(f"{tag}_dact", carried, w)
    g[f"{tag}_wd"] = _mm(act, dout, ta=True, tm=D_FF // 2, tn=D_MODEL, tk=1024, name=f"{tag}_dwd", scale=0.5)
    g[f"{tag}_wgt"] = carrying(_mm, f"{tag}_dwg", da, h, ta=True, tm=D_FF // 2, tn=D_MODEL, tk=1024)
    g[f"{tag}_wut"] = carrying(_mm, f"{tag}_dwu", du, h, ta=True, tm=D_FF // 2, tn=D_MODEL, tk=1024)
    dh = carrying(_mm, f"{tag}_dh_g", da, w[f"{tag}_wgt"], tm=1024, tn=D_MODEL, tk=D_FF)
    dh = carrying(_mm, f"{tag}_dh_u", du, w[f"{tag}_wut"], tm=1024, tn=D_MODEL, tk=D_FF, res=dh)
    dx, g[f"{tag}_norm"] = carrying(_rowwise_bwd, f"{tag}_drms", _rms_f, [x], [w[f"{tag}_norm"]], [dh],
                                    x_grad=[True], p_grad=[True], dx_groups=[[0]], dx_dtypes=[F32], tm=512,
                                    extra={0: dout})
    return dx


def _local_step(x, target, w, plan=None):
    plan = plan or _Plan()
    ones_bd = jnp.kron(jnp.eye(HB_HEADS, dtype=F32), jnp.ones((HB_DIM, HB_DIM), F32))
    g = {}
    x1, ffn1_saved = _ffn_fwd(x, w, "ffn1", plan, g)
    hm, = _rowwise(_rms_f, [x1], [w["mix_norm"]], [[0]], [BF16], tm=512, name="mix_rms")
    p_h = _mm(hm, w["w_in_h"], tm=2048, tn=256, tk=D_MODEL, name="inproj_h")
    p_r = _mm(hm, w["w_in_r"], tm=2048, tn=256, tk=D_MODEL, name="inproj_r")
    o_a, hgrn_states = _hgrn_fwd(p_h, w["lb0"], w["lb1"], w["hgrn_out_norm"])

    mu = w["mu_pad"]
    prep_xs = [(p_r, W_B, 0), (p_r, W_B, 1), (p_r, W_B, 2), (p_r, LORA_PAD, 6),
               ("prev", p_r, W_B, 0), ("prev", p_r, W_B, 1), ("prev", p_r, W_B, 2), ("prev", p_r, LORA_PAD, 6)]
    prep_ps = [(mu, W_B, 0), (mu, W_B, 1), (mu, W_B, 2), (mu, LORA_PAD, 6), w["rwkv_w0"], w["w2_pad"], w["rwkv_a0"],
               w["a2_pad"], w["g2_pad"], w["rwkv_k_k"], w["rwkv_k_a"], ones_bd]
    prep_f = _rwkv_prep_f
    r, lw, k2, v, a_vec, b_vec, gate = _rowwise(prep_f, prep_xs, prep_ps, [[0], [1], [2], [3], [4], [5], [6]],
                                                [F32] * 7, tm=256, name="rwkv_prep")
    seqs = [r, lw, k2, v, a_vec, b_vec]
    (y, rwkv_states), carried = _rwkv_fwd(seqs, comm=plan.comm("rwkv_fwd", g))
    plan.done("rwkv_fwd", carried, w)
    post_f = _rwkv_post_f
    post_xs = [y, r, k2, v, gate]
    post_ps = [w["rwkv_r_k"], w["rwkv_gn_w"], w["rwkv_gn_b"], ones_bd]
    o, = _rowwise(lambda o_a_, *rest: (o_a_,) + tuple(post_f(*rest)), [o_a] + post_xs, post_ps, [[0, 1]], [F32],
                  tm=256, name="rwkv_post")
    x2 = _mm(o, w["w_out"], tm=2048, tn=256, tk=D_MODEL, name="outproj", res=x1)
    x3, ffn2_saved = _ffn_fwd(x2, w, "ffn2", plan, g)
    dx3, g["final_norm"], loss = _final_loss(x3, w["final_norm"], target, tm=256)

    dx2 = _ffn_bwd(dx3, x2, w, ffn2_saved, "ffn2", plan, g)
    do = _mm(dx2, w["w_out"], tb=True, tm=2048, tn=256, tk=D_MODEL, name="outproj_do")
    g["w_out"] = _mm(o, dx2, ta=True, tm=D_MODEL, tn=D_MODEL, tk=1024, name="outproj_dw")

    (dp_h, g["lb0"], g["lb1"], g["hgrn_out_norm"]), carried = _hgrn_bwd(
        p_h, w["lb0"], w["lb1"], w["hgrn_out_norm"], hgrn_states, do, 0, comm=plan.comm("hgrn_bwd", g))
    plan.done("hgrn_bwd", carried, w)
    post_out = _rowwise_bwd(post_f, post_xs, post_ps, [(do, W_B, 1)], x_grad=[True] * 5, p_grad=[True] * 3 + [False],
                            dx_groups=[[0], [1], [2], [3], [4]], dx_dtypes=[F32] * 5, tm=256, name="rwkv_post_bwd")
    dy, dr1, dk1, dv1, dgate, g["rwkv_r_k"], g["rwkv_gn_w"], g["rwkv_gn_b"] = post_out
    (dr2, dlw, dk2, dv2, da_vec, db_vec), carried = _rwkv_bwd(seqs, rwkv_states, dy, comm=plan.comm("rwkv_bwd", g))
    plan.done("rwkv_bwd", carried, w)

    def prep2_f(*vals):
        r_, lw_, k2_, v_, a_, b_, g_ = prep_f(*vals)
        return r_, lw_, k2_, v_, a_, b_, g_, r_, k2_, v_

    prep_comm = plan.comm("rwkv_prep_bwd", g)
    prep_out = _rowwise_bwd(prep2_f, prep_xs, prep_ps, [dr2, dlw, dk2, dv2, da_vec, db_vec, dgate, dr1, dk1, dv1],
                            x_grad=[True] * 8, p_grad=[True] * 11 + [False], dx_groups=[[0, 1, 2, 3], [4, 5, 6, 7]],
                            dx_dtypes=[F32], tm=256, name="rwkv_prep_bwd", fold_next=(0, 1), comm=prep_comm)
    prep_out, carried = prep_out if prep_comm is not None else (prep_out, [])
    plan.done("rwkv_prep_bwd", carried, w)
    dp_r = prep_out[0]
    (dmu_r, dmu_k, dmu_v, dmu_lo, g["rwkv_w0"], g["w2_pad"], g["rwkv_a0"], g["a2_pad"], g["g2_pad"],
     g["rwkv_k_k"], g["rwkv_k_a"]) = prep_out[1:]
    g["mu_pad"] = jnp.concatenate([dmu_r, dmu_k, dmu_v, dmu_lo], axis=1)
    dhm = _mm(dp_h, w["w_in_h"], tb=True, tm=1024, tn=D_MODEL, tk=N_HGRN_COLS, name="inproj_dh_h")
    dhm = _mm(dp_r, w["w_in_r"], tb=True, tm=1024, tn=D_MODEL, tk=N_RWKV_PAD, name="inproj_dh_r", res=dhm)
    g["w_in_h"] = _mm(hm, dp_h, ta=True, tm=D_MODEL, tn=D_MODEL, tk=1024, name="inproj_dw_h")
    g["w_in_r"] = _mm(hm, dp_r, ta=True, tm=D_MODEL, tn=N_RWKV_PAD // 2, tk=1024, name="inproj_dw_r")
    mix_comm = plan.comm("mix_drms", g)
    mix_out = _rowwise_bwd(_rms_f, [x1], [w["mix_norm"]], [dhm], x_grad=[True], p_grad=[True], dx_groups=[[0]],
                           dx_dtypes=[F32], tm=512, name="mix_drms", extra={0: dx2}, comm=mix_comm)
    (dx1, g["mix_norm"]), carried = mix_out if mix_comm is not None else (mix_out, [])
    plan.done("mix_drms", carried, w)
    dx0 = _ffn_bwd(dx1, x, w, ffn1_saved, "ffn1", plan, g)
    return loss, dx0, g


HBM_SPEC = pl.BlockSpec(memory_space=pl.ANY)

Comm = collections.namedtuple("Comm", "arrays out_shapes aliased sem_shapes start finish")


def _join_comms(first, second):
    assert first.aliased == second.aliased
    n, s = len(first.arrays), len(first.sem_shapes)

    def start(ins, outs, sems):
        first.start(ins[:n], outs[:n], sems[:s])
        second.start(ins[n:], outs[n:], sems[s:])

    def finish(ins, outs, sems):
        first.finish(ins[:n], outs[:n], sems[:s])
        second.finish(ins[n:], outs[n:], sems[s:])

    return Comm(list(first.arrays) + list(second.arrays), list(first.out_shapes) + list(second.out_shapes),
                first.aliased, list(first.sem_shapes) + list(second.sem_shapes), start, finish)


def _run_comm(comm, name):
    n = len(comm.arrays)

    def body(*refs):
        ins, outs, sems = refs[:n], refs[n:2 * n], refs[2 * n:]
        comm.start(ins, outs, sems)
        comm.finish(ins, outs, sems)

    return pl.pallas_call(
        body, name=name, in_specs=[HBM_SPEC] * n, out_specs=[HBM_SPEC] * n, out_shape=list(comm.out_shapes),
        input_output_aliases={t: t for t in range(n)} if comm.aliased else {},
        scratch_shapes=list(comm.sem_shapes))(*comm.arrays)


def _hosting_call(body, comm, *, name, grid, in_specs, out_specs, out_shape, scratch_shapes, args):
    sem = ("arbitrary",) * len(grid)
    if comm is None:
        res = pl.pallas_call(body, name=name, grid=grid, in_specs=in_specs, out_specs=out_specs, out_shape=out_shape,
                             scratch_shapes=scratch_shapes, compiler_params=_params(sem))(*args)
        return list(res), []
    ni, no, ns, nc = len(in_specs), len(out_specs), len(scratch_shapes), len(comm.arrays)

    def wrapped(*refs):
        ins, cins = refs[:ni], refs[ni:ni + nc]
        outs, couts = refs[ni + nc:ni + nc + no], refs[ni + nc + no:ni + 2 * nc + no]
        scr, sems = refs[ni + 2 * nc + no:ni + 2 * nc + no + ns], refs[ni + 2 * nc + no + ns:]
        first = functools.reduce(jnp.logical_and, [pl.program_id(k) == 0 for k in range(len(grid))])
        last = functools.reduce(jnp.logical_and, [pl.program_id(k) == grid[k] - 1 for k in range(len(grid))])

        @pl.when(first)
        def _():
            comm.start(cins, couts, sems)

        body(*ins, *outs, *scr)

        @pl.when(last)
        def _():
            comm.finish(cins, couts, sems)

    res = pl.pallas_call(
        wrapped, name=name, grid=grid, in_specs=list(in_specs) + [HBM_SPEC] * nc,
        out_specs=list(out_specs) + [HBM_SPEC] * nc, out_shape=list(out_shape) + list(comm.out_shapes),
        scratch_shapes=list(scratch_shapes) + list(comm.sem_shapes),
        input_output_aliases={ni + t: no + t for t in range(nc)} if comm.aliased else {},
        compiler_params=_params(sem))(*args, *comm.arrays)
    return list(res[:no]), list(res[no:])


def _chips(x, y):
    return [(1 - x, y), (x, 1 - y), (1 - x, 1 - y)]


def _gather_comm(bufs):
    n = len(bufs)

    def copies(outs, sems):
        ici_send, ici_recv, d2d_send, d2d_recv = sems
        x, y, c = lax.axis_index("x"), lax.axis_index("y"), lax.axis_index("c")

        def half(t, slot, hc):
            hr = bufs[t].shape[1] // 2
            return outs[t].at[slot, pl.ds(pl.multiple_of(hc * hr, 16), hr), :]

        def ici(t, j, slot, px, py):
            return pltpu.make_async_remote_copy(src_ref=half(t, slot, c), dst_ref=half(t, slot, c),
                                                send_sem=ici_send.at[3 * t + j], recv_sem=ici_recv.at[3 * t + j],
                                                device_id=(px, py, c), device_id_type=MESH)

        def d2d(t, j, slot, hc):
            return pltpu.make_async_remote_copy(src_ref=half(t, slot, hc), dst_ref=half(t, slot, hc),
                                                send_sem=d2d_send.at[3 * t + j], recv_sem=d2d_recv.at[3 * t + j],
                                                device_id=(x, y, 1 - c), device_id_type=MESH)

        peers = [(t, j, px, py) for t in range(n) for j, (px, py) in enumerate(_chips(x, y))]
        return ici, d2d, peers, 2 * x + y, c

    def start(ins, outs, sems):
        ici, _, peers, me, _ = copies(outs, sems)
        for t, j, px, py in peers:
            ici(t, j, me, px, py).start()

    def finish(ins, outs, sems):
        ici, d2d, peers, me, c = copies(outs, sems)
        for t, j, px, py in peers:
            ici(t, j, 2 * px + py, px, py).wait_recv()
            d2d(t, j, 2 * px + py, c).start()
        for t, j, px, py in peers:
            d2d(t, j, 2 * px + py, 1 - c).wait_recv()
        for t, j, px, py in peers:
            ici(t, j, me, px, py).wait_send()
            d2d(t, j, 2 * px + py, c).wait_send()

    return Comm(list(bufs), [SDS(b.shape, b.dtype) for b in bufs], True, [pltpu.SemaphoreType.DMA((3 * n,))] * 4,
                start, finish)


def _sibling_exchange_comm(gs):
    n = len(gs)

    def copies(ins, outs, sems):
        x, y, c = lax.axis_index("x"), lax.axis_index("y"), lax.axis_index("c")
        cps = []
        for t in range(n):
            hr = gs[t].shape[1] // 2
            src = ins[t].at[:, pl.ds(pl.multiple_of((1 - c) * hr, SUBLANES), hr), :]
            cps.append(pltpu.make_async_remote_copy(src_ref=src, dst_ref=outs[t], send_sem=sems[0].at[t],
                                                    recv_sem=sems[1].at[t], device_id=(x, y, 1 - c),
                                                    device_id_type=MESH))
        return cps

    def start(ins, outs, sems):
        for cp in copies(ins, outs, sems):
            cp.start()

    def finish(ins, outs, sems):
        for cp in copies(ins, outs, sems):
            cp.wait()

    return Comm(list(gs), [SDS((N_CHIPS, g.shape[1] // 2, g.shape[2]), g.dtype) for g in gs], False,
                [pltpu.SemaphoreType.DMA((n,))] * 2, start, finish)


def _chip_exchange_comm(ss):
    n = len(ss)

    def copies(ins, outs, sems):
        x, y, c = lax.axis_index("x"), lax.axis_index("y"), lax.axis_index("c")
        me = 2 * x + y

        def copy(t, j, px, py, src_slot, dst_slot):
            return pltpu.make_async_remote_copy(src_ref=ins[t].at[src_slot], dst_ref=outs[t].at[dst_slot],
                                                send_sem=sems[0].at[3 * t + j], recv_sem=sems[1].at[3 * t + j],
                                                device_id=(px, py, c), device_id_type=MESH)

        peers = [(t, j, px, py) for t in range(n) for j, (px, py) in enumerate(_chips(x, y))]
        return copy, peers, me

    def start(ins, outs, sems):
        copy, peers, me = copies(ins, outs, sems)
        for t, j, px, py in peers:
            copy(t, j, px, py, 2 * px + py, me).start()

    def finish(ins, outs, sems):
        copy, peers, me = copies(ins, outs, sems)
        for t, j, px, py in peers:
            copy(t, j, px, py, me, 2 * px + py).wait_recv()
        for t, j, px, py in peers:
            copy(t, j, px, py, 2 * px + py, me).wait_send()

    return Comm(list(ss), [SDS(s.shape, s.dtype) for s in ss], False, [pltpu.SemaphoreType.DMA((3 * n,))] * 2,
                start, finish)


def _sibling_swap_comm(fs):
    n = len(fs)

    def copies(ins, outs, sems):
        x, y, c = lax.axis_index("x"), lax.axis_index("y"), lax.axis_index("c")
        return [pltpu.make_async_remote_copy(src_ref=ins[t], dst_ref=outs[t], send_sem=sems[0].at[t],
                                             recv_sem=sems[1].at[t], device_id=(x, y, 1 - c), device_id_type=MESH)
                for t in range(n)]

    def start(ins, outs, sems):
        for cp in copies(ins, outs, sems):
            cp.start()

    def finish(ins, outs, sems):
        for cp in copies(ins, outs, sems):
            cp.wait()

    return Comm(list(fs), [SDS(f.shape, f.dtype) for f in fs], False, [pltpu.SemaphoreType.DMA((n,))] * 2,
                start, finish)


def _row_tile(rows, cap=512):
    best = SUBLANES
    for tr in range(SUBLANES, min(rows, cap) + 1, SUBLANES):
        if rows % tr == 0:
            best = tr
    return best


def _add_halves(g4, r4, c_idx, name):
    _, hr, lanes = r4.shape
    tr = _row_tile(hr)
    nb = hr // tr

    def body(c_ref, a_ref, b_ref, o_ref):
        o_ref[...] = (a_ref[...] + b_ref[...]).astype(o_ref.dtype)

    grid_spec = pltpu.PrefetchScalarGridSpec(
        num_scalar_prefetch=1, grid=(N_CHIPS, nb),
        in_specs=[pl.BlockSpec((None, tr, lanes), lambda q, i, c_ref: (q, c_ref[0] * nb + i, 0)),
                  pl.BlockSpec((None, tr, lanes), lambda q, i, c_ref: (q, i, 0))],
        out_specs=pl.BlockSpec((None, tr, lanes), lambda q, i, c_ref: (q, i, 0)))
    return pl.pallas_call(body, name=name, grid_spec=grid_spec, out_shape=SDS(r4.shape, BF16),
                          compiler_params=_params(("parallel", "parallel")))(c_idx, g4, r4)


def _sum_chips(r4, s4, me_idx, name):
    _, rows, lanes = r4.shape
    tr = _row_tile(rows)

    def body(me_ref, a_ref, b_ref, c_ref, d_ref, own_ref, o_ref):
        own = own_ref[...].astype(F32)
        p = [jnp.where(me_ref[0] == q, own, ref[...].astype(F32)) for q, ref in enumerate((a_ref, b_ref, c_ref, d_ref))]
        o_ref[...] = ((p[0] + p[1]) + p[2]) + p[3]

    other = lambda q: (lambda i, me_ref: (jnp.where(me_ref[0] == q, (q + 1) % N_CHIPS, q), i, 0))
    grid_spec = pltpu.PrefetchScalarGridSpec(
        num_scalar_prefetch=1, grid=(rows // tr,),
        in_specs=[pl.BlockSpec((None, tr, lanes), other(q)) for q in range(N_CHIPS)]
        + [pl.BlockSpec((None, tr, lanes), lambda i, me_ref: (me_ref[0], i, 0))],
        out_specs=pl.BlockSpec((tr, lanes), lambda i, me_ref: (i, 0)))
    return pl.pallas_call(body, name=name, grid_spec=grid_spec, out_shape=SDS((rows, lanes), F32),
                          compiler_params=_params(("parallel",)))(me_idx, r4, r4, r4, r4, s4)


def _adamw(wf, g_own, g_other, mf, vf, c_idx, name):
    rows, lanes = wf.shape
    hr = rows // 2
    tr = _row_tile(hr)
    nb = hr // tr
    c1 = 1.0 / (1.0 - ADAM_B1 ** ADAM_STEP)
    c2 = 1.0 / (1.0 - ADAM_B2 ** ADAM_STEP)

    def body(c_ref, w_ref, go_ref, gx_ref, m_ref, v_ref, g_ref, d_ref, nm_ref, nv_ref):
        gv = jnp.where(pl.program_id(0) == c_ref[0], go_ref[...], gx_ref[...])
        m = ADAM_B1 * m_ref[...] + (1.0 - ADAM_B1) * gv
        v = ADAM_B2 * v_ref[...] + (1.0 - ADAM_B2) * (gv * gv)
        g_ref[...] = gv
        d_ref[...] = -ADAM_LR * ((m * c1) / (jnp.sqrt(v * c2) + ADAM_EPS) + ADAM_WD * w_ref[...])
        nm_ref[...] = m
        nv_ref[...] = v

    full = pl.BlockSpec((tr, lanes), lambda h, i, c_ref: (h * nb + i, 0))
    half = pl.BlockSpec((tr, lanes), lambda h, i, c_ref: (i, 0))
    grid_spec = pltpu.PrefetchScalarGridSpec(num_scalar_prefetch=1, grid=(2, nb),
                                             in_specs=[full, half, half, full, full], out_specs=[full] * 4)
    return pl.pallas_call(body, name=name, grid_spec=grid_spec, out_shape=[SDS((rows, lanes), F32)] * 4,
                          compiler_params=_params(("parallel", "parallel")))(c_idx, wf, g_own, g_other, mf, vf)


BIG = ("ffn1_w_gate", "ffn1_w_up", "ffn1_w_down", "ffn2_w_gate", "ffn2_w_up", "ffn2_w_down", "w_out", "w_in")
TRANSPOSED = ("ffn1_w_gate", "ffn1_w_up", "ffn2_w_gate", "ffn2_w_up")
PACKED = ("rwkv_w2", "rwkv_a2", "rwkv_g2")
SMALL_SHAPES = {"ffn1_norm": (1, D_MODEL), "mix_norm": (1, D_MODEL), "hgrn_lb_logits": (2, W_A),
                "hgrn_out_norm": (1, W_A), "rwkv_shift_mu": (1, N_RWKV_COLS), "rwkv_w0": (1, W_B),
                "rwkv_a0": (1, W_B), "rwkv_k_k": (1, W_B), "rwkv_k_a": (1, W_B),
                "rwkv_r_k": (1, HB_HEADS, HB_DIM), "rwkv_gn_w": (1, W_B), "rwkv_gn_b": (1, W_B),
                "ffn2_norm": (1, D_MODEL), "final_norm": (D_MODEL,)}
PACK_ELEMS = sum(_numel(_shard_shape(n)) for n in PACKED) + sum(_numel(SMALL_SHAPES[n]) for n in SMALL)
PACK_ROWS = -(-PACK_ELEMS // (32 * LANES)) * 32


def _to_rows(name, shard):
    return shard[0].T if name in TRANSPOSED else shard[0]


def _from_rows(name, rows):
    return (rows.T if name in TRANSPOSED else rows)[None]


def _pack(sharded, small):
    flat = jnp.concatenate([sharded[n].reshape(-1) for n in PACKED] + [small[n].reshape(-1) for n in SMALL])
    return jnp.pad(flat, (0, PACK_ROWS * LANES - flat.shape[0])).reshape(PACK_ROWS, LANES)


def _unpack(packed):
    flat, out, off = packed.reshape(-1), {}, 0
    for n in PACKED:
        shp = _shard_shape(n)
        out[n] = flat[off:off + _numel(shp)].reshape((1,) + shp)
        off += _numel(shp)
    for n in SMALL:
        shp = SMALL_SHAPES[n]
        out[n] = flat[off:off + _numel(shp)].reshape(shp)
        off += _numel(shp)
    return out


def _quarter(full, name, q):
    shape, ax = SHARDED_SHAPES[name]
    w = shape[ax] // N_CHIPS
    return lax.slice_in_dim(full, q * w, (q + 1) * w, axis=ax)


def kernel(x, ffn1_norm, ffn1_w_gate, ffn1_w_up, ffn1_w_down, mix_norm, w_in, hgrn_lb_logits, hgrn_out_norm, rwkv_shift_mu, rwkv_w0, rwkv_w2, rwkv_a0, rwkv_a2, rwkv_g2, rwkv_k_k, rwkv_k_a, rwkv_r_k, rwkv_gn_w, rwkv_gn_b, w_out, ffn2_norm, ffn2_w_gate, ffn2_w_up, ffn2_w_down, final_norm, loss_target, m_ffn1_norm, m_ffn1_w_gate, m_ffn1_w_up, m_ffn1_w_down, m_mix_norm, m_w_in, m_hgrn_lb_logits, m_hgrn_out_norm, m_rwkv_shift_mu, m_rwkv_w0, m_rwkv_w2, m_rwkv_a0, m_rwkv_a2, m_rwkv_g2, m_rwkv_k_k, m_rwkv_k_a, m_rwkv_r_k, m_rwkv_gn_w, m_rwkv_gn_b, m_w_out, m_ffn2_norm, m_ffn2_w_gate, m_ffn2_w_up, m_ffn2_w_down, m_final_norm, v_ffn1_norm, v_ffn1_w_gate, v_ffn1_w_up, v_ffn1_w_down, v_mix_norm, v_w_in, v_hgrn_lb_logits, v_hgrn_out_norm, v_rwkv_shift_mu, v_rwkv_w0, v_rwkv_w2, v_rwkv_a0, v_rwkv_a2, v_rwkv_g2, v_rwkv_k_k, v_rwkv_k_a, v_rwkv_r_k, v_rwkv_gn_w, v_rwkv_gn_b, v_w_out, v_ffn2_norm, v_ffn2_w_gate, v_ffn2_w_up, v_ffn2_w_down, v_final_norm):
    args = dict(locals())
    wts = {n: args[n] for n in ALL_WEIGHTS}
    moms = {n: args["m_" + n] for n in ALL_WEIGHTS}
    vars_ = {n: args["v_" + n] for n in ALL_WEIGHTS}

    me = 2 * lax.axis_index("x") + lax.axis_index("y")
    c_idx = lax.axis_index("c").astype(jnp.int32).reshape(1)
    me_idx = me.astype(jnp.int32).reshape(1)
    shard_of = {n: _to_rows(n, wts[n]).astype(BF16) for n in BIG}
    shard_of["packed"] = _pack(wts, {n: wts[n] for n in SMALL}).astype(BF16)
    group = {"ffn1": BIG[0:3], "ffn2": BIG[3:6]}

    def slot_bufs(names):
        return [lax.dynamic_update_slice(jnp.zeros((N_CHIPS,) + shard_of[n].shape, BF16), shard_of[n][None],
                                         (me, 0, 0)) for n in names]

    def ffn_weights(tag, gathered):
        return {f"{tag}_wgt": gathered[0].reshape(D_FF, D_MODEL), f"{tag}_wut": gathered[1].reshape(D_FF, D_MODEL),
                f"{tag}_wd": gathered[2].reshape(D_FF, D_MODEL)}

    def w_in_weights(gathered):
        w_in_full = jnp.concatenate([gathered[0][q] for q in range(N_CHIPS)], axis=1)
        return {"w_in_h": w_in_full[:, :N_HGRN_COLS],
                "w_in_r": jnp.pad(w_in_full[:, N_HGRN_COLS:], ((0, 0), (0, N_RWKV_PAD - N_RWKV_COLS)))}

    def mixer_weights(gathered):
        w_out_full = gathered[0].reshape(D_MODEL, D_MODEL)
        packs = gathered[1].reshape(N_CHIPS, PACK_ROWS * LANES)
        full, off = {}, 0
        for n in PACKED:
            shp = _shard_shape(n)
            full[n] = jnp.concatenate([packs[q, off:off + _numel(shp)].reshape(shp) for q in range(N_CHIPS)], axis=1)
            off += _numel(shp)
        zrow = lambda nrow: jnp.zeros((nrow, W_B), BF16)
        return {"w_out": w_out_full,
                "w2_pad": jnp.concatenate([full["rwkv_w2"], zrow(LORA_PAD - 32)], axis=0),
                "a2_pad": jnp.concatenate([zrow(32), full["rwkv_a2"], zrow(LORA_PAD - 64)], axis=0),
                "g2_pad": jnp.concatenate([zrow(64), full["rwkv_g2"], zrow(LORA_PAD - 160)], axis=0)}

    plan = _Plan()
    w = {}
    plan.carry("ffn1_rms", lambda g: _gather_comm(slot_bufs(group["ffn1"][:2])),
               lambda res, w_: w_.update({"ffn1_wgt": res[0].reshape(D_FF, D_MODEL),
                                          "ffn1_wut": res[1].reshape(D_FF, D_MODEL)}))

    def after_gate_up(res, w_):
        w_["ffn1_wd"] = res[0].reshape(D_FF, D_MODEL)
        w_.update(w_in_weights(res[1:]))

    plan.carry("ffn1_gate_up", lambda g: _gather_comm(slot_bufs(("ffn1_w_down", "w_in"))), after_gate_up)
    plan.carry("ffn1_down", lambda g: _gather_comm(slot_bufs(("w_out", "packed"))),
               lambda res, w_: w_.update(mixer_weights(res)))
    plan.carry("rwkv_fwd", lambda g: _gather_comm(slot_bufs(group["ffn2"])),
               lambda res, w_: w_.update(ffn_weights("ffn2", res)))
    w["ffn1_norm"], w["ffn2_norm"] = ffn1_norm, ffn2_norm
    w["mix_norm"] = mix_norm
    w["lb0"], w["lb1"] = hgrn_lb_logits[0:1], hgrn_lb_logits[1:2]
    w["hgrn_out_norm"] = hgrn_out_norm
    w["mu_pad"] = jnp.pad(rwkv_shift_mu, ((0, 0), (0, N_RWKV_PAD - N_RWKV_COLS)))
    for n in ("rwkv_w0", "rwkv_a0", "rwkv_k_k", "rwkv_k_a", "rwkv_gn_w", "rwkv_gn_b"):
        w[n] = wts[n]
    w["rwkv_r_k"] = rwkv_r_k.reshape(1, W_B)
    w["final_norm"] = final_norm.reshape(1, D_MODEL)

    def reduce_rows(names, gs):
        r1 = _run_comm(_sibling_exchange_comm(gs), "grad_sibling_exchange")
        s4 = [_add_halves(gt, rt, c_idx, f"grad_add_halves_{n}") for gt, rt, n in zip(gs, r1, names)]
        r2 = _run_comm(_chip_exchange_comm(s4), "grad_chip_exchange")
        return [_sum_chips(rt, st, me_idx, f"grad_sum_chips_{n}") for rt, st, n in zip(r2, s4, names)]

    early, swapped = {}, {}

    def reduce_early(names, grads_of, sibling_host, chips_host, swap_host):
        def sibling_comm(g):
            early[names, "gs"] = grads_of(g)
            return _sibling_exchange_comm(early[names, "gs"])

        def after_sibling(res, w_):
            early[names, "s4"] = [_add_halves(gt, rt, c_idx, f"grad_add_halves_{n}")
                                  for gt, rt, n in zip(early[names, "gs"], res, names)]

        def after_chips(res, w_):
            early.update(zip(names, [_sum_chips(rt, st, me_idx, f"grad_sum_chips_{n}")
                                     for rt, st, n in zip(res, early[names, "s4"], names)]))

        plan.carry(sibling_host, sibling_comm, after_sibling)
        plan.carry(chips_host, lambda g: _chip_exchange_comm(early[names, "s4"]), after_chips)
        plan.carry(swap_host, lambda g: _sibling_swap_comm([early[n] for n in names]),
                   lambda res, w_: swapped.update(zip(names, res)))

    def proj_grads(g):
        g_w_in = jnp.concatenate([g["w_in_h"], g["w_in_r"][:, :N_RWKV_COLS]], axis=1)
        return [g["w_out"].reshape(N_CHIPS, -1, D_MODEL),
                jnp.stack([_quarter(g_w_in, "w_in", q) for q in range(N_CHIPS)])]

    rows_of = lambda keys: (lambda g: [g[k].reshape(N_CHIPS, -1, D_MODEL) for k in keys])
    reduce_early(group["ffn2"], rows_of(("ffn2_wgt", "ffn2_wut", "ffn2_wd")), "hgrn_bwd", "rwkv_bwd", "rwkv_prep_bwd")
    reduce_early(("w_out", "w_in"), proj_grads, "mix_drms", "ffn1_dact", "ffn1_dwg")
    reduce_early(("ffn1_w_down",), rows_of(("ffn1_wd",)), "ffn1_dwg", "ffn1_dwu", "ffn1_dh_g")
    reduce_early(("ffn1_w_gate",), rows_of(("ffn1_wgt",)), "ffn1_dwu", "ffn1_dh_g", "ffn1_dh_u")
    reduce_early(("ffn1_w_up",), rows_of(("ffn1_wut",)), "ffn1_dh_g", "ffn1_dh_u", "ffn1_drms")
    loss_slab, grad_x, g = _local_step(x[0], loss_target[0], w, plan)
    loss = lax.psum(loss_slab[0, 0], ("x", "y", "c"))

    gfull = {
        "rwkv_w2": g["w2_pad"][0:32], "rwkv_a2": g["a2_pad"][32:64], "rwkv_g2": g["g2_pad"][64:160],
    }
    gsmall = {
        "ffn1_norm": g["ffn1_norm"], "mix_norm": g["mix_norm"],
        "hgrn_lb_logits": jnp.concatenate([g["lb0"], g["lb1"]], axis=0), "hgrn_out_norm": g["hgrn_out_norm"],
        "rwkv_shift_mu": g["mu_pad"][:, :N_RWKV_COLS], "rwkv_w0": g["rwkv_w0"], "rwkv_a0": g["rwkv_a0"],
        "rwkv_k_k": g["rwkv_k_k"], "rwkv_k_a": g["rwkv_k_a"], "rwkv_r_k": g["rwkv_r_k"],
        "rwkv_gn_w": g["rwkv_gn_w"], "rwkv_gn_b": g["rwkv_gn_b"], "ffn2_norm": g["ffn2_norm"],
        "final_norm": g["final_norm"],
    }
    packed = jnp.stack([_pack({n: _quarter(gfull[n], n, q) for n in PACKED}, gsmall) for q in range(N_CHIPS)])
    early["packed"], = reduce_rows(["packed"], [packed])
    swapped["packed"], = _run_comm(_sibling_swap_comm([early["packed"]]), "grad_sibling_swap")
    names = list(BIG) + ["packed"]
    own, other = [early[n] for n in names], [swapped[n] for n in names]

    def rows_list(d):
        return [_to_rows(n, d[n]) for n in BIG] + [_pack(d, {n: d[n] for n in SMALL})]

    outs = [_adamw(wt, go, gx, mt, vt, c_idx, f"adamw_{n}")
            for wt, go, gx, mt, vt, n in zip(rows_list(wts), own, other, rows_list(moms), rows_list(vars_), names)]
    results = []
    for k in range(4):
        per = [outs[i][k] for i in range(len(names))]
        d = {n: _from_rows(n, z) for n, z in zip(BIG, per[:-1])}
        d.update(_unpack(per[-1]))
        results.append(d)
    return (loss, grad_x[None], *[r[n] for r in results for n in ALL_WEIGHTS])
```

```python
import collections
import functools

import jax
import jax.numpy as jnp
from jax import lax
from jax.experimental import pallas as pl
from jax.experimental.pallas import tpu as pltpu

F32 = jnp.float32
BF16 = jnp.bfloat16
SDS = jax.ShapeDtypeStruct
MESH = pl.DeviceIdType.MESH

D_MODEL = 1024
D_FF = 2816
W_A = 512
W_B = 512
HA_HEADS, HA_DIM = 4, 128
HB_HEADS, HB_DIM = 8, 64
HGRN_CHUNK = 64
HGRN_GROUP = 8
RWKV_CHUNK = 16
RWKV_GROUP = 8
N_HGRN_COLS = 4 * W_A
N_RWKV_COLS = 3 * W_B + 32 + 32 + 96
N_RWKV_PAD = 1792
LORA_PAD = 256
NORM_EPS = 1e-6
RWKV_GN_EPS = 64e-5
L2_EPS = 1e-12
ADAM_LR, ADAM_B1, ADAM_B2, ADAM_EPS, ADAM_WD, ADAM_STEP = 0.001, 0.9, 0.999, 1e-8, 0.01, 10

N_CHIPS = 4
VMEM_LIMIT_V7X = 56 * 1024 * 1024
LANES = 1024

SHARDED_SHAPES = {
    "ffn1_w_gate": ((D_MODEL, D_FF), 1), "ffn1_w_up": ((D_MODEL, D_FF), 1), "ffn1_w_down": ((D_FF, D_MODEL), 0),
    "w_in": ((D_MODEL, N_HGRN_COLS + N_RWKV_COLS), 1), "rwkv_w2": ((32, W_B), 1), "rwkv_a2": ((32, W_B), 1),
    "rwkv_g2": ((96, W_B), 1), "w_out": ((D_MODEL, D_MODEL), 0),
    "ffn2_w_gate": ((D_MODEL, D_FF), 1), "ffn2_w_up": ((D_MODEL, D_FF), 1), "ffn2_w_down": ((D_FF, D_MODEL), 0),
}
SMALL = ("ffn1_norm", "mix_norm", "hgrn_lb_logits", "hgrn_out_norm", "rwkv_shift_mu", "rwkv_w0", "rwkv_a0",
         "rwkv_k_k", "rwkv_k_a", "rwkv_r_k", "rwkv_gn_w", "rwkv_gn_b", "ffn2_norm", "final_norm")
ALL_WEIGHTS = ("ffn1_norm", "ffn1_w_gate", "ffn1_w_up", "ffn1_w_down", "mix_norm", "w_in", "hgrn_lb_logits",
               "hgrn_out_norm", "rwkv_shift_mu", "rwkv_w0", "rwkv_w2", "rwkv_a0", "rwkv_a2", "rwkv_g2", "rwkv_k_k",
               "rwkv_k_a", "rwkv_r_k", "rwkv_gn_w", "rwkv_gn_b", "w_out", "ffn2_norm", "ffn2_w_gate", "ffn2_w_up",
               "ffn2_w_down", "final_norm")


def _shard_shape(name):
    shape, ax = SHARDED_SHAPES[name]
    return tuple(s // N_CHIPS if i == ax else s for i, s in enumerate(shape))


def _numel(shape):
    n = 1
    for s in shape:
        n *= s
    return n


def _params(sem=None):
    return pltpu.CompilerParams(dimension_semantics=sem, vmem_limit_bytes=VMEM_LIMIT_V7X)


def _split2(x):
    hi = x.astype(BF16)
    return hi, (x.astype(F32) - hi.astype(F32)).astype(BF16)


def _dg(x, y, cx, cy, hi):
    dn = (((cx,), (cy,)), ((), ()))
    dot = lambda p, q: lax.dot_general(p, q, dn, preferred_element_type=F32)
    if hi == "x3":
        (xh, xl), (yh, yl) = _split2(x), _split2(y)
        return dot(xh, yh) + (dot(xh, yl) + dot(xl, yh))
    return dot(x.astype(BF16), y.astype(BF16))


def _make_mm(hi, cotangent_forms=None):
    @jax.custom_vjp
    def nn(x, y):
        return _dg(x, y, 1, 0, hi)

    @jax.custom_vjp
    def nt(x, y):
        return _dg(x, y, 1, 1, hi)

    @jax.custom_vjp
    def tn(x, y):
        return _dg(x, y, 0, 0, hi)

    bnn, bnt, btn = cotangent_forms or (nn, nt, tn)
    nn.defvjp(lambda x, y: (nn(x, y), (x, y)), lambda r, g: (bnt(g, r[1]), btn(r[0], g)))
    nt.defvjp(lambda x, y: (nt(x, y), (x, y)), lambda r, g: (bnn(g, r[1]), btn(g, r[0])))
    tn.defvjp(lambda x, y: (tn(x, y), (x, y)), lambda r, g: (bnt(r[1], g), bnn(r[0], g)))
    return nn, nt, tn


_nn, _nt, _tn = _make_mm(False)
_nn_x3, _nt_x3, _tn_x3 = _make_mm("x3", (_nn, _nt, _tn))


def _tri_apply(x, transpose):
    c = x.shape[0]
    tri = (lax.broadcasted_iota(jnp.int32, (c, c), 1) <= lax.broadcasted_iota(jnp.int32, (c, c), 0)).astype(BF16)
    dn = (((0 if transpose else 1,), (0,)), ((), ()))
    p1, p2 = _split2(x)
    dot = lambda p: lax.dot_general(tri, p, dn, preferred_element_type=F32)
    return dot(p1) + dot(p2)


@jax.custom_vjp
def _cumsum_rows(x):
    return _tri_apply(x, False)


_cumsum_rows.defvjp(lambda x: (_tri_apply(x, False), None), lambda _, g: (_tri_apply(g, True),))


def _sigmoid(x):
    return 1.0 / (1.0 + jnp.exp(-x))


def _silu(x):
    return x * _sigmoid(x)


def _softplus(z):
    return jnp.maximum(z, 0.0) + jnp.log(1.0 + jnp.exp(-jnp.abs(z)))


def _mm(a, b, *, ta=False, tb=False, tm, tn, tk, name, out_dtype=F32, res=None, scale=None, comm=None):
    m = a.shape[1] if ta else a.shape[0]
    kdim = a.shape[0] if ta else a.shape[1]
    n = b.shape[0] if tb else b.shape[1]
    assert (b.shape[1] if tb else b.shape[0]) == kdim
    tm, tn, tk = min(tm, m), min(tn, n), min(tk, kdim)
    assert m % tm == 0 and n % tn == 0 and kdim % tk == 0, (name, m, n, kdim)
    nk = kdim // tk
    a_spec = pl.BlockSpec((tk, tm), lambda i, j, k: (k, i)) if ta else pl.BlockSpec((tm, tk), lambda i, j, k: (i, k))
    b_spec = pl.BlockSpec((tn, tk), lambda i, j, k: (j, k)) if tb else pl.BlockSpec((tk, tn), lambda i, j, k: (k, j))
    o_spec = pl.BlockSpec((tm, tn), lambda i, j, k: (i, j))
    ca, cb = (0 if ta else 1), (1 if tb else 0)

    def body(*refs):
        if res is not None:
            a_ref, b_ref, r_ref, o_ref, acc_ref = refs
        else:
            a_ref, b_ref, o_ref, acc_ref = refs
        k = pl.program_id(2)

        @pl.when(k == 0)
        def _():
            acc_ref[...] = jnp.zeros_like(acc_ref)

        acc_ref[...] += _dg(a_ref[...], b_ref[...], ca, cb, False)

        @pl.when(k == nk - 1)
        def _():
            acc = acc_ref[...]
            if scale is not None:
                acc = acc * scale
            if res is not None:
                acc = r_ref[...] + acc
            o_ref[...] = acc.astype(out_dtype)

    in_specs = [a_spec, b_spec] + ([o_spec] if res is not None else [])
    args = (a, b) + ((res,) if res is not None else ())
    if comm is None:
        return pl.pallas_call(
            body, name=name, grid=(m // tm, n // tn, nk), in_specs=in_specs, out_specs=o_spec,
            out_shape=SDS((m, n), out_dtype), scratch_shapes=[pltpu.VMEM((tm, tn), F32)],
            compiler_params=_params(("parallel", "parallel", "arbitrary")))(*args)
    (out,), carried = _hosting_call(
        body, comm, name=name, grid=(m // tm, n // tn, nk), in_specs=in_specs, out_specs=[o_spec],
        out_shape=[SDS((m, n), out_dtype)], scratch_shapes=[pltpu.VMEM((tm, tn), F32)], args=args)
    return out, carried


def _row_spec(x, tm, tile_of=lambda i: i):
    if isinstance(x, tuple):
        arr, w, j = x
        return arr, pl.BlockSpec((tm, w), lambda i, j=j: (tile_of(i), j))
    return x, pl.BlockSpec((tm, x.shape[1]), lambda i: (tile_of(i), 0))


def _par_spec(p):
    if isinstance(p, tuple):
        arr, w, j = p
        return arr, pl.BlockSpec((arr.shape[0], w), lambda i, j=j: (0, j))
    return p, pl.BlockSpec(p.shape, lambda i: (0, 0))


def _store_groups(refs, groups, vals):
    for ref, idxs in zip(refs, groups):
        off = 0
        for ix in idxs:
            v = vals[ix]
            ref[:, off:off + v.shape[1]] = v.astype(ref.dtype)
            off += v.shape[1]


SUBLANES = 8


def _x_plan(xs, tm, t, tile_of=lambda i: i):
    arrays, specs, plan = [], [], []
    nb = tm // SUBLANES
    for x in xs:
        if isinstance(x, tuple) and isinstance(x[0], str):
            kind, arr, w, j = x
            if kind == "prev":
                halo = lambda i, j=j: (jnp.maximum(tile_of(i) * nb - 1, 0), j)
            else:
                halo = lambda i, j=j: (jnp.minimum((tile_of(i) + 1) * nb, t // SUBLANES - 1), j)
            arrays += [arr, arr]
            specs += [pl.BlockSpec((tm, w), lambda i, j=j: (tile_of(i), j)), pl.BlockSpec((SUBLANES, w), halo)]
            plan.append((kind, 2, w))
        else:
            arr, spec = _row_spec(x, tm, tile_of)
            arrays.append(arr)
            specs.append(spec)
            plan.append(("plain", 1, spec.block_shape[1]))
    return arrays, specs, plan


def _x_vals(refs, plan, tm, nt, tile_of=lambda i: i):
    vals, k = [], 0
    i = tile_of(pl.program_id(0))
    rows = lax.broadcasted_iota(jnp.int32, (tm, 1), 0)
    for kind, n, _ in plan:
        main = refs[k][...].astype(F32)
        if kind == "prev":
            edge = jnp.where(i == 0, 0.0, refs[k + 1][SUBLANES - 1:SUBLANES, :].astype(F32))
            main = jnp.where(rows == 0, edge, pltpu.roll(main, 1, 0))
        elif kind == "next":
            edge = jnp.where(i == nt - 1, 0.0, refs[k + 1][0:1, :].astype(F32))
            main = jnp.where(rows == tm - 1, edge, pltpu.roll(main, tm - 1, 0))
        vals.append(main)
        k += n
    return vals


def _tile_rows(xs, tm):
    arr = xs[0]
    if isinstance(arr, tuple):
        arr = arr[1] if isinstance(arr[0], str) else arr[0]
    return min(tm, arr.shape[0]), arr.shape[0]


def _rowwise(f, xs, params, out_groups, out_dtypes, *, tm, name, comm=None):
    tm, t = _tile_rows(xs, tm)
    nt = t // tm
    xa, xspecs, plan = _x_plan(xs, tm, t)
    pa, pspecs = (zip(*[_par_spec(p) for p in params]) if params else ((), ()))
    nxr, npar = len(xa), len(pa)
    x_sds = [SDS((tm, w), F32) for _, _, w in plan]
    p_sds = [SDS(s.block_shape, F32) for s in pspecs]
    outs_sds = jax.eval_shape(lambda *vals: f(*vals), *x_sds, *p_sds)
    widths = [sum(outs_sds[ix].shape[1] for ix in idxs) for idxs in out_groups]

    def body(*refs):
        vals = _x_vals(refs[:nxr], plan, tm, nt) + [r[...].astype(F32) for r in refs[nxr:nxr + npar]]
        outs = f(*vals)
        _store_groups(refs[nxr + npar:], out_groups, outs)

    res, carried = _hosting_call(
        body, comm, name=name, grid=(nt,), in_specs=list(xspecs) + list(pspecs),
        out_specs=[pl.BlockSpec((tm, w), lambda i: (i, 0)) for w in widths],
        out_shape=[SDS((t, w), dt) for w, dt in zip(widths, out_dtypes)], scratch_shapes=[], args=(*xa, *pa))
    return res if comm is None else (res, carried)


def _rowwise_bwd(f, xs, params, cots, *, x_grad, p_grad, dx_groups, dx_dtypes, tm, name, extra=None, comm=None,
                 fold_next=None):
    tm, t = _tile_rows(xs, tm)
    nt = t // tm
    tile_of = (lambda i: nt - 1 - i) if fold_next else (lambda i: i)
    xa, xspecs, plan = _x_plan(xs, tm, t, tile_of)
    pa, pspecs = (zip(*[_par_spec(p) for p in params]) if params else ((), ()))
    ca, cspecs = zip(*[_row_spec(c, tm, tile_of) for c in cots])
    extra = extra or {}
    ekeys = sorted(extra)
    ea, especs = (zip(*[_row_spec(extra[k], tm, tile_of) for k in ekeys]) if ekeys else ((), ()))
    nx, nxr, npar, nc, ne = len(plan), len(xa), len(pa), len(ca), len(ea)
    gx = [i for i in range(nx) if x_grad[i]]
    gp = [i for i in range(npar) if p_grad[i]]
    all_widths = [sum(plan[gx[ix]][2] for ix in idxs) for idxs in dx_groups]
    emitted = [k for k in range(len(dx_groups)) if not (fold_next and k == fold_next[1])]
    widths = [all_widths[k] for k in emitted]
    ng = len(emitted)

    def body(*refs):
        ins = refs[:nxr + npar + nc + ne]
        outs = refs[nxr + npar + nc + ne:]
        vals = _x_vals(ins[:nxr], plan, tm, nt, tile_of) + [r[...].astype(F32) for r in ins[nxr:nxr + npar]]
        cvals = tuple(r[...].astype(F32) for r in ins[nxr + npar:nxr + npar + nc])
        evals = [r[...].astype(F32) for r in ins[nxr + npar + nc:]]
        diff_idx = gx + [nx + i for i in gp]

        def g(*dargs):
            full = list(vals)
            for ix, v in zip(diff_idx, dargs):
                full[ix] = v
            return tuple(f(*full))

        _, vjp = jax.vjp(g, *[vals[ix] for ix in diff_idx])
        grads = vjp(cvals)
        dxs = list(grads[:len(gx)])
        for k, ev in zip(ekeys, evals):
            dxs[k] = dxs[k] + ev
        _store_groups(outs[:ng], [dx_groups[k] for k in emitted], dxs)
        i = pl.program_id(0)
        if fold_next:
            main_ref, carry_ref = outs[emitted.index(fold_next[0])], refs[-1]
            rows = lax.broadcasted_iota(jnp.int32, (tm, 1), 0)
            off = 0
            for ix in dx_groups[fold_next[1]]:
                piece = dxs[ix]
                cols = slice(off, off + piece.shape[1])
                edge = jnp.where(i == 0, 0.0, carry_ref[0:1, cols])
                main_ref[:, cols] += jnp.where(rows == tm - 1, edge, pltpu.roll(piece, tm - 1, 0))
                carry_ref[:, cols] = piece[:SUBLANES]
                off += piece.shape[1]
        for ref, gval in zip(outs[ng:ng + len(gp)], grads[len(gx):]):
            @pl.when(i == 0)
            def _(ref=ref):
                ref[...] = jnp.zeros_like(ref)
            ref[...] += gval

    dp_specs = [pl.BlockSpec(pspecs[i].block_shape, lambda i: (0, 0)) for i in gp]
    dp_shapes = [SDS(pspecs[i].block_shape, F32) for i in gp]
    scratch = [pltpu.VMEM((SUBLANES, all_widths[fold_next[1]]), F32)] if fold_next else []
    res, carried = _hosting_call(
        body, comm, name=name, grid=(nt,), in_specs=list(xspecs) + list(pspecs) + list(cspecs) + list(especs),
        out_specs=[pl.BlockSpec((tm, w), lambda i: (tile_of(i), 0)) for w in widths] + dp_specs,
        out_shape=[SDS((t, w), dt) for w, dt in zip(widths, dx_dtypes)] + dp_shapes, scratch_shapes=scratch,
        args=(*xa, *pa, *ca, *ea))
    return res if comm is None else (res, carried)


def _rms_f(x, g):
    return (x * lax.rsqrt(jnp.mean(x * x, axis=-1, keepdims=True) + NORM_EPS) * g,)


def _group_sum_impl(x, ones_bd):
    p1, p2 = _split2(x)
    dot = lambda p: lax.dot_general(p, ones_bd.astype(BF16), (((1,), (0,)), ((), ())), preferred_element_type=F32)
    return dot(p1) + dot(p2)


@jax.custom_vjp
def _group_sum(x, ones_bd):
    return _group_sum_impl(x, ones_bd)


_group_sum.defvjp(lambda x, o: (_group_sum_impl(x, o), o),
                  lambda o, g: (_group_sum_impl(g, o), jnp.zeros_like(o)))


def _rwkv_prep_f(r, k, v, lo, rp, kp, vp, lop, mu_r, mu_k, mu_v, mu_lo, w0, w2p, a0, a2p, g2p, k_k, k_a, ones_bd):
    r = r + mu_r * (rp - r)
    k = k + mu_k * (kp - k)
    v = v + mu_v * (vp - v)
    lo = lo + mu_lo * (lop - lo)
    w_log = -_softplus(-(w0 + _nn(jnp.tanh(lo), w2p))) - 0.5
    lw = -jnp.exp(w_log)
    a_g = _sigmoid(a0 + _nn(lo, a2p))
    g = _nn(_sigmoid(lo), g2p)
    kk = k * k_k
    kk = kk / jnp.maximum(jnp.sqrt(_group_sum(kk * kk, ones_bd)), L2_EPS)
    k2 = k * (1.0 + (a_g - 1.0) * k_a)
    return r, lw, k2, v, -kk, kk * a_g, g


def _rwkv_post_f(y, r, k2, v, g, r_k, gn_w, gn_b, ones_bd):
    inv_n = 1.0 / HB_DIM
    mean = _group_sum(y, ones_bd) * inv_n
    yc = y - mean
    var = _group_sum(yc * yc, ones_bd) * inv_n
    yn = yc * lax.rsqrt(var + RWKV_GN_EPS) * gn_w + gn_b
    bonus = _group_sum(r * k2 * r_k, ones_bd) * v
    return ((yn + bonus) * g,)


def _tri(c, strict=False):
    ii = lax.broadcasted_iota(jnp.int32, (c, c), 0)
    jj = lax.broadcasted_iota(jnp.int32, (c, c), 1)
    return (jj < ii) if strict else (jj <= ii)


def _hgrn_step(st0, q_a, f_a, i_a, g_a, l0, l1, onorm):
    nh, nj = len(q_a), len(q_a[0])
    c = q_a[0][0].shape[0]
    combos = [(j, h) for j in range(nj) for h in range(nh)]
    every = lambda fn: {q: fn(q) for q in combos}
    at_ = lambda d: (lambda q: d[q[1]][q[0]])
    qa_, fa_, ia_, ga_ = (at_(z) for z in (q_a, f_a, i_a, g_a))
    incl = _tri(c)
    rows = lax.broadcasted_iota(jnp.int32, (c, 1), 0)
    lb = []
    for h in range(nh):
        mx = jnp.maximum(l0[h], l1[h])
        e0, e1 = jnp.exp(l0[h] - mx), jnp.exp(l1[h] - mx)
        lb.append(e0 / (e0 + e1))
    forget = every(lambda q: lb[q[1]] + (1.0 - lb[q[1]]) * _sigmoid(fa_(q)))
    qs = every(lambda q: _silu(qa_(q)))
    kk = every(lambda q: 1.0 - forget[q])
    lf = every(lambda q: jnp.log(forget[q]))
    bcum = every(lambda q: _cumsum_rows(lf[q]))
    bref = every(lambda q: jnp.sum(jnp.where(rows <= c // 2, lf[q], 0.0), axis=0, keepdims=True))
    blast = every(lambda q: jnp.sum(lf[q], axis=0, keepdims=True))
    scores = every(lambda q: jnp.where(incl, _nt(qs[q] * jnp.exp(bcum[q] - bref[q]),
                                                 kk[q] * jnp.exp(bref[q] - bcum[q])), 0.0))
    intra = every(lambda q: _nn(scores[q], ia_(q)))
    qb = every(lambda q: qs[q] * jnp.exp(bcum[q]))
    upd = every(lambda q: _tn(ia_(q), kk[q] * jnp.exp(blast[q] - bcum[q])))
    dec = every(lambda q: jnp.exp(blast[q]))
    st = list(st0)
    o = {}
    for j in range(nj):
        for h in range(nh):
            o[(j, h)] = intra[(j, h)] + _nt(qb[(j, h)], st[h])
        st = [st[h] * dec[(j, h)] + upd[(j, h)] for h in range(nh)]
    out = every(lambda q: o[q] * lax.rsqrt(jnp.mean(o[q] * o[q], axis=-1, keepdims=True) + NORM_EPS)
                * onorm[q[1]] * _silu(ga_(q)))
    return [[out[(j, h)] for j in range(nj)] for h in range(nh)], st


def _hgrn_blocks(ref, nj, c):
    return [[ref[j * c:(j + 1) * c, h * HA_DIM:(h + 1) * HA_DIM] for j in range(nj)] for h in range(HA_HEADS)]


def _hgrn_cols(ref):
    return [ref[:, h * HA_DIM:(h + 1) * HA_DIM] for h in range(HA_HEADS)]


def _hgrn_fwd(p_h, l0, l1, onorm):
    t = p_h.shape[0]
    cc, nj = HGRN_CHUNK, HGRN_GROUP
    c = cc * nj
    n = t // c

    def body(q_ref, f_ref, i_ref, g_ref, l0_ref, l1_ref, on_ref, o_ref, hs_ref, st_ref):
        @pl.when(pl.program_id(0) == 0)
        def _():
            st_ref[...] = jnp.zeros_like(st_ref)

        hs_ref[0] = st_ref[...]
        o, st1 = _hgrn_step([st_ref[h] for h in range(HA_HEADS)],
                            *[_hgrn_blocks(ref, nj, cc) for ref in (q_ref, f_ref, i_ref, g_ref)],
                            _hgrn_cols(l0_ref), _hgrn_cols(l1_ref), _hgrn_cols(on_ref))
        for h in range(HA_HEADS):
            for j in range(nj):
                o_ref[j * cc:(j + 1) * cc, h * HA_DIM:(h + 1) * HA_DIM] = o[h][j]
            st_ref[h] = st1[h]

    col = lambda j: pl.BlockSpec((c, W_A), lambda i, j=j: (i, j))
    par = pl.BlockSpec((1, W_A), lambda i: (0, 0))
    return pl.pallas_call(
        body, name="hgrn_fwd", grid=(n,), in_specs=[col(0), col(1), col(2), col(3), par, par, par],
        out_specs=[pl.BlockSpec((c, W_A), lambda i: (i, 0)),
                   pl.BlockSpec((1, HA_HEADS, HA_DIM, HA_DIM), lambda i: (i, 0, 0, 0))],
        out_shape=[SDS((t, W_A), F32), SDS((n, HA_HEADS, HA_DIM, HA_DIM), F32)],
        scratch_shapes=[pltpu.VMEM((HA_HEADS, HA_DIM, HA_DIM), F32)],
        compiler_params=_params(("arbitrary",)))(p_h, p_h, p_h, p_h, l0, l1, onorm)


def _hgrn_bwd(p_h, l0, l1, onorm, hs, do, do_col, comm=None):
    t = p_h.shape[0]
    cc, nj = HGRN_CHUNK, HGRN_GROUP
    c = cc * nj
    n = t // c

    def body(q_ref, f_ref, i_ref, g_ref, l0_ref, l1_ref, on_ref, hs_ref, do_ref,
             dp_ref, dl0_ref, dl1_ref, don_ref, dst_ref):
        @pl.when(pl.program_id(0) == 0)
        def _():
            dst_ref[...] = jnp.zeros_like(dst_ref)
            dl0_ref[...] = jnp.zeros_like(dl0_ref)
            dl1_ref[...] = jnp.zeros_like(dl1_ref)
            don_ref[...] = jnp.zeros_like(don_ref)

        args = ([hs_ref[0, h] for h in range(HA_HEADS)],
                *[_hgrn_blocks(ref, nj, cc) for ref in (q_ref, f_ref, i_ref, g_ref)],
                _hgrn_cols(l0_ref), _hgrn_cols(l1_ref), _hgrn_cols(on_ref))
        _, vjp = jax.vjp(_hgrn_step, *args)
        dst0, dq, df, di, dg, dl0, dl1, don = vjp((_hgrn_blocks(do_ref, nj, cc),
                                                   [dst_ref[h] for h in range(HA_HEADS)]))
        for h in range(HA_HEADS):
            sl = slice(h * HA_DIM, (h + 1) * HA_DIM)
            for k, dv in enumerate((dq, df, di, dg)):
                for j in range(nj):
                    dp_ref[j * cc:(j + 1) * cc, k * W_A + h * HA_DIM:k * W_A + (h + 1) * HA_DIM] = dv[h][j]
            dl0_ref[:, sl] += dl0[h]
            dl1_ref[:, sl] += dl1[h]
            don_ref[:, sl] += don[h]
            dst_ref[h] = dst0[h]

    col = lambda j: pl.BlockSpec((c, W_A), lambda i, j=j: (n - 1 - i, j))
    par = pl.BlockSpec((1, W_A), lambda i: (0, 0))
    return _hosting_call(
        body, comm, name="hgrn_bwd", grid=(n,),
        in_specs=[col(0), col(1), col(2), col(3), par, par, par,
                  pl.BlockSpec((1, HA_HEADS, HA_DIM, HA_DIM), lambda i: (n - 1 - i, 0, 0, 0)),
                  pl.BlockSpec((c, W_A), lambda i: (n - 1 - i, do_col))],
        out_specs=[pl.BlockSpec((c, N_HGRN_COLS), lambda i: (n - 1 - i, 0)), par, par, par],
        out_shape=[SDS((t, N_HGRN_COLS), F32), SDS((1, W_A), F32), SDS((1, W_A), F32), SDS((1, W_A), F32)],
        scratch_shapes=[pltpu.VMEM((HA_HEADS, HA_DIM, HA_DIM), F32)],
        args=(p_h, p_h, p_h, p_h, l0, l1, onorm, hs, do))


HB_PAIRS = HB_HEADS // 2
PAIR_W = 2 * HB_DIM


def _head_lane_masks():
    lane = lax.broadcasted_iota(jnp.int32, (1, PAIR_W), 1)
    return (lane < HB_DIM).astype(F32), (lane >= HB_DIM).astype(F32)


@jax.custom_vjp
def _stack_heads(x):
    m0, m1 = _head_lane_masks()
    return jnp.concatenate([x * m0, x * m1], axis=0)


def _stack_heads_bwd(_, g):
    m0, m1 = _head_lane_masks()
    c = g.shape[0] // 2
    return (g[:c] * m0 + g[c:] * m1,)


_stack_heads.defvjp(lambda x: (_stack_heads(x), None), _stack_heads_bwd)


@jax.custom_vjp
def _unstack_heads(ys):
    c = ys.shape[0] // 2
    return ys[:c] + ys[c:]


_unstack_heads.defvjp(lambda ys: (_unstack_heads(ys), None), lambda _, g: (_stack_heads(g),))


def _same_head_block(c):
    ii = lax.broadcasted_iota(jnp.int32, (2 * c, 2 * c), 0)
    jj = lax.broadcasted_iota(jnp.int32, (2 * c, 2 * c), 1)
    same = (ii < c) == (jj < c)
    return same & (jj <= ii), same & (jj < ii), (ii == jj).astype(F32)


@jax.custom_vjp
def _rows_join(top, bottom):
    return jnp.concatenate([top, bottom], axis=0)


def _rows_join_bwd(n_top, g):
    return g[:n_top], g[n_top:]


_rows_join.defvjp(lambda top, bottom: (_rows_join(top, bottom), top.shape[0]), _rows_join_bwd)


def _rows_split_impl(x, n_top):
    return x[:n_top], x[n_top:]


_rows_split = jax.custom_vjp(_rows_split_impl, nondiff_argnums=(1,))
_rows_split.defvjp(lambda x, n_top: (_rows_split_impl(x, n_top), None),
                   lambda n_top, _, g: (jnp.concatenate([g[0], g[1]], axis=0),))


def _rwkv_step(s0, r, lw, k, v, a, b):
    npair, nj = len(r), len(r[0])
    c = r[0][0].shape[0]
    combos = [(j, p) for j in range(nj) for p in range(npair)]
    every = lambda fn: {q: fn(q) for q in combos}
    at_ = lambda d: (lambda q: d[q[1]][q[0]])
    r_, lw_, k_, v_, a_, b_ = (at_(z) for z in (r, lw, k, v, a, b))
    incl, strict, eye = _same_head_block(c)

    gam = every(lambda q: _cumsum_rows(lw_(q)))
    gtot = every(lambda q: jnp.sum(lw_(q), axis=0, keepdims=True))
    eneg = every(lambda q: jnp.exp(-gam[q]))
    edec = every(lambda q: jnp.exp(gtot[q] - gam[q]))
    at = every(lambda q: _stack_heads(a_(q) * jnp.exp(gam[q] - lw_(q))))
    rt = every(lambda q: _stack_heads(r_(q) * jnp.exp(gam[q])))
    bt = every(lambda q: _stack_heads(b_(q) * eneg[q]))
    kt = every(lambda q: _stack_heads(k_(q) * eneg[q]))
    bdec = every(lambda q: _stack_heads(b_(q) * edec[q]))
    kdec = every(lambda q: _stack_heads(k_(q) * edec[q]))
    vs = every(lambda q: _stack_heads(v_(q)))
    a_ab = every(lambda q: jnp.where(strict, _nt(at[q], bt[q]), 0.0))
    a_ak = every(lambda q: jnp.where(strict, _nt(at[q], kt[q]), 0.0))
    a_rb = every(lambda q: jnp.where(incl, _nt(rt[q], bt[q]), 0.0))
    a_rk = every(lambda q: jnp.where(incl, _nt(rt[q], kt[q]), 0.0))
    tinv = every(lambda q: eye + a_ab[q])
    pw = a_ab
    span = 2
    while span < c:
        pw = every(lambda q, pw=pw: _nn_x3(pw[q], pw[q]))
        tinv = every(lambda q, pw=pw, tinv=tinv: tinv[q] + _nn_x3(pw[q], tinv[q]))
        span *= 2
    akv = every(lambda q: _nn(a_ak[q], vs[q]))
    w1 = every(lambda q: _nn(tinv[q], at[q]))
    u0 = every(lambda q: _nn(tinv[q], akv[q]))
    wr = every(lambda q: _rows_join(w1[q], rt[q]))
    bk = every(lambda q: _rows_join(bdec[q], kdec[q]))
    yv = every(lambda q: _nn(a_rk[q], vs[q]))
    gdec = every(lambda q: jnp.exp(gtot[q]))

    s = list(s0)
    y = [[None] * nj for _ in range(npair)]
    for j in range(nj):
        both = {p: _rows_split(_nt(wr[(j, p)], s[p]), 2 * c) for p in range(npair)}
        u = {p: both[p][0] + u0[(j, p)] for p in range(npair)}
        for p in range(npair):
            y[p][j] = _unstack_heads(both[p][1] + _nn(a_rb[(j, p)], u[p]) + yv[(j, p)])
        s = [s[p] * gdec[(j, p)] + _tn(_rows_join(u[p], vs[(j, p)]), bk[(j, p)]) for p in range(npair)]
    return y, s


def _rwkv_blocks(ref, nj, c):
    return [[ref[j * c:(j + 1) * c, p * PAIR_W:(p + 1) * PAIR_W] for j in range(nj)] for p in range(HB_PAIRS)]


def _rwkv_fwd(seqs, comm=None):
    t = seqs[0].shape[0]
    c, nj = RWKV_CHUNK, RWKV_GROUP
    n = t // (c * nj)

    def body(r_ref, lw_ref, k_ref, v_ref, a_ref, b_ref, y_ref, hs_ref, st_ref):
        @pl.when(pl.program_id(0) == 0)
        def _():
            st_ref[...] = jnp.zeros_like(st_ref)

        hs_ref[0] = st_ref[...]
        s0 = [st_ref[p] for p in range(HB_PAIRS)]
        y, s1 = _rwkv_step(s0, *[_rwkv_blocks(ref, nj, c) for ref in (r_ref, lw_ref, k_ref, v_ref, a_ref, b_ref)])
        for p in range(HB_PAIRS):
            for j in range(nj):
                y_ref[j * c:(j + 1) * c, p * PAIR_W:(p + 1) * PAIR_W] = y[p][j]
            st_ref[p] = s1[p]

    seq = pl.BlockSpec((c * nj, W_B), lambda i: (i, 0))
    return _hosting_call(
        body, comm, name="rwkv_fwd", grid=(n,), in_specs=[seq] * 6,
        out_specs=[seq, pl.BlockSpec((1, HB_PAIRS, PAIR_W, PAIR_W), lambda i: (i, 0, 0, 0))],
        out_shape=[SDS((t, W_B), F32), SDS((n, HB_PAIRS, PAIR_W, PAIR_W), F32)],
        scratch_shapes=[pltpu.VMEM((HB_PAIRS, PAIR_W, PAIR_W), F32)], args=tuple(seqs))


def _rwkv_bwd(seqs, hs, dy, comm=None):
    t = seqs[0].shape[0]
    c, nj = RWKV_CHUNK, RWKV_GROUP
    n = t // (c * nj)

    def body(r_ref, lw_ref, k_ref, v_ref, a_ref, b_ref, hs_ref, dy_ref,
             dr_ref, dlw_ref, dk_ref, dv_ref, da_ref, db_ref, dst_ref):
        @pl.when(pl.program_id(0) == 0)
        def _():
            dst_ref[...] = jnp.zeros_like(dst_ref)

        s0 = [hs_ref[0, p] for p in range(HB_PAIRS)]
        seq_vals = [_rwkv_blocks(ref, nj, c) for ref in (r_ref, lw_ref, k_ref, v_ref, a_ref, b_ref)]
        _, vjp = jax.vjp(_rwkv_step, s0, *seq_vals)
        grads = vjp((_rwkv_blocks(dy_ref, nj, c), [dst_ref[p] for p in range(HB_PAIRS)]))
        for ref, gr in zip((dr_ref, dlw_ref, dk_ref, dv_ref, da_ref, db_ref), grads[1:]):
            for p in range(HB_PAIRS):
                for j in range(nj):
                    ref[j * c:(j + 1) * c, p * PAIR_W:(p + 1) * PAIR_W] = gr[p][j]
        m0, m1 = _head_lane_masks()
        rows0 = (lax.broadcasted_iota(jnp.int32, (PAIR_W, 1), 0) < HB_DIM).astype(F32)
        blocks = rows0 * m0 + (1.0 - rows0) * m1
        for p in range(HB_PAIRS):
            dst_ref[p] = grads[0][p] * blocks

    seq = pl.BlockSpec((c * nj, W_B), lambda i: (n - 1 - i, 0))
    return _hosting_call(
        body, comm, name="rwkv_bwd", grid=(n,),
        in_specs=[seq] * 6 + [pl.BlockSpec((1, HB_PAIRS, PAIR_W, PAIR_W), lambda i: (n - 1 - i, 0, 0, 0)), seq],
        out_specs=[seq] * 6, out_shape=[SDS((t, W_B), F32)] * 6,
        scratch_shapes=[pltpu.VMEM((HB_PAIRS, PAIR_W, PAIR_W), F32)], args=(*seqs, hs, dy))


def _final_loss(x3, fnorm, target, *, tm):
    t, d = x3.shape

    def body(x_ref, g_ref, t_ref, dx_ref, dg_ref, loss_ref):
        @pl.when(pl.program_id(0) == 0)
        def _():
            dg_ref[...] = jnp.zeros_like(dg_ref)
            loss_ref[...] = jnp.zeros_like(loss_ref)

        x, g = x_ref[...], g_ref[...]
        rinv = lax.rsqrt(jnp.mean(x * x, axis=-1, keepdims=True) + NORM_EPS)
        xh = x * rinv
        diff = xh * g - t_ref[...]
        loss_ref[...] += 0.5 * jnp.sum(jnp.mean(diff * diff, axis=-1, keepdims=True))
        dy = diff * (1.0 / d)
        dg_ref[...] += jnp.sum(dy * xh, axis=0, keepdims=True)
        dxh = dy * g
        dx_ref[...] = rinv * (dxh - xh * jnp.mean(dxh * xh, axis=-1, keepdims=True))

    row = pl.BlockSpec((tm, d), lambda i: (i, 0))
    return pl.pallas_call(
        body, name="final_loss", grid=(t // tm,), in_specs=[row, pl.BlockSpec((1, d), lambda i: (0, 0)), row],
        out_specs=[row, pl.BlockSpec((1, d), lambda i: (0, 0)), pl.BlockSpec((8, 128), lambda i: (0, 0))],
        out_shape=[SDS((t, d), F32), SDS((1, d), F32), SDS((8, 128), F32)],
        compiler_params=_params(("arbitrary",)))(x3, fnorm, target)


def _gate_up_act(h, wgt, wut, *, tm, tn, name, comm=None):
    t, d = h.shape
    tm = min(tm, t)

    def body(h_ref, g_ref, u_ref, a_out, u_out, act_out):
        hv = h_ref[...]
        a = _dg(hv, g_ref[...], 1, 1, False)
        u = _dg(hv, u_ref[...], 1, 1, False)
        a_out[...] = a.astype(a_out.dtype)
        u_out[...] = u.astype(u_out.dtype)
        act_out[...] = (_silu(a) * u).astype(act_out.dtype)

    wspec = pl.BlockSpec((tn, d), lambda i, j: (j, 0))
    ospec = pl.BlockSpec((tm, tn), lambda i, j: (i, j))
    return _hosting_call(
        body, comm, name=name, grid=(t // tm, D_FF // tn),
        in_specs=[pl.BlockSpec((tm, d), lambda i, j: (i, 0)), wspec, wspec], out_specs=[ospec, ospec, ospec],
        out_shape=[SDS((t, D_FF), BF16), SDS((t, D_FF), BF16), SDS((t, D_FF), BF16)], scratch_shapes=[],
        args=(h, wgt, wut))


def _dact_swiglu(dout, wd, a, u, *, tm, tn, name, comm=None):
    t, d = dout.shape
    tm = min(tm, t)

    def body(d_ref, w_ref, a_ref, u_ref, da_out, du_out):
        dact = 0.5 * _dg(d_ref[...], w_ref[...], 1, 1, False)
        av, uv = a_ref[...].astype(F32), u_ref[...].astype(F32)
        s = _sigmoid(av)
        da_out[...] = (dact * uv * (s * (1.0 + av * (1.0 - s)))).astype(da_out.dtype)
        du_out[...] = (dact * (av * s)).astype(du_out.dtype)

    tile = pl.BlockSpec((tm, tn), lambda i, j: (i, j))
    return _hosting_call(
        body, comm, name=name, grid=(t // tm, D_FF // tn),
        in_specs=[pl.BlockSpec((tm, d), lambda i, j: (i, 0)), pl.BlockSpec((tn, d), lambda i, j: (j, 0)), tile, tile],
        out_specs=[tile, tile], out_shape=[SDS((t, D_FF), BF16), SDS((t, D_FF), BF16)], scratch_shapes=[],
        args=(dout, wd, a, u))


class _Plan:
    def __init__(self):
        self.entries, self.counts = collections.defaultdict(list), {}

    def carry(self, host, comm_of, after):
        self.entries[host].append((comm_of, after))

    def comm(self, host, g):
        comms = [comm_of(g) for comm_of, _ in self.entries.get(host, [])]
        self.counts[host] = [len(c.arrays) for c in comms]
        return functools.reduce(_join_comms, comms) if comms else None

    def done(self, host, results, w):
        start = 0
        for (_, after), n in zip(self.entries.get(host, []), self.counts.get(host, [])):
            after(results[start:start + n], w)
            start += n


def _ffn_fwd(x, w, tag, plan, g):
    comm = plan.comm(f"{tag}_rms", g)
    res = _rowwise(_rms_f, [x], [w[f"{tag}_norm"]], [[0]], [BF16], tm=512, name=f"{tag}_rms", comm=comm)
    (h,), carried = res if comm is not None else (res, [])
    plan.done(f"{tag}_rms", carried, w)
    (a, u, act), carried = _gate_up_act(h, w[f"{tag}_wgt"], w[f"{tag}_wut"], tm=2048, tn=256, name=f"{tag}_gate_up",
                                        comm=plan.comm(f"{tag}_gate_up", g))
    plan.done(f"{tag}_gate_up", carried, w)
    comm = plan.comm(f"{tag}_down", g)
    out = _mm(act, w[f"{tag}_wd"], tm=1024, tn=D_MODEL, tk=D_FF, name=f"{tag}_down", res=x, scale=0.5, comm=comm)
    if comm is not None:
        out, carried = out
        plan.done(f"{tag}_down", carried, w)
    return out, (h, a, u, act)


def _ffn_bwd(dout, x, w, saved, tag, plan, g):
    h, a, u, act = saved

    def carrying(fn, host, *args, **kwargs):
        comm = plan.comm(host, g)
        res = fn(*args, name=host, comm=comm, **kwargs)
        out, carried = res if comm is not None else (res, [])
        plan.done(host, carried, w)
        return out

    (da, du), carried = _dact_swiglu(dout, w[f"{tag}_wd"], a, u, tm=2048, tn=256, name=f"{tag}_dact",
                                     comm=plan.comm(f"{tag}_dact", g))
    plan.done(f"{tag}_dact", carried, w)
    g[f"{tag}_wd"] = _mm(act, dout, ta=True, tm=D_FF // 2, tn=D_MODEL, tk=1024, name=f"{tag}_dwd", scale=0.5)
    g[f"{tag}_wgt"] = carrying(_mm, f"{tag}_dwg", da, h, ta=True, tm=D_FF // 2, tn=D_MODEL, tk=1024)
    g[f"{tag}_wut"] = carrying(_mm, f"{tag}_dwu", du, h, ta=True, tm=D_FF // 2, tn=D_MODEL, tk=1024)
    dh = carrying(_mm, f"{tag}_dh_g", da, w[f"{tag}_wgt"], tm=1024, tn=D_MODEL, tk=D_FF)
    dh = carrying(_mm, f"{tag}_dh_u", du, w[f"{tag}_wut"], tm=1024, tn=D_MODEL, tk=D_FF, res=dh)
    dx, g[f"{tag}_norm"] = carrying(_rowwise_bwd, f"{tag}_drms", _rms_f, [x], [w[f"{tag}_norm"]], [dh],
                                    x_grad=[True], p_grad=[True], dx_groups=[[0]], dx_dtypes=[F32], tm=512,
                                    extra={0: dout})
    return dx


def _local_step(x, target, w, plan=None):
    plan = plan or _Plan()
    ones_bd = jnp.kron(jnp.eye(HB_HEADS, dtype=F32), jnp.ones((HB_DIM, HB_DIM), F32))
    g = {}
    x1, ffn1_saved = _ffn_fwd(x, w, "ffn1", plan, g)
    hm, = _rowwise(_rms_f, [x1], [w["mix_norm"]], [[0]], [BF16], tm=512, name="mix_rms")
    p_h = _mm(hm, w["w_in_h"], tm=2048, tn=256, tk=D_MODEL, name="inproj_h")
    p_r = _mm(hm, w["w_in_r"], tm=2048, tn=256, tk=D_MODEL, name="inproj_r")
    o_a, hgrn_states = _hgrn_fwd(p_h, w["lb0"], w["lb1"], w["hgrn_out_norm"])

    mu = w["mu_pad"]
    prep_xs = [(p_r, W_B, 0), (p_r, W_B, 1), (p_r, W_B, 2), (p_r, LORA_PAD, 6),
               ("prev", p_r, W_B, 0), ("prev", p_r, W_B, 1), ("prev", p_r, W_B, 2), ("prev", p_r, LORA_PAD, 6)]
    prep_ps = [(mu, W_B, 0), (mu, W_B, 1), (mu, W_B, 2), (mu, LORA_PAD, 6), w["rwkv_w0"], w["w2_pad"], w["rwkv_a0"],
               w["a2_pad"], w["g2_pad"], w["rwkv_k_k"], w["rwkv_k_a"], ones_bd]
    prep_f = _rwkv_prep_f
    r, lw, k2, v, a_vec, b_vec, gate = _rowwise(prep_f, prep_xs, prep_ps, [[0], [1], [2], [3], [4], [5], [6]],
                                                [F32] * 7, tm=256, name="rwkv_prep")
    seqs = [r, lw, k2, v, a_vec, b_vec]
    (y, rwkv_states), carried = _rwkv_fwd(seqs, comm=plan.comm("rwkv_fwd", g))
    plan.done("rwkv_fwd", carried, w)
    post_f = _rwkv_post_f
    post_xs = [y, r, k2, v, gate]
    post_ps = [w["rwkv_r_k"], w["rwkv_gn_w"], w["rwkv_gn_b"], ones_bd]
    o, = _rowwise(lambda o_a_, *rest: (o_a_,) + tuple(post_f(*rest)), [o_a] + post_xs, post_ps, [[0, 1]], [F32],
                  tm=256, name="rwkv_post")
    x2 = _mm(o, w["w_out"], tm=2048, tn=256, tk=D_MODEL, name="outproj", res=x1)
    x3, ffn2_saved = _ffn_fwd(x2, w, "ffn2", plan, g)
    dx3, g["final_norm"], loss = _final_loss(x3, w["final_norm"], target, tm=256)

    dx2 = _ffn_bwd(dx3, x2, w, ffn2_saved, "ffn2", plan, g)
    do = _mm(dx2, w["w_out"], tb=True, tm=2048, tn=256, tk=D_MODEL, name="outproj_do")
    g["w_out"] = _mm(o, dx2, ta=True, tm=D_MODEL, tn=D_MODEL, tk=1024, name="outproj_dw")

    (dp_h, g["lb0"], g["lb1"], g["hgrn_out_norm"]), carried = _hgrn_bwd(
        p_h, w["lb0"], w["lb1"], w["hgrn_out_norm"], hgrn_states, do, 0, comm=plan.comm("hgrn_bwd", g))
    plan.done("hgrn_bwd", carried, w)
    post_out = _rowwise_bwd(post_f, post_xs, post_ps, [(do, W_B, 1)], x_grad=[True] * 5, p_grad=[True] * 3 + [False],
                            dx_groups=[[0], [1], [2], [3], [4]], dx_dtypes=[F32] * 5, tm=256, name="rwkv_post_bwd")
    dy, dr1, dk1, dv1, dgate, g["rwkv_r_k"], g["rwkv_gn_w"], g["rwkv_gn_b"] = post_out
    (dr2, dlw, dk2, dv2, da_vec, db_vec), carried = _rwkv_bwd(seqs, rwkv_states, dy, comm=plan.comm("rwkv_bwd", g))
    plan.done("rwkv_bwd", carried, w)

    def prep2_f(*vals):
        r_, lw_, k2_, v_, a_, b_, g_ = prep_f(*vals)
        return r_, lw_, k2_, v_, a_, b_, g_, r_, k2_, v_

    prep_comm = plan.comm("rwkv_prep_bwd", g)
    prep_out = _rowwise_bwd(prep2_f, prep_xs, prep_ps, [dr2, dlw, dk2, dv2, da_vec, db_vec, dgate, dr1, dk1, dv1],
                            x_grad=[True] * 8, p_grad=[True] * 11 + [False], dx_groups=[[0, 1, 2, 3], [4, 5, 6, 7]],
                            dx_dtypes=[F32], tm=256, name="rwkv_prep_bwd", fold_next=(0, 1), comm=prep_comm)
    prep_out, carried = prep_out if prep_comm is not None else (prep_out, [])
    plan.done("rwkv_prep_bwd", carried, w)
    dp_r = prep_out[0]
    (dmu_r, dmu_k, dmu_v, dmu_lo, g["rwkv_w0"], g["w2_pad"], g["rwkv_a0"], g["a2_pad"], g["g2_pad"],
     g["rwkv_k_k"], g["rwkv_k_a"]) = prep_out[1:]
    g["mu_pad"] = jnp.concatenate([dmu_r, dmu_k, dmu_v, dmu_lo], axis=1)
    dhm = _mm(dp_h, w["w_in_h"], tb=True, tm=1024, tn=D_MODEL, tk=N_HGRN_COLS, name="inproj_dh_h")
    dhm = _mm(dp_r, w["w_in_r"], tb=True, tm=1024, tn=D_MODEL, tk=N_RWKV_PAD, name="inproj_dh_r", res=dhm)
    g["w_in_h"] = _mm(hm, dp_h, ta=True, tm=D_MODEL, tn=D_MODEL, tk=1024, name="inproj_dw_h")
    g["w_in_r"] = _mm(hm, dp_r, ta=True, tm=D_MODEL, tn=N_RWKV_PAD // 2, tk=1024, name="inproj_dw_r")
    mix_comm = plan.comm("mix_drms", g)
    mix_out = _rowwise_bwd(_rms_f, [x1], [w["mix_norm"]], [dhm], x_grad=[True], p_grad=[True], dx_groups=[[0]],
                           dx_dtypes=[F32], tm=512, name="mix_drms", extra={0: dx2}, comm=mix_comm)
    (dx1, g["mix_norm"]), carried = mix_out if mix_comm is not None else (mix_out, [])
    plan.done("mix_drms", carried, w)
    dx0 = _ffn_bwd(dx1, x, w, ffn1_saved, "ffn1", plan, g)
    return loss, dx0, g


HBM_SPEC = pl.BlockSpec(memory_space=pl.ANY)

Comm = collections.namedtuple("Comm", "arrays out_shapes aliased sem_shapes start finish")


def _join_comms(first, second):
    assert first.aliased == second.aliased
    n, s = len(first.arrays), len(first.sem_shapes)

    def start(ins, outs, sems):
        first.start(ins[:n], outs[:n], sems[:s])
        second.start(ins[n:], outs[n:], sems[s:])

    def finish(ins, outs, sems):
        first.finish(ins[:n], outs[:n], sems[:s])
        second.finish(ins[n:], outs[n:], sems[s:])

    return Comm(list(first.arrays) + list(second.arrays), list(first.out_shapes) + list(second.out_shapes),
                first.aliased, list(first.sem_shapes) + list(second.sem_shapes), start, finish)


def _run_comm(comm, name):
    n = len(comm.arrays)

    def body(*refs):
        ins, outs, sems = refs[:n], refs[n:2 * n], refs[2 * n:]
        comm.start(ins, outs, sems)
        comm.finish(ins, outs, sems)

    return pl.pallas_call(
        body, name=name, in_specs=[HBM_SPEC] * n, out_specs=[HBM_SPEC] * n, out_shape=list(comm.out_shapes),
        input_output_aliases={t: t for t in range(n)} if comm.aliased else {},
        scratch_shapes=list(comm.sem_shapes))(*comm.arrays)


def _hosting_call(body, comm, *, name, grid, in_specs, out_specs, out_shape, scratch_shapes, args):
    sem = ("arbitrary",) * len(grid)
    if comm is None:
        res = pl.pallas_call(body, name=name, grid=grid, in_specs=in_specs, out_specs=out_specs, out_shape=out_shape,
                             scratch_shapes=scratch_shapes, compiler_params=_params(sem))(*args)
        return list(res), []
    ni, no, ns, nc = len(in_specs), len(out_specs), len(scratch_shapes), len(comm.arrays)

    def wrapped(*refs):
        ins, cins = refs[:ni], refs[ni:ni + nc]
        outs, couts = refs[ni + nc:ni + nc + no], refs[ni + nc + no:ni + 2 * nc + no]
        scr, sems = refs[ni + 2 * nc + no:ni + 2 * nc + no + ns], refs[ni + 2 * nc + no + ns:]
        first = functools.reduce(jnp.logical_and, [pl.program_id(k) == 0 for k in range(len(grid))])
        last = functools.reduce(jnp.logical_and, [pl.program_id(k) == grid[k] - 1 for k in range(len(grid))])

        @pl.when(first)
        def _():
            comm.start(cins, couts, sems)

        body(*ins, *outs, *scr)

        @pl.when(last)
        def _():
            comm.finish(cins, couts, sems)

    res = pl.pallas_call(
        wrapped, name=name, grid=grid, in_specs=list(in_specs) + [HBM_SPEC] * nc,
        out_specs=list(out_specs) + [HBM_SPEC] * nc, out_shape=list(out_shape) + list(comm.out_shapes),
        scratch_shapes=list(scratch_shapes) + list(comm.sem_shapes),
        input_output_aliases={ni + t: no + t for t in range(nc)} if comm.aliased else {},
        compiler_params=_params(sem))(*args, *comm.arrays)
    return list(res[:no]), list(res[no:])


def _chips(x, y):
    return [(1 - x, y), (x, 1 - y), (1 - x, 1 - y)]


def _gather_comm(bufs):
    n = len(bufs)

    def copies(outs, sems):
        ici_send, ici_recv, d2d_send, d2d_recv = sems
        x, y, c = lax.axis_index("x"), lax.axis_index("y"), lax.axis_index("c")

        def half(t, slot, hc):
            hr = bufs[t].shape[1] // 2
            return outs[t].at[slot, pl.ds(pl.multiple_of(hc * hr, 16), hr), :]

        def ici(t, j, slot, px, py):
            return pltpu.make_async_remote_copy(src_ref=half(t, slot, c), dst_ref=half(t, slot, c),
                                                send_sem=ici_send.at[3 * t + j], recv_sem=ici_recv.at[3 * t + j],
                                                device_id=(px, py, c), device_id_type=MESH)

        def d2d(t, j, slot, hc):
            return pltpu.make_async_remote_copy(src_ref=half(t, slot, hc), dst_ref=half(t, slot, hc),
                                                send_sem=d2d_send.at[3 * t + j], recv_sem=d2d_recv.at[3 * t + j],
                                                device_id=(x, y, 1 - c), device_id_type=MESH)

        peers = [(t, j, px, py) for t in range(n) for j, (px, py) in enumerate(_chips(x, y))]
        return ici, d2d, peers, 2 * x + y, c

    def start(ins, outs, sems):
        ici, _, peers, me, _ = copies(outs, sems)
        for t, j, px, py in peers:
            ici(t, j, me, px, py).start()

    def finish(ins, outs, sems):
        ici, d2d, peers, me, c = copies(outs, sems)
        for t, j, px, py in peers:
            ici(t, j, 2 * px + py, px, py).wait_recv()
            d2d(t, j, 2 * px + py, c).start()
        for t, j, px, py in peers:
            d2d(t, j, 2 * px + py, 1 - c).wait_recv()
        for t, j, px, py in peers:
            ici(t, j, me, px, py).wait_send()
            d2d(t, j, 2 * px + py, c).wait_send()

    return Comm(list(bufs), [SDS(b.shape, b.dtype) for b in bufs], True, [pltpu.SemaphoreType.DMA((3 * n,))] * 4,
                start, finish)


def _sibling_exchange_comm(gs):
    n = len(gs)

    def copies(ins, outs, sems):
        x, y, c = lax.axis_index("x"), lax.axis_index("y"), lax.axis_index("c")
        cps = []
        for t in range(n):
            hr = gs[t].shape[1] // 2
            src = ins[t].at[:, pl.ds(pl.multiple_of((1 - c) * hr, SUBLANES), hr), :]
            cps.append(pltpu.make_async_remote_copy(src_ref=src, dst_ref=outs[t], send_sem=sems[0].at[t],
                                                    recv_sem=sems[1].at[t], device_id=(x, y, 1 - c),
                                                    device_id_type=MESH))
        return cps

    def start(ins, outs, sems):
        for cp in copies(ins, outs, sems):
            cp.start()

    def finish(ins, outs, sems):
        for cp in copies(ins, outs, sems):
            cp.wait()

    return Comm(list(gs), [SDS((N_CHIPS, g.shape[1] // 2, g.shape[2]), g.dtype) for g in gs], False,
                [pltpu.SemaphoreType.DMA((n,))] * 2, start, finish)


def _chip_exchange_comm(ss):
    n = len(ss)

    def copies(ins, outs, sems):
        x, y, c = lax.axis_index("x"), lax.axis_index("y"), lax.axis_index("c")
        me = 2 * x + y

        def copy(t, j, px, py, src_slot, dst_slot):
            return pltpu.make_async_remote_copy(src_ref=ins[t].at[src_slot], dst_ref=outs[t].at[dst_slot],
                                                send_sem=sems[0].at[3 * t + j], recv_sem=sems[1].at[3 * t + j],
                                                device_id=(px, py, c), device_id_type=MESH)

        peers = [(t, j, px, py) for t in range(n) for j, (px, py) in enumerate(_chips(x, y))]
        return copy, peers, me

    def start(ins, outs, sems):
        copy, peers, me = copies(ins, outs, sems)
        for t, j, px, py in peers:
            copy(t, j, px, py, 2 * px + py, me).start()

    def finish(ins, outs, sems):
        copy, peers, me = copies(ins, outs, sems)
        for t, j, px, py in peers:
            copy(t, j, px, py, me, 2 * px + py).wait_recv()
        for t, j, px, py in peers:
            copy(t, j, px, py, 2 * px + py, me).wait_send()

    return Comm(list(ss), [SDS(s.shape, s.dtype) for s in ss], False, [pltpu.SemaphoreType.DMA((3 * n,))] * 2,
                start, finish)


def _sibling_swap_comm(fs):
    n = len(fs)

    def copies(ins, outs, sems):
        x, y, c = lax.axis_index("x"), lax.axis_index("y"), lax.axis_index("c")
        return [pltpu.make_async_remote_copy(src_ref=ins[t], dst_ref=outs[t], send_sem=sems[0].at[t],
                                             recv_sem=sems[1].at[t], device_id=(x, y, 1 - c), device_id_type=MESH)
                for t in range(n)]

    def start(ins, outs, sems):
        for cp in copies(ins, outs, sems):
            cp.start()

    def finish(ins, outs, sems):
        for cp in copies(ins, outs, sems):
            cp.wait()

    return Comm(list(fs), [SDS(f.shape, f.dtype) for f in fs], False, [pltpu.SemaphoreType.DMA((n,))] * 2,
                start, finish)


def _row_tile(rows, cap=512):
    best = SUBLANES
    for tr in range(SUBLANES, min(rows, cap) + 1, SUBLANES):
        if rows % tr == 0:
            best = tr
    return best


def _add_halves(g4, r4, c_idx, name):
    _, hr, lanes = r4.shape
    tr = _row_tile(hr)
    nb = hr // tr

    def body(c_ref, a_ref, b_ref, o_ref):
        o_ref[...] = (a_ref[...] + b_ref[...]).astype(o_ref.dtype)

    grid_spec = pltpu.PrefetchScalarGridSpec(
        num_scalar_prefetch=1, grid=(N_CHIPS, nb),
        in_specs=[pl.BlockSpec((None, tr, lanes), lambda q, i, c_ref: (q, c_ref[0] * nb + i, 0)),
                  pl.BlockSpec((None, tr, lanes), lambda q, i, c_ref: (q, i, 0))],
        out_specs=pl.BlockSpec((None, tr, lanes), lambda q, i, c_ref: (q, i, 0)))
    return pl.pallas_call(body, name=name, grid_spec=grid_spec, out_shape=SDS(r4.shape, BF16),
                          compiler_params=_params(("parallel", "parallel")))(c_idx, g4, r4)


def _sum_chips(r4, s4, me_idx, name):
    _, rows, lanes = r4.shape
    tr = _row_tile(rows)

    def body(me_ref, a_ref, b_ref, c_ref, d_ref, own_ref, o_ref):
        own = own_ref[...].astype(F32)
        p = [jnp.where(me_ref[0] == q, own, ref[...].astype(F32)) for q, ref in enumerate((a_ref, b_ref, c_ref, d_ref))]
        o_ref[...] = ((p[0] + p[1]) + p[2]) + p[3]

    other = lambda q: (lambda i, me_ref: (jnp.where(me_ref[0] == q, (q + 1) % N_CHIPS, q), i, 0))
    grid_spec = pltpu.PrefetchScalarGridSpec(
        num_scalar_prefetch=1, grid=(rows // tr,),
        in_specs=[pl.BlockSpec((None, tr, lanes), other(q)) for q in range(N_CHIPS)]
        + [pl.BlockSpec((None, tr, lanes), lambda i, me_ref: (me_ref[0], i, 0))],
        out_specs=pl.BlockSpec((tr, lanes), lambda i, me_ref: (i, 0)))
    return pl.pallas_call(body, name=name, grid_spec=grid_spec, out_shape=SDS((rows, lanes), F32),
                          compiler_params=_params(("parallel",)))(me_idx, r4, r4, r4, r4, s4)


def _adamw(wf, g_own, g_other, mf, vf, c_idx, name):
    rows, lanes = wf.shape
    hr = rows // 2
    tr = _row_tile(hr)
    nb = hr // tr
    c1 = 1.0 / (1.0 - ADAM_B1 ** ADAM_STEP)
    c2 = 1.0 / (1.0 - ADAM_B2 ** ADAM_STEP)

    def body(c_ref, w_ref, go_ref, gx_ref, m_ref, v_ref, g_ref, d_ref, nm_ref, nv_ref):
        gv = jnp.where(pl.program_id(0) == c_ref[0], go_ref[...], gx_ref[...])
        m = ADAM_B1 * m_ref[...] + (1.0 - ADAM_B1) * gv
        v = ADAM_B2 * v_ref[...] + (1.0 - ADAM_B2) * (gv * gv)
        g_ref[...] = gv
        d_ref[...] = -ADAM_LR * ((m * c1) / (jnp.sqrt(v * c2) + ADAM_EPS) + ADAM_WD * w_ref[...])
        nm_ref[...] = m
        nv_ref[...] = v

    full = pl.BlockSpec((tr, lanes), lambda h, i, c_ref: (h * nb + i, 0))
    half = pl.BlockSpec((tr, lanes), lambda h, i, c_ref: (i, 0))
    grid_spec = pltpu.PrefetchScalarGridSpec(num_scalar_prefetch=1, grid=(2, nb),
                                             in_specs=[full, half, half, full, full], out_specs=[full] * 4)
    return pl.pallas_call(body, name=name, grid_spec=grid_spec, out_shape=[SDS((rows, lanes), F32)] * 4,
                          compiler_params=_params(("parallel", "parallel")))(c_idx, wf, g_own, g_other, mf, vf)


BIG = ("ffn1_w_gate", "ffn1_w_up", "ffn1_w_down", "ffn2_w_gate", "ffn2_w_up", "ffn2_w_down", "w_out", "w_in")
TRANSPOSED = ("ffn1_w_gate", "ffn1_w_up", "ffn2_w_gate", "ffn2_w_up")
PACKED = ("rwkv_w2", "rwkv_a2", "rwkv_g2")
SMALL_SHAPES = {"ffn1_norm": (1, D_MODEL), "mix_norm": (1, D_MODEL), "hgrn_lb_logits": (2, W_A),
                "hgrn_out_norm": (1, W_A), "rwkv_shift_mu": (1, N_RWKV_COLS), "rwkv_w0": (1, W_B),
                "rwkv_a0": (1, W_B), "rwkv_k_k": (1, W_B), "rwkv_k_a": (1, W_B),
                "rwkv_r_k": (1, HB_HEADS, HB_DIM), "rwkv_gn_w": (1, W_B), "rwkv_gn_b": (1, W_B),
                "ffn2_norm": (1, D_MODEL), "final_norm": (D_MODEL,)}
PACK_ELEMS = sum(_numel(_shard_shape(n)) for n in PACKED) + sum(_numel(SMALL_SHAPES[n]) for n in SMALL)
PACK_ROWS = -(-PACK_ELEMS // (32 * LANES)) * 32


def _to_rows(name, shard):
    return shard[0].T if name in TRANSPOSED else shard[0]


def _from_rows(name, rows):
    return (rows.T if name in TRANSPOSED else rows)[None]


def _pack(sharded, small):
    flat = jnp.concatenate([sharded[n].reshape(-1) for n in PACKED] + [small[n].reshape(-1) for n in SMALL])
    return jnp.pad(flat, (0, PACK_ROWS * LANES - flat.shape[0])).reshape(PACK_ROWS, LANES)


def _unpack(packed):
    flat, out, off = packed.reshape(-1), {}, 0
    for n in PACKED:
        shp = _shard_shape(n)
        out[n] = flat[off:off + _numel(shp)].reshape((1,) + shp)
        off += _numel(shp)
    for n in SMALL:
        shp = SMALL_SHAPES[n]
        out[n] = flat[off:off + _numel(shp)].reshape(shp)
        off += _numel(shp)
    return out


def _quarter(full, name, q):
    shape, ax = SHARDED_SHAPES[name]
    w = shape[ax] // N_CHIPS
    return lax.slice_in_dim(full, q * w, (q + 1) * w, axis=ax)


def kernel(x, ffn1_norm, ffn1_w_gate, ffn1_w_up, ffn1_w_down, mix_norm, w_in, hgrn_lb_logits, hgrn_out_norm, rwkv_shift_mu, rwkv_w0, rwkv_w2, rwkv_a0, rwkv_a2, rwkv_g2, rwkv_k_k, rwkv_k_a, rwkv_r_k, rwkv_gn_w, rwkv_gn_b, w_out, ffn2_norm, ffn2_w_gate, ffn2_w_up, ffn2_w_down, final_norm, loss_target, m_ffn1_norm, m_ffn1_w_gate, m_ffn1_w_up, m_ffn1_w_down, m_mix_norm, m_w_in, m_hgrn_lb_logits, m_hgrn_out_norm, m_rwkv_shift_mu, m_rwkv_w0, m_rwkv_w2, m_rwkv_a0, m_rwkv_a2, m_rwkv_g2, m_rwkv_k_k, m_rwkv_k_a, m_rwkv_r_k, m_rwkv_gn_w, m_rwkv_gn_b, m_w_out, m_ffn2_norm, m_ffn2_w_gate, m_ffn2_w_up, m_ffn2_w_down, m_final_norm, v_ffn1_norm, v_ffn1_w_gate, v_ffn1_w_up, v_ffn1_w_down, v_mix_norm, v_w_in, v_hgrn_lb_logits, v_hgrn_out_norm, v_rwkv_shift_mu, v_rwkv_w0, v_rwkv_w2, v_rwkv_a0, v_rwkv_a2, v_rwkv_g2, v_rwkv_k_k, v_rwkv_k_a, v_rwkv_r_k, v_rwkv_gn_w, v_rwkv_gn_b, v_w_out, v_ffn2_norm, v_ffn2_w_gate, v_ffn2_w_up, v_ffn2_w_down, v_final_norm):
    args = dict(locals())
    wts = {n: args[n] for n in ALL_WEIGHTS}
    moms = {n: args["m_" + n] for n in ALL_WEIGHTS}
    vars_ = {n: args["v_" + n] for n in ALL_WEIGHTS}

    me = 2 * lax.axis_index("x") + lax.axis_index("y")
    c_idx = lax.axis_index("c").astype(jnp.int32).reshape(1)
    me_idx = me.astype(jnp.int32).reshape(1)
    shard_of = {n: _to_rows(n, wts[n]).astype(BF16) for n in BIG}
    shard_of["packed"] = _pack(wts, {n: wts[n] for n in SMALL}).astype(BF16)
    group = {"ffn1": BIG[0:3], "ffn2": BIG[3:6]}

    def slot_bufs(names):
        return [lax.dynamic_update_slice(jnp.zeros((N_CHIPS,) + shard_of[n].shape, BF16), shard_of[n][None],
                                         (me, 0, 0)) for n in names]

    def ffn_weights(tag, gathered):
        return {f"{tag}_wgt": gathered[0].reshape(D_FF, D_MODEL), f"{tag}_wut": gathered[1].reshape(D_FF, D_MODEL),
                f"{tag}_wd": gathered[2].reshape(D_FF, D_MODEL)}

    def w_in_weights(gathered):
        w_in_full = jnp.concatenate([gathered[0][q] for q in range(N_CHIPS)], axis=1)
        return {"w_in_h": w_in_full[:, :N_HGRN_COLS],
                "w_in_r": jnp.pad(w_in_full[:, N_HGRN_COLS:], ((0, 0), (0, N_RWKV_PAD - N_RWKV_COLS)))}

    def mixer_weights(gathered):
        w_out_full = gathered[0].reshape(D_MODEL, D_MODEL)
        packs = gathered[1].reshape(N_CHIPS, PACK_ROWS * LANES)
        full, off = {}, 0
        for n in PACKED:
            shp = _shard_shape(n)
            full[n] = jnp.concatenate([packs[q, off:off + _numel(shp)].reshape(shp) for q in range(N_CHIPS)], axis=1)
            off += _numel(shp)
        zrow = lambda nrow: jnp.zeros((nrow, W_B), BF16)
        return {"w_out": w_out_full,
                "w2_pad": jnp.concatenate([full["rwkv_w2"], zrow(LORA_PAD - 32)], axis=0),
                "a2_pad": jnp.concatenate([zrow(32), full["rwkv_a2"], zrow(LORA_PAD - 64)], axis=0),
                "g2_pad": jnp.concatenate([zrow(64), full["rwkv_g2"], zrow(LORA_PAD - 160)], axis=0)}

    plan = _Plan()
    w = {}
    plan.carry("ffn1_rms", lambda g: _gather_comm(slot_bufs(group["ffn1"][:2])),
               lambda res, w_: w_.update({"ffn1_wgt": res[0].reshape(D_FF, D_MODEL),
                                          "ffn1_wut": res[1].reshape(D_FF, D_MODEL)}))

    def after_gate_up(res, w_):
        w_["ffn1_wd"] = res[0].reshape(D_FF, D_MODEL)
        w_.update(w_in_weights(res[1:]))

    plan.carry("ffn1_gate_up", lambda g: _gather_comm(slot_bufs(("ffn1_w_down", "w_in"))), after_gate_up)
    plan.carry("ffn1_down", lambda g: _gather_comm(slot_bufs(("w_out", "packed"))),
               lambda res, w_: w_.update(mixer_weights(res)))
    plan.carry("rwkv_fwd", lambda g: _gather_comm(slot_bufs(group["ffn2"])),
               lambda res, w_: w_.update(ffn_weights("ffn2", res)))
    w["ffn1_norm"], w["ffn2_norm"] = ffn1_norm, ffn2_norm
    w["mix_norm"] = mix_norm
    w["lb0"], w["lb1"] = hgrn_lb_logits[0:1], hgrn_lb_logits[1:2]
    w["hgrn_out_norm"] = hgrn_out_norm
    w["mu_pad"] = jnp.pad(rwkv_shift_mu, ((0, 0), (0, N_RWKV_PAD - N_RWKV_COLS)))
    for n in ("rwkv_w0", "rwkv_a0", "rwkv_k_k", "rwkv_k_a", "rwkv_gn_w", "rwkv_gn_b"):
        w[n] = wts[n]
    w["rwkv_r_k"] = rwkv_r_k.reshape(1, W_B)
    w["final_norm"] = final_norm.reshape(1, D_MODEL)

    def reduce_rows(names, gs):
        r1 = _run_comm(_sibling_exchange_comm(gs), "grad_sibling_exchange")
        s4 = [_add_halves(gt, rt, c_idx, f"grad_add_halves_{n}") for gt, rt, n in zip(gs, r1, names)]
        r2 = _run_comm(_chip_exchange_comm(s4), "grad_chip_exchange")
        return [_sum_chips(rt, st, me_idx, f"grad_sum_chips_{n}") for rt, st, n in zip(r2, s4, names)]

    early, swapped = {}, {}

    def reduce_early(names, grads_of, sibling_host, chips_host, swap_host):
        def sibling_comm(g):
            early[names, "gs"] = grads_of(g)
            return _sibling_exchange_comm(early[names, "gs"])

        def after_sibling(res, w_):
            early[names, "s4"] = [_add_halves(gt, rt, c_idx, f"grad_add_halves_{n}")
                                  for gt, rt, n in zip(early[names, "gs"], res, names)]

        def after_chips(res, w_):
            early.update(zip(names, [_sum_chips(rt, st, me_idx, f"grad_sum_chips_{n}")
                                     for rt, st, n in zip(res, early[names, "s4"], names)]))

        plan.carry(sibling_host, sibling_comm, after_sibling)
        plan.carry(chips_host, lambda g: _chip_exchange_comm(early[names, "s4"]), after_chips)
        if swap_host:
            plan.carry(swap_host, lambda g: _sibling_swap_comm([early[n] for n in names]),
                       lambda res, w_: swapped.update(zip(names, res)))

    def proj_grads(g):
        g_w_in = jnp.concatenate([g["w_in_h"], g["w_in_r"][:, :N_RWKV_COLS]], axis=1)
        return [g["w_out"].reshape(N_CHIPS, -1, D_MODEL),
                jnp.stack([_quarter(g_w_in, "w_in", q) for q in range(N_CHIPS)])]

    rows_of = lambda keys: (lambda g: [g[k].reshape(N_CHIPS, -1, D_MODEL) for k in keys])
    reduce_early(group["ffn2"], rows_of(("ffn2_wgt", "ffn2_wut", "ffn2_wd")), "hgrn_bwd", "rwkv_bwd", "rwkv_prep_bwd")
    reduce_early(("w_out", "w_in"), proj_grads, "mix_drms", "ffn1_dact", "ffn1_dwg")
    reduce_early(("ffn1_w_down",), rows_of(("ffn1_wd",)), "ffn1_dwg", "ffn1_dwu", "ffn1_dh_g")
    reduce_early(("ffn1_w_gate",), rows_of(("ffn1_wgt",)), "ffn1_dwu", "ffn1_dh_g", "ffn1_dh_u")
    reduce_early(("ffn1_w_up",), rows_of(("ffn1_wut",)), "ffn1_dh_g", "ffn1_dh_u", None)
    loss_slab, grad_x, g = _local_step(x[0], loss_target[0], w, plan)
    loss = lax.psum(loss_slab[0, 0], ("x", "y", "c"))

    gfull = {
        "rwkv_w2": g["w2_pad"][0:32], "rwkv_a2": g["a2_pad"][32:64], "rwkv_g2": g["g2_pad"][64:160],
    }
    gsmall = {
        "ffn1_norm": g["ffn1_norm"], "mix_norm": g["mix_norm"],
        "hgrn_lb_logits": jnp.concatenate([g["lb0"], g["lb1"]], axis=0), "hgrn_out_norm": g["hgrn_out_norm"],
        "rwkv_shift_mu": g["mu_pad"][:, :N_RWKV_COLS], "rwkv_w0": g["rwkv_w0"], "rwkv_a0": g["rwkv_a0"],
        "rwkv_k_k": g["rwkv_k_k"], "rwkv_k_a": g["rwkv_k_a"], "rwkv_r_k": g["rwkv_r_k"],
        "rwkv_gn_w": g["rwkv_gn_w"], "rwkv_gn_b": g["rwkv_gn_b"], "ffn2_norm": g["ffn2_norm"],
        "final_norm": g["final_norm"],
    }
    packed = jnp.stack([_pack({n: _quarter(gfull[n], n, q) for n in PACKED}, gsmall) for q in range(N_CHIPS)])
    early["packed"], = reduce_rows(["packed"], [packed])
    last = ["ffn1_w_up", "packed"]
    swapped.update(zip(last, _run_comm(_sibling_swap_comm([early[n] for n in last]), "grad_sibling_swap")))
    names = list(BIG) + ["packed"]
    own, other = [early[n] for n in names], [swapped[n] for n in names]

    def rows_list(d):
        return [_to_rows(n, d[n]) for n in BIG] + [_pack(d, {n: d[n] for n in SMALL})]

    outs = [_adamw(wt, go, gx, mt, vt, c_idx, f"adamw_{n}")
            for wt, go, gx, mt, vt, n in zip(rows_list(wts), own, other, rows_list(moms), rows_list(vars_), names)]
    results = []
    for k in range(4):
        per = [outs[i][k] for i in range(len(names))]
        d = {n: _from_rows(n, z) for n, z in zip(BIG, per[:-1])}
        d.update(_unpack(per[-1]))
        results.append(d)
    return (loss, grad_x[None], *[r[n] for r in results for n in ALL_WEIGHTS])
```

```python
import collections
import functools

import jax
import jax.numpy as jnp
from jax import lax
from jax.experimental import pallas as pl
from jax.experimental.pallas import tpu as pltpu

F32 = jnp.float32
BF16 = jnp.bfloat16
SDS = jax.ShapeDtypeStruct
MESH = pl.DeviceIdType.MESH

D_MODEL = 1024
D_FF = 2816
W_A = 512
W_B = 512
HA_HEADS, HA_DIM = 4, 128
HB_HEADS, HB_DIM = 8, 64
HGRN_CHUNK = 64
HGRN_GROUP = 8
RWKV_CHUNK = 16
RWKV_GROUP = 8
N_HGRN_COLS = 4 * W_A
N_RWKV_COLS = 3 * W_B + 32 + 32 + 96
N_RWKV_PAD = 1792
LORA_PAD = 256
NORM_EPS = 1e-6
RWKV_GN_EPS = 64e-5
L2_EPS = 1e-12
ADAM_LR, ADAM_B1, ADAM_B2, ADAM_EPS, ADAM_WD, ADAM_STEP = 0.001, 0.9, 0.999, 1e-8, 0.01, 10

N_CHIPS = 4
VMEM_LIMIT_V7X = 56 * 1024 * 1024
LANES = 1024

SHARDED_SHAPES = {
    "ffn1_w_gate": ((D_MODEL, D_FF), 1), "ffn1_w_up": ((D_MODEL, D_FF), 1), "ffn1_w_down": ((D_FF, D_MODEL), 0),
    "w_in": ((D_MODEL, N_HGRN_COLS + N_RWKV_COLS), 1), "rwkv_w2": ((32, W_B), 1), "rwkv_a2": ((32, W_B), 1),
    "rwkv_g2": ((96, W_B), 1), "w_out": ((D_MODEL, D_MODEL), 0),
    "ffn2_w_gate": ((D_MODEL, D_FF), 1), "ffn2_w_up": ((D_MODEL, D_FF), 1), "ffn2_w_down": ((D_FF, D_MODEL), 0),
}
SMALL = ("ffn1_norm", "mix_norm", "hgrn_lb_logits", "hgrn_out_norm", "rwkv_shift_mu", "rwkv_w0", "rwkv_a0",
         "rwkv_k_k", "rwkv_k_a", "rwkv_r_k", "rwkv_gn_w", "rwkv_gn_b", "ffn2_norm", "final_norm")
ALL_WEIGHTS = ("ffn1_norm", "ffn1_w_gate", "ffn1_w_up", "ffn1_w_down", "mix_norm", "w_in", "hgrn_lb_logits",
               "hgrn_out_norm", "rwkv_shift_mu", "rwkv_w0", "rwkv_w2", "rwkv_a0", "rwkv_a2", "rwkv_g2", "rwkv_k_k",
               "rwkv_k_a", "rwkv_r_k", "rwkv_gn_w", "rwkv_gn_b", "w_out", "ffn2_norm", "ffn2_w_gate", "ffn2_w_up",
               "ffn2_w_down", "final_norm")


def _shard_shape(name):
    shape, ax = SHARDED_SHAPES[name]
    return tuple(s // N_CHIPS if i == ax else s for i, s in enumerate(shape))


def _numel(shape):
    n = 1
    for s in shape:
        n *= s
    return n


def _params(sem=None):
    return pltpu.CompilerParams(dimension_semantics=sem, vmem_limit_bytes=VMEM_LIMIT_V7X)


def _split2(x):
    hi = x.astype(BF16)
    return hi, (x.astype(F32) - hi.astype(F32)).astype(BF16)


def _dg(x, y, cx, cy, hi):
    dn = (((cx,), (cy,)), ((), ()))
    dot = lambda p, q: lax.dot_general(p, q, dn, preferred_element_type=F32)
    if hi == "x3":
        (xh, xl), (yh, yl) = _split2(x), _split2(y)
        return dot(xh, yh) + (dot(xh, yl) + dot(xl, yh))
    return dot(x.astype(BF16), y.astype(BF16))


def _make_mm(hi, cotangent_forms=None):
    @jax.custom_vjp
    def nn(x, y):
        return _dg(x, y, 1, 0, hi)

    @jax.custom_vjp
    def nt(x, y):
        return _dg(x, y, 1, 1, hi)

    @jax.custom_vjp
    def tn(x, y):
        return _dg(x, y, 0, 0, hi)

    bnn, bnt, btn = cotangent_forms or (nn, nt, tn)
    nn.defvjp(lambda x, y: (nn(x, y), (x, y)), lambda r, g: (bnt(g, r[1]), btn(r[0], g)))
    nt.defvjp(lambda x, y: (nt(x, y), (x, y)), lambda r, g: (bnn(g, r[1]), btn(g, r[0])))
    tn.defvjp(lambda x, y: (tn(x, y), (x, y)), lambda r, g: (bnt(r[1], g), bnn(r[0], g)))
    return nn, nt, tn


_nn, _nt, _tn = _make_mm(False)
_nn_x3, _nt_x3, _tn_x3 = _make_mm("x3", (_nn, _nt, _tn))


def _tri_apply(x, transpose):
    c = x.shape[0]
    tri = (lax.broadcasted_iota(jnp.int32, (c, c), 1) <= lax.broadcasted_iota(jnp.int32, (c, c), 0)).astype(BF16)
    dn = (((0 if transpose else 1,), (0,)), ((), ()))
    p1, p2 = _split2(x)
    dot = lambda p: lax.dot_general(tri, p, dn, preferred_element_type=F32)
    return dot(p1) + dot(p2)


@jax.custom_vjp
def _cumsum_rows(x):
    return _tri_apply(x, False)


_cumsum_rows.defvjp(lambda x: (_tri_apply(x, False), None), lambda _, g: (_tri_apply(g, True),))


def _sigmoid(x):
    return 1.0 / (1.0 + jnp.exp(-x))


def _silu(x):
    return x * _sigmoid(x)


def _softplus(z):
    return jnp.maximum(z, 0.0) + jnp.log(1.0 + jnp.exp(-jnp.abs(z)))


def _mm(a, b, *, ta=False, tb=False, tm, tn, tk, name, out_dtype=F32, res=None, scale=None, comm=None):
    m = a.shape[1] if ta else a.shape[0]
    kdim = a.shape[0] if ta else a.shape[1]
    n = b.shape[0] if tb else b.shape[1]
    assert (b.shape[1] if tb else b.shape[0]) == kdim
    tm, tn, tk = min(tm, m), min(tn, n), min(tk, kdim)
    assert m % tm == 0 and n % tn == 0 and kdim % tk == 0, (name, m, n, kdim)
    nk = kdim // tk
    a_spec = pl.BlockSpec((tk, tm), lambda i, j, k: (k, i)) if ta else pl.BlockSpec((tm, tk), lambda i, j, k: (i, k))
    b_spec = pl.BlockSpec((tn, tk), lambda i, j, k: (j, k)) if tb else pl.BlockSpec((tk, tn), lambda i, j, k: (k, j))
    o_spec = pl.BlockSpec((tm, tn), lambda i, j, k: (i, j))
    ca, cb = (0 if ta else 1), (1 if tb else 0)

    def body(*refs):
        if res is not None:
            a_ref, b_ref, r_ref, o_ref, acc_ref = refs
        else:
            a_ref, b_ref, o_ref, acc_ref = refs
        k = pl.program_id(2)

        @pl.when(k == 0)
        def _():
            acc_ref[...] = jnp.zeros_like(acc_ref)

        acc_ref[...] += _dg(a_ref[...], b_ref[...], ca, cb, False)

        @pl.when(k == nk - 1)
        def _():
            acc = acc_ref[...]
            if scale is not None:
                acc = acc * scale
            if res is not None:
                acc = r_ref[...] + acc
            o_ref[...] = acc.astype(out_dtype)

    in_specs = [a_spec, b_spec] + ([o_spec] if res is not None else [])
    args = (a, b) + ((res,) if res is not None else ())
    if comm is None:
        return pl.pallas_call(
            body, name=name, grid=(m // tm, n // tn, nk), in_specs=in_specs, out_specs=o_spec,
            out_shape=SDS((m, n), out_dtype), scratch_shapes=[pltpu.VMEM((tm, tn), F32)],
            compiler_params=_params(("parallel", "parallel", "arbitrary")))(*args)
    (out,), carried = _hosting_call(
        body, comm, name=name, grid=(m // tm, n // tn, nk), in_specs=in_specs, out_specs=[o_spec],
        out_shape=[SDS((m, n), out_dtype)], scratch_shapes=[pltpu.VMEM((tm, tn), F32)], args=args)
    return out, carried


def _row_spec(x, tm, tile_of=lambda i: i):
    if isinstance(x, tuple):
        arr, w, j = x
        return arr, pl.BlockSpec((tm, w), lambda i, j=j: (tile_of(i), j))
    return x, pl.BlockSpec((tm, x.shape[1]), lambda i: (tile_of(i), 0))


def _par_spec(p):
    if isinstance(p, tuple):
        arr, w, j = p
        return arr, pl.BlockSpec((arr.shape[0], w), lambda i, j=j: (0, j))
    return p, pl.BlockSpec(p.shape, lambda i: (0, 0))


def _store_groups(refs, groups, vals):
    for ref, idxs in zip(refs, groups):
        off = 0
        for ix in idxs:
            v = vals[ix]
            ref[:, off:off + v.shape[1]] = v.astype(ref.dtype)
            off += v.shape[1]


SUBLANES = 8


def _x_plan(xs, tm, t, tile_of=lambda i: i):
    arrays, specs, plan = [], [], []
    nb = tm // SUBLANES
    for x in xs:
        if isinstance(x, tuple) and isinstance(x[0], str):
            kind, arr, w, j = x
            if kind == "prev":
                halo = lambda i, j=j: (jnp.maximum(tile_of(i) * nb - 1, 0), j)
            else:
                halo = lambda i, j=j: (jnp.minimum((tile_of(i) + 1) * nb, t // SUBLANES - 1), j)
            arrays += [arr, arr]
            specs += [pl.BlockSpec((tm, w), lambda i, j=j: (tile_of(i), j)), pl.BlockSpec((SUBLANES, w), halo)]
            plan.append((kind, 2, w))
        else:
            arr, spec = _row_spec(x, tm, tile_of)
            arrays.append(arr)
            specs.append(spec)
            plan.append(("plain", 1, spec.block_shape[1]))
    return arrays, specs, plan


def _x_vals(refs, plan, tm, nt, tile_of=lambda i: i):
    vals, k = [], 0
    i = tile_of(pl.program_id(0))
    rows = lax.broadcasted_iota(jnp.int32, (tm, 1), 0)
    for kind, n, _ in plan:
        main = refs[k][...].astype(F32)
        if kind == "prev":
            edge = jnp.where(i == 0, 0.0, refs[k + 1][SUBLANES - 1:SUBLANES, :].astype(F32))
            main = jnp.where(rows == 0, edge, pltpu.roll(main, 1, 0))
        elif kind == "next":
            edge = jnp.where(i == nt - 1, 0.0, refs[k + 1][0:1, :].astype(F32))
            main = jnp.where(rows == tm - 1, edge, pltpu.roll(main, tm - 1, 0))
        vals.append(main)
        k += n
    return vals


def _tile_rows(xs, tm):
    arr = xs[0]
    if isinstance(arr, tuple):
        arr = arr[1] if isinstance(arr[0], str) else arr[0]
    return min(tm, arr.shape[0]), arr.shape[0]


def _rowwise(f, xs, params, out_groups, out_dtypes, *, tm, name, comm=None):
    tm, t = _tile_rows(xs, tm)
    nt = t // tm
    xa, xspecs, plan = _x_plan(xs, tm, t)
    pa, pspecs = (zip(*[_par_spec(p) for p in params]) if params else ((), ()))
    nxr, npar = len(xa), len(pa)
    x_sds = [SDS((tm, w), F32) for _, _, w in plan]
    p_sds = [SDS(s.block_shape, F32) for s in pspecs]
    outs_sds = jax.eval_shape(lambda *vals: f(*vals), *x_sds, *p_sds)
    widths = [sum(outs_sds[ix].shape[1] for ix in idxs) for idxs in out_groups]

    def body(*refs):
        vals = _x_vals(refs[:nxr], plan, tm, nt) + [r[...].astype(F32) for r in refs[nxr:nxr + npar]]
        outs = f(*vals)
        _store_groups(refs[nxr + npar:], out_groups, outs)

    res, carried = _hosting_call(
        body, comm, name=name, grid=(nt,), in_specs=list(xspecs) + list(pspecs),
        out_specs=[pl.BlockSpec((tm, w), lambda i: (i, 0)) for w in widths],
        out_shape=[SDS((t, w), dt) for w, dt in zip(widths, out_dtypes)], scratch_shapes=[], args=(*xa, *pa))
    return res if comm is None else (res, carried)


def _rowwise_bwd(f, xs, params, cots, *, x_grad, p_grad, dx_groups, dx_dtypes, tm, name, extra=None, comm=None,
                 fold_next=None):
    tm, t = _tile_rows(xs, tm)
    nt = t // tm
    tile_of = (lambda i: nt - 1 - i) if fold_next else (lambda i: i)
    xa, xspecs, plan = _x_plan(xs, tm, t, tile_of)
    pa, pspecs = (zip(*[_par_spec(p) for p in params]) if params else ((), ()))
    ca, cspecs = zip(*[_row_spec(c, tm, tile_of) for c in cots])
    extra = extra or {}
    ekeys = sorted(extra)
    ea, especs = (zip(*[_row_spec(extra[k], tm, tile_of) for k in ekeys]) if ekeys else ((), ()))
    nx, nxr, npar, nc, ne = len(plan), len(xa), len(pa), len(ca), len(ea)
    gx = [i for i in range(nx) if x_grad[i]]
    gp = [i for i in range(npar) if p_grad[i]]
    all_widths = [sum(plan[gx[ix]][2] for ix in idxs) for idxs in dx_groups]
    emitted = [k for k in range(len(dx_groups)) if not (fold_next and k == fold_next[1])]
    widths = [all_widths[k] for k in emitted]
    ng = len(emitted)

    def body(*refs):
        ins = refs[:nxr + npar + nc + ne]
        outs = refs[nxr + npar + nc + ne:]
        vals = _x_vals(ins[:nxr], plan, tm, nt, tile_of) + [r[...].astype(F32) for r in ins[nxr:nxr + npar]]
        cvals = tuple(r[...].astype(F32) for r in ins[nxr + npar:nxr + npar + nc])
        evals = [r[...].astype(F32) for r in ins[nxr + npar + nc:]]
        diff_idx = gx + [nx + i for i in gp]

        def g(*dargs):
            full = list(vals)
            for ix, v in zip(diff_idx, dargs):
                full[ix] = v
            return tuple(f(*full))

        _, vjp = jax.vjp(g, *[vals[ix] for ix in diff_idx])
        grads = vjp(cvals)
        dxs = list(grads[:len(gx)])
        for k, ev in zip(ekeys, evals):
            dxs[k] = dxs[k] + ev
        _store_groups(outs[:ng], [dx_groups[k] for k in emitted], dxs)
        i = pl.program_id(0)
        if fold_next:
            main_ref, carry_ref = outs[emitted.index(fold_next[0])], refs[-1]
            rows = lax.broadcasted_iota(jnp.int32, (tm, 1), 0)
            off = 0
            for ix in dx_groups[fold_next[1]]:
                piece = dxs[ix]
                cols = slice(off, off + piece.shape[1])
                edge = jnp.where(i == 0, 0.0, carry_ref[0:1, cols])
                main_ref[:, cols] += jnp.where(rows == tm - 1, edge, pltpu.roll(piece, tm - 1, 0))
                carry_ref[:, cols] = piece[:SUBLANES]
                off += piece.shape[1]
        for ref, gval in zip(outs[ng:ng + len(gp)], grads[len(gx):]):
            @pl.when(i == 0)
            def _(ref=ref):
                ref[...] = jnp.zeros_like(ref)
            ref[...] += gval

    dp_specs = [pl.BlockSpec(pspecs[i].block_shape, lambda i: (0, 0)) for i in gp]
    dp_shapes = [SDS(pspecs[i].block_shape, F32) for i in gp]
    scratch = [pltpu.VMEM((SUBLANES, all_widths[fold_next[1]]), F32)] if fold_next else []
    res, carried = _hosting_call(
        body, comm, name=name, grid=(nt,), in_specs=list(xspecs) + list(pspecs) + list(cspecs) + list(especs),
        out_specs=[pl.BlockSpec((tm, w), lambda i: (tile_of(i), 0)) for w in widths] + dp_specs,
        out_shape=[SDS((t, w), dt) for w, dt in zip(widths, dx_dtypes)] + dp_shapes, scratch_shapes=scratch,
        args=(*xa, *pa, *ca, *ea))
    return res if comm is None else (res, carried)


def _rms_f(x, g):
    return (x * lax.rsqrt(jnp.mean(x * x, axis=-1, keepdims=True) + NORM_EPS) * g,)


def _group_sum_impl(x, ones_bd):
    p1, p2 = _split2(x)
    dot = lambda p: lax.dot_general(p, ones_bd.astype(BF16), (((1,), (0,)), ((), ())), preferred_element_type=F32)
    return dot(p1) + dot(p2)


@jax.custom_vjp
def _group_sum(x, ones_bd):
    return _group_sum_impl(x, ones_bd)


_group_sum.defvjp(lambda x, o: (_group_sum_impl(x, o), o),
                  lambda o, g: (_group_sum_impl(g, o), jnp.zeros_like(o)))


def _rwkv_prep_f(r, k, v, lo, rp, kp, vp, lop, mu_r, mu_k, mu_v, mu_lo, w0, w2p, a0, a2p, g2p, k_k, k_a, ones_bd):
    r = r + mu_r * (rp - r)
    k = k + mu_k * (kp - k)
    v = v + mu_v * (vp - v)
    lo = lo + mu_lo * (lop - lo)
    w_log = -_softplus(-(w0 + _nn(jnp.tanh(lo), w2p))) - 0.5
    lw = -jnp.exp(w_log)
    a_g = _sigmoid(a0 + _nn(lo, a2p))
    g = _nn(_sigmoid(lo), g2p)
    kk = k * k_k
    kk = kk / jnp.maximum(jnp.sqrt(_group_sum(kk * kk, ones_bd)), L2_EPS)
    k2 = k * (1.0 + (a_g - 1.0) * k_a)
    return r, lw, k2, v, -kk, kk * a_g, g


def _rwkv_post_f(y, r, k2, v, g, r_k, gn_w, gn_b, ones_bd):
    inv_n = 1.0 / HB_DIM
    mean = _group_sum(y, ones_bd) * inv_n
    yc = y - mean
    var = _group_sum(yc * yc, ones_bd) * inv_n
    yn = yc * lax.rsqrt(var + RWKV_GN_EPS) * gn_w + gn_b
    bonus = _group_sum(r * k2 * r_k, ones_bd) * v
    return ((yn + bonus) * g,)


def _tri(c, strict=False):
    ii = lax.broadcasted_iota(jnp.int32, (c, c), 0)
    jj = lax.broadcasted_iota(jnp.int32, (c, c), 1)
    return (jj < ii) if strict else (jj <= ii)


def _hgrn_step(st0, q_a, f_a, i_a, g_a, l0, l1, onorm):
    nh, nj = len(q_a), len(q_a[0])
    c = q_a[0][0].shape[0]
    combos = [(j, h) for j in range(nj) for h in range(nh)]
    every = lambda fn: {q: fn(q) for q in combos}
    at_ = lambda d: (lambda q: d[q[1]][q[0]])
    qa_, fa_, ia_, ga_ = (at_(z) for z in (q_a, f_a, i_a, g_a))
    incl = _tri(c)
    rows = lax.broadcasted_iota(jnp.int32, (c, 1), 0)
    lb = []
    for h in range(nh):
        mx = jnp.maximum(l0[h], l1[h])
        e0, e1 = jnp.exp(l0[h] - mx), jnp.exp(l1[h] - mx)
        lb.append(e0 / (e0 + e1))
    forget = every(lambda q: lb[q[1]] + (1.0 - lb[q[1]]) * _sigmoid(fa_(q)))
    qs = every(lambda q: _silu(qa_(q)))
    kk = every(lambda q: 1.0 - forget[q])
    lf = every(lambda q: jnp.log(forget[q]))
    bcum = every(lambda q: _cumsum_rows(lf[q]))
    bref = every(lambda q: jnp.sum(jnp.where(rows <= c // 2, lf[q], 0.0), axis=0, keepdims=True))
    blast = every(lambda q: jnp.sum(lf[q], axis=0, keepdims=True))
    scores = every(lambda q: jnp.where(incl, _nt(qs[q] * jnp.exp(bcum[q] - bref[q]),
                                                 kk[q] * jnp.exp(bref[q] - bcum[q])), 0.0))
    intra = every(lambda q: _nn(scores[q], ia_(q)))
    qb = every(lambda q: qs[q] * jnp.exp(bcum[q]))
    upd = every(lambda q: _tn(ia_(q), kk[q] * jnp.exp(blast[q] - bcum[q])))
    dec = every(lambda q: jnp.exp(blast[q]))
    st = list(st0)
    o = {}
    for j in range(nj):
        for h in range(nh):
            o[(j, h)] = intra[(j, h)] + _nt(qb[(j, h)], st[h])
        st = [st[h] * dec[(j, h)] + upd[(j, h)] for h in range(nh)]
    out = every(lambda q: o[q] * lax.rsqrt(jnp.mean(o[q] * o[q], axis=-1, keepdims=True) + NORM_EPS)
                * onorm[q[1]] * _silu(ga_(q)))
    return [[out[(j, h)] for j in range(nj)] for h in range(nh)], st


def _hgrn_blocks(ref, nj, c):
    return [[ref[j * c:(j + 1) * c, h * HA_DIM:(h + 1) * HA_DIM] for j in range(nj)] for h in range(HA_HEADS)]


def _hgrn_cols(ref):
    return [ref[:, h * HA_DIM:(h + 1) * HA_DIM] for h in range(HA_HEADS)]


def _hgrn_fwd(p_h, l0, l1, onorm):
    t = p_h.shape[0]
    cc, nj = HGRN_CHUNK, HGRN_GROUP
    c = cc * nj
    n = t // c

    def body(q_ref, f_ref, i_ref, g_ref, l0_ref, l1_ref, on_ref, o_ref, hs_ref, st_ref):
        @pl.when(pl.program_id(0) == 0)
        def _():
            st_ref[...] = jnp.zeros_like(st_ref)

        hs_ref[0] = st_ref[...]
        o, st1 = _hgrn_step([st_ref[h] for h in range(HA_HEADS)],
                            *[_hgrn_blocks(ref, nj, cc) for ref in (q_ref, f_ref, i_ref, g_ref)],
                            _hgrn_cols(l0_ref), _hgrn_cols(l1_ref), _hgrn_cols(on_ref))
        for h in range(HA_HEADS):
            for j in range(nj):
                o_ref[j * cc:(j + 1) * cc, h * HA_DIM:(h + 1) * HA_DIM] = o[h][j]
            st_ref[h] = st1[h]

    col = lambda j: pl.BlockSpec((c, W_A), lambda i, j=j: (i, j))
    par = pl.BlockSpec((1, W_A), lambda i: (0, 0))
    return pl.pallas_call(
        body, name="hgrn_fwd", grid=(n,), in_specs=[col(0), col(1), col(2), col(3), par, par, par],
        out_specs=[pl.BlockSpec((c, W_A), lambda i: (i, 0)),
                   pl.BlockSpec((1, HA_HEADS, HA_DIM, HA_DIM), lambda i: (i, 0, 0, 0))],
        out_shape=[SDS((t, W_A), F32), SDS((n, HA_HEADS, HA_DIM, HA_DIM), F32)],
        scratch_shapes=[pltpu.VMEM((HA_HEADS, HA_DIM, HA_DIM), F32)],
        compiler_params=_params(("arbitrary",)))(p_h, p_h, p_h, p_h, l0, l1, onorm)


def _hgrn_bwd(p_h, l0, l1, onorm, hs, do, do_col, comm=None):
    t = p_h.shape[0]
    cc, nj = HGRN_CHUNK, HGRN_GROUP
    c = cc * nj
    n = t // c

    def body(q_ref, f_ref, i_ref, g_ref, l0_ref, l1_ref, on_ref, hs_ref, do_ref,
             dp_ref, dl0_ref, dl1_ref, don_ref, dst_ref):
        @pl.when(pl.program_id(0) == 0)
        def _():
            dst_ref[...] = jnp.zeros_like(dst_ref)
            dl0_ref[...] = jnp.zeros_like(dl0_ref)
            dl1_ref[...] = jnp.zeros_like(dl1_ref)
            don_ref[...] = jnp.zeros_like(don_ref)

        args = ([hs_ref[0, h] for h in range(HA_HEADS)],
                *[_hgrn_blocks(ref, nj, cc) for ref in (q_ref, f_ref, i_ref, g_ref)],
                _hgrn_cols(l0_ref), _hgrn_cols(l1_ref), _hgrn_cols(on_ref))
        _, vjp = jax.vjp(_hgrn_step, *args)
        dst0, dq, df, di, dg, dl0, dl1, don = vjp((_hgrn_blocks(do_ref, nj, cc),
                                                   [dst_ref[h] for h in range(HA_HEADS)]))
        for h in range(HA_HEADS):
            sl = slice(h * HA_DIM, (h + 1) * HA_DIM)
            for k, dv in enumerate((dq, df, di, dg)):
                for j in range(nj):
                    dp_ref[j * cc:(j + 1) * cc, k * W_A + h * HA_DIM:k * W_A + (h + 1) * HA_DIM] = dv[h][j]
            dl0_ref[:, sl] += dl0[h]
            dl1_ref[:, sl] += dl1[h]
            don_ref[:, sl] += don[h]
            dst_ref[h] = dst0[h]

    col = lambda j: pl.BlockSpec((c, W_A), lambda i, j=j: (n - 1 - i, j))
    par = pl.BlockSpec((1, W_A), lambda i: (0, 0))
    return _hosting_call(
        body, comm, name="hgrn_bwd", grid=(n,),
        in_specs=[col(0), col(1), col(2), col(3), par, par, par,
                  pl.BlockSpec((1, HA_HEADS, HA_DIM, HA_DIM), lambda i: (n - 1 - i, 0, 0, 0)),
                  pl.BlockSpec((c, W_A), lambda i: (n - 1 - i, do_col))],
        out_specs=[pl.BlockSpec((c, N_HGRN_COLS), lambda i: (n - 1 - i, 0)), par, par, par],
        out_shape=[SDS((t, N_HGRN_COLS), F32), SDS((1, W_A), F32), SDS((1, W_A), F32), SDS((1, W_A), F32)],
        scratch_shapes=[pltpu.VMEM((HA_HEADS, HA_DIM, HA_DIM), F32)],
        args=(p_h, p_h, p_h, p_h, l0, l1, onorm, hs, do))


HB_PAIRS = HB_HEADS // 2
PAIR_W = 2 * HB_DIM


def _head_lane_masks():
    lane = lax.broadcasted_iota(jnp.int32, (1, PAIR_W), 1)
    return (lane < HB_DIM).astype(F32), (lane >= HB_DIM).astype(F32)


@jax.custom_vjp
def _stack_heads(x):
    m0, m1 = _head_lane_masks()
    return jnp.concatenate([x * m0, x * m1], axis=0)


def _stack_heads_bwd(_, g):
    m0, m1 = _head_lane_masks()
    c = g.shape[0] // 2
    return (g[:c] * m0 + g[c:] * m1,)


_stack_heads.defvjp(lambda x: (_stack_heads(x), None), _stack_heads_bwd)


@jax.custom_vjp
def _unstack_heads(ys):
    c = ys.shape[0] // 2
    return ys[:c] + ys[c:]


_unstack_heads.defvjp(lambda ys: (_unstack_heads(ys), None), lambda _, g: (_stack_heads(g),))


def _same_head_block(c):
    ii = lax.broadcasted_iota(jnp.int32, (2 * c, 2 * c), 0)
    jj = lax.broadcasted_iota(jnp.int32, (2 * c, 2 * c), 1)
    same = (ii < c) == (jj < c)
    return same & (jj <= ii), same & (jj < ii), (ii == jj).astype(F32)


@jax.custom_vjp
def _rows_join(top, bottom):
    return jnp.concatenate([top, bottom], axis=0)


def _rows_join_bwd(n_top, g):
    return g[:n_top], g[n_top:]


_rows_join.defvjp(lambda top, bottom: (_rows_join(top, bottom), top.shape[0]), _rows_join_bwd)


def _rows_split_impl(x, n_top):
    return x[:n_top], x[n_top:]


_rows_split = jax.custom_vjp(_rows_split_impl, nondiff_argnums=(1,))
_rows_split.defvjp(lambda x, n_top: (_rows_split_impl(x, n_top), None),
                   lambda n_top, _, g: (jnp.concatenate([g[0], g[1]], axis=0),))


def _rwkv_step(s0, r, lw, k, v, a, b):
    npair, nj = len(r), len(r[0])
    c = r[0][0].shape[0]
    combos = [(j, p) for j in range(nj) for p in range(npair)]
    every = lambda fn: {q: fn(q) for q in combos}
    at_ = lambda d: (lambda q: d[q[1]][q[0]])
    r_, lw_, k_, v_, a_, b_ = (at_(z) for z in (r, lw, k, v, a, b))
    incl, strict, eye = _same_head_block(c)

    gam = every(lambda q: _cumsum_rows(lw_(q)))
    gtot = every(lambda q: jnp.sum(lw_(q), axis=0, keepdims=True))
    eneg = every(lambda q: jnp.exp(-gam[q]))
    edec = every(lambda q: jnp.exp(gtot[q] - gam[q]))
    at = every(lambda q: _stack_heads(a_(q) * jnp.exp(gam[q] - lw_(q))))
    rt = every(lambda q: _stack_heads(r_(q) * jnp.exp(gam[q])))
    bt = every(lambda q: _stack_heads(b_(q) * eneg[q]))
    kt = every(lambda q: _stack_heads(k_(q) * eneg[q]))
    bdec = every(lambda q: _stack_heads(b_(q) * edec[q]))
    kdec = every(lambda q: _stack_heads(k_(q) * edec[q]))
    vs = every(lambda q: _stack_heads(v_(q)))
    a_ab = every(lambda q: jnp.where(strict, _nt(at[q], bt[q]), 0.0))
    a_ak = every(lambda q: jnp.where(strict, _nt(at[q], kt[q]), 0.0))
    a_rb = every(lambda q: jnp.where(incl, _nt(rt[q], bt[q]), 0.0))
    a_rk = every(lambda q: jnp.where(incl, _nt(rt[q], kt[q]), 0.0))
    tinv = every(lambda q: eye + a_ab[q])
    pw = a_ab
    span = 2
    while span < c:
        pw = every(lambda q, pw=pw: _nn_x3(pw[q], pw[q]))
        tinv = every(lambda q, pw=pw, tinv=tinv: tinv[q] + _nn_x3(pw[q], tinv[q]))
        span *= 2
    akv = every(lambda q: _nn(a_ak[q], vs[q]))
    w1 = every(lambda q: _nn(tinv[q], at[q]))
    u0 = every(lambda q: _nn(tinv[q], akv[q]))
    wr = every(lambda q: _rows_join(w1[q], rt[q]))
    bk = every(lambda q: _rows_join(bdec[q], kdec[q]))
    yv = every(lambda q: _nn(a_rk[q], vs[q]))
    gdec = every(lambda q: jnp.exp(gtot[q]))

    s = list(s0)
    y = [[None] * nj for _ in range(npair)]
    for j in range(nj):
        both = {p: _rows_split(_nt(wr[(j, p)], s[p]), 2 * c) for p in range(npair)}
        u = {p: both[p][0] + u0[(j, p)] for p in range(npair)}
        for p in range(npair):
            y[p][j] = _unstack_heads(both[p][1] + _nn(a_rb[(j, p)], u[p]) + yv[(j, p)])
        s = [s[p] * gdec[(j, p)] + _tn(_rows_join(u[p], vs[(j, p)]), bk[(j, p)]) for p in range(npair)]
    return y, s


def _rwkv_blocks(ref, nj, c):
    return [[ref[j * c:(j + 1) * c, p * PAIR_W:(p + 1) * PAIR_W] for j in range(nj)] for p in range(HB_PAIRS)]


def _rwkv_fwd(seqs, comm=None):
    t = seqs[0].shape[0]
    c, nj = RWKV_CHUNK, RWKV_GROUP
    n = t // (c * nj)

    def body(r_ref, lw_ref, k_ref, v_ref, a_ref, b_ref, y_ref, hs_ref, st_ref):
        @pl.when(pl.program_id(0) == 0)
        def _():
            st_ref[...] = jnp.zeros_like(st_ref)

        hs_ref[0] = st_ref[...]
        s0 = [st_ref[p] for p in range(HB_PAIRS)]
        y, s1 = _rwkv_step(s0, *[_rwkv_blocks(ref, nj, c) for ref in (r_ref, lw_ref, k_ref, v_ref, a_ref, b_ref)])
        for p in range(HB_PAIRS):
            for j in range(nj):
                y_ref[j * c:(j + 1) * c, p * PAIR_W:(p + 1) * PAIR_W] = y[p][j]
            st_ref[p] = s1[p]

    seq = pl.BlockSpec((c * nj, W_B), lambda i: (i, 0))
    return _hosting_call(
        body, comm, name="rwkv_fwd", grid=(n,), in_specs=[seq] * 6,
        out_specs=[seq, pl.BlockSpec((1, HB_PAIRS, PAIR_W, PAIR_W), lambda i: (i, 0, 0, 0))],
        out_shape=[SDS((t, W_B), F32), SDS((n, HB_PAIRS, PAIR_W, PAIR_W), F32)],
        scratch_shapes=[pltpu.VMEM((HB_PAIRS, PAIR_W, PAIR_W), F32)], args=tuple(seqs))


def _rwkv_bwd(seqs, hs, dy, comm=None):
    t = seqs[0].shape[0]
    c, nj = RWKV_CHUNK, RWKV_GROUP
    n = t // (c * nj)

    def body(r_ref, lw_ref, k_ref, v_ref, a_ref, b_ref, hs_ref, dy_ref,
             dr_ref, dlw_ref, dk_ref, dv_ref, da_ref, db_ref, dst_ref):
        @pl.when(pl.program_id(0) == 0)
        def _():
            dst_ref[...] = jnp.zeros_like(dst_ref)

        s0 = [hs_ref[0, p] for p in range(HB_PAIRS)]
        seq_vals = [_rwkv_blocks(ref, nj, c) for ref in (r_ref, lw_ref, k_ref, v_ref, a_ref, b_ref)]
        _, vjp = jax.vjp(_rwkv_step, s0, *seq_vals)
        grads = vjp((_rwkv_blocks(dy_ref, nj, c), [dst_ref[p] for p in range(HB_PAIRS)]))
        for ref, gr in zip((dr_ref, dlw_ref, dk_ref, dv_ref, da_ref, db_ref), grads[1:]):
            for p in range(HB_PAIRS):
                for j in range(nj):
                    ref[j * c:(j + 1) * c, p * PAIR_W:(p + 1) * PAIR_W] = gr[p][j]
        m0, m1 = _head_lane_masks()
        rows0 = (lax.broadcasted_iota(jnp.int32, (PAIR_W, 1), 0) < HB_DIM).astype(F32)
        blocks = rows0 * m0 + (1.0 - rows0) * m1
        for p in range(HB_PAIRS):
            dst_ref[p] = grads[0][p] * blocks

    seq = pl.BlockSpec((c * nj, W_B), lambda i: (n - 1 - i, 0))
    return _hosting_call(
        body, comm, name="rwkv_bwd", grid=(n,),
        in_specs=[seq] * 6 + [pl.BlockSpec((1, HB_PAIRS, PAIR_W, PAIR_W), lambda i: (n - 1 - i, 0, 0, 0)), seq],
        out_specs=[seq] * 6, out_shape=[SDS((t, W_B), F32)] * 6,
        scratch_shapes=[pltpu.VMEM((HB_PAIRS, PAIR_W, PAIR_W), F32)], args=(*seqs, hs, dy))


def _final_loss(x3, fnorm, target, *, tm):
    t, d = x3.shape

    def body(x_ref, g_ref, t_ref, dx_ref, dg_ref, loss_ref):
        @pl.when(pl.program_id(0) == 0)
        def _():
            dg_ref[...] = jnp.zeros_like(dg_ref)
            loss_ref[...] = jnp.zeros_like(loss_ref)

        x, g = x_ref[...], g_ref[...]
        rinv = lax.rsqrt(jnp.mean(x * x, axis=-1, keepdims=True) + NORM_EPS)
        xh = x * rinv
        diff = xh * g - t_ref[...]
        loss_ref[...] += 0.5 * jnp.sum(jnp.mean(diff * diff, axis=-1, keepdims=True))
        dy = diff * (1.0 / d)
        dg_ref[...] += jnp.sum(dy * xh, axis=0, keepdims=True)
        dxh = dy * g
        dx_ref[...] = rinv * (dxh - xh * jnp.mean(dxh * xh, axis=-1, keepdims=True))

    row = pl.BlockSpec((tm, d), lambda i: (i, 0))
    return pl.pallas_call(
        body, name="final_loss", grid=(t // tm,), in_specs=[row, pl.BlockSpec((1, d), lambda i: (0, 0)), row],
        out_specs=[row, pl.BlockSpec((1, d), lambda i: (0, 0)), pl.BlockSpec((8, 128), lambda i: (0, 0))],
        out_shape=[SDS((t, d), F32), SDS((1, d), F32), SDS((8, 128), F32)],
        compiler_params=_params(("arbitrary",)))(x3, fnorm, target)


def _gate_up_act(h, wgt, wut, *, tm, tn, name, comm=None):
    t, d = h.shape
    tm = min(tm, t)

    def body(h_ref, g_ref, u_ref, a_out, u_out, act_out):
        hv = h_ref[...]
        a = _dg(hv, g_ref[...], 1, 1, False)
        u = _dg(hv, u_ref[...], 1, 1, False)
        a_out[...] = a.astype(a_out.dtype)
        u_out[...] = u.astype(u_out.dtype)
        act_out[...] = (_silu(a) * u).astype(act_out.dtype)

    wspec = pl.BlockSpec((tn, d), lambda i, j: (j, 0))
    ospec = pl.BlockSpec((tm, tn), lambda i, j: (i, j))
    return _hosting_call(
        body, comm, name=name, grid=(t // tm, D_FF // tn),
        in_specs=[pl.BlockSpec((tm, d), lambda i, j: (i, 0)), wspec, wspec], out_specs=[ospec, ospec, ospec],
        out_shape=[SDS((t, D_FF), BF16), SDS((t, D_FF), BF16), SDS((t, D_FF), BF16)], scratch_shapes=[],
        args=(h, wgt, wut))


def _dact_swiglu(dout, wd, a, u, *, tm, tn, name, comm=None):
    t, d = dout.shape
    tm = min(tm, t)

    def body(d_ref, w_ref, a_ref, u_ref, da_out, du_out):
        dact = 0.5 * _dg(d_ref[...], w_ref[...], 1, 1, False)
        av, uv = a_ref[...].astype(F32), u_ref[...].astype(F32)
        s = _sigmoid(av)
        da_out[...] = (dact * uv * (s * (1.0 + av * (1.0 - s)))).astype(da_out.dtype)
        du_out[...] = (dact * (av * s)).astype(du_out.dtype)

    tile = pl.BlockSpec((tm, tn), lambda i, j: (i, j))
    return _hosting_call(
        body, comm, name=name, grid=(t // tm, D_FF // tn),
        in_specs=[pl.BlockSpec((tm, d), lambda i, j: (i, 0)), pl.BlockSpec((tn, d), lambda i, j: (j, 0)), tile, tile],
        out_specs=[tile, tile], out_shape=[SDS((t, D_FF), BF16), SDS((t, D_FF), BF16)], scratch_shapes=[],
        args=(dout, wd, a, u))


class _Plan:
    def __init__(self):
        self.entries, self.counts = collections.defaultdict(list), {}

    def carry(self, host, comm_of, after):
        self.entries[host].append((comm_of, after))

    def comm(self, host, g):
        comms = [comm_of(g) for comm_of, _ in self.entries.get(host, [])]
        self.counts[host] = [len(c.arrays) for c in comms]
        return functools.reduce(_join_comms, comms) if comms else None

    def done(self, host, results, w):
        start = 0
        for (_, after), n in zip(self.entries.get(host, []), self.counts.get(host, [])):
            after(results[start:start + n], w)
            start += n


def _ffn_fwd(x, w, tag, plan, g):
    comm = plan.comm(f"{tag}_rms", g)
    res = _rowwise(_rms_f, [x], [w[f"{tag}_norm"]], [[0]], [BF16], tm=512, name=f"{tag}_rms", comm=comm)
    (h,), carried = res if comm is not None else (res, [])
    plan.done(f"{tag}_rms", carried, w)
    (a, u, act), carried = _gate_up_act(h, w[f"{tag}_wgt"], w[f"{tag}_wut"], tm=2048, tn=256, name=f"{tag}_gate_up",
                                        comm=plan.comm(f"{tag}_gate_up", g))
    plan.done(f"{tag}_gate_up", carried, w)
    comm = plan.comm(f"{tag}_down", g)
    out = _mm(act, w[f"{tag}_wd"], tm=1024, tn=D_MODEL, tk=D_FF, name=f"{tag}_down", res=x, scale=0.5, comm=comm)
    if comm is not None:
        out, carried = out
        plan.done(f"{tag}_down", carried, w)
    return out, (h, a, u, act)


def _ffn_bwd(dout, x, w, saved, tag, plan, g):
    h, a, u, act = saved

    def carrying(fn, host, *args, **kwargs):
        comm = plan.comm(host, g)
        res = fn(*args, name=host, comm=comm, **kwargs)
        out, carried = res if comm is not None else (res, [])
        plan.done(host, carried, w)
        return out

    (da, du), carried = _dact_swiglu(dout, w[f"{tag}_wd"], a, u, tm=2048, tn=256, name=f"{tag}_dact",
                                     comm=plan.comm(f"{tag}_dact", g))
    plan.done(f"{tag}_dact", carried, w)
    g[f"{tag}_wd"] = _mm(act, dout, ta=True, tm=D_FF // 2, tn=D_MODEL, tk=1024, name=f"{tag}_dwd", scale=0.5)
    g[f"{tag}_wgt"] = carrying(_mm, f"{tag}_dwg", da, h, ta=True, tm=D_FF // 2, tn=D_MODEL, tk=1024)
    g[f"{tag}_wut"] = carrying(_mm, f"{tag}_dwu", du, h, ta=True, tm=D_FF // 2, tn=D_MODEL, tk=1024)
    dh = carrying(_mm, f"{tag}_dh_g", da, w[f"{tag}_wgt"], tm=1024, tn=D_MODEL, tk=D_FF)
    dh = carrying(_mm, f"{tag}_dh_u", du, w[f"{tag}_wut"], tm=1024, tn=D_MODEL, tk=D_FF, res=dh)
    dx, g[f"{tag}_norm"] = carrying(_rowwise_bwd, f"{tag}_drms", _rms_f, [x], [w[f"{tag}_norm"]], [dh],
                                    x_grad=[True], p_grad=[True], dx_groups=[[0]], dx_dtypes=[F32], tm=512,
                                    extra={0: dout})
    return dx


def _local_step(x, target, w, plan=None):
    plan = plan or _Plan()
    ones_bd = jnp.kron(jnp.eye(HB_HEADS, dtype=F32), jnp.ones((HB_DIM, HB_DIM), F32))
    g = {}
    x1, ffn1_saved = _ffn_fwd(x, w, "ffn1", plan, g)
    hm, = _rowwise(_rms_f, [x1], [w["mix_norm"]], [[0]], [BF16], tm=512, name="mix_rms")
    p_h = _mm(hm, w["w_in_h"], tm=2048, tn=256, tk=D_MODEL, name="inproj_h")
    p_r = _mm(hm, w["w_in_r"], tm=2048, tn=256, tk=D_MODEL, name="inproj_r")
    o_a, hgrn_states = _hgrn_fwd(p_h, w["lb0"], w["lb1"], w["hgrn_out_norm"])

    mu = w["mu_pad"]
    prep_xs = [(p_r, W_B, 0), (p_r, W_B, 1), (p_r, W_B, 2), (p_r, LORA_PAD, 6),
               ("prev", p_r, W_B, 0), ("prev", p_r, W_B, 1), ("prev", p_r, W_B, 2), ("prev", p_r, LORA_PAD, 6)]
    prep_ps = [(mu, W_B, 0), (mu, W_B, 1), (mu, W_B, 2), (mu, LORA_PAD, 6), w["rwkv_w0"], w["w2_pad"], w["rwkv_a0"],
               w["a2_pad"], w["g2_pad"], w["rwkv_k_k"], w["rwkv_k_a"], ones_bd]
    prep_f = _rwkv_prep_f
    r, lw, k2, v, a_vec, b_vec, gate = _rowwise(prep_f, prep_xs, prep_ps, [[0], [1], [2], [3], [4], [5], [6]],
                                                [F32] * 7, tm=256, name="rwkv_prep")
    seqs = [r, lw, k2, v, a_vec, b_vec]
    (y, rwkv_states), carried = _rwkv_fwd(seqs, comm=plan.comm("rwkv_fwd", g))
    plan.done("rwkv_fwd", carried, w)
    post_f = _rwkv_post_f
    post_xs = [y, r, k2, v, gate]
    post_ps = [w["rwkv_r_k"], w["rwkv_gn_w"], w["rwkv_gn_b"], ones_bd]
    o, = _rowwise(lambda o_a_, *rest: (o_a_,) + tuple(post_f(*rest)), [o_a] + post_xs, post_ps, [[0, 1]], [F32],
                  tm=256, name="rwkv_post")
    x2 = _mm(o, w["w_out"], tm=2048, tn=256, tk=D_MODEL, name="outproj", res=x1)
    x3, ffn2_saved = _ffn_fwd(x2, w, "ffn2", plan, g)
    dx3, g["final_norm"], loss = _final_loss(x3, w["final_norm"], target, tm=256)

    dx2 = _ffn_bwd(dx3, x2, w, ffn2_saved, "ffn2", plan, g)
    do = _mm(dx2, w["w_out"], tb=True, tm=2048, tn=256, tk=D_MODEL, name="outproj_do")
    g["w_out"] = _mm(o, dx2, ta=True, tm=D_MODEL, tn=D_MODEL, tk=1024, name="outproj_dw")

    (dp_h, g["lb0"], g["lb1"], g["hgrn_out_norm"]), carried = _hgrn_bwd(
        p_h, w["lb0"], w["lb1"], w["hgrn_out_norm"], hgrn_states, do, 0, comm=plan.comm("hgrn_bwd", g))
    plan.done("hgrn_bwd", carried, w)
    post_out = _rowwise_bwd(post_f, post_xs, post_ps, [(do, W_B, 1)], x_grad=[True] * 5, p_grad=[True] * 3 + [False],
                            dx_groups=[[0], [1], [2], [3], [4]], dx_dtypes=[F32] * 5, tm=256, name="rwkv_post_bwd")
    dy, dr1, dk1, dv1, dgate, g["rwkv_r_k"], g["rwkv_gn_w"], g["rwkv_gn_b"] = post_out
    (dr2, dlw, dk2, dv2, da_vec, db_vec), carried = _rwkv_bwd(seqs, rwkv_states, dy, comm=plan.comm("rwkv_bwd", g))
    plan.done("rwkv_bwd", carried, w)

    def prep2_f(*vals):
        r_, lw_, k2_, v_, a_, b_, g_ = prep_f(*vals)
        return r_, lw_, k2_, v_, a_, b_, g_, r_, k2_, v_

    prep_comm = plan.comm("rwkv_prep_bwd", g)
    prep_out = _rowwise_bwd(prep2_f, prep_xs, prep_ps, [dr2, dlw, dk2, dv2, da_vec, db_vec, dgate, dr1, dk1, dv1],
                            x_grad=[True] * 8, p_grad=[True] * 11 + [False], dx_groups=[[0, 1, 2, 3], [4, 5, 6, 7]],
                            dx_dtypes=[F32], tm=256, name="rwkv_prep_bwd", fold_next=(0, 1), comm=prep_comm)
    prep_out, carried = prep_out if prep_comm is not None else (prep_out, [])
    plan.done("rwkv_prep_bwd", carried, w)
    dp_r = prep_out[0]
    (dmu_r, dmu_k, dmu_v, dmu_lo, g["rwkv_w0"], g["w2_pad"], g["rwkv_a0"], g["a2_pad"], g["g2_pad"],
     g["rwkv_k_k"], g["rwkv_k_a"]) = prep_out[1:]
    g["mu_pad"] = jnp.concatenate([dmu_r, dmu_k, dmu_v, dmu_lo], axis=1)
    dhm = _mm(dp_h, w["w_in_h"], tb=True, tm=1024, tn=D_MODEL, tk=N_HGRN_COLS, name="inproj_dh_h")
    dhm = _mm(dp_r, w["w_in_r"], tb=True, tm=1024, tn=D_MODEL, tk=N_RWKV_PAD, name="inproj_dh_r", res=dhm)
    g["w_in_h"] = _mm(hm, dp_h, ta=True, tm=D_MODEL, tn=D_MODEL, tk=1024, name="inproj_dw_h")
    g["w_in_r"] = _mm(hm, dp_r, ta=True, tm=D_MODEL, tn=N_RWKV_PAD // 2, tk=1024, name="inproj_dw_r")
    mix_comm = plan.comm("mix_drms", g)
    mix_out = _rowwise_bwd(_rms_f, [x1], [w["mix_norm"]], [dhm], x_grad=[True], p_grad=[True], dx_groups=[[0]],
                           dx_dtypes=[F32], tm=512, name="mix_drms", extra={0: dx2}, comm=mix_comm)
    (dx1, g["mix_norm"]), carried = mix_out if mix_comm is not None else (mix_out, [])
    plan.done("mix_drms", carried, w)
    dx0 = _ffn_bwd(dx1, x, w, ffn1_saved, "ffn1", plan, g)
    return loss, dx0, g


HBM_SPEC = pl.BlockSpec(memory_space=pl.ANY)

Comm = collections.namedtuple("Comm", "arrays out_shapes aliased sem_shapes start finish")


def _join_comms(first, second):
    assert first.aliased == second.aliased
    n, s = len(first.arrays), len(first.sem_shapes)

    def start(ins, outs, sems):
        first.start(ins[:n], outs[:n], sems[:s])
        second.start(ins[n:], outs[n:], sems[s:])

    def finish(ins, outs, sems):
        first.finish(ins[:n], outs[:n], sems[:s])
        second.finish(ins[n:], outs[n:], sems[s:])

    return Comm(list(first.arrays) + list(second.arrays), list(first.out_shapes) + list(second.out_shapes),
                first.aliased, list(first.sem_shapes) + list(second.sem_shapes), start, finish)


def _run_comm(comm, name):
    n = len(comm.arrays)

    def body(*refs):
        ins, outs, sems = refs[:n], refs[n:2 * n], refs[2 * n:]
        comm.start(ins, outs, sems)
        comm.finish(ins, outs, sems)

    return pl.pallas_call(
        body, name=name, in_specs=[HBM_SPEC] * n, out_specs=[HBM_SPEC] * n, out_shape=list(comm.out_shapes),
        input_output_aliases={t: t for t in range(n)} if comm.aliased else {},
        scratch_shapes=list(comm.sem_shapes))(*comm.arrays)


def _hosting_call(body, comm, *, name, grid, in_specs, out_specs, out_shape, scratch_shapes, args):
    sem = ("arbitrary",) * len(grid)
    if comm is None:
        res = pl.pallas_call(body, name=name, grid=grid, in_specs=in_specs, out_specs=out_specs, out_shape=out_shape,
                             scratch_shapes=scratch_shapes, compiler_params=_params(sem))(*args)
        return list(res), []
    ni, no, ns, nc = len(in_specs), len(out_specs), len(scratch_shapes), len(comm.arrays)

    def wrapped(*refs):
        ins, cins = refs[:ni], refs[ni:ni + nc]
        outs, couts = refs[ni + nc:ni + nc + no], refs[ni + nc + no:ni + 2 * nc + no]
        scr, sems = refs[ni + 2 * nc + no:ni + 2 * nc + no + ns], refs[ni + 2 * nc + no + ns:]
        first = functools.reduce(jnp.logical_and, [pl.program_id(k) == 0 for k in range(len(grid))])
        last = functools.reduce(jnp.logical_and, [pl.program_id(k) == grid[k] - 1 for k in range(len(grid))])

        @pl.when(first)
        def _():
            comm.start(cins, couts, sems)

        body(*ins, *outs, *scr)

        @pl.when(last)
        def _():
            comm.finish(cins, couts, sems)

    res = pl.pallas_call(
        wrapped, name=name, grid=grid, in_specs=list(in_specs) + [HBM_SPEC] * nc,
        out_specs=list(out_specs) + [HBM_SPEC] * nc, out_shape=list(out_shape) + list(comm.out_shapes),
        scratch_shapes=list(scratch_shapes) + list(comm.sem_shapes),
        input_output_aliases={ni + t: no + t for t in range(nc)} if comm.aliased else {},
        compiler_params=_params(sem))(*args, *comm.arrays)
    return list(res[:no]), list(res[no:])


def _chips(x, y):
    return [(1 - x, y), (x, 1 - y), (1 - x, 1 - y)]


def _gather_comm(bufs):
    n = len(bufs)

    def copies(outs, sems):
        ici_send, ici_recv, d2d_send, d2d_recv = sems
        x, y, c = lax.axis_index("x"), lax.axis_index("y"), lax.axis_index("c")

        def half(t, slot, hc):
            hr = bufs[t].shape[1] // 2
            return outs[t].at[slot, pl.ds(pl.multiple_of(hc * hr, 16), hr), :]

        def ici(t, j, slot, px, py):
            return pltpu.make_async_remote_copy(src_ref=half(t, slot, c), dst_ref=half(t, slot, c),
                                                send_sem=ici_send.at[3 * t + j], recv_sem=ici_recv.at[3 * t + j],
                                                device_id=(px, py, c), device_id_type=MESH)

        def d2d(t, j, slot, hc):
            return pltpu.make_async_remote_copy(src_ref=half(t, slot, hc), dst_ref=half(t, slot, hc),
                                                send_sem=d2d_send.at[3 * t + j], recv_sem=d2d_recv.at[3 * t + j],
                                                device_id=(x, y, 1 - c), device_id_type=MESH)

        peers = [(t, j, px, py) for t in range(n) for j, (px, py) in enumerate(_chips(x, y))]
        return ici, d2d, peers, 2 * x + y, c

    def start(ins, outs, sems):
        ici, _, peers, me, _ = copies(outs, sems)
        for t, j, px, py in peers:
            ici(t, j, me, px, py).start()

    def finish(ins, outs, sems):
        ici, d2d, peers, me, c = copies(outs, sems)
        for t, j, px, py in peers:
            ici(t, j, 2 * px + py, px, py).wait_recv()
            d2d(t, j, 2 * px + py, c).start()
        for t, j, px, py in peers:
            d2d(t, j, 2 * px + py, 1 - c).wait_recv()
        for t, j, px, py in peers:
            ici(t, j, me, px, py).wait_send()
            d2d(t, j, 2 * px + py, c).wait_send()

    return Comm(list(bufs), [SDS(b.shape, b.dtype) for b in bufs], True, [pltpu.SemaphoreType.DMA((3 * n,))] * 4,
                start, finish)


def _sibling_exchange_comm(gs):
    n = len(gs)

    def copies(ins, outs, sems):
        x, y, c = lax.axis_index("x"), lax.axis_index("y"), lax.axis_index("c")
        cps = []
        for t in range(n):
            hr = gs[t].shape[1] // 2
            src = ins[t].at[:, pl.ds(pl.multiple_of((1 - c) * hr, SUBLANES), hr), :]
            cps.append(pltpu.make_async_remote_copy(src_ref=src, dst_ref=outs[t], send_sem=sems[0].at[t],
                                                    recv_sem=sems[1].at[t], device_id=(x, y, 1 - c),
                                                    device_id_type=MESH))
        return cps

    def start(ins, outs, sems):
        for cp in copies(ins, outs, sems):
            cp.start()

    def finish(ins, outs, sems):
        for cp in copies(ins, outs, sems):
            cp.wait()

    return Comm(list(gs), [SDS((N_CHIPS, g.shape[1] // 2, g.shape[2]), g.dtype) for g in gs], False,
                [pltpu.SemaphoreType.DMA((n,))] * 2, start, finish)


def _chip_exchange_comm(ss):
    n = len(ss)

    def copies(ins, outs, sems):
        x, y, c = lax.axis_index("x"), lax.axis_index("y"), lax.axis_index("c")
        me = 2 * x + y

        def copy(t, j, px, py, src_slot, dst_slot):
            return pltpu.make_async_remote_copy(src_ref=ins[t].at[src_slot], dst_ref=outs[t].at[dst_slot],
                                                send_sem=sems[0].at[3 * t + j], recv_sem=sems[1].at[3 * t + j],
                                                device_id=(px, py, c), device_id_type=MESH)

        peers = [(t, j, px, py) for t in range(n) for j, (px, py) in enumerate(_chips(x, y))]
        return copy, peers, me

    def start(ins, outs, sems):
        copy, peers, me = copies(ins, outs, sems)
        for t, j, px, py in peers:
            copy(t, j, px, py, 2 * px + py, me).start()

    def finish(ins, outs, sems):
        copy, peers, me = copies(ins, outs, sems)
        for t, j, px, py in peers:
            copy(t, j, px, py, me, 2 * px + py).wait_recv()
        for t, j, px, py in peers:
            copy(t, j, px, py, 2 * px + py, me).wait_send()

    return Comm(list(ss), [SDS(s.shape, s.dtype) for s in ss], False, [pltpu.SemaphoreType.DMA((3 * n,))] * 2,
                start, finish)


def _sibling_swap_comm(fs):
    n = len(fs)

    def copies(ins, outs, sems):
        x, y, c = lax.axis_index("x"), lax.axis_index("y"), lax.axis_index("c")
        return [pltpu.make_async_remote_copy(src_ref=ins[t], dst_ref=outs[t], send_sem=sems[0].at[t],
                                             recv_sem=sems[1].at[t], device_id=(x, y, 1 - c), device_id_type=MESH)
                for t in range(n)]

    def start(ins, outs, sems):
        for cp in copies(ins, outs, sems):
            cp.start()

    def finish(ins, outs, sems):
        for cp in copies(ins, outs, sems):
            cp.wait()

    return Comm(list(fs), [SDS(f.shape, f.dtype) for f in fs], False, [pltpu.SemaphoreType.DMA((n,))] * 2,
                start, finish)


def _row_tile(rows, cap=512):
    best = SUBLANES
    for tr in range(SUBLANES, min(rows, cap) + 1, SUBLANES):
        if rows % tr == 0:
            best = tr
    return best


def _add_halves(g4, r4, c_idx, name):
    _, hr, lanes = r4.shape
    tr = _row_tile(hr)
    nb = hr // tr

    def body(c_ref, a_ref, b_ref, o_ref):
        o_ref[...] = (a_ref[...] + b_ref[...]).astype(o_ref.dtype)

    grid_spec = pltpu.PrefetchScalarGridSpec(
        num_scalar_prefetch=1, grid=(N_CHIPS, nb),
        in_specs=[pl.BlockSpec((None, tr, lanes), lambda q, i, c_ref: (q, c_ref[0] * nb + i, 0)),
                  pl.BlockSpec((None, tr, lanes), lambda q, i, c_ref: (q, i, 0))],
        out_specs=pl.BlockSpec((None, tr, lanes), lambda q, i, c_ref: (q, i, 0)))
    return pl.pallas_call(body, name=name, grid_spec=grid_spec, out_shape=SDS(r4.shape, BF16),
                          compiler_params=_params(("parallel", "parallel")))(c_idx, g4, r4)


def _sum_chips(r4, s4, me_idx, name):
    _, rows, lanes = r4.shape
    tr = _row_tile(rows)

    def body(me_ref, a_ref, b_ref, c_ref, d_ref, own_ref, o_ref):
        own = own_ref[...].astype(F32)
        p = [jnp.where(me_ref[0] == q, own, ref[...].astype(F32)) for q, ref in enumerate((a_ref, b_ref, c_ref, d_ref))]
        o_ref[...] = ((p[0] + p[1]) + p[2]) + p[3]

    other = lambda q: (lambda i, me_ref: (jnp.where(me_ref[0] == q, (q + 1) % N_CHIPS, q), i, 0))
    grid_spec = pltpu.PrefetchScalarGridSpec(
        num_scalar_prefetch=1, grid=(rows // tr,),
        in_specs=[pl.BlockSpec((None, tr, lanes), other(q)) for q in range(N_CHIPS)]
        + [pl.BlockSpec((None, tr, lanes), lambda i, me_ref: (me_ref[0], i, 0))],
        out_specs=pl.BlockSpec((tr, lanes), lambda i, me_ref: (i, 0)))
    return pl.pallas_call(body, name=name, grid_spec=grid_spec, out_shape=SDS((rows, lanes), F32),
                          compiler_params=_params(("parallel",)))(me_idx, r4, r4, r4, r4, s4)


def _adamw(wf, g_own, g_other, mf, vf, c_idx, name):
    rows, lanes = wf.shape
    hr = rows // 2
    tr = _row_tile(hr)
    nb = hr // tr
    c1 = 1.0 / (1.0 - ADAM_B1 ** ADAM_STEP)
    c2 = 1.0 / (1.0 - ADAM_B2 ** ADAM_STEP)

    def body(c_ref, w_ref, go_ref, gx_ref, m_ref, v_ref, g_ref, d_ref, nm_ref, nv_ref):
        gv = jnp.where(pl.program_id(0) == c_ref[0], go_ref[...], gx_ref[...])
        m = ADAM_B1 * m_ref[...] + (1.0 - ADAM_B1) * gv
        v = ADAM_B2 * v_ref[...] + (1.0 - ADAM_B2) * (gv * gv)
        g_ref[...] = gv
        d_ref[...] = -ADAM_LR * ((m * c1) / (jnp.sqrt(v * c2) + ADAM_EPS) + ADAM_WD * w_ref[...])
        nm_ref[...] = m
        nv_ref[...] = v

    full = pl.BlockSpec((tr, lanes), lambda h, i, c_ref: (h * nb + i, 0))
    half = pl.BlockSpec((tr, lanes), lambda h, i, c_ref: (i, 0))
    grid_spec = pltpu.PrefetchScalarGridSpec(num_scalar_prefetch=1, grid=(2, nb),
                                             in_specs=[full, half, half, full, full], out_specs=[full] * 4)
    return pl.pallas_call(body, name=name, grid_spec=grid_spec, out_shape=[SDS((rows, lanes), F32)] * 4,
                          compiler_params=_params(("parallel", "parallel")))(c_idx, wf, g_own, g_other, mf, vf)


BIG = ("ffn1_w_gate", "ffn1_w_up", "ffn1_w_down", "ffn2_w_gate", "ffn2_w_up", "ffn2_w_down", "w_out", "w_in")
TRANSPOSED = ("ffn1_w_gate", "ffn1_w_up", "ffn2_w_gate", "ffn2_w_up")
PACKED = ("rwkv_w2", "rwkv_a2", "rwkv_g2")
SMALL_SHAPES = {"ffn1_norm": (1, D_MODEL), "mix_norm": (1, D_MODEL), "hgrn_lb_logits": (2, W_A),
                "hgrn_out_norm": (1, W_A), "rwkv_shift_mu": (1, N_RWKV_COLS), "rwkv_w0": (1, W_B),
                "rwkv_a0": (1, W_B), "rwkv_k_k": (1, W_B), "rwkv_k_a": (1, W_B),
                "rwkv_r_k": (1, HB_HEADS, HB_DIM), "rwkv_gn_w": (1, W_B), "rwkv_gn_b": (1, W_B),
                "ffn2_norm": (1, D_MODEL), "final_norm": (D_MODEL,)}
PACK_ELEMS = sum(_numel(_shard_shape(n)) for n in PACKED) + sum(_numel(SMALL_SHAPES[n]) for n in SMALL)
PACK_ROWS = -(-PACK_ELEMS // (32 * LANES)) * 32


def _to_rows(name, shard):
    return shard[0].T if name in TRANSPOSED else shard[0]


def _from_rows(name, rows):
    return (rows.T if name in TRANSPOSED else rows)[None]


def _pack(sharded, small):
    flat = jnp.concatenate([sharded[n].reshape(-1) for n in PACKED] + [small[n].reshape(-1) for n in SMALL])
    return jnp.pad(flat, (0, PACK_ROWS * LANES - flat.shape[0])).reshape(PACK_ROWS, LANES)


def _unpack(packed):
    flat, out, off = packed.reshape(-1), {}, 0
    for n in PACKED:
        shp = _shard_shape(n)
        out[n] = flat[off:off + _numel(shp)].reshape((1,) + shp)
        off += _numel(shp)
    for n in SMALL:
        shp = SMALL_SHAPES[n]
        out[n] = flat[off:off + _numel(shp)].reshape(shp)
        off += _numel(shp)
    return out


def _quarter(full, name, q):
    shape, ax = SHARDED_SHAPES[name]
    w = shape[ax] // N_CHIPS
    return lax.slice_in_dim(full, q * w, (q + 1) * w, axis=ax)


def kernel(x, ffn1_norm, ffn1_w_gate, ffn1_w_up, ffn1_w_down, mix_norm, w_in, hgrn_lb_logits, hgrn_out_norm, rwkv_shift_mu, rwkv_w0, rwkv_w2, rwkv_a0, rwkv_a2, rwkv_g2, rwkv_k_k, rwkv_k_a, rwkv_r_k, rwkv_gn_w, rwkv_gn_b, w_out, ffn2_norm, ffn2_w_gate, ffn2_w_up, ffn2_w_down, final_norm, loss_target, m_ffn1_norm, m_ffn1_w_gate, m_ffn1_w_up, m_ffn1_w_down, m_mix_norm, m_w_in, m_hgrn_lb_logits, m_hgrn_out_norm, m_rwkv_shift_mu, m_rwkv_w0, m_rwkv_w2, m_rwkv_a0, m_rwkv_a2, m_rwkv_g2, m_rwkv_k_k, m_rwkv_k_a, m_rwkv_r_k, m_rwkv_gn_w, m_rwkv_gn_b, m_w_out, m_ffn2_norm, m_ffn2_w_gate, m_ffn2_w_up, m_ffn2_w_down, m_final_norm, v_ffn1_norm, v_ffn1_w_gate, v_ffn1_w_up, v_ffn1_w_down, v_mix_norm, v_w_in, v_hgrn_lb_logits, v_hgrn_out_norm, v_rwkv_shift_mu, v_rwkv_w0, v_rwkv_w2, v_rwkv_a0, v_rwkv_a2, v_rwkv_g2, v_rwkv_k_k, v_rwkv_k_a, v_rwkv_r_k, v_rwkv_gn_w, v_rwkv_gn_b, v_w_out, v_ffn2_norm, v_ffn2_w_gate, v_ffn2_w_up, v_ffn2_w_down, v_final_norm):
    args = dict(locals())
    wts = {n: args[n] for n in ALL_WEIGHTS}
    moms = {n: args["m_" + n] for n in ALL_WEIGHTS}
    vars_ = {n: args["v_" + n] for n in ALL_WEIGHTS}

    me = 2 * lax.axis_index("x") + lax.axis_index("y")
    c_idx = lax.axis_index("c").astype(jnp.int32).reshape(1)
    me_idx = me.astype(jnp.int32).reshape(1)
    shard_of = {n: _to_rows(n, wts[n]).astype(BF16) for n in BIG}
    shard_of["packed"] = _pack(wts, {n: wts[n] for n in SMALL}).astype(BF16)
    group = {"ffn1": BIG[0:3], "ffn2": BIG[3:6]}

    def slot_bufs(names):
        return [lax.dynamic_update_slice(lax.empty((N_CHIPS,) + shard_of[n].shape, BF16), shard_of[n][None],
                                         (me, 0, 0)) for n in names]

    def ffn_weights(tag, gathered):
        return {f"{tag}_wgt": gathered[0].reshape(D_FF, D_MODEL), f"{tag}_wut": gathered[1].reshape(D_FF, D_MODEL),
                f"{tag}_wd": gathered[2].reshape(D_FF, D_MODEL)}

    def w_in_weights(gathered):
        w_in_full = jnp.concatenate([gathered[0][q] for q in range(N_CHIPS)], axis=1)
        return {"w_in_h": w_in_full[:, :N_HGRN_COLS],
                "w_in_r": jnp.pad(w_in_full[:, N_HGRN_COLS:], ((0, 0), (0, N_RWKV_PAD - N_RWKV_COLS)))}

    def mixer_weights(gathered):
        w_out_full = gathered[0].reshape(D_MODEL, D_MODEL)
        packs = gathered[1].reshape(N_CHIPS, PACK_ROWS * LANES)
        full, off = {}, 0
        for n in PACKED:
            shp = _shard_shape(n)
            full[n] = jnp.concatenate([packs[q, off:off + _numel(shp)].reshape(shp) for q in range(N_CHIPS)], axis=1)
            off += _numel(shp)
        zrow = lambda nrow: jnp.zeros((nrow, W_B), BF16)
        return {"w_out": w_out_full,
                "w2_pad": jnp.concatenate([full["rwkv_w2"], zrow(LORA_PAD - 32)], axis=0),
                "a2_pad": jnp.concatenate([zrow(32), full["rwkv_a2"], zrow(LORA_PAD - 64)], axis=0),
                "g2_pad": jnp.concatenate([zrow(64), full["rwkv_g2"], zrow(LORA_PAD - 160)], axis=0)}

    plan = _Plan()
    w = {}
    plan.carry("ffn1_rms", lambda g: _gather_comm(slot_bufs(group["ffn1"][:2])),
               lambda res, w_: w_.update({"ffn1_wgt": res[0].reshape(D_FF, D_MODEL),
                                          "ffn1_wut": res[1].reshape(D_FF, D_MODEL)}))

    def after_gate_up(res, w_):
        w_["ffn1_wd"] = res[0].reshape(D_FF, D_MODEL)
        w_.update(w_in_weights(res[1:]))

    plan.carry("ffn1_gate_up", lambda g: _gather_comm(slot_bufs(("ffn1_w_down", "w_in"))), after_gate_up)
    plan.carry("ffn1_down", lambda g: _gather_comm(slot_bufs(("w_out", "packed"))),
               lambda res, w_: w_.update(mixer_weights(res)))
    plan.carry("rwkv_fwd", lambda g: _gather_comm(slot_bufs(group["ffn2"])),
               lambda res, w_: w_.update(ffn_weights("ffn2", res)))
    w["ffn1_norm"], w["ffn2_norm"] = ffn1_norm, ffn2_norm
    w["mix_norm"] = mix_norm
    w["lb0"], w["lb1"] = hgrn_lb_logits[0:1], hgrn_lb_logits[1:2]
    w["hgrn_out_norm"] = hgrn_out_norm
    w["mu_pad"] = jnp.pad(rwkv_shift_mu, ((0, 0), (0, N_RWKV_PAD - N_RWKV_COLS)))
    for n in ("rwkv_w0", "rwkv_a0", "rwkv_k_k", "rwkv_k_a", "rwkv_gn_w", "rwkv_gn_b"):
        w[n] = wts[n]
    w["rwkv_r_k"] = rwkv_r_k.reshape(1, W_B)
    w["final_norm"] = final_norm.reshape(1, D_MODEL)

    def reduce_rows(names, gs):
        r1 = _run_comm(_sibling_exchange_comm(gs), "grad_sibling_exchange")
        s4 = [_add_halves(gt, rt, c_idx, f"grad_add_halves_{n}") for gt, rt, n in zip(gs, r1, names)]
        r2 = _run_comm(_chip_exchange_comm(s4), "grad_chip_exchange")
        return [_sum_chips(rt, st, me_idx, f"grad_sum_chips_{n}") for rt, st, n in zip(r2, s4, names)]

    early, swapped = {}, {}

    def reduce_early(names, grads_of, sibling_host, chips_host, swap_host):
        def sibling_comm(g):
            early[names, "gs"] = grads_of(g)
            return _sibling_exchange_comm(early[names, "gs"])

        def after_sibling(res, w_):
            early[names, "s4"] = [_add_halves(gt, rt, c_idx, f"grad_add_halves_{n}")
                                  for gt, rt, n in zip(early[names, "gs"], res, names)]

        def after_chips(res, w_):
            early.update(zip(names, [_sum_chips(rt, st, me_idx, f"grad_sum_chips_{n}")
                                     for rt, st, n in zip(res, early[names, "s4"], names)]))

        plan.carry(sibling_host, sibling_comm, after_sibling)
        plan.carry(chips_host, lambda g: _chip_exchange_comm(early[names, "s4"]), after_chips)
        if swap_host:
            plan.carry(swap_host, lambda g: _sibling_swap_comm([early[n] for n in names]),
                       lambda res, w_: swapped.update(zip(names, res)))

    def proj_grads(g):
        g_w_in = jnp.concatenate([g["w_in_h"], g["w_in_r"][:, :N_RWKV_COLS]], axis=1)
        return [g["w_out"].reshape(N_CHIPS, -1, D_MODEL),
                jnp.stack([_quarter(g_w_in, "w_in", q) for q in range(N_CHIPS)])]

    rows_of = lambda keys: (lambda g: [g[k].reshape(N_CHIPS, -1, D_MODEL) for k in keys])
    reduce_early(group["ffn2"], rows_of(("ffn2_wgt", "ffn2_wut", "ffn2_wd")), "hgrn_bwd", "rwkv_bwd", "rwkv_prep_bwd")
    reduce_early(("w_out", "w_in"), proj_grads, "mix_drms", "ffn1_dact", "ffn1_dwg")
    reduce_early(("ffn1_w_down",), rows_of(("ffn1_wd",)), "ffn1_dwg", "ffn1_dwu", "ffn1_dh_g")
    reduce_early(("ffn1_w_gate",), rows_of(("ffn1_wgt",)), "ffn1_dwu", "ffn1_dh_g", "ffn1_dh_u")
    reduce_early(("ffn1_w_up",), rows_of(("ffn1_wut",)), "ffn1_dh_g", "ffn1_dh_u", None)
    loss_slab, grad_x, g = _local_step(x[0], loss_target[0], w, plan)
    loss = lax.psum(loss_slab[0, 0], ("x", "y", "c"))

    gfull = {
        "rwkv_w2": g["w2_pad"][0:32], "rwkv_a2": g["a2_pad"][32:64], "rwkv_g2": g["g2_pad"][64:160],
    }
    gsmall = {
        "ffn1_norm": g["ffn1_norm"], "mix_norm": g["mix_norm"],
        "hgrn_lb_logits": jnp.concatenate([g["lb0"], g["lb1"]], axis=0), "hgrn_out_norm": g["hgrn_out_norm"],
        "rwkv_shift_mu": g["mu_pad"][:, :N_RWKV_COLS], "rwkv_w0": g["rwkv_w0"], "rwkv_a0": g["rwkv_a0"],
        "rwkv_k_k": g["rwkv_k_k"], "rwkv_k_a": g["rwkv_k_a"], "rwkv_r_k": g["rwkv_r_k"],
        "rwkv_gn_w": g["rwkv_gn_w"], "rwkv_gn_b": g["rwkv_gn_b"], "ffn2_norm": g["ffn2_norm"],
        "final_norm": g["final_norm"],
    }
    packed = jnp.stack([_pack({n: _quarter(gfull[n], n, q) for n in PACKED}, gsmall) for q in range(N_CHIPS)])
    early["packed"], = reduce_rows(["packed"], [packed])
    last = ["ffn1_w_up", "packed"]
    swapped.update(zip(last, _run_comm(_sibling_swap_comm([early[n] for n in last]), "grad_sibling_swap")))
    names = list(BIG) + ["packed"]
    own, other = [early[n] for n in names], [swapped[n] for n in names]

    def rows_list(d):
        return [_to_rows(n, d[n]) for n in BIG] + [_pack(d, {n: d[n] for n in SMALL})]

    outs = [_adamw(wt, go, gx, mt, vt, c_idx, f"adamw_{n}")
            for wt, go, gx, mt, vt, n in zip(rows_list(wts), own, other, rows_list(moms), rows_list(vars_), names)]
    results = []
    for k in range(4):
        per = [outs[i][k] for i in range(len(names))]
        d = {n: _from_rows(n, z) for n, z in zip(BIG, per[:-1])}
        d.update(_unpack(per[-1]))
        results.append(d)
    return (loss, grad_x[None], *[r[n] for r in results for n in ALL_WEIGHTS])
```

```python
import collections
import functools

import jax
import jax.numpy as jnp
from jax import lax
from jax.experimental import pallas as pl
from jax.experimental.pallas import tpu as pltpu

F32 = jnp.float32
BF16 = jnp.bfloat16
SDS = jax.ShapeDtypeStruct
MESH = pl.DeviceIdType.MESH

D_MODEL = 1024
D_FF = 2816
W_A = 512
W_B = 512
HA_HEADS, HA_DIM = 4, 128
HB_HEADS, HB_DIM = 8, 64
HGRN_CHUNK = 64
HGRN_GROUP = 8
RWKV_CHUNK = 16
RWKV_GROUP = 8
N_HGRN_COLS = 4 * W_A
N_RWKV_COLS = 3 * W_B + 32 + 32 + 96
N_RWKV_PAD = 1792
LORA_PAD = 256
NORM_EPS = 1e-6
RWKV_GN_EPS = 64e-5
L2_EPS = 1e-12
ADAM_LR, ADAM_B1, ADAM_B2, ADAM_EPS, ADAM_WD, ADAM_STEP = 0.001, 0.9, 0.999, 1e-8, 0.01, 10

N_CHIPS = 4
VMEM_LIMIT_V7X = 56 * 1024 * 1024
LANES = 1024

SHARDED_SHAPES = {
    "ffn1_w_gate": ((D_MODEL, D_FF), 1), "ffn1_w_up": ((D_MODEL, D_FF), 1), "ffn1_w_down": ((D_FF, D_MODEL), 0),
    "w_in": ((D_MODEL, N_HGRN_COLS + N_RWKV_COLS), 1), "rwkv_w2": ((32, W_B), 1), "rwkv_a2": ((32, W_B), 1),
    "rwkv_g2": ((96, W_B), 1), "w_out": ((D_MODEL, D_MODEL), 0),
    "ffn2_w_gate": ((D_MODEL, D_FF), 1), "ffn2_w_up": ((D_MODEL, D_FF), 1), "ffn2_w_down": ((D_FF, D_MODEL), 0),
}
SMALL = ("ffn1_norm", "mix_norm", "hgrn_lb_logits", "hgrn_out_norm", "rwkv_shift_mu", "rwkv_w0", "rwkv_a0",
         "rwkv_k_k", "rwkv_k_a", "rwkv_r_k", "rwkv_gn_w", "rwkv_gn_b", "ffn2_norm", "final_norm")
ALL_WEIGHTS = ("ffn1_norm", "ffn1_w_gate", "ffn1_w_up", "ffn1_w_down", "mix_norm", "w_in", "hgrn_lb_logits",
               "hgrn_out_norm", "rwkv_shift_mu", "rwkv_w0", "rwkv_w2", "rwkv_a0", "rwkv_a2", "rwkv_g2", "rwkv_k_k",
               "rwkv_k_a", "rwkv_r_k", "rwkv_gn_w", "rwkv_gn_b", "w_out", "ffn2_norm", "ffn2_w_gate", "ffn2_w_up",
               "ffn2_w_down", "final_norm")


def _shard_shape(name):
    shape, ax = SHARDED_SHAPES[name]
    return tuple(s // N_CHIPS if i == ax else s for i, s in enumerate(shape))


def _numel(shape):
    n = 1
    for s in shape:
        n *= s
    return n


def _params(sem=None):
    return pltpu.CompilerParams(dimension_semantics=sem, vmem_limit_bytes=VMEM_LIMIT_V7X)


def _split2(x):
    hi = x.astype(BF16)
    return hi, (x.astype(F32) - hi.astype(F32)).astype(BF16)


def _dg(x, y, cx, cy, hi):
    dn = (((cx,), (cy,)), ((), ()))
    dot = lambda p, q: lax.dot_general(p, q, dn, preferred_element_type=F32)
    if hi == "x3":
        (xh, xl), (yh, yl) = _split2(x), _split2(y)
        return dot(xh, yh) + (dot(xh, yl) + dot(xl, yh))
    return dot(x.astype(BF16), y.astype(BF16))


def _make_mm(hi, cotangent_forms=None):
    @jax.custom_vjp
    def nn(x, y):
        return _dg(x, y, 1, 0, hi)

    @jax.custom_vjp
    def nt(x, y):
        return _dg(x, y, 1, 1, hi)

    @jax.custom_vjp
    def tn(x, y):
        return _dg(x, y, 0, 0, hi)

    bnn, bnt, btn = cotangent_forms or (nn, nt, tn)
    nn.defvjp(lambda x, y: (nn(x, y), (x, y)), lambda r, g: (bnt(g, r[1]), btn(r[0], g)))
    nt.defvjp(lambda x, y: (nt(x, y), (x, y)), lambda r, g: (bnn(g, r[1]), btn(g, r[0])))
    tn.defvjp(lambda x, y: (tn(x, y), (x, y)), lambda r, g: (bnt(r[1], g), bnn(r[0], g)))
    return nn, nt, tn


_nn, _nt, _tn = _make_mm(False)
_nn_x3, _nt_x3, _tn_x3 = _make_mm("x3", (_nn, _nt, _tn))


def _tri_apply(x, transpose):
    c = x.shape[0]
    tri = (lax.broadcasted_iota(jnp.int32, (c, c), 1) <= lax.broadcasted_iota(jnp.int32, (c, c), 0)).astype(BF16)
    dn = (((0 if transpose else 1,), (0,)), ((), ()))
    p1, p2 = _split2(x)
    dot = lambda p: lax.dot_general(tri, p, dn, preferred_element_type=F32)
    return dot(p1) + dot(p2)


@jax.custom_vjp
def _cumsum_rows(x):
    return _tri_apply(x, False)


_cumsum_rows.defvjp(lambda x: (_tri_apply(x, False), None), lambda _, g: (_tri_apply(g, True),))


def _sigmoid(x):
    return 1.0 / (1.0 + jnp.exp(-x))


def _silu(x):
    return x * _sigmoid(x)


def _softplus(z):
    return jnp.maximum(z, 0.0) + jnp.log(1.0 + jnp.exp(-jnp.abs(z)))


def _mm(a, b, *, ta=False, tb=False, tm, tn, tk, name, out_dtype=F32, res=None, scale=None, comm=None):
    m = a.shape[1] if ta else a.shape[0]
    kdim = a.shape[0] if ta else a.shape[1]
    n = b.shape[0] if tb else b.shape[1]
    assert (b.shape[1] if tb else b.shape[0]) == kdim
    tm, tn, tk = min(tm, m), min(tn, n), min(tk, kdim)
    assert m % tm == 0 and n % tn == 0 and kdim % tk == 0, (name, m, n, kdim)
    nk = kdim // tk
    a_spec = pl.BlockSpec((tk, tm), lambda i, j, k: (k, i)) if ta else pl.BlockSpec((tm, tk), lambda i, j, k: (i, k))
    b_spec = pl.BlockSpec((tn, tk), lambda i, j, k: (j, k)) if tb else pl.BlockSpec((tk, tn), lambda i, j, k: (k, j))
    o_spec = pl.BlockSpec((tm, tn), lambda i, j, k: (i, j))
    ca, cb = (0 if ta else 1), (1 if tb else 0)

    def body(*refs):
        if res is not None:
            a_ref, b_ref, r_ref, o_ref, acc_ref = refs
        else:
            a_ref, b_ref, o_ref, acc_ref = refs
        k = pl.program_id(2)

        @pl.when(k == 0)
        def _():
            acc_ref[...] = jnp.zeros_like(acc_ref)

        acc_ref[...] += _dg(a_ref[...], b_ref[...], ca, cb, False)

        @pl.when(k == nk - 1)
        def _():
            acc = acc_ref[...]
            if scale is not None:
                acc = acc * scale
            if res is not None:
                acc = r_ref[...] + acc
            o_ref[...] = acc.astype(out_dtype)

    in_specs = [a_spec, b_spec] + ([o_spec] if res is not None else [])
    args = (a, b) + ((res,) if res is not None else ())
    if comm is None:
        return pl.pallas_call(
            body, name=name, grid=(m // tm, n // tn, nk), in_specs=in_specs, out_specs=o_spec,
            out_shape=SDS((m, n), out_dtype), scratch_shapes=[pltpu.VMEM((tm, tn), F32)],
            compiler_params=_params(("parallel", "parallel", "arbitrary")))(*args)
    (out,), carried = _hosting_call(
        body, comm, name=name, grid=(m // tm, n // tn, nk), in_specs=in_specs, out_specs=[o_spec],
        out_shape=[SDS((m, n), out_dtype)], scratch_shapes=[pltpu.VMEM((tm, tn), F32)], args=args)
    return out, carried


def _row_spec(x, tm, tile_of=lambda i: i):
    if isinstance(x, tuple):
        arr, w, j = x
        return arr, pl.BlockSpec((tm, w), lambda i, j=j: (tile_of(i), j))
    return x, pl.BlockSpec((tm, x.shape[1]), lambda i: (tile_of(i), 0))


def _par_spec(p):
    if isinstance(p, tuple):
        arr, w, j = p
        return arr, pl.BlockSpec((arr.shape[0], w), lambda i, j=j: (0, j))
    return p, pl.BlockSpec(p.shape, lambda i: (0, 0))


def _store_groups(refs, groups, vals):
    for ref, idxs in zip(refs, groups):
        off = 0
        for ix in idxs:
            v = vals[ix]
            ref[:, off:off + v.shape[1]] = v.astype(ref.dtype)
            off += v.shape[1]


SUBLANES = 8


def _x_plan(xs, tm, t, tile_of=lambda i: i):
    arrays, specs, plan = [], [], []
    nb = tm // SUBLANES
    for x in xs:
        if isinstance(x, tuple) and isinstance(x[0], str):
            kind, arr, w, j = x
            if kind == "prev":
                halo = lambda i, j=j: (jnp.maximum(tile_of(i) * nb - 1, 0), j)
            else:
                halo = lambda i, j=j: (jnp.minimum((tile_of(i) + 1) * nb, t // SUBLANES - 1), j)
            arrays += [arr, arr]
            specs += [pl.BlockSpec((tm, w), lambda i, j=j: (tile_of(i), j)), pl.BlockSpec((SUBLANES, w), halo)]
            plan.append((kind, 2, w))
        else:
            arr, spec = _row_spec(x, tm, tile_of)
            arrays.append(arr)
            specs.append(spec)
            plan.append(("plain", 1, spec.block_shape[1]))
    return arrays, specs, plan


def _x_vals(refs, plan, tm, nt, tile_of=lambda i: i):
    vals, k = [], 0
    i = tile_of(pl.program_id(0))
    rows = lax.broadcasted_iota(jnp.int32, (tm, 1), 0)
    for kind, n, _ in plan:
        main = refs[k][...].astype(F32)
        if kind == "prev":
            edge = jnp.where(i == 0, 0.0, refs[k + 1][SUBLANES - 1:SUBLANES, :].astype(F32))
            main = jnp.where(rows == 0, edge, pltpu.roll(main, 1, 0))
        elif kind == "next":
            edge = jnp.where(i == nt - 1, 0.0, refs[k + 1][0:1, :].astype(F32))
            main = jnp.where(rows == tm - 1, edge, pltpu.roll(main, tm - 1, 0))
        vals.append(main)
        k += n
    return vals


def _tile_rows(xs, tm):
    arr = xs[0]
    if isinstance(arr, tuple):
        arr = arr[1] if isinstance(arr[0], str) else arr[0]
    return min(tm, arr.shape[0]), arr.shape[0]


def _rowwise(f, xs, params, out_groups, out_dtypes, *, tm, name, comm=None):
    tm, t = _tile_rows(xs, tm)
    nt = t // tm
    xa, xspecs, plan = _x_plan(xs, tm, t)
    pa, pspecs = (zip(*[_par_spec(p) for p in params]) if params else ((), ()))
    nxr, npar = len(xa), len(pa)
    x_sds = [SDS((tm, w), F32) for _, _, w in plan]
    p_sds = [SDS(s.block_shape, F32) for s in pspecs]
    outs_sds = jax.eval_shape(lambda *vals: f(*vals), *x_sds, *p_sds)
    widths = [sum(outs_sds[ix].shape[1] for ix in idxs) for idxs in out_groups]

    def body(*refs):
        vals = _x_vals(refs[:nxr], plan, tm, nt) + [r[...].astype(F32) for r in refs[nxr:nxr + npar]]
        outs = f(*vals)
        _store_groups(refs[nxr + npar:], out_groups, outs)

    res, carried = _hosting_call(
        body, comm, name=name, grid=(nt,), in_specs=list(xspecs) + list(pspecs),
        out_specs=[pl.BlockSpec((tm, w), lambda i: (i, 0)) for w in widths],
        out_shape=[SDS((t, w), dt) for w, dt in zip(widths, out_dtypes)], scratch_shapes=[], args=(*xa, *pa))
    return res if comm is None else (res, carried)


def _rowwise_bwd(f, xs, params, cots, *, x_grad, p_grad, dx_groups, dx_dtypes, tm, name, extra=None, comm=None,
                 fold_next=None):
    tm, t = _tile_rows(xs, tm)
    nt = t // tm
    tile_of = (lambda i: nt - 1 - i) if fold_next else (lambda i: i)
    xa, xspecs, plan = _x_plan(xs, tm, t, tile_of)
    pa, pspecs = (zip(*[_par_spec(p) for p in params]) if params else ((), ()))
    ca, cspecs = zip(*[_row_spec(c, tm, tile_of) for c in cots])
    extra = extra or {}
    ekeys = sorted(extra)
    ea, especs = (zip(*[_row_spec(extra[k], tm, tile_of) for k in ekeys]) if ekeys else ((), ()))
    nx, nxr, npar, nc, ne = len(plan), len(xa), len(pa), len(ca), len(ea)
    gx = [i for i in range(nx) if x_grad[i]]
    gp = [i for i in range(npar) if p_grad[i]]
    all_widths = [sum(plan[gx[ix]][2] for ix in idxs) for idxs in dx_groups]
    emitted = [k for k in range(len(dx_groups)) if not (fold_next and k == fold_next[1])]
    widths = [all_widths[k] for k in emitted]
    ng = len(emitted)

    def body(*refs):
        ins = refs[:nxr + npar + nc + ne]
        outs = refs[nxr + npar + nc + ne:]
        vals = _x_vals(ins[:nxr], plan, tm, nt, tile_of) + [r[...].astype(F32) for r in ins[nxr:nxr + npar]]
        cvals = tuple(r[...].astype(F32) for r in ins[nxr + npar:nxr + npar + nc])
        evals = [r[...].astype(F32) for r in ins[nxr + npar + nc:]]
        diff_idx = gx + [nx + i for i in gp]

        def g(*dargs):
            full = list(vals)
            for ix, v in zip(diff_idx, dargs):
                full[ix] = v
            return tuple(f(*full))

        _, vjp = jax.vjp(g, *[vals[ix] for ix in diff_idx])
        grads = vjp(cvals)
        dxs = list(grads[:len(gx)])
        for k, ev in zip(ekeys, evals):
            dxs[k] = dxs[k] + ev
        _store_groups(outs[:ng], [dx_groups[k] for k in emitted], dxs)
        i = pl.program_id(0)
        if fold_next:
            main_ref, carry_ref = outs[emitted.index(fold_next[0])], refs[-1]
            rows = lax.broadcasted_iota(jnp.int32, (tm, 1), 0)
            off = 0
            for ix in dx_groups[fold_next[1]]:
                piece = dxs[ix]
                cols = slice(off, off + piece.shape[1])
                edge = jnp.where(i == 0, 0.0, carry_ref[0:1, cols])
                main_ref[:, cols] += jnp.where(rows == tm - 1, edge, pltpu.roll(piece, tm - 1, 0))
                carry_ref[:, cols] = piece[:SUBLANES]
                off += piece.shape[1]
        for ref, gval in zip(outs[ng:ng + len(gp)], grads[len(gx):]):
            @pl.when(i == 0)
            def _(ref=ref):
                ref[...] = jnp.zeros_like(ref)
            ref[...] += gval

    dp_specs = [pl.BlockSpec(pspecs[i].block_shape, lambda i: (0, 0)) for i in gp]
    dp_shapes = [SDS(pspecs[i].block_shape, F32) for i in gp]
    scratch = [pltpu.VMEM((SUBLANES, all_widths[fold_next[1]]), F32)] if fold_next else []
    res, carried = _hosting_call(
        body, comm, name=name, grid=(nt,), in_specs=list(xspecs) + list(pspecs) + list(cspecs) + list(especs),
        out_specs=[pl.BlockSpec((tm, w), lambda i: (tile_of(i), 0)) for w in widths] + dp_specs,
        out_shape=[SDS((t, w), dt) for w, dt in zip(widths, dx_dtypes)] + dp_shapes, scratch_shapes=scratch,
        args=(*xa, *pa, *ca, *ea))
    return res if comm is None else (res, carried)


def _rms_f(x, g):
    return (x * lax.rsqrt(jnp.mean(x * x, axis=-1, keepdims=True) + NORM_EPS) * g,)


def _group_sum_impl(x, ones_bd):
    p1, p2 = _split2(x)
    dot = lambda p: lax.dot_general(p, ones_bd.astype(BF16), (((1,), (0,)), ((), ())), preferred_element_type=F32)
    return dot(p1) + dot(p2)


@jax.custom_vjp
def _group_sum(x, ones_bd):
    return _group_sum_impl(x, ones_bd)


_group_sum.defvjp(lambda x, o: (_group_sum_impl(x, o), o),
                  lambda o, g: (_group_sum_impl(g, o), jnp.zeros_like(o)))


def _rwkv_prep_f(r, k, v, lo, rp, kp, vp, lop, mu_r, mu_k, mu_v, mu_lo, w0, w2p, a0, a2p, g2p, k_k, k_a, ones_bd):
    r = r + mu_r * (rp - r)
    k = k + mu_k * (kp - k)
    v = v + mu_v * (vp - v)
    lo = lo + mu_lo * (lop - lo)
    w_log = -_softplus(-(w0 + _nn(jnp.tanh(lo), w2p))) - 0.5
    lw = -jnp.exp(w_log)
    a_g = _sigmoid(a0 + _nn(lo, a2p))
    g = _nn(_sigmoid(lo), g2p)
    kk = k * k_k
    kk = kk / jnp.maximum(jnp.sqrt(_group_sum(kk * kk, ones_bd)), L2_EPS)
    k2 = k * (1.0 + (a_g - 1.0) * k_a)
    return r, lw, k2, v, -kk, kk * a_g, g


def _rwkv_post_f(y, r, k2, v, g, r_k, gn_w, gn_b, ones_bd):
    inv_n = 1.0 / HB_DIM
    mean = _group_sum(y, ones_bd) * inv_n
    yc = y - mean
    var = _group_sum(yc * yc, ones_bd) * inv_n
    yn = yc * lax.rsqrt(var + RWKV_GN_EPS) * gn_w + gn_b
    bonus = _group_sum(r * k2 * r_k, ones_bd) * v
    return ((yn + bonus) * g,)


def _tri(c, strict=False):
    ii = lax.broadcasted_iota(jnp.int32, (c, c), 0)
    jj = lax.broadcasted_iota(jnp.int32, (c, c), 1)
    return (jj < ii) if strict else (jj <= ii)


def _hgrn_step(st0, q_a, f_a, i_a, g_a, l0, l1, onorm):
    nh, nj = len(q_a), len(q_a[0])
    c = q_a[0][0].shape[0]
    combos = [(j, h) for j in range(nj) for h in range(nh)]
    every = lambda fn: {q: fn(q) for q in combos}
    at_ = lambda d: (lambda q: d[q[1]][q[0]])
    qa_, fa_, ia_, ga_ = (at_(z) for z in (q_a, f_a, i_a, g_a))
    incl = _tri(c)
    rows = lax.broadcasted_iota(jnp.int32, (c, 1), 0)
    lb = []
    for h in range(nh):
        mx = jnp.maximum(l0[h], l1[h])
        e0, e1 = jnp.exp(l0[h] - mx), jnp.exp(l1[h] - mx)
        lb.append(e0 / (e0 + e1))
    forget = every(lambda q: lb[q[1]] + (1.0 - lb[q[1]]) * _sigmoid(fa_(q)))
    qs = every(lambda q: _silu(qa_(q)))
    kk = every(lambda q: 1.0 - forget[q])
    lf = every(lambda q: jnp.log(forget[q]))
    bcum = every(lambda q: _cumsum_rows(lf[q]))
    bref = every(lambda q: jnp.sum(jnp.where(rows <= c // 2, lf[q], 0.0), axis=0, keepdims=True))
    blast = every(lambda q: jnp.sum(lf[q], axis=0, keepdims=True))
    scores = every(lambda q: jnp.where(incl, _nt(qs[q] * jnp.exp(bcum[q] - bref[q]),
                                                 kk[q] * jnp.exp(bref[q] - bcum[q])), 0.0))
    intra = every(lambda q: _nn(scores[q], ia_(q)))
    qb = every(lambda q: qs[q] * jnp.exp(bcum[q]))
    upd = every(lambda q: _tn(ia_(q), kk[q] * jnp.exp(blast[q] - bcum[q])))
    dec = every(lambda q: jnp.exp(blast[q]))
    st = list(st0)
    o = {}
    for j in range(nj):
        for h in range(nh):
            o[(j, h)] = intra[(j, h)] + _nt(qb[(j, h)], st[h])
        st = [st[h] * dec[(j, h)] + upd[(j, h)] for h in range(nh)]
    out = every(lambda q: o[q] * lax.rsqrt(jnp.mean(o[q] * o[q], axis=-1, keepdims=True) + NORM_EPS)
                * onorm[q[1]] * _silu(ga_(q)))
    return [[out[(j, h)] for j in range(nj)] for h in range(nh)], st


def _hgrn_blocks(ref, nj, c):
    return [[ref[j * c:(j + 1) * c, h * HA_DIM:(h + 1) * HA_DIM] for j in range(nj)] for h in range(HA_HEADS)]


def _hgrn_cols(ref):
    return [ref[:, h * HA_DIM:(h + 1) * HA_DIM] for h in range(HA_HEADS)]


def _hgrn_fwd(p_h, l0, l1, onorm):
    t = p_h.shape[0]
    cc, nj = HGRN_CHUNK, HGRN_GROUP
    c = cc * nj
    n = t // c

    def body(q_ref, f_ref, i_ref, g_ref, l0_ref, l1_ref, on_ref, o_ref, hs_ref, st_ref):
        @pl.when(pl.program_id(0) == 0)
        def _():
            st_ref[...] = jnp.zeros_like(st_ref)

        hs_ref[0] = st_ref[...]
        o, st1 = _hgrn_step([st_ref[h] for h in range(HA_HEADS)],
                            *[_hgrn_blocks(ref, nj, cc) for ref in (q_ref, f_ref, i_ref, g_ref)],
                            _hgrn_cols(l0_ref), _hgrn_cols(l1_ref), _hgrn_cols(on_ref))
        for h in range(HA_HEADS):
            for j in range(nj):
                o_ref[j * cc:(j + 1) * cc, h * HA_DIM:(h + 1) * HA_DIM] = o[h][j]
            st_ref[h] = st1[h]

    col = lambda j: pl.BlockSpec((c, W_A), lambda i, j=j: (i, j))
    par = pl.BlockSpec((1, W_A), lambda i: (0, 0))
    return pl.pallas_call(
        body, name="hgrn_fwd", grid=(n,), in_specs=[col(0), col(1), col(2), col(3), par, par, par],
        out_specs=[pl.BlockSpec((c, W_A), lambda i: (i, 0)),
                   pl.BlockSpec((1, HA_HEADS, HA_DIM, HA_DIM), lambda i: (i, 0, 0, 0))],
        out_shape=[SDS((t, W_A), F32), SDS((n, HA_HEADS, HA_DIM, HA_DIM), F32)],
        scratch_shapes=[pltpu.VMEM((HA_HEADS, HA_DIM, HA_DIM), F32)],
        compiler_params=_params(("arbitrary",)))(p_h, p_h, p_h, p_h, l0, l1, onorm)


def _hgrn_bwd(p_h, l0, l1, onorm, hs, do, do_col, comm=None):
    t = p_h.shape[0]
    cc, nj = HGRN_CHUNK, HGRN_GROUP
    c = cc * nj
    n = t // c

    def body(q_ref, f_ref, i_ref, g_ref, l0_ref, l1_ref, on_ref, hs_ref, do_ref,
             dp_ref, dl0_ref, dl1_ref, don_ref, dst_ref):
        @pl.when(pl.program_id(0) == 0)
        def _():
            dst_ref[...] = jnp.zeros_like(dst_ref)
            dl0_ref[...] = jnp.zeros_like(dl0_ref)
            dl1_ref[...] = jnp.zeros_like(dl1_ref)
            don_ref[...] = jnp.zeros_like(don_ref)

        args = ([hs_ref[0, h] for h in range(HA_HEADS)],
                *[_hgrn_blocks(ref, nj, cc) for ref in (q_ref, f_ref, i_ref, g_ref)],
                _hgrn_cols(l0_ref), _hgrn_cols(l1_ref), _hgrn_cols(on_ref))
        _, vjp = jax.vjp(_hgrn_step, *args)
        dst0, dq, df, di, dg, dl0, dl1, don = vjp((_hgrn_blocks(do_ref, nj, cc),
                                                   [dst_ref[h] for h in range(HA_HEADS)]))
        for h in range(HA_HEADS):
            sl = slice(h * HA_DIM, (h + 1) * HA_DIM)
            for k, dv in enumerate((dq, df, di, dg)):
                for j in range(nj):
                    dp_ref[j * cc:(j + 1) * cc, k * W_A + h * HA_DIM:k * W_A + (h + 1) * HA_DIM] = dv[h][j]
            dl0_ref[:, sl] += dl0[h]
            dl1_ref[:, sl] += dl1[h]
            don_ref[:, sl] += don[h]
            dst_ref[h] = dst0[h]

    col = lambda j: pl.BlockSpec((c, W_A), lambda i, j=j: (n - 1 - i, j))
    par = pl.BlockSpec((1, W_A), lambda i: (0, 0))
    return _hosting_call(
        body, comm, name="hgrn_bwd", grid=(n,),
        in_specs=[col(0), col(1), col(2), col(3), par, par, par,
                  pl.BlockSpec((1, HA_HEADS, HA_DIM, HA_DIM), lambda i: (n - 1 - i, 0, 0, 0)),
                  pl.BlockSpec((c, W_A), lambda i: (n - 1 - i, do_col))],
        out_specs=[pl.BlockSpec((c, N_HGRN_COLS), lambda i: (n - 1 - i, 0)), par, par, par],
        out_shape=[SDS((t, N_HGRN_COLS), F32), SDS((1, W_A), F32), SDS((1, W_A), F32), SDS((1, W_A), F32)],
        scratch_shapes=[pltpu.VMEM((HA_HEADS, HA_DIM, HA_DIM), F32)],
        args=(p_h, p_h, p_h, p_h, l0, l1, onorm, hs, do))


HB_PAIRS = HB_HEADS // 2
PAIR_W = 2 * HB_DIM


def _head_lane_masks():
    lane = lax.broadcasted_iota(jnp.int32, (1, PAIR_W), 1)
    return (lane < HB_DIM).astype(F32), (lane >= HB_DIM).astype(F32)


@jax.custom_vjp
def _stack_heads(x):
    m0, m1 = _head_lane_masks()
    return jnp.concatenate([x * m0, x * m1], axis=0)


def _stack_heads_bwd(_, g):
    m0, m1 = _head_lane_masks()
    c = g.shape[0] // 2
    return (g[:c] * m0 + g[c:] * m1,)


_stack_heads.defvjp(lambda x: (_stack_heads(x), None), _stack_heads_bwd)


@jax.custom_vjp
def _unstack_heads(ys):
    c = ys.shape[0] // 2
    return ys[:c] + ys[c:]


_unstack_heads.defvjp(lambda ys: (_unstack_heads(ys), None), lambda _, g: (_stack_heads(g),))


def _same_head_block(c):
    ii = lax.broadcasted_iota(jnp.int32, (2 * c, 2 * c), 0)
    jj = lax.broadcasted_iota(jnp.int32, (2 * c, 2 * c), 1)
    same = (ii < c) == (jj < c)
    return same & (jj <= ii), same & (jj < ii), (ii == jj).astype(F32)


@jax.custom_vjp
def _rows_join(top, bottom):
    return jnp.concatenate([top, bottom], axis=0)


def _rows_join_bwd(n_top, g):
    return g[:n_top], g[n_top:]


_rows_join.defvjp(lambda top, bottom: (_rows_join(top, bottom), top.shape[0]), _rows_join_bwd)


def _rows_split_impl(x, n_top):
    return x[:n_top], x[n_top:]


_rows_split = jax.custom_vjp(_rows_split_impl, nondiff_argnums=(1,))
_rows_split.defvjp(lambda x, n_top: (_rows_split_impl(x, n_top), None),
                   lambda n_top, _, g: (jnp.concatenate([g[0], g[1]], axis=0),))


def _rwkv_step(s0, r, lw, k, v, a, b):
    npair, nj = len(r), len(r[0])
    c = r[0][0].shape[0]
    combos = [(j, p) for j in range(nj) for p in range(npair)]
    every = lambda fn: {q: fn(q) for q in combos}
    at_ = lambda d: (lambda q: d[q[1]][q[0]])
    r_, lw_, k_, v_, a_, b_ = (at_(z) for z in (r, lw, k, v, a, b))
    incl, strict, eye = _same_head_block(c)

    gam = every(lambda q: _cumsum_rows(lw_(q)))
    gtot = every(lambda q: jnp.sum(lw_(q), axis=0, keepdims=True))
    eneg = every(lambda q: jnp.exp(-gam[q]))
    edec = every(lambda q: jnp.exp(gtot[q] - gam[q]))
    at = every(lambda q: _stack_heads(a_(q) * jnp.exp(gam[q] - lw_(q))))
    rt = every(lambda q: _stack_heads(r_(q) * jnp.exp(gam[q])))
    bt = every(lambda q: _stack_heads(b_(q) * eneg[q]))
    kt = every(lambda q: _stack_heads(k_(q) * eneg[q]))
    bdec = every(lambda q: _stack_heads(b_(q) * edec[q]))
    kdec = every(lambda q: _stack_heads(k_(q) * edec[q]))
    vs = every(lambda q: _stack_heads(v_(q)))
    a_ab = every(lambda q: jnp.where(strict, _nt(at[q], bt[q]), 0.0))
    a_ak = every(lambda q: jnp.where(strict, _nt(at[q], kt[q]), 0.0))
    a_rb = every(lambda q: jnp.where(incl, _nt(rt[q], bt[q]), 0.0))
    a_rk = every(lambda q: jnp.where(incl, _nt(rt[q], kt[q]), 0.0))
    tinv = every(lambda q: eye + a_ab[q])
    pw = a_ab
    span = 2
    while span < c:
        pw = every(lambda q, pw=pw: _nn_x3(pw[q], pw[q]))
        tinv = every(lambda q, pw=pw, tinv=tinv: tinv[q] + _nn_x3(pw[q], tinv[q]))
        span *= 2
    akv = every(lambda q: _nn(a_ak[q], vs[q]))
    w1 = every(lambda q: _nn(tinv[q], at[q]))
    u0 = every(lambda q: _nn(tinv[q], akv[q]))
    wr = every(lambda q: _rows_join(w1[q], rt[q]))
    bk = every(lambda q: _rows_join(bdec[q], kdec[q]))
    yv = every(lambda q: _nn(a_rk[q], vs[q]))
    gdec = every(lambda q: jnp.exp(gtot[q]))

    s = list(s0)
    y = [[None] * nj for _ in range(npair)]
    for j in range(nj):
        both = {p: _rows_split(_nt(wr[(j, p)], s[p]), 2 * c) for p in range(npair)}
        u = {p: both[p][0] + u0[(j, p)] for p in range(npair)}
        for p in range(npair):
            y[p][j] = _unstack_heads(both[p][1] + _nn(a_rb[(j, p)], u[p]) + yv[(j, p)])
        s = [s[p] * gdec[(j, p)] + _tn(_rows_join(u[p], vs[(j, p)]), bk[(j, p)]) for p in range(npair)]
    return y, s


def _rwkv_blocks(ref, nj, c):
    return [[ref[j * c:(j + 1) * c, p * PAIR_W:(p + 1) * PAIR_W] for j in range(nj)] for p in range(HB_PAIRS)]


def _rwkv_fwd(seqs, comm=None):
    t = seqs[0].shape[0]
    c, nj = RWKV_CHUNK, RWKV_GROUP
    n = t // (c * nj)

    def body(r_ref, lw_ref, k_ref, v_ref, a_ref, b_ref, y_ref, hs_ref, st_ref):
        @pl.when(pl.program_id(0) == 0)
        def _():
            st_ref[...] = jnp.zeros_like(st_ref)

        hs_ref[0] = st_ref[...]
        s0 = [st_ref[p] for p in range(HB_PAIRS)]
        y, s1 = _rwkv_step(s0, *[_rwkv_blocks(ref, nj, c) for ref in (r_ref, lw_ref, k_ref, v_ref, a_ref, b_ref)])
        for p in range(HB_PAIRS):
            for j in range(nj):
                y_ref[j * c:(j + 1) * c, p * PAIR_W:(p + 1) * PAIR_W] = y[p][j]
            st_ref[p] = s1[p]

    seq = pl.BlockSpec((c * nj, W_B), lambda i: (i, 0))
    return _hosting_call(
        body, comm, name="rwkv_fwd", grid=(n,), in_specs=[seq] * 6,
        out_specs=[seq, pl.BlockSpec((1, HB_PAIRS, PAIR_W, PAIR_W), lambda i: (i, 0, 0, 0))],
        out_shape=[SDS((t, W_B), F32), SDS((n, HB_PAIRS, PAIR_W, PAIR_W), F32)],
        scratch_shapes=[pltpu.VMEM((HB_PAIRS, PAIR_W, PAIR_W), F32)], args=tuple(seqs))


def _rwkv_bwd(seqs, hs, dy, comm=None):
    t = seqs[0].shape[0]
    c, nj = RWKV_CHUNK, RWKV_GROUP
    n = t // (c * nj)

    def body(r_ref, lw_ref, k_ref, v_ref, a_ref, b_ref, hs_ref, dy_ref,
             dr_ref, dlw_ref, dk_ref, dv_ref, da_ref, db_ref, dst_ref):
        @pl.when(pl.program_id(0) == 0)
        def _():
            dst_ref[...] = jnp.zeros_like(dst_ref)

        s0 = [hs_ref[0, p] for p in range(HB_PAIRS)]
        seq_vals = [_rwkv_blocks(ref, nj, c) for ref in (r_ref, lw_ref, k_ref, v_ref, a_ref, b_ref)]
        _, vjp = jax.vjp(_rwkv_step, s0, *seq_vals)
        grads = vjp((_rwkv_blocks(dy_ref, nj, c), [dst_ref[p] for p in range(HB_PAIRS)]))
        for ref, gr in zip((dr_ref, dlw_ref, dk_ref, dv_ref, da_ref, db_ref), grads[1:]):
            for p in range(HB_PAIRS):
                for j in range(nj):
                    ref[j * c:(j + 1) * c, p * PAIR_W:(p + 1) * PAIR_W] = gr[p][j]
        m0, m1 = _head_lane_masks()
        rows0 = (lax.broadcasted_iota(jnp.int32, (PAIR_W, 1), 0) < HB_DIM).astype(F32)
        blocks = rows0 * m0 + (1.0 - rows0) * m1
        for p in range(HB_PAIRS):
            dst_ref[p] = grads[0][p] * blocks

    seq = pl.BlockSpec((c * nj, W_B), lambda i: (n - 1 - i, 0))
    return _hosting_call(
        body, comm, name="rwkv_bwd", grid=(n,),
        in_specs=[seq] * 6 + [pl.BlockSpec((1, HB_PAIRS, PAIR_W, PAIR_W), lambda i: (n - 1 - i, 0, 0, 0)), seq],
        out_specs=[seq] * 6, out_shape=[SDS((t, W_B), F32)] * 6,
        scratch_shapes=[pltpu.VMEM((HB_PAIRS, PAIR_W, PAIR_W), F32)], args=(*seqs, hs, dy))


def _final_loss(x3, fnorm, target, *, tm):
    t, d = x3.shape

    def body(x_ref, g_ref, t_ref, dx_ref, dg_ref, loss_ref):
        @pl.when(pl.program_id(0) == 0)
        def _():
            dg_ref[...] = jnp.zeros_like(dg_ref)
            loss_ref[...] = jnp.zeros_like(loss_ref)

        x, g = x_ref[...], g_ref[...]
        rinv = lax.rsqrt(jnp.mean(x * x, axis=-1, keepdims=True) + NORM_EPS)
        xh = x * rinv
        diff = xh * g - t_ref[...]
        loss_ref[...] += 0.5 * jnp.sum(jnp.mean(diff * diff, axis=-1, keepdims=True))
        dy = diff * (1.0 / d)
        dg_ref[...] += jnp.sum(dy * xh, axis=0, keepdims=True)
        dxh = dy * g
        dx_ref[...] = rinv * (dxh - xh * jnp.mean(dxh * xh, axis=-1, keepdims=True))

    row = pl.BlockSpec((tm, d), lambda i: (i, 0))
    return pl.pallas_call(
        body, name="final_loss", grid=(t // tm,), in_specs=[row, pl.BlockSpec((1, d), lambda i: (0, 0)), row],
        out_specs=[row, pl.BlockSpec((1, d), lambda i: (0, 0)), pl.BlockSpec((8, 128), lambda i: (0, 0))],
        out_shape=[SDS((t, d), F32), SDS((1, d), F32), SDS((8, 128), F32)],
        compiler_params=_params(("arbitrary",)))(x3, fnorm, target)


def _gate_up_act(h, wgt, wut, *, tm, tn, name, comm=None):
    t, d = h.shape
    tm = min(tm, t)

    def body(h_ref, g_ref, u_ref, a_out, u_out, act_out):
        hv = h_ref[...]
        a = _dg(hv, g_ref[...], 1, 1, False)
        u = _dg(hv, u_ref[...], 1, 1, False)
        a_out[...] = a.astype(a_out.dtype)
        u_out[...] = u.astype(u_out.dtype)
        act_out[...] = (_silu(a) * u).astype(act_out.dtype)

    wspec = pl.BlockSpec((tn, d), lambda i, j: (j, 0))
    ospec = pl.BlockSpec((tm, tn), lambda i, j: (i, j))
    return _hosting_call(
        body, comm, name=name, grid=(t // tm, D_FF // tn),
        in_specs=[pl.BlockSpec((tm, d), lambda i, j: (i, 0)), wspec, wspec], out_specs=[ospec, ospec, ospec],
        out_shape=[SDS((t, D_FF), BF16), SDS((t, D_FF), BF16), SDS((t, D_FF), BF16)], scratch_shapes=[],
        args=(h, wgt, wut))


def _dact_swiglu(dout, wd, a, u, *, tm, tn, name, comm=None):
    t, d = dout.shape
    tm = min(tm, t)

    def body(d_ref, w_ref, a_ref, u_ref, da_out, du_out):
        dact = 0.5 * _dg(d_ref[...], w_ref[...], 1, 1, False)
        av, uv = a_ref[...].astype(F32), u_ref[...].astype(F32)
        s = _sigmoid(av)
        da_out[...] = (dact * uv * (s * (1.0 + av * (1.0 - s)))).astype(da_out.dtype)
        du_out[...] = (dact * (av * s)).astype(du_out.dtype)

    tile = pl.BlockSpec((tm, tn), lambda i, j: (i, j))
    return _hosting_call(
        body, comm, name=name, grid=(t // tm, D_FF // tn),
        in_specs=[pl.BlockSpec((tm, d), lambda i, j: (i, 0)), pl.BlockSpec((tn, d), lambda i, j: (j, 0)), tile, tile],
        out_specs=[tile, tile], out_shape=[SDS((t, D_FF), BF16), SDS((t, D_FF), BF16)], scratch_shapes=[],
        args=(dout, wd, a, u))


class _Plan:
    def __init__(self):
        self.entries, self.counts = collections.defaultdict(list), {}

    def carry(self, host, comm_of, after):
        self.entries[host].append((comm_of, after))

    def comm(self, host, g):
        comms = [comm_of(g) for comm_of, _ in self.entries.get(host, [])]
        self.counts[host] = [len(c.arrays) for c in comms]
        return functools.reduce(_join_comms, comms) if comms else None

    def done(self, host, results, w):
        start = 0
        for (_, after), n in zip(self.entries.get(host, []), self.counts.get(host, [])):
            after(results[start:start + n], w)
            start += n


def _ffn_fwd(x, w, tag, plan, g):
    comm = plan.comm(f"{tag}_rms", g)
    res = _rowwise(_rms_f, [x], [w[f"{tag}_norm"]], [[0]], [BF16], tm=512, name=f"{tag}_rms", comm=comm)
    (h,), carried = res if comm is not None else (res, [])
    plan.done(f"{tag}_rms", carried, w)
    (a, u, act), carried = _gate_up_act(h, w[f"{tag}_wgt"], w[f"{tag}_wut"], tm=2048, tn=256, name=f"{tag}_gate_up",
                                        comm=plan.comm(f"{tag}_gate_up", g))
    plan.done(f"{tag}_gate_up", carried, w)
    comm = plan.comm(f"{tag}_down", g)
    out = _mm(act, w[f"{tag}_wd"], tm=1024, tn=D_MODEL, tk=D_FF, name=f"{tag}_down", res=x, scale=0.5, comm=comm)
    if comm is not None:
        out, carried = out
        plan.done(f"{tag}_down", carried, w)
    return out, (h, a, u, act)


def _ffn_bwd(dout, x, w, saved, tag, plan, g):
    h, a, u, act = saved

    def carrying(fn, host, *args, **kwargs):
        comm = plan.comm(host, g)
        res = fn(*args, name=host, comm=comm, **kwargs)
        out, carried = res if comm is not None else (res, [])
        plan.done(host, carried, w)
        return out

    (da, du), carried = _dact_swiglu(dout, w[f"{tag}_wd"], a, u, tm=2048, tn=256, name=f"{tag}_dact",
                                     comm=plan.comm(f"{tag}_dact", g))
    plan.done(f"{tag}_dact", carried, w)
    g[f"{tag}_wd"] = _mm(act, dout, ta=True, tm=D_FF // 2, tn=D_MODEL, tk=1024, name=f"{tag}_dwd", scale=0.5)
    g[f"{tag}_wgt"] = carrying(_mm, f"{tag}_dwg", da, h, ta=True, tm=D_FF // 2, tn=D_MODEL, tk=1024)
    g[f"{tag}_wut"] = carrying(_mm, f"{tag}_dwu", du, h, ta=True, tm=D_FF // 2, tn=D_MODEL, tk=1024)
    dh = carrying(_mm, f"{tag}_dh_g", da, w[f"{tag}_wgt"], tm=1024, tn=D_MODEL, tk=D_FF)
    dh = carrying(_mm, f"{tag}_dh_u", du, w[f"{tag}_wut"], tm=1024, tn=D_MODEL, tk=D_FF, res=dh)
    dx, g[f"{tag}_norm"] = carrying(_rowwise_bwd, f"{tag}_drms", _rms_f, [x], [w[f"{tag}_norm"]], [dh],
                                    x_grad=[True], p_grad=[True], dx_groups=[[0]], dx_dtypes=[F32], tm=512,
                                    extra={0: dout})
    return dx


def _local_step(x, target, w, plan=None):
    plan = plan or _Plan()
    ones_bd = jnp.kron(jnp.eye(HB_HEADS, dtype=F32), jnp.ones((HB_DIM, HB_DIM), F32))
    g = {}
    x1, ffn1_saved = _ffn_fwd(x, w, "ffn1", plan, g)
    hm, = _rowwise(_rms_f, [x1], [w["mix_norm"]], [[0]], [BF16], tm=512, name="mix_rms")
    p_h = _mm(hm, w["w_in_h"], tm=2048, tn=256, tk=D_MODEL, name="inproj_h")
    p_r = _mm(hm, w["w_in_r"], tm=2048, tn=256, tk=D_MODEL, name="inproj_r")
    o_a, hgrn_states = _hgrn_fwd(p_h, w["lb0"], w["lb1"], w["hgrn_out_norm"])

    mu = w["mu_pad"]
    prep_xs = [(p_r, W_B, 0), (p_r, W_B, 1), (p_r, W_B, 2), (p_r, LORA_PAD, 6),
               ("prev", p_r, W_B, 0), ("prev", p_r, W_B, 1), ("prev", p_r, W_B, 2), ("prev", p_r, LORA_PAD, 6)]
    prep_ps = [(mu, W_B, 0), (mu, W_B, 1), (mu, W_B, 2), (mu, LORA_PAD, 6), w["rwkv_w0"], w["w2_pad"], w["rwkv_a0"],
               w["a2_pad"], w["g2_pad"], w["rwkv_k_k"], w["rwkv_k_a"], ones_bd]
    prep_f = _rwkv_prep_f
    r, lw, k2, v, a_vec, b_vec, gate = _rowwise(prep_f, prep_xs, prep_ps, [[0], [1], [2], [3], [4], [5], [6]],
                                                [F32] * 7, tm=256, name="rwkv_prep")
    seqs = [r, lw, k2, v, a_vec, b_vec]
    (y, rwkv_states), carried = _rwkv_fwd(seqs, comm=plan.comm("rwkv_fwd", g))
    plan.done("rwkv_fwd", carried, w)
    post_f = _rwkv_post_f
    post_xs = [y, r, k2, v, gate]
    post_ps = [w["rwkv_r_k"], w["rwkv_gn_w"], w["rwkv_gn_b"], ones_bd]
    o, = _rowwise(lambda o_a_, *rest: (o_a_,) + tuple(post_f(*rest)), [o_a] + post_xs, post_ps, [[0, 1]], [F32],
                  tm=256, name="rwkv_post")
    x2 = _mm(o, w["w_out"], tm=2048, tn=256, tk=D_MODEL, name="outproj", res=x1)
    x3, ffn2_saved = _ffn_fwd(x2, w, "ffn2", plan, g)
    dx3, g["final_norm"], loss = _final_loss(x3, w["final_norm"], target, tm=256)

    dx2 = _ffn_bwd(dx3, x2, w, ffn2_saved, "ffn2", plan, g)
    do = _mm(dx2, w["w_out"], tb=True, tm=2048, tn=256, tk=D_MODEL, name="outproj_do")
    g["w_out"] = _mm(o, dx2, ta=True, tm=D_MODEL, tn=D_MODEL, tk=1024, name="outproj_dw")

    (dp_h, g["lb0"], g["lb1"], g["hgrn_out_norm"]), carried = _hgrn_bwd(
        p_h, w["lb0"], w["lb1"], w["hgrn_out_norm"], hgrn_states, do, 0, comm=plan.comm("hgrn_bwd", g))
    plan.done("hgrn_bwd", carried, w)
    post_out = _rowwise_bwd(post_f, post_xs, post_ps, [(do, W_B, 1)], x_grad=[True] * 5, p_grad=[True] * 3 + [False],
                            dx_groups=[[0], [1], [2], [3], [4]], dx_dtypes=[F32] * 5, tm=256, name="rwkv_post_bwd")
    dy, dr1, dk1, dv1, dgate, g["rwkv_r_k"], g["rwkv_gn_w"], g["rwkv_gn_b"] = post_out
    (dr2, dlw, dk2, dv2, da_vec, db_vec), carried = _rwkv_bwd(seqs, rwkv_states, dy, comm=plan.comm("rwkv_bwd", g))
    plan.done("rwkv_bwd", carried, w)

    def prep2_f(*vals):
        r_, lw_, k2_, v_, a_, b_, g_ = prep_f(*vals)
        return r_, lw_, k2_, v_, a_, b_, g_, r_, k2_, v_

    prep_comm = plan.comm("rwkv_prep_bwd", g)
    prep_out = _rowwise_bwd(prep2_f, prep_xs, prep_ps, [dr2, dlw, dk2, dv2, da_vec, db_vec, dgate, dr1, dk1, dv1],
                            x_grad=[True] * 8, p_grad=[True] * 11 + [False], dx_groups=[[0, 1, 2, 3], [4, 5, 6, 7]],
                            dx_dtypes=[F32], tm=256, name="rwkv_prep_bwd", fold_next=(0, 1), comm=prep_comm)
    prep_out, carried = prep_out if prep_comm is not None else (prep_out, [])
    plan.done("rwkv_prep_bwd", carried, w)
    dp_r = prep_out[0]
    (dmu_r, dmu_k, dmu_v, dmu_lo, g["rwkv_w0"], g["w2_pad"], g["rwkv_a0"], g["a2_pad"], g["g2_pad"],
     g["rwkv_k_k"], g["rwkv_k_a"]) = prep_out[1:]
    g["mu_pad"] = jnp.concatenate([dmu_r, dmu_k, dmu_v, dmu_lo], axis=1)
    dhm = _mm(dp_h, w["w_in_h"], tb=True, tm=1024, tn=D_MODEL, tk=N_HGRN_COLS, name="inproj_dh_h")
    dhm = _mm(dp_r, w["w_in_r"], tb=True, tm=1024, tn=D_MODEL, tk=N_RWKV_PAD, name="inproj_dh_r", res=dhm)
    g["w_in_h"] = _mm(hm, dp_h, ta=True, tm=D_MODEL, tn=D_MODEL, tk=1024, name="inproj_dw_h")
    g["w_in_r"] = _mm(hm, dp_r, ta=True, tm=D_MODEL, tn=N_RWKV_PAD // 2, tk=1024, name="inproj_dw_r")
    mix_comm = plan.comm("mix_drms", g)
    mix_out = _rowwise_bwd(_rms_f, [x1], [w["mix_norm"]], [dhm], x_grad=[True], p_grad=[True], dx_groups=[[0]],
                           dx_dtypes=[F32], tm=512, name="mix_drms", extra={0: dx2}, comm=mix_comm)
    (dx1, g["mix_norm"]), carried = mix_out if mix_comm is not None else (mix_out, [])
    plan.done("mix_drms", carried, w)
    dx0 = _ffn_bwd(dx1, x, w, ffn1_saved, "ffn1", plan, g)
    return loss, dx0, g


HBM_SPEC = pl.BlockSpec(memory_space=pl.ANY)

Comm = collections.namedtuple("Comm", "arrays out_shapes aliased sem_shapes start finish")


def _join_comms(first, second):
    n, s = len(first.arrays), len(first.sem_shapes)

    def start(ins, outs, sems):
        first.start(ins[:n], outs[:n], sems[:s])
        second.start(ins[n:], outs[n:], sems[s:])

    def finish(ins, outs, sems):
        first.finish(ins[:n], outs[:n], sems[:s])
        second.finish(ins[n:], outs[n:], sems[s:])

    return Comm(list(first.arrays) + list(second.arrays), list(first.out_shapes) + list(second.out_shapes),
                list(first.aliased) + list(second.aliased), list(first.sem_shapes) + list(second.sem_shapes),
                start, finish)


def _run_comm(comm, name):
    n = len(comm.arrays)

    def body(*refs):
        ins, outs, sems = refs[:n], refs[n:2 * n], refs[2 * n:]
        comm.start(ins, outs, sems)
        comm.finish(ins, outs, sems)

    return pl.pallas_call(
        body, name=name, in_specs=[HBM_SPEC] * n, out_specs=[HBM_SPEC] * n, out_shape=list(comm.out_shapes),
        input_output_aliases={t: t for t in range(n) if comm.aliased[t]},
        scratch_shapes=list(comm.sem_shapes))(*comm.arrays)


def _hosting_call(body, comm, *, name, grid, in_specs, out_specs, out_shape, scratch_shapes, args):
    sem = ("arbitrary",) * len(grid)
    if comm is None:
        res = pl.pallas_call(body, name=name, grid=grid, in_specs=in_specs, out_specs=out_specs, out_shape=out_shape,
                             scratch_shapes=scratch_shapes, compiler_params=_params(sem))(*args)
        return list(res), []
    ni, no, ns, nc = len(in_specs), len(out_specs), len(scratch_shapes), len(comm.arrays)

    def wrapped(*refs):
        ins, cins = refs[:ni], refs[ni:ni + nc]
        outs, couts = refs[ni + nc:ni + nc + no], refs[ni + nc + no:ni + 2 * nc + no]
        scr, sems = refs[ni + 2 * nc + no:ni + 2 * nc + no + ns], refs[ni + 2 * nc + no + ns:]
        first = functools.reduce(jnp.logical_and, [pl.program_id(k) == 0 for k in range(len(grid))])
        last = functools.reduce(jnp.logical_and, [pl.program_id(k) == grid[k] - 1 for k in range(len(grid))])

        @pl.when(first)
        def _():
            comm.start(cins, couts, sems)

        body(*ins, *outs, *scr)

        @pl.when(last)
        def _():
            comm.finish(cins, couts, sems)

    res = pl.pallas_call(
        wrapped, name=name, grid=grid, in_specs=list(in_specs) + [HBM_SPEC] * nc,
        out_specs=list(out_specs) + [HBM_SPEC] * nc, out_shape=list(out_shape) + list(comm.out_shapes),
        scratch_shapes=list(scratch_shapes) + list(comm.sem_shapes),
        input_output_aliases={ni + t: no + t for t in range(nc) if comm.aliased[t]},
        compiler_params=_params(sem))(*args, *comm.arrays)
    return list(res[:no]), list(res[no:])


def _chips(x, y):
    return [(1 - x, y), (x, 1 - y), (1 - x, 1 - y)]


def _gather_comm(bufs):
    n = len(bufs)

    def copies(outs, sems):
        ici_send, ici_recv, d2d_send, d2d_recv = sems
        x, y, c = lax.axis_index("x"), lax.axis_index("y"), lax.axis_index("c")

        def half(t, slot, hc):
            hr = bufs[t].shape[1] // 2
            return outs[t].at[slot, pl.ds(pl.multiple_of(hc * hr, 16), hr), :]

        def ici(t, j, slot, px, py):
            return pltpu.make_async_remote_copy(src_ref=half(t, slot, c), dst_ref=half(t, slot, c),
                                                send_sem=ici_send.at[3 * t + j], recv_sem=ici_recv.at[3 * t + j],
                                                device_id=(px, py, c), device_id_type=MESH)

        def d2d(t, j, slot, hc):
            return pltpu.make_async_remote_copy(src_ref=half(t, slot, hc), dst_ref=half(t, slot, hc),
                                                send_sem=d2d_send.at[3 * t + j], recv_sem=d2d_recv.at[3 * t + j],
                                                device_id=(x, y, 1 - c), device_id_type=MESH)

        peers = [(t, j, px, py) for t in range(n) for j, (px, py) in enumerate(_chips(x, y))]
        return ici, d2d, peers, 2 * x + y, c

    def start(ins, outs, sems):
        ici, _, peers, me, _ = copies(outs, sems)
        for t, j, px, py in peers:
            ici(t, j, me, px, py).start()

    def finish(ins, outs, sems):
        ici, d2d, peers, me, c = copies(outs, sems)
        for t, j, px, py in peers:
            ici(t, j, 2 * px + py, px, py).wait_recv()
            d2d(t, j, 2 * px + py, c).start()
        for t, j, px, py in peers:
            d2d(t, j, 2 * px + py, 1 - c).wait_recv()
        for t, j, px, py in peers:
            ici(t, j, me, px, py).wait_send()
            d2d(t, j, 2 * px + py, c).wait_send()

    return Comm(list(bufs), [SDS(b.shape, b.dtype) for b in bufs], [True] * n,
                [pltpu.SemaphoreType.DMA((3 * n,))] * 4, start, finish)


def _sibling_exchange_comm(gs):
    n = len(gs)

    def copies(ins, outs, sems):
        x, y, c = lax.axis_index("x"), lax.axis_index("y"), lax.axis_index("c")
        cps = []
        for t in range(n):
            hr = gs[t].shape[1] // 2
            src = ins[t].at[:, pl.ds(pl.multiple_of((1 - c) * hr, SUBLANES), hr), :]
            cps.append(pltpu.make_async_remote_copy(src_ref=src, dst_ref=outs[t], send_sem=sems[0].at[t],
                                                    recv_sem=sems[1].at[t], device_id=(x, y, 1 - c),
                                                    device_id_type=MESH))
        return cps

    def start(ins, outs, sems):
        for cp in copies(ins, outs, sems):
            cp.start()

    def finish(ins, outs, sems):
        for cp in copies(ins, outs, sems):
            cp.wait()

    return Comm(list(gs), [SDS((N_CHIPS, g.shape[1] // 2, g.shape[2]), g.dtype) for g in gs], [False] * n,
                [pltpu.SemaphoreType.DMA((n,))] * 2, start, finish)


def _own_rows(c, rows):
    return pl.ds(pl.multiple_of(c * (rows // 2), 16), rows // 2)


def _chip_exchange_comm(ss):
    n = len(ss)

    def copies(ins, outs, sems):
        x, y, c = lax.axis_index("x"), lax.axis_index("y"), lax.axis_index("c")
        me = 2 * x + y

        def copy(t, j, px, py, src_slot, dst_slot):
            rows = _own_rows(c, ss[t].shape[1])
            return pltpu.make_async_remote_copy(src_ref=ins[t].at[src_slot, rows, :],
                                                dst_ref=outs[t].at[dst_slot, rows, :],
                                                send_sem=sems[0].at[3 * t + j], recv_sem=sems[1].at[3 * t + j],
                                                device_id=(px, py, c), device_id_type=MESH)

        peers = [(t, j, px, py) for t in range(n) for j, (px, py) in enumerate(_chips(x, y))]
        return copy, peers, me

    def start(ins, outs, sems):
        copy, peers, me = copies(ins, outs, sems)
        for t, j, px, py in peers:
            copy(t, j, px, py, 2 * px + py, me).start()

    def finish(ins, outs, sems):
        copy, peers, me = copies(ins, outs, sems)
        for t, j, px, py in peers:
            copy(t, j, px, py, me, 2 * px + py).wait_recv()
        for t, j, px, py in peers:
            copy(t, j, px, py, 2 * px + py, me).wait_send()

    return Comm(list(ss), [SDS(s.shape, s.dtype) for s in ss], [False] * n,
                [pltpu.SemaphoreType.DMA((3 * n,))] * 2, start, finish)


def _sibling_swap_comm(rs, ss):
    n = len(rs)

    def copies(outs, sems):
        x, y, c = lax.axis_index("x"), lax.axis_index("y"), lax.axis_index("c")
        cps = []
        for t in range(n):
            rows = _own_rows(c, rs[t].shape[1])
            held = [outs[t].at[2 * px + py, rows, :] for px, py in _chips(x, y)] + [outs[n + t].at[2 * x + y, rows, :]]
            cps += [pltpu.make_async_remote_copy(src_ref=ref, dst_ref=ref, send_sem=sems[0].at[4 * t + j],
                                                 recv_sem=sems[1].at[4 * t + j], device_id=(x, y, 1 - c),
                                                 device_id_type=MESH) for j, ref in enumerate(held)]
        return cps

    def start(ins, outs, sems):
        for cp in copies(outs, sems):
            cp.start()

    def finish(ins, outs, sems):
        for cp in copies(outs, sems):
            cp.wait()

    both = list(rs) + list(ss)
    return Comm(both, [SDS(b.shape, b.dtype) for b in both], [True] * (2 * n),
                [pltpu.SemaphoreType.DMA((4 * n,))] * 2, start, finish)


def _row_tile(rows, cap=512):
    best = SUBLANES
    for tr in range(SUBLANES, min(rows, cap) + 1, SUBLANES):
        if rows % tr == 0:
            best = tr
    return best


def _add_halves(g4, r4, c_idx, name):
    _, hr, lanes = r4.shape
    tr = _row_tile(hr)
    nb = hr // tr

    def body(c_ref, a_ref, b_ref, o_ref):
        o_ref[...] = (a_ref[...] + b_ref[...]).astype(o_ref.dtype)

    owned = pl.BlockSpec((None, tr, lanes), lambda q, i, c_ref: (q, c_ref[0] * nb + i, 0))
    grid_spec = pltpu.PrefetchScalarGridSpec(
        num_scalar_prefetch=1, grid=(N_CHIPS, nb),
        in_specs=[owned, pl.BlockSpec((None, tr, lanes), lambda q, i, c_ref: (q, i, 0))], out_specs=owned)
    return pl.pallas_call(body, name=name, grid_spec=grid_spec, out_shape=SDS(g4.shape, BF16),
                          compiler_params=_params(("parallel", "parallel")))(c_idx, g4, r4)


def _adamw(wf, r4, s4, mf, vf, me_idx, name):
    rows, lanes = wf.shape
    tr = _row_tile(rows, cap=176)
    c1 = 1.0 / (1.0 - ADAM_B1 ** ADAM_STEP)
    c2 = 1.0 / (1.0 - ADAM_B2 ** ADAM_STEP)

    def body(me_ref, w_ref, a_ref, b_ref, c_ref, d_ref, own_ref, m_ref, v_ref, g_ref, delta_ref, nm_ref, nv_ref):
        own = own_ref[...].astype(F32)
        p = [jnp.where(me_ref[0] == q, own, ref[...].astype(F32)) for q, ref in enumerate((a_ref, b_ref, c_ref, d_ref))]
        gv = ((p[0] + p[1]) + p[2]) + p[3]
        m = ADAM_B1 * m_ref[...] + (1.0 - ADAM_B1) * gv
        v = ADAM_B2 * v_ref[...] + (1.0 - ADAM_B2) * (gv * gv)
        g_ref[...] = gv
        delta_ref[...] = -ADAM_LR * ((m * c1) / (jnp.sqrt(v * c2) + ADAM_EPS) + ADAM_WD * w_ref[...])
        nm_ref[...] = m
        nv_ref[...] = v

    other = lambda q: (lambda i, me_ref: (jnp.where(me_ref[0] == q, (q + 1) % N_CHIPS, q), i, 0))
    full = pl.BlockSpec((tr, lanes), lambda i, me_ref: (i, 0))
    grid_spec = pltpu.PrefetchScalarGridSpec(
        num_scalar_prefetch=1, grid=(rows // tr,),
        in_specs=[full] + [pl.BlockSpec((None, tr, lanes), other(q)) for q in range(N_CHIPS)]
        + [pl.BlockSpec((None, tr, lanes), lambda i, me_ref: (me_ref[0], i, 0)), full, full],
        out_specs=[full] * 4)
    return pl.pallas_call(body, name=name, grid_spec=grid_spec, out_shape=[SDS((rows, lanes), F32)] * 4,
                          compiler_params=_params(("parallel",)))(me_idx, wf, r4, r4, r4, r4, s4, mf, vf)


BIG = ("ffn1_w_gate", "ffn1_w_up", "ffn1_w_down", "ffn2_w_gate", "ffn2_w_up", "ffn2_w_down", "w_out", "w_in")
TRANSPOSED = ("ffn1_w_gate", "ffn1_w_up", "ffn2_w_gate", "ffn2_w_up")
PACKED = ("rwkv_w2", "rwkv_a2", "rwkv_g2")
SMALL_SHAPES = {"ffn1_norm": (1, D_MODEL), "mix_norm": (1, D_MODEL), "hgrn_lb_logits": (2, W_A),
                "hgrn_out_norm": (1, W_A), "rwkv_shift_mu": (1, N_RWKV_COLS), "rwkv_w0": (1, W_B),
                "rwkv_a0": (1, W_B), "rwkv_k_k": (1, W_B), "rwkv_k_a": (1, W_B),
                "rwkv_r_k": (1, HB_HEADS, HB_DIM), "rwkv_gn_w": (1, W_B), "rwkv_gn_b": (1, W_B),
                "ffn2_norm": (1, D_MODEL), "final_norm": (D_MODEL,)}
PACK_ELEMS = sum(_numel(_shard_shape(n)) for n in PACKED) + sum(_numel(SMALL_SHAPES[n]) for n in SMALL)
PACK_ROWS = -(-PACK_ELEMS // (32 * LANES)) * 32


def _to_rows(name, shard):
    return shard[0].T if name in TRANSPOSED else shard[0]


def _from_rows(name, rows):
    return (rows.T if name in TRANSPOSED else rows)[None]


def _pack(sharded, small):
    flat = jnp.concatenate([sharded[n].reshape(-1) for n in PACKED] + [small[n].reshape(-1) for n in SMALL])
    return jnp.pad(flat, (0, PACK_ROWS * LANES - flat.shape[0])).reshape(PACK_ROWS, LANES)


def _unpack(packed):
    flat, out, off = packed.reshape(-1), {}, 0
    for n in PACKED:
        shp = _shard_shape(n)
        out[n] = flat[off:off + _numel(shp)].reshape((1,) + shp)
        off += _numel(shp)
    for n in SMALL:
        shp = SMALL_SHAPES[n]
        out[n] = flat[off:off + _numel(shp)].reshape(shp)
        off += _numel(shp)
    return out


def _quarter(full, name, q):
    shape, ax = SHARDED_SHAPES[name]
    w = shape[ax] // N_CHIPS
    return lax.slice_in_dim(full, q * w, (q + 1) * w, axis=ax)


def kernel(x, ffn1_norm, ffn1_w_gate, ffn1_w_up, ffn1_w_down, mix_norm, w_in, hgrn_lb_logits, hgrn_out_norm, rwkv_shift_mu, rwkv_w0, rwkv_w2, rwkv_a0, rwkv_a2, rwkv_g2, rwkv_k_k, rwkv_k_a, rwkv_r_k, rwkv_gn_w, rwkv_gn_b, w_out, ffn2_norm, ffn2_w_gate, ffn2_w_up, ffn2_w_down, final_norm, loss_target, m_ffn1_norm, m_ffn1_w_gate, m_ffn1_w_up, m_ffn1_w_down, m_mix_norm, m_w_in, m_hgrn_lb_logits, m_hgrn_out_norm, m_rwkv_shift_mu, m_rwkv_w0, m_rwkv_w2, m_rwkv_a0, m_rwkv_a2, m_rwkv_g2, m_rwkv_k_k, m_rwkv_k_a, m_rwkv_r_k, m_rwkv_gn_w, m_rwkv_gn_b, m_w_out, m_ffn2_norm, m_ffn2_w_gate, m_ffn2_w_up, m_ffn2_w_down, m_final_norm, v_ffn1_norm, v_ffn1_w_gate, v_ffn1_w_up, v_ffn1_w_down, v_mix_norm, v_w_in, v_hgrn_lb_logits, v_hgrn_out_norm, v_rwkv_shift_mu, v_rwkv_w0, v_rwkv_w2, v_rwkv_a0, v_rwkv_a2, v_rwkv_g2, v_rwkv_k_k, v_rwkv_k_a, v_rwkv_r_k, v_rwkv_gn_w, v_rwkv_gn_b, v_w_out, v_ffn2_norm, v_ffn2_w_gate, v_ffn2_w_up, v_ffn2_w_down, v_final_norm):
    args = dict(locals())
    wts = {n: args[n] for n in ALL_WEIGHTS}
    moms = {n: args["m_" + n] for n in ALL_WEIGHTS}
    vars_ = {n: args["v_" + n] for n in ALL_WEIGHTS}

    me = 2 * lax.axis_index("x") + lax.axis_index("y")
    c_idx = lax.axis_index("c").astype(jnp.int32).reshape(1)
    me_idx = me.astype(jnp.int32).reshape(1)
    shard_of = {n: _to_rows(n, wts[n]).astype(BF16) for n in BIG}
    shard_of["packed"] = _pack(wts, {n: wts[n] for n in SMALL}).astype(BF16)
    group = {"ffn1": BIG[0:3], "ffn2": BIG[3:6]}

    def slot_bufs(names):
        return [lax.dynamic_update_slice(lax.empty((N_CHIPS,) + shard_of[n].shape, BF16), shard_of[n][None],
                                         (me, 0, 0)) for n in names]

    def ffn_weights(tag, gathered):
        return {f"{tag}_wgt": gathered[0].reshape(D_FF, D_MODEL), f"{tag}_wut": gathered[1].reshape(D_FF, D_MODEL),
                f"{tag}_wd": gathered[2].reshape(D_FF, D_MODEL)}

    def w_in_weights(gathered):
        w_in_full = jnp.concatenate([gathered[0][q] for q in range(N_CHIPS)], axis=1)
        return {"w_in_h": w_in_full[:, :N_HGRN_COLS],
                "w_in_r": jnp.pad(w_in_full[:, N_HGRN_COLS:], ((0, 0), (0, N_RWKV_PAD - N_RWKV_COLS)))}

    def mixer_weights(gathered):
        w_out_full = gathered[0].reshape(D_MODEL, D_MODEL)
        packs = gathered[1].reshape(N_CHIPS, PACK_ROWS * LANES)
        full, off = {}, 0
        for n in PACKED:
            shp = _shard_shape(n)
            full[n] = jnp.concatenate([packs[q, off:off + _numel(shp)].reshape(shp) for q in range(N_CHIPS)], axis=1)
            off += _numel(shp)
        zrow = lambda nrow: jnp.zeros((nrow, W_B), BF16)
        return {"w_out": w_out_full,
                "w2_pad": jnp.concatenate([full["rwkv_w2"], zrow(LORA_PAD - 32)], axis=0),
                "a2_pad": jnp.concatenate([zrow(32), full["rwkv_a2"], zrow(LORA_PAD - 64)], axis=0),
                "g2_pad": jnp.concatenate([zrow(64), full["rwkv_g2"], zrow(LORA_PAD - 160)], axis=0)}

    plan = _Plan()
    w = {}
    plan.carry("ffn1_rms", lambda g: _gather_comm(slot_bufs(group["ffn1"][:2])),
               lambda res, w_: w_.update({"ffn1_wgt": res[0].reshape(D_FF, D_MODEL),
                                          "ffn1_wut": res[1].reshape(D_FF, D_MODEL)}))

    def after_gate_up(res, w_):
        w_["ffn1_wd"] = res[0].reshape(D_FF, D_MODEL)
        w_.update(w_in_weights(res[1:]))

    plan.carry("ffn1_gate_up", lambda g: _gather_comm(slot_bufs(("ffn1_w_down", "w_in"))), after_gate_up)
    plan.carry("ffn1_down", lambda g: _gather_comm(slot_bufs(("w_out", "packed"))),
               lambda res, w_: w_.update(mixer_weights(res)))
    plan.carry("rwkv_fwd", lambda g: _gather_comm(slot_bufs(group["ffn2"])),
               lambda res, w_: w_.update(ffn_weights("ffn2", res)))
    w["ffn1_norm"], w["ffn2_norm"] = ffn1_norm, ffn2_norm
    w["mix_norm"] = mix_norm
    w["lb0"], w["lb1"] = hgrn_lb_logits[0:1], hgrn_lb_logits[1:2]
    w["hgrn_out_norm"] = hgrn_out_norm
    w["mu_pad"] = jnp.pad(rwkv_shift_mu, ((0, 0), (0, N_RWKV_PAD - N_RWKV_COLS)))
    for n in ("rwkv_w0", "rwkv_a0", "rwkv_k_k", "rwkv_k_a", "rwkv_gn_w", "rwkv_gn_b"):
        w[n] = wts[n]
    w["rwkv_r_k"] = rwkv_r_k.reshape(1, W_B)
    w["final_norm"] = final_norm.reshape(1, D_MODEL)

    def reduce_rows(names, gs):
        r1 = _run_comm(_sibling_exchange_comm(gs), "grad_sibling_exchange")
        s4 = [_add_halves(gt, rt, c_idx, f"grad_add_halves_{n}") for gt, rt, n in zip(gs, r1, names)]
        return list(zip(_run_comm(_chip_exchange_comm(s4), "grad_chip_exchange"), s4))

    def swap_comm(names):
        return _sibling_swap_comm([early[n][0] for n in names], [early[n][1] for n in names])

    def after_swap(names):
        return lambda res, w_: swapped.update(zip(names, zip(res[:len(names)], res[len(names):])))

    early, swapped = {}, {}

    def reduce_early(names, grads_of, sibling_host, chips_host, swap_host):
        def sibling_comm(g):
            early[names, "gs"] = grads_of(g)
            return _sibling_exchange_comm(early[names, "gs"])

        def after_sibling(res, w_):
            early[names, "s4"] = [_add_halves(gt, rt, c_idx, f"grad_add_halves_{n}")
                                  for gt, rt, n in zip(early[names, "gs"], res, names)]

        plan.carry(sibling_host, sibling_comm, after_sibling)
        plan.carry(chips_host, lambda g: _chip_exchange_comm(early[names, "s4"]),
                   lambda res, w_: early.update(zip(names, zip(res, early[names, "s4"]))))
        if swap_host:
            plan.carry(swap_host, lambda g: swap_comm(names), after_swap(names))

    def proj_grads(g):
        g_w_in = jnp.concatenate([g["w_in_h"], g["w_in_r"][:, :N_RWKV_COLS]], axis=1)
        return [g["w_out"].reshape(N_CHIPS, -1, D_MODEL),
                jnp.stack([_quarter(g_w_in, "w_in", q) for q in range(N_CHIPS)])]

    rows_of = lambda keys: (lambda g: [g[k].reshape(N_CHIPS, -1, D_MODEL) for k in keys])
    reduce_early(group["ffn2"], rows_of(("ffn2_wgt", "ffn2_wut", "ffn2_wd")), "hgrn_bwd", "rwkv_bwd", "rwkv_prep_bwd")
    reduce_early(("w_out", "w_in"), proj_grads, "mix_drms", "ffn1_dact", "ffn1_dwg")
    reduce_early(("ffn1_w_down",), rows_of(("ffn1_wd",)), "ffn1_dwg", "ffn1_dwu", "ffn1_dh_g")
    reduce_early(("ffn1_w_gate",), rows_of(("ffn1_wgt",)), "ffn1_dwu", "ffn1_dh_g", "ffn1_dh_u")
    reduce_early(("ffn1_w_up",), rows_of(("ffn1_wut",)), "ffn1_dh_g", "ffn1_dh_u", None)
    loss_slab, grad_x, g = _local_step(x[0], loss_target[0], w, plan)
    loss = lax.psum(loss_slab[0, 0], ("x", "y", "c"))

    gfull = {
        "rwkv_w2": g["w2_pad"][0:32], "rwkv_a2": g["a2_pad"][32:64], "rwkv_g2": g["g2_pad"][64:160],
    }
    gsmall = {
        "ffn1_norm": g["ffn1_norm"], "mix_norm": g["mix_norm"],
        "hgrn_lb_logits": jnp.concatenate([g["lb0"], g["lb1"]], axis=0), "hgrn_out_norm": g["hgrn_out_norm"],
        "rwkv_shift_mu": g["mu_pad"][:, :N_RWKV_COLS], "rwkv_w0": g["rwkv_w0"], "rwkv_a0": g["rwkv_a0"],
        "rwkv_k_k": g["rwkv_k_k"], "rwkv_k_a": g["rwkv_k_a"], "rwkv_r_k": g["rwkv_r_k"],
        "rwkv_gn_w": g["rwkv_gn_w"], "rwkv_gn_b": g["rwkv_gn_b"], "ffn2_norm": g["ffn2_norm"],
        "final_norm": g["final_norm"],
    }
    packed = jnp.stack([_pack({n: _quarter(gfull[n], n, q) for n in PACKED}, gsmall) for q in range(N_CHIPS)])
    early["packed"], = reduce_rows(["packed"], [packed])
    last = ["ffn1_w_up", "packed"]
    after_swap(last)(_run_comm(swap_comm(last), "grad_sibling_swap"), w)
    names = list(BIG) + ["packed"]

    def rows_list(d):
        return [_to_rows(n, d[n]) for n in BIG] + [_pack(d, {n: d[n] for n in SMALL})]

    outs = [_adamw(wt, *swapped[n], mt, vt, me_idx, f"adamw_{n}")
            for wt, mt, vt, n in zip(rows_list(wts), rows_list(moms), rows_list(vars_), names)]
    results = []
    for k in range(4):
        per = [outs[i][k] for i in range(len(names))]
        d = {n: _from_rows(n, z) for n, z in zip(BIG, per[:-1])}
        d.update(_unpack(per[-1]))
        results.append(d)
    return (loss, grad_x[None], *[r[n] for r in results for n in ALL_WEIGHTS])
```

```python
import collections
import functools

import jax
import jax.numpy as jnp
from jax import lax
from jax.experimental import pallas as pl
from jax.experimental.pallas import tpu as pltpu

F32 = jnp.float32
BF16 = jnp.bfloat16
SDS = jax.ShapeDtypeStruct
MESH = pl.DeviceIdType.MESH

D_MODEL = 1024
D_FF = 2816
W_A = 512
W_B = 512
HA_HEADS, HA_DIM = 4, 128
HB_HEADS, HB_DIM = 8, 64
HGRN_CHUNK = 64
HGRN_GROUP = 8
RWKV_CHUNK = 16
RWKV_GROUP = 8
N_HGRN_COLS = 4 * W_A
N_RWKV_COLS = 3 * W_B + 32 + 32 + 96
N_RWKV_PAD = 1792
LORA_PAD = 256
NORM_EPS = 1e-6
RWKV_GN_EPS = 64e-5
L2_EPS = 1e-12
ADAM_LR, ADAM_B1, ADAM_B2, ADAM_EPS, ADAM_WD, ADAM_STEP = 0.001, 0.9, 0.999, 1e-8, 0.01, 10

N_CHIPS = 4
VMEM_LIMIT_V7X = 56 * 1024 * 1024
LANES = 1024

SHARDED_SHAPES = {
    "ffn1_w_gate": ((D_MODEL, D_FF), 1), "ffn1_w_up": ((D_MODEL, D_FF), 1), "ffn1_w_down": ((D_FF, D_MODEL), 0),
    "w_in": ((D_MODEL, N_HGRN_COLS + N_RWKV_COLS), 1), "rwkv_w2": ((32, W_B), 1), "rwkv_a2": ((32, W_B), 1),
    "rwkv_g2": ((96, W_B), 1), "w_out": ((D_MODEL, D_MODEL), 0),
    "ffn2_w_gate": ((D_MODEL, D_FF), 1), "ffn2_w_up": ((D_MODEL, D_FF), 1), "ffn2_w_down": ((D_FF, D_MODEL), 0),
}
SMALL = ("ffn1_norm", "mix_norm", "hgrn_lb_logits", "hgrn_out_norm", "rwkv_shift_mu", "rwkv_w0", "rwkv_a0",
         "rwkv_k_k", "rwkv_k_a", "rwkv_r_k", "rwkv_gn_w", "rwkv_gn_b", "ffn2_norm", "final_norm")
ALL_WEIGHTS = ("ffn1_norm", "ffn1_w_gate", "ffn1_w_up", "ffn1_w_down", "mix_norm", "w_in", "hgrn_lb_logits",
               "hgrn_out_norm", "rwkv_shift_mu", "rwkv_w0", "rwkv_w2", "rwkv_a0", "rwkv_a2", "rwkv_g2", "rwkv_k_k",
               "rwkv_k_a", "rwkv_r_k", "rwkv_gn_w", "rwkv_gn_b", "w_out", "ffn2_norm", "ffn2_w_gate", "ffn2_w_up",
               "ffn2_w_down", "final_norm")


def _shard_shape(name):
    shape, ax = SHARDED_SHAPES[name]
    return tuple(s // N_CHIPS if i == ax else s for i, s in enumerate(shape))


def _numel(shape):
    n = 1
    for s in shape:
        n *= s
    return n


def _params(sem=None):
    return pltpu.CompilerParams(dimension_semantics=sem, vmem_limit_bytes=VMEM_LIMIT_V7X)


def _split2(x):
    hi = x.astype(BF16)
    return hi, (x.astype(F32) - hi.astype(F32)).astype(BF16)


def _dg(x, y, cx, cy, hi):
    dn = (((cx,), (cy,)), ((), ()))
    dot = lambda p, q: lax.dot_general(p, q, dn, preferred_element_type=F32)
    if hi == "x3":
        (xh, xl), (yh, yl) = _split2(x), _split2(y)
        return dot(xh, yh) + (dot(xh, yl) + dot(xl, yh))
    return dot(x.astype(BF16), y.astype(BF16))


def _make_mm(hi, cotangent_forms=None):
    @jax.custom_vjp
    def nn(x, y):
        return _dg(x, y, 1, 0, hi)

    @jax.custom_vjp
    def nt(x, y):
        return _dg(x, y, 1, 1, hi)

    @jax.custom_vjp
    def tn(x, y):
        return _dg(x, y, 0, 0, hi)

    bnn, bnt, btn = cotangent_forms or (nn, nt, tn)
    nn.defvjp(lambda x, y: (nn(x, y), (x, y)), lambda r, g: (bnt(g, r[1]), btn(r[0], g)))
    nt.defvjp(lambda x, y: (nt(x, y), (x, y)), lambda r, g: (bnn(g, r[1]), btn(g, r[0])))
    tn.defvjp(lambda x, y: (tn(x, y), (x, y)), lambda r, g: (bnt(r[1], g), bnn(r[0], g)))
    return nn, nt, tn


_nn, _nt, _tn = _make_mm(False)
_nn_x3, _nt_x3, _tn_x3 = _make_mm("x3", (_nn, _nt, _tn))


def _tri_apply(x, transpose):
    c = x.shape[0]
    tri = (lax.broadcasted_iota(jnp.int32, (c, c), 1) <= lax.broadcasted_iota(jnp.int32, (c, c), 0)).astype(BF16)
    dn = (((0 if transpose else 1,), (0,)), ((), ()))
    p1, p2 = _split2(x)
    dot = lambda p: lax.dot_general(tri, p, dn, preferred_element_type=F32)
    return dot(p1) + dot(p2)


@jax.custom_vjp
def _cumsum_rows(x):
    return _tri_apply(x, False)


_cumsum_rows.defvjp(lambda x: (_tri_apply(x, False), None), lambda _, g: (_tri_apply(g, True),))


def _sigmoid(x):
    return 1.0 / (1.0 + jnp.exp(-x))


def _silu(x):
    return x * _sigmoid(x)


def _softplus(z):
    return jnp.maximum(z, 0.0) + jnp.log(1.0 + jnp.exp(-jnp.abs(z)))


def _mm(a, b, *, ta=False, tb=False, tm, tn, tk, name, out_dtype=F32, res=None, scale=None, comm=None):
    m = a.shape[1] if ta else a.shape[0]
    kdim = a.shape[0] if ta else a.shape[1]
    n = b.shape[0] if tb else b.shape[1]
    assert (b.shape[1] if tb else b.shape[0]) == kdim
    tm, tn, tk = min(tm, m), min(tn, n), min(tk, kdim)
    assert m % tm == 0 and n % tn == 0 and kdim % tk == 0, (name, m, n, kdim)
    nk = kdim // tk
    a_spec = pl.BlockSpec((tk, tm), lambda i, j, k: (k, i)) if ta else pl.BlockSpec((tm, tk), lambda i, j, k: (i, k))
    b_spec = pl.BlockSpec((tn, tk), lambda i, j, k: (j, k)) if tb else pl.BlockSpec((tk, tn), lambda i, j, k: (k, j))
    o_spec = pl.BlockSpec((tm, tn), lambda i, j, k: (i, j))
    ca, cb = (0 if ta else 1), (1 if tb else 0)

    def body(*refs):
        if res is not None:
            a_ref, b_ref, r_ref, o_ref, acc_ref = refs
        else:
            a_ref, b_ref, o_ref, acc_ref = refs
        k = pl.program_id(2)

        @pl.when(k == 0)
        def _():
            acc_ref[...] = jnp.zeros_like(acc_ref)

        acc_ref[...] += _dg(a_ref[...], b_ref[...], ca, cb, False)

        @pl.when(k == nk - 1)
        def _():
            acc = acc_ref[...]
            if scale is not None:
                acc = acc * scale
            if res is not None:
                acc = r_ref[...] + acc
            o_ref[...] = acc.astype(out_dtype)

    in_specs = [a_spec, b_spec] + ([o_spec] if res is not None else [])
    args = (a, b) + ((res,) if res is not None else ())
    if comm is None:
        return pl.pallas_call(
            body, name=name, grid=(m // tm, n // tn, nk), in_specs=in_specs, out_specs=o_spec,
            out_shape=SDS((m, n), out_dtype), scratch_shapes=[pltpu.VMEM((tm, tn), F32)],
            compiler_params=_params(("parallel", "parallel", "arbitrary")))(*args)
    (out,), carried = _hosting_call(
        body, comm, name=name, grid=(m // tm, n // tn, nk), in_specs=in_specs, out_specs=[o_spec],
        out_shape=[SDS((m, n), out_dtype)], scratch_shapes=[pltpu.VMEM((tm, tn), F32)], args=args)
    return out, carried


def _row_spec(x, tm, tile_of=lambda i: i):
    if isinstance(x, tuple):
        arr, w, j = x
        return arr, pl.BlockSpec((tm, w), lambda i, j=j: (tile_of(i), j))
    return x, pl.BlockSpec((tm, x.shape[1]), lambda i: (tile_of(i), 0))


def _par_spec(p):
    if isinstance(p, tuple):
        arr, w, j = p
        return arr, pl.BlockSpec((arr.shape[0], w), lambda i, j=j: (0, j))
    return p, pl.BlockSpec(p.shape, lambda i: (0, 0))


def _store_groups(refs, groups, vals):
    for ref, idxs in zip(refs, groups):
        off = 0
        for ix in idxs:
            v = vals[ix]
            ref[:, off:off + v.shape[1]] = v.astype(ref.dtype)
            off += v.shape[1]


SUBLANES = 8


def _x_plan(xs, tm, t, tile_of=lambda i: i):
    arrays, specs, plan = [], [], []
    nb = tm // SUBLANES
    for x in xs:
        if isinstance(x, tuple) and isinstance(x[0], str):
            kind, arr, w, j = x
            if kind == "prev":
                halo = lambda i, j=j: (jnp.maximum(tile_of(i) * nb - 1, 0), j)
            else:
                halo = lambda i, j=j: (jnp.minimum((tile_of(i) + 1) * nb, t // SUBLANES - 1), j)
            arrays += [arr, arr]
            specs += [pl.BlockSpec((tm, w), lambda i, j=j: (tile_of(i), j)), pl.BlockSpec((SUBLANES, w), halo)]
            plan.append((kind, 2, w))
        else:
            arr, spec = _row_spec(x, tm, tile_of)
            arrays.append(arr)
            specs.append(spec)
            plan.append(("plain", 1, spec.block_shape[1]))
    return arrays, specs, plan


def _x_vals(refs, plan, tm, nt, tile_of=lambda i: i):
    vals, k = [], 0
    i = tile_of(pl.program_id(0))
    rows = lax.broadcasted_iota(jnp.int32, (tm, 1), 0)
    for kind, n, _ in plan:
        main = refs[k][...].astype(F32)
        if kind == "prev":
            edge = jnp.where(i == 0, 0.0, refs[k + 1][SUBLANES - 1:SUBLANES, :].astype(F32))
            main = jnp.where(rows == 0, edge, pltpu.roll(main, 1, 0))
        elif kind == "next":
            edge = jnp.where(i == nt - 1, 0.0, refs[k + 1][0:1, :].astype(F32))
            main = jnp.where(rows == tm - 1, edge, pltpu.roll(main, tm - 1, 0))
        vals.append(main)
        k += n
    return vals


def _tile_rows(xs, tm):
    arr = xs[0]
    if isinstance(arr, tuple):
        arr = arr[1] if isinstance(arr[0], str) else arr[0]
    return min(tm, arr.shape[0]), arr.shape[0]


def _rowwise(f, xs, params, out_groups, out_dtypes, *, tm, name, comm=None):
    tm, t = _tile_rows(xs, tm)
    nt = t // tm
    xa, xspecs, plan = _x_plan(xs, tm, t)
    pa, pspecs = (zip(*[_par_spec(p) for p in params]) if params else ((), ()))
    nxr, npar = len(xa), len(pa)
    x_sds = [SDS((tm, w), F32) for _, _, w in plan]
    p_sds = [SDS(s.block_shape, F32) for s in pspecs]
    outs_sds = jax.eval_shape(lambda *vals: f(*vals), *x_sds, *p_sds)
    widths = [sum(outs_sds[ix].shape[1] for ix in idxs) for idxs in out_groups]

    def body(*refs):
        vals = _x_vals(refs[:nxr], plan, tm, nt) + [r[...].astype(F32) for r in refs[nxr:nxr + npar]]
        outs = f(*vals)
        _store_groups(refs[nxr + npar:], out_groups, outs)

    res, carried = _hosting_call(
        body, comm, name=name, grid=(nt,), in_specs=list(xspecs) + list(pspecs),
        out_specs=[pl.BlockSpec((tm, w), lambda i: (i, 0)) for w in widths],
        out_shape=[SDS((t, w), dt) for w, dt in zip(widths, out_dtypes)], scratch_shapes=[], args=(*xa, *pa))
    return res if comm is None else (res, carried)


def _rowwise_bwd(f, xs, params, cots, *, x_grad, p_grad, dx_groups, dx_dtypes, tm, name, extra=None, comm=None,
                 fold_next=None):
    tm, t = _tile_rows(xs, tm)
    nt = t // tm
    tile_of = (lambda i: nt - 1 - i) if fold_next else (lambda i: i)
    xa, xspecs, plan = _x_plan(xs, tm, t, tile_of)
    pa, pspecs = (zip(*[_par_spec(p) for p in params]) if params else ((), ()))
    ca, cspecs = zip(*[_row_spec(c, tm, tile_of) for c in cots])
    extra = extra or {}
    ekeys = sorted(extra)
    ea, especs = (zip(*[_row_spec(extra[k], tm, tile_of) for k in ekeys]) if ekeys else ((), ()))
    nx, nxr, npar, nc, ne = len(plan), len(xa), len(pa), len(ca), len(ea)
    gx = [i for i in range(nx) if x_grad[i]]
    gp = [i for i in range(npar) if p_grad[i]]
    all_widths = [sum(plan[gx[ix]][2] for ix in idxs) for idxs in dx_groups]
    emitted = [k for k in range(len(dx_groups)) if not (fold_next and k == fold_next[1])]
    widths = [all_widths[k] for k in emitted]
    ng = len(emitted)

    def body(*refs):
        ins = refs[:nxr + npar + nc + ne]
        outs = refs[nxr + npar + nc + ne:]
        vals = _x_vals(ins[:nxr], plan, tm, nt, tile_of) + [r[...].astype(F32) for r in ins[nxr:nxr + npar]]
        cvals = tuple(r[...].astype(F32) for r in ins[nxr + npar:nxr + npar + nc])
        evals = [r[...].astype(F32) for r in ins[nxr + npar + nc:]]
        diff_idx = gx + [nx + i for i in gp]

        def g(*dargs):
            full = list(vals)
            for ix, v in zip(diff_idx, dargs):
                full[ix] = v
            return tuple(f(*full))

        _, vjp = jax.vjp(g, *[vals[ix] for ix in diff_idx])
        grads = vjp(cvals)
        dxs = list(grads[:len(gx)])
        for k, ev in zip(ekeys, evals):
            dxs[k] = dxs[k] + ev
        _store_groups(outs[:ng], [dx_groups[k] for k in emitted], dxs)
        i = pl.program_id(0)
        if fold_next:
            main_ref, carry_ref = outs[emitted.index(fold_next[0])], refs[-1]
            rows = lax.broadcasted_iota(jnp.int32, (tm, 1), 0)
            off = 0
            for ix in dx_groups[fold_next[1]]:
                piece = dxs[ix]
                cols = slice(off, off + piece.shape[1])
                edge = jnp.where(i == 0, 0.0, carry_ref[0:1, cols])
                main_ref[:, cols] += jnp.where(rows == tm - 1, edge, pltpu.roll(piece, tm - 1, 0))
                carry_ref[:, cols] = piece[:SUBLANES]
                off += piece.shape[1]
        for ref, gval in zip(outs[ng:ng + len(gp)], grads[len(gx):]):
            @pl.when(i == 0)
            def _(ref=ref):
                ref[...] = jnp.zeros_like(ref)
            ref[...] += gval

    dp_specs = [pl.BlockSpec(pspecs[i].block_shape, lambda i: (0, 0)) for i in gp]
    dp_shapes = [SDS(pspecs[i].block_shape, F32) for i in gp]
    scratch = [pltpu.VMEM((SUBLANES, all_widths[fold_next[1]]), F32)] if fold_next else []
    res, carried = _hosting_call(
        body, comm, name=name, grid=(nt,), in_specs=list(xspecs) + list(pspecs) + list(cspecs) + list(especs),
        out_specs=[pl.BlockSpec((tm, w), lambda i: (tile_of(i), 0)) for w in widths] + dp_specs,
        out_shape=[SDS((t, w), dt) for w, dt in zip(widths, dx_dtypes)] + dp_shapes, scratch_shapes=scratch,
        args=(*xa, *pa, *ca, *ea))
    return res if comm is None else (res, carried)


def _rms_f(x, g):
    return (x * lax.rsqrt(jnp.mean(x * x, axis=-1, keepdims=True) + NORM_EPS) * g,)


def _group_sum_impl(x, ones_bd):
    p1, p2 = _split2(x)
    dot = lambda p: lax.dot_general(p, ones_bd.astype(BF16), (((1,), (0,)), ((), ())), preferred_element_type=F32)
    return dot(p1) + dot(p2)


@jax.custom_vjp
def _group_sum(x, ones_bd):
    return _group_sum_impl(x, ones_bd)


_group_sum.defvjp(lambda x, o: (_group_sum_impl(x, o), o),
                  lambda o, g: (_group_sum_impl(g, o), jnp.zeros_like(o)))


def _rwkv_prep_f(r, k, v, lo, rp, kp, vp, lop, mu_r, mu_k, mu_v, mu_lo, w0, w2p, a0, a2p, g2p, k_k, k_a, ones_bd):
    r = r + mu_r * (rp - r)
    k = k + mu_k * (kp - k)
    v = v + mu_v * (vp - v)
    lo = lo + mu_lo * (lop - lo)
    w_log = -_softplus(-(w0 + _nn(jnp.tanh(lo), w2p))) - 0.5
    lw = -jnp.exp(w_log)
    a_g = _sigmoid(a0 + _nn(lo, a2p))
    g = _nn(_sigmoid(lo), g2p)
    kk = k * k_k
    kk = kk / jnp.maximum(jnp.sqrt(_group_sum(kk * kk, ones_bd)), L2_EPS)
    k2 = k * (1.0 + (a_g - 1.0) * k_a)
    return r, lw, k2, v, -kk, kk * a_g, g


def _rwkv_post_f(y, r, k2, v, g, r_k, gn_w, gn_b, ones_bd):
    inv_n = 1.0 / HB_DIM
    mean = _group_sum(y, ones_bd) * inv_n
    yc = y - mean
    var = _group_sum(yc * yc, ones_bd) * inv_n
    yn = yc * lax.rsqrt(var + RWKV_GN_EPS) * gn_w + gn_b
    bonus = _group_sum(r * k2 * r_k, ones_bd) * v
    return ((yn + bonus) * g,)


def _tri(c, strict=False):
    ii = lax.broadcasted_iota(jnp.int32, (c, c), 0)
    jj = lax.broadcasted_iota(jnp.int32, (c, c), 1)
    return (jj < ii) if strict else (jj <= ii)


def _hgrn_step(st0, q_a, f_a, i_a, g_a, l0, l1, onorm):
    nh, nj = len(q_a), len(q_a[0])
    c = q_a[0][0].shape[0]
    combos = [(j, h) for j in range(nj) for h in range(nh)]
    every = lambda fn: {q: fn(q) for q in combos}
    at_ = lambda d: (lambda q: d[q[1]][q[0]])
    qa_, fa_, ia_, ga_ = (at_(z) for z in (q_a, f_a, i_a, g_a))
    incl = _tri(c)
    rows = lax.broadcasted_iota(jnp.int32, (c, 1), 0)
    lb = []
    for h in range(nh):
        mx = jnp.maximum(l0[h], l1[h])
        e0, e1 = jnp.exp(l0[h] - mx), jnp.exp(l1[h] - mx)
        lb.append(e0 / (e0 + e1))
    forget = every(lambda q: lb[q[1]] + (1.0 - lb[q[1]]) * _sigmoid(fa_(q)))
    qs = every(lambda q: _silu(qa_(q)))
    kk = every(lambda q: 1.0 - forget[q])
    lf = every(lambda q: jnp.log(forget[q]))
    bcum = every(lambda q: _cumsum_rows(lf[q]))
    bref = every(lambda q: jnp.sum(jnp.where(rows <= c // 2, lf[q], 0.0), axis=0, keepdims=True))
    blast = every(lambda q: jnp.sum(lf[q], axis=0, keepdims=True))
    scores = every(lambda q: jnp.where(incl, _nt(qs[q] * jnp.exp(bcum[q] - bref[q]),
                                                 kk[q] * jnp.exp(bref[q] - bcum[q])), 0.0))
    intra = every(lambda q: _nn(scores[q], ia_(q)))
    qb = every(lambda q: qs[q] * jnp.exp(bcum[q]))
    upd = every(lambda q: _tn(ia_(q), kk[q] * jnp.exp(blast[q] - bcum[q])))
    dec = every(lambda q: jnp.exp(blast[q]))
    st = list(st0)
    o = {}
    for j in range(nj):
        for h in range(nh):
            o[(j, h)] = intra[(j, h)] + _nt(qb[(j, h)], st[h])
        st = [st[h] * dec[(j, h)] + upd[(j, h)] for h in range(nh)]
    out = every(lambda q: o[q] * lax.rsqrt(jnp.mean(o[q] * o[q], axis=-1, keepdims=True) + NORM_EPS)
                * onorm[q[1]] * _silu(ga_(q)))
    return [[out[(j, h)] for j in range(nj)] for h in range(nh)], st


def _hgrn_blocks(ref, nj, c):
    return [[ref[j * c:(j + 1) * c, h * HA_DIM:(h + 1) * HA_DIM] for j in range(nj)] for h in range(HA_HEADS)]


def _hgrn_cols(ref):
    return [ref[:, h * HA_DIM:(h + 1) * HA_DIM] for h in range(HA_HEADS)]


def _hgrn_fwd(p_h, l0, l1, onorm):
    t = p_h.shape[0]
    cc, nj = HGRN_CHUNK, HGRN_GROUP
    c = cc * nj
    n = t // c

    def body(q_ref, f_ref, i_ref, g_ref, l0_ref, l1_ref, on_ref, o_ref, hs_ref, st_ref):
        @pl.when(pl.program_id(0) == 0)
        def _():
            st_ref[...] = jnp.zeros_like(st_ref)

        hs_ref[0] = st_ref[...]
        o, st1 = _hgrn_step([st_ref[h] for h in range(HA_HEADS)],
                            *[_hgrn_blocks(ref, nj, cc) for ref in (q_ref, f_ref, i_ref, g_ref)],
                            _hgrn_cols(l0_ref), _hgrn_cols(l1_ref), _hgrn_cols(on_ref))
        for h in range(HA_HEADS):
            for j in range(nj):
                o_ref[j * cc:(j + 1) * cc, h * HA_DIM:(h + 1) * HA_DIM] = o[h][j]
            st_ref[h] = st1[h]

    col = lambda j: pl.BlockSpec((c, W_A), lambda i, j=j: (i, j))
    par = pl.BlockSpec((1, W_A), lambda i: (0, 0))
    return pl.pallas_call(
        body, name="hgrn_fwd", grid=(n,), in_specs=[col(0), col(1), col(2), col(3), par, par, par],
        out_specs=[pl.BlockSpec((c, W_A), lambda i: (i, 0)),
                   pl.BlockSpec((1, HA_HEADS, HA_DIM, HA_DIM), lambda i: (i, 0, 0, 0))],
        out_shape=[SDS((t, W_A), F32), SDS((n, HA_HEADS, HA_DIM, HA_DIM), F32)],
        scratch_shapes=[pltpu.VMEM((HA_HEADS, HA_DIM, HA_DIM), F32)],
        compiler_params=_params(("arbitrary",)))(p_h, p_h, p_h, p_h, l0, l1, onorm)


def _hgrn_bwd(p_h, l0, l1, onorm, hs, do, do_col, comm=None):
    t = p_h.shape[0]
    cc, nj = HGRN_CHUNK, HGRN_GROUP
    c = cc * nj
    n = t // c

    def body(q_ref, f_ref, i_ref, g_ref, l0_ref, l1_ref, on_ref, hs_ref, do_ref,
             dp_ref, dl0_ref, dl1_ref, don_ref, dst_ref):
        @pl.when(pl.program_id(0) == 0)
        def _():
            dst_ref[...] = jnp.zeros_like(dst_ref)
            dl0_ref[...] = jnp.zeros_like(dl0_ref)
            dl1_ref[...] = jnp.zeros_like(dl1_ref)
            don_ref[...] = jnp.zeros_like(don_ref)

        args = ([hs_ref[0, h] for h in range(HA_HEADS)],
                *[_hgrn_blocks(ref, nj, cc) for ref in (q_ref, f_ref, i_ref, g_ref)],
                _hgrn_cols(l0_ref), _hgrn_cols(l1_ref), _hgrn_cols(on_ref))
        _, vjp = jax.vjp(_hgrn_step, *args)
        dst0, dq, df, di, dg, dl0, dl1, don = vjp((_hgrn_blocks(do_ref, nj, cc),
                                                   [dst_ref[h] for h in range(HA_HEADS)]))
        for h in range(HA_HEADS):
            sl = slice(h * HA_DIM, (h + 1) * HA_DIM)
            for k, dv in enumerate((dq, df, di, dg)):
                for j in range(nj):
                    dp_ref[j * cc:(j + 1) * cc, k * W_A + h * HA_DIM:k * W_A + (h + 1) * HA_DIM] = dv[h][j]
            dl0_ref[:, sl] += dl0[h]
            dl1_ref[:, sl] += dl1[h]
            don_ref[:, sl] += don[h]
            dst_ref[h] = dst0[h]

    col = lambda j: pl.BlockSpec((c, W_A), lambda i, j=j: (n - 1 - i, j))
    par = pl.BlockSpec((1, W_A), lambda i: (0, 0))
    return _hosting_call(
        body, comm, name="hgrn_bwd", grid=(n,),
        in_specs=[col(0), col(1), col(2), col(3), par, par, par,
                  pl.BlockSpec((1, HA_HEADS, HA_DIM, HA_DIM), lambda i: (n - 1 - i, 0, 0, 0)),
                  pl.BlockSpec((c, W_A), lambda i: (n - 1 - i, do_col))],
        out_specs=[pl.BlockSpec((c, N_HGRN_COLS), lambda i: (n - 1 - i, 0)), par, par, par],
        out_shape=[SDS((t, N_HGRN_COLS), F32), SDS((1, W_A), F32), SDS((1, W_A), F32), SDS((1, W_A), F32)],
        scratch_shapes=[pltpu.VMEM((HA_HEADS, HA_DIM, HA_DIM), F32)],
        args=(p_h, p_h, p_h, p_h, l0, l1, onorm, hs, do))


HB_PAIRS = HB_HEADS // 2
PAIR_W = 2 * HB_DIM


def _head_lane_masks():
    lane = lax.broadcasted_iota(jnp.int32, (1, PAIR_W), 1)
    return (lane < HB_DIM).astype(F32), (lane >= HB_DIM).astype(F32)


@jax.custom_vjp
def _stack_heads(x):
    m0, m1 = _head_lane_masks()
    return jnp.concatenate([x * m0, x * m1], axis=0)


def _stack_heads_bwd(_, g):
    m0, m1 = _head_lane_masks()
    c = g.shape[0] // 2
    return (g[:c] * m0 + g[c:] * m1,)


_stack_heads.defvjp(lambda x: (_stack_heads(x), None), _stack_heads_bwd)


@jax.custom_vjp
def _unstack_heads(ys):
    c = ys.shape[0] // 2
    return ys[:c] + ys[c:]


_unstack_heads.defvjp(lambda ys: (_unstack_heads(ys), None), lambda _, g: (_stack_heads(g),))


def _same_head_block(c):
    ii = lax.broadcasted_iota(jnp.int32, (2 * c, 2 * c), 0)
    jj = lax.broadcasted_iota(jnp.int32, (2 * c, 2 * c), 1)
    same = (ii < c) == (jj < c)
    return same & (jj <= ii), same & (jj < ii), (ii == jj).astype(F32)


@jax.custom_vjp
def _rows_join(top, bottom):
    return jnp.concatenate([top, bottom], axis=0)


def _rows_join_bwd(n_top, g):
    return g[:n_top], g[n_top:]


_rows_join.defvjp(lambda top, bottom: (_rows_join(top, bottom), top.shape[0]), _rows_join_bwd)


def _rows_split_impl(x, n_top):
    return x[:n_top], x[n_top:]


_rows_split = jax.custom_vjp(_rows_split_impl, nondiff_argnums=(1,))
_rows_split.defvjp(lambda x, n_top: (_rows_split_impl(x, n_top), None),
                   lambda n_top, _, g: (jnp.concatenate([g[0], g[1]], axis=0),))


def _rwkv_step(s0, r, lw, k, v, a, b):
    npair, nj = len(r), len(r[0])
    c = r[0][0].shape[0]
    combos = [(j, p) for j in range(nj) for p in range(npair)]
    every = lambda fn: {q: fn(q) for q in combos}
    at_ = lambda d: (lambda q: d[q[1]][q[0]])
    r_, lw_, k_, v_, a_, b_ = (at_(z) for z in (r, lw, k, v, a, b))
    incl, strict, eye = _same_head_block(c)

    gam = every(lambda q: _cumsum_rows(lw_(q)))
    gtot = every(lambda q: jnp.sum(lw_(q), axis=0, keepdims=True))
    eneg = every(lambda q: jnp.exp(-gam[q]))
    edec = every(lambda q: jnp.exp(gtot[q] - gam[q]))
    at = every(lambda q: _stack_heads(a_(q) * jnp.exp(gam[q] - lw_(q))))
    rt = every(lambda q: _stack_heads(r_(q) * jnp.exp(gam[q])))
    bt = every(lambda q: _stack_heads(b_(q) * eneg[q]))
    kt = every(lambda q: _stack_heads(k_(q) * eneg[q]))
    bdec = every(lambda q: _stack_heads(b_(q) * edec[q]))
    kdec = every(lambda q: _stack_heads(k_(q) * edec[q]))
    vs = every(lambda q: _stack_heads(v_(q)))
    a_ab = every(lambda q: jnp.where(strict, _nt(at[q], bt[q]), 0.0))
    a_ak = every(lambda q: jnp.where(strict, _nt(at[q], kt[q]), 0.0))
    a_rb = every(lambda q: jnp.where(incl, _nt(rt[q], bt[q]), 0.0))
    a_rk = every(lambda q: jnp.where(incl, _nt(rt[q], kt[q]), 0.0))
    tinv = every(lambda q: eye + a_ab[q])
    pw = a_ab
    span = 2
    while span < c:
        pw = every(lambda q, pw=pw: _nn_x3(pw[q], pw[q]))
        tinv = every(lambda q, pw=pw, tinv=tinv: tinv[q] + _nn_x3(pw[q], tinv[q]))
        span *= 2
    akv = every(lambda q: _nn(a_ak[q], vs[q]))
    w1 = every(lambda q: _nn(tinv[q], at[q]))
    u0 = every(lambda q: _nn(tinv[q], akv[q]))
    wr = every(lambda q: _rows_join(w1[q], rt[q]))
    bk = every(lambda q: _rows_join(bdec[q], kdec[q]))
    yv = every(lambda q: _nn(a_rk[q], vs[q]))
    gdec = every(lambda q: jnp.exp(gtot[q]))

    s = list(s0)
    y = [[None] * nj for _ in range(npair)]
    for j in range(nj):
        both = {p: _rows_split(_nt(wr[(j, p)], s[p]), 2 * c) for p in range(npair)}
        u = {p: both[p][0] + u0[(j, p)] for p in range(npair)}
        for p in range(npair):
            y[p][j] = _unstack_heads(both[p][1] + _nn(a_rb[(j, p)], u[p]) + yv[(j, p)])
        s = [s[p] * gdec[(j, p)] + _tn(_rows_join(u[p], vs[(j, p)]), bk[(j, p)]) for p in range(npair)]
    return y, s


def _rwkv_blocks(ref, nj, c):
    return [[ref[j * c:(j + 1) * c, p * PAIR_W:(p + 1) * PAIR_W] for j in range(nj)] for p in range(HB_PAIRS)]


def _rwkv_fwd(seqs, comm=None):
    t = seqs[0].shape[0]
    c, nj = RWKV_CHUNK, RWKV_GROUP
    n = t // (c * nj)

    def body(r_ref, lw_ref, k_ref, v_ref, a_ref, b_ref, y_ref, hs_ref, st_ref):
        @pl.when(pl.program_id(0) == 0)
        def _():
            st_ref[...] = jnp.zeros_like(st_ref)

        hs_ref[0] = st_ref[...]
        s0 = [st_ref[p] for p in range(HB_PAIRS)]
        y, s1 = _rwkv_step(s0, *[_rwkv_blocks(ref, nj, c) for ref in (r_ref, lw_ref, k_ref, v_ref, a_ref, b_ref)])
        for p in range(HB_PAIRS):
            for j in range(nj):
                y_ref[j * c:(j + 1) * c, p * PAIR_W:(p + 1) * PAIR_W] = y[p][j]
            st_ref[p] = s1[p]

    seq = pl.BlockSpec((c * nj, W_B), lambda i: (i, 0))
    return _hosting_call(
        body, comm, name="rwkv_fwd", grid=(n,), in_specs=[seq] * 6,
        out_specs=[seq, pl.BlockSpec((1, HB_PAIRS, PAIR_W, PAIR_W), lambda i: (i, 0, 0, 0))],
        out_shape=[SDS((t, W_B), F32), SDS((n, HB_PAIRS, PAIR_W, PAIR_W), F32)],
        scratch_shapes=[pltpu.VMEM((HB_PAIRS, PAIR_W, PAIR_W), F32)], args=tuple(seqs))


def _rwkv_bwd(seqs, hs, dy, comm=None):
    t = seqs[0].shape[0]
    c, nj = RWKV_CHUNK, RWKV_GROUP
    n = t // (c * nj)

    def body(r_ref, lw_ref, k_ref, v_ref, a_ref, b_ref, hs_ref, dy_ref,
             dr_ref, dlw_ref, dk_ref, dv_ref, da_ref, db_ref, dst_ref):
        @pl.when(pl.program_id(0) == 0)
        def _():
            dst_ref[...] = jnp.zeros_like(dst_ref)

        s0 = [hs_ref[0, p] for p in range(HB_PAIRS)]
        seq_vals = [_rwkv_blocks(ref, nj, c) for ref in (r_ref, lw_ref, k_ref, v_ref, a_ref, b_ref)]
        _, vjp = jax.vjp(_rwkv_step, s0, *seq_vals)
        grads = vjp((_rwkv_blocks(dy_ref, nj, c), [dst_ref[p] for p in range(HB_PAIRS)]))
        for ref, gr in zip((dr_ref, dlw_ref, dk_ref, dv_ref, da_ref, db_ref), grads[1:]):
            for p in range(HB_PAIRS):
                for j in range(nj):
                    ref[j * c:(j + 1) * c, p * PAIR_W:(p + 1) * PAIR_W] = gr[p][j]
        m0, m1 = _head_lane_masks()
        rows0 = (lax.broadcasted_iota(jnp.int32, (PAIR_W, 1), 0) < HB_DIM).astype(F32)
        blocks = rows0 * m0 + (1.0 - rows0) * m1
        for p in range(HB_PAIRS):
            dst_ref[p] = grads[0][p] * blocks

    seq = pl.BlockSpec((c * nj, W_B), lambda i: (n - 1 - i, 0))
    return _hosting_call(
        body, comm, name="rwkv_bwd", grid=(n,),
        in_specs=[seq] * 6 + [pl.BlockSpec((1, HB_PAIRS, PAIR_W, PAIR_W), lambda i: (n - 1 - i, 0, 0, 0)), seq],
        out_specs=[seq] * 6, out_shape=[SDS((t, W_B), F32)] * 6,
        scratch_shapes=[pltpu.VMEM((HB_PAIRS, PAIR_W, PAIR_W), F32)], args=(*seqs, hs, dy))


def _final_loss(x3, fnorm, target, *, tm):
    t, d = x3.shape

    def body(x_ref, g_ref, t_ref, dx_ref, dg_ref, loss_ref):
        @pl.when(pl.program_id(0) == 0)
        def _():
            dg_ref[...] = jnp.zeros_like(dg_ref)
            loss_ref[...] = jnp.zeros_like(loss_ref)

        x, g = x_ref[...], g_ref[...]
        rinv = lax.rsqrt(jnp.mean(x * x, axis=-1, keepdims=True) + NORM_EPS)
        xh = x * rinv
        diff = xh * g - t_ref[...]
        loss_ref[...] += 0.5 * jnp.sum(jnp.mean(diff * diff, axis=-1, keepdims=True))
        dy = diff * (1.0 / d)
        dg_ref[...] += jnp.sum(dy * xh, axis=0, keepdims=True)
        dxh = dy * g
        dx_ref[...] = rinv * (dxh - xh * jnp.mean(dxh * xh, axis=-1, keepdims=True))

    row = pl.BlockSpec((tm, d), lambda i: (i, 0))
    return pl.pallas_call(
        body, name="final_loss", grid=(t // tm,), in_specs=[row, pl.BlockSpec((1, d), lambda i: (0, 0)), row],
        out_specs=[row, pl.BlockSpec((1, d), lambda i: (0, 0)), pl.BlockSpec((8, 128), lambda i: (0, 0))],
        out_shape=[SDS((t, d), F32), SDS((1, d), F32), SDS((8, 128), F32)],
        compiler_params=_params(("arbitrary",)))(x3, fnorm, target)


def _gate_up_act(h, wgt, wut, *, tm, tn, name, comm=None):
    t, d = h.shape
    tm = min(tm, t)

    def body(h_ref, g_ref, u_ref, a_out, u_out, act_out):
        hv = h_ref[...]
        a = _dg(hv, g_ref[...], 1, 1, False)
        u = _dg(hv, u_ref[...], 1, 1, False)
        a_out[...] = a.astype(a_out.dtype)
        u_out[...] = u.astype(u_out.dtype)
        act_out[...] = (_silu(a) * u).astype(act_out.dtype)

    wspec = pl.BlockSpec((tn, d), lambda i, j: (j, 0))
    ospec = pl.BlockSpec((tm, tn), lambda i, j: (i, j))
    return _hosting_call(
        body, comm, name=name, grid=(t // tm, D_FF // tn),
        in_specs=[pl.BlockSpec((tm, d), lambda i, j: (i, 0)), wspec, wspec], out_specs=[ospec, ospec, ospec],
        out_shape=[SDS((t, D_FF), BF16), SDS((t, D_FF), BF16), SDS((t, D_FF), BF16)], scratch_shapes=[],
        args=(h, wgt, wut))


def _dact_swiglu(dout, wd, a, u, *, tm, tn, name, comm=None):
    t, d = dout.shape
    tm = min(tm, t)

    def body(d_ref, w_ref, a_ref, u_ref, da_out, du_out):
        dact = 0.5 * _dg(d_ref[...], w_ref[...], 1, 1, False)
        av, uv = a_ref[...].astype(F32), u_ref[...].astype(F32)
        s = _sigmoid(av)
        da_out[...] = (dact * uv * (s * (1.0 + av * (1.0 - s)))).astype(da_out.dtype)
        du_out[...] = (dact * (av * s)).astype(du_out.dtype)

    tile = pl.BlockSpec((tm, tn), lambda i, j: (i, j))
    return _hosting_call(
        body, comm, name=name, grid=(t // tm, D_FF // tn),
        in_specs=[pl.BlockSpec((tm, d), lambda i, j: (i, 0)), pl.BlockSpec((tn, d), lambda i, j: (j, 0)), tile, tile],
        out_specs=[tile, tile], out_shape=[SDS((t, D_FF), BF16), SDS((t, D_FF), BF16)], scratch_shapes=[],
        args=(dout, wd, a, u))


class _Plan:
    def __init__(self):
        self.entries, self.counts = collections.defaultdict(list), {}

    def carry(self, host, comm_of, after):
        self.entries[host].append((comm_of, after))

    def comm(self, host, g):
        comms = [comm_of(g) for comm_of, _ in self.entries.get(host, [])]
        self.counts[host] = [len(c.arrays) for c in comms]
        return functools.reduce(_join_comms, comms) if comms else None

    def done(self, host, results, w):
        start = 0
        for (_, after), n in zip(self.entries.get(host, []), self.counts.get(host, [])):
            after(results[start:start + n], w)
            start += n


def _ffn_fwd(x, w, tag, plan, g):
    comm = plan.comm(f"{tag}_rms", g)
    res = _rowwise(_rms_f, [x], [w[f"{tag}_norm"]], [[0]], [BF16], tm=512, name=f"{tag}_rms", comm=comm)
    (h,), carried = res if comm is not None else (res, [])
    plan.done(f"{tag}_rms", carried, w)
    (a, u, act), carried = _gate_up_act(h, w[f"{tag}_wgt"], w[f"{tag}_wut"], tm=2048, tn=256, name=f"{tag}_gate_up",
                                        comm=plan.comm(f"{tag}_gate_up", g))
    plan.done(f"{tag}_gate_up", carried, w)
    comm = plan.comm(f"{tag}_down", g)
    out = _mm(act, w[f"{tag}_wd"], tm=1024, tn=D_MODEL, tk=D_FF, name=f"{tag}_down", res=x, scale=0.5, comm=comm)
    if comm is not None:
        out, carried = out
        plan.done(f"{tag}_down", carried, w)
    return out, (h, a, u, act)


def _ffn_bwd(dout, x, w, saved, tag, plan, g):
    h, a, u, act = saved

    def carrying(fn, host, *args, **kwargs):
        comm = plan.comm(host, g)
        res = fn(*args, name=host, comm=comm, **kwargs)
        out, carried = res if comm is not None else (res, [])
        plan.done(host, carried, w)
        return out

    (da, du), carried = _dact_swiglu(dout, w[f"{tag}_wd"], a, u, tm=2048, tn=256, name=f"{tag}_dact",
                                     comm=plan.comm(f"{tag}_dact", g))
    plan.done(f"{tag}_dact", carried, w)
    g[f"{tag}_wd"] = _mm(act, dout, ta=True, tm=D_FF // 2, tn=D_MODEL, tk=1024, name=f"{tag}_dwd", scale=0.5)
    g[f"{tag}_wgt"] = carrying(_mm, f"{tag}_dwg", da, h, ta=True, tm=D_FF // 2, tn=D_MODEL, tk=1024)
    g[f"{tag}_wut"] = carrying(_mm, f"{tag}_dwu", du, h, ta=True, tm=D_FF // 2, tn=D_MODEL, tk=1024)
    dh = carrying(_mm, f"{tag}_dh_g", da, w[f"{tag}_wgt"], tm=1024, tn=D_MODEL, tk=D_FF)
    dh = carrying(_mm, f"{tag}_dh_u", du, w[f"{tag}_wut"], tm=1024, tn=D_MODEL, tk=D_FF, res=dh)
    dx, g[f"{tag}_norm"] = carrying(_rowwise_bwd, f"{tag}_drms", _rms_f, [x], [w[f"{tag}_norm"]], [dh],
                                    x_grad=[True], p_grad=[True], dx_groups=[[0]], dx_dtypes=[F32], tm=512,
                                    extra={0: dout})
    return dx


def _local_step(x, target, w, plan=None):
    plan = plan or _Plan()
    ones_bd = jnp.kron(jnp.eye(HB_HEADS, dtype=F32), jnp.ones((HB_DIM, HB_DIM), F32))
    g = {}
    x1, ffn1_saved = _ffn_fwd(x, w, "ffn1", plan, g)
    hm, = _rowwise(_rms_f, [x1], [w["mix_norm"]], [[0]], [BF16], tm=512, name="mix_rms")
    p_h = _mm(hm, w["w_in_h"], tm=2048, tn=256, tk=D_MODEL, name="inproj_h")
    p_r = _mm(hm, w["w_in_r"], tm=2048, tn=256, tk=D_MODEL, name="inproj_r")
    o_a, hgrn_states = _hgrn_fwd(p_h, w["lb0"], w["lb1"], w["hgrn_out_norm"])

    mu = w["mu_pad"]
    prep_xs = [(p_r, W_B, 0), (p_r, W_B, 1), (p_r, W_B, 2), (p_r, LORA_PAD, 6),
               ("prev", p_r, W_B, 0), ("prev", p_r, W_B, 1), ("prev", p_r, W_B, 2), ("prev", p_r, LORA_PAD, 6)]
    prep_ps = [(mu, W_B, 0), (mu, W_B, 1), (mu, W_B, 2), (mu, LORA_PAD, 6), w["rwkv_w0"], w["w2_pad"], w["rwkv_a0"],
               w["a2_pad"], w["g2_pad"], w["rwkv_k_k"], w["rwkv_k_a"], ones_bd]
    prep_f = _rwkv_prep_f
    r, lw, k2, v, a_vec, b_vec, gate = _rowwise(prep_f, prep_xs, prep_ps, [[0], [1], [2], [3], [4], [5], [6]],
                                                [F32] * 7, tm=256, name="rwkv_prep")
    seqs = [r, lw, k2, v, a_vec, b_vec]
    (y, rwkv_states), carried = _rwkv_fwd(seqs, comm=plan.comm("rwkv_fwd", g))
    plan.done("rwkv_fwd", carried, w)
    post_f = _rwkv_post_f
    post_xs = [y, r, k2, v, gate]
    post_ps = [w["rwkv_r_k"], w["rwkv_gn_w"], w["rwkv_gn_b"], ones_bd]
    o, = _rowwise(lambda o_a_, *rest: (o_a_,) + tuple(post_f(*rest)), [o_a] + post_xs, post_ps, [[0, 1]], [F32],
                  tm=256, name="rwkv_post")
    x2 = _mm(o, w["w_out"], tm=2048, tn=256, tk=D_MODEL, name="outproj", res=x1)
    x3, ffn2_saved = _ffn_fwd(x2, w, "ffn2", plan, g)
    dx3, g["final_norm"], loss = _final_loss(x3, w["final_norm"], target, tm=256)

    dx2 = _ffn_bwd(dx3, x2, w, ffn2_saved, "ffn2", plan, g)
    do = _mm(dx2, w["w_out"], tb=True, tm=2048, tn=256, tk=D_MODEL, name="outproj_do")
    g["w_out"] = _mm(o, dx2, ta=True, tm=D_MODEL, tn=D_MODEL, tk=1024, name="outproj_dw")

    (dp_h, g["lb0"], g["lb1"], g["hgrn_out_norm"]), carried = _hgrn_bwd(
        p_h, w["lb0"], w["lb1"], w["hgrn_out_norm"], hgrn_states, do, 0, comm=plan.comm("hgrn_bwd", g))
    plan.done("hgrn_bwd", carried, w)
    post_out = _rowwise_bwd(post_f, post_xs, post_ps, [(do, W_B, 1)], x_grad=[True] * 5, p_grad=[True] * 3 + [False],
                            dx_groups=[[0], [1], [2], [3], [4]], dx_dtypes=[F32] * 5, tm=256, name="rwkv_post_bwd")
    dy, dr1, dk1, dv1, dgate, g["rwkv_r_k"], g["rwkv_gn_w"], g["rwkv_gn_b"] = post_out
    (dr2, dlw, dk2, dv2, da_vec, db_vec), carried = _rwkv_bwd(seqs, rwkv_states, dy, comm=plan.comm("rwkv_bwd", g))
    plan.done("rwkv_bwd", carried, w)

    def prep2_f(*vals):
        r_, lw_, k2_, v_, a_, b_, g_ = prep_f(*vals)
        return r_, lw_, k2_, v_, a_, b_, g_, r_, k2_, v_

    prep_comm = plan.comm("rwkv_prep_bwd", g)
    prep_out = _rowwise_bwd(prep2_f, prep_xs, prep_ps, [dr2, dlw, dk2, dv2, da_vec, db_vec, dgate, dr1, dk1, dv1],
                            x_grad=[True] * 8, p_grad=[True] * 11 + [False], dx_groups=[[0, 1, 2, 3], [4, 5, 6, 7]],
                            dx_dtypes=[F32], tm=256, name="rwkv_prep_bwd", fold_next=(0, 1), comm=prep_comm)
    prep_out, carried = prep_out if prep_comm is not None else (prep_out, [])
    plan.done("rwkv_prep_bwd", carried, w)
    dp_r = prep_out[0]
    (dmu_r, dmu_k, dmu_v, dmu_lo, g["rwkv_w0"], g["w2_pad"], g["rwkv_a0"], g["a2_pad"], g["g2_pad"],
     g["rwkv_k_k"], g["rwkv_k_a"]) = prep_out[1:]
    g["mu_pad"] = jnp.concatenate([dmu_r, dmu_k, dmu_v, dmu_lo], axis=1)
    dhm = _mm(dp_h, w["w_in_h"], tb=True, tm=1024, tn=D_MODEL, tk=N_HGRN_COLS, name="inproj_dh_h")
    dhm = _mm(dp_r, w["w_in_r"], tb=True, tm=1024, tn=D_MODEL, tk=N_RWKV_PAD, name="inproj_dh_r", res=dhm)
    g["w_in_h"] = _mm(hm, dp_h, ta=True, tm=D_MODEL, tn=D_MODEL, tk=1024, name="inproj_dw_h")
    g["w_in_r"] = _mm(hm, dp_r, ta=True, tm=D_MODEL, tn=N_RWKV_PAD // 2, tk=1024, name="inproj_dw_r")
    mix_comm = plan.comm("mix_drms", g)
    mix_out = _rowwise_bwd(_rms_f, [x1], [w["mix_norm"]], [dhm], x_grad=[True], p_grad=[True], dx_groups=[[0]],
                           dx_dtypes=[F32], tm=512, name="mix_drms", extra={0: dx2}, comm=mix_comm)
    (dx1, g["mix_norm"]), carried = mix_out if mix_comm is not None else (mix_out, [])
    plan.done("mix_drms", carried, w)
    dx0 = _ffn_bwd(dx1, x, w, ffn1_saved, "ffn1", plan, g)
    return loss, dx0, g


HBM_SPEC = pl.BlockSpec(memory_space=pl.ANY)

Comm = collections.namedtuple("Comm", "arrays out_shapes aliased sem_shapes start finish")


def _join_comms(first, second):
    n, s = len(first.arrays), len(first.sem_shapes)

    def start(ins, outs, sems):
        first.start(ins[:n], outs[:n], sems[:s])
        second.start(ins[n:], outs[n:], sems[s:])

    def finish(ins, outs, sems):
        first.finish(ins[:n], outs[:n], sems[:s])
        second.finish(ins[n:], outs[n:], sems[s:])

    return Comm(list(first.arrays) + list(second.arrays), list(first.out_shapes) + list(second.out_shapes),
                list(first.aliased) + list(second.aliased), list(first.sem_shapes) + list(second.sem_shapes),
                start, finish)


def _run_comm(comm, name):
    n = len(comm.arrays)

    def body(*refs):
        ins, outs, sems = refs[:n], refs[n:2 * n], refs[2 * n:]
        comm.start(ins, outs, sems)
        comm.finish(ins, outs, sems)

    return pl.pallas_call(
        body, name=name, in_specs=[HBM_SPEC] * n, out_specs=[HBM_SPEC] * n, out_shape=list(comm.out_shapes),
        input_output_aliases={t: t for t in range(n) if comm.aliased[t]},
        scratch_shapes=list(comm.sem_shapes))(*comm.arrays)


def _hosting_call(body, comm, *, name, grid, in_specs, out_specs, out_shape, scratch_shapes, args):
    sem = ("arbitrary",) * len(grid)
    if comm is None:
        res = pl.pallas_call(body, name=name, grid=grid, in_specs=in_specs, out_specs=out_specs, out_shape=out_shape,
                             scratch_shapes=scratch_shapes, compiler_params=_params(sem))(*args)
        return list(res), []
    ni, no, ns, nc = len(in_specs), len(out_specs), len(scratch_shapes), len(comm.arrays)

    def wrapped(*refs):
        ins, cins = refs[:ni], refs[ni:ni + nc]
        outs, couts = refs[ni + nc:ni + nc + no], refs[ni + nc + no:ni + 2 * nc + no]
        scr, sems = refs[ni + 2 * nc + no:ni + 2 * nc + no + ns], refs[ni + 2 * nc + no + ns:]
        first = functools.reduce(jnp.logical_and, [pl.program_id(k) == 0 for k in range(len(grid))])
        last = functools.reduce(jnp.logical_and, [pl.program_id(k) == grid[k] - 1 for k in range(len(grid))])

        @pl.when(first)
        def _():
            comm.start(cins, couts, sems)

        body(*ins, *outs, *scr)

        @pl.when(last)
        def _():
            comm.finish(cins, couts, sems)

    res = pl.pallas_call(
        wrapped, name=name, grid=grid, in_specs=list(in_specs) + [HBM_SPEC] * nc,
        out_specs=list(out_specs) + [HBM_SPEC] * nc, out_shape=list(out_shape) + list(comm.out_shapes),
        scratch_shapes=list(scratch_shapes) + list(comm.sem_shapes),
        input_output_aliases={ni + t: no + t for t in range(nc) if comm.aliased[t]},
        compiler_params=_params(sem))(*args, *comm.arrays)
    return list(res[:no]), list(res[no:])


def _chips(x, y):
    return [(1 - x, y), (x, 1 - y), (1 - x, 1 - y)]


def _gather_comm(bufs):
    n = len(bufs)

    def copies(outs, sems):
        ici_send, ici_recv, d2d_send, d2d_recv = sems
        x, y, c = lax.axis_index("x"), lax.axis_index("y"), lax.axis_index("c")

        def half(t, slot, hc):
            hr = bufs[t].shape[1] // 2
            return outs[t].at[slot, pl.ds(pl.multiple_of(hc * hr, 16), hr), :]

        def ici(t, j, slot, px, py):
            return pltpu.make_async_remote_copy(src_ref=half(t, slot, c), dst_ref=half(t, slot, c),
                                                send_sem=ici_send.at[3 * t + j], recv_sem=ici_recv.at[3 * t + j],
                                                device_id=(px, py, c), device_id_type=MESH)

        def d2d(t, j, slot, hc):
            return pltpu.make_async_remote_copy(src_ref=half(t, slot, hc), dst_ref=half(t, slot, hc),
                                                send_sem=d2d_send.at[3 * t + j], recv_sem=d2d_recv.at[3 * t + j],
                                                device_id=(x, y, 1 - c), device_id_type=MESH)

        peers = [(t, j, px, py) for t in range(n) for j, (px, py) in enumerate(_chips(x, y))]
        return ici, d2d, peers, 2 * x + y, c

    def start(ins, outs, sems):
        ici, _, peers, me, _ = copies(outs, sems)
        for t, j, px, py in peers:
            ici(t, j, me, px, py).start()

    def finish(ins, outs, sems):
        ici, d2d, peers, me, c = copies(outs, sems)
        for t, j, px, py in peers:
            ici(t, j, 2 * px + py, px, py).wait_recv()
            d2d(t, j, 2 * px + py, c).start()
        for t, j, px, py in peers:
            d2d(t, j, 2 * px + py, 1 - c).wait_recv()
        for t, j, px, py in peers:
            ici(t, j, me, px, py).wait_send()
            d2d(t, j, 2 * px + py, c).wait_send()

    return Comm(list(bufs), [SDS(b.shape, b.dtype) for b in bufs], [True] * n,
                [pltpu.SemaphoreType.DMA((3 * n,))] * 4, start, finish)


def _sibling_exchange_comm(gs):
    n = len(gs)

    def copies(ins, outs, sems):
        x, y, c = lax.axis_index("x"), lax.axis_index("y"), lax.axis_index("c")
        cps = []
        for t in range(n):
            hr = gs[t].shape[1] // 2
            src = ins[t].at[:, pl.ds(pl.multiple_of((1 - c) * hr, SUBLANES), hr), :]
            cps.append(pltpu.make_async_remote_copy(src_ref=src, dst_ref=outs[t], send_sem=sems[0].at[t],
                                                    recv_sem=sems[1].at[t], device_id=(x, y, 1 - c),
                                                    device_id_type=MESH))
        return cps

    def start(ins, outs, sems):
        for cp in copies(ins, outs, sems):
            cp.start()

    def finish(ins, outs, sems):
        for cp in copies(ins, outs, sems):
            cp.wait()

    return Comm(list(gs), [SDS((N_CHIPS, g.shape[1] // 2, g.shape[2]), g.dtype) for g in gs], [False] * n,
                [pltpu.SemaphoreType.DMA((n,))] * 2, start, finish)


def _own_rows(c, rows):
    return pl.ds(pl.multiple_of(c * (rows // 2), 16), rows // 2)


def _chip_exchange_comm(ss):
    n = len(ss)

    def copies(ins, outs, sems):
        x, y, c = lax.axis_index("x"), lax.axis_index("y"), lax.axis_index("c")
        me = 2 * x + y

        def copy(t, j, px, py, src_slot, dst_slot):
            rows = _own_rows(c, ss[t].shape[1])
            return pltpu.make_async_remote_copy(src_ref=ins[t].at[src_slot, rows, :],
                                                dst_ref=outs[t].at[dst_slot, rows, :],
                                                send_sem=sems[0].at[3 * t + j], recv_sem=sems[1].at[3 * t + j],
                                                device_id=(px, py, c), device_id_type=MESH)

        peers = [(t, j, px, py) for t in range(n) for j, (px, py) in enumerate(_chips(x, y))]
        return copy, peers, me

    def start(ins, outs, sems):
        copy, peers, me = copies(ins, outs, sems)
        for t, j, px, py in peers:
            copy(t, j, px, py, 2 * px + py, me).start()

    def finish(ins, outs, sems):
        copy, peers, me = copies(ins, outs, sems)
        for t, j, px, py in peers:
            copy(t, j, px, py, me, 2 * px + py).wait_recv()
        for t, j, px, py in peers:
            copy(t, j, px, py, 2 * px + py, me).wait_send()

    return Comm(list(ss), [SDS(s.shape, s.dtype) for s in ss], [False] * n,
                [pltpu.SemaphoreType.DMA((3 * n,))] * 2, start, finish)


def _sibling_swap_comm(rs, ss):
    n = len(rs)

    def copies(outs, sems):
        x, y, c = lax.axis_index("x"), lax.axis_index("y"), lax.axis_index("c")
        cps = []
        for t in range(n):
            rows = _own_rows(c, rs[t].shape[1])
            held = [outs[t].at[2 * px + py, rows, :] for px, py in _chips(x, y)] + [outs[n + t].at[2 * x + y, rows, :]]
            cps += [pltpu.make_async_remote_copy(src_ref=ref, dst_ref=ref, send_sem=sems[0].at[4 * t + j],
                                                 recv_sem=sems[1].at[4 * t + j], device_id=(x, y, 1 - c),
                                                 device_id_type=MESH) for j, ref in enumerate(held)]
        return cps

    def start(ins, outs, sems):
        for cp in copies(outs, sems):
            cp.start()

    def finish(ins, outs, sems):
        for cp in copies(outs, sems):
            cp.wait()

    both = list(rs) + list(ss)
    return Comm(both, [SDS(b.shape, b.dtype) for b in both], [True] * (2 * n),
                [pltpu.SemaphoreType.DMA((4 * n,))] * 2, start, finish)


def _row_tile(rows, cap=512):
    best = SUBLANES
    for tr in range(SUBLANES, min(rows, cap) + 1, SUBLANES):
        if rows % tr == 0:
            best = tr
    return best


def _add_halves(g4, r4, c_idx, name):
    _, hr, lanes = r4.shape
    tr = _row_tile(hr)
    nb = hr // tr

    def body(c_ref, a_ref, b_ref, o_ref):
        o_ref[...] = (a_ref[...] + b_ref[...]).astype(o_ref.dtype)

    owned = pl.BlockSpec((None, tr, lanes), lambda q, i, c_ref: (q, c_ref[0] * nb + i, 0))
    grid_spec = pltpu.PrefetchScalarGridSpec(
        num_scalar_prefetch=1, grid=(N_CHIPS, nb),
        in_specs=[owned, pl.BlockSpec((None, tr, lanes), lambda q, i, c_ref: (q, i, 0))], out_specs=owned)
    return pl.pallas_call(body, name=name, grid_spec=grid_spec, out_shape=SDS(g4.shape, BF16),
                          compiler_params=_params(("parallel", "parallel")))(c_idx, g4, r4)


def _adamw(wf, r4, s4, mf, vf, me_idx, name):
    rows, lanes = wf.shape
    tr = rows // 2
    assert tr % 16 == 0
    c1 = 1.0 / (1.0 - ADAM_B1 ** ADAM_STEP)
    c2 = 1.0 / (1.0 - ADAM_B2 ** ADAM_STEP)

    def body(me_ref, w_ref, a_ref, b_ref, c_ref, d_ref, own_ref, m_ref, v_ref, g_ref, delta_ref, nm_ref, nv_ref):
        own = own_ref[...].astype(F32)
        p = [jnp.where(me_ref[0] == q, own, ref[...].astype(F32)) for q, ref in enumerate((a_ref, b_ref, c_ref, d_ref))]
        gv = ((p[0] + p[1]) + p[2]) + p[3]
        m = ADAM_B1 * m_ref[...] + (1.0 - ADAM_B1) * gv
        v = ADAM_B2 * v_ref[...] + (1.0 - ADAM_B2) * (gv * gv)
        g_ref[...] = gv
        delta_ref[...] = -ADAM_LR * ((m * c1) / (jnp.sqrt(v * c2) + ADAM_EPS) + ADAM_WD * w_ref[...])
        nm_ref[...] = m
        nv_ref[...] = v

    other = lambda q: (lambda i, me_ref: (jnp.where(me_ref[0] == q, (q + 1) % N_CHIPS, q), i, 0))
    full = pl.BlockSpec((tr, lanes), lambda i, me_ref: (i, 0))
    grid_spec = pltpu.PrefetchScalarGridSpec(
        num_scalar_prefetch=1, grid=(rows // tr,),
        in_specs=[full] + [pl.BlockSpec((None, tr, lanes), other(q)) for q in range(N_CHIPS)]
        + [pl.BlockSpec((None, tr, lanes), lambda i, me_ref: (me_ref[0], i, 0)), full, full],
        out_specs=[full] * 4)
    return pl.pallas_call(body, name=name, grid_spec=grid_spec, out_shape=[SDS((rows, lanes), F32)] * 4,
                          compiler_params=_params(("parallel",)))(me_idx, wf, r4, r4, r4, r4, s4, mf, vf)


BIG = ("ffn1_w_gate", "ffn1_w_up", "ffn1_w_down", "ffn2_w_gate", "ffn2_w_up", "ffn2_w_down", "w_out", "w_in")
TRANSPOSED = ("ffn1_w_gate", "ffn1_w_up", "ffn2_w_gate", "ffn2_w_up")
PACKED = ("rwkv_w2", "rwkv_a2", "rwkv_g2")
SMALL_SHAPES = {"ffn1_norm": (1, D_MODEL), "mix_norm": (1, D_MODEL), "hgrn_lb_logits": (2, W_A),
                "hgrn_out_norm": (1, W_A), "rwkv_shift_mu": (1, N_RWKV_COLS), "rwkv_w0": (1, W_B),
                "rwkv_a0": (1, W_B), "rwkv_k_k": (1, W_B), "rwkv_k_a": (1, W_B),
                "rwkv_r_k": (1, HB_HEADS, HB_DIM), "rwkv_gn_w": (1, W_B), "rwkv_gn_b": (1, W_B),
                "ffn2_norm": (1, D_MODEL), "final_norm": (D_MODEL,)}
PACK_ELEMS = sum(_numel(_shard_shape(n)) for n in PACKED) + sum(_numel(SMALL_SHAPES[n]) for n in SMALL)
PACK_ROWS = -(-PACK_ELEMS // (32 * LANES)) * 32


def _to_rows(name, shard):
    return shard[0].T if name in TRANSPOSED else shard[0]


def _from_rows(name, rows):
    return (rows.T if name in TRANSPOSED else rows)[None]


def _pack(sharded, small):
    flat = jnp.concatenate([sharded[n].reshape(-1) for n in PACKED] + [small[n].reshape(-1) for n in SMALL])
    return jnp.pad(flat, (0, PACK_ROWS * LANES - flat.shape[0])).reshape(PACK_ROWS, LANES)


def _unpack(packed):
    flat, out, off = packed.reshape(-1), {}, 0
    for n in PACKED:
        shp = _shard_shape(n)
        out[n] = flat[off:off + _numel(shp)].reshape((1,) + shp)
        off += _numel(shp)
    for n in SMALL:
        shp = SMALL_SHAPES[n]
        out[n] = flat[off:off + _numel(shp)].reshape(shp)
        off += _numel(shp)
    return out


def _quarter(full, name, q):
    shape, ax = SHARDED_SHAPES[name]
    w = shape[ax] // N_CHIPS
    return lax.slice_in_dim(full, q * w, (q + 1) * w, axis=ax)


def kernel(x, ffn1_norm, ffn1_w_gate, ffn1_w_up, ffn1_w_down, mix_norm, w_in, hgrn_lb_logits, hgrn_out_norm, rwkv_shift_mu, rwkv_w0, rwkv_w2, rwkv_a0, rwkv_a2, rwkv_g2, rwkv_k_k, rwkv_k_a, rwkv_r_k, rwkv_gn_w, rwkv_gn_b, w_out, ffn2_norm, ffn2_w_gate, ffn2_w_up, ffn2_w_down, final_norm, loss_target, m_ffn1_norm, m_ffn1_w_gate, m_ffn1_w_up, m_ffn1_w_down, m_mix_norm, m_w_in, m_hgrn_lb_logits, m_hgrn_out_norm, m_rwkv_shift_mu, m_rwkv_w0, m_rwkv_w2, m_rwkv_a0, m_rwkv_a2, m_rwkv_g2, m_rwkv_k_k, m_rwkv_k_a, m_rwkv_r_k, m_rwkv_gn_w, m_rwkv_gn_b, m_w_out, m_ffn2_norm, m_ffn2_w_gate, m_ffn2_w_up, m_ffn2_w_down, m_final_norm, v_ffn1_norm, v_ffn1_w_gate, v_ffn1_w_up, v_ffn1_w_down, v_mix_norm, v_w_in, v_hgrn_lb_logits, v_hgrn_out_norm, v_rwkv_shift_mu, v_rwkv_w0, v_rwkv_w2, v_rwkv_a0, v_rwkv_a2, v_rwkv_g2, v_rwkv_k_k, v_rwkv_k_a, v_rwkv_r_k, v_rwkv_gn_w, v_rwkv_gn_b, v_w_out, v_ffn2_norm, v_ffn2_w_gate, v_ffn2_w_up, v_ffn2_w_down, v_final_norm):
    args = dict(locals())
    wts = {n: args[n] for n in ALL_WEIGHTS}
    moms = {n: args["m_" + n] for n in ALL_WEIGHTS}
    vars_ = {n: args["v_" + n] for n in ALL_WEIGHTS}

    me = 2 * lax.axis_index("x") + lax.axis_index("y")
    c_idx = lax.axis_index("c").astype(jnp.int32).reshape(1)
    me_idx = me.astype(jnp.int32).reshape(1)
    shard_of = {n: _to_rows(n, wts[n]).astype(BF16) for n in BIG}
    shard_of["packed"] = _pack(wts, {n: wts[n] for n in SMALL}).astype(BF16)
    group = {"ffn1": BIG[0:3], "ffn2": BIG[3:6]}

    def slot_bufs(names):
        return [lax.dynamic_update_slice(lax.empty((N_CHIPS,) + shard_of[n].shape, BF16), shard_of[n][None],
                                         (me, 0, 0)) for n in names]

    def ffn_weights(tag, gathered):
        return {f"{tag}_wgt": gathered[0].reshape(D_FF, D_MODEL), f"{tag}_wut": gathered[1].reshape(D_FF, D_MODEL),
                f"{tag}_wd": gathered[2].reshape(D_FF, D_MODEL)}

    def w_in_weights(gathered):
        w_in_full = jnp.concatenate([gathered[0][q] for q in range(N_CHIPS)], axis=1)
        return {"w_in_h": w_in_full[:, :N_HGRN_COLS],
                "w_in_r": jnp.pad(w_in_full[:, N_HGRN_COLS:], ((0, 0), (0, N_RWKV_PAD - N_RWKV_COLS)))}

    def mixer_weights(gathered):
        w_out_full = gathered[0].reshape(D_MODEL, D_MODEL)
        packs = gathered[1].reshape(N_CHIPS, PACK_ROWS * LANES)
        full, off = {}, 0
        for n in PACKED:
            shp = _shard_shape(n)
            full[n] = jnp.concatenate([packs[q, off:off + _numel(shp)].reshape(shp) for q in range(N_CHIPS)], axis=1)
            off += _numel(shp)
        zrow = lambda nrow: jnp.zeros((nrow, W_B), BF16)
        return {"w_out": w_out_full,
                "w2_pad": jnp.concatenate([full["rwkv_w2"], zrow(LORA_PAD - 32)], axis=0),
                "a2_pad": jnp.concatenate([zrow(32), full["rwkv_a2"], zrow(LORA_PAD - 64)], axis=0),
                "g2_pad": jnp.concatenate([zrow(64), full["rwkv_g2"], zrow(LORA_PAD - 160)], axis=0)}

    plan = _Plan()
    w = {}
    plan.carry("ffn1_rms", lambda g: _gather_comm(slot_bufs(group["ffn1"][:2])),
               lambda res, w_: w_.update({"ffn1_wgt": res[0].reshape(D_FF, D_MODEL),
                                          "ffn1_wut": res[1].reshape(D_FF, D_MODEL)}))

    def after_gate_up(res, w_):
        w_["ffn1_wd"] = res[0].reshape(D_FF, D_MODEL)
        w_.update(w_in_weights(res[1:]))

    plan.carry("ffn1_gate_up", lambda g: _gather_comm(slot_bufs(("ffn1_w_down", "w_in"))), after_gate_up)
    plan.carry("ffn1_down", lambda g: _gather_comm(slot_bufs(("w_out", "packed"))),
               lambda res, w_: w_.update(mixer_weights(res)))
    plan.carry("rwkv_fwd", lambda g: _gather_comm(slot_bufs(group["ffn2"])),
               lambda res, w_: w_.update(ffn_weights("ffn2", res)))
    w["ffn1_norm"], w["ffn2_norm"] = ffn1_norm, ffn2_norm
    w["mix_norm"] = mix_norm
    w["lb0"], w["lb1"] = hgrn_lb_logits[0:1], hgrn_lb_logits[1:2]
    w["hgrn_out_norm"] = hgrn_out_norm
    w["mu_pad"] = jnp.pad(rwkv_shift_mu, ((0, 0), (0, N_RWKV_PAD - N_RWKV_COLS)))
    for n in ("rwkv_w0", "rwkv_a0", "rwkv_k_k", "rwkv_k_a", "rwkv_gn_w", "rwkv_gn_b"):
        w[n] = wts[n]
    w["rwkv_r_k"] = rwkv_r_k.reshape(1, W_B)
    w["final_norm"] = final_norm.reshape(1, D_MODEL)

    def reduce_rows(names, gs):
        r1 = _run_comm(_sibling_exchange_comm(gs), "grad_sibling_exchange")
        s4 = [_add_halves(gt, rt, c_idx, f"grad_add_halves_{n}") for gt, rt, n in zip(gs, r1, names)]
        return list(zip(_run_comm(_chip_exchange_comm(s4), "grad_chip_exchange"), s4))

    def swap_comm(names):
        return _sibling_swap_comm([early[n][0] for n in names], [early[n][1] for n in names])

    def after_swap(names):
        return lambda res, w_: swapped.update(zip(names, zip(res[:len(names)], res[len(names):])))

    early, swapped = {}, {}

    def reduce_early(names, grads_of, sibling_host, chips_host, swap_host):
        def sibling_comm(g):
            early[names, "gs"] = grads_of(g)
            return _sibling_exchange_comm(early[names, "gs"])

        def after_sibling(res, w_):
            early[names, "s4"] = [_add_halves(gt, rt, c_idx, f"grad_add_halves_{n}")
                                  for gt, rt, n in zip(early[names, "gs"], res, names)]

        plan.carry(sibling_host, sibling_comm, after_sibling)
        plan.carry(chips_host, lambda g: _chip_exchange_comm(early[names, "s4"]),
                   lambda res, w_: early.update(zip(names, zip(res, early[names, "s4"]))))
        if swap_host:
            plan.carry(swap_host, lambda g: swap_comm(names), after_swap(names))

    def proj_grads(g):
        g_w_in = jnp.concatenate([g["w_in_h"], g["w_in_r"][:, :N_RWKV_COLS]], axis=1)
        return [g["w_out"].reshape(N_CHIPS, -1, D_MODEL),
                jnp.stack([_quarter(g_w_in, "w_in", q) for q in range(N_CHIPS)])]

    rows_of = lambda keys: (lambda g: [g[k].reshape(N_CHIPS, -1, D_MODEL) for k in keys])
    reduce_early(group["ffn2"], rows_of(("ffn2_wgt", "ffn2_wut", "ffn2_wd")), "hgrn_bwd", "rwkv_bwd", "rwkv_prep_bwd")
    reduce_early(("w_out", "w_in"), proj_grads, "mix_drms", "ffn1_dact", "ffn1_dwg")
    reduce_early(("ffn1_w_down",), rows_of(("ffn1_wd",)), "ffn1_dwg", "ffn1_dwu", "ffn1_dh_g")
    reduce_early(("ffn1_w_gate",), rows_of(("ffn1_wgt",)), "ffn1_dwu", "ffn1_dh_g", "ffn1_dh_u")
    reduce_early(("ffn1_w_up",), rows_of(("ffn1_wut",)), "ffn1_dh_g", "ffn1_dh_u", None)
    loss_slab, grad_x, g = _local_step(x[0], loss_target[0], w, plan)
    loss = lax.psum(loss_slab[0, 0], ("x", "y", "c"))

    gfull = {
        "rwkv_w2": g["w2_pad"][0:32], "rwkv_a2": g["a2_pad"][32:64], "rwkv_g2": g["g2_pad"][64:160],
    }
    gsmall = {
        "ffn1_norm": g["ffn1_norm"], "mix_norm": g["mix_norm"],
        "hgrn_lb_logits": jnp.concatenate([g["lb0"], g["lb1"]], axis=0), "hgrn_out_norm": g["hgrn_out_norm"],
        "rwkv_shift_mu": g["mu_pad"][:, :N_RWKV_COLS], "rwkv_w0": g["rwkv_w0"], "rwkv_a0": g["rwkv_a0"],
        "rwkv_k_k": g["rwkv_k_k"], "rwkv_k_a": g["rwkv_k_a"], "rwkv_r_k": g["rwkv_r_k"],
        "rwkv_gn_w": g["rwkv_gn_w"], "rwkv_gn_b": g["rwkv_gn_b"], "ffn2_norm": g["ffn2_norm"],
        "final_norm": g["final_norm"],
    }
    packed = jnp.stack([_pack({n: _quarter(gfull[n], n, q) for n in PACKED}, gsmall) for q in range(N_CHIPS)])
    early["packed"], = reduce_rows(["packed"], [packed])
    last = ["ffn1_w_up", "packed"]
    after_swap(last)(_run_comm(swap_comm(last), "grad_sibling_swap"), w)
    names = list(BIG) + ["packed"]

    def rows_list(d):
        return [_to_rows(n, d[n]) for n in BIG] + [_pack(d, {n: d[n] for n in SMALL})]

    outs = [_adamw(wt, *swapped[n], mt, vt, me_idx, f"adamw_{n}")
            for wt, mt, vt, n in zip(rows_list(wts), rows_list(moms), rows_list(vars_), names)]
    results = []
    for k in range(4):
        per = [outs[i][k] for i in range(len(names))]
        d = {n: _from_rows(n, z) for n, z in zip(BIG, per[:-1])}
        d.update(_unpack(per[-1]))
        results.append(d)
    return (loss, grad_x[None], *[r[n] for r in results for n in ALL_WEIGHTS])
```

```python
import collections
import functools

import jax
import jax.numpy as jnp
from jax import lax
from jax.experimental import pallas as pl
from jax.experimental.pallas import tpu as pltpu

F32 = jnp.float32
BF16 = jnp.bfloat16
SDS = jax.ShapeDtypeStruct
MESH = pl.DeviceIdType.MESH

D_MODEL = 1024
D_FF = 2816
W_A = 512
W_B = 512
HA_HEADS, HA_DIM = 4, 128
HB_HEADS, HB_DIM = 8, 64
HGRN_CHUNK = 64
HGRN_GROUP = 8
RWKV_CHUNK = 16
RWKV_GROUP = 8
N_HGRN_COLS = 4 * W_A
N_RWKV_COLS = 3 * W_B + 32 + 32 + 96
N_RWKV_PAD = 1792
LORA_PAD = 256
NORM_EPS = 1e-6
RWKV_GN_EPS = 64e-5
L2_EPS = 1e-12
ADAM_LR, ADAM_B1, ADAM_B2, ADAM_EPS, ADAM_WD, ADAM_STEP = 0.001, 0.9, 0.999, 1e-8, 0.01, 10

N_CHIPS = 4
VMEM_LIMIT_V7X = 56 * 1024 * 1024
LANES = 1024

SHARDED_SHAPES = {
    "ffn1_w_gate": ((D_MODEL, D_FF), 1), "ffn1_w_up": ((D_MODEL, D_FF), 1), "ffn1_w_down": ((D_FF, D_MODEL), 0),
    "w_in": ((D_MODEL, N_HGRN_COLS + N_RWKV_COLS), 1), "rwkv_w2": ((32, W_B), 1), "rwkv_a2": ((32, W_B), 1),
    "rwkv_g2": ((96, W_B), 1), "w_out": ((D_MODEL, D_MODEL), 0),
    "ffn2_w_gate": ((D_MODEL, D_FF), 1), "ffn2_w_up": ((D_MODEL, D_FF), 1), "ffn2_w_down": ((D_FF, D_MODEL), 0),
}
SMALL = ("ffn1_norm", "mix_norm", "hgrn_lb_logits", "hgrn_out_norm", "rwkv_shift_mu", "rwkv_w0", "rwkv_a0",
         "rwkv_k_k", "rwkv_k_a", "rwkv_r_k", "rwkv_gn_w", "rwkv_gn_b", "ffn2_norm", "final_norm")
ALL_WEIGHTS = ("ffn1_norm", "ffn1_w_gate", "ffn1_w_up", "ffn1_w_down", "mix_norm", "w_in", "hgrn_lb_logits",
               "hgrn_out_norm", "rwkv_shift_mu", "rwkv_w0", "rwkv_w2", "rwkv_a0", "rwkv_a2", "rwkv_g2", "rwkv_k_k",
               "rwkv_k_a", "rwkv_r_k", "rwkv_gn_w", "rwkv_gn_b", "w_out", "ffn2_norm", "ffn2_w_gate", "ffn2_w_up",
               "ffn2_w_down", "final_norm")


def _shard_shape(name):
    shape, ax = SHARDED_SHAPES[name]
    return tuple(s // N_CHIPS if i == ax else s for i, s in enumerate(shape))


def _numel(shape):
    n = 1
    for s in shape:
        n *= s
    return n


def _params(sem=None):
    return pltpu.CompilerParams(dimension_semantics=sem, vmem_limit_bytes=VMEM_LIMIT_V7X)


def _split2(x):
    hi = x.astype(BF16)
    return hi, (x.astype(F32) - hi.astype(F32)).astype(BF16)


def _dg(x, y, cx, cy, hi):
    dn = (((cx,), (cy,)), ((), ()))
    dot = lambda p, q: lax.dot_general(p, q, dn, preferred_element_type=F32)
    if hi == "x3":
        (xh, xl), (yh, yl) = _split2(x), _split2(y)
        return dot(xh, yh) + (dot(xh, yl) + dot(xl, yh))
    return dot(x.astype(BF16), y.astype(BF16))


def _make_mm(hi, cotangent_forms=None):
    @jax.custom_vjp
    def nn(x, y):
        return _dg(x, y, 1, 0, hi)

    @jax.custom_vjp
    def nt(x, y):
        return _dg(x, y, 1, 1, hi)

    @jax.custom_vjp
    def tn(x, y):
        return _dg(x, y, 0, 0, hi)

    bnn, bnt, btn = cotangent_forms or (nn, nt, tn)
    nn.defvjp(lambda x, y: (nn(x, y), (x, y)), lambda r, g: (bnt(g, r[1]), btn(r[0], g)))
    nt.defvjp(lambda x, y: (nt(x, y), (x, y)), lambda r, g: (bnn(g, r[1]), btn(g, r[0])))
    tn.defvjp(lambda x, y: (tn(x, y), (x, y)), lambda r, g: (bnt(r[1], g), bnn(r[0], g)))
    return nn, nt, tn


_nn, _nt, _tn = _make_mm(False)
_nn_x3, _nt_x3, _tn_x3 = _make_mm("x3", (_nn, _nt, _tn))


def _tri_apply(x, transpose):
    c = x.shape[0]
    tri = (lax.broadcasted_iota(jnp.int32, (c, c), 1) <= lax.broadcasted_iota(jnp.int32, (c, c), 0)).astype(BF16)
    dn = (((0 if transpose else 1,), (0,)), ((), ()))
    p1, p2 = _split2(x)
    dot = lambda p: lax.dot_general(tri, p, dn, preferred_element_type=F32)
    return dot(p1) + dot(p2)


@jax.custom_vjp
def _cumsum_rows(x):
    return _tri_apply(x, False)


_cumsum_rows.defvjp(lambda x: (_tri_apply(x, False), None), lambda _, g: (_tri_apply(g, True),))


def _sigmoid(x):
    return 1.0 / (1.0 + jnp.exp(-x))


def _silu(x):
    return x * _sigmoid(x)


def _softplus(z):
    return jnp.maximum(z, 0.0) + jnp.log(1.0 + jnp.exp(-jnp.abs(z)))


def _mm(a, b, *, ta=False, tb=False, tm, tn, tk, name, out_dtype=F32, res=None, scale=None, comm=None):
    m = a.shape[1] if ta else a.shape[0]
    kdim = a.shape[0] if ta else a.shape[1]
    n = b.shape[0] if tb else b.shape[1]
    assert (b.shape[1] if tb else b.shape[0]) == kdim
    tm, tn, tk = min(tm, m), min(tn, n), min(tk, kdim)
    assert m % tm == 0 and n % tn == 0 and kdim % tk == 0, (name, m, n, kdim)
    nk = kdim // tk
    a_spec = pl.BlockSpec((tk, tm), lambda i, j, k: (k, i)) if ta else pl.BlockSpec((tm, tk), lambda i, j, k: (i, k))
    b_spec = pl.BlockSpec((tn, tk), lambda i, j, k: (j, k)) if tb else pl.BlockSpec((tk, tn), lambda i, j, k: (k, j))
    o_spec = pl.BlockSpec((tm, tn), lambda i, j, k: (i, j))
    ca, cb = (0 if ta else 1), (1 if tb else 0)

    def body(*refs):
        if res is not None:
            a_ref, b_ref, r_ref, o_ref, acc_ref = refs
        else:
            a_ref, b_ref, o_ref, acc_ref = refs
        k = pl.program_id(2)

        @pl.when(k == 0)
        def _():
            acc_ref[...] = jnp.zeros_like(acc_ref)

        acc_ref[...] += _dg(a_ref[...], b_ref[...], ca, cb, False)

        @pl.when(k == nk - 1)
        def _():
            acc = acc_ref[...]
            if scale is not None:
                acc = acc * scale
            if res is not None:
                acc = r_ref[...] + acc
            o_ref[...] = acc.astype(out_dtype)

    in_specs = [a_spec, b_spec] + ([o_spec] if res is not None else [])
    args = (a, b) + ((res,) if res is not None else ())
    if comm is None:
        return pl.pallas_call(
            body, name=name, grid=(m // tm, n // tn, nk), in_specs=in_specs, out_specs=o_spec,
            out_shape=SDS((m, n), out_dtype), scratch_shapes=[pltpu.VMEM((tm, tn), F32)],
            compiler_params=_params(("parallel", "parallel", "arbitrary")))(*args)
    (out,), carried = _hosting_call(
        body, comm, name=name, grid=(m // tm, n // tn, nk), in_specs=in_specs, out_specs=[o_spec],
        out_shape=[SDS((m, n), out_dtype)], scratch_shapes=[pltpu.VMEM((tm, tn), F32)], args=args)
    return out, carried


def _row_spec(x, tm, tile_of=lambda i: i):
    if isinstance(x, tuple):
        arr, w, j = x
        return arr, pl.BlockSpec((tm, w), lambda i, j=j: (tile_of(i), j))
    return x, pl.BlockSpec((tm, x.shape[1]), lambda i: (tile_of(i), 0))


def _par_spec(p):
    if isinstance(p, tuple):
        arr, w, j = p
        return arr, pl.BlockSpec((arr.shape[0], w), lambda i, j=j: (0, j))
    return p, pl.BlockSpec(p.shape, lambda i: (0, 0))


def _store_groups(refs, groups, vals):
    for ref, idxs in zip(refs, groups):
        off = 0
        for ix in idxs:
            v = vals[ix]
            ref[:, off:off + v.shape[1]] = v.astype(ref.dtype)
            off += v.shape[1]


SUBLANES = 8


def _x_plan(xs, tm, t, tile_of=lambda i: i):
    arrays, specs, plan = [], [], []
    nb = tm // SUBLANES
    for x in xs:
        if isinstance(x, tuple) and isinstance(x[0], str):
            kind, arr, w, j = x
            if kind == "prev":
                halo = lambda i, j=j: (jnp.maximum(tile_of(i) * nb - 1, 0), j)
            else:
                halo = lambda i, j=j: (jnp.minimum((tile_of(i) + 1) * nb, t // SUBLANES - 1), j)
            arrays += [arr, arr]
            specs += [pl.BlockSpec((tm, w), lambda i, j=j: (tile_of(i), j)), pl.BlockSpec((SUBLANES, w), halo)]
            plan.append((kind, 2, w))
        else:
            arr, spec = _row_spec(x, tm, tile_of)
            arrays.append(arr)
            specs.append(spec)
            plan.append(("plain", 1, spec.block_shape[1]))
    return arrays, specs, plan


def _x_vals(refs, plan, tm, nt, tile_of=lambda i: i):
    vals, k = [], 0
    i = tile_of(pl.program_id(0))
    rows = lax.broadcasted_iota(jnp.int32, (tm, 1), 0)
    for kind, n, _ in plan:
        main = refs[k][...].astype(F32)
        if kind == "prev":
            edge = jnp.where(i == 0, 0.0, refs[k + 1][SUBLANES - 1:SUBLANES, :].astype(F32))
            main = jnp.where(rows == 0, edge, pltpu.roll(main, 1, 0))
        elif kind == "next":
            edge = jnp.where(i == nt - 1, 0.0, refs[k + 1][0:1, :].astype(F32))
            main = jnp.where(rows == tm - 1, edge, pltpu.roll(main, tm - 1, 0))
        vals.append(main)
        k += n
    return vals


def _tile_rows(xs, tm):
    arr = xs[0]
    if isinstance(arr, tuple):
        arr = arr[1] if isinstance(arr[0], str) else arr[0]
    return min(tm, arr.shape[0]), arr.shape[0]


def _rowwise(f, xs, params, out_groups, out_dtypes, *, tm, name, comm=None):
    tm, t = _tile_rows(xs, tm)
    nt = t // tm
    xa, xspecs, plan = _x_plan(xs, tm, t)
    pa, pspecs = (zip(*[_par_spec(p) for p in params]) if params else ((), ()))
    nxr, npar = len(xa), len(pa)
    x_sds = [SDS((tm, w), F32) for _, _, w in plan]
    p_sds = [SDS(s.block_shape, F32) for s in pspecs]
    outs_sds = jax.eval_shape(lambda *vals: f(*vals), *x_sds, *p_sds)
    widths = [sum(outs_sds[ix].shape[1] for ix in idxs) for idxs in out_groups]

    def body(*refs):
        vals = _x_vals(refs[:nxr], plan, tm, nt) + [r[...].astype(F32) for r in refs[nxr:nxr + npar]]
        outs = f(*vals)
        _store_groups(refs[nxr + npar:], out_groups, outs)

    res, carried = _hosting_call(
        body, comm, name=name, grid=(nt,), in_specs=list(xspecs) + list(pspecs),
        out_specs=[pl.BlockSpec((tm, w), lambda i: (i, 0)) for w in widths],
        out_shape=[SDS((t, w), dt) for w, dt in zip(widths, out_dtypes)], scratch_shapes=[], args=(*xa, *pa))
    return res if comm is None else (res, carried)


def _rowwise_bwd(f, xs, params, cots, *, x_grad, p_grad, dx_groups, dx_dtypes, tm, name, extra=None, comm=None,
                 fold_next=None):
    tm, t = _tile_rows(xs, tm)
    nt = t // tm
    tile_of = (lambda i: nt - 1 - i) if fold_next else (lambda i: i)
    xa, xspecs, plan = _x_plan(xs, tm, t, tile_of)
    pa, pspecs = (zip(*[_par_spec(p) for p in params]) if params else ((), ()))
    ca, cspecs = zip(*[_row_spec(c, tm, tile_of) for c in cots])
    extra = extra or {}
    ekeys = sorted(extra)
    ea, especs = (zip(*[_row_spec(extra[k], tm, tile_of) for k in ekeys]) if ekeys else ((), ()))
    nx, nxr, npar, nc, ne = len(plan), len(xa), len(pa), len(ca), len(ea)
    gx = [i for i in range(nx) if x_grad[i]]
    gp = [i for i in range(npar) if p_grad[i]]
    all_widths = [sum(plan[gx[ix]][2] for ix in idxs) for idxs in dx_groups]
    emitted = [k for k in range(len(dx_groups)) if not (fold_next and k == fold_next[1])]
    widths = [all_widths[k] for k in emitted]
    ng = len(emitted)

    def body(*refs):
        ins = refs[:nxr + npar + nc + ne]
        outs = refs[nxr + npar + nc + ne:]
        vals = _x_vals(ins[:nxr], plan, tm, nt, tile_of) + [r[...].astype(F32) for r in ins[nxr:nxr + npar]]
        cvals = tuple(r[...].astype(F32) for r in ins[nxr + npar:nxr + npar + nc])
        evals = [r[...].astype(F32) for r in ins[nxr + npar + nc:]]
        diff_idx = gx + [nx + i for i in gp]

        def g(*dargs):
            full = list(vals)
            for ix, v in zip(diff_idx, dargs):
                full[ix] = v
            return tuple(f(*full))

        _, vjp = jax.vjp(g, *[vals[ix] for ix in diff_idx])
        grads = vjp(cvals)
        dxs = list(grads[:len(gx)])
        for k, ev in zip(ekeys, evals):
            dxs[k] = dxs[k] + ev
        _store_groups(outs[:ng], [dx_groups[k] for k in emitted], dxs)
        i = pl.program_id(0)
        if fold_next:
            main_ref, carry_ref = outs[emitted.index(fold_next[0])], refs[-1]
            rows = lax.broadcasted_iota(jnp.int32, (tm, 1), 0)
            off = 0
            for ix in dx_groups[fold_next[1]]:
                piece = dxs[ix]
                cols = slice(off, off + piece.shape[1])
                edge = jnp.where(i == 0, 0.0, carry_ref[0:1, cols])
                main_ref[:, cols] += jnp.where(rows == tm - 1, edge, pltpu.roll(piece, tm - 1, 0))
                carry_ref[:, cols] = piece[:SUBLANES]
                off += piece.shape[1]
        for ref, gval in zip(outs[ng:ng + len(gp)], grads[len(gx):]):
            @pl.when(i == 0)
            def _(ref=ref):
                ref[...] = jnp.zeros_like(ref)
            ref[...] += gval

    dp_specs = [pl.BlockSpec(pspecs[i].block_shape, lambda i: (0, 0)) for i in gp]
    dp_shapes = [SDS(pspecs[i].block_shape, F32) for i in gp]
    scratch = [pltpu.VMEM((SUBLANES, all_widths[fold_next[1]]), F32)] if fold_next else []
    res, carried = _hosting_call(
        body, comm, name=name, grid=(nt,), in_specs=list(xspecs) + list(pspecs) + list(cspecs) + list(especs),
        out_specs=[pl.BlockSpec((tm, w), lambda i: (tile_of(i), 0)) for w in widths] + dp_specs,
        out_shape=[SDS((t, w), dt) for w, dt in zip(widths, dx_dtypes)] + dp_shapes, scratch_shapes=scratch,
        args=(*xa, *pa, *ca, *ea))
    return res if comm is None else (res, carried)


def _rms_f(x, g):
    return (x * lax.rsqrt(jnp.mean(x * x, axis=-1, keepdims=True) + NORM_EPS) * g,)


def _group_sum_impl(x, ones_bd):
    p1, p2 = _split2(x)
    dot = lambda p: lax.dot_general(p, ones_bd.astype(BF16), (((1,), (0,)), ((), ())), preferred_element_type=F32)
    return dot(p1) + dot(p2)


@jax.custom_vjp
def _group_sum(x, ones_bd):
    return _group_sum_impl(x, ones_bd)


_group_sum.defvjp(lambda x, o: (_group_sum_impl(x, o), o),
                  lambda o, g: (_group_sum_impl(g, o), jnp.zeros_like(o)))


def _rwkv_prep_f(r, k, v, lo, rp, kp, vp, lop, mu_r, mu_k, mu_v, mu_lo, w0, w2p, a0, a2p, g2p, k_k, k_a, ones_bd):
    r = r + mu_r * (rp - r)
    k = k + mu_k * (kp - k)
    v = v + mu_v * (vp - v)
    lo = lo + mu_lo * (lop - lo)
    w_log = -_softplus(-(w0 + _nn(jnp.tanh(lo), w2p))) - 0.5
    lw = -jnp.exp(w_log)
    a_g = _sigmoid(a0 + _nn(lo, a2p))
    g = _nn(_sigmoid(lo), g2p)
    kk = k * k_k
    kk = kk / jnp.maximum(jnp.sqrt(_group_sum(kk * kk, ones_bd)), L2_EPS)
    k2 = k * (1.0 + (a_g - 1.0) * k_a)
    return r, lw, k2, v, -kk, kk * a_g, g


def _rwkv_post_f(y, r, k2, v, g, r_k, gn_w, gn_b, ones_bd):
    inv_n = 1.0 / HB_DIM
    mean = _group_sum(y, ones_bd) * inv_n
    yc = y - mean
    var = _group_sum(yc * yc, ones_bd) * inv_n
    yn = yc * lax.rsqrt(var + RWKV_GN_EPS) * gn_w + gn_b
    bonus = _group_sum(r * k2 * r_k, ones_bd) * v
    return ((yn + bonus) * g,)


def _tri(c, strict=False):
    ii = lax.broadcasted_iota(jnp.int32, (c, c), 0)
    jj = lax.broadcasted_iota(jnp.int32, (c, c), 1)
    return (jj < ii) if strict else (jj <= ii)


def _hgrn_step(st0, q_a, f_a, i_a, g_a, l0, l1, onorm):
    nh, nj = len(q_a), len(q_a[0])
    c = q_a[0][0].shape[0]
    combos = [(j, h) for j in range(nj) for h in range(nh)]
    every = lambda fn: {q: fn(q) for q in combos}
    at_ = lambda d: (lambda q: d[q[1]][q[0]])
    qa_, fa_, ia_, ga_ = (at_(z) for z in (q_a, f_a, i_a, g_a))
    incl = _tri(c)
    rows = lax.broadcasted_iota(jnp.int32, (c, 1), 0)
    lb = []
    for h in range(nh):
        mx = jnp.maximum(l0[h], l1[h])
        e0, e1 = jnp.exp(l0[h] - mx), jnp.exp(l1[h] - mx)
        lb.append(e0 / (e0 + e1))
    forget = every(lambda q: lb[q[1]] + (1.0 - lb[q[1]]) * _sigmoid(fa_(q)))
    qs = every(lambda q: _silu(qa_(q)))
    kk = every(lambda q: 1.0 - forget[q])
    lf = every(lambda q: jnp.log(forget[q]))
    bcum = every(lambda q: _cumsum_rows(lf[q]))
    bref = every(lambda q: jnp.sum(jnp.where(rows <= c // 2, lf[q], 0.0), axis=0, keepdims=True))
    blast = every(lambda q: jnp.sum(lf[q], axis=0, keepdims=True))
    scores = every(lambda q: jnp.where(incl, _nt(qs[q] * jnp.exp(bcum[q] - bref[q]),
                                                 kk[q] * jnp.exp(bref[q] - bcum[q])), 0.0))
    intra = every(lambda q: _nn(scores[q], ia_(q)))
    qb = every(lambda q: qs[q] * jnp.exp(bcum[q]))
    upd = every(lambda q: _tn(ia_(q), kk[q] * jnp.exp(blast[q] - bcum[q])))
    dec = every(lambda q: jnp.exp(blast[q]))
    st = list(st0)
    o = {}
    for j in range(nj):
        for h in range(nh):
            o[(j, h)] = intra[(j, h)] + _nt(qb[(j, h)], st[h])
        st = [st[h] * dec[(j, h)] + upd[(j, h)] for h in range(nh)]
    out = every(lambda q: o[q] * lax.rsqrt(jnp.mean(o[q] * o[q], axis=-1, keepdims=True) + NORM_EPS)
                * onorm[q[1]] * _silu(ga_(q)))
    return [[out[(j, h)] for j in range(nj)] for h in range(nh)], st


def _hgrn_blocks(ref, nj, c):
    return [[ref[j * c:(j + 1) * c, h * HA_DIM:(h + 1) * HA_DIM] for j in range(nj)] for h in range(HA_HEADS)]


def _hgrn_cols(ref):
    return [ref[:, h * HA_DIM:(h + 1) * HA_DIM] for h in range(HA_HEADS)]


def _hgrn_fwd(p_h, l0, l1, onorm):
    t = p_h.shape[0]
    cc, nj = HGRN_CHUNK, HGRN_GROUP
    c = cc * nj
    n = t // c

    def body(q_ref, f_ref, i_ref, g_ref, l0_ref, l1_ref, on_ref, o_ref, hs_ref, st_ref):
        @pl.when(pl.program_id(0) == 0)
        def _():
            st_ref[...] = jnp.zeros_like(st_ref)

        hs_ref[0] = st_ref[...]
        o, st1 = _hgrn_step([st_ref[h] for h in range(HA_HEADS)],
                            *[_hgrn_blocks(ref, nj, cc) for ref in (q_ref, f_ref, i_ref, g_ref)],
                            _hgrn_cols(l0_ref), _hgrn_cols(l1_ref), _hgrn_cols(on_ref))
        for h in range(HA_HEADS):
            for j in range(nj):
                o_ref[j * cc:(j + 1) * cc, h * HA_DIM:(h + 1) * HA_DIM] = o[h][j]
            st_ref[h] = st1[h]

    col = lambda j: pl.BlockSpec((c, W_A), lambda i, j=j: (i, j))
    par = pl.BlockSpec((1, W_A), lambda i: (0, 0))
    return pl.pallas_call(
        body, name="hgrn_fwd", grid=(n,), in_specs=[col(0), col(1), col(2), col(3), par, par, par],
        out_specs=[pl.BlockSpec((c, W_A), lambda i: (i, 0)),
                   pl.BlockSpec((1, HA_HEADS, HA_DIM, HA_DIM), lambda i: (i, 0, 0, 0))],
        out_shape=[SDS((t, W_A), F32), SDS((n, HA_HEADS, HA_DIM, HA_DIM), F32)],
        scratch_shapes=[pltpu.VMEM((HA_HEADS, HA_DIM, HA_DIM), F32)],
        compiler_params=_params(("arbitrary",)))(p_h, p_h, p_h, p_h, l0, l1, onorm)


def _hgrn_bwd(p_h, l0, l1, onorm, hs, do, do_col, comm=None):
    t = p_h.shape[0]
    cc, nj = HGRN_CHUNK, HGRN_GROUP
    c = cc * nj
    n = t // c

    def body(q_ref, f_ref, i_ref, g_ref, l0_ref, l1_ref, on_ref, hs_ref, do_ref,
             dp_ref, dl0_ref, dl1_ref, don_ref, dst_ref):
        @pl.when(pl.program_id(0) == 0)
        def _():
            dst_ref[...] = jnp.zeros_like(dst_ref)
            dl0_ref[...] = jnp.zeros_like(dl0_ref)
            dl1_ref[...] = jnp.zeros_like(dl1_ref)
            don_ref[...] = jnp.zeros_like(don_ref)

        args = ([hs_ref[0, h] for h in range(HA_HEADS)],
                *[_hgrn_blocks(ref, nj, cc) for ref in (q_ref, f_ref, i_ref, g_ref)],
                _hgrn_cols(l0_ref), _hgrn_cols(l1_ref), _hgrn_cols(on_ref))
        _, vjp = jax.vjp(_hgrn_step, *args)
        dst0, dq, df, di, dg, dl0, dl1, don = vjp((_hgrn_blocks(do_ref, nj, cc),
                                                   [dst_ref[h] for h in range(HA_HEADS)]))
        for h in range(HA_HEADS):
            sl = slice(h * HA_DIM, (h + 1) * HA_DIM)
            for k, dv in enumerate((dq, df, di, dg)):
                for j in range(nj):
                    dp_ref[j * cc:(j + 1) * cc, k * W_A + h * HA_DIM:k * W_A + (h + 1) * HA_DIM] = dv[h][j]
            dl0_ref[:, sl] += dl0[h]
            dl1_ref[:, sl] += dl1[h]
            don_ref[:, sl] += don[h]
            dst_ref[h] = dst0[h]

    col = lambda j: pl.BlockSpec((c, W_A), lambda i, j=j: (n - 1 - i, j))
    par = pl.BlockSpec((1, W_A), lambda i: (0, 0))
    return _hosting_call(
        body, comm, name="hgrn_bwd", grid=(n,),
        in_specs=[col(0), col(1), col(2), col(3), par, par, par,
                  pl.BlockSpec((1, HA_HEADS, HA_DIM, HA_DIM), lambda i: (n - 1 - i, 0, 0, 0)),
                  pl.BlockSpec((c, W_A), lambda i: (n - 1 - i, do_col))],
        out_specs=[pl.BlockSpec((c, N_HGRN_COLS), lambda i: (n - 1 - i, 0)), par, par, par],
        out_shape=[SDS((t, N_HGRN_COLS), F32), SDS((1, W_A), F32), SDS((1, W_A), F32), SDS((1, W_A), F32)],
        scratch_shapes=[pltpu.VMEM((HA_HEADS, HA_DIM, HA_DIM), F32)],
        args=(p_h, p_h, p_h, p_h, l0, l1, onorm, hs, do))


HB_PAIRS = HB_HEADS // 2
PAIR_W = 2 * HB_DIM


def _head_lane_masks():
    lane = lax.broadcasted_iota(jnp.int32, (1, PAIR_W), 1)
    return (lane < HB_DIM).astype(F32), (lane >= HB_DIM).astype(F32)


@jax.custom_vjp
def _stack_heads(x):
    m0, m1 = _head_lane_masks()
    return jnp.concatenate([x * m0, x * m1], axis=0)


def _stack_heads_bwd(_, g):
    m0, m1 = _head_lane_masks()
    c = g.shape[0] // 2
    return (g[:c] * m0 + g[c:] * m1,)


_stack_heads.defvjp(lambda x: (_stack_heads(x), None), _stack_heads_bwd)


@jax.custom_vjp
def _unstack_heads(ys):
    c = ys.shape[0] // 2
    return ys[:c] + ys[c:]


_unstack_heads.defvjp(lambda ys: (_unstack_heads(ys), None), lambda _, g: (_stack_heads(g),))


def _same_head_block(c):
    ii = lax.broadcasted_iota(jnp.int32, (2 * c, 2 * c), 0)
    jj = lax.broadcasted_iota(jnp.int32, (2 * c, 2 * c), 1)
    same = (ii < c) == (jj < c)
    return same & (jj <= ii), same & (jj < ii), (ii == jj).astype(F32)


@jax.custom_vjp
def _rows_join(top, bottom):
    return jnp.concatenate([top, bottom], axis=0)


def _rows_join_bwd(n_top, g):
    return g[:n_top], g[n_top:]


_rows_join.defvjp(lambda top, bottom: (_rows_join(top, bottom), top.shape[0]), _rows_join_bwd)


def _rows_split_impl(x, n_top):
    return x[:n_top], x[n_top:]


_rows_split = jax.custom_vjp(_rows_split_impl, nondiff_argnums=(1,))
_rows_split.defvjp(lambda x, n_top: (_rows_split_impl(x, n_top), None),
                   lambda n_top, _, g: (jnp.concatenate([g[0], g[1]], axis=0),))


def _rwkv_step(s0, r, lw, k, v, a, b):
    npair, nj = len(r), len(r[0])
    c = r[0][0].shape[0]
    combos = [(j, p) for j in range(nj) for p in range(npair)]
    every = lambda fn: {q: fn(q) for q in combos}
    at_ = lambda d: (lambda q: d[q[1]][q[0]])
    r_, lw_, k_, v_, a_, b_ = (at_(z) for z in (r, lw, k, v, a, b))
    incl, strict, eye = _same_head_block(c)

    gam = every(lambda q: _cumsum_rows(lw_(q)))
    gtot = every(lambda q: jnp.sum(lw_(q), axis=0, keepdims=True))
    eneg = every(lambda q: jnp.exp(-gam[q]))
    edec = every(lambda q: jnp.exp(gtot[q] - gam[q]))
    at = every(lambda q: _stack_heads(a_(q) * jnp.exp(gam[q] - lw_(q))))
    rt = every(lambda q: _stack_heads(r_(q) * jnp.exp(gam[q])))
    bt = every(lambda q: _stack_heads(b_(q) * eneg[q]))
    kt = every(lambda q: _stack_heads(k_(q) * eneg[q]))
    bdec = every(lambda q: _stack_heads(b_(q) * edec[q]))
    kdec = every(lambda q: _stack_heads(k_(q) * edec[q]))
    vs = every(lambda q: _stack_heads(v_(q)))
    a_ab = every(lambda q: jnp.where(strict, _nt(at[q], bt[q]), 0.0))
    a_ak = every(lambda q: jnp.where(strict, _nt(at[q], kt[q]), 0.0))
    a_rb = every(lambda q: jnp.where(incl, _nt(rt[q], bt[q]), 0.0))
    a_rk = every(lambda q: jnp.where(incl, _nt(rt[q], kt[q]), 0.0))
    tinv = every(lambda q: eye + a_ab[q])
    pw = a_ab
    span = 2
    while span < c:
        pw = every(lambda q, pw=pw: _nn_x3(pw[q], pw[q]))
        tinv = every(lambda q, pw=pw, tinv=tinv: tinv[q] + _nn_x3(pw[q], tinv[q]))
        span *= 2
    akv = every(lambda q: _nn(a_ak[q], vs[q]))
    w1 = every(lambda q: _nn(tinv[q], at[q]))
    u0 = every(lambda q: _nn(tinv[q], akv[q]))
    wr = every(lambda q: _rows_join(w1[q], rt[q]))
    bk = every(lambda q: _rows_join(bdec[q], kdec[q]))
    yv = every(lambda q: _nn(a_rk[q], vs[q]))
    gdec = every(lambda q: jnp.exp(gtot[q]))

    s = list(s0)
    y = [[None] * nj for _ in range(npair)]
    for j in range(nj):
        both = {p: _rows_split(_nt(wr[(j, p)], s[p]), 2 * c) for p in range(npair)}
        u = {p: both[p][0] + u0[(j, p)] for p in range(npair)}
        for p in range(npair):
            y[p][j] = _unstack_heads(both[p][1] + _nn(a_rb[(j, p)], u[p]) + yv[(j, p)])
        s = [s[p] * gdec[(j, p)] + _tn(_rows_join(u[p], vs[(j, p)]), bk[(j, p)]) for p in range(npair)]
    return y, s


def _rwkv_blocks(ref, nj, c):
    return [[ref[j * c:(j + 1) * c, p * PAIR_W:(p + 1) * PAIR_W] for j in range(nj)] for p in range(HB_PAIRS)]


def _rwkv_fwd(seqs, comm=None):
    t = seqs[0].shape[0]
    c, nj = RWKV_CHUNK, RWKV_GROUP
    n = t // (c * nj)

    def body(r_ref, lw_ref, k_ref, v_ref, a_ref, b_ref, y_ref, hs_ref, st_ref):
        @pl.when(pl.program_id(0) == 0)
        def _():
            st_ref[...] = jnp.zeros_like(st_ref)

        hs_ref[0] = st_ref[...]
        s0 = [st_ref[p] for p in range(HB_PAIRS)]
        y, s1 = _rwkv_step(s0, *[_rwkv_blocks(ref, nj, c) for ref in (r_ref, lw_ref, k_ref, v_ref, a_ref, b_ref)])
        for p in range(HB_PAIRS):
            for j in range(nj):
                y_ref[j * c:(j + 1) * c, p * PAIR_W:(p + 1) * PAIR_W] = y[p][j]
            st_ref[p] = s1[p]

    seq = pl.BlockSpec((c * nj, W_B), lambda i: (i, 0))
    return _hosting_call(
        body, comm, name="rwkv_fwd", grid=(n,), in_specs=[seq] * 6,
        out_specs=[seq, pl.BlockSpec((1, HB_PAIRS, PAIR_W, PAIR_W), lambda i: (i, 0, 0, 0))],
        out_shape=[SDS((t, W_B), F32), SDS((n, HB_PAIRS, PAIR_W, PAIR_W), F32)],
        scratch_shapes=[pltpu.VMEM((HB_PAIRS, PAIR_W, PAIR_W), F32)], args=tuple(seqs))


def _rwkv_bwd(seqs, hs, dy, comm=None):
    t = seqs[0].shape[0]
    c, nj = RWKV_CHUNK, RWKV_GROUP
    n = t // (c * nj)

    def body(r_ref, lw_ref, k_ref, v_ref, a_ref, b_ref, hs_ref, dy_ref,
             dr_ref, dlw_ref, dk_ref, dv_ref, da_ref, db_ref, dst_ref):
        @pl.when(pl.program_id(0) == 0)
        def _():
            dst_ref[...] = jnp.zeros_like(dst_ref)

        s0 = [hs_ref[0, p] for p in range(HB_PAIRS)]
        seq_vals = [_rwkv_blocks(ref, nj, c) for ref in (r_ref, lw_ref, k_ref, v_ref, a_ref, b_ref)]
        _, vjp = jax.vjp(_rwkv_step, s0, *seq_vals)
        grads = vjp((_rwkv_blocks(dy_ref, nj, c), [dst_ref[p] for p in range(HB_PAIRS)]))
        for ref, gr in zip((dr_ref, dlw_ref, dk_ref, dv_ref, da_ref, db_ref), grads[1:]):
            for p in range(HB_PAIRS):
                for j in range(nj):
                    ref[j * c:(j + 1) * c, p * PAIR_W:(p + 1) * PAIR_W] = gr[p][j]
        m0, m1 = _head_lane_masks()
        rows0 = (lax.broadcasted_iota(jnp.int32, (PAIR_W, 1), 0) < HB_DIM).astype(F32)
        blocks = rows0 * m0 + (1.0 - rows0) * m1
        for p in range(HB_PAIRS):
            dst_ref[p] = grads[0][p] * blocks

    seq = pl.BlockSpec((c * nj, W_B), lambda i: (n - 1 - i, 0))
    return _hosting_call(
        body, comm, name="rwkv_bwd", grid=(n,),
        in_specs=[seq] * 6 + [pl.BlockSpec((1, HB_PAIRS, PAIR_W, PAIR_W), lambda i: (n - 1 - i, 0, 0, 0)), seq],
        out_specs=[seq] * 6, out_shape=[SDS((t, W_B), F32)] * 6,
        scratch_shapes=[pltpu.VMEM((HB_PAIRS, PAIR_W, PAIR_W), F32)], args=(*seqs, hs, dy))


def _final_loss(x3, fnorm, target, *, tm):
    t, d = x3.shape

    def body(x_ref, g_ref, t_ref, dx_ref, dg_ref, loss_ref):
        @pl.when(pl.program_id(0) == 0)
        def _():
            dg_ref[...] = jnp.zeros_like(dg_ref)
            loss_ref[...] = jnp.zeros_like(loss_ref)

        x, g = x_ref[...], g_ref[...]
        rinv = lax.rsqrt(jnp.mean(x * x, axis=-1, keepdims=True) + NORM_EPS)
        xh = x * rinv
        diff = xh * g - t_ref[...]
        loss_ref[...] += 0.5 * jnp.sum(jnp.mean(diff * diff, axis=-1, keepdims=True))
        dy = diff * (1.0 / d)
        dg_ref[...] += jnp.sum(dy * xh, axis=0, keepdims=True)
        dxh = dy * g
        dx_ref[...] = rinv * (dxh - xh * jnp.mean(dxh * xh, axis=-1, keepdims=True))

    row = pl.BlockSpec((tm, d), lambda i: (i, 0))
    return pl.pallas_call(
        body, name="final_loss", grid=(t // tm,), in_specs=[row, pl.BlockSpec((1, d), lambda i: (0, 0)), row],
        out_specs=[row, pl.BlockSpec((1, d), lambda i: (0, 0)), pl.BlockSpec((8, 128), lambda i: (0, 0))],
        out_shape=[SDS((t, d), F32), SDS((1, d), F32), SDS((8, 128), F32)],
        compiler_params=_params(("arbitrary",)))(x3, fnorm, target)


def _gate_up_act(h, wgt, wut, *, tm, tn, name, comm=None):
    t, d = h.shape
    tm = min(tm, t)

    def body(h_ref, g_ref, u_ref, a_out, u_out, act_out):
        hv = h_ref[...]
        a = _dg(hv, g_ref[...], 1, 1, False)
        u = _dg(hv, u_ref[...], 1, 1, False)
        a_out[...] = a.astype(a_out.dtype)
        u_out[...] = u.astype(u_out.dtype)
        act_out[...] = (_silu(a) * u).astype(act_out.dtype)

    wspec = pl.BlockSpec((tn, d), lambda i, j: (j, 0))
    ospec = pl.BlockSpec((tm, tn), lambda i, j: (i, j))
    return _hosting_call(
        body, comm, name=name, grid=(t // tm, D_FF // tn),
        in_specs=[pl.BlockSpec((tm, d), lambda i, j: (i, 0)), wspec, wspec], out_specs=[ospec, ospec, ospec],
        out_shape=[SDS((t, D_FF), BF16), SDS((t, D_FF), BF16), SDS((t, D_FF), BF16)], scratch_shapes=[],
        args=(h, wgt, wut))


def _dact_swiglu(dout, wd, a, u, *, tm, tn, name, comm=None):
    t, d = dout.shape
    tm = min(tm, t)

    def body(d_ref, w_ref, a_ref, u_ref, da_out, du_out):
        dact = 0.5 * _dg(d_ref[...], w_ref[...], 1, 1, False)
        av, uv = a_ref[...].astype(F32), u_ref[...].astype(F32)
        s = _sigmoid(av)
        da_out[...] = (dact * uv * (s * (1.0 + av * (1.0 - s)))).astype(da_out.dtype)
        du_out[...] = (dact * (av * s)).astype(du_out.dtype)

    tile = pl.BlockSpec((tm, tn), lambda i, j: (i, j))
    return _hosting_call(
        body, comm, name=name, grid=(t // tm, D_FF // tn),
        in_specs=[pl.BlockSpec((tm, d), lambda i, j: (i, 0)), pl.BlockSpec((tn, d), lambda i, j: (j, 0)), tile, tile],
        out_specs=[tile, tile], out_shape=[SDS((t, D_FF), BF16), SDS((t, D_FF), BF16)], scratch_shapes=[],
        args=(dout, wd, a, u))


class _Plan:
    def __init__(self):
        self.entries, self.counts = collections.defaultdict(list), {}

    def carry(self, host, comm_of, after):
        self.entries[host].append((comm_of, after))

    def comm(self, host, g):
        comms = [comm_of(g) for comm_of, _ in self.entries.get(host, [])]
        self.counts[host] = [len(c.arrays) for c in comms]
        return functools.reduce(_join_comms, comms) if comms else None

    def done(self, host, results, w):
        start = 0
        for (_, after), n in zip(self.entries.get(host, []), self.counts.get(host, [])):
            after(results[start:start + n], w)
            start += n


def _ffn_fwd(x, w, tag, plan, g):
    comm = plan.comm(f"{tag}_rms", g)
    res = _rowwise(_rms_f, [x], [w[f"{tag}_norm"]], [[0]], [BF16], tm=512, name=f"{tag}_rms", comm=comm)
    (h,), carried = res if comm is not None else (res, [])
    plan.done(f"{tag}_rms", carried, w)
    (a, u, act), carried = _gate_up_act(h, w[f"{tag}_wgt"], w[f"{tag}_wut"], tm=2048, tn=256, name=f"{tag}_gate_up",
                                        comm=plan.comm(f"{tag}_gate_up", g))
    plan.done(f"{tag}_gate_up", carried, w)
    comm = plan.comm(f"{tag}_down", g)
    out = _mm(act, w[f"{tag}_wd"], tm=1024, tn=D_MODEL, tk=D_FF, name=f"{tag}_down", res=x, scale=0.5, comm=comm)
    if comm is not None:
        out, carried = out
        plan.done(f"{tag}_down", carried, w)
    return out, (h, a, u, act)


def _ffn_bwd(dout, x, w, saved, tag, plan, g):
    h, a, u, act = saved

    def carrying(fn, host, *args, **kwargs):
        comm = plan.comm(host, g)
        res = fn(*args, name=host, comm=comm, **kwargs)
        out, carried = res if comm is not None else (res, [])
        plan.done(host, carried, w)
        return out

    (da, du), carried = _dact_swiglu(dout, w[f"{tag}_wd"], a, u, tm=2048, tn=256, name=f"{tag}_dact",
                                     comm=plan.comm(f"{tag}_dact", g))
    plan.done(f"{tag}_dact", carried, w)
    g[f"{tag}_wd"] = _mm(act, dout, ta=True, tm=D_FF // 2, tn=D_MODEL, tk=1024, name=f"{tag}_dwd", scale=0.5)
    g[f"{tag}_wgt"] = carrying(_mm, f"{tag}_dwg", da, h, ta=True, tm=D_FF // 2, tn=D_MODEL, tk=1024)
    g[f"{tag}_wut"] = carrying(_mm, f"{tag}_dwu", du, h, ta=True, tm=D_FF // 2, tn=D_MODEL, tk=1024)
    dh = carrying(_mm, f"{tag}_dh_g", da, w[f"{tag}_wgt"], tm=1024, tn=D_MODEL, tk=D_FF)
    dh = carrying(_mm, f"{tag}_dh_u", du, w[f"{tag}_wut"], tm=1024, tn=D_MODEL, tk=D_FF, res=dh)
    dx, g[f"{tag}_norm"] = carrying(_rowwise_bwd, f"{tag}_drms", _rms_f, [x], [w[f"{tag}_norm"]], [dh],
                                    x_grad=[True], p_grad=[True], dx_groups=[[0]], dx_dtypes=[F32], tm=512,
                                    extra={0: dout})
    return dx


def _local_step(x, target, w, plan=None):
    plan = plan or _Plan()
    ones_bd = jnp.kron(jnp.eye(HB_HEADS, dtype=F32), jnp.ones((HB_DIM, HB_DIM), F32))
    g = {}
    x1, ffn1_saved = _ffn_fwd(x, w, "ffn1", plan, g)
    hm, = _rowwise(_rms_f, [x1], [w["mix_norm"]], [[0]], [BF16], tm=512, name="mix_rms")
    p_h = _mm(hm, w["w_in_h"], tm=2048, tn=256, tk=D_MODEL, name="inproj_h")
    p_r = _mm(hm, w["w_in_r"], tm=2048, tn=256, tk=D_MODEL, name="inproj_r")
    o_a, hgrn_states = _hgrn_fwd(p_h, w["lb0"], w["lb1"], w["hgrn_out_norm"])

    mu = w["mu_pad"]
    prep_xs = [(p_r, W_B, 0), (p_r, W_B, 1), (p_r, W_B, 2), (p_r, LORA_PAD, 6),
               ("prev", p_r, W_B, 0), ("prev", p_r, W_B, 1), ("prev", p_r, W_B, 2), ("prev", p_r, LORA_PAD, 6)]
    prep_ps = [(mu, W_B, 0), (mu, W_B, 1), (mu, W_B, 2), (mu, LORA_PAD, 6), w["rwkv_w0"], w["w2_pad"], w["rwkv_a0"],
               w["a2_pad"], w["g2_pad"], w["rwkv_k_k"], w["rwkv_k_a"], ones_bd]
    prep_f = _rwkv_prep_f
    r, lw, k2, v, a_vec, b_vec, gate = _rowwise(prep_f, prep_xs, prep_ps, [[0], [1], [2], [3], [4], [5], [6]],
                                                [F32] * 7, tm=256, name="rwkv_prep")
    seqs = [r, lw, k2, v, a_vec, b_vec]
    (y, rwkv_states), carried = _rwkv_fwd(seqs, comm=plan.comm("rwkv_fwd", g))
    plan.done("rwkv_fwd", carried, w)
    post_f = _rwkv_post_f
    post_xs = [y, r, k2, v, gate]
    post_ps = [w["rwkv_r_k"], w["rwkv_gn_w"], w["rwkv_gn_b"], ones_bd]
    o, = _rowwise(lambda o_a_, *rest: (o_a_,) + tuple(post_f(*rest)), [o_a] + post_xs, post_ps, [[0, 1]], [F32],
                  tm=256, name="rwkv_post")
    x2 = _mm(o, w["w_out"], tm=2048, tn=256, tk=D_MODEL, name="outproj", res=x1)
    x3, ffn2_saved = _ffn_fwd(x2, w, "ffn2", plan, g)
    dx3, g["final_norm"], loss = _final_loss(x3, w["final_norm"], target, tm=256)

    dx2 = _ffn_bwd(dx3, x2, w, ffn2_saved, "ffn2", plan, g)
    do = _mm(dx2, w["w_out"], tb=True, tm=2048, tn=256, tk=D_MODEL, name="outproj_do")
    g["w_out"] = _mm(o, dx2, ta=True, tm=D_MODEL, tn=D_MODEL, tk=1024, name="outproj_dw")

    (dp_h, g["lb0"], g["lb1"], g["hgrn_out_norm"]), carried = _hgrn_bwd(
        p_h, w["lb0"], w["lb1"], w["hgrn_out_norm"], hgrn_states, do, 0, comm=plan.comm("hgrn_bwd", g))
    plan.done("hgrn_bwd", carried, w)
    post_out = _rowwise_bwd(post_f, post_xs, post_ps, [(do, W_B, 1)], x_grad=[True] * 5, p_grad=[True] * 3 + [False],
                            dx_groups=[[0], [1], [2], [3], [4]], dx_dtypes=[F32] * 5, tm=256, name="rwkv_post_bwd")
    dy, dr1, dk1, dv1, dgate, g["rwkv_r_k"], g["rwkv_gn_w"], g["rwkv_gn_b"] = post_out
    (dr2, dlw, dk2, dv2, da_vec, db_vec), carried = _rwkv_bwd(seqs, rwkv_states, dy, comm=plan.comm("rwkv_bwd", g))
    plan.done("rwkv_bwd", carried, w)

    def prep2_f(*vals):
        r_, lw_, k2_, v_, a_, b_, g_ = prep_f(*vals)
        return r_, lw_, k2_, v_, a_, b_, g_, r_, k2_, v_

    prep_comm = plan.comm("rwkv_prep_bwd", g)
    prep_out = _rowwise_bwd(prep2_f, prep_xs, prep_ps, [dr2, dlw, dk2, dv2, da_vec, db_vec, dgate, dr1, dk1, dv1],
                            x_grad=[True] * 8, p_grad=[True] * 11 + [False], dx_groups=[[0, 1, 2, 3], [4, 5, 6, 7]],
                            dx_dtypes=[F32], tm=256, name="rwkv_prep_bwd", fold_next=(0, 1), comm=prep_comm)
    prep_out, carried = prep_out if prep_comm is not None else (prep_out, [])
    plan.done("rwkv_prep_bwd", carried, w)
    dp_r = prep_out[0]
    (dmu_r, dmu_k, dmu_v, dmu_lo, g["rwkv_w0"], g["w2_pad"], g["rwkv_a0"], g["a2_pad"], g["g2_pad"],
     g["rwkv_k_k"], g["rwkv_k_a"]) = prep_out[1:]
    g["mu_pad"] = jnp.concatenate([dmu_r, dmu_k, dmu_v, dmu_lo], axis=1)
    dhm = _mm(dp_h, w["w_in_h"], tb=True, tm=1024, tn=D_MODEL, tk=N_HGRN_COLS, name="inproj_dh_h")
    dhm = _mm(dp_r, w["w_in_r"], tb=True, tm=1024, tn=D_MODEL, tk=N_RWKV_PAD, name="inproj_dh_r", res=dhm)
    g["w_in_h"] = _mm(hm, dp_h, ta=True, tm=D_MODEL, tn=D_MODEL, tk=1024, name="inproj_dw_h")
    g["w_in_r"] = _mm(hm, dp_r, ta=True, tm=D_MODEL, tn=N_RWKV_PAD // 2, tk=1024, name="inproj_dw_r")
    mix_comm = plan.comm("mix_drms", g)
    mix_out = _rowwise_bwd(_rms_f, [x1], [w["mix_norm"]], [dhm], x_grad=[True], p_grad=[True], dx_groups=[[0]],
                           dx_dtypes=[F32], tm=512, name="mix_drms", extra={0: dx2}, comm=mix_comm)
    (dx1, g["mix_norm"]), carried = mix_out if mix_comm is not None else (mix_out, [])
    plan.done("mix_drms", carried, w)
    dx0 = _ffn_bwd(dx1, x, w, ffn1_saved, "ffn1", plan, g)
    return loss, dx0, g


HBM_SPEC = pl.BlockSpec(memory_space=pl.ANY)

Comm = collections.namedtuple("Comm", "arrays out_shapes aliased sem_shapes start finish")


def _join_comms(first, second):
    n, s = len(first.arrays), len(first.sem_shapes)

    def start(ins, outs, sems):
        first.start(ins[:n], outs[:n], sems[:s])
        second.start(ins[n:], outs[n:], sems[s:])

    def finish(ins, outs, sems):
        first.finish(ins[:n], outs[:n], sems[:s])
        second.finish(ins[n:], outs[n:], sems[s:])

    return Comm(list(first.arrays) + list(second.arrays), list(first.out_shapes) + list(second.out_shapes),
                list(first.aliased) + list(second.aliased), list(first.sem_shapes) + list(second.sem_shapes),
                start, finish)


def _hosting_call(body, comm, *, name, grid, in_specs, out_specs, out_shape, scratch_shapes, args, prefetch=()):
    sem = ("arbitrary",) * len(grid)
    npre = len(prefetch)

    def call(fn, in_specs_, out_specs_, out_shape_, scratch_, aliases, operands):
        if not npre:
            return pl.pallas_call(fn, name=name, grid=grid, in_specs=in_specs_, out_specs=out_specs_,
                                  out_shape=out_shape_, scratch_shapes=scratch_, input_output_aliases=aliases,
                                  compiler_params=_params(sem))(*operands)
        grid_spec = pltpu.PrefetchScalarGridSpec(num_scalar_prefetch=npre, grid=grid, in_specs=in_specs_,
                                                 out_specs=out_specs_, scratch_shapes=scratch_)
        return pl.pallas_call(fn, name=name, grid_spec=grid_spec, out_shape=out_shape_, input_output_aliases=aliases,
                              compiler_params=_params(sem))(*prefetch, *operands)

    if comm is None:
        return list(call(body, list(in_specs), list(out_specs), list(out_shape), list(scratch_shapes), {}, args)), []
    ni, no, ns, nc = len(in_specs), len(out_specs), len(scratch_shapes), len(comm.arrays)

    def wrapped(*refs):
        pre, refs = refs[:npre], refs[npre:]
        ins, cins = refs[:ni], refs[ni:ni + nc]
        outs, couts = refs[ni + nc:ni + nc + no], refs[ni + nc + no:ni + 2 * nc + no]
        scr, sems = refs[ni + 2 * nc + no:ni + 2 * nc + no + ns], refs[ni + 2 * nc + no + ns:]
        first = functools.reduce(jnp.logical_and, [pl.program_id(k) == 0 for k in range(len(grid))])
        last = functools.reduce(jnp.logical_and, [pl.program_id(k) == grid[k] - 1 for k in range(len(grid))])

        @pl.when(first)
        def _():
            comm.start(cins, couts, sems)

        body(*pre, *ins, *outs, *scr)

        @pl.when(last)
        def _():
            comm.finish(cins, couts, sems)

    res = call(wrapped, list(in_specs) + [HBM_SPEC] * nc, list(out_specs) + [HBM_SPEC] * nc,
               list(out_shape) + list(comm.out_shapes), list(scratch_shapes) + list(comm.sem_shapes),
               {npre + ni + t: no + t for t in range(nc) if comm.aliased[t]}, (*args, *comm.arrays))
    return list(res[:no]), list(res[no:])


def _chips(x, y):
    return [(1 - x, y), (x, 1 - y), (1 - x, 1 - y)]


def _gather_comm(bufs):
    n = len(bufs)

    def copies(outs, sems):
        ici_send, ici_recv, d2d_send, d2d_recv = sems
        x, y, c = lax.axis_index("x"), lax.axis_index("y"), lax.axis_index("c")

        def half(t, slot, hc):
            hr = bufs[t].shape[1] // 2
            return outs[t].at[slot, pl.ds(pl.multiple_of(hc * hr, 16), hr), :]

        def ici(t, j, slot, px, py):
            return pltpu.make_async_remote_copy(src_ref=half(t, slot, c), dst_ref=half(t, slot, c),
                                                send_sem=ici_send.at[3 * t + j], recv_sem=ici_recv.at[3 * t + j],
                                                device_id=(px, py, c), device_id_type=MESH)

        def d2d(t, j, slot, hc):
            return pltpu.make_async_remote_copy(src_ref=half(t, slot, hc), dst_ref=half(t, slot, hc),
                                                send_sem=d2d_send.at[3 * t + j], recv_sem=d2d_recv.at[3 * t + j],
                                                device_id=(x, y, 1 - c), device_id_type=MESH)

        peers = [(t, j, px, py) for t in range(n) for j, (px, py) in enumerate(_chips(x, y))]
        return ici, d2d, peers, 2 * x + y, c

    def start(ins, outs, sems):
        ici, _, peers, me, _ = copies(outs, sems)
        for t, j, px, py in peers:
            ici(t, j, me, px, py).start()

    def finish(ins, outs, sems):
        ici, d2d, peers, me, c = copies(outs, sems)
        for t, j, px, py in peers:
            ici(t, j, 2 * px + py, px, py).wait_recv()
            d2d(t, j, 2 * px + py, c).start()
        for t, j, px, py in peers:
            d2d(t, j, 2 * px + py, 1 - c).wait_recv()
        for t, j, px, py in peers:
            ici(t, j, me, px, py).wait_send()
            d2d(t, j, 2 * px + py, c).wait_send()

    return Comm(list(bufs), [SDS(b.shape, b.dtype) for b in bufs], [True] * n,
                [pltpu.SemaphoreType.DMA((3 * n,))] * 4, start, finish)


def _sibling_exchange_comm(gs):
    n = len(gs)

    def copies(ins, outs, sems):
        x, y, c = lax.axis_index("x"), lax.axis_index("y"), lax.axis_index("c")
        cps = []
        for t in range(n):
            hr = gs[t].shape[1] // 2
            src = ins[t].at[:, pl.ds(pl.multiple_of((1 - c) * hr, SUBLANES), hr), :]
            cps.append(pltpu.make_async_remote_copy(src_ref=src, dst_ref=outs[t], send_sem=sems[0].at[t],
                                                    recv_sem=sems[1].at[t], device_id=(x, y, 1 - c),
                                                    device_id_type=MESH))
        return cps

    def start(ins, outs, sems):
        for cp in copies(ins, outs, sems):
            cp.start()

    def finish(ins, outs, sems):
        for cp in copies(ins, outs, sems):
            cp.wait()

    return Comm(list(gs), [SDS((N_CHIPS, g.shape[1] // 2, g.shape[2]), g.dtype) for g in gs], [False] * n,
                [pltpu.SemaphoreType.DMA((n,))] * 2, start, finish)


def _own_rows(c, rows):
    return pl.ds(pl.multiple_of(c * (rows // 2), 16), rows // 2)


def _chip_exchange_comm(ss):
    n = len(ss)

    def copies(ins, outs, sems):
        x, y, c = lax.axis_index("x"), lax.axis_index("y"), lax.axis_index("c")
        me = 2 * x + y

        def copy(t, j, px, py, src_slot, dst_slot):
            rows = _own_rows(c, ss[t].shape[1])
            return pltpu.make_async_remote_copy(src_ref=ins[t].at[src_slot, rows, :],
                                                dst_ref=outs[t].at[dst_slot, rows, :],
                                                send_sem=sems[0].at[3 * t + j], recv_sem=sems[1].at[3 * t + j],
                                                device_id=(px, py, c), device_id_type=MESH)

        peers = [(t, j, px, py) for t in range(n) for j, (px, py) in enumerate(_chips(x, y))]
        return copy, peers, me

    def start(ins, outs, sems):
        copy, peers, me = copies(ins, outs, sems)
        for t, j, px, py in peers:
            copy(t, j, px, py, 2 * px + py, me).start()

    def finish(ins, outs, sems):
        copy, peers, me = copies(ins, outs, sems)
        for t, j, px, py in peers:
            copy(t, j, px, py, me, 2 * px + py).wait_recv()
        for t, j, px, py in peers:
            copy(t, j, px, py, 2 * px + py, me).wait_send()

    return Comm(list(ss), [SDS(s.shape, s.dtype) for s in ss], [False] * n,
                [pltpu.SemaphoreType.DMA((3 * n,))] * 2, start, finish)


def _sibling_swap_comm(rs, ss):
    n = len(rs)

    def copies(outs, sems):
        x, y, c = lax.axis_index("x"), lax.axis_index("y"), lax.axis_index("c")
        cps = []
        for t in range(n):
            rows = _own_rows(c, rs[t].shape[1])
            held = [outs[t].at[2 * px + py, rows, :] for px, py in _chips(x, y)] + [outs[n + t].at[2 * x + y, rows, :]]
            cps += [pltpu.make_async_remote_copy(src_ref=ref, dst_ref=ref, send_sem=sems[0].at[4 * t + j],
                                                 recv_sem=sems[1].at[4 * t + j], device_id=(x, y, 1 - c),
                                                 device_id_type=MESH) for j, ref in enumerate(held)]
        return cps

    def start(ins, outs, sems):
        for cp in copies(outs, sems):
            cp.start()

    def finish(ins, outs, sems):
        for cp in copies(outs, sems):
            cp.wait()

    both = list(rs) + list(ss)
    return Comm(both, [SDS(b.shape, b.dtype) for b in both], [True] * (2 * n),
                [pltpu.SemaphoreType.DMA((4 * n,))] * 2, start, finish)


def _row_tile(rows, cap=512):
    best = SUBLANES
    for tr in range(SUBLANES, min(rows, cap) + 1, SUBLANES):
        if rows % tr == 0:
            best = tr
    return best


def _add_halves(g4, r4, c_idx, name):
    _, hr, lanes = r4.shape
    tr = _row_tile(hr)
    nb = hr // tr

    def body(c_ref, a_ref, b_ref, o_ref):
        o_ref[...] = (a_ref[...] + b_ref[...]).astype(o_ref.dtype)

    owned = pl.BlockSpec((None, tr, lanes), lambda q, i, c_ref: (q, c_ref[0] * nb + i, 0))
    grid_spec = pltpu.PrefetchScalarGridSpec(
        num_scalar_prefetch=1, grid=(N_CHIPS, nb),
        in_specs=[owned, pl.BlockSpec((None, tr, lanes), lambda q, i, c_ref: (q, i, 0))], out_specs=owned)
    return pl.pallas_call(body, name=name, grid_spec=grid_spec, out_shape=SDS(g4.shape, BF16),
                          compiler_params=_params(("parallel", "parallel")))(c_idx, g4, r4)


def _adamw(wf, r4, s4, mf, vf, me_idx, name, comm=None):
    rows, lanes = wf.shape
    tr = rows // 2
    assert tr % 16 == 0
    c1 = 1.0 / (1.0 - ADAM_B1 ** ADAM_STEP)
    c2 = 1.0 / (1.0 - ADAM_B2 ** ADAM_STEP)

    def body(me_ref, w_ref, a_ref, b_ref, c_ref, d_ref, own_ref, m_ref, v_ref, g_ref, delta_ref, nm_ref, nv_ref):
        own = own_ref[...].astype(F32)
        p = [jnp.where(me_ref[0] == q, own, ref[...].astype(F32)) for q, ref in enumerate((a_ref, b_ref, c_ref, d_ref))]
        gv = ((p[0] + p[1]) + p[2]) + p[3]
        m = ADAM_B1 * m_ref[...] + (1.0 - ADAM_B1) * gv
        v = ADAM_B2 * v_ref[...] + (1.0 - ADAM_B2) * (gv * gv)
        g_ref[...] = gv
        delta_ref[...] = -ADAM_LR * ((m * c1) / (jnp.sqrt(v * c2) + ADAM_EPS) + ADAM_WD * w_ref[...])
        nm_ref[...] = m
        nv_ref[...] = v

    other = lambda q: (lambda i, me_ref: (jnp.where(me_ref[0] == q, (q + 1) % N_CHIPS, q), i, 0))
    full = pl.BlockSpec((tr, lanes), lambda i, me_ref: (i, 0))
    return _hosting_call(
        body, comm, name=name, grid=(rows // tr,), prefetch=(me_idx,),
        in_specs=[full] + [pl.BlockSpec((None, tr, lanes), other(q)) for q in range(N_CHIPS)]
        + [pl.BlockSpec((None, tr, lanes), lambda i, me_ref: (me_ref[0], i, 0)), full, full],
        out_specs=[full] * 4, out_shape=[SDS((rows, lanes), F32)] * 4, scratch_shapes=[],
        args=(wf, r4, r4, r4, r4, s4, mf, vf))


BIG = ("ffn1_w_gate", "ffn1_w_up", "ffn1_w_down", "ffn2_w_gate", "ffn2_w_up", "ffn2_w_down", "w_out", "w_in")
TRANSPOSED = ("ffn1_w_gate", "ffn1_w_up", "ffn2_w_gate", "ffn2_w_up")
PACKED = ("rwkv_w2", "rwkv_a2", "rwkv_g2")
SMALL_SHAPES = {"ffn1_norm": (1, D_MODEL), "mix_norm": (1, D_MODEL), "hgrn_lb_logits": (2, W_A),
                "hgrn_out_norm": (1, W_A), "rwkv_shift_mu": (1, N_RWKV_COLS), "rwkv_w0": (1, W_B),
                "rwkv_a0": (1, W_B), "rwkv_k_k": (1, W_B), "rwkv_k_a": (1, W_B),
                "rwkv_r_k": (1, HB_HEADS, HB_DIM), "rwkv_gn_w": (1, W_B), "rwkv_gn_b": (1, W_B),
                "ffn2_norm": (1, D_MODEL), "final_norm": (D_MODEL,)}
PACK_ELEMS = sum(_numel(_shard_shape(n)) for n in PACKED) + sum(_numel(SMALL_SHAPES[n]) for n in SMALL)
PACK_ROWS = -(-PACK_ELEMS // (32 * LANES)) * 32


def _to_rows(name, shard):
    return shard[0].T if name in TRANSPOSED else shard[0]


def _from_rows(name, rows):
    return (rows.T if name in TRANSPOSED else rows)[None]


def _pack(sharded, small):
    flat = jnp.concatenate([sharded[n].reshape(-1) for n in PACKED] + [small[n].reshape(-1) for n in SMALL])
    return jnp.pad(flat, (0, PACK_ROWS * LANES - flat.shape[0])).reshape(PACK_ROWS, LANES)


def _unpack(packed):
    flat, out, off = packed.reshape(-1), {}, 0
    for n in PACKED:
        shp = _shard_shape(n)
        out[n] = flat[off:off + _numel(shp)].reshape((1,) + shp)
        off += _numel(shp)
    for n in SMALL:
        shp = SMALL_SHAPES[n]
        out[n] = flat[off:off + _numel(shp)].reshape(shp)
        off += _numel(shp)
    return out


def _quarter(full, name, q):
    shape, ax = SHARDED_SHAPES[name]
    w = shape[ax] // N_CHIPS
    return lax.slice_in_dim(full, q * w, (q + 1) * w, axis=ax)


def kernel(x, ffn1_norm, ffn1_w_gate, ffn1_w_up, ffn1_w_down, mix_norm, w_in, hgrn_lb_logits, hgrn_out_norm, rwkv_shift_mu, rwkv_w0, rwkv_w2, rwkv_a0, rwkv_a2, rwkv_g2, rwkv_k_k, rwkv_k_a, rwkv_r_k, rwkv_gn_w, rwkv_gn_b, w_out, ffn2_norm, ffn2_w_gate, ffn2_w_up, ffn2_w_down, final_norm, loss_target, m_ffn1_norm, m_ffn1_w_gate, m_ffn1_w_up, m_ffn1_w_down, m_mix_norm, m_w_in, m_hgrn_lb_logits, m_hgrn_out_norm, m_rwkv_shift_mu, m_rwkv_w0, m_rwkv_w2, m_rwkv_a0, m_rwkv_a2, m_rwkv_g2, m_rwkv_k_k, m_rwkv_k_a, m_rwkv_r_k, m_rwkv_gn_w, m_rwkv_gn_b, m_w_out, m_ffn2_norm, m_ffn2_w_gate, m_ffn2_w_up, m_ffn2_w_down, m_final_norm, v_ffn1_norm, v_ffn1_w_gate, v_ffn1_w_up, v_ffn1_w_down, v_mix_norm, v_w_in, v_hgrn_lb_logits, v_hgrn_out_norm, v_rwkv_shift_mu, v_rwkv_w0, v_rwkv_w2, v_rwkv_a0, v_rwkv_a2, v_rwkv_g2, v_rwkv_k_k, v_rwkv_k_a, v_rwkv_r_k, v_rwkv_gn_w, v_rwkv_gn_b, v_w_out, v_ffn2_norm, v_ffn2_w_gate, v_ffn2_w_up, v_ffn2_w_down, v_final_norm):
    args = dict(locals())
    wts = {n: args[n] for n in ALL_WEIGHTS}
    moms = {n: args["m_" + n] for n in ALL_WEIGHTS}
    vars_ = {n: args["v_" + n] for n in ALL_WEIGHTS}

    me = 2 * lax.axis_index("x") + lax.axis_index("y")
    c_idx = lax.axis_index("c").astype(jnp.int32).reshape(1)
    me_idx = me.astype(jnp.int32).reshape(1)
    shard_of = {n: _to_rows(n, wts[n]).astype(BF16) for n in BIG}
    shard_of["packed"] = _pack(wts, {n: wts[n] for n in SMALL}).astype(BF16)
    group = {"ffn1": BIG[0:3], "ffn2": BIG[3:6]}

    def slot_bufs(names):
        return [lax.dynamic_update_slice(lax.empty((N_CHIPS,) + shard_of[n].shape, BF16), shard_of[n][None],
                                         (me, 0, 0)) for n in names]

    def ffn_weights(tag, gathered):
        return {f"{tag}_wgt": gathered[0].reshape(D_FF, D_MODEL), f"{tag}_wut": gathered[1].reshape(D_FF, D_MODEL),
                f"{tag}_wd": gathered[2].reshape(D_FF, D_MODEL)}

    def w_in_weights(gathered):
        w_in_full = jnp.concatenate([gathered[0][q] for q in range(N_CHIPS)], axis=1)
        return {"w_in_h": w_in_full[:, :N_HGRN_COLS],
                "w_in_r": jnp.pad(w_in_full[:, N_HGRN_COLS:], ((0, 0), (0, N_RWKV_PAD - N_RWKV_COLS)))}

    def mixer_weights(gathered):
        w_out_full = gathered[0].reshape(D_MODEL, D_MODEL)
        packs = gathered[1].reshape(N_CHIPS, PACK_ROWS * LANES)
        full, off = {}, 0
        for n in PACKED:
            shp = _shard_shape(n)
            full[n] = jnp.concatenate([packs[q, off:off + _numel(shp)].reshape(shp) for q in range(N_CHIPS)], axis=1)
            off += _numel(shp)
        zrow = lambda nrow: jnp.zeros((nrow, W_B), BF16)
        return {"w_out": w_out_full,
                "w2_pad": jnp.concatenate([full["rwkv_w2"], zrow(LORA_PAD - 32)], axis=0),
                "a2_pad": jnp.concatenate([zrow(32), full["rwkv_a2"], zrow(LORA_PAD - 64)], axis=0),
                "g2_pad": jnp.concatenate([zrow(64), full["rwkv_g2"], zrow(LORA_PAD - 160)], axis=0)}

    plan = _Plan()
    w = {}
    plan.carry("ffn1_rms", lambda g: _gather_comm(slot_bufs(group["ffn1"][:2])),
               lambda res, w_: w_.update({"ffn1_wgt": res[0].reshape(D_FF, D_MODEL),
                                          "ffn1_wut": res[1].reshape(D_FF, D_MODEL)}))

    def after_gate_up(res, w_):
        w_["ffn1_wd"] = res[0].reshape(D_FF, D_MODEL)
        w_.update(w_in_weights(res[1:]))

    plan.carry("ffn1_gate_up", lambda g: _gather_comm(slot_bufs(("ffn1_w_down", "w_in"))), after_gate_up)
    plan.carry("ffn1_down", lambda g: _gather_comm(slot_bufs(("w_out", "packed"))),
               lambda res, w_: w_.update(mixer_weights(res)))
    plan.carry("rwkv_fwd", lambda g: _gather_comm(slot_bufs(group["ffn2"])),
               lambda res, w_: w_.update(ffn_weights("ffn2", res)))
    w["ffn1_norm"], w["ffn2_norm"] = ffn1_norm, ffn2_norm
    w["mix_norm"] = mix_norm
    w["lb0"], w["lb1"] = hgrn_lb_logits[0:1], hgrn_lb_logits[1:2]
    w["hgrn_out_norm"] = hgrn_out_norm
    w["mu_pad"] = jnp.pad(rwkv_shift_mu, ((0, 0), (0, N_RWKV_PAD - N_RWKV_COLS)))
    for n in ("rwkv_w0", "rwkv_a0", "rwkv_k_k", "rwkv_k_a", "rwkv_gn_w", "rwkv_gn_b"):
        w[n] = wts[n]
    w["rwkv_r_k"] = rwkv_r_k.reshape(1, W_B)
    w["final_norm"] = final_norm.reshape(1, D_MODEL)

    def swap_comm(names):
        return _sibling_swap_comm([early[n][0] for n in names], [early[n][1] for n in names])

    def after_swap(names):
        return lambda res, w_: swapped.update(zip(names, zip(res[:len(names)], res[len(names):])))

    early, swapped = {}, {}

    def reduce_early(names, grads_of, sibling_host, chips_host, swap_host):
        def sibling_comm(g):
            early[names, "gs"] = grads_of(g)
            return _sibling_exchange_comm(early[names, "gs"])

        def after_sibling(res, w_):
            early[names, "s4"] = [_add_halves(gt, rt, c_idx, f"grad_add_halves_{n}")
                                  for gt, rt, n in zip(early[names, "gs"], res, names)]

        plan.carry(sibling_host, sibling_comm, after_sibling)
        plan.carry(chips_host, lambda g: _chip_exchange_comm(early[names, "s4"]),
                   lambda res, w_: early.update(zip(names, zip(res, early[names, "s4"]))))
        if swap_host:
            plan.carry(swap_host, lambda g: swap_comm(names), after_swap(names))

    def proj_grads(g):
        g_w_in = jnp.concatenate([g["w_in_h"], g["w_in_r"][:, :N_RWKV_COLS]], axis=1)
        return [g["w_out"].reshape(N_CHIPS, -1, D_MODEL),
                jnp.stack([_quarter(g_w_in, "w_in", q) for q in range(N_CHIPS)])]

    rows_of = lambda keys: (lambda g: [g[k].reshape(N_CHIPS, -1, D_MODEL) for k in keys])
    reduce_early(group["ffn2"], rows_of(("ffn2_wgt", "ffn2_wut", "ffn2_wd")), "hgrn_bwd", "rwkv_bwd", "rwkv_prep_bwd")
    reduce_early(("w_out", "w_in"), proj_grads, "mix_drms", "ffn1_dact", "ffn1_dwg")
    reduce_early(("ffn1_w_down",), rows_of(("ffn1_wd",)), "ffn1_dwg", "ffn1_dwu", "ffn1_dh_g")
    reduce_early(("ffn1_w_gate",), rows_of(("ffn1_wgt",)), "ffn1_dwu", "ffn1_dh_g", "ffn1_dh_u")
    reduce_early(("ffn1_w_up",), rows_of(("ffn1_wut",)), "ffn1_dh_g", "ffn1_dh_u", None)
    loss_slab, grad_x, g = _local_step(x[0], loss_target[0], w, plan)
    loss = lax.psum(loss_slab[0, 0], ("x", "y", "c"))

    gfull = {
        "rwkv_w2": g["w2_pad"][0:32], "rwkv_a2": g["a2_pad"][32:64], "rwkv_g2": g["g2_pad"][64:160],
    }
    gsmall = {
        "ffn1_norm": g["ffn1_norm"], "mix_norm": g["mix_norm"],
        "hgrn_lb_logits": jnp.concatenate([g["lb0"], g["lb1"]], axis=0), "hgrn_out_norm": g["hgrn_out_norm"],
        "rwkv_shift_mu": g["mu_pad"][:, :N_RWKV_COLS], "rwkv_w0": g["rwkv_w0"], "rwkv_a0": g["rwkv_a0"],
        "rwkv_k_k": g["rwkv_k_k"], "rwkv_k_a": g["rwkv_k_a"], "rwkv_r_k": g["rwkv_r_k"],
        "rwkv_gn_w": g["rwkv_gn_w"], "rwkv_gn_b": g["rwkv_gn_b"], "ffn2_norm": g["ffn2_norm"],
        "final_norm": g["final_norm"],
    }
    packed = jnp.stack([_pack({n: _quarter(gfull[n], n, q) for n in PACKED}, gsmall) for q in range(N_CHIPS)])
    names = list(BIG) + ["packed"]

    def rows_list(d):
        return [_to_rows(n, d[n]) for n in BIG] + [_pack(d, {n: d[n] for n in SMALL})]

    state = dict(zip(names, zip(rows_list(wts), rows_list(moms), rows_list(vars_))))
    outs = {}

    def adamw(n, comm=None):
        wt, mt, vt = state[n]
        outs[n], carried = _adamw(wt, *swapped[n], mt, vt, me_idx, f"adamw_{n}", comm=comm)
        return carried

    r1, = adamw("ffn2_w_gate", _sibling_exchange_comm([packed]))
    s4 = _add_halves(packed, r1, c_idx, "grad_add_halves_packed")
    r2, = adamw("ffn2_w_up", _chip_exchange_comm([s4]))
    early["packed"] = (r2, s4)
    last = ["ffn1_w_up", "packed"]
    after_swap(last)(adamw("ffn2_w_down", swap_comm(last)), w)
    for n in names:
        if n not in outs:
            adamw(n)
    results = []
    for k in range(4):
        per = [outs[n][k] for n in names]
        d = {n: _from_rows(n, z) for n, z in zip(BIG, per[:-1])}
        d.update(_unpack(per[-1]))
        results.append(d)
    return (loss, grad_x[None], *[r[n] for r in results for n in ALL_WEIGHTS])
```

```python
import collections
import functools

import jax
import jax.numpy as jnp
from jax import lax
from jax.experimental import pallas as pl
from jax.experimental.pallas import tpu as pltpu

F32 = jnp.float32
BF16 = jnp.bfloat16
SDS = jax.ShapeDtypeStruct
MESH = pl.DeviceIdType.MESH

D_MODEL = 1024
D_FF = 2816
W_A = 512
W_B = 512
HA_HEADS, HA_DIM = 4, 128
HB_HEADS, HB_DIM = 8, 64
HGRN_CHUNK = 64
HGRN_GROUP = 8
RWKV_CHUNK = 16
RWKV_GROUP = 8
N_HGRN_COLS = 4 * W_A
N_RWKV_COLS = 3 * W_B + 32 + 32 + 96
N_RWKV_PAD = 1792
LORA_PAD = 256
NORM_EPS = 1e-6
RWKV_GN_EPS = 64e-5
L2_EPS = 1e-12
ADAM_LR, ADAM_B1, ADAM_B2, ADAM_EPS, ADAM_WD, ADAM_STEP = 0.001, 0.9, 0.999, 1e-8, 0.01, 10

N_CHIPS = 4
VMEM_LIMIT_V7X = 56 * 1024 * 1024
LANES = 1024

SHARDED_SHAPES = {
    "ffn1_w_gate": ((D_MODEL, D_FF), 1), "ffn1_w_up": ((D_MODEL, D_FF), 1), "ffn1_w_down": ((D_FF, D_MODEL), 0),
    "w_in": ((D_MODEL, N_HGRN_COLS + N_RWKV_COLS), 1), "rwkv_w2": ((32, W_B), 1), "rwkv_a2": ((32, W_B), 1),
    "rwkv_g2": ((96, W_B), 1), "w_out": ((D_MODEL, D_MODEL), 0),
    "ffn2_w_gate": ((D_MODEL, D_FF), 1), "ffn2_w_up": ((D_MODEL, D_FF), 1), "ffn2_w_down": ((D_FF, D_MODEL), 0),
}
SMALL = ("ffn1_norm", "mix_norm", "hgrn_lb_logits", "hgrn_out_norm", "rwkv_shift_mu", "rwkv_w0", "rwkv_a0",
         "rwkv_k_k", "rwkv_k_a", "rwkv_r_k", "rwkv_gn_w", "rwkv_gn_b", "ffn2_norm", "final_norm")
ALL_WEIGHTS = ("ffn1_norm", "ffn1_w_gate", "ffn1_w_up", "ffn1_w_down", "mix_norm", "w_in", "hgrn_lb_logits",
               "hgrn_out_norm", "rwkv_shift_mu", "rwkv_w0", "rwkv_w2", "rwkv_a0", "rwkv_a2", "rwkv_g2", "rwkv_k_k",
               "rwkv_k_a", "rwkv_r_k", "rwkv_gn_w", "rwkv_gn_b", "w_out", "ffn2_norm", "ffn2_w_gate", "ffn2_w_up",
               "ffn2_w_down", "final_norm")


def _shard_shape(name):
    shape, ax = SHARDED_SHAPES[name]
    return tuple(s // N_CHIPS if i == ax else s for i, s in enumerate(shape))


def _numel(shape):
    n = 1
    for s in shape:
        n *= s
    return n


def _params(sem=None):
    return pltpu.CompilerParams(dimension_semantics=sem, vmem_limit_bytes=VMEM_LIMIT_V7X)


def _split2(x):
    hi = x.astype(BF16)
    return hi, (x.astype(F32) - hi.astype(F32)).astype(BF16)


def _dg(x, y, cx, cy, hi):
    dn = (((cx,), (cy,)), ((), ()))
    dot = lambda p, q: lax.dot_general(p, q, dn, preferred_element_type=F32)
    if hi == "x3":
        (xh, xl), (yh, yl) = _split2(x), _split2(y)
        return dot(xh, yh) + (dot(xh, yl) + dot(xl, yh))
    return dot(x.astype(BF16), y.astype(BF16))


def _make_mm(hi, cotangent_forms=None):
    @jax.custom_vjp
    def nn(x, y):
        return _dg(x, y, 1, 0, hi)

    @jax.custom_vjp
    def nt(x, y):
        return _dg(x, y, 1, 1, hi)

    @jax.custom_vjp
    def tn(x, y):
        return _dg(x, y, 0, 0, hi)

    bnn, bnt, btn = cotangent_forms or (nn, nt, tn)
    nn.defvjp(lambda x, y: (nn(x, y), (x, y)), lambda r, g: (bnt(g, r[1]), btn(r[0], g)))
    nt.defvjp(lambda x, y: (nt(x, y), (x, y)), lambda r, g: (bnn(g, r[1]), btn(g, r[0])))
    tn.defvjp(lambda x, y: (tn(x, y), (x, y)), lambda r, g: (bnt(r[1], g), bnn(r[0], g)))
    return nn, nt, tn


_nn, _nt, _tn = _make_mm(False)
_nn_x3, _nt_x3, _tn_x3 = _make_mm("x3", (_nn, _nt, _tn))


def _tri_apply(x, transpose):
    c = x.shape[0]
    tri = (lax.broadcasted_iota(jnp.int32, (c, c), 1) <= lax.broadcasted_iota(jnp.int32, (c, c), 0)).astype(BF16)
    dn = (((0 if transpose else 1,), (0,)), ((), ()))
    p1, p2 = _split2(x)
    dot = lambda p: lax.dot_general(tri, p, dn, preferred_element_type=F32)
    return dot(p1) + dot(p2)


@jax.custom_vjp
def _cumsum_rows(x):
    return _tri_apply(x, False)


_cumsum_rows.defvjp(lambda x: (_tri_apply(x, False), None), lambda _, g: (_tri_apply(g, True),))


def _sigmoid(x):
    return 1.0 / (1.0 + jnp.exp(-x))


def _silu(x):
    return x * _sigmoid(x)


def _softplus(z):
    return jnp.maximum(z, 0.0) + jnp.log(1.0 + jnp.exp(-jnp.abs(z)))


def _mm(a, b, *, ta=False, tb=False, tm, tn, tk, name, out_dtype=F32, res=None, scale=None, comm=None):
    m = a.shape[1] if ta else a.shape[0]
    kdim = a.shape[0] if ta else a.shape[1]
    n = b.shape[0] if tb else b.shape[1]
    assert (b.shape[1] if tb else b.shape[0]) == kdim
    tm, tn, tk = min(tm, m), min(tn, n), min(tk, kdim)
    assert m % tm == 0 and n % tn == 0 and kdim % tk == 0, (name, m, n, kdim)
    nk = kdim // tk
    a_spec = pl.BlockSpec((tk, tm), lambda i, j, k: (k, i)) if ta else pl.BlockSpec((tm, tk), lambda i, j, k: (i, k))
    b_spec = pl.BlockSpec((tn, tk), lambda i, j, k: (j, k)) if tb else pl.BlockSpec((tk, tn), lambda i, j, k: (k, j))
    o_spec = pl.BlockSpec((tm, tn), lambda i, j, k: (i, j))
    ca, cb = (0 if ta else 1), (1 if tb else 0)

    def body(*refs):
        if res is not None:
            a_ref, b_ref, r_ref, o_ref, acc_ref = refs
        else:
            a_ref, b_ref, o_ref, acc_ref = refs
        k = pl.program_id(2)

        @pl.when(k == 0)
        def _():
            acc_ref[...] = jnp.zeros_like(acc_ref)

        acc_ref[...] += _dg(a_ref[...], b_ref[...], ca, cb, False)

        @pl.when(k == nk - 1)
        def _():
            acc = acc_ref[...]
            if scale is not None:
                acc = acc * scale
            if res is not None:
                acc = r_ref[...] + acc
            o_ref[...] = acc.astype(out_dtype)

    in_specs = [a_spec, b_spec] + ([o_spec] if res is not None else [])
    args = (a, b) + ((res,) if res is not None else ())
    if comm is None:
        return pl.pallas_call(
            body, name=name, grid=(m // tm, n // tn, nk), in_specs=in_specs, out_specs=o_spec,
            out_shape=SDS((m, n), out_dtype), scratch_shapes=[pltpu.VMEM((tm, tn), F32)],
            compiler_params=_params(("parallel", "parallel", "arbitrary")))(*args)
    (out,), carried = _hosting_call(
        body, comm, name=name, grid=(m // tm, n // tn, nk), in_specs=in_specs, out_specs=[o_spec],
        out_shape=[SDS((m, n), out_dtype)], scratch_shapes=[pltpu.VMEM((tm, tn), F32)], args=args)
    return out, carried


def _mm_pair(a1, b1, a2, b2, *, tm, name):
    m, kdim = a1.shape
    n = b1.shape[1]
    tm = min(tm, m)
    assert a2.shape == a1.shape and b1.shape == b2.shape == (kdim, n) and m % tm == 0

    def body(a1_ref, b1_ref, a2_ref, b2_ref, o_ref):
        o_ref[...] = _dg(a1_ref[...], b1_ref[...], 1, 0, False) + _dg(a2_ref[...], b2_ref[...], 1, 0, False)

    a_spec = pl.BlockSpec((tm, kdim), lambda i: (i, 0))
    b_spec = pl.BlockSpec((kdim, n), lambda i: (0, 0))
    return pl.pallas_call(
        body, name=name, grid=(m // tm,), in_specs=[a_spec, b_spec, a_spec, b_spec],
        out_specs=pl.BlockSpec((tm, n), lambda i: (i, 0)), out_shape=SDS((m, n), F32),
        compiler_params=_params(("parallel",)))(a1, b1, a2, b2)


def _row_spec(x, tm, tile_of=lambda i: i):
    if isinstance(x, tuple):
        arr, w, j = x
        return arr, pl.BlockSpec((tm, w), lambda i, j=j: (tile_of(i), j))
    return x, pl.BlockSpec((tm, x.shape[1]), lambda i: (tile_of(i), 0))


def _par_spec(p):
    if isinstance(p, tuple):
        arr, w, j = p
        return arr, pl.BlockSpec((arr.shape[0], w), lambda i, j=j: (0, j))
    return p, pl.BlockSpec(p.shape, lambda i: (0, 0))


def _store_groups(refs, groups, vals):
    for ref, idxs in zip(refs, groups):
        off = 0
        for ix in idxs:
            v = vals[ix]
            ref[:, off:off + v.shape[1]] = v.astype(ref.dtype)
            off += v.shape[1]


SUBLANES = 8


def _x_plan(xs, tm, t, tile_of=lambda i: i):
    arrays, specs, plan = [], [], []
    nb = tm // SUBLANES
    for x in xs:
        if isinstance(x, tuple) and isinstance(x[0], str):
            kind, arr, w, j = x
            if kind == "prev":
                halo = lambda i, j=j: (jnp.maximum(tile_of(i) * nb - 1, 0), j)
            else:
                halo = lambda i, j=j: (jnp.minimum((tile_of(i) + 1) * nb, t // SUBLANES - 1), j)
            arrays += [arr, arr]
            specs += [pl.BlockSpec((tm, w), lambda i, j=j: (tile_of(i), j)), pl.BlockSpec((SUBLANES, w), halo)]
            plan.append((kind, 2, w))
        else:
            arr, spec = _row_spec(x, tm, tile_of)
            arrays.append(arr)
            specs.append(spec)
            plan.append(("plain", 1, spec.block_shape[1]))
    return arrays, specs, plan


def _x_vals(refs, plan, tm, nt, tile_of=lambda i: i):
    vals, k = [], 0
    i = tile_of(pl.program_id(0))
    rows = lax.broadcasted_iota(jnp.int32, (tm, 1), 0)
    for kind, n, _ in plan:
        main = refs[k][...].astype(F32)
        if kind == "prev":
            edge = jnp.where(i == 0, 0.0, refs[k + 1][SUBLANES - 1:SUBLANES, :].astype(F32))
            main = jnp.where(rows == 0, edge, pltpu.roll(main, 1, 0))
        elif kind == "next":
            edge = jnp.where(i == nt - 1, 0.0, refs[k + 1][0:1, :].astype(F32))
            main = jnp.where(rows == tm - 1, edge, pltpu.roll(main, tm - 1, 0))
        vals.append(main)
        k += n
    return vals


def _tile_rows(xs, tm):
    arr = xs[0]
    if isinstance(arr, tuple):
        arr = arr[1] if isinstance(arr[0], str) else arr[0]
    return min(tm, arr.shape[0]), arr.shape[0]


def _rowwise(f, xs, params, out_groups, out_dtypes, *, tm, name, comm=None):
    tm, t = _tile_rows(xs, tm)
    nt = t // tm
    xa, xspecs, plan = _x_plan(xs, tm, t)
    pa, pspecs = (zip(*[_par_spec(p) for p in params]) if params else ((), ()))
    nxr, npar = len(xa), len(pa)
    x_sds = [SDS((tm, w), F32) for _, _, w in plan]
    p_sds = [SDS(s.block_shape, F32) for s in pspecs]
    outs_sds = jax.eval_shape(lambda *vals: f(*vals), *x_sds, *p_sds)
    widths = [sum(outs_sds[ix].shape[1] for ix in idxs) for idxs in out_groups]

    def body(*refs):
        vals = _x_vals(refs[:nxr], plan, tm, nt) + [r[...].astype(F32) for r in refs[nxr:nxr + npar]]
        outs = f(*vals)
        _store_groups(refs[nxr + npar:], out_groups, outs)

    res, carried = _hosting_call(
        body, comm, name=name, grid=(nt,), in_specs=list(xspecs) + list(pspecs),
        out_specs=[pl.BlockSpec((tm, w), lambda i: (i, 0)) for w in widths],
        out_shape=[SDS((t, w), dt) for w, dt in zip(widths, out_dtypes)], scratch_shapes=[], args=(*xa, *pa))
    return res if comm is None else (res, carried)


def _rowwise_bwd(f, xs, params, cots, *, x_grad, p_grad, dx_groups, dx_dtypes, tm, name, extra=None, comm=None,
                 fold_next=None):
    tm, t = _tile_rows(xs, tm)
    nt = t // tm
    tile_of = (lambda i: nt - 1 - i) if fold_next else (lambda i: i)
    xa, xspecs, plan = _x_plan(xs, tm, t, tile_of)
    pa, pspecs = (zip(*[_par_spec(p) for p in params]) if params else ((), ()))
    ca, cspecs = zip(*[_row_spec(c, tm, tile_of) for c in cots])
    extra = extra or {}
    ekeys = sorted(extra)
    ea, especs = (zip(*[_row_spec(extra[k], tm, tile_of) for k in ekeys]) if ekeys else ((), ()))
    nx, nxr, npar, nc, ne = len(plan), len(xa), len(pa), len(ca), len(ea)
    gx = [i for i in range(nx) if x_grad[i]]
    gp = [i for i in range(npar) if p_grad[i]]
    all_widths = [sum(plan[gx[ix]][2] for ix in idxs) for idxs in dx_groups]
    emitted = [k for k in range(len(dx_groups)) if not (fold_next and k == fold_next[1])]
    widths = [all_widths[k] for k in emitted]
    ng = len(emitted)

    def body(*refs):
        ins = refs[:nxr + npar + nc + ne]
        outs = refs[nxr + npar + nc + ne:]
        vals = _x_vals(ins[:nxr], plan, tm, nt, tile_of) + [r[...].astype(F32) for r in ins[nxr:nxr + npar]]
        cvals = tuple(r[...].astype(F32) for r in ins[nxr + npar:nxr + npar + nc])
        evals = [r[...].astype(F32) for r in ins[nxr + npar + nc:]]
        diff_idx = gx + [nx + i for i in gp]

        def g(*dargs):
            full = list(vals)
            for ix, v in zip(diff_idx, dargs):
                full[ix] = v
            return tuple(f(*full))

        _, vjp = jax.vjp(g, *[vals[ix] for ix in diff_idx])
        grads = vjp(cvals)
        dxs = list(grads[:len(gx)])
        for k, ev in zip(ekeys, evals):
            dxs[k] = dxs[k] + ev
        _store_groups(outs[:ng], [dx_groups[k] for k in emitted], dxs)
        i = pl.program_id(0)
        if fold_next:
            main_ref, carry_ref = outs[emitted.index(fold_next[0])], refs[-1]
            rows = lax.broadcasted_iota(jnp.int32, (tm, 1), 0)
            off = 0
            for ix in dx_groups[fold_next[1]]:
                piece = dxs[ix]
                cols = slice(off, off + piece.shape[1])
                edge = jnp.where(i == 0, 0.0, carry_ref[0:1, cols])
                main_ref[:, cols] += jnp.where(rows == tm - 1, edge, pltpu.roll(piece, tm - 1, 0))
                carry_ref[:, cols] = piece[:SUBLANES]
                off += piece.shape[1]
        for ref, gval in zip(outs[ng:ng + len(gp)], grads[len(gx):]):
            @pl.when(i == 0)
            def _(ref=ref):
                ref[...] = jnp.zeros_like(ref)
            ref[...] += gval

    dp_specs = [pl.BlockSpec(pspecs[i].block_shape, lambda i: (0, 0)) for i in gp]
    dp_shapes = [SDS(pspecs[i].block_shape, F32) for i in gp]
    scratch = [pltpu.VMEM((SUBLANES, all_widths[fold_next[1]]), F32)] if fold_next else []
    res, carried = _hosting_call(
        body, comm, name=name, grid=(nt,), in_specs=list(xspecs) + list(pspecs) + list(cspecs) + list(especs),
        out_specs=[pl.BlockSpec((tm, w), lambda i: (tile_of(i), 0)) for w in widths] + dp_specs,
        out_shape=[SDS((t, w), dt) for w, dt in zip(widths, dx_dtypes)] + dp_shapes, scratch_shapes=scratch,
        args=(*xa, *pa, *ca, *ea))
    return res if comm is None else (res, carried)


def _rms_f(x, g):
    return (x * lax.rsqrt(jnp.mean(x * x, axis=-1, keepdims=True) + NORM_EPS) * g,)


def _group_sum_impl(x, ones_bd):
    p1, p2 = _split2(x)
    dot = lambda p: lax.dot_general(p, ones_bd.astype(BF16), (((1,), (0,)), ((), ())), preferred_element_type=F32)
    return dot(p1) + dot(p2)


@jax.custom_vjp
def _group_sum(x, ones_bd):
    return _group_sum_impl(x, ones_bd)


_group_sum.defvjp(lambda x, o: (_group_sum_impl(x, o), o),
                  lambda o, g: (_group_sum_impl(g, o), jnp.zeros_like(o)))


def _rwkv_prep_f(r, k, v, lo, rp, kp, vp, lop, mu_r, mu_k, mu_v, mu_lo, w0, w2p, a0, a2p, g2p, k_k, k_a, ones_bd):
    r = r + mu_r * (rp - r)
    k = k + mu_k * (kp - k)
    v = v + mu_v * (vp - v)
    lo = lo + mu_lo * (lop - lo)
    w_log = -_softplus(-(w0 + _nn(jnp.tanh(lo), w2p))) - 0.5
    lw = -jnp.exp(w_log)
    a_g = _sigmoid(a0 + _nn(lo, a2p))
    g = _nn(_sigmoid(lo), g2p)
    kk = k * k_k
    kk = kk / jnp.maximum(jnp.sqrt(_group_sum(kk * kk, ones_bd)), L2_EPS)
    k2 = k * (1.0 + (a_g - 1.0) * k_a)
    return r, lw, k2, v, -kk, kk * a_g, g


def _rwkv_post_f(y, r, k2, v, g, r_k, gn_w, gn_b, ones_bd):
    inv_n = 1.0 / HB_DIM
    mean = _group_sum(y, ones_bd) * inv_n
    yc = y - mean
    var = _group_sum(yc * yc, ones_bd) * inv_n
    yn = yc * lax.rsqrt(var + RWKV_GN_EPS) * gn_w + gn_b
    bonus = _group_sum(r * k2 * r_k, ones_bd) * v
    return ((yn + bonus) * g,)


def _tri(c, strict=False):
    ii = lax.broadcasted_iota(jnp.int32, (c, c), 0)
    jj = lax.broadcasted_iota(jnp.int32, (c, c), 1)
    return (jj < ii) if strict else (jj <= ii)


def _hgrn_step(st0, q_a, f_a, i_a, g_a, l0, l1, onorm):
    nh, nj = len(q_a), len(q_a[0])
    c = q_a[0][0].shape[0]
    combos = [(j, h) for j in range(nj) for h in range(nh)]
    every = lambda fn: {q: fn(q) for q in combos}
    at_ = lambda d: (lambda q: d[q[1]][q[0]])
    qa_, fa_, ia_, ga_ = (at_(z) for z in (q_a, f_a, i_a, g_a))
    incl = _tri(c)
    rows = lax.broadcasted_iota(jnp.int32, (c, 1), 0)
    lb = []
    for h in range(nh):
        mx = jnp.maximum(l0[h], l1[h])
        e0, e1 = jnp.exp(l0[h] - mx), jnp.exp(l1[h] - mx)
        lb.append(e0 / (e0 + e1))
    forget = every(lambda q: lb[q[1]] + (1.0 - lb[q[1]]) * _sigmoid(fa_(q)))
    qs = every(lambda q: _silu(qa_(q)))
    kk = every(lambda q: 1.0 - forget[q])
    lf = every(lambda q: jnp.log(forget[q]))
    bcum = every(lambda q: _cumsum_rows(lf[q]))
    bref = every(lambda q: jnp.sum(jnp.where(rows <= c // 2, lf[q], 0.0), axis=0, keepdims=True))
    blast = every(lambda q: jnp.sum(lf[q], axis=0, keepdims=True))
    scores = every(lambda q: jnp.where(incl, _nt(qs[q] * jnp.exp(bcum[q] - bref[q]),
                                                 kk[q] * jnp.exp(bref[q] - bcum[q])), 0.0))
    intra = every(lambda q: _nn(scores[q], ia_(q)))
    qb = every(lambda q: qs[q] * jnp.exp(bcum[q]))
    upd = every(lambda q: _tn(ia_(q), kk[q] * jnp.exp(blast[q] - bcum[q])))
    dec = every(lambda q: jnp.exp(blast[q]))
    st = list(st0)
    o = {}
    for j in range(nj):
        for h in range(nh):
            o[(j, h)] = intra[(j, h)] + _nt(qb[(j, h)], st[h])
        st = [st[h] * dec[(j, h)] + upd[(j, h)] for h in range(nh)]
    out = every(lambda q: o[q] * lax.rsqrt(jnp.mean(o[q] * o[q], axis=-1, keepdims=True) + NORM_EPS)
                * onorm[q[1]] * _silu(ga_(q)))
    return [[out[(j, h)] for j in range(nj)] for h in range(nh)], st


def _hgrn_blocks(ref, nj, c):
    return [[ref[j * c:(j + 1) * c, h * HA_DIM:(h + 1) * HA_DIM] for j in range(nj)] for h in range(HA_HEADS)]


def _hgrn_cols(ref):
    return [ref[:, h * HA_DIM:(h + 1) * HA_DIM] for h in range(HA_HEADS)]


def _hgrn_fwd(p_h, l0, l1, onorm):
    t = p_h.shape[0]
    cc, nj = HGRN_CHUNK, HGRN_GROUP
    c = cc * nj
    n = t // c

    def body(q_ref, f_ref, i_ref, g_ref, l0_ref, l1_ref, on_ref, o_ref, hs_ref, st_ref):
        @pl.when(pl.program_id(0) == 0)
        def _():
            st_ref[...] = jnp.zeros_like(st_ref)

        hs_ref[0] = st_ref[...]
        o, st1 = _hgrn_step([st_ref[h] for h in range(HA_HEADS)],
                            *[_hgrn_blocks(ref, nj, cc) for ref in (q_ref, f_ref, i_ref, g_ref)],
                            _hgrn_cols(l0_ref), _hgrn_cols(l1_ref), _hgrn_cols(on_ref))
        for h in range(HA_HEADS):
            for j in range(nj):
                o_ref[j * cc:(j + 1) * cc, h * HA_DIM:(h + 1) * HA_DIM] = o[h][j]
            st_ref[h] = st1[h]

    col = lambda j: pl.BlockSpec((c, W_A), lambda i, j=j: (i, j))
    par = pl.BlockSpec((1, W_A), lambda i: (0, 0))
    return pl.pallas_call(
        body, name="hgrn_fwd", grid=(n,), in_specs=[col(0), col(1), col(2), col(3), par, par, par],
        out_specs=[pl.BlockSpec((c, W_A), lambda i: (i, 0)),
                   pl.BlockSpec((1, HA_HEADS, HA_DIM, HA_DIM), lambda i: (i, 0, 0, 0))],
        out_shape=[SDS((t, W_A), F32), SDS((n, HA_HEADS, HA_DIM, HA_DIM), F32)],
        scratch_shapes=[pltpu.VMEM((HA_HEADS, HA_DIM, HA_DIM), F32)],
        compiler_params=_params(("arbitrary",)))(p_h, p_h, p_h, p_h, l0, l1, onorm)


def _hgrn_bwd(p_h, l0, l1, onorm, hs, do, do_col, comm=None):
    t = p_h.shape[0]
    cc, nj = HGRN_CHUNK, HGRN_GROUP
    c = cc * nj
    n = t // c

    def body(q_ref, f_ref, i_ref, g_ref, l0_ref, l1_ref, on_ref, hs_ref, do_ref,
             dp_ref, dl0_ref, dl1_ref, don_ref, dst_ref):
        @pl.when(pl.program_id(0) == 0)
        def _():
            dst_ref[...] = jnp.zeros_like(dst_ref)
            dl0_ref[...] = jnp.zeros_like(dl0_ref)
            dl1_ref[...] = jnp.zeros_like(dl1_ref)
            don_ref[...] = jnp.zeros_like(don_ref)

        args = ([hs_ref[0, h] for h in range(HA_HEADS)],
                *[_hgrn_blocks(ref, nj, cc) for ref in (q_ref, f_ref, i_ref, g_ref)],
                _hgrn_cols(l0_ref), _hgrn_cols(l1_ref), _hgrn_cols(on_ref))
        _, vjp = jax.vjp(_hgrn_step, *args)
        dst0, dq, df, di, dg, dl0, dl1, don = vjp((_hgrn_blocks(do_ref, nj, cc),
                                                   [dst_ref[h] for h in range(HA_HEADS)]))
        for h in range(HA_HEADS):
            sl = slice(h * HA_DIM, (h + 1) * HA_DIM)
            for k, dv in enumerate((dq, df, di, dg)):
                for j in range(nj):
                    dp_ref[j * cc:(j + 1) * cc, k * W_A + h * HA_DIM:k * W_A + (h + 1) * HA_DIM] = dv[h][j]
            dl0_ref[:, sl] += dl0[h]
            dl1_ref[:, sl] += dl1[h]
            don_ref[:, sl] += don[h]
            dst_ref[h] = dst0[h]

    col = lambda j: pl.BlockSpec((c, W_A), lambda i, j=j: (n - 1 - i, j))
    par = pl.BlockSpec((1, W_A), lambda i: (0, 0))
    return _hosting_call(
        body, comm, name="hgrn_bwd", grid=(n,),
        in_specs=[col(0), col(1), col(2), col(3), par, par, par,
                  pl.BlockSpec((1, HA_HEADS, HA_DIM, HA_DIM), lambda i: (n - 1 - i, 0, 0, 0)),
                  pl.BlockSpec((c, W_A), lambda i: (n - 1 - i, do_col))],
        out_specs=[pl.BlockSpec((c, N_HGRN_COLS), lambda i: (n - 1 - i, 0)), par, par, par],
        out_shape=[SDS((t, N_HGRN_COLS), F32), SDS((1, W_A), F32), SDS((1, W_A), F32), SDS((1, W_A), F32)],
        scratch_shapes=[pltpu.VMEM((HA_HEADS, HA_DIM, HA_DIM), F32)],
        args=(p_h, p_h, p_h, p_h, l0, l1, onorm, hs, do))


HB_PAIRS = HB_HEADS // 2
PAIR_W = 2 * HB_DIM


def _head_lane_masks():
    lane = lax.broadcasted_iota(jnp.int32, (1, PAIR_W), 1)
    return (lane < HB_DIM).astype(F32), (lane >= HB_DIM).astype(F32)


@jax.custom_vjp
def _stack_heads(x):
    m0, m1 = _head_lane_masks()
    return jnp.concatenate([x * m0, x * m1], axis=0)


def _stack_heads_bwd(_, g):
    m0, m1 = _head_lane_masks()
    c = g.shape[0] // 2
    return (g[:c] * m0 + g[c:] * m1,)


_stack_heads.defvjp(lambda x: (_stack_heads(x), None), _stack_heads_bwd)


@jax.custom_vjp
def _unstack_heads(ys):
    c = ys.shape[0] // 2
    return ys[:c] + ys[c:]


_unstack_heads.defvjp(lambda ys: (_unstack_heads(ys), None), lambda _, g: (_stack_heads(g),))


def _same_head_block(c):
    ii = lax.broadcasted_iota(jnp.int32, (2 * c, 2 * c), 0)
    jj = lax.broadcasted_iota(jnp.int32, (2 * c, 2 * c), 1)
    same = (ii < c) == (jj < c)
    return same & (jj <= ii), same & (jj < ii), (ii == jj).astype(F32)


@jax.custom_vjp
def _rows_join(top, bottom):
    return jnp.concatenate([top, bottom], axis=0)


def _rows_join_bwd(n_top, g):
    return g[:n_top], g[n_top:]


_rows_join.defvjp(lambda top, bottom: (_rows_join(top, bottom), top.shape[0]), _rows_join_bwd)


def _rows_split_impl(x, n_top):
    return x[:n_top], x[n_top:]


_rows_split = jax.custom_vjp(_rows_split_impl, nondiff_argnums=(1,))
_rows_split.defvjp(lambda x, n_top: (_rows_split_impl(x, n_top), None),
                   lambda n_top, _, g: (jnp.concatenate([g[0], g[1]], axis=0),))


def _rwkv_step(s0, r, lw, k, v, a, b):
    npair, nj = len(r), len(r[0])
    c = r[0][0].shape[0]
    combos = [(j, p) for j in range(nj) for p in range(npair)]
    every = lambda fn: {q: fn(q) for q in combos}
    at_ = lambda d: (lambda q: d[q[1]][q[0]])
    r_, lw_, k_, v_, a_, b_ = (at_(z) for z in (r, lw, k, v, a, b))
    incl, strict, eye = _same_head_block(c)

    gam = every(lambda q: _cumsum_rows(lw_(q)))
    gtot = every(lambda q: jnp.sum(lw_(q), axis=0, keepdims=True))
    eneg = every(lambda q: jnp.exp(-gam[q]))
    edec = every(lambda q: jnp.exp(gtot[q] - gam[q]))
    at = every(lambda q: _stack_heads(a_(q) * jnp.exp(gam[q] - lw_(q))))
    rt = every(lambda q: _stack_heads(r_(q) * jnp.exp(gam[q])))
    bt = every(lambda q: _stack_heads(b_(q) * eneg[q]))
    kt = every(lambda q: _stack_heads(k_(q) * eneg[q]))
    bdec = every(lambda q: _stack_heads(b_(q) * edec[q]))
    kdec = every(lambda q: _stack_heads(k_(q) * edec[q]))
    vs = every(lambda q: _stack_heads(v_(q)))
    a_ab = every(lambda q: jnp.where(strict, _nt(at[q], bt[q]), 0.0))
    a_ak = every(lambda q: jnp.where(strict, _nt(at[q], kt[q]), 0.0))
    a_rb = every(lambda q: jnp.where(incl, _nt(rt[q], bt[q]), 0.0))
    a_rk = every(lambda q: jnp.where(incl, _nt(rt[q], kt[q]), 0.0))
    tinv = every(lambda q: eye + a_ab[q])
    pw = a_ab
    span = 2
    while span < c:
        pw = every(lambda q, pw=pw: _nn_x3(pw[q], pw[q]))
        tinv = every(lambda q, pw=pw, tinv=tinv: tinv[q] + _nn_x3(pw[q], tinv[q]))
        span *= 2
    akv = every(lambda q: _nn(a_ak[q], vs[q]))
    w1 = every(lambda q: _nn(tinv[q], at[q]))
    u0 = every(lambda q: _nn(tinv[q], akv[q]))
    wr = every(lambda q: _rows_join(w1[q], rt[q]))
    bk = every(lambda q: _rows_join(bdec[q], kdec[q]))
    yv = every(lambda q: _nn(a_rk[q], vs[q]))
    gdec = every(lambda q: jnp.exp(gtot[q]))

    s = list(s0)
    y = [[None] * nj for _ in range(npair)]
    for j in range(nj):
        both = {p: _rows_split(_nt(wr[(j, p)], s[p]), 2 * c) for p in range(npair)}
        u = {p: both[p][0] + u0[(j, p)] for p in range(npair)}
        for p in range(npair):
            y[p][j] = _unstack_heads(both[p][1] + _nn(a_rb[(j, p)], u[p]) + yv[(j, p)])
        s = [s[p] * gdec[(j, p)] + _tn(_rows_join(u[p], vs[(j, p)]), bk[(j, p)]) for p in range(npair)]
    return y, s


def _rwkv_blocks(ref, nj, c):
    return [[ref[j * c:(j + 1) * c, p * PAIR_W:(p + 1) * PAIR_W] for j in range(nj)] for p in range(HB_PAIRS)]


def _rwkv_fwd(seqs, comm=None):
    t = seqs[0].shape[0]
    c, nj = RWKV_CHUNK, RWKV_GROUP
    n = t // (c * nj)

    def body(r_ref, lw_ref, k_ref, v_ref, a_ref, b_ref, y_ref, hs_ref, st_ref):
        @pl.when(pl.program_id(0) == 0)
        def _():
            st_ref[...] = jnp.zeros_like(st_ref)

        hs_ref[0] = st_ref[...]
        s0 = [st_ref[p] for p in range(HB_PAIRS)]
        y, s1 = _rwkv_step(s0, *[_rwkv_blocks(ref, nj, c) for ref in (r_ref, lw_ref, k_ref, v_ref, a_ref, b_ref)])
        for p in range(HB_PAIRS):
            for j in range(nj):
                y_ref[j * c:(j + 1) * c, p * PAIR_W:(p + 1) * PAIR_W] = y[p][j]
            st_ref[p] = s1[p]

    seq = pl.BlockSpec((c * nj, W_B), lambda i: (i, 0))
    return _hosting_call(
        body, comm, name="rwkv_fwd", grid=(n,), in_specs=[seq] * 6,
        out_specs=[seq, pl.BlockSpec((1, HB_PAIRS, PAIR_W, PAIR_W), lambda i: (i, 0, 0, 0))],
        out_shape=[SDS((t, W_B), F32), SDS((n, HB_PAIRS, PAIR_W, PAIR_W), F32)],
        scratch_shapes=[pltpu.VMEM((HB_PAIRS, PAIR_W, PAIR_W), F32)], args=tuple(seqs))


def _rwkv_bwd(seqs, hs, dy, comm=None):
    t = seqs[0].shape[0]
    c, nj = RWKV_CHUNK, RWKV_GROUP
    n = t // (c * nj)

    def body(r_ref, lw_ref, k_ref, v_ref, a_ref, b_ref, hs_ref, dy_ref,
             dr_ref, dlw_ref, dk_ref, dv_ref, da_ref, db_ref, dst_ref):
        @pl.when(pl.program_id(0) == 0)
        def _():
            dst_ref[...] = jnp.zeros_like(dst_ref)

        s0 = [hs_ref[0, p] for p in range(HB_PAIRS)]
        seq_vals = [_rwkv_blocks(ref, nj, c) for ref in (r_ref, lw_ref, k_ref, v_ref, a_ref, b_ref)]
        _, vjp = jax.vjp(_rwkv_step, s0, *seq_vals)
        grads = vjp((_rwkv_blocks(dy_ref, nj, c), [dst_ref[p] for p in range(HB_PAIRS)]))
        for ref, gr in zip((dr_ref, dlw_ref, dk_ref, dv_ref, da_ref, db_ref), grads[1:]):
            for p in range(HB_PAIRS):
                for j in range(nj):
                    ref[j * c:(j + 1) * c, p * PAIR_W:(p + 1) * PAIR_W] = gr[p][j]
        m0, m1 = _head_lane_masks()
        rows0 = (lax.broadcasted_iota(jnp.int32, (PAIR_W, 1), 0) < HB_DIM).astype(F32)
        blocks = rows0 * m0 + (1.0 - rows0) * m1
        for p in range(HB_PAIRS):
            dst_ref[p] = grads[0][p] * blocks

    seq = pl.BlockSpec((c * nj, W_B), lambda i: (n - 1 - i, 0))
    return _hosting_call(
        body, comm, name="rwkv_bwd", grid=(n,),
        in_specs=[seq] * 6 + [pl.BlockSpec((1, HB_PAIRS, PAIR_W, PAIR_W), lambda i: (n - 1 - i, 0, 0, 0)), seq],
        out_specs=[seq] * 6, out_shape=[SDS((t, W_B), F32)] * 6,
        scratch_shapes=[pltpu.VMEM((HB_PAIRS, PAIR_W, PAIR_W), F32)], args=(*seqs, hs, dy))


def _final_loss(x3, fnorm, target, *, tm):
    t, d = x3.shape

    def body(x_ref, g_ref, t_ref, dx_ref, dg_ref, loss_ref):
        @pl.when(pl.program_id(0) == 0)
        def _():
            dg_ref[...] = jnp.zeros_like(dg_ref)
            loss_ref[...] = jnp.zeros_like(loss_ref)

        x, g = x_ref[...], g_ref[...]
        rinv = lax.rsqrt(jnp.mean(x * x, axis=-1, keepdims=True) + NORM_EPS)
        xh = x * rinv
        diff = xh * g - t_ref[...]
        loss_ref[...] += 0.5 * jnp.sum(jnp.mean(diff * diff, axis=-1, keepdims=True))
        dy = diff * (1.0 / d)
        dg_ref[...] += jnp.sum(dy * xh, axis=0, keepdims=True)
        dxh = dy * g
        dx_ref[...] = rinv * (dxh - xh * jnp.mean(dxh * xh, axis=-1, keepdims=True))

    row = pl.BlockSpec((tm, d), lambda i: (i, 0))
    return pl.pallas_call(
        body, name="final_loss", grid=(t // tm,), in_specs=[row, pl.BlockSpec((1, d), lambda i: (0, 0)), row],
        out_specs=[row, pl.BlockSpec((1, d), lambda i: (0, 0)), pl.BlockSpec((8, 128), lambda i: (0, 0))],
        out_shape=[SDS((t, d), F32), SDS((1, d), F32), SDS((8, 128), F32)],
        compiler_params=_params(("arbitrary",)))(x3, fnorm, target)


def _gate_up_act(h, wgt, wut, *, tm, tn, name, comm=None):
    t, d = h.shape
    tm = min(tm, t)

    def body(h_ref, g_ref, u_ref, a_out, u_out, act_out):
        hv = h_ref[...]
        a = _dg(hv, g_ref[...], 1, 1, False)
        u = _dg(hv, u_ref[...], 1, 1, False)
        a_out[...] = a.astype(a_out.dtype)
        u_out[...] = u.astype(u_out.dtype)
        act_out[...] = (_silu(a) * u).astype(act_out.dtype)

    wspec = pl.BlockSpec((tn, d), lambda i, j: (j, 0))
    ospec = pl.BlockSpec((tm, tn), lambda i, j: (i, j))
    return _hosting_call(
        body, comm, name=name, grid=(t // tm, D_FF // tn),
        in_specs=[pl.BlockSpec((tm, d), lambda i, j: (i, 0)), wspec, wspec], out_specs=[ospec, ospec, ospec],
        out_shape=[SDS((t, D_FF), BF16), SDS((t, D_FF), BF16), SDS((t, D_FF), BF16)], scratch_shapes=[],
        args=(h, wgt, wut))


def _dact_swiglu(dout, wd, a, u, *, tm, tn, name, comm=None):
    t, d = dout.shape
    tm = min(tm, t)

    def body(d_ref, w_ref, a_ref, u_ref, da_out, du_out):
        dact = 0.5 * _dg(d_ref[...], w_ref[...], 1, 1, False)
        av, uv = a_ref[...].astype(F32), u_ref[...].astype(F32)
        s = _sigmoid(av)
        da_out[...] = (dact * uv * (s * (1.0 + av * (1.0 - s)))).astype(da_out.dtype)
        du_out[...] = (dact * (av * s)).astype(du_out.dtype)

    tile = pl.BlockSpec((tm, tn), lambda i, j: (i, j))
    return _hosting_call(
        body, comm, name=name, grid=(t // tm, D_FF // tn),
        in_specs=[pl.BlockSpec((tm, d), lambda i, j: (i, 0)), pl.BlockSpec((tn, d), lambda i, j: (j, 0)), tile, tile],
        out_specs=[tile, tile], out_shape=[SDS((t, D_FF), BF16), SDS((t, D_FF), BF16)], scratch_shapes=[],
        args=(dout, wd, a, u))


class _Plan:
    def __init__(self):
        self.entries, self.counts = collections.defaultdict(list), {}

    def carry(self, host, comm_of, after):
        self.entries[host].append((comm_of, after))

    def comm(self, host, g):
        comms = [comm_of(g) for comm_of, _ in self.entries.get(host, [])]
        self.counts[host] = [len(c.arrays) for c in comms]
        return functools.reduce(_join_comms, comms) if comms else None

    def done(self, host, results, w):
        start = 0
        for (_, after), n in zip(self.entries.get(host, []), self.counts.get(host, [])):
            after(results[start:start + n], w)
            start += n


def _ffn_fwd(x, w, tag, plan, g):
    comm = plan.comm(f"{tag}_rms", g)
    res = _rowwise(_rms_f, [x], [w[f"{tag}_norm"]], [[0]], [BF16], tm=512, name=f"{tag}_rms", comm=comm)
    (h,), carried = res if comm is not None else (res, [])
    plan.done(f"{tag}_rms", carried, w)
    (a, u, act), carried = _gate_up_act(h, w[f"{tag}_wgt"], w[f"{tag}_wut"], tm=2048, tn=256, name=f"{tag}_gate_up",
                                        comm=plan.comm(f"{tag}_gate_up", g))
    plan.done(f"{tag}_gate_up", carried, w)
    comm = plan.comm(f"{tag}_down", g)
    out = _mm(act, w[f"{tag}_wd"], tm=1024, tn=D_MODEL, tk=D_FF, name=f"{tag}_down", res=x, scale=0.5, comm=comm)
    if comm is not None:
        out, carried = out
        plan.done(f"{tag}_down", carried, w)
    return out, (h, a, u, act)


def _ffn_bwd(dout, x, w, saved, tag, plan, g):
    h, a, u, act = saved

    def carrying(fn, host, *args, **kwargs):
        comm = plan.comm(host, g)
        res = fn(*args, name=host, comm=comm, **kwargs)
        out, carried = res if comm is not None else (res, [])
        plan.done(host, carried, w)
        return out

    (da, du), carried = _dact_swiglu(dout, w[f"{tag}_wd"], a, u, tm=2048, tn=256, name=f"{tag}_dact",
                                     comm=plan.comm(f"{tag}_dact", g))
    plan.done(f"{tag}_dact", carried, w)
    g[f"{tag}_wd"] = _mm(act, dout, ta=True, tm=D_FF // 2, tn=D_MODEL, tk=1024, name=f"{tag}_dwd", scale=0.5)
    g[f"{tag}_wgt"] = carrying(_mm, f"{tag}_dwg", da, h, ta=True, tm=D_FF // 2, tn=D_MODEL, tk=1024)
    g[f"{tag}_wut"] = carrying(_mm, f"{tag}_dwu", du, h, ta=True, tm=D_FF // 2, tn=D_MODEL, tk=1024)
    if plan.entries.get(f"{tag}_dh_g") or plan.entries.get(f"{tag}_dh_u"):
        dh = carrying(_mm, f"{tag}_dh_g", da, w[f"{tag}_wgt"], tm=1024, tn=D_MODEL, tk=D_FF)
        dh = carrying(_mm, f"{tag}_dh_u", du, w[f"{tag}_wut"], tm=1024, tn=D_MODEL, tk=D_FF, res=dh)
    else:
        dh = _mm_pair(da, w[f"{tag}_wgt"], du, w[f"{tag}_wut"], tm=512, name=f"{tag}_dh")
    dx, g[f"{tag}_norm"] = carrying(_rowwise_bwd, f"{tag}_drms", _rms_f, [x], [w[f"{tag}_norm"]], [dh],
                                    x_grad=[True], p_grad=[True], dx_groups=[[0]], dx_dtypes=[F32], tm=512,
                                    extra={0: dout})
    return dx


def _local_step(x, target, w, plan=None):
    plan = plan or _Plan()
    ones_bd = jnp.kron(jnp.eye(HB_HEADS, dtype=F32), jnp.ones((HB_DIM, HB_DIM), F32))
    g = {}
    x1, ffn1_saved = _ffn_fwd(x, w, "ffn1", plan, g)
    hm, = _rowwise(_rms_f, [x1], [w["mix_norm"]], [[0]], [BF16], tm=512, name="mix_rms")
    p_h = _mm(hm, w["w_in_h"], tm=2048, tn=256, tk=D_MODEL, name="inproj_h")
    p_r = _mm(hm, w["w_in_r"], tm=2048, tn=256, tk=D_MODEL, name="inproj_r")
    o_a, hgrn_states = _hgrn_fwd(p_h, w["lb0"], w["lb1"], w["hgrn_out_norm"])

    mu = w["mu_pad"]
    prep_xs = [(p_r, W_B, 0), (p_r, W_B, 1), (p_r, W_B, 2), (p_r, LORA_PAD, 6),
               ("prev", p_r, W_B, 0), ("prev", p_r, W_B, 1), ("prev", p_r, W_B, 2), ("prev", p_r, LORA_PAD, 6)]
    prep_ps = [(mu, W_B, 0), (mu, W_B, 1), (mu, W_B, 2), (mu, LORA_PAD, 6), w["rwkv_w0"], w["w2_pad"], w["rwkv_a0"],
               w["a2_pad"], w["g2_pad"], w["rwkv_k_k"], w["rwkv_k_a"], ones_bd]
    prep_f = _rwkv_prep_f
    r, lw, k2, v, a_vec, b_vec, gate = _rowwise(prep_f, prep_xs, prep_ps, [[0], [1], [2], [3], [4], [5], [6]],
                                                [F32] * 7, tm=256, name="rwkv_prep")
    seqs = [r, lw, k2, v, a_vec, b_vec]
    (y, rwkv_states), carried = _rwkv_fwd(seqs, comm=plan.comm("rwkv_fwd", g))
    plan.done("rwkv_fwd", carried, w)
    post_f = _rwkv_post_f
    post_xs = [y, r, k2, v, gate]
    post_ps = [w["rwkv_r_k"], w["rwkv_gn_w"], w["rwkv_gn_b"], ones_bd]
    o, = _rowwise(lambda o_a_, *rest: (o_a_,) + tuple(post_f(*rest)), [o_a] + post_xs, post_ps, [[0, 1]], [F32],
                  tm=256, name="rwkv_post")
    x2 = _mm(o, w["w_out"], tm=2048, tn=256, tk=D_MODEL, name="outproj", res=x1)
    x3, ffn2_saved = _ffn_fwd(x2, w, "ffn2", plan, g)
    dx3, g["final_norm"], loss = _final_loss(x3, w["final_norm"], target, tm=256)

    dx2 = _ffn_bwd(dx3, x2, w, ffn2_saved, "ffn2", plan, g)
    do = _mm(dx2, w["w_out"], tb=True, tm=2048, tn=256, tk=D_MODEL, name="outproj_do")
    g["w_out"] = _mm(o, dx2, ta=True, tm=D_MODEL, tn=D_MODEL, tk=1024, name="outproj_dw")

    (dp_h, g["lb0"], g["lb1"], g["hgrn_out_norm"]), carried = _hgrn_bwd(
        p_h, w["lb0"], w["lb1"], w["hgrn_out_norm"], hgrn_states, do, 0, comm=plan.comm("hgrn_bwd", g))
    plan.done("hgrn_bwd", carried, w)
    post_out = _rowwise_bwd(post_f, post_xs, post_ps, [(do, W_B, 1)], x_grad=[True] * 5, p_grad=[True] * 3 + [False],
                            dx_groups=[[0], [1], [2], [3], [4]], dx_dtypes=[F32] * 5, tm=256, name="rwkv_post_bwd")
    dy, dr1, dk1, dv1, dgate, g["rwkv_r_k"], g["rwkv_gn_w"], g["rwkv_gn_b"] = post_out
    (dr2, dlw, dk2, dv2, da_vec, db_vec), carried = _rwkv_bwd(seqs, rwkv_states, dy, comm=plan.comm("rwkv_bwd", g))
    plan.done("rwkv_bwd", carried, w)

    def prep2_f(*vals):
        r_, lw_, k2_, v_, a_, b_, g_ = prep_f(*vals)
        return r_, lw_, k2_, v_, a_, b_, g_, r_, k2_, v_

    prep_comm = plan.comm("rwkv_prep_bwd", g)
    prep_out = _rowwise_bwd(prep2_f, prep_xs, prep_ps, [dr2, dlw, dk2, dv2, da_vec, db_vec, dgate, dr1, dk1, dv1],
                            x_grad=[True] * 8, p_grad=[True] * 11 + [False], dx_groups=[[0, 1, 2, 3], [4, 5, 6, 7]],
                            dx_dtypes=[F32], tm=256, name="rwkv_prep_bwd", fold_next=(0, 1), comm=prep_comm)
    prep_out, carried = prep_out if prep_comm is not None else (prep_out, [])
    plan.done("rwkv_prep_bwd", carried, w)
    dp_r = prep_out[0]
    (dmu_r, dmu_k, dmu_v, dmu_lo, g["rwkv_w0"], g["w2_pad"], g["rwkv_a0"], g["a2_pad"], g["g2_pad"],
     g["rwkv_k_k"], g["rwkv_k_a"]) = prep_out[1:]
    g["mu_pad"] = jnp.concatenate([dmu_r, dmu_k, dmu_v, dmu_lo], axis=1)
    dhm = _mm(dp_h, w["w_in_h"], tb=True, tm=1024, tn=D_MODEL, tk=N_HGRN_COLS, name="inproj_dh_h")
    dhm = _mm(dp_r, w["w_in_r"], tb=True, tm=1024, tn=D_MODEL, tk=N_RWKV_PAD, name="inproj_dh_r", res=dhm)
    g["w_in_h"] = _mm(hm, dp_h, ta=True, tm=D_MODEL, tn=D_MODEL, tk=1024, name="inproj_dw_h")
    g["w_in_r"] = _mm(hm, dp_r, ta=True, tm=D_MODEL, tn=N_RWKV_PAD // 2, tk=1024, name="inproj_dw_r")
    mix_comm = plan.comm("mix_drms", g)
    mix_out = _rowwise_bwd(_rms_f, [x1], [w["mix_norm"]], [dhm], x_grad=[True], p_grad=[True], dx_groups=[[0]],
                           dx_dtypes=[F32], tm=512, name="mix_drms", extra={0: dx2}, comm=mix_comm)
    (dx1, g["mix_norm"]), carried = mix_out if mix_comm is not None else (mix_out, [])
    plan.done("mix_drms", carried, w)
    dx0 = _ffn_bwd(dx1, x, w, ffn1_saved, "ffn1", plan, g)
    return loss, dx0, g


HBM_SPEC = pl.BlockSpec(memory_space=pl.ANY)

Comm = collections.namedtuple("Comm", "arrays out_shapes aliased sem_shapes start finish")


def _join_comms(first, second):
    n, s = len(first.arrays), len(first.sem_shapes)

    def start(ins, outs, sems):
        first.start(ins[:n], outs[:n], sems[:s])
        second.start(ins[n:], outs[n:], sems[s:])

    def finish(ins, outs, sems):
        first.finish(ins[:n], outs[:n], sems[:s])
        second.finish(ins[n:], outs[n:], sems[s:])

    return Comm(list(first.arrays) + list(second.arrays), list(first.out_shapes) + list(second.out_shapes),
                list(first.aliased) + list(second.aliased), list(first.sem_shapes) + list(second.sem_shapes),
                start, finish)


def _run_comm(comm, name):
    n = len(comm.arrays)

    def body(*refs):
        ins, outs, sems = refs[:n], refs[n:2 * n], refs[2 * n:]
        comm.start(ins, outs, sems)
        comm.finish(ins, outs, sems)

    return pl.pallas_call(
        body, name=name, in_specs=[HBM_SPEC] * n, out_specs=[HBM_SPEC] * n, out_shape=list(comm.out_shapes),
        input_output_aliases={t: t for t in range(n) if comm.aliased[t]},
        scratch_shapes=list(comm.sem_shapes))(*comm.arrays)


def _hosting_call(body, comm, *, name, grid, in_specs, out_specs, out_shape, scratch_shapes, args):
    sem = ("arbitrary",) * len(grid)
    if comm is None:
        res = pl.pallas_call(body, name=name, grid=grid, in_specs=in_specs, out_specs=out_specs, out_shape=out_shape,
                             scratch_shapes=scratch_shapes, compiler_params=_params(sem))(*args)
        return list(res), []
    ni, no, ns, nc = len(in_specs), len(out_specs), len(scratch_shapes), len(comm.arrays)

    def wrapped(*refs):
        ins, cins = refs[:ni], refs[ni:ni + nc]
        outs, couts = refs[ni + nc:ni + nc + no], refs[ni + nc + no:ni + 2 * nc + no]
        scr, sems = refs[ni + 2 * nc + no:ni + 2 * nc + no + ns], refs[ni + 2 * nc + no + ns:]
        first = functools.reduce(jnp.logical_and, [pl.program_id(k) == 0 for k in range(len(grid))])
        last = functools.reduce(jnp.logical_and, [pl.program_id(k) == grid[k] - 1 for k in range(len(grid))])

        @pl.when(first)
        def _():
            comm.start(cins, couts, sems)

        body(*ins, *outs, *scr)

        @pl.when(last)
        def _():
            comm.finish(cins, couts, sems)

    res = pl.pallas_call(
        wrapped, name=name, grid=grid, in_specs=list(in_specs) + [HBM_SPEC] * nc,
        out_specs=list(out_specs) + [HBM_SPEC] * nc, out_shape=list(out_shape) + list(comm.out_shapes),
        scratch_shapes=list(scratch_shapes) + list(comm.sem_shapes),
        input_output_aliases={ni + t: no + t for t in range(nc) if comm.aliased[t]},
        compiler_params=_params(sem))(*args, *comm.arrays)
    return list(res[:no]), list(res[no:])


def _chips(x, y):
    return [(1 - x, y), (x, 1 - y), (1 - x, 1 - y)]


def _gather_comm(bufs):
    n = len(bufs)

    def copies(outs, sems):
        ici_send, ici_recv, d2d_send, d2d_recv = sems
        x, y, c = lax.axis_index("x"), lax.axis_index("y"), lax.axis_index("c")

        def half(t, slot, hc):
            hr = bufs[t].shape[1] // 2
            return outs[t].at[slot, pl.ds(pl.multiple_of(hc * hr, 16), hr), :]

        def ici(t, j, slot, px, py):
            return pltpu.make_async_remote_copy(src_ref=half(t, slot, c), dst_ref=half(t, slot, c),
                                                send_sem=ici_send.at[3 * t + j], recv_sem=ici_recv.at[3 * t + j],
                                                device_id=(px, py, c), device_id_type=MESH)

        def d2d(t, j, slot, hc):
            return pltpu.make_async_remote_copy(src_ref=half(t, slot, hc), dst_ref=half(t, slot, hc),
                                                send_sem=d2d_send.at[3 * t + j], recv_sem=d2d_recv.at[3 * t + j],
                                                device_id=(x, y, 1 - c), device_id_type=MESH)

        peers = [(t, j, px, py) for t in range(n) for j, (px, py) in enumerate(_chips(x, y))]
        return ici, d2d, peers, 2 * x + y, c

    def start(ins, outs, sems):
        ici, _, peers, me, _ = copies(outs, sems)
        for t, j, px, py in peers:
            ici(t, j, me, px, py).start()

    def finish(ins, outs, sems):
        ici, d2d, peers, me, c = copies(outs, sems)
        for t, j, px, py in peers:
            ici(t, j, 2 * px + py, px, py).wait_recv()
            d2d(t, j, 2 * px + py, c).start()
        for t, j, px, py in peers:
            d2d(t, j, 2 * px + py, 1 - c).wait_recv()
        for t, j, px, py in peers:
            ici(t, j, me, px, py).wait_send()
            d2d(t, j, 2 * px + py, c).wait_send()

    return Comm(list(bufs), [SDS(b.shape, b.dtype) for b in bufs], [True] * n,
                [pltpu.SemaphoreType.DMA((3 * n,))] * 4, start, finish)


def _sibling_exchange_comm(gs):
    n = len(gs)

    def copies(ins, outs, sems):
        x, y, c = lax.axis_index("x"), lax.axis_index("y"), lax.axis_index("c")
        cps = []
        for t in range(n):
            hr = gs[t].shape[1] // 2
            src = ins[t].at[:, pl.ds(pl.multiple_of((1 - c) * hr, SUBLANES), hr), :]
            cps.append(pltpu.make_async_remote_copy(src_ref=src, dst_ref=outs[t], send_sem=sems[0].at[t],
                                                    recv_sem=sems[1].at[t], device_id=(x, y, 1 - c),
                                                    device_id_type=MESH))
        return cps

    def start(ins, outs, sems):
        for cp in copies(ins, outs, sems):
            cp.start()

    def finish(ins, outs, sems):
        for cp in copies(ins, outs, sems):
            cp.wait()

    return Comm(list(gs), [SDS((N_CHIPS, g.shape[1] // 2, g.shape[2]), g.dtype) for g in gs], [False] * n,
                [pltpu.SemaphoreType.DMA((n,))] * 2, start, finish)


def _own_rows(c, rows):
    return pl.ds(pl.multiple_of(c * (rows // 2), 16), rows // 2)


def _chip_exchange_comm(ss):
    n = len(ss)

    def copies(ins, outs, sems):
        x, y, c = lax.axis_index("x"), lax.axis_index("y"), lax.axis_index("c")
        me = 2 * x + y

        def copy(t, j, px, py, src_slot, dst_slot):
            rows = _own_rows(c, ss[t].shape[1])
            return pltpu.make_async_remote_copy(src_ref=ins[t].at[src_slot, rows, :],
                                                dst_ref=outs[t].at[dst_slot, rows, :],
                                                send_sem=sems[0].at[3 * t + j], recv_sem=sems[1].at[3 * t + j],
                                                device_id=(px, py, c), device_id_type=MESH)

        peers = [(t, j, px, py) for t in range(n) for j, (px, py) in enumerate(_chips(x, y))]
        return copy, peers, me

    def start(ins, outs, sems):
        copy, peers, me = copies(ins, outs, sems)
        for t, j, px, py in peers:
            copy(t, j, px, py, 2 * px + py, me).start()

    def finish(ins, outs, sems):
        copy, peers, me = copies(ins, outs, sems)
        for t, j, px, py in peers:
            copy(t, j, px, py, me, 2 * px + py).wait_recv()
        for t, j, px, py in peers:
            copy(t, j, px, py, 2 * px + py, me).wait_send()

    return Comm(list(ss), [SDS(s.shape, s.dtype) for s in ss], [False] * n,
                [pltpu.SemaphoreType.DMA((3 * n,))] * 2, start, finish)


def _sibling_swap_comm(rs, ss):
    n = len(rs)

    def copies(outs, sems):
        x, y, c = lax.axis_index("x"), lax.axis_index("y"), lax.axis_index("c")
        cps = []
        for t in range(n):
            rows = _own_rows(c, rs[t].shape[1])
            held = [outs[t].at[2 * px + py, rows, :] for px, py in _chips(x, y)] + [outs[n + t].at[2 * x + y, rows, :]]
            cps += [pltpu.make_async_remote_copy(src_ref=ref, dst_ref=ref, send_sem=sems[0].at[4 * t + j],
                                                 recv_sem=sems[1].at[4 * t + j], device_id=(x, y, 1 - c),
                                                 device_id_type=MESH) for j, ref in enumerate(held)]
        return cps

    def start(ins, outs, sems):
        for cp in copies(outs, sems):
            cp.start()

    def finish(ins, outs, sems):
        for cp in copies(outs, sems):
            cp.wait()

    both = list(rs) + list(ss)
    return Comm(both, [SDS(b.shape, b.dtype) for b in both], [True] * (2 * n),
                [pltpu.SemaphoreType.DMA((4 * n,))] * 2, start, finish)


def _row_tile(rows, cap=512):
    best = SUBLANES
    for tr in range(SUBLANES, min(rows, cap) + 1, SUBLANES):
        if rows % tr == 0:
            best = tr
    return best


def _add_halves(g4, r4, c_idx, name):
    _, hr, lanes = r4.shape
    tr = _row_tile(hr)
    nb = hr // tr

    def body(c_ref, a_ref, b_ref, o_ref):
        o_ref[...] = (a_ref[...] + b_ref[...]).astype(o_ref.dtype)

    pair = 2
    owned = pl.BlockSpec((pair, tr, lanes), lambda q, i, c_ref: (q, c_ref[0] * nb + i, 0))
    grid_spec = pltpu.PrefetchScalarGridSpec(
        num_scalar_prefetch=1, grid=(N_CHIPS // pair, nb),
        in_specs=[owned, pl.BlockSpec((pair, tr, lanes), lambda q, i, c_ref: (q, i, 0))], out_specs=owned)
    return pl.pallas_call(body, name=name, grid_spec=grid_spec, out_shape=SDS(g4.shape, BF16),
                          compiler_params=_params(("parallel", "parallel")))(c_idx, g4, r4)


def _adamw(wf, r4, s4, mf, vf, me_idx, name):
    rows, lanes = wf.shape
    tr = rows // 2
    assert tr % 16 == 0
    c1 = 1.0 / (1.0 - ADAM_B1 ** ADAM_STEP)
    c2 = 1.0 / (1.0 - ADAM_B2 ** ADAM_STEP)

    def body(me_ref, w_ref, a_ref, b_ref, c_ref, d_ref, own_ref, m_ref, v_ref, g_ref, delta_ref, nm_ref, nv_ref):
        own = own_ref[...].astype(F32)
        p = [jnp.where(me_ref[0] == q, own, ref[...].astype(F32)) for q, ref in enumerate((a_ref, b_ref, c_ref, d_ref))]
        gv = ((p[0] + p[1]) + p[2]) + p[3]
        m = ADAM_B1 * m_ref[...] + (1.0 - ADAM_B1) * gv
        v = ADAM_B2 * v_ref[...] + (1.0 - ADAM_B2) * (gv * gv)
        g_ref[...] = gv
        delta_ref[...] = -ADAM_LR * ((m * c1) / (jnp.sqrt(v * c2) + ADAM_EPS) + ADAM_WD * w_ref[...])
        nm_ref[...] = m
        nv_ref[...] = v

    other = lambda q: (lambda i, me_ref: (jnp.where(me_ref[0] == q, (q + 1) % N_CHIPS, q), i, 0))
    full = pl.BlockSpec((tr, lanes), lambda i, me_ref: (i, 0))
    grid_spec = pltpu.PrefetchScalarGridSpec(
        num_scalar_prefetch=1, grid=(rows // tr,),
        in_specs=[full] + [pl.BlockSpec((None, tr, lanes), other(q)) for q in range(N_CHIPS)]
        + [pl.BlockSpec((None, tr, lanes), lambda i, me_ref: (me_ref[0], i, 0)), full, full],
        out_specs=[full] * 4)
    return pl.pallas_call(body, name=name, grid_spec=grid_spec, out_shape=[SDS((rows, lanes), F32)] * 4,
                          compiler_params=_params(("parallel",)))(me_idx, wf, r4, r4, r4, r4, s4, mf, vf)


BIG = ("ffn1_w_gate", "ffn1_w_up", "ffn1_w_down", "ffn2_w_gate", "ffn2_w_up", "ffn2_w_down", "w_out", "w_in")
TRANSPOSED = ("ffn1_w_gate", "ffn1_w_up", "ffn2_w_gate", "ffn2_w_up")
PACKED = ("rwkv_w2", "rwkv_a2", "rwkv_g2")
SMALL_SHAPES = {"ffn1_norm": (1, D_MODEL), "mix_norm": (1, D_MODEL), "hgrn_lb_logits": (2, W_A),
                "hgrn_out_norm": (1, W_A), "rwkv_shift_mu": (1, N_RWKV_COLS), "rwkv_w0": (1, W_B),
                "rwkv_a0": (1, W_B), "rwkv_k_k": (1, W_B), "rwkv_k_a": (1, W_B),
                "rwkv_r_k": (1, HB_HEADS, HB_DIM), "rwkv_gn_w": (1, W_B), "rwkv_gn_b": (1, W_B),
                "ffn2_norm": (1, D_MODEL), "final_norm": (D_MODEL,)}
PACK_ELEMS = sum(_numel(_shard_shape(n)) for n in PACKED) + sum(_numel(SMALL_SHAPES[n]) for n in SMALL)
PACK_ROWS = -(-PACK_ELEMS // (32 * LANES)) * 32


def _to_rows(name, shard):
    return shard[0].T if name in TRANSPOSED else shard[0]


def _from_rows(name, rows):
    return (rows.T if name in TRANSPOSED else rows)[None]


def _pack(sharded, small):
    flat = jnp.concatenate([sharded[n].reshape(-1) for n in PACKED] + [small[n].reshape(-1) for n in SMALL])
    return jnp.pad(flat, (0, PACK_ROWS * LANES - flat.shape[0])).reshape(PACK_ROWS, LANES)


def _unpack(packed):
    flat, out, off = packed.reshape(-1), {}, 0
    for n in PACKED:
        shp = _shard_shape(n)
        out[n] = flat[off:off + _numel(shp)].reshape((1,) + shp)
        off += _numel(shp)
    for n in SMALL:
        shp = SMALL_SHAPES[n]
        out[n] = flat[off:off + _numel(shp)].reshape(shp)
        off += _numel(shp)
    return out


def _quarter(full, name, q):
    shape, ax = SHARDED_SHAPES[name]
    w = shape[ax] // N_CHIPS
    return lax.slice_in_dim(full, q * w, (q + 1) * w, axis=ax)


def kernel(x, ffn1_norm, ffn1_w_gate, ffn1_w_up, ffn1_w_down, mix_norm, w_in, hgrn_lb_logits, hgrn_out_norm, rwkv_shift_mu, rwkv_w0, rwkv_w2, rwkv_a0, rwkv_a2, rwkv_g2, rwkv_k_k, rwkv_k_a, rwkv_r_k, rwkv_gn_w, rwkv_gn_b, w_out, ffn2_norm, ffn2_w_gate, ffn2_w_up, ffn2_w_down, final_norm, loss_target, m_ffn1_norm, m_ffn1_w_gate, m_ffn1_w_up, m_ffn1_w_down, m_mix_norm, m_w_in, m_hgrn_lb_logits, m_hgrn_out_norm, m_rwkv_shift_mu, m_rwkv_w0, m_rwkv_w2, m_rwkv_a0, m_rwkv_a2, m_rwkv_g2, m_rwkv_k_k, m_rwkv_k_a, m_rwkv_r_k, m_rwkv_gn_w, m_rwkv_gn_b, m_w_out, m_ffn2_norm, m_ffn2_w_gate, m_ffn2_w_up, m_ffn2_w_down, m_final_norm, v_ffn1_norm, v_ffn1_w_gate, v_ffn1_w_up, v_ffn1_w_down, v_mix_norm, v_w_in, v_hgrn_lb_logits, v_hgrn_out_norm, v_rwkv_shift_mu, v_rwkv_w0, v_rwkv_w2, v_rwkv_a0, v_rwkv_a2, v_rwkv_g2, v_rwkv_k_k, v_rwkv_k_a, v_rwkv_r_k, v_rwkv_gn_w, v_rwkv_gn_b, v_w_out, v_ffn2_norm, v_ffn2_w_gate, v_ffn2_w_up, v_ffn2_w_down, v_final_norm):
    args = dict(locals())
    wts = {n: args[n] for n in ALL_WEIGHTS}
    moms = {n: args["m_" + n] for n in ALL_WEIGHTS}
    vars_ = {n: args["v_" + n] for n in ALL_WEIGHTS}

    me = 2 * lax.axis_index("x") + lax.axis_index("y")
    c_idx = lax.axis_index("c").astype(jnp.int32).reshape(1)
    me_idx = me.astype(jnp.int32).reshape(1)
    shard_of = {n: _to_rows(n, wts[n]).astype(BF16) for n in BIG}
    shard_of["packed"] = _pack(wts, {n: wts[n] for n in SMALL}).astype(BF16)
    group = {"ffn1": BIG[0:3], "ffn2": BIG[3:6]}

    def slot_bufs(names):
        return [lax.dynamic_update_slice(lax.empty((N_CHIPS,) + shard_of[n].shape, BF16), shard_of[n][None],
                                         (me, 0, 0)) for n in names]

    def ffn_weights(tag, gathered):
        return {f"{tag}_wgt": gathered[0].reshape(D_FF, D_MODEL), f"{tag}_wut": gathered[1].reshape(D_FF, D_MODEL),
                f"{tag}_wd": gathered[2].reshape(D_FF, D_MODEL)}

    def w_in_weights(gathered):
        w_in_full = jnp.concatenate([gathered[0][q] for q in range(N_CHIPS)], axis=1)
        return {"w_in_h": w_in_full[:, :N_HGRN_COLS],
                "w_in_r": jnp.pad(w_in_full[:, N_HGRN_COLS:], ((0, 0), (0, N_RWKV_PAD - N_RWKV_COLS)))}

    def mixer_weights(gathered):
        w_out_full = gathered[0].reshape(D_MODEL, D_MODEL)
        packs = gathered[1].reshape(N_CHIPS, PACK_ROWS * LANES)
        full, off = {}, 0
        for n in PACKED:
            shp = _shard_shape(n)
            full[n] = jnp.concatenate([packs[q, off:off + _numel(shp)].reshape(shp) for q in range(N_CHIPS)], axis=1)
            off += _numel(shp)
        zrow = lambda nrow: jnp.zeros((nrow, W_B), BF16)
        return {"w_out": w_out_full,
                "w2_pad": jnp.concatenate([full["rwkv_w2"], zrow(LORA_PAD - 32)], axis=0),
                "a2_pad": jnp.concatenate([zrow(32), full["rwkv_a2"], zrow(LORA_PAD - 64)], axis=0),
                "g2_pad": jnp.concatenate([zrow(64), full["rwkv_g2"], zrow(LORA_PAD - 160)], axis=0)}

    plan = _Plan()
    w = {}
    plan.carry("ffn1_rms", lambda g: _gather_comm(slot_bufs(group["ffn1"][:2])),
               lambda res, w_: w_.update({"ffn1_wgt": res[0].reshape(D_FF, D_MODEL),
                                          "ffn1_wut": res[1].reshape(D_FF, D_MODEL)}))

    def after_gate_up(res, w_):
        w_["ffn1_wd"] = res[0].reshape(D_FF, D_MODEL)
        w_.update(w_in_weights(res[1:]))

    plan.carry("ffn1_gate_up", lambda g: _gather_comm(slot_bufs(("ffn1_w_down", "w_in"))), after_gate_up)
    plan.carry("ffn1_down", lambda g: _gather_comm(slot_bufs(("w_out", "packed"))),
               lambda res, w_: w_.update(mixer_weights(res)))
    plan.carry("rwkv_fwd", lambda g: _gather_comm(slot_bufs(group["ffn2"])),
               lambda res, w_: w_.update(ffn_weights("ffn2", res)))
    w["ffn1_norm"], w["ffn2_norm"] = ffn1_norm, ffn2_norm
    w["mix_norm"] = mix_norm
    w["lb0"], w["lb1"] = hgrn_lb_logits[0:1], hgrn_lb_logits[1:2]
    w["hgrn_out_norm"] = hgrn_out_norm
    w["mu_pad"] = jnp.pad(rwkv_shift_mu, ((0, 0), (0, N_RWKV_PAD - N_RWKV_COLS)))
    for n in ("rwkv_w0", "rwkv_a0", "rwkv_k_k", "rwkv_k_a", "rwkv_gn_w", "rwkv_gn_b"):
        w[n] = wts[n]
    w["rwkv_r_k"] = rwkv_r_k.reshape(1, W_B)
    w["final_norm"] = final_norm.reshape(1, D_MODEL)

    def reduce_rows(names, gs):
        r1 = _run_comm(_sibling_exchange_comm(gs), "grad_sibling_exchange")
        s4 = [_add_halves(gt, rt, c_idx, f"grad_add_halves_{n}") for gt, rt, n in zip(gs, r1, names)]
        return list(zip(_run_comm(_chip_exchange_comm(s4), "grad_chip_exchange"), s4))

    def swap_comm(names):
        return _sibling_swap_comm([early[n][0] for n in names], [early[n][1] for n in names])

    def after_swap(names):
        return lambda res, w_: swapped.update(zip(names, zip(res[:len(names)], res[len(names):])))

    early, swapped = {}, {}

    def reduce_early(names, grads_of, sibling_host, chips_host, swap_host):
        def sibling_comm(g):
            early[names, "gs"] = grads_of(g)
            return _sibling_exchange_comm(early[names, "gs"])

        def after_sibling(res, w_):
            early[names, "s4"] = [_add_halves(gt, rt, c_idx, f"grad_add_halves_{n}")
                                  for gt, rt, n in zip(early[names, "gs"], res, names)]

        plan.carry(sibling_host, sibling_comm, after_sibling)
        plan.carry(chips_host, lambda g: _chip_exchange_comm(early[names, "s4"]),
                   lambda res, w_: early.update(zip(names, zip(res, early[names, "s4"]))))
        if swap_host:
            plan.carry(swap_host, lambda g: swap_comm(names), after_swap(names))

    def proj_grads(g):
        g_w_in = jnp.concatenate([g["w_in_h"], g["w_in_r"][:, :N_RWKV_COLS]], axis=1)
        return [g["w_out"].reshape(N_CHIPS, -1, D_MODEL),
                jnp.stack([_quarter(g_w_in, "w_in", q) for q in range(N_CHIPS)])]

    rows_of = lambda keys: (lambda g: [g[k].reshape(N_CHIPS, -1, D_MODEL) for k in keys])
    reduce_early(group["ffn2"], rows_of(("ffn2_wgt", "ffn2_wut", "ffn2_wd")), "hgrn_bwd", "rwkv_bwd", "rwkv_prep_bwd")
    reduce_early(("w_out", "w_in"), proj_grads, "mix_drms", "ffn1_dact", "ffn1_dwg")
    reduce_early(("ffn1_w_down",), rows_of(("ffn1_wd",)), "ffn1_dwg", "ffn1_dwu", "ffn1_dh_g")
    reduce_early(("ffn1_w_gate",), rows_of(("ffn1_wgt",)), "ffn1_dwu", "ffn1_dh_g", "ffn1_dh_u")
    reduce_early(("ffn1_w_up",), rows_of(("ffn1_wut",)), "ffn1_dh_g", "ffn1_dh_u", None)
    loss_slab, grad_x, g = _local_step(x[0], loss_target[0], w, plan)
    loss = lax.psum(loss_slab[0, 0], ("x", "y", "c"))

    gfull = {
        "rwkv_w2": g["w2_pad"][0:32], "rwkv_a2": g["a2_pad"][32:64], "rwkv_g2": g["g2_pad"][64:160],
    }
    gsmall = {
        "ffn1_norm": g["ffn1_norm"], "mix_norm": g["mix_norm"],
        "hgrn_lb_logits": jnp.concatenate([g["lb0"], g["lb1"]], axis=0), "hgrn_out_norm": g["hgrn_out_norm"],
        "rwkv_shift_mu": g["mu_pad"][:, :N_RWKV_COLS], "rwkv_w0": g["rwkv_w0"], "rwkv_a0": g["rwkv_a0"],
        "rwkv_k_k": g["rwkv_k_k"], "rwkv_k_a": g["rwkv_k_a"], "rwkv_r_k": g["rwkv_r_k"],
        "rwkv_gn_w": g["rwkv_gn_w"], "rwkv_gn_b": g["rwkv_gn_b"], "ffn2_norm": g["ffn2_norm"],
        "final_norm": g["final_norm"],
    }
    packed = jnp.stack([_pack({n: _quarter(gfull[n], n, q) for n in PACKED}, gsmall) for q in range(N_CHIPS)])
    early["packed"], = reduce_rows(["packed"], [packed])
    last = ["ffn1_w_up", "packed"]
    after_swap(last)(_run_comm(swap_comm(last), "grad_sibling_swap"), w)
    names = list(BIG) + ["packed"]

    def rows_list(d):
        return [_to_rows(n, d[n]) for n in BIG] + [_pack(d, {n: d[n] for n in SMALL})]

    outs = [_adamw(wt, *swapped[n], mt, vt, me_idx, f"adamw_{n}")
            for wt, mt, vt, n in zip(rows_list(wts), rows_list(moms), rows_list(vars_), names)]
    results = []
    for k in range(4):
        per = [outs[i][k] for i in range(len(names))]
        d = {n: _from_rows(n, z) for n, z in zip(BIG, per[:-1])}
        d.update(_unpack(per[-1]))
        results.append(d)
    return (loss, grad_x[None], *[r[n] for r in results for n in ALL_WEIGHTS])
```

```python
import collections
import functools

import jax
import jax.numpy as jnp
from jax import lax
from jax.experimental import pallas as pl
from jax.experimental.pallas import tpu as pltpu

F32 = jnp.float32
BF16 = jnp.bfloat16
SDS = jax.ShapeDtypeStruct
MESH = pl.DeviceIdType.MESH

D_MODEL = 1024
D_FF = 2816
W_A = 512
W_B = 512
HA_HEADS, HA_DIM = 4, 128
HB_HEADS, HB_DIM = 8, 64
HGRN_CHUNK = 64
HGRN_GROUP = 8
RWKV_CHUNK = 16
RWKV_GROUP = 8
N_HGRN_COLS = 4 * W_A
N_RWKV_COLS = 3 * W_B + 32 + 32 + 96
N_RWKV_PAD = 1792
LORA_PAD = 256
NORM_EPS = 1e-6
RWKV_GN_EPS = 64e-5
L2_EPS = 1e-12
ADAM_LR, ADAM_B1, ADAM_B2, ADAM_EPS, ADAM_WD, ADAM_STEP = 0.001, 0.9, 0.999, 1e-8, 0.01, 10

N_CHIPS = 4
VMEM_LIMIT_V7X = 56 * 1024 * 1024
LANES = 1024

SHARDED_SHAPES = {
    "ffn1_w_gate": ((D_MODEL, D_FF), 1), "ffn1_w_up": ((D_MODEL, D_FF), 1), "ffn1_w_down": ((D_FF, D_MODEL), 0),
    "w_in": ((D_MODEL, N_HGRN_COLS + N_RWKV_COLS), 1), "rwkv_w2": ((32, W_B), 1), "rwkv_a2": ((32, W_B), 1),
    "rwkv_g2": ((96, W_B), 1), "w_out": ((D_MODEL, D_MODEL), 0),
    "ffn2_w_gate": ((D_MODEL, D_FF), 1), "ffn2_w_up": ((D_MODEL, D_FF), 1), "ffn2_w_down": ((D_FF, D_MODEL), 0),
}
SMALL = ("ffn1_norm", "mix_norm", "hgrn_lb_logits", "hgrn_out_norm", "rwkv_shift_mu", "rwkv_w0", "rwkv_a0",
         "rwkv_k_k", "rwkv_k_a", "rwkv_r_k", "rwkv_gn_w", "rwkv_gn_b", "ffn2_norm", "final_norm")
ALL_WEIGHTS = ("ffn1_norm", "ffn1_w_gate", "ffn1_w_up", "ffn1_w_down", "mix_norm", "w_in", "hgrn_lb_logits",
               "hgrn_out_norm", "rwkv_shift_mu", "rwkv_w0", "rwkv_w2", "rwkv_a0", "rwkv_a2", "rwkv_g2", "rwkv_k_k",
               "rwkv_k_a", "rwkv_r_k", "rwkv_gn_w", "rwkv_gn_b", "w_out", "ffn2_norm", "ffn2_w_gate", "ffn2_w_up",
               "ffn2_w_down", "final_norm")


def _shard_shape(name):
    shape, ax = SHARDED_SHAPES[name]
    return tuple(s // N_CHIPS if i == ax else s for i, s in enumerate(shape))


def _numel(shape):
    n = 1
    for s in shape:
        n *= s
    return n


def _params(sem=None):
    return pltpu.CompilerParams(dimension_semantics=sem, vmem_limit_bytes=VMEM_LIMIT_V7X)


def _split2(x):
    hi = x.astype(BF16)
    return hi, (x.astype(F32) - hi.astype(F32)).astype(BF16)


def _dg(x, y, cx, cy, hi):
    dn = (((cx,), (cy,)), ((), ()))
    dot = lambda p, q: lax.dot_general(p, q, dn, preferred_element_type=F32)
    if hi == "x3":
        (xh, xl), (yh, yl) = _split2(x), _split2(y)
        return dot(xh, yh) + (dot(xh, yl) + dot(xl, yh))
    return dot(x.astype(BF16), y.astype(BF16))


def _make_mm(hi, cotangent_forms=None):
    @jax.custom_vjp
    def nn(x, y):
        return _dg(x, y, 1, 0, hi)

    @jax.custom_vjp
    def nt(x, y):
        return _dg(x, y, 1, 1, hi)

    @jax.custom_vjp
    def tn(x, y):
        return _dg(x, y, 0, 0, hi)

    bnn, bnt, btn = cotangent_forms or (nn, nt, tn)
    nn.defvjp(lambda x, y: (nn(x, y), (x, y)), lambda r, g: (bnt(g, r[1]), btn(r[0], g)))
    nt.defvjp(lambda x, y: (nt(x, y), (x, y)), lambda r, g: (bnn(g, r[1]), btn(g, r[0])))
    tn.defvjp(lambda x, y: (tn(x, y), (x, y)), lambda r, g: (bnt(r[1], g), bnn(r[0], g)))
    return nn, nt, tn


_nn, _nt, _tn = _make_mm(False)
_nn_x3, _nt_x3, _tn_x3 = _make_mm("x3", (_nn, _nt, _tn))


def _tri_apply(x, transpose):
    c = x.shape[0]
    tri = (lax.broadcasted_iota(jnp.int32, (c, c), 1) <= lax.broadcasted_iota(jnp.int32, (c, c), 0)).astype(BF16)
    dn = (((0 if transpose else 1,), (0,)), ((), ()))
    p1, p2 = _split2(x)
    dot = lambda p: lax.dot_general(tri, p, dn, preferred_element_type=F32)
    return dot(p1) + dot(p2)


@jax.custom_vjp
def _cumsum_rows(x):
    return _tri_apply(x, False)


_cumsum_rows.defvjp(lambda x: (_tri_apply(x, False), None), lambda _, g: (_tri_apply(g, True),))


def _sigmoid(x):
    return 1.0 / (1.0 + jnp.exp(-x))


def _silu(x):
    return x * _sigmoid(x)


def _softplus(z):
    return jnp.maximum(z, 0.0) + jnp.log(1.0 + jnp.exp(-jnp.abs(z)))


def _mm(a, b, *, ta=False, tb=False, tm, tn, tk, name, out_dtype=F32, res=None, scale=None, comm=None):
    m = a.shape[1] if ta else a.shape[0]
    kdim = a.shape[0] if ta else a.shape[1]
    n = b.shape[0] if tb else b.shape[1]
    assert (b.shape[1] if tb else b.shape[0]) == kdim
    tm, tn, tk = min(tm, m), min(tn, n), min(tk, kdim)
    assert m % tm == 0 and n % tn == 0 and kdim % tk == 0, (name, m, n, kdim)
    nk = kdim // tk
    a_spec = pl.BlockSpec((tk, tm), lambda i, j, k: (k, i)) if ta else pl.BlockSpec((tm, tk), lambda i, j, k: (i, k))
    b_spec = pl.BlockSpec((tn, tk), lambda i, j, k: (j, k)) if tb else pl.BlockSpec((tk, tn), lambda i, j, k: (k, j))
    o_spec = pl.BlockSpec((tm, tn), lambda i, j, k: (i, j))
    ca, cb = (0 if ta else 1), (1 if tb else 0)

    def body(*refs):
        if res is not None:
            a_ref, b_ref, r_ref, o_ref, acc_ref = refs
        else:
            a_ref, b_ref, o_ref, acc_ref = refs
        k = pl.program_id(2)

        @pl.when(k == 0)
        def _():
            acc_ref[...] = jnp.zeros_like(acc_ref)

        acc_ref[...] += _dg(a_ref[...], b_ref[...], ca, cb, False)

        @pl.when(k == nk - 1)
        def _():
            acc = acc_ref[...]
            if scale is not None:
                acc = acc * scale
            if res is not None:
                acc = r_ref[...] + acc
            o_ref[...] = acc.astype(out_dtype)

    in_specs = [a_spec, b_spec] + ([o_spec] if res is not None else [])
    args = (a, b) + ((res,) if res is not None else ())
    if comm is None:
        return pl.pallas_call(
            body, name=name, grid=(m // tm, n // tn, nk), in_specs=in_specs, out_specs=o_spec,
            out_shape=SDS((m, n), out_dtype), scratch_shapes=[pltpu.VMEM((tm, tn), F32)],
            compiler_params=_params(("parallel", "parallel", "arbitrary")))(*args)
    (out,), carried = _hosting_call(
        body, comm, name=name, grid=(m // tm, n // tn, nk), in_specs=in_specs, out_specs=[o_spec],
        out_shape=[SDS((m, n), out_dtype)], scratch_shapes=[pltpu.VMEM((tm, tn), F32)], args=args)
    return out, carried


def _mm_pair(a1, b1, a2, b2, *, tb=False, tm, name):
    m = a1.shape[0]
    n = b1.shape[0] if tb else b1.shape[1]
    tm = min(tm, m)
    cb = 1 if tb else 0
    assert a2.shape[0] == m and m % tm == 0
    assert all(b.shape[cb] == a.shape[1] and b.shape[1 - cb] == n for a, b in ((a1, b1), (a2, b2)))

    def body(a1_ref, b1_ref, a2_ref, b2_ref, o_ref):
        o_ref[...] = _dg(a1_ref[...], b1_ref[...], 1, cb, False) + _dg(a2_ref[...], b2_ref[...], 1, cb, False)

    a_spec = lambda a: pl.BlockSpec((tm, a.shape[1]), lambda i: (i, 0))
    b_spec = lambda b: pl.BlockSpec(b.shape, lambda i: (0, 0))
    return pl.pallas_call(
        body, name=name, grid=(m // tm,), in_specs=[a_spec(a1), b_spec(b1), a_spec(a2), b_spec(b2)],
        out_specs=pl.BlockSpec((tm, n), lambda i: (i, 0)), out_shape=SDS((m, n), F32),
        compiler_params=_params(("parallel",)))(a1, b1, a2, b2)


def _row_spec(x, tm, tile_of=lambda i: i):
    if isinstance(x, tuple):
        arr, w, j = x
        return arr, pl.BlockSpec((tm, w), lambda i, j=j: (tile_of(i), j))
    return x, pl.BlockSpec((tm, x.shape[1]), lambda i: (tile_of(i), 0))


def _par_spec(p):
    if isinstance(p, tuple):
        arr, w, j = p
        return arr, pl.BlockSpec((arr.shape[0], w), lambda i, j=j: (0, j))
    return p, pl.BlockSpec(p.shape, lambda i: (0, 0))


def _store_groups(refs, groups, vals):
    for ref, idxs in zip(refs, groups):
        off = 0
        for ix in idxs:
            v = vals[ix]
            ref[:, off:off + v.shape[1]] = v.astype(ref.dtype)
            off += v.shape[1]


SUBLANES = 8


def _x_plan(xs, tm, t, tile_of=lambda i: i):
    arrays, specs, plan = [], [], []
    nb = tm // SUBLANES
    for x in xs:
        if isinstance(x, tuple) and isinstance(x[0], str):
            kind, arr, w, j = x
            if kind == "prev":
                halo = lambda i, j=j: (jnp.maximum(tile_of(i) * nb - 1, 0), j)
            else:
                halo = lambda i, j=j: (jnp.minimum((tile_of(i) + 1) * nb, t // SUBLANES - 1), j)
            arrays += [arr, arr]
            specs += [pl.BlockSpec((tm, w), lambda i, j=j: (tile_of(i), j)), pl.BlockSpec((SUBLANES, w), halo)]
            plan.append((kind, 2, w))
        else:
            arr, spec = _row_spec(x, tm, tile_of)
            arrays.append(arr)
            specs.append(spec)
            plan.append(("plain", 1, spec.block_shape[1]))
    return arrays, specs, plan


def _x_vals(refs, plan, tm, nt, tile_of=lambda i: i):
    vals, k = [], 0
    i = tile_of(pl.program_id(0))
    rows = lax.broadcasted_iota(jnp.int32, (tm, 1), 0)
    for kind, n, _ in plan:
        main = refs[k][...].astype(F32)
        if kind == "prev":
            edge = jnp.where(i == 0, 0.0, refs[k + 1][SUBLANES - 1:SUBLANES, :].astype(F32))
            main = jnp.where(rows == 0, edge, pltpu.roll(main, 1, 0))
        elif kind == "next":
            edge = jnp.where(i == nt - 1, 0.0, refs[k + 1][0:1, :].astype(F32))
            main = jnp.where(rows == tm - 1, edge, pltpu.roll(main, tm - 1, 0))
        vals.append(main)
        k += n
    return vals


def _tile_rows(xs, tm):
    arr = xs[0]
    if isinstance(arr, tuple):
        arr = arr[1] if isinstance(arr[0], str) else arr[0]
    return min(tm, arr.shape[0]), arr.shape[0]


def _rowwise(f, xs, params, out_groups, out_dtypes, *, tm, name, comm=None):
    tm, t = _tile_rows(xs, tm)
    nt = t // tm
    xa, xspecs, plan = _x_plan(xs, tm, t)
    pa, pspecs = (zip(*[_par_spec(p) for p in params]) if params else ((), ()))
    nxr, npar = len(xa), len(pa)
    x_sds = [SDS((tm, w), F32) for _, _, w in plan]
    p_sds = [SDS(s.block_shape, F32) for s in pspecs]
    outs_sds = jax.eval_shape(lambda *vals: f(*vals), *x_sds, *p_sds)
    widths = [sum(outs_sds[ix].shape[1] for ix in idxs) for idxs in out_groups]

    def body(*refs):
        vals = _x_vals(refs[:nxr], plan, tm, nt) + [r[...].astype(F32) for r in refs[nxr:nxr + npar]]
        outs = f(*vals)
        _store_groups(refs[nxr + npar:], out_groups, outs)

    res, carried = _hosting_call(
        body, comm, name=name, grid=(nt,), in_specs=list(xspecs) + list(pspecs),
        out_specs=[pl.BlockSpec((tm, w), lambda i: (i, 0)) for w in widths],
        out_shape=[SDS((t, w), dt) for w, dt in zip(widths, out_dtypes)], scratch_shapes=[], args=(*xa, *pa))
    return res if comm is None else (res, carried)


def _rowwise_bwd(f, xs, params, cots, *, x_grad, p_grad, dx_groups, dx_dtypes, tm, name, extra=None, comm=None,
                 fold_next=None):
    tm, t = _tile_rows(xs, tm)
    nt = t // tm
    tile_of = (lambda i: nt - 1 - i) if fold_next else (lambda i: i)
    xa, xspecs, plan = _x_plan(xs, tm, t, tile_of)
    pa, pspecs = (zip(*[_par_spec(p) for p in params]) if params else ((), ()))
    ca, cspecs = zip(*[_row_spec(c, tm, tile_of) for c in cots])
    extra = extra or {}
    ekeys = sorted(extra)
    ea, especs = (zip(*[_row_spec(extra[k], tm, tile_of) for k in ekeys]) if ekeys else ((), ()))
    nx, nxr, npar, nc, ne = len(plan), len(xa), len(pa), len(ca), len(ea)
    gx = [i for i in range(nx) if x_grad[i]]
    gp = [i for i in range(npar) if p_grad[i]]
    all_widths = [sum(plan[gx[ix]][2] for ix in idxs) for idxs in dx_groups]
    emitted = [k for k in range(len(dx_groups)) if not (fold_next and k == fold_next[1])]
    widths = [all_widths[k] for k in emitted]
    ng = len(emitted)

    def body(*refs):
        ins = refs[:nxr + npar + nc + ne]
        outs = refs[nxr + npar + nc + ne:]
        vals = _x_vals(ins[:nxr], plan, tm, nt, tile_of) + [r[...].astype(F32) for r in ins[nxr:nxr + npar]]
        cvals = tuple(r[...].astype(F32) for r in ins[nxr + npar:nxr + npar + nc])
        evals = [r[...].astype(F32) for r in ins[nxr + npar + nc:]]
        diff_idx = gx + [nx + i for i in gp]

        def g(*dargs):
            full = list(vals)
            for ix, v in zip(diff_idx, dargs):
                full[ix] = v
            return tuple(f(*full))

        _, vjp = jax.vjp(g, *[vals[ix] for ix in diff_idx])
        grads = vjp(cvals)
        dxs = list(grads[:len(gx)])
        for k, ev in zip(ekeys, evals):
            dxs[k] = dxs[k] + ev
        _store_groups(outs[:ng], [dx_groups[k] for k in emitted], dxs)
        i = pl.program_id(0)
        if fold_next:
            main_ref, carry_ref = outs[emitted.index(fold_next[0])], refs[-1]
            rows = lax.broadcasted_iota(jnp.int32, (tm, 1), 0)
            off = 0
            for ix in dx_groups[fold_next[1]]:
                piece = dxs[ix]
                cols = slice(off, off + piece.shape[1])
                edge = jnp.where(i == 0, 0.0, carry_ref[0:1, cols])
                main_ref[:, cols] += jnp.where(rows == tm - 1, edge, pltpu.roll(piece, tm - 1, 0))
                carry_ref[:, cols] = piece[:SUBLANES]
                off += piece.shape[1]
        for ref, gval in zip(outs[ng:ng + len(gp)], grads[len(gx):]):
            @pl.when(i == 0)
            def _(ref=ref):
                ref[...] = jnp.zeros_like(ref)
            ref[...] += gval

    dp_specs = [pl.BlockSpec(pspecs[i].block_shape, lambda i: (0, 0)) for i in gp]
    dp_shapes = [SDS(pspecs[i].block_shape, F32) for i in gp]
    scratch = [pltpu.VMEM((SUBLANES, all_widths[fold_next[1]]), F32)] if fold_next else []
    res, carried = _hosting_call(
        body, comm, name=name, grid=(nt,), in_specs=list(xspecs) + list(pspecs) + list(cspecs) + list(especs),
        out_specs=[pl.BlockSpec((tm, w), lambda i: (tile_of(i), 0)) for w in widths] + dp_specs,
        out_shape=[SDS((t, w), dt) for w, dt in zip(widths, dx_dtypes)] + dp_shapes, scratch_shapes=scratch,
        args=(*xa, *pa, *ca, *ea))
    return res if comm is None else (res, carried)


def _rms_f(x, g):
    return (x * lax.rsqrt(jnp.mean(x * x, axis=-1, keepdims=True) + NORM_EPS) * g,)


def _group_sum_impl(x, ones_bd):
    p1, p2 = _split2(x)
    dot = lambda p: lax.dot_general(p, ones_bd.astype(BF16), (((1,), (0,)), ((), ())), preferred_element_type=F32)
    return dot(p1) + dot(p2)


@jax.custom_vjp
def _group_sum(x, ones_bd):
    return _group_sum_impl(x, ones_bd)


_group_sum.defvjp(lambda x, o: (_group_sum_impl(x, o), o),
                  lambda o, g: (_group_sum_impl(g, o), jnp.zeros_like(o)))


def _rwkv_prep_f(r, k, v, lo, rp, kp, vp, lop, mu_r, mu_k, mu_v, mu_lo, w0, w2p, a0, a2p, g2p, k_k, k_a, ones_bd):
    r = r + mu_r * (rp - r)
    k = k + mu_k * (kp - k)
    v = v + mu_v * (vp - v)
    lo = lo + mu_lo * (lop - lo)
    w_log = -_softplus(-(w0 + _nn(jnp.tanh(lo), w2p))) - 0.5
    lw = -jnp.exp(w_log)
    a_g = _sigmoid(a0 + _nn(lo, a2p))
    g = _nn(_sigmoid(lo), g2p)
    kk = k * k_k
    kk = kk / jnp.maximum(jnp.sqrt(_group_sum(kk * kk, ones_bd)), L2_EPS)
    k2 = k * (1.0 + (a_g - 1.0) * k_a)
    return r, lw, k2, v, -kk, kk * a_g, g


def _rwkv_post_f(y, r, k2, v, g, r_k, gn_w, gn_b, ones_bd):
    inv_n = 1.0 / HB_DIM
    mean = _group_sum(y, ones_bd) * inv_n
    yc = y - mean
    var = _group_sum(yc * yc, ones_bd) * inv_n
    yn = yc * lax.rsqrt(var + RWKV_GN_EPS) * gn_w + gn_b
    bonus = _group_sum(r * k2 * r_k, ones_bd) * v
    return ((yn + bonus) * g,)


def _tri(c, strict=False):
    ii = lax.broadcasted_iota(jnp.int32, (c, c), 0)
    jj = lax.broadcasted_iota(jnp.int32, (c, c), 1)
    return (jj < ii) if strict else (jj <= ii)


def _hgrn_step(st0, q_a, f_a, i_a, g_a, l0, l1, onorm):
    nh, nj = len(q_a), len(q_a[0])
    c = q_a[0][0].shape[0]
    combos = [(j, h) for j in range(nj) for h in range(nh)]
    every = lambda fn: {q: fn(q) for q in combos}
    at_ = lambda d: (lambda q: d[q[1]][q[0]])
    qa_, fa_, ia_, ga_ = (at_(z) for z in (q_a, f_a, i_a, g_a))
    incl = _tri(c)
    rows = lax.broadcasted_iota(jnp.int32, (c, 1), 0)
    lb = []
    for h in range(nh):
        mx = jnp.maximum(l0[h], l1[h])
        e0, e1 = jnp.exp(l0[h] - mx), jnp.exp(l1[h] - mx)
        lb.append(e0 / (e0 + e1))
    forget = every(lambda q: lb[q[1]] + (1.0 - lb[q[1]]) * _sigmoid(fa_(q)))
    qs = every(lambda q: _silu(qa_(q)))
    kk = every(lambda q: 1.0 - forget[q])
    lf = every(lambda q: jnp.log(forget[q]))
    bcum = every(lambda q: _cumsum_rows(lf[q]))
    bref = every(lambda q: jnp.sum(jnp.where(rows <= c // 2, lf[q], 0.0), axis=0, keepdims=True))
    blast = every(lambda q: jnp.sum(lf[q], axis=0, keepdims=True))
    scores = every(lambda q: jnp.where(incl, _nt(qs[q] * jnp.exp(bcum[q] - bref[q]),
                                                 kk[q] * jnp.exp(bref[q] - bcum[q])), 0.0))
    intra = every(lambda q: _nn(scores[q], ia_(q)))
    qb = every(lambda q: qs[q] * jnp.exp(bcum[q]))
    upd = every(lambda q: _tn(ia_(q), kk[q] * jnp.exp(blast[q] - bcum[q])))
    dec = every(lambda q: jnp.exp(blast[q]))
    st = list(st0)
    o = {}
    for j in range(nj):
        for h in range(nh):
            o[(j, h)] = intra[(j, h)] + _nt(qb[(j, h)], st[h])
        st = [st[h] * dec[(j, h)] + upd[(j, h)] for h in range(nh)]
    out = every(lambda q: o[q] * lax.rsqrt(jnp.mean(o[q] * o[q], axis=-1, keepdims=True) + NORM_EPS)
                * onorm[q[1]] * _silu(ga_(q)))
    return [[out[(j, h)] for j in range(nj)] for h in range(nh)], st


def _hgrn_blocks(ref, nj, c):
    return [[ref[j * c:(j + 1) * c, h * HA_DIM:(h + 1) * HA_DIM] for j in range(nj)] for h in range(HA_HEADS)]


def _hgrn_cols(ref):
    return [ref[:, h * HA_DIM:(h + 1) * HA_DIM] for h in range(HA_HEADS)]


def _hgrn_fwd(p_h, l0, l1, onorm):
    t = p_h.shape[0]
    cc, nj = HGRN_CHUNK, HGRN_GROUP
    c = cc * nj
    n = t // c

    def body(q_ref, f_ref, i_ref, g_ref, l0_ref, l1_ref, on_ref, o_ref, hs_ref, st_ref):
        @pl.when(pl.program_id(0) == 0)
        def _():
            st_ref[...] = jnp.zeros_like(st_ref)

        hs_ref[0] = st_ref[...]
        o, st1 = _hgrn_step([st_ref[h] for h in range(HA_HEADS)],
                            *[_hgrn_blocks(ref, nj, cc) for ref in (q_ref, f_ref, i_ref, g_ref)],
                            _hgrn_cols(l0_ref), _hgrn_cols(l1_ref), _hgrn_cols(on_ref))
        for h in range(HA_HEADS):
            for j in range(nj):
                o_ref[j * cc:(j + 1) * cc, h * HA_DIM:(h + 1) * HA_DIM] = o[h][j]
            st_ref[h] = st1[h]

    col = lambda j: pl.BlockSpec((c, W_A), lambda i, j=j: (i, j))
    par = pl.BlockSpec((1, W_A), lambda i: (0, 0))
    return pl.pallas_call(
        body, name="hgrn_fwd", grid=(n,), in_specs=[col(0), col(1), col(2), col(3), par, par, par],
        out_specs=[pl.BlockSpec((c, W_A), lambda i: (i, 0)),
                   pl.BlockSpec((1, HA_HEADS, HA_DIM, HA_DIM), lambda i: (i, 0, 0, 0))],
        out_shape=[SDS((t, W_A), F32), SDS((n, HA_HEADS, HA_DIM, HA_DIM), F32)],
        scratch_shapes=[pltpu.VMEM((HA_HEADS, HA_DIM, HA_DIM), F32)],
        compiler_params=_params(("arbitrary",)))(p_h, p_h, p_h, p_h, l0, l1, onorm)


def _hgrn_bwd(p_h, l0, l1, onorm, hs, do, do_col, comm=None):
    t = p_h.shape[0]
    cc, nj = HGRN_CHUNK, HGRN_GROUP
    c = cc * nj
    n = t // c

    def body(q_ref, f_ref, i_ref, g_ref, l0_ref, l1_ref, on_ref, hs_ref, do_ref,
             dp_ref, dl0_ref, dl1_ref, don_ref, dst_ref):
        @pl.when(pl.program_id(0) == 0)
        def _():
            dst_ref[...] = jnp.zeros_like(dst_ref)
            dl0_ref[...] = jnp.zeros_like(dl0_ref)
            dl1_ref[...] = jnp.zeros_like(dl1_ref)
            don_ref[...] = jnp.zeros_like(don_ref)

        args = ([hs_ref[0, h] for h in range(HA_HEADS)],
                *[_hgrn_blocks(ref, nj, cc) for ref in (q_ref, f_ref, i_ref, g_ref)],
                _hgrn_cols(l0_ref), _hgrn_cols(l1_ref), _hgrn_cols(on_ref))
        _, vjp = jax.vjp(_hgrn_step, *args)
        dst0, dq, df, di, dg, dl0, dl1, don = vjp((_hgrn_blocks(do_ref, nj, cc),
                                                   [dst_ref[h] for h in range(HA_HEADS)]))
        for h in range(HA_HEADS):
            sl = slice(h * HA_DIM, (h + 1) * HA_DIM)
            for k, dv in enumerate((dq, df, di, dg)):
                for j in range(nj):
                    dp_ref[j * cc:(j + 1) * cc, k * W_A + h * HA_DIM:k * W_A + (h + 1) * HA_DIM] = dv[h][j]
            dl0_ref[:, sl] += dl0[h]
            dl1_ref[:, sl] += dl1[h]
            don_ref[:, sl] += don[h]
            dst_ref[h] = dst0[h]

    col = lambda j: pl.BlockSpec((c, W_A), lambda i, j=j: (n - 1 - i, j))
    par = pl.BlockSpec((1, W_A), lambda i: (0, 0))
    return _hosting_call(
        body, comm, name="hgrn_bwd", grid=(n,),
        in_specs=[col(0), col(1), col(2), col(3), par, par, par,
                  pl.BlockSpec((1, HA_HEADS, HA_DIM, HA_DIM), lambda i: (n - 1 - i, 0, 0, 0)),
                  pl.BlockSpec((c, W_A), lambda i: (n - 1 - i, do_col))],
        out_specs=[pl.BlockSpec((c, N_HGRN_COLS), lambda i: (n - 1 - i, 0)), par, par, par],
        out_shape=[SDS((t, N_HGRN_COLS), F32), SDS((1, W_A), F32), SDS((1, W_A), F32), SDS((1, W_A), F32)],
        scratch_shapes=[pltpu.VMEM((HA_HEADS, HA_DIM, HA_DIM), F32)],
        args=(p_h, p_h, p_h, p_h, l0, l1, onorm, hs, do))


HB_PAIRS = HB_HEADS // 2
PAIR_W = 2 * HB_DIM


def _head_lane_masks():
    lane = lax.broadcasted_iota(jnp.int32, (1, PAIR_W), 1)
    return (lane < HB_DIM).astype(F32), (lane >= HB_DIM).astype(F32)


@jax.custom_vjp
def _stack_heads(x):
    m0, m1 = _head_lane_masks()
    return jnp.concatenate([x * m0, x * m1], axis=0)


def _stack_heads_bwd(_, g):
    m0, m1 = _head_lane_masks()
    c = g.shape[0] // 2
    return (g[:c] * m0 + g[c:] * m1,)


_stack_heads.defvjp(lambda x: (_stack_heads(x), None), _stack_heads_bwd)


@jax.custom_vjp
def _unstack_heads(ys):
    c = ys.shape[0] // 2
    return ys[:c] + ys[c:]


_unstack_heads.defvjp(lambda ys: (_unstack_heads(ys), None), lambda _, g: (_stack_heads(g),))


def _same_head_block(c):
    ii = lax.broadcasted_iota(jnp.int32, (2 * c, 2 * c), 0)
    jj = lax.broadcasted_iota(jnp.int32, (2 * c, 2 * c), 1)
    same = (ii < c) == (jj < c)
    return same & (jj <= ii), same & (jj < ii), (ii == jj).astype(F32)


@jax.custom_vjp
def _rows_join(top, bottom):
    return jnp.concatenate([top, bottom], axis=0)


def _rows_join_bwd(n_top, g):
    return g[:n_top], g[n_top:]


_rows_join.defvjp(lambda top, bottom: (_rows_join(top, bottom), top.shape[0]), _rows_join_bwd)


def _rows_split_impl(x, n_top):
    return x[:n_top], x[n_top:]


_rows_split = jax.custom_vjp(_rows_split_impl, nondiff_argnums=(1,))
_rows_split.defvjp(lambda x, n_top: (_rows_split_impl(x, n_top), None),
                   lambda n_top, _, g: (jnp.concatenate([g[0], g[1]], axis=0),))


def _rwkv_step(s0, r, lw, k, v, a, b):
    npair, nj = len(r), len(r[0])
    c = r[0][0].shape[0]
    combos = [(j, p) for j in range(nj) for p in range(npair)]
    every = lambda fn: {q: fn(q) for q in combos}
    at_ = lambda d: (lambda q: d[q[1]][q[0]])
    r_, lw_, k_, v_, a_, b_ = (at_(z) for z in (r, lw, k, v, a, b))
    incl, strict, eye = _same_head_block(c)

    gam = every(lambda q: _cumsum_rows(lw_(q)))
    gtot = every(lambda q: jnp.sum(lw_(q), axis=0, keepdims=True))
    eneg = every(lambda q: jnp.exp(-gam[q]))
    edec = every(lambda q: jnp.exp(gtot[q] - gam[q]))
    at = every(lambda q: _stack_heads(a_(q) * jnp.exp(gam[q] - lw_(q))))
    rt = every(lambda q: _stack_heads(r_(q) * jnp.exp(gam[q])))
    bt = every(lambda q: _stack_heads(b_(q) * eneg[q]))
    kt = every(lambda q: _stack_heads(k_(q) * eneg[q]))
    bdec = every(lambda q: _stack_heads(b_(q) * edec[q]))
    kdec = every(lambda q: _stack_heads(k_(q) * edec[q]))
    vs = every(lambda q: _stack_heads(v_(q)))
    a_ab = every(lambda q: jnp.where(strict, _nt(at[q], bt[q]), 0.0))
    a_ak = every(lambda q: jnp.where(strict, _nt(at[q], kt[q]), 0.0))
    a_rb = every(lambda q: jnp.where(incl, _nt(rt[q], bt[q]), 0.0))
    a_rk = every(lambda q: jnp.where(incl, _nt(rt[q], kt[q]), 0.0))
    tinv = every(lambda q: eye + a_ab[q])
    pw = a_ab
    span = 2
    while span < c:
        pw = every(lambda q, pw=pw: _nn_x3(pw[q], pw[q]))
        tinv = every(lambda q, pw=pw, tinv=tinv: tinv[q] + _nn_x3(pw[q], tinv[q]))
        span *= 2
    akv = every(lambda q: _nn(a_ak[q], vs[q]))
    w1 = every(lambda q: _nn(tinv[q], at[q]))
    u0 = every(lambda q: _nn(tinv[q], akv[q]))
    wr = every(lambda q: _rows_join(w1[q], rt[q]))
    bk = every(lambda q: _rows_join(bdec[q], kdec[q]))
    yv = every(lambda q: _nn(a_rk[q], vs[q]))
    gdec = every(lambda q: jnp.exp(gtot[q]))

    s = list(s0)
    y = [[None] * nj for _ in range(npair)]
    for j in range(nj):
        both = {p: _rows_split(_nt(wr[(j, p)], s[p]), 2 * c) for p in range(npair)}
        u = {p: both[p][0] + u0[(j, p)] for p in range(npair)}
        for p in range(npair):
            y[p][j] = _unstack_heads(both[p][1] + _nn(a_rb[(j, p)], u[p]) + yv[(j, p)])
        s = [s[p] * gdec[(j, p)] + _tn(_rows_join(u[p], vs[(j, p)]), bk[(j, p)]) for p in range(npair)]
    return y, s


def _rwkv_blocks(ref, nj, c):
    return [[ref[j * c:(j + 1) * c, p * PAIR_W:(p + 1) * PAIR_W] for j in range(nj)] for p in range(HB_PAIRS)]


def _rwkv_fwd(seqs, comm=None):
    t = seqs[0].shape[0]
    c, nj = RWKV_CHUNK, RWKV_GROUP
    n = t // (c * nj)

    def body(r_ref, lw_ref, k_ref, v_ref, a_ref, b_ref, y_ref, hs_ref, st_ref):
        @pl.when(pl.program_id(0) == 0)
        def _():
            st_ref[...] = jnp.zeros_like(st_ref)

        hs_ref[0] = st_ref[...]
        s0 = [st_ref[p] for p in range(HB_PAIRS)]
        y, s1 = _rwkv_step(s0, *[_rwkv_blocks(ref, nj, c) for ref in (r_ref, lw_ref, k_ref, v_ref, a_ref, b_ref)])
        for p in range(HB_PAIRS):
            for j in range(nj):
                y_ref[j * c:(j + 1) * c, p * PAIR_W:(p + 1) * PAIR_W] = y[p][j]
            st_ref[p] = s1[p]

    seq = pl.BlockSpec((c * nj, W_B), lambda i: (i, 0))
    return _hosting_call(
        body, comm, name="rwkv_fwd", grid=(n,), in_specs=[seq] * 6,
        out_specs=[seq, pl.BlockSpec((1, HB_PAIRS, PAIR_W, PAIR_W), lambda i: (i, 0, 0, 0))],
        out_shape=[SDS((t, W_B), F32), SDS((n, HB_PAIRS, PAIR_W, PAIR_W), F32)],
        scratch_shapes=[pltpu.VMEM((HB_PAIRS, PAIR_W, PAIR_W), F32)], args=tuple(seqs))


def _rwkv_bwd(seqs, hs, dy, comm=None):
    t = seqs[0].shape[0]
    c, nj = RWKV_CHUNK, RWKV_GROUP
    n = t // (c * nj)

    def body(r_ref, lw_ref, k_ref, v_ref, a_ref, b_ref, hs_ref, dy_ref,
             dr_ref, dlw_ref, dk_ref, dv_ref, da_ref, db_ref, dst_ref):
        @pl.when(pl.program_id(0) == 0)
        def _():
            dst_ref[...] = jnp.zeros_like(dst_ref)

        s0 = [hs_ref[0, p] for p in range(HB_PAIRS)]
        seq_vals = [_rwkv_blocks(ref, nj, c) for ref in (r_ref, lw_ref, k_ref, v_ref, a_ref, b_ref)]
        _, vjp = jax.vjp(_rwkv_step, s0, *seq_vals)
        grads = vjp((_rwkv_blocks(dy_ref, nj, c), [dst_ref[p] for p in range(HB_PAIRS)]))
        for ref, gr in zip((dr_ref, dlw_ref, dk_ref, dv_ref, da_ref, db_ref), grads[1:]):
            for p in range(HB_PAIRS):
                for j in range(nj):
                    ref[j * c:(j + 1) * c, p * PAIR_W:(p + 1) * PAIR_W] = gr[p][j]
        m0, m1 = _head_lane_masks()
        rows0 = (lax.broadcasted_iota(jnp.int32, (PAIR_W, 1), 0) < HB_DIM).astype(F32)
        blocks = rows0 * m0 + (1.0 - rows0) * m1
        for p in range(HB_PAIRS):
            dst_ref[p] = grads[0][p] * blocks

    seq = pl.BlockSpec((c * nj, W_B), lambda i: (n - 1 - i, 0))
    return _hosting_call(
        body, comm, name="rwkv_bwd", grid=(n,),
        in_specs=[seq] * 6 + [pl.BlockSpec((1, HB_PAIRS, PAIR_W, PAIR_W), lambda i: (n - 1 - i, 0, 0, 0)), seq],
        out_specs=[seq] * 6, out_shape=[SDS((t, W_B), F32)] * 6,
        scratch_shapes=[pltpu.VMEM((HB_PAIRS, PAIR_W, PAIR_W), F32)], args=(*seqs, hs, dy))


def _final_loss(x3, fnorm, target, *, tm):
    t, d = x3.shape

    def body(x_ref, g_ref, t_ref, dx_ref, dg_ref, loss_ref):
        @pl.when(pl.program_id(0) == 0)
        def _():
            dg_ref[...] = jnp.zeros_like(dg_ref)
            loss_ref[...] = jnp.zeros_like(loss_ref)

        x, g = x_ref[...], g_ref[...]
        rinv = lax.rsqrt(jnp.mean(x * x, axis=-1, keepdims=True) + NORM_EPS)
        xh = x * rinv
        diff = xh * g - t_ref[...]
        loss_ref[...] += 0.5 * jnp.sum(jnp.mean(diff * diff, axis=-1, keepdims=True))
        dy = diff * (1.0 / d)
        dg_ref[...] += jnp.sum(dy * xh, axis=0, keepdims=True)
        dxh = dy * g
        dx_ref[...] = rinv * (dxh - xh * jnp.mean(dxh * xh, axis=-1, keepdims=True))

    row = pl.BlockSpec((tm, d), lambda i: (i, 0))
    return pl.pallas_call(
        body, name="final_loss", grid=(t // tm,), in_specs=[row, pl.BlockSpec((1, d), lambda i: (0, 0)), row],
        out_specs=[row, pl.BlockSpec((1, d), lambda i: (0, 0)), pl.BlockSpec((8, 128), lambda i: (0, 0))],
        out_shape=[SDS((t, d), F32), SDS((1, d), F32), SDS((8, 128), F32)],
        compiler_params=_params(("arbitrary",)))(x3, fnorm, target)


def _gate_up_act(h, wgt, wut, *, tm, tn, name, comm=None):
    t, d = h.shape
    tm = min(tm, t)

    def body(h_ref, g_ref, u_ref, a_out, u_out, act_out):
        hv = h_ref[...]
        a = _dg(hv, g_ref[...], 1, 1, False)
        u = _dg(hv, u_ref[...], 1, 1, False)
        a_out[...] = a.astype(a_out.dtype)
        u_out[...] = u.astype(u_out.dtype)
        act_out[...] = (_silu(a) * u).astype(act_out.dtype)

    wspec = pl.BlockSpec((tn, d), lambda i, j: (j, 0))
    ospec = pl.BlockSpec((tm, tn), lambda i, j: (i, j))
    return _hosting_call(
        body, comm, name=name, grid=(t // tm, D_FF // tn),
        in_specs=[pl.BlockSpec((tm, d), lambda i, j: (i, 0)), wspec, wspec], out_specs=[ospec, ospec, ospec],
        out_shape=[SDS((t, D_FF), BF16), SDS((t, D_FF), BF16), SDS((t, D_FF), BF16)], scratch_shapes=[],
        args=(h, wgt, wut))


def _dact_swiglu(dout, wd, a, u, *, tm, tn, name, comm=None):
    t, d = dout.shape
    tm = min(tm, t)

    def body(d_ref, w_ref, a_ref, u_ref, da_out, du_out):
        dact = 0.5 * _dg(d_ref[...], w_ref[...], 1, 1, False)
        av, uv = a_ref[...].astype(F32), u_ref[...].astype(F32)
        s = _sigmoid(av)
        da_out[...] = (dact * uv * (s * (1.0 + av * (1.0 - s)))).astype(da_out.dtype)
        du_out[...] = (dact * (av * s)).astype(du_out.dtype)

    tile = pl.BlockSpec((tm, tn), lambda i, j: (i, j))
    return _hosting_call(
        body, comm, name=name, grid=(t // tm, D_FF // tn),
        in_specs=[pl.BlockSpec((tm, d), lambda i, j: (i, 0)), pl.BlockSpec((tn, d), lambda i, j: (j, 0)), tile, tile],
        out_specs=[tile, tile], out_shape=[SDS((t, D_FF), BF16), SDS((t, D_FF), BF16)], scratch_shapes=[],
        args=(dout, wd, a, u))


class _Plan:
    def __init__(self):
        self.entries, self.counts = collections.defaultdict(list), {}

    def carry(self, host, comm_of, after):
        self.entries[host].append((comm_of, after))

    def comm(self, host, g):
        comms = [comm_of(g) for comm_of, _ in self.entries.get(host, [])]
        self.counts[host] = [len(c.arrays) for c in comms]
        return functools.reduce(_join_comms, comms) if comms else None

    def done(self, host, results, w):
        start = 0
        for (_, after), n in zip(self.entries.get(host, []), self.counts.get(host, [])):
            after(results[start:start + n], w)
            start += n


def _ffn_fwd(x, w, tag, plan, g):
    comm = plan.comm(f"{tag}_rms", g)
    res = _rowwise(_rms_f, [x], [w[f"{tag}_norm"]], [[0]], [BF16], tm=512, name=f"{tag}_rms", comm=comm)
    (h,), carried = res if comm is not None else (res, [])
    plan.done(f"{tag}_rms", carried, w)
    (a, u, act), carried = _gate_up_act(h, w[f"{tag}_wgt"], w[f"{tag}_wut"], tm=2048, tn=256, name=f"{tag}_gate_up",
                                        comm=plan.comm(f"{tag}_gate_up", g))
    plan.done(f"{tag}_gate_up", carried, w)
    comm = plan.comm(f"{tag}_down", g)
    out = _mm(act, w[f"{tag}_wd"], tm=1024, tn=D_MODEL, tk=D_FF, name=f"{tag}_down", res=x, scale=0.5, comm=comm)
    if comm is not None:
        out, carried = out
        plan.done(f"{tag}_down", carried, w)
    return out, (h, a, u, act)


def _ffn_bwd(dout, x, w, saved, tag, plan, g):
    h, a, u, act = saved

    def carrying(fn, host, *args, **kwargs):
        comm = plan.comm(host, g)
        res = fn(*args, name=host, comm=comm, **kwargs)
        out, carried = res if comm is not None else (res, [])
        plan.done(host, carried, w)
        return out

    (da, du), carried = _dact_swiglu(dout, w[f"{tag}_wd"], a, u, tm=2048, tn=256, name=f"{tag}_dact",
                                     comm=plan.comm(f"{tag}_dact", g))
    plan.done(f"{tag}_dact", carried, w)
    g[f"{tag}_wd"] = _mm(act, dout, ta=True, tm=D_FF // 2, tn=D_MODEL, tk=1024, name=f"{tag}_dwd", scale=0.5)
    g[f"{tag}_wgt"] = carrying(_mm, f"{tag}_dwg", da, h, ta=True, tm=D_FF // 2, tn=D_MODEL, tk=1024)
    g[f"{tag}_wut"] = carrying(_mm, f"{tag}_dwu", du, h, ta=True, tm=D_FF // 2, tn=D_MODEL, tk=1024)
    if plan.entries.get(f"{tag}_dh_g") or plan.entries.get(f"{tag}_dh_u"):
        dh = carrying(_mm, f"{tag}_dh_g", da, w[f"{tag}_wgt"], tm=1024, tn=D_MODEL, tk=D_FF)
        dh = carrying(_mm, f"{tag}_dh_u", du, w[f"{tag}_wut"], tm=1024, tn=D_MODEL, tk=D_FF, res=dh)
    else:
        dh = _mm_pair(da, w[f"{tag}_wgt"], du, w[f"{tag}_wut"], tm=512, name=f"{tag}_dh")
    dx, g[f"{tag}_norm"] = carrying(_rowwise_bwd, f"{tag}_drms", _rms_f, [x], [w[f"{tag}_norm"]], [dh],
                                    x_grad=[True], p_grad=[True], dx_groups=[[0]], dx_dtypes=[F32], tm=512,
                                    extra={0: dout})
    return dx


def _local_step(x, target, w, plan=None):
    plan = plan or _Plan()
    ones_bd = jnp.kron(jnp.eye(HB_HEADS, dtype=F32), jnp.ones((HB_DIM, HB_DIM), F32))
    g = {}
    x1, ffn1_saved = _ffn_fwd(x, w, "ffn1", plan, g)
    hm, = _rowwise(_rms_f, [x1], [w["mix_norm"]], [[0]], [BF16], tm=512, name="mix_rms")
    p_h = _mm(hm, w["w_in_h"], tm=2048, tn=256, tk=D_MODEL, name="inproj_h")
    p_r = _mm(hm, w["w_in_r"], tm=2048, tn=256, tk=D_MODEL, name="inproj_r")
    o_a, hgrn_states = _hgrn_fwd(p_h, w["lb0"], w["lb1"], w["hgrn_out_norm"])

    mu = w["mu_pad"]
    prep_xs = [(p_r, W_B, 0), (p_r, W_B, 1), (p_r, W_B, 2), (p_r, LORA_PAD, 6),
               ("prev", p_r, W_B, 0), ("prev", p_r, W_B, 1), ("prev", p_r, W_B, 2), ("prev", p_r, LORA_PAD, 6)]
    prep_ps = [(mu, W_B, 0), (mu, W_B, 1), (mu, W_B, 2), (mu, LORA_PAD, 6), w["rwkv_w0"], w["w2_pad"], w["rwkv_a0"],
               w["a2_pad"], w["g2_pad"], w["rwkv_k_k"], w["rwkv_k_a"], ones_bd]
    prep_f = _rwkv_prep_f
    r, lw, k2, v, a_vec, b_vec, gate = _rowwise(prep_f, prep_xs, prep_ps, [[0], [1], [2], [3], [4], [5], [6]],
                                                [F32] * 7, tm=256, name="rwkv_prep")
    seqs = [r, lw, k2, v, a_vec, b_vec]
    (y, rwkv_states), carried = _rwkv_fwd(seqs, comm=plan.comm("rwkv_fwd", g))
    plan.done("rwkv_fwd", carried, w)
    post_f = _rwkv_post_f
    post_xs = [y, r, k2, v, gate]
    post_ps = [w["rwkv_r_k"], w["rwkv_gn_w"], w["rwkv_gn_b"], ones_bd]
    o, = _rowwise(lambda o_a_, *rest: (o_a_,) + tuple(post_f(*rest)), [o_a] + post_xs, post_ps, [[0, 1]], [F32],
                  tm=256, name="rwkv_post")
    x2 = _mm(o, w["w_out"], tm=2048, tn=256, tk=D_MODEL, name="outproj", res=x1)
    x3, ffn2_saved = _ffn_fwd(x2, w, "ffn2", plan, g)
    dx3, g["final_norm"], loss = _final_loss(x3, w["final_norm"], target, tm=256)

    dx2 = _ffn_bwd(dx3, x2, w, ffn2_saved, "ffn2", plan, g)
    do = _mm(dx2, w["w_out"], tb=True, tm=2048, tn=256, tk=D_MODEL, name="outproj_do")
    g["w_out"] = _mm(o, dx2, ta=True, tm=D_MODEL, tn=D_MODEL, tk=1024, name="outproj_dw")

    (dp_h, g["lb0"], g["lb1"], g["hgrn_out_norm"]), carried = _hgrn_bwd(
        p_h, w["lb0"], w["lb1"], w["hgrn_out_norm"], hgrn_states, do, 0, comm=plan.comm("hgrn_bwd", g))
    plan.done("hgrn_bwd", carried, w)
    post_out = _rowwise_bwd(post_f, post_xs, post_ps, [(do, W_B, 1)], x_grad=[True] * 5, p_grad=[True] * 3 + [False],
                            dx_groups=[[0], [1], [2], [3], [4]], dx_dtypes=[F32] * 5, tm=256, name="rwkv_post_bwd")
    dy, dr1, dk1, dv1, dgate, g["rwkv_r_k"], g["rwkv_gn_w"], g["rwkv_gn_b"] = post_out
    (dr2, dlw, dk2, dv2, da_vec, db_vec), carried = _rwkv_bwd(seqs, rwkv_states, dy, comm=plan.comm("rwkv_bwd", g))
    plan.done("rwkv_bwd", carried, w)

    def prep2_f(*vals):
        r_, lw_, k2_, v_, a_, b_, g_ = prep_f(*vals)
        return r_, lw_, k2_, v_, a_, b_, g_, r_, k2_, v_

    prep_comm = plan.comm("rwkv_prep_bwd", g)
    prep_out = _rowwise_bwd(prep2_f, prep_xs, prep_ps, [dr2, dlw, dk2, dv2, da_vec, db_vec, dgate, dr1, dk1, dv1],
                            x_grad=[True] * 8, p_grad=[True] * 11 + [False], dx_groups=[[0, 1, 2, 3], [4, 5, 6, 7]],
                            dx_dtypes=[F32], tm=256, name="rwkv_prep_bwd", fold_next=(0, 1), comm=prep_comm)
    prep_out, carried = prep_out if prep_comm is not None else (prep_out, [])
    plan.done("rwkv_prep_bwd", carried, w)
    dp_r = prep_out[0]
    (dmu_r, dmu_k, dmu_v, dmu_lo, g["rwkv_w0"], g["w2_pad"], g["rwkv_a0"], g["a2_pad"], g["g2_pad"],
     g["rwkv_k_k"], g["rwkv_k_a"]) = prep_out[1:]
    g["mu_pad"] = jnp.concatenate([dmu_r, dmu_k, dmu_v, dmu_lo], axis=1)
    dhm = _mm_pair(dp_h, w["w_in_h"], dp_r, w["w_in_r"], tb=True, tm=512, name="inproj_dh")
    g["w_in_h"] = _mm(hm, dp_h, ta=True, tm=D_MODEL, tn=D_MODEL, tk=1024, name="inproj_dw_h")
    g["w_in_r"] = _mm(hm, dp_r, ta=True, tm=D_MODEL, tn=N_RWKV_PAD // 2, tk=1024, name="inproj_dw_r")
    mix_comm = plan.comm("mix_drms", g)
    mix_out = _rowwise_bwd(_rms_f, [x1], [w["mix_norm"]], [dhm], x_grad=[True], p_grad=[True], dx_groups=[[0]],
                           dx_dtypes=[F32], tm=512, name="mix_drms", extra={0: dx2}, comm=mix_comm)
    (dx1, g["mix_norm"]), carried = mix_out if mix_comm is not None else (mix_out, [])
    plan.done("mix_drms", carried, w)
    dx0 = _ffn_bwd(dx1, x, w, ffn1_saved, "ffn1", plan, g)
    return loss, dx0, g


HBM_SPEC = pl.BlockSpec(memory_space=pl.ANY)

Comm = collections.namedtuple("Comm", "arrays out_shapes aliased sem_shapes start finish")


def _join_comms(first, second):
    n, s = len(first.arrays), len(first.sem_shapes)

    def start(ins, outs, sems):
        first.start(ins[:n], outs[:n], sems[:s])
        second.start(ins[n:], outs[n:], sems[s:])

    def finish(ins, outs, sems):
        first.finish(ins[:n], outs[:n], sems[:s])
        second.finish(ins[n:], outs[n:], sems[s:])

    return Comm(list(first.arrays) + list(second.arrays), list(first.out_shapes) + list(second.out_shapes),
                list(first.aliased) + list(second.aliased), list(first.sem_shapes) + list(second.sem_shapes),
                start, finish)


def _run_comm(comm, name):
    n = len(comm.arrays)

    def body(*refs):
        ins, outs, sems = refs[:n], refs[n:2 * n], refs[2 * n:]
        comm.start(ins, outs, sems)
        comm.finish(ins, outs, sems)

    return pl.pallas_call(
        body, name=name, in_specs=[HBM_SPEC] * n, out_specs=[HBM_SPEC] * n, out_shape=list(comm.out_shapes),
        input_output_aliases={t: t for t in range(n) if comm.aliased[t]},
        scratch_shapes=list(comm.sem_shapes))(*comm.arrays)


def _hosting_call(body, comm, *, name, grid, in_specs, out_specs, out_shape, scratch_shapes, args):
    sem = ("arbitrary",) * len(grid)
    if comm is None:
        res = pl.pallas_call(body, name=name, grid=grid, in_specs=in_specs, out_specs=out_specs, out_shape=out_shape,
                             scratch_shapes=scratch_shapes, compiler_params=_params(sem))(*args)
        return list(res), []
    ni, no, ns, nc = len(in_specs), len(out_specs), len(scratch_shapes), len(comm.arrays)

    def wrapped(*refs):
        ins, cins = refs[:ni], refs[ni:ni + nc]
        outs, couts = refs[ni + nc:ni + nc + no], refs[ni + nc + no:ni + 2 * nc + no]
        scr, sems = refs[ni + 2 * nc + no:ni + 2 * nc + no + ns], refs[ni + 2 * nc + no + ns:]
        first = functools.reduce(jnp.logical_and, [pl.program_id(k) == 0 for k in range(len(grid))])
        last = functools.reduce(jnp.logical_and, [pl.program_id(k) == grid[k] - 1 for k in range(len(grid))])

        @pl.when(first)
        def _():
            comm.start(cins, couts, sems)

        body(*ins, *outs, *scr)

        @pl.when(last)
        def _():
            comm.finish(cins, couts, sems)

    res = pl.pallas_call(
        wrapped, name=name, grid=grid, in_specs=list(in_specs) + [HBM_SPEC] * nc,
        out_specs=list(out_specs) + [HBM_SPEC] * nc, out_shape=list(out_shape) + list(comm.out_shapes),
        scratch_shapes=list(scratch_shapes) + list(comm.sem_shapes),
        input_output_aliases={ni + t: no + t for t in range(nc) if comm.aliased[t]},
        compiler_params=_params(sem))(*args, *comm.arrays)
    return list(res[:no]), list(res[no:])


def _chips(x, y):
    return [(1 - x, y), (x, 1 - y), (1 - x, 1 - y)]


def _gather_comm(bufs):
    n = len(bufs)

    def copies(outs, sems):
        ici_send, ici_recv, d2d_send, d2d_recv = sems
        x, y, c = lax.axis_index("x"), lax.axis_index("y"), lax.axis_index("c")

        def half(t, slot, hc):
            hr = bufs[t].shape[1] // 2
            return outs[t].at[slot, pl.ds(pl.multiple_of(hc * hr, 16), hr), :]

        def ici(t, j, slot, px, py):
            return pltpu.make_async_remote_copy(src_ref=half(t, slot, c), dst_ref=half(t, slot, c),
                                                send_sem=ici_send.at[3 * t + j], recv_sem=ici_recv.at[3 * t + j],
                                                device_id=(px, py, c), device_id_type=MESH)

        def d2d(t, j, slot, hc):
            return pltpu.make_async_remote_copy(src_ref=half(t, slot, hc), dst_ref=half(t, slot, hc),
                                                send_sem=d2d_send.at[3 * t + j], recv_sem=d2d_recv.at[3 * t + j],
                                                device_id=(x, y, 1 - c), device_id_type=MESH)

        peers = [(t, j, px, py) for t in range(n) for j, (px, py) in enumerate(_chips(x, y))]
        return ici, d2d, peers, 2 * x + y, c

    def start(ins, outs, sems):
        ici, _, peers, me, _ = copies(outs, sems)
        for t, j, px, py in peers:
            ici(t, j, me, px, py).start()

    def finish(ins, outs, sems):
        ici, d2d, peers, me, c = copies(outs, sems)
        for t, j, px, py in peers:
            ici(t, j, 2 * px + py, px, py).wait_recv()
            d2d(t, j, 2 * px + py, c).start()
        for t, j, px, py in peers:
            d2d(t, j, 2 * px + py, 1 - c).wait_recv()
        for t, j, px, py in peers:
            ici(t, j, me, px, py).wait_send()
            d2d(t, j, 2 * px + py, c).wait_send()

    return Comm(list(bufs), [SDS(b.shape, b.dtype) for b in bufs], [True] * n,
                [pltpu.SemaphoreType.DMA((3 * n,))] * 4, start, finish)


def _sibling_exchange_comm(gs):
    n = len(gs)

    def copies(ins, outs, sems):
        x, y, c = lax.axis_index("x"), lax.axis_index("y"), lax.axis_index("c")
        cps = []
        for t in range(n):
            hr = gs[t].shape[1] // 2
            src = ins[t].at[:, pl.ds(pl.multiple_of((1 - c) * hr, SUBLANES), hr), :]
            cps.append(pltpu.make_async_remote_copy(src_ref=src, dst_ref=outs[t], send_sem=sems[0].at[t],
                                                    recv_sem=sems[1].at[t], device_id=(x, y, 1 - c),
                                                    device_id_type=MESH))
        return cps

    def start(ins, outs, sems):
        for cp in copies(ins, outs, sems):
            cp.start()

    def finish(ins, outs, sems):
        for cp in copies(ins, outs, sems):
            cp.wait()

    return Comm(list(gs), [SDS((N_CHIPS, g.shape[1] // 2, g.shape[2]), g.dtype) for g in gs], [False] * n,
                [pltpu.SemaphoreType.DMA((n,))] * 2, start, finish)


def _own_rows(c, rows):
    return pl.ds(pl.multiple_of(c * (rows // 2), 16), rows // 2)


def _chip_exchange_comm(ss):
    n = len(ss)

    def copies(ins, outs, sems):
        x, y, c = lax.axis_index("x"), lax.axis_index("y"), lax.axis_index("c")
        me = 2 * x + y

        def copy(t, j, px, py, src_slot, dst_slot):
            rows = _own_rows(c, ss[t].shape[1])
            return pltpu.make_async_remote_copy(src_ref=ins[t].at[src_slot, rows, :],
                                                dst_ref=outs[t].at[dst_slot, rows, :],
                                                send_sem=sems[0].at[3 * t + j], recv_sem=sems[1].at[3 * t + j],
                                                device_id=(px, py, c), device_id_type=MESH)

        peers = [(t, j, px, py) for t in range(n) for j, (px, py) in enumerate(_chips(x, y))]
        return copy, peers, me

    def start(ins, outs, sems):
        copy, peers, me = copies(ins, outs, sems)
        for t, j, px, py in peers:
            copy(t, j, px, py, 2 * px + py, me).start()

    def finish(ins, outs, sems):
        copy, peers, me = copies(ins, outs, sems)
        for t, j, px, py in peers:
            copy(t, j, px, py, me, 2 * px + py).wait_recv()
        for t, j, px, py in peers:
            copy(t, j, px, py, 2 * px + py, me).wait_send()

    return Comm(list(ss), [SDS(s.shape, s.dtype) for s in ss], [False] * n,
                [pltpu.SemaphoreType.DMA((3 * n,))] * 2, start, finish)


def _sibling_swap_comm(rs, ss):
    n = len(rs)

    def copies(outs, sems):
        x, y, c = lax.axis_index("x"), lax.axis_index("y"), lax.axis_index("c")
        cps = []
        for t in range(n):
            rows = _own_rows(c, rs[t].shape[1])
            held = [outs[t].at[2 * px + py, rows, :] for px, py in _chips(x, y)] + [outs[n + t].at[2 * x + y, rows, :]]
            cps += [pltpu.make_async_remote_copy(src_ref=ref, dst_ref=ref, send_sem=sems[0].at[4 * t + j],
                                                 recv_sem=sems[1].at[4 * t + j], device_id=(x, y, 1 - c),
                                                 device_id_type=MESH) for j, ref in enumerate(held)]
        return cps

    def start(ins, outs, sems):
        for cp in copies(outs, sems):
            cp.start()

    def finish(ins, outs, sems):
        for cp in copies(outs, sems):
            cp.wait()

    both = list(rs) + list(ss)
    return Comm(both, [SDS(b.shape, b.dtype) for b in both], [True] * (2 * n),
                [pltpu.SemaphoreType.DMA((4 * n,))] * 2, start, finish)


def _row_tile(rows, cap=512):
    best = SUBLANES
    for tr in range(SUBLANES, min(rows, cap) + 1, SUBLANES):
        if rows % tr == 0:
            best = tr
    return best


def _add_halves(g4, r4, c_idx, name):
    _, hr, lanes = r4.shape
    tr = _row_tile(hr)
    nb = hr // tr

    def body(c_ref, a_ref, b_ref, o_ref):
        o_ref[...] = (a_ref[...] + b_ref[...]).astype(o_ref.dtype)

    pair = 2
    owned = pl.BlockSpec((pair, tr, lanes), lambda q, i, c_ref: (q, c_ref[0] * nb + i, 0))
    grid_spec = pltpu.PrefetchScalarGridSpec(
        num_scalar_prefetch=1, grid=(N_CHIPS // pair, nb),
        in_specs=[owned, pl.BlockSpec((pair, tr, lanes), lambda q, i, c_ref: (q, i, 0))], out_specs=owned)
    return pl.pallas_call(body, name=name, grid_spec=grid_spec, out_shape=SDS(g4.shape, BF16),
                          compiler_params=_params(("parallel", "parallel")))(c_idx, g4, r4)


def _adamw(wf, r4, s4, mf, vf, me_idx, name):
    rows, lanes = wf.shape
    tr = rows // 2
    assert tr % 16 == 0
    c1 = 1.0 / (1.0 - ADAM_B1 ** ADAM_STEP)
    c2 = 1.0 / (1.0 - ADAM_B2 ** ADAM_STEP)

    def body(me_ref, w_ref, a_ref, b_ref, c_ref, d_ref, own_ref, m_ref, v_ref, g_ref, delta_ref, nm_ref, nv_ref):
        own = own_ref[...].astype(F32)
        p = [jnp.where(me_ref[0] == q, own, ref[...].astype(F32)) for q, ref in enumerate((a_ref, b_ref, c_ref, d_ref))]
        gv = ((p[0] + p[1]) + p[2]) + p[3]
        m = ADAM_B1 * m_ref[...] + (1.0 - ADAM_B1) * gv
        v = ADAM_B2 * v_ref[...] + (1.0 - ADAM_B2) * (gv * gv)
        g_ref[...] = gv
        delta_ref[...] = -ADAM_LR * ((m * c1) / (jnp.sqrt(v * c2) + ADAM_EPS) + ADAM_WD * w_ref[...])
        nm_ref[...] = m
        nv_ref[...] = v

    other = lambda q: (lambda i, me_ref: (jnp.where(me_ref[0] == q, (q + 1) % N_CHIPS, q), i, 0))
    full = pl.BlockSpec((tr, lanes), lambda i, me_ref: (i, 0))
    grid_spec = pltpu.PrefetchScalarGridSpec(
        num_scalar_prefetch=1, grid=(rows // tr,),
        in_specs=[full] + [pl.BlockSpec((None, tr, lanes), other(q)) for q in range(N_CHIPS)]
        + [pl.BlockSpec((None, tr, lanes), lambda i, me_ref: (me_ref[0], i, 0)), full, full],
        out_specs=[full] * 4)
    return pl.pallas_call(body, name=name, grid_spec=grid_spec, out_shape=[SDS((rows, lanes), F32)] * 4,
                          compiler_params=_params(("parallel",)))(me_idx, wf, r4, r4, r4, r4, s4, mf, vf)


BIG = ("ffn1_w_gate", "ffn1_w_up", "ffn1_w_down", "ffn2_w_gate", "ffn2_w_up", "ffn2_w_down", "w_out", "w_in")
TRANSPOSED = ("ffn1_w_gate", "ffn1_w_up", "ffn2_w_gate", "ffn2_w_up")
PACKED = ("rwkv_w2", "rwkv_a2", "rwkv_g2")
SMALL_SHAPES = {"ffn1_norm": (1, D_MODEL), "mix_norm": (1, D_MODEL), "hgrn_lb_logits": (2, W_A),
                "hgrn_out_norm": (1, W_A), "rwkv_shift_mu": (1, N_RWKV_COLS), "rwkv_w0": (1, W_B),
                "rwkv_a0": (1, W_B), "rwkv_k_k": (1, W_B), "rwkv_k_a": (1, W_B),
                "rwkv_r_k": (1, HB_HEADS, HB_DIM), "rwkv_gn_w": (1, W_B), "rwkv_gn_b": (1, W_B),
                "ffn2_norm": (1, D_MODEL), "final_norm": (D_MODEL,)}
PACK_ELEMS = sum(_numel(_shard_shape(n)) for n in PACKED) + sum(_numel(SMALL_SHAPES[n]) for n in SMALL)
PACK_ROWS = -(-PACK_ELEMS // (32 * LANES)) * 32


def _to_rows(name, shard):
    return shard[0].T if name in TRANSPOSED else shard[0]


def _from_rows(name, rows):
    return (rows.T if name in TRANSPOSED else rows)[None]


def _pack(sharded, small):
    flat = jnp.concatenate([sharded[n].reshape(-1) for n in PACKED] + [small[n].reshape(-1) for n in SMALL])
    return jnp.pad(flat, (0, PACK_ROWS * LANES - flat.shape[0])).reshape(PACK_ROWS, LANES)


def _unpack(packed):
    flat, out, off = packed.reshape(-1), {}, 0
    for n in PACKED:
        shp = _shard_shape(n)
        out[n] = flat[off:off + _numel(shp)].reshape((1,) + shp)
        off += _numel(shp)
    for n in SMALL:
        shp = SMALL_SHAPES[n]
        out[n] = flat[off:off + _numel(shp)].reshape(shp)
        off += _numel(shp)
    return out


def _quarter(full, name, q):
    shape, ax = SHARDED_SHAPES[name]
    w = shape[ax] // N_CHIPS
    return lax.slice_in_dim(full, q * w, (q + 1) * w, axis=ax)


def kernel(x, ffn1_norm, ffn1_w_gate, ffn1_w_up, ffn1_w_down, mix_norm, w_in, hgrn_lb_logits, hgrn_out_norm, rwkv_shift_mu, rwkv_w0, rwkv_w2, rwkv_a0, rwkv_a2, rwkv_g2, rwkv_k_k, rwkv_k_a, rwkv_r_k, rwkv_gn_w, rwkv_gn_b, w_out, ffn2_norm, ffn2_w_gate, ffn2_w_up, ffn2_w_down, final_norm, loss_target, m_ffn1_norm, m_ffn1_w_gate, m_ffn1_w_up, m_ffn1_w_down, m_mix_norm, m_w_in, m_hgrn_lb_logits, m_hgrn_out_norm, m_rwkv_shift_mu, m_rwkv_w0, m_rwkv_w2, m_rwkv_a0, m_rwkv_a2, m_rwkv_g2, m_rwkv_k_k, m_rwkv_k_a, m_rwkv_r_k, m_rwkv_gn_w, m_rwkv_gn_b, m_w_out, m_ffn2_norm, m_ffn2_w_gate, m_ffn2_w_up, m_ffn2_w_down, m_final_norm, v_ffn1_norm, v_ffn1_w_gate, v_ffn1_w_up, v_ffn1_w_down, v_mix_norm, v_w_in, v_hgrn_lb_logits, v_hgrn_out_norm, v_rwkv_shift_mu, v_rwkv_w0, v_rwkv_w2, v_rwkv_a0, v_rwkv_a2, v_rwkv_g2, v_rwkv_k_k, v_rwkv_k_a, v_rwkv_r_k, v_rwkv_gn_w, v_rwkv_gn_b, v_w_out, v_ffn2_norm, v_ffn2_w_gate, v_ffn2_w_up, v_ffn2_w_down, v_final_norm):
    args = dict(locals())
    wts = {n: args[n] for n in ALL_WEIGHTS}
    moms = {n: args["m_" + n] for n in ALL_WEIGHTS}
    vars_ = {n: args["v_" + n] for n in ALL_WEIGHTS}

    me = 2 * lax.axis_index("x") + lax.axis_index("y")
    c_idx = lax.axis_index("c").astype(jnp.int32).reshape(1)
    me_idx = me.astype(jnp.int32).reshape(1)
    shard_of = {n: _to_rows(n, wts[n]).astype(BF16) for n in BIG}
    shard_of["packed"] = _pack(wts, {n: wts[n] for n in SMALL}).astype(BF16)
    group = {"ffn1": BIG[0:3], "ffn2": BIG[3:6]}

    def slot_bufs(names):
        return [lax.dynamic_update_slice(lax.empty((N_CHIPS,) + shard_of[n].shape, BF16), shard_of[n][None],
                                         (me, 0, 0)) for n in names]

    def ffn_weights(tag, gathered):
        return {f"{tag}_wgt": gathered[0].reshape(D_FF, D_MODEL), f"{tag}_wut": gathered[1].reshape(D_FF, D_MODEL),
                f"{tag}_wd": gathered[2].reshape(D_FF, D_MODEL)}

    def w_in_weights(gathered):
        w_in_full = jnp.concatenate([gathered[0][q] for q in range(N_CHIPS)], axis=1)
        return {"w_in_h": w_in_full[:, :N_HGRN_COLS],
                "w_in_r": jnp.pad(w_in_full[:, N_HGRN_COLS:], ((0, 0), (0, N_RWKV_PAD - N_RWKV_COLS)))}

    def mixer_weights(gathered):
        w_out_full = gathered[0].reshape(D_MODEL, D_MODEL)
        packs = gathered[1].reshape(N_CHIPS, PACK_ROWS * LANES)
        full, off = {}, 0
        for n in PACKED:
            shp = _shard_shape(n)
            full[n] = jnp.concatenate([packs[q, off:off + _numel(shp)].reshape(shp) for q in range(N_CHIPS)], axis=1)
            off += _numel(shp)
        zrow = lambda nrow: jnp.zeros((nrow, W_B), BF16)
        return {"w_out": w_out_full,
                "w2_pad": jnp.concatenate([full["rwkv_w2"], zrow(LORA_PAD - 32)], axis=0),
                "a2_pad": jnp.concatenate([zrow(32), full["rwkv_a2"], zrow(LORA_PAD - 64)], axis=0),
                "g2_pad": jnp.concatenate([zrow(64), full["rwkv_g2"], zrow(LORA_PAD - 160)], axis=0)}

    plan = _Plan()
    w = {}
    plan.carry("ffn1_rms", lambda g: _gather_comm(slot_bufs(group["ffn1"][:2])),
               lambda res, w_: w_.update({"ffn1_wgt": res[0].reshape(D_FF, D_MODEL),
                                          "ffn1_wut": res[1].reshape(D_FF, D_MODEL)}))

    def after_gate_up(res, w_):
        w_["ffn1_wd"] = res[0].reshape(D_FF, D_MODEL)
        w_.update(w_in_weights(res[1:]))

    plan.carry("ffn1_gate_up", lambda g: _gather_comm(slot_bufs(("ffn1_w_down", "w_in"))), after_gate_up)
    plan.carry("ffn1_down", lambda g: _gather_comm(slot_bufs(("w_out", "packed"))),
               lambda res, w_: w_.update(mixer_weights(res)))
    plan.carry("rwkv_fwd", lambda g: _gather_comm(slot_bufs(group["ffn2"])),
               lambda res, w_: w_.update(ffn_weights("ffn2", res)))
    w["ffn1_norm"], w["ffn2_norm"] = ffn1_norm, ffn2_norm
    w["mix_norm"] = mix_norm
    w["lb0"], w["lb1"] = hgrn_lb_logits[0:1], hgrn_lb_logits[1:2]
    w["hgrn_out_norm"] = hgrn_out_norm
    w["mu_pad"] = jnp.pad(rwkv_shift_mu, ((0, 0), (0, N_RWKV_PAD - N_RWKV_COLS)))
    for n in ("rwkv_w0", "rwkv_a0", "rwkv_k_k", "rwkv_k_a", "rwkv_gn_w", "rwkv_gn_b"):
        w[n] = wts[n]
    w["rwkv_r_k"] = rwkv_r_k.reshape(1, W_B)
    w["final_norm"] = final_norm.reshape(1, D_MODEL)

    def reduce_rows(names, gs):
        r1 = _run_comm(_sibling_exchange_comm(gs), "grad_sibling_exchange")
        s4 = [_add_halves(gt, rt, c_idx, f"grad_add_halves_{n}") for gt, rt, n in zip(gs, r1, names)]
        return list(zip(_run_comm(_chip_exchange_comm(s4), "grad_chip_exchange"), s4))

    def swap_comm(names):
        return _sibling_swap_comm([early[n][0] for n in names], [early[n][1] for n in names])

    def after_swap(names):
        return lambda res, w_: swapped.update(zip(names, zip(res[:len(names)], res[len(names):])))

    early, swapped = {}, {}

    def reduce_early(names, grads_of, sibling_host, chips_host, swap_host):
        def sibling_comm(g):
            early[names, "gs"] = grads_of(g)
            return _sibling_exchange_comm(early[names, "gs"])

        def after_sibling(res, w_):
            early[names, "s4"] = [_add_halves(gt, rt, c_idx, f"grad_add_halves_{n}")
                                  for gt, rt, n in zip(early[names, "gs"], res, names)]

        plan.carry(sibling_host, sibling_comm, after_sibling)
        plan.carry(chips_host, lambda g: _chip_exchange_comm(early[names, "s4"]),
                   lambda res, w_: early.update(zip(names, zip(res, early[names, "s4"]))))
        if swap_host:
            plan.carry(swap_host, lambda g: swap_comm(names), after_swap(names))

    def proj_grads(g):
        g_w_in = jnp.concatenate([g["w_in_h"], g["w_in_r"][:, :N_RWKV_COLS]], axis=1)
        return [g["w_out"].reshape(N_CHIPS, -1, D_MODEL),
                jnp.stack([_quarter(g_w_in, "w_in", q) for q in range(N_CHIPS)])]

    rows_of = lambda keys: (lambda g: [g[k].reshape(N_CHIPS, -1, D_MODEL) for k in keys])
    reduce_early(group["ffn2"], rows_of(("ffn2_wgt", "ffn2_wut", "ffn2_wd")), "hgrn_bwd", "rwkv_bwd", "rwkv_prep_bwd")
    reduce_early(("w_out", "w_in"), proj_grads, "mix_drms", "ffn1_dact", "ffn1_dwg")
    reduce_early(("ffn1_w_down",), rows_of(("ffn1_wd",)), "ffn1_dwg", "ffn1_dwu", "ffn1_dh_g")
    reduce_early(("ffn1_w_gate",), rows_of(("ffn1_wgt",)), "ffn1_dwu", "ffn1_dh_g", "ffn1_dh_u")
    reduce_early(("ffn1_w_up",), rows_of(("ffn1_wut",)), "ffn1_dh_g", "ffn1_dh_u", None)
    loss_slab, grad_x, g = _local_step(x[0], loss_target[0], w, plan)
    loss = lax.psum(loss_slab[0, 0], ("x", "y", "c"))

    gfull = {
        "rwkv_w2": g["w2_pad"][0:32], "rwkv_a2": g["a2_pad"][32:64], "rwkv_g2": g["g2_pad"][64:160],
    }
    gsmall = {
        "ffn1_norm": g["ffn1_norm"], "mix_norm": g["mix_norm"],
        "hgrn_lb_logits": jnp.concatenate([g["lb0"], g["lb1"]], axis=0), "hgrn_out_norm": g["hgrn_out_norm"],
        "rwkv_shift_mu": g["mu_pad"][:, :N_RWKV_COLS], "rwkv_w0": g["rwkv_w0"], "rwkv_a0": g["rwkv_a0"],
        "rwkv_k_k": g["rwkv_k_k"], "rwkv_k_a": g["rwkv_k_a"], "rwkv_r_k": g["rwkv_r_k"],
        "rwkv_gn_w": g["rwkv_gn_w"], "rwkv_gn_b": g["rwkv_gn_b"], "ffn2_norm": g["ffn2_norm"],
        "final_norm": g["final_norm"],
    }
    packed = jnp.stack([_pack({n: _quarter(gfull[n], n, q) for n in PACKED}, gsmall) for q in range(N_CHIPS)])
    early["packed"], = reduce_rows(["packed"], [packed])
    last = ["ffn1_w_up", "packed"]
    after_swap(last)(_run_comm(swap_comm(last), "grad_sibling_swap"), w)
    names = list(BIG) + ["packed"]

    def rows_list(d):
        return [_to_rows(n, d[n]) for n in BIG] + [_pack(d, {n: d[n] for n in SMALL})]

    outs = [_adamw(wt, *swapped[n], mt, vt, me_idx, f"adamw_{n}")
            for wt, mt, vt, n in zip(rows_list(wts), rows_list(moms), rows_list(vars_), names)]
    results = []
    for k in range(4):
        per = [outs[i][k] for i in range(len(names))]
        d = {n: _from_rows(n, z) for n, z in zip(BIG, per[:-1])}
        d.update(_unpack(per[-1]))
        results.append(d)
    return (loss, grad_x[None], *[r[n] for r in results for n in ALL_WEIGHTS])
```

```python
import collections
import functools

import jax
import jax.numpy as jnp
from jax import lax
from jax.experimental import pallas as pl
from jax.experimental.pallas import tpu as pltpu

F32 = jnp.float32
BF16 = jnp.bfloat16
SDS = jax.ShapeDtypeStruct
MESH = pl.DeviceIdType.MESH

D_MODEL = 1024
D_FF = 2816
W_A = 512
W_B = 512
HA_HEADS, HA_DIM = 4, 128
HB_HEADS, HB_DIM = 8, 64
HGRN_CHUNK = 64
HGRN_GROUP = 8
RWKV_CHUNK = 16
RWKV_GROUP = 8
N_HGRN_COLS = 4 * W_A
N_RWKV_COLS = 3 * W_B + 32 + 32 + 96
N_RWKV_PAD = 1792
LORA_PAD = 256
NORM_EPS = 1e-6
RWKV_GN_EPS = 64e-5
L2_EPS = 1e-12
ADAM_LR, ADAM_B1, ADAM_B2, ADAM_EPS, ADAM_WD, ADAM_STEP = 0.001, 0.9, 0.999, 1e-8, 0.01, 10

N_CHIPS = 4
VMEM_LIMIT_V7X = 56 * 1024 * 1024
LANES = 1024

SHARDED_SHAPES = {
    "ffn1_w_gate": ((D_MODEL, D_FF), 1), "ffn1_w_up": ((D_MODEL, D_FF), 1), "ffn1_w_down": ((D_FF, D_MODEL), 0),
    "w_in": ((D_MODEL, N_HGRN_COLS + N_RWKV_COLS), 1), "rwkv_w2": ((32, W_B), 1), "rwkv_a2": ((32, W_B), 1),
    "rwkv_g2": ((96, W_B), 1), "w_out": ((D_MODEL, D_MODEL), 0),
    "ffn2_w_gate": ((D_MODEL, D_FF), 1), "ffn2_w_up": ((D_MODEL, D_FF), 1), "ffn2_w_down": ((D_FF, D_MODEL), 0),
}
SMALL = ("ffn1_norm", "mix_norm", "hgrn_lb_logits", "hgrn_out_norm", "rwkv_shift_mu", "rwkv_w0", "rwkv_a0",
         "rwkv_k_k", "rwkv_k_a", "rwkv_r_k", "rwkv_gn_w", "rwkv_gn_b", "ffn2_norm", "final_norm")
ALL_WEIGHTS = ("ffn1_norm", "ffn1_w_gate", "ffn1_w_up", "ffn1_w_down", "mix_norm", "w_in", "hgrn_lb_logits",
               "hgrn_out_norm", "rwkv_shift_mu", "rwkv_w0", "rwkv_w2", "rwkv_a0", "rwkv_a2", "rwkv_g2", "rwkv_k_k",
               "rwkv_k_a", "rwkv_r_k", "rwkv_gn_w", "rwkv_gn_b", "w_out", "ffn2_norm", "ffn2_w_gate", "ffn2_w_up",
               "ffn2_w_down", "final_norm")


def _shard_shape(name):
    shape, ax = SHARDED_SHAPES[name]
    return tuple(s // N_CHIPS if i == ax else s for i, s in enumerate(shape))


def _numel(shape):
    n = 1
    for s in shape:
        n *= s
    return n


def _params(sem=None):
    return pltpu.CompilerParams(dimension_semantics=sem, vmem_limit_bytes=VMEM_LIMIT_V7X)


def _split2(x):
    hi = x.astype(BF16)
    return hi, (x.astype(F32) - hi.astype(F32)).astype(BF16)


def _dg(x, y, cx, cy, hi):
    dn = (((cx,), (cy,)), ((), ()))
    dot = lambda p, q: lax.dot_general(p, q, dn, preferred_element_type=F32)
    if hi == "x3":
        (xh, xl), (yh, yl) = _split2(x), _split2(y)
        return dot(xh, yh) + (dot(xh, yl) + dot(xl, yh))
    return dot(x.astype(BF16), y.astype(BF16))


def _make_mm(hi, cotangent_forms=None):
    @jax.custom_vjp
    def nn(x, y):
        return _dg(x, y, 1, 0, hi)

    @jax.custom_vjp
    def nt(x, y):
        return _dg(x, y, 1, 1, hi)

    @jax.custom_vjp
    def tn(x, y):
        return _dg(x, y, 0, 0, hi)

    bnn, bnt, btn = cotangent_forms or (nn, nt, tn)
    nn.defvjp(lambda x, y: (nn(x, y), (x, y)), lambda r, g: (bnt(g, r[1]), btn(r[0], g)))
    nt.defvjp(lambda x, y: (nt(x, y), (x, y)), lambda r, g: (bnn(g, r[1]), btn(g, r[0])))
    tn.defvjp(lambda x, y: (tn(x, y), (x, y)), lambda r, g: (bnt(r[1], g), bnn(r[0], g)))
    return nn, nt, tn


_nn, _nt, _tn = _make_mm(False)
_nn_x3, _nt_x3, _tn_x3 = _make_mm("x3", (_nn, _nt, _tn))


def _tri_apply(x, transpose):
    c = x.shape[0]
    tri = (lax.broadcasted_iota(jnp.int32, (c, c), 1) <= lax.broadcasted_iota(jnp.int32, (c, c), 0)).astype(BF16)
    dn = (((0 if transpose else 1,), (0,)), ((), ()))
    p1, p2 = _split2(x)
    dot = lambda p: lax.dot_general(tri, p, dn, preferred_element_type=F32)
    return dot(p1) + dot(p2)


@jax.custom_vjp
def _cumsum_rows(x):
    return _tri_apply(x, False)


_cumsum_rows.defvjp(lambda x: (_tri_apply(x, False), None), lambda _, g: (_tri_apply(g, True),))


def _sigmoid(x):
    return 1.0 / (1.0 + jnp.exp(-x))


def _silu(x):
    return x * _sigmoid(x)


def _softplus(z):
    return jnp.maximum(z, 0.0) + jnp.log(1.0 + jnp.exp(-jnp.abs(z)))


def _mm(a, b, *, ta=False, tb=False, tm, tn, tk, name, out_dtype=F32, res=None, scale=None, comm=None):
    m = a.shape[1] if ta else a.shape[0]
    kdim = a.shape[0] if ta else a.shape[1]
    n = b.shape[0] if tb else b.shape[1]
    assert (b.shape[1] if tb else b.shape[0]) == kdim
    tm, tn, tk = min(tm, m), min(tn, n), min(tk, kdim)
    assert m % tm == 0 and n % tn == 0 and kdim % tk == 0, (name, m, n, kdim)
    nk = kdim // tk
    a_spec = pl.BlockSpec((tk, tm), lambda i, j, k: (k, i)) if ta else pl.BlockSpec((tm, tk), lambda i, j, k: (i, k))
    b_spec = pl.BlockSpec((tn, tk), lambda i, j, k: (j, k)) if tb else pl.BlockSpec((tk, tn), lambda i, j, k: (k, j))
    o_spec = pl.BlockSpec((tm, tn), lambda i, j, k: (i, j))
    ca, cb = (0 if ta else 1), (1 if tb else 0)

    def body(*refs):
        if res is not None:
            a_ref, b_ref, r_ref, o_ref, acc_ref = refs
        else:
            a_ref, b_ref, o_ref, acc_ref = refs
        k = pl.program_id(2)

        @pl.when(k == 0)
        def _():
            acc_ref[...] = jnp.zeros_like(acc_ref)

        acc_ref[...] += _dg(a_ref[...], b_ref[...], ca, cb, False)

        @pl.when(k == nk - 1)
        def _():
            acc = acc_ref[...]
            if scale is not None:
                acc = acc * scale
            if res is not None:
                acc = r_ref[...] + acc
            o_ref[...] = acc.astype(out_dtype)

    in_specs = [a_spec, b_spec] + ([o_spec] if res is not None else [])
    args = (a, b) + ((res,) if res is not None else ())
    if comm is None:
        return pl.pallas_call(
            body, name=name, grid=(m // tm, n // tn, nk), in_specs=in_specs, out_specs=o_spec,
            out_shape=SDS((m, n), out_dtype), scratch_shapes=[pltpu.VMEM((tm, tn), F32)],
            compiler_params=_params(("parallel", "parallel", "arbitrary")))(*args)
    (out,), carried = _hosting_call(
        body, comm, name=name, grid=(m // tm, n // tn, nk), in_specs=in_specs, out_specs=[o_spec],
        out_shape=[SDS((m, n), out_dtype)], scratch_shapes=[pltpu.VMEM((tm, tn), F32)], args=args)
    return out, carried


def _mm_pair(a1, b1, a2, b2, *, tb=False, tm, name):
    m = a1.shape[0]
    n = b1.shape[0] if tb else b1.shape[1]
    tm = min(tm, m)
    cb = 1 if tb else 0
    assert a2.shape[0] == m and m % tm == 0
    assert all(b.shape[cb] == a.shape[1] and b.shape[1 - cb] == n for a, b in ((a1, b1), (a2, b2)))

    def body(a1_ref, b1_ref, a2_ref, b2_ref, o_ref):
        o_ref[...] = _dg(a1_ref[...], b1_ref[...], 1, cb, False) + _dg(a2_ref[...], b2_ref[...], 1, cb, False)

    a_spec = lambda a: pl.BlockSpec((tm, a.shape[1]), lambda i: (i, 0))
    b_spec = lambda b: pl.BlockSpec(b.shape, lambda i: (0, 0))
    return pl.pallas_call(
        body, name=name, grid=(m // tm,), in_specs=[a_spec(a1), b_spec(b1), a_spec(a2), b_spec(b2)],
        out_specs=pl.BlockSpec((tm, n), lambda i: (i, 0)), out_shape=SDS((m, n), F32),
        compiler_params=_params(("parallel",)))(a1, b1, a2, b2)


def _row_spec(x, tm, tile_of=lambda i: i):
    if isinstance(x, tuple):
        arr, w, j = x
        return arr, pl.BlockSpec((tm, w), lambda i, j=j: (tile_of(i), j))
    return x, pl.BlockSpec((tm, x.shape[1]), lambda i: (tile_of(i), 0))


def _par_spec(p):
    if isinstance(p, tuple):
        arr, w, j = p
        return arr, pl.BlockSpec((arr.shape[0], w), lambda i, j=j: (0, j))
    return p, pl.BlockSpec(p.shape, lambda i: (0, 0))


def _store_groups(refs, groups, vals):
    for ref, idxs in zip(refs, groups):
        off = 0
        for ix in idxs:
            v = vals[ix]
            ref[:, off:off + v.shape[1]] = v.astype(ref.dtype)
            off += v.shape[1]


SUBLANES = 8


def _x_plan(xs, tm, t, tile_of=lambda i: i):
    arrays, specs, plan = [], [], []
    nb = tm // SUBLANES
    for x in xs:
        if isinstance(x, tuple) and isinstance(x[0], str):
            kind, arr, w, j = x
            if kind == "prev":
                halo = lambda i, j=j: (jnp.maximum(tile_of(i) * nb - 1, 0), j)
            else:
                halo = lambda i, j=j: (jnp.minimum((tile_of(i) + 1) * nb, t // SUBLANES - 1), j)
            arrays += [arr, arr]
            specs += [pl.BlockSpec((tm, w), lambda i, j=j: (tile_of(i), j)), pl.BlockSpec((SUBLANES, w), halo)]
            plan.append((kind, 2, w))
        else:
            arr, spec = _row_spec(x, tm, tile_of)
            arrays.append(arr)
            specs.append(spec)
            plan.append(("plain", 1, spec.block_shape[1]))
    return arrays, specs, plan


def _x_vals(refs, plan, tm, nt, tile_of=lambda i: i):
    vals, k = [], 0
    i = tile_of(pl.program_id(0))
    rows = lax.broadcasted_iota(jnp.int32, (tm, 1), 0)
    for kind, n, _ in plan:
        main = refs[k][...].astype(F32)
        if kind == "prev":
            edge = jnp.where(i == 0, 0.0, refs[k + 1][SUBLANES - 1:SUBLANES, :].astype(F32))
            main = jnp.where(rows == 0, edge, pltpu.roll(main, 1, 0))
        elif kind == "next":
            edge = jnp.where(i == nt - 1, 0.0, refs[k + 1][0:1, :].astype(F32))
            main = jnp.where(rows == tm - 1, edge, pltpu.roll(main, tm - 1, 0))
        vals.append(main)
        k += n
    return vals


def _tile_rows(xs, tm):
    arr = xs[0]
    if isinstance(arr, tuple):
        arr = arr[1] if isinstance(arr[0], str) else arr[0]
    return min(tm, arr.shape[0]), arr.shape[0]


def _rowwise(f, xs, params, out_groups, out_dtypes, *, tm, name, comm=None):
    tm, t = _tile_rows(xs, tm)
    nt = t // tm
    xa, xspecs, plan = _x_plan(xs, tm, t)
    pa, pspecs = (zip(*[_par_spec(p) for p in params]) if params else ((), ()))
    nxr, npar = len(xa), len(pa)
    x_sds = [SDS((tm, w), F32) for _, _, w in plan]
    p_sds = [SDS(s.block_shape, F32) for s in pspecs]
    outs_sds = jax.eval_shape(lambda *vals: f(*vals), *x_sds, *p_sds)
    widths = [sum(outs_sds[ix].shape[1] for ix in idxs) for idxs in out_groups]

    def body(*refs):
        vals = _x_vals(refs[:nxr], plan, tm, nt) + [r[...].astype(F32) for r in refs[nxr:nxr + npar]]
        outs = f(*vals)
        _store_groups(refs[nxr + npar:], out_groups, outs)

    res, carried = _hosting_call(
        body, comm, name=name, grid=(nt,), in_specs=list(xspecs) + list(pspecs),
        out_specs=[pl.BlockSpec((tm, w), lambda i: (i, 0)) for w in widths],
        out_shape=[SDS((t, w), dt) for w, dt in zip(widths, out_dtypes)], scratch_shapes=[], args=(*xa, *pa))
    return res if comm is None else (res, carried)


def _rowwise_bwd(f, xs, params, cots, *, x_grad, p_grad, dx_groups, dx_dtypes, tm, name, extra=None, comm=None,
                 fold_next=None):
    tm, t = _tile_rows(xs, tm)
    nt = t // tm
    tile_of = (lambda i: nt - 1 - i) if fold_next else (lambda i: i)
    xa, xspecs, plan = _x_plan(xs, tm, t, tile_of)
    pa, pspecs = (zip(*[_par_spec(p) for p in params]) if params else ((), ()))
    ca, cspecs = zip(*[_row_spec(c, tm, tile_of) for c in cots])
    extra = extra or {}
    ekeys = sorted(extra)
    ea, especs = (zip(*[_row_spec(extra[k], tm, tile_of) for k in ekeys]) if ekeys else ((), ()))
    nx, nxr, npar, nc, ne = len(plan), len(xa), len(pa), len(ca), len(ea)
    gx = [i for i in range(nx) if x_grad[i]]
    gp = [i for i in range(npar) if p_grad[i]]
    all_widths = [sum(plan[gx[ix]][2] for ix in idxs) for idxs in dx_groups]
    emitted = [k for k in range(len(dx_groups)) if not (fold_next and k == fold_next[1])]
    widths = [all_widths[k] for k in emitted]
    ng = len(emitted)

    def body(*refs):
        ins = refs[:nxr + npar + nc + ne]
        outs = refs[nxr + npar + nc + ne:]
        vals = _x_vals(ins[:nxr], plan, tm, nt, tile_of) + [r[...].astype(F32) for r in ins[nxr:nxr + npar]]
        cvals = tuple(r[...].astype(F32) for r in ins[nxr + npar:nxr + npar + nc])
        evals = [r[...].astype(F32) for r in ins[nxr + npar + nc:]]
        diff_idx = gx + [nx + i for i in gp]

        def g(*dargs):
            full = list(vals)
            for ix, v in zip(diff_idx, dargs):
                full[ix] = v
            return tuple(f(*full))

        _, vjp = jax.vjp(g, *[vals[ix] for ix in diff_idx])
        grads = vjp(cvals)
        dxs = list(grads[:len(gx)])
        for k, ev in zip(ekeys, evals):
            dxs[k] = dxs[k] + ev
        _store_groups(outs[:ng], [dx_groups[k] for k in emitted], dxs)
        i = pl.program_id(0)
        if fold_next:
            main_ref, carry_ref = outs[emitted.index(fold_next[0])], refs[-1]
            rows = lax.broadcasted_iota(jnp.int32, (tm, 1), 0)
            off = 0
            for ix in dx_groups[fold_next[1]]:
                piece = dxs[ix]
                cols = slice(off, off + piece.shape[1])
                edge = jnp.where(i == 0, 0.0, carry_ref[0:1, cols])
                main_ref[:, cols] += jnp.where(rows == tm - 1, edge, pltpu.roll(piece, tm - 1, 0))
                carry_ref[:, cols] = piece[:SUBLANES]
                off += piece.shape[1]
        for ref, gval in zip(outs[ng:ng + len(gp)], grads[len(gx):]):
            @pl.when(i == 0)
            def _(ref=ref):
                ref[...] = jnp.zeros_like(ref)
            ref[...] += gval

    dp_specs = [pl.BlockSpec(pspecs[i].block_shape, lambda i: (0, 0)) for i in gp]
    dp_shapes = [SDS(pspecs[i].block_shape, F32) for i in gp]
    scratch = [pltpu.VMEM((SUBLANES, all_widths[fold_next[1]]), F32)] if fold_next else []
    res, carried = _hosting_call(
        body, comm, name=name, grid=(nt,), in_specs=list(xspecs) + list(pspecs) + list(cspecs) + list(especs),
        out_specs=[pl.BlockSpec((tm, w), lambda i: (tile_of(i), 0)) for w in widths] + dp_specs,
        out_shape=[SDS((t, w), dt) for w, dt in zip(widths, dx_dtypes)] + dp_shapes, scratch_shapes=scratch,
        args=(*xa, *pa, *ca, *ea))
    return res if comm is None else (res, carried)


def _rms_f(x, g):
    return (x * lax.rsqrt(jnp.mean(x * x, axis=-1, keepdims=True) + NORM_EPS) * g,)


def _group_sum_impl(x, ones_bd):
    p1, p2 = _split2(x)
    dot = lambda p: lax.dot_general(p, ones_bd.astype(BF16), (((1,), (0,)), ((), ())), preferred_element_type=F32)
    return dot(p1) + dot(p2)


@jax.custom_vjp
def _group_sum(x, ones_bd):
    return _group_sum_impl(x, ones_bd)


_group_sum.defvjp(lambda x, o: (_group_sum_impl(x, o), o),
                  lambda o, g: (_group_sum_impl(g, o), jnp.zeros_like(o)))


def _rwkv_prep_f(r, k, v, lo, rp, kp, vp, lop, mu_r, mu_k, mu_v, mu_lo, w0, w2p, a0, a2p, g2p, k_k, k_a, ones_bd):
    r = r + mu_r * (rp - r)
    k = k + mu_k * (kp - k)
    v = v + mu_v * (vp - v)
    lo = lo + mu_lo * (lop - lo)
    w_log = -_softplus(-(w0 + _nn(jnp.tanh(lo), w2p))) - 0.5
    lw = -jnp.exp(w_log)
    a_g = _sigmoid(a0 + _nn(lo, a2p))
    g = _nn(_sigmoid(lo), g2p)
    kk = k * k_k
    kk = kk / jnp.maximum(jnp.sqrt(_group_sum(kk * kk, ones_bd)), L2_EPS)
    k2 = k * (1.0 + (a_g - 1.0) * k_a)
    return r, lw, k2, v, -kk, kk * a_g, g


def _rwkv_post_f(y, r, k2, v, g, r_k, gn_w, gn_b, ones_bd):
    inv_n = 1.0 / HB_DIM
    mean = _group_sum(y, ones_bd) * inv_n
    yc = y - mean
    var = _group_sum(yc * yc, ones_bd) * inv_n
    yn = yc * lax.rsqrt(var + RWKV_GN_EPS) * gn_w + gn_b
    bonus = _group_sum(r * k2 * r_k, ones_bd) * v
    return ((yn + bonus) * g,)


def _tri(c, strict=False):
    ii = lax.broadcasted_iota(jnp.int32, (c, c), 0)
    jj = lax.broadcasted_iota(jnp.int32, (c, c), 1)
    return (jj < ii) if strict else (jj <= ii)


def _hgrn_step(st0, q_a, f_a, i_a, g_a, l0, l1, onorm):
    nh, nj = len(q_a), len(q_a[0])
    c = q_a[0][0].shape[0]
    combos = [(j, h) for j in range(nj) for h in range(nh)]
    every = lambda fn: {q: fn(q) for q in combos}
    at_ = lambda d: (lambda q: d[q[1]][q[0]])
    qa_, fa_, ia_, ga_ = (at_(z) for z in (q_a, f_a, i_a, g_a))
    incl = _tri(c)
    rows = lax.broadcasted_iota(jnp.int32, (c, 1), 0)
    lb = []
    for h in range(nh):
        mx = jnp.maximum(l0[h], l1[h])
        e0, e1 = jnp.exp(l0[h] - mx), jnp.exp(l1[h] - mx)
        lb.append(e0 / (e0 + e1))
    forget = every(lambda q: lb[q[1]] + (1.0 - lb[q[1]]) * _sigmoid(fa_(q)))
    qs = every(lambda q: _silu(qa_(q)))
    kk = every(lambda q: 1.0 - forget[q])
    lf = every(lambda q: jnp.log(forget[q]))
    bcum = every(lambda q: _cumsum_rows(lf[q]))
    bref = every(lambda q: jnp.sum(jnp.where(rows <= c // 2, lf[q], 0.0), axis=0, keepdims=True))
    blast = every(lambda q: jnp.sum(lf[q], axis=0, keepdims=True))
    scores = every(lambda q: jnp.where(incl, _nt(qs[q] * jnp.exp(bcum[q] - bref[q]),
                                                 kk[q] * jnp.exp(bref[q] - bcum[q])), 0.0))
    intra = every(lambda q: _nn(scores[q], ia_(q)))
    qb = every(lambda q: qs[q] * jnp.exp(bcum[q]))
    upd = every(lambda q: _tn(ia_(q), kk[q] * jnp.exp(blast[q] - bcum[q])))
    dec = every(lambda q: jnp.exp(blast[q]))
    st = list(st0)
    o = {}
    for j in range(nj):
        for h in range(nh):
            o[(j, h)] = intra[(j, h)] + _nt(qb[(j, h)], st[h])
        st = [st[h] * dec[(j, h)] + upd[(j, h)] for h in range(nh)]
    out = every(lambda q: o[q] * lax.rsqrt(jnp.mean(o[q] * o[q], axis=-1, keepdims=True) + NORM_EPS)
                * onorm[q[1]] * _silu(ga_(q)))
    return [[out[(j, h)] for j in range(nj)] for h in range(nh)], st


def _hgrn_blocks(ref, nj, c):
    return [[ref[j * c:(j + 1) * c, h * HA_DIM:(h + 1) * HA_DIM] for j in range(nj)] for h in range(HA_HEADS)]


def _hgrn_cols(ref):
    return [ref[:, h * HA_DIM:(h + 1) * HA_DIM] for h in range(HA_HEADS)]


def _hgrn_fwd(p_h, l0, l1, onorm):
    t = p_h.shape[0]
    cc, nj = HGRN_CHUNK, HGRN_GROUP
    c = cc * nj
    n = t // c

    def body(q_ref, f_ref, i_ref, g_ref, l0_ref, l1_ref, on_ref, o_ref, hs_ref, st_ref):
        @pl.when(pl.program_id(0) == 0)
        def _():
            st_ref[...] = jnp.zeros_like(st_ref)

        hs_ref[0] = st_ref[...]
        o, st1 = _hgrn_step([st_ref[h] for h in range(HA_HEADS)],
                            *[_hgrn_blocks(ref, nj, cc) for ref in (q_ref, f_ref, i_ref, g_ref)],
                            _hgrn_cols(l0_ref), _hgrn_cols(l1_ref), _hgrn_cols(on_ref))
        for h in range(HA_HEADS):
            for j in range(nj):
                o_ref[j * cc:(j + 1) * cc, h * HA_DIM:(h + 1) * HA_DIM] = o[h][j]
            st_ref[h] = st1[h]

    col = lambda j: pl.BlockSpec((c, W_A), lambda i, j=j: (i, j))
    par = pl.BlockSpec((1, W_A), lambda i: (0, 0))
    return pl.pallas_call(
        body, name="hgrn_fwd", grid=(n,), in_specs=[col(0), col(1), col(2), col(3), par, par, par],
        out_specs=[pl.BlockSpec((c, W_A), lambda i: (i, 0)),
                   pl.BlockSpec((1, HA_HEADS, HA_DIM, HA_DIM), lambda i: (i, 0, 0, 0))],
        out_shape=[SDS((t, W_A), F32), SDS((n, HA_HEADS, HA_DIM, HA_DIM), F32)],
        scratch_shapes=[pltpu.VMEM((HA_HEADS, HA_DIM, HA_DIM), F32)],
        compiler_params=_params(("arbitrary",)))(p_h, p_h, p_h, p_h, l0, l1, onorm)


def _hgrn_bwd(p_h, l0, l1, onorm, hs, do, do_col, comm=None):
    t = p_h.shape[0]
    cc, nj = HGRN_CHUNK, HGRN_GROUP
    c = cc * nj
    n = t // c

    def body(q_ref, f_ref, i_ref, g_ref, l0_ref, l1_ref, on_ref, hs_ref, do_ref,
             dp_ref, dl0_ref, dl1_ref, don_ref, dst_ref):
        @pl.when(pl.program_id(0) == 0)
        def _():
            dst_ref[...] = jnp.zeros_like(dst_ref)
            dl0_ref[...] = jnp.zeros_like(dl0_ref)
            dl1_ref[...] = jnp.zeros_like(dl1_ref)
            don_ref[...] = jnp.zeros_like(don_ref)

        args = ([hs_ref[0, h] for h in range(HA_HEADS)],
                *[_hgrn_blocks(ref, nj, cc) for ref in (q_ref, f_ref, i_ref, g_ref)],
                _hgrn_cols(l0_ref), _hgrn_cols(l1_ref), _hgrn_cols(on_ref))
        _, vjp = jax.vjp(_hgrn_step, *args)
        dst0, dq, df, di, dg, dl0, dl1, don = vjp((_hgrn_blocks(do_ref, nj, cc),
                                                   [dst_ref[h] for h in range(HA_HEADS)]))
        for h in range(HA_HEADS):
            sl = slice(h * HA_DIM, (h + 1) * HA_DIM)
            for k, dv in enumerate((dq, df, di, dg)):
                for j in range(nj):
                    dp_ref[j * cc:(j + 1) * cc, k * W_A + h * HA_DIM:k * W_A + (h + 1) * HA_DIM] = dv[h][j]
            dl0_ref[:, sl] += dl0[h]
            dl1_ref[:, sl] += dl1[h]
            don_ref[:, sl] += don[h]
            dst_ref[h] = dst0[h]

    col = lambda j: pl.BlockSpec((c, W_A), lambda i, j=j: (n - 1 - i, j))
    par = pl.BlockSpec((1, W_A), lambda i: (0, 0))
    return _hosting_call(
        body, comm, name="hgrn_bwd", grid=(n,),
        in_specs=[col(0), col(1), col(2), col(3), par, par, par,
                  pl.BlockSpec((1, HA_HEADS, HA_DIM, HA_DIM), lambda i: (n - 1 - i, 0, 0, 0)),
                  pl.BlockSpec((c, W_A), lambda i: (n - 1 - i, do_col))],
        out_specs=[pl.BlockSpec((c, N_HGRN_COLS), lambda i: (n - 1 - i, 0)), par, par, par],
        out_shape=[SDS((t, N_HGRN_COLS), F32), SDS((1, W_A), F32), SDS((1, W_A), F32), SDS((1, W_A), F32)],
        scratch_shapes=[pltpu.VMEM((HA_HEADS, HA_DIM, HA_DIM), F32)],
        args=(p_h, p_h, p_h, p_h, l0, l1, onorm, hs, do))


HB_PAIRS = HB_HEADS // 2
PAIR_W = 2 * HB_DIM


def _head_lane_masks():
    lane = lax.broadcasted_iota(jnp.int32, (1, PAIR_W), 1)
    return (lane < HB_DIM).astype(F32), (lane >= HB_DIM).astype(F32)


@jax.custom_vjp
def _stack_heads(x):
    m0, m1 = _head_lane_masks()
    return jnp.concatenate([x * m0, x * m1], axis=0)


def _stack_heads_bwd(_, g):
    m0, m1 = _head_lane_masks()
    c = g.shape[0] // 2
    return (g[:c] * m0 + g[c:] * m1,)


_stack_heads.defvjp(lambda x: (_stack_heads(x), None), _stack_heads_bwd)


@jax.custom_vjp
def _unstack_heads(ys):
    c = ys.shape[0] // 2
    return ys[:c] + ys[c:]


_unstack_heads.defvjp(lambda ys: (_unstack_heads(ys), None), lambda _, g: (_stack_heads(g),))


def _same_head_block(c):
    ii = lax.broadcasted_iota(jnp.int32, (2 * c, 2 * c), 0)
    jj = lax.broadcasted_iota(jnp.int32, (2 * c, 2 * c), 1)
    same = (ii < c) == (jj < c)
    return same & (jj <= ii), same & (jj < ii), (ii == jj).astype(F32)


@jax.custom_vjp
def _rows_join(top, bottom):
    return jnp.concatenate([top, bottom], axis=0)


def _rows_join_bwd(n_top, g):
    return g[:n_top], g[n_top:]


_rows_join.defvjp(lambda top, bottom: (_rows_join(top, bottom), top.shape[0]), _rows_join_bwd)


def _rows_split_impl(x, n_top):
    return x[:n_top], x[n_top:]


_rows_split = jax.custom_vjp(_rows_split_impl, nondiff_argnums=(1,))
_rows_split.defvjp(lambda x, n_top: (_rows_split_impl(x, n_top), None),
                   lambda n_top, _, g: (jnp.concatenate([g[0], g[1]], axis=0),))


def _rwkv_step(s0, r, lw, k, v, a, b):
    npair, nj = len(r), len(r[0])
    c = r[0][0].shape[0]
    combos = [(j, p) for j in range(nj) for p in range(npair)]
    every = lambda fn: {q: fn(q) for q in combos}
    at_ = lambda d: (lambda q: d[q[1]][q[0]])
    r_, lw_, k_, v_, a_, b_ = (at_(z) for z in (r, lw, k, v, a, b))
    incl, strict, eye = _same_head_block(c)

    gam = every(lambda q: _cumsum_rows(lw_(q)))
    gtot = every(lambda q: jnp.sum(lw_(q), axis=0, keepdims=True))
    eneg = every(lambda q: jnp.exp(-gam[q]))
    edec = every(lambda q: jnp.exp(gtot[q] - gam[q]))
    at = every(lambda q: _stack_heads(a_(q) * jnp.exp(gam[q] - lw_(q))))
    rt = every(lambda q: _stack_heads(r_(q) * jnp.exp(gam[q])))
    bt = every(lambda q: _stack_heads(b_(q) * eneg[q]))
    kt = every(lambda q: _stack_heads(k_(q) * eneg[q]))
    bdec = every(lambda q: _stack_heads(b_(q) * edec[q]))
    kdec = every(lambda q: _stack_heads(k_(q) * edec[q]))
    vs = every(lambda q: _stack_heads(v_(q)))
    a_ab = every(lambda q: jnp.where(strict, _nt(at[q], bt[q]), 0.0))
    a_ak = every(lambda q: jnp.where(strict, _nt(at[q], kt[q]), 0.0))
    a_rb = every(lambda q: jnp.where(incl, _nt(rt[q], bt[q]), 0.0))
    a_rk = every(lambda q: jnp.where(incl, _nt(rt[q], kt[q]), 0.0))
    tinv = every(lambda q: eye + a_ab[q])
    pw = a_ab
    span = 2
    while span < c:
        pw = every(lambda q, pw=pw: _nn_x3(pw[q], pw[q]))
        tinv = every(lambda q, pw=pw, tinv=tinv: tinv[q] + _nn_x3(pw[q], tinv[q]))
        span *= 2
    akv = every(lambda q: _nn(a_ak[q], vs[q]))
    w1 = every(lambda q: _nn(tinv[q], at[q]))
    u0 = every(lambda q: _nn(tinv[q], akv[q]))
    wr = every(lambda q: _rows_join(w1[q], rt[q]))
    bk = every(lambda q: _rows_join(bdec[q], kdec[q]))
    yv = every(lambda q: _nn(a_rk[q], vs[q]))
    gdec = every(lambda q: jnp.exp(gtot[q]))

    s = list(s0)
    y = [[None] * nj for _ in range(npair)]
    for j in range(nj):
        both = {p: _rows_split(_nt(wr[(j, p)], s[p]), 2 * c) for p in range(npair)}
        u = {p: both[p][0] + u0[(j, p)] for p in range(npair)}
        for p in range(npair):
            y[p][j] = _unstack_heads(both[p][1] + _nn(a_rb[(j, p)], u[p]) + yv[(j, p)])
        s = [s[p] * gdec[(j, p)] + _tn(_rows_join(u[p], vs[(j, p)]), bk[(j, p)]) for p in range(npair)]
    return y, s


def _rwkv_blocks(ref, nj, c):
    return [[ref[j * c:(j + 1) * c, p * PAIR_W:(p + 1) * PAIR_W] for j in range(nj)] for p in range(HB_PAIRS)]


def _rwkv_fwd(seqs, comm=None):
    t = seqs[0].shape[0]
    c, nj = RWKV_CHUNK, RWKV_GROUP
    n = t // (c * nj)

    def body(r_ref, lw_ref, k_ref, v_ref, a_ref, b_ref, y_ref, hs_ref, st_ref):
        @pl.when(pl.program_id(0) == 0)
        def _():
            st_ref[...] = jnp.zeros_like(st_ref)

        hs_ref[0] = st_ref[...]
        s0 = [st_ref[p] for p in range(HB_PAIRS)]
        y, s1 = _rwkv_step(s0, *[_rwkv_blocks(ref, nj, c) for ref in (r_ref, lw_ref, k_ref, v_ref, a_ref, b_ref)])
        for p in range(HB_PAIRS):
            for j in range(nj):
                y_ref[j * c:(j + 1) * c, p * PAIR_W:(p + 1) * PAIR_W] = y[p][j]
            st_ref[p] = s1[p]

    seq = pl.BlockSpec((c * nj, W_B), lambda i: (i, 0))
    return _hosting_call(
        body, comm, name="rwkv_fwd", grid=(n,), in_specs=[seq] * 6,
        out_specs=[seq, pl.BlockSpec((1, HB_PAIRS, PAIR_W, PAIR_W), lambda i: (i, 0, 0, 0))],
        out_shape=[SDS((t, W_B), F32), SDS((n, HB_PAIRS, PAIR_W, PAIR_W), F32)],
        scratch_shapes=[pltpu.VMEM((HB_PAIRS, PAIR_W, PAIR_W), F32)], args=tuple(seqs))


def _rwkv_bwd(seqs, hs, dy, comm=None):
    t = seqs[0].shape[0]
    c, nj = RWKV_CHUNK, RWKV_GROUP
    n = t // (c * nj)

    def body(r_ref, lw_ref, k_ref, v_ref, a_ref, b_ref, hs_ref, dy_ref,
             dr_ref, dlw_ref, dk_ref, dv_ref, da_ref, db_ref, dst_ref):
        @pl.when(pl.program_id(0) == 0)
        def _():
            dst_ref[...] = jnp.zeros_like(dst_ref)

        s0 = [hs_ref[0, p] for p in range(HB_PAIRS)]
        seq_vals = [_rwkv_blocks(ref, nj, c) for ref in (r_ref, lw_ref, k_ref, v_ref, a_ref, b_ref)]
        _, vjp = jax.vjp(_rwkv_step, s0, *seq_vals)
        grads = vjp((_rwkv_blocks(dy_ref, nj, c), [dst_ref[p] for p in range(HB_PAIRS)]))
        for ref, gr in zip((dr_ref, dlw_ref, dk_ref, dv_ref, da_ref, db_ref), grads[1:]):
            for p in range(HB_PAIRS):
                for j in range(nj):
                    ref[j * c:(j + 1) * c, p * PAIR_W:(p + 1) * PAIR_W] = gr[p][j]
        m0, m1 = _head_lane_masks()
        rows0 = (lax.broadcasted_iota(jnp.int32, (PAIR_W, 1), 0) < HB_DIM).astype(F32)
        blocks = rows0 * m0 + (1.0 - rows0) * m1
        for p in range(HB_PAIRS):
            dst_ref[p] = grads[0][p] * blocks

    seq = pl.BlockSpec((c * nj, W_B), lambda i: (n - 1 - i, 0))
    return _hosting_call(
        body, comm, name="rwkv_bwd", grid=(n,),
        in_specs=[seq] * 6 + [pl.BlockSpec((1, HB_PAIRS, PAIR_W, PAIR_W), lambda i: (n - 1 - i, 0, 0, 0)), seq],
        out_specs=[seq] * 6, out_shape=[SDS((t, W_B), F32)] * 6,
        scratch_shapes=[pltpu.VMEM((HB_PAIRS, PAIR_W, PAIR_W), F32)], args=(*seqs, hs, dy))


def _final_loss(x3, fnorm, target, *, tm):
    t, d = x3.shape

    def body(x_ref, g_ref, t_ref, dx_ref, dg_ref, loss_ref):
        @pl.when(pl.program_id(0) == 0)
        def _():
            dg_ref[...] = jnp.zeros_like(dg_ref)
            loss_ref[...] = jnp.zeros_like(loss_ref)

        x, g = x_ref[...], g_ref[...]
        rinv = lax.rsqrt(jnp.mean(x * x, axis=-1, keepdims=True) + NORM_EPS)
        xh = x * rinv
        diff = xh * g - t_ref[...]
        loss_ref[...] += 0.5 * jnp.sum(jnp.mean(diff * diff, axis=-1, keepdims=True))
        dy = diff * (1.0 / d)
        dg_ref[...] += jnp.sum(dy * xh, axis=0, keepdims=True)
        dxh = dy * g
        dx_ref[...] = rinv * (dxh - xh * jnp.mean(dxh * xh, axis=-1, keepdims=True))

    row = pl.BlockSpec((tm, d), lambda i: (i, 0))
    return pl.pallas_call(
        body, name="final_loss", grid=(t // tm,), in_specs=[row, pl.BlockSpec((1, d), lambda i: (0, 0)), row],
        out_specs=[row, pl.BlockSpec((1, d), lambda i: (0, 0)), pl.BlockSpec((8, 128), lambda i: (0, 0))],
        out_shape=[SDS((t, d), F32), SDS((1, d), F32), SDS((8, 128), F32)],
        compiler_params=_params(("arbitrary",)))(x3, fnorm, target)


def _gate_up_act(h, wgt, wut, *, tm, tn, name, comm=None):
    t, d = h.shape
    tm = min(tm, t)

    def body(h_ref, g_ref, u_ref, a_out, u_out, act_out):
        hv = h_ref[...]
        a = _dg(hv, g_ref[...], 1, 1, False)
        u = _dg(hv, u_ref[...], 1, 1, False)
        a_out[...] = a.astype(a_out.dtype)
        u_out[...] = u.astype(u_out.dtype)
        act_out[...] = (_silu(a) * u).astype(act_out.dtype)

    wspec = pl.BlockSpec((tn, d), lambda i, j: (j, 0))
    ospec = pl.BlockSpec((tm, tn), lambda i, j: (i, j))
    return _hosting_call(
        body, comm, name=name, grid=(t // tm, D_FF // tn),
        in_specs=[pl.BlockSpec((tm, d), lambda i, j: (i, 0)), wspec, wspec], out_specs=[ospec, ospec, ospec],
        out_shape=[SDS((t, D_FF), BF16), SDS((t, D_FF), BF16), SDS((t, D_FF), BF16)], scratch_shapes=[],
        args=(h, wgt, wut))


def _up_act(h, wut, a, *, tm, tn, name, comm=None):
    t, d = h.shape
    tm = min(tm, t)

    def body(h_ref, u_ref, a_ref, u_out, act_out):
        u = _dg(h_ref[...], u_ref[...], 1, 1, False)
        u_out[...] = u.astype(u_out.dtype)
        act_out[...] = (_silu(a_ref[...].astype(F32)) * u).astype(act_out.dtype)

    tile = pl.BlockSpec((tm, tn), lambda i, j: (i, j))
    return _hosting_call(
        body, comm, name=name, grid=(t // tm, D_FF // tn),
        in_specs=[pl.BlockSpec((tm, d), lambda i, j: (i, 0)), pl.BlockSpec((tn, d), lambda i, j: (j, 0)), tile],
        out_specs=[tile, tile], out_shape=[SDS((t, D_FF), BF16), SDS((t, D_FF), BF16)], scratch_shapes=[],
        args=(h, wut, a))


def _dact_swiglu(dout, wd, a, u, *, tm, tn, name, comm=None):
    t, d = dout.shape
    tm = min(tm, t)

    def body(d_ref, w_ref, a_ref, u_ref, da_out, du_out):
        dact = 0.5 * _dg(d_ref[...], w_ref[...], 1, 1, False)
        av, uv = a_ref[...].astype(F32), u_ref[...].astype(F32)
        s = _sigmoid(av)
        da_out[...] = (dact * uv * (s * (1.0 + av * (1.0 - s)))).astype(da_out.dtype)
        du_out[...] = (dact * (av * s)).astype(du_out.dtype)

    tile = pl.BlockSpec((tm, tn), lambda i, j: (i, j))
    return _hosting_call(
        body, comm, name=name, grid=(t // tm, D_FF // tn),
        in_specs=[pl.BlockSpec((tm, d), lambda i, j: (i, 0)), pl.BlockSpec((tn, d), lambda i, j: (j, 0)), tile, tile],
        out_specs=[tile, tile], out_shape=[SDS((t, D_FF), BF16), SDS((t, D_FF), BF16)], scratch_shapes=[],
        args=(dout, wd, a, u))


class _Plan:
    def __init__(self):
        self.entries, self.counts = collections.defaultdict(list), {}

    def carry(self, host, comm_of, after):
        self.entries[host].append((comm_of, after))

    def comm(self, host, g):
        comms = [comm_of(g) for comm_of, _ in self.entries.get(host, [])]
        self.counts[host] = [len(c.arrays) for c in comms]
        return functools.reduce(_join_comms, comms) if comms else None

    def done(self, host, results, w):
        start = 0
        for (_, after), n in zip(self.entries.get(host, []), self.counts.get(host, [])):
            after(results[start:start + n], w)
            start += n


def _ffn_fwd(x, w, tag, plan, g):
    comm = plan.comm(f"{tag}_rms", g)
    res = _rowwise(_rms_f, [x], [w[f"{tag}_norm"]], [[0]], [BF16], tm=512, name=f"{tag}_rms", comm=comm)
    (h,), carried = res if comm is not None else (res, [])
    plan.done(f"{tag}_rms", carried, w)
    if plan.entries.get(f"{tag}_gate"):
        a, carried = _mm(h, w[f"{tag}_wgt"], tb=True, tm=2048, tn=256, tk=D_MODEL, name=f"{tag}_gate",
                         out_dtype=BF16, comm=plan.comm(f"{tag}_gate", g))
        plan.done(f"{tag}_gate", carried, w)
        (u, act), carried = _up_act(h, w[f"{tag}_wut"], a, tm=2048, tn=256, name=f"{tag}_up_act",
                                    comm=plan.comm(f"{tag}_up_act", g))
        plan.done(f"{tag}_up_act", carried, w)
    else:
        (a, u, act), carried = _gate_up_act(h, w[f"{tag}_wgt"], w[f"{tag}_wut"], tm=2048, tn=256,
                                            name=f"{tag}_gate_up", comm=plan.comm(f"{tag}_gate_up", g))
        plan.done(f"{tag}_gate_up", carried, w)
    comm = plan.comm(f"{tag}_down", g)
    out = _mm(act, w[f"{tag}_wd"], tm=1024, tn=D_MODEL, tk=D_FF, name=f"{tag}_down", res=x, scale=0.5, comm=comm)
    if comm is not None:
        out, carried = out
        plan.done(f"{tag}_down", carried, w)
    return out, (h, a, u, act)


def _ffn_bwd(dout, x, w, saved, tag, plan, g):
    h, a, u, act = saved

    def carrying(fn, host, *args, **kwargs):
        comm = plan.comm(host, g)
        res = fn(*args, name=host, comm=comm, **kwargs)
        out, carried = res if comm is not None else (res, [])
        plan.done(host, carried, w)
        return out

    (da, du), carried = _dact_swiglu(dout, w[f"{tag}_wd"], a, u, tm=2048, tn=256, name=f"{tag}_dact",
                                     comm=plan.comm(f"{tag}_dact", g))
    plan.done(f"{tag}_dact", carried, w)
    g[f"{tag}_wd"] = _mm(act, dout, ta=True, tm=D_FF // 2, tn=D_MODEL, tk=1024, name=f"{tag}_dwd", scale=0.5)
    g[f"{tag}_wgt"] = carrying(_mm, f"{tag}_dwg", da, h, ta=True, tm=D_FF // 2, tn=D_MODEL, tk=1024)
    g[f"{tag}_wut"] = carrying(_mm, f"{tag}_dwu", du, h, ta=True, tm=D_FF // 2, tn=D_MODEL, tk=1024)
    if plan.entries.get(f"{tag}_dh_g") or plan.entries.get(f"{tag}_dh_u"):
        dh = carrying(_mm, f"{tag}_dh_g", da, w[f"{tag}_wgt"], tm=1024, tn=D_MODEL, tk=D_FF)
        dh = carrying(_mm, f"{tag}_dh_u", du, w[f"{tag}_wut"], tm=1024, tn=D_MODEL, tk=D_FF, res=dh)
    else:
        dh = _mm_pair(da, w[f"{tag}_wgt"], du, w[f"{tag}_wut"], tm=512, name=f"{tag}_dh")
    dx, g[f"{tag}_norm"] = carrying(_rowwise_bwd, f"{tag}_drms", _rms_f, [x], [w[f"{tag}_norm"]], [dh],
                                    x_grad=[True], p_grad=[True], dx_groups=[[0]], dx_dtypes=[F32], tm=512,
                                    extra={0: dout})
    return dx


def _local_step(x, target, w, plan=None):
    plan = plan or _Plan()
    ones_bd = jnp.kron(jnp.eye(HB_HEADS, dtype=F32), jnp.ones((HB_DIM, HB_DIM), F32))
    g = {}
    x1, ffn1_saved = _ffn_fwd(x, w, "ffn1", plan, g)
    hm, = _rowwise(_rms_f, [x1], [w["mix_norm"]], [[0]], [BF16], tm=512, name="mix_rms")
    p_h = _mm(hm, w["w_in_h"], tm=2048, tn=256, tk=D_MODEL, name="inproj_h")
    p_r = _mm(hm, w["w_in_r"], tm=2048, tn=256, tk=D_MODEL, name="inproj_r")
    o_a, hgrn_states = _hgrn_fwd(p_h, w["lb0"], w["lb1"], w["hgrn_out_norm"])

    mu = w["mu_pad"]
    prep_xs = [(p_r, W_B, 0), (p_r, W_B, 1), (p_r, W_B, 2), (p_r, LORA_PAD, 6),
               ("prev", p_r, W_B, 0), ("prev", p_r, W_B, 1), ("prev", p_r, W_B, 2), ("prev", p_r, LORA_PAD, 6)]
    prep_ps = [(mu, W_B, 0), (mu, W_B, 1), (mu, W_B, 2), (mu, LORA_PAD, 6), w["rwkv_w0"], w["w2_pad"], w["rwkv_a0"],
               w["a2_pad"], w["g2_pad"], w["rwkv_k_k"], w["rwkv_k_a"], ones_bd]
    prep_f = _rwkv_prep_f
    r, lw, k2, v, a_vec, b_vec, gate = _rowwise(prep_f, prep_xs, prep_ps, [[0], [1], [2], [3], [4], [5], [6]],
                                                [F32] * 7, tm=256, name="rwkv_prep")
    seqs = [r, lw, k2, v, a_vec, b_vec]
    (y, rwkv_states), carried = _rwkv_fwd(seqs, comm=plan.comm("rwkv_fwd", g))
    plan.done("rwkv_fwd", carried, w)
    post_f = _rwkv_post_f
    post_xs = [y, r, k2, v, gate]
    post_ps = [w["rwkv_r_k"], w["rwkv_gn_w"], w["rwkv_gn_b"], ones_bd]
    o, = _rowwise(lambda o_a_, *rest: (o_a_,) + tuple(post_f(*rest)), [o_a] + post_xs, post_ps, [[0, 1]], [F32],
                  tm=256, name="rwkv_post")
    x2 = _mm(o, w["w_out"], tm=2048, tn=256, tk=D_MODEL, name="outproj", res=x1)
    x3, ffn2_saved = _ffn_fwd(x2, w, "ffn2", plan, g)
    dx3, g["final_norm"], loss = _final_loss(x3, w["final_norm"], target, tm=256)

    dx2 = _ffn_bwd(dx3, x2, w, ffn2_saved, "ffn2", plan, g)
    do = _mm(dx2, w["w_out"], tb=True, tm=2048, tn=256, tk=D_MODEL, name="outproj_do")
    g["w_out"] = _mm(o, dx2, ta=True, tm=D_MODEL, tn=D_MODEL, tk=1024, name="outproj_dw")

    (dp_h, g["lb0"], g["lb1"], g["hgrn_out_norm"]), carried = _hgrn_bwd(
        p_h, w["lb0"], w["lb1"], w["hgrn_out_norm"], hgrn_states, do, 0, comm=plan.comm("hgrn_bwd", g))
    plan.done("hgrn_bwd", carried, w)
    post_out = _rowwise_bwd(post_f, post_xs, post_ps, [(do, W_B, 1)], x_grad=[True] * 5, p_grad=[True] * 3 + [False],
                            dx_groups=[[0], [1], [2], [3], [4]], dx_dtypes=[F32] * 5, tm=256, name="rwkv_post_bwd")
    dy, dr1, dk1, dv1, dgate, g["rwkv_r_k"], g["rwkv_gn_w"], g["rwkv_gn_b"] = post_out
    (dr2, dlw, dk2, dv2, da_vec, db_vec), carried = _rwkv_bwd(seqs, rwkv_states, dy, comm=plan.comm("rwkv_bwd", g))
    plan.done("rwkv_bwd", carried, w)

    def prep2_f(*vals):
        r_, lw_, k2_, v_, a_, b_, g_ = prep_f(*vals)
        return r_, lw_, k2_, v_, a_, b_, g_, r_, k2_, v_

    prep_comm = plan.comm("rwkv_prep_bwd", g)
    prep_out = _rowwise_bwd(prep2_f, prep_xs, prep_ps, [dr2, dlw, dk2, dv2, da_vec, db_vec, dgate, dr1, dk1, dv1],
                            x_grad=[True] * 8, p_grad=[True] * 11 + [False], dx_groups=[[0, 1, 2, 3], [4, 5, 6, 7]],
                            dx_dtypes=[F32], tm=256, name="rwkv_prep_bwd", fold_next=(0, 1), comm=prep_comm)
    prep_out, carried = prep_out if prep_comm is not None else (prep_out, [])
    plan.done("rwkv_prep_bwd", carried, w)
    dp_r = prep_out[0]
    (dmu_r, dmu_k, dmu_v, dmu_lo, g["rwkv_w0"], g["w2_pad"], g["rwkv_a0"], g["a2_pad"], g["g2_pad"],
     g["rwkv_k_k"], g["rwkv_k_a"]) = prep_out[1:]
    g["mu_pad"] = jnp.concatenate([dmu_r, dmu_k, dmu_v, dmu_lo], axis=1)
    dhm = _mm_pair(dp_h, w["w_in_h"], dp_r, w["w_in_r"], tb=True, tm=512, name="inproj_dh")
    g["w_in_h"] = _mm(hm, dp_h, ta=True, tm=D_MODEL, tn=D_MODEL, tk=1024, name="inproj_dw_h")
    g["w_in_r"] = _mm(hm, dp_r, ta=True, tm=D_MODEL, tn=N_RWKV_PAD // 2, tk=1024, name="inproj_dw_r")
    mix_comm = plan.comm("mix_drms", g)
    mix_out = _rowwise_bwd(_rms_f, [x1], [w["mix_norm"]], [dhm], x_grad=[True], p_grad=[True], dx_groups=[[0]],
                           dx_dtypes=[F32], tm=512, name="mix_drms", extra={0: dx2}, comm=mix_comm)
    (dx1, g["mix_norm"]), carried = mix_out if mix_comm is not None else (mix_out, [])
    plan.done("mix_drms", carried, w)
    dx0 = _ffn_bwd(dx1, x, w, ffn1_saved, "ffn1", plan, g)
    return loss, dx0, g


HBM_SPEC = pl.BlockSpec(memory_space=pl.ANY)

Comm = collections.namedtuple("Comm", "arrays out_shapes aliased sem_shapes start finish")


def _join_comms(first, second):
    n, s = len(first.arrays), len(first.sem_shapes)

    def start(ins, outs, sems):
        first.start(ins[:n], outs[:n], sems[:s])
        second.start(ins[n:], outs[n:], sems[s:])

    def finish(ins, outs, sems):
        first.finish(ins[:n], outs[:n], sems[:s])
        second.finish(ins[n:], outs[n:], sems[s:])

    return Comm(list(first.arrays) + list(second.arrays), list(first.out_shapes) + list(second.out_shapes),
                list(first.aliased) + list(second.aliased), list(first.sem_shapes) + list(second.sem_shapes),
                start, finish)


def _run_comm(comm, name):
    n = len(comm.arrays)

    def body(*refs):
        ins, outs, sems = refs[:n], refs[n:2 * n], refs[2 * n:]
        comm.start(ins, outs, sems)
        comm.finish(ins, outs, sems)

    return pl.pallas_call(
        body, name=name, in_specs=[HBM_SPEC] * n, out_specs=[HBM_SPEC] * n, out_shape=list(comm.out_shapes),
        input_output_aliases={t: t for t in range(n) if comm.aliased[t]},
        scratch_shapes=list(comm.sem_shapes))(*comm.arrays)


def _hosting_call(body, comm, *, name, grid, in_specs, out_specs, out_shape, scratch_shapes, args):
    sem = ("arbitrary",) * len(grid)
    if comm is None:
        res = pl.pallas_call(body, name=name, grid=grid, in_specs=in_specs, out_specs=out_specs, out_shape=out_shape,
                             scratch_shapes=scratch_shapes, compiler_params=_params(sem))(*args)
        return list(res), []
    ni, no, ns, nc = len(in_specs), len(out_specs), len(scratch_shapes), len(comm.arrays)

    def wrapped(*refs):
        ins, cins = refs[:ni], refs[ni:ni + nc]
        outs, couts = refs[ni + nc:ni + nc + no], refs[ni + nc + no:ni + 2 * nc + no]
        scr, sems = refs[ni + 2 * nc + no:ni + 2 * nc + no + ns], refs[ni + 2 * nc + no + ns:]
        first = functools.reduce(jnp.logical_and, [pl.program_id(k) == 0 for k in range(len(grid))])
        last = functools.reduce(jnp.logical_and, [pl.program_id(k) == grid[k] - 1 for k in range(len(grid))])

        @pl.when(first)
        def _():
            comm.start(cins, couts, sems)

        body(*ins, *outs, *scr)

        @pl.when(last)
        def _():
            comm.finish(cins, couts, sems)

    res = pl.pallas_call(
        wrapped, name=name, grid=grid, in_specs=list(in_specs) + [HBM_SPEC] * nc,
        out_specs=list(out_specs) + [HBM_SPEC] * nc, out_shape=list(out_shape) + list(comm.out_shapes),
        scratch_shapes=list(scratch_shapes) + list(comm.sem_shapes),
        input_output_aliases={ni + t: no + t for t in range(nc) if comm.aliased[t]},
        compiler_params=_params(sem))(*args, *comm.arrays)
    return list(res[:no]), list(res[no:])


def _chips(x, y):
    return [(1 - x, y), (x, 1 - y), (1 - x, 1 - y)]


def _gather_comm(bufs):
    n = len(bufs)

    def copies(outs, sems):
        ici_send, ici_recv, d2d_send, d2d_recv = sems
        x, y, c = lax.axis_index("x"), lax.axis_index("y"), lax.axis_index("c")

        def half(t, slot, hc):
            hr = bufs[t].shape[1] // 2
            return outs[t].at[slot, pl.ds(pl.multiple_of(hc * hr, 16), hr), :]

        def ici(t, j, slot, px, py):
            return pltpu.make_async_remote_copy(src_ref=half(t, slot, c), dst_ref=half(t, slot, c),
                                                send_sem=ici_send.at[3 * t + j], recv_sem=ici_recv.at[3 * t + j],
                                                device_id=(px, py, c), device_id_type=MESH)

        def d2d(t, j, slot, hc):
            return pltpu.make_async_remote_copy(src_ref=half(t, slot, hc), dst_ref=half(t, slot, hc),
                                                send_sem=d2d_send.at[3 * t + j], recv_sem=d2d_recv.at[3 * t + j],
                                                device_id=(x, y, 1 - c), device_id_type=MESH)

        peers = [(t, j, px, py) for t in range(n) for j, (px, py) in enumerate(_chips(x, y))]
        return ici, d2d, peers, 2 * x + y, c

    def start(ins, outs, sems):
        ici, _, peers, me, _ = copies(outs, sems)
        for t, j, px, py in peers:
            ici(t, j, me, px, py).start()

    def finish(ins, outs, sems):
        ici, d2d, peers, me, c = copies(outs, sems)
        for t, j, px, py in peers:
            ici(t, j, 2 * px + py, px, py).wait_recv()
            d2d(t, j, 2 * px + py, c).start()
        for t, j, px, py in peers:
            d2d(t, j, 2 * px + py, 1 - c).wait_recv()
        for t, j, px, py in peers:
            ici(t, j, me, px, py).wait_send()
            d2d(t, j, 2 * px + py, c).wait_send()

    return Comm(list(bufs), [SDS(b.shape, b.dtype) for b in bufs], [True] * n,
                [pltpu.SemaphoreType.DMA((3 * n,))] * 4, start, finish)


def _sibling_exchange_comm(gs):
    n = len(gs)

    def copies(ins, outs, sems):
        x, y, c = lax.axis_index("x"), lax.axis_index("y"), lax.axis_index("c")
        cps = []
        for t in range(n):
            hr = gs[t].shape[1] // 2
            src = ins[t].at[:, pl.ds(pl.multiple_of((1 - c) * hr, SUBLANES), hr), :]
            cps.append(pltpu.make_async_remote_copy(src_ref=src, dst_ref=outs[t], send_sem=sems[0].at[t],
                                                    recv_sem=sems[1].at[t], device_id=(x, y, 1 - c),
                                                    device_id_type=MESH))
        return cps

    def start(ins, outs, sems):
        for cp in copies(ins, outs, sems):
            cp.start()

    def finish(ins, outs, sems):
        for cp in copies(ins, outs, sems):
            cp.wait()

    return Comm(list(gs), [SDS((N_CHIPS, g.shape[1] // 2, g.shape[2]), g.dtype) for g in gs], [False] * n,
                [pltpu.SemaphoreType.DMA((n,))] * 2, start, finish)


def _own_rows(c, rows):
    return pl.ds(pl.multiple_of(c * (rows // 2), 16), rows // 2)


def _chip_exchange_comm(ss):
    n = len(ss)

    def copies(ins, outs, sems):
        x, y, c = lax.axis_index("x"), lax.axis_index("y"), lax.axis_index("c")
        me = 2 * x + y

        def copy(t, j, px, py, src_slot, dst_slot):
            rows = _own_rows(c, ss[t].shape[1])
            return pltpu.make_async_remote_copy(src_ref=ins[t].at[src_slot, rows, :],
                                                dst_ref=outs[t].at[dst_slot, rows, :],
                                                send_sem=sems[0].at[3 * t + j], recv_sem=sems[1].at[3 * t + j],
                                                device_id=(px, py, c), device_id_type=MESH)

        peers = [(t, j, px, py) for t in range(n) for j, (px, py) in enumerate(_chips(x, y))]
        return copy, peers, me

    def start(ins, outs, sems):
        copy, peers, me = copies(ins, outs, sems)
        for t, j, px, py in peers:
            copy(t, j, px, py, 2 * px + py, me).start()

    def finish(ins, outs, sems):
        copy, peers, me = copies(ins, outs, sems)
        for t, j, px, py in peers:
            copy(t, j, px, py, me, 2 * px + py).wait_recv()
        for t, j, px, py in peers:
            copy(t, j, px, py, 2 * px + py, me).wait_send()

    return Comm(list(ss), [SDS(s.shape, s.dtype) for s in ss], [False] * n,
                [pltpu.SemaphoreType.DMA((3 * n,))] * 2, start, finish)


def _sibling_swap_comm(rs, ss):
    n = len(rs)

    def copies(outs, sems):
        x, y, c = lax.axis_index("x"), lax.axis_index("y"), lax.axis_index("c")
        cps = []
        for t in range(n):
            rows = _own_rows(c, rs[t].shape[1])
            held = [outs[t].at[2 * px + py, rows, :] for px, py in _chips(x, y)] + [outs[n + t].at[2 * x + y, rows, :]]
            cps += [pltpu.make_async_remote_copy(src_ref=ref, dst_ref=ref, send_sem=sems[0].at[4 * t + j],
                                                 recv_sem=sems[1].at[4 * t + j], device_id=(x, y, 1 - c),
                                                 device_id_type=MESH) for j, ref in enumerate(held)]
        return cps

    def start(ins, outs, sems):
        for cp in copies(outs, sems):
            cp.start()

    def finish(ins, outs, sems):
        for cp in copies(outs, sems):
            cp.wait()

    both = list(rs) + list(ss)
    return Comm(both, [SDS(b.shape, b.dtype) for b in both], [True] * (2 * n),
                [pltpu.SemaphoreType.DMA((4 * n,))] * 2, start, finish)


def _row_tile(rows, cap=512):
    best = SUBLANES
    for tr in range(SUBLANES, min(rows, cap) + 1, SUBLANES):
        if rows % tr == 0:
            best = tr
    return best


def _add_halves(g4, r4, c_idx, name):
    _, hr, lanes = r4.shape
    tr = _row_tile(hr)
    nb = hr // tr

    def body(c_ref, a_ref, b_ref, o_ref):
        o_ref[...] = (a_ref[...] + b_ref[...]).astype(o_ref.dtype)

    pair = 2
    owned = pl.BlockSpec((pair, tr, lanes), lambda q, i, c_ref: (q, c_ref[0] * nb + i, 0))
    grid_spec = pltpu.PrefetchScalarGridSpec(
        num_scalar_prefetch=1, grid=(N_CHIPS // pair, nb),
        in_specs=[owned, pl.BlockSpec((pair, tr, lanes), lambda q, i, c_ref: (q, i, 0))], out_specs=owned)
    return pl.pallas_call(body, name=name, grid_spec=grid_spec, out_shape=SDS(g4.shape, BF16),
                          compiler_params=_params(("parallel", "parallel")))(c_idx, g4, r4)


def _adamw(wf, r4, s4, mf, vf, me_idx, name):
    rows, lanes = wf.shape
    tr = rows // 2
    assert tr % 16 == 0
    c1 = 1.0 / (1.0 - ADAM_B1 ** ADAM_STEP)
    c2 = 1.0 / (1.0 - ADAM_B2 ** ADAM_STEP)

    def body(me_ref, w_ref, a_ref, b_ref, c_ref, d_ref, own_ref, m_ref, v_ref, g_ref, delta_ref, nm_ref, nv_ref):
        own = own_ref[...].astype(F32)
        p = [jnp.where(me_ref[0] == q, own, ref[...].astype(F32)) for q, ref in enumerate((a_ref, b_ref, c_ref, d_ref))]
        gv = ((p[0] + p[1]) + p[2]) + p[3]
        m = ADAM_B1 * m_ref[...] + (1.0 - ADAM_B1) * gv
        v = ADAM_B2 * v_ref[...] + (1.0 - ADAM_B2) * (gv * gv)
        g_ref[...] = gv
        delta_ref[...] = -ADAM_LR * ((m * c1) / (jnp.sqrt(v * c2) + ADAM_EPS) + ADAM_WD * w_ref[...])
        nm_ref[...] = m
        nv_ref[...] = v

    other = lambda q: (lambda i, me_ref: (jnp.where(me_ref[0] == q, (q + 1) % N_CHIPS, q), i, 0))
    full = pl.BlockSpec((tr, lanes), lambda i, me_ref: (i, 0))
    grid_spec = pltpu.PrefetchScalarGridSpec(
        num_scalar_prefetch=1, grid=(rows // tr,),
        in_specs=[full] + [pl.BlockSpec((None, tr, lanes), other(q)) for q in range(N_CHIPS)]
        + [pl.BlockSpec((None, tr, lanes), lambda i, me_ref: (me_ref[0], i, 0)), full, full],
        out_specs=[full] * 4)
    return pl.pallas_call(body, name=name, grid_spec=grid_spec, out_shape=[SDS((rows, lanes), F32)] * 4,
                          compiler_params=_params(("parallel",)))(me_idx, wf, r4, r4, r4, r4, s4, mf, vf)


BIG = ("ffn1_w_gate", "ffn1_w_up", "ffn1_w_down", "ffn2_w_gate", "ffn2_w_up", "ffn2_w_down", "w_out", "w_in")
TRANSPOSED = ("ffn1_w_gate", "ffn1_w_up", "ffn2_w_gate", "ffn2_w_up")
PACKED = ("rwkv_w2", "rwkv_a2", "rwkv_g2")
SMALL_SHAPES = {"ffn1_norm": (1, D_MODEL), "mix_norm": (1, D_MODEL), "hgrn_lb_logits": (2, W_A),
                "hgrn_out_norm": (1, W_A), "rwkv_shift_mu": (1, N_RWKV_COLS), "rwkv_w0": (1, W_B),
                "rwkv_a0": (1, W_B), "rwkv_k_k": (1, W_B), "rwkv_k_a": (1, W_B),
                "rwkv_r_k": (1, HB_HEADS, HB_DIM), "rwkv_gn_w": (1, W_B), "rwkv_gn_b": (1, W_B),
                "ffn2_norm": (1, D_MODEL), "final_norm": (D_MODEL,)}
PACK_ELEMS = sum(_numel(_shard_shape(n)) for n in PACKED) + sum(_numel(SMALL_SHAPES[n]) for n in SMALL)
PACK_ROWS = -(-PACK_ELEMS // (32 * LANES)) * 32


def _to_rows(name, shard):
    return shard[0].T if name in TRANSPOSED else shard[0]


def _from_rows(name, rows):
    return (rows.T if name in TRANSPOSED else rows)[None]


def _pack(sharded, small):
    flat = jnp.concatenate([sharded[n].reshape(-1) for n in PACKED] + [small[n].reshape(-1) for n in SMALL])
    return jnp.pad(flat, (0, PACK_ROWS * LANES - flat.shape[0])).reshape(PACK_ROWS, LANES)


def _unpack(packed):
    flat, out, off = packed.reshape(-1), {}, 0
    for n in PACKED:
        shp = _shard_shape(n)
        out[n] = flat[off:off + _numel(shp)].reshape((1,) + shp)
        off += _numel(shp)
    for n in SMALL:
        shp = SMALL_SHAPES[n]
        out[n] = flat[off:off + _numel(shp)].reshape(shp)
        off += _numel(shp)
    return out


def _quarter(full, name, q):
    shape, ax = SHARDED_SHAPES[name]
    w = shape[ax] // N_CHIPS
    return lax.slice_in_dim(full, q * w, (q + 1) * w, axis=ax)


def kernel(x, ffn1_norm, ffn1_w_gate, ffn1_w_up, ffn1_w_down, mix_norm, w_in, hgrn_lb_logits, hgrn_out_norm, rwkv_shift_mu, rwkv_w0, rwkv_w2, rwkv_a0, rwkv_a2, rwkv_g2, rwkv_k_k, rwkv_k_a, rwkv_r_k, rwkv_gn_w, rwkv_gn_b, w_out, ffn2_norm, ffn2_w_gate, ffn2_w_up, ffn2_w_down, final_norm, loss_target, m_ffn1_norm, m_ffn1_w_gate, m_ffn1_w_up, m_ffn1_w_down, m_mix_norm, m_w_in, m_hgrn_lb_logits, m_hgrn_out_norm, m_rwkv_shift_mu, m_rwkv_w0, m_rwkv_w2, m_rwkv_a0, m_rwkv_a2, m_rwkv_g2, m_rwkv_k_k, m_rwkv_k_a, m_rwkv_r_k, m_rwkv_gn_w, m_rwkv_gn_b, m_w_out, m_ffn2_norm, m_ffn2_w_gate, m_ffn2_w_up, m_ffn2_w_down, m_final_norm, v_ffn1_norm, v_ffn1_w_gate, v_ffn1_w_up, v_ffn1_w_down, v_mix_norm, v_w_in, v_hgrn_lb_logits, v_hgrn_out_norm, v_rwkv_shift_mu, v_rwkv_w0, v_rwkv_w2, v_rwkv_a0, v_rwkv_a2, v_rwkv_g2, v_rwkv_k_k, v_rwkv_k_a, v_rwkv_r_k, v_rwkv_gn_w, v_rwkv_gn_b, v_w_out, v_ffn2_norm, v_ffn2_w_gate, v_ffn2_w_up, v_ffn2_w_down, v_final_norm):
    args = dict(locals())
    wts = {n: args[n] for n in ALL_WEIGHTS}
    moms = {n: args["m_" + n] for n in ALL_WEIGHTS}
    vars_ = {n: args["v_" + n] for n in ALL_WEIGHTS}

    me = 2 * lax.axis_index("x") + lax.axis_index("y")
    c_idx = lax.axis_index("c").astype(jnp.int32).reshape(1)
    me_idx = me.astype(jnp.int32).reshape(1)
    shard_of = {n: _to_rows(n, wts[n]).astype(BF16) for n in BIG}
    shard_of["packed"] = _pack(wts, {n: wts[n] for n in SMALL}).astype(BF16)
    group = {"ffn1": BIG[0:3], "ffn2": BIG[3:6]}

    def slot_bufs(names):
        return [lax.dynamic_update_slice(lax.empty((N_CHIPS,) + shard_of[n].shape, BF16), shard_of[n][None],
                                         (me, 0, 0)) for n in names]

    def ffn_weights(tag, gathered):
        return {f"{tag}_wgt": gathered[0].reshape(D_FF, D_MODEL), f"{tag}_wut": gathered[1].reshape(D_FF, D_MODEL),
                f"{tag}_wd": gathered[2].reshape(D_FF, D_MODEL)}

    def w_in_weights(gathered):
        w_in_full = jnp.concatenate([gathered[0][q] for q in range(N_CHIPS)], axis=1)
        return {"w_in_h": w_in_full[:, :N_HGRN_COLS],
                "w_in_r": jnp.pad(w_in_full[:, N_HGRN_COLS:], ((0, 0), (0, N_RWKV_PAD - N_RWKV_COLS)))}

    def mixer_weights(gathered):
        w_out_full = gathered[0].reshape(D_MODEL, D_MODEL)
        packs = gathered[1].reshape(N_CHIPS, PACK_ROWS * LANES)
        full, off = {}, 0
        for n in PACKED:
            shp = _shard_shape(n)
            full[n] = jnp.concatenate([packs[q, off:off + _numel(shp)].reshape(shp) for q in range(N_CHIPS)], axis=1)
            off += _numel(shp)
        zrow = lambda nrow: jnp.zeros((nrow, W_B), BF16)
        return {"w_out": w_out_full,
                "w2_pad": jnp.concatenate([full["rwkv_w2"], zrow(LORA_PAD - 32)], axis=0),
                "a2_pad": jnp.concatenate([zrow(32), full["rwkv_a2"], zrow(LORA_PAD - 64)], axis=0),
                "g2_pad": jnp.concatenate([zrow(64), full["rwkv_g2"], zrow(LORA_PAD - 160)], axis=0)}

    plan = _Plan()
    w = {}
    plan.carry("ffn1_rms", lambda g: _gather_comm(slot_bufs(("ffn1_w_gate",))),
               lambda res, w_: w_.update({"ffn1_wgt": res[0].reshape(D_FF, D_MODEL)}))
    plan.carry("ffn1_gate", lambda g: _gather_comm(slot_bufs(("ffn1_w_up",))),
               lambda res, w_: w_.update({"ffn1_wut": res[0].reshape(D_FF, D_MODEL)}))

    def after_up_act(res, w_):
        w_["ffn1_wd"] = res[0].reshape(D_FF, D_MODEL)
        w_.update(w_in_weights(res[1:]))

    plan.carry("ffn1_up_act", lambda g: _gather_comm(slot_bufs(("ffn1_w_down", "w_in"))), after_up_act)
    plan.carry("ffn1_down", lambda g: _gather_comm(slot_bufs(("w_out", "packed"))),
               lambda res, w_: w_.update(mixer_weights(res)))
    plan.carry("rwkv_fwd", lambda g: _gather_comm(slot_bufs(group["ffn2"])),
               lambda res, w_: w_.update(ffn_weights("ffn2", res)))
    w["ffn1_norm"], w["ffn2_norm"] = ffn1_norm, ffn2_norm
    w["mix_norm"] = mix_norm
    w["lb0"], w["lb1"] = hgrn_lb_logits[0:1], hgrn_lb_logits[1:2]
    w["hgrn_out_norm"] = hgrn_out_norm
    w["mu_pad"] = jnp.pad(rwkv_shift_mu, ((0, 0), (0, N_RWKV_PAD - N_RWKV_COLS)))
    for n in ("rwkv_w0", "rwkv_a0", "rwkv_k_k", "rwkv_k_a", "rwkv_gn_w", "rwkv_gn_b"):
        w[n] = wts[n]
    w["rwkv_r_k"] = rwkv_r_k.reshape(1, W_B)
    w["final_norm"] = final_norm.reshape(1, D_MODEL)

    def reduce_rows(names, gs):
        r1 = _run_comm(_sibling_exchange_comm(gs), "grad_sibling_exchange")
        s4 = [_add_halves(gt, rt, c_idx, f"grad_add_halves_{n}") for gt, rt, n in zip(gs, r1, names)]
        return list(zip(_run_comm(_chip_exchange_comm(s4), "grad_chip_exchange"), s4))

    def swap_comm(names):
        return _sibling_swap_comm([early[n][0] for n in names], [early[n][1] for n in names])

    def after_swap(names):
        return lambda res, w_: swapped.update(zip(names, zip(res[:len(names)], res[len(names):])))

    early, swapped = {}, {}

    def reduce_early(names, grads_of, sibling_host, chips_host, swap_host):
        def sibling_comm(g):
            early[names, "gs"] = grads_of(g)
            return _sibling_exchange_comm(early[names, "gs"])

        def after_sibling(res, w_):
            early[names, "s4"] = [_add_halves(gt, rt, c_idx, f"grad_add_halves_{n}")
                                  for gt, rt, n in zip(early[names, "gs"], res, names)]

        plan.carry(sibling_host, sibling_comm, after_sibling)
        plan.carry(chips_host, lambda g: _chip_exchange_comm(early[names, "s4"]),
                   lambda res, w_: early.update(zip(names, zip(res, early[names, "s4"]))))
        if swap_host:
            plan.carry(swap_host, lambda g: swap_comm(names), after_swap(names))

    def proj_grads(g):
        g_w_in = jnp.concatenate([g["w_in_h"], g["w_in_r"][:, :N_RWKV_COLS]], axis=1)
        return [g["w_out"].reshape(N_CHIPS, -1, D_MODEL),
                jnp.stack([_quarter(g_w_in, "w_in", q) for q in range(N_CHIPS)])]

    rows_of = lambda keys: (lambda g: [g[k].reshape(N_CHIPS, -1, D_MODEL) for k in keys])
    reduce_early(group["ffn2"], rows_of(("ffn2_wgt", "ffn2_wut", "ffn2_wd")), "hgrn_bwd", "rwkv_bwd", "rwkv_prep_bwd")
    reduce_early(("w_out", "w_in"), proj_grads, "mix_drms", "ffn1_dact", "ffn1_dwg")
    reduce_early(("ffn1_w_down",), rows_of(("ffn1_wd",)), "ffn1_dwg", "ffn1_dwu", "ffn1_dh_g")
    reduce_early(("ffn1_w_gate",), rows_of(("ffn1_wgt",)), "ffn1_dwu", "ffn1_dh_g", "ffn1_dh_u")
    reduce_early(("ffn1_w_up",), rows_of(("ffn1_wut",)), "ffn1_dh_g", "ffn1_dh_u", None)
    loss_slab, grad_x, g = _local_step(x[0], loss_target[0], w, plan)
    loss = lax.psum(loss_slab[0, 0], ("x", "y", "c"))

    gfull = {
        "rwkv_w2": g["w2_pad"][0:32], "rwkv_a2": g["a2_pad"][32:64], "rwkv_g2": g["g2_pad"][64:160],
    }
    gsmall = {
        "ffn1_norm": g["ffn1_norm"], "mix_norm": g["mix_norm"],
        "hgrn_lb_logits": jnp.concatenate([g["lb0"], g["lb1"]], axis=0), "hgrn_out_norm": g["hgrn_out_norm"],
        "rwkv_shift_mu": g["mu_pad"][:, :N_RWKV_COLS], "rwkv_w0": g["rwkv_w0"], "rwkv_a0": g["rwkv_a0"],
        "rwkv_k_k": g["rwkv_k_k"], "rwkv_k_a": g["rwkv_k_a"], "rwkv_r_k": g["rwkv_r_k"],
        "rwkv_gn_w": g["rwkv_gn_w"], "rwkv_gn_b": g["rwkv_gn_b"], "ffn2_norm": g["ffn2_norm"],
        "final_norm": g["final_norm"],
    }
    packed = jnp.stack([_pack({n: _quarter(gfull[n], n, q) for n in PACKED}, gsmall) for q in range(N_CHIPS)])
    early["packed"], = reduce_rows(["packed"], [packed])
    last = ["ffn1_w_up", "packed"]
    after_swap(last)(_run_comm(swap_comm(last), "grad_sibling_swap"), w)
    names = list(BIG) + ["packed"]

    def rows_list(d):
        return [_to_rows(n, d[n]) for n in BIG] + [_pack(d, {n: d[n] for n in SMALL})]

    outs = [_adamw(wt, *swapped[n], mt, vt, me_idx, f"adamw_{n}")
            for wt, mt, vt, n in zip(rows_list(wts), rows_list(moms), rows_list(vars_), names)]
    results = []
    for k in range(4):
        per = [outs[i][k] for i in range(len(names))]
        d = {n: _from_rows(n, z) for n, z in zip(BIG, per[:-1])}
        d.update(_unpack(per[-1]))
        results.append(d)
    return (loss, grad_x[None], *[r[n] for r in results for n in ALL_WEIGHTS])
```

```python
import collections
import functools

import jax
import jax.numpy as jnp
from jax import lax
from jax.experimental import pallas as pl
from jax.experimental.pallas import tpu as pltpu

F32 = jnp.float32
BF16 = jnp.bfloat16
SDS = jax.ShapeDtypeStruct
MESH = pl.DeviceIdType.MESH

D_MODEL = 1024
D_FF = 2816
W_A = 512
W_B = 512
HA_HEADS, HA_DIM = 4, 128
HB_HEADS, HB_DIM = 8, 64
HGRN_CHUNK = 64
HGRN_GROUP = 8
RWKV_CHUNK = 16
RWKV_GROUP = 8
N_HGRN_COLS = 4 * W_A
N_RWKV_COLS = 3 * W_B + 32 + 32 + 96
N_RWKV_PAD = 1792
LORA_PAD = 256
NORM_EPS = 1e-6
RWKV_GN_EPS = 64e-5
L2_EPS = 1e-12
ADAM_LR, ADAM_B1, ADAM_B2, ADAM_EPS, ADAM_WD, ADAM_STEP = 0.001, 0.9, 0.999, 1e-8, 0.01, 10

N_CHIPS = 4
VMEM_LIMIT_V7X = 56 * 1024 * 1024
LANES = 1024

SHARDED_SHAPES = {
    "ffn1_w_gate": ((D_MODEL, D_FF), 1), "ffn1_w_up": ((D_MODEL, D_FF), 1), "ffn1_w_down": ((D_FF, D_MODEL), 0),
    "w_in": ((D_MODEL, N_HGRN_COLS + N_RWKV_COLS), 1), "rwkv_w2": ((32, W_B), 1), "rwkv_a2": ((32, W_B), 1),
    "rwkv_g2": ((96, W_B), 1), "w_out": ((D_MODEL, D_MODEL), 0),
    "ffn2_w_gate": ((D_MODEL, D_FF), 1), "ffn2_w_up": ((D_MODEL, D_FF), 1), "ffn2_w_down": ((D_FF, D_MODEL), 0),
}
SMALL = ("ffn1_norm", "mix_norm", "hgrn_lb_logits", "hgrn_out_norm", "rwkv_shift_mu", "rwkv_w0", "rwkv_a0",
         "rwkv_k_k", "rwkv_k_a", "rwkv_r_k", "rwkv_gn_w", "rwkv_gn_b", "ffn2_norm", "final_norm")
ALL_WEIGHTS = ("ffn1_norm", "ffn1_w_gate", "ffn1_w_up", "ffn1_w_down", "mix_norm", "w_in", "hgrn_lb_logits",
               "hgrn_out_norm", "rwkv_shift_mu", "rwkv_w0", "rwkv_w2", "rwkv_a0", "rwkv_a2", "rwkv_g2", "rwkv_k_k",
               "rwkv_k_a", "rwkv_r_k", "rwkv_gn_w", "rwkv_gn_b", "w_out", "ffn2_norm", "ffn2_w_gate", "ffn2_w_up",
               "ffn2_w_down", "final_norm")


def _shard_shape(name):
    shape, ax = SHARDED_SHAPES[name]
    return tuple(s // N_CHIPS if i == ax else s for i, s in enumerate(shape))


def _numel(shape):
    n = 1
    for s in shape:
        n *= s
    return n


def _params(sem=None):
    return pltpu.CompilerParams(dimension_semantics=sem, vmem_limit_bytes=VMEM_LIMIT_V7X)


def _split2(x):
    hi = x.astype(BF16)
    return hi, (x.astype(F32) - hi.astype(F32)).astype(BF16)


def _dg(x, y, cx, cy, hi):
    dn = (((cx,), (cy,)), ((), ()))
    dot = lambda p, q: lax.dot_general(p, q, dn, preferred_element_type=F32)
    if hi == "x3":
        (xh, xl), (yh, yl) = _split2(x), _split2(y)
        return dot(xh, yh) + (dot(xh, yl) + dot(xl, yh))
    return dot(x.astype(BF16), y.astype(BF16))


def _make_mm(hi, cotangent_forms=None):
    @jax.custom_vjp
    def nn(x, y):
        return _dg(x, y, 1, 0, hi)

    @jax.custom_vjp
    def nt(x, y):
        return _dg(x, y, 1, 1, hi)

    @jax.custom_vjp
    def tn(x, y):
        return _dg(x, y, 0, 0, hi)

    bnn, bnt, btn = cotangent_forms or (nn, nt, tn)
    nn.defvjp(lambda x, y: (nn(x, y), (x, y)), lambda r, g: (bnt(g, r[1]), btn(r[0], g)))
    nt.defvjp(lambda x, y: (nt(x, y), (x, y)), lambda r, g: (bnn(g, r[1]), btn(g, r[0])))
    tn.defvjp(lambda x, y: (tn(x, y), (x, y)), lambda r, g: (bnt(r[1], g), bnn(r[0], g)))
    return nn, nt, tn


_nn, _nt, _tn = _make_mm(False)
_nn_x3, _nt_x3, _tn_x3 = _make_mm("x3", (_nn, _nt, _tn))


def _tri_apply(x, transpose):
    c = x.shape[0]
    tri = (lax.broadcasted_iota(jnp.int32, (c, c), 1) <= lax.broadcasted_iota(jnp.int32, (c, c), 0)).astype(BF16)
    dn = (((0 if transpose else 1,), (0,)), ((), ()))
    p1, p2 = _split2(x)
    dot = lambda p: lax.dot_general(tri, p, dn, preferred_element_type=F32)
    return dot(p1) + dot(p2)


@jax.custom_vjp
def _cumsum_rows(x):
    return _tri_apply(x, False)


_cumsum_rows.defvjp(lambda x: (_tri_apply(x, False), None), lambda _, g: (_tri_apply(g, True),))


def _sigmoid(x):
    return 1.0 / (1.0 + jnp.exp(-x))


def _silu(x):
    return x * _sigmoid(x)


def _softplus(z):
    return jnp.maximum(z, 0.0) + jnp.log(1.0 + jnp.exp(-jnp.abs(z)))


def _mm(a, b, *, ta=False, tb=False, tm, tn, tk, name, out_dtype=F32, res=None, scale=None, comm=None):
    m = a.shape[1] if ta else a.shape[0]
    kdim = a.shape[0] if ta else a.shape[1]
    n = b.shape[0] if tb else b.shape[1]
    assert (b.shape[1] if tb else b.shape[0]) == kdim
    tm, tn, tk = min(tm, m), min(tn, n), min(tk, kdim)
    assert m % tm == 0 and n % tn == 0 and kdim % tk == 0, (name, m, n, kdim)
    nk = kdim // tk
    a_spec = pl.BlockSpec((tk, tm), lambda i, j, k: (k, i)) if ta else pl.BlockSpec((tm, tk), lambda i, j, k: (i, k))
    b_spec = pl.BlockSpec((tn, tk), lambda i, j, k: (j, k)) if tb else pl.BlockSpec((tk, tn), lambda i, j, k: (k, j))
    o_spec = pl.BlockSpec((tm, tn), lambda i, j, k: (i, j))
    ca, cb = (0 if ta else 1), (1 if tb else 0)

    def body(*refs):
        if res is not None:
            a_ref, b_ref, r_ref, o_ref, acc_ref = refs
        else:
            a_ref, b_ref, o_ref, acc_ref = refs
        k = pl.program_id(2)

        @pl.when(k == 0)
        def _():
            acc_ref[...] = jnp.zeros_like(acc_ref)

        acc_ref[...] += _dg(a_ref[...], b_ref[...], ca, cb, False)

        @pl.when(k == nk - 1)
        def _():
            acc = acc_ref[...]
            if scale is not None:
                acc = acc * scale
            if res is not None:
                acc = r_ref[...] + acc
            o_ref[...] = acc.astype(out_dtype)

    in_specs = [a_spec, b_spec] + ([o_spec] if res is not None else [])
    args = (a, b) + ((res,) if res is not None else ())
    if comm is None:
        return pl.pallas_call(
            body, name=name, grid=(m // tm, n // tn, nk), in_specs=in_specs, out_specs=o_spec,
            out_shape=SDS((m, n), out_dtype), scratch_shapes=[pltpu.VMEM((tm, tn), F32)],
            compiler_params=_params(("parallel", "parallel", "arbitrary")))(*args)
    (out,), carried = _hosting_call(
        body, comm, name=name, grid=(m // tm, n // tn, nk), in_specs=in_specs, out_specs=[o_spec],
        out_shape=[SDS((m, n), out_dtype)], scratch_shapes=[pltpu.VMEM((tm, tn), F32)], args=args)
    return out, carried


def _mm_pair(a1, b1, a2, b2, *, tb=False, tm, name):
    m = a1.shape[0]
    n = b1.shape[0] if tb else b1.shape[1]
    tm = min(tm, m)
    cb = 1 if tb else 0
    assert a2.shape[0] == m and m % tm == 0
    assert all(b.shape[cb] == a.shape[1] and b.shape[1 - cb] == n for a, b in ((a1, b1), (a2, b2)))

    def body(a1_ref, b1_ref, a2_ref, b2_ref, o_ref):
        o_ref[...] = _dg(a1_ref[...], b1_ref[...], 1, cb, False) + _dg(a2_ref[...], b2_ref[...], 1, cb, False)

    a_spec = lambda a: pl.BlockSpec((tm, a.shape[1]), lambda i: (i, 0))
    b_spec = lambda b: pl.BlockSpec(b.shape, lambda i: (0, 0))
    return pl.pallas_call(
        body, name=name, grid=(m // tm,), in_specs=[a_spec(a1), b_spec(b1), a_spec(a2), b_spec(b2)],
        out_specs=pl.BlockSpec((tm, n), lambda i: (i, 0)), out_shape=SDS((m, n), F32),
        compiler_params=_params(("parallel",)))(a1, b1, a2, b2)


def _row_spec(x, tm, tile_of=lambda i: i):
    if isinstance(x, tuple):
        arr, w, j = x
        return arr, pl.BlockSpec((tm, w), lambda i, j=j: (tile_of(i), j))
    return x, pl.BlockSpec((tm, x.shape[1]), lambda i: (tile_of(i), 0))


def _par_spec(p):
    if isinstance(p, tuple):
        arr, w, j = p
        return arr, pl.BlockSpec((arr.shape[0], w), lambda i, j=j: (0, j))
    return p, pl.BlockSpec(p.shape, lambda i: (0, 0))


def _store_groups(refs, groups, vals):
    for ref, idxs in zip(refs, groups):
        off = 0
        for ix in idxs:
            v = vals[ix]
            ref[:, off:off + v.shape[1]] = v.astype(ref.dtype)
            off += v.shape[1]


SUBLANES = 8


def _x_plan(xs, tm, t, tile_of=lambda i: i):
    arrays, specs, plan = [], [], []
    nb = tm // SUBLANES
    for x in xs:
        if isinstance(x, tuple) and isinstance(x[0], str):
            kind, arr, w, j = x
            if kind == "prev":
                halo = lambda i, j=j: (jnp.maximum(tile_of(i) * nb - 1, 0), j)
            else:
                halo = lambda i, j=j: (jnp.minimum((tile_of(i) + 1) * nb, t // SUBLANES - 1), j)
            arrays += [arr, arr]
            specs += [pl.BlockSpec((tm, w), lambda i, j=j: (tile_of(i), j)), pl.BlockSpec((SUBLANES, w), halo)]
            plan.append((kind, 2, w))
        else:
            arr, spec = _row_spec(x, tm, tile_of)
            arrays.append(arr)
            specs.append(spec)
            plan.append(("plain", 1, spec.block_shape[1]))
    return arrays, specs, plan


def _x_vals(refs, plan, tm, nt, tile_of=lambda i: i):
    vals, k = [], 0
    i = tile_of(pl.program_id(0))
    rows = lax.broadcasted_iota(jnp.int32, (tm, 1), 0)
    for kind, n, _ in plan:
        main = refs[k][...].astype(F32)
        if kind == "prev":
            edge = jnp.where(i == 0, 0.0, refs[k + 1][SUBLANES - 1:SUBLANES, :].astype(F32))
            main = jnp.where(rows == 0, edge, pltpu.roll(main, 1, 0))
        elif kind == "next":
            edge = jnp.where(i == nt - 1, 0.0, refs[k + 1][0:1, :].astype(F32))
            main = jnp.where(rows == tm - 1, edge, pltpu.roll(main, tm - 1, 0))
        vals.append(main)
        k += n
    return vals


def _tile_rows(xs, tm):
    arr = xs[0]
    if isinstance(arr, tuple):
        arr = arr[1] if isinstance(arr[0], str) else arr[0]
    return min(tm, arr.shape[0]), arr.shape[0]


def _rowwise(f, xs, params, out_groups, out_dtypes, *, tm, name, comm=None):
    tm, t = _tile_rows(xs, tm)
    nt = t // tm
    xa, xspecs, plan = _x_plan(xs, tm, t)
    pa, pspecs = (zip(*[_par_spec(p) for p in params]) if params else ((), ()))
    nxr, npar = len(xa), len(pa)
    x_sds = [SDS((tm, w), F32) for _, _, w in plan]
    p_sds = [SDS(s.block_shape, F32) for s in pspecs]
    outs_sds = jax.eval_shape(lambda *vals: f(*vals), *x_sds, *p_sds)
    widths = [sum(outs_sds[ix].shape[1] for ix in idxs) for idxs in out_groups]

    def body(*refs):
        vals = _x_vals(refs[:nxr], plan, tm, nt) + [r[...].astype(F32) for r in refs[nxr:nxr + npar]]
        outs = f(*vals)
        _store_groups(refs[nxr + npar:], out_groups, outs)

    res, carried = _hosting_call(
        body, comm, name=name, grid=(nt,), in_specs=list(xspecs) + list(pspecs),
        out_specs=[pl.BlockSpec((tm, w), lambda i: (i, 0)) for w in widths],
        out_shape=[SDS((t, w), dt) for w, dt in zip(widths, out_dtypes)], scratch_shapes=[], args=(*xa, *pa))
    return res if comm is None else (res, carried)


def _rowwise_bwd(f, xs, params, cots, *, x_grad, p_grad, dx_groups, dx_dtypes, tm, name, extra=None, comm=None,
                 fold_next=None):
    tm, t = _tile_rows(xs, tm)
    nt = t // tm
    tile_of = (lambda i: nt - 1 - i) if fold_next else (lambda i: i)
    xa, xspecs, plan = _x_plan(xs, tm, t, tile_of)
    pa, pspecs = (zip(*[_par_spec(p) for p in params]) if params else ((), ()))
    ca, cspecs = zip(*[_row_spec(c, tm, tile_of) for c in cots])
    extra = extra or {}
    ekeys = sorted(extra)
    ea, especs = (zip(*[_row_spec(extra[k], tm, tile_of) for k in ekeys]) if ekeys else ((), ()))
    nx, nxr, npar, nc, ne = len(plan), len(xa), len(pa), len(ca), len(ea)
    gx = [i for i in range(nx) if x_grad[i]]
    gp = [i for i in range(npar) if p_grad[i]]
    all_widths = [sum(plan[gx[ix]][2] for ix in idxs) for idxs in dx_groups]
    emitted = [k for k in range(len(dx_groups)) if not (fold_next and k == fold_next[1])]
    widths = [all_widths[k] for k in emitted]
    ng = len(emitted)

    def body(*refs):
        ins = refs[:nxr + npar + nc + ne]
        outs = refs[nxr + npar + nc + ne:]
        vals = _x_vals(ins[:nxr], plan, tm, nt, tile_of) + [r[...].astype(F32) for r in ins[nxr:nxr + npar]]
        cvals = tuple(r[...].astype(F32) for r in ins[nxr + npar:nxr + npar + nc])
        evals = [r[...].astype(F32) for r in ins[nxr + npar + nc:]]
        diff_idx = gx + [nx + i for i in gp]

        def g(*dargs):
            full = list(vals)
            for ix, v in zip(diff_idx, dargs):
                full[ix] = v
            return tuple(f(*full))

        _, vjp = jax.vjp(g, *[vals[ix] for ix in diff_idx])
        grads = vjp(cvals)
        dxs = list(grads[:len(gx)])
        for k, ev in zip(ekeys, evals):
            dxs[k] = dxs[k] + ev
        _store_groups(outs[:ng], [dx_groups[k] for k in emitted], dxs)
        i = pl.program_id(0)
        if fold_next:
            main_ref, carry_ref = outs[emitted.index(fold_next[0])], refs[-1]
            rows = lax.broadcasted_iota(jnp.int32, (tm, 1), 0)
            off = 0
            for ix in dx_groups[fold_next[1]]:
                piece = dxs[ix]
                cols = slice(off, off + piece.shape[1])
                edge = jnp.where(i == 0, 0.0, carry_ref[0:1, cols])
                main_ref[:, cols] += jnp.where(rows == tm - 1, edge, pltpu.roll(piece, tm - 1, 0))
                carry_ref[:, cols] = piece[:SUBLANES]
                off += piece.shape[1]
        for ref, gval in zip(outs[ng:ng + len(gp)], grads[len(gx):]):
            @pl.when(i == 0)
            def _(ref=ref):
                ref[...] = jnp.zeros_like(ref)
            ref[...] += gval

    dp_specs = [pl.BlockSpec(pspecs[i].block_shape, lambda i: (0, 0)) for i in gp]
    dp_shapes = [SDS(pspecs[i].block_shape, F32) for i in gp]
    scratch = [pltpu.VMEM((SUBLANES, all_widths[fold_next[1]]), F32)] if fold_next else []
    res, carried = _hosting_call(
        body, comm, name=name, grid=(nt,), in_specs=list(xspecs) + list(pspecs) + list(cspecs) + list(especs),
        out_specs=[pl.BlockSpec((tm, w), lambda i: (tile_of(i), 0)) for w in widths] + dp_specs,
        out_shape=[SDS((t, w), dt) for w, dt in zip(widths, dx_dtypes)] + dp_shapes, scratch_shapes=scratch,
        args=(*xa, *pa, *ca, *ea))
    return res if comm is None else (res, carried)


def _rms_f(x, g):
    return (x * lax.rsqrt(jnp.mean(x * x, axis=-1, keepdims=True) + NORM_EPS) * g,)


def _group_sum_impl(x, ones_bd):
    p1, p2 = _split2(x)
    dot = lambda p: lax.dot_general(p, ones_bd.astype(BF16), (((1,), (0,)), ((), ())), preferred_element_type=F32)
    return dot(p1) + dot(p2)


@jax.custom_vjp
def _group_sum(x, ones_bd):
    return _group_sum_impl(x, ones_bd)


_group_sum.defvjp(lambda x, o: (_group_sum_impl(x, o), o),
                  lambda o, g: (_group_sum_impl(g, o), jnp.zeros_like(o)))


def _rwkv_prep_f(r, k, v, lo, rp, kp, vp, lop, mu_r, mu_k, mu_v, mu_lo, w0, w2p, a0, a2p, g2p, k_k, k_a, ones_bd):
    r = r + mu_r * (rp - r)
    k = k + mu_k * (kp - k)
    v = v + mu_v * (vp - v)
    lo = lo + mu_lo * (lop - lo)
    w_log = -_softplus(-(w0 + _nn(jnp.tanh(lo), w2p))) - 0.5
    lw = -jnp.exp(w_log)
    a_g = _sigmoid(a0 + _nn(lo, a2p))
    g = _nn(_sigmoid(lo), g2p)
    kk = k * k_k
    kk = kk / jnp.maximum(jnp.sqrt(_group_sum(kk * kk, ones_bd)), L2_EPS)
    k2 = k * (1.0 + (a_g - 1.0) * k_a)
    return r, lw, k2, v, -kk, kk * a_g, g


def _rwkv_post_f(y, r, k2, v, g, r_k, gn_w, gn_b, ones_bd):
    inv_n = 1.0 / HB_DIM
    mean = _group_sum(y, ones_bd) * inv_n
    yc = y - mean
    var = _group_sum(yc * yc, ones_bd) * inv_n
    yn = yc * lax.rsqrt(var + RWKV_GN_EPS) * gn_w + gn_b
    bonus = _group_sum(r * k2 * r_k, ones_bd) * v
    return ((yn + bonus) * g,)


def _tri(c, strict=False):
    ii = lax.broadcasted_iota(jnp.int32, (c, c), 0)
    jj = lax.broadcasted_iota(jnp.int32, (c, c), 1)
    return (jj < ii) if strict else (jj <= ii)


def _hgrn_step(st0, q_a, f_a, i_a, g_a, l0, l1, onorm):
    nh, nj = len(q_a), len(q_a[0])
    c = q_a[0][0].shape[0]
    combos = [(j, h) for j in range(nj) for h in range(nh)]
    every = lambda fn: {q: fn(q) for q in combos}
    at_ = lambda d: (lambda q: d[q[1]][q[0]])
    qa_, fa_, ia_, ga_ = (at_(z) for z in (q_a, f_a, i_a, g_a))
    incl = _tri(c)
    rows = lax.broadcasted_iota(jnp.int32, (c, 1), 0)
    lb = []
    for h in range(nh):
        mx = jnp.maximum(l0[h], l1[h])
        e0, e1 = jnp.exp(l0[h] - mx), jnp.exp(l1[h] - mx)
        lb.append(e0 / (e0 + e1))
    forget = every(lambda q: lb[q[1]] + (1.0 - lb[q[1]]) * _sigmoid(fa_(q)))
    qs = every(lambda q: _silu(qa_(q)))
    kk = every(lambda q: 1.0 - forget[q])
    lf = every(lambda q: jnp.log(forget[q]))
    bcum = every(lambda q: _cumsum_rows(lf[q]))
    bref = every(lambda q: jnp.sum(jnp.where(rows <= c // 2, lf[q], 0.0), axis=0, keepdims=True))
    blast = every(lambda q: jnp.sum(lf[q], axis=0, keepdims=True))
    scores = every(lambda q: jnp.where(incl, _nt(qs[q] * jnp.exp(bcum[q] - bref[q]),
                                                 kk[q] * jnp.exp(bref[q] - bcum[q])), 0.0))
    intra = every(lambda q: _nn(scores[q], ia_(q)))
    qb = every(lambda q: qs[q] * jnp.exp(bcum[q]))
    upd = every(lambda q: _tn(ia_(q), kk[q] * jnp.exp(blast[q] - bcum[q])))
    dec = every(lambda q: jnp.exp(blast[q]))
    st = list(st0)
    o = {}
    for j in range(nj):
        for h in range(nh):
            o[(j, h)] = intra[(j, h)] + _nt(qb[(j, h)], st[h])
        st = [st[h] * dec[(j, h)] + upd[(j, h)] for h in range(nh)]
    out = every(lambda q: o[q] * lax.rsqrt(jnp.mean(o[q] * o[q], axis=-1, keepdims=True) + NORM_EPS)
                * onorm[q[1]] * _silu(ga_(q)))
    return [[out[(j, h)] for j in range(nj)] for h in range(nh)], st


def _hgrn_blocks(ref, nj, c):
    return [[ref[j * c:(j + 1) * c, h * HA_DIM:(h + 1) * HA_DIM] for j in range(nj)] for h in range(HA_HEADS)]


def _hgrn_cols(ref):
    return [ref[:, h * HA_DIM:(h + 1) * HA_DIM] for h in range(HA_HEADS)]


def _hgrn_fwd(p_h, l0, l1, onorm):
    t = p_h.shape[0]
    cc, nj = HGRN_CHUNK, HGRN_GROUP
    c = cc * nj
    n = t // c

    def body(q_ref, f_ref, i_ref, g_ref, l0_ref, l1_ref, on_ref, o_ref, hs_ref, st_ref):
        @pl.when(pl.program_id(0) == 0)
        def _():
            st_ref[...] = jnp.zeros_like(st_ref)

        hs_ref[0] = st_ref[...]
        o, st1 = _hgrn_step([st_ref[h] for h in range(HA_HEADS)],
                            *[_hgrn_blocks(ref, nj, cc) for ref in (q_ref, f_ref, i_ref, g_ref)],
                            _hgrn_cols(l0_ref), _hgrn_cols(l1_ref), _hgrn_cols(on_ref))
        for h in range(HA_HEADS):
            for j in range(nj):
                o_ref[j * cc:(j + 1) * cc, h * HA_DIM:(h + 1) * HA_DIM] = o[h][j]
            st_ref[h] = st1[h]

    col = lambda j: pl.BlockSpec((c, W_A), lambda i, j=j: (i, j))
    par = pl.BlockSpec((1, W_A), lambda i: (0, 0))
    return pl.pallas_call(
        body, name="hgrn_fwd", grid=(n,), in_specs=[col(0), col(1), col(2), col(3), par, par, par],
        out_specs=[pl.BlockSpec((c, W_A), lambda i: (i, 0)),
                   pl.BlockSpec((1, HA_HEADS, HA_DIM, HA_DIM), lambda i: (i, 0, 0, 0))],
        out_shape=[SDS((t, W_A), F32), SDS((n, HA_HEADS, HA_DIM, HA_DIM), F32)],
        scratch_shapes=[pltpu.VMEM((HA_HEADS, HA_DIM, HA_DIM), F32)],
        compiler_params=_params(("arbitrary",)))(p_h, p_h, p_h, p_h, l0, l1, onorm)


def _hgrn_bwd(p_h, l0, l1, onorm, hs, do, do_col, comm=None):
    t = p_h.shape[0]
    cc, nj = HGRN_CHUNK, HGRN_GROUP
    c = cc * nj
    n = t // c

    def body(q_ref, f_ref, i_ref, g_ref, l0_ref, l1_ref, on_ref, hs_ref, do_ref,
             dp_ref, dl0_ref, dl1_ref, don_ref, dst_ref):
        @pl.when(pl.program_id(0) == 0)
        def _():
            dst_ref[...] = jnp.zeros_like(dst_ref)
            dl0_ref[...] = jnp.zeros_like(dl0_ref)
            dl1_ref[...] = jnp.zeros_like(dl1_ref)
            don_ref[...] = jnp.zeros_like(don_ref)

        args = ([hs_ref[0, h] for h in range(HA_HEADS)],
                *[_hgrn_blocks(ref, nj, cc) for ref in (q_ref, f_ref, i_ref, g_ref)],
                _hgrn_cols(l0_ref), _hgrn_cols(l1_ref), _hgrn_cols(on_ref))
        _, vjp = jax.vjp(_hgrn_step, *args)
        dst0, dq, df, di, dg, dl0, dl1, don = vjp((_hgrn_blocks(do_ref, nj, cc),
                                                   [dst_ref[h] for h in range(HA_HEADS)]))
        for h in range(HA_HEADS):
            sl = slice(h * HA_DIM, (h + 1) * HA_DIM)
            for k, dv in enumerate((dq, df, di, dg)):
                for j in range(nj):
                    dp_ref[j * cc:(j + 1) * cc, k * W_A + h * HA_DIM:k * W_A + (h + 1) * HA_DIM] = dv[h][j]
            dl0_ref[:, sl] += dl0[h]
            dl1_ref[:, sl] += dl1[h]
            don_ref[:, sl] += don[h]
            dst_ref[h] = dst0[h]

    col = lambda j: pl.BlockSpec((c, W_A), lambda i, j=j: (n - 1 - i, j))
    par = pl.BlockSpec((1, W_A), lambda i: (0, 0))
    return _hosting_call(
        body, comm, name="hgrn_bwd", grid=(n,),
        in_specs=[col(0), col(1), col(2), col(3), par, par, par,
                  pl.BlockSpec((1, HA_HEADS, HA_DIM, HA_DIM), lambda i: (n - 1 - i, 0, 0, 0)),
                  pl.BlockSpec((c, W_A), lambda i: (n - 1 - i, do_col))],
        out_specs=[pl.BlockSpec((c, N_HGRN_COLS), lambda i: (n - 1 - i, 0)), par, par, par],
        out_shape=[SDS((t, N_HGRN_COLS), F32), SDS((1, W_A), F32), SDS((1, W_A), F32), SDS((1, W_A), F32)],
        scratch_shapes=[pltpu.VMEM((HA_HEADS, HA_DIM, HA_DIM), F32)],
        args=(p_h, p_h, p_h, p_h, l0, l1, onorm, hs, do))


HB_PAIRS = HB_HEADS // 2
PAIR_W = 2 * HB_DIM


def _head_lane_masks():
    lane = lax.broadcasted_iota(jnp.int32, (1, PAIR_W), 1)
    return (lane < HB_DIM).astype(F32), (lane >= HB_DIM).astype(F32)


@jax.custom_vjp
def _stack_heads(x):
    m0, m1 = _head_lane_masks()
    return jnp.concatenate([x * m0, x * m1], axis=0)


def _stack_heads_bwd(_, g):
    m0, m1 = _head_lane_masks()
    c = g.shape[0] // 2
    return (g[:c] * m0 + g[c:] * m1,)


_stack_heads.defvjp(lambda x: (_stack_heads(x), None), _stack_heads_bwd)


@jax.custom_vjp
def _unstack_heads(ys):
    c = ys.shape[0] // 2
    return ys[:c] + ys[c:]


_unstack_heads.defvjp(lambda ys: (_unstack_heads(ys), None), lambda _, g: (_stack_heads(g),))


def _same_head_block(c):
    ii = lax.broadcasted_iota(jnp.int32, (2 * c, 2 * c), 0)
    jj = lax.broadcasted_iota(jnp.int32, (2 * c, 2 * c), 1)
    same = (ii < c) == (jj < c)
    return same & (jj <= ii), same & (jj < ii), (ii == jj).astype(F32)


@jax.custom_vjp
def _rows_join(top, bottom):
    return jnp.concatenate([top, bottom], axis=0)


def _rows_join_bwd(n_top, g):
    return g[:n_top], g[n_top:]


_rows_join.defvjp(lambda top, bottom: (_rows_join(top, bottom), top.shape[0]), _rows_join_bwd)


def _rows_split_impl(x, n_top):
    return x[:n_top], x[n_top:]


_rows_split = jax.custom_vjp(_rows_split_impl, nondiff_argnums=(1,))
_rows_split.defvjp(lambda x, n_top: (_rows_split_impl(x, n_top), None),
                   lambda n_top, _, g: (jnp.concatenate([g[0], g[1]], axis=0),))


def _rwkv_step(s0, r, lw, k, v, a, b):
    npair, nj = len(r), len(r[0])
    c = r[0][0].shape[0]
    combos = [(j, p) for j in range(nj) for p in range(npair)]
    every = lambda fn: {q: fn(q) for q in combos}
    at_ = lambda d: (lambda q: d[q[1]][q[0]])
    r_, lw_, k_, v_, a_, b_ = (at_(z) for z in (r, lw, k, v, a, b))
    incl, strict, eye = _same_head_block(c)

    gam = every(lambda q: _cumsum_rows(lw_(q)))
    gtot = every(lambda q: jnp.sum(lw_(q), axis=0, keepdims=True))
    eneg = every(lambda q: jnp.exp(-gam[q]))
    edec = every(lambda q: jnp.exp(gtot[q] - gam[q]))
    at = every(lambda q: _stack_heads(a_(q) * jnp.exp(gam[q] - lw_(q))))
    rt = every(lambda q: _stack_heads(r_(q) * jnp.exp(gam[q])))
    bt = every(lambda q: _stack_heads(b_(q) * eneg[q]))
    kt = every(lambda q: _stack_heads(k_(q) * eneg[q]))
    bdec = every(lambda q: _stack_heads(b_(q) * edec[q]))
    kdec = every(lambda q: _stack_heads(k_(q) * edec[q]))
    vs = every(lambda q: _stack_heads(v_(q)))
    a_ab = every(lambda q: jnp.where(strict, _nt(at[q], bt[q]), 0.0))
    a_ak = every(lambda q: jnp.where(strict, _nt(at[q], kt[q]), 0.0))
    a_rb = every(lambda q: jnp.where(incl, _nt(rt[q], bt[q]), 0.0))
    a_rk = every(lambda q: jnp.where(incl, _nt(rt[q], kt[q]), 0.0))
    tinv = every(lambda q: eye + a_ab[q])
    pw = a_ab
    span = 2
    while span < c:
        pw = every(lambda q, pw=pw: _nn_x3(pw[q], pw[q]))
        tinv = every(lambda q, pw=pw, tinv=tinv: tinv[q] + _nn_x3(pw[q], tinv[q]))
        span *= 2
    akv = every(lambda q: _nn(a_ak[q], vs[q]))
    w1 = every(lambda q: _nn(tinv[q], at[q]))
    u0 = every(lambda q: _nn(tinv[q], akv[q]))
    wr = every(lambda q: _rows_join(w1[q], rt[q]))
    bk = every(lambda q: _rows_join(bdec[q], kdec[q]))
    yv = every(lambda q: _nn(a_rk[q], vs[q]))
    gdec = every(lambda q: jnp.exp(gtot[q]))

    s = list(s0)
    y = [[None] * nj for _ in range(npair)]
    for j in range(nj):
        both = {p: _rows_split(_nt(wr[(j, p)], s[p]), 2 * c) for p in range(npair)}
        u = {p: both[p][0] + u0[(j, p)] for p in range(npair)}
        for p in range(npair):
            y[p][j] = _unstack_heads(both[p][1] + _nn(a_rb[(j, p)], u[p]) + yv[(j, p)])
        s = [s[p] * gdec[(j, p)] + _tn(_rows_join(u[p], vs[(j, p)]), bk[(j, p)]) for p in range(npair)]
    return y, s


def _rwkv_blocks(ref, nj, c):
    return [[ref[j * c:(j + 1) * c, p * PAIR_W:(p + 1) * PAIR_W] for j in range(nj)] for p in range(HB_PAIRS)]


def _rwkv_fwd(seqs, comm=None):
    t = seqs[0].shape[0]
    c, nj = RWKV_CHUNK, RWKV_GROUP
    n = t // (c * nj)

    def body(r_ref, lw_ref, k_ref, v_ref, a_ref, b_ref, y_ref, hs_ref, st_ref):
        @pl.when(pl.program_id(0) == 0)
        def _():
            st_ref[...] = jnp.zeros_like(st_ref)

        hs_ref[0] = st_ref[...]
        s0 = [st_ref[p] for p in range(HB_PAIRS)]
        y, s1 = _rwkv_step(s0, *[_rwkv_blocks(ref, nj, c) for ref in (r_ref, lw_ref, k_ref, v_ref, a_ref, b_ref)])
        for p in range(HB_PAIRS):
            for j in range(nj):
                y_ref[j * c:(j + 1) * c, p * PAIR_W:(p + 1) * PAIR_W] = y[p][j]
            st_ref[p] = s1[p]

    seq = pl.BlockSpec((c * nj, W_B), lambda i: (i, 0))
    return _hosting_call(
        body, comm, name="rwkv_fwd", grid=(n,), in_specs=[seq] * 6,
        out_specs=[seq, pl.BlockSpec((1, HB_PAIRS, PAIR_W, PAIR_W), lambda i: (i, 0, 0, 0))],
        out_shape=[SDS((t, W_B), F32), SDS((n, HB_PAIRS, PAIR_W, PAIR_W), F32)],
        scratch_shapes=[pltpu.VMEM((HB_PAIRS, PAIR_W, PAIR_W), F32)], args=tuple(seqs))


def _rwkv_bwd(seqs, hs, dy, comm=None):
    t = seqs[0].shape[0]
    c, nj = RWKV_CHUNK, RWKV_GROUP
    n = t // (c * nj)

    def body(r_ref, lw_ref, k_ref, v_ref, a_ref, b_ref, hs_ref, dy_ref,
             dr_ref, dlw_ref, dk_ref, dv_ref, da_ref, db_ref, dst_ref):
        @pl.when(pl.program_id(0) == 0)
        def _():
            dst_ref[...] = jnp.zeros_like(dst_ref)

        s0 = [hs_ref[0, p] for p in range(HB_PAIRS)]
        seq_vals = [_rwkv_blocks(ref, nj, c) for ref in (r_ref, lw_ref, k_ref, v_ref, a_ref, b_ref)]
        _, vjp = jax.vjp(_rwkv_step, s0, *seq_vals)
        grads = vjp((_rwkv_blocks(dy_ref, nj, c), [dst_ref[p] for p in range(HB_PAIRS)]))
        for ref, gr in zip((dr_ref, dlw_ref, dk_ref, dv_ref, da_ref, db_ref), grads[1:]):
            for p in range(HB_PAIRS):
                for j in range(nj):
                    ref[j * c:(j + 1) * c, p * PAIR_W:(p + 1) * PAIR_W] = gr[p][j]
        m0, m1 = _head_lane_masks()
        rows0 = (lax.broadcasted_iota(jnp.int32, (PAIR_W, 1), 0) < HB_DIM).astype(F32)
        blocks = rows0 * m0 + (1.0 - rows0) * m1
        for p in range(HB_PAIRS):
            dst_ref[p] = grads[0][p] * blocks

    seq = pl.BlockSpec((c * nj, W_B), lambda i: (n - 1 - i, 0))
    return _hosting_call(
        body, comm, name="rwkv_bwd", grid=(n,),
        in_specs=[seq] * 6 + [pl.BlockSpec((1, HB_PAIRS, PAIR_W, PAIR_W), lambda i: (n - 1 - i, 0, 0, 0)), seq],
        out_specs=[seq] * 6, out_shape=[SDS((t, W_B), F32)] * 6,
        scratch_shapes=[pltpu.VMEM((HB_PAIRS, PAIR_W, PAIR_W), F32)], args=(*seqs, hs, dy))


def _final_loss(x3, fnorm, target, *, tm):
    t, d = x3.shape

    def body(x_ref, g_ref, t_ref, dx_ref, dg_ref, loss_ref):
        @pl.when(pl.program_id(0) == 0)
        def _():
            dg_ref[...] = jnp.zeros_like(dg_ref)
            loss_ref[...] = jnp.zeros_like(loss_ref)

        x, g = x_ref[...], g_ref[...]
        rinv = lax.rsqrt(jnp.mean(x * x, axis=-1, keepdims=True) + NORM_EPS)
        xh = x * rinv
        diff = xh * g - t_ref[...]
        loss_ref[...] += 0.5 * jnp.sum(jnp.mean(diff * diff, axis=-1, keepdims=True))
        dy = diff * (1.0 / d)
        dg_ref[...] += jnp.sum(dy * xh, axis=0, keepdims=True)
        dxh = dy * g
        dx_ref[...] = rinv * (dxh - xh * jnp.mean(dxh * xh, axis=-1, keepdims=True))

    row = pl.BlockSpec((tm, d), lambda i: (i, 0))
    return pl.pallas_call(
        body, name="final_loss", grid=(t // tm,), in_specs=[row, pl.BlockSpec((1, d), lambda i: (0, 0)), row],
        out_specs=[row, pl.BlockSpec((1, d), lambda i: (0, 0)), pl.BlockSpec((8, 128), lambda i: (0, 0))],
        out_shape=[SDS((t, d), F32), SDS((1, d), F32), SDS((8, 128), F32)],
        compiler_params=_params(("arbitrary",)))(x3, fnorm, target)


def _gate_up_act(h, wgt, wut, *, tm, tn, name, comm=None):
    t, d = h.shape
    tm = min(tm, t)

    def body(h_ref, g_ref, u_ref, a_out, u_out, act_out):
        hv = h_ref[...]
        a = _dg(hv, g_ref[...], 1, 1, False)
        u = _dg(hv, u_ref[...], 1, 1, False)
        a_out[...] = a.astype(a_out.dtype)
        u_out[...] = u.astype(u_out.dtype)
        act_out[...] = (_silu(a) * u).astype(act_out.dtype)

    wspec = pl.BlockSpec((tn, d), lambda i, j: (j, 0))
    ospec = pl.BlockSpec((tm, tn), lambda i, j: (i, j))
    return _hosting_call(
        body, comm, name=name, grid=(t // tm, D_FF // tn),
        in_specs=[pl.BlockSpec((tm, d), lambda i, j: (i, 0)), wspec, wspec], out_specs=[ospec, ospec, ospec],
        out_shape=[SDS((t, D_FF), BF16), SDS((t, D_FF), BF16), SDS((t, D_FF), BF16)], scratch_shapes=[],
        args=(h, wgt, wut))


def _dact_swiglu(dout, wd, a, u, *, tm, tn, name, comm=None):
    t, d = dout.shape
    tm = min(tm, t)

    def body(d_ref, w_ref, a_ref, u_ref, da_out, du_out):
        dact = 0.5 * _dg(d_ref[...], w_ref[...], 1, 1, False)
        av, uv = a_ref[...].astype(F32), u_ref[...].astype(F32)
        s = _sigmoid(av)
        da_out[...] = (dact * uv * (s * (1.0 + av * (1.0 - s)))).astype(da_out.dtype)
        du_out[...] = (dact * (av * s)).astype(du_out.dtype)

    tile = pl.BlockSpec((tm, tn), lambda i, j: (i, j))
    return _hosting_call(
        body, comm, name=name, grid=(t // tm, D_FF // tn),
        in_specs=[pl.BlockSpec((tm, d), lambda i, j: (i, 0)), pl.BlockSpec((tn, d), lambda i, j: (j, 0)), tile, tile],
        out_specs=[tile, tile], out_shape=[SDS((t, D_FF), BF16), SDS((t, D_FF), BF16)], scratch_shapes=[],
        args=(dout, wd, a, u))


class _Plan:
    def __init__(self):
        self.entries, self.counts = collections.defaultdict(list), {}

    def carry(self, host, comm_of, after):
        self.entries[host].append((comm_of, after))

    def comm(self, host, g):
        comms = [comm_of(g) for comm_of, _ in self.entries.get(host, [])]
        self.counts[host] = [len(c.arrays) for c in comms]
        return functools.reduce(_join_comms, comms) if comms else None

    def done(self, host, results, w):
        start = 0
        for (_, after), n in zip(self.entries.get(host, []), self.counts.get(host, [])):
            after(results[start:start + n], w)
            start += n


def _ffn_fwd(x, w, tag, plan, g):
    comm = plan.comm(f"{tag}_rms", g)
    res = _rowwise(_rms_f, [x], [w[f"{tag}_norm"]], [[0]], [BF16], tm=512, name=f"{tag}_rms", comm=comm)
    (h,), carried = res if comm is not None else (res, [])
    plan.done(f"{tag}_rms", carried, w)
    (a, u, act), carried = _gate_up_act(h, w[f"{tag}_wgt"], w[f"{tag}_wut"], tm=2048, tn=256, name=f"{tag}_gate_up",
                                        comm=plan.comm(f"{tag}_gate_up", g))
    plan.done(f"{tag}_gate_up", carried, w)
    comm = plan.comm(f"{tag}_down", g)
    out = _mm(act, w[f"{tag}_wd"], tm=1024, tn=D_MODEL, tk=D_FF, name=f"{tag}_down", res=x, scale=0.5, comm=comm)
    if comm is not None:
        out, carried = out
        plan.done(f"{tag}_down", carried, w)
    return out, (h, a, u, act)


def _ffn_bwd(dout, x, w, saved, tag, plan, g):
    h, a, u, act = saved

    def carrying(fn, host, *args, **kwargs):
        comm = plan.comm(host, g)
        res = fn(*args, name=host, comm=comm, **kwargs)
        out, carried = res if comm is not None else (res, [])
        plan.done(host, carried, w)
        return out

    (da, du), carried = _dact_swiglu(dout, w[f"{tag}_wd"], a, u, tm=2048, tn=256, name=f"{tag}_dact",
                                     comm=plan.comm(f"{tag}_dact", g))
    plan.done(f"{tag}_dact", carried, w)
    g[f"{tag}_wd"] = _mm(act, dout, ta=True, tm=D_FF // 2, tn=D_MODEL, tk=1024, name=f"{tag}_dwd", scale=0.5)
    g[f"{tag}_wgt"] = carrying(_mm, f"{tag}_dwg", da, h, ta=True, tm=D_FF // 2, tn=D_MODEL, tk=1024)
    g[f"{tag}_wut"] = carrying(_mm, f"{tag}_dwu", du, h, ta=True, tm=D_FF // 2, tn=D_MODEL, tk=1024)
    if plan.entries.get(f"{tag}_dh_g") or plan.entries.get(f"{tag}_dh_u"):
        dh = carrying(_mm, f"{tag}_dh_g", da, w[f"{tag}_wgt"], tm=1024, tn=D_MODEL, tk=D_FF)
        dh = carrying(_mm, f"{tag}_dh_u", du, w[f"{tag}_wut"], tm=1024, tn=D_MODEL, tk=D_FF, res=dh)
    else:
        dh = _mm_pair(da, w[f"{tag}_wgt"], du, w[f"{tag}_wut"], tm=512, name=f"{tag}_dh")
    dx, g[f"{tag}_norm"] = carrying(_rowwise_bwd, f"{tag}_drms", _rms_f, [x], [w[f"{tag}_norm"]], [dh],
                                    x_grad=[True], p_grad=[True], dx_groups=[[0]], dx_dtypes=[F32], tm=512,
                                    extra={0: dout})
    return dx


def _local_step(x, target, w, plan=None):
    plan = plan or _Plan()
    ones_bd = jnp.kron(jnp.eye(HB_HEADS, dtype=F32), jnp.ones((HB_DIM, HB_DIM), F32))
    g = {}
    x1, ffn1_saved = _ffn_fwd(x, w, "ffn1", plan, g)
    hm, = _rowwise(_rms_f, [x1], [w["mix_norm"]], [[0]], [BF16], tm=512, name="mix_rms")
    p_h = _mm(hm, w["w_in_h"], tm=2048, tn=256, tk=D_MODEL, name="inproj_h")
    p_r = _mm(hm, w["w_in_r"], tm=2048, tn=256, tk=D_MODEL, name="inproj_r")
    o_a, hgrn_states = _hgrn_fwd(p_h, w["lb0"], w["lb1"], w["hgrn_out_norm"])

    mu = w["mu_pad"]
    prep_xs = [(p_r, W_B, 0), (p_r, W_B, 1), (p_r, W_B, 2), (p_r, LORA_PAD, 6),
               ("prev", p_r, W_B, 0), ("prev", p_r, W_B, 1), ("prev", p_r, W_B, 2), ("prev", p_r, LORA_PAD, 6)]
    prep_ps = [(mu, W_B, 0), (mu, W_B, 1), (mu, W_B, 2), (mu, LORA_PAD, 6), w["rwkv_w0"], w["w2_pad"], w["rwkv_a0"],
               w["a2_pad"], w["g2_pad"], w["rwkv_k_k"], w["rwkv_k_a"], ones_bd]
    prep_f = _rwkv_prep_f
    r, lw, k2, v, a_vec, b_vec, gate = _rowwise(prep_f, prep_xs, prep_ps, [[0], [1], [2], [3], [4], [5], [6]],
                                                [F32] * 7, tm=256, name="rwkv_prep")
    seqs = [r, lw, k2, v, a_vec, b_vec]
    (y, rwkv_states), carried = _rwkv_fwd(seqs, comm=plan.comm("rwkv_fwd", g))
    plan.done("rwkv_fwd", carried, w)
    post_f = _rwkv_post_f
    post_xs = [y, r, k2, v, gate]
    post_ps = [w["rwkv_r_k"], w["rwkv_gn_w"], w["rwkv_gn_b"], ones_bd]
    o, = _rowwise(lambda o_a_, *rest: (o_a_,) + tuple(post_f(*rest)), [o_a] + post_xs, post_ps, [[0, 1]], [F32],
                  tm=256, name="rwkv_post")
    x2 = _mm(o, w["w_out"], tm=2048, tn=256, tk=D_MODEL, name="outproj", res=x1)
    x3, ffn2_saved = _ffn_fwd(x2, w, "ffn2", plan, g)
    dx3, g["final_norm"], loss = _final_loss(x3, w["final_norm"], target, tm=256)

    dx2 = _ffn_bwd(dx3, x2, w, ffn2_saved, "ffn2", plan, g)
    do = _mm(dx2, w["w_out"], tb=True, tm=2048, tn=256, tk=D_MODEL, name="outproj_do")
    g["w_out"] = _mm(o, dx2, ta=True, tm=D_MODEL, tn=D_MODEL, tk=1024, name="outproj_dw")

    (dp_h, g["lb0"], g["lb1"], g["hgrn_out_norm"]), carried = _hgrn_bwd(
        p_h, w["lb0"], w["lb1"], w["hgrn_out_norm"], hgrn_states, do, 0, comm=plan.comm("hgrn_bwd", g))
    plan.done("hgrn_bwd", carried, w)
    post_out = _rowwise_bwd(post_f, post_xs, post_ps, [(do, W_B, 1)], x_grad=[True] * 5, p_grad=[True] * 3 + [False],
                            dx_groups=[[0], [1], [2], [3], [4]], dx_dtypes=[F32] * 5, tm=256, name="rwkv_post_bwd")
    dy, dr1, dk1, dv1, dgate, g["rwkv_r_k"], g["rwkv_gn_w"], g["rwkv_gn_b"] = post_out
    (dr2, dlw, dk2, dv2, da_vec, db_vec), carried = _rwkv_bwd(seqs, rwkv_states, dy, comm=plan.comm("rwkv_bwd", g))
    plan.done("rwkv_bwd", carried, w)

    def prep2_f(*vals):
        r_, lw_, k2_, v_, a_, b_, g_ = prep_f(*vals)
        return r_, lw_, k2_, v_, a_, b_, g_, r_, k2_, v_

    prep_comm = plan.comm("rwkv_prep_bwd", g)
    prep_out = _rowwise_bwd(prep2_f, prep_xs, prep_ps, [dr2, dlw, dk2, dv2, da_vec, db_vec, dgate, dr1, dk1, dv1],
                            x_grad=[True] * 8, p_grad=[True] * 11 + [False], dx_groups=[[0, 1, 2, 3], [4, 5, 6, 7]],
                            dx_dtypes=[F32], tm=256, name="rwkv_prep_bwd", fold_next=(0, 1), comm=prep_comm)
    prep_out, carried = prep_out if prep_comm is not None else (prep_out, [])
    plan.done("rwkv_prep_bwd", carried, w)
    dp_r = prep_out[0]
    (dmu_r, dmu_k, dmu_v, dmu_lo, g["rwkv_w0"], g["w2_pad"], g["rwkv_a0"], g["a2_pad"], g["g2_pad"],
     g["rwkv_k_k"], g["rwkv_k_a"]) = prep_out[1:]
    g["mu_pad"] = jnp.concatenate([dmu_r, dmu_k, dmu_v, dmu_lo], axis=1)
    dhm = _mm_pair(dp_h, w["w_in_h"], dp_r, w["w_in_r"], tb=True, tm=512, name="inproj_dh")
    g["w_in_h"] = _mm(hm, dp_h, ta=True, tm=D_MODEL, tn=D_MODEL, tk=1024, name="inproj_dw_h")
    g["w_in_r"] = _mm(hm, dp_r, ta=True, tm=D_MODEL, tn=N_RWKV_PAD // 2, tk=1024, name="inproj_dw_r")
    mix_comm = plan.comm("mix_drms", g)
    mix_out = _rowwise_bwd(_rms_f, [x1], [w["mix_norm"]], [dhm], x_grad=[True], p_grad=[True], dx_groups=[[0]],
                           dx_dtypes=[F32], tm=512, name="mix_drms", extra={0: dx2}, comm=mix_comm)
    (dx1, g["mix_norm"]), carried = mix_out if mix_comm is not None else (mix_out, [])
    plan.done("mix_drms", carried, w)
    dx0 = _ffn_bwd(dx1, x, w, ffn1_saved, "ffn1", plan, g)
    return loss, dx0, g


HBM_SPEC = pl.BlockSpec(memory_space=pl.ANY)

Comm = collections.namedtuple("Comm", "arrays out_shapes aliased sem_shapes start finish")


def _join_comms(first, second):
    n, s = len(first.arrays), len(first.sem_shapes)

    def start(ins, outs, sems):
        first.start(ins[:n], outs[:n], sems[:s])
        second.start(ins[n:], outs[n:], sems[s:])

    def finish(ins, outs, sems):
        first.finish(ins[:n], outs[:n], sems[:s])
        second.finish(ins[n:], outs[n:], sems[s:])

    return Comm(list(first.arrays) + list(second.arrays), list(first.out_shapes) + list(second.out_shapes),
                list(first.aliased) + list(second.aliased), list(first.sem_shapes) + list(second.sem_shapes),
                start, finish)


def _run_comm(comm, name):
    n = len(comm.arrays)

    def body(*refs):
        ins, outs, sems = refs[:n], refs[n:2 * n], refs[2 * n:]
        comm.start(ins, outs, sems)
        comm.finish(ins, outs, sems)

    return pl.pallas_call(
        body, name=name, in_specs=[HBM_SPEC] * n, out_specs=[HBM_SPEC] * n, out_shape=list(comm.out_shapes),
        input_output_aliases={t: t for t in range(n) if comm.aliased[t]},
        scratch_shapes=list(comm.sem_shapes))(*comm.arrays)


def _hosting_call(body, comm, *, name, grid, in_specs, out_specs, out_shape, scratch_shapes, args):
    sem = ("arbitrary",) * len(grid)
    if comm is None:
        res = pl.pallas_call(body, name=name, grid=grid, in_specs=in_specs, out_specs=out_specs, out_shape=out_shape,
                             scratch_shapes=scratch_shapes, compiler_params=_params(sem))(*args)
        return list(res), []
    ni, no, ns, nc = len(in_specs), len(out_specs), len(scratch_shapes), len(comm.arrays)

    def wrapped(*refs):
        ins, cins = refs[:ni], refs[ni:ni + nc]
        outs, couts = refs[ni + nc:ni + nc + no], refs[ni + nc + no:ni + 2 * nc + no]
        scr, sems = refs[ni + 2 * nc + no:ni + 2 * nc + no + ns], refs[ni + 2 * nc + no + ns:]
        first = functools.reduce(jnp.logical_and, [pl.program_id(k) == 0 for k in range(len(grid))])
        last = functools.reduce(jnp.logical_and, [pl.program_id(k) == grid[k] - 1 for k in range(len(grid))])

        @pl.when(first)
        def _():
            comm.start(cins, couts, sems)

        body(*ins, *outs, *scr)

        @pl.when(last)
        def _():
            comm.finish(cins, couts, sems)

    res = pl.pallas_call(
        wrapped, name=name, grid=grid, in_specs=list(in_specs) + [HBM_SPEC] * nc,
        out_specs=list(out_specs) + [HBM_SPEC] * nc, out_shape=list(out_shape) + list(comm.out_shapes),
        scratch_shapes=list(scratch_shapes) + list(comm.sem_shapes),
        input_output_aliases={ni + t: no + t for t in range(nc) if comm.aliased[t]},
        compiler_params=_params(sem))(*args, *comm.arrays)
    return list(res[:no]), list(res[no:])


def _chips(x, y):
    return [(1 - x, y), (x, 1 - y), (1 - x, 1 - y)]


def _gather_comm(bufs):
    n = len(bufs)

    def copies(outs, sems):
        ici_send, ici_recv, d2d_send, d2d_recv = sems
        x, y, c = lax.axis_index("x"), lax.axis_index("y"), lax.axis_index("c")

        def half(t, slot, hc):
            hr = bufs[t].shape[1] // 2
            return outs[t].at[slot, pl.ds(pl.multiple_of(hc * hr, 16), hr), :]

        def ici(t, j, slot, px, py):
            return pltpu.make_async_remote_copy(src_ref=half(t, slot, c), dst_ref=half(t, slot, c),
                                                send_sem=ici_send.at[3 * t + j], recv_sem=ici_recv.at[3 * t + j],
                                                device_id=(px, py, c), device_id_type=MESH)

        def d2d(t, j, slot, hc):
            return pltpu.make_async_remote_copy(src_ref=half(t, slot, hc), dst_ref=half(t, slot, hc),
                                                send_sem=d2d_send.at[3 * t + j], recv_sem=d2d_recv.at[3 * t + j],
                                                device_id=(x, y, 1 - c), device_id_type=MESH)

        peers = [(t, j, px, py) for t in range(n) for j, (px, py) in enumerate(_chips(x, y))]
        return ici, d2d, peers, 2 * x + y, c

    def start(ins, outs, sems):
        ici, _, peers, me, _ = copies(outs, sems)
        for t, j, px, py in peers:
            ici(t, j, me, px, py).start()

    def finish(ins, outs, sems):
        ici, d2d, peers, me, c = copies(outs, sems)
        for t, j, px, py in peers:
            ici(t, j, 2 * px + py, px, py).wait_recv()
            d2d(t, j, 2 * px + py, c).start()
        for t, j, px, py in peers:
            d2d(t, j, 2 * px + py, 1 - c).wait_recv()
        for t, j, px, py in peers:
            ici(t, j, me, px, py).wait_send()
            d2d(t, j, 2 * px + py, c).wait_send()

    return Comm(list(bufs), [SDS(b.shape, b.dtype) for b in bufs], [True] * n,
                [pltpu.SemaphoreType.DMA((3 * n,))] * 4, start, finish)


def _sibling_exchange_comm(gs):
    n = len(gs)

    def copies(ins, outs, sems):
        x, y, c = lax.axis_index("x"), lax.axis_index("y"), lax.axis_index("c")
        cps = []
        for t in range(n):
            hr = gs[t].shape[1] // 2
            src = ins[t].at[:, pl.ds(pl.multiple_of((1 - c) * hr, SUBLANES), hr), :]
            cps.append(pltpu.make_async_remote_copy(src_ref=src, dst_ref=outs[t], send_sem=sems[0].at[t],
                                                    recv_sem=sems[1].at[t], device_id=(x, y, 1 - c),
                                                    device_id_type=MESH))
        return cps

    def start(ins, outs, sems):
        for cp in copies(ins, outs, sems):
            cp.start()

    def finish(ins, outs, sems):
        for cp in copies(ins, outs, sems):
            cp.wait()

    return Comm(list(gs), [SDS((N_CHIPS, g.shape[1] // 2, g.shape[2]), g.dtype) for g in gs], [False] * n,
                [pltpu.SemaphoreType.DMA((n,))] * 2, start, finish)


def _own_rows(c, rows):
    return pl.ds(pl.multiple_of(c * (rows // 2), 16), rows // 2)


def _chip_exchange_comm(ss):
    n = len(ss)

    def copies(ins, outs, sems):
        x, y, c = lax.axis_index("x"), lax.axis_index("y"), lax.axis_index("c")
        me = 2 * x + y

        def copy(t, j, px, py, src_slot, dst_slot):
            rows = _own_rows(c, ss[t].shape[1])
            return pltpu.make_async_remote_copy(src_ref=ins[t].at[src_slot, rows, :],
                                                dst_ref=outs[t].at[dst_slot, rows, :],
                                                send_sem=sems[0].at[3 * t + j], recv_sem=sems[1].at[3 * t + j],
                                                device_id=(px, py, c), device_id_type=MESH)

        peers = [(t, j, px, py) for t in range(n) for j, (px, py) in enumerate(_chips(x, y))]
        return copy, peers, me

    def start(ins, outs, sems):
        copy, peers, me = copies(ins, outs, sems)
        for t, j, px, py in peers:
            copy(t, j, px, py, 2 * px + py, me).start()

    def finish(ins, outs, sems):
        copy, peers, me = copies(ins, outs, sems)
        for t, j, px, py in peers:
            copy(t, j, px, py, me, 2 * px + py).wait_recv()
        for t, j, px, py in peers:
            copy(t, j, px, py, 2 * px + py, me).wait_send()

    return Comm(list(ss), [SDS(s.shape, s.dtype) for s in ss], [False] * n,
                [pltpu.SemaphoreType.DMA((3 * n,))] * 2, start, finish)


def _sibling_swap_comm(rs, ss):
    n = len(rs)

    def copies(outs, sems):
        x, y, c = lax.axis_index("x"), lax.axis_index("y"), lax.axis_index("c")
        cps = []
        for t in range(n):
            rows = _own_rows(c, rs[t].shape[1])
            held = [outs[t].at[2 * px + py, rows, :] for px, py in _chips(x, y)] + [outs[n + t].at[2 * x + y, rows, :]]
            cps += [pltpu.make_async_remote_copy(src_ref=ref, dst_ref=ref, send_sem=sems[0].at[4 * t + j],
                                                 recv_sem=sems[1].at[4 * t + j], device_id=(x, y, 1 - c),
                                                 device_id_type=MESH) for j, ref in enumerate(held)]
        return cps

    def start(ins, outs, sems):
        for cp in copies(outs, sems):
            cp.start()

    def finish(ins, outs, sems):
        for cp in copies(outs, sems):
            cp.wait()

    both = list(rs) + list(ss)
    return Comm(both, [SDS(b.shape, b.dtype) for b in both], [True] * (2 * n),
                [pltpu.SemaphoreType.DMA((4 * n,))] * 2, start, finish)


def _row_tile(rows, cap=512):
    best = SUBLANES
    for tr in range(SUBLANES, min(rows, cap) + 1, SUBLANES):
        if rows % tr == 0:
            best = tr
    return best


def _add_halves(g4, r4, c_idx, name):
    _, hr, lanes = r4.shape
    tr = _row_tile(hr)
    nb = hr // tr

    def body(c_ref, a_ref, b_ref, o_ref):
        o_ref[...] = (a_ref[...] + b_ref[...]).astype(o_ref.dtype)

    pair = 2
    owned = pl.BlockSpec((pair, tr, lanes), lambda q, i, c_ref: (q, c_ref[0] * nb + i, 0))
    grid_spec = pltpu.PrefetchScalarGridSpec(
        num_scalar_prefetch=1, grid=(N_CHIPS // pair, nb),
        in_specs=[owned, pl.BlockSpec((pair, tr, lanes), lambda q, i, c_ref: (q, i, 0))], out_specs=owned)
    return pl.pallas_call(body, name=name, grid_spec=grid_spec, out_shape=SDS(g4.shape, BF16),
                          compiler_params=_params(("parallel", "parallel")))(c_idx, g4, r4)


def _adamw(wf, r4, s4, mf, vf, me_idx, name):
    rows, lanes = wf.shape
    tr = rows // 2
    assert tr % 16 == 0
    c1 = 1.0 / (1.0 - ADAM_B1 ** ADAM_STEP)
    c2 = 1.0 / (1.0 - ADAM_B2 ** ADAM_STEP)

    def body(me_ref, w_ref, a_ref, b_ref, c_ref, d_ref, own_ref, m_ref, v_ref, g_ref, delta_ref, nm_ref, nv_ref):
        own = own_ref[...].astype(F32)
        p = [jnp.where(me_ref[0] == q, own, ref[...].astype(F32)) for q, ref in enumerate((a_ref, b_ref, c_ref, d_ref))]
        gv = ((p[0] + p[1]) + p[2]) + p[3]
        m = ADAM_B1 * m_ref[...] + (1.0 - ADAM_B1) * gv
        v = ADAM_B2 * v_ref[...] + (1.0 - ADAM_B2) * (gv * gv)
        g_ref[...] = gv
        delta_ref[...] = -ADAM_LR * ((m * c1) / (jnp.sqrt(v * c2) + ADAM_EPS) + ADAM_WD * w_ref[...])
        nm_ref[...] = m
        nv_ref[...] = v

    other = lambda q: (lambda i, me_ref: (jnp.where(me_ref[0] == q, (q + 1) % N_CHIPS, q), i, 0))
    full = pl.BlockSpec((tr, lanes), lambda i, me_ref: (i, 0))
    grid_spec = pltpu.PrefetchScalarGridSpec(
        num_scalar_prefetch=1, grid=(rows // tr,),
        in_specs=[full] + [pl.BlockSpec((None, tr, lanes), other(q)) for q in range(N_CHIPS)]
        + [pl.BlockSpec((None, tr, lanes), lambda i, me_ref: (me_ref[0], i, 0)), full, full],
        out_specs=[full] * 4)
    return pl.pallas_call(body, name=name, grid_spec=grid_spec, out_shape=[SDS((rows, lanes), F32)] * 4,
                          compiler_params=_params(("parallel",)))(me_idx, wf, r4, r4, r4, r4, s4, mf, vf)


BIG = ("ffn1_w_gate", "ffn1_w_up", "ffn1_w_down", "ffn2_w_gate", "ffn2_w_up", "ffn2_w_down", "w_out", "w_in")
TRANSPOSED = ("ffn1_w_gate", "ffn1_w_up", "ffn2_w_gate", "ffn2_w_up")
PACKED = ("rwkv_w2", "rwkv_a2", "rwkv_g2")
SMALL_SHAPES = {"ffn1_norm": (1, D_MODEL), "mix_norm": (1, D_MODEL), "hgrn_lb_logits": (2, W_A),
                "hgrn_out_norm": (1, W_A), "rwkv_shift_mu": (1, N_RWKV_COLS), "rwkv_w0": (1, W_B),
                "rwkv_a0": (1, W_B), "rwkv_k_k": (1, W_B), "rwkv_k_a": (1, W_B),
                "rwkv_r_k": (1, HB_HEADS, HB_DIM), "rwkv_gn_w": (1, W_B), "rwkv_gn_b": (1, W_B),
                "ffn2_norm": (1, D_MODEL), "final_norm": (D_MODEL,)}
PACK_ELEMS = sum(_numel(_shard_shape(n)) for n in PACKED) + sum(_numel(SMALL_SHAPES[n]) for n in SMALL)
PACK_ROWS = -(-PACK_ELEMS // (32 * LANES)) * 32


def _to_rows(name, shard):
    return shard[0].T if name in TRANSPOSED else shard[0]


def _from_rows(name, rows):
    return (rows.T if name in TRANSPOSED else rows)[None]


def _pack(sharded, small):
    flat = jnp.concatenate([sharded[n].reshape(-1) for n in PACKED] + [small[n].reshape(-1) for n in SMALL])
    return jnp.pad(flat, (0, PACK_ROWS * LANES - flat.shape[0])).reshape(PACK_ROWS, LANES)


def _unpack(packed):
    flat, out, off = packed.reshape(-1), {}, 0
    for n in PACKED:
        shp = _shard_shape(n)
        out[n] = flat[off:off + _numel(shp)].reshape((1,) + shp)
        off += _numel(shp)
    for n in SMALL:
        shp = SMALL_SHAPES[n]
        out[n] = flat[off:off + _numel(shp)].reshape(shp)
        off += _numel(shp)
    return out


def _quarter(full, name, q):
    shape, ax = SHARDED_SHAPES[name]
    w = shape[ax] // N_CHIPS
    return lax.slice_in_dim(full, q * w, (q + 1) * w, axis=ax)


def kernel(x, ffn1_norm, ffn1_w_gate, ffn1_w_up, ffn1_w_down, mix_norm, w_in, hgrn_lb_logits, hgrn_out_norm, rwkv_shift_mu, rwkv_w0, rwkv_w2, rwkv_a0, rwkv_a2, rwkv_g2, rwkv_k_k, rwkv_k_a, rwkv_r_k, rwkv_gn_w, rwkv_gn_b, w_out, ffn2_norm, ffn2_w_gate, ffn2_w_up, ffn2_w_down, final_norm, loss_target, m_ffn1_norm, m_ffn1_w_gate, m_ffn1_w_up, m_ffn1_w_down, m_mix_norm, m_w_in, m_hgrn_lb_logits, m_hgrn_out_norm, m_rwkv_shift_mu, m_rwkv_w0, m_rwkv_w2, m_rwkv_a0, m_rwkv_a2, m_rwkv_g2, m_rwkv_k_k, m_rwkv_k_a, m_rwkv_r_k, m_rwkv_gn_w, m_rwkv_gn_b, m_w_out, m_ffn2_norm, m_ffn2_w_gate, m_ffn2_w_up, m_ffn2_w_down, m_final_norm, v_ffn1_norm, v_ffn1_w_gate, v_ffn1_w_up, v_ffn1_w_down, v_mix_norm, v_w_in, v_hgrn_lb_logits, v_hgrn_out_norm, v_rwkv_shift_mu, v_rwkv_w0, v_rwkv_w2, v_rwkv_a0, v_rwkv_a2, v_rwkv_g2, v_rwkv_k_k, v_rwkv_k_a, v_rwkv_r_k, v_rwkv_gn_w, v_rwkv_gn_b, v_w_out, v_ffn2_norm, v_ffn2_w_gate, v_ffn2_w_up, v_ffn2_w_down, v_final_norm):
    args = dict(locals())
    wts = {n: args[n] for n in ALL_WEIGHTS}
    moms = {n: args["m_" + n] for n in ALL_WEIGHTS}
    vars_ = {n: args["v_" + n] for n in ALL_WEIGHTS}

    me = 2 * lax.axis_index("x") + lax.axis_index("y")
    c_idx = lax.axis_index("c").astype(jnp.int32).reshape(1)
    me_idx = me.astype(jnp.int32).reshape(1)
    shard_of = {n: _to_rows(n, wts[n]).astype(BF16) for n in BIG}
    shard_of["packed"] = _pack(wts, {n: wts[n] for n in SMALL}).astype(BF16)
    shard_of["w_in_top"], shard_of["w_in_bottom"] = jnp.split(shard_of["w_in"], 2, axis=0)
    group = {"ffn1": BIG[0:3], "ffn2": BIG[3:6]}

    def slot_bufs(names):
        return [lax.dynamic_update_slice(lax.empty((N_CHIPS,) + shard_of[n].shape, BF16), shard_of[n][None],
                                         (me, 0, 0)) for n in names]

    def ffn_weights(tag, gathered):
        return {f"{tag}_wgt": gathered[0].reshape(D_FF, D_MODEL), f"{tag}_wut": gathered[1].reshape(D_FF, D_MODEL),
                f"{tag}_wd": gathered[2].reshape(D_FF, D_MODEL)}

    def w_in_weights(top, bottom):
        w_in_full = jnp.concatenate([jnp.concatenate([top[q] for q in range(N_CHIPS)], axis=1),
                                     jnp.concatenate([bottom[q] for q in range(N_CHIPS)], axis=1)], axis=0)
        return {"w_in_h": w_in_full[:, :N_HGRN_COLS],
                "w_in_r": jnp.pad(w_in_full[:, N_HGRN_COLS:], ((0, 0), (0, N_RWKV_PAD - N_RWKV_COLS)))}

    def mixer_weights(gathered):
        w_out_full = gathered[0].reshape(D_MODEL, D_MODEL)
        packs = gathered[1].reshape(N_CHIPS, PACK_ROWS * LANES)
        full, off = {}, 0
        for n in PACKED:
            shp = _shard_shape(n)
            full[n] = jnp.concatenate([packs[q, off:off + _numel(shp)].reshape(shp) for q in range(N_CHIPS)], axis=1)
            off += _numel(shp)
        zrow = lambda nrow: jnp.zeros((nrow, W_B), BF16)
        return {"w_out": w_out_full,
                "w2_pad": jnp.concatenate([full["rwkv_w2"], zrow(LORA_PAD - 32)], axis=0),
                "a2_pad": jnp.concatenate([zrow(32), full["rwkv_a2"], zrow(LORA_PAD - 64)], axis=0),
                "g2_pad": jnp.concatenate([zrow(64), full["rwkv_g2"], zrow(LORA_PAD - 160)], axis=0)}

    plan = _Plan()
    w = {}
    plan.carry("ffn1_rms", lambda g: _gather_comm(slot_bufs(group["ffn1"][:2])),
               lambda res, w_: w_.update({"ffn1_wgt": res[0].reshape(D_FF, D_MODEL),
                                          "ffn1_wut": res[1].reshape(D_FF, D_MODEL)}))

    def after_gate_up(res, w_):
        w_["ffn1_wd"], w_["w_in_top"] = res[0].reshape(D_FF, D_MODEL), res[1]

    def after_down(res, w_):
        w_.update(mixer_weights(res[:2]))
        w_.update(w_in_weights(w_.pop("w_in_top"), res[2]))

    plan.carry("ffn1_gate_up", lambda g: _gather_comm(slot_bufs(("ffn1_w_down", "w_in_top"))), after_gate_up)
    plan.carry("ffn1_down", lambda g: _gather_comm(slot_bufs(("w_out", "packed", "w_in_bottom"))), after_down)
    plan.carry("rwkv_fwd", lambda g: _gather_comm(slot_bufs(group["ffn2"])),
               lambda res, w_: w_.update(ffn_weights("ffn2", res)))
    w["ffn1_norm"], w["ffn2_norm"] = ffn1_norm, ffn2_norm
    w["mix_norm"] = mix_norm
    w["lb0"], w["lb1"] = hgrn_lb_logits[0:1], hgrn_lb_logits[1:2]
    w["hgrn_out_norm"] = hgrn_out_norm
    w["mu_pad"] = jnp.pad(rwkv_shift_mu, ((0, 0), (0, N_RWKV_PAD - N_RWKV_COLS)))
    for n in ("rwkv_w0", "rwkv_a0", "rwkv_k_k", "rwkv_k_a", "rwkv_gn_w", "rwkv_gn_b"):
        w[n] = wts[n]
    w["rwkv_r_k"] = rwkv_r_k.reshape(1, W_B)
    w["final_norm"] = final_norm.reshape(1, D_MODEL)

    def reduce_rows(names, gs):
        r1 = _run_comm(_sibling_exchange_comm(gs), "grad_sibling_exchange")
        s4 = [_add_halves(gt, rt, c_idx, f"grad_add_halves_{n}") for gt, rt, n in zip(gs, r1, names)]
        return list(zip(_run_comm(_chip_exchange_comm(s4), "grad_chip_exchange"), s4))

    def swap_comm(names):
        return _sibling_swap_comm([early[n][0] for n in names], [early[n][1] for n in names])

    def after_swap(names):
        return lambda res, w_: swapped.update(zip(names, zip(res[:len(names)], res[len(names):])))

    early, swapped = {}, {}

    def reduce_early(names, grads_of, sibling_host, chips_host, swap_host):
        def sibling_comm(g):
            early[names, "gs"] = grads_of(g)
            return _sibling_exchange_comm(early[names, "gs"])

        def after_sibling(res, w_):
            early[names, "s4"] = [_add_halves(gt, rt, c_idx, f"grad_add_halves_{n}")
                                  for gt, rt, n in zip(early[names, "gs"], res, names)]

        plan.carry(sibling_host, sibling_comm, after_sibling)
        plan.carry(chips_host, lambda g: _chip_exchange_comm(early[names, "s4"]),
                   lambda res, w_: early.update(zip(names, zip(res, early[names, "s4"]))))
        if swap_host:
            plan.carry(swap_host, lambda g: swap_comm(names), after_swap(names))

    def proj_grads(g):
        g_w_in = jnp.concatenate([g["w_in_h"], g["w_in_r"][:, :N_RWKV_COLS]], axis=1)
        return [g["w_out"].reshape(N_CHIPS, -1, D_MODEL),
                jnp.stack([_quarter(g_w_in, "w_in", q) for q in range(N_CHIPS)])]

    rows_of = lambda keys: (lambda g: [g[k].reshape(N_CHIPS, -1, D_MODEL) for k in keys])
    reduce_early(group["ffn2"], rows_of(("ffn2_wgt", "ffn2_wut", "ffn2_wd")), "hgrn_bwd", "rwkv_bwd", "rwkv_prep_bwd")
    reduce_early(("w_out", "w_in"), proj_grads, "mix_drms", "ffn1_dact", "ffn1_dwg")
    reduce_early(("ffn1_w_down",), rows_of(("ffn1_wd",)), "ffn1_dwg", "ffn1_dwu", "ffn1_dh_g")
    reduce_early(("ffn1_w_gate",), rows_of(("ffn1_wgt",)), "ffn1_dwu", "ffn1_dh_g", "ffn1_dh_u")
    reduce_early(("ffn1_w_up",), rows_of(("ffn1_wut",)), "ffn1_dh_g", "ffn1_dh_u", None)
    loss_slab, grad_x, g = _local_step(x[0], loss_target[0], w, plan)
    loss = lax.psum(loss_slab[0, 0], ("x", "y", "c"))

    gfull = {
        "rwkv_w2": g["w2_pad"][0:32], "rwkv_a2": g["a2_pad"][32:64], "rwkv_g2": g["g2_pad"][64:160],
    }
    gsmall = {
        "ffn1_norm": g["ffn1_norm"], "mix_norm": g["mix_norm"],
        "hgrn_lb_logits": jnp.concatenate([g["lb0"], g["lb1"]], axis=0), "hgrn_out_norm": g["hgrn_out_norm"],
        "rwkv_shift_mu": g["mu_pad"][:, :N_RWKV_COLS], "rwkv_w0": g["rwkv_w0"], "rwkv_a0": g["rwkv_a0"],
        "rwkv_k_k": g["rwkv_k_k"], "rwkv_k_a": g["rwkv_k_a"], "rwkv_r_k": g["rwkv_r_k"],
        "rwkv_gn_w": g["rwkv_gn_w"], "rwkv_gn_b": g["rwkv_gn_b"], "ffn2_norm": g["ffn2_norm"],
        "final_norm": g["final_norm"],
    }
    packed = jnp.stack([_pack({n: _quarter(gfull[n], n, q) for n in PACKED}, gsmall) for q in range(N_CHIPS)])
    early["packed"], = reduce_rows(["packed"], [packed])
    last = ["ffn1_w_up", "packed"]
    after_swap(last)(_run_comm(swap_comm(last), "grad_sibling_swap"), w)
    names = list(BIG) + ["packed"]

    def rows_list(d):
        return [_to_rows(n, d[n]) for n in BIG] + [_pack(d, {n: d[n] for n in SMALL})]

    outs = [_adamw(wt, *swapped[n], mt, vt, me_idx, f"adamw_{n}")
            for wt, mt, vt, n in zip(rows_list(wts), rows_list(moms), rows_list(vars_), names)]
    results = []
    for k in range(4):
        per = [outs[i][k] for i in range(len(names))]
        d = {n: _from_rows(n, z) for n, z in zip(BIG, per[:-1])}
        d.update(_unpack(per[-1]))
        results.append(d)
    return (loss, grad_x[None], *[r[n] for r in results for n in ALL_WEIGHTS])
```

```python
import collections
import functools

import jax
import jax.numpy as jnp
from jax import lax
from jax.experimental import pallas as pl
from jax.experimental.pallas import tpu as pltpu

F32 = jnp.float32
BF16 = jnp.bfloat16
SDS = jax.ShapeDtypeStruct
MESH = pl.DeviceIdType.MESH

D_MODEL = 1024
D_FF = 2816
W_A = 512
W_B = 512
HA_HEADS, HA_DIM = 4, 128
HB_HEADS, HB_DIM = 8, 64
HGRN_CHUNK = 64
HGRN_GROUP = 8
RWKV_CHUNK = 16
RWKV_GROUP = 8
N_HGRN_COLS = 4 * W_A
N_RWKV_COLS = 3 * W_B + 32 + 32 + 96
N_RWKV_PAD = 1792
LORA_PAD = 256
NORM_EPS = 1e-6
RWKV_GN_EPS = 64e-5
L2_EPS = 1e-12
ADAM_LR, ADAM_B1, ADAM_B2, ADAM_EPS, ADAM_WD, ADAM_STEP = 0.001, 0.9, 0.999, 1e-8, 0.01, 10

N_CHIPS = 4
VMEM_LIMIT_V7X = 56 * 1024 * 1024
LANES = 1024

SHARDED_SHAPES = {
    "ffn1_w_gate": ((D_MODEL, D_FF), 1), "ffn1_w_up": ((D_MODEL, D_FF), 1), "ffn1_w_down": ((D_FF, D_MODEL), 0),
    "w_in": ((D_MODEL, N_HGRN_COLS + N_RWKV_COLS), 1), "rwkv_w2": ((32, W_B), 1), "rwkv_a2": ((32, W_B), 1),
    "rwkv_g2": ((96, W_B), 1), "w_out": ((D_MODEL, D_MODEL), 0),
    "ffn2_w_gate": ((D_MODEL, D_FF), 1), "ffn2_w_up": ((D_MODEL, D_FF), 1), "ffn2_w_down": ((D_FF, D_MODEL), 0),
}
SMALL = ("ffn1_norm", "mix_norm", "hgrn_lb_logits", "hgrn_out_norm", "rwkv_shift_mu", "rwkv_w0", "rwkv_a0",
         "rwkv_k_k", "rwkv_k_a", "rwkv_r_k", "rwkv_gn_w", "rwkv_gn_b", "ffn2_norm", "final_norm")
ALL_WEIGHTS = ("ffn1_norm", "ffn1_w_gate", "ffn1_w_up", "ffn1_w_down", "mix_norm", "w_in", "hgrn_lb_logits",
               "hgrn_out_norm", "rwkv_shift_mu", "rwkv_w0", "rwkv_w2", "rwkv_a0", "rwkv_a2", "rwkv_g2", "rwkv_k_k",
               "rwkv_k_a", "rwkv_r_k", "rwkv_gn_w", "rwkv_gn_b", "w_out", "ffn2_norm", "ffn2_w_gate", "ffn2_w_up",
               "ffn2_w_down", "final_norm")


def _shard_shape(name):
    shape, ax = SHARDED_SHAPES[name]
    return tuple(s // N_CHIPS if i == ax else s for i, s in enumerate(shape))


def _numel(shape):
    n = 1
    for s in shape:
        n *= s
    return n


def _params(sem=None):
    return pltpu.CompilerParams(dimension_semantics=sem, vmem_limit_bytes=VMEM_LIMIT_V7X)


def _split2(x):
    hi = x.astype(BF16)
    return hi, (x.astype(F32) - hi.astype(F32)).astype(BF16)


def _dg(x, y, cx, cy, hi):
    dn = (((cx,), (cy,)), ((), ()))
    dot = lambda p, q: lax.dot_general(p, q, dn, preferred_element_type=F32)
    if hi == "x3":
        (xh, xl), (yh, yl) = _split2(x), _split2(y)
        return dot(xh, yh) + (dot(xh, yl) + dot(xl, yh))
    return dot(x.astype(BF16), y.astype(BF16))


def _make_mm(hi, cotangent_forms=None):
    @jax.custom_vjp
    def nn(x, y):
        return _dg(x, y, 1, 0, hi)

    @jax.custom_vjp
    def nt(x, y):
        return _dg(x, y, 1, 1, hi)

    @jax.custom_vjp
    def tn(x, y):
        return _dg(x, y, 0, 0, hi)

    bnn, bnt, btn = cotangent_forms or (nn, nt, tn)
    nn.defvjp(lambda x, y: (nn(x, y), (x, y)), lambda r, g: (bnt(g, r[1]), btn(r[0], g)))
    nt.defvjp(lambda x, y: (nt(x, y), (x, y)), lambda r, g: (bnn(g, r[1]), btn(g, r[0])))
    tn.defvjp(lambda x, y: (tn(x, y), (x, y)), lambda r, g: (bnt(r[1], g), bnn(r[0], g)))
    return nn, nt, tn


_nn, _nt, _tn = _make_mm(False)
_nn_x3, _nt_x3, _tn_x3 = _make_mm("x3", (_nn, _nt, _tn))


def _tri_apply(x, transpose):
    c = x.shape[0]
    tri = (lax.broadcasted_iota(jnp.int32, (c, c), 1) <= lax.broadcasted_iota(jnp.int32, (c, c), 0)).astype(BF16)
    dn = (((0 if transpose else 1,), (0,)), ((), ()))
    p1, p2 = _split2(x)
    dot = lambda p: lax.dot_general(tri, p, dn, preferred_element_type=F32)
    return dot(p1) + dot(p2)


@jax.custom_vjp
def _cumsum_rows(x):
    return _tri_apply(x, False)


_cumsum_rows.defvjp(lambda x: (_tri_apply(x, False), None), lambda _, g: (_tri_apply(g, True),))


def _sigmoid(x):
    return 1.0 / (1.0 + jnp.exp(-x))


def _silu(x):
    return x * _sigmoid(x)


def _softplus(z):
    return jnp.maximum(z, 0.0) + jnp.log(1.0 + jnp.exp(-jnp.abs(z)))


def _mm(a, b, *, ta=False, tb=False, tm, tn, tk, name, out_dtype=F32, res=None, scale=None, comm=None):
    m = a.shape[1] if ta else a.shape[0]
    kdim = a.shape[0] if ta else a.shape[1]
    n = b.shape[0] if tb else b.shape[1]
    assert (b.shape[1] if tb else b.shape[0]) == kdim
    tm, tn, tk = min(tm, m), min(tn, n), min(tk, kdim)
    assert m % tm == 0 and n % tn == 0 and kdim % tk == 0, (name, m, n, kdim)
    nk = kdim // tk
    a_spec = pl.BlockSpec((tk, tm), lambda i, j, k: (k, i)) if ta else pl.BlockSpec((tm, tk), lambda i, j, k: (i, k))
    b_spec = pl.BlockSpec((tn, tk), lambda i, j, k: (j, k)) if tb else pl.BlockSpec((tk, tn), lambda i, j, k: (k, j))
    o_spec = pl.BlockSpec((tm, tn), lambda i, j, k: (i, j))
    ca, cb = (0 if ta else 1), (1 if tb else 0)

    def body(*refs):
        if res is not None:
            a_ref, b_ref, r_ref, o_ref, acc_ref = refs
        else:
            a_ref, b_ref, o_ref, acc_ref = refs
        k = pl.program_id(2)

        @pl.when(k == 0)
        def _():
            acc_ref[...] = jnp.zeros_like(acc_ref)

        acc_ref[...] += _dg(a_ref[...], b_ref[...], ca, cb, False)

        @pl.when(k == nk - 1)
        def _():
            acc = acc_ref[...]
            if scale is not None:
                acc = acc * scale
            if res is not None:
                acc = r_ref[...] + acc
            o_ref[...] = acc.astype(out_dtype)

    in_specs = [a_spec, b_spec] + ([o_spec] if res is not None else [])
    args = (a, b) + ((res,) if res is not None else ())
    if comm is None:
        return pl.pallas_call(
            body, name=name, grid=(m // tm, n // tn, nk), in_specs=in_specs, out_specs=o_spec,
            out_shape=SDS((m, n), out_dtype), scratch_shapes=[pltpu.VMEM((tm, tn), F32)],
            compiler_params=_params(("parallel", "parallel", "arbitrary")))(*args)
    (out,), carried = _hosting_call(
        body, comm, name=name, grid=(m // tm, n // tn, nk), in_specs=in_specs, out_specs=[o_spec],
        out_shape=[SDS((m, n), out_dtype)], scratch_shapes=[pltpu.VMEM((tm, tn), F32)], args=args)
    return out, carried


def _mm_pair(a1, b1, a2, b2, *, tb=False, tm, name):
    m = a1.shape[0]
    n = b1.shape[0] if tb else b1.shape[1]
    tm = min(tm, m)
    cb = 1 if tb else 0
    assert a2.shape[0] == m and m % tm == 0
    assert all(b.shape[cb] == a.shape[1] and b.shape[1 - cb] == n for a, b in ((a1, b1), (a2, b2)))

    def body(a1_ref, b1_ref, a2_ref, b2_ref, o_ref):
        o_ref[...] = _dg(a1_ref[...], b1_ref[...], 1, cb, False) + _dg(a2_ref[...], b2_ref[...], 1, cb, False)

    a_spec = lambda a: pl.BlockSpec((tm, a.shape[1]), lambda i: (i, 0))
    b_spec = lambda b: pl.BlockSpec(b.shape, lambda i: (0, 0))
    return pl.pallas_call(
        body, name=name, grid=(m // tm,), in_specs=[a_spec(a1), b_spec(b1), a_spec(a2), b_spec(b2)],
        out_specs=pl.BlockSpec((tm, n), lambda i: (i, 0)), out_shape=SDS((m, n), F32),
        compiler_params=_params(("parallel",)))(a1, b1, a2, b2)


def _row_spec(x, tm, tile_of=lambda i: i):
    if isinstance(x, tuple):
        arr, w, j = x
        return arr, pl.BlockSpec((tm, w), lambda i, j=j: (tile_of(i), j))
    return x, pl.BlockSpec((tm, x.shape[1]), lambda i: (tile_of(i), 0))


def _par_spec(p):
    if isinstance(p, tuple):
        arr, w, j = p
        return arr, pl.BlockSpec((arr.shape[0], w), lambda i, j=j: (0, j))
    return p, pl.BlockSpec(p.shape, lambda i: (0, 0))


def _store_groups(refs, groups, vals):
    for ref, idxs in zip(refs, groups):
        off = 0
        for ix in idxs:
            v = vals[ix]
            ref[:, off:off + v.shape[1]] = v.astype(ref.dtype)
            off += v.shape[1]


SUBLANES = 8


def _x_plan(xs, tm, t, tile_of=lambda i: i):
    arrays, specs, plan = [], [], []
    nb = tm // SUBLANES
    for x in xs:
        if isinstance(x, tuple) and isinstance(x[0], str):
            kind, arr, w, j = x
            if kind == "prev":
                halo = lambda i, j=j: (jnp.maximum(tile_of(i) * nb - 1, 0), j)
            else:
                halo = lambda i, j=j: (jnp.minimum((tile_of(i) + 1) * nb, t // SUBLANES - 1), j)
            arrays += [arr, arr]
            specs += [pl.BlockSpec((tm, w), lambda i, j=j: (tile_of(i), j)), pl.BlockSpec((SUBLANES, w), halo)]
            plan.append((kind, 2, w))
        else:
            arr, spec = _row_spec(x, tm, tile_of)
            arrays.append(arr)
            specs.append(spec)
            plan.append(("plain", 1, spec.block_shape[1]))
    return arrays, specs, plan


def _x_vals(refs, plan, tm, nt, tile_of=lambda i: i):
    vals, k = [], 0
    i = tile_of(pl.program_id(0))
    rows = lax.broadcasted_iota(jnp.int32, (tm, 1), 0)
    for kind, n, _ in plan:
        main = refs[k][...].astype(F32)
        if kind == "prev":
            edge = jnp.where(i == 0, 0.0, refs[k + 1][SUBLANES - 1:SUBLANES, :].astype(F32))
            main = jnp.where(rows == 0, edge, pltpu.roll(main, 1, 0))
        elif kind == "next":
            edge = jnp.where(i == nt - 1, 0.0, refs[k + 1][0:1, :].astype(F32))
            main = jnp.where(rows == tm - 1, edge, pltpu.roll(main, tm - 1, 0))
        vals.append(main)
        k += n
    return vals


def _tile_rows(xs, tm):
    arr = xs[0]
    if isinstance(arr, tuple):
        arr = arr[1] if isinstance(arr[0], str) else arr[0]
    return min(tm, arr.shape[0]), arr.shape[0]


def _rowwise(f, xs, params, out_groups, out_dtypes, *, tm, name, comm=None):
    tm, t = _tile_rows(xs, tm)
    nt = t // tm
    xa, xspecs, plan = _x_plan(xs, tm, t)
    pa, pspecs = (zip(*[_par_spec(p) for p in params]) if params else ((), ()))
    nxr, npar = len(xa), len(pa)
    x_sds = [SDS((tm, w), F32) for _, _, w in plan]
    p_sds = [SDS(s.block_shape, F32) for s in pspecs]
    outs_sds = jax.eval_shape(lambda *vals: f(*vals), *x_sds, *p_sds)
    widths = [sum(outs_sds[ix].shape[1] for ix in idxs) for idxs in out_groups]

    def body(*refs):
        vals = _x_vals(refs[:nxr], plan, tm, nt) + [r[...].astype(F32) for r in refs[nxr:nxr + npar]]
        outs = f(*vals)
        _store_groups(refs[nxr + npar:], out_groups, outs)

    res, carried = _hosting_call(
        body, comm, name=name, grid=(nt,), in_specs=list(xspecs) + list(pspecs),
        out_specs=[pl.BlockSpec((tm, w), lambda i: (i, 0)) for w in widths],
        out_shape=[SDS((t, w), dt) for w, dt in zip(widths, out_dtypes)], scratch_shapes=[], args=(*xa, *pa))
    return res if comm is None else (res, carried)


def _rowwise_bwd(f, xs, params, cots, *, x_grad, p_grad, dx_groups, dx_dtypes, tm, name, extra=None, comm=None,
                 fold_next=None):
    tm, t = _tile_rows(xs, tm)
    nt = t // tm
    tile_of = (lambda i: nt - 1 - i) if fold_next else (lambda i: i)
    xa, xspecs, plan = _x_plan(xs, tm, t, tile_of)
    pa, pspecs = (zip(*[_par_spec(p) for p in params]) if params else ((), ()))
    ca, cspecs = zip(*[_row_spec(c, tm, tile_of) for c in cots])
    extra = extra or {}
    ekeys = sorted(extra)
    ea, especs = (zip(*[_row_spec(extra[k], tm, tile_of) for k in ekeys]) if ekeys else ((), ()))
    nx, nxr, npar, nc, ne = len(plan), len(xa), len(pa), len(ca), len(ea)
    gx = [i for i in range(nx) if x_grad[i]]
    gp = [i for i in range(npar) if p_grad[i]]
    all_widths = [sum(plan[gx[ix]][2] for ix in idxs) for idxs in dx_groups]
    emitted = [k for k in range(len(dx_groups)) if not (fold_next and k == fold_next[1])]
    widths = [all_widths[k] for k in emitted]
    ng = len(emitted)

    def body(*refs):
        ins = refs[:nxr + npar + nc + ne]
        outs = refs[nxr + npar + nc + ne:]
        vals = _x_vals(ins[:nxr], plan, tm, nt, tile_of) + [r[...].astype(F32) for r in ins[nxr:nxr + npar]]
        cvals = tuple(r[...].astype(F32) for r in ins[nxr + npar:nxr + npar + nc])
        evals = [r[...].astype(F32) for r in ins[nxr + npar + nc:]]
        diff_idx = gx + [nx + i for i in gp]

        def g(*dargs):
            full = list(vals)
            for ix, v in zip(diff_idx, dargs):
                full[ix] = v
            return tuple(f(*full))

        _, vjp = jax.vjp(g, *[vals[ix] for ix in diff_idx])
        grads = vjp(cvals)
        dxs = list(grads[:len(gx)])
        for k, ev in zip(ekeys, evals):
            dxs[k] = dxs[k] + ev
        _store_groups(outs[:ng], [dx_groups[k] for k in emitted], dxs)
        i = pl.program_id(0)
        if fold_next:
            main_ref, carry_ref = outs[emitted.index(fold_next[0])], refs[-1]
            rows = lax.broadcasted_iota(jnp.int32, (tm, 1), 0)
            off = 0
            for ix in dx_groups[fold_next[1]]:
                piece = dxs[ix]
                cols = slice(off, off + piece.shape[1])
                edge = jnp.where(i == 0, 0.0, carry_ref[0:1, cols])
                main_ref[:, cols] += jnp.where(rows == tm - 1, edge, pltpu.roll(piece, tm - 1, 0))
                carry_ref[:, cols] = piece[:SUBLANES]
                off += piece.shape[1]
        for ref, gval in zip(outs[ng:ng + len(gp)], grads[len(gx):]):
            @pl.when(i == 0)
            def _(ref=ref):
                ref[...] = jnp.zeros_like(ref)
            ref[...] += gval

    dp_specs = [pl.BlockSpec(pspecs[i].block_shape, lambda i: (0, 0)) for i in gp]
    dp_shapes = [SDS(pspecs[i].block_shape, F32) for i in gp]
    scratch = [pltpu.VMEM((SUBLANES, all_widths[fold_next[1]]), F32)] if fold_next else []
    res, carried = _hosting_call(
        body, comm, name=name, grid=(nt,), in_specs=list(xspecs) + list(pspecs) + list(cspecs) + list(especs),
        out_specs=[pl.BlockSpec((tm, w), lambda i: (tile_of(i), 0)) for w in widths] + dp_specs,
        out_shape=[SDS((t, w), dt) for w, dt in zip(widths, dx_dtypes)] + dp_shapes, scratch_shapes=scratch,
        args=(*xa, *pa, *ca, *ea))
    return res if comm is None else (res, carried)


def _rms_f(x, g):
    return (x * lax.rsqrt(jnp.mean(x * x, axis=-1, keepdims=True) + NORM_EPS) * g,)


def _group_sum_impl(x, ones_bd):
    p1, p2 = _split2(x)
    dot = lambda p: lax.dot_general(p, ones_bd.astype(BF16), (((1,), (0,)), ((), ())), preferred_element_type=F32)
    return dot(p1) + dot(p2)


@jax.custom_vjp
def _group_sum(x, ones_bd):
    return _group_sum_impl(x, ones_bd)


_group_sum.defvjp(lambda x, o: (_group_sum_impl(x, o), o),
                  lambda o, g: (_group_sum_impl(g, o), jnp.zeros_like(o)))


def _rwkv_prep_f(r, k, v, lo, rp, kp, vp, lop, mu_r, mu_k, mu_v, mu_lo, w0, w2p, a0, a2p, g2p, k_k, k_a, ones_bd):
    r = r + mu_r * (rp - r)
    k = k + mu_k * (kp - k)
    v = v + mu_v * (vp - v)
    lo = lo + mu_lo * (lop - lo)
    w_log = -_softplus(-(w0 + _nn(jnp.tanh(lo), w2p))) - 0.5
    lw = -jnp.exp(w_log)
    a_g = _sigmoid(a0 + _nn(lo, a2p))
    g = _nn(_sigmoid(lo), g2p)
    kk = k * k_k
    kk = kk / jnp.maximum(jnp.sqrt(_group_sum(kk * kk, ones_bd)), L2_EPS)
    k2 = k * (1.0 + (a_g - 1.0) * k_a)
    return r, lw, k2, v, -kk, kk * a_g, g


def _rwkv_post_f(y, r, k2, v, g, r_k, gn_w, gn_b, ones_bd):
    inv_n = 1.0 / HB_DIM
    mean = _group_sum(y, ones_bd) * inv_n
    yc = y - mean
    var = _group_sum(yc * yc, ones_bd) * inv_n
    yn = yc * lax.rsqrt(var + RWKV_GN_EPS) * gn_w + gn_b
    bonus = _group_sum(r * k2 * r_k, ones_bd) * v
    return ((yn + bonus) * g,)


def _tri(c, strict=False):
    ii = lax.broadcasted_iota(jnp.int32, (c, c), 0)
    jj = lax.broadcasted_iota(jnp.int32, (c, c), 1)
    return (jj < ii) if strict else (jj <= ii)


def _hgrn_step(st0, q_a, f_a, i_a, g_a, l0, l1, onorm):
    nh, nj = len(q_a), len(q_a[0])
    c = q_a[0][0].shape[0]
    combos = [(j, h) for j in range(nj) for h in range(nh)]
    every = lambda fn: {q: fn(q) for q in combos}
    at_ = lambda d: (lambda q: d[q[1]][q[0]])
    qa_, fa_, ia_, ga_ = (at_(z) for z in (q_a, f_a, i_a, g_a))
    incl = _tri(c)
    rows = lax.broadcasted_iota(jnp.int32, (c, 1), 0)
    lb = []
    for h in range(nh):
        mx = jnp.maximum(l0[h], l1[h])
        e0, e1 = jnp.exp(l0[h] - mx), jnp.exp(l1[h] - mx)
        lb.append(e0 / (e0 + e1))
    forget = every(lambda q: lb[q[1]] + (1.0 - lb[q[1]]) * _sigmoid(fa_(q)))
    qs = every(lambda q: _silu(qa_(q)))
    kk = every(lambda q: 1.0 - forget[q])
    lf = every(lambda q: jnp.log(forget[q]))
    bcum = every(lambda q: _cumsum_rows(lf[q]))
    bref = every(lambda q: jnp.sum(jnp.where(rows <= c // 2, lf[q], 0.0), axis=0, keepdims=True))
    blast = every(lambda q: jnp.sum(lf[q], axis=0, keepdims=True))
    scores = every(lambda q: jnp.where(incl, _nt(qs[q] * jnp.exp(bcum[q] - bref[q]),
                                                 kk[q] * jnp.exp(bref[q] - bcum[q])), 0.0))
    intra = every(lambda q: _nn(scores[q], ia_(q)))
    qb = every(lambda q: qs[q] * jnp.exp(bcum[q]))
    upd = every(lambda q: _tn(ia_(q), kk[q] * jnp.exp(blast[q] - bcum[q])))
    dec = every(lambda q: jnp.exp(blast[q]))
    st = list(st0)
    o = {}
    for j in range(nj):
        for h in range(nh):
            o[(j, h)] = intra[(j, h)] + _nt(qb[(j, h)], st[h])
        st = [st[h] * dec[(j, h)] + upd[(j, h)] for h in range(nh)]
    out = every(lambda q: o[q] * lax.rsqrt(jnp.mean(o[q] * o[q], axis=-1, keepdims=True) + NORM_EPS)
                * onorm[q[1]] * _silu(ga_(q)))
    return [[out[(j, h)] for j in range(nj)] for h in range(nh)], st


def _hgrn_blocks(ref, nj, c):
    return [[ref[j * c:(j + 1) * c, h * HA_DIM:(h + 1) * HA_DIM] for j in range(nj)] for h in range(HA_HEADS)]


def _hgrn_cols(ref):
    return [ref[:, h * HA_DIM:(h + 1) * HA_DIM] for h in range(HA_HEADS)]


def _hgrn_fwd(p_h, l0, l1, onorm):
    t = p_h.shape[0]
    cc, nj = HGRN_CHUNK, HGRN_GROUP
    c = cc * nj
    n = t // c

    def body(q_ref, f_ref, i_ref, g_ref, l0_ref, l1_ref, on_ref, o_ref, hs_ref, st_ref):
        @pl.when(pl.program_id(0) == 0)
        def _():
            st_ref[...] = jnp.zeros_like(st_ref)

        hs_ref[0] = st_ref[...]
        o, st1 = _hgrn_step([st_ref[h] for h in range(HA_HEADS)],
                            *[_hgrn_blocks(ref, nj, cc) for ref in (q_ref, f_ref, i_ref, g_ref)],
                            _hgrn_cols(l0_ref), _hgrn_cols(l1_ref), _hgrn_cols(on_ref))
        for h in range(HA_HEADS):
            for j in range(nj):
                o_ref[j * cc:(j + 1) * cc, h * HA_DIM:(h + 1) * HA_DIM] = o[h][j]
            st_ref[h] = st1[h]

    col = lambda j: pl.BlockSpec((c, W_A), lambda i, j=j: (i, j))
    par = pl.BlockSpec((1, W_A), lambda i: (0, 0))
    return pl.pallas_call(
        body, name="hgrn_fwd", grid=(n,), in_specs=[col(0), col(1), col(2), col(3), par, par, par],
        out_specs=[pl.BlockSpec((c, W_A), lambda i: (i, 0)),
                   pl.BlockSpec((1, HA_HEADS, HA_DIM, HA_DIM), lambda i: (i, 0, 0, 0))],
        out_shape=[SDS((t, W_A), F32), SDS((n, HA_HEADS, HA_DIM, HA_DIM), F32)],
        scratch_shapes=[pltpu.VMEM((HA_HEADS, HA_DIM, HA_DIM), F32)],
        compiler_params=_params(("arbitrary",)))(p_h, p_h, p_h, p_h, l0, l1, onorm)


def _hgrn_bwd(p_h, l0, l1, onorm, hs, do, do_col, comm=None):
    t = p_h.shape[0]
    cc, nj = HGRN_CHUNK, HGRN_GROUP
    c = cc * nj
    n = t // c

    def body(q_ref, f_ref, i_ref, g_ref, l0_ref, l1_ref, on_ref, hs_ref, do_ref,
             dp_ref, dl0_ref, dl1_ref, don_ref, dst_ref):
        @pl.when(pl.program_id(0) == 0)
        def _():
            dst_ref[...] = jnp.zeros_like(dst_ref)
            dl0_ref[...] = jnp.zeros_like(dl0_ref)
            dl1_ref[...] = jnp.zeros_like(dl1_ref)
            don_ref[...] = jnp.zeros_like(don_ref)

        args = ([hs_ref[0, h] for h in range(HA_HEADS)],
                *[_hgrn_blocks(ref, nj, cc) for ref in (q_ref, f_ref, i_ref, g_ref)],
                _hgrn_cols(l0_ref), _hgrn_cols(l1_ref), _hgrn_cols(on_ref))
        _, vjp = jax.vjp(_hgrn_step, *args)
        dst0, dq, df, di, dg, dl0, dl1, don = vjp((_hgrn_blocks(do_ref, nj, cc),
                                                   [dst_ref[h] for h in range(HA_HEADS)]))
        for h in range(HA_HEADS):
            sl = slice(h * HA_DIM, (h + 1) * HA_DIM)
            for k, dv in enumerate((dq, df, di, dg)):
                for j in range(nj):
                    dp_ref[j * cc:(j + 1) * cc, k * W_A + h * HA_DIM:k * W_A + (h + 1) * HA_DIM] = dv[h][j]
            dl0_ref[:, sl] += dl0[h]
            dl1_ref[:, sl] += dl1[h]
            don_ref[:, sl] += don[h]
            dst_ref[h] = dst0[h]

    col = lambda j: pl.BlockSpec((c, W_A), lambda i, j=j: (n - 1 - i, j))
    par = pl.BlockSpec((1, W_A), lambda i: (0, 0))
    return _hosting_call(
        body, comm, name="hgrn_bwd", grid=(n,),
        in_specs=[col(0), col(1), col(2), col(3), par, par, par,
                  pl.BlockSpec((1, HA_HEADS, HA_DIM, HA_DIM), lambda i: (n - 1 - i, 0, 0, 0)),
                  pl.BlockSpec((c, W_A), lambda i: (n - 1 - i, do_col))],
        out_specs=[pl.BlockSpec((c, N_HGRN_COLS), lambda i: (n - 1 - i, 0)), par, par, par],
        out_shape=[SDS((t, N_HGRN_COLS), F32), SDS((1, W_A), F32), SDS((1, W_A), F32), SDS((1, W_A), F32)],
        scratch_shapes=[pltpu.VMEM((HA_HEADS, HA_DIM, HA_DIM), F32)],
        args=(p_h, p_h, p_h, p_h, l0, l1, onorm, hs, do))


HB_PAIRS = HB_HEADS // 2
PAIR_W = 2 * HB_DIM


def _head_lane_masks():
    lane = lax.broadcasted_iota(jnp.int32, (1, PAIR_W), 1)
    return (lane < HB_DIM).astype(F32), (lane >= HB_DIM).astype(F32)


@jax.custom_vjp
def _stack_heads(x):
    m0, m1 = _head_lane_masks()
    return jnp.concatenate([x * m0, x * m1], axis=0)


def _stack_heads_bwd(_, g):
    m0, m1 = _head_lane_masks()
    c = g.shape[0] // 2
    return (g[:c] * m0 + g[c:] * m1,)


_stack_heads.defvjp(lambda x: (_stack_heads(x), None), _stack_heads_bwd)


@jax.custom_vjp
def _unstack_heads(ys):
    c = ys.shape[0] // 2
    return ys[:c] + ys[c:]


_unstack_heads.defvjp(lambda ys: (_unstack_heads(ys), None), lambda _, g: (_stack_heads(g),))


def _same_head_block(c):
    ii = lax.broadcasted_iota(jnp.int32, (2 * c, 2 * c), 0)
    jj = lax.broadcasted_iota(jnp.int32, (2 * c, 2 * c), 1)
    same = (ii < c) == (jj < c)
    return same & (jj <= ii), same & (jj < ii), (ii == jj).astype(F32)


@jax.custom_vjp
def _rows_join(top, bottom):
    return jnp.concatenate([top, bottom], axis=0)


def _rows_join_bwd(n_top, g):
    return g[:n_top], g[n_top:]


_rows_join.defvjp(lambda top, bottom: (_rows_join(top, bottom), top.shape[0]), _rows_join_bwd)


def _rows_split_impl(x, n_top):
    return x[:n_top], x[n_top:]


_rows_split = jax.custom_vjp(_rows_split_impl, nondiff_argnums=(1,))
_rows_split.defvjp(lambda x, n_top: (_rows_split_impl(x, n_top), None),
                   lambda n_top, _, g: (jnp.concatenate([g[0], g[1]], axis=0),))


def _rwkv_step(s0, r, lw, k, v, a, b):
    npair, nj = len(r), len(r[0])
    c = r[0][0].shape[0]
    combos = [(j, p) for j in range(nj) for p in range(npair)]
    every = lambda fn: {q: fn(q) for q in combos}
    at_ = lambda d: (lambda q: d[q[1]][q[0]])
    r_, lw_, k_, v_, a_, b_ = (at_(z) for z in (r, lw, k, v, a, b))
    incl, strict, eye = _same_head_block(c)

    gam = every(lambda q: _cumsum_rows(lw_(q)))
    gtot = every(lambda q: jnp.sum(lw_(q), axis=0, keepdims=True))
    eneg = every(lambda q: jnp.exp(-gam[q]))
    edec = every(lambda q: jnp.exp(gtot[q] - gam[q]))
    at = every(lambda q: _stack_heads(a_(q) * jnp.exp(gam[q] - lw_(q))))
    rt = every(lambda q: _stack_heads(r_(q) * jnp.exp(gam[q])))
    bt = every(lambda q: _stack_heads(b_(q) * eneg[q]))
    kt = every(lambda q: _stack_heads(k_(q) * eneg[q]))
    bdec = every(lambda q: _stack_heads(b_(q) * edec[q]))
    kdec = every(lambda q: _stack_heads(k_(q) * edec[q]))
    vs = every(lambda q: _stack_heads(v_(q)))
    a_ab = every(lambda q: jnp.where(strict, _nt(at[q], bt[q]), 0.0))
    a_ak = every(lambda q: jnp.where(strict, _nt(at[q], kt[q]), 0.0))
    a_rb = every(lambda q: jnp.where(incl, _nt(rt[q], bt[q]), 0.0))
    a_rk = every(lambda q: jnp.where(incl, _nt(rt[q], kt[q]), 0.0))
    tinv = every(lambda q: eye + a_ab[q])
    pw = a_ab
    span = 2
    while span < c:
        pw = every(lambda q, pw=pw: _nn_x3(pw[q], pw[q]))
        tinv = every(lambda q, pw=pw, tinv=tinv: tinv[q] + _nn_x3(pw[q], tinv[q]))
        span *= 2
    akv = every(lambda q: _nn(a_ak[q], vs[q]))
    w1 = every(lambda q: _nn(tinv[q], at[q]))
    u0 = every(lambda q: _nn(tinv[q], akv[q]))
    wr = every(lambda q: _rows_join(w1[q], rt[q]))
    bk = every(lambda q: _rows_join(bdec[q], kdec[q]))
    yv = every(lambda q: _nn(a_rk[q], vs[q]))
    gdec = every(lambda q: jnp.exp(gtot[q]))

    s = list(s0)
    y = [[None] * nj for _ in range(npair)]
    for j in range(nj):
        both = {p: _rows_split(_nt(wr[(j, p)], s[p]), 2 * c) for p in range(npair)}
        u = {p: both[p][0] + u0[(j, p)] for p in range(npair)}
        for p in range(npair):
            y[p][j] = _unstack_heads(both[p][1] + _nn(a_rb[(j, p)], u[p]) + yv[(j, p)])
        s = [s[p] * gdec[(j, p)] + _tn(_rows_join(u[p], vs[(j, p)]), bk[(j, p)]) for p in range(npair)]
    return y, s


def _rwkv_blocks(ref, nj, c):
    return [[ref[j * c:(j + 1) * c, p * PAIR_W:(p + 1) * PAIR_W] for j in range(nj)] for p in range(HB_PAIRS)]


def _rwkv_fwd(seqs, comm=None):
    t = seqs[0].shape[0]
    c, nj = RWKV_CHUNK, RWKV_GROUP
    n = t // (c * nj)

    def body(r_ref, lw_ref, k_ref, v_ref, a_ref, b_ref, y_ref, hs_ref, st_ref):
        @pl.when(pl.program_id(0) == 0)
        def _():
            st_ref[...] = jnp.zeros_like(st_ref)

        hs_ref[0] = st_ref[...]
        s0 = [st_ref[p] for p in range(HB_PAIRS)]
        y, s1 = _rwkv_step(s0, *[_rwkv_blocks(ref, nj, c) for ref in (r_ref, lw_ref, k_ref, v_ref, a_ref, b_ref)])
        for p in range(HB_PAIRS):
            for j in range(nj):
                y_ref[j * c:(j + 1) * c, p * PAIR_W:(p + 1) * PAIR_W] = y[p][j]
            st_ref[p] = s1[p]

    seq = pl.BlockSpec((c * nj, W_B), lambda i: (i, 0))
    return _hosting_call(
        body, comm, name="rwkv_fwd", grid=(n,), in_specs=[seq] * 6,
        out_specs=[seq, pl.BlockSpec((1, HB_PAIRS, PAIR_W, PAIR_W), lambda i: (i, 0, 0, 0))],
        out_shape=[SDS((t, W_B), F32), SDS((n, HB_PAIRS, PAIR_W, PAIR_W), F32)],
        scratch_shapes=[pltpu.VMEM((HB_PAIRS, PAIR_W, PAIR_W), F32)], args=tuple(seqs))


def _rwkv_bwd(seqs, hs, dy, comm=None):
    t = seqs[0].shape[0]
    c, nj = RWKV_CHUNK, RWKV_GROUP
    n = t // (c * nj)

    def body(r_ref, lw_ref, k_ref, v_ref, a_ref, b_ref, hs_ref, dy_ref,
             dr_ref, dlw_ref, dk_ref, dv_ref, da_ref, db_ref, dst_ref):
        @pl.when(pl.program_id(0) == 0)
        def _():
            dst_ref[...] = jnp.zeros_like(dst_ref)

        s0 = [hs_ref[0, p] for p in range(HB_PAIRS)]
        seq_vals = [_rwkv_blocks(ref, nj, c) for ref in (r_ref, lw_ref, k_ref, v_ref, a_ref, b_ref)]
        _, vjp = jax.vjp(_rwkv_step, s0, *seq_vals)
        grads = vjp((_rwkv_blocks(dy_ref, nj, c), [dst_ref[p] for p in range(HB_PAIRS)]))
        for ref, gr in zip((dr_ref, dlw_ref, dk_ref, dv_ref, da_ref, db_ref), grads[1:]):
            for p in range(HB_PAIRS):
                for j in range(nj):
                    ref[j * c:(j + 1) * c, p * PAIR_W:(p + 1) * PAIR_W] = gr[p][j]
        m0, m1 = _head_lane_masks()
        rows0 = (lax.broadcasted_iota(jnp.int32, (PAIR_W, 1), 0) < HB_DIM).astype(F32)
        blocks = rows0 * m0 + (1.0 - rows0) * m1
        for p in range(HB_PAIRS):
            dst_ref[p] = grads[0][p] * blocks

    seq = pl.BlockSpec((c * nj, W_B), lambda i: (n - 1 - i, 0))
    return _hosting_call(
        body, comm, name="rwkv_bwd", grid=(n,),
        in_specs=[seq] * 6 + [pl.BlockSpec((1, HB_PAIRS, PAIR_W, PAIR_W), lambda i: (n - 1 - i, 0, 0, 0)), seq],
        out_specs=[seq] * 6, out_shape=[SDS((t, W_B), F32)] * 6,
        scratch_shapes=[pltpu.VMEM((HB_PAIRS, PAIR_W, PAIR_W), F32)], args=(*seqs, hs, dy))


def _final_loss(x3, fnorm, target, *, tm):
    t, d = x3.shape

    def body(x_ref, g_ref, t_ref, dx_ref, dg_ref, loss_ref):
        @pl.when(pl.program_id(0) == 0)
        def _():
            dg_ref[...] = jnp.zeros_like(dg_ref)
            loss_ref[...] = jnp.zeros_like(loss_ref)

        x, g = x_ref[...], g_ref[...]
        rinv = lax.rsqrt(jnp.mean(x * x, axis=-1, keepdims=True) + NORM_EPS)
        xh = x * rinv
        diff = xh * g - t_ref[...]
        loss_ref[...] += 0.5 * jnp.sum(jnp.mean(diff * diff, axis=-1, keepdims=True))
        dy = diff * (1.0 / d)
        dg_ref[...] += jnp.sum(dy * xh, axis=0, keepdims=True)
        dxh = dy * g
        dx_ref[...] = rinv * (dxh - xh * jnp.mean(dxh * xh, axis=-1, keepdims=True))

    row = pl.BlockSpec((tm, d), lambda i: (i, 0))
    return pl.pallas_call(
        body, name="final_loss", grid=(t // tm,), in_specs=[row, pl.BlockSpec((1, d), lambda i: (0, 0)), row],
        out_specs=[row, pl.BlockSpec((1, d), lambda i: (0, 0)), pl.BlockSpec((8, 128), lambda i: (0, 0))],
        out_shape=[SDS((t, d), F32), SDS((1, d), F32), SDS((8, 128), F32)],
        compiler_params=_params(("arbitrary",)))(x3, fnorm, target)


def _gate_up_act(h, wgt, wut, *, tm, tn, name, comm=None):
    t, d = h.shape
    tm = min(tm, t)

    def body(h_ref, g_ref, u_ref, a_out, u_out, act_out):
        hv = h_ref[...]
        a = _dg(hv, g_ref[...], 1, 1, False)
        u = _dg(hv, u_ref[...], 1, 1, False)
        a_out[...] = a.astype(a_out.dtype)
        u_out[...] = u.astype(u_out.dtype)
        act_out[...] = (_silu(a) * u).astype(act_out.dtype)

    wspec = pl.BlockSpec((tn, d), lambda i, j: (j, 0))
    ospec = pl.BlockSpec((tm, tn), lambda i, j: (i, j))
    return _hosting_call(
        body, comm, name=name, grid=(t // tm, D_FF // tn),
        in_specs=[pl.BlockSpec((tm, d), lambda i, j: (i, 0)), wspec, wspec], out_specs=[ospec, ospec, ospec],
        out_shape=[SDS((t, D_FF), BF16), SDS((t, D_FF), BF16), SDS((t, D_FF), BF16)], scratch_shapes=[],
        args=(h, wgt, wut))


def _dact_swiglu(dout, wd, a, u, *, tm, tn, name, comm=None):
    t, d = dout.shape
    tm = min(tm, t)

    def body(d_ref, w_ref, a_ref, u_ref, da_out, du_out):
        dact = 0.5 * _dg(d_ref[...], w_ref[...], 1, 1, False)
        av, uv = a_ref[...].astype(F32), u_ref[...].astype(F32)
        s = _sigmoid(av)
        da_out[...] = (dact * uv * (s * (1.0 + av * (1.0 - s)))).astype(da_out.dtype)
        du_out[...] = (dact * (av * s)).astype(du_out.dtype)

    tile = pl.BlockSpec((tm, tn), lambda i, j: (i, j))
    return _hosting_call(
        body, comm, name=name, grid=(t // tm, D_FF // tn),
        in_specs=[pl.BlockSpec((tm, d), lambda i, j: (i, 0)), pl.BlockSpec((tn, d), lambda i, j: (j, 0)), tile, tile],
        out_specs=[tile, tile], out_shape=[SDS((t, D_FF), BF16), SDS((t, D_FF), BF16)], scratch_shapes=[],
        args=(dout, wd, a, u))


class _Plan:
    def __init__(self):
        self.entries, self.counts = collections.defaultdict(list), {}

    def carry(self, host, comm_of, after):
        self.entries[host].append((comm_of, after))

    def comm(self, host, g):
        comms = [comm_of(g) for comm_of, _ in self.entries.get(host, [])]
        self.counts[host] = [len(c.arrays) for c in comms]
        return functools.reduce(_join_comms, comms) if comms else None

    def done(self, host, results, w):
        start = 0
        for (_, after), n in zip(self.entries.get(host, []), self.counts.get(host, [])):
            after(results[start:start + n], w)
            start += n


def _ffn_fwd(x, w, tag, plan, g):
    comm = plan.comm(f"{tag}_rms", g)
    res = _rowwise(_rms_f, [x], [w[f"{tag}_norm"]], [[0]], [BF16], tm=1024, name=f"{tag}_rms", comm=comm)
    (h,), carried = res if comm is not None else (res, [])
    plan.done(f"{tag}_rms", carried, w)
    (a, u, act), carried = _gate_up_act(h, w[f"{tag}_wgt"], w[f"{tag}_wut"], tm=2048, tn=256, name=f"{tag}_gate_up",
                                        comm=plan.comm(f"{tag}_gate_up", g))
    plan.done(f"{tag}_gate_up", carried, w)
    comm = plan.comm(f"{tag}_down", g)
    out = _mm(act, w[f"{tag}_wd"], tm=1024, tn=D_MODEL, tk=D_FF, name=f"{tag}_down", res=x, scale=0.5, comm=comm)
    if comm is not None:
        out, carried = out
        plan.done(f"{tag}_down", carried, w)
    return out, (h, a, u, act)


def _ffn_bwd(dout, x, w, saved, tag, plan, g):
    h, a, u, act = saved

    def carrying(fn, host, *args, **kwargs):
        comm = plan.comm(host, g)
        res = fn(*args, name=host, comm=comm, **kwargs)
        out, carried = res if comm is not None else (res, [])
        plan.done(host, carried, w)
        return out

    (da, du), carried = _dact_swiglu(dout, w[f"{tag}_wd"], a, u, tm=2048, tn=256, name=f"{tag}_dact",
                                     comm=plan.comm(f"{tag}_dact", g))
    plan.done(f"{tag}_dact", carried, w)
    g[f"{tag}_wd"] = _mm(act, dout, ta=True, tm=D_FF // 2, tn=D_MODEL, tk=1024, name=f"{tag}_dwd", scale=0.5)
    g[f"{tag}_wgt"] = carrying(_mm, f"{tag}_dwg", da, h, ta=True, tm=D_FF // 2, tn=D_MODEL, tk=1024)
    g[f"{tag}_wut"] = carrying(_mm, f"{tag}_dwu", du, h, ta=True, tm=D_FF // 2, tn=D_MODEL, tk=1024)
    if plan.entries.get(f"{tag}_dh_g") or plan.entries.get(f"{tag}_dh_u"):
        dh = carrying(_mm, f"{tag}_dh_g", da, w[f"{tag}_wgt"], tm=1024, tn=D_MODEL, tk=D_FF)
        dh = carrying(_mm, f"{tag}_dh_u", du, w[f"{tag}_wut"], tm=1024, tn=D_MODEL, tk=D_FF, res=dh)
    else:
        dh = _mm_pair(da, w[f"{tag}_wgt"], du, w[f"{tag}_wut"], tm=512, name=f"{tag}_dh")
    dx, g[f"{tag}_norm"] = carrying(_rowwise_bwd, f"{tag}_drms", _rms_f, [x], [w[f"{tag}_norm"]], [dh],
                                    x_grad=[True], p_grad=[True], dx_groups=[[0]], dx_dtypes=[F32], tm=512,
                                    extra={0: dout})
    return dx


def _local_step(x, target, w, plan=None):
    plan = plan or _Plan()
    ones_bd = jnp.kron(jnp.eye(HB_HEADS, dtype=F32), jnp.ones((HB_DIM, HB_DIM), F32))
    g = {}
    x1, ffn1_saved = _ffn_fwd(x, w, "ffn1", plan, g)
    hm, = _rowwise(_rms_f, [x1], [w["mix_norm"]], [[0]], [BF16], tm=1024, name="mix_rms")
    p_h = _mm(hm, w["w_in_h"], tm=2048, tn=256, tk=D_MODEL, name="inproj_h")
    p_r = _mm(hm, w["w_in_r"], tm=2048, tn=256, tk=D_MODEL, name="inproj_r")
    o_a, hgrn_states = _hgrn_fwd(p_h, w["lb0"], w["lb1"], w["hgrn_out_norm"])

    mu = w["mu_pad"]
    prep_xs = [(p_r, W_B, 0), (p_r, W_B, 1), (p_r, W_B, 2), (p_r, LORA_PAD, 6),
               ("prev", p_r, W_B, 0), ("prev", p_r, W_B, 1), ("prev", p_r, W_B, 2), ("prev", p_r, LORA_PAD, 6)]
    prep_ps = [(mu, W_B, 0), (mu, W_B, 1), (mu, W_B, 2), (mu, LORA_PAD, 6), w["rwkv_w0"], w["w2_pad"], w["rwkv_a0"],
               w["a2_pad"], w["g2_pad"], w["rwkv_k_k"], w["rwkv_k_a"], ones_bd]
    prep_f = _rwkv_prep_f
    r, lw, k2, v, a_vec, b_vec, gate = _rowwise(prep_f, prep_xs, prep_ps, [[0], [1], [2], [3], [4], [5], [6]],
                                                [F32] * 7, tm=256, name="rwkv_prep")
    seqs = [r, lw, k2, v, a_vec, b_vec]
    (y, rwkv_states), carried = _rwkv_fwd(seqs, comm=plan.comm("rwkv_fwd", g))
    plan.done("rwkv_fwd", carried, w)
    post_f = _rwkv_post_f
    post_xs = [y, r, k2, v, gate]
    post_ps = [w["rwkv_r_k"], w["rwkv_gn_w"], w["rwkv_gn_b"], ones_bd]
    o, = _rowwise(lambda o_a_, *rest: (o_a_,) + tuple(post_f(*rest)), [o_a] + post_xs, post_ps, [[0, 1]], [F32],
                  tm=256, name="rwkv_post")
    x2 = _mm(o, w["w_out"], tm=2048, tn=256, tk=D_MODEL, name="outproj", res=x1)
    x3, ffn2_saved = _ffn_fwd(x2, w, "ffn2", plan, g)
    dx3, g["final_norm"], loss = _final_loss(x3, w["final_norm"], target, tm=512)

    dx2 = _ffn_bwd(dx3, x2, w, ffn2_saved, "ffn2", plan, g)
    do = _mm(dx2, w["w_out"], tb=True, tm=2048, tn=256, tk=D_MODEL, name="outproj_do")
    g["w_out"] = _mm(o, dx2, ta=True, tm=D_MODEL, tn=D_MODEL, tk=1024, name="outproj_dw")

    (dp_h, g["lb0"], g["lb1"], g["hgrn_out_norm"]), carried = _hgrn_bwd(
        p_h, w["lb0"], w["lb1"], w["hgrn_out_norm"], hgrn_states, do, 0, comm=plan.comm("hgrn_bwd", g))
    plan.done("hgrn_bwd", carried, w)
    post_out = _rowwise_bwd(post_f, post_xs, post_ps, [(do, W_B, 1)], x_grad=[True] * 5, p_grad=[True] * 3 + [False],
                            dx_groups=[[0], [1], [2], [3], [4]], dx_dtypes=[F32] * 5, tm=256, name="rwkv_post_bwd")
    dy, dr1, dk1, dv1, dgate, g["rwkv_r_k"], g["rwkv_gn_w"], g["rwkv_gn_b"] = post_out
    (dr2, dlw, dk2, dv2, da_vec, db_vec), carried = _rwkv_bwd(seqs, rwkv_states, dy, comm=plan.comm("rwkv_bwd", g))
    plan.done("rwkv_bwd", carried, w)

    def prep2_f(*vals):
        r_, lw_, k2_, v_, a_, b_, g_ = prep_f(*vals)
        return r_, lw_, k2_, v_, a_, b_, g_, r_, k2_, v_

    prep_comm = plan.comm("rwkv_prep_bwd", g)
    prep_out = _rowwise_bwd(prep2_f, prep_xs, prep_ps, [dr2, dlw, dk2, dv2, da_vec, db_vec, dgate, dr1, dk1, dv1],
                            x_grad=[True] * 8, p_grad=[True] * 11 + [False], dx_groups=[[0, 1, 2, 3], [4, 5, 6, 7]],
                            dx_dtypes=[F32], tm=256, name="rwkv_prep_bwd", fold_next=(0, 1), comm=prep_comm)
    prep_out, carried = prep_out if prep_comm is not None else (prep_out, [])
    plan.done("rwkv_prep_bwd", carried, w)
    dp_r = prep_out[0]
    (dmu_r, dmu_k, dmu_v, dmu_lo, g["rwkv_w0"], g["w2_pad"], g["rwkv_a0"], g["a2_pad"], g["g2_pad"],
     g["rwkv_k_k"], g["rwkv_k_a"]) = prep_out[1:]
    g["mu_pad"] = jnp.concatenate([dmu_r, dmu_k, dmu_v, dmu_lo], axis=1)
    dhm = _mm_pair(dp_h, w["w_in_h"], dp_r, w["w_in_r"], tb=True, tm=512, name="inproj_dh")
    g["w_in_h"] = _mm(hm, dp_h, ta=True, tm=D_MODEL, tn=D_MODEL, tk=1024, name="inproj_dw_h")
    g["w_in_r"] = _mm(hm, dp_r, ta=True, tm=D_MODEL, tn=N_RWKV_PAD // 2, tk=1024, name="inproj_dw_r")
    mix_comm = plan.comm("mix_drms", g)
    mix_out = _rowwise_bwd(_rms_f, [x1], [w["mix_norm"]], [dhm], x_grad=[True], p_grad=[True], dx_groups=[[0]],
                           dx_dtypes=[F32], tm=512, name="mix_drms", extra={0: dx2}, comm=mix_comm)
    (dx1, g["mix_norm"]), carried = mix_out if mix_comm is not None else (mix_out, [])
    plan.done("mix_drms", carried, w)
    dx0 = _ffn_bwd(dx1, x, w, ffn1_saved, "ffn1", plan, g)
    return loss, dx0, g


HBM_SPEC = pl.BlockSpec(memory_space=pl.ANY)

Comm = collections.namedtuple("Comm", "arrays out_shapes aliased sem_shapes start finish")


def _join_comms(first, second):
    n, s = len(first.arrays), len(first.sem_shapes)

    def start(ins, outs, sems):
        first.start(ins[:n], outs[:n], sems[:s])
        second.start(ins[n:], outs[n:], sems[s:])

    def finish(ins, outs, sems):
        first.finish(ins[:n], outs[:n], sems[:s])
        second.finish(ins[n:], outs[n:], sems[s:])

    return Comm(list(first.arrays) + list(second.arrays), list(first.out_shapes) + list(second.out_shapes),
                list(first.aliased) + list(second.aliased), list(first.sem_shapes) + list(second.sem_shapes),
                start, finish)


def _run_comm(comm, name):
    n = len(comm.arrays)

    def body(*refs):
        ins, outs, sems = refs[:n], refs[n:2 * n], refs[2 * n:]
        comm.start(ins, outs, sems)
        comm.finish(ins, outs, sems)

    return pl.pallas_call(
        body, name=name, in_specs=[HBM_SPEC] * n, out_specs=[HBM_SPEC] * n, out_shape=list(comm.out_shapes),
        input_output_aliases={t: t for t in range(n) if comm.aliased[t]},
        scratch_shapes=list(comm.sem_shapes))(*comm.arrays)


def _hosting_call(body, comm, *, name, grid, in_specs, out_specs, out_shape, scratch_shapes, args):
    sem = ("arbitrary",) * len(grid)
    if comm is None:
        res = pl.pallas_call(body, name=name, grid=grid, in_specs=in_specs, out_specs=out_specs, out_shape=out_shape,
                             scratch_shapes=scratch_shapes, compiler_params=_params(sem))(*args)
        return list(res), []
    ni, no, ns, nc = len(in_specs), len(out_specs), len(scratch_shapes), len(comm.arrays)

    def wrapped(*refs):
        ins, cins = refs[:ni], refs[ni:ni + nc]
        outs, couts = refs[ni + nc:ni + nc + no], refs[ni + nc + no:ni + 2 * nc + no]
        scr, sems = refs[ni + 2 * nc + no:ni + 2 * nc + no + ns], refs[ni + 2 * nc + no + ns:]
        first = functools.reduce(jnp.logical_and, [pl.program_id(k) == 0 for k in range(len(grid))])
        last = functools.reduce(jnp.logical_and, [pl.program_id(k) == grid[k] - 1 for k in range(len(grid))])

        @pl.when(first)
        def _():
            comm.start(cins, couts, sems)

        body(*ins, *outs, *scr)

        @pl.when(last)
        def _():
            comm.finish(cins, couts, sems)

    res = pl.pallas_call(
        wrapped, name=name, grid=grid, in_specs=list(in_specs) + [HBM_SPEC] * nc,
        out_specs=list(out_specs) + [HBM_SPEC] * nc, out_shape=list(out_shape) + list(comm.out_shapes),
        scratch_shapes=list(scratch_shapes) + list(comm.sem_shapes),
        input_output_aliases={ni + t: no + t for t in range(nc) if comm.aliased[t]},
        compiler_params=_params(sem))(*args, *comm.arrays)
    return list(res[:no]), list(res[no:])


def _chips(x, y):
    return [(1 - x, y), (x, 1 - y), (1 - x, 1 - y)]


def _gather_comm(bufs):
    n = len(bufs)

    def copies(outs, sems):
        ici_send, ici_recv, d2d_send, d2d_recv = sems
        x, y, c = lax.axis_index("x"), lax.axis_index("y"), lax.axis_index("c")

        def half(t, slot, hc):
            hr = bufs[t].shape[1] // 2
            return outs[t].at[slot, pl.ds(pl.multiple_of(hc * hr, 16), hr), :]

        def ici(t, j, slot, px, py):
            return pltpu.make_async_remote_copy(src_ref=half(t, slot, c), dst_ref=half(t, slot, c),
                                                send_sem=ici_send.at[3 * t + j], recv_sem=ici_recv.at[3 * t + j],
                                                device_id=(px, py, c), device_id_type=MESH)

        def d2d(t, j, slot, hc):
            return pltpu.make_async_remote_copy(src_ref=half(t, slot, hc), dst_ref=half(t, slot, hc),
                                                send_sem=d2d_send.at[3 * t + j], recv_sem=d2d_recv.at[3 * t + j],
                                                device_id=(x, y, 1 - c), device_id_type=MESH)

        peers = [(t, j, px, py) for t in range(n) for j, (px, py) in enumerate(_chips(x, y))]
        return ici, d2d, peers, 2 * x + y, c

    def start(ins, outs, sems):
        ici, _, peers, me, _ = copies(outs, sems)
        for t, j, px, py in peers:
            ici(t, j, me, px, py).start()

    def finish(ins, outs, sems):
        ici, d2d, peers, me, c = copies(outs, sems)
        for t, j, px, py in peers:
            ici(t, j, 2 * px + py, px, py).wait_recv()
            d2d(t, j, 2 * px + py, c).start()
        for t, j, px, py in peers:
            d2d(t, j, 2 * px + py, 1 - c).wait_recv()
        for t, j, px, py in peers:
            ici(t, j, me, px, py).wait_send()
            d2d(t, j, 2 * px + py, c).wait_send()

    return Comm(list(bufs), [SDS(b.shape, b.dtype) for b in bufs], [True] * n,
                [pltpu.SemaphoreType.DMA((3 * n,))] * 4, start, finish)


def _sibling_exchange_comm(gs):
    n = len(gs)

    def copies(ins, outs, sems):
        x, y, c = lax.axis_index("x"), lax.axis_index("y"), lax.axis_index("c")
        cps = []
        for t in range(n):
            hr = gs[t].shape[1] // 2
            src = ins[t].at[:, pl.ds(pl.multiple_of((1 - c) * hr, SUBLANES), hr), :]
            cps.append(pltpu.make_async_remote_copy(src_ref=src, dst_ref=outs[t], send_sem=sems[0].at[t],
                                                    recv_sem=sems[1].at[t], device_id=(x, y, 1 - c),
                                                    device_id_type=MESH))
        return cps

    def start(ins, outs, sems):
        for cp in copies(ins, outs, sems):
            cp.start()

    def finish(ins, outs, sems):
        for cp in copies(ins, outs, sems):
            cp.wait()

    return Comm(list(gs), [SDS((N_CHIPS, g.shape[1] // 2, g.shape[2]), g.dtype) for g in gs], [False] * n,
                [pltpu.SemaphoreType.DMA((n,))] * 2, start, finish)


def _own_rows(c, rows):
    return pl.ds(pl.multiple_of(c * (rows // 2), 16), rows // 2)


def _chip_exchange_comm(ss):
    n = len(ss)

    def copies(ins, outs, sems):
        x, y, c = lax.axis_index("x"), lax.axis_index("y"), lax.axis_index("c")
        me = 2 * x + y

        def copy(t, j, px, py, src_slot, dst_slot):
            rows = _own_rows(c, ss[t].shape[1])
            return pltpu.make_async_remote_copy(src_ref=ins[t].at[src_slot, rows, :],
                                                dst_ref=outs[t].at[dst_slot, rows, :],
                                                send_sem=sems[0].at[3 * t + j], recv_sem=sems[1].at[3 * t + j],
                                                device_id=(px, py, c), device_id_type=MESH)

        peers = [(t, j, px, py) for t in range(n) for j, (px, py) in enumerate(_chips(x, y))]
        return copy, peers, me

    def start(ins, outs, sems):
        copy, peers, me = copies(ins, outs, sems)
        for t, j, px, py in peers:
            copy(t, j, px, py, 2 * px + py, me).start()

    def finish(ins, outs, sems):
        copy, peers, me = copies(ins, outs, sems)
        for t, j, px, py in peers:
            copy(t, j, px, py, me, 2 * px + py).wait_recv()
        for t, j, px, py in peers:
            copy(t, j, px, py, 2 * px + py, me).wait_send()

    return Comm(list(ss), [SDS(s.shape, s.dtype) for s in ss], [False] * n,
                [pltpu.SemaphoreType.DMA((3 * n,))] * 2, start, finish)


def _sibling_swap_comm(rs, ss):
    n = len(rs)

    def copies(outs, sems):
        x, y, c = lax.axis_index("x"), lax.axis_index("y"), lax.axis_index("c")
        cps = []
        for t in range(n):
            rows = _own_rows(c, rs[t].shape[1])
            held = [outs[t].at[2 * px + py, rows, :] for px, py in _chips(x, y)] + [outs[n + t].at[2 * x + y, rows, :]]
            cps += [pltpu.make_async_remote_copy(src_ref=ref, dst_ref=ref, send_sem=sems[0].at[4 * t + j],
                                                 recv_sem=sems[1].at[4 * t + j], device_id=(x, y, 1 - c),
                                                 device_id_type=MESH) for j, ref in enumerate(held)]
        return cps

    def start(ins, outs, sems):
        for cp in copies(outs, sems):
            cp.start()

    def finish(ins, outs, sems):
        for cp in copies(outs, sems):
            cp.wait()

    both = list(rs) + list(ss)
    return Comm(both, [SDS(b.shape, b.dtype) for b in both], [True] * (2 * n),
                [pltpu.SemaphoreType.DMA((4 * n,))] * 2, start, finish)


def _row_tile(rows, cap=512):
    best = SUBLANES
    for tr in range(SUBLANES, min(rows, cap) + 1, SUBLANES):
        if rows % tr == 0:
            best = tr
    return best


def _add_halves(g4, r4, c_idx, name):
    _, hr, lanes = r4.shape
    tr = _row_tile(hr)
    nb = hr // tr

    def body(c_ref, a_ref, b_ref, o_ref):
        o_ref[...] = (a_ref[...] + b_ref[...]).astype(o_ref.dtype)

    pair = 2
    owned = pl.BlockSpec((pair, tr, lanes), lambda q, i, c_ref: (q, c_ref[0] * nb + i, 0))
    grid_spec = pltpu.PrefetchScalarGridSpec(
        num_scalar_prefetch=1, grid=(N_CHIPS // pair, nb),
        in_specs=[owned, pl.BlockSpec((pair, tr, lanes), lambda q, i, c_ref: (q, i, 0))], out_specs=owned)
    return pl.pallas_call(body, name=name, grid_spec=grid_spec, out_shape=SDS(g4.shape, BF16),
                          compiler_params=_params(("parallel", "parallel")))(c_idx, g4, r4)


def _adamw(wf, r4, s4, mf, vf, me_idx, name):
    rows, lanes = wf.shape
    tr = rows // 2
    assert tr % 16 == 0
    c1 = 1.0 / (1.0 - ADAM_B1 ** ADAM_STEP)
    c2 = 1.0 / (1.0 - ADAM_B2 ** ADAM_STEP)

    def body(me_ref, w_ref, a_ref, b_ref, c_ref, d_ref, own_ref, m_ref, v_ref, g_ref, delta_ref, nm_ref, nv_ref):
        own = own_ref[...].astype(F32)
        p = [jnp.where(me_ref[0] == q, own, ref[...].astype(F32)) for q, ref in enumerate((a_ref, b_ref, c_ref, d_ref))]
        gv = ((p[0] + p[1]) + p[2]) + p[3]
        m = ADAM_B1 * m_ref[...] + (1.0 - ADAM_B1) * gv
        v = ADAM_B2 * v_ref[...] + (1.0 - ADAM_B2) * (gv * gv)
        g_ref[...] = gv
        delta_ref[...] = -ADAM_LR * ((m * c1) / (jnp.sqrt(v * c2) + ADAM_EPS) + ADAM_WD * w_ref[...])
        nm_ref[...] = m
        nv_ref[...] = v

    other = lambda q: (lambda i, me_ref: (jnp.where(me_ref[0] == q, (q + 1) % N_CHIPS, q), i, 0))
    full = pl.BlockSpec((tr, lanes), lambda i, me_ref: (i, 0))
    grid_spec = pltpu.PrefetchScalarGridSpec(
        num_scalar_prefetch=1, grid=(rows // tr,),
        in_specs=[full] + [pl.BlockSpec((None, tr, lanes), other(q)) for q in range(N_CHIPS)]
        + [pl.BlockSpec((None, tr, lanes), lambda i, me_ref: (me_ref[0], i, 0)), full, full],
        out_specs=[full] * 4)
    return pl.pallas_call(body, name=name, grid_spec=grid_spec, out_shape=[SDS((rows, lanes), F32)] * 4,
                          compiler_params=_params(("parallel",)))(me_idx, wf, r4, r4, r4, r4, s4, mf, vf)


BIG = ("ffn1_w_gate", "ffn1_w_up", "ffn1_w_down", "ffn2_w_gate", "ffn2_w_up", "ffn2_w_down", "w_out", "w_in")
TRANSPOSED = ("ffn1_w_gate", "ffn1_w_up", "ffn2_w_gate", "ffn2_w_up")
PACKED = ("rwkv_w2", "rwkv_a2", "rwkv_g2")
SMALL_SHAPES = {"ffn1_norm": (1, D_MODEL), "mix_norm": (1, D_MODEL), "hgrn_lb_logits": (2, W_A),
                "hgrn_out_norm": (1, W_A), "rwkv_shift_mu": (1, N_RWKV_COLS), "rwkv_w0": (1, W_B),
                "rwkv_a0": (1, W_B), "rwkv_k_k": (1, W_B), "rwkv_k_a": (1, W_B),
                "rwkv_r_k": (1, HB_HEADS, HB_DIM), "rwkv_gn_w": (1, W_B), "rwkv_gn_b": (1, W_B),
                "ffn2_norm": (1, D_MODEL), "final_norm": (D_MODEL,)}
PACK_ELEMS = sum(_numel(_shard_shape(n)) for n in PACKED) + sum(_numel(SMALL_SHAPES[n]) for n in SMALL)
PACK_ROWS = -(-PACK_ELEMS // (32 * LANES)) * 32


def _to_rows(name, shard):
    return shard[0].T if name in TRANSPOSED else shard[0]


def _from_rows(name, rows):
    return (rows.T if name in TRANSPOSED else rows)[None]


def _pack(sharded, small):
    flat = jnp.concatenate([sharded[n].reshape(-1) for n in PACKED] + [small[n].reshape(-1) for n in SMALL])
    return jnp.pad(flat, (0, PACK_ROWS * LANES - flat.shape[0])).reshape(PACK_ROWS, LANES)


def _unpack(packed):
    flat, out, off = packed.reshape(-1), {}, 0
    for n in PACKED:
        shp = _shard_shape(n)
        out[n] = flat[off:off + _numel(shp)].reshape((1,) + shp)
        off += _numel(shp)
    for n in SMALL:
        shp = SMALL_SHAPES[n]
        out[n] = flat[off:off + _numel(shp)].reshape(shp)
        off += _numel(shp)
    return out


def _quarter(full, name, q):
    shape, ax = SHARDED_SHAPES[name]
    w = shape[ax] // N_CHIPS
    return lax.slice_in_dim(full, q * w, (q + 1) * w, axis=ax)


def kernel(x, ffn1_norm, ffn1_w_gate, ffn1_w_up, ffn1_w_down, mix_norm, w_in, hgrn_lb_logits, hgrn_out_norm, rwkv_shift_mu, rwkv_w0, rwkv_w2, rwkv_a0, rwkv_a2, rwkv_g2, rwkv_k_k, rwkv_k_a, rwkv_r_k, rwkv_gn_w, rwkv_gn_b, w_out, ffn2_norm, ffn2_w_gate, ffn2_w_up, ffn2_w_down, final_norm, loss_target, m_ffn1_norm, m_ffn1_w_gate, m_ffn1_w_up, m_ffn1_w_down, m_mix_norm, m_w_in, m_hgrn_lb_logits, m_hgrn_out_norm, m_rwkv_shift_mu, m_rwkv_w0, m_rwkv_w2, m_rwkv_a0, m_rwkv_a2, m_rwkv_g2, m_rwkv_k_k, m_rwkv_k_a, m_rwkv_r_k, m_rwkv_gn_w, m_rwkv_gn_b, m_w_out, m_ffn2_norm, m_ffn2_w_gate, m_ffn2_w_up, m_ffn2_w_down, m_final_norm, v_ffn1_norm, v_ffn1_w_gate, v_ffn1_w_up, v_ffn1_w_down, v_mix_norm, v_w_in, v_hgrn_lb_logits, v_hgrn_out_norm, v_rwkv_shift_mu, v_rwkv_w0, v_rwkv_w2, v_rwkv_a0, v_rwkv_a2, v_rwkv_g2, v_rwkv_k_k, v_rwkv_k_a, v_rwkv_r_k, v_rwkv_gn_w, v_rwkv_gn_b, v_w_out, v_ffn2_norm, v_ffn2_w_gate, v_ffn2_w_up, v_ffn2_w_down, v_final_norm):
    args = dict(locals())
    wts = {n: args[n] for n in ALL_WEIGHTS}
    moms = {n: args["m_" + n] for n in ALL_WEIGHTS}
    vars_ = {n: args["v_" + n] for n in ALL_WEIGHTS}

    me = 2 * lax.axis_index("x") + lax.axis_index("y")
    c_idx = lax.axis_index("c").astype(jnp.int32).reshape(1)
    me_idx = me.astype(jnp.int32).reshape(1)
    shard_of = {n: _to_rows(n, wts[n]).astype(BF16) for n in BIG}
    shard_of["packed"] = _pack(wts, {n: wts[n] for n in SMALL}).astype(BF16)
    group = {"ffn1": BIG[0:3], "ffn2": BIG[3:6]}

    def slot_bufs(names):
        return [lax.dynamic_update_slice(lax.empty((N_CHIPS,) + shard_of[n].shape, BF16), shard_of[n][None],
                                         (me, 0, 0)) for n in names]

    def ffn_weights(tag, gathered):
        return {f"{tag}_wgt": gathered[0].reshape(D_FF, D_MODEL), f"{tag}_wut": gathered[1].reshape(D_FF, D_MODEL),
                f"{tag}_wd": gathered[2].reshape(D_FF, D_MODEL)}

    def w_in_weights(gathered):
        w_in_full = jnp.concatenate([gathered[0][q] for q in range(N_CHIPS)], axis=1)
        return {"w_in_h": w_in_full[:, :N_HGRN_COLS],
                "w_in_r": jnp.pad(w_in_full[:, N_HGRN_COLS:], ((0, 0), (0, N_RWKV_PAD - N_RWKV_COLS)))}

    def mixer_weights(gathered):
        w_out_full = gathered[0].reshape(D_MODEL, D_MODEL)
        packs = gathered[1].reshape(N_CHIPS, PACK_ROWS * LANES)
        full, off = {}, 0
        for n in PACKED:
            shp = _shard_shape(n)
            full[n] = jnp.concatenate([packs[q, off:off + _numel(shp)].reshape(shp) for q in range(N_CHIPS)], axis=1)
            off += _numel(shp)
        zrow = lambda nrow: jnp.zeros((nrow, W_B), BF16)
        return {"w_out": w_out_full,
                "w2_pad": jnp.concatenate([full["rwkv_w2"], zrow(LORA_PAD - 32)], axis=0),
                "a2_pad": jnp.concatenate([zrow(32), full["rwkv_a2"], zrow(LORA_PAD - 64)], axis=0),
                "g2_pad": jnp.concatenate([zrow(64), full["rwkv_g2"], zrow(LORA_PAD - 160)], axis=0)}

    plan = _Plan()
    w = {}
    plan.carry("ffn1_rms", lambda g: _gather_comm(slot_bufs(group["ffn1"][:2])),
               lambda res, w_: w_.update({"ffn1_wgt": res[0].reshape(D_FF, D_MODEL),
                                          "ffn1_wut": res[1].reshape(D_FF, D_MODEL)}))

    def after_gate_up(res, w_):
        w_["ffn1_wd"] = res[0].reshape(D_FF, D_MODEL)
        w_.update(w_in_weights(res[1:]))

    plan.carry("ffn1_gate_up", lambda g: _gather_comm(slot_bufs(("ffn1_w_down", "w_in"))), after_gate_up)
    plan.carry("ffn1_down", lambda g: _gather_comm(slot_bufs(("w_out", "packed"))),
               lambda res, w_: w_.update(mixer_weights(res)))
    plan.carry("rwkv_fwd", lambda g: _gather_comm(slot_bufs(group["ffn2"])),
               lambda res, w_: w_.update(ffn_weights("ffn2", res)))
    w["ffn1_norm"], w["ffn2_norm"] = ffn1_norm, ffn2_norm
    w["mix_norm"] = mix_norm
    w["lb0"], w["lb1"] = hgrn_lb_logits[0:1], hgrn_lb_logits[1:2]
    w["hgrn_out_norm"] = hgrn_out_norm
    w["mu_pad"] = jnp.pad(rwkv_shift_mu, ((0, 0), (0, N_RWKV_PAD - N_RWKV_COLS)))
    for n in ("rwkv_w0", "rwkv_a0", "rwkv_k_k", "rwkv_k_a", "rwkv_gn_w", "rwkv_gn_b"):
        w[n] = wts[n]
    w["rwkv_r_k"] = rwkv_r_k.reshape(1, W_B)
    w["final_norm"] = final_norm.reshape(1, D_MODEL)

    def reduce_rows(names, gs):
        r1 = _run_comm(_sibling_exchange_comm(gs), "grad_sibling_exchange")
        s4 = [_add_halves(gt, rt, c_idx, f"grad_add_halves_{n}") for gt, rt, n in zip(gs, r1, names)]
        return list(zip(_run_comm(_chip_exchange_comm(s4), "grad_chip_exchange"), s4))

    def swap_comm(names):
        return _sibling_swap_comm([early[n][0] for n in names], [early[n][1] for n in names])

    def after_swap(names):
        return lambda res, w_: swapped.update(zip(names, zip(res[:len(names)], res[len(names):])))

    early, swapped = {}, {}

    def reduce_early(names, grads_of, sibling_host, chips_host, swap_host):
        def sibling_comm(g):
            early[names, "gs"] = grads_of(g)
            return _sibling_exchange_comm(early[names, "gs"])

        def after_sibling(res, w_):
            early[names, "s4"] = [_add_halves(gt, rt, c_idx, f"grad_add_halves_{n}")
                                  for gt, rt, n in zip(early[names, "gs"], res, names)]

        plan.carry(sibling_host, sibling_comm, after_sibling)
        plan.carry(chips_host, lambda g: _chip_exchange_comm(early[names, "s4"]),
                   lambda res, w_: early.update(zip(names, zip(res, early[names, "s4"]))))
        if swap_host:
            plan.carry(swap_host, lambda g: swap_comm(names), after_swap(names))

    def proj_grads(g):
        g_w_in = jnp.concatenate([g["w_in_h"], g["w_in_r"][:, :N_RWKV_COLS]], axis=1)
        return [g["w_out"].reshape(N_CHIPS, -1, D_MODEL),
                jnp.stack([_quarter(g_w_in, "w_in", q) for q in range(N_CHIPS)])]

    rows_of = lambda keys: (lambda g: [g[k].reshape(N_CHIPS, -1, D_MODEL) for k in keys])
    reduce_early(group["ffn2"], rows_of(("ffn2_wgt", "ffn2_wut", "ffn2_wd")), "hgrn_bwd", "rwkv_bwd", "rwkv_prep_bwd")
    reduce_early(("w_out", "w_in"), proj_grads, "mix_drms", "ffn1_dact", "ffn1_dwg")
    reduce_early(("ffn1_w_down",), rows_of(("ffn1_wd",)), "ffn1_dwg", "ffn1_dwu", "ffn1_dh_g")
    reduce_early(("ffn1_w_gate",), rows_of(("ffn1_wgt",)), "ffn1_dwu", "ffn1_dh_g", "ffn1_dh_u")
    reduce_early(("ffn1_w_up",), rows_of(("ffn1_wut",)), "ffn1_dh_g", "ffn1_dh_u", None)
    loss_slab, grad_x, g = _local_step(x[0], loss_target[0], w, plan)
    loss = lax.psum(loss_slab[0, 0], ("x", "y", "c"))

    gfull = {
        "rwkv_w2": g["w2_pad"][0:32], "rwkv_a2": g["a2_pad"][32:64], "rwkv_g2": g["g2_pad"][64:160],
    }
    gsmall = {
        "ffn1_norm": g["ffn1_norm"], "mix_norm": g["mix_norm"],
        "hgrn_lb_logits": jnp.concatenate([g["lb0"], g["lb1"]], axis=0), "hgrn_out_norm": g["hgrn_out_norm"],
        "rwkv_shift_mu": g["mu_pad"][:, :N_RWKV_COLS], "rwkv_w0": g["rwkv_w0"], "rwkv_a0": g["rwkv_a0"],
        "rwkv_k_k": g["rwkv_k_k"], "rwkv_k_a": g["rwkv_k_a"], "rwkv_r_k": g["rwkv_r_k"],
        "rwkv_gn_w": g["rwkv_gn_w"], "rwkv_gn_b": g["rwkv_gn_b"], "ffn2_norm": g["ffn2_norm"],
        "final_norm": g["final_norm"],
    }
    packed = jnp.stack([_pack({n: _quarter(gfull[n], n, q) for n in PACKED}, gsmall) for q in range(N_CHIPS)])
    early["packed"], = reduce_rows(["packed"], [packed])
    last = ["ffn1_w_up", "packed"]
    after_swap(last)(_run_comm(swap_comm(last), "grad_sibling_swap"), w)
    names = list(BIG) + ["packed"]

    def rows_list(d):
        return [_to_rows(n, d[n]) for n in BIG] + [_pack(d, {n: d[n] for n in SMALL})]

    outs = [_adamw(wt, *swapped[n], mt, vt, me_idx, f"adamw_{n}")
            for wt, mt, vt, n in zip(rows_list(wts), rows_list(moms), rows_list(vars_), names)]
    results = []
    for k in range(4):
        per = [outs[i][k] for i in range(len(names))]
        d = {n: _from_rows(n, z) for n, z in zip(BIG, per[:-1])}
        d.update(_unpack(per[-1]))
        results.append(d)
    return (loss, grad_x[None], *[r[n] for r in results for n in ALL_WEIGHTS])
```

```python
import collections
import functools

import jax
import jax.numpy as jnp
from jax import lax
from jax.experimental import pallas as pl
from jax.experimental.pallas import tpu as pltpu

F32 = jnp.float32
BF16 = jnp.bfloat16
SDS = jax.ShapeDtypeStruct
MESH = pl.DeviceIdType.MESH

D_MODEL = 1024
D_FF = 2816
W_A = 512
W_B = 512
HA_HEADS, HA_DIM = 4, 128
HB_HEADS, HB_DIM = 8, 64
HGRN_CHUNK = 64
HGRN_GROUP = 8
RWKV_CHUNK = 16
RWKV_GROUP = 8
N_HGRN_COLS = 4 * W_A
N_RWKV_COLS = 3 * W_B + 32 + 32 + 96
N_RWKV_PAD = 1792
LORA_PAD = 256
NORM_EPS = 1e-6
RWKV_GN_EPS = 64e-5
L2_EPS = 1e-12
ADAM_LR, ADAM_B1, ADAM_B2, ADAM_EPS, ADAM_WD, ADAM_STEP = 0.001, 0.9, 0.999, 1e-8, 0.01, 10

N_CHIPS = 4
VMEM_LIMIT_V7X = 56 * 1024 * 1024
LANES = 1024

SHARDED_SHAPES = {
    "ffn1_w_gate": ((D_MODEL, D_FF), 1), "ffn1_w_up": ((D_MODEL, D_FF), 1), "ffn1_w_down": ((D_FF, D_MODEL), 0),
    "w_in": ((D_MODEL, N_HGRN_COLS + N_RWKV_COLS), 1), "rwkv_w2": ((32, W_B), 1), "rwkv_a2": ((32, W_B), 1),
    "rwkv_g2": ((96, W_B), 1), "w_out": ((D_MODEL, D_MODEL), 0),
    "ffn2_w_gate": ((D_MODEL, D_FF), 1), "ffn2_w_up": ((D_MODEL, D_FF), 1), "ffn2_w_down": ((D_FF, D_MODEL), 0),
}
SMALL = ("ffn1_norm", "mix_norm", "hgrn_lb_logits", "hgrn_out_norm", "rwkv_shift_mu", "rwkv_w0", "rwkv_a0",
         "rwkv_k_k", "rwkv_k_a", "rwkv_r_k", "rwkv_gn_w", "rwkv_gn_b", "ffn2_norm", "final_norm")
ALL_WEIGHTS = ("ffn1_norm", "ffn1_w_gate", "ffn1_w_up", "ffn1_w_down", "mix_norm", "w_in", "hgrn_lb_logits",
               "hgrn_out_norm", "rwkv_shift_mu", "rwkv_w0", "rwkv_w2", "rwkv_a0", "rwkv_a2", "rwkv_g2", "rwkv_k_k",
               "rwkv_k_a", "rwkv_r_k", "rwkv_gn_w", "rwkv_gn_b", "w_out", "ffn2_norm", "ffn2_w_gate", "ffn2_w_up",
               "ffn2_w_down", "final_norm")


def _shard_shape(name):
    shape, ax = SHARDED_SHAPES[name]
    return tuple(s // N_CHIPS if i == ax else s for i, s in enumerate(shape))


def _numel(shape):
    n = 1
    for s in shape:
        n *= s
    return n


def _params(sem=None):
    return pltpu.CompilerParams(dimension_semantics=sem, vmem_limit_bytes=VMEM_LIMIT_V7X)


def _split2(x):
    hi = x.astype(BF16)
    return hi, (x.astype(F32) - hi.astype(F32)).astype(BF16)


def _dg(x, y, cx, cy, hi):
    dn = (((cx,), (cy,)), ((), ()))
    dot = lambda p, q: lax.dot_general(p, q, dn, preferred_element_type=F32)
    if hi == "x3":
        (xh, xl), (yh, yl) = _split2(x), _split2(y)
        return dot(xh, yh) + (dot(xh, yl) + dot(xl, yh))
    return dot(x.astype(BF16), y.astype(BF16))


def _make_mm(hi, cotangent_forms=None):
    @jax.custom_vjp
    def nn(x, y):
        return _dg(x, y, 1, 0, hi)

    @jax.custom_vjp
    def nt(x, y):
        return _dg(x, y, 1, 1, hi)

    @jax.custom_vjp
    def tn(x, y):
        return _dg(x, y, 0, 0, hi)

    bnn, bnt, btn = cotangent_forms or (nn, nt, tn)
    nn.defvjp(lambda x, y: (nn(x, y), (x, y)), lambda r, g: (bnt(g, r[1]), btn(r[0], g)))
    nt.defvjp(lambda x, y: (nt(x, y), (x, y)), lambda r, g: (bnn(g, r[1]), btn(g, r[0])))
    tn.defvjp(lambda x, y: (tn(x, y), (x, y)), lambda r, g: (bnt(r[1], g), bnn(r[0], g)))
    return nn, nt, tn


_nn, _nt, _tn = _make_mm(False)
_nn_x3, _nt_x3, _tn_x3 = _make_mm("x3", (_nn, _nt, _tn))


def _tri_apply(x, transpose):
    c = x.shape[0]
    tri = (lax.broadcasted_iota(jnp.int32, (c, c), 1) <= lax.broadcasted_iota(jnp.int32, (c, c), 0)).astype(BF16)
    dn = (((0 if transpose else 1,), (0,)), ((), ()))
    p1, p2 = _split2(x)
    dot = lambda p: lax.dot_general(tri, p, dn, preferred_element_type=F32)
    return dot(p1) + dot(p2)


@jax.custom_vjp
def _cumsum_rows(x):
    return _tri_apply(x, False)


_cumsum_rows.defvjp(lambda x: (_tri_apply(x, False), None), lambda _, g: (_tri_apply(g, True),))


def _sigmoid(x):
    return 1.0 / (1.0 + jnp.exp(-x))


def _silu(x):
    return x * _sigmoid(x)


def _softplus(z):
    return jnp.maximum(z, 0.0) + jnp.log(1.0 + jnp.exp(-jnp.abs(z)))


def _mm(a, b, *, ta=False, tb=False, tm, tn, tk, name, out_dtype=F32, res=None, scale=None, comm=None):
    m = a.shape[1] if ta else a.shape[0]
    kdim = a.shape[0] if ta else a.shape[1]
    n = b.shape[0] if tb else b.shape[1]
    assert (b.shape[1] if tb else b.shape[0]) == kdim
    tm, tn, tk = min(tm, m), min(tn, n), min(tk, kdim)
    assert m % tm == 0 and n % tn == 0 and kdim % tk == 0, (name, m, n, kdim)
    nk = kdim // tk
    a_spec = pl.BlockSpec((tk, tm), lambda i, j, k: (k, i)) if ta else pl.BlockSpec((tm, tk), lambda i, j, k: (i, k))
    b_spec = pl.BlockSpec((tn, tk), lambda i, j, k: (j, k)) if tb else pl.BlockSpec((tk, tn), lambda i, j, k: (k, j))
    o_spec = pl.BlockSpec((tm, tn), lambda i, j, k: (i, j))
    ca, cb = (0 if ta else 1), (1 if tb else 0)

    def body(*refs):
        if res is not None:
            a_ref, b_ref, r_ref, o_ref, acc_ref = refs
        else:
            a_ref, b_ref, o_ref, acc_ref = refs
        k = pl.program_id(2)

        @pl.when(k == 0)
        def _():
            acc_ref[...] = jnp.zeros_like(acc_ref)

        acc_ref[...] += _dg(a_ref[...], b_ref[...], ca, cb, False)

        @pl.when(k == nk - 1)
        def _():
            acc = acc_ref[...]
            if scale is not None:
                acc = acc * scale
            if res is not None:
                acc = r_ref[...] + acc
            o_ref[...] = acc.astype(out_dtype)

    in_specs = [a_spec, b_spec] + ([o_spec] if res is not None else [])
    args = (a, b) + ((res,) if res is not None else ())
    if comm is None:
        return pl.pallas_call(
            body, name=name, grid=(m // tm, n // tn, nk), in_specs=in_specs, out_specs=o_spec,
            out_shape=SDS((m, n), out_dtype), scratch_shapes=[pltpu.VMEM((tm, tn), F32)],
            compiler_params=_params(("parallel", "parallel", "arbitrary")))(*args)
    (out,), carried = _hosting_call(
        body, comm, name=name, grid=(m // tm, n // tn, nk), in_specs=in_specs, out_specs=[o_spec],
        out_shape=[SDS((m, n), out_dtype)], scratch_shapes=[pltpu.VMEM((tm, tn), F32)], args=args)
    return out, carried


def _mm_pair(a1, b1, a2, b2, *, tb=False, tm, name):
    m = a1.shape[0]
    n = b1.shape[0] if tb else b1.shape[1]
    tm = min(tm, m)
    cb = 1 if tb else 0
    assert a2.shape[0] == m and m % tm == 0
    assert all(b.shape[cb] == a.shape[1] and b.shape[1 - cb] == n for a, b in ((a1, b1), (a2, b2)))

    def body(a1_ref, b1_ref, a2_ref, b2_ref, o_ref):
        o_ref[...] = _dg(a1_ref[...], b1_ref[...], 1, cb, False) + _dg(a2_ref[...], b2_ref[...], 1, cb, False)

    a_spec = lambda a: pl.BlockSpec((tm, a.shape[1]), lambda i: (i, 0))
    b_spec = lambda b: pl.BlockSpec(b.shape, lambda i: (0, 0))
    return pl.pallas_call(
        body, name=name, grid=(m // tm,), in_specs=[a_spec(a1), b_spec(b1), a_spec(a2), b_spec(b2)],
        out_specs=pl.BlockSpec((tm, n), lambda i: (i, 0)), out_shape=SDS((m, n), F32),
        compiler_params=_params(("parallel",)))(a1, b1, a2, b2)


def _row_spec(x, tm, tile_of=lambda i: i):
    if isinstance(x, tuple):
        arr, w, j = x
        return arr, pl.BlockSpec((tm, w), lambda i, j=j: (tile_of(i), j))
    return x, pl.BlockSpec((tm, x.shape[1]), lambda i: (tile_of(i), 0))


def _par_spec(p):
    if isinstance(p, tuple):
        arr, w, j = p
        return arr, pl.BlockSpec((arr.shape[0], w), lambda i, j=j: (0, j))
    return p, pl.BlockSpec(p.shape, lambda i: (0, 0))


def _store_groups(refs, groups, vals):
    for ref, idxs in zip(refs, groups):
        off = 0
        for ix in idxs:
            v = vals[ix]
            ref[:, off:off + v.shape[1]] = v.astype(ref.dtype)
            off += v.shape[1]


SUBLANES = 8


def _x_plan(xs, tm, t, tile_of=lambda i: i):
    arrays, specs, plan = [], [], []
    nb = tm // SUBLANES
    for x in xs:
        if isinstance(x, tuple) and isinstance(x[0], str):
            kind, arr, w, j = x
            if kind == "prev":
                halo = lambda i, j=j: (jnp.maximum(tile_of(i) * nb - 1, 0), j)
            else:
                halo = lambda i, j=j: (jnp.minimum((tile_of(i) + 1) * nb, t // SUBLANES - 1), j)
            arrays += [arr, arr]
            specs += [pl.BlockSpec((tm, w), lambda i, j=j: (tile_of(i), j)), pl.BlockSpec((SUBLANES, w), halo)]
            plan.append((kind, 2, w))
        else:
            arr, spec = _row_spec(x, tm, tile_of)
            arrays.append(arr)
            specs.append(spec)
            plan.append(("plain", 1, spec.block_shape[1]))
    return arrays, specs, plan


def _x_vals(refs, plan, tm, nt, tile_of=lambda i: i):
    vals, k = [], 0
    i = tile_of(pl.program_id(0))
    rows = lax.broadcasted_iota(jnp.int32, (tm, 1), 0)
    for kind, n, _ in plan:
        main = refs[k][...].astype(F32)
        if kind == "prev":
            edge = jnp.where(i == 0, 0.0, refs[k + 1][SUBLANES - 1:SUBLANES, :].astype(F32))
            main = jnp.where(rows == 0, edge, pltpu.roll(main, 1, 0))
        elif kind == "next":
            edge = jnp.where(i == nt - 1, 0.0, refs[k + 1][0:1, :].astype(F32))
            main = jnp.where(rows == tm - 1, edge, pltpu.roll(main, tm - 1, 0))
        vals.append(main)
        k += n
    return vals


def _tile_rows(xs, tm):
    arr = xs[0]
    if isinstance(arr, tuple):
        arr = arr[1] if isinstance(arr[0], str) else arr[0]
    return min(tm, arr.shape[0]), arr.shape[0]


def _rowwise(f, xs, params, out_groups, out_dtypes, *, tm, name, comm=None):
    tm, t = _tile_rows(xs, tm)
    nt = t // tm
    xa, xspecs, plan = _x_plan(xs, tm, t)
    pa, pspecs = (zip(*[_par_spec(p) for p in params]) if params else ((), ()))
    nxr, npar = len(xa), len(pa)
    x_sds = [SDS((tm, w), F32) for _, _, w in plan]
    p_sds = [SDS(s.block_shape, F32) for s in pspecs]
    outs_sds = jax.eval_shape(lambda *vals: f(*vals), *x_sds, *p_sds)
    widths = [sum(outs_sds[ix].shape[1] for ix in idxs) for idxs in out_groups]

    def body(*refs):
        vals = _x_vals(refs[:nxr], plan, tm, nt) + [r[...].astype(F32) for r in refs[nxr:nxr + npar]]
        outs = f(*vals)
        _store_groups(refs[nxr + npar:], out_groups, outs)

    res, carried = _hosting_call(
        body, comm, name=name, grid=(nt,), in_specs=list(xspecs) + list(pspecs),
        out_specs=[pl.BlockSpec((tm, w), lambda i: (i, 0)) for w in widths],
        out_shape=[SDS((t, w), dt) for w, dt in zip(widths, out_dtypes)], scratch_shapes=[], args=(*xa, *pa))
    return res if comm is None else (res, carried)


def _rowwise_bwd(f, xs, params, cots, *, x_grad, p_grad, dx_groups, dx_dtypes, tm, name, extra=None, comm=None,
                 fold_next=None):
    tm, t = _tile_rows(xs, tm)
    nt = t // tm
    tile_of = (lambda i: nt - 1 - i) if fold_next else (lambda i: i)
    xa, xspecs, plan = _x_plan(xs, tm, t, tile_of)
    pa, pspecs = (zip(*[_par_spec(p) for p in params]) if params else ((), ()))
    ca, cspecs = zip(*[_row_spec(c, tm, tile_of) for c in cots])
    extra = extra or {}
    ekeys = sorted(extra)
    ea, especs = (zip(*[_row_spec(extra[k], tm, tile_of) for k in ekeys]) if ekeys else ((), ()))
    nx, nxr, npar, nc, ne = len(plan), len(xa), len(pa), len(ca), len(ea)
    gx = [i for i in range(nx) if x_grad[i]]
    gp = [i for i in range(npar) if p_grad[i]]
    all_widths = [sum(plan[gx[ix]][2] for ix in idxs) for idxs in dx_groups]
    emitted = [k for k in range(len(dx_groups)) if not (fold_next and k == fold_next[1])]
    widths = [all_widths[k] for k in emitted]
    ng = len(emitted)

    def body(*refs):
        ins = refs[:nxr + npar + nc + ne]
        outs = refs[nxr + npar + nc + ne:]
        vals = _x_vals(ins[:nxr], plan, tm, nt, tile_of) + [r[...].astype(F32) for r in ins[nxr:nxr + npar]]
        cvals = tuple(r[...].astype(F32) for r in ins[nxr + npar:nxr + npar + nc])
        evals = [r[...].astype(F32) for r in ins[nxr + npar + nc:]]
        diff_idx = gx + [nx + i for i in gp]

        def g(*dargs):
            full = list(vals)
            for ix, v in zip(diff_idx, dargs):
                full[ix] = v
            return tuple(f(*full))

        _, vjp = jax.vjp(g, *[vals[ix] for ix in diff_idx])
        grads = vjp(cvals)
        dxs = list(grads[:len(gx)])
        for k, ev in zip(ekeys, evals):
            dxs[k] = dxs[k] + ev
        _store_groups(outs[:ng], [dx_groups[k] for k in emitted], dxs)
        i = pl.program_id(0)
        if fold_next:
            main_ref, carry_ref = outs[emitted.index(fold_next[0])], refs[-1]
            rows = lax.broadcasted_iota(jnp.int32, (tm, 1), 0)
            off = 0
            for ix in dx_groups[fold_next[1]]:
                piece = dxs[ix]
                cols = slice(off, off + piece.shape[1])
                edge = jnp.where(i == 0, 0.0, carry_ref[0:1, cols])
                main_ref[:, cols] += jnp.where(rows == tm - 1, edge, pltpu.roll(piece, tm - 1, 0))
                carry_ref[:, cols] = piece[:SUBLANES]
                off += piece.shape[1]
        for ref, gval in zip(outs[ng:ng + len(gp)], grads[len(gx):]):
            @pl.when(i == 0)
            def _(ref=ref):
                ref[...] = jnp.zeros_like(ref)
            ref[...] += gval

    dp_specs = [pl.BlockSpec(pspecs[i].block_shape, lambda i: (0, 0)) for i in gp]
    dp_shapes = [SDS(pspecs[i].block_shape, F32) for i in gp]
    scratch = [pltpu.VMEM((SUBLANES, all_widths[fold_next[1]]), F32)] if fold_next else []
    res, carried = _hosting_call(
        body, comm, name=name, grid=(nt,), in_specs=list(xspecs) + list(pspecs) + list(cspecs) + list(especs),
        out_specs=[pl.BlockSpec((tm, w), lambda i: (tile_of(i), 0)) for w in widths] + dp_specs,
        out_shape=[SDS((t, w), dt) for w, dt in zip(widths, dx_dtypes)] + dp_shapes, scratch_shapes=scratch,
        args=(*xa, *pa, *ca, *ea))
    return res if comm is None else (res, carried)


def _rms_f(x, g):
    return (x * lax.rsqrt(jnp.mean(x * x, axis=-1, keepdims=True) + NORM_EPS) * g,)


def _group_sum_impl(x, ones_bd):
    p1, p2 = _split2(x)
    dot = lambda p: lax.dot_general(p, ones_bd.astype(BF16), (((1,), (0,)), ((), ())), preferred_element_type=F32)
    return dot(p1) + dot(p2)


@jax.custom_vjp
def _group_sum(x, ones_bd):
    return _group_sum_impl(x, ones_bd)


_group_sum.defvjp(lambda x, o: (_group_sum_impl(x, o), o),
                  lambda o, g: (_group_sum_impl(g, o), jnp.zeros_like(o)))


def _rwkv_prep_f(r, k, v, lo, rp, kp, vp, lop, mu_r, mu_k, mu_v, mu_lo, w0, w2p, a0, a2p, g2p, k_k, k_a, ones_bd):
    r = r + mu_r * (rp - r)
    k = k + mu_k * (kp - k)
    v = v + mu_v * (vp - v)
    lo = lo + mu_lo * (lop - lo)
    w_log = -_softplus(-(w0 + _nn(jnp.tanh(lo), w2p))) - 0.5
    lw = -jnp.exp(w_log)
    a_g = _sigmoid(a0 + _nn(lo, a2p))
    g = _nn(_sigmoid(lo), g2p)
    kk = k * k_k
    kk = kk / jnp.maximum(jnp.sqrt(_group_sum(kk * kk, ones_bd)), L2_EPS)
    k2 = k * (1.0 + (a_g - 1.0) * k_a)
    return r, lw, k2, v, -kk, kk * a_g, g


def _rwkv_post_f(y, r, k2, v, g, r_k, gn_w, gn_b, ones_bd):
    inv_n = 1.0 / HB_DIM
    mean = _group_sum(y, ones_bd) * inv_n
    yc = y - mean
    var = _group_sum(yc * yc, ones_bd) * inv_n
    yn = yc * lax.rsqrt(var + RWKV_GN_EPS) * gn_w + gn_b
    bonus = _group_sum(r * k2 * r_k, ones_bd) * v
    return ((yn + bonus) * g,)


def _tri(c, strict=False):
    ii = lax.broadcasted_iota(jnp.int32, (c, c), 0)
    jj = lax.broadcasted_iota(jnp.int32, (c, c), 1)
    return (jj < ii) if strict else (jj <= ii)


def _hgrn_step(st0, q_a, f_a, i_a, g_a, l0, l1, onorm):
    nh, nj = len(q_a), len(q_a[0])
    c = q_a[0][0].shape[0]
    combos = [(j, h) for j in range(nj) for h in range(nh)]
    every = lambda fn: {q: fn(q) for q in combos}
    at_ = lambda d: (lambda q: d[q[1]][q[0]])
    qa_, fa_, ia_, ga_ = (at_(z) for z in (q_a, f_a, i_a, g_a))
    incl = _tri(c)
    rows = lax.broadcasted_iota(jnp.int32, (c, 1), 0)
    lb = []
    for h in range(nh):
        mx = jnp.maximum(l0[h], l1[h])
        e0, e1 = jnp.exp(l0[h] - mx), jnp.exp(l1[h] - mx)
        lb.append(e0 / (e0 + e1))
    forget = every(lambda q: lb[q[1]] + (1.0 - lb[q[1]]) * _sigmoid(fa_(q)))
    qs = every(lambda q: _silu(qa_(q)))
    kk = every(lambda q: 1.0 - forget[q])
    lf = every(lambda q: jnp.log(forget[q]))
    bcum = every(lambda q: _cumsum_rows(lf[q]))
    bref = every(lambda q: jnp.sum(jnp.where(rows <= c // 2, lf[q], 0.0), axis=0, keepdims=True))
    blast = every(lambda q: jnp.sum(lf[q], axis=0, keepdims=True))
    scores = every(lambda q: jnp.where(incl, _nt(qs[q] * jnp.exp(bcum[q] - bref[q]),
                                                 kk[q] * jnp.exp(bref[q] - bcum[q])), 0.0))
    intra = every(lambda q: _nn(scores[q], ia_(q)))
    qb = every(lambda q: qs[q] * jnp.exp(bcum[q]))
    upd = every(lambda q: _tn(ia_(q), kk[q] * jnp.exp(blast[q] - bcum[q])))
    dec = every(lambda q: jnp.exp(blast[q]))
    st = list(st0)
    o = {}
    for j in range(nj):
        for h in range(nh):
            o[(j, h)] = intra[(j, h)] + _nt(qb[(j, h)], st[h])
        st = [st[h] * dec[(j, h)] + upd[(j, h)] for h in range(nh)]
    out = every(lambda q: o[q] * lax.rsqrt(jnp.mean(o[q] * o[q], axis=-1, keepdims=True) + NORM_EPS)
                * onorm[q[1]] * _silu(ga_(q)))
    return [[out[(j, h)] for j in range(nj)] for h in range(nh)], st


def _hgrn_blocks(ref, nj, c):
    return [[ref[j * c:(j + 1) * c, h * HA_DIM:(h + 1) * HA_DIM] for j in range(nj)] for h in range(HA_HEADS)]


def _hgrn_cols(ref):
    return [ref[:, h * HA_DIM:(h + 1) * HA_DIM] for h in range(HA_HEADS)]


def _hgrn_fwd(p_h, l0, l1, onorm):
    t = p_h.shape[0]
    cc, nj = HGRN_CHUNK, HGRN_GROUP
    c = cc * nj
    n = t // c

    def body(q_ref, f_ref, i_ref, g_ref, l0_ref, l1_ref, on_ref, o_ref, hs_ref, st_ref):
        @pl.when(pl.program_id(0) == 0)
        def _():
            st_ref[...] = jnp.zeros_like(st_ref)

        hs_ref[0] = st_ref[...]
        o, st1 = _hgrn_step([st_ref[h] for h in range(HA_HEADS)],
                            *[_hgrn_blocks(ref, nj, cc) for ref in (q_ref, f_ref, i_ref, g_ref)],
                            _hgrn_cols(l0_ref), _hgrn_cols(l1_ref), _hgrn_cols(on_ref))
        for h in range(HA_HEADS):
            for j in range(nj):
                o_ref[j * cc:(j + 1) * cc, h * HA_DIM:(h + 1) * HA_DIM] = o[h][j]
            st_ref[h] = st1[h]

    col = lambda j: pl.BlockSpec((c, W_A), lambda i, j=j: (i, j))
    par = pl.BlockSpec((1, W_A), lambda i: (0, 0))
    return pl.pallas_call(
        body, name="hgrn_fwd", grid=(n,), in_specs=[col(0), col(1), col(2), col(3), par, par, par],
        out_specs=[pl.BlockSpec((c, W_A), lambda i: (i, 0)),
                   pl.BlockSpec((1, HA_HEADS, HA_DIM, HA_DIM), lambda i: (i, 0, 0, 0))],
        out_shape=[SDS((t, W_A), F32), SDS((n, HA_HEADS, HA_DIM, HA_DIM), F32)],
        scratch_shapes=[pltpu.VMEM((HA_HEADS, HA_DIM, HA_DIM), F32)],
        compiler_params=_params(("arbitrary",)))(p_h, p_h, p_h, p_h, l0, l1, onorm)


def _hgrn_bwd(p_h, l0, l1, onorm, hs, do, do_col, comm=None):
    t = p_h.shape[0]
    cc, nj = HGRN_CHUNK, HGRN_GROUP
    c = cc * nj
    n = t // c

    def body(q_ref, f_ref, i_ref, g_ref, l0_ref, l1_ref, on_ref, hs_ref, do_ref,
             dp_ref, dl0_ref, dl1_ref, don_ref, dst_ref):
        @pl.when(pl.program_id(0) == 0)
        def _():
            dst_ref[...] = jnp.zeros_like(dst_ref)
            dl0_ref[...] = jnp.zeros_like(dl0_ref)
            dl1_ref[...] = jnp.zeros_like(dl1_ref)
            don_ref[...] = jnp.zeros_like(don_ref)

        args = ([hs_ref[0, h] for h in range(HA_HEADS)],
                *[_hgrn_blocks(ref, nj, cc) for ref in (q_ref, f_ref, i_ref, g_ref)],
                _hgrn_cols(l0_ref), _hgrn_cols(l1_ref), _hgrn_cols(on_ref))
        _, vjp = jax.vjp(_hgrn_step, *args)
        dst0, dq, df, di, dg, dl0, dl1, don = vjp((_hgrn_blocks(do_ref, nj, cc),
                                                   [dst_ref[h] for h in range(HA_HEADS)]))
        for h in range(HA_HEADS):
            sl = slice(h * HA_DIM, (h + 1) * HA_DIM)
            for k, dv in enumerate((dq, df, di, dg)):
                for j in range(nj):
                    dp_ref[j * cc:(j + 1) * cc, k * W_A + h * HA_DIM:k * W_A + (h + 1) * HA_DIM] = dv[h][j]
            dl0_ref[:, sl] += dl0[h]
            dl1_ref[:, sl] += dl1[h]
            don_ref[:, sl] += don[h]
            dst_ref[h] = dst0[h]

    col = lambda j: pl.BlockSpec((c, W_A), lambda i, j=j: (n - 1 - i, j))
    par = pl.BlockSpec((1, W_A), lambda i: (0, 0))
    return _hosting_call(
        body, comm, name="hgrn_bwd", grid=(n,),
        in_specs=[col(0), col(1), col(2), col(3), par, par, par,
                  pl.BlockSpec((1, HA_HEADS, HA_DIM, HA_DIM), lambda i: (n - 1 - i, 0, 0, 0)),
                  pl.BlockSpec((c, W_A), lambda i: (n - 1 - i, do_col))],
        out_specs=[pl.BlockSpec((c, N_HGRN_COLS), lambda i: (n - 1 - i, 0)), par, par, par],
        out_shape=[SDS((t, N_HGRN_COLS), F32), SDS((1, W_A), F32), SDS((1, W_A), F32), SDS((1, W_A), F32)],
        scratch_shapes=[pltpu.VMEM((HA_HEADS, HA_DIM, HA_DIM), F32)],
        args=(p_h, p_h, p_h, p_h, l0, l1, onorm, hs, do))


HB_PAIRS = HB_HEADS // 2
PAIR_W = 2 * HB_DIM


def _head_lane_masks():
    lane = lax.broadcasted_iota(jnp.int32, (1, PAIR_W), 1)
    return (lane < HB_DIM).astype(F32), (lane >= HB_DIM).astype(F32)


@jax.custom_vjp
def _stack_heads(x):
    m0, m1 = _head_lane_masks()
    return jnp.concatenate([x * m0, x * m1], axis=0)


def _stack_heads_bwd(_, g):
    m0, m1 = _head_lane_masks()
    c = g.shape[0] // 2
    return (g[:c] * m0 + g[c:] * m1,)


_stack_heads.defvjp(lambda x: (_stack_heads(x), None), _stack_heads_bwd)


@jax.custom_vjp
def _unstack_heads(ys):
    c = ys.shape[0] // 2
    return ys[:c] + ys[c:]


_unstack_heads.defvjp(lambda ys: (_unstack_heads(ys), None), lambda _, g: (_stack_heads(g),))


def _same_head_block(c):
    ii = lax.broadcasted_iota(jnp.int32, (2 * c, 2 * c), 0)
    jj = lax.broadcasted_iota(jnp.int32, (2 * c, 2 * c), 1)
    same = (ii < c) == (jj < c)
    return same & (jj <= ii), same & (jj < ii), (ii == jj).astype(F32)


@jax.custom_vjp
def _rows_join(top, bottom):
    return jnp.concatenate([top, bottom], axis=0)


def _rows_join_bwd(n_top, g):
    return g[:n_top], g[n_top:]


_rows_join.defvjp(lambda top, bottom: (_rows_join(top, bottom), top.shape[0]), _rows_join_bwd)


def _rows_split_impl(x, n_top):
    return x[:n_top], x[n_top:]


_rows_split = jax.custom_vjp(_rows_split_impl, nondiff_argnums=(1,))
_rows_split.defvjp(lambda x, n_top: (_rows_split_impl(x, n_top), None),
                   lambda n_top, _, g: (jnp.concatenate([g[0], g[1]], axis=0),))


def _rwkv_step(s0, r, lw, k, v, a, b):
    npair, nj = len(r), len(r[0])
    c = r[0][0].shape[0]
    combos = [(j, p) for j in range(nj) for p in range(npair)]
    every = lambda fn: {q: fn(q) for q in combos}
    at_ = lambda d: (lambda q: d[q[1]][q[0]])
    r_, lw_, k_, v_, a_, b_ = (at_(z) for z in (r, lw, k, v, a, b))
    incl, strict, eye = _same_head_block(c)

    gam = every(lambda q: _cumsum_rows(lw_(q)))
    gtot = every(lambda q: jnp.sum(lw_(q), axis=0, keepdims=True))
    eneg = every(lambda q: jnp.exp(-gam[q]))
    edec = every(lambda q: jnp.exp(gtot[q] - gam[q]))
    at = every(lambda q: _stack_heads(a_(q) * jnp.exp(gam[q] - lw_(q))))
    rt = every(lambda q: _stack_heads(r_(q) * jnp.exp(gam[q])))
    bt = every(lambda q: _stack_heads(b_(q) * eneg[q]))
    kt = every(lambda q: _stack_heads(k_(q) * eneg[q]))
    bdec = every(lambda q: _stack_heads(b_(q) * edec[q]))
    kdec = every(lambda q: _stack_heads(k_(q) * edec[q]))
    vs = every(lambda q: _stack_heads(v_(q)))
    a_ab = every(lambda q: jnp.where(strict, _nt(at[q], bt[q]), 0.0))
    a_ak = every(lambda q: jnp.where(strict, _nt(at[q], kt[q]), 0.0))
    a_rb = every(lambda q: jnp.where(incl, _nt(rt[q], bt[q]), 0.0))
    a_rk = every(lambda q: jnp.where(incl, _nt(rt[q], kt[q]), 0.0))
    tinv = every(lambda q: eye + a_ab[q])
    pw = a_ab
    span = 2
    while span < c:
        pw = every(lambda q, pw=pw: _nn_x3(pw[q], pw[q]))
        tinv = every(lambda q, pw=pw, tinv=tinv: tinv[q] + _nn_x3(pw[q], tinv[q]))
        span *= 2
    akv = every(lambda q: _nn(a_ak[q], vs[q]))
    w1 = every(lambda q: _nn(tinv[q], at[q]))
    u0 = every(lambda q: _nn(tinv[q], akv[q]))
    wr = every(lambda q: _rows_join(w1[q], rt[q]))
    bk = every(lambda q: _rows_join(bdec[q], kdec[q]))
    yv = every(lambda q: _nn(a_rk[q], vs[q]))
    gdec = every(lambda q: jnp.exp(gtot[q]))

    s = list(s0)
    y = [[None] * nj for _ in range(npair)]
    for j in range(nj):
        both = {p: _rows_split(_nt(wr[(j, p)], s[p]), 2 * c) for p in range(npair)}
        u = {p: both[p][0] + u0[(j, p)] for p in range(npair)}
        for p in range(npair):
            y[p][j] = _unstack_heads(both[p][1] + _nn(a_rb[(j, p)], u[p]) + yv[(j, p)])
        s = [s[p] * gdec[(j, p)] + _tn(_rows_join(u[p], vs[(j, p)]), bk[(j, p)]) for p in range(npair)]
    return y, s


def _rwkv_blocks(ref, nj, c):
    return [[ref[j * c:(j + 1) * c, p * PAIR_W:(p + 1) * PAIR_W] for j in range(nj)] for p in range(HB_PAIRS)]


def _rwkv_fwd(seqs, comm=None):
    t = seqs[0].shape[0]
    c, nj = RWKV_CHUNK, RWKV_GROUP
    n = t // (c * nj)

    def body(r_ref, lw_ref, k_ref, v_ref, a_ref, b_ref, y_ref, hs_ref, st_ref):
        @pl.when(pl.program_id(0) == 0)
        def _():
            st_ref[...] = jnp.zeros_like(st_ref)

        hs_ref[0] = st_ref[...]
        s0 = [st_ref[p] for p in range(HB_PAIRS)]
        y, s1 = _rwkv_step(s0, *[_rwkv_blocks(ref, nj, c) for ref in (r_ref, lw_ref, k_ref, v_ref, a_ref, b_ref)])
        for p in range(HB_PAIRS):
            for j in range(nj):
                y_ref[j * c:(j + 1) * c, p * PAIR_W:(p + 1) * PAIR_W] = y[p][j]
            st_ref[p] = s1[p]

    seq = pl.BlockSpec((c * nj, W_B), lambda i: (i, 0))
    return _hosting_call(
        body, comm, name="rwkv_fwd", grid=(n,), in_specs=[seq] * 6,
        out_specs=[seq, pl.BlockSpec((1, HB_PAIRS, PAIR_W, PAIR_W), lambda i: (i, 0, 0, 0))],
        out_shape=[SDS((t, W_B), F32), SDS((n, HB_PAIRS, PAIR_W, PAIR_W), F32)],
        scratch_shapes=[pltpu.VMEM((HB_PAIRS, PAIR_W, PAIR_W), F32)], args=tuple(seqs))


def _rwkv_bwd(seqs, hs, dy, comm=None):
    t = seqs[0].shape[0]
    c, nj = RWKV_CHUNK, RWKV_GROUP
    n = t // (c * nj)

    def body(r_ref, lw_ref, k_ref, v_ref, a_ref, b_ref, hs_ref, dy_ref,
             dr_ref, dlw_ref, dk_ref, dv_ref, da_ref, db_ref, dst_ref):
        @pl.when(pl.program_id(0) == 0)
        def _():
            dst_ref[...] = jnp.zeros_like(dst_ref)

        s0 = [hs_ref[0, p] for p in range(HB_PAIRS)]
        seq_vals = [_rwkv_blocks(ref, nj, c) for ref in (r_ref, lw_ref, k_ref, v_ref, a_ref, b_ref)]
        _, vjp = jax.vjp(_rwkv_step, s0, *seq_vals)
        grads = vjp((_rwkv_blocks(dy_ref, nj, c), [dst_ref[p] for p in range(HB_PAIRS)]))
        for ref, gr in zip((dr_ref, dlw_ref, dk_ref, dv_ref, da_ref, db_ref), grads[1:]):
            for p in range(HB_PAIRS):
                for j in range(nj):
                    ref[j * c:(j + 1) * c, p * PAIR_W:(p + 1) * PAIR_W] = gr[p][j]
        m0, m1 = _head_lane_masks()
        rows0 = (lax.broadcasted_iota(jnp.int32, (PAIR_W, 1), 0) < HB_DIM).astype(F32)
        blocks = rows0 * m0 + (1.0 - rows0) * m1
        for p in range(HB_PAIRS):
            dst_ref[p] = grads[0][p] * blocks

    seq = pl.BlockSpec((c * nj, W_B), lambda i: (n - 1 - i, 0))
    return _hosting_call(
        body, comm, name="rwkv_bwd", grid=(n,),
        in_specs=[seq] * 6 + [pl.BlockSpec((1, HB_PAIRS, PAIR_W, PAIR_W), lambda i: (n - 1 - i, 0, 0, 0)), seq],
        out_specs=[seq] * 6, out_shape=[SDS((t, W_B), F32)] * 6,
        scratch_shapes=[pltpu.VMEM((HB_PAIRS, PAIR_W, PAIR_W), F32)], args=(*seqs, hs, dy))


def _final_loss(x3, fnorm, target, *, tm):
    t, d = x3.shape

    def body(x_ref, g_ref, t_ref, dx_ref, dg_ref, loss_ref):
        @pl.when(pl.program_id(0) == 0)
        def _():
            dg_ref[...] = jnp.zeros_like(dg_ref)
            loss_ref[...] = jnp.zeros_like(loss_ref)

        x, g = x_ref[...], g_ref[...]
        rinv = lax.rsqrt(jnp.mean(x * x, axis=-1, keepdims=True) + NORM_EPS)
        xh = x * rinv
        diff = xh * g - t_ref[...]
        loss_ref[...] += 0.5 * jnp.sum(jnp.mean(diff * diff, axis=-1, keepdims=True))
        dy = diff * (1.0 / d)
        dg_ref[...] += jnp.sum(dy * xh, axis=0, keepdims=True)
        dxh = dy * g
        dx_ref[...] = rinv * (dxh - xh * jnp.mean(dxh * xh, axis=-1, keepdims=True))

    row = pl.BlockSpec((tm, d), lambda i: (i, 0))
    return pl.pallas_call(
        body, name="final_loss", grid=(t // tm,), in_specs=[row, pl.BlockSpec((1, d), lambda i: (0, 0)), row],
        out_specs=[row, pl.BlockSpec((1, d), lambda i: (0, 0)), pl.BlockSpec((8, 128), lambda i: (0, 0))],
        out_shape=[SDS((t, d), F32), SDS((1, d), F32), SDS((8, 128), F32)],
        compiler_params=_params(("arbitrary",)))(x3, fnorm, target)


def _gate_up_act(h, wgt, wut, *, tm, tn, name, comm=None):
    t, d = h.shape
    tm = min(tm, t)

    def body(h_ref, g_ref, u_ref, a_out, u_out, act_out):
        hv = h_ref[...]
        a = _dg(hv, g_ref[...], 1, 1, False)
        u = _dg(hv, u_ref[...], 1, 1, False)
        a_out[...] = a.astype(a_out.dtype)
        u_out[...] = u.astype(u_out.dtype)
        act_out[...] = (_silu(a) * u).astype(act_out.dtype)

    wspec = pl.BlockSpec((tn, d), lambda i, j: (j, 0))
    ospec = pl.BlockSpec((tm, tn), lambda i, j: (i, j))
    return _hosting_call(
        body, comm, name=name, grid=(t // tm, D_FF // tn),
        in_specs=[pl.BlockSpec((tm, d), lambda i, j: (i, 0)), wspec, wspec], out_specs=[ospec, ospec, ospec],
        out_shape=[SDS((t, D_FF), BF16), SDS((t, D_FF), BF16), SDS((t, D_FF), BF16)], scratch_shapes=[],
        args=(h, wgt, wut))


def _dact_swiglu(dout, wd, a, u, *, tm, tn, name, comm=None):
    t, d = dout.shape
    tm = min(tm, t)

    def body(d_ref, w_ref, a_ref, u_ref, da_out, du_out):
        dact = 0.5 * _dg(d_ref[...], w_ref[...], 1, 1, False)
        av, uv = a_ref[...].astype(F32), u_ref[...].astype(F32)
        s = _sigmoid(av)
        da_out[...] = (dact * uv * (s * (1.0 + av * (1.0 - s)))).astype(da_out.dtype)
        du_out[...] = (dact * (av * s)).astype(du_out.dtype)

    tile = pl.BlockSpec((tm, tn), lambda i, j: (i, j))
    return _hosting_call(
        body, comm, name=name, grid=(t // tm, D_FF // tn),
        in_specs=[pl.BlockSpec((tm, d), lambda i, j: (i, 0)), pl.BlockSpec((tn, d), lambda i, j: (j, 0)), tile, tile],
        out_specs=[tile, tile], out_shape=[SDS((t, D_FF), BF16), SDS((t, D_FF), BF16)], scratch_shapes=[],
        args=(dout, wd, a, u))


class _Plan:
    def __init__(self):
        self.entries, self.counts = collections.defaultdict(list), {}

    def carry(self, host, comm_of, after):
        self.entries[host].append((comm_of, after))

    def comm(self, host, g):
        comms = [comm_of(g) for comm_of, _ in self.entries.get(host, [])]
        self.counts[host] = [len(c.arrays) for c in comms]
        return functools.reduce(_join_comms, comms) if comms else None

    def done(self, host, results, w):
        start = 0
        for (_, after), n in zip(self.entries.get(host, []), self.counts.get(host, [])):
            after(results[start:start + n], w)
            start += n


def _ffn_fwd(x, w, tag, plan, g):
    comm = plan.comm(f"{tag}_rms", g)
    res = _rowwise(_rms_f, [x], [w[f"{tag}_norm"]], [[0]], [BF16], tm=1024, name=f"{tag}_rms", comm=comm)
    (h,), carried = res if comm is not None else (res, [])
    plan.done(f"{tag}_rms", carried, w)
    (a, u, act), carried = _gate_up_act(h, w[f"{tag}_wgt"], w[f"{tag}_wut"], tm=2048, tn=256, name=f"{tag}_gate_up",
                                        comm=plan.comm(f"{tag}_gate_up", g))
    plan.done(f"{tag}_gate_up", carried, w)
    comm = plan.comm(f"{tag}_down", g)
    out = _mm(act, w[f"{tag}_wd"], tm=1024, tn=D_MODEL, tk=D_FF, name=f"{tag}_down", res=x, scale=0.5, comm=comm)
    if comm is not None:
        out, carried = out
        plan.done(f"{tag}_down", carried, w)
    return out, (h, a, u, act)


def _ffn_bwd(dout, x, w, saved, tag, plan, g):
    h, a, u, act = saved

    def carrying(fn, host, *args, **kwargs):
        comm = plan.comm(host, g)
        res = fn(*args, name=host, comm=comm, **kwargs)
        out, carried = res if comm is not None else (res, [])
        plan.done(host, carried, w)
        return out

    (da, du), carried = _dact_swiglu(dout, w[f"{tag}_wd"], a, u, tm=2048, tn=256, name=f"{tag}_dact",
                                     comm=plan.comm(f"{tag}_dact", g))
    plan.done(f"{tag}_dact", carried, w)
    g[f"{tag}_wd"] = _mm(act, dout, ta=True, tm=D_FF // 2, tn=D_MODEL, tk=1024, name=f"{tag}_dwd", scale=0.5)
    g[f"{tag}_wgt"] = carrying(_mm, f"{tag}_dwg", da, h, ta=True, tm=D_FF // 2, tn=D_MODEL, tk=1024)
    g[f"{tag}_wut"] = carrying(_mm, f"{tag}_dwu", du, h, ta=True, tm=D_FF // 2, tn=D_MODEL, tk=1024)
    if plan.entries.get(f"{tag}_dh_g") or plan.entries.get(f"{tag}_dh_u"):
        dh = carrying(_mm, f"{tag}_dh_g", da, w[f"{tag}_wgt"], tm=1024, tn=D_MODEL, tk=D_FF)
        dh = carrying(_mm, f"{tag}_dh_u", du, w[f"{tag}_wut"], tm=1024, tn=D_MODEL, tk=D_FF, res=dh)
    else:
        dh = _mm_pair(da, w[f"{tag}_wgt"], du, w[f"{tag}_wut"], tm=512, name=f"{tag}_dh")
    dx, g[f"{tag}_norm"] = carrying(_rowwise_bwd, f"{tag}_drms", _rms_f, [x], [w[f"{tag}_norm"]], [dh],
                                    x_grad=[True], p_grad=[True], dx_groups=[[0]], dx_dtypes=[F32], tm=512,
                                    extra={0: dout})
    return dx


def _local_step(x, target, w, plan=None):
    plan = plan or _Plan()
    ones_bd = jnp.kron(jnp.eye(HB_HEADS, dtype=F32), jnp.ones((HB_DIM, HB_DIM), F32))
    g = {}
    x1, ffn1_saved = _ffn_fwd(x, w, "ffn1", plan, g)
    hm, = _rowwise(_rms_f, [x1], [w["mix_norm"]], [[0]], [BF16], tm=1024, name="mix_rms")
    p_h = _mm(hm, w["w_in_h"], tm=2048, tn=256, tk=D_MODEL, name="inproj_h")
    p_r = _mm(hm, w["w_in_r"], tm=2048, tn=256, tk=D_MODEL, name="inproj_r")
    o_a, hgrn_states = _hgrn_fwd(p_h, w["lb0"], w["lb1"], w["hgrn_out_norm"])

    mu = w["mu_pad"]
    prep_xs = [(p_r, W_B, 0), (p_r, W_B, 1), (p_r, W_B, 2), (p_r, LORA_PAD, 6),
               ("prev", p_r, W_B, 0), ("prev", p_r, W_B, 1), ("prev", p_r, W_B, 2), ("prev", p_r, LORA_PAD, 6)]
    prep_ps = [(mu, W_B, 0), (mu, W_B, 1), (mu, W_B, 2), (mu, LORA_PAD, 6), w["rwkv_w0"], w["w2_pad"], w["rwkv_a0"],
               w["a2_pad"], w["g2_pad"], w["rwkv_k_k"], w["rwkv_k_a"], ones_bd]
    prep_f = _rwkv_prep_f
    r, lw, k2, v, a_vec, b_vec, gate = _rowwise(prep_f, prep_xs, prep_ps, [[0], [1], [2], [3], [4], [5], [6]],
                                                [F32] * 7, tm=512, name="rwkv_prep")
    seqs = [r, lw, k2, v, a_vec, b_vec]
    (y, rwkv_states), carried = _rwkv_fwd(seqs, comm=plan.comm("rwkv_fwd", g))
    plan.done("rwkv_fwd", carried, w)
    post_f = _rwkv_post_f
    post_xs = [y, r, k2, v, gate]
    post_ps = [w["rwkv_r_k"], w["rwkv_gn_w"], w["rwkv_gn_b"], ones_bd]
    o, = _rowwise(lambda o_a_, *rest: (o_a_,) + tuple(post_f(*rest)), [o_a] + post_xs, post_ps, [[0, 1]], [F32],
                  tm=512, name="rwkv_post")
    x2 = _mm(o, w["w_out"], tm=2048, tn=256, tk=D_MODEL, name="outproj", res=x1)
    x3, ffn2_saved = _ffn_fwd(x2, w, "ffn2", plan, g)
    dx3, g["final_norm"], loss = _final_loss(x3, w["final_norm"], target, tm=512)

    dx2 = _ffn_bwd(dx3, x2, w, ffn2_saved, "ffn2", plan, g)
    do = _mm(dx2, w["w_out"], tb=True, tm=2048, tn=256, tk=D_MODEL, name="outproj_do")
    g["w_out"] = _mm(o, dx2, ta=True, tm=D_MODEL, tn=D_MODEL, tk=1024, name="outproj_dw")

    (dp_h, g["lb0"], g["lb1"], g["hgrn_out_norm"]), carried = _hgrn_bwd(
        p_h, w["lb0"], w["lb1"], w["hgrn_out_norm"], hgrn_states, do, 0, comm=plan.comm("hgrn_bwd", g))
    plan.done("hgrn_bwd", carried, w)
    post_out = _rowwise_bwd(post_f, post_xs, post_ps, [(do, W_B, 1)], x_grad=[True] * 5, p_grad=[True] * 3 + [False],
                            dx_groups=[[0], [1], [2], [3], [4]], dx_dtypes=[F32] * 5, tm=256, name="rwkv_post_bwd")
    dy, dr1, dk1, dv1, dgate, g["rwkv_r_k"], g["rwkv_gn_w"], g["rwkv_gn_b"] = post_out
    (dr2, dlw, dk2, dv2, da_vec, db_vec), carried = _rwkv_bwd(seqs, rwkv_states, dy, comm=plan.comm("rwkv_bwd", g))
    plan.done("rwkv_bwd", carried, w)

    def prep2_f(*vals):
        r_, lw_, k2_, v_, a_, b_, g_ = prep_f(*vals)
        return r_, lw_, k2_, v_, a_, b_, g_, r_, k2_, v_

    prep_comm = plan.comm("rwkv_prep_bwd", g)
    prep_out = _rowwise_bwd(prep2_f, prep_xs, prep_ps, [dr2, dlw, dk2, dv2, da_vec, db_vec, dgate, dr1, dk1, dv1],
                            x_grad=[True] * 8, p_grad=[True] * 11 + [False], dx_groups=[[0, 1, 2, 3], [4, 5, 6, 7]],
                            dx_dtypes=[F32], tm=256, name="rwkv_prep_bwd", fold_next=(0, 1), comm=prep_comm)
    prep_out, carried = prep_out if prep_comm is not None else (prep_out, [])
    plan.done("rwkv_prep_bwd", carried, w)
    dp_r = prep_out[0]
    (dmu_r, dmu_k, dmu_v, dmu_lo, g["rwkv_w0"], g["w2_pad"], g["rwkv_a0"], g["a2_pad"], g["g2_pad"],
     g["rwkv_k_k"], g["rwkv_k_a"]) = prep_out[1:]
    g["mu_pad"] = jnp.concatenate([dmu_r, dmu_k, dmu_v, dmu_lo], axis=1)
    dhm = _mm_pair(dp_h, w["w_in_h"], dp_r, w["w_in_r"], tb=True, tm=512, name="inproj_dh")
    g["w_in_h"] = _mm(hm, dp_h, ta=True, tm=D_MODEL, tn=D_MODEL, tk=1024, name="inproj_dw_h")
    g["w_in_r"] = _mm(hm, dp_r, ta=True, tm=D_MODEL, tn=N_RWKV_PAD // 2, tk=1024, name="inproj_dw_r")
    mix_comm = plan.comm("mix_drms", g)
    mix_out = _rowwise_bwd(_rms_f, [x1], [w["mix_norm"]], [dhm], x_grad=[True], p_grad=[True], dx_groups=[[0]],
                           dx_dtypes=[F32], tm=512, name="mix_drms", extra={0: dx2}, comm=mix_comm)
    (dx1, g["mix_norm"]), carried = mix_out if mix_comm is not None else (mix_out, [])
    plan.done("mix_drms", carried, w)
    dx0 = _ffn_bwd(dx1, x, w, ffn1_saved, "ffn1", plan, g)
    return loss, dx0, g


HBM_SPEC = pl.BlockSpec(memory_space=pl.ANY)

Comm = collections.namedtuple("Comm", "arrays out_shapes aliased sem_shapes start finish")


def _join_comms(first, second):
    n, s = len(first.arrays), len(first.sem_shapes)

    def start(ins, outs, sems):
        first.start(ins[:n], outs[:n], sems[:s])
        second.start(ins[n:], outs[n:], sems[s:])

    def finish(ins, outs, sems):
        first.finish(ins[:n], outs[:n], sems[:s])
        second.finish(ins[n:], outs[n:], sems[s:])

    return Comm(list(first.arrays) + list(second.arrays), list(first.out_shapes) + list(second.out_shapes),
                list(first.aliased) + list(second.aliased), list(first.sem_shapes) + list(second.sem_shapes),
                start, finish)


def _run_comm(comm, name):
    n = len(comm.arrays)

    def body(*refs):
        ins, outs, sems = refs[:n], refs[n:2 * n], refs[2 * n:]
        comm.start(ins, outs, sems)
        comm.finish(ins, outs, sems)

    return pl.pallas_call(
        body, name=name, in_specs=[HBM_SPEC] * n, out_specs=[HBM_SPEC] * n, out_shape=list(comm.out_shapes),
        input_output_aliases={t: t for t in range(n) if comm.aliased[t]},
        scratch_shapes=list(comm.sem_shapes))(*comm.arrays)


def _hosting_call(body, comm, *, name, grid, in_specs, out_specs, out_shape, scratch_shapes, args):
    sem = ("arbitrary",) * len(grid)
    if comm is None:
        res = pl.pallas_call(body, name=name, grid=grid, in_specs=in_specs, out_specs=out_specs, out_shape=out_shape,
                             scratch_shapes=scratch_shapes, compiler_params=_params(sem))(*args)
        return list(res), []
    ni, no, ns, nc = len(in_specs), len(out_specs), len(scratch_shapes), len(comm.arrays)

    def wrapped(*refs):
        ins, cins = refs[:ni], refs[ni:ni + nc]
        outs, couts = refs[ni + nc:ni + nc + no], refs[ni + nc + no:ni + 2 * nc + no]
        scr, sems = refs[ni + 2 * nc + no:ni + 2 * nc + no + ns], refs[ni + 2 * nc + no + ns:]
        first = functools.reduce(jnp.logical_and, [pl.program_id(k) == 0 for k in range(len(grid))])
        last = functools.reduce(jnp.logical_and, [pl.program_id(k) == grid[k] - 1 for k in range(len(grid))])

        @pl.when(first)
        def _():
            comm.start(cins, couts, sems)

        body(*ins, *outs, *scr)

        @pl.when(last)
        def _():
            comm.finish(cins, couts, sems)

    res = pl.pallas_call(
        wrapped, name=name, grid=grid, in_specs=list(in_specs) + [HBM_SPEC] * nc,
        out_specs=list(out_specs) + [HBM_SPEC] * nc, out_shape=list(out_shape) + list(comm.out_shapes),
        scratch_shapes=list(scratch_shapes) + list(comm.sem_shapes),
        input_output_aliases={ni + t: no + t for t in range(nc) if comm.aliased[t]},
        compiler_params=_params(sem))(*args, *comm.arrays)
    return list(res[:no]), list(res[no:])


def _chips(x, y):
    return [(1 - x, y), (x, 1 - y), (1 - x, 1 - y)]


def _gather_comm(bufs):
    n = len(bufs)

    def copies(outs, sems):
        ici_send, ici_recv, d2d_send, d2d_recv = sems
        x, y, c = lax.axis_index("x"), lax.axis_index("y"), lax.axis_index("c")

        def half(t, slot, hc):
            hr = bufs[t].shape[1] // 2
            return outs[t].at[slot, pl.ds(pl.multiple_of(hc * hr, 16), hr), :]

        def ici(t, j, slot, px, py):
            return pltpu.make_async_remote_copy(src_ref=half(t, slot, c), dst_ref=half(t, slot, c),
                                                send_sem=ici_send.at[3 * t + j], recv_sem=ici_recv.at[3 * t + j],
                                                device_id=(px, py, c), device_id_type=MESH)

        def d2d(t, j, slot, hc):
            return pltpu.make_async_remote_copy(src_ref=half(t, slot, hc), dst_ref=half(t, slot, hc),
                                                send_sem=d2d_send.at[3 * t + j], recv_sem=d2d_recv.at[3 * t + j],
                                                device_id=(x, y, 1 - c), device_id_type=MESH)

        peers = [(t, j, px, py) for t in range(n) for j, (px, py) in enumerate(_chips(x, y))]
        return ici, d2d, peers, 2 * x + y, c

    def start(ins, outs, sems):
        ici, _, peers, me, _ = copies(outs, sems)
        for t, j, px, py in peers:
            ici(t, j, me, px, py).start()

    def finish(ins, outs, sems):
        ici, d2d, peers, me, c = copies(outs, sems)
        for t, j, px, py in peers:
            ici(t, j, 2 * px + py, px, py).wait_recv()
            d2d(t, j, 2 * px + py, c).start()
        for t, j, px, py in peers:
            d2d(t, j, 2 * px + py, 1 - c).wait_recv()
        for t, j, px, py in peers:
            ici(t, j, me, px, py).wait_send()
            d2d(t, j, 2 * px + py, c).wait_send()

    return Comm(list(bufs), [SDS(b.shape, b.dtype) for b in bufs], [True] * n,
                [pltpu.SemaphoreType.DMA((3 * n,))] * 4, start, finish)


def _sibling_exchange_comm(gs):
    n = len(gs)

    def copies(ins, outs, sems):
        x, y, c = lax.axis_index("x"), lax.axis_index("y"), lax.axis_index("c")
        cps = []
        for t in range(n):
            hr = gs[t].shape[1] // 2
            src = ins[t].at[:, pl.ds(pl.multiple_of((1 - c) * hr, SUBLANES), hr), :]
            cps.append(pltpu.make_async_remote_copy(src_ref=src, dst_ref=outs[t], send_sem=sems[0].at[t],
                                                    recv_sem=sems[1].at[t], device_id=(x, y, 1 - c),
                                                    device_id_type=MESH))
        return cps

    def start(ins, outs, sems):
        for cp in copies(ins, outs, sems):
            cp.start()

    def finish(ins, outs, sems):
        for cp in copies(ins, outs, sems):
            cp.wait()

    return Comm(list(gs), [SDS((N_CHIPS, g.shape[1] // 2, g.shape[2]), g.dtype) for g in gs], [False] * n,
                [pltpu.SemaphoreType.DMA((n,))] * 2, start, finish)


def _own_rows(c, rows):
    return pl.ds(pl.multiple_of(c * (rows // 2), 16), rows // 2)


def _chip_exchange_comm(ss):
    n = len(ss)

    def copies(ins, outs, sems):
        x, y, c = lax.axis_index("x"), lax.axis_index("y"), lax.axis_index("c")
        me = 2 * x + y

        def copy(t, j, px, py, src_slot, dst_slot):
            rows = _own_rows(c, ss[t].shape[1])
            return pltpu.make_async_remote_copy(src_ref=ins[t].at[src_slot, rows, :],
                                                dst_ref=outs[t].at[dst_slot, rows, :],
                                                send_sem=sems[0].at[3 * t + j], recv_sem=sems[1].at[3 * t + j],
                                                device_id=(px, py, c), device_id_type=MESH)

        peers = [(t, j, px, py) for t in range(n) for j, (px, py) in enumerate(_chips(x, y))]
        return copy, peers, me

    def start(ins, outs, sems):
        copy, peers, me = copies(ins, outs, sems)
        for t, j, px, py in peers:
            copy(t, j, px, py, 2 * px + py, me).start()

    def finish(ins, outs, sems):
        copy, peers, me = copies(ins, outs, sems)
        for t, j, px, py in peers:
            copy(t, j, px, py, me, 2 * px + py).wait_recv()
        for t, j, px, py in peers:
            copy(t, j, px, py, 2 * px + py, me).wait_send()

    return Comm(list(ss), [SDS(s.shape, s.dtype) for s in ss], [False] * n,
                [pltpu.SemaphoreType.DMA((3 * n,))] * 2, start, finish)


def _sibling_swap_comm(rs, ss):
    n = len(rs)

    def copies(outs, sems):
        x, y, c = lax.axis_index("x"), lax.axis_index("y"), lax.axis_index("c")
        cps = []
        for t in range(n):
            rows = _own_rows(c, rs[t].shape[1])
            held = [outs[t].at[2 * px + py, rows, :] for px, py in _chips(x, y)] + [outs[n + t].at[2 * x + y, rows, :]]
            cps += [pltpu.make_async_remote_copy(src_ref=ref, dst_ref=ref, send_sem=sems[0].at[4 * t + j],
                                                 recv_sem=sems[1].at[4 * t + j], device_id=(x, y, 1 - c),
                                                 device_id_type=MESH) for j, ref in enumerate(held)]
        return cps

    def start(ins, outs, sems):
        for cp in copies(outs, sems):
            cp.start()

    def finish(ins, outs, sems):
        for cp in copies(outs, sems):
            cp.wait()

    both = list(rs) + list(ss)
    return Comm(both, [SDS(b.shape, b.dtype) for b in both], [True] * (2 * n),
                [pltpu.SemaphoreType.DMA((4 * n,))] * 2, start, finish)


def _row_tile(rows, cap=512):
    best = SUBLANES
    for tr in range(SUBLANES, min(rows, cap) + 1, SUBLANES):
        if rows % tr == 0:
            best = tr
    return best


def _add_halves(g4, r4, c_idx, name):
    _, hr, lanes = r4.shape
    tr = _row_tile(hr)
    nb = hr // tr

    def body(c_ref, a_ref, b_ref, o_ref):
        o_ref[...] = (a_ref[...] + b_ref[...]).astype(o_ref.dtype)

    pair = 2
    owned = pl.BlockSpec((pair, tr, lanes), lambda q, i, c_ref: (q, c_ref[0] * nb + i, 0))
    grid_spec = pltpu.PrefetchScalarGridSpec(
        num_scalar_prefetch=1, grid=(N_CHIPS // pair, nb),
        in_specs=[owned, pl.BlockSpec((pair, tr, lanes), lambda q, i, c_ref: (q, i, 0))], out_specs=owned)
    return pl.pallas_call(body, name=name, grid_spec=grid_spec, out_shape=SDS(g4.shape, BF16),
                          compiler_params=_params(("parallel", "parallel")))(c_idx, g4, r4)


def _adamw(wf, r4, s4, mf, vf, me_idx, name):
    rows, lanes = wf.shape
    tr = rows // 2
    assert tr % 16 == 0
    c1 = 1.0 / (1.0 - ADAM_B1 ** ADAM_STEP)
    c2 = 1.0 / (1.0 - ADAM_B2 ** ADAM_STEP)

    def body(me_ref, w_ref, a_ref, b_ref, c_ref, d_ref, own_ref, m_ref, v_ref, g_ref, delta_ref, nm_ref, nv_ref):
        own = own_ref[...].astype(F32)
        p = [jnp.where(me_ref[0] == q, own, ref[...].astype(F32)) for q, ref in enumerate((a_ref, b_ref, c_ref, d_ref))]
        gv = ((p[0] + p[1]) + p[2]) + p[3]
        m = ADAM_B1 * m_ref[...] + (1.0 - ADAM_B1) * gv
        v = ADAM_B2 * v_ref[...] + (1.0 - ADAM_B2) * (gv * gv)
        g_ref[...] = gv
        delta_ref[...] = -ADAM_LR * ((m * c1) / (jnp.sqrt(v * c2) + ADAM_EPS) + ADAM_WD * w_ref[...])
        nm_ref[...] = m
        nv_ref[...] = v

    other = lambda q: (lambda i, me_ref: (jnp.where(me_ref[0] == q, (q + 1) % N_CHIPS, q), i, 0))
    full = pl.BlockSpec((tr, lanes), lambda i, me_ref: (i, 0))
    grid_spec = pltpu.PrefetchScalarGridSpec(
        num_scalar_prefetch=1, grid=(rows // tr,),
        in_specs=[full] + [pl.BlockSpec((None, tr, lanes), other(q)) for q in range(N_CHIPS)]
        + [pl.BlockSpec((None, tr, lanes), lambda i, me_ref: (me_ref[0], i, 0)), full, full],
        out_specs=[full] * 4)
    return pl.pallas_call(body, name=name, grid_spec=grid_spec, out_shape=[SDS((rows, lanes), F32)] * 4,
                          compiler_params=_params(("parallel",)))(me_idx, wf, r4, r4, r4, r4, s4, mf, vf)


BIG = ("ffn1_w_gate", "ffn1_w_up", "ffn1_w_down", "ffn2_w_gate", "ffn2_w_up", "ffn2_w_down", "w_out", "w_in")
TRANSPOSED = ("ffn1_w_gate", "ffn1_w_up", "ffn2_w_gate", "ffn2_w_up")
PACKED = ("rwkv_w2", "rwkv_a2", "rwkv_g2")
SMALL_SHAPES = {"ffn1_norm": (1, D_MODEL), "mix_norm": (1, D_MODEL), "hgrn_lb_logits": (2, W_A),
                "hgrn_out_norm": (1, W_A), "rwkv_shift_mu": (1, N_RWKV_COLS), "rwkv_w0": (1, W_B),
                "rwkv_a0": (1, W_B), "rwkv_k_k": (1, W_B), "rwkv_k_a": (1, W_B),
                "rwkv_r_k": (1, HB_HEADS, HB_DIM), "rwkv_gn_w": (1, W_B), "rwkv_gn_b": (1, W_B),
                "ffn2_norm": (1, D_MODEL), "final_norm": (D_MODEL,)}
PACK_ELEMS = sum(_numel(_shard_shape(n)) for n in PACKED) + sum(_numel(SMALL_SHAPES[n]) for n in SMALL)
PACK_ROWS = -(-PACK_ELEMS // (32 * LANES)) * 32


def _to_rows(name, shard):
    return shard[0].T if name in TRANSPOSED else shard[0]


def _from_rows(name, rows):
    return (rows.T if name in TRANSPOSED else rows)[None]


def _pack(sharded, small):
    flat = jnp.concatenate([sharded[n].reshape(-1) for n in PACKED] + [small[n].reshape(-1) for n in SMALL])
    return jnp.pad(flat, (0, PACK_ROWS * LANES - flat.shape[0])).reshape(PACK_ROWS, LANES)


def _unpack(packed):
    flat, out, off = packed.reshape(-1), {}, 0
    for n in PACKED:
        shp = _shard_shape(n)
        out[n] = flat[off:off + _numel(shp)].reshape((1,) + shp)
        off += _numel(shp)
    for n in SMALL:
        shp = SMALL_SHAPES[n]
        out[n] = flat[off:off + _numel(shp)].reshape(shp)
        off += _numel(shp)
    return out


def _quarter(full, name, q):
    shape, ax = SHARDED_SHAPES[name]
    w = shape[ax] // N_CHIPS
    return lax.slice_in_dim(full, q * w, (q + 1) * w, axis=ax)


def kernel(x, ffn1_norm, ffn1_w_gate, ffn1_w_up, ffn1_w_down, mix_norm, w_in, hgrn_lb_logits, hgrn_out_norm, rwkv_shift_mu, rwkv_w0, rwkv_w2, rwkv_a0, rwkv_a2, rwkv_g2, rwkv_k_k, rwkv_k_a, rwkv_r_k, rwkv_gn_w, rwkv_gn_b, w_out, ffn2_norm, ffn2_w_gate, ffn2_w_up, ffn2_w_down, final_norm, loss_target, m_ffn1_norm, m_ffn1_w_gate, m_ffn1_w_up, m_ffn1_w_down, m_mix_norm, m_w_in, m_hgrn_lb_logits, m_hgrn_out_norm, m_rwkv_shift_mu, m_rwkv_w0, m_rwkv_w2, m_rwkv_a0, m_rwkv_a2, m_rwkv_g2, m_rwkv_k_k, m_rwkv_k_a, m_rwkv_r_k, m_rwkv_gn_w, m_rwkv_gn_b, m_w_out, m_ffn2_norm, m_ffn2_w_gate, m_ffn2_w_up, m_ffn2_w_down, m_final_norm, v_ffn1_norm, v_ffn1_w_gate, v_ffn1_w_up, v_ffn1_w_down, v_mix_norm, v_w_in, v_hgrn_lb_logits, v_hgrn_out_norm, v_rwkv_shift_mu, v_rwkv_w0, v_rwkv_w2, v_rwkv_a0, v_rwkv_a2, v_rwkv_g2, v_rwkv_k_k, v_rwkv_k_a, v_rwkv_r_k, v_rwkv_gn_w, v_rwkv_gn_b, v_w_out, v_ffn2_norm, v_ffn2_w_gate, v_ffn2_w_up, v_ffn2_w_down, v_final_norm):
    args = dict(locals())
    wts = {n: args[n] for n in ALL_WEIGHTS}
    moms = {n: args["m_" + n] for n in ALL_WEIGHTS}
    vars_ = {n: args["v_" + n] for n in ALL_WEIGHTS}

    me = 2 * lax.axis_index("x") + lax.axis_index("y")
    c_idx = lax.axis_index("c").astype(jnp.int32).reshape(1)
    me_idx = me.astype(jnp.int32).reshape(1)
    shard_of = {n: _to_rows(n, wts[n]).astype(BF16) for n in BIG}
    shard_of["packed"] = _pack(wts, {n: wts[n] for n in SMALL}).astype(BF16)
    group = {"ffn1": BIG[0:3], "ffn2": BIG[3:6]}

    def slot_bufs(names):
        return [lax.dynamic_update_slice(lax.empty((N_CHIPS,) + shard_of[n].shape, BF16), shard_of[n][None],
                                         (me, 0, 0)) for n in names]

    def ffn_weights(tag, gathered):
        return {f"{tag}_wgt": gathered[0].reshape(D_FF, D_MODEL), f"{tag}_wut": gathered[1].reshape(D_FF, D_MODEL),
                f"{tag}_wd": gathered[2].reshape(D_FF, D_MODEL)}

    def w_in_weights(gathered):
        w_in_full = jnp.concatenate([gathered[0][q] for q in range(N_CHIPS)], axis=1)
        return {"w_in_h": w_in_full[:, :N_HGRN_COLS],
                "w_in_r": jnp.pad(w_in_full[:, N_HGRN_COLS:], ((0, 0), (0, N_RWKV_PAD - N_RWKV_COLS)))}

    def mixer_weights(gathered):
        w_out_full = gathered[0].reshape(D_MODEL, D_MODEL)
        packs = gathered[1].reshape(N_CHIPS, PACK_ROWS * LANES)
        full, off = {}, 0
        for n in PACKED:
            shp = _shard_shape(n)
            full[n] = jnp.concatenate([packs[q, off:off + _numel(shp)].reshape(shp) for q in range(N_CHIPS)], axis=1)
            off += _numel(shp)
        zrow = lambda nrow: jnp.zeros((nrow, W_B), BF16)
        return {"w_out": w_out_full,
                "w2_pad": jnp.concatenate([full["rwkv_w2"], zrow(LORA_PAD - 32)], axis=0),
                "a2_pad": jnp.concatenate([zrow(32), full["rwkv_a2"], zrow(LORA_PAD - 64)], axis=0),
                "g2_pad": jnp.concatenate([zrow(64), full["rwkv_g2"], zrow(LORA_PAD - 160)], axis=0)}

    plan = _Plan()
    w = {}
    plan.carry("ffn1_rms", lambda g: _gather_comm(slot_bufs(group["ffn1"][:2])),
               lambda res, w_: w_.update({"ffn1_wgt": res[0].reshape(D_FF, D_MODEL),
                                          "ffn1_wut": res[1].reshape(D_FF, D_MODEL)}))

    def after_gate_up(res, w_):
        w_["ffn1_wd"] = res[0].reshape(D_FF, D_MODEL)
        w_.update(w_in_weights(res[1:]))

    plan.carry("ffn1_gate_up", lambda g: _gather_comm(slot_bufs(("ffn1_w_down", "w_in"))), after_gate_up)
    plan.carry("ffn1_down", lambda g: _gather_comm(slot_bufs(("w_out", "packed"))),
               lambda res, w_: w_.update(mixer_weights(res)))
    plan.carry("rwkv_fwd", lambda g: _gather_comm(slot_bufs(group["ffn2"])),
               lambda res, w_: w_.update(ffn_weights("ffn2", res)))
    w["ffn1_norm"], w["ffn2_norm"] = ffn1_norm, ffn2_norm
    w["mix_norm"] = mix_norm
    w["lb0"], w["lb1"] = hgrn_lb_logits[0:1], hgrn_lb_logits[1:2]
    w["hgrn_out_norm"] = hgrn_out_norm
    w["mu_pad"] = jnp.pad(rwkv_shift_mu, ((0, 0), (0, N_RWKV_PAD - N_RWKV_COLS)))
    for n in ("rwkv_w0", "rwkv_a0", "rwkv_k_k", "rwkv_k_a", "rwkv_gn_w", "rwkv_gn_b"):
        w[n] = wts[n]
    w["rwkv_r_k"] = rwkv_r_k.reshape(1, W_B)
    w["final_norm"] = final_norm.reshape(1, D_MODEL)

    def reduce_rows(names, gs):
        r1 = _run_comm(_sibling_exchange_comm(gs), "grad_sibling_exchange")
        s4 = [_add_halves(gt, rt, c_idx, f"grad_add_halves_{n}") for gt, rt, n in zip(gs, r1, names)]
        return list(zip(_run_comm(_chip_exchange_comm(s4), "grad_chip_exchange"), s4))

    def swap_comm(names):
        return _sibling_swap_comm([early[n][0] for n in names], [early[n][1] for n in names])

    def after_swap(names):
        return lambda res, w_: swapped.update(zip(names, zip(res[:len(names)], res[len(names):])))

    early, swapped = {}, {}

    def reduce_early(names, grads_of, sibling_host, chips_host, swap_host):
        def sibling_comm(g):
            early[names, "gs"] = grads_of(g)
            return _sibling_exchange_comm(early[names, "gs"])

        def after_sibling(res, w_):
            early[names, "s4"] = [_add_halves(gt, rt, c_idx, f"grad_add_halves_{n}")
                                  for gt, rt, n in zip(early[names, "gs"], res, names)]

        plan.carry(sibling_host, sibling_comm, after_sibling)
        plan.carry(chips_host, lambda g: _chip_exchange_comm(early[names, "s4"]),
                   lambda res, w_: early.update(zip(names, zip(res, early[names, "s4"]))))
        if swap_host:
            plan.carry(swap_host, lambda g: swap_comm(names), after_swap(names))

    def proj_grads(g):
        g_w_in = jnp.concatenate([g["w_in_h"], g["w_in_r"][:, :N_RWKV_COLS]], axis=1)
        return [g["w_out"].reshape(N_CHIPS, -1, D_MODEL),
                jnp.stack([_quarter(g_w_in, "w_in", q) for q in range(N_CHIPS)])]

    rows_of = lambda keys: (lambda g: [g[k].reshape(N_CHIPS, -1, D_MODEL) for k in keys])
    reduce_early(group["ffn2"], rows_of(("ffn2_wgt", "ffn2_wut", "ffn2_wd")), "hgrn_bwd", "rwkv_bwd", "rwkv_prep_bwd")
    reduce_early(("w_out", "w_in"), proj_grads, "mix_drms", "ffn1_dact", "ffn1_dwg")
    reduce_early(("ffn1_w_down",), rows_of(("ffn1_wd",)), "ffn1_dwg", "ffn1_dwu", "ffn1_dh_g")
    reduce_early(("ffn1_w_gate",), rows_of(("ffn1_wgt",)), "ffn1_dwu", "ffn1_dh_g", "ffn1_dh_u")
    reduce_early(("ffn1_w_up",), rows_of(("ffn1_wut",)), "ffn1_dh_g", "ffn1_dh_u", None)
    loss_slab, grad_x, g = _local_step(x[0], loss_target[0], w, plan)
    loss = lax.psum(loss_slab[0, 0], ("x", "y", "c"))

    gfull = {
        "rwkv_w2": g["w2_pad"][0:32], "rwkv_a2": g["a2_pad"][32:64], "rwkv_g2": g["g2_pad"][64:160],
    }
    gsmall = {
        "ffn1_norm": g["ffn1_norm"], "mix_norm": g["mix_norm"],
        "hgrn_lb_logits": jnp.concatenate([g["lb0"], g["lb1"]], axis=0), "hgrn_out_norm": g["hgrn_out_norm"],
        "rwkv_shift_mu": g["mu_pad"][:, :N_RWKV_COLS], "rwkv_w0": g["rwkv_w0"], "rwkv_a0": g["rwkv_a0"],
        "rwkv_k_k": g["rwkv_k_k"], "rwkv_k_a": g["rwkv_k_a"], "rwkv_r_k": g["rwkv_r_k"],
        "rwkv_gn_w": g["rwkv_gn_w"], "rwkv_gn_b": g["rwkv_gn_b"], "ffn2_norm": g["ffn2_norm"],
        "final_norm": g["final_norm"],
    }
    packed = jnp.stack([_pack({n: _quarter(gfull[n], n, q) for n in PACKED}, gsmall) for q in range(N_CHIPS)])
    early["packed"], = reduce_rows(["packed"], [packed])
    last = ["ffn1_w_up", "packed"]
    after_swap(last)(_run_comm(swap_comm(last), "grad_sibling_swap"), w)
    names = list(BIG) + ["packed"]

    def rows_list(d):
        return [_to_rows(n, d[n]) for n in BIG] + [_pack(d, {n: d[n] for n in SMALL})]

    outs = [_adamw(wt, *swapped[n], mt, vt, me_idx, f"adamw_{n}")
            for wt, mt, vt, n in zip(rows_list(wts), rows_list(moms), rows_list(vars_), names)]
    results = []
    for k in range(4):
        per = [outs[i][k] for i in range(len(names))]
        d = {n: _from_rows(n, z) for n, z in zip(BIG, per[:-1])}
        d.update(_unpack(per[-1]))
        results.append(d)
    return (loss, grad_x[None], *[r[n] for r in results for n in ALL_WEIGHTS])
```

```python
import collections
import functools

import jax
import jax.numpy as jnp
from jax import lax
from jax.experimental import pallas as pl
from jax.experimental.pallas import tpu as pltpu

F32 = jnp.float32
BF16 = jnp.bfloat16
SDS = jax.ShapeDtypeStruct
MESH = pl.DeviceIdType.MESH

D_MODEL = 1024
D_FF = 2816
W_A = 512
W_B = 512
HA_HEADS, HA_DIM = 4, 128
HB_HEADS, HB_DIM = 8, 64
HGRN_CHUNK = 64
HGRN_GROUP = 8
RWKV_CHUNK = 16
RWKV_GROUP = 8
N_HGRN_COLS = 4 * W_A
N_RWKV_COLS = 3 * W_B + 32 + 32 + 96
N_RWKV_PAD = 1792
LORA_PAD = 256
NORM_EPS = 1e-6
RWKV_GN_EPS = 64e-5
L2_EPS = 1e-12
ADAM_LR, ADAM_B1, ADAM_B2, ADAM_EPS, ADAM_WD, ADAM_STEP = 0.001, 0.9, 0.999, 1e-8, 0.01, 10

N_CHIPS = 4
VMEM_LIMIT_V7X = 56 * 1024 * 1024
LANES = 1024

SHARDED_SHAPES = {
    "ffn1_w_gate": ((D_MODEL, D_FF), 1), "ffn1_w_up": ((D_MODEL, D_FF), 1), "ffn1_w_down": ((D_FF, D_MODEL), 0),
    "w_in": ((D_MODEL, N_HGRN_COLS + N_RWKV_COLS), 1), "rwkv_w2": ((32, W_B), 1), "rwkv_a2": ((32, W_B), 1),
    "rwkv_g2": ((96, W_B), 1), "w_out": ((D_MODEL, D_MODEL), 0),
    "ffn2_w_gate": ((D_MODEL, D_FF), 1), "ffn2_w_up": ((D_MODEL, D_FF), 1), "ffn2_w_down": ((D_FF, D_MODEL), 0),
}
SMALL = ("ffn1_norm", "mix_norm", "hgrn_lb_logits", "hgrn_out_norm", "rwkv_shift_mu", "rwkv_w0", "rwkv_a0",
         "rwkv_k_k", "rwkv_k_a", "rwkv_r_k", "rwkv_gn_w", "rwkv_gn_b", "ffn2_norm", "final_norm")
ALL_WEIGHTS = ("ffn1_norm", "ffn1_w_gate", "ffn1_w_up", "ffn1_w_down", "mix_norm", "w_in", "hgrn_lb_logits",
               "hgrn_out_norm", "rwkv_shift_mu", "rwkv_w0", "rwkv_w2", "rwkv_a0", "rwkv_a2", "rwkv_g2", "rwkv_k_k",
               "rwkv_k_a", "rwkv_r_k", "rwkv_gn_w", "rwkv_gn_b", "w_out", "ffn2_norm", "ffn2_w_gate", "ffn2_w_up",
               "ffn2_w_down", "final_norm")


def _shard_shape(name):
    shape, ax = SHARDED_SHAPES[name]
    return tuple(s // N_CHIPS if i == ax else s for i, s in enumerate(shape))


def _numel(shape):
    n = 1
    for s in shape:
        n *= s
    return n


def _params(sem=None):
    return pltpu.CompilerParams(dimension_semantics=sem, vmem_limit_bytes=VMEM_LIMIT_V7X)


def _split2(x):
    hi = x.astype(BF16)
    return hi, (x.astype(F32) - hi.astype(F32)).astype(BF16)


def _dg(x, y, cx, cy, hi):
    dn = (((cx,), (cy,)), ((), ()))
    dot = lambda p, q: lax.dot_general(p, q, dn, preferred_element_type=F32)
    if hi == "x3":
        (xh, xl), (yh, yl) = _split2(x), _split2(y)
        return dot(xh, yh) + (dot(xh, yl) + dot(xl, yh))
    return dot(x.astype(BF16), y.astype(BF16))


def _make_mm(hi, cotangent_forms=None):
    @jax.custom_vjp
    def nn(x, y):
        return _dg(x, y, 1, 0, hi)

    @jax.custom_vjp
    def nt(x, y):
        return _dg(x, y, 1, 1, hi)

    @jax.custom_vjp
    def tn(x, y):
        return _dg(x, y, 0, 0, hi)

    bnn, bnt, btn = cotangent_forms or (nn, nt, tn)
    nn.defvjp(lambda x, y: (nn(x, y), (x, y)), lambda r, g: (bnt(g, r[1]), btn(r[0], g)))
    nt.defvjp(lambda x, y: (nt(x, y), (x, y)), lambda r, g: (bnn(g, r[1]), btn(g, r[0])))
    tn.defvjp(lambda x, y: (tn(x, y), (x, y)), lambda r, g: (bnt(r[1], g), bnn(r[0], g)))
    return nn, nt, tn


_nn, _nt, _tn = _make_mm(False)
_nn_x3, _nt_x3, _tn_x3 = _make_mm("x3", (_nn, _nt, _tn))


def _tri_apply(x, transpose):
    c = x.shape[0]
    tri = (lax.broadcasted_iota(jnp.int32, (c, c), 1) <= lax.broadcasted_iota(jnp.int32, (c, c), 0)).astype(BF16)
    dn = (((0 if transpose else 1,), (0,)), ((), ()))
    p1, p2 = _split2(x)
    dot = lambda p: lax.dot_general(tri, p, dn, preferred_element_type=F32)
    return dot(p1) + dot(p2)


@jax.custom_vjp
def _cumsum_rows(x):
    return _tri_apply(x, False)


_cumsum_rows.defvjp(lambda x: (_tri_apply(x, False), None), lambda _, g: (_tri_apply(g, True),))


def _sigmoid(x):
    return 1.0 / (1.0 + jnp.exp(-x))


def _silu(x):
    return x * _sigmoid(x)


def _softplus(z):
    return jnp.maximum(z, 0.0) + jnp.log(1.0 + jnp.exp(-jnp.abs(z)))


def _mm(a, b, *, ta=False, tb=False, tm, tn, tk, name, out_dtype=F32, res=None, scale=None, comm=None):
    m = a.shape[1] if ta else a.shape[0]
    kdim = a.shape[0] if ta else a.shape[1]
    n = b.shape[0] if tb else b.shape[1]
    assert (b.shape[1] if tb else b.shape[0]) == kdim
    tm, tn, tk = min(tm, m), min(tn, n), min(tk, kdim)
    assert m % tm == 0 and n % tn == 0 and kdim % tk == 0, (name, m, n, kdim)
    nk = kdim // tk
    a_spec = pl.BlockSpec((tk, tm), lambda i, j, k: (k, i)) if ta else pl.BlockSpec((tm, tk), lambda i, j, k: (i, k))
    b_spec = pl.BlockSpec((tn, tk), lambda i, j, k: (j, k)) if tb else pl.BlockSpec((tk, tn), lambda i, j, k: (k, j))
    o_spec = pl.BlockSpec((tm, tn), lambda i, j, k: (i, j))
    ca, cb = (0 if ta else 1), (1 if tb else 0)

    def body(*refs):
        if res is not None:
            a_ref, b_ref, r_ref, o_ref, acc_ref = refs
        else:
            a_ref, b_ref, o_ref, acc_ref = refs
        k = pl.program_id(2)

        @pl.when(k == 0)
        def _():
            acc_ref[...] = jnp.zeros_like(acc_ref)

        acc_ref[...] += _dg(a_ref[...], b_ref[...], ca, cb, False)

        @pl.when(k == nk - 1)
        def _():
            acc = acc_ref[...]
            if scale is not None:
                acc = acc * scale
            if res is not None:
                acc = r_ref[...] + acc
            o_ref[...] = acc.astype(out_dtype)

    in_specs = [a_spec, b_spec] + ([o_spec] if res is not None else [])
    args = (a, b) + ((res,) if res is not None else ())
    if comm is None:
        return pl.pallas_call(
            body, name=name, grid=(m // tm, n // tn, nk), in_specs=in_specs, out_specs=o_spec,
            out_shape=SDS((m, n), out_dtype), scratch_shapes=[pltpu.VMEM((tm, tn), F32)],
            compiler_params=_params(("parallel", "parallel", "arbitrary")))(*args)
    (out,), carried = _hosting_call(
        body, comm, name=name, grid=(m // tm, n // tn, nk), in_specs=in_specs, out_specs=[o_spec],
        out_shape=[SDS((m, n), out_dtype)], scratch_shapes=[pltpu.VMEM((tm, tn), F32)], args=args)
    return out, carried


def _mm_pair(a1, b1, a2, b2, *, tb=False, tm, name):
    m = a1.shape[0]
    n = b1.shape[0] if tb else b1.shape[1]
    tm = min(tm, m)
    cb = 1 if tb else 0
    assert a2.shape[0] == m and m % tm == 0
    assert all(b.shape[cb] == a.shape[1] and b.shape[1 - cb] == n for a, b in ((a1, b1), (a2, b2)))

    def body(a1_ref, b1_ref, a2_ref, b2_ref, o_ref):
        o_ref[...] = _dg(a1_ref[...], b1_ref[...], 1, cb, False) + _dg(a2_ref[...], b2_ref[...], 1, cb, False)

    a_spec = lambda a: pl.BlockSpec((tm, a.shape[1]), lambda i: (i, 0))
    b_spec = lambda b: pl.BlockSpec(b.shape, lambda i: (0, 0))
    return pl.pallas_call(
        body, name=name, grid=(m // tm,), in_specs=[a_spec(a1), b_spec(b1), a_spec(a2), b_spec(b2)],
        out_specs=pl.BlockSpec((tm, n), lambda i: (i, 0)), out_shape=SDS((m, n), F32),
        compiler_params=_params(("parallel",)))(a1, b1, a2, b2)


def _row_spec(x, tm, tile_of=lambda i: i):
    if isinstance(x, tuple):
        arr, w, j = x
        return arr, pl.BlockSpec((tm, w), lambda i, j=j: (tile_of(i), j))
    return x, pl.BlockSpec((tm, x.shape[1]), lambda i: (tile_of(i), 0))


def _par_spec(p):
    if isinstance(p, tuple):
        arr, w, j = p
        return arr, pl.BlockSpec((arr.shape[0], w), lambda i, j=j: (0, j))
    return p, pl.BlockSpec(p.shape, lambda i: (0, 0))


def _store_groups(refs, groups, vals):
    for ref, idxs in zip(refs, groups):
        off = 0
        for ix in idxs:
            v = vals[ix]
            ref[:, off:off + v.shape[1]] = v.astype(ref.dtype)
            off += v.shape[1]


SUBLANES = 8


def _x_plan(xs, tm, t, tile_of=lambda i: i):
    arrays, specs, plan = [], [], []
    nb = tm // SUBLANES
    for x in xs:
        if isinstance(x, tuple) and isinstance(x[0], str):
            kind, arr, w, j = x
            if kind == "prev":
                halo = lambda i, j=j: (jnp.maximum(tile_of(i) * nb - 1, 0), j)
            else:
                halo = lambda i, j=j: (jnp.minimum((tile_of(i) + 1) * nb, t // SUBLANES - 1), j)
            arrays += [arr, arr]
            specs += [pl.BlockSpec((tm, w), lambda i, j=j: (tile_of(i), j)), pl.BlockSpec((SUBLANES, w), halo)]
            plan.append((kind, 2, w))
        else:
            arr, spec = _row_spec(x, tm, tile_of)
            arrays.append(arr)
            specs.append(spec)
            plan.append(("plain", 1, spec.block_shape[1]))
    return arrays, specs, plan


def _x_vals(refs, plan, tm, nt, tile_of=lambda i: i):
    vals, k = [], 0
    i = tile_of(pl.program_id(0))
    rows = lax.broadcasted_iota(jnp.int32, (tm, 1), 0)
    for kind, n, _ in plan:
        main = refs[k][...].astype(F32)
        if kind == "prev":
            edge = jnp.where(i == 0, 0.0, refs[k + 1][SUBLANES - 1:SUBLANES, :].astype(F32))
            main = jnp.where(rows == 0, edge, pltpu.roll(main, 1, 0))
        elif kind == "next":
            edge = jnp.where(i == nt - 1, 0.0, refs[k + 1][0:1, :].astype(F32))
            main = jnp.where(rows == tm - 1, edge, pltpu.roll(main, tm - 1, 0))
        vals.append(main)
        k += n
    return vals


def _tile_rows(xs, tm):
    arr = xs[0]
    if isinstance(arr, tuple):
        arr = arr[1] if isinstance(arr[0], str) else arr[0]
    return min(tm, arr.shape[0]), arr.shape[0]


def _rowwise(f, xs, params, out_groups, out_dtypes, *, tm, name, comm=None):
    tm, t = _tile_rows(xs, tm)
    nt = t // tm
    xa, xspecs, plan = _x_plan(xs, tm, t)
    pa, pspecs = (zip(*[_par_spec(p) for p in params]) if params else ((), ()))
    nxr, npar = len(xa), len(pa)
    x_sds = [SDS((tm, w), F32) for _, _, w in plan]
    p_sds = [SDS(s.block_shape, F32) for s in pspecs]
    outs_sds = jax.eval_shape(lambda *vals: f(*vals), *x_sds, *p_sds)
    widths = [sum(outs_sds[ix].shape[1] for ix in idxs) for idxs in out_groups]

    def body(*refs):
        vals = _x_vals(refs[:nxr], plan, tm, nt) + [r[...].astype(F32) for r in refs[nxr:nxr + npar]]
        outs = f(*vals)
        _store_groups(refs[nxr + npar:], out_groups, outs)

    res, carried = _hosting_call(
        body, comm, name=name, grid=(nt,), in_specs=list(xspecs) + list(pspecs),
        out_specs=[pl.BlockSpec((tm, w), lambda i: (i, 0)) for w in widths],
        out_shape=[SDS((t, w), dt) for w, dt in zip(widths, out_dtypes)], scratch_shapes=[], args=(*xa, *pa))
    return res if comm is None else (res, carried)


def _rowwise_bwd(f, xs, params, cots, *, x_grad, p_grad, dx_groups, dx_dtypes, tm, name, extra=None, comm=None,
                 fold_next=None):
    tm, t = _tile_rows(xs, tm)
    nt = t // tm
    tile_of = (lambda i: nt - 1 - i) if fold_next else (lambda i: i)
    xa, xspecs, plan = _x_plan(xs, tm, t, tile_of)
    pa, pspecs = (zip(*[_par_spec(p) for p in params]) if params else ((), ()))
    ca, cspecs = zip(*[_row_spec(c, tm, tile_of) for c in cots])
    extra = extra or {}
    ekeys = sorted(extra)
    ea, especs = (zip(*[_row_spec(extra[k], tm, tile_of) for k in ekeys]) if ekeys else ((), ()))
    nx, nxr, npar, nc, ne = len(plan), len(xa), len(pa), len(ca), len(ea)
    gx = [i for i in range(nx) if x_grad[i]]
    gp = [i for i in range(npar) if p_grad[i]]
    all_widths = [sum(plan[gx[ix]][2] for ix in idxs) for idxs in dx_groups]
    emitted = [k for k in range(len(dx_groups)) if not (fold_next and k == fold_next[1])]
    widths = [all_widths[k] for k in emitted]
    ng = len(emitted)

    def body(*refs):
        ins = refs[:nxr + npar + nc + ne]
        outs = refs[nxr + npar + nc + ne:]
        vals = _x_vals(ins[:nxr], plan, tm, nt, tile_of) + [r[...].astype(F32) for r in ins[nxr:nxr + npar]]
        cvals = tuple(r[...].astype(F32) for r in ins[nxr + npar:nxr + npar + nc])
        evals = [r[...].astype(F32) for r in ins[nxr + npar + nc:]]
        diff_idx = gx + [nx + i for i in gp]

        def g(*dargs):
            full = list(vals)
            for ix, v in zip(diff_idx, dargs):
                full[ix] = v
            return tuple(f(*full))

        _, vjp = jax.vjp(g, *[vals[ix] for ix in diff_idx])
        grads = vjp(cvals)
        dxs = list(grads[:len(gx)])
        for k, ev in zip(ekeys, evals):
            dxs[k] = dxs[k] + ev
        _store_groups(outs[:ng], [dx_groups[k] for k in emitted], dxs)
        i = pl.program_id(0)
        if fold_next:
            main_ref, carry_ref = outs[emitted.index(fold_next[0])], refs[-1]
            rows = lax.broadcasted_iota(jnp.int32, (tm, 1), 0)
            off = 0
            for ix in dx_groups[fold_next[1]]:
                piece = dxs[ix]
                cols = slice(off, off + piece.shape[1])
                edge = jnp.where(i == 0, 0.0, carry_ref[0:1, cols])
                main_ref[:, cols] += jnp.where(rows == tm - 1, edge, pltpu.roll(piece, tm - 1, 0))
                carry_ref[:, cols] = piece[:SUBLANES]
                off += piece.shape[1]
        for ref, gval in zip(outs[ng:ng + len(gp)], grads[len(gx):]):
            @pl.when(i == 0)
            def _(ref=ref):
                ref[...] = jnp.zeros_like(ref)
            ref[...] += gval

    dp_specs = [pl.BlockSpec(pspecs[i].block_shape, lambda i: (0, 0)) for i in gp]
    dp_shapes = [SDS(pspecs[i].block_shape, F32) for i in gp]
    scratch = [pltpu.VMEM((SUBLANES, all_widths[fold_next[1]]), F32)] if fold_next else []
    res, carried = _hosting_call(
        body, comm, name=name, grid=(nt,), in_specs=list(xspecs) + list(pspecs) + list(cspecs) + list(especs),
        out_specs=[pl.BlockSpec((tm, w), lambda i: (tile_of(i), 0)) for w in widths] + dp_specs,
        out_shape=[SDS((t, w), dt) for w, dt in zip(widths, dx_dtypes)] + dp_shapes, scratch_shapes=scratch,
        args=(*xa, *pa, *ca, *ea))
    return res if comm is None else (res, carried)


def _rms_f(x, g):
    return (x * lax.rsqrt(jnp.mean(x * x, axis=-1, keepdims=True) + NORM_EPS) * g,)


def _group_sum_impl(x, ones_bd):
    p1, p2 = _split2(x)
    dot = lambda p: lax.dot_general(p, ones_bd.astype(BF16), (((1,), (0,)), ((), ())), preferred_element_type=F32)
    return dot(p1) + dot(p2)


@jax.custom_vjp
def _group_sum(x, ones_bd):
    return _group_sum_impl(x, ones_bd)


_group_sum.defvjp(lambda x, o: (_group_sum_impl(x, o), o),
                  lambda o, g: (_group_sum_impl(g, o), jnp.zeros_like(o)))


def _rwkv_prep_f(r, k, v, lo, rp, kp, vp, lop, mu_r, mu_k, mu_v, mu_lo, w0, w2p, a0, a2p, g2p, k_k, k_a, ones_bd):
    r = r + mu_r * (rp - r)
    k = k + mu_k * (kp - k)
    v = v + mu_v * (vp - v)
    lo = lo + mu_lo * (lop - lo)
    w_log = -_softplus(-(w0 + _nn(jnp.tanh(lo), w2p))) - 0.5
    lw = -jnp.exp(w_log)
    a_g = _sigmoid(a0 + _nn(lo, a2p))
    g = _nn(_sigmoid(lo), g2p)
    kk = k * k_k
    kk = kk / jnp.maximum(jnp.sqrt(_group_sum(kk * kk, ones_bd)), L2_EPS)
    k2 = k * (1.0 + (a_g - 1.0) * k_a)
    return r, lw, k2, v, -kk, kk * a_g, g


def _rwkv_post_f(y, r, k2, v, g, r_k, gn_w, gn_b, ones_bd):
    inv_n = 1.0 / HB_DIM
    mean = _group_sum(y, ones_bd) * inv_n
    yc = y - mean
    var = _group_sum(yc * yc, ones_bd) * inv_n
    yn = yc * lax.rsqrt(var + RWKV_GN_EPS) * gn_w + gn_b
    bonus = _group_sum(r * k2 * r_k, ones_bd) * v
    return ((yn + bonus) * g,)


def _tri(c, strict=False):
    ii = lax.broadcasted_iota(jnp.int32, (c, c), 0)
    jj = lax.broadcasted_iota(jnp.int32, (c, c), 1)
    return (jj < ii) if strict else (jj <= ii)


def _hgrn_step(st0, q_a, f_a, i_a, g_a, l0, l1, onorm):
    nh, nj = len(q_a), len(q_a[0])
    c = q_a[0][0].shape[0]
    combos = [(j, h) for j in range(nj) for h in range(nh)]
    every = lambda fn: {q: fn(q) for q in combos}
    at_ = lambda d: (lambda q: d[q[1]][q[0]])
    qa_, fa_, ia_, ga_ = (at_(z) for z in (q_a, f_a, i_a, g_a))
    incl = _tri(c)
    rows = lax.broadcasted_iota(jnp.int32, (c, 1), 0)
    lb = []
    for h in range(nh):
        mx = jnp.maximum(l0[h], l1[h])
        e0, e1 = jnp.exp(l0[h] - mx), jnp.exp(l1[h] - mx)
        lb.append(e0 / (e0 + e1))
    forget = every(lambda q: lb[q[1]] + (1.0 - lb[q[1]]) * _sigmoid(fa_(q)))
    qs = every(lambda q: _silu(qa_(q)))
    kk = every(lambda q: 1.0 - forget[q])
    lf = every(lambda q: jnp.log(forget[q]))
    bcum = every(lambda q: _cumsum_rows(lf[q]))
    bref = every(lambda q: jnp.sum(jnp.where(rows <= c // 2, lf[q], 0.0), axis=0, keepdims=True))
    blast = every(lambda q: jnp.sum(lf[q], axis=0, keepdims=True))
    scores = every(lambda q: jnp.where(incl, _nt(qs[q] * jnp.exp(bcum[q] - bref[q]),
                                                 kk[q] * jnp.exp(bref[q] - bcum[q])), 0.0))
    intra = every(lambda q: _nn(scores[q], ia_(q)))
    qb = every(lambda q: qs[q] * jnp.exp(bcum[q]))
    upd = every(lambda q: _tn(ia_(q), kk[q] * jnp.exp(blast[q] - bcum[q])))
    dec = every(lambda q: jnp.exp(blast[q]))
    st = list(st0)
    o = {}
    for j in range(nj):
        for h in range(nh):
            o[(j, h)] = intra[(j, h)] + _nt(qb[(j, h)], st[h])
        st = [st[h] * dec[(j, h)] + upd[(j, h)] for h in range(nh)]
    out = every(lambda q: o[q] * lax.rsqrt(jnp.mean(o[q] * o[q], axis=-1, keepdims=True) + NORM_EPS)
                * onorm[q[1]] * _silu(ga_(q)))
    return [[out[(j, h)] for j in range(nj)] for h in range(nh)], st


def _hgrn_blocks(ref, nj, c):
    return [[ref[j * c:(j + 1) * c, h * HA_DIM:(h + 1) * HA_DIM] for j in range(nj)] for h in range(HA_HEADS)]


def _hgrn_cols(ref):
    return [ref[:, h * HA_DIM:(h + 1) * HA_DIM] for h in range(HA_HEADS)]


def _hgrn_fwd(p_h, l0, l1, onorm):
    t = p_h.shape[0]
    cc, nj = HGRN_CHUNK, HGRN_GROUP
    c = cc * nj
    n = t // c

    def body(q_ref, f_ref, i_ref, g_ref, l0_ref, l1_ref, on_ref, o_ref, hs_ref, st_ref):
        @pl.when(pl.program_id(0) == 0)
        def _():
            st_ref[...] = jnp.zeros_like(st_ref)

        hs_ref[0] = st_ref[...]
        o, st1 = _hgrn_step([st_ref[h] for h in range(HA_HEADS)],
                            *[_hgrn_blocks(ref, nj, cc) for ref in (q_ref, f_ref, i_ref, g_ref)],
                            _hgrn_cols(l0_ref), _hgrn_cols(l1_ref), _hgrn_cols(on_ref))
        for h in range(HA_HEADS):
            for j in range(nj):
                o_ref[j * cc:(j + 1) * cc, h * HA_DIM:(h + 1) * HA_DIM] = o[h][j]
            st_ref[h] = st1[h]

    col = lambda j: pl.BlockSpec((c, W_A), lambda i, j=j: (i, j))
    par = pl.BlockSpec((1, W_A), lambda i: (0, 0))
    return pl.pallas_call(
        body, name="hgrn_fwd", grid=(n,), in_specs=[col(0), col(1), col(2), col(3), par, par, par],
        out_specs=[pl.BlockSpec((c, W_A), lambda i: (i, 0)),
                   pl.BlockSpec((1, HA_HEADS, HA_DIM, HA_DIM), lambda i: (i, 0, 0, 0))],
        out_shape=[SDS((t, W_A), F32), SDS((n, HA_HEADS, HA_DIM, HA_DIM), F32)],
        scratch_shapes=[pltpu.VMEM((HA_HEADS, HA_DIM, HA_DIM), F32)],
        compiler_params=_params(("arbitrary",)))(p_h, p_h, p_h, p_h, l0, l1, onorm)


def _hgrn_bwd(p_h, l0, l1, onorm, hs, do, do_col, comm=None):
    t = p_h.shape[0]
    cc, nj = HGRN_CHUNK, HGRN_GROUP
    c = cc * nj
    n = t // c

    def body(q_ref, f_ref, i_ref, g_ref, l0_ref, l1_ref, on_ref, hs_ref, do_ref,
             dp_ref, dl0_ref, dl1_ref, don_ref, dst_ref):
        @pl.when(pl.program_id(0) == 0)
        def _():
            dst_ref[...] = jnp.zeros_like(dst_ref)
            dl0_ref[...] = jnp.zeros_like(dl0_ref)
            dl1_ref[...] = jnp.zeros_like(dl1_ref)
            don_ref[...] = jnp.zeros_like(don_ref)

        args = ([hs_ref[0, h] for h in range(HA_HEADS)],
                *[_hgrn_blocks(ref, nj, cc) for ref in (q_ref, f_ref, i_ref, g_ref)],
                _hgrn_cols(l0_ref), _hgrn_cols(l1_ref), _hgrn_cols(on_ref))
        _, vjp = jax.vjp(_hgrn_step, *args)
        dst0, dq, df, di, dg, dl0, dl1, don = vjp((_hgrn_blocks(do_ref, nj, cc),
                                                   [dst_ref[h] for h in range(HA_HEADS)]))
        for h in range(HA_HEADS):
            sl = slice(h * HA_DIM, (h + 1) * HA_DIM)
            for k, dv in enumerate((dq, df, di, dg)):
                for j in range(nj):
                    dp_ref[j * cc:(j + 1) * cc, k * W_A + h * HA_DIM:k * W_A + (h + 1) * HA_DIM] = dv[h][j]
            dl0_ref[:, sl] += dl0[h]
            dl1_ref[:, sl] += dl1[h]
            don_ref[:, sl] += don[h]
            dst_ref[h] = dst0[h]

    col = lambda j: pl.BlockSpec((c, W_A), lambda i, j=j: (n - 1 - i, j))
    par = pl.BlockSpec((1, W_A), lambda i: (0, 0))
    return _hosting_call(
        body, comm, name="hgrn_bwd", grid=(n,),
        in_specs=[col(0), col(1), col(2), col(3), par, par, par,
                  pl.BlockSpec((1, HA_HEADS, HA_DIM, HA_DIM), lambda i: (n - 1 - i, 0, 0, 0)),
                  pl.BlockSpec((c, W_A), lambda i: (n - 1 - i, do_col))],
        out_specs=[pl.BlockSpec((c, N_HGRN_COLS), lambda i: (n - 1 - i, 0)), par, par, par],
        out_shape=[SDS((t, N_HGRN_COLS), F32), SDS((1, W_A), F32), SDS((1, W_A), F32), SDS((1, W_A), F32)],
        scratch_shapes=[pltpu.VMEM((HA_HEADS, HA_DIM, HA_DIM), F32)],
        args=(p_h, p_h, p_h, p_h, l0, l1, onorm, hs, do))


HB_PAIRS = HB_HEADS // 2
PAIR_W = 2 * HB_DIM


def _head_lane_masks():
    lane = lax.broadcasted_iota(jnp.int32, (1, PAIR_W), 1)
    return (lane < HB_DIM).astype(F32), (lane >= HB_DIM).astype(F32)


@jax.custom_vjp
def _stack_heads(x):
    m0, m1 = _head_lane_masks()
    return jnp.concatenate([x * m0, x * m1], axis=0)


def _stack_heads_bwd(_, g):
    m0, m1 = _head_lane_masks()
    c = g.shape[0] // 2
    return (g[:c] * m0 + g[c:] * m1,)


_stack_heads.defvjp(lambda x: (_stack_heads(x), None), _stack_heads_bwd)


@jax.custom_vjp
def _unstack_heads(ys):
    c = ys.shape[0] // 2
    return ys[:c] + ys[c:]


_unstack_heads.defvjp(lambda ys: (_unstack_heads(ys), None), lambda _, g: (_stack_heads(g),))


def _same_head_block(c):
    ii = lax.broadcasted_iota(jnp.int32, (2 * c, 2 * c), 0)
    jj = lax.broadcasted_iota(jnp.int32, (2 * c, 2 * c), 1)
    same = (ii < c) == (jj < c)
    return same & (jj <= ii), same & (jj < ii), (ii == jj).astype(F32)


@jax.custom_vjp
def _rows_join(top, bottom):
    return jnp.concatenate([top, bottom], axis=0)


def _rows_join_bwd(n_top, g):
    return g[:n_top], g[n_top:]


_rows_join.defvjp(lambda top, bottom: (_rows_join(top, bottom), top.shape[0]), _rows_join_bwd)


def _rows_split_impl(x, n_top):
    return x[:n_top], x[n_top:]


_rows_split = jax.custom_vjp(_rows_split_impl, nondiff_argnums=(1,))
_rows_split.defvjp(lambda x, n_top: (_rows_split_impl(x, n_top), None),
                   lambda n_top, _, g: (jnp.concatenate([g[0], g[1]], axis=0),))


def _rwkv_step(s0, r, lw, k, v, a, b):
    npair, nj = len(r), len(r[0])
    c = r[0][0].shape[0]
    combos = [(j, p) for j in range(nj) for p in range(npair)]
    every = lambda fn: {q: fn(q) for q in combos}
    at_ = lambda d: (lambda q: d[q[1]][q[0]])
    r_, lw_, k_, v_, a_, b_ = (at_(z) for z in (r, lw, k, v, a, b))
    incl, strict, eye = _same_head_block(c)

    gam = every(lambda q: _cumsum_rows(lw_(q)))
    gtot = every(lambda q: jnp.sum(lw_(q), axis=0, keepdims=True))
    eneg = every(lambda q: jnp.exp(-gam[q]))
    edec = every(lambda q: jnp.exp(gtot[q] - gam[q]))
    at = every(lambda q: _stack_heads(a_(q) * jnp.exp(gam[q] - lw_(q))))
    rt = every(lambda q: _stack_heads(r_(q) * jnp.exp(gam[q])))
    bt = every(lambda q: _stack_heads(b_(q) * eneg[q]))
    kt = every(lambda q: _stack_heads(k_(q) * eneg[q]))
    bdec = every(lambda q: _stack_heads(b_(q) * edec[q]))
    kdec = every(lambda q: _stack_heads(k_(q) * edec[q]))
    vs = every(lambda q: _stack_heads(v_(q)))
    a_ab = every(lambda q: jnp.where(strict, _nt(at[q], bt[q]), 0.0))
    a_ak = every(lambda q: jnp.where(strict, _nt(at[q], kt[q]), 0.0))
    a_rb = every(lambda q: jnp.where(incl, _nt(rt[q], bt[q]), 0.0))
    a_rk = every(lambda q: jnp.where(incl, _nt(rt[q], kt[q]), 0.0))
    tinv = every(lambda q: eye + a_ab[q])
    pw = a_ab
    span = 2
    while span < c:
        pw = every(lambda q, pw=pw: _nn_x3(pw[q], pw[q]))
        tinv = every(lambda q, pw=pw, tinv=tinv: tinv[q] + _nn_x3(pw[q], tinv[q]))
        span *= 2
    akv = every(lambda q: _nn(a_ak[q], vs[q]))
    w1 = every(lambda q: _nn(tinv[q], at[q]))
    u0 = every(lambda q: _nn(tinv[q], akv[q]))
    wr = every(lambda q: _rows_join(w1[q], rt[q]))
    bk = every(lambda q: _rows_join(bdec[q], kdec[q]))
    yv = every(lambda q: _nn(a_rk[q], vs[q]))
    gdec = every(lambda q: jnp.exp(gtot[q]))

    s = list(s0)
    y = [[None] * nj for _ in range(npair)]
    for j in range(nj):
        both = {p: _rows_split(_nt(wr[(j, p)], s[p]), 2 * c) for p in range(npair)}
        u = {p: both[p][0] + u0[(j, p)] for p in range(npair)}
        for p in range(npair):
            y[p][j] = _unstack_heads(both[p][1] + _nn(a_rb[(j, p)], u[p]) + yv[(j, p)])
        s = [s[p] * gdec[(j, p)] + _tn(_rows_join(u[p], vs[(j, p)]), bk[(j, p)]) for p in range(npair)]
    return y, s


def _rwkv_blocks(ref, nj, c):
    return [[ref[j * c:(j + 1) * c, p * PAIR_W:(p + 1) * PAIR_W] for j in range(nj)] for p in range(HB_PAIRS)]


def _rwkv_fwd(seqs, comm=None):
    t = seqs[0].shape[0]
    c, nj = RWKV_CHUNK, RWKV_GROUP
    n = t // (c * nj)

    def body(r_ref, lw_ref, k_ref, v_ref, a_ref, b_ref, y_ref, hs_ref, st_ref):
        @pl.when(pl.program_id(0) == 0)
        def _():
            st_ref[...] = jnp.zeros_like(st_ref)

        hs_ref[0] = st_ref[...]
        s0 = [st_ref[p] for p in range(HB_PAIRS)]
        y, s1 = _rwkv_step(s0, *[_rwkv_blocks(ref, nj, c) for ref in (r_ref, lw_ref, k_ref, v_ref, a_ref, b_ref)])
        for p in range(HB_PAIRS):
            for j in range(nj):
                y_ref[j * c:(j + 1) * c, p * PAIR_W:(p + 1) * PAIR_W] = y[p][j]
            st_ref[p] = s1[p]

    seq = pl.BlockSpec((c * nj, W_B), lambda i: (i, 0))
    return _hosting_call(
        body, comm, name="rwkv_fwd", grid=(n,), in_specs=[seq] * 6,
        out_specs=[seq, pl.BlockSpec((1, HB_PAIRS, PAIR_W, PAIR_W), lambda i: (i, 0, 0, 0))],
        out_shape=[SDS((t, W_B), F32), SDS((n, HB_PAIRS, PAIR_W, PAIR_W), F32)],
        scratch_shapes=[pltpu.VMEM((HB_PAIRS, PAIR_W, PAIR_W), F32)], args=tuple(seqs))


def _rwkv_bwd(seqs, hs, dy, comm=None):
    t = seqs[0].shape[0]
    c, nj = RWKV_CHUNK, RWKV_GROUP
    n = t // (c * nj)

    def body(r_ref, lw_ref, k_ref, v_ref, a_ref, b_ref, hs_ref, dy_ref,
             dr_ref, dlw_ref, dk_ref, dv_ref, da_ref, db_ref, dst_ref):
        @pl.when(pl.program_id(0) == 0)
        def _():
            dst_ref[...] = jnp.zeros_like(dst_ref)

        s0 = [hs_ref[0, p] for p in range(HB_PAIRS)]
        seq_vals = [_rwkv_blocks(ref, nj, c) for ref in (r_ref, lw_ref, k_ref, v_ref, a_ref, b_ref)]
        _, vjp = jax.vjp(_rwkv_step, s0, *seq_vals)
        grads = vjp((_rwkv_blocks(dy_ref, nj, c), [dst_ref[p] for p in range(HB_PAIRS)]))
        for ref, gr in zip((dr_ref, dlw_ref, dk_ref, dv_ref, da_ref, db_ref), grads[1:]):
            for p in range(HB_PAIRS):
                for j in range(nj):
                    ref[j * c:(j + 1) * c, p * PAIR_W:(p + 1) * PAIR_W] = gr[p][j]
        m0, m1 = _head_lane_masks()
        rows0 = (lax.broadcasted_iota(jnp.int32, (PAIR_W, 1), 0) < HB_DIM).astype(F32)
        blocks = rows0 * m0 + (1.0 - rows0) * m1
        for p in range(HB_PAIRS):
            dst_ref[p] = grads[0][p] * blocks

    seq = pl.BlockSpec((c * nj, W_B), lambda i: (n - 1 - i, 0))
    return _hosting_call(
        body, comm, name="rwkv_bwd", grid=(n,),
        in_specs=[seq] * 6 + [pl.BlockSpec((1, HB_PAIRS, PAIR_W, PAIR_W), lambda i: (n - 1 - i, 0, 0, 0)), seq],
        out_specs=[seq] * 6, out_shape=[SDS((t, W_B), F32)] * 6,
        scratch_shapes=[pltpu.VMEM((HB_PAIRS, PAIR_W, PAIR_W), F32)], args=(*seqs, hs, dy))


def _final_loss(x3, fnorm, target, *, tm):
    t, d = x3.shape

    def body(x_ref, g_ref, t_ref, dx_ref, dg_ref, loss_ref):
        @pl.when(pl.program_id(0) == 0)
        def _():
            dg_ref[...] = jnp.zeros_like(dg_ref)
            loss_ref[...] = jnp.zeros_like(loss_ref)

        x, g = x_ref[...], g_ref[...]
        rinv = lax.rsqrt(jnp.mean(x * x, axis=-1, keepdims=True) + NORM_EPS)
        xh = x * rinv
        diff = xh * g - t_ref[...]
        loss_ref[...] += 0.5 * jnp.sum(jnp.mean(diff * diff, axis=-1, keepdims=True))
        dy = diff * (1.0 / d)
        dg_ref[...] += jnp.sum(dy * xh, axis=0, keepdims=True)
        dxh = dy * g
        dx_ref[...] = rinv * (dxh - xh * jnp.mean(dxh * xh, axis=-1, keepdims=True))

    row = pl.BlockSpec((tm, d), lambda i: (i, 0))
    return pl.pallas_call(
        body, name="final_loss", grid=(t // tm,), in_specs=[row, pl.BlockSpec((1, d), lambda i: (0, 0)), row],
        out_specs=[row, pl.BlockSpec((1, d), lambda i: (0, 0)), pl.BlockSpec((8, 128), lambda i: (0, 0))],
        out_shape=[SDS((t, d), F32), SDS((1, d), F32), SDS((8, 128), F32)],
        compiler_params=_params(("arbitrary",)))(x3, fnorm, target)


def _gate_up_act(h, wgt, wut, *, tm, tn, name, comm=None):
    t, d = h.shape
    tm = min(tm, t)

    def body(h_ref, g_ref, u_ref, a_out, u_out, act_out):
        hv = h_ref[...]
        a = _dg(hv, g_ref[...], 1, 1, False)
        u = _dg(hv, u_ref[...], 1, 1, False)
        a_out[...] = a.astype(a_out.dtype)
        u_out[...] = u.astype(u_out.dtype)
        act_out[...] = (_silu(a) * u).astype(act_out.dtype)

    wspec = pl.BlockSpec((tn, d), lambda i, j: (j, 0))
    ospec = pl.BlockSpec((tm, tn), lambda i, j: (i, j))
    return _hosting_call(
        body, comm, name=name, grid=(t // tm, D_FF // tn),
        in_specs=[pl.BlockSpec((tm, d), lambda i, j: (i, 0)), wspec, wspec], out_specs=[ospec, ospec, ospec],
        out_shape=[SDS((t, D_FF), BF16), SDS((t, D_FF), BF16), SDS((t, D_FF), BF16)], scratch_shapes=[],
        args=(h, wgt, wut))


def _dact_swiglu(dout, wd, a, u, *, tm, tn, name, comm=None):
    t, d = dout.shape
    tm = min(tm, t)

    def body(d_ref, w_ref, a_ref, u_ref, da_out, du_out):
        dact = 0.5 * _dg(d_ref[...], w_ref[...], 1, 1, False)
        av, uv = a_ref[...].astype(F32), u_ref[...].astype(F32)
        s = _sigmoid(av)
        da_out[...] = (dact * uv * (s * (1.0 + av * (1.0 - s)))).astype(da_out.dtype)
        du_out[...] = (dact * (av * s)).astype(du_out.dtype)

    tile = pl.BlockSpec((tm, tn), lambda i, j: (i, j))
    return _hosting_call(
        body, comm, name=name, grid=(t // tm, D_FF // tn),
        in_specs=[pl.BlockSpec((tm, d), lambda i, j: (i, 0)), pl.BlockSpec((tn, d), lambda i, j: (j, 0)), tile, tile],
        out_specs=[tile, tile], out_shape=[SDS((t, D_FF), BF16), SDS((t, D_FF), BF16)], scratch_shapes=[],
        args=(dout, wd, a, u))


class _Plan:
    def __init__(self):
        self.entries, self.counts = collections.defaultdict(list), {}

    def carry(self, host, comm_of, after):
        self.entries[host].append((comm_of, after))

    def comm(self, host, g):
        comms = [comm_of(g) for comm_of, _ in self.entries.get(host, [])]
        self.counts[host] = [len(c.arrays) for c in comms]
        return functools.reduce(_join_comms, comms) if comms else None

    def done(self, host, results, w):
        start = 0
        for (_, after), n in zip(self.entries.get(host, []), self.counts.get(host, [])):
            after(results[start:start + n], w)
            start += n


def _ffn_fwd(x, w, tag, plan, g):
    comm = plan.comm(f"{tag}_rms", g)
    res = _rowwise(_rms_f, [x], [w[f"{tag}_norm"]], [[0]], [BF16], tm=1024, name=f"{tag}_rms", comm=comm)
    (h,), carried = res if comm is not None else (res, [])
    plan.done(f"{tag}_rms", carried, w)
    (a, u, act), carried = _gate_up_act(h, w[f"{tag}_wgt"], w[f"{tag}_wut"], tm=2048, tn=256, name=f"{tag}_gate_up",
                                        comm=plan.comm(f"{tag}_gate_up", g))
    plan.done(f"{tag}_gate_up", carried, w)
    comm = plan.comm(f"{tag}_down", g)
    out = _mm(act, w[f"{tag}_wd"], tm=1024, tn=D_MODEL, tk=D_FF, name=f"{tag}_down", res=x, scale=0.5, comm=comm)
    if comm is not None:
        out, carried = out
        plan.done(f"{tag}_down", carried, w)
    return out, (h, a, u, act)


def _ffn_bwd(dout, x, w, saved, tag, plan, g):
    h, a, u, act = saved

    def carrying(fn, host, *args, **kwargs):
        comm = plan.comm(host, g)
        res = fn(*args, name=host, comm=comm, **kwargs)
        out, carried = res if comm is not None else (res, [])
        plan.done(host, carried, w)
        return out

    (da, du), carried = _dact_swiglu(dout, w[f"{tag}_wd"], a, u, tm=2048, tn=256, name=f"{tag}_dact",
                                     comm=plan.comm(f"{tag}_dact", g))
    plan.done(f"{tag}_dact", carried, w)
    g[f"{tag}_wd"] = _mm(act, dout, ta=True, tm=D_FF // 2, tn=D_MODEL, tk=1024, name=f"{tag}_dwd", scale=0.5)
    g[f"{tag}_wgt"] = carrying(_mm, f"{tag}_dwg", da, h, ta=True, tm=D_FF // 2, tn=D_MODEL, tk=1024)
    g[f"{tag}_wut"] = carrying(_mm, f"{tag}_dwu", du, h, ta=True, tm=D_FF // 2, tn=D_MODEL, tk=1024)
    if plan.entries.get(f"{tag}_dh_g") or plan.entries.get(f"{tag}_dh_u"):
        dh = carrying(_mm, f"{tag}_dh_g", da, w[f"{tag}_wgt"], tm=1024, tn=D_MODEL, tk=D_FF)
        dh = carrying(_mm, f"{tag}_dh_u", du, w[f"{tag}_wut"], tm=1024, tn=D_MODEL, tk=D_FF, res=dh)
    else:
        dh = _mm_pair(da, w[f"{tag}_wgt"], du, w[f"{tag}_wut"], tm=512, name=f"{tag}_dh")
    dx, g[f"{tag}_norm"] = carrying(_rowwise_bwd, f"{tag}_drms", _rms_f, [x], [w[f"{tag}_norm"]], [dh],
                                    x_grad=[True], p_grad=[True], dx_groups=[[0]], dx_dtypes=[F32], tm=512,
                                    extra={0: dout})
    return dx


def _local_step(x, target, w, plan=None):
    plan = plan or _Plan()
    ones_bd = jnp.kron(jnp.eye(HB_HEADS, dtype=F32), jnp.ones((HB_DIM, HB_DIM), F32))
    g = {}
    x1, ffn1_saved = _ffn_fwd(x, w, "ffn1", plan, g)
    hm, = _rowwise(_rms_f, [x1], [w["mix_norm"]], [[0]], [BF16], tm=1024, name="mix_rms")
    p_h = _mm(hm, w["w_in_h"], tm=2048, tn=256, tk=D_MODEL, name="inproj_h")
    p_r = _mm(hm, w["w_in_r"], tm=2048, tn=256, tk=D_MODEL, name="inproj_r")
    o_a, hgrn_states = _hgrn_fwd(p_h, w["lb0"], w["lb1"], w["hgrn_out_norm"])

    mu = w["mu_pad"]
    prep_xs = [(p_r, W_B, 0), (p_r, W_B, 1), (p_r, W_B, 2), (p_r, LORA_PAD, 6),
               ("prev", p_r, W_B, 0), ("prev", p_r, W_B, 1), ("prev", p_r, W_B, 2), ("prev", p_r, LORA_PAD, 6)]
    prep_ps = [(mu, W_B, 0), (mu, W_B, 1), (mu, W_B, 2), (mu, LORA_PAD, 6), w["rwkv_w0"], w["w2_pad"], w["rwkv_a0"],
               w["a2_pad"], w["g2_pad"], w["rwkv_k_k"], w["rwkv_k_a"], ones_bd]
    prep_f = _rwkv_prep_f
    r, lw, k2, v, a_vec, b_vec, gate = _rowwise(prep_f, prep_xs, prep_ps, [[0], [1], [2], [3], [4], [5], [6]],
                                                [F32] * 7, tm=512, name="rwkv_prep")
    seqs = [r, lw, k2, v, a_vec, b_vec]
    (y, rwkv_states), carried = _rwkv_fwd(seqs, comm=plan.comm("rwkv_fwd", g))
    plan.done("rwkv_fwd", carried, w)
    post_f = _rwkv_post_f
    post_xs = [y, r, k2, v, gate]
    post_ps = [w["rwkv_r_k"], w["rwkv_gn_w"], w["rwkv_gn_b"], ones_bd]
    o, = _rowwise(lambda o_a_, *rest: (o_a_,) + tuple(post_f(*rest)), [o_a] + post_xs, post_ps, [[0, 1]], [F32],
                  tm=512, name="rwkv_post")
    x2 = _mm(o, w["w_out"], tm=2048, tn=256, tk=D_MODEL, name="outproj", res=x1)
    x3, ffn2_saved = _ffn_fwd(x2, w, "ffn2", plan, g)
    dx3, g["final_norm"], loss = _final_loss(x3, w["final_norm"], target, tm=512)

    dx2 = _ffn_bwd(dx3, x2, w, ffn2_saved, "ffn2", plan, g)
    do = _mm(dx2, w["w_out"], tb=True, tm=2048, tn=256, tk=D_MODEL, name="outproj_do")
    g["w_out"] = _mm(o, dx2, ta=True, tm=D_MODEL, tn=D_MODEL, tk=1024, name="outproj_dw")

    (dp_h, g["lb0"], g["lb1"], g["hgrn_out_norm"]), carried = _hgrn_bwd(
        p_h, w["lb0"], w["lb1"], w["hgrn_out_norm"], hgrn_states, do, 0, comm=plan.comm("hgrn_bwd", g))
    plan.done("hgrn_bwd", carried, w)
    post_out = _rowwise_bwd(post_f, post_xs, post_ps, [(do, W_B, 1)], x_grad=[True] * 5, p_grad=[True] * 3 + [False],
                            dx_groups=[[0], [1], [2], [3], [4]], dx_dtypes=[F32] * 5, tm=512, name="rwkv_post_bwd")
    dy, dr1, dk1, dv1, dgate, g["rwkv_r_k"], g["rwkv_gn_w"], g["rwkv_gn_b"] = post_out
    (dr2, dlw, dk2, dv2, da_vec, db_vec), carried = _rwkv_bwd(seqs, rwkv_states, dy, comm=plan.comm("rwkv_bwd", g))
    plan.done("rwkv_bwd", carried, w)

    def prep2_f(*vals):
        r_, lw_, k2_, v_, a_, b_, g_ = prep_f(*vals)
        return r_, lw_, k2_, v_, a_, b_, g_, r_, k2_, v_

    prep_comm = plan.comm("rwkv_prep_bwd", g)
    prep_out = _rowwise_bwd(prep2_f, prep_xs, prep_ps, [dr2, dlw, dk2, dv2, da_vec, db_vec, dgate, dr1, dk1, dv1],
                            x_grad=[True] * 8, p_grad=[True] * 11 + [False], dx_groups=[[0, 1, 2, 3], [4, 5, 6, 7]],
                            dx_dtypes=[F32], tm=256, name="rwkv_prep_bwd", fold_next=(0, 1), comm=prep_comm)
    prep_out, carried = prep_out if prep_comm is not None else (prep_out, [])
    plan.done("rwkv_prep_bwd", carried, w)
    dp_r = prep_out[0]
    (dmu_r, dmu_k, dmu_v, dmu_lo, g["rwkv_w0"], g["w2_pad"], g["rwkv_a0"], g["a2_pad"], g["g2_pad"],
     g["rwkv_k_k"], g["rwkv_k_a"]) = prep_out[1:]
    g["mu_pad"] = jnp.concatenate([dmu_r, dmu_k, dmu_v, dmu_lo], axis=1)
    dhm = _mm_pair(dp_h, w["w_in_h"], dp_r, w["w_in_r"], tb=True, tm=512, name="inproj_dh")
    g["w_in_h"] = _mm(hm, dp_h, ta=True, tm=D_MODEL, tn=D_MODEL, tk=1024, name="inproj_dw_h")
    g["w_in_r"] = _mm(hm, dp_r, ta=True, tm=D_MODEL, tn=N_RWKV_PAD // 2, tk=1024, name="inproj_dw_r")
    mix_comm = plan.comm("mix_drms", g)
    mix_out = _rowwise_bwd(_rms_f, [x1], [w["mix_norm"]], [dhm], x_grad=[True], p_grad=[True], dx_groups=[[0]],
                           dx_dtypes=[F32], tm=512, name="mix_drms", extra={0: dx2}, comm=mix_comm)
    (dx1, g["mix_norm"]), carried = mix_out if mix_comm is not None else (mix_out, [])
    plan.done("mix_drms", carried, w)
    dx0 = _ffn_bwd(dx1, x, w, ffn1_saved, "ffn1", plan, g)
    return loss, dx0, g


HBM_SPEC = pl.BlockSpec(memory_space=pl.ANY)

Comm = collections.namedtuple("Comm", "arrays out_shapes aliased sem_shapes start finish")


def _join_comms(first, second):
    n, s = len(first.arrays), len(first.sem_shapes)

    def start(ins, outs, sems):
        first.start(ins[:n], outs[:n], sems[:s])
        second.start(ins[n:], outs[n:], sems[s:])

    def finish(ins, outs, sems):
        first.finish(ins[:n], outs[:n], sems[:s])
        second.finish(ins[n:], outs[n:], sems[s:])

    return Comm(list(first.arrays) + list(second.arrays), list(first.out_shapes) + list(second.out_shapes),
                list(first.aliased) + list(second.aliased), list(first.sem_shapes) + list(second.sem_shapes),
                start, finish)


def _run_comm(comm, name):
    n = len(comm.arrays)

    def body(*refs):
        ins, outs, sems = refs[:n], refs[n:2 * n], refs[2 * n:]
        comm.start(ins, outs, sems)
        comm.finish(ins, outs, sems)

    return pl.pallas_call(
        body, name=name, in_specs=[HBM_SPEC] * n, out_specs=[HBM_SPEC] * n, out_shape=list(comm.out_shapes),
        input_output_aliases={t: t for t in range(n) if comm.aliased[t]},
        scratch_shapes=list(comm.sem_shapes))(*comm.arrays)


def _hosting_call(body, comm, *, name, grid, in_specs, out_specs, out_shape, scratch_shapes, args):
    sem = ("arbitrary",) * len(grid)
    if comm is None:
        res = pl.pallas_call(body, name=name, grid=grid, in_specs=in_specs, out_specs=out_specs, out_shape=out_shape,
                             scratch_shapes=scratch_shapes, compiler_params=_params(sem))(*args)
        return list(res), []
    ni, no, ns, nc = len(in_specs), len(out_specs), len(scratch_shapes), len(comm.arrays)

    def wrapped(*refs):
        ins, cins = refs[:ni], refs[ni:ni + nc]
        outs, couts = refs[ni + nc:ni + nc + no], refs[ni + nc + no:ni + 2 * nc + no]
        scr, sems = refs[ni + 2 * nc + no:ni + 2 * nc + no + ns], refs[ni + 2 * nc + no + ns:]
        first = functools.reduce(jnp.logical_and, [pl.program_id(k) == 0 for k in range(len(grid))])
        last = functools.reduce(jnp.logical_and, [pl.program_id(k) == grid[k] - 1 for k in range(len(grid))])

        @pl.when(first)
        def _():
            comm.start(cins, couts, sems)

        body(*ins, *outs, *scr)

        @pl.when(last)
        def _():
            comm.finish(cins, couts, sems)

    res = pl.pallas_call(
        wrapped, name=name, grid=grid, in_specs=list(in_specs) + [HBM_SPEC] * nc,
        out_specs=list(out_specs) + [HBM_SPEC] * nc, out_shape=list(out_shape) + list(comm.out_shapes),
        scratch_shapes=list(scratch_shapes) + list(comm.sem_shapes),
        input_output_aliases={ni + t: no + t for t in range(nc) if comm.aliased[t]},
        compiler_params=_params(sem))(*args, *comm.arrays)
    return list(res[:no]), list(res[no:])


def _chips(x, y):
    return [(1 - x, y), (x, 1 - y), (1 - x, 1 - y)]


def _gather_comm(bufs):
    n = len(bufs)

    def copies(outs, sems):
        ici_send, ici_recv, d2d_send, d2d_recv = sems
        x, y, c = lax.axis_index("x"), lax.axis_index("y"), lax.axis_index("c")

        def half(t, slot, hc):
            hr = bufs[t].shape[1] // 2
            return outs[t].at[slot, pl.ds(pl.multiple_of(hc * hr, 16), hr), :]

        def ici(t, j, slot, px, py):
            return pltpu.make_async_remote_copy(src_ref=half(t, slot, c), dst_ref=half(t, slot, c),
                                                send_sem=ici_send.at[3 * t + j], recv_sem=ici_recv.at[3 * t + j],
                                                device_id=(px, py, c), device_id_type=MESH)

        def d2d(t, j, slot, hc):
            return pltpu.make_async_remote_copy(src_ref=half(t, slot, hc), dst_ref=half(t, slot, hc),
                                                send_sem=d2d_send.at[3 * t + j], recv_sem=d2d_recv.at[3 * t + j],
                                                device_id=(x, y, 1 - c), device_id_type=MESH)

        peers = [(t, j, px, py) for t in range(n) for j, (px, py) in enumerate(_chips(x, y))]
        return ici, d2d, peers, 2 * x + y, c

    def start(ins, outs, sems):
        ici, _, peers, me, _ = copies(outs, sems)
        for t, j, px, py in peers:
            ici(t, j, me, px, py).start()

    def finish(ins, outs, sems):
        ici, d2d, peers, me, c = copies(outs, sems)
        for t, j, px, py in peers:
            ici(t, j, 2 * px + py, px, py).wait_recv()
            d2d(t, j, 2 * px + py, c).start()
        for t, j, px, py in peers:
            d2d(t, j, 2 * px + py, 1 - c).wait_recv()
        for t, j, px, py in peers:
            ici(t, j, me, px, py).wait_send()
            d2d(t, j, 2 * px + py, c).wait_send()

    return Comm(list(bufs), [SDS(b.shape, b.dtype) for b in bufs], [True] * n,
                [pltpu.SemaphoreType.DMA((3 * n,))] * 4, start, finish)


def _sibling_exchange_comm(gs):
    n = len(gs)

    def copies(ins, outs, sems):
        x, y, c = lax.axis_index("x"), lax.axis_index("y"), lax.axis_index("c")
        cps = []
        for t in range(n):
            hr = gs[t].shape[1] // 2
            src = ins[t].at[:, pl.ds(pl.multiple_of((1 - c) * hr, SUBLANES), hr), :]
            cps.append(pltpu.make_async_remote_copy(src_ref=src, dst_ref=outs[t], send_sem=sems[0].at[t],
                                                    recv_sem=sems[1].at[t], device_id=(x, y, 1 - c),
                                                    device_id_type=MESH))
        return cps

    def start(ins, outs, sems):
        for cp in copies(ins, outs, sems):
            cp.start()

    def finish(ins, outs, sems):
        for cp in copies(ins, outs, sems):
            cp.wait()

    return Comm(list(gs), [SDS((N_CHIPS, g.shape[1] // 2, g.shape[2]), g.dtype) for g in gs], [False] * n,
                [pltpu.SemaphoreType.DMA((n,))] * 2, start, finish)


def _own_rows(c, rows):
    return pl.ds(pl.multiple_of(c * (rows // 2), 16), rows // 2)


def _chip_exchange_comm(ss):
    n = len(ss)

    def copies(ins, outs, sems):
        x, y, c = lax.axis_index("x"), lax.axis_index("y"), lax.axis_index("c")
        me = 2 * x + y

        def copy(t, j, px, py, src_slot, dst_slot):
            rows = _own_rows(c, ss[t].shape[1])
            return pltpu.make_async_remote_copy(src_ref=ins[t].at[src_slot, rows, :],
                                                dst_ref=outs[t].at[dst_slot, rows, :],
                                                send_sem=sems[0].at[3 * t + j], recv_sem=sems[1].at[3 * t + j],
                                                device_id=(px, py, c), device_id_type=MESH)

        peers = [(t, j, px, py) for t in range(n) for j, (px, py) in enumerate(_chips(x, y))]
        return copy, peers, me

    def start(ins, outs, sems):
        copy, peers, me = copies(ins, outs, sems)
        for t, j, px, py in peers:
            copy(t, j, px, py, 2 * px + py, me).start()

    def finish(ins, outs, sems):
        copy, peers, me = copies(ins, outs, sems)
        for t, j, px, py in peers:
            copy(t, j, px, py, me, 2 * px + py).wait_recv()
        for t, j, px, py in peers:
            copy(t, j, px, py, 2 * px + py, me).wait_send()

    return Comm(list(ss), [SDS(s.shape, s.dtype) for s in ss], [False] * n,
                [pltpu.SemaphoreType.DMA((3 * n,))] * 2, start, finish)


def _sibling_swap_comm(rs, ss):
    n = len(rs)

    def copies(outs, sems):
        x, y, c = lax.axis_index("x"), lax.axis_index("y"), lax.axis_index("c")
        cps = []
        for t in range(n):
            rows = _own_rows(c, rs[t].shape[1])
            held = [outs[t].at[2 * px + py, rows, :] for px, py in _chips(x, y)] + [outs[n + t].at[2 * x + y, rows, :]]
            cps += [pltpu.make_async_remote_copy(src_ref=ref, dst_ref=ref, send_sem=sems[0].at[4 * t + j],
                                                 recv_sem=sems[1].at[4 * t + j], device_id=(x, y, 1 - c),
                                                 device_id_type=MESH) for j, ref in enumerate(held)]
        return cps

    def start(ins, outs, sems):
        for cp in copies(outs, sems):
            cp.start()

    def finish(ins, outs, sems):
        for cp in copies(outs, sems):
            cp.wait()

    both = list(rs) + list(ss)
    return Comm(both, [SDS(b.shape, b.dtype) for b in both], [True] * (2 * n),
                [pltpu.SemaphoreType.DMA((4 * n,))] * 2, start, finish)


def _row_tile(rows, cap=512):
    best = SUBLANES
    for tr in range(SUBLANES, min(rows, cap) + 1, SUBLANES):
        if rows % tr == 0:
            best = tr
    return best


def _add_halves(g4, r4, c_idx, name):
    _, hr, lanes = r4.shape
    tr = _row_tile(hr)
    nb = hr // tr

    def body(c_ref, a_ref, b_ref, o_ref):
        o_ref[...] = (a_ref[...] + b_ref[...]).astype(o_ref.dtype)

    pair = 2
    owned = pl.BlockSpec((pair, tr, lanes), lambda q, i, c_ref: (q, c_ref[0] * nb + i, 0))
    grid_spec = pltpu.PrefetchScalarGridSpec(
        num_scalar_prefetch=1, grid=(N_CHIPS // pair, nb),
        in_specs=[owned, pl.BlockSpec((pair, tr, lanes), lambda q, i, c_ref: (q, i, 0))], out_specs=owned)
    return pl.pallas_call(body, name=name, grid_spec=grid_spec, out_shape=SDS(g4.shape, BF16),
                          compiler_params=_params(("parallel", "parallel")))(c_idx, g4, r4)


def _adamw(wf, r4, s4, mf, vf, me_idx, name):
    rows, lanes = wf.shape
    tr = rows // 2
    assert tr % 16 == 0
    c1 = 1.0 / (1.0 - ADAM_B1 ** ADAM_STEP)
    c2 = 1.0 / (1.0 - ADAM_B2 ** ADAM_STEP)

    def body(me_ref, w_ref, a_ref, b_ref, c_ref, d_ref, own_ref, m_ref, v_ref, g_ref, delta_ref, nm_ref, nv_ref):
        own = own_ref[...].astype(F32)
        p = [jnp.where(me_ref[0] == q, own, ref[...].astype(F32)) for q, ref in enumerate((a_ref, b_ref, c_ref, d_ref))]
        gv = ((p[0] + p[1]) + p[2]) + p[3]
        m = ADAM_B1 * m_ref[...] + (1.0 - ADAM_B1) * gv
        v = ADAM_B2 * v_ref[...] + (1.0 - ADAM_B2) * (gv * gv)
        g_ref[...] = gv
        delta_ref[...] = -ADAM_LR * ((m * c1) / (jnp.sqrt(v * c2) + ADAM_EPS) + ADAM_WD * w_ref[...])
        nm_ref[...] = m
        nv_ref[...] = v

    other = lambda q: (lambda i, me_ref: (jnp.where(me_ref[0] == q, (q + 1) % N_CHIPS, q), i, 0))
    full = pl.BlockSpec((tr, lanes), lambda i, me_ref: (i, 0))
    grid_spec = pltpu.PrefetchScalarGridSpec(
        num_scalar_prefetch=1, grid=(rows // tr,),
        in_specs=[full] + [pl.BlockSpec((None, tr, lanes), other(q)) for q in range(N_CHIPS)]
        + [pl.BlockSpec((None, tr, lanes), lambda i, me_ref: (me_ref[0], i, 0)), full, full],
        out_specs=[full] * 4)
    return pl.pallas_call(body, name=name, grid_spec=grid_spec, out_shape=[SDS((rows, lanes), F32)] * 4,
                          compiler_params=_params(("parallel",)))(me_idx, wf, r4, r4, r4, r4, s4, mf, vf)


BIG = ("ffn1_w_gate", "ffn1_w_up", "ffn1_w_down", "ffn2_w_gate", "ffn2_w_up", "ffn2_w_down", "w_out", "w_in")
TRANSPOSED = ("ffn1_w_gate", "ffn1_w_up", "ffn2_w_gate", "ffn2_w_up")
PACKED = ("rwkv_w2", "rwkv_a2", "rwkv_g2")
SMALL_SHAPES = {"ffn1_norm": (1, D_MODEL), "mix_norm": (1, D_MODEL), "hgrn_lb_logits": (2, W_A),
                "hgrn_out_norm": (1, W_A), "rwkv_shift_mu": (1, N_RWKV_COLS), "rwkv_w0": (1, W_B),
                "rwkv_a0": (1, W_B), "rwkv_k_k": (1, W_B), "rwkv_k_a": (1, W_B),
                "rwkv_r_k": (1, HB_HEADS, HB_DIM), "rwkv_gn_w": (1, W_B), "rwkv_gn_b": (1, W_B),
                "ffn2_norm": (1, D_MODEL), "final_norm": (D_MODEL,)}
PACK_ELEMS = sum(_numel(_shard_shape(n)) for n in PACKED) + sum(_numel(SMALL_SHAPES[n]) for n in SMALL)
PACK_ROWS = -(-PACK_ELEMS // (32 * LANES)) * 32


def _to_rows(name, shard):
    return shard[0].T if name in TRANSPOSED else shard[0]


def _from_rows(name, rows):
    return (rows.T if name in TRANSPOSED else rows)[None]


def _pack(sharded, small):
    flat = jnp.concatenate([sharded[n].reshape(-1) for n in PACKED] + [small[n].reshape(-1) for n in SMALL])
    return jnp.pad(flat, (0, PACK_ROWS * LANES - flat.shape[0])).reshape(PACK_ROWS, LANES)


def _unpack(packed):
    flat, out, off = packed.reshape(-1), {}, 0
    for n in PACKED:
        shp = _shard_shape(n)
        out[n] = flat[off:off + _numel(shp)].reshape((1,) + shp)
        off += _numel(shp)
    for n in SMALL:
        shp = SMALL_SHAPES[n]
        out[n] = flat[off:off + _numel(shp)].reshape(shp)
        off += _numel(shp)
    return out


def _quarter(full, name, q):
    shape, ax = SHARDED_SHAPES[name]
    w = shape[ax] // N_CHIPS
    return lax.slice_in_dim(full, q * w, (q + 1) * w, axis=ax)


def kernel(x, ffn1_norm, ffn1_w_gate, ffn1_w_up, ffn1_w_down, mix_norm, w_in, hgrn_lb_logits, hgrn_out_norm, rwkv_shift_mu, rwkv_w0, rwkv_w2, rwkv_a0, rwkv_a2, rwkv_g2, rwkv_k_k, rwkv_k_a, rwkv_r_k, rwkv_gn_w, rwkv_gn_b, w_out, ffn2_norm, ffn2_w_gate, ffn2_w_up, ffn2_w_down, final_norm, loss_target, m_ffn1_norm, m_ffn1_w_gate, m_ffn1_w_up, m_ffn1_w_down, m_mix_norm, m_w_in, m_hgrn_lb_logits, m_hgrn_out_norm, m_rwkv_shift_mu, m_rwkv_w0, m_rwkv_w2, m_rwkv_a0, m_rwkv_a2, m_rwkv_g2, m_rwkv_k_k, m_rwkv_k_a, m_rwkv_r_k, m_rwkv_gn_w, m_rwkv_gn_b, m_w_out, m_ffn2_norm, m_ffn2_w_gate, m_ffn2_w_up, m_ffn2_w_down, m_final_norm, v_ffn1_norm, v_ffn1_w_gate, v_ffn1_w_up, v_ffn1_w_down, v_mix_norm, v_w_in, v_hgrn_lb_logits, v_hgrn_out_norm, v_rwkv_shift_mu, v_rwkv_w0, v_rwkv_w2, v_rwkv_a0, v_rwkv_a2, v_rwkv_g2, v_rwkv_k_k, v_rwkv_k_a, v_rwkv_r_k, v_rwkv_gn_w, v_rwkv_gn_b, v_w_out, v_ffn2_norm, v_ffn2_w_gate, v_ffn2_w_up, v_ffn2_w_down, v_final_norm):
    args = dict(locals())
    wts = {n: args[n] for n in ALL_WEIGHTS}
    moms = {n: args["m_" + n] for n in ALL_WEIGHTS}
    vars_ = {n: args["v_" + n] for n in ALL_WEIGHTS}

    me = 2 * lax.axis_index("x") + lax.axis_index("y")
    c_idx = lax.axis_index("c").astype(jnp.int32).reshape(1)
    me_idx = me.astype(jnp.int32).reshape(1)
    shard_of = {n: _to_rows(n, wts[n]).astype(BF16) for n in BIG}
    shard_of["packed"] = _pack(wts, {n: wts[n] for n in SMALL}).astype(BF16)
    group = {"ffn1": BIG[0:3], "ffn2": BIG[3:6]}

    def slot_bufs(names):
        return [lax.dynamic_update_slice(lax.empty((N_CHIPS,) + shard_of[n].shape, BF16), shard_of[n][None],
                                         (me, 0, 0)) for n in names]

    def ffn_weights(tag, gathered):
        return {f"{tag}_wgt": gathered[0].reshape(D_FF, D_MODEL), f"{tag}_wut": gathered[1].reshape(D_FF, D_MODEL),
                f"{tag}_wd": gathered[2].reshape(D_FF, D_MODEL)}

    def w_in_weights(gathered):
        w_in_full = jnp.concatenate([gathered[0][q] for q in range(N_CHIPS)], axis=1)
        return {"w_in_h": w_in_full[:, :N_HGRN_COLS],
                "w_in_r": jnp.pad(w_in_full[:, N_HGRN_COLS:], ((0, 0), (0, N_RWKV_PAD - N_RWKV_COLS)))}

    def mixer_weights(gathered):
        w_out_full = gathered[0].reshape(D_MODEL, D_MODEL)
        packs = gathered[1].reshape(N_CHIPS, PACK_ROWS * LANES)
        full, off = {}, 0
        for n in PACKED:
            shp = _shard_shape(n)
            full[n] = jnp.concatenate([packs[q, off:off + _numel(shp)].reshape(shp) for q in range(N_CHIPS)], axis=1)
            off += _numel(shp)
        zrow = lambda nrow: jnp.zeros((nrow, W_B), BF16)
        return {"w_out": w_out_full,
                "w2_pad": jnp.concatenate([full["rwkv_w2"], zrow(LORA_PAD - 32)], axis=0),
                "a2_pad": jnp.concatenate([zrow(32), full["rwkv_a2"], zrow(LORA_PAD - 64)], axis=0),
                "g2_pad": jnp.concatenate([zrow(64), full["rwkv_g2"], zrow(LORA_PAD - 160)], axis=0)}

    plan = _Plan()
    w = {}
    plan.carry("ffn1_rms", lambda g: _gather_comm(slot_bufs(group["ffn1"][:2])),
               lambda res, w_: w_.update({"ffn1_wgt": res[0].reshape(D_FF, D_MODEL),
                                          "ffn1_wut": res[1].reshape(D_FF, D_MODEL)}))

    def after_gate_up(res, w_):
        w_["ffn1_wd"] = res[0].reshape(D_FF, D_MODEL)
        w_.update(w_in_weights(res[1:]))

    plan.carry("ffn1_gate_up", lambda g: _gather_comm(slot_bufs(("ffn1_w_down", "w_in"))), after_gate_up)
    plan.carry("ffn1_down", lambda g: _gather_comm(slot_bufs(("w_out", "packed"))),
               lambda res, w_: w_.update(mixer_weights(res)))
    plan.carry("rwkv_fwd", lambda g: _gather_comm(slot_bufs(group["ffn2"])),
               lambda res, w_: w_.update(ffn_weights("ffn2", res)))
    w["ffn1_norm"], w["ffn2_norm"] = ffn1_norm, ffn2_norm
    w["mix_norm"] = mix_norm
    w["lb0"], w["lb1"] = hgrn_lb_logits[0:1], hgrn_lb_logits[1:2]
    w["hgrn_out_norm"] = hgrn_out_norm
    w["mu_pad"] = jnp.pad(rwkv_shift_mu, ((0, 0), (0, N_RWKV_PAD - N_RWKV_COLS)))
    for n in ("rwkv_w0", "rwkv_a0", "rwkv_k_k", "rwkv_k_a", "rwkv_gn_w", "rwkv_gn_b"):
        w[n] = wts[n]
    w["rwkv_r_k"] = rwkv_r_k.reshape(1, W_B)
    w["final_norm"] = final_norm.reshape(1, D_MODEL)

    def reduce_rows(names, gs):
        r1 = _run_comm(_sibling_exchange_comm(gs), "grad_sibling_exchange")
        s4 = [_add_halves(gt, rt, c_idx, f"grad_add_halves_{n}") for gt, rt, n in zip(gs, r1, names)]
        return list(zip(_run_comm(_chip_exchange_comm(s4), "grad_chip_exchange"), s4))

    def swap_comm(names):
        return _sibling_swap_comm([early[n][0] for n in names], [early[n][1] for n in names])

    def after_swap(names):
        return lambda res, w_: swapped.update(zip(names, zip(res[:len(names)], res[len(names):])))

    early, swapped = {}, {}

    def reduce_early(names, grads_of, sibling_host, chips_host, swap_host):
        def sibling_comm(g):
            early[names, "gs"] = grads_of(g)
            return _sibling_exchange_comm(early[names, "gs"])

        def after_sibling(res, w_):
            early[names, "s4"] = [_add_halves(gt, rt, c_idx, f"grad_add_halves_{n}")
                                  for gt, rt, n in zip(early[names, "gs"], res, names)]

        plan.carry(sibling_host, sibling_comm, after_sibling)
        plan.carry(chips_host, lambda g: _chip_exchange_comm(early[names, "s4"]),
                   lambda res, w_: early.update(zip(names, zip(res, early[names, "s4"]))))
        if swap_host:
            plan.carry(swap_host, lambda g: swap_comm(names), after_swap(names))

    def proj_grads(g):
        g_w_in = jnp.concatenate([g["w_in_h"], g["w_in_r"][:, :N_RWKV_COLS]], axis=1)
        return [g["w_out"].reshape(N_CHIPS, -1, D_MODEL),
                jnp.stack([_quarter(g_w_in, "w_in", q) for q in range(N_CHIPS)])]

    rows_of = lambda keys: (lambda g: [g[k].reshape(N_CHIPS, -1, D_MODEL) for k in keys])
    reduce_early(group["ffn2"], rows_of(("ffn2_wgt", "ffn2_wut", "ffn2_wd")), "hgrn_bwd", "rwkv_bwd", "rwkv_prep_bwd")
    reduce_early(("w_out", "w_in"), proj_grads, "mix_drms", "ffn1_dact", "ffn1_dwg")
    reduce_early(("ffn1_w_down",), rows_of(("ffn1_wd",)), "ffn1_dwg", "ffn1_dwu", "ffn1_dh_g")
    reduce_early(("ffn1_w_gate",), rows_of(("ffn1_wgt",)), "ffn1_dwu", "ffn1_dh_g", "ffn1_dh_u")
    reduce_early(("ffn1_w_up",), rows_of(("ffn1_wut",)), "ffn1_dh_g", "ffn1_dh_u", None)
    loss_slab, grad_x, g = _local_step(x[0], loss_target[0], w, plan)
    loss = lax.psum(loss_slab[0, 0], ("x", "y", "c"))

    gfull = {
        "rwkv_w2": g["w2_pad"][0:32], "rwkv_a2": g["a2_pad"][32:64], "rwkv_g2": g["g2_pad"][64:160],
    }
    gsmall = {
        "ffn1_norm": g["ffn1_norm"], "mix_norm": g["mix_norm"],
        "hgrn_lb_logits": jnp.concatenate([g["lb0"], g["lb1"]], axis=0), "hgrn_out_norm": g["hgrn_out_norm"],
        "rwkv_shift_mu": g["mu_pad"][:, :N_RWKV_COLS], "rwkv_w0": g["rwkv_w0"], "rwkv_a0": g["rwkv_a0"],
        "rwkv_k_k": g["rwkv_k_k"], "rwkv_k_a": g["rwkv_k_a"], "rwkv_r_k": g["rwkv_r_k"],
        "rwkv_gn_w": g["rwkv_gn_w"], "rwkv_gn_b": g["rwkv_gn_b"], "ffn2_norm": g["ffn2_norm"],
        "final_norm": g["final_norm"],
    }
    packed = jnp.stack([_pack({n: _quarter(gfull[n], n, q) for n in PACKED}, gsmall) for q in range(N_CHIPS)])
    early["packed"], = reduce_rows(["packed"], [packed])
    last = ["ffn1_w_up", "packed"]
    after_swap(last)(_run_comm(swap_comm(last), "grad_sibling_swap"), w)
    names = list(BIG) + ["packed"]

    def rows_list(d):
        return [_to_rows(n, d[n]) for n in BIG] + [_pack(d, {n: d[n] for n in SMALL})]

    outs = [_adamw(wt, *swapped[n], mt, vt, me_idx, f"adamw_{n}")
            for wt, mt, vt, n in zip(rows_list(wts), rows_list(moms), rows_list(vars_), names)]
    results = []
    for k in range(4):
        per = [outs[i][k] for i in range(len(names))]
        d = {n: _from_rows(n, z) for n, z in zip(BIG, per[:-1])}
        d.update(_unpack(per[-1]))
        results.append(d)
    return (loss, grad_x[None], *[r[n] for r in results for n in ALL_WEIGHTS])
```
